```python
import jax, jax.numpy as jnp
from jax import lax
import numpy as np

D_MODEL = 2048
BATCH = 8
SEQ = 2048
DEPTH = 2

N_HEADS_TOTAL = 16
HEAD_DIM = D_MODEL // N_HEADS_TOTAL
DIL_PATTERNS = ((128, 1), (512, 4), (2048, 16))
N_DIL_GROUPS = len(DIL_PATTERNS)
HEADS_PER_GROUP = 4
A_HEADS = N_DIL_GROUPS * HEADS_PER_GROUP
A_OUT = HEADS_PER_GROUP * HEAD_DIM
SB_HEADS = 4
B_OUT = SB_HEADS * HEAD_DIM
A_W = A_HEADS * HEAD_DIM
IN_SPLITS = (A_W, A_W, A_W, B_OUT, B_OUT, B_OUT, D_MODEL, D_MODEL)
IN_WIDTH = sum(IN_SPLITS)
D_FF = -(-8 * D_MODEL // (3 * 256)) * 256
BLOCK = 128
ROPE_THETA = 10000.0
EPS = 1e-6

kernel_name = "hybrid_dilated_stickbreaking_adaln_block"


def rms_norm(x, g):
    xf = x.astype(jnp.float32)
    y = xf * lax.rsqrt(jnp.mean(xf * xf, axis=-1, keepdims=True) + EPS)
    return y * g.astype(jnp.float32)


def rope_tables(seq):
    inv = jnp.power(ROPE_THETA, -jnp.arange(0, HEAD_DIM, 2, dtype=jnp.float32) / HEAD_DIM)
    ang = jnp.arange(seq, dtype=jnp.float32)[:, None] * inv[None, :]
    return jnp.cos(ang), jnp.sin(ang)


def apply_rope(x, cos, sin):
    half = HEAD_DIM // 2
    x1, x2 = x[..., :half], x[..., half:]
    cs, sn = cos[None, :, None, :], sin[None, :, None, :]
    return jnp.concatenate([x1 * cs - x2 * sn, x2 * cs + x1 * sn], axis=-1)


def dilated_window_attention(q, k, v, window, dilation):
    B, S, H, hd = q.shape
    L = S // dilation
    w_sub = window // dilation
    Lp = -(-L // BLOCK) * BLOCK
    nb = Lp // BLOCK

    def to_sub(t):
        t = t.reshape(B, L, dilation, H, hd).transpose(0, 2, 3, 1, 4)
        t = jnp.pad(t, ((0, 0), (0, 0), (0, 0), (0, Lp - L), (0, 0)))
        return t.reshape(B, dilation, H, nb, BLOCK, hd)

    qb, kb, vb = to_sub(q), to_sub(k), to_sub(v)

    def with_prev(t):
        prev = jnp.pad(t, ((0, 0), (0, 0), (0, 0), (1, 0), (0, 0), (0, 0)))[:, :, :, :nb]
        return jnp.concatenate([prev, t], axis=4)

    kk, vv = with_prev(kb), with_prev(vb)
    s = jnp.einsum('brhnqd,brhnkd->brhnqk', qb, kk) * (hd ** -0.5)
    qi = jnp.arange(BLOCK)[:, None]
    kj = jnp.arange(2 * BLOCK)[None, :]
    dist = qi + BLOCK - kj
    band = (dist >= 0) & (dist <= w_sub)
    valid = (jnp.arange(nb)[:, None, None] * BLOCK + kj[None] - BLOCK) >= 0
    mask = band[None] & valid
    s = jnp.where(mask, s, -jnp.inf)
    m = jnp.max(s, axis=-1, keepdims=True)
    p = jnp.exp(s - m)
    den = jnp.sum(p, axis=-1)
    o = jnp.einsum('brhnqk,brhnkd->brhnqd', p, vv) / den[..., None]
    lse = m[..., 0] + jnp.log(den)
    o = o.reshape(B, dilation, H, Lp, hd)[:, :, :, :L].transpose(0, 3, 1, 2, 4).reshape(B, S, H, hd)
    lse = lse.reshape(B, dilation, H, Lp)[:, :, :, :L].transpose(0, 3, 1, 2).reshape(B, S, H)
    return o, lse


def mixer_a(qa, ka, va, qn_g, kn_g, cos, sin):
    B, S = qa.shape[0], qa.shape[1]
    q = apply_rope(rms_norm(qa, qn_g), cos, sin)
    k = apply_rope(rms_norm(ka, kn_g), cos, sin)
    v = va.astype(jnp.float32)
    outs, lses = [], []
    for g, (window, dilation) in enumerate(DIL_PATTERNS):
        lo, hi = g * HEADS_PER_GROUP, (g + 1) * HEADS_PER_GROUP
        o, lse = dilated_window_attention(q[:, :, lo:hi], k[:, :, lo:hi], v[:, :, lo:hi], window, dilation)
        outs.append(o)
        lses.append(lse)
    o = jnp.stack(outs, 0)
    w = jax.nn.softmax(jnp.stack(lses, 0), axis=0)
    return jnp.sum(w[..., None] * o, axis=0).reshape(B, S, A_OUT)


def stick_breaking_attention(q, k, v):
    B, S, H, hd = q.shape
    nb = S // BLOCK
    qh = q.astype(jnp.float32).transpose(0, 2, 1, 3)
    kh = k.astype(jnp.float32).transpose(0, 2, 1, 3)
    vh = v.astype(jnp.float32).transpose(0, 2, 1, 3)
    qblocks = qh.reshape(B, H, nb, BLOCK, hd).transpose(2, 0, 1, 3, 4)
    key_pos = jnp.arange(S)
    scale = hd ** -0.5

    def one_block(args):
        qblk, i = args
        z = jnp.einsum('bhqd,bhkd->bhqk', qblk, kh) * scale
        qpos = i * BLOCK + jnp.arange(BLOCK)
        causal = key_pos[None, :] < qpos[:, None]
        log_beta = jax.nn.log_sigmoid(z)
        log_1mb = jnp.where(causal, jax.nn.log_sigmoid(-z), 0.0)
        shifted = jnp.concatenate([log_1mb[..., 1:], jnp.zeros_like(log_1mb[..., :1])], axis=-1)
        after = lax.cumsum(shifted, axis=3, reverse=True)
        a = jnp.where(causal, jnp.exp(log_beta + after), 0.0)
        return jnp.einsum('bhqk,bhkd->bhqd', a, vh)

    o = lax.map(one_block, (qblocks, jnp.arange(nb)))
    return o.transpose(1, 0, 3, 2, 4).reshape(B, S, H * hd)


def _fwd_setup_inputs(seed: int = 0) -> dict:
    key = jax.random.key(seed)
    ks = jax.random.split(key, 16)
    f32 = jnp.float32

    def w(k, shape, fan_in, mult=1.0):
        return jax.random.normal(k, shape, f32) * (mult * fan_in ** -0.5)

    return {
        "x": jax.random.normal(ks[0], (BATCH, SEQ, D_MODEL), f32),
        "c": jax.random.normal(ks[1], (BATCH, D_MODEL), f32),
        "w_ada": w(ks[2], (DEPTH, D_MODEL, 6 * D_MODEL), D_MODEL, 0.5),
        "b_ada": 0.01 * jax.random.normal(ks[3], (DEPTH, 6 * D_MODEL), f32),
        "norm1_g": 1.0 + 0.02 * jax.random.normal(ks[4], (DEPTH, D_MODEL), f32),
        "norm2_g": 1.0 + 0.02 * jax.random.normal(ks[5], (DEPTH, D_MODEL), f32),
        "w_in": w(ks[6], (DEPTH, D_MODEL, IN_WIDTH), D_MODEL),
        "qn_g": 1.0 + 0.02 * jax.random.normal(ks[7], (DEPTH, HEAD_DIM), f32),
        "kn_g": 1.0 + 0.02 * jax.random.normal(ks[8], (DEPTH, HEAD_DIM), f32),
        "w_branch_a": w(ks[9], (DEPTH, A_OUT, D_MODEL), A_OUT),
        "w_branch_b": w(ks[10], (DEPTH, B_OUT, D_MODEL), B_OUT),
        "w_out": w(ks[11], (DEPTH, D_MODEL, D_MODEL), D_MODEL),
        "w_gate_up": w(ks[12], (DEPTH, D_MODEL, 2 * D_FF), D_MODEL),
        "w_down": w(ks[13], (DEPTH, D_FF, D_MODEL), D_FF),
    }


def _fwd_reference(x, c, w_ada, b_ada, norm1_g, norm2_g, w_in, qn_g, kn_g,
              w_branch_a, w_branch_b, w_out, w_gate_up, w_down):
    B, S, D = x.shape
    cos, sin = rope_tables(S)
    offs = [0]
    for n in IN_SPLITS:
        offs.append(offs[-1] + n)
    c_act = jax.nn.silu(c)
    h = x
    for l in range(DEPTH):
        mod = (c_act @ w_ada[l] + b_ada[l])[:, None, :]
        shift1 = mod[..., 0 * D:1 * D]
        scale1 = mod[..., 1 * D:2 * D]
        gate1 = mod[..., 2 * D:3 * D]
        shift2 = mod[..., 3 * D:4 * D]
        scale2 = mod[..., 4 * D:5 * D]
        gate2 = mod[..., 5 * D:6 * D]

        u = (rms_norm(h, norm1_g[l]) * (1.0 + scale1) + shift1).astype(h.dtype)
        proj = u @ w_in[l]
        parts = [proj[..., offs[i]:offs[i + 1]] for i in range(len(IN_SPLITS))]
        qa = parts[0].reshape(B, S, A_HEADS, HEAD_DIM)
        ka = parts[1].reshape(B, S, A_HEADS, HEAD_DIM)
        va = parts[2].reshape(B, S, A_HEADS, HEAD_DIM)
        qb = parts[3].reshape(B, S, SB_HEADS, HEAD_DIM)
        kb = parts[4].reshape(B, S, SB_HEADS, HEAD_DIM)
        vb = parts[5].reshape(B, S, SB_HEADS, HEAD_DIM)
        ga, gb = parts[6], parts[7]
        o_a = mixer_a(qa, ka, va, qn_g[l], kn_g[l], cos, sin).astype(h.dtype)
        o_b = stick_breaking_attention(qb, kb, vb).astype(h.dtype)
        y_a = o_a @ w_branch_a[l]
        y_b = o_b @ w_branch_b[l]
        merged = jax.nn.sigmoid(ga) * y_a + jax.nn.sigmoid(gb) * y_b
        h = h + gate1 * (merged @ w_out[l])

        u2 = (rms_norm(h, norm2_g[l]) * (1.0 + scale2) + shift2).astype(h.dtype)
        gu = u2 @ w_gate_up[l]
        h = h + gate2 * ((jax.nn.silu(gu[..., :D_FF]) * gu[..., D_FF:]) @ w_down[l])
    return h


import jax as _jax
import jax.numpy as _jnp

TWIN_FORMAT = 'train_step'
FWD_PARAMS = ['x', 'c', 'w_ada', 'b_ada', 'norm1_g', 'norm2_g', 'w_in', 'qn_g', 'kn_g', 'w_branch_a', 'w_branch_b', 'w_out', 'w_gate_up', 'w_down']
TWIN_WEIGHTS = ['w_ada', 'b_ada', 'norm1_g', 'norm2_g', 'w_in', 'qn_g', 'kn_g', 'w_branch_a', 'w_branch_b', 'w_out', 'w_gate_up', 'w_down']
TWIN_DIFF_INPUT = 'x'
TWIN_INPUTS = ['x', 'c', 'w_ada', 'b_ada', 'norm1_g', 'norm2_g', 'w_in', 'qn_g', 'kn_g', 'w_branch_a', 'w_branch_b', 'w_out', 'w_gate_up', 'w_down', 'loss_target', 'm_w_ada', 'm_b_ada', 'm_norm1_g', 'm_norm2_g', 'm_w_in', 'm_qn_g', 'm_kn_g', 'm_w_branch_a', 'm_w_branch_b', 'm_w_out', 'm_w_gate_up', 'm_w_down', 'v_w_ada', 'v_b_ada', 'v_norm1_g', 'v_norm2_g', 'v_w_in', 'v_qn_g', 'v_kn_g', 'v_w_branch_a', 'v_w_branch_b', 'v_w_out', 'v_w_gate_up', 'v_w_down']
TWIN_OUTPUTS = ['loss', 'grad_x', 'grad_w_ada', 'grad_b_ada', 'grad_norm1_g', 'grad_norm2_g', 'grad_w_in', 'grad_qn_g', 'grad_kn_g', 'grad_w_branch_a', 'grad_w_branch_b', 'grad_w_out', 'grad_w_gate_up', 'grad_w_down', 'delta_w_ada', 'delta_b_ada', 'delta_norm1_g', 'delta_norm2_g', 'delta_w_in', 'delta_qn_g', 'delta_kn_g', 'delta_w_branch_a', 'delta_w_branch_b', 'delta_w_out', 'delta_w_gate_up', 'delta_w_down', 'new_m_w_ada', 'new_m_b_ada', 'new_m_norm1_g', 'new_m_norm2_g', 'new_m_w_in', 'new_m_qn_g', 'new_m_kn_g', 'new_m_w_branch_a', 'new_m_w_branch_b', 'new_m_w_out', 'new_m_w_gate_up', 'new_m_w_down', 'new_v_w_ada', 'new_v_b_ada', 'new_v_norm1_g', 'new_v_norm2_g', 'new_v_w_in', 'new_v_qn_g', 'new_v_kn_g', 'new_v_w_branch_a', 'new_v_w_branch_b', 'new_v_w_out', 'new_v_w_gate_up', 'new_v_w_down']
TWIN_LEAF_KINDS = {'loss': 'loss', 'grad_x': 'grad_x', 'grad_w_ada': 'grad_w', 'grad_b_ada': 'grad_w', 'grad_norm1_g': 'grad_w', 'grad_norm2_g': 'grad_w', 'grad_w_in': 'grad_w', 'grad_qn_g': 'grad_w', 'grad_kn_g': 'grad_w', 'grad_w_branch_a': 'grad_w', 'grad_w_branch_b': 'grad_w', 'grad_w_out': 'grad_w', 'grad_w_gate_up': 'grad_w', 'grad_w_down': 'grad_w', 'delta_w_ada': 'delta_w', 'delta_b_ada': 'delta_w', 'delta_norm1_g': 'delta_w', 'delta_norm2_g': 'delta_w', 'delta_w_in': 'delta_w', 'delta_qn_g': 'delta_w', 'delta_kn_g': 'delta_w', 'delta_w_branch_a': 'delta_w', 'delta_w_branch_b': 'delta_w', 'delta_w_out': 'delta_w', 'delta_w_gate_up': 'delta_w', 'delta_w_down': 'delta_w', 'new_m_w_ada': 'new_m', 'new_m_b_ada': 'new_m', 'new_m_norm1_g': 'new_m', 'new_m_norm2_g': 'new_m', 'new_m_w_in': 'new_m', 'new_m_qn_g': 'new_m', 'new_m_kn_g': 'new_m', 'new_m_w_branch_a': 'new_m', 'new_m_w_branch_b': 'new_m', 'new_m_w_out': 'new_m', 'new_m_w_gate_up': 'new_m', 'new_m_w_down': 'new_m', 'new_v_w_ada': 'new_v', 'new_v_b_ada': 'new_v', 'new_v_norm1_g': 'new_v', 'new_v_norm2_g': 'new_v', 'new_v_w_in': 'new_v', 'new_v_qn_g': 'new_v', 'new_v_kn_g': 'new_v', 'new_v_w_branch_a': 'new_v', 'new_v_w_branch_b': 'new_v', 'new_v_w_out': 'new_v', 'new_v_w_gate_up': 'new_v', 'new_v_w_down': 'new_v'}


def _forward(args):
    return _fwd_reference(*[args[k] for k in FWD_PARAMS])


def _output_shape():
    out = _jax.eval_shape(lambda: _forward(_fwd_setup_inputs(0)))
    return out.shape, out.dtype

N_MICROBATCH = 1
ADAM_LR = 0.001
ADAM_B1 = 0.9
ADAM_B2 = 0.999
ADAM_EPS = 1e-08
ADAM_WD = 0.01
ADAM_STEP = 10
PER_EXAMPLE_BATCH_AXIS = {'x': 0, 'c': 0, 'loss_target': 0}
SHARED_INPUTS = []
_WEIGHT_DTYPES = {'w_ada': _jnp.float32, 'b_ada': _jnp.float32, 'norm1_g': _jnp.float32, 'norm2_g': _jnp.float32, 'w_in': _jnp.float32, 'qn_g': _jnp.float32, 'kn_g': _jnp.float32, 'w_branch_a': _jnp.float32, 'w_branch_b': _jnp.float32, 'w_out': _jnp.float32, 'w_gate_up': _jnp.float32, 'w_down': _jnp.float32}
MOMENT_SCALE = {'w_ada': 1.949926e-01, 'b_ada': 4.255571e-01, 'norm1_g': 1.517170e-01, 'norm2_g': 7.790411e-01, 'w_in': 1.924755e-02, 'qn_g': 2.561734e-02, 'kn_g': 2.571623e-02, 'w_branch_a': 1.338311e-02, 'w_branch_b': 2.539693e-02, 'w_out': 2.584441e-02, 'w_gate_up': 1.752701e-02, 'w_down': 2.363685e-02}


def _to_microbatches(a, axis):
    t = _jnp.moveaxis(a, axis, 0)
    t = t.reshape((N_MICROBATCH, t.shape[0] // N_MICROBATCH) + t.shape[1:])
    return _jnp.moveaxis(t, 1, axis + 1)


def setup_inputs(seed: int = 0) -> dict:
    inp = _fwd_setup_inputs(seed)
    key = _jax.random.fold_in(_jax.random.key(seed), 7919)
    shape, _ = _output_shape()
    out = dict(inp)
    out["loss_target"] = _jax.random.normal(_jax.random.fold_in(key, 0), shape, _jnp.float32)
    for i, name in enumerate(TWIN_WEIGHTS):
        w = inp[name].astype(_jnp.float32)
        if MOMENT_SCALE is None:
            s = _jnp.sqrt(_jnp.mean(_jnp.square(w)) + 1e-30)
        else:
            s = MOMENT_SCALE[name]
        km, kv = _jax.random.split(_jax.random.fold_in(key, i + 1))
        out[name] = w
        out["m_" + name] = s * _jax.random.normal(km, w.shape, _jnp.float32)
        out["v_" + name] = (s * s) * _jax.random.uniform(kv, w.shape, _jnp.float32, 0.5, 1.5)
    if N_MICROBATCH > 1:
        for name, axis in PER_EXAMPLE_BATCH_AXIS.items():
            out[name] = _to_microbatches(out[name], axis)
    return {'x': out['x'], 'c': out['c'], 'w_ada': out['w_ada'], 'b_ada': out['b_ada'], 'norm1_g': out['norm1_g'], 'norm2_g': out['norm2_g'], 'w_in': out['w_in'], 'qn_g': out['qn_g'], 'kn_g': out['kn_g'], 'w_branch_a': out['w_branch_a'], 'w_branch_b': out['w_branch_b'], 'w_out': out['w_out'], 'w_gate_up': out['w_gate_up'], 'w_down': out['w_down'], 'loss_target': out['loss_target'], 'm_w_ada': out['m_w_ada'], 'm_b_ada': out['m_b_ada'], 'm_norm1_g': out['m_norm1_g'], 'm_norm2_g': out['m_norm2_g'], 'm_w_in': out['m_w_in'], 'm_qn_g': out['m_qn_g'], 'm_kn_g': out['m_kn_g'], 'm_w_branch_a': out['m_w_branch_a'], 'm_w_branch_b': out['m_w_branch_b'], 'm_w_out': out['m_w_out'], 'm_w_gate_up': out['m_w_gate_up'], 'm_w_down': out['m_w_down'], 'v_w_ada': out['v_w_ada'], 'v_b_ada': out['v_b_ada'], 'v_norm1_g': out['v_norm1_g'], 'v_norm2_g': out['v_norm2_g'], 'v_w_in': out['v_w_in'], 'v_qn_g': out['v_qn_g'], 'v_kn_g': out['v_kn_g'], 'v_w_branch_a': out['v_w_branch_a'], 'v_w_branch_b': out['v_w_branch_b'], 'v_w_out': out['v_w_out'], 'v_w_gate_up': out['v_w_gate_up'], 'v_w_down': out['v_w_down']}


def _loss(weights, diff, rest, loss_target):
    with _jax.named_scope("forward"):
        args = {**rest, TWIN_DIFF_INPUT: diff, **{k: w.astype(_WEIGHT_DTYPES[k]) for k, w in weights.items()}}
        y = _forward(args)
    with _jax.named_scope("loss_head"):
        err = _jnp.square(y.astype(_jnp.float32) - loss_target)
        return 0.5 * _jnp.sum(_jnp.mean(err, axis=-1)) if err.ndim else 0.5 * err


def _adamw(w, g, m, v):
    m = ADAM_B1 * m + (1.0 - ADAM_B1) * g
    v = ADAM_B2 * v + (1.0 - ADAM_B2) * _jnp.square(g)
    m_hat = m / (1.0 - ADAM_B1 ** ADAM_STEP)
    v_hat = v / (1.0 - ADAM_B2 ** ADAM_STEP)
    delta = -ADAM_LR * (m_hat / (_jnp.sqrt(v_hat) + ADAM_EPS) + ADAM_WD * w)
    return delta, m, v


def reference(x, c, w_ada, b_ada, norm1_g, norm2_g, w_in, qn_g, kn_g, w_branch_a, w_branch_b, w_out, w_gate_up, w_down, loss_target, m_w_ada, m_b_ada, m_norm1_g, m_norm2_g, m_w_in, m_qn_g, m_kn_g, m_w_branch_a, m_w_branch_b, m_w_out, m_w_gate_up, m_w_down, v_w_ada, v_b_ada, v_norm1_g, v_norm2_g, v_w_in, v_qn_g, v_kn_g, v_w_branch_a, v_w_branch_b, v_w_out, v_w_gate_up, v_w_down):
    given = dict(x=x, c=c, w_ada=w_ada, b_ada=b_ada, norm1_g=norm1_g, norm2_g=norm2_g, w_in=w_in, qn_g=qn_g, kn_g=kn_g, w_branch_a=w_branch_a, w_branch_b=w_branch_b, w_out=w_out, w_gate_up=w_gate_up, w_down=w_down, loss_target=loss_target, m_w_ada=m_w_ada, m_b_ada=m_b_ada, m_norm1_g=m_norm1_g, m_norm2_g=m_norm2_g, m_w_in=m_w_in, m_qn_g=m_qn_g, m_kn_g=m_kn_g, m_w_branch_a=m_w_branch_a, m_w_branch_b=m_w_branch_b, m_w_out=m_w_out, m_w_gate_up=m_w_gate_up, m_w_down=m_w_down, v_w_ada=v_w_ada, v_b_ada=v_b_ada, v_norm1_g=v_norm1_g, v_norm2_g=v_norm2_g, v_w_in=v_w_in, v_qn_g=v_qn_g, v_kn_g=v_kn_g, v_w_branch_a=v_w_branch_a, v_w_branch_b=v_w_branch_b, v_w_out=v_w_out, v_w_gate_up=v_w_gate_up, v_w_down=v_w_down)
    weights = {n: given[n] for n in TWIN_WEIGHTS}
    shared = {n: given[n] for n in SHARED_INPUTS}
    per_example = {n: given[n] for n in ['x', 'c']}
    grad_fn = _jax.value_and_grad(_loss, argnums=(0, 1))

    def one_microbatch(ex, loss_target):
        ex = dict(ex)
        diff = ex.pop(TWIN_DIFF_INPUT)
        return grad_fn(weights, diff, {**shared, **ex}, loss_target)

    if N_MICROBATCH == 1:
        loss, (grad_w, grad_x) = one_microbatch(per_example, given["loss_target"])
    else:
        def body(carry, xs):
            loss_sum, grad_sum = carry
            l_k, (gw_k, gx_k) = one_microbatch(xs[0], xs[1])
            with _jax.named_scope("update"):
                return (loss_sum + l_k, _jax.tree.map(_jnp.add, grad_sum, gw_k)), gx_k

        init = (_jnp.zeros((), _jnp.float32), _jax.tree.map(_jnp.zeros_like, weights))
        (loss, grad_w), grad_x = _jax.lax.scan(body, init, (per_example, given["loss_target"]))
    with _jax.named_scope("update"):
        delta_w, new_m, new_v = {}, {}, {}
        for n in TWIN_WEIGHTS:
            delta_w[n], new_m[n], new_v[n] = _adamw(weights[n], grad_w[n], given["m_" + n], given["v_" + n])
    return (loss, grad_x, *[grad_w[n] for n in TWIN_WEIGHTS], *[delta_w[n] for n in TWIN_WEIGHTS],
            *[new_m[n] for n in TWIN_WEIGHTS], *[new_v[n] for n in TWIN_WEIGHTS])
```

```python
import functools

import jax
import jax.numpy as jnp
from jax import lax
from jax.experimental import pallas as pl
from jax.experimental.pallas import tpu as pltpu

F32 = jnp.float32
BF16 = jnp.bfloat16

HEAD_DIM = 128
BLOCK = 128
DILATIONS = (1, 4, 16)
HEADS_PER_GROUP = 4
A_HEADS = 12
SB_HEADS = 4
GROUP_W = HEADS_PER_GROUP * HEAD_DIM
A_W = A_HEADS * HEAD_DIM
B_W = SB_HEADS * HEAD_DIM
OFF_QA, OFF_KA, OFF_VA = 0, A_W, 2 * A_W
OFF_QB, OFF_KB, OFF_VB = 3 * A_W, 3 * A_W + B_W, 3 * A_W + 2 * B_W
OFF_GATES = 3 * A_W + 3 * B_W
ROPE_THETA = 10000.0
EPS = 1e-6
ATT_SCALE = HEAD_DIM ** -0.5
MASKED = -1e30

ADAM_LR, ADAM_B1, ADAM_B2, ADAM_EPS, ADAM_WD, ADAM_STEP = 0.001, 0.9, 0.999, 1e-08, 0.01, 10

N_DEV = 8
N_CHIPS = 4
V7X_VMEM_LIMIT_BYTES = 56 * 1024 * 1024
ELEMWISE_BLOCK_BYTES = 1024 * 1024
MESH = pl.DeviceIdType.MESH

NN = (((1,), (0,)), ((), ()))
NT = (((1,), (1,)), ((), ()))
TN = (((0,), (0,)), ((), ()))


def _dot(a, b, dims=NN):
    return lax.dot_general(a, b, dims, preferred_element_type=F32)


def _tile(n, cap, mult=128):
    best = None
    for t in range(mult, min(n, cap) + 1, mult):
        if n % t == 0:
            best = t
    if best is None:
        assert n <= 2 * cap, (n, cap)
        return n
    return best


def _rows(r, c):
    return _tile(r, max(16, ELEMWISE_BLOCK_BYTES // (4 * c)), 16)


def _pcall(body, *, name, out_shape, grid=None, in_specs=None, out_specs=None, scratch=(), aliases=None,
           grid_spec=None):
    kwargs = {}
    if grid_spec is not None:
        kwargs["grid_spec"] = grid_spec
    else:
        if grid is not None:
            kwargs["grid"] = grid
        kwargs["in_specs"] = in_specs
        kwargs["out_specs"] = out_specs
        kwargs["scratch_shapes"] = list(scratch)
    return pl.pallas_call(
        body, name=name, out_shape=out_shape, input_output_aliases=aliases or {},
        compiler_params=pltpu.CompilerParams(vmem_limit_bytes=V7X_VMEM_LIMIT_BYTES), **kwargs)


def _mm(a, b, *, name, ta=False, tb=False, out_dtype=F32, caps=(1024, 1024, 1024), stack=None):
    kdim, m = a.shape if ta else a.shape[::-1]
    n, k2 = b.shape if tb else b.shape[::-1]
    assert kdim == k2, (a.shape, b.shape, ta, tb)
    tm, tn, tk = _tile(m, caps[0]), _tile(n, caps[1]), _tile(kdim, caps[2])
    nk = kdim // tk
    dims = (((0 if ta else 1,), (1 if tb else 0,)), ((), ()))

    def body(*refs):
        a_ref, b_ref = refs[0], refs[1]
        part = _dot(a_ref[...].astype(BF16), b_ref[...].astype(BF16), dims)
        if nk == 1:
            o_ref = refs[-1]
            o_ref[...] = part.astype(o_ref.dtype)
            return
        o_ref, acc_ref = refs[-2], refs[-1]
        k = pl.program_id(2)

        @pl.when(k == 0)
        def _():
            acc_ref[...] = part

        @pl.when(k > 0)
        def _():
            acc_ref[...] += part

        @pl.when(k == nk - 1)
        def _():
            o_ref[...] = acc_ref[...].astype(o_ref.dtype)

    a_spec = (pl.BlockSpec((tk, tm), lambda i, j, k: (k, i)) if ta
              else pl.BlockSpec((tm, tk), lambda i, j, k: (i, k)))
    b_spec = (pl.BlockSpec((tn, tk), lambda i, j, k: (j, k)) if tb
              else pl.BlockSpec((tk, tn), lambda i, j, k: (k, j)))
    ins, in_specs, aliases = [a, b], [a_spec, b_spec], {}
    if stack is None:
        out_shape = jax.ShapeDtypeStruct((m, n), out_dtype)
        out_spec = pl.BlockSpec((tm, tn), lambda i, j, k: (i, j))
    else:
        layer, n_layers, buf = stack
        out_shape = jax.ShapeDtypeStruct((n_layers, m, n), out_dtype)
        out_spec = pl.BlockSpec((None, tm, tn), lambda i, j, k: (layer, i, j))
        if buf is not None:
            ins.append(buf)
            in_specs.append(pl.BlockSpec(memory_space=pl.ANY))
            aliases = {2: 0}
    scratch = [] if nk == 1 else [pltpu.VMEM((tm, tn), F32)]
    return _pcall(body, name=name, out_shape=out_shape, grid=(m // tm, n // tn, nk), in_specs=in_specs,
                  out_specs=out_spec, scratch=scratch, aliases=aliases)(*ins)


def _rmsmod_fwd(h, g, scale, shift, *, name):
    s, d = h.shape
    ts = _rows(s, d)

    def body(h_ref, g_ref, sc_ref, sh_ref, u_ref):
        hf = h_ref[...]
        r = lax.rsqrt(jnp.mean(hf * hf, axis=-1, keepdims=True) + EPS)
        u_ref[...] = (((hf * r) * g_ref[...]) * (1.0 + sc_ref[...]) + sh_ref[...]).astype(BF16)

    row = pl.BlockSpec((ts, d), lambda i: (i, 0))
    vec = pl.BlockSpec((1, d), lambda i: (0, 0))
    return _pcall(body, name=name, out_shape=jax.ShapeDtypeStruct((s, d), BF16), grid=(s // ts,),
                  in_specs=[row, vec, vec, vec], out_specs=row)(h, g, scale, shift)


def _rmsmod_bwd(du, h, g, scale, dres, *, name):
    s, d = h.shape
    ts = _rows(s, d)

    def body(du_ref, h_ref, g_ref, sc_ref, dres_ref, dh_ref, dsh_ref, dsc_ref, dg_ref):
        @pl.when(pl.program_id(0) == 0)
        def _():
            dsh_ref[...] = jnp.zeros_like(dsh_ref)
            dsc_ref[...] = jnp.zeros_like(dsc_ref)
            dg_ref[...] = jnp.zeros_like(dg_ref)

        hf, duf, gain = h_ref[...], du_ref[...], g_ref[...]
        r = lax.rsqrt(jnp.mean(hf * hf, axis=-1, keepdims=True) + EPS)
        xh = hf * r
        dn = duf * (1.0 + sc_ref[...])
        dsh_ref[...] += jnp.sum(duf, axis=0, keepdims=True)
        dsc_ref[...] += jnp.sum(duf * (xh * gain), axis=0, keepdims=True)
        dg_ref[...] += jnp.sum(dn * xh, axis=0, keepdims=True)
        dxh = dn * gain
        dh_ref[...] = dres_ref[...] + r * (dxh - xh * jnp.mean(dxh * xh, axis=-1, keepdims=True))

    row = pl.BlockSpec((ts, d), lambda i: (i, 0))
    vec = pl.BlockSpec((1, d), lambda i: (0, 0))
    vshape = jax.ShapeDtypeStruct((1, d), F32)
    return _pcall(body, name=name, out_shape=(jax.ShapeDtypeStruct((s, d), F32), vshape, vshape, vshape),
                  grid=(s // ts,), in_specs=[row, row, vec, vec, row],
                  out_specs=(row, vec, vec, vec))(du, h, g, scale, dres)


def _resid_gate(h, gate, t, *, name):
    s, d = h.shape
    ts = _rows(s, d)

    def body(h_ref, g_ref, t_ref, o_ref):
        o_ref[...] = h_ref[...] + g_ref[...] * t_ref[...]

    row = pl.BlockSpec((ts, d), lambda i: (i, 0))
    vec = pl.BlockSpec((1, d), lambda i: (0, 0))
    return _pcall(body, name=name, out_shape=jax.ShapeDtypeStruct((s, d), F32), grid=(s // ts,),
                  in_specs=[row, vec, row], out_specs=row)(h, gate, t)


def _resid_gate_bwd(dh, t, gate, *, name):
    s, d = dh.shape
    ts = _rows(s, d)

    def body(dh_ref, t_ref, g_ref, dt_ref, dg_ref):
        @pl.when(pl.program_id(0) == 0)
        def _():
            dg_ref[...] = jnp.zeros_like(dg_ref)

        dhf = dh_ref[...]
        dt_ref[...] = (dhf * g_ref[...]).astype(BF16)
        dg_ref[...] += jnp.sum(dhf * t_ref[...], axis=0, keepdims=True)

    row = pl.BlockSpec((ts, d), lambda i: (i, 0))
    vec = pl.BlockSpec((1, d), lambda i: (0, 0))
    return _pcall(body, name=name,
                  out_shape=(jax.ShapeDtypeStruct((s, d), BF16), jax.ShapeDtypeStruct((1, d), F32)),
                  grid=(s // ts,), in_specs=[row, row, vec], out_specs=(row, vec))(dh, t, gate)


def _merge_fwd(proj, y_a, y_b, *, name):
    s, d = y_a.shape
    ts = _rows(s, d)
    ga_blk = OFF_GATES // d

    def body(ga_ref, gb_ref, ya_ref, yb_ref, o_ref):
        o_ref[...] = (jax.nn.sigmoid(ga_ref[...]) * ya_ref[...]
                      + jax.nn.sigmoid(gb_ref[...]) * yb_ref[...]).astype(BF16)

    row = pl.BlockSpec((ts, d), lambda i: (i, 0))
    ga = pl.BlockSpec((ts, d), lambda i: (i, ga_blk))
    gb = pl.BlockSpec((ts, d), lambda i: (i, ga_blk + 1))
    return _pcall(body, name=name, out_shape=jax.ShapeDtypeStruct((s, d), BF16), grid=(s // ts,),
                  in_specs=[ga, gb, row, row], out_specs=row)(proj, proj, y_a, y_b)


def _merge_bwd(dm, proj, y_a, y_b, *, name):
    s, d = y_a.shape
    ts = _rows(s, d)
    ga_blk = OFF_GATES // d

    def body(dm_ref, ga_ref, gb_ref, ya_ref, yb_ref, dya_ref, dyb_ref, dga_ref, dgb_ref):
        dmf = dm_ref[...]
        sa, sb = jax.nn.sigmoid(ga_ref[...]), jax.nn.sigmoid(gb_ref[...])
        dya_ref[...] = (dmf * sa).astype(BF16)
        dyb_ref[...] = (dmf * sb).astype(BF16)
        dga_ref[...] = (dmf * ya_ref[...] * (sa * (1.0 - sa))).astype(BF16)
        dgb_ref[...] = (dmf * yb_ref[...] * (sb * (1.0 - sb))).astype(BF16)

    row = pl.BlockSpec((ts, d), lambda i: (i, 0))
    ga = pl.BlockSpec((ts, d), lambda i: (i, ga_blk))
    gb = pl.BlockSpec((ts, d), lambda i: (i, ga_blk + 1))
    shp = jax.ShapeDtypeStruct((s, d), BF16)
    return _pcall(body, name=name, out_shape=(shp, shp, shp, shp), grid=(s // ts,),
                  in_specs=[row, ga, gb, row, row], out_specs=(row, row, row, row))(dm, proj, proj, y_a, y_b)


def _swiglu_fwd(gu, *, name):
    s, f2 = gu.shape
    f = f2 // 2
    ts = _rows(s, f)

    def body(g_ref, u_ref, a_ref):
        gf = g_ref[...]
        a_ref[...] = ((gf * jax.nn.sigmoid(gf)) * u_ref[...]).astype(BF16)

    return _pcall(body, name=name, out_shape=jax.ShapeDtypeStruct((s, f), BF16), grid=(s // ts,),
                  in_specs=[pl.BlockSpec((ts, f), lambda i: (i, 0)), pl.BlockSpec((ts, f), lambda i: (i, 1))],
                  out_specs=pl.BlockSpec((ts, f), lambda i: (i, 0)))(gu, gu)


def _swiglu_bwd(gu, da, *, name):
    s, f2 = gu.shape
    f = f2 // 2
    ts = _rows(s, f)

    def body(g_ref, u_ref, da_ref, o_ref):
        gf, daf = g_ref[...], da_ref[...]
        sg = jax.nn.sigmoid(gf)

        @pl.when(pl.program_id(1) == 0)
        def _():
            o_ref[...] = (daf * u_ref[...] * (sg * (1.0 + gf * (1.0 - sg)))).astype(BF16)

        @pl.when(pl.program_id(1) == 1)
        def _():
            o_ref[...] = (daf * (gf * sg)).astype(BF16)

    return _pcall(body, name=name, out_shape=jax.ShapeDtypeStruct((s, f2), BF16), grid=(s // ts, 2),
                  in_specs=[pl.BlockSpec((ts, f), lambda i, j: (i, 0)), pl.BlockSpec((ts, f), lambda i, j: (i, 1)),
                            pl.BlockSpec((ts, f), lambda i, j: (i, 0))],
                  out_specs=pl.BlockSpec((ts, f), lambda i, j: (i, j)))(gu, gu, da)


def _loss_fwd(y, tgt, *, name):
    s, d = y.shape
    ts = _rows(s, d)

    def body(y_ref, t_ref, l_ref, dy_ref):
        @pl.when(pl.program_id(0) == 0)
        def _():
            l_ref[...] = jnp.zeros_like(l_ref)

        e = y_ref[...] - t_ref[...]
        dy_ref[...] = e * (1.0 / d)
        per_tok = jnp.sum(e * e, axis=1, keepdims=True) * (1.0 / d)
        l_ref[...] += 0.5 * jnp.sum(per_tok, axis=0, keepdims=True)

    row = pl.BlockSpec((ts, d), lambda i: (i, 0))
    return _pcall(body, name=name,
                  out_shape=(jax.ShapeDtypeStruct((1, 128), F32), jax.ShapeDtypeStruct((s, d), F32)),
                  grid=(s // ts,), in_specs=[row, row],
                  out_specs=(pl.BlockSpec((1, 128), lambda i: (0, 0)), row))(y, tgt)


def _rope_tables(seq):
    inv = jnp.power(ROPE_THETA, -jnp.arange(0, HEAD_DIM, 2, dtype=F32) / HEAD_DIM)
    ang = jnp.arange(seq, dtype=F32)[:, None] * inv[None, :]
    cos, sin = jnp.cos(ang), jnp.sin(ang)
    return jnp.concatenate([cos, cos], axis=1), jnp.concatenate([-sin, sin], axis=1)


def _qkrope_fwd(proj, gains, cos2, sin2, *, name):
    s = proj.shape[0]
    ts = _rows(s, 4 * HEAD_DIM)

    def body(x_ref, g_ref, c_ref, s_ref, o_ref):
        x = x_ref[...]
        y = (x * lax.rsqrt(jnp.mean(x * x, axis=-1, keepdims=True) + EPS)) * g_ref[...]
        o_ref[...] = (y * c_ref[...] + pltpu.roll(y, HEAD_DIM // 2, 1) * s_ref[...]).astype(BF16)

    head = pl.BlockSpec((ts, HEAD_DIM), lambda i, j: (i, j))
    tab = pl.BlockSpec((ts, HEAD_DIM), lambda i, j: (i, 0))
    gain = pl.BlockSpec((None, 1, HEAD_DIM), lambda i, j: (j // A_HEADS, 0, 0))
    return _pcall(body, name=name, out_shape=jax.ShapeDtypeStruct((s, 2 * A_W), BF16),
                  grid=(s // ts, 2 * A_HEADS), in_specs=[head, gain, tab, tab], out_specs=head)(
                      proj, gains, cos2, sin2)


def _qkrope_bwd(dqk, proj, gains, cos2, sin2, *, name):
    s = proj.shape[0]
    ts = _rows(s, 4 * HEAD_DIM)

    def body(d_ref, x_ref, g_ref, c_ref, s_ref, dx_ref, dg_ref):
        j, i = pl.program_id(0), pl.program_id(1)

        @pl.when(jnp.logical_and(j % A_HEADS == 0, i == 0))
        def _():
            dg_ref[...] = jnp.zeros_like(dg_ref)

        dout = d_ref[...]
        dy = dout * c_ref[...] + pltpu.roll(dout * s_ref[...], HEAD_DIM // 2, 1)
        x = x_ref[...]
        r = lax.rsqrt(jnp.mean(x * x, axis=-1, keepdims=True) + EPS)
        xh = x * r
        dg_ref[...] += jnp.sum(dy * xh, axis=0, keepdims=True)
        dxh = dy * g_ref[...]
        dx_ref[...] = (r * (dxh - xh * jnp.mean(dxh * xh, axis=-1, keepdims=True))).astype(BF16)

    head = pl.BlockSpec((ts, HEAD_DIM), lambda j, i: (i, j))
    tab = pl.BlockSpec((ts, HEAD_DIM), lambda j, i: (i, 0))
    gain = pl.BlockSpec((None, 1, HEAD_DIM), lambda j, i: (j // A_HEADS, 0, 0))
    return _pcall(body, name=name,
                  out_shape=(jax.ShapeDtypeStruct((s, 2 * A_W), BF16), jax.ShapeDtypeStruct((2, 1, HEAD_DIM), F32)),
                  grid=(2 * A_HEADS, s // ts), in_specs=[head, head, gain, tab, tab],
                  out_specs=(head, gain))(dqk, proj, gains, cos2, sin2)


def _block_rows(blk):
    if isinstance(blk, int):
        return pl.ds(blk * BLOCK, BLOCK)
    return pl.ds(pl.multiple_of(blk * BLOCK, BLOCK), BLOCK)


def _band_masks(n, with_prev):
    row = lax.broadcasted_iota(jnp.int32, (BLOCK, BLOCK), 0)
    col = lax.broadcasted_iota(jnp.int32, (BLOCK, BLOCK), 1)
    cur = col <= row
    if not with_prev:
        return [(n, cur)]
    prev = col >= row + jnp.where(n >= 1, 0, BLOCK)
    return [(n, cur), (jnp.maximum(n - 1, 0), prev)]


def _dil_fwd(q_arr, k_arr, v_arr, offs, length, dil, *, name):
    nj, nb = dil * HEADS_PER_GROUP, length // BLOCK
    qo, ko, vo = offs

    def body(q_ref, k_ref, v_ref, o_ref, l_ref):
        n = pl.program_id(1)
        q = q_ref[...].astype(BF16)
        parts = []
        for blk, mask in _band_masks(n, nb > 1):
            rows = _block_rows(blk)
            sc = _dot(q, k_ref[rows, :].astype(BF16), NT) * ATT_SCALE
            parts.append((jnp.where(mask, sc, MASKED), rows))
        m = parts[0][0].max(axis=-1, keepdims=True)
        for sc, _ in parts[1:]:
            m = jnp.maximum(m, sc.max(axis=-1, keepdims=True))
        den = jnp.zeros((BLOCK, 1), F32)
        acc = jnp.zeros((BLOCK, HEAD_DIM), F32)
        for sc, rows in parts:
            p = jnp.exp(sc - m)
            den = den + jnp.sum(p, axis=-1, keepdims=True)
            acc = acc + _dot(p.astype(BF16), v_ref[rows, :].astype(BF16))
        o_ref[...] = acc / den
        l_ref[...] = jnp.broadcast_to(m + jnp.log(den), (BLOCK, HEAD_DIM))

    qspec = pl.BlockSpec((BLOCK, HEAD_DIM), lambda j, n: (n, qo + j))
    kspec = pl.BlockSpec((length, HEAD_DIM), lambda j, n: (0, ko + j))
    vspec = pl.BlockSpec((length, HEAD_DIM), lambda j, n: (0, vo + j))
    ospec = pl.BlockSpec((BLOCK, HEAD_DIM), lambda j, n: (n, j))
    shp = jax.ShapeDtypeStruct((length, nj * HEAD_DIM), F32)
    return _pcall(body, name=name, out_shape=(shp, shp), grid=(nj, nb), in_specs=[qspec, kspec, vspec],
                  out_specs=(ospec, ospec))(q_arr, k_arr, v_arr)


def _dil_bwd(q_arr, k_arr, v_arr, offs, o, lse, do, dlse, length, dil, *, name):
    nj, nb = dil * HEADS_PER_GROUP, length // BLOCK
    qo, ko, vo = offs

    def body(q_ref, k_ref, v_ref, o_ref, l_ref, do_ref, dl_ref, dq_ref, dk_ref, dv_ref):
        dk_ref[...] = jnp.zeros_like(dk_ref)
        dv_ref[...] = jnp.zeros_like(dv_ref)

        def step(n, carry):
            qrows = _block_rows(n)
            q = q_ref[qrows, :].astype(BF16)
            dof = do_ref[qrows, :]
            dob = dof.astype(BF16)
            lse_b = l_ref[qrows, :]
            shift = dl_ref[qrows, :] - jnp.sum(dof * o_ref[qrows, :], axis=-1, keepdims=True)
            dq = jnp.zeros((BLOCK, HEAD_DIM), F32)
            for blk, mask in _band_masks(n, nb > 1):
                rows = _block_rows(blk)
                kk, vv = k_ref[rows, :].astype(BF16), v_ref[rows, :].astype(BF16)
                sc = _dot(q, kk, NT) * ATT_SCALE
                p = jnp.where(mask, jnp.exp(sc - lse_b), 0.0)
                ds = (p * (_dot(dob, vv, NT) + shift)).astype(BF16)
                dq = dq + _dot(ds, kk)
                dk_ref[rows, :] += _dot(ds, q, TN) * ATT_SCALE
                dv_ref[rows, :] += _dot(p.astype(BF16), dob, TN)
            dq_ref[qrows, :] = dq * ATT_SCALE
            return carry

        if nb == 1:
            step(0, 0)
        else:
            lax.fori_loop(0, nb, step, 0)

    def col(off):
        return pl.BlockSpec((length, HEAD_DIM), lambda j: (0, off + j))

    shp = jax.ShapeDtypeStruct((length, nj * HEAD_DIM), F32)
    return _pcall(body, name=name, out_shape=(shp, shp, shp), grid=(nj,),
                  in_specs=[col(qo), col(ko), col(vo), col(0), col(0), col(0), col(0)],
                  out_specs=(col(0), col(0), col(0)))(q_arr, k_arr, v_arr, o, lse, do, dlse)


def _combine_weights(l_refs):
    ls = [r[...] for r in l_refs]
    m = jnp.maximum(jnp.maximum(ls[0], ls[1]), ls[2])
    es = [jnp.exp(l - m) for l in ls]
    den = es[0] + es[1] + es[2]
    return [e / den for e in es]


def _combine_fwd(os_, lses, *, name):
    s = os_[0].shape[0]
    ts = _rows(s, GROUP_W)

    def body(o0, o1, o2, l0, l1, l2, out_ref):
        w = _combine_weights((l0, l1, l2))
        out_ref[...] = (w[0] * o0[...] + w[1] * o1[...] + w[2] * o2[...]).astype(BF16)

    row = pl.BlockSpec((ts, GROUP_W), lambda i: (i, 0))
    return _pcall(body, name=name, out_shape=jax.ShapeDtypeStruct((s, GROUP_W), BF16), grid=(s // ts,),
                  in_specs=[row] * 6, out_specs=row)(*os_, *lses)


def _combine_bwd(do_a, os_, lses, *, name):
    s = do_a.shape[0]
    ts = _rows(s, GROUP_W)

    def body(d_ref, o0, o1, o2, l0, l1, l2, do0, do1, do2, dl0, dl1, dl2):
        w = _combine_weights((l0, l1, l2))
        d = d_ref[...]
        og = [o0[...], o1[...], o2[...]]
        oa = w[0] * og[0] + w[1] * og[1] + w[2] * og[2]
        ta = jnp.sum(d * oa, axis=-1, keepdims=True)
        for g, (do_ref, dl_ref) in enumerate(((do0, dl0), (do1, dl1), (do2, dl2))):
            do_ref[...] = w[g] * d
            dl_ref[...] = w[g] * (jnp.sum(d * og[g], axis=-1, keepdims=True) - ta)

    head = pl.BlockSpec((ts, HEAD_DIM), lambda i, h: (i, h))
    shp = jax.ShapeDtypeStruct((s, GROUP_W), F32)
    return _pcall(body, name=name, out_shape=(shp,) * 6, grid=(s // ts, HEADS_PER_GROUP),
                  in_specs=[head] * 7, out_specs=(head,) * 6)(do_a, *os_, *lses)


def _dot_exact(x, ones_mask):
    hi = x.astype(BF16)
    r1 = x - hi.astype(F32)
    mid = r1.astype(BF16)
    lo = (r1 - mid.astype(F32)).astype(BF16)
    return _dot(hi, ones_mask) + _dot(mid, ones_mask) + _dot(lo, ones_mask)


def _sb_scores(q, kk, j, i):
    row = lax.broadcasted_iota(jnp.int32, (BLOCK, BLOCK), 0)
    col = lax.broadcasted_iota(jnp.int32, (BLOCK, BLOCK), 1)
    mask = col < row + jnp.where(j < i, BLOCK, 0)
    z = _dot(q, kk, NT) * ATT_SCALE
    sp = jnp.log1p(jnp.exp(-jnp.abs(z)))
    log_beta = jnp.minimum(z, 0.0) - sp
    log_1mb = jnp.where(mask, jnp.minimum(-z, 0.0) - sp, 0.0)
    return z, log_beta, log_1mb, mask


def _sb_weights(log_beta, log_1mb, mask, run, upper):
    after = run + _dot_exact(log_1mb, upper)
    return jnp.where(mask, jnp.exp(log_beta + after), 0.0)


def _tri(strict_lower):
    row = lax.broadcasted_iota(jnp.int32, (BLOCK, BLOCK), 0)
    col = lax.broadcasted_iota(jnp.int32, (BLOCK, BLOCK), 1)
    return ((row > col) if strict_lower else (row < col)).astype(BF16)


def _sb_fwd(proj, *, name):
    s = proj.shape[0]
    nqb = s // BLOCK
    qb, kb, vb = OFF_QB // HEAD_DIM, OFF_KB // HEAD_DIM, OFF_VB // HEAD_DIM

    def body(q_ref, k_ref, v_ref, o_ref):
        i = pl.program_id(1)
        q = q_ref[...].astype(BF16)
        upper = _tri(True)

        def step(t, carry):
            acc, run = carry
            j = i - t
            rows = _block_rows(j)
            _, log_beta, log_1mb, mask = _sb_scores(q, k_ref[rows, :].astype(BF16), j, i)
            a = _sb_weights(log_beta, log_1mb, mask, run, upper)
            acc = acc + _dot(a.astype(BF16), v_ref[rows, :].astype(BF16))
            return acc, run + jnp.sum(log_1mb, axis=-1, keepdims=True)

        acc, _ = lax.fori_loop(0, i + 1, step, (jnp.zeros((BLOCK, HEAD_DIM), F32), jnp.zeros((BLOCK, 1), F32)))
        o_ref[...] = acc.astype(BF16)

    return _pcall(body, name=name, out_shape=jax.ShapeDtypeStruct((s, B_W), BF16), grid=(SB_HEADS, nqb),
                  in_specs=[pl.BlockSpec((BLOCK, HEAD_DIM), lambda h, i: (i, qb + h)),
                            pl.BlockSpec((s, HEAD_DIM), lambda h, i: (0, kb + h)),
                            pl.BlockSpec((s, HEAD_DIM), lambda h, i: (0, vb + h))],
                  out_specs=pl.BlockSpec((BLOCK, HEAD_DIM), lambda h, i: (i, h)))(proj, proj, proj)


def _sb_bwd(proj, do_b, *, name):
    s = proj.shape[0]
    nqb = s // BLOCK
    qb, kb, vb = OFF_QB // HEAD_DIM, OFF_KB // HEAD_DIM, OFF_VB // HEAD_DIM

    def body(q_ref, k_ref, v_ref, do_ref, dq_ref, dk_ref, dv_ref, z_s, a_s):
        i = pl.program_id(1)

        @pl.when(i == 0)
        def _():
            dk_ref[...] = jnp.zeros_like(dk_ref)
            dv_ref[...] = jnp.zeros_like(dv_ref)

        q = q_ref[...].astype(BF16)
        dob = do_ref[...].astype(BF16)
        upper, lower = _tri(True), _tri(False)

        def recompute(t, run):
            j = i - t
            rows = _block_rows(j)
            z, log_beta, log_1mb, mask = _sb_scores(q, k_ref[rows, :].astype(BF16), j, i)
            z_s[j] = z
            a_s[j] = _sb_weights(log_beta, log_1mb, mask, run, upper)
            return run + jnp.sum(log_1mb, axis=-1, keepdims=True)

        lax.fori_loop(0, i + 1, recompute, jnp.zeros((BLOCK, 1), F32))

        def grads(j, carry):
            dq, run = carry
            rows = _block_rows(j)
            kk, vv = k_ref[rows, :].astype(BF16), v_ref[rows, :].astype(BF16)
            z, a = z_s[j], a_s[j]
            row = lax.broadcasted_iota(jnp.int32, (BLOCK, BLOCK), 0)
            col = lax.broadcasted_iota(jnp.int32, (BLOCK, BLOCK), 1)
            mask = col < row + jnp.where(j < i, BLOCK, 0)
            de = _dot(dob, vv, NT) * a
            before = run + _dot_exact(de, lower)
            dz = (de * jax.nn.sigmoid(-z) - jnp.where(mask, jax.nn.sigmoid(z), 0.0) * before).astype(BF16)
            dk_ref[rows, :] += _dot(dz, q, TN) * ATT_SCALE
            dv_ref[rows, :] += _dot(a.astype(BF16), dob, TN)
            return dq + _dot(dz, kk), run + jnp.sum(de, axis=-1, keepdims=True)

        dq, _ = lax.fori_loop(0, i + 1, grads, (jnp.zeros((BLOCK, HEAD_DIM), F32), jnp.zeros((BLOCK, 1), F32)))
        dq_ref[...] = dq * ATT_SCALE

    blk = pl.BlockSpec((BLOCK, HEAD_DIM), lambda h, i: (i, h))
    full = pl.BlockSpec((s, HEAD_DIM), lambda h, i: (0, h))
    shp = jax.ShapeDtypeStruct((s, B_W), F32)
    return _pcall(body, name=name, out_shape=(shp, shp, shp), grid=(SB_HEADS, nqb),
                  in_specs=[pl.BlockSpec((BLOCK, HEAD_DIM), lambda h, i: (i, qb + h)),
                            pl.BlockSpec((s, HEAD_DIM), lambda h, i: (0, kb + h)),
                            pl.BlockSpec((s, HEAD_DIM), lambda h, i: (0, vb + h)), blk],
                  out_specs=(blk, full, full),
                  scratch=[pltpu.VMEM((nqb, BLOCK, BLOCK), F32), pltpu.VMEM((nqb, BLOCK, BLOCK), F32)])(
                      proj, proj, proj, do_b)


def _coords():
    return lax.axis_index("x"), lax.axis_index("y"), lax.axis_index("c")


def _flip(v, bit):
    return 1 - v if bit else v


def _shard_of(ref, axis, idx, size):
    if axis == 0:
        sl = pl.ds(pl.multiple_of(idx * size, 16), size)
        return ref.at[sl, :] if len(ref.shape) == 2 else ref.at[:, sl, :]
    sl = pl.ds(pl.multiple_of(idx * size, 128), size)
    return ref.at[:, sl] if len(ref.shape) == 2 else ref.at[:, :, sl]


def _small_allgather(v, *, name, silu=False):
    n = v.shape[1]

    def body(v_ref, out_ref, send_sems, recv_sems):
        x, y, c = _coords()
        me = 4 * x + 2 * y + c
        val = v_ref[...]
        out_ref[me] = val * jax.nn.sigmoid(val) if silu else val
        copies = []
        for k in range(1, N_DEV):
            peer = (_flip(x, k & 4), _flip(y, k & 2), _flip(c, k & 1))
            copies.append(pltpu.make_async_remote_copy(
                src_ref=out_ref.at[me], dst_ref=out_ref.at[me], send_sem=send_sems.at[k - 1],
                recv_sem=recv_sems.at[k - 1], device_id=peer, device_id_type=MESH))
        for cp in copies:
            cp.start()
        for cp in copies:
            cp.wait_recv()
        for cp in copies:
            cp.wait_send()

    return _pcall(body, name=name, out_shape=jax.ShapeDtypeStruct((N_DEV, 1, n), F32),
                  in_specs=[pl.BlockSpec(memory_space=pltpu.VMEM)], out_specs=pl.BlockSpec(memory_space=pltpu.VMEM),
                  scratch=[pltpu.SemaphoreType.DMA((N_DEV - 1,)), pltpu.SemaphoreType.DMA((N_DEV - 1,))])(v)


def _gather_weights(shards, axes, *, name):
    nt = len(shards)
    sizes = [sh.shape[ax] for sh, ax in zip(shards, axes)]
    out_shape = tuple(
        jax.ShapeDtypeStruct(tuple(d * N_DEV if a == ax else d for a, d in enumerate(sh.shape)), sh.dtype)
        for sh, ax in zip(shards, axes))

    def body(*refs):
        ins, outs = refs[:nt], refs[nt:2 * nt]
        send_sems, recv_sems, local_sems = refs[2 * nt:]
        x, y, c = _coords()
        me, sibling = (x, y, c), (x, y, 1 - c)
        chips = [(1 - x, y), (x, 1 - y), (1 - x, 1 - y)]

        def slot(t, dev):
            return _shard_of(outs[t], axes[t], 4 * dev[0] + 2 * dev[1] + dev[2], sizes[t])

        def copy(t, k, block, to, src=None):
            return pltpu.make_async_remote_copy(
                src_ref=slot(t, block) if src is None else src, dst_ref=slot(t, block),
                send_sem=send_sems.at[t, k], recv_sem=recv_sems.at[t, k], device_id=to, device_id_type=MESH)

        mine = [pltpu.make_async_copy(ins[t], slot(t, me), local_sems.at[t]) for t in range(nt)]
        for cp in mine:
            cp.start()
        first = []
        for t in range(nt):
            first.append(copy(t, 0, me, sibling, src=ins[t]))
            first += [copy(t, 1 + j, me, (*chip, c), src=ins[t]) for j, chip in enumerate(chips)]
        for cp in first:
            cp.start()
        passed = []
        for j, chip in enumerate(chips):
            for t in range(nt):
                copy(t, 1 + j, (*chip, c), me).wait_recv()
                fwd = copy(t, 4 + j, (*chip, c), sibling)
                fwd.start()
                passed.append(fwd)
        for t in range(nt):
            copy(t, 0, sibling, me).wait_recv()
            for j, chip in enumerate(chips):
                copy(t, 4 + j, (*chip, 1 - c), me).wait_recv()
        for cp in first + passed:
            cp.wait_send()
        for cp in mine:
            cp.wait()

    hbm = pl.BlockSpec(memory_space=pl.ANY)
    return _pcall(body, name=name, out_shape=out_shape, in_specs=[hbm] * nt, out_specs=(hbm,) * nt,
                  scratch=[pltpu.SemaphoreType.DMA((nt, 7)), pltpu.SemaphoreType.DMA((nt, 7)),
                           pltpu.SemaphoreType.DMA((nt,))])(*shards)


def _rs_pair_exchange(grads, axes, *, name):
    nt = len(grads)
    sizes = [g.shape[1 + ax] // N_DEV for g, ax in zip(grads, axes)]

    def recv_shape(g, ax):
        dims = list(g.shape)
        dims[1 + ax] //= N_DEV
        return jax.ShapeDtypeStruct((dims[0], N_CHIPS, dims[1], dims[2]), g.dtype)

    def body(*refs):
        ins, outs = refs[:nt], refs[nt:2 * nt]
        send_sems, recv_sems = refs[2 * nt:]
        x, y, c = _coords()
        copies = []
        for t in range(nt):
            for q in range(N_CHIPS):
                copies.append(pltpu.make_async_remote_copy(
                    src_ref=_shard_of(ins[t], axes[t], 2 * q + 1 - c, sizes[t]), dst_ref=outs[t].at[:, q],
                    send_sem=send_sems.at[t, q], recv_sem=recv_sems.at[t, q], device_id=(x, y, 1 - c),
                    device_id_type=MESH))
        for cp in copies:
            cp.start()
        for cp in copies:
            cp.wait_recv()
        for cp in copies:
            cp.wait_send()

    hbm = pl.BlockSpec(memory_space=pl.ANY)
    return _pcall(body, name=name, out_shape=tuple(recv_shape(g, ax) for g, ax in zip(grads, axes)),
                  in_specs=[hbm] * nt, out_specs=(hbm,) * nt,
                  scratch=[pltpu.SemaphoreType.DMA((nt, N_CHIPS)), pltpu.SemaphoreType.DMA((nt, N_CHIPS))])(*grads)


def _pair_sum(grad, sib, core, axis, *, name):
    nl, _, r, c = sib.shape
    tr = _rows(r, c)
    nrt = r // tr

    def body(core_ref, g_ref, s_ref, o_ref):
        o_ref[...] = (g_ref[...].astype(F32) + s_ref[...].astype(F32)).astype(BF16)

    if axis == 0:
        gspec = pl.BlockSpec((None, tr, c), lambda l, q, i, core_ref: (l, (2 * q + core_ref[0]) * nrt + i, 0))
    else:
        gspec = pl.BlockSpec((None, tr, c), lambda l, q, i, core_ref: (l, i, 2 * q + core_ref[0]))
    sspec = pl.BlockSpec((None, None, tr, c), lambda l, q, i, core_ref: (l, q, i, 0))
    grid_spec = pltpu.PrefetchScalarGridSpec(num_scalar_prefetch=1, grid=(nl, N_CHIPS, nrt),
                                             in_specs=[gspec, sspec], out_specs=sspec)
    return _pcall(body, name=name, out_shape=jax.ShapeDtypeStruct(sib.shape, BF16), grid_spec=grid_spec)(
        core, grad, sib)


def _rs_chip_exchange(sums, *, name):
    nt = len(sums)

    def body(*refs):
        ins, outs = refs[:nt], refs[nt:2 * nt]
        send_sems, recv_sems = refs[2 * nt:]
        x, y, c = _coords()
        copies = []
        for t in range(nt):
            for k in range(1, N_CHIPS):
                px, py = _flip(x, k & 2), _flip(y, k & 1)
                copies.append(pltpu.make_async_remote_copy(
                    src_ref=ins[t].at[:, 2 * px + py], dst_ref=outs[t].at[:, k - 1],
                    send_sem=send_sems.at[t, k - 1], recv_sem=recv_sems.at[t, k - 1], device_id=(px, py, c),
                    device_id_type=MESH))
        for cp in copies:
            cp.start()
        for cp in copies:
            cp.wait_recv()
        for cp in copies:
            cp.wait_send()

    hbm = pl.BlockSpec(memory_space=pl.ANY)
    out_shape = tuple(jax.ShapeDtypeStruct((s.shape[0], N_CHIPS - 1) + s.shape[2:], s.dtype) for s in sums)
    return _pcall(body, name=name, out_shape=out_shape, in_specs=[hbm] * nt, out_specs=(hbm,) * nt,
                  scratch=[pltpu.SemaphoreType.DMA((nt, N_CHIPS - 1)),
                           pltpu.SemaphoreType.DMA((nt, N_CHIPS - 1))])(*sums)


def _adam_math(g, w, m, v):
    m2 = ADAM_B1 * m + (1.0 - ADAM_B1) * g
    v2 = ADAM_B2 * v + (1.0 - ADAM_B2) * (g * g)
    m_hat = m2 / (1.0 - ADAM_B1 ** ADAM_STEP)
    v_hat = v2 / (1.0 - ADAM_B2 ** ADAM_STEP)
    delta = -ADAM_LR * (m_hat / (jnp.sqrt(v_hat) + ADAM_EPS) + ADAM_WD * w)
    return delta, m2, v2


def _adamw_sharded(chip_sums, remote, chip, w, m, v, *, name):
    nl, r, c = w.shape
    tr = _rows(r, c)

    def body(chip_ref, p_ref, r0_ref, r1_ref, r2_ref, w_ref, m_ref, v_ref, g_out, d_out, m_out, v_out):
        g = ((p_ref[...].astype(F32) + r0_ref[...].astype(F32)) + r1_ref[...].astype(F32)) + r2_ref[...].astype(F32)
        g_out[...] = g
        d_out[...], m_out[...], v_out[...] = _adam_math(g, w_ref[...], m_ref[...], v_ref[...])

    pspec = pl.BlockSpec((None, None, tr, c), lambda l, i, chip_ref: (l, chip_ref[0], i, 0))

    def rspec(k):
        return pl.BlockSpec((None, None, tr, c), lambda l, i, chip_ref: (l, k, i, 0))

    wspec = pl.BlockSpec((None, tr, c), lambda l, i, chip_ref: (l, i, 0))
    grid_spec = pltpu.PrefetchScalarGridSpec(
        num_scalar_prefetch=1, grid=(nl, r // tr),
        in_specs=[pspec, rspec(0), rspec(1), rspec(2), wspec, wspec, wspec], out_specs=(wspec,) * 4)
    shp = jax.ShapeDtypeStruct(w.shape, F32)
    return _pcall(body, name=name, out_shape=(shp,) * 4, grid_spec=grid_spec)(
        chip, chip_sums, remote, remote, remote, w, m, v)


def _adamw_local(g, w, m, v, *, name):
    nl, r, c = w.shape
    tr = _rows(r, c)

    def body(g_ref, w_ref, m_ref, v_ref, d_out, m_out, v_out):
        d_out[...], m_out[...], v_out[...] = _adam_math(g_ref[...], w_ref[...], m_ref[...], v_ref[...])

    spec = pl.BlockSpec((None, tr, c), lambda l, i: (l, i, 0))
    shp = jax.ShapeDtypeStruct(w.shape, F32)
    return _pcall(body, name=name, out_shape=(shp,) * 3, grid=(nl, r // tr), in_specs=[spec] * 4,
                  out_specs=(spec,) * 3)(g, w, m, v)


def _adamw_replicated(parts, w, m, v, *, name):
    n = w.shape[1]

    def body(p_ref, w_ref, m_ref, v_ref, g_out, d_out, m_out, v_out):
        g = p_ref[0]
        for k in range(1, N_DEV):
            g = g + p_ref[k]
        g_out[...] = g
        d_out[...], m_out[...], v_out[...] = _adam_math(g, w_ref[...], m_ref[...], v_ref[...])

    vm = pl.BlockSpec(memory_space=pltpu.VMEM)
    shp = jax.ShapeDtypeStruct((1, n), F32)
    return _pcall(body, name=name, out_shape=(shp,) * 4, in_specs=[vm] * 4, out_specs=(vm,) * 4)(parts, w, m, v)


def _group_views(qk, proj, g, dil, seq):
    if dil == 1:
        return (qk, qk, proj), (0, A_HEADS, 2 * A_HEADS)
    length = seq // dil
    lo = g * GROUP_W
    q = qk[:, lo:lo + GROUP_W].reshape(length, dil * GROUP_W)
    k = qk[:, A_W + lo:A_W + lo + GROUP_W].reshape(length, dil * GROUP_W)
    v = proj[:, OFF_VA + lo:OFF_VA + lo + GROUP_W].astype(BF16).reshape(length, dil * GROUP_W)
    return (q, k, v), (0, 0, 0)


def _mod_rows(mod, d):
    return [mod[:, i * d:(i + 1) * d] for i in range(6)]


def _layer_fwd(h, mod, g1, g2, gains, wts, cos2, sin2):
    seq, d = h.shape
    sh1, sc1, ga1, sh2, sc2, ga2 = _mod_rows(mod, d)
    sv = {"h_in": h}
    u = _rmsmod_fwd(h, g1, sc1, sh1, name="rmsmod_fwd")
    proj = _mm(u, wts["w_in"], name="mm_in")
    qk = _qkrope_fwd(proj, gains, cos2, sin2, name="qkrope_fwd")
    os_, lses = [], []
    for g, dil in enumerate(DILATIONS):
        arrs, offs = _group_views(qk, proj, g, dil, seq)
        o, lse = _dil_fwd(*arrs, offs, seq // dil, dil, name=f"dil_fwd_{dil}")
        os_.append(o.reshape(seq, GROUP_W))
        lses.append(lse.reshape(seq, GROUP_W))
    o_a = _combine_fwd(os_, lses, name="combine_fwd")
    o_b = _sb_fwd(proj, name="sb_fwd")
    y_a = _mm(o_a, wts["w_branch_a"], name="mm_branch")
    y_b = _mm(o_b, wts["w_branch_b"], name="mm_branch")
    merged = _merge_fwd(proj, y_a, y_b, name="merge_fwd")
    t = _mm(merged, wts["w_out"], name="mm_out")
    h_mid = _resid_gate(h, ga1, t, name="resid_gate")
    u2 = _rmsmod_fwd(h_mid, g2, sc2, sh2, name="rmsmod_fwd")
    gu = _mm(u2, wts["w_gate_up"], name="mm_gate_up")
    a = _swiglu_fwd(gu, name="swiglu_fwd")
    f = _mm(a, wts["w_down"], name="mm_down")
    h_out = _resid_gate(h_mid, ga2, f, name="resid_gate")
    sv.update(u=u, proj=proj, qk=qk, os=os_, lses=lses, o_a=o_a, o_b=o_b, y_a=y_a, y_b=y_b, merged=merged, t=t,
              h_mid=h_mid, u2=u2, gu=gu, a=a, f=f)
    return h_out, sv


def _layer_bwd(dh, sv, mod, g1, g2, gains, wts, cos2, sin2, layer, n_layers, gbufs):
    seq, d = dh.shape
    sh1, sc1, ga1, sh2, sc2, ga2 = _mod_rows(mod, d)

    def wgrad(key, act, dout):
        gbufs[key] = _mm(act, dout, ta=True, out_dtype=BF16, name="mm_wgrad_" + key,
                         stack=(layer, n_layers, gbufs.get(key)))

    df, dgate2 = _resid_gate_bwd(dh, sv["f"], ga2, name="resid_gate_bwd")
    da = _mm(df, wts["w_down"], tb=True, name="mm_down_t")
    wgrad("w_down", sv["a"], df)
    dgu = _swiglu_bwd(sv["gu"], da, name="swiglu_bwd")
    du2 = _mm(dgu, wts["w_gate_up"], tb=True, name="mm_gate_up_t")
    wgrad("w_gate_up", sv["u2"], dgu)
    dh_mid, dsh2, dsc2, dg2 = _rmsmod_bwd(du2, sv["h_mid"], g2, sc2, dh, name="rmsmod_bwd")

    dt, dgate1 = _resid_gate_bwd(dh_mid, sv["t"], ga1, name="resid_gate_bwd")
    dmerged = _mm(dt, wts["w_out"], tb=True, name="mm_out_t")
    wgrad("w_out", sv["merged"], dt)
    dy_a, dy_b, dga, dgb = _merge_bwd(dmerged, sv["proj"], sv["y_a"], sv["y_b"], name="merge_bwd")
    do_a = _mm(dy_a, wts["w_branch_a"], tb=True, name="mm_branch_t")
    do_b = _mm(dy_b, wts["w_branch_b"], tb=True, name="mm_branch_t")
    wgrad("w_branch_a", sv["o_a"], dy_a)
    wgrad("w_branch_b", sv["o_b"], dy_b)
    dqb, dkb, dvb = _sb_bwd(sv["proj"], do_b, name="sb_bwd")
    comb = _combine_bwd(do_a, sv["os"], sv["lses"], name="combine_bwd")
    dos, dls = comb[:3], comb[3:]
    dqs, dks, dvs = [], [], []
    for g, dil in enumerate(DILATIONS):
        length = seq // dil
        arrs, offs = _group_views(sv["qk"], sv["proj"], g, dil, seq)
        view = (length, dil * GROUP_W)
        dq, dk, dv = _dil_bwd(*arrs, offs, sv["os"][g].reshape(view), sv["lses"][g].reshape(view),
                              dos[g].reshape(view), dls[g].reshape(view), length, dil, name=f"dil_bwd_{dil}")
        dqs.append(dq.reshape(seq, GROUP_W))
        dks.append(dk.reshape(seq, GROUP_W))
        dvs.append(dv.reshape(seq, GROUP_W))
    dqk, dgains = _qkrope_bwd(jnp.concatenate(dqs + dks, axis=1), sv["proj"], gains, cos2, sin2,
                              name="qkrope_bwd")
    dproj = jnp.concatenate(
        [dqk] + [t_.astype(BF16) for t_ in dvs + [dqb, dkb, dvb]] + [dga, dgb], axis=1)
    du = _mm(dproj, wts["w_in"], tb=True, name="mm_in_t")
    wgrad("w_in", sv["u"], dproj)
    dh_in, dsh1, dsc1, dg1 = _rmsmod_bwd(du, sv["h_in"], g1, sc1, dh_mid, name="rmsmod_bwd")
    dmod = jnp.concatenate([dsh1, dsc1, dgate1, dsh2, dsc2, dgate2], axis=1)
    return dh_in, dmod, dg1, dg2, dgains


GATHERED = ("w_in", "w_branch_a", "w_branch_b", "w_out", "w_gate_up", "w_down")
SHARD_AXIS = {"w_in": 1, "w_branch_a": 1, "w_branch_b": 1, "w_out": 0, "w_gate_up": 1, "w_down": 0}


def kernel(x, c, w_ada, b_ada, norm1_g, norm2_g, w_in, qn_g, kn_g, w_branch_a, w_branch_b, w_out, w_gate_up, w_down, loss_target, m_w_ada, m_b_ada, m_norm1_g, m_norm2_g, m_w_in, m_qn_g, m_kn_g, m_w_branch_a, m_w_branch_b, m_w_out, m_w_gate_up, m_w_down, v_w_ada, v_b_ada, v_norm1_g, v_norm2_g, v_w_in, v_qn_g, v_kn_g, v_w_branch_a, v_w_branch_b, v_w_out, v_w_gate_up, v_w_down):
    seq, d = x.shape[1], x.shape[2]
    depth = w_in.shape[0]
    weights = dict(w_in=w_in, w_branch_a=w_branch_a, w_branch_b=w_branch_b, w_out=w_out, w_gate_up=w_gate_up,
                   w_down=w_down)
    moments_m = dict(w_in=m_w_in, w_branch_a=m_w_branch_a, w_branch_b=m_w_branch_b, w_out=m_w_out,
                     w_gate_up=m_w_gate_up, w_down=m_w_down)
    moments_v = dict(w_in=v_w_in, w_branch_a=v_w_branch_a, w_branch_b=v_w_branch_b, w_out=v_w_out,
                     w_gate_up=v_w_gate_up, w_down=v_w_down)
    xi, yi, ci = _coords()
    me = 4 * xi + 2 * yi + ci
    core = jnp.reshape(ci, (1,)).astype(jnp.int32)
    chip = jnp.reshape(2 * xi + yi, (1,)).astype(jnp.int32)

    ada_w = w_ada.shape[2]
    c_act = _small_allgather(c, name="comm_gather_c", silu=True).reshape(N_DEV, d)
    c_pad = jnp.concatenate([c_act, jnp.zeros_like(c_act)], axis=0).astype(BF16)
    bias = lax.dynamic_slice(b_ada, (0, me * ada_w), (depth, ada_w))
    mod_part = jnp.stack([_mm(c_pad, w_ada[l], name="mm_ada")[:N_DEV] for l in range(depth)]) + bias[:, None, :]
    mod_all = _small_allgather(mod_part.reshape(1, depth * N_DEV * ada_w), name="comm_gather_mod")
    mod_all = mod_all.reshape(N_DEV, depth, N_DEV, ada_w)
    mod_mine = lax.dynamic_index_in_dim(mod_all, me, axis=2, keepdims=False)
    mods = jnp.transpose(mod_mine, (1, 0, 2)).reshape(depth, 1, 6 * d)

    full = []
    for l in range(depth):
        shards = [weights[k][l].astype(BF16) for k in GATHERED]
        outs = _gather_weights(shards, [SHARD_AXIS[k] for k in GATHERED], name="comm_gather_weights")
        full.append(dict(zip(GATHERED, outs)))

    cos2, sin2 = _rope_tables(seq)
    gains = [jnp.stack([qn_g[l], kn_g[l]])[:, None, :] for l in range(depth)]
    g1s = [norm1_g[l][None] for l in range(depth)]
    g2s = [norm2_g[l][None] for l in range(depth)]

    h = x[0]
    saved = []
    for l in range(depth):
        h, sv = _layer_fwd(h, mods[l], g1s[l], g2s[l], gains[l], full[l], cos2, sin2)
        saved.append(sv)
    loss_part, dh = _loss_fwd(h, loss_target[0], name="loss")
    loss = lax.psum(loss_part[0, 0], ("x", "y", "c"))
    gbufs = {}
    dmods, dg1s, dg2s, dgains = [None] * depth, [None] * depth, [None] * depth, [None] * depth
    for l in reversed(range(depth)):
        dh, dmods[l], dg1s[l], dg2s[l], dgains[l] = _layer_bwd(
            dh, saved[l], mods[l], g1s[l], g2s[l], gains[l], full[l], cos2, sin2, l, depth, gbufs)
    grad_x = dh[None]

    small = jnp.concatenate(
        dmods + dg1s + dg2s + [dgains[l][0] for l in range(depth)] + [dgains[l][1] for l in range(depth)], axis=1)
    small_all = _small_allgather(small, name="comm_gather_small")

    def pack(b, n1, n2, qn, kn):
        return jnp.concatenate([t_.reshape(1, -1) for t_ in (b, n1, n2, qn, kn)], axis=1)

    sg, sd, sm, sv_ = _adamw_replicated(small_all, pack(b_ada, norm1_g, norm2_g, qn_g, kn_g),
                                        pack(m_b_ada, m_norm1_g, m_norm2_g, m_qn_g, m_kn_g),
                                        pack(v_b_ada, v_norm1_g, v_norm2_g, v_qn_g, v_kn_g), name="adamw_replicated")

    def unpack(p):
        sizes = [depth * 6 * d, depth * d, depth * d, depth * HEAD_DIM, depth * HEAD_DIM]
        shapes = [b_ada.shape, norm1_g.shape, norm2_g.shape, qn_g.shape, kn_g.shape]
        out, off = [], 0
        for n, shp in zip(sizes, shapes):
            out.append(p[0, off:off + n].reshape(shp))
            off += n
        return dict(zip(("b_ada", "norm1_g", "norm2_g", "qn_g", "kn_g"), out))

    ug, ud, um, uv = unpack(sg), unpack(sd), unpack(sm), unpack(sv_)
    res = {k: dict(g=ug[k], d=ud[k], m=um[k], v=uv[k]) for k in ug}

    dmod_all = small_all[:, 0, :depth * 6 * d].reshape(N_DEV, depth, 6 * d)
    g_ada = None
    for l in range(depth):
        dm = lax.dynamic_slice(dmod_all[:, l, :], (0, me * ada_w), (N_DEV, ada_w))
        dm = jnp.concatenate([dm, jnp.zeros_like(dm)], axis=0).astype(BF16)
        g_ada = _mm(c_pad, dm, ta=True, name="mm_wgrad_ada", stack=(l, depth, g_ada))
    d_ada, m_ada, v_ada = _adamw_local(g_ada, w_ada, m_w_ada, v_w_ada, name="adamw_local")
    res["w_ada"] = dict(g=g_ada, d=d_ada, m=m_ada, v=v_ada)

    axes = [SHARD_AXIS[k] for k in GATHERED]
    sib = _rs_pair_exchange([gbufs[k] for k in GATHERED], axes, name="comm_rs_pair")
    sums = [_pair_sum(gbufs[k], s_, core, ax, name="pair_sum_" + k) for k, s_, ax in zip(GATHERED, sib, axes)]
    remote = _rs_chip_exchange(sums, name="comm_rs_chip")
    for k, p_, r_ in zip(GATHERED, sums, remote):
        g_, d_, m_, v_ = _adamw_sharded(p_, r_, chip, weights[k], moments_m[k], moments_v[k], name="adamw_" + k)
        res[k] = dict(g=g_, d=d_, m=m_, v=v_)

    order = ("w_ada", "b_ada", "norm1_g", "norm2_g", "w_in", "qn_g", "kn_g", "w_branch_a", "w_branch_b", "w_out",
             "w_gate_up", "w_down")
    return (loss, grad_x, *[res[k]["g"] for k in order], *[res[k]["d"] for k in order],
            *[res[k]["m"] for k in order], *[res[k]["v"] for k in order])
```

```python
import functools

import jax
import jax.numpy as jnp
from jax import lax
from jax.experimental import pallas as pl
from jax.experimental.pallas import tpu as pltpu

F32 = jnp.float32
BF16 = jnp.bfloat16

HEAD_DIM = 128
BLOCK = 128
DILATIONS = (1, 4, 16)
HEADS_PER_GROUP = 4
A_HEADS = 12
SB_HEADS = 4
GROUP_W = HEADS_PER_GROUP * HEAD_DIM
A_W = A_HEADS * HEAD_DIM
B_W = SB_HEADS * HEAD_DIM
OFF_QA, OFF_KA, OFF_VA = 0, A_W, 2 * A_W
OFF_QB, OFF_KB, OFF_VB = 3 * A_W, 3 * A_W + B_W, 3 * A_W + 2 * B_W
OFF_GATES = 3 * A_W + 3 * B_W
ROPE_THETA = 10000.0
EPS = 1e-6
ATT_SCALE = HEAD_DIM ** -0.5
MASKED = -1e30

ADAM_LR, ADAM_B1, ADAM_B2, ADAM_EPS, ADAM_WD, ADAM_STEP = 0.001, 0.9, 0.999, 1e-08, 0.01, 10

N_DEV = 8
N_CHIPS = 4
V7X_VMEM_LIMIT_BYTES = 56 * 1024 * 1024
ELEMWISE_BLOCK_BYTES = 1024 * 1024
MESH = pl.DeviceIdType.MESH

NN = (((1,), (0,)), ((), ()))
NT = (((1,), (1,)), ((), ()))
TN = (((0,), (0,)), ((), ()))


def _dot(a, b, dims=NN):
    return lax.dot_general(a, b, dims, preferred_element_type=F32)


def _tile(n, cap, mult=128):
    best = None
    for t in range(mult, min(n, cap) + 1, mult):
        if n % t == 0:
            best = t
    if best is None:
        assert n <= 2 * cap, (n, cap)
        return n
    return best


def _rows(r, c):
    return _tile(r, max(16, ELEMWISE_BLOCK_BYTES // (4 * c)), 16)


def _pcall(body, *, name, out_shape, grid=None, in_specs=None, out_specs=None, scratch=(), aliases=None,
           grid_spec=None):
    kwargs = {}
    if grid_spec is not None:
        kwargs["grid_spec"] = grid_spec
    else:
        if grid is not None:
            kwargs["grid"] = grid
        kwargs["in_specs"] = in_specs
        kwargs["out_specs"] = out_specs
        kwargs["scratch_shapes"] = list(scratch)
    return pl.pallas_call(
        body, name=name, out_shape=out_shape, input_output_aliases=aliases or {},
        compiler_params=pltpu.CompilerParams(vmem_limit_bytes=V7X_VMEM_LIMIT_BYTES), **kwargs)


def _mm(a, b, *, name, ta=False, tb=False, out_dtype=F32, caps=(1024, 1024, 1024), stack=None):
    kdim, m = a.shape if ta else a.shape[::-1]
    n, k2 = b.shape if tb else b.shape[::-1]
    assert kdim == k2, (a.shape, b.shape, ta, tb)
    tm, tn, tk = _tile(m, caps[0]), _tile(n, caps[1]), _tile(kdim, caps[2])
    nk = kdim // tk
    dims = (((0 if ta else 1,), (1 if tb else 0,)), ((), ()))

    def body(*refs):
        a_ref, b_ref = refs[0], refs[1]
        part = _dot(a_ref[...].astype(BF16), b_ref[...].astype(BF16), dims)
        if nk == 1:
            o_ref = refs[-1]
            o_ref[...] = part.astype(o_ref.dtype)
            return
        o_ref, acc_ref = refs[-2], refs[-1]
        k = pl.program_id(2)

        @pl.when(k == 0)
        def _():
            acc_ref[...] = part

        @pl.when(k > 0)
        def _():
            acc_ref[...] += part

        @pl.when(k == nk - 1)
        def _():
            o_ref[...] = acc_ref[...].astype(o_ref.dtype)

    a_spec = (pl.BlockSpec((tk, tm), lambda i, j, k: (k, i)) if ta
              else pl.BlockSpec((tm, tk), lambda i, j, k: (i, k)))
    b_spec = (pl.BlockSpec((tn, tk), lambda i, j, k: (j, k)) if tb
              else pl.BlockSpec((tk, tn), lambda i, j, k: (k, j)))
    ins, in_specs, aliases = [a, b], [a_spec, b_spec], {}
    if stack is None:
        out_shape = jax.ShapeDtypeStruct((m, n), out_dtype)
        out_spec = pl.BlockSpec((tm, tn), lambda i, j, k: (i, j))
    else:
        layer, n_layers, buf = stack
        out_shape = jax.ShapeDtypeStruct((n_layers, m, n), out_dtype)
        out_spec = pl.BlockSpec((None, tm, tn), lambda i, j, k: (layer, i, j))
        if buf is not None:
            ins.append(buf)
            in_specs.append(pl.BlockSpec(memory_space=pl.ANY))
            aliases = {2: 0}
    scratch = [] if nk == 1 else [pltpu.VMEM((tm, tn), F32)]
    return _pcall(body, name=name, out_shape=out_shape, grid=(m // tm, n // tn, nk), in_specs=in_specs,
                  out_specs=out_spec, scratch=scratch, aliases=aliases)(*ins)


def _rmsmod_fwd(h, g, scale, shift, *, name):
    s, d = h.shape
    ts = _rows(s, d)

    def body(h_ref, g_ref, sc_ref, sh_ref, u_ref):
        hf = h_ref[...]
        r = lax.rsqrt(jnp.mean(hf * hf, axis=-1, keepdims=True) + EPS)
        u_ref[...] = (((hf * r) * g_ref[...]) * (1.0 + sc_ref[...]) + sh_ref[...]).astype(BF16)

    row = pl.BlockSpec((ts, d), lambda i: (i, 0))
    vec = pl.BlockSpec((1, d), lambda i: (0, 0))
    return _pcall(body, name=name, out_shape=jax.ShapeDtypeStruct((s, d), BF16), grid=(s // ts,),
                  in_specs=[row, vec, vec, vec], out_specs=row)(h, g, scale, shift)


def _rmsmod_bwd(du, h, g, scale, dres, *, name):
    s, d = h.shape
    ts = _rows(s, d)

    def body(du_ref, h_ref, g_ref, sc_ref, dres_ref, dh_ref, dsh_ref, dsc_ref, dg_ref):
        @pl.when(pl.program_id(0) == 0)
        def _():
            dsh_ref[...] = jnp.zeros_like(dsh_ref)
            dsc_ref[...] = jnp.zeros_like(dsc_ref)
            dg_ref[...] = jnp.zeros_like(dg_ref)

        hf, duf, gain = h_ref[...], du_ref[...], g_ref[...]
        r = lax.rsqrt(jnp.mean(hf * hf, axis=-1, keepdims=True) + EPS)
        xh = hf * r
        dn = duf * (1.0 + sc_ref[...])
        dsh_ref[...] += jnp.sum(duf, axis=0, keepdims=True)
        dsc_ref[...] += jnp.sum(duf * (xh * gain), axis=0, keepdims=True)
        dg_ref[...] += jnp.sum(dn * xh, axis=0, keepdims=True)
        dxh = dn * gain
        dh_ref[...] = dres_ref[...] + r * (dxh - xh * jnp.mean(dxh * xh, axis=-1, keepdims=True))

    row = pl.BlockSpec((ts, d), lambda i: (i, 0))
    vec = pl.BlockSpec((1, d), lambda i: (0, 0))
    vshape = jax.ShapeDtypeStruct((1, d), F32)
    return _pcall(body, name=name, out_shape=(jax.ShapeDtypeStruct((s, d), F32), vshape, vshape, vshape),
                  grid=(s // ts,), in_specs=[row, row, vec, vec, row],
                  out_specs=(row, vec, vec, vec))(du, h, g, scale, dres)


def _resid_gate(h, gate, t, *, name):
    s, d = h.shape
    ts = _rows(s, d)

    def body(h_ref, g_ref, t_ref, o_ref):
        o_ref[...] = h_ref[...] + g_ref[...] * t_ref[...]

    row = pl.BlockSpec((ts, d), lambda i: (i, 0))
    vec = pl.BlockSpec((1, d), lambda i: (0, 0))
    return _pcall(body, name=name, out_shape=jax.ShapeDtypeStruct((s, d), F32), grid=(s // ts,),
                  in_specs=[row, vec, row], out_specs=row)(h, gate, t)


def _resid_gate_bwd(dh, t, gate, *, name):
    s, d = dh.shape
    ts = _rows(s, d)

    def body(dh_ref, t_ref, g_ref, dt_ref, dg_ref):
        @pl.when(pl.program_id(0) == 0)
        def _():
            dg_ref[...] = jnp.zeros_like(dg_ref)

        dhf = dh_ref[...]
        dt_ref[...] = (dhf * g_ref[...]).astype(BF16)
        dg_ref[...] += jnp.sum(dhf * t_ref[...], axis=0, keepdims=True)

    row = pl.BlockSpec((ts, d), lambda i: (i, 0))
    vec = pl.BlockSpec((1, d), lambda i: (0, 0))
    return _pcall(body, name=name,
                  out_shape=(jax.ShapeDtypeStruct((s, d), BF16), jax.ShapeDtypeStruct((1, d), F32)),
                  grid=(s // ts,), in_specs=[row, row, vec], out_specs=(row, vec))(dh, t, gate)


def _merge_fwd(proj, y_a, y_b, *, name):
    s, d = y_a.shape
    ts = _rows(s, d)
    ga_blk = OFF_GATES // d

    def body(ga_ref, gb_ref, ya_ref, yb_ref, o_ref):
        o_ref[...] = (jax.nn.sigmoid(ga_ref[...]) * ya_ref[...]
                      + jax.nn.sigmoid(gb_ref[...]) * yb_ref[...]).astype(BF16)

    row = pl.BlockSpec((ts, d), lambda i: (i, 0))
    ga = pl.BlockSpec((ts, d), lambda i: (i, ga_blk))
    gb = pl.BlockSpec((ts, d), lambda i: (i, ga_blk + 1))
    return _pcall(body, name=name, out_shape=jax.ShapeDtypeStruct((s, d), BF16), grid=(s // ts,),
                  in_specs=[ga, gb, row, row], out_specs=row)(proj, proj, y_a, y_b)


def _merge_bwd(dm, proj, y_a, y_b, *, name):
    s, d = y_a.shape
    ts = _rows(s, d)
    ga_blk = OFF_GATES // d

    def body(dm_ref, ga_ref, gb_ref, ya_ref, yb_ref, dya_ref, dyb_ref, dga_ref, dgb_ref):
        dmf = dm_ref[...]
        sa, sb = jax.nn.sigmoid(ga_ref[...]), jax.nn.sigmoid(gb_ref[...])
        dya_ref[...] = (dmf * sa).astype(BF16)
        dyb_ref[...] = (dmf * sb).astype(BF16)
        dga_ref[...] = (dmf * ya_ref[...] * (sa * (1.0 - sa))).astype(BF16)
        dgb_ref[...] = (dmf * yb_ref[...] * (sb * (1.0 - sb))).astype(BF16)

    row = pl.BlockSpec((ts, d), lambda i: (i, 0))
    ga = pl.BlockSpec((ts, d), lambda i: (i, ga_blk))
    gb = pl.BlockSpec((ts, d), lambda i: (i, ga_blk + 1))
    shp = jax.ShapeDtypeStruct((s, d), BF16)
    return _pcall(body, name=name, out_shape=(shp, shp, shp, shp), grid=(s // ts,),
                  in_specs=[row, ga, gb, row, row], out_specs=(row, row, row, row))(dm, proj, proj, y_a, y_b)


def _swiglu_fwd(gu, *, name):
    s, f2 = gu.shape
    f = f2 // 2
    ts = _rows(s, f)

    def body(g_ref, u_ref, a_ref):
        gf = g_ref[...]
        a_ref[...] = ((gf * jax.nn.sigmoid(gf)) * u_ref[...]).astype(BF16)

    return _pcall(body, name=name, out_shape=jax.ShapeDtypeStruct((s, f), BF16), grid=(s // ts,),
                  in_specs=[pl.BlockSpec((ts, f), lambda i: (i, 0)), pl.BlockSpec((ts, f), lambda i: (i, 1))],
                  out_specs=pl.BlockSpec((ts, f), lambda i: (i, 0)))(gu, gu)


def _swiglu_bwd(gu, da, *, name):
    s, f2 = gu.shape
    f = f2 // 2
    ts = _rows(s, f)

    def body(g_ref, u_ref, da_ref, o_ref):
        gf, daf = g_ref[...], da_ref[...]
        sg = jax.nn.sigmoid(gf)
        o_ref[:, :f] = (daf * u_ref[...] * (sg * (1.0 + gf * (1.0 - sg)))).astype(BF16)
        o_ref[:, f:] = (daf * (gf * sg)).astype(BF16)

    return _pcall(body, name=name, out_shape=jax.ShapeDtypeStruct((s, f2), BF16), grid=(s // ts,),
                  in_specs=[pl.BlockSpec((ts, f), lambda i: (i, 0)), pl.BlockSpec((ts, f), lambda i: (i, 1)),
                            pl.BlockSpec((ts, f), lambda i: (i, 0))],
                  out_specs=pl.BlockSpec((ts, f2), lambda i: (i, 0)))(gu, gu, da)


def _loss_fwd(y, tgt, *, name):
    s, d = y.shape
    ts = _rows(s, d)

    def body(y_ref, t_ref, l_ref, dy_ref):
        @pl.when(pl.program_id(0) == 0)
        def _():
            l_ref[...] = jnp.zeros_like(l_ref)

        e = y_ref[...] - t_ref[...]
        dy_ref[...] = e * (1.0 / d)
        per_tok = jnp.sum(e * e, axis=1, keepdims=True) * (1.0 / d)
        l_ref[...] += 0.5 * jnp.sum(per_tok, axis=0, keepdims=True)

    row = pl.BlockSpec((ts, d), lambda i: (i, 0))
    return _pcall(body, name=name,
                  out_shape=(jax.ShapeDtypeStruct((1, 128), F32), jax.ShapeDtypeStruct((s, d), F32)),
                  grid=(s // ts,), in_specs=[row, row],
                  out_specs=(pl.BlockSpec((1, 128), lambda i: (0, 0)), row))(y, tgt)


def _rope_tables(seq):
    inv = jnp.power(ROPE_THETA, -jnp.arange(0, HEAD_DIM, 2, dtype=F32) / HEAD_DIM)
    ang = jnp.arange(seq, dtype=F32)[:, None] * inv[None, :]
    cos, sin = jnp.cos(ang), jnp.sin(ang)
    return jnp.concatenate([cos, cos], axis=1), jnp.concatenate([-sin, sin], axis=1)


def _qkrope_fwd(proj, gains, cos2, sin2, *, name):
    s = proj.shape[0]
    ts = _rows(s, A_W)

    def body(x_ref, g_ref, c_ref, s_ref, o_ref):
        gain, cos, sin = g_ref[...], c_ref[...], s_ref[...]
        for h in range(A_HEADS):
            lanes = slice(h * HEAD_DIM, (h + 1) * HEAD_DIM)
            x = x_ref[:, lanes]
            y = (x * lax.rsqrt(jnp.mean(x * x, axis=-1, keepdims=True) + EPS)) * gain
            o_ref[:, lanes] = (y * cos + pltpu.roll(y, HEAD_DIM // 2, 1) * sin).astype(BF16)

    heads = pl.BlockSpec((ts, A_W), lambda i, j: (i, j))
    tab = pl.BlockSpec((ts, HEAD_DIM), lambda i, j: (i, 0))
    gain = pl.BlockSpec((None, 1, HEAD_DIM), lambda i, j: (j, 0, 0))
    return _pcall(body, name=name, out_shape=jax.ShapeDtypeStruct((s, 2 * A_W), BF16),
                  grid=(s // ts, 2), in_specs=[heads, gain, tab, tab], out_specs=heads)(
                      proj, gains, cos2, sin2)


def _qkrope_bwd(dqk, proj, gains, cos2, sin2, *, name):
    s = proj.shape[0]
    ts = _rows(s, A_W)

    def body(d_ref, x_ref, g_ref, c_ref, s_ref, dx_ref, dg_ref):
        @pl.when(pl.program_id(1) == 0)
        def _():
            dg_ref[...] = jnp.zeros_like(dg_ref)

        gain, cos, sin = g_ref[...], c_ref[...], s_ref[...]
        dg = jnp.zeros((1, HEAD_DIM), F32)
        for h in range(A_HEADS):
            lanes = slice(h * HEAD_DIM, (h + 1) * HEAD_DIM)
            dout = d_ref[:, lanes]
            dy = dout * cos + pltpu.roll(dout * sin, HEAD_DIM // 2, 1)
            x = x_ref[:, lanes]
            r = lax.rsqrt(jnp.mean(x * x, axis=-1, keepdims=True) + EPS)
            xh = x * r
            dg = dg + jnp.sum(dy * xh, axis=0, keepdims=True)
            dxh = dy * gain
            dx_ref[:, lanes] = (r * (dxh - xh * jnp.mean(dxh * xh, axis=-1, keepdims=True))).astype(BF16)
        dg_ref[...] += dg

    heads = pl.BlockSpec((ts, A_W), lambda j, i: (i, j))
    tab = pl.BlockSpec((ts, HEAD_DIM), lambda j, i: (i, 0))
    gain = pl.BlockSpec((None, 1, HEAD_DIM), lambda j, i: (j, 0, 0))
    return _pcall(body, name=name,
                  out_shape=(jax.ShapeDtypeStruct((s, 2 * A_W), BF16), jax.ShapeDtypeStruct((2, 1, HEAD_DIM), F32)),
                  grid=(2, s // ts), in_specs=[heads, heads, gain, tab, tab],
                  out_specs=(heads, gain))(dqk, proj, gains, cos2, sin2)


def _block_rows(blk):
    if isinstance(blk, int):
        return pl.ds(blk * BLOCK, BLOCK)
    return pl.ds(pl.multiple_of(blk * BLOCK, BLOCK), BLOCK)


def _band_masks(n, with_prev):
    row = lax.broadcasted_iota(jnp.int32, (BLOCK, BLOCK), 0)
    col = lax.broadcasted_iota(jnp.int32, (BLOCK, BLOCK), 1)
    cur = col <= row
    if not with_prev:
        return [(n, cur)]
    prev = col >= row + jnp.where(n >= 1, 0, BLOCK)
    return [(n, cur), (jnp.maximum(n - 1, 0), prev)]


def _dil_fwd(q_arr, k_arr, v_arr, offs, length, dil, *, name):
    nj, nb = dil * HEADS_PER_GROUP, length // BLOCK
    ju = HEADS_PER_GROUP
    qo, ko, vo = (off // ju for off in offs)
    assert all(off % ju == 0 for off in offs)

    def body(q_ref, k_ref, v_ref, o_ref, l_ref):
        n = pl.program_id(1)
        masks = _band_masks(n, nb > 1)
        for cb in range(ju):
            lanes = slice(cb * HEAD_DIM, (cb + 1) * HEAD_DIM)
            q = q_ref[:, lanes].astype(BF16)
            parts = []
            for blk, mask in masks:
                rows = _block_rows(blk)
                sc = _dot(q, k_ref[rows, lanes].astype(BF16), NT) * ATT_SCALE
                parts.append((jnp.where(mask, sc, MASKED), rows))
            m = parts[0][0].max(axis=-1, keepdims=True)
            for sc, _ in parts[1:]:
                m = jnp.maximum(m, sc.max(axis=-1, keepdims=True))
            den = jnp.zeros((BLOCK, 1), F32)
            acc = jnp.zeros((BLOCK, HEAD_DIM), F32)
            for sc, rows in parts:
                p = jnp.exp(sc - m)
                den = den + jnp.sum(p, axis=-1, keepdims=True)
                acc = acc + _dot(p.astype(BF16), v_ref[rows, lanes].astype(BF16))
            o_ref[:, lanes] = acc / den
            l_ref[:, lanes] = jnp.broadcast_to(m + jnp.log(den), (BLOCK, HEAD_DIM))

    qspec = pl.BlockSpec((BLOCK, ju * HEAD_DIM), lambda j, n: (n, qo + j))
    kspec = pl.BlockSpec((length, ju * HEAD_DIM), lambda j, n: (0, ko + j))
    vspec = pl.BlockSpec((length, ju * HEAD_DIM), lambda j, n: (0, vo + j))
    ospec = pl.BlockSpec((BLOCK, ju * HEAD_DIM), lambda j, n: (n, j))
    shp = jax.ShapeDtypeStruct((length, nj * HEAD_DIM), F32)
    return _pcall(body, name=name, out_shape=(shp, shp), grid=(nj // ju, nb), in_specs=[qspec, kspec, vspec],
                  out_specs=(ospec, ospec))(q_arr, k_arr, v_arr)


def _dil_bwd(q_arr, k_arr, v_arr, offs, o, lse, do, dlse, length, dil, *, name):
    nj, nb = dil * HEADS_PER_GROUP, length // BLOCK
    ju = HEADS_PER_GROUP if length <= 4 * BLOCK else 2
    qo, ko, vo = (off // ju for off in offs)
    assert all(off % ju == 0 for off in offs)

    def body(q_ref, k_ref, v_ref, o_ref, l_ref, do_ref, dl_ref, dq_ref, dk_ref, dv_ref):
        dk_ref[...] = jnp.zeros_like(dk_ref)
        dv_ref[...] = jnp.zeros_like(dv_ref)

        def step(n, carry):
            qrows = _block_rows(n)
            masks = _band_masks(n, nb > 1)
            for cb in range(ju):
                lanes = slice(cb * HEAD_DIM, (cb + 1) * HEAD_DIM)
                q = q_ref[qrows, lanes].astype(BF16)
                dof = do_ref[qrows, lanes]
                dob = dof.astype(BF16)
                lse_b = l_ref[qrows, lanes]
                shift = dl_ref[qrows, lanes] - jnp.sum(dof * o_ref[qrows, lanes], axis=-1, keepdims=True)
                dq = jnp.zeros((BLOCK, HEAD_DIM), F32)
                for blk, mask in masks:
                    rows = _block_rows(blk)
                    kk, vv = k_ref[rows, lanes].astype(BF16), v_ref[rows, lanes].astype(BF16)
                    sc = _dot(q, kk, NT) * ATT_SCALE
                    p = jnp.where(mask, jnp.exp(sc - lse_b), 0.0)
                    ds = (p * (_dot(dob, vv, NT) + shift)).astype(BF16)
                    dq = dq + _dot(ds, kk)
                    dk_ref[rows, lanes] += _dot(ds, q, TN) * ATT_SCALE
                    dv_ref[rows, lanes] += _dot(p.astype(BF16), dob, TN)
                dq_ref[qrows, lanes] = dq * ATT_SCALE
            return carry

        if nb == 1:
            step(0, 0)
        else:
            lax.fori_loop(0, nb, step, 0)

    def col(off):
        return pl.BlockSpec((length, ju * HEAD_DIM), lambda j: (0, off + j))

    shp = jax.ShapeDtypeStruct((length, nj * HEAD_DIM), F32)
    return _pcall(body, name=name, out_shape=(shp, shp, shp), grid=(nj // ju,),
                  in_specs=[col(qo), col(ko), col(vo), col(0), col(0), col(0), col(0)],
                  out_specs=(col(0), col(0), col(0)))(q_arr, k_arr, v_arr, o, lse, do, dlse)


def _combine_weights(l_refs):
    ls = [r[...] for r in l_refs]
    m = jnp.maximum(jnp.maximum(ls[0], ls[1]), ls[2])
    es = [jnp.exp(l - m) for l in ls]
    den = es[0] + es[1] + es[2]
    return [e / den for e in es]


def _combine_fwd(os_, lses, *, name):
    s = os_[0].shape[0]
    ts = _rows(s, GROUP_W)

    def body(o0, o1, o2, l0, l1, l2, out_ref):
        w = _combine_weights((l0, l1, l2))
        out_ref[...] = (w[0] * o0[...] + w[1] * o1[...] + w[2] * o2[...]).astype(BF16)

    row = pl.BlockSpec((ts, GROUP_W), lambda i: (i, 0))
    return _pcall(body, name=name, out_shape=jax.ShapeDtypeStruct((s, GROUP_W), BF16), grid=(s // ts,),
                  in_specs=[row] * 6, out_specs=row)(*os_, *lses)


def _combine_bwd(do_a, os_, lses, *, name):
    s = do_a.shape[0]
    ts = _rows(s, GROUP_W)

    def body(d_ref, o0, o1, o2, l0, l1, l2, do0, do1, do2, dl0, dl1, dl2):
        w = _combine_weights((l0, l1, l2))
        d = d_ref[...]
        og = [o0[...], o1[...], o2[...]]
        oa = w[0] * og[0] + w[1] * og[1] + w[2] * og[2]
        ta = jnp.sum(d * oa, axis=-1, keepdims=True)
        for g, (do_ref, dl_ref) in enumerate(((do0, dl0), (do1, dl1), (do2, dl2))):
            do_ref[...] = w[g] * d
            dl_ref[...] = w[g] * (jnp.sum(d * og[g], axis=-1, keepdims=True) - ta)

    head = pl.BlockSpec((ts, HEAD_DIM), lambda i, h: (i, h))
    shp = jax.ShapeDtypeStruct((s, GROUP_W), F32)
    return _pcall(body, name=name, out_shape=(shp,) * 6, grid=(s // ts, HEADS_PER_GROUP),
                  in_specs=[head] * 7, out_specs=(head,) * 6)(do_a, *os_, *lses)


def _dot_exact(x, ones_mask):
    hi = x.astype(BF16)
    r1 = x - hi.astype(F32)
    mid = r1.astype(BF16)
    lo = (r1 - mid.astype(F32)).astype(BF16)
    return _dot(hi, ones_mask) + _dot(mid, ones_mask) + _dot(lo, ones_mask)


SB_QROWS = 2 * BLOCK
SB_UNROLL = 4


def _sb_mask(j, i):
    row = lax.broadcasted_iota(jnp.int32, (SB_QROWS, BLOCK), 0)
    col = lax.broadcasted_iota(jnp.int32, (SB_QROWS, BLOCK), 1)
    return col + (j * BLOCK - i * SB_QROWS) < row


def _sb_steps(i):
    return ((i + 1) * (SB_QROWS // BLOCK) + SB_UNROLL - 1) // SB_UNROLL


def _sb_scores(q, kk, j, i):
    mask = _sb_mask(j, i)
    z = _dot(q, kk, NT) * ATT_SCALE
    sp = jnp.log1p(jnp.exp(-jnp.abs(z)))
    log_beta = jnp.minimum(z, 0.0) - sp
    log_1mb = jnp.where(mask, jnp.minimum(-z, 0.0) - sp, 0.0)
    return z, log_beta, log_1mb, mask


def _sb_weights(log_beta, log_1mb, mask, run, upper):
    after = run + _dot_exact(log_1mb, upper)
    return jnp.where(mask, jnp.exp(log_beta + after), 0.0)


def _tri(strict_lower):
    row = lax.broadcasted_iota(jnp.int32, (BLOCK, BLOCK), 0)
    col = lax.broadcasted_iota(jnp.int32, (BLOCK, BLOCK), 1)
    return ((row > col) if strict_lower else (row < col)).astype(BF16)


def _sb_fwd(proj, *, name):
    s = proj.shape[0]
    assert s % (BLOCK * SB_UNROLL) == 0 and s % SB_QROWS == 0
    qb, kb, vb = OFF_QB // HEAD_DIM, OFF_KB // HEAD_DIM, OFF_VB // HEAD_DIM

    def body(q_ref, k_ref, v_ref, o_ref):
        i = pl.program_id(1)
        q = q_ref[...].astype(BF16)
        upper = _tri(True)
        nsteps = _sb_steps(i)

        def step(t, carry):
            acc, run = carry
            for b in reversed(range(SB_UNROLL)):
                j = (nsteps - 1 - t) * SB_UNROLL + b
                rows = _block_rows(j)
                _, log_beta, log_1mb, mask = _sb_scores(q, k_ref[rows, :].astype(BF16), j, i)
                a = _sb_weights(log_beta, log_1mb, mask, run, upper)
                acc = acc + _dot(a.astype(BF16), v_ref[rows, :].astype(BF16))
                run = run + jnp.sum(log_1mb, axis=-1, keepdims=True)
            return acc, run

        acc, _ = lax.fori_loop(0, nsteps, step,
                               (jnp.zeros((SB_QROWS, HEAD_DIM), F32), jnp.zeros((SB_QROWS, 1), F32)))
        o_ref[...] = acc.astype(BF16)

    return _pcall(body, name=name, out_shape=jax.ShapeDtypeStruct((s, B_W), BF16), grid=(SB_HEADS, s // SB_QROWS),
                  in_specs=[pl.BlockSpec((SB_QROWS, HEAD_DIM), lambda h, i: (i, qb + h)),
                            pl.BlockSpec((s, HEAD_DIM), lambda h, i: (0, kb + h)),
                            pl.BlockSpec((s, HEAD_DIM), lambda h, i: (0, vb + h))],
                  out_specs=pl.BlockSpec((SB_QROWS, HEAD_DIM), lambda h, i: (i, h)))(proj, proj, proj)


def _sb_bwd(proj, do_b, *, name):
    s = proj.shape[0]
    assert s % (BLOCK * SB_UNROLL) == 0 and s % SB_QROWS == 0
    nkb = s // BLOCK
    qb, kb, vb = OFF_QB // HEAD_DIM, OFF_KB // HEAD_DIM, OFF_VB // HEAD_DIM

    def body(q_ref, k_ref, v_ref, do_ref, dq_ref, dk_ref, dv_ref, z_s, a_s):
        i = pl.program_id(1)

        @pl.when(i == 0)
        def _():
            dk_ref[...] = jnp.zeros_like(dk_ref)
            dv_ref[...] = jnp.zeros_like(dv_ref)

        q = q_ref[...].astype(BF16)
        dob = do_ref[...].astype(BF16)
        upper, lower = _tri(True), _tri(False)
        nsteps = _sb_steps(i)

        def recompute(t, run):
            for b in reversed(range(SB_UNROLL)):
                j = (nsteps - 1 - t) * SB_UNROLL + b
                z, log_beta, log_1mb, mask = _sb_scores(q, k_ref[_block_rows(j), :].astype(BF16), j, i)
                z_s[j] = z
                a_s[j] = _sb_weights(log_beta, log_1mb, mask, run, upper)
                run = run + jnp.sum(log_1mb, axis=-1, keepdims=True)
            return run

        lax.fori_loop(0, nsteps, recompute, jnp.zeros((SB_QROWS, 1), F32))

        def grads(t, carry):
            dq, run = carry
            for b in range(SB_UNROLL):
                j = t * SB_UNROLL + b
                rows = _block_rows(j)
                kk, vv = k_ref[rows, :].astype(BF16), v_ref[rows, :].astype(BF16)
                z, a = z_s[j], a_s[j]
                de = _dot(dob, vv, NT) * a
                before = run + _dot_exact(de, lower)
                dz = (de * jax.nn.sigmoid(-z)
                      - jnp.where(_sb_mask(j, i), jax.nn.sigmoid(z), 0.0) * before).astype(BF16)
                dk_ref[rows, :] += _dot(dz, q, TN) * ATT_SCALE
                dv_ref[rows, :] += _dot(a.astype(BF16), dob, TN)
                dq = dq + _dot(dz, kk)
                run = run + jnp.sum(de, axis=-1, keepdims=True)
            return dq, run

        dq, _ = lax.fori_loop(0, nsteps, grads,
                              (jnp.zeros((SB_QROWS, HEAD_DIM), F32), jnp.zeros((SB_QROWS, 1), F32)))
        dq_ref[...] = dq * ATT_SCALE

    blk = pl.BlockSpec((SB_QROWS, HEAD_DIM), lambda h, i: (i, h))
    full = pl.BlockSpec((s, HEAD_DIM), lambda h, i: (0, h))
    shp = jax.ShapeDtypeStruct((s, B_W), F32)
    return _pcall(body, name=name, out_shape=(shp, shp, shp), grid=(SB_HEADS, s // SB_QROWS),
                  in_specs=[pl.BlockSpec((SB_QROWS, HEAD_DIM), lambda h, i: (i, qb + h)),
                            pl.BlockSpec((s, HEAD_DIM), lambda h, i: (0, kb + h)),
                            pl.BlockSpec((s, HEAD_DIM), lambda h, i: (0, vb + h)), blk],
                  out_specs=(blk, full, full),
                  scratch=[pltpu.VMEM((nkb, SB_QROWS, BLOCK), F32), pltpu.VMEM((nkb, SB_QROWS, BLOCK), F32)])(
                      proj, proj, proj, do_b)


def _coords():
    return lax.axis_index("x"), lax.axis_index("y"), lax.axis_index("c")


def _flip(v, bit):
    return 1 - v if bit else v


def _shard_of(ref, axis, idx, size):
    if axis == 0:
        sl = pl.ds(pl.multiple_of(idx * size, 16), size)
        return ref.at[sl, :] if len(ref.shape) == 2 else ref.at[:, sl, :]
    sl = pl.ds(pl.multiple_of(idx * size, 128), size)
    return ref.at[:, sl] if len(ref.shape) == 2 else ref.at[:, :, sl]


def _small_allgather(v, *, name, silu=False):
    n = v.shape[1]

    def body(v_ref, out_ref, send_sems, recv_sems):
        x, y, c = _coords()
        me = 4 * x + 2 * y + c
        val = v_ref[...]
        out_ref[me] = val * jax.nn.sigmoid(val) if silu else val
        copies = []
        for k in range(1, N_DEV):
            peer = (_flip(x, k & 4), _flip(y, k & 2), _flip(c, k & 1))
            copies.append(pltpu.make_async_remote_copy(
                src_ref=out_ref.at[me], dst_ref=out_ref.at[me], send_sem=send_sems.at[k - 1],
                recv_sem=recv_sems.at[k - 1], device_id=peer, device_id_type=MESH))
        for cp in copies:
            cp.start()
        for cp in copies:
            cp.wait_recv()
        for cp in copies:
            cp.wait_send()

    return _pcall(body, name=name, out_shape=jax.ShapeDtypeStruct((N_DEV, 1, n), F32),
                  in_specs=[pl.BlockSpec(memory_space=pltpu.VMEM)], out_specs=pl.BlockSpec(memory_space=pltpu.VMEM),
                  scratch=[pltpu.SemaphoreType.DMA((N_DEV - 1,)), pltpu.SemaphoreType.DMA((N_DEV - 1,))])(v)


def _gather_weights(shards, axes, *, name):
    nt = len(shards)
    sizes = [sh.shape[ax] for sh, ax in zip(shards, axes)]
    out_shape = tuple(
        jax.ShapeDtypeStruct(tuple(d * N_DEV if a == ax else d for a, d in enumerate(sh.shape)), sh.dtype)
        for sh, ax in zip(shards, axes))

    def body(*refs):
        ins, outs = refs[:nt], refs[nt:2 * nt]
        send_sems, recv_sems, local_sems = refs[2 * nt:]
        x, y, c = _coords()
        me, sibling = (x, y, c), (x, y, 1 - c)
        chips = [(1 - x, y), (x, 1 - y), (1 - x, 1 - y)]

        def slot(t, dev):
            return _shard_of(outs[t], axes[t], 4 * dev[0] + 2 * dev[1] + dev[2], sizes[t])

        def copy(t, k, block, to, src=None):
            return pltpu.make_async_remote_copy(
                src_ref=slot(t, block) if src is None else src, dst_ref=slot(t, block),
                send_sem=send_sems.at[t, k], recv_sem=recv_sems.at[t, k], device_id=to, device_id_type=MESH)

        mine = [pltpu.make_async_copy(ins[t], slot(t, me), local_sems.at[t]) for t in range(nt)]
        for cp in mine:
            cp.start()
        first = []
        for t in range(nt):
            first.append(copy(t, 0, me, sibling, src=ins[t]))
            first += [copy(t, 1 + j, me, (*chip, c), src=ins[t]) for j, chip in enumerate(chips)]
        for cp in first:
            cp.start()
        passed = []
        for j, chip in enumerate(chips):
            for t in range(nt):
                copy(t, 1 + j, (*chip, c), me).wait_recv()
                fwd = copy(t, 4 + j, (*chip, c), sibling)
                fwd.start()
                passed.append(fwd)
        for t in range(nt):
            copy(t, 0, sibling, me).wait_recv()
            for j, chip in enumerate(chips):
                copy(t, 4 + j, (*chip, 1 - c), me).wait_recv()
        for cp in first + passed:
            cp.wait_send()
        for cp in mine:
            cp.wait()

    hbm = pl.BlockSpec(memory_space=pl.ANY)
    return _pcall(body, name=name, out_shape=out_shape, in_specs=[hbm] * nt, out_specs=(hbm,) * nt,
                  scratch=[pltpu.SemaphoreType.DMA((nt, 7)), pltpu.SemaphoreType.DMA((nt, 7)),
                           pltpu.SemaphoreType.DMA((nt,))])(*shards)


def _rs_pair_exchange(grads, axes, *, name):
    nt = len(grads)
    sizes = [g.shape[1 + ax] // N_DEV for g, ax in zip(grads, axes)]

    def recv_shape(g, ax):
        dims = list(g.shape)
        dims[1 + ax] //= N_DEV
        return jax.ShapeDtypeStruct((dims[0], N_CHIPS, dims[1], dims[2]), g.dtype)

    def body(*refs):
        ins, outs = refs[:nt], refs[nt:2 * nt]
        send_sems, recv_sems = refs[2 * nt:]
        x, y, c = _coords()
        copies = []
        for t in range(nt):
            for q in range(N_CHIPS):
                copies.append(pltpu.make_async_remote_copy(
                    src_ref=_shard_of(ins[t], axes[t], 2 * q + 1 - c, sizes[t]), dst_ref=outs[t].at[:, q],
                    send_sem=send_sems.at[t, q], recv_sem=recv_sems.at[t, q], device_id=(x, y, 1 - c),
                    device_id_type=MESH))
        for cp in copies:
            cp.start()
        for cp in copies:
            cp.wait_recv()
        for cp in copies:
            cp.wait_send()

    hbm = pl.BlockSpec(memory_space=pl.ANY)
    return _pcall(body, name=name, out_shape=tuple(recv_shape(g, ax) for g, ax in zip(grads, axes)),
                  in_specs=[hbm] * nt, out_specs=(hbm,) * nt,
                  scratch=[pltpu.SemaphoreType.DMA((nt, N_CHIPS)), pltpu.SemaphoreType.DMA((nt, N_CHIPS))])(*grads)


def _pair_sum(grad, sib, core, axis, *, name):
    nl, _, r, c = sib.shape
    tr = _rows(r, c)
    nrt = r // tr

    def body(core_ref, g_ref, s_ref, o_ref):
        o_ref[...] = (g_ref[...].astype(F32) + s_ref[...].astype(F32)).astype(BF16)

    if axis == 0:
        gspec = pl.BlockSpec((None, tr, c), lambda l, q, i, core_ref: (l, (2 * q + core_ref[0]) * nrt + i, 0))
    else:
        gspec = pl.BlockSpec((None, tr, c), lambda l, q, i, core_ref: (l, i, 2 * q + core_ref[0]))
    sspec = pl.BlockSpec((None, None, tr, c), lambda l, q, i, core_ref: (l, q, i, 0))
    grid_spec = pltpu.PrefetchScalarGridSpec(num_scalar_prefetch=1, grid=(nl, N_CHIPS, nrt),
                                             in_specs=[gspec, sspec], out_specs=sspec)
    return _pcall(body, name=name, out_shape=jax.ShapeDtypeStruct(sib.shape, BF16), grid_spec=grid_spec)(
        core, grad, sib)


def _rs_chip_exchange(sums, *, name):
    nt = len(sums)

    def body(*refs):
        ins, outs = refs[:nt], refs[nt:2 * nt]
        send_sems, recv_sems = refs[2 * nt:]
        x, y, c = _coords()
        copies = []
        for t in range(nt):
            for k in range(1, N_CHIPS):
                px, py = _flip(x, k & 2), _flip(y, k & 1)
                copies.append(pltpu.make_async_remote_copy(
                    src_ref=ins[t].at[:, 2 * px + py], dst_ref=outs[t].at[:, k - 1],
                    send_sem=send_sems.at[t, k - 1], recv_sem=recv_sems.at[t, k - 1], device_id=(px, py, c),
                    device_id_type=MESH))
        for cp in copies:
            cp.start()
        for cp in copies:
            cp.wait_recv()
        for cp in copies:
            cp.wait_send()

    hbm = pl.BlockSpec(memory_space=pl.ANY)
    out_shape = tuple(jax.ShapeDtypeStruct((s.shape[0], N_CHIPS - 1) + s.shape[2:], s.dtype) for s in sums)
    return _pcall(body, name=name, out_shape=out_shape, in_specs=[hbm] * nt, out_specs=(hbm,) * nt,
                  scratch=[pltpu.SemaphoreType.DMA((nt, N_CHIPS - 1)),
                           pltpu.SemaphoreType.DMA((nt, N_CHIPS - 1))])(*sums)


def _adam_math(g, w, m, v):
    m2 = ADAM_B1 * m + (1.0 - ADAM_B1) * g
    v2 = ADAM_B2 * v + (1.0 - ADAM_B2) * (g * g)
    m_hat = m2 / (1.0 - ADAM_B1 ** ADAM_STEP)
    v_hat = v2 / (1.0 - ADAM_B2 ** ADAM_STEP)
    delta = -ADAM_LR * (m_hat / (jnp.sqrt(v_hat) + ADAM_EPS) + ADAM_WD * w)
    return delta, m2, v2


def _adamw_sharded(chip_sums, remote, chip, w, m, v, *, name):
    nl, r, c = w.shape
    tr = _rows(r, c)

    def body(chip_ref, p_ref, r0_ref, r1_ref, r2_ref, w_ref, m_ref, v_ref, g_out, d_out, m_out, v_out):
        g = ((p_ref[...].astype(F32) + r0_ref[...].astype(F32)) + r1_ref[...].astype(F32)) + r2_ref[...].astype(F32)
        g_out[...] = g
        d_out[...], m_out[...], v_out[...] = _adam_math(g, w_ref[...], m_ref[...], v_ref[...])

    pspec = pl.BlockSpec((None, None, tr, c), lambda l, i, chip_ref: (l, chip_ref[0], i, 0))

    def rspec(k):
        return pl.BlockSpec((None, None, tr, c), lambda l, i, chip_ref: (l, k, i, 0))

    wspec = pl.BlockSpec((None, tr, c), lambda l, i, chip_ref: (l, i, 0))
    grid_spec = pltpu.PrefetchScalarGridSpec(
        num_scalar_prefetch=1, grid=(nl, r // tr),
        in_specs=[pspec, rspec(0), rspec(1), rspec(2), wspec, wspec, wspec], out_specs=(wspec,) * 4)
    shp = jax.ShapeDtypeStruct(w.shape, F32)
    return _pcall(body, name=name, out_shape=(shp,) * 4, grid_spec=grid_spec)(
        chip, chip_sums, remote, remote, remote, w, m, v)


def _adamw_local(g, w, m, v, *, name):
    nl, r, c = w.shape
    tr = _rows(r, c)

    def body(g_ref, w_ref, m_ref, v_ref, d_out, m_out, v_out):
        d_out[...], m_out[...], v_out[...] = _adam_math(g_ref[...], w_ref[...], m_ref[...], v_ref[...])

    spec = pl.BlockSpec((None, tr, c), lambda l, i: (l, i, 0))
    shp = jax.ShapeDtypeStruct(w.shape, F32)
    return _pcall(body, name=name, out_shape=(shp,) * 3, grid=(nl, r // tr), in_specs=[spec] * 4,
                  out_specs=(spec,) * 3)(g, w, m, v)


def _adamw_replicated(parts, w, m, v, *, name):
    n = w.shape[1]

    def body(p_ref, w_ref, m_ref, v_ref, g_out, d_out, m_out, v_out):
        g = p_ref[0]
        for k in range(1, N_DEV):
            g = g + p_ref[k]
        g_out[...] = g
        d_out[...], m_out[...], v_out[...] = _adam_math(g, w_ref[...], m_ref[...], v_ref[...])

    vm = pl.BlockSpec(memory_space=pltpu.VMEM)
    shp = jax.ShapeDtypeStruct((1, n), F32)
    return _pcall(body, name=name, out_shape=(shp,) * 4, in_specs=[vm] * 4, out_specs=(vm,) * 4)(parts, w, m, v)


def _group_views(qk, proj, g, dil, seq):
    if dil == 1:
        return (qk, qk, proj), (0, A_HEADS, 2 * A_HEADS)
    length = seq // dil
    lo = g * GROUP_W
    q = qk[:, lo:lo + GROUP_W].reshape(length, dil * GROUP_W)
    k = qk[:, A_W + lo:A_W + lo + GROUP_W].reshape(length, dil * GROUP_W)
    v = proj[:, OFF_VA + lo:OFF_VA + lo + GROUP_W].astype(BF16).reshape(length, dil * GROUP_W)
    return (q, k, v), (0, 0, 0)


def _mod_rows(mod, d):
    return [mod[:, i * d:(i + 1) * d] for i in range(6)]


def _layer_fwd(h, mod, g1, g2, gains, wts, cos2, sin2):
    seq, d = h.shape
    sh1, sc1, ga1, sh2, sc2, ga2 = _mod_rows(mod, d)
    sv = {"h_in": h}
    u = _rmsmod_fwd(h, g1, sc1, sh1, name="rmsmod_fwd")
    proj = _mm(u, wts["w_in"], name="mm_in")
    qk = _qkrope_fwd(proj, gains, cos2, sin2, name="qkrope_fwd")
    os_, lses = [], []
    for g, dil in enumerate(DILATIONS):
        arrs, offs = _group_views(qk, proj, g, dil, seq)
        o, lse = _dil_fwd(*arrs, offs, seq // dil, dil, name=f"dil_fwd_{dil}")
        os_.append(o.reshape(seq, GROUP_W))
        lses.append(lse.reshape(seq, GROUP_W))
    o_a = _combine_fwd(os_, lses, name="combine_fwd")
    o_b = _sb_fwd(proj, name="sb_fwd")
    y_a = _mm(o_a, wts["w_branch_a"], name="mm_branch")
    y_b = _mm(o_b, wts["w_branch_b"], name="mm_branch")
    merged = _merge_fwd(proj, y_a, y_b, name="merge_fwd")
    t = _mm(merged, wts["w_out"], name="mm_out")
    h_mid = _resid_gate(h, ga1, t, name="resid_gate")
    u2 = _rmsmod_fwd(h_mid, g2, sc2, sh2, name="rmsmod_fwd")
    gu = _mm(u2, wts["w_gate_up"], name="mm_gate_up")
    a = _swiglu_fwd(gu, name="swiglu_fwd")
    f = _mm(a, wts["w_down"], name="mm_down")
    h_out = _resid_gate(h_mid, ga2, f, name="resid_gate")
    sv.update(u=u, proj=proj, qk=qk, os=os_, lses=lses, o_a=o_a, o_b=o_b, y_a=y_a, y_b=y_b, merged=merged, t=t,
              h_mid=h_mid, u2=u2, gu=gu, a=a, f=f)
    return h_out, sv


def _layer_bwd(dh, sv, mod, g1, g2, gains, wts, cos2, sin2, layer, n_layers, gbufs):
    seq, d = dh.shape
    sh1, sc1, ga1, sh2, sc2, ga2 = _mod_rows(mod, d)

    def wgrad(key, act, dout):
        gbufs[key] = _mm(act, dout, ta=True, out_dtype=BF16, name="mm_wgrad_" + key,
                         stack=(layer, n_layers, gbufs.get(key)))

    df, dgate2 = _resid_gate_bwd(dh, sv["f"], ga2, name="resid_gate_bwd")
    da = _mm(df, wts["w_down"], tb=True, name="mm_down_t")
    wgrad("w_down", sv["a"], df)
    dgu = _swiglu_bwd(sv["gu"], da, name="swiglu_bwd")
    du2 = _mm(dgu, wts["w_gate_up"], tb=True, name="mm_gate_up_t")
    wgrad("w_gate_up", sv["u2"], dgu)
    dh_mid, dsh2, dsc2, dg2 = _rmsmod_bwd(du2, sv["h_mid"], g2, sc2, dh, name="rmsmod_bwd")

    dt, dgate1 = _resid_gate_bwd(dh_mid, sv["t"], ga1, name="resid_gate_bwd")
    dmerged = _mm(dt, wts["w_out"], tb=True, name="mm_out_t")
    wgrad("w_out", sv["merged"], dt)
    dy_a, dy_b, dga, dgb = _merge_bwd(dmerged, sv["proj"], sv["y_a"], sv["y_b"], name="merge_bwd")
    do_a = _mm(dy_a, wts["w_branch_a"], tb=True, name="mm_branch_t")
    do_b = _mm(dy_b, wts["w_branch_b"], tb=True, name="mm_branch_t")
    wgrad("w_branch_a", sv["o_a"], dy_a)
    wgrad("w_branch_b", sv["o_b"], dy_b)
    dqb, dkb, dvb = _sb_bwd(sv["proj"], do_b, name="sb_bwd")
    comb = _combine_bwd(do_a, sv["os"], sv["lses"], name="combine_bwd")
    dos, dls = comb[:3], comb[3:]
    dqs, dks, dvs = [], [], []
    for g, dil in enumerate(DILATIONS):
        length = seq // dil
        arrs, offs = _group_views(sv["qk"], sv["proj"], g, dil, seq)
        view = (length, dil * GROUP_W)
        dq, dk, dv = _dil_bwd(*arrs, offs, sv["os"][g].reshape(view), sv["lses"][g].reshape(view),
                              dos[g].reshape(view), dls[g].reshape(view), length, dil, name=f"dil_bwd_{dil}")
        dqs.append(dq.reshape(seq, GROUP_W))
        dks.append(dk.reshape(seq, GROUP_W))
        dvs.append(dv.reshape(seq, GROUP_W))
    dqk, dgains = _qkrope_bwd(jnp.concatenate(dqs + dks, axis=1), sv["proj"], gains, cos2, sin2,
                              name="qkrope_bwd")
    dproj = jnp.concatenate(
        [dqk] + [t_.astype(BF16) for t_ in dvs + [dqb, dkb, dvb]] + [dga, dgb], axis=1)
    du = _mm(dproj, wts["w_in"], tb=True, name="mm_in_t")
    wgrad("w_in", sv["u"], dproj)
    dh_in, dsh1, dsc1, dg1 = _rmsmod_bwd(du, sv["h_in"], g1, sc1, dh_mid, name="rmsmod_bwd")
    dmod = jnp.concatenate([dsh1, dsc1, dgate1, dsh2, dsc2, dgate2], axis=1)
    return dh_in, dmod, dg1, dg2, dgains


GATHERED = ("w_in", "w_branch_a", "w_branch_b", "w_out", "w_gate_up", "w_down")
SHARD_AXIS = {"w_in": 1, "w_branch_a": 1, "w_branch_b": 1, "w_out": 0, "w_gate_up": 1, "w_down": 0}


def kernel(x, c, w_ada, b_ada, norm1_g, norm2_g, w_in, qn_g, kn_g, w_branch_a, w_branch_b, w_out, w_gate_up, w_down, loss_target, m_w_ada, m_b_ada, m_norm1_g, m_norm2_g, m_w_in, m_qn_g, m_kn_g, m_w_branch_a, m_w_branch_b, m_w_out, m_w_gate_up, m_w_down, v_w_ada, v_b_ada, v_norm1_g, v_norm2_g, v_w_in, v_qn_g, v_kn_g, v_w_branch_a, v_w_branch_b, v_w_out, v_w_gate_up, v_w_down):
    seq, d = x.shape[1], x.shape[2]
    depth = w_in.shape[0]
    weights = dict(w_in=w_in, w_branch_a=w_branch_a, w_branch_b=w_branch_b, w_out=w_out, w_gate_up=w_gate_up,
                   w_down=w_down)
    moments_m = dict(w_in=m_w_in, w_branch_a=m_w_branch_a, w_branch_b=m_w_branch_b, w_out=m_w_out,
                     w_gate_up=m_w_gate_up, w_down=m_w_down)
    moments_v = dict(w_in=v_w_in, w_branch_a=v_w_branch_a, w_branch_b=v_w_branch_b, w_out=v_w_out,
                     w_gate_up=v_w_gate_up, w_down=v_w_down)
    xi, yi, ci = _coords()
    me = 4 * xi + 2 * yi + ci
    core = jnp.reshape(ci, (1,)).astype(jnp.int32)
    chip = jnp.reshape(2 * xi + yi, (1,)).astype(jnp.int32)

    ada_w = w_ada.shape[2]
    c_act = _small_allgather(c, name="comm_gather_c", silu=True).reshape(N_DEV, d)
    c_pad = jnp.concatenate([c_act, jnp.zeros_like(c_act)], axis=0).astype(BF16)
    bias = lax.dynamic_slice(b_ada, (0, me * ada_w), (depth, ada_w))
    mod_part = jnp.stack([_mm(c_pad, w_ada[l], name="mm_ada")[:N_DEV] for l in range(depth)]) + bias[:, None, :]
    mod_all = _small_allgather(mod_part.reshape(1, depth * N_DEV * ada_w), name="comm_gather_mod")
    mod_all = mod_all.reshape(N_DEV, depth, N_DEV, ada_w)
    mod_mine = lax.dynamic_index_in_dim(mod_all, me, axis=2, keepdims=False)
    mods = jnp.transpose(mod_mine, (1, 0, 2)).reshape(depth, 1, 6 * d)

    full = []
    for l in range(depth):
        shards = [weights[k][l].astype(BF16) for k in GATHERED]
        outs = _gather_weights(shards, [SHARD_AXIS[k] for k in GATHERED], name="comm_gather_weights")
        full.append(dict(zip(GATHERED, outs)))

    cos2, sin2 = _rope_tables(seq)
    gains = [jnp.stack([qn_g[l], kn_g[l]])[:, None, :] for l in range(depth)]
    g1s = [norm1_g[l][None] for l in range(depth)]
    g2s = [norm2_g[l][None] for l in range(depth)]

    h = x[0]
    saved = []
    for l in range(depth):
        h, sv = _layer_fwd(h, mods[l], g1s[l], g2s[l], gains[l], full[l], cos2, sin2)
        saved.append(sv)
    loss_part, dh = _loss_fwd(h, loss_target[0], name="loss")
    loss = lax.psum(loss_part[0, 0], ("x", "y", "c"))
    gbufs = {}
    dmods, dg1s, dg2s, dgains = [None] * depth, [None] * depth, [None] * depth, [None] * depth
    for l in reversed(range(depth)):
        dh, dmods[l], dg1s[l], dg2s[l], dgains[l] = _layer_bwd(
            dh, saved[l], mods[l], g1s[l], g2s[l], gains[l], full[l], cos2, sin2, l, depth, gbufs)
    grad_x = dh[None]

    small = jnp.concatenate(
        dmods + dg1s + dg2s + [dgains[l][0] for l in range(depth)] + [dgains[l][1] for l in range(depth)], axis=1)
    small_all = _small_allgather(small, name="comm_gather_small")

    def pack(b, n1, n2, qn, kn):
        return jnp.concatenate([t_.reshape(1, -1) for t_ in (b, n1, n2, qn, kn)], axis=1)

    sg, sd, sm, sv_ = _adamw_replicated(small_all, pack(b_ada, norm1_g, norm2_g, qn_g, kn_g),
                                        pack(m_b_ada, m_norm1_g, m_norm2_g, m_qn_g, m_kn_g),
                                        pack(v_b_ada, v_norm1_g, v_norm2_g, v_qn_g, v_kn_g), name="adamw_replicated")

    def unpack(p):
        sizes = [depth * 6 * d, depth * d, depth * d, depth * HEAD_DIM, depth * HEAD_DIM]
        shapes = [b_ada.shape, norm1_g.shape, norm2_g.shape, qn_g.shape, kn_g.shape]
        out, off = [], 0
        for n, shp in zip(sizes, shapes):
            out.append(p[0, off:off + n].reshape(shp))
            off += n
        return dict(zip(("b_ada", "norm1_g", "norm2_g", "qn_g", "kn_g"), out))

    ug, ud, um, uv = unpack(sg), unpack(sd), unpack(sm), unpack(sv_)
    res = {k: dict(g=ug[k], d=ud[k], m=um[k], v=uv[k]) for k in ug}

    dmod_all = small_all[:, 0, :depth * 6 * d].reshape(N_DEV, depth, 6 * d)
    g_ada = None
    for l in range(depth):
        dm = lax.dynamic_slice(dmod_all[:, l, :], (0, me * ada_w), (N_DEV, ada_w))
        dm = jnp.concatenate([dm, jnp.zeros_like(dm)], axis=0).astype(BF16)
        g_ada = _mm(c_pad, dm, ta=True, name="mm_wgrad_ada", stack=(l, depth, g_ada))
    d_ada, m_ada, v_ada = _adamw_local(g_ada, w_ada, m_w_ada, v_w_ada, name="adamw_local")
    res["w_ada"] = dict(g=g_ada, d=d_ada, m=m_ada, v=v_ada)

    axes = [SHARD_AXIS[k] for k in GATHERED]
    sib = _rs_pair_exchange([gbufs[k] for k in GATHERED], axes, name="comm_rs_pair")
    sums = [_pair_sum(gbufs[k], s_, core, ax, name="pair_sum_" + k) for k, s_, ax in zip(GATHERED, sib, axes)]
    remote = _rs_chip_exchange(sums, name="comm_rs_chip")
    for k, p_, r_ in zip(GATHERED, sums, remote):
        g_, d_, m_, v_ = _adamw_sharded(p_, r_, chip, weights[k], moments_m[k], moments_v[k], name="adamw_" + k)
        res[k] = dict(g=g_, d=d_, m=m_, v=v_)

    order = ("w_ada", "b_ada", "norm1_g", "norm2_g", "w_in", "qn_g", "kn_g", "w_branch_a", "w_branch_b", "w_out",
             "w_gate_up", "w_down")
    return (loss, grad_x, *[res[k]["g"] for k in order], *[res[k]["d"] for k in order],
            *[res[k]["m"] for k in order], *[res[k]["v"] for k in order])
```

```python
import functools

import jax
import jax.numpy as jnp
from jax import lax
from jax.experimental import pallas as pl
from jax.experimental.pallas import tpu as pltpu

F32 = jnp.float32
BF16 = jnp.bfloat16

HEAD_DIM = 128
BLOCK = 128
DILATIONS = (1, 4, 16)
HEADS_PER_GROUP = 4
A_HEADS = 12
SB_HEADS = 4
GROUP_W = HEADS_PER_GROUP * HEAD_DIM
A_W = A_HEADS * HEAD_DIM
B_W = SB_HEADS * HEAD_DIM
OFF_QA, OFF_KA, OFF_VA = 0, A_W, 2 * A_W
OFF_QB, OFF_KB, OFF_VB = 3 * A_W, 3 * A_W + B_W, 3 * A_W + 2 * B_W
OFF_GATES = 3 * A_W + 3 * B_W
ROPE_THETA = 10000.0
EPS = 1e-6
ATT_SCALE = HEAD_DIM ** -0.5
MASKED = -1e30

ADAM_LR, ADAM_B1, ADAM_B2, ADAM_EPS, ADAM_WD, ADAM_STEP = 0.001, 0.9, 0.999, 1e-08, 0.01, 10

N_DEV = 8
N_CHIPS = 4
V7X_VMEM_LIMIT_BYTES = 56 * 1024 * 1024
ELEMWISE_BLOCK_BYTES = 1024 * 1024
MESH = pl.DeviceIdType.MESH

NN = (((1,), (0,)), ((), ()))
NT = (((1,), (1,)), ((), ()))
TN = (((0,), (0,)), ((), ()))


def _dot(a, b, dims=NN):
    return lax.dot_general(a, b, dims, preferred_element_type=F32)


def _tile(n, cap, mult=128):
    best = None
    for t in range(mult, min(n, cap) + 1, mult):
        if n % t == 0:
            best = t
    if best is None:
        assert n <= 2 * cap, (n, cap)
        return n
    return best


def _rows(r, c):
    return _tile(r, max(16, ELEMWISE_BLOCK_BYTES // (4 * c)), 16)


def _pcall(body, *, name, out_shape, grid=None, in_specs=None, out_specs=None, scratch=(), aliases=None,
           grid_spec=None, deps=()):
    kwargs = {}
    deps = list(deps)
    if grid_spec is not None:
        assert not deps
        kwargs["grid_spec"] = grid_spec
    else:
        if grid is not None:
            kwargs["grid"] = grid
        n_in = len(in_specs)
        kwargs["in_specs"] = list(in_specs) + [pl.BlockSpec(memory_space=pl.ANY)] * len(deps)
        kwargs["out_specs"] = out_specs
        kwargs["scratch_shapes"] = list(scratch)
        if deps:
            inner = body

            def body(*refs):
                return inner(*refs[:n_in], *refs[n_in + len(deps):])

    call = pl.pallas_call(
        body, name=name, out_shape=out_shape, input_output_aliases=aliases or {},
        compiler_params=pltpu.CompilerParams(vmem_limit_bytes=V7X_VMEM_LIMIT_BYTES), **kwargs)
    return (lambda *args: call(*args, *deps)) if deps else call


def _mm(a, b, *, name, ta=False, tb=False, out_dtype=F32, caps=(1024, 1024, 1024), stack=None, deps=()):
    kdim, m = a.shape if ta else a.shape[::-1]
    n, k2 = b.shape if tb else b.shape[::-1]
    assert kdim == k2, (a.shape, b.shape, ta, tb)
    tm, tn, tk = _tile(m, caps[0]), _tile(n, caps[1]), _tile(kdim, caps[2])
    nk = kdim // tk
    dims = (((0 if ta else 1,), (1 if tb else 0,)), ((), ()))

    def body(*refs):
        a_ref, b_ref = refs[0], refs[1]
        part = _dot(a_ref[...].astype(BF16), b_ref[...].astype(BF16), dims)
        if nk == 1:
            o_ref = refs[-1]
            o_ref[...] = part.astype(o_ref.dtype)
            return
        o_ref, acc_ref = refs[-2], refs[-1]
        k = pl.program_id(2)

        @pl.when(k == 0)
        def _():
            acc_ref[...] = part

        @pl.when(k > 0)
        def _():
            acc_ref[...] += part

        @pl.when(k == nk - 1)
        def _():
            o_ref[...] = acc_ref[...].astype(o_ref.dtype)

    a_spec = (pl.BlockSpec((tk, tm), lambda i, j, k: (k, i)) if ta
              else pl.BlockSpec((tm, tk), lambda i, j, k: (i, k)))
    b_spec = (pl.BlockSpec((tn, tk), lambda i, j, k: (j, k)) if tb
              else pl.BlockSpec((tk, tn), lambda i, j, k: (k, j)))
    ins, in_specs, aliases = [a, b], [a_spec, b_spec], {}
    if stack is None:
        out_shape = jax.ShapeDtypeStruct((m, n), out_dtype)
        out_spec = pl.BlockSpec((tm, tn), lambda i, j, k: (i, j))
    else:
        layer, n_layers, buf = stack
        out_shape = jax.ShapeDtypeStruct((n_layers, m, n), out_dtype)
        out_spec = pl.BlockSpec((None, tm, tn), lambda i, j, k: (layer, i, j))
        if buf is not None:
            ins.append(buf)
            in_specs.append(pl.BlockSpec(memory_space=pl.ANY))
            aliases = {2: 0}
    scratch = [] if nk == 1 else [pltpu.VMEM((tm, tn), F32)]
    return _pcall(body, name=name, out_shape=out_shape, grid=(m // tm, n // tn, nk), in_specs=in_specs,
                  out_specs=out_spec, scratch=scratch, aliases=aliases, deps=deps)(*ins)


def _rmsmod_fwd(h, g, scale, shift, *, name, deps=()):
    s, d = h.shape
    ts = _rows(s, d)

    def body(h_ref, g_ref, sc_ref, sh_ref, u_ref):
        hf = h_ref[...]
        r = lax.rsqrt(jnp.mean(hf * hf, axis=-1, keepdims=True) + EPS)
        u_ref[...] = (((hf * r) * g_ref[...]) * (1.0 + sc_ref[...]) + sh_ref[...]).astype(BF16)

    row = pl.BlockSpec((ts, d), lambda i: (i, 0))
    vec = pl.BlockSpec((1, d), lambda i: (0, 0))
    return _pcall(body, name=name, out_shape=jax.ShapeDtypeStruct((s, d), BF16), grid=(s // ts,),
                  in_specs=[row, vec, vec, vec], out_specs=row, deps=deps)(h, g, scale, shift)


def _rmsmod_bwd(du, h, g, scale, dres, *, name):
    s, d = h.shape
    ts = _rows(s, d)

    def body(du_ref, h_ref, g_ref, sc_ref, dres_ref, dh_ref, dsh_ref, dsc_ref, dg_ref):
        @pl.when(pl.program_id(0) == 0)
        def _():
            dsh_ref[...] = jnp.zeros_like(dsh_ref)
            dsc_ref[...] = jnp.zeros_like(dsc_ref)
            dg_ref[...] = jnp.zeros_like(dg_ref)

        hf, duf, gain = h_ref[...], du_ref[...], g_ref[...]
        r = lax.rsqrt(jnp.mean(hf * hf, axis=-1, keepdims=True) + EPS)
        xh = hf * r
        dn = duf * (1.0 + sc_ref[...])
        dsh_ref[...] += jnp.sum(duf, axis=0, keepdims=True)
        dsc_ref[...] += jnp.sum(duf * (xh * gain), axis=0, keepdims=True)
        dg_ref[...] += jnp.sum(dn * xh, axis=0, keepdims=True)
        dxh = dn * gain
        dh_ref[...] = dres_ref[...] + r * (dxh - xh * jnp.mean(dxh * xh, axis=-1, keepdims=True))

    row = pl.BlockSpec((ts, d), lambda i: (i, 0))
    vec = pl.BlockSpec((1, d), lambda i: (0, 0))
    vshape = jax.ShapeDtypeStruct((1, d), F32)
    return _pcall(body, name=name, out_shape=(jax.ShapeDtypeStruct((s, d), F32), vshape, vshape, vshape),
                  grid=(s // ts,), in_specs=[row, row, vec, vec, row],
                  out_specs=(row, vec, vec, vec))(du, h, g, scale, dres)


def _resid_gate(h, gate, t, *, name):
    s, d = h.shape
    ts = _rows(s, d)

    def body(h_ref, g_ref, t_ref, o_ref):
        o_ref[...] = h_ref[...] + g_ref[...] * t_ref[...]

    row = pl.BlockSpec((ts, d), lambda i: (i, 0))
    vec = pl.BlockSpec((1, d), lambda i: (0, 0))
    return _pcall(body, name=name, out_shape=jax.ShapeDtypeStruct((s, d), F32), grid=(s // ts,),
                  in_specs=[row, vec, row], out_specs=row)(h, gate, t)


def _resid_gate_bwd(dh, t, gate, *, name, deps=()):
    s, d = dh.shape
    ts = _rows(s, d)

    def body(dh_ref, t_ref, g_ref, dt_ref, dg_ref):
        @pl.when(pl.program_id(0) == 0)
        def _():
            dg_ref[...] = jnp.zeros_like(dg_ref)

        dhf = dh_ref[...]
        dt_ref[...] = (dhf * g_ref[...]).astype(BF16)
        dg_ref[...] += jnp.sum(dhf * t_ref[...], axis=0, keepdims=True)

    row = pl.BlockSpec((ts, d), lambda i: (i, 0))
    vec = pl.BlockSpec((1, d), lambda i: (0, 0))
    return _pcall(body, name=name,
                  out_shape=(jax.ShapeDtypeStruct((s, d), BF16), jax.ShapeDtypeStruct((1, d), F32)),
                  grid=(s // ts,), in_specs=[row, row, vec], out_specs=(row, vec), deps=deps)(dh, t, gate)


def _merge_fwd(proj, y_a, y_b, *, name):
    s, d = y_a.shape
    ts = _rows(s, d)
    ga_blk = OFF_GATES // d

    def body(ga_ref, gb_ref, ya_ref, yb_ref, o_ref):
        o_ref[...] = (jax.nn.sigmoid(ga_ref[...]) * ya_ref[...]
                      + jax.nn.sigmoid(gb_ref[...]) * yb_ref[...]).astype(BF16)

    row = pl.BlockSpec((ts, d), lambda i: (i, 0))
    ga = pl.BlockSpec((ts, d), lambda i: (i, ga_blk))
    gb = pl.BlockSpec((ts, d), lambda i: (i, ga_blk + 1))
    return _pcall(body, name=name, out_shape=jax.ShapeDtypeStruct((s, d), BF16), grid=(s // ts,),
                  in_specs=[ga, gb, row, row], out_specs=row)(proj, proj, y_a, y_b)


def _merge_bwd(dm, proj, y_a, y_b, *, name):
    s, d = y_a.shape
    ts = _rows(s, d)
    ga_blk = OFF_GATES // d

    def body(dm_ref, ga_ref, gb_ref, ya_ref, yb_ref, dya_ref, dyb_ref, dga_ref, dgb_ref):
        dmf = dm_ref[...]
        sa, sb = jax.nn.sigmoid(ga_ref[...]), jax.nn.sigmoid(gb_ref[...])
        dya_ref[...] = (dmf * sa).astype(BF16)
        dyb_ref[...] = (dmf * sb).astype(BF16)
        dga_ref[...] = (dmf * ya_ref[...] * (sa * (1.0 - sa))).astype(BF16)
        dgb_ref[...] = (dmf * yb_ref[...] * (sb * (1.0 - sb))).astype(BF16)

    row = pl.BlockSpec((ts, d), lambda i: (i, 0))
    ga = pl.BlockSpec((ts, d), lambda i: (i, ga_blk))
    gb = pl.BlockSpec((ts, d), lambda i: (i, ga_blk + 1))
    shp = jax.ShapeDtypeStruct((s, d), BF16)
    return _pcall(body, name=name, out_shape=(shp, shp, shp, shp), grid=(s // ts,),
                  in_specs=[row, ga, gb, row, row], out_specs=(row, row, row, row))(dm, proj, proj, y_a, y_b)


def _swiglu_fwd(gu, *, name):
    s, f2 = gu.shape
    f = f2 // 2
    ts = _rows(s, f)

    def body(g_ref, u_ref, a_ref):
        gf = g_ref[...]
        a_ref[...] = ((gf * jax.nn.sigmoid(gf)) * u_ref[...]).astype(BF16)

    return _pcall(body, name=name, out_shape=jax.ShapeDtypeStruct((s, f), BF16), grid=(s // ts,),
                  in_specs=[pl.BlockSpec((ts, f), lambda i: (i, 0)), pl.BlockSpec((ts, f), lambda i: (i, 1))],
                  out_specs=pl.BlockSpec((ts, f), lambda i: (i, 0)))(gu, gu)


def _swiglu_bwd(gu, da, *, name):
    s, f2 = gu.shape
    f = f2 // 2
    ts = _rows(s, f)

    def body(g_ref, u_ref, da_ref, o_ref):
        gf, daf = g_ref[...], da_ref[...]
        sg = jax.nn.sigmoid(gf)
        o_ref[:, :f] = (daf * u_ref[...] * (sg * (1.0 + gf * (1.0 - sg)))).astype(BF16)
        o_ref[:, f:] = (daf * (gf * sg)).astype(BF16)

    return _pcall(body, name=name, out_shape=jax.ShapeDtypeStruct((s, f2), BF16), grid=(s // ts,),
                  in_specs=[pl.BlockSpec((ts, f), lambda i: (i, 0)), pl.BlockSpec((ts, f), lambda i: (i, 1)),
                            pl.BlockSpec((ts, f), lambda i: (i, 0))],
                  out_specs=pl.BlockSpec((ts, f2), lambda i: (i, 0)))(gu, gu, da)


def _loss_fwd(y, tgt, *, name):
    s, d = y.shape
    ts = _rows(s, d)

    def body(y_ref, t_ref, l_ref, dy_ref):
        @pl.when(pl.program_id(0) == 0)
        def _():
            l_ref[...] = jnp.zeros_like(l_ref)

        e = y_ref[...] - t_ref[...]
        dy_ref[...] = e * (1.0 / d)
        per_tok = jnp.sum(e * e, axis=1, keepdims=True) * (1.0 / d)
        l_ref[...] += 0.5 * jnp.sum(per_tok, axis=0, keepdims=True)

    row = pl.BlockSpec((ts, d), lambda i: (i, 0))
    return _pcall(body, name=name,
                  out_shape=(jax.ShapeDtypeStruct((1, 128), F32), jax.ShapeDtypeStruct((s, d), F32)),
                  grid=(s // ts,), in_specs=[row, row],
                  out_specs=(pl.BlockSpec((1, 128), lambda i: (0, 0)), row))(y, tgt)


def _rope_tables(seq):
    inv = jnp.power(ROPE_THETA, -jnp.arange(0, HEAD_DIM, 2, dtype=F32) / HEAD_DIM)
    ang = jnp.arange(seq, dtype=F32)[:, None] * inv[None, :]
    cos, sin = jnp.cos(ang), jnp.sin(ang)
    return jnp.concatenate([cos, cos], axis=1), jnp.concatenate([-sin, sin], axis=1)


def _qkrope_fwd(proj, gains, cos2, sin2, *, name):
    s = proj.shape[0]
    ts = _rows(s, A_W)

    def body(x_ref, g_ref, c_ref, s_ref, o_ref):
        gain, cos, sin = g_ref[...], c_ref[...], s_ref[...]
        for h in range(A_HEADS):
            lanes = slice(h * HEAD_DIM, (h + 1) * HEAD_DIM)
            x = x_ref[:, lanes]
            y = (x * lax.rsqrt(jnp.mean(x * x, axis=-1, keepdims=True) + EPS)) * gain
            o_ref[:, lanes] = (y * cos + pltpu.roll(y, HEAD_DIM // 2, 1) * sin).astype(BF16)

    heads = pl.BlockSpec((ts, A_W), lambda i, j: (i, j))
    tab = pl.BlockSpec((ts, HEAD_DIM), lambda i, j: (i, 0))
    gain = pl.BlockSpec((None, 1, HEAD_DIM), lambda i, j: (j, 0, 0))
    return _pcall(body, name=name, out_shape=jax.ShapeDtypeStruct((s, 2 * A_W), BF16),
                  grid=(s // ts, 2), in_specs=[heads, gain, tab, tab], out_specs=heads)(
                      proj, gains, cos2, sin2)


def _qkrope_bwd(dqk, proj, gains, cos2, sin2, *, name):
    s = proj.shape[0]
    ts = _rows(s, A_W)

    def body(d_ref, x_ref, g_ref, c_ref, s_ref, dx_ref, dg_ref):
        @pl.when(pl.program_id(1) == 0)
        def _():
            dg_ref[...] = jnp.zeros_like(dg_ref)

        gain, cos, sin = g_ref[...], c_ref[...], s_ref[...]
        dg = jnp.zeros((1, HEAD_DIM), F32)
        for h in range(A_HEADS):
            lanes = slice(h * HEAD_DIM, (h + 1) * HEAD_DIM)
            dout = d_ref[:, lanes]
            dy = dout * cos + pltpu.roll(dout * sin, HEAD_DIM // 2, 1)
            x = x_ref[:, lanes]
            r = lax.rsqrt(jnp.mean(x * x, axis=-1, keepdims=True) + EPS)
            xh = x * r
            dg = dg + jnp.sum(dy * xh, axis=0, keepdims=True)
            dxh = dy * gain
            dx_ref[:, lanes] = (r * (dxh - xh * jnp.mean(dxh * xh, axis=-1, keepdims=True))).astype(BF16)
        dg_ref[...] += dg

    heads = pl.BlockSpec((ts, A_W), lambda j, i: (i, j))
    tab = pl.BlockSpec((ts, HEAD_DIM), lambda j, i: (i, 0))
    gain = pl.BlockSpec((None, 1, HEAD_DIM), lambda j, i: (j, 0, 0))
    return _pcall(body, name=name,
                  out_shape=(jax.ShapeDtypeStruct((s, 2 * A_W), BF16), jax.ShapeDtypeStruct((2, 1, HEAD_DIM), F32)),
                  grid=(2, s // ts), in_specs=[heads, heads, gain, tab, tab],
                  out_specs=(heads, gain))(dqk, proj, gains, cos2, sin2)


def _block_rows(blk):
    if isinstance(blk, int):
        return pl.ds(blk * BLOCK, BLOCK)
    return pl.ds(pl.multiple_of(blk * BLOCK, BLOCK), BLOCK)


def _band_masks(n, with_prev):
    row = lax.broadcasted_iota(jnp.int32, (BLOCK, BLOCK), 0)
    col = lax.broadcasted_iota(jnp.int32, (BLOCK, BLOCK), 1)
    cur = col <= row
    if not with_prev:
        return [(n, cur)]
    prev = col >= row + jnp.where(n >= 1, 0, BLOCK)
    return [(n, cur), (jnp.maximum(n - 1, 0), prev)]


def _dil_fwd(q_arr, k_arr, v_arr, offs, length, dil, *, name):
    nj, nb = dil * HEADS_PER_GROUP, length // BLOCK
    ju = HEADS_PER_GROUP
    qo, ko, vo = (off // ju for off in offs)
    assert all(off % ju == 0 for off in offs)

    def body(q_ref, k_ref, v_ref, o_ref, l_ref):
        n = pl.program_id(1)
        masks = _band_masks(n, nb > 1)
        for cb in range(ju):
            lanes = slice(cb * HEAD_DIM, (cb + 1) * HEAD_DIM)
            q = q_ref[:, lanes].astype(BF16)
            parts = []
            for blk, mask in masks:
                rows = _block_rows(blk)
                sc = _dot(q, k_ref[rows, lanes].astype(BF16), NT) * ATT_SCALE
                parts.append((jnp.where(mask, sc, MASKED), rows))
            m = parts[0][0].max(axis=-1, keepdims=True)
            for sc, _ in parts[1:]:
                m = jnp.maximum(m, sc.max(axis=-1, keepdims=True))
            den = jnp.zeros((BLOCK, 1), F32)
            acc = jnp.zeros((BLOCK, HEAD_DIM), F32)
            for sc, rows in parts:
                p = jnp.exp(sc - m)
                den = den + jnp.sum(p, axis=-1, keepdims=True)
                acc = acc + _dot(p.astype(BF16), v_ref[rows, lanes].astype(BF16))
            o_ref[:, lanes] = acc / den
            l_ref[:, lanes] = jnp.broadcast_to(m + jnp.log(den), (BLOCK, HEAD_DIM))

    qspec = pl.BlockSpec((BLOCK, ju * HEAD_DIM), lambda j, n: (n, qo + j))
    kspec = pl.BlockSpec((length, ju * HEAD_DIM), lambda j, n: (0, ko + j))
    vspec = pl.BlockSpec((length, ju * HEAD_DIM), lambda j, n: (0, vo + j))
    ospec = pl.BlockSpec((BLOCK, ju * HEAD_DIM), lambda j, n: (n, j))
    shp = jax.ShapeDtypeStruct((length, nj * HEAD_DIM), F32)
    return _pcall(body, name=name, out_shape=(shp, shp), grid=(nj // ju, nb), in_specs=[qspec, kspec, vspec],
                  out_specs=(ospec, ospec))(q_arr, k_arr, v_arr)


def _dil_bwd(q_arr, k_arr, v_arr, offs, o, lse, do, dlse, length, dil, *, name):
    nj, nb = dil * HEADS_PER_GROUP, length // BLOCK
    ju = HEADS_PER_GROUP if length <= 4 * BLOCK else 2
    qo, ko, vo = (off // ju for off in offs)
    assert all(off % ju == 0 for off in offs)

    def body(q_ref, k_ref, v_ref, o_ref, l_ref, do_ref, dl_ref, dq_ref, dk_ref, dv_ref):
        dk_ref[...] = jnp.zeros_like(dk_ref)
        dv_ref[...] = jnp.zeros_like(dv_ref)

        def step(n, carry):
            qrows = _block_rows(n)
            masks = _band_masks(n, nb > 1)
            for cb in range(ju):
                lanes = slice(cb * HEAD_DIM, (cb + 1) * HEAD_DIM)
                q = q_ref[qrows, lanes].astype(BF16)
                dof = do_ref[qrows, lanes]
                dob = dof.astype(BF16)
                lse_b = l_ref[qrows, lanes]
                shift = dl_ref[qrows, lanes] - jnp.sum(dof * o_ref[qrows, lanes], axis=-1, keepdims=True)
                dq = jnp.zeros((BLOCK, HEAD_DIM), F32)
                for blk, mask in masks:
                    rows = _block_rows(blk)
                    kk, vv = k_ref[rows, lanes].astype(BF16), v_ref[rows, lanes].astype(BF16)
                    sc = _dot(q, kk, NT) * ATT_SCALE
                    p = jnp.where(mask, jnp.exp(sc - lse_b), 0.0)
                    ds = (p * (_dot(dob, vv, NT) + shift)).astype(BF16)
                    dq = dq + _dot(ds, kk)
                    dk_ref[rows, lanes] += _dot(ds, q, TN) * ATT_SCALE
                    dv_ref[rows, lanes] += _dot(p.astype(BF16), dob, TN)
                dq_ref[qrows, lanes] = dq * ATT_SCALE
            return carry

        if nb == 1:
            step(0, 0)
        else:
            lax.fori_loop(0, nb, step, 0)

    def col(off):
        return pl.BlockSpec((length, ju * HEAD_DIM), lambda j: (0, off + j))

    shp = jax.ShapeDtypeStruct((length, nj * HEAD_DIM), F32)
    return _pcall(body, name=name, out_shape=(shp, shp, shp), grid=(nj // ju,),
                  in_specs=[col(qo), col(ko), col(vo), col(0), col(0), col(0), col(0)],
                  out_specs=(col(0), col(0), col(0)))(q_arr, k_arr, v_arr, o, lse, do, dlse)


def _combine_weights(l_refs):
    ls = [r[...] for r in l_refs]
    m = jnp.maximum(jnp.maximum(ls[0], ls[1]), ls[2])
    es = [jnp.exp(l - m) for l in ls]
    den = es[0] + es[1] + es[2]
    return [e / den for e in es]


def _combine_fwd(os_, lses, *, name):
    s = os_[0].shape[0]
    ts = _rows(s, GROUP_W)

    def body(o0, o1, o2, l0, l1, l2, out_ref):
        w = _combine_weights((l0, l1, l2))
        out_ref[...] = (w[0] * o0[...] + w[1] * o1[...] + w[2] * o2[...]).astype(BF16)

    row = pl.BlockSpec((ts, GROUP_W), lambda i: (i, 0))
    return _pcall(body, name=name, out_shape=jax.ShapeDtypeStruct((s, GROUP_W), BF16), grid=(s // ts,),
                  in_specs=[row] * 6, out_specs=row)(*os_, *lses)


def _combine_bwd(do_a, os_, lses, *, name):
    s = do_a.shape[0]
    ts = _rows(s, GROUP_W)

    def body(d_ref, o0, o1, o2, l0, l1, l2, do0, do1, do2, dl0, dl1, dl2):
        w = _combine_weights((l0, l1, l2))
        d = d_ref[...]
        og = [o0[...], o1[...], o2[...]]
        oa = w[0] * og[0] + w[1] * og[1] + w[2] * og[2]
        ta = jnp.sum(d * oa, axis=-1, keepdims=True)
        for g, (do_ref, dl_ref) in enumerate(((do0, dl0), (do1, dl1), (do2, dl2))):
            do_ref[...] = w[g] * d
            dl_ref[...] = w[g] * (jnp.sum(d * og[g], axis=-1, keepdims=True) - ta)

    head = pl.BlockSpec((ts, HEAD_DIM), lambda i, h: (i, h))
    shp = jax.ShapeDtypeStruct((s, GROUP_W), F32)
    return _pcall(body, name=name, out_shape=(shp,) * 6, grid=(s // ts, HEADS_PER_GROUP),
                  in_specs=[head] * 7, out_specs=(head,) * 6)(do_a, *os_, *lses)


def _dot_exact(x, ones_mask):
    hi = x.astype(BF16)
    r1 = x - hi.astype(F32)
    mid = r1.astype(BF16)
    lo = (r1 - mid.astype(F32)).astype(BF16)
    return _dot(hi, ones_mask) + _dot(mid, ones_mask) + _dot(lo, ones_mask)


SB_QROWS = 2 * BLOCK
SB_UNROLL = 4


def _sb_mask(j, i):
    row = lax.broadcasted_iota(jnp.int32, (SB_QROWS, BLOCK), 0)
    col = lax.broadcasted_iota(jnp.int32, (SB_QROWS, BLOCK), 1)
    return col + (j * BLOCK - i * SB_QROWS) < row


def _sb_steps(i):
    return ((i + 1) * (SB_QROWS // BLOCK) + SB_UNROLL - 1) // SB_UNROLL


def _sb_scores(q, kk, j, i):
    mask = _sb_mask(j, i)
    z = _dot(q, kk, NT) * ATT_SCALE
    sp = jnp.log1p(jnp.exp(-jnp.abs(z)))
    log_beta = jnp.minimum(z, 0.0) - sp
    log_1mb = jnp.where(mask, jnp.minimum(-z, 0.0) - sp, 0.0)
    return z, log_beta, log_1mb, mask


def _sb_weights(log_beta, log_1mb, mask, run, upper):
    after = run + _dot_exact(log_1mb, upper)
    return jnp.where(mask, jnp.exp(log_beta + after), 0.0)


def _tri(strict_lower):
    row = lax.broadcasted_iota(jnp.int32, (BLOCK, BLOCK), 0)
    col = lax.broadcasted_iota(jnp.int32, (BLOCK, BLOCK), 1)
    return ((row > col) if strict_lower else (row < col)).astype(BF16)


def _sb_fwd(proj, *, name):
    s = proj.shape[0]
    assert s % (BLOCK * SB_UNROLL) == 0 and s % SB_QROWS == 0
    qb, kb, vb = OFF_QB // HEAD_DIM, OFF_KB // HEAD_DIM, OFF_VB // HEAD_DIM

    def body(q_ref, k_ref, v_ref, o_ref):
        i = pl.program_id(1)
        q = q_ref[...].astype(BF16)
        upper = _tri(True)
        nsteps = _sb_steps(i)

        def step(t, carry):
            acc, run = carry
            for b in reversed(range(SB_UNROLL)):
                j = (nsteps - 1 - t) * SB_UNROLL + b
                rows = _block_rows(j)
                _, log_beta, log_1mb, mask = _sb_scores(q, k_ref[rows, :].astype(BF16), j, i)
                a = _sb_weights(log_beta, log_1mb, mask, run, upper)
                acc = acc + _dot(a.astype(BF16), v_ref[rows, :].astype(BF16))
                run = run + jnp.sum(log_1mb, axis=-1, keepdims=True)
            return acc, run

        acc, _ = lax.fori_loop(0, nsteps, step,
                               (jnp.zeros((SB_QROWS, HEAD_DIM), F32), jnp.zeros((SB_QROWS, 1), F32)))
        o_ref[...] = acc.astype(BF16)

    return _pcall(body, name=name, out_shape=jax.ShapeDtypeStruct((s, B_W), BF16), grid=(SB_HEADS, s // SB_QROWS),
                  in_specs=[pl.BlockSpec((SB_QROWS, HEAD_DIM), lambda h, i: (i, qb + h)),
                            pl.BlockSpec((s, HEAD_DIM), lambda h, i: (0, kb + h)),
                            pl.BlockSpec((s, HEAD_DIM), lambda h, i: (0, vb + h))],
                  out_specs=pl.BlockSpec((SB_QROWS, HEAD_DIM), lambda h, i: (i, h)))(proj, proj, proj)


def _sb_bwd(proj, do_b, *, name):
    s = proj.shape[0]
    assert s % (BLOCK * SB_UNROLL) == 0 and s % SB_QROWS == 0
    nkb = s // BLOCK
    qb, kb, vb = OFF_QB // HEAD_DIM, OFF_KB // HEAD_DIM, OFF_VB // HEAD_DIM

    def body(q_ref, k_ref, v_ref, do_ref, dq_ref, dk_ref, dv_ref, z_s, a_s):
        i = pl.program_id(1)

        @pl.when(i == 0)
        def _():
            dk_ref[...] = jnp.zeros_like(dk_ref)
            dv_ref[...] = jnp.zeros_like(dv_ref)

        q = q_ref[...].astype(BF16)
        dob = do_ref[...].astype(BF16)
        upper, lower = _tri(True), _tri(False)
        nsteps = _sb_steps(i)

        def recompute(t, run):
            for b in reversed(range(SB_UNROLL)):
                j = (nsteps - 1 - t) * SB_UNROLL + b
                z, log_beta, log_1mb, mask = _sb_scores(q, k_ref[_block_rows(j), :].astype(BF16), j, i)
                z_s[j] = z
                a_s[j] = _sb_weights(log_beta, log_1mb, mask, run, upper)
                run = run + jnp.sum(log_1mb, axis=-1, keepdims=True)
            return run

        lax.fori_loop(0, nsteps, recompute, jnp.zeros((SB_QROWS, 1), F32))

        def grads(t, carry):
            dq, run = carry
            for b in range(SB_UNROLL):
                j = t * SB_UNROLL + b
                rows = _block_rows(j)
                kk, vv = k_ref[rows, :].astype(BF16), v_ref[rows, :].astype(BF16)
                z, a = z_s[j], a_s[j]
                de = _dot(dob, vv, NT) * a
                before = run + _dot_exact(de, lower)
                dz = (de * jax.nn.sigmoid(-z)
                      - jnp.where(_sb_mask(j, i), jax.nn.sigmoid(z), 0.0) * before).astype(BF16)
                dk_ref[rows, :] += _dot(dz, q, TN) * ATT_SCALE
                dv_ref[rows, :] += _dot(a.astype(BF16), dob, TN)
                dq = dq + _dot(dz, kk)
                run = run + jnp.sum(de, axis=-1, keepdims=True)
            return dq, run

        dq, _ = lax.fori_loop(0, nsteps, grads,
                              (jnp.zeros((SB_QROWS, HEAD_DIM), F32), jnp.zeros((SB_QROWS, 1), F32)))
        dq_ref[...] = dq * ATT_SCALE

    blk = pl.BlockSpec((SB_QROWS, HEAD_DIM), lambda h, i: (i, h))
    full = pl.BlockSpec((s, HEAD_DIM), lambda h, i: (0, h))
    shp = jax.ShapeDtypeStruct((s, B_W), F32)
    return _pcall(body, name=name, out_shape=(shp, shp, shp), grid=(SB_HEADS, s // SB_QROWS),
                  in_specs=[pl.BlockSpec((SB_QROWS, HEAD_DIM), lambda h, i: (i, qb + h)),
                            pl.BlockSpec((s, HEAD_DIM), lambda h, i: (0, kb + h)),
                            pl.BlockSpec((s, HEAD_DIM), lambda h, i: (0, vb + h)), blk],
                  out_specs=(blk, full, full),
                  scratch=[pltpu.VMEM((nkb, SB_QROWS, BLOCK), F32), pltpu.VMEM((nkb, SB_QROWS, BLOCK), F32)])(
                      proj, proj, proj, do_b)


def _coords():
    return lax.axis_index("x"), lax.axis_index("y"), lax.axis_index("c")


def _flip(v, bit):
    return 1 - v if bit else v


def _shard_of(ref, axis, idx, size):
    if axis == 0:
        sl = pl.ds(pl.multiple_of(idx * size, 16), size)
        return ref.at[sl, :] if len(ref.shape) == 2 else ref.at[:, sl, :]
    sl = pl.ds(pl.multiple_of(idx * size, 128), size)
    return ref.at[:, sl] if len(ref.shape) == 2 else ref.at[:, :, sl]


def _small_allgather(v, *, name, silu=False, deps=()):
    n = v.shape[1]

    def body(v_ref, out_ref, send_sems, recv_sems):
        x, y, c = _coords()
        me = 4 * x + 2 * y + c
        val = v_ref[...]
        out_ref[me] = val * jax.nn.sigmoid(val) if silu else val
        copies = []
        for k in range(1, N_DEV):
            peer = (_flip(x, k & 4), _flip(y, k & 2), _flip(c, k & 1))
            copies.append(pltpu.make_async_remote_copy(
                src_ref=out_ref.at[me], dst_ref=out_ref.at[me], send_sem=send_sems.at[k - 1],
                recv_sem=recv_sems.at[k - 1], device_id=peer, device_id_type=MESH))
        for cp in copies:
            cp.start()
        for cp in copies:
            cp.wait_recv()
        for cp in copies:
            cp.wait_send()

    return _pcall(body, name=name, out_shape=jax.ShapeDtypeStruct((N_DEV, 1, n), F32),
                  in_specs=[pl.BlockSpec(memory_space=pltpu.VMEM)], out_specs=pl.BlockSpec(memory_space=pltpu.VMEM),
                  scratch=[pltpu.SemaphoreType.DMA((N_DEV - 1,)), pltpu.SemaphoreType.DMA((N_DEV - 1,))],
                  deps=deps)(v)


def _gather_weights(shards, axes, *, name):
    nt = len(shards)
    sizes = [sh.shape[ax] for sh, ax in zip(shards, axes)]
    out_shape = tuple(
        jax.ShapeDtypeStruct(tuple(d * N_DEV if a == ax else d for a, d in enumerate(sh.shape)), sh.dtype)
        for sh, ax in zip(shards, axes))

    def body(*refs):
        ins, outs = refs[:nt], refs[nt:2 * nt]
        send_sems, recv_sems, local_sems = refs[2 * nt:]
        x, y, c = _coords()
        me, sibling = (x, y, c), (x, y, 1 - c)
        chips = [(1 - x, y), (x, 1 - y), (1 - x, 1 - y)]

        def slot(t, dev):
            return _shard_of(outs[t], axes[t], 4 * dev[0] + 2 * dev[1] + dev[2], sizes[t])

        def copy(t, k, block, to, src=None):
            return pltpu.make_async_remote_copy(
                src_ref=slot(t, block) if src is None else src, dst_ref=slot(t, block),
                send_sem=send_sems.at[t, k], recv_sem=recv_sems.at[t, k], device_id=to, device_id_type=MESH)

        mine = [pltpu.make_async_copy(ins[t], slot(t, me), local_sems.at[t]) for t in range(nt)]
        for cp in mine:
            cp.start()
        first = []
        for t in range(nt):
            first.append(copy(t, 0, me, sibling, src=ins[t]))
            first += [copy(t, 1 + j, me, (*chip, c), src=ins[t]) for j, chip in enumerate(chips)]
        for cp in first:
            cp.start()
        passed = []
        for j, chip in enumerate(chips):
            for t in range(nt):
                copy(t, 1 + j, (*chip, c), me).wait_recv()
                fwd = copy(t, 4 + j, (*chip, c), sibling)
                fwd.start()
                passed.append(fwd)
        for t in range(nt):
            copy(t, 0, sibling, me).wait_recv()
            for j, chip in enumerate(chips):
                copy(t, 4 + j, (*chip, 1 - c), me).wait_recv()
        for cp in first + passed:
            cp.wait_send()
        for cp in mine:
            cp.wait()

    hbm = pl.BlockSpec(memory_space=pl.ANY)
    return _pcall(body, name=name, out_shape=out_shape, in_specs=[hbm] * nt, out_specs=(hbm,) * nt,
                  scratch=[pltpu.SemaphoreType.DMA((nt, 7)), pltpu.SemaphoreType.DMA((nt, 7)),
                           pltpu.SemaphoreType.DMA((nt,))])(*shards)


def _rs_pair_exchange(grads, axes, *, name):
    nt = len(grads)
    sizes = [g.shape[ax] // N_DEV for g, ax in zip(grads, axes)]

    def recv_shape(g, ax):
        dims = list(g.shape)
        dims[ax] //= N_DEV
        return jax.ShapeDtypeStruct((N_CHIPS, dims[0], dims[1]), g.dtype)

    def body(*refs):
        ins, outs = refs[:nt], refs[nt:2 * nt]
        send_sems, recv_sems = refs[2 * nt:]
        x, y, c = _coords()
        copies = []
        for t in range(nt):
            for q in range(N_CHIPS):
                copies.append(pltpu.make_async_remote_copy(
                    src_ref=_shard_of(ins[t], axes[t], 2 * q + 1 - c, sizes[t]), dst_ref=outs[t].at[q],
                    send_sem=send_sems.at[t, q], recv_sem=recv_sems.at[t, q], device_id=(x, y, 1 - c),
                    device_id_type=MESH))
        for cp in copies:
            cp.start()
        for cp in copies:
            cp.wait_recv()
        for cp in copies:
            cp.wait_send()

    hbm = pl.BlockSpec(memory_space=pl.ANY)
    return _pcall(body, name=name, out_shape=tuple(recv_shape(g, ax) for g, ax in zip(grads, axes)),
                  in_specs=[hbm] * nt, out_specs=(hbm,) * nt,
                  scratch=[pltpu.SemaphoreType.DMA((nt, N_CHIPS)), pltpu.SemaphoreType.DMA((nt, N_CHIPS))])(*grads)


def _pair_sum(grad, sib, core, axis, *, name):
    _, r, c = sib.shape
    tr = _rows(r, c)
    nrt = r // tr

    def body(core_ref, g_ref, s_ref, o_ref):
        o_ref[...] = (g_ref[...].astype(F32) + s_ref[...].astype(F32)).astype(BF16)

    if axis == 0:
        gspec = pl.BlockSpec((tr, c), lambda q, i, core_ref: ((2 * q + core_ref[0]) * nrt + i, 0))
    else:
        gspec = pl.BlockSpec((tr, c), lambda q, i, core_ref: (i, 2 * q + core_ref[0]))
    sspec = pl.BlockSpec((None, tr, c), lambda q, i, core_ref: (q, i, 0))
    grid_spec = pltpu.PrefetchScalarGridSpec(num_scalar_prefetch=1, grid=(N_CHIPS, nrt),
                                             in_specs=[gspec, sspec], out_specs=sspec)
    return _pcall(body, name=name, out_shape=jax.ShapeDtypeStruct(sib.shape, BF16), grid_spec=grid_spec)(
        core, grad, sib)


HBM_SPEC = pl.BlockSpec(memory_space=pltpu.HBM)
SEM_SPEC = pl.BlockSpec(memory_space=pltpu.SEMAPHORE)
SPLIT_PARAMS = dict(has_side_effects=pltpu.SideEffectType.DATAFLOW_SIDE_EFFECTING)


def _hbm(a):
    return pltpu.with_memory_space_constraint(a, pltpu.HBM)


def _split_start(copies_fn, buffers, sem_shape, after, *, name):
    n = len(buffers)
    rows, cols = sem_shape
    ns = rows * cols

    def body(*refs):
        sems = refs[n + 1:n + 1 + 2 * ns]
        for cp in copies_fn(refs[:n], _sem_rows(sems[:ns], cols), _sem_rows(sems[ns:], cols)):
            cp.start()
        refs[-1][...] = jnp.zeros_like(refs[-1])

    sem = pltpu.SemaphoreType.DMA(())
    outs = pl.pallas_call(
        body, name=name,
        out_shape=((sem,) * (2 * ns) + tuple(pltpu.HBM(b.shape, b.dtype) for b in buffers)
                   + (jax.ShapeDtypeStruct((8, 128), F32),)),
        in_specs=(HBM_SPEC,) * n + (pl.BlockSpec(memory_space=pl.ANY),),
        out_specs=(SEM_SPEC,) * (2 * ns) + (HBM_SPEC,) * n + (pl.BlockSpec(memory_space=pltpu.VMEM),),
        input_output_aliases={i: 2 * ns + i for i in range(n)},
        compiler_params=pltpu.CompilerParams(**SPLIT_PARAMS))(*[_hbm(b) for b in buffers], after)
    return list(outs[:ns]), list(outs[ns:2 * ns]), list(outs[2 * ns:2 * ns + n]), outs[-1]


def _split_wait(copies_fn, send_sems, recv_sems, buffers, after, sem_rows, *, name):
    n, ns = len(buffers), len(send_sems)
    cols = ns // sem_rows

    def body(*refs):
        sems = refs[n:n + 2 * ns]
        copies = copies_fn(refs[:n], _sem_rows(sems[:ns], cols), _sem_rows(sems[ns:], cols))
        for cp in copies:
            cp.wait_send()
        for cp in copies:
            cp.wait_recv()

    outs = pl.pallas_call(
        body, name=name, out_shape=tuple(pltpu.HBM(b.shape, b.dtype) for b in buffers),
        in_specs=(HBM_SPEC,) * n + (SEM_SPEC,) * (2 * ns) + (pl.BlockSpec(memory_space=pl.ANY),),
        out_specs=(HBM_SPEC,) * n, input_output_aliases={i: i for i in range(n)},
        compiler_params=pltpu.CompilerParams(**SPLIT_PARAMS))(*buffers, *send_sems, *recv_sems, after)
    return list(outs)


def _sem_rows(sems, cols):
    return [sems[i:i + cols] for i in range(0, len(sems), cols)]


def _empty_hbm(shape, dtype):
    return _hbm(lax.empty(shape, dtype))


def _place_own_blocks(shards, axes, *, name):
    nt = len(shards)
    sizes = [sh.shape[ax] for sh, ax in zip(shards, axes)]
    out_shape = tuple(
        jax.ShapeDtypeStruct(tuple(d * N_DEV if a == ax else d for a, d in enumerate(sh.shape)), sh.dtype)
        for sh, ax in zip(shards, axes))

    def body(*refs):
        ins, outs, sems = refs[:nt], refs[nt:2 * nt], refs[2 * nt]
        x, y, c = _coords()
        copies = [pltpu.make_async_copy(ins[t], _shard_of(outs[t], axes[t], 4 * x + 2 * y + c, sizes[t]), sems.at[t])
                  for t in range(nt)]
        for cp in copies:
            cp.start()
        for cp in copies:
            cp.wait()

    hbm = pl.BlockSpec(memory_space=pl.ANY)
    return _pcall(body, name=name, out_shape=out_shape, in_specs=[hbm] * nt, out_specs=(hbm,) * nt,
                  scratch=[pltpu.SemaphoreType.DMA((nt,))])(*shards)


class _SplitGather:
    def __init__(self, shards, axes, tag):
        self.axes, self.tag, self.nt = list(axes), tag, len(shards)
        self.sizes = [sh.shape[ax] for sh, ax in zip(shards, axes)]
        self.shards = list(shards)

    def _slot(self, ref, t, dev):
        return _shard_of(ref, self.axes[t], 4 * dev[0] + 2 * dev[1] + dev[2], self.sizes[t])

    def _first_copies(self, refs, send_sems, recv_sems):
        nt = self.nt
        x, y, c = _coords()
        peers = [(x, y, 1 - c), (1 - x, y, c), (x, 1 - y, c), (1 - x, 1 - y, c)]
        return [pltpu.make_async_remote_copy(
            src_ref=refs[t], dst_ref=self._slot(refs[nt + t], t, (x, y, c)), send_sem=send_sems[t][k],
            recv_sem=recv_sems[t][k], device_id=peer, device_id_type=MESH)
            for t in range(nt) for k, peer in enumerate(peers)]

    def _forward_copies(self, refs, send_sems, recv_sems):
        x, y, c = _coords()
        chips = [(1 - x, y), (x, 1 - y), (1 - x, 1 - y)]
        return [pltpu.make_async_remote_copy(
            src_ref=self._slot(refs[t], t, (*chip, c)), dst_ref=self._slot(refs[t], t, (*chip, c)),
            send_sem=send_sems[t][j], recv_sem=recv_sems[t][j], device_id=(x, y, 1 - c), device_id_type=MESH)
            for t in range(self.nt) for j, chip in enumerate(chips)]

    def first(self, after):
        fulls = _place_own_blocks(self.shards, self.axes, name=f"place_{self.tag}")
        self.s1, self.r1, self.bufs, token = _split_start(
            self._first_copies, self.shards + list(fulls), (self.nt, 4), after,
            name=f"comm_gather1_start_{self.tag}")
        return token

    def forward(self, after):
        bufs = _split_wait(self._first_copies, self.s1, self.r1, self.bufs, after, self.nt,
                           name=f"comm_gather1_wait_{self.tag}")
        self.s2, self.r2, self.fulls, token = _split_start(
            self._forward_copies, bufs[self.nt:], (self.nt, 3), after, name=f"comm_gather2_start_{self.tag}")
        return token

    def finish(self, after):
        return _split_wait(self._forward_copies, self.s2, self.r2, self.fulls, after, self.nt,
                           name=f"comm_gather2_wait_{self.tag}")


class _SplitChipExchange:
    def __init__(self, sums, tag):
        self.nt, self.tag = len(sums), tag
        self.sums = list(sums)

    def _copies(self, refs, send_sems, recv_sems):
        nt = self.nt
        x, y, c = _coords()
        copies = []
        for t in range(nt):
            for k in range(1, N_CHIPS):
                px, py = _flip(x, k & 2), _flip(y, k & 1)
                copies.append(pltpu.make_async_remote_copy(
                    src_ref=refs[t].at[2 * px + py], dst_ref=refs[nt + t].at[k - 1], send_sem=send_sems[t][k - 1],
                    recv_sem=recv_sems[t][k - 1], device_id=(px, py, c), device_id_type=MESH))
        return copies

    def start(self):
        landing = [_empty_hbm((N_CHIPS - 1,) + s.shape[1:], s.dtype) for s in self.sums]
        self.s, self.r, self.bufs, token = _split_start(
            self._copies, self.sums + landing, (self.nt, N_CHIPS - 1), self.sums[-1],
            name=f"comm_rs_chip_start_{self.tag}")
        return token

    def finish(self, after):
        bufs = _split_wait(self._copies, self.s, self.r, self.bufs, after, self.nt,
                           name=f"comm_rs_chip_wait_{self.tag}")
        return bufs[:self.nt], bufs[self.nt:]


def _adam_math(g, w, m, v):
    m2 = ADAM_B1 * m + (1.0 - ADAM_B1) * g
    v2 = ADAM_B2 * v + (1.0 - ADAM_B2) * (g * g)
    m_hat = m2 / (1.0 - ADAM_B1 ** ADAM_STEP)
    v_hat = v2 / (1.0 - ADAM_B2 ** ADAM_STEP)
    delta = -ADAM_LR * (m_hat / (jnp.sqrt(v_hat) + ADAM_EPS) + ADAM_WD * w)
    return delta, m2, v2


def _adamw_sharded(chip_sums, remote, chip, w, m, v, layer, prev, *, name):
    nl, r, c = w.shape
    tr = _rows(r, c)

    def body(*refs):
        p_ref, r0_ref, r1_ref, r2_ref, w_ref, m_ref, v_ref = refs[1:8]
        g_out, d_out, m_out, v_out = refs[-4:]
        g = ((p_ref[...].astype(F32) + r0_ref[...].astype(F32)) + r1_ref[...].astype(F32)) + r2_ref[...].astype(F32)
        g_out[...] = g
        d_out[...], m_out[...], v_out[...] = _adam_math(g, w_ref[...], m_ref[...], v_ref[...])

    pspec = pl.BlockSpec((None, tr, c), lambda i, chip_ref: (chip_ref[0], i, 0))

    def rspec(k):
        return pl.BlockSpec((None, tr, c), lambda i, chip_ref: (k, i, 0))

    wspec = pl.BlockSpec((None, tr, c), lambda i, chip_ref: (layer, i, 0))
    in_specs = [pspec, rspec(0), rspec(1), rspec(2), wspec, wspec, wspec]
    args = [chip, chip_sums, remote, remote, remote, w, m, v]
    aliases = {}
    if prev is not None:
        in_specs += [pl.BlockSpec(memory_space=pl.ANY)] * 4
        aliases = {len(args) + i: i for i in range(4)}
        args += list(prev)
    grid_spec = pltpu.PrefetchScalarGridSpec(num_scalar_prefetch=1, grid=(r // tr,), in_specs=in_specs,
                                             out_specs=(wspec,) * 4)
    shp = jax.ShapeDtypeStruct(w.shape, F32)
    return _pcall(body, name=name, out_shape=(shp,) * 4, grid_spec=grid_spec, aliases=aliases)(*args)


def _adamw_local(g, w, m, v, *, name):
    nl, r, c = w.shape
    tr = _rows(r, c)

    def body(g_ref, w_ref, m_ref, v_ref, d_out, m_out, v_out):
        d_out[...], m_out[...], v_out[...] = _adam_math(g_ref[...], w_ref[...], m_ref[...], v_ref[...])

    spec = pl.BlockSpec((None, tr, c), lambda l, i: (l, i, 0))
    shp = jax.ShapeDtypeStruct(w.shape, F32)
    return _pcall(body, name=name, out_shape=(shp,) * 3, grid=(nl, r // tr), in_specs=[spec] * 4,
                  out_specs=(spec,) * 3)(g, w, m, v)


def _adamw_replicated(parts, w, m, v, *, name):
    n = w.shape[1]

    def body(p_ref, w_ref, m_ref, v_ref, g_out, d_out, m_out, v_out):
        g = p_ref[0]
        for k in range(1, N_DEV):
            g = g + p_ref[k]
        g_out[...] = g
        d_out[...], m_out[...], v_out[...] = _adam_math(g, w_ref[...], m_ref[...], v_ref[...])

    vm = pl.BlockSpec(memory_space=pltpu.VMEM)
    shp = jax.ShapeDtypeStruct((1, n), F32)
    return _pcall(body, name=name, out_shape=(shp,) * 4, in_specs=[vm] * 4, out_specs=(vm,) * 4)(parts, w, m, v)


def _group_views(qk, proj, g, dil, seq):
    if dil == 1:
        return (qk, qk, proj), (0, A_HEADS, 2 * A_HEADS)
    length = seq // dil
    lo = g * GROUP_W
    q = qk[:, lo:lo + GROUP_W].reshape(length, dil * GROUP_W)
    k = qk[:, A_W + lo:A_W + lo + GROUP_W].reshape(length, dil * GROUP_W)
    v = proj[:, OFF_VA + lo:OFF_VA + lo + GROUP_W].astype(BF16).reshape(length, dil * GROUP_W)
    return (q, k, v), (0, 0, 0)


def _mod_rows(mod, d):
    return [mod[:, i * d:(i + 1) * d] for i in range(6)]


MIXER_W = ("w_in", "w_branch_a", "w_branch_b", "w_out")
FFN_W = ("w_gate_up", "w_down")
SHARD_AXIS = {"w_in": 1, "w_branch_a": 1, "w_branch_b": 1, "w_out": 0, "w_gate_up": 1, "w_down": 0}


def _mixer_fwd_a(h, mod, g1, gains, w_in, cos2, sin2, deps):
    seq, d = h.shape
    sh1, sc1 = _mod_rows(mod, d)[:2]
    u = _rmsmod_fwd(h, g1, sc1, sh1, name="rmsmod_fwd", deps=deps)
    proj = _mm(u, w_in, name="mm_in")
    qk = _qkrope_fwd(proj, gains, cos2, sin2, name="qkrope_fwd")
    os_, lses = [], []
    for g, dil in enumerate(DILATIONS):
        arrs, offs = _group_views(qk, proj, g, dil, seq)
        o, lse = _dil_fwd(*arrs, offs, seq // dil, dil, name=f"dil_fwd_{dil}")
        os_.append(o.reshape(seq, GROUP_W))
        lses.append(lse.reshape(seq, GROUP_W))
    o_a = _combine_fwd(os_, lses, name="combine_fwd")
    o_b = _sb_fwd(proj, name="sb_fwd")
    return dict(h_in=h, u=u, proj=proj, qk=qk, os=os_, lses=lses, o_a=o_a, o_b=o_b)


def _mixer_fwd_b(sv, mod, wts, deps):
    d = sv["h_in"].shape[1]
    ga1 = _mod_rows(mod, d)[2]
    y_a = _mm(sv["o_a"], wts["w_branch_a"], name="mm_branch", deps=deps)
    y_b = _mm(sv["o_b"], wts["w_branch_b"], name="mm_branch")
    merged = _merge_fwd(sv["proj"], y_a, y_b, name="merge_fwd")
    t = _mm(merged, wts["w_out"], name="mm_out")
    h_mid = _resid_gate(sv["h_in"], ga1, t, name="resid_gate")
    sv.update(y_a=y_a, y_b=y_b, merged=merged, t=t, h_mid=h_mid)
    return h_mid


def _ffn_fwd_a(sv, mod, g2, w_gate_up, deps):
    d = sv["h_mid"].shape[1]
    sh2, sc2 = _mod_rows(mod, d)[3:5]
    u2 = _rmsmod_fwd(sv["h_mid"], g2, sc2, sh2, name="rmsmod_fwd", deps=deps)
    gu = _mm(u2, w_gate_up, name="mm_gate_up")
    a = _swiglu_fwd(gu, name="swiglu_fwd")
    sv.update(u2=u2, gu=gu, a=a)
    return a


def _ffn_fwd_b(sv, mod, w_down, deps):
    d = sv["h_mid"].shape[1]
    ga2 = _mod_rows(mod, d)[5]
    f = _mm(sv["a"], w_down, name="mm_down", deps=deps)
    sv["f"] = f
    return _resid_gate(sv["h_mid"], ga2, f, name="resid_gate")


def _wgrad(act, dout, key):
    return _mm(act, dout, ta=True, out_dtype=BF16, name="mm_wgrad_" + key)


def _ffn_bwd(dh, sv, mod, g2, wts, deps):
    d = dh.shape[1]
    sc2, ga2 = _mod_rows(mod, d)[4:6]
    df, dgate2 = _resid_gate_bwd(dh, sv["f"], ga2, name="resid_gate_bwd", deps=deps)
    da = _mm(df, wts["w_down"], tb=True, name="mm_down_t")
    grads = {"w_down": _wgrad(sv["a"], df, "w_down")}
    dgu = _swiglu_bwd(sv["gu"], da, name="swiglu_bwd")
    du2 = _mm(dgu, wts["w_gate_up"], tb=True, name="mm_gate_up_t")
    grads["w_gate_up"] = _wgrad(sv["u2"], dgu, "w_gate_up")
    dh_mid, dsh2, dsc2, dg2 = _rmsmod_bwd(du2, sv["h_mid"], g2, sc2, dh, name="rmsmod_bwd")
    return dh_mid, [dsh2, dsc2, dgate2], dg2, grads


def _mixer_bwd(dh_mid, sv, mod, g1, gains, wts, cos2, sin2, deps):
    seq, d = dh_mid.shape
    sc1, ga1 = _mod_rows(mod, d)[1:3]
    dt, dgate1 = _resid_gate_bwd(dh_mid, sv["t"], ga1, name="resid_gate_bwd", deps=deps)
    dmerged = _mm(dt, wts["w_out"], tb=True, name="mm_out_t")
    grads = {"w_out": _wgrad(sv["merged"], dt, "w_out")}
    dy_a, dy_b, dga, dgb = _merge_bwd(dmerged, sv["proj"], sv["y_a"], sv["y_b"], name="merge_bwd")
    do_a = _mm(dy_a, wts["w_branch_a"], tb=True, name="mm_branch_t")
    do_b = _mm(dy_b, wts["w_branch_b"], tb=True, name="mm_branch_t")
    grads["w_branch_a"] = _wgrad(sv["o_a"], dy_a, "w_branch_a")
    grads["w_branch_b"] = _wgrad(sv["o_b"], dy_b, "w_branch_b")
    dqb, dkb, dvb = _sb_bwd(sv["proj"], do_b, name="sb_bwd")
    comb = _combine_bwd(do_a, sv["os"], sv["lses"], name="combine_bwd")
    dos, dls = comb[:3], comb[3:]
    dqs, dks, dvs = [], [], []
    for g, dil in enumerate(DILATIONS):
        length = seq // dil
        arrs, offs = _group_views(sv["qk"], sv["proj"], g, dil, seq)
        view = (length, dil * GROUP_W)
        dq, dk, dv = _dil_bwd(*arrs, offs, sv["os"][g].reshape(view), sv["lses"][g].reshape(view),
                              dos[g].reshape(view), dls[g].reshape(view), length, dil, name=f"dil_bwd_{dil}")
        dqs.append(dq.reshape(seq, GROUP_W))
        dks.append(dk.reshape(seq, GROUP_W))
        dvs.append(dv.reshape(seq, GROUP_W))
    dqk, dgains = _qkrope_bwd(jnp.concatenate(dqs + dks, axis=1), sv["proj"], gains, cos2, sin2,
                              name="qkrope_bwd")
    dproj = jnp.concatenate(
        [dqk] + [t_.astype(BF16) for t_ in dvs + [dqb, dkb, dvb]] + [dga, dgb], axis=1)
    du = _mm(dproj, wts["w_in"], tb=True, name="mm_in_t")
    grads["w_in"] = _wgrad(sv["u"], dproj, "w_in")
    dh_in, dsh1, dsc1, dg1 = _rmsmod_bwd(du, sv["h_in"], g1, sc1, dh_mid, name="rmsmod_bwd")
    return dh_in, [dsh1, dsc1, dgate1], dg1, dgains, grads


def kernel(x, c, w_ada, b_ada, norm1_g, norm2_g, w_in, qn_g, kn_g, w_branch_a, w_branch_b, w_out, w_gate_up, w_down, loss_target, m_w_ada, m_b_ada, m_norm1_g, m_norm2_g, m_w_in, m_qn_g, m_kn_g, m_w_branch_a, m_w_branch_b, m_w_out, m_w_gate_up, m_w_down, v_w_ada, v_b_ada, v_norm1_g, v_norm2_g, v_w_in, v_qn_g, v_kn_g, v_w_branch_a, v_w_branch_b, v_w_out, v_w_gate_up, v_w_down):
    seq, d = x.shape[1], x.shape[2]
    depth = w_in.shape[0]
    weights = dict(w_in=w_in, w_branch_a=w_branch_a, w_branch_b=w_branch_b, w_out=w_out, w_gate_up=w_gate_up,
                   w_down=w_down)
    moments_m = dict(w_in=m_w_in, w_branch_a=m_w_branch_a, w_branch_b=m_w_branch_b, w_out=m_w_out,
                     w_gate_up=m_w_gate_up, w_down=m_w_down)
    moments_v = dict(w_in=v_w_in, w_branch_a=v_w_branch_a, w_branch_b=v_w_branch_b, w_out=v_w_out,
                     w_gate_up=v_w_gate_up, w_down=v_w_down)
    xi, yi, ci = _coords()
    me = 4 * xi + 2 * yi + ci
    core = jnp.reshape(ci, (1,)).astype(jnp.int32)
    chip = jnp.reshape(2 * xi + yi, (1,)).astype(jnp.int32)

    ada_w = w_ada.shape[2]
    c_act = _small_allgather(c, name="comm_gather_c", silu=True).reshape(N_DEV, d)
    c_pad = jnp.concatenate([c_act, jnp.zeros_like(c_act)], axis=0).astype(BF16)
    bias = lax.dynamic_slice(b_ada, (0, me * ada_w), (depth, ada_w))
    mod_part = jnp.stack([_mm(c_pad, w_ada[l], name="mm_ada")[:N_DEV] for l in range(depth)]) + bias[:, None, :]
    mod_all = _small_allgather(mod_part.reshape(1, depth * N_DEV * ada_w), name="comm_gather_mod")
    mod_all = mod_all.reshape(N_DEV, depth, N_DEV, ada_w)
    mod_mine = lax.dynamic_index_in_dim(mod_all, me, axis=2, keepdims=False)
    mods = jnp.transpose(mod_mine, (1, 0, 2)).reshape(depth, 1, 6 * d)

    cos2, sin2 = _rope_tables(seq)
    gains = [jnp.stack([qn_g[l], kn_g[l]])[:, None, :] for l in range(depth)]
    g1s = [norm1_g[l][None] for l in range(depth)]
    g2s = [norm2_g[l][None] for l in range(depth)]

    def gather_of(keys, l, tag):
        return _SplitGather([weights[k][l].astype(BF16) for k in keys], [SHARD_AXIS[k] for k in keys], f"{tag}{l}")

    wm = dict(zip(MIXER_W, _gather_weights([weights[k][0].astype(BF16) for k in MIXER_W],
                                           [SHARD_AXIS[k] for k in MIXER_W], name="comm_gather_weights")))
    gf = gather_of(FFN_W, 0, "ffn")
    deps = [gf.first(after=wm["w_in"])]
    h = x[0]
    saved, full = [], []
    for l in range(depth):
        last = l + 1 == depth
        sv = _mixer_fwd_a(h, mods[l], g1s[l], gains[l], wm["w_in"], cos2, sin2, deps)
        deps = [gf.forward(after=sv["o_b"])]
        if not last:
            gm_next = gather_of(MIXER_W, l + 1, "mixer")
            deps.append(gm_next.first(after=sv["o_b"]))
        h_mid = _mixer_fwd_b(sv, mods[l], wm, deps)
        wf = dict(zip(FFN_W, gf.finish(after=h_mid)))
        a = _ffn_fwd_a(sv, mods[l], g2s[l], wf["w_gate_up"], [])
        deps = []
        if not last:
            deps.append(gm_next.forward(after=a))
            gf = gather_of(FFN_W, l + 1, "ffn")
            deps.append(gf.first(after=a))
        h = _ffn_fwd_b(sv, mods[l], wf["w_down"], deps)
        saved.append(sv)
        full.append({**wm, **wf})
        if not last:
            wm = dict(zip(MIXER_W, gm_next.finish(after=h)))
            deps = []
    loss_part, dh = _loss_fwd(h, loss_target[0], name="loss")
    loss = lax.psum(loss_part[0, 0], ("x", "y", "c"))

    done, pending = [], None

    def reduce_start(keys, grads, l):
        axes = [SHARD_AXIS[k] for k in keys]
        tag = f"{keys[0]}{l}"
        sib = _rs_pair_exchange([grads[k] for k in keys], axes, name="comm_rs_pair_" + tag)
        sums = [_pair_sum(grads[k], s_, core, ax, name="pair_sum_" + k) for k, s_, ax in zip(keys, sib, axes)]
        ex = _SplitChipExchange(sums, tag)
        return (keys, l, ex), [ex.start()]

    def reduce_finish(item, after):
        keys, l, ex = item
        sums, remote = ex.finish(after)
        done.append((keys, l, sums, remote))

    dmods, dg1s, dg2s, dgains = [None] * depth, [None] * depth, [None] * depth, [None] * depth
    deps = []
    for l in reversed(range(depth)):
        dh_mid, dmod_f, dg2s[l], grads = _ffn_bwd(dh, saved[l], mods[l], g2s[l], full[l], deps)
        if pending is not None:
            reduce_finish(pending, dh_mid)
        pending, deps = reduce_start(FFN_W, grads, l)
        dh, dmod_m, dg1s[l], dgains[l], grads = _mixer_bwd(dh_mid, saved[l], mods[l], g1s[l], gains[l], full[l],
                                                           cos2, sin2, deps)
        dmods[l] = jnp.concatenate(dmod_m + dmod_f, axis=1)
        reduce_finish(pending, dh)
        pending, deps = reduce_start(MIXER_W, grads, l)
    grad_x = dh[None]

    small = jnp.concatenate(
        dmods + dg1s + dg2s + [dgains[l][0] for l in range(depth)] + [dgains[l][1] for l in range(depth)], axis=1)
    small_all = _small_allgather(small, name="comm_gather_small", deps=deps)

    def pack(b, n1, n2, qn, kn):
        return jnp.concatenate([t_.reshape(1, -1) for t_ in (b, n1, n2, qn, kn)], axis=1)

    sg, sd, sm, sv_ = _adamw_replicated(small_all, pack(b_ada, norm1_g, norm2_g, qn_g, kn_g),
                                        pack(m_b_ada, m_norm1_g, m_norm2_g, m_qn_g, m_kn_g),
                                        pack(v_b_ada, v_norm1_g, v_norm2_g, v_qn_g, v_kn_g), name="adamw_replicated")

    def unpack(p):
        sizes = [depth * 6 * d, depth * d, depth * d, depth * HEAD_DIM, depth * HEAD_DIM]
        shapes = [b_ada.shape, norm1_g.shape, norm2_g.shape, qn_g.shape, kn_g.shape]
        out, off = [], 0
        for n, shp in zip(sizes, shapes):
            out.append(p[0, off:off + n].reshape(shp))
            off += n
        return dict(zip(("b_ada", "norm1_g", "norm2_g", "qn_g", "kn_g"), out))

    ug, ud, um, uv = unpack(sg), unpack(sd), unpack(sm), unpack(sv_)
    res = {k: dict(g=ug[k], d=ud[k], m=um[k], v=uv[k]) for k in ug}

    dmod_all = small_all[:, 0, :depth * 6 * d].reshape(N_DEV, depth, 6 * d)
    g_ada = None
    for l in range(depth):
        dm = lax.dynamic_slice(dmod_all[:, l, :], (0, me * ada_w), (N_DEV, ada_w))
        dm = jnp.concatenate([dm, jnp.zeros_like(dm)], axis=0).astype(BF16)
        g_ada = _mm(c_pad, dm, ta=True, name="mm_wgrad_ada", stack=(l, depth, g_ada))
    d_ada, m_ada, v_ada = _adamw_local(g_ada, w_ada, m_w_ada, v_w_ada, name="adamw_local")
    res["w_ada"] = dict(g=g_ada, d=d_ada, m=m_ada, v=v_ada)

    stacked = {}

    def update(keys, l, sums, remote):
        for k, p_, r_ in zip(keys, sums, remote):
            stacked[k] = _adamw_sharded(p_, r_, chip, weights[k], moments_m[k], moments_v[k], l, stacked.get(k),
                                        name="adamw_" + k)

    for item in done:
        update(*item)
    reduce_finish(pending, stacked[FFN_W[0]][0])
    update(*done[-1])
    for k, (g_, d_, m_, v_) in stacked.items():
        res[k] = dict(g=g_, d=d_, m=m_, v=v_)

    order = ("w_ada", "b_ada", "norm1_g", "norm2_g", "w_in", "qn_g", "kn_g", "w_branch_a", "w_branch_b", "w_out",
             "w_gate_up", "w_down")
    return (loss, grad_x, *[res[k]["g"] for k in order], *[res[k]["d"] for k in order],
            *[res[k]["m"] for k in order], *[res[k]["v"] for k in order])
```

```python
import functools

import jax
import jax.numpy as jnp
from jax import lax
from jax.experimental import pallas as pl
from jax.experimental.pallas import tpu as pltpu

F32 = jnp.float32
BF16 = jnp.bfloat16

HEAD_DIM = 128
BLOCK = 128
DILATIONS = (1, 4, 16)
HEADS_PER_GROUP = 4
A_HEADS = 12
SB_HEADS = 4
GROUP_W = HEADS_PER_GROUP * HEAD_DIM
A_W = A_HEADS * HEAD_DIM
B_W = SB_HEADS * HEAD_DIM
OFF_QA, OFF_KA, OFF_VA = 0, A_W, 2 * A_W
OFF_QB, OFF_KB, OFF_VB = 3 * A_W, 3 * A_W + B_W, 3 * A_W + 2 * B_W
OFF_GATES = 3 * A_W + 3 * B_W
ROPE_THETA = 10000.0
EPS = 1e-6
ATT_SCALE = HEAD_DIM ** -0.5
MASKED = -1e30

ADAM_LR, ADAM_B1, ADAM_B2, ADAM_EPS, ADAM_WD, ADAM_STEP = 0.001, 0.9, 0.999, 1e-08, 0.01, 10

N_DEV = 8
N_CHIPS = 4
V7X_VMEM_LIMIT_BYTES = 56 * 1024 * 1024
ELEMWISE_BLOCK_BYTES = 1024 * 1024
MESH = pl.DeviceIdType.MESH

NN = (((1,), (0,)), ((), ()))
NT = (((1,), (1,)), ((), ()))
TN = (((0,), (0,)), ((), ()))


def _dot(a, b, dims=NN):
    return lax.dot_general(a, b, dims, preferred_element_type=F32)


def _tile(n, cap, mult=128):
    best = None
    for t in range(mult, min(n, cap) + 1, mult):
        if n % t == 0:
            best = t
    if best is None:
        assert n <= 2 * cap, (n, cap)
        return n
    return best


def _rows(r, c):
    return _tile(r, max(16, ELEMWISE_BLOCK_BYTES // (4 * c)), 16)


def _pcall(body, *, name, out_shape, grid=None, in_specs=None, out_specs=None, scratch=(), aliases=None,
           grid_spec=None, deps=()):
    kwargs = {}
    deps = list(deps)
    if grid_spec is not None:
        assert not deps
        kwargs["grid_spec"] = grid_spec
    else:
        if grid is not None:
            kwargs["grid"] = grid
        n_in = len(in_specs)
        kwargs["in_specs"] = list(in_specs) + [pl.BlockSpec(memory_space=pl.ANY)] * len(deps)
        kwargs["out_specs"] = out_specs
        kwargs["scratch_shapes"] = list(scratch)
        if deps:
            inner = body

            def body(*refs):
                return inner(*refs[:n_in], *refs[n_in + len(deps):])

    call = pl.pallas_call(
        body, name=name, out_shape=out_shape, input_output_aliases=aliases or {},
        compiler_params=pltpu.CompilerParams(vmem_limit_bytes=V7X_VMEM_LIMIT_BYTES), **kwargs)
    return (lambda *args: call(*args, *deps)) if deps else call


def _mm(a, b, *, name, ta=False, tb=False, out_dtype=F32, caps=(1024, 1024, 1024), stack=None, deps=()):
    kdim, m = a.shape if ta else a.shape[::-1]
    n, k2 = b.shape if tb else b.shape[::-1]
    assert kdim == k2, (a.shape, b.shape, ta, tb)
    tm, tn, tk = _tile(m, caps[0]), _tile(n, caps[1]), _tile(kdim, caps[2])
    nk = kdim // tk
    dims = (((0 if ta else 1,), (1 if tb else 0,)), ((), ()))

    def body(*refs):
        a_ref, b_ref = refs[0], refs[1]
        part = _dot(a_ref[...].astype(BF16), b_ref[...].astype(BF16), dims)
        if nk == 1:
            o_ref = refs[-1]
            o_ref[...] = part.astype(o_ref.dtype)
            return
        o_ref, acc_ref = refs[-2], refs[-1]
        k = pl.program_id(2)

        @pl.when(k == 0)
        def _():
            acc_ref[...] = part

        @pl.when(k > 0)
        def _():
            acc_ref[...] += part

        @pl.when(k == nk - 1)
        def _():
            o_ref[...] = acc_ref[...].astype(o_ref.dtype)

    a_spec = (pl.BlockSpec((tk, tm), lambda i, j, k: (k, i)) if ta
              else pl.BlockSpec((tm, tk), lambda i, j, k: (i, k)))
    b_spec = (pl.BlockSpec((tn, tk), lambda i, j, k: (j, k)) if tb
              else pl.BlockSpec((tk, tn), lambda i, j, k: (k, j)))
    ins, in_specs, aliases = [a, b], [a_spec, b_spec], {}
    if stack is None:
        out_shape = jax.ShapeDtypeStruct((m, n), out_dtype)
        out_spec = pl.BlockSpec((tm, tn), lambda i, j, k: (i, j))
    else:
        layer, n_layers, buf = stack
        out_shape = jax.ShapeDtypeStruct((n_layers, m, n), out_dtype)
        out_spec = pl.BlockSpec((None, tm, tn), lambda i, j, k: (layer, i, j))
        if buf is not None:
            ins.append(buf)
            in_specs.append(pl.BlockSpec(memory_space=pl.ANY))
            aliases = {2: 0}
    scratch = [] if nk == 1 else [pltpu.VMEM((tm, tn), F32)]
    return _pcall(body, name=name, out_shape=out_shape, grid=(m // tm, n // tn, nk), in_specs=in_specs,
                  out_specs=out_spec, scratch=scratch, aliases=aliases, deps=deps)(*ins)


def _rmsmod_fwd(h, g, scale, shift, *, name, deps=()):
    s, d = h.shape
    ts = _rows(s, d)

    def body(h_ref, g_ref, sc_ref, sh_ref, u_ref):
        hf = h_ref[...]
        r = lax.rsqrt(jnp.mean(hf * hf, axis=-1, keepdims=True) + EPS)
        u_ref[...] = (((hf * r) * g_ref[...]) * (1.0 + sc_ref[...]) + sh_ref[...]).astype(BF16)

    row = pl.BlockSpec((ts, d), lambda i: (i, 0))
    vec = pl.BlockSpec((1, d), lambda i: (0, 0))
    return _pcall(body, name=name, out_shape=jax.ShapeDtypeStruct((s, d), BF16), grid=(s // ts,),
                  in_specs=[row, vec, vec, vec], out_specs=row, deps=deps)(h, g, scale, shift)


def _rmsmod_bwd(du, h, g, scale, dres, *, name):
    s, d = h.shape
    ts = _rows(s, d)

    def body(du_ref, h_ref, g_ref, sc_ref, dres_ref, dh_ref, dsh_ref, dsc_ref, dg_ref):
        @pl.when(pl.program_id(0) == 0)
        def _():
            dsh_ref[...] = jnp.zeros_like(dsh_ref)
            dsc_ref[...] = jnp.zeros_like(dsc_ref)
            dg_ref[...] = jnp.zeros_like(dg_ref)

        hf, duf, gain = h_ref[...], du_ref[...], g_ref[...]
        r = lax.rsqrt(jnp.mean(hf * hf, axis=-1, keepdims=True) + EPS)
        xh = hf * r
        dn = duf * (1.0 + sc_ref[...])
        dsh_ref[...] += jnp.sum(duf, axis=0, keepdims=True)
        dsc_ref[...] += jnp.sum(duf * (xh * gain), axis=0, keepdims=True)
        dg_ref[...] += jnp.sum(dn * xh, axis=0, keepdims=True)
        dxh = dn * gain
        dh_ref[...] = dres_ref[...] + r * (dxh - xh * jnp.mean(dxh * xh, axis=-1, keepdims=True))

    row = pl.BlockSpec((ts, d), lambda i: (i, 0))
    vec = pl.BlockSpec((1, d), lambda i: (0, 0))
    vshape = jax.ShapeDtypeStruct((1, d), F32)
    return _pcall(body, name=name, out_shape=(jax.ShapeDtypeStruct((s, d), F32), vshape, vshape, vshape),
                  grid=(s // ts,), in_specs=[row, row, vec, vec, row],
                  out_specs=(row, vec, vec, vec))(du, h, g, scale, dres)


def _resid_gate(h, gate, t, *, name):
    s, d = h.shape
    ts = _rows(s, d)

    def body(h_ref, g_ref, t_ref, o_ref):
        o_ref[...] = h_ref[...] + g_ref[...] * t_ref[...]

    row = pl.BlockSpec((ts, d), lambda i: (i, 0))
    vec = pl.BlockSpec((1, d), lambda i: (0, 0))
    return _pcall(body, name=name, out_shape=jax.ShapeDtypeStruct((s, d), F32), grid=(s // ts,),
                  in_specs=[row, vec, row], out_specs=row)(h, gate, t)


def _resid_gate_bwd(dh, t, gate, *, name, deps=()):
    s, d = dh.shape
    ts = _rows(s, d)

    def body(dh_ref, t_ref, g_ref, dt_ref, dg_ref):
        @pl.when(pl.program_id(0) == 0)
        def _():
            dg_ref[...] = jnp.zeros_like(dg_ref)

        dhf = dh_ref[...]
        dt_ref[...] = (dhf * g_ref[...]).astype(BF16)
        dg_ref[...] += jnp.sum(dhf * t_ref[...], axis=0, keepdims=True)

    row = pl.BlockSpec((ts, d), lambda i: (i, 0))
    vec = pl.BlockSpec((1, d), lambda i: (0, 0))
    return _pcall(body, name=name,
                  out_shape=(jax.ShapeDtypeStruct((s, d), BF16), jax.ShapeDtypeStruct((1, d), F32)),
                  grid=(s // ts,), in_specs=[row, row, vec], out_specs=(row, vec), deps=deps)(dh, t, gate)


def _merge_fwd(proj, y_a, y_b, *, name):
    s, d = y_a.shape
    ts = _rows(s, d)
    ga_blk = OFF_GATES // d

    def body(ga_ref, gb_ref, ya_ref, yb_ref, o_ref):
        o_ref[...] = (jax.nn.sigmoid(ga_ref[...]) * ya_ref[...]
                      + jax.nn.sigmoid(gb_ref[...]) * yb_ref[...]).astype(BF16)

    row = pl.BlockSpec((ts, d), lambda i: (i, 0))
    ga = pl.BlockSpec((ts, d), lambda i: (i, ga_blk))
    gb = pl.BlockSpec((ts, d), lambda i: (i, ga_blk + 1))
    return _pcall(body, name=name, out_shape=jax.ShapeDtypeStruct((s, d), BF16), grid=(s // ts,),
                  in_specs=[ga, gb, row, row], out_specs=row)(proj, proj, y_a, y_b)


def _merge_bwd(dm, proj, y_a, y_b, *, name):
    s, d = y_a.shape
    ts = _rows(s, d)
    ga_blk = OFF_GATES // d

    def body(dm_ref, ga_ref, gb_ref, ya_ref, yb_ref, dya_ref, dyb_ref, dga_ref, dgb_ref):
        dmf = dm_ref[...]
        sa, sb = jax.nn.sigmoid(ga_ref[...]), jax.nn.sigmoid(gb_ref[...])
        dya_ref[...] = (dmf * sa).astype(BF16)
        dyb_ref[...] = (dmf * sb).astype(BF16)
        dga_ref[...] = (dmf * ya_ref[...] * (sa * (1.0 - sa))).astype(BF16)
        dgb_ref[...] = (dmf * yb_ref[...] * (sb * (1.0 - sb))).astype(BF16)

    row = pl.BlockSpec((ts, d), lambda i: (i, 0))
    ga = pl.BlockSpec((ts, d), lambda i: (i, ga_blk))
    gb = pl.BlockSpec((ts, d), lambda i: (i, ga_blk + 1))
    shp = jax.ShapeDtypeStruct((s, d), BF16)
    return _pcall(body, name=name, out_shape=(shp, shp, shp, shp), grid=(s // ts,),
                  in_specs=[row, ga, gb, row, row], out_specs=(row, row, row, row))(dm, proj, proj, y_a, y_b)


def _swiglu_fwd(gu, *, name):
    s, f2 = gu.shape
    f = f2 // 2
    ts = _rows(s, f)

    def body(g_ref, u_ref, a_ref):
        gf = g_ref[...]
        a_ref[...] = ((gf * jax.nn.sigmoid(gf)) * u_ref[...]).astype(BF16)

    return _pcall(body, name=name, out_shape=jax.ShapeDtypeStruct((s, f), BF16), grid=(s // ts,),
                  in_specs=[pl.BlockSpec((ts, f), lambda i: (i, 0)), pl.BlockSpec((ts, f), lambda i: (i, 1))],
                  out_specs=pl.BlockSpec((ts, f), lambda i: (i, 0)))(gu, gu)


def _swiglu_bwd(gu, da, *, name):
    s, f2 = gu.shape
    f = f2 // 2
    ts = _rows(s, f)

    def body(g_ref, u_ref, da_ref, o_ref):
        gf, daf = g_ref[...], da_ref[...]
        sg = jax.nn.sigmoid(gf)
        o_ref[:, :f] = (daf * u_ref[...] * (sg * (1.0 + gf * (1.0 - sg)))).astype(BF16)
        o_ref[:, f:] = (daf * (gf * sg)).astype(BF16)

    return _pcall(body, name=name, out_shape=jax.ShapeDtypeStruct((s, f2), BF16), grid=(s // ts,),
                  in_specs=[pl.BlockSpec((ts, f), lambda i: (i, 0)), pl.BlockSpec((ts, f), lambda i: (i, 1)),
                            pl.BlockSpec((ts, f), lambda i: (i, 0))],
                  out_specs=pl.BlockSpec((ts, f2), lambda i: (i, 0)))(gu, gu, da)


def _loss_fwd(y, tgt, *, name):
    s, d = y.shape
    ts = _rows(s, d)

    def body(y_ref, t_ref, l_ref, dy_ref):
        @pl.when(pl.program_id(0) == 0)
        def _():
            l_ref[...] = jnp.zeros_like(l_ref)

        e = y_ref[...] - t_ref[...]
        dy_ref[...] = e * (1.0 / d)
        per_tok = jnp.sum(e * e, axis=1, keepdims=True) * (1.0 / d)
        l_ref[...] += 0.5 * jnp.sum(per_tok, axis=0, keepdims=True)

    row = pl.BlockSpec((ts, d), lambda i: (i, 0))
    return _pcall(body, name=name,
                  out_shape=(jax.ShapeDtypeStruct((1, 128), F32), jax.ShapeDtypeStruct((s, d), F32)),
                  grid=(s // ts,), in_specs=[row, row],
                  out_specs=(pl.BlockSpec((1, 128), lambda i: (0, 0)), row))(y, tgt)


def _rope_tables(seq):
    inv = jnp.power(ROPE_THETA, -jnp.arange(0, HEAD_DIM, 2, dtype=F32) / HEAD_DIM)
    ang = jnp.arange(seq, dtype=F32)[:, None] * inv[None, :]
    cos, sin = jnp.cos(ang), jnp.sin(ang)
    return jnp.concatenate([cos, cos], axis=1), jnp.concatenate([-sin, sin], axis=1)


def _qkrope_fwd(proj, gains, cos2, sin2, *, name):
    s = proj.shape[0]
    ts = _rows(s, A_W)

    def body(x_ref, g_ref, c_ref, s_ref, o_ref):
        gain, cos, sin = g_ref[...], c_ref[...], s_ref[...]
        for h in range(A_HEADS):
            lanes = slice(h * HEAD_DIM, (h + 1) * HEAD_DIM)
            x = x_ref[:, lanes]
            y = (x * lax.rsqrt(jnp.mean(x * x, axis=-1, keepdims=True) + EPS)) * gain
            o_ref[:, lanes] = (y * cos + pltpu.roll(y, HEAD_DIM // 2, 1) * sin).astype(BF16)

    heads = pl.BlockSpec((ts, A_W), lambda i, j: (i, j))
    tab = pl.BlockSpec((ts, HEAD_DIM), lambda i, j: (i, 0))
    gain = pl.BlockSpec((None, 1, HEAD_DIM), lambda i, j: (j, 0, 0))
    return _pcall(body, name=name, out_shape=jax.ShapeDtypeStruct((s, 2 * A_W), BF16),
                  grid=(s // ts, 2), in_specs=[heads, gain, tab, tab], out_specs=heads)(
                      proj, gains, cos2, sin2)


def _qkrope_bwd(dqk, proj, gains, cos2, sin2, *, name):
    s = proj.shape[0]
    ts = _rows(s, A_W)

    def body(d_ref, x_ref, g_ref, c_ref, s_ref, dx_ref, dg_ref):
        @pl.when(pl.program_id(1) == 0)
        def _():
            dg_ref[...] = jnp.zeros_like(dg_ref)

        gain, cos, sin = g_ref[...], c_ref[...], s_ref[...]
        dg = jnp.zeros((1, HEAD_DIM), F32)
        for h in range(A_HEADS):
            lanes = slice(h * HEAD_DIM, (h + 1) * HEAD_DIM)
            dout = d_ref[:, lanes]
            dy = dout * cos + pltpu.roll(dout * sin, HEAD_DIM // 2, 1)
            x = x_ref[:, lanes]
            r = lax.rsqrt(jnp.mean(x * x, axis=-1, keepdims=True) + EPS)
            xh = x * r
            dg = dg + jnp.sum(dy * xh, axis=0, keepdims=True)
            dxh = dy * gain
            dx_ref[:, lanes] = (r * (dxh - xh * jnp.mean(dxh * xh, axis=-1, keepdims=True))).astype(BF16)
        dg_ref[...] += dg

    heads = pl.BlockSpec((ts, A_W), lambda j, i: (i, j))
    tab = pl.BlockSpec((ts, HEAD_DIM), lambda j, i: (i, 0))
    gain = pl.BlockSpec((None, 1, HEAD_DIM), lambda j, i: (j, 0, 0))
    return _pcall(body, name=name,
                  out_shape=(jax.ShapeDtypeStruct((s, 2 * A_W), BF16), jax.ShapeDtypeStruct((2, 1, HEAD_DIM), F32)),
                  grid=(2, s // ts), in_specs=[heads, heads, gain, tab, tab],
                  out_specs=(heads, gain))(dqk, proj, gains, cos2, sin2)


def _block_rows(blk):
    if isinstance(blk, int):
        return pl.ds(blk * BLOCK, BLOCK)
    return pl.ds(pl.multiple_of(blk * BLOCK, BLOCK), BLOCK)


def _band_masks(n, with_prev):
    row = lax.broadcasted_iota(jnp.int32, (BLOCK, BLOCK), 0)
    col = lax.broadcasted_iota(jnp.int32, (BLOCK, BLOCK), 1)
    cur = col <= row
    if not with_prev:
        return [(n, cur)]
    prev = col >= row + jnp.where(n >= 1, 0, BLOCK)
    return [(n, cur), (jnp.maximum(n - 1, 0), prev)]


def _dil_fwd(q_arr, k_arr, v_arr, offs, length, dil, *, name):
    nj, nb = dil * HEADS_PER_GROUP, length // BLOCK
    ju = HEADS_PER_GROUP
    qo, ko, vo = (off // ju for off in offs)
    assert all(off % ju == 0 for off in offs)

    def body(q_ref, k_ref, v_ref, o_ref, l_ref):
        n = pl.program_id(1)
        masks = _band_masks(n, nb > 1)
        for cb in range(ju):
            lanes = slice(cb * HEAD_DIM, (cb + 1) * HEAD_DIM)
            q = q_ref[:, lanes].astype(BF16)
            parts = []
            for blk, mask in masks:
                rows = _block_rows(blk)
                sc = _dot(q, k_ref[rows, lanes].astype(BF16), NT) * ATT_SCALE
                parts.append((jnp.where(mask, sc, MASKED), rows))
            m = parts[0][0].max(axis=-1, keepdims=True)
            for sc, _ in parts[1:]:
                m = jnp.maximum(m, sc.max(axis=-1, keepdims=True))
            den = jnp.zeros((BLOCK, 1), F32)
            acc = jnp.zeros((BLOCK, HEAD_DIM), F32)
            for sc, rows in parts:
                p = jnp.exp(sc - m)
                den = den + jnp.sum(p, axis=-1, keepdims=True)
                acc = acc + _dot(p.astype(BF16), v_ref[rows, lanes].astype(BF16))
            o_ref[:, lanes] = acc / den
            l_ref[:, lanes] = jnp.broadcast_to(m + jnp.log(den), (BLOCK, HEAD_DIM))

    qspec = pl.BlockSpec((BLOCK, ju * HEAD_DIM), lambda j, n: (n, qo + j))
    kspec = pl.BlockSpec((length, ju * HEAD_DIM), lambda j, n: (0, ko + j))
    vspec = pl.BlockSpec((length, ju * HEAD_DIM), lambda j, n: (0, vo + j))
    ospec = pl.BlockSpec((BLOCK, ju * HEAD_DIM), lambda j, n: (n, j))
    shp = jax.ShapeDtypeStruct((length, nj * HEAD_DIM), F32)
    return _pcall(body, name=name, out_shape=(shp, shp), grid=(nj // ju, nb), in_specs=[qspec, kspec, vspec],
                  out_specs=(ospec, ospec))(q_arr, k_arr, v_arr)


def _dil_bwd(q_arr, k_arr, v_arr, offs, o, lse, do, dlse, length, dil, *, name):
    nj, nb = dil * HEADS_PER_GROUP, length // BLOCK
    ju = HEADS_PER_GROUP if length <= 4 * BLOCK else 2
    qo, ko, vo = (off // ju for off in offs)
    assert all(off % ju == 0 for off in offs)

    def body(q_ref, k_ref, v_ref, o_ref, l_ref, do_ref, dl_ref, dq_ref, dk_ref, dv_ref):
        dk_ref[...] = jnp.zeros_like(dk_ref)
        dv_ref[...] = jnp.zeros_like(dv_ref)

        def step(n, carry):
            qrows = _block_rows(n)
            masks = _band_masks(n, nb > 1)
            for cb in range(ju):
                lanes = slice(cb * HEAD_DIM, (cb + 1) * HEAD_DIM)
                q = q_ref[qrows, lanes].astype(BF16)
                dof = do_ref[qrows, lanes]
                dob = dof.astype(BF16)
                lse_b = l_ref[qrows, lanes]
                shift = dl_ref[qrows, lanes] - jnp.sum(dof * o_ref[qrows, lanes], axis=-1, keepdims=True)
                dq = jnp.zeros((BLOCK, HEAD_DIM), F32)
                for blk, mask in masks:
                    rows = _block_rows(blk)
                    kk, vv = k_ref[rows, lanes].astype(BF16), v_ref[rows, lanes].astype(BF16)
                    sc = _dot(q, kk, NT) * ATT_SCALE
                    p = jnp.where(mask, jnp.exp(sc - lse_b), 0.0)
                    ds = (p * (_dot(dob, vv, NT) + shift)).astype(BF16)
                    dq = dq + _dot(ds, kk)
                    dk_ref[rows, lanes] += _dot(ds, q, TN) * ATT_SCALE
                    dv_ref[rows, lanes] += _dot(p.astype(BF16), dob, TN)
                dq_ref[qrows, lanes] = dq * ATT_SCALE
            return carry

        if nb == 1:
            step(0, 0)
        else:
            lax.fori_loop(0, nb, step, 0)

    def col(off):
        return pl.BlockSpec((length, ju * HEAD_DIM), lambda j: (0, off + j))

    shp = jax.ShapeDtypeStruct((length, nj * HEAD_DIM), F32)
    return _pcall(body, name=name, out_shape=(shp, shp, shp), grid=(nj // ju,),
                  in_specs=[col(qo), col(ko), col(vo), col(0), col(0), col(0), col(0)],
                  out_specs=(col(0), col(0), col(0)))(q_arr, k_arr, v_arr, o, lse, do, dlse)


def _combine_weights(l_refs):
    ls = [r[...] for r in l_refs]
    m = jnp.maximum(jnp.maximum(ls[0], ls[1]), ls[2])
    es = [jnp.exp(l - m) for l in ls]
    den = es[0] + es[1] + es[2]
    return [e / den for e in es]


def _combine_fwd(os_, lses, *, name):
    s = os_[0].shape[0]
    ts = _rows(s, GROUP_W)

    def body(o0, o1, o2, l0, l1, l2, out_ref):
        w = _combine_weights((l0, l1, l2))
        out_ref[...] = (w[0] * o0[...] + w[1] * o1[...] + w[2] * o2[...]).astype(BF16)

    row = pl.BlockSpec((ts, GROUP_W), lambda i: (i, 0))
    return _pcall(body, name=name, out_shape=jax.ShapeDtypeStruct((s, GROUP_W), BF16), grid=(s // ts,),
                  in_specs=[row] * 6, out_specs=row)(*os_, *lses)


def _combine_bwd(do_a, os_, lses, *, name):
    s = do_a.shape[0]
    ts = _rows(s, GROUP_W)

    def body(d_ref, o0, o1, o2, l0, l1, l2, do0, do1, do2, dl0, dl1, dl2):
        w = _combine_weights((l0, l1, l2))
        d = d_ref[...]
        og = [o0[...], o1[...], o2[...]]
        oa = w[0] * og[0] + w[1] * og[1] + w[2] * og[2]
        ta = jnp.sum(d * oa, axis=-1, keepdims=True)
        for g, (do_ref, dl_ref) in enumerate(((do0, dl0), (do1, dl1), (do2, dl2))):
            do_ref[...] = w[g] * d
            dl_ref[...] = w[g] * (jnp.sum(d * og[g], axis=-1, keepdims=True) - ta)

    head = pl.BlockSpec((ts, HEAD_DIM), lambda i, h: (i, h))
    shp = jax.ShapeDtypeStruct((s, GROUP_W), F32)
    return _pcall(body, name=name, out_shape=(shp,) * 6, grid=(s // ts, HEADS_PER_GROUP),
                  in_specs=[head] * 7, out_specs=(head,) * 6)(do_a, *os_, *lses)


def _dot_exact(x, ones_mask):
    hi = x.astype(BF16)
    r1 = x - hi.astype(F32)
    mid = r1.astype(BF16)
    lo = (r1 - mid.astype(F32)).astype(BF16)
    return _dot(hi, ones_mask) + _dot(mid, ones_mask) + _dot(lo, ones_mask)


SB_QROWS = 2 * BLOCK
SB_UNROLL = 4


def _sb_mask(j, i):
    row = lax.broadcasted_iota(jnp.int32, (SB_QROWS, BLOCK), 0)
    col = lax.broadcasted_iota(jnp.int32, (SB_QROWS, BLOCK), 1)
    return col + (j * BLOCK - i * SB_QROWS) < row


def _sb_steps(i):
    return ((i + 1) * (SB_QROWS // BLOCK) + SB_UNROLL - 1) // SB_UNROLL


def _sb_scores(q, kk, j, i):
    mask = _sb_mask(j, i)
    z = _dot(q, kk, NT) * ATT_SCALE
    sp = jnp.log1p(jnp.exp(-jnp.abs(z)))
    log_beta = jnp.minimum(z, 0.0) - sp
    log_1mb = jnp.where(mask, jnp.minimum(-z, 0.0) - sp, 0.0)
    return z, log_beta, log_1mb, mask


def _sb_weights(log_beta, log_1mb, mask, run, upper):
    after = run + _dot_exact(log_1mb, upper)
    return jnp.where(mask, jnp.exp(log_beta + after), 0.0)


def _tri(strict_lower):
    row = lax.broadcasted_iota(jnp.int32, (BLOCK, BLOCK), 0)
    col = lax.broadcasted_iota(jnp.int32, (BLOCK, BLOCK), 1)
    return ((row > col) if strict_lower else (row < col)).astype(BF16)


def _sb_fwd(proj, *, name):
    s = proj.shape[0]
    assert s % (BLOCK * SB_UNROLL) == 0 and s % SB_QROWS == 0
    qb, kb, vb = OFF_QB // HEAD_DIM, OFF_KB // HEAD_DIM, OFF_VB // HEAD_DIM

    def body(q_ref, k_ref, v_ref, o_ref):
        i = pl.program_id(1)
        q = q_ref[...].astype(BF16)
        upper = _tri(True)
        nsteps = _sb_steps(i)

        def step(t, carry):
            acc, run = carry
            for b in reversed(range(SB_UNROLL)):
                j = (nsteps - 1 - t) * SB_UNROLL + b
                rows = _block_rows(j)
                _, log_beta, log_1mb, mask = _sb_scores(q, k_ref[rows, :].astype(BF16), j, i)
                a = _sb_weights(log_beta, log_1mb, mask, run, upper)
                acc = acc + _dot(a.astype(BF16), v_ref[rows, :].astype(BF16))
                run = run + jnp.sum(log_1mb, axis=-1, keepdims=True)
            return acc, run

        acc, _ = lax.fori_loop(0, nsteps, step,
                               (jnp.zeros((SB_QROWS, HEAD_DIM), F32), jnp.zeros((SB_QROWS, 1), F32)))
        o_ref[...] = acc.astype(BF16)

    return _pcall(body, name=name, out_shape=jax.ShapeDtypeStruct((s, B_W), BF16), grid=(SB_HEADS, s // SB_QROWS),
                  in_specs=[pl.BlockSpec((SB_QROWS, HEAD_DIM), lambda h, i: (i, qb + h)),
                            pl.BlockSpec((s, HEAD_DIM), lambda h, i: (0, kb + h)),
                            pl.BlockSpec((s, HEAD_DIM), lambda h, i: (0, vb + h))],
                  out_specs=pl.BlockSpec((SB_QROWS, HEAD_DIM), lambda h, i: (i, h)))(proj, proj, proj)


def _sb_bwd(proj, do_b, *, name):
    s = proj.shape[0]
    assert s % (BLOCK * SB_UNROLL) == 0 and s % SB_QROWS == 0
    nkb = s // BLOCK
    qb, kb, vb = OFF_QB // HEAD_DIM, OFF_KB // HEAD_DIM, OFF_VB // HEAD_DIM

    def body(q_ref, k_ref, v_ref, do_ref, dq_ref, dk_ref, dv_ref, z_s, a_s):
        i = pl.program_id(1)

        @pl.when(i == 0)
        def _():
            dk_ref[...] = jnp.zeros_like(dk_ref)
            dv_ref[...] = jnp.zeros_like(dv_ref)

        q = q_ref[...].astype(BF16)
        dob = do_ref[...].astype(BF16)
        upper, lower = _tri(True), _tri(False)
        nsteps = _sb_steps(i)

        def recompute(t, run):
            for b in reversed(range(SB_UNROLL)):
                j = (nsteps - 1 - t) * SB_UNROLL + b
                z, log_beta, log_1mb, mask = _sb_scores(q, k_ref[_block_rows(j), :].astype(BF16), j, i)
                z_s[j] = z
                a_s[j] = _sb_weights(log_beta, log_1mb, mask, run, upper)
                run = run + jnp.sum(log_1mb, axis=-1, keepdims=True)
            return run

        lax.fori_loop(0, nsteps, recompute, jnp.zeros((SB_QROWS, 1), F32))

        def grads(t, carry):
            dq, run = carry
            for b in range(SB_UNROLL):
                j = t * SB_UNROLL + b
                rows = _block_rows(j)
                kk, vv = k_ref[rows, :].astype(BF16), v_ref[rows, :].astype(BF16)
                z, a = z_s[j], a_s[j]
                de = _dot(dob, vv, NT) * a
                before = run + _dot_exact(de, lower)
                dz = (de * jax.nn.sigmoid(-z)
                      - jnp.where(_sb_mask(j, i), jax.nn.sigmoid(z), 0.0) * before).astype(BF16)
                dk_ref[rows, :] += _dot(dz, q, TN) * ATT_SCALE
                dv_ref[rows, :] += _dot(a.astype(BF16), dob, TN)
                dq = dq + _dot(dz, kk)
                run = run + jnp.sum(de, axis=-1, keepdims=True)
            return dq, run

        dq, _ = lax.fori_loop(0, nsteps, grads,
                              (jnp.zeros((SB_QROWS, HEAD_DIM), F32), jnp.zeros((SB_QROWS, 1), F32)))
        dq_ref[...] = dq * ATT_SCALE

    blk = pl.BlockSpec((SB_QROWS, HEAD_DIM), lambda h, i: (i, h))
    full = pl.BlockSpec((s, HEAD_DIM), lambda h, i: (0, h))
    shp = jax.ShapeDtypeStruct((s, B_W), F32)
    return _pcall(body, name=name, out_shape=(shp, shp, shp), grid=(SB_HEADS, s // SB_QROWS),
                  in_specs=[pl.BlockSpec((SB_QROWS, HEAD_DIM), lambda h, i: (i, qb + h)),
                            pl.BlockSpec((s, HEAD_DIM), lambda h, i: (0, kb + h)),
                            pl.BlockSpec((s, HEAD_DIM), lambda h, i: (0, vb + h)), blk],
                  out_specs=(blk, full, full),
                  scratch=[pltpu.VMEM((nkb, SB_QROWS, BLOCK), F32), pltpu.VMEM((nkb, SB_QROWS, BLOCK), F32)])(
                      proj, proj, proj, do_b)


def _coords():
    return lax.axis_index("x"), lax.axis_index("y"), lax.axis_index("c")


def _flip(v, bit):
    return 1 - v if bit else v


def _shard_of(ref, axis, idx, size):
    if axis == 0:
        sl = pl.ds(pl.multiple_of(idx * size, 16), size)
        return ref.at[sl, :] if len(ref.shape) == 2 else ref.at[:, sl, :]
    sl = pl.ds(pl.multiple_of(idx * size, 128), size)
    return ref.at[:, sl] if len(ref.shape) == 2 else ref.at[:, :, sl]


def _small_allgather(v, *, name, silu=False, deps=()):
    n = v.shape[1]

    def body(v_ref, out_ref, send_sems, recv_sems):
        x, y, c = _coords()
        me = 4 * x + 2 * y + c
        val = v_ref[...]
        out_ref[me] = val * jax.nn.sigmoid(val) if silu else val
        copies = []
        for k in range(1, N_DEV):
            peer = (_flip(x, k & 4), _flip(y, k & 2), _flip(c, k & 1))
            copies.append(pltpu.make_async_remote_copy(
                src_ref=out_ref.at[me], dst_ref=out_ref.at[me], send_sem=send_sems.at[k - 1],
                recv_sem=recv_sems.at[k - 1], device_id=peer, device_id_type=MESH))
        for cp in copies:
            cp.start()
        for cp in copies:
            cp.wait_recv()
        for cp in copies:
            cp.wait_send()

    return _pcall(body, name=name, out_shape=jax.ShapeDtypeStruct((N_DEV, 1, n), F32),
                  in_specs=[pl.BlockSpec(memory_space=pltpu.VMEM)], out_specs=pl.BlockSpec(memory_space=pltpu.VMEM),
                  scratch=[pltpu.SemaphoreType.DMA((N_DEV - 1,)), pltpu.SemaphoreType.DMA((N_DEV - 1,))],
                  deps=deps)(v)


def _cast_place(w, layer, axis, me, *, name):
    _, r, c = w.shape
    tr = _rows(r, c)
    nrt = r // tr

    def body(me_ref, w_ref, o_ref):
        o_ref[...] = w_ref[...].astype(BF16)

    wspec = pl.BlockSpec((None, tr, c), lambda i, me_ref: (layer, i, 0))
    if axis == 0:
        ospec = pl.BlockSpec((tr, c), lambda i, me_ref: (me_ref[0] * nrt + i, 0))
        shape = (r * N_DEV, c)
    else:
        ospec = pl.BlockSpec((tr, c), lambda i, me_ref: (i, me_ref[0]))
        shape = (r, c * N_DEV)
    grid_spec = pltpu.PrefetchScalarGridSpec(num_scalar_prefetch=1, grid=(nrt,), in_specs=[wspec], out_specs=ospec)
    return _pcall(body, name=name, out_shape=jax.ShapeDtypeStruct(shape, BF16), grid_spec=grid_spec)(me, w)


def _gather_weights(fulls, axes, *, name):
    nt = len(fulls)
    sizes = [f.shape[ax] // N_DEV for f, ax in zip(fulls, axes)]

    def body(*refs):
        outs = refs[nt:2 * nt]
        send_sems, recv_sems = refs[2 * nt:]
        x, y, c = _coords()
        me, sibling = (x, y, c), (x, y, 1 - c)
        chips = [(1 - x, y), (x, 1 - y), (1 - x, 1 - y)]

        def slot(t, dev):
            return _shard_of(outs[t], axes[t], 4 * dev[0] + 2 * dev[1] + dev[2], sizes[t])

        def copy(t, k, block, to):
            return pltpu.make_async_remote_copy(
                src_ref=slot(t, block), dst_ref=slot(t, block),
                send_sem=send_sems.at[t, k], recv_sem=recv_sems.at[t, k], device_id=to, device_id_type=MESH)

        first = []
        for t in range(nt):
            first.append(copy(t, 0, me, sibling))
            first += [copy(t, 1 + j, me, (*chip, c)) for j, chip in enumerate(chips)]
        for cp in first:
            cp.start()
        passed = []
        for j, chip in enumerate(chips):
            for t in range(nt):
                copy(t, 1 + j, (*chip, c), me).wait_recv()
                fwd = copy(t, 4 + j, (*chip, c), sibling)
                fwd.start()
                passed.append(fwd)
        for t in range(nt):
            copy(t, 0, sibling, me).wait_recv()
            for j, chip in enumerate(chips):
                copy(t, 4 + j, (*chip, 1 - c), me).wait_recv()
        for cp in first + passed:
            cp.wait_send()

    hbm = pl.BlockSpec(memory_space=pl.ANY)
    return _pcall(body, name=name, out_shape=tuple(jax.ShapeDtypeStruct(f.shape, f.dtype) for f in fulls),
                  in_specs=[hbm] * nt, out_specs=(hbm,) * nt, aliases={t: t for t in range(nt)},
                  scratch=[pltpu.SemaphoreType.DMA((nt, 7)), pltpu.SemaphoreType.DMA((nt, 7))])(*fulls)


def _rs_pair_exchange(grads, axes, *, name):
    nt = len(grads)
    sizes = [g.shape[ax] // N_DEV for g, ax in zip(grads, axes)]

    def recv_shape(g, ax):
        dims = list(g.shape)
        dims[ax] //= N_DEV
        return jax.ShapeDtypeStruct((N_CHIPS, dims[0], dims[1]), g.dtype)

    def body(*refs):
        ins, outs = refs[:nt], refs[nt:2 * nt]
        send_sems, recv_sems = refs[2 * nt:]
        x, y, c = _coords()
        copies = []
        for t in range(nt):
            for q in range(N_CHIPS):
                copies.append(pltpu.make_async_remote_copy(
                    src_ref=_shard_of(ins[t], axes[t], 2 * q + 1 - c, sizes[t]), dst_ref=outs[t].at[q],
                    send_sem=send_sems.at[t, q], recv_sem=recv_sems.at[t, q], device_id=(x, y, 1 - c),
                    device_id_type=MESH))
        for cp in copies:
            cp.start()
        for cp in copies:
            cp.wait_recv()
        for cp in copies:
            cp.wait_send()

    hbm = pl.BlockSpec(memory_space=pl.ANY)
    return _pcall(body, name=name, out_shape=tuple(recv_shape(g, ax) for g, ax in zip(grads, axes)),
                  in_specs=[hbm] * nt, out_specs=(hbm,) * nt,
                  scratch=[pltpu.SemaphoreType.DMA((nt, N_CHIPS)), pltpu.SemaphoreType.DMA((nt, N_CHIPS))])(*grads)


def _pair_sum(grad, sib, core, axis, *, name):
    _, r, c = sib.shape
    tr = _rows(r, c)
    nrt = r // tr

    def body(core_ref, g_ref, s_ref, o_ref):
        o_ref[...] = (g_ref[...].astype(F32) + s_ref[...].astype(F32)).astype(BF16)

    if axis == 0:
        gspec = pl.BlockSpec((tr, c), lambda q, i, core_ref: ((2 * q + core_ref[0]) * nrt + i, 0))
    else:
        gspec = pl.BlockSpec((tr, c), lambda q, i, core_ref: (i, 2 * q + core_ref[0]))
    sspec = pl.BlockSpec((None, tr, c), lambda q, i, core_ref: (q, i, 0))
    grid_spec = pltpu.PrefetchScalarGridSpec(num_scalar_prefetch=1, grid=(N_CHIPS, nrt),
                                             in_specs=[gspec, sspec], out_specs=sspec)
    return _pcall(body, name=name, out_shape=jax.ShapeDtypeStruct(sib.shape, BF16), grid_spec=grid_spec)(
        core, grad, sib)


HBM_SPEC = pl.BlockSpec(memory_space=pltpu.HBM)
SEM_SPEC = pl.BlockSpec(memory_space=pltpu.SEMAPHORE)
SPLIT_PARAMS = dict(has_side_effects=pltpu.SideEffectType.DATAFLOW_SIDE_EFFECTING)


def _hbm(a):
    return pltpu.with_memory_space_constraint(a, pltpu.HBM)


def _split_start(copies_fn, buffers, sem_shape, after, *, name):
    n = len(buffers)
    rows, cols = sem_shape
    ns = rows * cols

    def body(*refs):
        sems = refs[n + 1:n + 1 + 2 * ns]
        for cp in copies_fn(refs[:n], _sem_rows(sems[:ns], cols), _sem_rows(sems[ns:], cols)):
            cp.start()
        refs[-1][...] = jnp.zeros_like(refs[-1])

    sem = pltpu.SemaphoreType.DMA(())
    outs = pl.pallas_call(
        body, name=name,
        out_shape=((sem,) * (2 * ns) + tuple(pltpu.HBM(b.shape, b.dtype) for b in buffers)
                   + (jax.ShapeDtypeStruct((8, 128), F32),)),
        in_specs=(HBM_SPEC,) * n + (pl.BlockSpec(memory_space=pl.ANY),),
        out_specs=(SEM_SPEC,) * (2 * ns) + (HBM_SPEC,) * n + (pl.BlockSpec(memory_space=pltpu.VMEM),),
        input_output_aliases={i: 2 * ns + i for i in range(n)},
        compiler_params=pltpu.CompilerParams(**SPLIT_PARAMS))(*[_hbm(b) for b in buffers], after)
    return list(outs[:ns]), list(outs[ns:2 * ns]), list(outs[2 * ns:2 * ns + n]), outs[-1]


def _split_wait(copies_fn, send_sems, recv_sems, buffers, after, sem_rows, *, name):
    n, ns = len(buffers), len(send_sems)
    cols = ns // sem_rows

    def body(*refs):
        sems = refs[n:n + 2 * ns]
        copies = copies_fn(refs[:n], _sem_rows(sems[:ns], cols), _sem_rows(sems[ns:], cols))
        for cp in copies:
            cp.wait_send()
        for cp in copies:
            cp.wait_recv()

    outs = pl.pallas_call(
        body, name=name, out_shape=tuple(pltpu.HBM(b.shape, b.dtype) for b in buffers),
        in_specs=(HBM_SPEC,) * n + (SEM_SPEC,) * (2 * ns) + (pl.BlockSpec(memory_space=pl.ANY),),
        out_specs=(HBM_SPEC,) * n, input_output_aliases={i: i for i in range(n)},
        compiler_params=pltpu.CompilerParams(**SPLIT_PARAMS))(*buffers, *send_sems, *recv_sems, after)
    return list(outs)


def _sem_rows(sems, cols):
    return [sems[i:i + cols] for i in range(0, len(sems), cols)]


def _empty_hbm(shape, dtype):
    return _hbm(lax.empty(shape, dtype))


class _SplitGather:
    def __init__(self, fulls, axes, tag):
        self.axes, self.tag, self.nt = list(axes), tag, len(fulls)
        self.sizes = [f.shape[ax] // N_DEV for f, ax in zip(fulls, axes)]
        self.fulls = list(fulls)

    def _slot(self, ref, t, dev):
        return _shard_of(ref, self.axes[t], 4 * dev[0] + 2 * dev[1] + dev[2], self.sizes[t])

    def _first_copies(self, refs, send_sems, recv_sems):
        x, y, c = _coords()
        peers = [(x, y, 1 - c), (1 - x, y, c), (x, 1 - y, c), (1 - x, 1 - y, c)]
        return [pltpu.make_async_remote_copy(
            src_ref=self._slot(refs[t], t, (x, y, c)), dst_ref=self._slot(refs[t], t, (x, y, c)),
            send_sem=send_sems[t][k], recv_sem=recv_sems[t][k], device_id=peer, device_id_type=MESH)
            for t in range(self.nt) for k, peer in enumerate(peers)]

    def _forward_copies(self, refs, send_sems, recv_sems):
        x, y, c = _coords()
        chips = [(1 - x, y), (x, 1 - y), (1 - x, 1 - y)]
        return [pltpu.make_async_remote_copy(
            src_ref=self._slot(refs[t], t, (*chip, c)), dst_ref=self._slot(refs[t], t, (*chip, c)),
            send_sem=send_sems[t][j], recv_sem=recv_sems[t][j], device_id=(x, y, 1 - c), device_id_type=MESH)
            for t in range(self.nt) for j, chip in enumerate(chips)]

    def first(self, after):
        self.s1, self.r1, self.fulls, token = _split_start(
            self._first_copies, self.fulls, (self.nt, 4), after, name=f"comm_gather1_start_{self.tag}")
        return token

    def forward(self, after):
        bufs = _split_wait(self._first_copies, self.s1, self.r1, self.fulls, after, self.nt,
                           name=f"comm_gather1_wait_{self.tag}")
        self.s2, self.r2, self.fulls, token = _split_start(
            self._forward_copies, bufs, (self.nt, 3), after, name=f"comm_gather2_start_{self.tag}")
        return token

    def finish(self, after):
        return _split_wait(self._forward_copies, self.s2, self.r2, self.fulls, after, self.nt,
                           name=f"comm_gather2_wait_{self.tag}")


class _SplitChipExchange:
    def __init__(self, sums, tag):
        self.nt, self.tag = len(sums), tag
        self.sums = list(sums)

    def _copies(self, refs, send_sems, recv_sems):
        nt = self.nt
        x, y, c = _coords()
        copies = []
        for t in range(nt):
            for k in range(1, N_CHIPS):
                px, py = _flip(x, k & 2), _flip(y, k & 1)
                copies.append(pltpu.make_async_remote_copy(
                    src_ref=refs[t].at[2 * px + py], dst_ref=refs[nt + t].at[k - 1], send_sem=send_sems[t][k - 1],
                    recv_sem=recv_sems[t][k - 1], device_id=(px, py, c), device_id_type=MESH))
        return copies

    def start(self):
        landing = [_empty_hbm((N_CHIPS - 1,) + s.shape[1:], s.dtype) for s in self.sums]
        self.s, self.r, self.bufs, token = _split_start(
            self._copies, self.sums + landing, (self.nt, N_CHIPS - 1), self.sums[-1],
            name=f"comm_rs_chip_start_{self.tag}")
        return token

    def finish(self, after):
        bufs = _split_wait(self._copies, self.s, self.r, self.bufs, after, self.nt,
                           name=f"comm_rs_chip_wait_{self.tag}")
        return bufs[:self.nt], bufs[self.nt:]


def _adam_math(g, w, m, v):
    m2 = ADAM_B1 * m + (1.0 - ADAM_B1) * g
    v2 = ADAM_B2 * v + (1.0 - ADAM_B2) * (g * g)
    m_hat = m2 / (1.0 - ADAM_B1 ** ADAM_STEP)
    v_hat = v2 / (1.0 - ADAM_B2 ** ADAM_STEP)
    delta = -ADAM_LR * (m_hat / (jnp.sqrt(v_hat) + ADAM_EPS) + ADAM_WD * w)
    return delta, m2, v2


def _adamw_sharded(chip_sums, remote, chip, w, m, v, layer, prev, deps, *, name):
    nl, r, c = w.shape
    tr = _rows(r, c)

    def body(*refs):
        p_ref, r0_ref, r1_ref, r2_ref, w_ref, m_ref, v_ref = refs[1:8]
        g_out, d_out, m_out, v_out = refs[-4:]
        g = ((p_ref[...].astype(F32) + r0_ref[...].astype(F32)) + r1_ref[...].astype(F32)) + r2_ref[...].astype(F32)
        g_out[...] = g
        d_out[...], m_out[...], v_out[...] = _adam_math(g, w_ref[...], m_ref[...], v_ref[...])

    pspec = pl.BlockSpec((None, tr, c), lambda i, chip_ref: (chip_ref[0], i, 0))

    def rspec(k):
        return pl.BlockSpec((None, tr, c), lambda i, chip_ref: (k, i, 0))

    wspec = pl.BlockSpec((None, tr, c), lambda i, chip_ref: (layer, i, 0))
    in_specs = [pspec, rspec(0), rspec(1), rspec(2), wspec, wspec, wspec]
    args = [chip, chip_sums, remote, remote, remote, w, m, v]
    aliases = {}
    if prev is not None:
        in_specs += [pl.BlockSpec(memory_space=pl.ANY)] * 4
        aliases = {len(args) + i: i for i in range(4)}
        args += list(prev)
    in_specs += [pl.BlockSpec(memory_space=pl.ANY)] * len(deps)
    args += list(deps)
    grid_spec = pltpu.PrefetchScalarGridSpec(num_scalar_prefetch=1, grid=(r // tr,), in_specs=in_specs,
                                             out_specs=(wspec,) * 4)
    shp = jax.ShapeDtypeStruct(w.shape, F32)
    return _pcall(body, name=name, out_shape=(shp,) * 4, grid_spec=grid_spec, aliases=aliases)(*args)


def _adamw_local(g, w, m, v, *, name):
    nl, r, c = w.shape
    tr = _rows(r, c)

    def body(g_ref, w_ref, m_ref, v_ref, d_out, m_out, v_out):
        d_out[...], m_out[...], v_out[...] = _adam_math(g_ref[...], w_ref[...], m_ref[...], v_ref[...])

    spec = pl.BlockSpec((None, tr, c), lambda l, i: (l, i, 0))
    shp = jax.ShapeDtypeStruct(w.shape, F32)
    return _pcall(body, name=name, out_shape=(shp,) * 3, grid=(nl, r // tr), in_specs=[spec] * 4,
                  out_specs=(spec,) * 3)(g, w, m, v)


def _adamw_replicated(parts, w, m, v, *, name):
    n = w.shape[1]

    def body(p_ref, w_ref, m_ref, v_ref, g_out, d_out, m_out, v_out):
        g = p_ref[0]
        for k in range(1, N_DEV):
            g = g + p_ref[k]
        g_out[...] = g
        d_out[...], m_out[...], v_out[...] = _adam_math(g, w_ref[...], m_ref[...], v_ref[...])

    vm = pl.BlockSpec(memory_space=pltpu.VMEM)
    shp = jax.ShapeDtypeStruct((1, n), F32)
    return _pcall(body, name=name, out_shape=(shp,) * 4, in_specs=[vm] * 4, out_specs=(vm,) * 4)(parts, w, m, v)


def _group_views(qk, proj, g, dil, seq):
    if dil == 1:
        return (qk, qk, proj), (0, A_HEADS, 2 * A_HEADS)
    length = seq // dil
    lo = g * GROUP_W
    q = qk[:, lo:lo + GROUP_W].reshape(length, dil * GROUP_W)
    k = qk[:, A_W + lo:A_W + lo + GROUP_W].reshape(length, dil * GROUP_W)
    v = proj[:, OFF_VA + lo:OFF_VA + lo + GROUP_W].astype(BF16).reshape(length, dil * GROUP_W)
    return (q, k, v), (0, 0, 0)


def _mod_rows(mod, d):
    return [mod[:, i * d:(i + 1) * d] for i in range(6)]


MIXER_W = ("w_in", "w_branch_a", "w_branch_b", "w_out")
FFN_W = ("w_gate_up", "w_down")
SHARD_AXIS = {"w_in": 1, "w_branch_a": 1, "w_branch_b": 1, "w_out": 0, "w_gate_up": 1, "w_down": 0}


def _mixer_fwd_a(h, mod, g1, gains, w_in, cos2, sin2, deps):
    seq, d = h.shape
    sh1, sc1 = _mod_rows(mod, d)[:2]
    u = _rmsmod_fwd(h, g1, sc1, sh1, name="rmsmod_fwd", deps=deps)
    proj = _mm(u, w_in, name="mm_in")
    qk = _qkrope_fwd(proj, gains, cos2, sin2, name="qkrope_fwd")
    os_, lses = [], []
    for g, dil in enumerate(DILATIONS):
        arrs, offs = _group_views(qk, proj, g, dil, seq)
        o, lse = _dil_fwd(*arrs, offs, seq // dil, dil, name=f"dil_fwd_{dil}")
        os_.append(o.reshape(seq, GROUP_W))
        lses.append(lse.reshape(seq, GROUP_W))
    o_a = _combine_fwd(os_, lses, name="combine_fwd")
    o_b = _sb_fwd(proj, name="sb_fwd")
    return dict(h_in=h, u=u, proj=proj, qk=qk, os=os_, lses=lses, o_a=o_a, o_b=o_b)


def _mixer_fwd_b(sv, mod, wts, deps):
    d = sv["h_in"].shape[1]
    ga1 = _mod_rows(mod, d)[2]
    y_a = _mm(sv["o_a"], wts["w_branch_a"], name="mm_branch", deps=deps)
    y_b = _mm(sv["o_b"], wts["w_branch_b"], name="mm_branch")
    merged = _merge_fwd(sv["proj"], y_a, y_b, name="merge_fwd")
    t = _mm(merged, wts["w_out"], name="mm_out")
    h_mid = _resid_gate(sv["h_in"], ga1, t, name="resid_gate")
    sv.update(y_a=y_a, y_b=y_b, merged=merged, t=t, h_mid=h_mid)
    return h_mid


def _ffn_fwd_a(sv, mod, g2, w_gate_up, deps):
    d = sv["h_mid"].shape[1]
    sh2, sc2 = _mod_rows(mod, d)[3:5]
    u2 = _rmsmod_fwd(sv["h_mid"], g2, sc2, sh2, name="rmsmod_fwd", deps=deps)
    gu = _mm(u2, w_gate_up, name="mm_gate_up")
    a = _swiglu_fwd(gu, name="swiglu_fwd")
    sv.update(u2=u2, gu=gu, a=a)
    return a


def _ffn_fwd_b(sv, mod, w_down, deps):
    d = sv["h_mid"].shape[1]
    ga2 = _mod_rows(mod, d)[5]
    f = _mm(sv["a"], w_down, name="mm_down", deps=deps)
    sv["f"] = f
    return _resid_gate(sv["h_mid"], ga2, f, name="resid_gate")


def _wgrad(act, dout, key):
    return _mm(act, dout, ta=True, out_dtype=BF16, name="mm_wgrad_" + key)


def _ffn_bwd(dh, sv, mod, g2, wts, deps):
    d = dh.shape[1]
    sc2, ga2 = _mod_rows(mod, d)[4:6]
    df, dgate2 = _resid_gate_bwd(dh, sv["f"], ga2, name="resid_gate_bwd", deps=deps)
    da = _mm(df, wts["w_down"], tb=True, name="mm_down_t")
    grads = {"w_down": _wgrad(sv["a"], df, "w_down")}
    dgu = _swiglu_bwd(sv["gu"], da, name="swiglu_bwd")
    du2 = _mm(dgu, wts["w_gate_up"], tb=True, name="mm_gate_up_t")
    grads["w_gate_up"] = _wgrad(sv["u2"], dgu, "w_gate_up")
    dh_mid, dsh2, dsc2, dg2 = _rmsmod_bwd(du2, sv["h_mid"], g2, sc2, dh, name="rmsmod_bwd")
    return dh_mid, [dsh2, dsc2, dgate2], dg2, grads


def _mixer_bwd(dh_mid, sv, mod, g1, gains, wts, cos2, sin2, deps):
    seq, d = dh_mid.shape
    sc1, ga1 = _mod_rows(mod, d)[1:3]
    dt, dgate1 = _resid_gate_bwd(dh_mid, sv["t"], ga1, name="resid_gate_bwd", deps=deps)
    dmerged = _mm(dt, wts["w_out"], tb=True, name="mm_out_t")
    grads = {"w_out": _wgrad(sv["merged"], dt, "w_out")}
    dy_a, dy_b, dga, dgb = _merge_bwd(dmerged, sv["proj"], sv["y_a"], sv["y_b"], name="merge_bwd")
    do_a = _mm(dy_a, wts["w_branch_a"], tb=True, name="mm_branch_t")
    do_b = _mm(dy_b, wts["w_branch_b"], tb=True, name="mm_branch_t")
    grads["w_branch_a"] = _wgrad(sv["o_a"], dy_a, "w_branch_a")
    grads["w_branch_b"] = _wgrad(sv["o_b"], dy_b, "w_branch_b")
    dqb, dkb, dvb = _sb_bwd(sv["proj"], do_b, name="sb_bwd")
    comb = _combine_bwd(do_a, sv["os"], sv["lses"], name="combine_bwd")
    dos, dls = comb[:3], comb[3:]
    dqs, dks, dvs = [], [], []
    for g, dil in enumerate(DILATIONS):
        length = seq // dil
        arrs, offs = _group_views(sv["qk"], sv["proj"], g, dil, seq)
        view = (length, dil * GROUP_W)
        dq, dk, dv = _dil_bwd(*arrs, offs, sv["os"][g].reshape(view), sv["lses"][g].reshape(view),
                              dos[g].reshape(view), dls[g].reshape(view), length, dil, name=f"dil_bwd_{dil}")
        dqs.append(dq.reshape(seq, GROUP_W))
        dks.append(dk.reshape(seq, GROUP_W))
        dvs.append(dv.reshape(seq, GROUP_W))
    dqk, dgains = _qkrope_bwd(jnp.concatenate(dqs + dks, axis=1), sv["proj"], gains, cos2, sin2,
                              name="qkrope_bwd")
    dproj = jnp.concatenate(
        [dqk] + [t_.astype(BF16) for t_ in dvs + [dqb, dkb, dvb]] + [dga, dgb], axis=1)
    du = _mm(dproj, wts["w_in"], tb=True, name="mm_in_t")
    grads["w_in"] = _wgrad(sv["u"], dproj, "w_in")
    dh_in, dsh1, dsc1, dg1 = _rmsmod_bwd(du, sv["h_in"], g1, sc1, dh_mid, name="rmsmod_bwd")
    return dh_in, [dsh1, dsc1, dgate1], dg1, dgains, grads


def kernel(x, c, w_ada, b_ada, norm1_g, norm2_g, w_in, qn_g, kn_g, w_branch_a, w_branch_b, w_out, w_gate_up, w_down, loss_target, m_w_ada, m_b_ada, m_norm1_g, m_norm2_g, m_w_in, m_qn_g, m_kn_g, m_w_branch_a, m_w_branch_b, m_w_out, m_w_gate_up, m_w_down, v_w_ada, v_b_ada, v_norm1_g, v_norm2_g, v_w_in, v_qn_g, v_kn_g, v_w_branch_a, v_w_branch_b, v_w_out, v_w_gate_up, v_w_down):
    seq, d = x.shape[1], x.shape[2]
    depth = w_in.shape[0]
    weights = dict(w_in=w_in, w_branch_a=w_branch_a, w_branch_b=w_branch_b, w_out=w_out, w_gate_up=w_gate_up,
                   w_down=w_down)
    moments_m = dict(w_in=m_w_in, w_branch_a=m_w_branch_a, w_branch_b=m_w_branch_b, w_out=m_w_out,
                     w_gate_up=m_w_gate_up, w_down=m_w_down)
    moments_v = dict(w_in=v_w_in, w_branch_a=v_w_branch_a, w_branch_b=v_w_branch_b, w_out=v_w_out,
                     w_gate_up=v_w_gate_up, w_down=v_w_down)
    xi, yi, ci = _coords()
    me = 4 * xi + 2 * yi + ci
    core = jnp.reshape(ci, (1,)).astype(jnp.int32)
    chip = jnp.reshape(2 * xi + yi, (1,)).astype(jnp.int32)

    ada_w = w_ada.shape[2]
    c_act = _small_allgather(c, name="comm_gather_c", silu=True).reshape(N_DEV, d)
    c_pad = jnp.concatenate([c_act, jnp.zeros_like(c_act)], axis=0).astype(BF16)
    bias = lax.dynamic_slice(b_ada, (0, me * ada_w), (depth, ada_w))
    mod_part = jnp.stack([_mm(c_pad, w_ada[l], name="mm_ada")[:N_DEV] for l in range(depth)]) + bias[:, None, :]
    mod_all = _small_allgather(mod_part.reshape(1, depth * N_DEV * ada_w), name="comm_gather_mod")
    mod_all = mod_all.reshape(N_DEV, depth, N_DEV, ada_w)
    mod_mine = lax.dynamic_index_in_dim(mod_all, me, axis=2, keepdims=False)
    mods = jnp.transpose(mod_mine, (1, 0, 2)).reshape(depth, 1, 6 * d)

    cos2, sin2 = _rope_tables(seq)
    gains = [jnp.stack([qn_g[l], kn_g[l]])[:, None, :] for l in range(depth)]
    g1s = [norm1_g[l][None] for l in range(depth)]
    g2s = [norm2_g[l][None] for l in range(depth)]

    me_arr = jnp.reshape(me, (1,)).astype(jnp.int32)

    def placed(keys, l):
        return [_cast_place(weights[k], l, SHARD_AXIS[k], me_arr, name="cast_place_" + k) for k in keys]

    def gather_of(keys, l, tag):
        return _SplitGather(placed(keys, l), [SHARD_AXIS[k] for k in keys], f"{tag}{l}")

    wm = dict(zip(MIXER_W, _gather_weights(placed(MIXER_W, 0), [SHARD_AXIS[k] for k in MIXER_W],
                                           name="comm_gather_weights")))
    gf = gather_of(FFN_W, 0, "ffn")
    deps = [gf.first(after=wm["w_in"])]
    h = x[0]
    saved, full = [], []
    for l in range(depth):
        last = l + 1 == depth
        sv = _mixer_fwd_a(h, mods[l], g1s[l], gains[l], wm["w_in"], cos2, sin2, deps)
        deps = [gf.forward(after=sv["o_b"])]
        if not last:
            gm_next = gather_of(MIXER_W, l + 1, "mixer")
            deps.append(gm_next.first(after=sv["o_b"]))
        h_mid = _mixer_fwd_b(sv, mods[l], wm, deps)
        wf = dict(zip(FFN_W, gf.finish(after=h_mid)))
        a = _ffn_fwd_a(sv, mods[l], g2s[l], wf["w_gate_up"], [])
        deps = []
        if not last:
            deps.append(gm_next.forward(after=a))
            gf = gather_of(FFN_W, l + 1, "ffn")
            deps.append(gf.first(after=a))
        h = _ffn_fwd_b(sv, mods[l], wf["w_down"], deps)
        saved.append(sv)
        full.append({**wm, **wf})
        if not last:
            wm = dict(zip(MIXER_W, gm_next.finish(after=h)))
            deps = []
    loss_part, dh = _loss_fwd(h, loss_target[0], name="loss")
    loss = lax.psum(loss_part[0, 0], ("x", "y", "c"))

    done, pending = [], None

    def reduce_start(keys, grads, l):
        axes = [SHARD_AXIS[k] for k in keys]
        tag = f"{keys[0]}{l}"
        sib = _rs_pair_exchange([grads[k] for k in keys], axes, name="comm_rs_pair_" + tag)
        sums = [_pair_sum(grads[k], s_, core, ax, name="pair_sum_" + k) for k, s_, ax in zip(keys, sib, axes)]
        ex = _SplitChipExchange(sums, tag)
        return (keys, l, ex), [ex.start()]

    def reduce_finish(item, after):
        keys, l, ex = item
        sums, remote = ex.finish(after)
        done.append((keys, l, sums, remote))

    dmods, dg1s, dg2s, dgains = [None] * depth, [None] * depth, [None] * depth, [None] * depth
    deps = []
    for l in reversed(range(depth)):
        dh_mid, dmod_f, dg2s[l], grads = _ffn_bwd(dh, saved[l], mods[l], g2s[l], full[l], deps)
        if pending is not None:
            reduce_finish(pending, dh_mid)
        pending, deps = reduce_start(FFN_W, grads, l)
        dh, dmod_m, dg1s[l], dgains[l], grads = _mixer_bwd(dh_mid, saved[l], mods[l], g1s[l], gains[l], full[l],
                                                           cos2, sin2, deps)
        dmods[l] = jnp.concatenate(dmod_m + dmod_f, axis=1)
        reduce_finish(pending, dh)
        pending, deps = reduce_start(MIXER_W, grads, l)
    grad_x = dh[None]

    small = jnp.concatenate(
        dmods + dg1s + dg2s + [dgains[l][0] for l in range(depth)] + [dgains[l][1] for l in range(depth)], axis=1)

    stacked = {}

    def update(keys, l, sums, remote, deps):
        for k, p_, r_ in zip(keys, sums, remote):
            stacked[k] = _adamw_sharded(p_, r_, chip, weights[k], moments_m[k], moments_v[k], l, stacked.get(k),
                                        deps, name="adamw_" + k)

    for item in done:
        update(*item, deps)
    reduce_finish(pending, stacked[FFN_W[0]][0])
    update(*done[-1], [])
    small_all = _small_allgather(small, name="comm_gather_small", deps=[done[-1][3][0]])

    def pack(b, n1, n2, qn, kn):
        return jnp.concatenate([t_.reshape(1, -1) for t_ in (b, n1, n2, qn, kn)], axis=1)

    sg, sd, sm, sv_ = _adamw_replicated(small_all, pack(b_ada, norm1_g, norm2_g, qn_g, kn_g),
                                        pack(m_b_ada, m_norm1_g, m_norm2_g, m_qn_g, m_kn_g),
                                        pack(v_b_ada, v_norm1_g, v_norm2_g, v_qn_g, v_kn_g), name="adamw_replicated")

    def unpack(p):
        sizes = [depth * 6 * d, depth * d, depth * d, depth * HEAD_DIM, depth * HEAD_DIM]
        shapes = [b_ada.shape, norm1_g.shape, norm2_g.shape, qn_g.shape, kn_g.shape]
        out, off = [], 0
        for n, shp in zip(sizes, shapes):
            out.append(p[0, off:off + n].reshape(shp))
            off += n
        return dict(zip(("b_ada", "norm1_g", "norm2_g", "qn_g", "kn_g"), out))

    ug, ud, um, uv = unpack(sg), unpack(sd), unpack(sm), unpack(sv_)
    res = {k: dict(g=ug[k], d=ud[k], m=um[k], v=uv[k]) for k in ug}

    dmod_all = small_all[:, 0, :depth * 6 * d].reshape(N_DEV, depth, 6 * d)
    g_ada = None
    for l in range(depth):
        dm = lax.dynamic_slice(dmod_all[:, l, :], (0, me * ada_w), (N_DEV, ada_w))
        dm = jnp.concatenate([dm, jnp.zeros_like(dm)], axis=0).astype(BF16)
        g_ada = _mm(c_pad, dm, ta=True, name="mm_wgrad_ada", stack=(l, depth, g_ada))
    d_ada, m_ada, v_ada = _adamw_local(g_ada, w_ada, m_w_ada, v_w_ada, name="adamw_local")
    res["w_ada"] = dict(g=g_ada, d=d_ada, m=m_ada, v=v_ada)

    for k, (g_, d_, m_, v_) in stacked.items():
        res[k] = dict(g=g_, d=d_, m=m_, v=v_)

    order = ("w_ada", "b_ada", "norm1_g", "norm2_g", "w_in", "qn_g", "kn_g", "w_branch_a", "w_branch_b", "w_out",
             "w_gate_up", "w_down")
    return (loss, grad_x, *[res[k]["g"] for k in order], *[res[k]["d"] for k in order],
            *[res[k]["m"] for k in order], *[res[k]["v"] for k in order])
```

```python
import functools

import jax
import jax.numpy as jnp
from jax import lax
from jax.experimental import pallas as pl
from jax.experimental.pallas import tpu as pltpu

F32 = jnp.float32
BF16 = jnp.bfloat16

HEAD_DIM = 128
BLOCK = 128
DILATIONS = (1, 4, 16)
HEADS_PER_GROUP = 4
A_HEADS = 12
SB_HEADS = 4
GROUP_W = HEADS_PER_GROUP * HEAD_DIM
A_W = A_HEADS * HEAD_DIM
B_W = SB_HEADS * HEAD_DIM
OFF_QA, OFF_KA, OFF_VA = 0, A_W, 2 * A_W
OFF_QB, OFF_KB, OFF_VB = 3 * A_W, 3 * A_W + B_W, 3 * A_W + 2 * B_W
OFF_GATES = 3 * A_W + 3 * B_W
ROPE_THETA = 10000.0
EPS = 1e-6
ATT_SCALE = HEAD_DIM ** -0.5
MASKED = -1e30

ADAM_LR, ADAM_B1, ADAM_B2, ADAM_EPS, ADAM_WD, ADAM_STEP = 0.001, 0.9, 0.999, 1e-08, 0.01, 10

N_DEV = 8
N_CHIPS = 4
V7X_VMEM_LIMIT_BYTES = 56 * 1024 * 1024
ELEMWISE_BLOCK_BYTES = 2 * 1024 * 1024
MESH = pl.DeviceIdType.MESH

NN = (((1,), (0,)), ((), ()))
NT = (((1,), (1,)), ((), ()))
TN = (((0,), (0,)), ((), ()))


def _dot(a, b, dims=NN):
    return lax.dot_general(a, b, dims, preferred_element_type=F32)


def _tile(n, cap, mult=128):
    best = None
    for t in range(mult, min(n, cap) + 1, mult):
        if n % t == 0:
            best = t
    if best is None:
        assert n <= 2 * cap, (n, cap)
        return n
    return best


def _rows(r, c):
    return _tile(r, max(16, ELEMWISE_BLOCK_BYTES // (4 * c)), 16)


def _pcall(body, *, name, out_shape, grid=None, in_specs=None, out_specs=None, scratch=(), aliases=None,
           grid_spec=None, deps=()):
    kwargs = {}
    deps = list(deps)
    if grid_spec is not None:
        assert not deps
        kwargs["grid_spec"] = grid_spec
    else:
        if grid is not None:
            kwargs["grid"] = grid
        n_in = len(in_specs)
        kwargs["in_specs"] = list(in_specs) + [pl.BlockSpec(memory_space=pl.ANY)] * len(deps)
        kwargs["out_specs"] = out_specs
        kwargs["scratch_shapes"] = list(scratch)
        if deps:
            inner = body

            def body(*refs):
                return inner(*refs[:n_in], *refs[n_in + len(deps):])

    call = pl.pallas_call(
        body, name=name, out_shape=out_shape, input_output_aliases=aliases or {},
        compiler_params=pltpu.CompilerParams(vmem_limit_bytes=V7X_VMEM_LIMIT_BYTES), **kwargs)
    return (lambda *args: call(*args, *deps)) if deps else call


def _mm(a, b, *, name, ta=False, tb=False, out_dtype=F32, caps=(1024, 1024, 2048), stack=None, deps=()):
    kdim, m = a.shape if ta else a.shape[::-1]
    n, k2 = b.shape if tb else b.shape[::-1]
    assert kdim == k2, (a.shape, b.shape, ta, tb)
    tm, tn, tk = _tile(m, caps[0]), _tile(n, caps[1]), _tile(kdim, caps[2])
    nk = kdim // tk
    dims = (((0 if ta else 1,), (1 if tb else 0,)), ((), ()))

    def body(*refs):
        a_ref, b_ref = refs[0], refs[1]
        part = _dot(a_ref[...].astype(BF16), b_ref[...].astype(BF16), dims)
        if nk == 1:
            o_ref = refs[-1]
            o_ref[...] = part.astype(o_ref.dtype)
            return
        o_ref, acc_ref = refs[-2], refs[-1]
        k = pl.program_id(2)

        @pl.when(k == 0)
        def _():
            acc_ref[...] = part

        @pl.when(k > 0)
        def _():
            acc_ref[...] += part

        @pl.when(k == nk - 1)
        def _():
            o_ref[...] = acc_ref[...].astype(o_ref.dtype)

    a_spec = (pl.BlockSpec((tk, tm), lambda i, j, k: (k, i)) if ta
              else pl.BlockSpec((tm, tk), lambda i, j, k: (i, k)))
    b_spec = (pl.BlockSpec((tn, tk), lambda i, j, k: (j, k)) if tb
              else pl.BlockSpec((tk, tn), lambda i, j, k: (k, j)))
    ins, in_specs, aliases = [a, b], [a_spec, b_spec], {}
    if stack is None:
        out_shape = jax.ShapeDtypeStruct((m, n), out_dtype)
        out_spec = pl.BlockSpec((tm, tn), lambda i, j, k: (i, j))
    else:
        layer, n_layers, buf = stack
        out_shape = jax.ShapeDtypeStruct((n_layers, m, n), out_dtype)
        out_spec = pl.BlockSpec((None, tm, tn), lambda i, j, k: (layer, i, j))
        if buf is not None:
            ins.append(buf)
            in_specs.append(pl.BlockSpec(memory_space=pl.ANY))
            aliases = {2: 0}
    scratch = [] if nk == 1 else [pltpu.VMEM((tm, tn), F32)]
    return _pcall(body, name=name, out_shape=out_shape, grid=(m // tm, n // tn, nk), in_specs=in_specs,
                  out_specs=out_spec, scratch=scratch, aliases=aliases, deps=deps)(*ins)


def _rmsmod_fwd(h, g, scale, shift, *, name, deps=()):
    s, d = h.shape
    ts = _rows(s, d)

    def body(h_ref, g_ref, sc_ref, sh_ref, u_ref):
        hf = h_ref[...]
        r = lax.rsqrt(jnp.mean(hf * hf, axis=-1, keepdims=True) + EPS)
        u_ref[...] = (((hf * r) * g_ref[...]) * (1.0 + sc_ref[...]) + sh_ref[...]).astype(BF16)

    row = pl.BlockSpec((ts, d), lambda i: (i, 0))
    vec = pl.BlockSpec((1, d), lambda i: (0, 0))
    return _pcall(body, name=name, out_shape=jax.ShapeDtypeStruct((s, d), BF16), grid=(s // ts,),
                  in_specs=[row, vec, vec, vec], out_specs=row, deps=deps)(h, g, scale, shift)


def _rmsmod_bwd(du, h, g, scale, dres, *, name):
    s, d = h.shape
    ts = _rows(s, d)

    def body(du_ref, h_ref, g_ref, sc_ref, dres_ref, dh_ref, dsh_ref, dsc_ref, dg_ref):
        @pl.when(pl.program_id(0) == 0)
        def _():
            dsh_ref[...] = jnp.zeros_like(dsh_ref)
            dsc_ref[...] = jnp.zeros_like(dsc_ref)
            dg_ref[...] = jnp.zeros_like(dg_ref)

        hf, duf, gain = h_ref[...], du_ref[...], g_ref[...]
        r = lax.rsqrt(jnp.mean(hf * hf, axis=-1, keepdims=True) + EPS)
        xh = hf * r
        dn = duf * (1.0 + sc_ref[...])
        dsh_ref[...] += jnp.sum(duf, axis=0, keepdims=True)
        dsc_ref[...] += jnp.sum(duf * (xh * gain), axis=0, keepdims=True)
        dg_ref[...] += jnp.sum(dn * xh, axis=0, keepdims=True)
        dxh = dn * gain
        dh_ref[...] = dres_ref[...] + r * (dxh - xh * jnp.mean(dxh * xh, axis=-1, keepdims=True))

    row = pl.BlockSpec((ts, d), lambda i: (i, 0))
    vec = pl.BlockSpec((1, d), lambda i: (0, 0))
    vshape = jax.ShapeDtypeStruct((1, d), F32)
    return _pcall(body, name=name, out_shape=(jax.ShapeDtypeStruct((s, d), F32), vshape, vshape, vshape),
                  grid=(s // ts,), in_specs=[row, row, vec, vec, row],
                  out_specs=(row, vec, vec, vec))(du, h, g, scale, dres)


def _resid_gate(h, gate, t, *, name):
    s, d = h.shape
    ts = _rows(s, d)

    def body(h_ref, g_ref, t_ref, o_ref):
        o_ref[...] = h_ref[...] + g_ref[...] * t_ref[...]

    row = pl.BlockSpec((ts, d), lambda i: (i, 0))
    vec = pl.BlockSpec((1, d), lambda i: (0, 0))
    return _pcall(body, name=name, out_shape=jax.ShapeDtypeStruct((s, d), F32), grid=(s // ts,),
                  in_specs=[row, vec, row], out_specs=row)(h, gate, t)


def _resid_gate_bwd(dh, t, gate, *, name, deps=()):
    s, d = dh.shape
    ts = _rows(s, d)

    def body(dh_ref, t_ref, g_ref, dt_ref, dg_ref):
        @pl.when(pl.program_id(0) == 0)
        def _():
            dg_ref[...] = jnp.zeros_like(dg_ref)

        dhf = dh_ref[...]
        dt_ref[...] = (dhf * g_ref[...]).astype(BF16)
        dg_ref[...] += jnp.sum(dhf * t_ref[...], axis=0, keepdims=True)

    row = pl.BlockSpec((ts, d), lambda i: (i, 0))
    vec = pl.BlockSpec((1, d), lambda i: (0, 0))
    return _pcall(body, name=name,
                  out_shape=(jax.ShapeDtypeStruct((s, d), BF16), jax.ShapeDtypeStruct((1, d), F32)),
                  grid=(s // ts,), in_specs=[row, row, vec], out_specs=(row, vec), deps=deps)(dh, t, gate)


def _merge_fwd(proj, y_a, y_b, *, name):
    s, d = y_a.shape
    ts = _rows(s, d)
    ga_blk = OFF_GATES // d

    def body(ga_ref, gb_ref, ya_ref, yb_ref, o_ref):
        o_ref[...] = (jax.nn.sigmoid(ga_ref[...]) * ya_ref[...]
                      + jax.nn.sigmoid(gb_ref[...]) * yb_ref[...]).astype(BF16)

    row = pl.BlockSpec((ts, d), lambda i: (i, 0))
    ga = pl.BlockSpec((ts, d), lambda i: (i, ga_blk))
    gb = pl.BlockSpec((ts, d), lambda i: (i, ga_blk + 1))
    return _pcall(body, name=name, out_shape=jax.ShapeDtypeStruct((s, d), BF16), grid=(s // ts,),
                  in_specs=[ga, gb, row, row], out_specs=row)(proj, proj, y_a, y_b)


def _merge_bwd(dm, proj, y_a, y_b, *, name):
    s, d = y_a.shape
    ts = _rows(s, d)
    ga_blk = OFF_GATES // d

    def body(dm_ref, ga_ref, gb_ref, ya_ref, yb_ref, dya_ref, dyb_ref, dga_ref, dgb_ref):
        dmf = dm_ref[...]
        sa, sb = jax.nn.sigmoid(ga_ref[...]), jax.nn.sigmoid(gb_ref[...])
        dya_ref[...] = (dmf * sa).astype(BF16)
        dyb_ref[...] = (dmf * sb).astype(BF16)
        dga_ref[...] = (dmf * ya_ref[...] * (sa * (1.0 - sa))).astype(BF16)
        dgb_ref[...] = (dmf * yb_ref[...] * (sb * (1.0 - sb))).astype(BF16)

    row = pl.BlockSpec((ts, d), lambda i: (i, 0))
    ga = pl.BlockSpec((ts, d), lambda i: (i, ga_blk))
    gb = pl.BlockSpec((ts, d), lambda i: (i, ga_blk + 1))
    shp = jax.ShapeDtypeStruct((s, d), BF16)
    return _pcall(body, name=name, out_shape=(shp, shp, shp, shp), grid=(s // ts,),
                  in_specs=[row, ga, gb, row, row], out_specs=(row, row, row, row))(dm, proj, proj, y_a, y_b)


def _mm_swiglu(u2, w_gate_up, *, name):
    s, d = u2.shape
    f = w_gate_up.shape[1] // 2
    tm, tn = _tile(s, 1024), _tile(f, 512)
    nj = f // tn

    def body(x_ref, wg_ref, wu_ref, a_ref, g_ref, u_ref):
        x = x_ref[...]
        gf, uf = _dot(x, wg_ref[...]), _dot(x, wu_ref[...])
        a_ref[...] = ((gf * jax.nn.sigmoid(gf)) * uf).astype(BF16)
        g_ref[...] = gf.astype(BF16)
        u_ref[...] = uf.astype(BF16)

    out = pl.BlockSpec((tm, tn), lambda i, j: (i, j))
    shp = jax.ShapeDtypeStruct((s, f), BF16)
    return _pcall(body, name=name, out_shape=(shp, shp, shp), grid=(s // tm, nj),
                  in_specs=[pl.BlockSpec((tm, d), lambda i, j: (i, 0)), pl.BlockSpec((d, tn), lambda i, j: (0, j)),
                            pl.BlockSpec((d, tn), lambda i, j: (0, nj + j))],
                  out_specs=(out, out, out))(u2, w_gate_up, w_gate_up)


def _swiglu_bwd(g, u, da, *, name):
    s, f = g.shape
    ts = _rows(s, f)

    def body(g_ref, u_ref, da_ref, o_ref):
        gf, daf = g_ref[...].astype(F32), da_ref[...].astype(F32)
        sg = jax.nn.sigmoid(gf)
        o_ref[:, :f] = (daf * u_ref[...].astype(F32) * (sg * (1.0 + gf * (1.0 - sg)))).astype(BF16)
        o_ref[:, f:] = (daf * (gf * sg)).astype(BF16)

    row = pl.BlockSpec((ts, f), lambda i: (i, 0))
    return _pcall(body, name=name, out_shape=jax.ShapeDtypeStruct((s, 2 * f), BF16), grid=(s // ts,),
                  in_specs=[row, row, row], out_specs=pl.BlockSpec((ts, 2 * f), lambda i: (i, 0)))(g, u, da)


def _loss_fwd(y, tgt, *, name):
    s, d = y.shape
    ts = _rows(s, d)

    def body(y_ref, t_ref, l_ref, dy_ref):
        @pl.when(pl.program_id(0) == 0)
        def _():
            l_ref[...] = jnp.zeros_like(l_ref)

        e = y_ref[...] - t_ref[...]
        dy_ref[...] = e * (1.0 / d)
        per_tok = jnp.sum(e * e, axis=1, keepdims=True) * (1.0 / d)
        l_ref[...] += 0.5 * jnp.sum(per_tok, axis=0, keepdims=True)

    row = pl.BlockSpec((ts, d), lambda i: (i, 0))
    return _pcall(body, name=name,
                  out_shape=(jax.ShapeDtypeStruct((1, 128), F32), jax.ShapeDtypeStruct((s, d), F32)),
                  grid=(s // ts,), in_specs=[row, row],
                  out_specs=(pl.BlockSpec((1, 128), lambda i: (0, 0)), row))(y, tgt)


def _rope_tables(seq):
    inv = jnp.power(ROPE_THETA, -jnp.arange(0, HEAD_DIM, 2, dtype=F32) / HEAD_DIM)
    ang = jnp.arange(seq, dtype=F32)[:, None] * inv[None, :]
    cos, sin = jnp.cos(ang), jnp.sin(ang)
    return jnp.concatenate([cos, cos], axis=1), jnp.concatenate([-sin, sin], axis=1)


def _qkrope_fwd(proj, gains, cos2, sin2, *, name):
    s = proj.shape[0]
    ts = _rows(s, A_W)

    def body(x_ref, g_ref, c_ref, s_ref, o_ref):
        gain, cos, sin = g_ref[...], c_ref[...], s_ref[...]
        for h in range(A_HEADS):
            lanes = slice(h * HEAD_DIM, (h + 1) * HEAD_DIM)
            x = x_ref[:, lanes]
            y = (x * lax.rsqrt(jnp.mean(x * x, axis=-1, keepdims=True) + EPS)) * gain
            o_ref[:, lanes] = (y * cos + pltpu.roll(y, HEAD_DIM // 2, 1) * sin).astype(BF16)

    heads = pl.BlockSpec((ts, A_W), lambda i, j: (i, j))
    tab = pl.BlockSpec((ts, HEAD_DIM), lambda i, j: (i, 0))
    gain = pl.BlockSpec((None, 1, HEAD_DIM), lambda i, j: (j, 0, 0))
    return _pcall(body, name=name, out_shape=jax.ShapeDtypeStruct((s, 2 * A_W), BF16),
                  grid=(s // ts, 2), in_specs=[heads, gain, tab, tab], out_specs=heads)(
                      proj, gains, cos2, sin2)


def _qkrope_bwd(dqk, proj, gains, cos2, sin2, *, name):
    s = proj.shape[0]
    ts = _rows(s, A_W)

    def body(d_ref, x_ref, g_ref, c_ref, s_ref, dx_ref, dg_ref):
        @pl.when(pl.program_id(1) == 0)
        def _():
            dg_ref[...] = jnp.zeros_like(dg_ref)

        gain, cos, sin = g_ref[...], c_ref[...], s_ref[...]
        dg = jnp.zeros((1, HEAD_DIM), F32)
        for h in range(A_HEADS):
            lanes = slice(h * HEAD_DIM, (h + 1) * HEAD_DIM)
            dout = d_ref[:, lanes]
            dy = dout * cos + pltpu.roll(dout * sin, HEAD_DIM // 2, 1)
            x = x_ref[:, lanes]
            r = lax.rsqrt(jnp.mean(x * x, axis=-1, keepdims=True) + EPS)
            xh = x * r
            dg = dg + jnp.sum(dy * xh, axis=0, keepdims=True)
            dxh = dy * gain
            dx_ref[:, lanes] = (r * (dxh - xh * jnp.mean(dxh * xh, axis=-1, keepdims=True))).astype(BF16)
        dg_ref[...] += dg

    heads = pl.BlockSpec((ts, A_W), lambda j, i: (i, j))
    tab = pl.BlockSpec((ts, HEAD_DIM), lambda j, i: (i, 0))
    gain = pl.BlockSpec((None, 1, HEAD_DIM), lambda j, i: (j, 0, 0))
    return _pcall(body, name=name,
                  out_shape=(jax.ShapeDtypeStruct((s, 2 * A_W), BF16), jax.ShapeDtypeStruct((2, 1, HEAD_DIM), F32)),
                  grid=(2, s // ts), in_specs=[heads, heads, gain, tab, tab],
                  out_specs=(heads, gain))(dqk, proj, gains, cos2, sin2)


def _block_rows(blk):
    if isinstance(blk, int):
        return pl.ds(blk * BLOCK, BLOCK)
    return pl.ds(pl.multiple_of(blk * BLOCK, BLOCK), BLOCK)


def _band_masks(n, with_prev):
    row = lax.broadcasted_iota(jnp.int32, (BLOCK, BLOCK), 0)
    col = lax.broadcasted_iota(jnp.int32, (BLOCK, BLOCK), 1)
    cur = col <= row
    if not with_prev:
        return [(n, cur)]
    prev = col >= row + jnp.where(n >= 1, 0, BLOCK)
    return [(n, cur), (jnp.maximum(n - 1, 0), prev)]


def _dil_fwd(q_arr, k_arr, v_arr, offs, length, dil, *, name):
    nj, nb = dil * HEADS_PER_GROUP, length // BLOCK
    ju = HEADS_PER_GROUP
    qo, ko, vo = (off // ju for off in offs)
    assert all(off % ju == 0 for off in offs)

    def body(q_ref, k_ref, v_ref, o_ref, l_ref):
        n = pl.program_id(1)
        masks = _band_masks(n, nb > 1)
        for cb in range(ju):
            lanes = slice(cb * HEAD_DIM, (cb + 1) * HEAD_DIM)
            q = q_ref[:, lanes].astype(BF16)
            parts = []
            for blk, mask in masks:
                rows = _block_rows(blk)
                sc = _dot(q, k_ref[rows, lanes].astype(BF16), NT) * ATT_SCALE
                parts.append((jnp.where(mask, sc, MASKED), rows))
            m = parts[0][0].max(axis=-1, keepdims=True)
            for sc, _ in parts[1:]:
                m = jnp.maximum(m, sc.max(axis=-1, keepdims=True))
            den = jnp.zeros((BLOCK, 1), F32)
            acc = jnp.zeros((BLOCK, HEAD_DIM), F32)
            for sc, rows in parts:
                p = jnp.exp(sc - m)
                den = den + jnp.sum(p, axis=-1, keepdims=True)
                acc = acc + _dot(p.astype(BF16), v_ref[rows, lanes].astype(BF16))
            o_ref[:, lanes] = acc / den
            l_ref[:, lanes] = jnp.broadcast_to(m + jnp.log(den), (BLOCK, HEAD_DIM))

    qspec = pl.BlockSpec((BLOCK, ju * HEAD_DIM), lambda j, n: (n, qo + j))
    kspec = pl.BlockSpec((length, ju * HEAD_DIM), lambda j, n: (0, ko + j))
    vspec = pl.BlockSpec((length, ju * HEAD_DIM), lambda j, n: (0, vo + j))
    ospec = pl.BlockSpec((BLOCK, ju * HEAD_DIM), lambda j, n: (n, j))
    shp = jax.ShapeDtypeStruct((length, nj * HEAD_DIM), F32)
    return _pcall(body, name=name, out_shape=(shp, shp), grid=(nj // ju, nb), in_specs=[qspec, kspec, vspec],
                  out_specs=(ospec, ospec))(q_arr, k_arr, v_arr)


def _dil_bwd(q_arr, k_arr, v_arr, offs, o, lse, do, dlse, length, dil, *, name):
    nj, nb = dil * HEADS_PER_GROUP, length // BLOCK
    ju = HEADS_PER_GROUP if length <= 4 * BLOCK else 2
    qo, ko, vo = (off // ju for off in offs)
    assert all(off % ju == 0 for off in offs)

    def body(q_ref, k_ref, v_ref, o_ref, l_ref, do_ref, dl_ref, dq_ref, dk_ref, dv_ref):
        dk_ref[...] = jnp.zeros_like(dk_ref)
        dv_ref[...] = jnp.zeros_like(dv_ref)

        def step(n, carry):
            qrows = _block_rows(n)
            masks = _band_masks(n, nb > 1)
            for cb in range(ju):
                lanes = slice(cb * HEAD_DIM, (cb + 1) * HEAD_DIM)
                q = q_ref[qrows, lanes].astype(BF16)
                dof = do_ref[qrows, lanes]
                dob = dof.astype(BF16)
                lse_b = l_ref[qrows, lanes]
                shift = dl_ref[qrows, lanes] - jnp.sum(dof * o_ref[qrows, lanes], axis=-1, keepdims=True)
                dq = jnp.zeros((BLOCK, HEAD_DIM), F32)
                for blk, mask in masks:
                    rows = _block_rows(blk)
                    kk, vv = k_ref[rows, lanes].astype(BF16), v_ref[rows, lanes].astype(BF16)
                    sc = _dot(q, kk, NT) * ATT_SCALE
                    p = jnp.where(mask, jnp.exp(sc - lse_b), 0.0)
                    ds = (p * (_dot(dob, vv, NT) + shift)).astype(BF16)
                    dq = dq + _dot(ds, kk)
                    dk_ref[rows, lanes] += _dot(ds, q, TN) * ATT_SCALE
                    dv_ref[rows, lanes] += _dot(p.astype(BF16), dob, TN)
                dq_ref[qrows, lanes] = dq * ATT_SCALE
            return carry

        if nb == 1:
            step(0, 0)
        else:
            lax.fori_loop(0, nb, step, 0)

    def col(off):
        return pl.BlockSpec((length, ju * HEAD_DIM), lambda j: (0, off + j))

    shp = jax.ShapeDtypeStruct((length, nj * HEAD_DIM), F32)
    return _pcall(body, name=name, out_shape=(shp, shp, shp), grid=(nj // ju,),
                  in_specs=[col(qo), col(ko), col(vo), col(0), col(0), col(0), col(0)],
                  out_specs=(col(0), col(0), col(0)))(q_arr, k_arr, v_arr, o, lse, do, dlse)


def _combine_weights(l_refs):
    ls = [r[...] for r in l_refs]
    m = jnp.maximum(jnp.maximum(ls[0], ls[1]), ls[2])
    es = [jnp.exp(l - m) for l in ls]
    den = es[0] + es[1] + es[2]
    return [e / den for e in es]


def _combine_fwd(os_, lses, *, name):
    s = os_[0].shape[0]
    ts = _rows(s, GROUP_W)

    def body(o0, o1, o2, l0, l1, l2, out_ref):
        w = _combine_weights((l0, l1, l2))
        out_ref[...] = (w[0] * o0[...] + w[1] * o1[...] + w[2] * o2[...]).astype(BF16)

    row = pl.BlockSpec((ts, GROUP_W), lambda i: (i, 0))
    return _pcall(body, name=name, out_shape=jax.ShapeDtypeStruct((s, GROUP_W), BF16), grid=(s // ts,),
                  in_specs=[row] * 6, out_specs=row)(*os_, *lses)


def _combine_bwd(do_a, os_, lses, *, name):
    s = do_a.shape[0]
    ts = _rows(s, GROUP_W)

    def body(d_ref, o0, o1, o2, l0, l1, l2, do0, do1, do2, dl0, dl1, dl2):
        w = _combine_weights((l0, l1, l2))
        d = d_ref[...]
        og = [o0[...], o1[...], o2[...]]
        oa = w[0] * og[0] + w[1] * og[1] + w[2] * og[2]
        ta = jnp.sum(d * oa, axis=-1, keepdims=True)
        for g, (do_ref, dl_ref) in enumerate(((do0, dl0), (do1, dl1), (do2, dl2))):
            do_ref[...] = w[g] * d
            dl_ref[...] = w[g] * (jnp.sum(d * og[g], axis=-1, keepdims=True) - ta)

    head = pl.BlockSpec((ts, HEAD_DIM), lambda i, h: (i, h))
    shp = jax.ShapeDtypeStruct((s, GROUP_W), F32)
    return _pcall(body, name=name, out_shape=(shp,) * 6, grid=(s // ts, HEADS_PER_GROUP),
                  in_specs=[head] * 7, out_specs=(head,) * 6)(do_a, *os_, *lses)


def _dot_exact(x, ones_mask):
    hi = x.astype(BF16)
    r1 = x - hi.astype(F32)
    mid = r1.astype(BF16)
    lo = (r1 - mid.astype(F32)).astype(BF16)
    return _dot(hi, ones_mask) + _dot(mid, ones_mask) + _dot(lo, ones_mask)


SB_QROWS = 2 * BLOCK
SB_UNROLL = 4


def _sb_mask(j, i):
    row = lax.broadcasted_iota(jnp.int32, (SB_QROWS, BLOCK), 0)
    col = lax.broadcasted_iota(jnp.int32, (SB_QROWS, BLOCK), 1)
    return col + (j * BLOCK - i * SB_QROWS) < row


def _sb_steps(i):
    return ((i + 1) * (SB_QROWS // BLOCK) + SB_UNROLL - 1) // SB_UNROLL


def _sb_scores(q, kk, j, i):
    mask = _sb_mask(j, i)
    z = _dot(q, kk, NT) * ATT_SCALE
    sp = jnp.log1p(jnp.exp(-jnp.abs(z)))
    log_beta = jnp.minimum(z, 0.0) - sp
    log_1mb = jnp.where(mask, jnp.minimum(-z, 0.0) - sp, 0.0)
    return z, log_beta, log_1mb, mask


def _sb_weights(log_beta, log_1mb, mask, run, upper):
    after = run + _dot_exact(log_1mb, upper)
    return jnp.where(mask, jnp.exp(log_beta + after), 0.0)


def _tri(strict_lower):
    row = lax.broadcasted_iota(jnp.int32, (BLOCK, BLOCK), 0)
    col = lax.broadcasted_iota(jnp.int32, (BLOCK, BLOCK), 1)
    return ((row > col) if strict_lower else (row < col)).astype(BF16)


def _sb_fwd(proj, *, name):
    s = proj.shape[0]
    assert s % (BLOCK * SB_UNROLL) == 0 and s % SB_QROWS == 0
    qb, kb, vb = OFF_QB // HEAD_DIM, OFF_KB // HEAD_DIM, OFF_VB // HEAD_DIM

    def body(q_ref, k_ref, v_ref, o_ref):
        i = pl.program_id(1)
        q = q_ref[...].astype(BF16)
        upper = _tri(True)
        nsteps = _sb_steps(i)

        def step(t, carry):
            acc, run = carry
            for b in reversed(range(SB_UNROLL)):
                j = (nsteps - 1 - t) * SB_UNROLL + b
                rows = _block_rows(j)
                _, log_beta, log_1mb, mask = _sb_scores(q, k_ref[rows, :].astype(BF16), j, i)
                a = _sb_weights(log_beta, log_1mb, mask, run, upper)
                acc = acc + _dot(a.astype(BF16), v_ref[rows, :].astype(BF16))
                run = run + jnp.sum(log_1mb, axis=-1, keepdims=True)
            return acc, run

        acc, _ = lax.fori_loop(0, nsteps, step,
                               (jnp.zeros((SB_QROWS, HEAD_DIM), F32), jnp.zeros((SB_QROWS, 1), F32)))
        o_ref[...] = acc.astype(BF16)

    return _pcall(body, name=name, out_shape=jax.ShapeDtypeStruct((s, B_W), BF16), grid=(SB_HEADS, s // SB_QROWS),
                  in_specs=[pl.BlockSpec((SB_QROWS, HEAD_DIM), lambda h, i: (i, qb + h)),
                            pl.BlockSpec((s, HEAD_DIM), lambda h, i: (0, kb + h)),
                            pl.BlockSpec((s, HEAD_DIM), lambda h, i: (0, vb + h))],
                  out_specs=pl.BlockSpec((SB_QROWS, HEAD_DIM), lambda h, i: (i, h)))(proj, proj, proj)


def _sb_bwd(proj, do_b, *, name):
    s = proj.shape[0]
    assert s % (BLOCK * SB_UNROLL) == 0 and s % SB_QROWS == 0
    nkb = s // BLOCK
    qb, kb, vb = OFF_QB // HEAD_DIM, OFF_KB // HEAD_DIM, OFF_VB // HEAD_DIM

    def body(q_ref, k_ref, v_ref, do_ref, dq_ref, dk_ref, dv_ref, z_s, a_s):
        i = pl.program_id(1)

        @pl.when(i == 0)
        def _():
            dk_ref[...] = jnp.zeros_like(dk_ref)
            dv_ref[...] = jnp.zeros_like(dv_ref)

        q = q_ref[...].astype(BF16)
        dob = do_ref[...].astype(BF16)
        upper, lower = _tri(True), _tri(False)
        nsteps = _sb_steps(i)

        def recompute(t, run):
            for b in reversed(range(SB_UNROLL)):
                j = (nsteps - 1 - t) * SB_UNROLL + b
                z, log_beta, log_1mb, mask = _sb_scores(q, k_ref[_block_rows(j), :].astype(BF16), j, i)
                z_s[j] = z
                a_s[j] = _sb_weights(log_beta, log_1mb, mask, run, upper)
                run = run + jnp.sum(log_1mb, axis=-1, keepdims=True)
            return run

        lax.fori_loop(0, nsteps, recompute, jnp.zeros((SB_QROWS, 1), F32))

        def grads(t, carry):
            dq, run = carry
            for b in range(SB_UNROLL):
                j = t * SB_UNROLL + b
                rows = _block_rows(j)
                kk, vv = k_ref[rows, :].astype(BF16), v_ref[rows, :].astype(BF16)
                z, a = z_s[j], a_s[j]
                de = _dot(dob, vv, NT) * a
                before = run + _dot_exact(de, lower)
                dz = (de * jax.nn.sigmoid(-z)
                      - jnp.where(_sb_mask(j, i), jax.nn.sigmoid(z), 0.0) * before).astype(BF16)
                dk_ref[rows, :] += _dot(dz, q, TN) * ATT_SCALE
                dv_ref[rows, :] += _dot(a.astype(BF16), dob, TN)
                dq = dq + _dot(dz, kk)
                run = run + jnp.sum(de, axis=-1, keepdims=True)
            return dq, run

        dq, _ = lax.fori_loop(0, nsteps, grads,
                              (jnp.zeros((SB_QROWS, HEAD_DIM), F32), jnp.zeros((SB_QROWS, 1), F32)))
        dq_ref[...] = dq * ATT_SCALE

    blk = pl.BlockSpec((SB_QROWS, HEAD_DIM), lambda h, i: (i, h))
    full = pl.BlockSpec((s, HEAD_DIM), lambda h, i: (0, h))
    shp = jax.ShapeDtypeStruct((s, B_W), F32)
    return _pcall(body, name=name, out_shape=(shp, shp, shp), grid=(SB_HEADS, s // SB_QROWS),
                  in_specs=[pl.BlockSpec((SB_QROWS, HEAD_DIM), lambda h, i: (i, qb + h)),
                            pl.BlockSpec((s, HEAD_DIM), lambda h, i: (0, kb + h)),
                            pl.BlockSpec((s, HEAD_DIM), lambda h, i: (0, vb + h)), blk],
                  out_specs=(blk, full, full),
                  scratch=[pltpu.VMEM((nkb, SB_QROWS, BLOCK), F32), pltpu.VMEM((nkb, SB_QROWS, BLOCK), F32)])(
                      proj, proj, proj, do_b)


def _coords():
    return lax.axis_index("x"), lax.axis_index("y"), lax.axis_index("c")


def _flip(v, bit):
    return 1 - v if bit else v


def _shard_of(ref, axis, idx, size):
    if axis == 0:
        sl = pl.ds(pl.multiple_of(idx * size, 16), size)
        return ref.at[sl, :] if len(ref.shape) == 2 else ref.at[:, sl, :]
    sl = pl.ds(pl.multiple_of(idx * size, 128), size)
    return ref.at[:, sl] if len(ref.shape) == 2 else ref.at[:, :, sl]


def _small_allgather(v, *, name, silu=False, deps=()):
    n = v.shape[1]

    def body(v_ref, out_ref, send_sems, recv_sems):
        x, y, c = _coords()
        me = 4 * x + 2 * y + c
        val = v_ref[...]
        out_ref[me] = val * jax.nn.sigmoid(val) if silu else val
        copies = []
        for k in range(1, N_DEV):
            peer = (_flip(x, k & 4), _flip(y, k & 2), _flip(c, k & 1))
            copies.append(pltpu.make_async_remote_copy(
                src_ref=out_ref.at[me], dst_ref=out_ref.at[me], send_sem=send_sems.at[k - 1],
                recv_sem=recv_sems.at[k - 1], device_id=peer, device_id_type=MESH))
        for cp in copies:
            cp.start()
        for cp in copies:
            cp.wait_recv()
        for cp in copies:
            cp.wait_send()

    return _pcall(body, name=name, out_shape=jax.ShapeDtypeStruct((N_DEV, 1, n), F32),
                  in_specs=[pl.BlockSpec(memory_space=pltpu.VMEM)], out_specs=pl.BlockSpec(memory_space=pltpu.VMEM),
                  scratch=[pltpu.SemaphoreType.DMA((N_DEV - 1,)), pltpu.SemaphoreType.DMA((N_DEV - 1,))],
                  deps=deps)(v)


def _cast_place(w, layer, axis, me, *, name):
    _, r, c = w.shape
    tr = _rows(r, c)
    nrt = r // tr

    def body(me_ref, w_ref, o_ref):
        o_ref[...] = w_ref[...].astype(BF16)

    wspec = pl.BlockSpec((None, tr, c), lambda i, me_ref: (layer, i, 0))
    if axis == 0:
        ospec = pl.BlockSpec((tr, c), lambda i, me_ref: (me_ref[0] * nrt + i, 0))
        shape = (r * N_DEV, c)
    else:
        ospec = pl.BlockSpec((tr, c), lambda i, me_ref: (i, me_ref[0]))
        shape = (r, c * N_DEV)
    grid_spec = pltpu.PrefetchScalarGridSpec(num_scalar_prefetch=1, grid=(nrt,), in_specs=[wspec], out_specs=ospec)
    return _pcall(body, name=name, out_shape=jax.ShapeDtypeStruct(shape, BF16), grid_spec=grid_spec)(me, w)


def _gather_weights(fulls, axes, *, name, deps=()):
    nt = len(fulls)
    sizes = [f.shape[ax] // N_DEV for f, ax in zip(fulls, axes)]

    def body(*refs):
        outs = refs[nt:2 * nt]
        send_sems, recv_sems = refs[2 * nt:]
        x, y, c = _coords()
        me, sibling = (x, y, c), (x, y, 1 - c)
        chips = [(1 - x, y), (x, 1 - y), (1 - x, 1 - y)]

        def slot(t, dev):
            return _shard_of(outs[t], axes[t], 4 * dev[0] + 2 * dev[1] + dev[2], sizes[t])

        def copy(t, k, block, to):
            return pltpu.make_async_remote_copy(
                src_ref=slot(t, block), dst_ref=slot(t, block),
                send_sem=send_sems.at[t, k], recv_sem=recv_sems.at[t, k], device_id=to, device_id_type=MESH)

        first = []
        for t in range(nt):
            first.append(copy(t, 0, me, sibling))
            first += [copy(t, 1 + j, me, (*chip, c)) for j, chip in enumerate(chips)]
        for cp in first:
            cp.start()
        passed = []
        for j, chip in enumerate(chips):
            for t in range(nt):
                copy(t, 1 + j, (*chip, c), me).wait_recv()
                fwd = copy(t, 4 + j, (*chip, c), sibling)
                fwd.start()
                passed.append(fwd)
        for t in range(nt):
            copy(t, 0, sibling, me).wait_recv()
            for j, chip in enumerate(chips):
                copy(t, 4 + j, (*chip, 1 - c), me).wait_recv()
        for cp in first + passed:
            cp.wait_send()

    hbm = pl.BlockSpec(memory_space=pl.ANY)
    return _pcall(body, name=name, out_shape=tuple(jax.ShapeDtypeStruct(f.shape, f.dtype) for f in fulls),
                  in_specs=[hbm] * nt, out_specs=(hbm,) * nt, aliases={t: t for t in range(nt)},
                  scratch=[pltpu.SemaphoreType.DMA((nt, 7)), pltpu.SemaphoreType.DMA((nt, 7))], deps=deps)(*fulls)


def _rs_pair_exchange(grads, axes, *, name):
    nt = len(grads)
    sizes = [g.shape[ax] // N_DEV for g, ax in zip(grads, axes)]

    def recv_shape(g, ax):
        dims = list(g.shape)
        dims[ax] //= N_DEV
        return jax.ShapeDtypeStruct((N_CHIPS, dims[0], dims[1]), g.dtype)

    def body(*refs):
        ins, outs = refs[:nt], refs[nt:2 * nt]
        send_sems, recv_sems = refs[2 * nt:]
        x, y, c = _coords()
        copies = []
        for t in range(nt):
            for q in range(N_CHIPS):
                copies.append(pltpu.make_async_remote_copy(
                    src_ref=_shard_of(ins[t], axes[t], 2 * q + 1 - c, sizes[t]), dst_ref=outs[t].at[q],
                    send_sem=send_sems.at[t, q], recv_sem=recv_sems.at[t, q], device_id=(x, y, 1 - c),
                    device_id_type=MESH))
        for cp in copies:
            cp.start()
        for cp in copies:
            cp.wait_recv()
        for cp in copies:
            cp.wait_send()

    hbm = pl.BlockSpec(memory_space=pl.ANY)
    return _pcall(body, name=name, out_shape=tuple(recv_shape(g, ax) for g, ax in zip(grads, axes)),
                  in_specs=[hbm] * nt, out_specs=(hbm,) * nt,
                  scratch=[pltpu.SemaphoreType.DMA((nt, N_CHIPS)), pltpu.SemaphoreType.DMA((nt, N_CHIPS))])(*grads)


def _pair_sum(grad, sib, core, axis, *, name):
    _, r, c = sib.shape
    tr = _rows(r, c)
    nrt = r // tr

    def body(core_ref, g_ref, s_ref, o_ref):
        o_ref[...] = (g_ref[...].astype(F32) + s_ref[...].astype(F32)).astype(BF16)

    if axis == 0:
        gspec = pl.BlockSpec((tr, c), lambda q, i, core_ref: ((2 * q + core_ref[0]) * nrt + i, 0))
    else:
        gspec = pl.BlockSpec((tr, c), lambda q, i, core_ref: (i, 2 * q + core_ref[0]))
    sspec = pl.BlockSpec((None, tr, c), lambda q, i, core_ref: (q, i, 0))
    grid_spec = pltpu.PrefetchScalarGridSpec(num_scalar_prefetch=1, grid=(N_CHIPS, nrt),
                                             in_specs=[gspec, sspec], out_specs=sspec)
    return _pcall(body, name=name, out_shape=jax.ShapeDtypeStruct(sib.shape, BF16), grid_spec=grid_spec)(
        core, grad, sib)


HBM_SPEC = pl.BlockSpec(memory_space=pltpu.HBM)
SEM_SPEC = pl.BlockSpec(memory_space=pltpu.SEMAPHORE)
SPLIT_PARAMS = dict(has_side_effects=pltpu.SideEffectType.DATAFLOW_SIDE_EFFECTING)


def _hbm(a):
    return pltpu.with_memory_space_constraint(a, pltpu.HBM)


def _split_start(copies_fn, buffers, sem_shape, after, *, name):
    n = len(buffers)
    rows, cols = sem_shape
    ns = rows * cols

    def body(*refs):
        sems = refs[n + 1:n + 1 + 2 * ns]
        for cp in copies_fn(refs[:n], _sem_rows(sems[:ns], cols), _sem_rows(sems[ns:], cols)):
            cp.start()
        refs[-1][...] = jnp.zeros_like(refs[-1])

    sem = pltpu.SemaphoreType.DMA(())
    outs = pl.pallas_call(
        body, name=name,
        out_shape=((sem,) * (2 * ns) + tuple(pltpu.HBM(b.shape, b.dtype) for b in buffers)
                   + (jax.ShapeDtypeStruct((8, 128), F32),)),
        in_specs=(HBM_SPEC,) * n + (pl.BlockSpec(memory_space=pl.ANY),),
        out_specs=(SEM_SPEC,) * (2 * ns) + (HBM_SPEC,) * n + (pl.BlockSpec(memory_space=pltpu.VMEM),),
        input_output_aliases={i: 2 * ns + i for i in range(n)},
        compiler_params=pltpu.CompilerParams(**SPLIT_PARAMS))(*[_hbm(b) for b in buffers], after)
    return list(outs[:ns]), list(outs[ns:2 * ns]), list(outs[2 * ns:2 * ns + n]), outs[-1]


def _split_wait(copies_fn, send_sems, recv_sems, buffers, after, sem_rows, *, name):
    n, ns = len(buffers), len(send_sems)
    cols = ns // sem_rows

    def body(*refs):
        sems = refs[n:n + 2 * ns]
        copies = copies_fn(refs[:n], _sem_rows(sems[:ns], cols), _sem_rows(sems[ns:], cols))
        for cp in copies:
            cp.wait_send()
        for cp in copies:
            cp.wait_recv()

    outs = pl.pallas_call(
        body, name=name, out_shape=tuple(pltpu.HBM(b.shape, b.dtype) for b in buffers),
        in_specs=(HBM_SPEC,) * n + (SEM_SPEC,) * (2 * ns) + (pl.BlockSpec(memory_space=pl.ANY),),
        out_specs=(HBM_SPEC,) * n, input_output_aliases={i: i for i in range(n)},
        compiler_params=pltpu.CompilerParams(**SPLIT_PARAMS))(*buffers, *send_sems, *recv_sems, after)
    return list(outs)


def _sem_rows(sems, cols):
    return [sems[i:i + cols] for i in range(0, len(sems), cols)]


def _empty_hbm(shape, dtype):
    return _hbm(lax.empty(shape, dtype))


class _SplitGather:
    def __init__(self, fulls, axes, tag):
        self.axes, self.tag, self.nt = list(axes), tag, len(fulls)
        self.sizes = [f.shape[ax] // N_DEV for f, ax in zip(fulls, axes)]
        self.fulls = list(fulls)

    def _slot(self, ref, t, dev):
        return _shard_of(ref, self.axes[t], 4 * dev[0] + 2 * dev[1] + dev[2], self.sizes[t])

    def _first_copies(self, refs, send_sems, recv_sems):
        x, y, c = _coords()
        peers = [(x, y, 1 - c), (1 - x, y, c), (x, 1 - y, c), (1 - x, 1 - y, c)]
        return [pltpu.make_async_remote_copy(
            src_ref=self._slot(refs[t], t, (x, y, c)), dst_ref=self._slot(refs[t], t, (x, y, c)),
            send_sem=send_sems[t][k], recv_sem=recv_sems[t][k], device_id=peer, device_id_type=MESH)
            for t in range(self.nt) for k, peer in enumerate(peers)]

    def _forward_copies(self, refs, send_sems, recv_sems):
        x, y, c = _coords()
        chips = [(1 - x, y), (x, 1 - y), (1 - x, 1 - y)]
        return [pltpu.make_async_remote_copy(
            src_ref=self._slot(refs[t], t, (*chip, c)), dst_ref=self._slot(refs[t], t, (*chip, c)),
            send_sem=send_sems[t][j], recv_sem=recv_sems[t][j], device_id=(x, y, 1 - c), device_id_type=MESH)
            for t in range(self.nt) for j, chip in enumerate(chips)]

    def first(self, after):
        self.s1, self.r1, self.fulls, token = _split_start(
            self._first_copies, self.fulls, (self.nt, 4), after, name=f"comm_gather1_start_{self.tag}")
        return token

    def forward(self, after):
        bufs = _split_wait(self._first_copies, self.s1, self.r1, self.fulls, after, self.nt,
                           name=f"comm_gather1_wait_{self.tag}")
        self.s2, self.r2, self.fulls, token = _split_start(
            self._forward_copies, bufs, (self.nt, 3), after, name=f"comm_gather2_start_{self.tag}")
        return token

    def finish(self, after):
        return _split_wait(self._forward_copies, self.s2, self.r2, self.fulls, after, self.nt,
                           name=f"comm_gather2_wait_{self.tag}")


class _SplitChipExchange:
    def __init__(self, sums, tag):
        self.nt, self.tag = len(sums), tag
        self.sums = list(sums)

    def _copies(self, refs, send_sems, recv_sems):
        nt = self.nt
        x, y, c = _coords()
        copies = []
        for t in range(nt):
            for k in range(1, N_CHIPS):
                px, py = _flip(x, k & 2), _flip(y, k & 1)
                copies.append(pltpu.make_async_remote_copy(
                    src_ref=refs[t].at[2 * px + py], dst_ref=refs[nt + t].at[k - 1], send_sem=send_sems[t][k - 1],
                    recv_sem=recv_sems[t][k - 1], device_id=(px, py, c), device_id_type=MESH))
        return copies

    def start(self):
        landing = [_empty_hbm((N_CHIPS - 1,) + s.shape[1:], s.dtype) for s in self.sums]
        self.s, self.r, self.bufs, token = _split_start(
            self._copies, self.sums + landing, (self.nt, N_CHIPS - 1), self.sums[-1],
            name=f"comm_rs_chip_start_{self.tag}")
        return token

    def finish(self, after):
        bufs = _split_wait(self._copies, self.s, self.r, self.bufs, after, self.nt,
                           name=f"comm_rs_chip_wait_{self.tag}")
        return bufs[:self.nt], bufs[self.nt:]


def _adam_math(g, w, m, v):
    m2 = ADAM_B1 * m + (1.0 - ADAM_B1) * g
    v2 = ADAM_B2 * v + (1.0 - ADAM_B2) * (g * g)
    m_hat = m2 / (1.0 - ADAM_B1 ** ADAM_STEP)
    v_hat = v2 / (1.0 - ADAM_B2 ** ADAM_STEP)
    delta = -ADAM_LR * (m_hat / (jnp.sqrt(v_hat) + ADAM_EPS) + ADAM_WD * w)
    return delta, m2, v2


def _adamw_sharded(chip_sums, remote, chip, w, m, v, layer, prev, deps, *, name):
    nl, r, c = w.shape
    tr = _rows(r, c)

    def body(*refs):
        p_ref, r0_ref, r1_ref, r2_ref, w_ref, m_ref, v_ref = refs[1:8]
        g_out, d_out, m_out, v_out = refs[-4:]
        g = ((p_ref[...].astype(F32) + r0_ref[...].astype(F32)) + r1_ref[...].astype(F32)) + r2_ref[...].astype(F32)
        g_out[...] = g
        d_out[...], m_out[...], v_out[...] = _adam_math(g, w_ref[...], m_ref[...], v_ref[...])

    pspec = pl.BlockSpec((None, tr, c), lambda i, chip_ref: (chip_ref[0], i, 0))

    def rspec(k):
        return pl.BlockSpec((None, tr, c), lambda i, chip_ref: (k, i, 0))

    wspec = pl.BlockSpec((None, tr, c), lambda i, chip_ref: (layer, i, 0))
    in_specs = [pspec, rspec(0), rspec(1), rspec(2), wspec, wspec, wspec]
    args = [chip, chip_sums, remote, remote, remote, w, m, v]
    aliases = {}
    if prev is not None:
        in_specs += [pl.BlockSpec(memory_space=pl.ANY)] * 4
        aliases = {len(args) + i: i for i in range(4)}
        args += list(prev)
    in_specs += [pl.BlockSpec(memory_space=pl.ANY)] * len(deps)
    args += list(deps)
    grid_spec = pltpu.PrefetchScalarGridSpec(num_scalar_prefetch=1, grid=(r // tr,), in_specs=in_specs,
                                             out_specs=(wspec,) * 4)
    shp = jax.ShapeDtypeStruct(w.shape, F32)
    return _pcall(body, name=name, out_shape=(shp,) * 4, grid_spec=grid_spec, aliases=aliases)(*args)


def _adamw_local(g, w, m, v, *, name):
    nl, r, c = w.shape
    tr = _rows(r, c)

    def body(g_ref, w_ref, m_ref, v_ref, d_out, m_out, v_out):
        d_out[...], m_out[...], v_out[...] = _adam_math(g_ref[...], w_ref[...], m_ref[...], v_ref[...])

    spec = pl.BlockSpec((None, tr, c), lambda l, i: (l, i, 0))
    shp = jax.ShapeDtypeStruct(w.shape, F32)
    return _pcall(body, name=name, out_shape=(shp,) * 3, grid=(nl, r // tr), in_specs=[spec] * 4,
                  out_specs=(spec,) * 3)(g, w, m, v)


def _adamw_replicated(parts, w, m, v, *, name):
    n = w.shape[1]

    def body(p_ref, w_ref, m_ref, v_ref, g_out, d_out, m_out, v_out):
        g = p_ref[0]
        for k in range(1, N_DEV):
            g = g + p_ref[k]
        g_out[...] = g
        d_out[...], m_out[...], v_out[...] = _adam_math(g, w_ref[...], m_ref[...], v_ref[...])

    vm = pl.BlockSpec(memory_space=pltpu.VMEM)
    shp = jax.ShapeDtypeStruct((1, n), F32)
    return _pcall(body, name=name, out_shape=(shp,) * 4, in_specs=[vm] * 4, out_specs=(vm,) * 4)(parts, w, m, v)


def _group_views(qk, proj, g, dil, seq):
    if dil == 1:
        return (qk, qk, proj), (0, A_HEADS, 2 * A_HEADS)
    length = seq // dil
    lo = g * GROUP_W
    q = qk[:, lo:lo + GROUP_W].reshape(length, dil * GROUP_W)
    k = qk[:, A_W + lo:A_W + lo + GROUP_W].reshape(length, dil * GROUP_W)
    v = proj[:, OFF_VA + lo:OFF_VA + lo + GROUP_W].astype(BF16).reshape(length, dil * GROUP_W)
    return (q, k, v), (0, 0, 0)


def _mod_rows(mod, d):
    return [mod[:, i * d:(i + 1) * d] for i in range(6)]


MIXER_W = ("w_in", "w_branch_a", "w_branch_b", "w_out")
FFN_W = ("w_gate_up", "w_down")
SHARD_AXIS = {"w_in": 1, "w_branch_a": 1, "w_branch_b": 1, "w_out": 0, "w_gate_up": 1, "w_down": 0}


def _mixer_fwd_a(h, mod, g1, gains, w_in, cos2, sin2, deps):
    seq, d = h.shape
    sh1, sc1 = _mod_rows(mod, d)[:2]
    u = _rmsmod_fwd(h, g1, sc1, sh1, name="rmsmod_fwd", deps=deps)
    proj = _mm(u, w_in, name="mm_in")
    qk = _qkrope_fwd(proj, gains, cos2, sin2, name="qkrope_fwd")
    os_, lses = [], []
    for g, dil in enumerate(DILATIONS):
        arrs, offs = _group_views(qk, proj, g, dil, seq)
        o, lse = _dil_fwd(*arrs, offs, seq // dil, dil, name=f"dil_fwd_{dil}")
        os_.append(o.reshape(seq, GROUP_W))
        lses.append(lse.reshape(seq, GROUP_W))
    o_a = _combine_fwd(os_, lses, name="combine_fwd")
    o_b = _sb_fwd(proj, name="sb_fwd")
    return dict(h_in=h, u=u, proj=proj, qk=qk, os=os_, lses=lses, o_a=o_a, o_b=o_b)


def _mixer_fwd_b(sv, mod, wts, deps):
    d = sv["h_in"].shape[1]
    ga1 = _mod_rows(mod, d)[2]
    y_a = _mm(sv["o_a"], wts["w_branch_a"], name="mm_branch", deps=deps)
    y_b = _mm(sv["o_b"], wts["w_branch_b"], name="mm_branch")
    merged = _merge_fwd(sv["proj"], y_a, y_b, name="merge_fwd")
    t = _mm(merged, wts["w_out"], name="mm_out")
    h_mid = _resid_gate(sv["h_in"], ga1, t, name="resid_gate")
    sv.update(y_a=y_a, y_b=y_b, merged=merged, t=t, h_mid=h_mid)
    return h_mid


def _ffn_fwd_a(sv, mod, g2, w_gate_up, deps):
    d = sv["h_mid"].shape[1]
    sh2, sc2 = _mod_rows(mod, d)[3:5]
    u2 = _rmsmod_fwd(sv["h_mid"], g2, sc2, sh2, name="rmsmod_fwd", deps=deps)
    a, g, u = _mm_swiglu(u2, w_gate_up, name="mm_gate_up")
    sv.update(u2=u2, g=g, up=u, a=a)
    return a


def _ffn_fwd_b(sv, mod, w_down, deps):
    d = sv["h_mid"].shape[1]
    ga2 = _mod_rows(mod, d)[5]
    f = _mm(sv["a"], w_down, name="mm_down", deps=deps)
    sv["f"] = f
    return _resid_gate(sv["h_mid"], ga2, f, name="resid_gate")


def _wgrad(act, dout, key):
    return _mm(act, dout, ta=True, out_dtype=BF16, name="mm_wgrad_" + key)


def _ffn_bwd(dh, sv, mod, g2, wts, deps):
    d = dh.shape[1]
    sc2, ga2 = _mod_rows(mod, d)[4:6]
    df, dgate2 = _resid_gate_bwd(dh, sv["f"], ga2, name="resid_gate_bwd", deps=deps)
    da = _mm(df, wts["w_down"], tb=True, out_dtype=BF16, name="mm_down_t")
    grads = {"w_down": _wgrad(sv["a"], df, "w_down")}
    dgu = _swiglu_bwd(sv["g"], sv["up"], da, name="swiglu_bwd")
    du2 = _mm(dgu, wts["w_gate_up"], tb=True, name="mm_gate_up_t")
    grads["w_gate_up"] = _wgrad(sv["u2"], dgu, "w_gate_up")
    dh_mid, dsh2, dsc2, dg2 = _rmsmod_bwd(du2, sv["h_mid"], g2, sc2, dh, name="rmsmod_bwd")
    return dh_mid, [dsh2, dsc2, dgate2], dg2, grads


def _mixer_bwd(dh_mid, sv, mod, g1, gains, wts, cos2, sin2, deps):
    seq, d = dh_mid.shape
    sc1, ga1 = _mod_rows(mod, d)[1:3]
    dt, dgate1 = _resid_gate_bwd(dh_mid, sv["t"], ga1, name="resid_gate_bwd", deps=deps)
    dmerged = _mm(dt, wts["w_out"], tb=True, name="mm_out_t")
    grads = {"w_out": _wgrad(sv["merged"], dt, "w_out")}
    dy_a, dy_b, dga, dgb = _merge_bwd(dmerged, sv["proj"], sv["y_a"], sv["y_b"], name="merge_bwd")
    do_a = _mm(dy_a, wts["w_branch_a"], tb=True, name="mm_branch_t")
    do_b = _mm(dy_b, wts["w_branch_b"], tb=True, name="mm_branch_t")
    grads["w_branch_a"] = _wgrad(sv["o_a"], dy_a, "w_branch_a")
    grads["w_branch_b"] = _wgrad(sv["o_b"], dy_b, "w_branch_b")
    dqb, dkb, dvb = _sb_bwd(sv["proj"], do_b, name="sb_bwd")
    comb = _combine_bwd(do_a, sv["os"], sv["lses"], name="combine_bwd")
    dos, dls = comb[:3], comb[3:]
    dqs, dks, dvs = [], [], []
    for g, dil in enumerate(DILATIONS):
        length = seq // dil
        arrs, offs = _group_views(sv["qk"], sv["proj"], g, dil, seq)
        view = (length, dil * GROUP_W)
        dq, dk, dv = _dil_bwd(*arrs, offs, sv["os"][g].reshape(view), sv["lses"][g].reshape(view),
                              dos[g].reshape(view), dls[g].reshape(view), length, dil, name=f"dil_bwd_{dil}")
        dqs.append(dq.reshape(seq, GROUP_W))
        dks.append(dk.reshape(seq, GROUP_W))
        dvs.append(dv.reshape(seq, GROUP_W))
    dqk, dgains = _qkrope_bwd(jnp.concatenate(dqs + dks, axis=1), sv["proj"], gains, cos2, sin2,
                              name="qkrope_bwd")
    dproj = jnp.concatenate(
        [dqk] + [t_.astype(BF16) for t_ in dvs + [dqb, dkb, dvb]] + [dga, dgb], axis=1)
    du = _mm(dproj, wts["w_in"], tb=True, name="mm_in_t")
    grads["w_in"] = _wgrad(sv["u"], dproj, "w_in")
    dh_in, dsh1, dsc1, dg1 = _rmsmod_bwd(du, sv["h_in"], g1, sc1, dh_mid, name="rmsmod_bwd")
    return dh_in, [dsh1, dsc1, dgate1], dg1, dgains, grads


def kernel(x, c, w_ada, b_ada, norm1_g, norm2_g, w_in, qn_g, kn_g, w_branch_a, w_branch_b, w_out, w_gate_up, w_down, loss_target, m_w_ada, m_b_ada, m_norm1_g, m_norm2_g, m_w_in, m_qn_g, m_kn_g, m_w_branch_a, m_w_branch_b, m_w_out, m_w_gate_up, m_w_down, v_w_ada, v_b_ada, v_norm1_g, v_norm2_g, v_w_in, v_qn_g, v_kn_g, v_w_branch_a, v_w_branch_b, v_w_out, v_w_gate_up, v_w_down):
    seq, d = x.shape[1], x.shape[2]
    depth = w_in.shape[0]
    weights = dict(w_in=w_in, w_branch_a=w_branch_a, w_branch_b=w_branch_b, w_out=w_out, w_gate_up=w_gate_up,
                   w_down=w_down)
    moments_m = dict(w_in=m_w_in, w_branch_a=m_w_branch_a, w_branch_b=m_w_branch_b, w_out=m_w_out,
                     w_gate_up=m_w_gate_up, w_down=m_w_down)
    moments_v = dict(w_in=v_w_in, w_branch_a=v_w_branch_a, w_branch_b=v_w_branch_b, w_out=v_w_out,
                     w_gate_up=v_w_gate_up, w_down=v_w_down)
    xi, yi, ci = _coords()
    me = 4 * xi + 2 * yi + ci
    core = jnp.reshape(ci, (1,)).astype(jnp.int32)
    chip = jnp.reshape(2 * xi + yi, (1,)).astype(jnp.int32)

    ada_w = w_ada.shape[2]
    c_act = _small_allgather(c, name="comm_gather_c", silu=True).reshape(N_DEV, d)
    c_pad = jnp.concatenate([c_act, jnp.zeros_like(c_act)], axis=0).astype(BF16)
    bias = lax.dynamic_slice(b_ada, (0, me * ada_w), (depth, ada_w))
    mod_part = jnp.stack([_mm(c_pad, w_ada[l], name="mm_ada")[:N_DEV] for l in range(depth)]) + bias[:, None, :]
    mod_all = _small_allgather(mod_part.reshape(1, depth * N_DEV * ada_w), name="comm_gather_mod")
    mod_all = mod_all.reshape(N_DEV, depth, N_DEV, ada_w)
    mod_mine = lax.dynamic_index_in_dim(mod_all, me, axis=2, keepdims=False)
    mods = jnp.transpose(mod_mine, (1, 0, 2)).reshape(depth, 1, 6 * d)

    cos2, sin2 = _rope_tables(seq)
    gains = [jnp.stack([qn_g[l], kn_g[l]])[:, None, :] for l in range(depth)]
    g1s = [norm1_g[l][None] for l in range(depth)]
    g2s = [norm2_g[l][None] for l in range(depth)]

    me_arr = jnp.reshape(me, (1,)).astype(jnp.int32)

    def placed(keys, l):
        return [_cast_place(weights[k], l, SHARD_AXIS[k], me_arr, name="cast_place_" + k) for k in keys]

    def gather_of(keys, l, tag):
        return _SplitGather(placed(keys, l), [SHARD_AXIS[k] for k in keys], f"{tag}{l}")

    wm = {"w_in": _gather_weights(placed(MIXER_W[:1], 0), [SHARD_AXIS["w_in"]], name="comm_gather_weights",
                                  deps=[mods])[0]}
    g_rest = gather_of(MIXER_W[1:], 0, "rest")
    gf = gather_of(FFN_W, 0, "ffn")
    deps = [g_rest.first(after=wm["w_in"]), gf.first(after=wm["w_in"])]
    h = x[0]
    saved, full = [], []
    for l in range(depth):
        last = l + 1 == depth
        sv = _mixer_fwd_a(h, mods[l], g1s[l], gains[l], wm["w_in"], cos2, sin2, deps)
        deps = [gf.forward(after=sv["o_b"])]
        if l == 0:
            deps.append(g_rest.forward(after=sv["o_b"]))
            wm.update(zip(MIXER_W[1:], g_rest.finish(after=sv["o_b"])))
        if not last:
            gm_next = gather_of(MIXER_W, l + 1, "mixer")
            deps.append(gm_next.first(after=sv["o_b"]))
        h_mid = _mixer_fwd_b(sv, mods[l], wm, deps)
        wf = dict(zip(FFN_W, gf.finish(after=h_mid)))
        a = _ffn_fwd_a(sv, mods[l], g2s[l], wf["w_gate_up"], [])
        deps = []
        if not last:
            deps.append(gm_next.forward(after=a))
            gf = gather_of(FFN_W, l + 1, "ffn")
            deps.append(gf.first(after=a))
        h = _ffn_fwd_b(sv, mods[l], wf["w_down"], deps)
        saved.append(sv)
        full.append({**wm, **wf})
        if not last:
            wm = dict(zip(MIXER_W, gm_next.finish(after=h)))
            deps = []
    loss_part, dh = _loss_fwd(h, loss_target[0], name="loss")
    loss = lax.psum(loss_part[0, 0], ("x", "y", "c"))

    done, pending = [], None

    def reduce_start(keys, grads, l):
        axes = [SHARD_AXIS[k] for k in keys]
        tag = f"{keys[0]}{l}"
        sib = _rs_pair_exchange([grads[k] for k in keys], axes, name="comm_rs_pair_" + tag)
        sums = [_pair_sum(grads[k], s_, core, ax, name="pair_sum_" + k) for k, s_, ax in zip(keys, sib, axes)]
        ex = _SplitChipExchange(sums, tag)
        return (keys, l, ex), [ex.start()]

    def reduce_finish(item, after):
        keys, l, ex = item
        sums, remote = ex.finish(after)
        done.append((keys, l, sums, remote))

    dmods, dg1s, dg2s, dgains = [None] * depth, [None] * depth, [None] * depth, [None] * depth
    deps = []
    for l in reversed(range(depth)):
        dh_mid, dmod_f, dg2s[l], grads = _ffn_bwd(dh, saved[l], mods[l], g2s[l], full[l], deps)
        if pending is not None:
            reduce_finish(pending, dh_mid)
        pending, deps = reduce_start(FFN_W, grads, l)
        dh, dmod_m, dg1s[l], dgains[l], grads = _mixer_bwd(dh_mid, saved[l], mods[l], g1s[l], gains[l], full[l],
                                                           cos2, sin2, deps)
        dmods[l] = jnp.concatenate(dmod_m + dmod_f, axis=1)
        reduce_finish(pending, dh)
        pending, deps = reduce_start(MIXER_W, grads, l)
    grad_x = dh[None]

    small = jnp.concatenate(
        dmods + dg1s + dg2s + [dgains[l][0] for l in range(depth)] + [dgains[l][1] for l in range(depth)], axis=1)

    stacked = {}

    def update(keys, l, sums, remote, deps):
        for k, p_, r_ in zip(keys, sums, remote):
            stacked[k] = _adamw_sharded(p_, r_, chip, weights[k], moments_m[k], moments_v[k], l, stacked.get(k),
                                        deps, name="adamw_" + k)

    for item in done:
        update(*item, deps)
    reduce_finish(pending, stacked[FFN_W[0]][0])
    update(*done[-1], [])
    small_all = _small_allgather(small, name="comm_gather_small", deps=[done[-1][3][0]])

    def pack(b, n1, n2, qn, kn):
        return jnp.concatenate([t_.reshape(1, -1) for t_ in (b, n1, n2, qn, kn)], axis=1)

    sg, sd, sm, sv_ = _adamw_replicated(small_all, pack(b_ada, norm1_g, norm2_g, qn_g, kn_g),
                                        pack(m_b_ada, m_norm1_g, m_norm2_g, m_qn_g, m_kn_g),
                                        pack(v_b_ada, v_norm1_g, v_norm2_g, v_qn_g, v_kn_g), name="adamw_replicated")

    def unpack(p):
        sizes = [depth * 6 * d, depth * d, depth * d, depth * HEAD_DIM, depth * HEAD_DIM]
        shapes = [b_ada.shape, norm1_g.shape, norm2_g.shape, qn_g.shape, kn_g.shape]
        out, off = [], 0
        for n, shp in zip(sizes, shapes):
            out.append(p[0, off:off + n].reshape(shp))
            off += n
        return dict(zip(("b_ada", "norm1_g", "norm2_g", "qn_g", "kn_g"), out))

    ug, ud, um, uv = unpack(sg), unpack(sd), unpack(sm), unpack(sv_)
    res = {k: dict(g=ug[k], d=ud[k], m=um[k], v=uv[k]) for k in ug}

    dmod_all = small_all[:, 0, :depth * 6 * d].reshape(N_DEV, depth, 6 * d)
    g_ada = None
    for l in range(depth):
        dm = lax.dynamic_slice(dmod_all[:, l, :], (0, me * ada_w), (N_DEV, ada_w))
        dm = jnp.concatenate([dm, jnp.zeros_like(dm)], axis=0).astype(BF16)
        g_ada = _mm(c_pad, dm, ta=True, name="mm_wgrad_ada", stack=(l, depth, g_ada))
    d_ada, m_ada, v_ada = _adamw_local(g_ada, w_ada, m_w_ada, v_w_ada, name="adamw_local")
    res["w_ada"] = dict(g=g_ada, d=d_ada, m=m_ada, v=v_ada)

    for k, (g_, d_, m_, v_) in stacked.items():
        res[k] = dict(g=g_, d=d_, m=m_, v=v_)

    order = ("w_ada", "b_ada", "norm1_g", "norm2_g", "w_in", "qn_g", "kn_g", "w_branch_a", "w_branch_b", "w_out",
             "w_gate_up", "w_down")
    return (loss, grad_x, *[res[k]["g"] for k in order], *[res[k]["d"] for k in order],
            *[res[k]["m"] for k in order], *[res[k]["v"] for k in order])
```

```python
import functools

import jax
import jax.numpy as jnp
from jax import lax
from jax.experimental import pallas as pl
from jax.experimental.pallas import tpu as pltpu

F32 = jnp.float32
BF16 = jnp.bfloat16

HEAD_DIM = 128
BLOCK = 128
DILATIONS = (1, 4, 16)
HEADS_PER_GROUP = 4
A_HEADS = 12
SB_HEADS = 4
GROUP_W = HEADS_PER_GROUP * HEAD_DIM
A_W = A_HEADS * HEAD_DIM
B_W = SB_HEADS * HEAD_DIM
OFF_QA, OFF_KA, OFF_VA = 0, A_W, 2 * A_W
OFF_QB, OFF_KB, OFF_VB = 3 * A_W, 3 * A_W + B_W, 3 * A_W + 2 * B_W
OFF_GATES = 3 * A_W + 3 * B_W
ROPE_THETA = 10000.0
EPS = 1e-6
ATT_SCALE = HEAD_DIM ** -0.5
MASKED = -1e30

ADAM_LR, ADAM_B1, ADAM_B2, ADAM_EPS, ADAM_WD, ADAM_STEP = 0.001, 0.9, 0.999, 1e-08, 0.01, 10

N_DEV = 8
N_CHIPS = 4
V7X_VMEM_LIMIT_BYTES = 56 * 1024 * 1024
ELEMWISE_BLOCK_BYTES = 2 * 1024 * 1024
MESH = pl.DeviceIdType.MESH

NN = (((1,), (0,)), ((), ()))
NT = (((1,), (1,)), ((), ()))
TN = (((0,), (0,)), ((), ()))


def _dot(a, b, dims=NN):
    return lax.dot_general(a, b, dims, preferred_element_type=F32)


def _tile(n, cap, mult=128):
    best = None
    for t in range(mult, min(n, cap) + 1, mult):
        if n % t == 0:
            best = t
    if best is None:
        assert n <= 2 * cap, (n, cap)
        return n
    return best


def _rows(r, c):
    return _tile(r, max(16, ELEMWISE_BLOCK_BYTES // (4 * c)), 16)


def _pcall(body, *, name, out_shape, grid=None, in_specs=None, out_specs=None, scratch=(), aliases=None,
           grid_spec=None, deps=()):
    kwargs = {}
    deps = list(deps)
    if grid_spec is not None:
        assert not deps
        kwargs["grid_spec"] = grid_spec
    else:
        if grid is not None:
            kwargs["grid"] = grid
        n_in = len(in_specs)
        kwargs["in_specs"] = list(in_specs) + [pl.BlockSpec(memory_space=pl.ANY)] * len(deps)
        kwargs["out_specs"] = out_specs
        kwargs["scratch_shapes"] = list(scratch)
        if deps:
            inner = body

            def body(*refs):
                return inner(*refs[:n_in], *refs[n_in + len(deps):])

    call = pl.pallas_call(
        body, name=name, out_shape=out_shape, input_output_aliases=aliases or {},
        compiler_params=pltpu.CompilerParams(vmem_limit_bytes=V7X_VMEM_LIMIT_BYTES), **kwargs)
    return (lambda *args: call(*args, *deps)) if deps else call


def _mm(a, b, *, name, ta=False, tb=False, out_dtype=F32, caps=(1024, 1024, 3072), stack=None, deps=()):
    kdim, m = a.shape if ta else a.shape[::-1]
    n, k2 = b.shape if tb else b.shape[::-1]
    assert kdim == k2, (a.shape, b.shape, ta, tb)
    tm, tn, tk = _tile(m, caps[0]), _tile(n, caps[1]), _tile(kdim, caps[2])
    nk = kdim // tk
    dims = (((0 if ta else 1,), (1 if tb else 0,)), ((), ()))

    def body(*refs):
        a_ref, b_ref = refs[0], refs[1]
        part = _dot(a_ref[...].astype(BF16), b_ref[...].astype(BF16), dims)
        if nk == 1:
            o_ref = refs[-1]
            o_ref[...] = part.astype(o_ref.dtype)
            return
        o_ref, acc_ref = refs[-2], refs[-1]
        k = pl.program_id(2)

        @pl.when(k == 0)
        def _():
            acc_ref[...] = part

        @pl.when(k > 0)
        def _():
            acc_ref[...] += part

        @pl.when(k == nk - 1)
        def _():
            o_ref[...] = acc_ref[...].astype(o_ref.dtype)

    a_spec = (pl.BlockSpec((tk, tm), lambda i, j, k: (k, i)) if ta
              else pl.BlockSpec((tm, tk), lambda i, j, k: (i, k)))
    b_spec = (pl.BlockSpec((tn, tk), lambda i, j, k: (j, k)) if tb
              else pl.BlockSpec((tk, tn), lambda i, j, k: (k, j)))
    ins, in_specs, aliases = [a, b], [a_spec, b_spec], {}
    if stack is None:
        out_shape = jax.ShapeDtypeStruct((m, n), out_dtype)
        out_spec = pl.BlockSpec((tm, tn), lambda i, j, k: (i, j))
    else:
        layer, n_layers, buf = stack
        out_shape = jax.ShapeDtypeStruct((n_layers, m, n), out_dtype)
        out_spec = pl.BlockSpec((None, tm, tn), lambda i, j, k: (layer, i, j))
        if buf is not None:
            ins.append(buf)
            in_specs.append(pl.BlockSpec(memory_space=pl.ANY))
            aliases = {2: 0}
    scratch = [] if nk == 1 else [pltpu.VMEM((tm, tn), F32)]
    return _pcall(body, name=name, out_shape=out_shape, grid=(m // tm, n // tn, nk), in_specs=in_specs,
                  out_specs=out_spec, scratch=scratch, aliases=aliases, deps=deps)(*ins)


def _rmsmod_fwd(h, g, scale, shift, *, name, deps=()):
    s, d = h.shape
    ts = _rows(s, d)

    def body(h_ref, g_ref, sc_ref, sh_ref, u_ref):
        hf = h_ref[...]
        r = lax.rsqrt(jnp.mean(hf * hf, axis=-1, keepdims=True) + EPS)
        u_ref[...] = (((hf * r) * g_ref[...]) * (1.0 + sc_ref[...]) + sh_ref[...]).astype(BF16)

    row = pl.BlockSpec((ts, d), lambda i: (i, 0))
    vec = pl.BlockSpec((1, d), lambda i: (0, 0))
    return _pcall(body, name=name, out_shape=jax.ShapeDtypeStruct((s, d), BF16), grid=(s // ts,),
                  in_specs=[row, vec, vec, vec], out_specs=row, deps=deps)(h, g, scale, shift)


def _rmsmod_bwd(du, h, g, scale, dres, *, name):
    s, d = h.shape
    ts = _rows(s, d)

    def body(du_ref, h_ref, g_ref, sc_ref, dres_ref, dh_ref, dsh_ref, dsc_ref, dg_ref):
        @pl.when(pl.program_id(0) == 0)
        def _():
            dsh_ref[...] = jnp.zeros_like(dsh_ref)
            dsc_ref[...] = jnp.zeros_like(dsc_ref)
            dg_ref[...] = jnp.zeros_like(dg_ref)

        hf, duf, gain = h_ref[...], du_ref[...], g_ref[...]
        r = lax.rsqrt(jnp.mean(hf * hf, axis=-1, keepdims=True) + EPS)
        xh = hf * r
        dn = duf * (1.0 + sc_ref[...])
        dsh_ref[...] += jnp.sum(duf, axis=0, keepdims=True)
        dsc_ref[...] += jnp.sum(duf * (xh * gain), axis=0, keepdims=True)
        dg_ref[...] += jnp.sum(dn * xh, axis=0, keepdims=True)
        dxh = dn * gain
        dh_ref[...] = dres_ref[...] + r * (dxh - xh * jnp.mean(dxh * xh, axis=-1, keepdims=True))

    row = pl.BlockSpec((ts, d), lambda i: (i, 0))
    vec = pl.BlockSpec((1, d), lambda i: (0, 0))
    vshape = jax.ShapeDtypeStruct((1, d), F32)
    return _pcall(body, name=name, out_shape=(jax.ShapeDtypeStruct((s, d), F32), vshape, vshape, vshape),
                  grid=(s // ts,), in_specs=[row, row, vec, vec, row],
                  out_specs=(row, vec, vec, vec))(du, h, g, scale, dres)


def _resid_gate(h, gate, t, *, name):
    s, d = h.shape
    ts = _rows(s, d)

    def body(h_ref, g_ref, t_ref, o_ref):
        o_ref[...] = h_ref[...] + g_ref[...] * t_ref[...]

    row = pl.BlockSpec((ts, d), lambda i: (i, 0))
    vec = pl.BlockSpec((1, d), lambda i: (0, 0))
    return _pcall(body, name=name, out_shape=jax.ShapeDtypeStruct((s, d), F32), grid=(s // ts,),
                  in_specs=[row, vec, row], out_specs=row)(h, gate, t)


def _resid_gate_bwd(dh, t, gate, *, name, deps=()):
    s, d = dh.shape
    ts = _rows(s, d)

    def body(dh_ref, t_ref, g_ref, dt_ref, dg_ref):
        @pl.when(pl.program_id(0) == 0)
        def _():
            dg_ref[...] = jnp.zeros_like(dg_ref)

        dhf = dh_ref[...]
        dt_ref[...] = (dhf * g_ref[...]).astype(BF16)
        dg_ref[...] += jnp.sum(dhf * t_ref[...], axis=0, keepdims=True)

    row = pl.BlockSpec((ts, d), lambda i: (i, 0))
    vec = pl.BlockSpec((1, d), lambda i: (0, 0))
    return _pcall(body, name=name,
                  out_shape=(jax.ShapeDtypeStruct((s, d), BF16), jax.ShapeDtypeStruct((1, d), F32)),
                  grid=(s // ts,), in_specs=[row, row, vec], out_specs=(row, vec), deps=deps)(dh, t, gate)


def _merge_fwd(proj, y_a, y_b, *, name):
    s, d = y_a.shape
    ts = _rows(s, d)
    ga_blk = OFF_GATES // d

    def body(ga_ref, gb_ref, ya_ref, yb_ref, o_ref):
        o_ref[...] = (jax.nn.sigmoid(ga_ref[...]) * ya_ref[...]
                      + jax.nn.sigmoid(gb_ref[...]) * yb_ref[...]).astype(BF16)

    row = pl.BlockSpec((ts, d), lambda i: (i, 0))
    ga = pl.BlockSpec((ts, d), lambda i: (i, ga_blk))
    gb = pl.BlockSpec((ts, d), lambda i: (i, ga_blk + 1))
    return _pcall(body, name=name, out_shape=jax.ShapeDtypeStruct((s, d), BF16), grid=(s // ts,),
                  in_specs=[ga, gb, row, row], out_specs=row)(proj, proj, y_a, y_b)


def _merge_bwd(dm, proj, y_a, y_b, *, name):
    s, d = y_a.shape
    ts = _rows(s, d)
    ga_blk = OFF_GATES // d

    def body(dm_ref, ga_ref, gb_ref, ya_ref, yb_ref, dya_ref, dyb_ref, dga_ref, dgb_ref):
        dmf = dm_ref[...]
        sa, sb = jax.nn.sigmoid(ga_ref[...]), jax.nn.sigmoid(gb_ref[...])
        dya_ref[...] = (dmf * sa).astype(BF16)
        dyb_ref[...] = (dmf * sb).astype(BF16)
        dga_ref[...] = (dmf * ya_ref[...] * (sa * (1.0 - sa))).astype(BF16)
        dgb_ref[...] = (dmf * yb_ref[...] * (sb * (1.0 - sb))).astype(BF16)

    row = pl.BlockSpec((ts, d), lambda i: (i, 0))
    ga = pl.BlockSpec((ts, d), lambda i: (i, ga_blk))
    gb = pl.BlockSpec((ts, d), lambda i: (i, ga_blk + 1))
    shp = jax.ShapeDtypeStruct((s, d), BF16)
    return _pcall(body, name=name, out_shape=(shp, shp, shp, shp), grid=(s // ts,),
                  in_specs=[row, ga, gb, row, row], out_specs=(row, row, row, row))(dm, proj, proj, y_a, y_b)


def _mm_swiglu(u2, w_gate_up, *, name):
    s, d = u2.shape
    f = w_gate_up.shape[1] // 2
    tm, tn = _tile(s, 1024), _tile(f, 512)
    nj = f // tn

    def body(x_ref, wg_ref, wu_ref, a_ref, g_ref, u_ref):
        x = x_ref[...]
        gf, uf = _dot(x, wg_ref[...]), _dot(x, wu_ref[...])
        a_ref[...] = ((gf * jax.nn.sigmoid(gf)) * uf).astype(BF16)
        g_ref[...] = gf.astype(BF16)
        u_ref[...] = uf.astype(BF16)

    out = pl.BlockSpec((tm, tn), lambda i, j: (i, j))
    shp = jax.ShapeDtypeStruct((s, f), BF16)
    return _pcall(body, name=name, out_shape=(shp, shp, shp), grid=(s // tm, nj),
                  in_specs=[pl.BlockSpec((tm, d), lambda i, j: (i, 0)), pl.BlockSpec((d, tn), lambda i, j: (0, j)),
                            pl.BlockSpec((d, tn), lambda i, j: (0, nj + j))],
                  out_specs=(out, out, out))(u2, w_gate_up, w_gate_up)


def _swiglu_bwd(g, u, da, *, name):
    s, f = g.shape
    ts = _rows(s, f)

    def body(g_ref, u_ref, da_ref, o_ref):
        gf, daf = g_ref[...].astype(F32), da_ref[...].astype(F32)
        sg = jax.nn.sigmoid(gf)
        o_ref[:, :f] = (daf * u_ref[...].astype(F32) * (sg * (1.0 + gf * (1.0 - sg)))).astype(BF16)
        o_ref[:, f:] = (daf * (gf * sg)).astype(BF16)

    row = pl.BlockSpec((ts, f), lambda i: (i, 0))
    return _pcall(body, name=name, out_shape=jax.ShapeDtypeStruct((s, 2 * f), BF16), grid=(s // ts,),
                  in_specs=[row, row, row], out_specs=pl.BlockSpec((ts, 2 * f), lambda i: (i, 0)))(g, u, da)


def _loss_fwd(y, tgt, *, name):
    s, d = y.shape
    ts = _rows(s, d)

    def body(y_ref, t_ref, l_ref, dy_ref):
        @pl.when(pl.program_id(0) == 0)
        def _():
            l_ref[...] = jnp.zeros_like(l_ref)

        e = y_ref[...] - t_ref[...]
        dy_ref[...] = e * (1.0 / d)
        per_tok = jnp.sum(e * e, axis=1, keepdims=True) * (1.0 / d)
        l_ref[...] += 0.5 * jnp.sum(per_tok, axis=0, keepdims=True)

    row = pl.BlockSpec((ts, d), lambda i: (i, 0))
    return _pcall(body, name=name,
                  out_shape=(jax.ShapeDtypeStruct((1, 128), F32), jax.ShapeDtypeStruct((s, d), F32)),
                  grid=(s // ts,), in_specs=[row, row],
                  out_specs=(pl.BlockSpec((1, 128), lambda i: (0, 0)), row))(y, tgt)


def _rope_tables(seq):
    inv = jnp.power(ROPE_THETA, -jnp.arange(0, HEAD_DIM, 2, dtype=F32) / HEAD_DIM)
    ang = jnp.arange(seq, dtype=F32)[:, None] * inv[None, :]
    cos, sin = jnp.cos(ang), jnp.sin(ang)
    return jnp.concatenate([cos, cos], axis=1), jnp.concatenate([-sin, sin], axis=1)


def _qkrope_fwd(proj, gains, cos2, sin2, *, name):
    s = proj.shape[0]
    ts = _rows(s, A_W)

    def body(x_ref, g_ref, c_ref, s_ref, o_ref):
        gain, cos, sin = g_ref[...], c_ref[...], s_ref[...]
        for h in range(A_HEADS):
            lanes = slice(h * HEAD_DIM, (h + 1) * HEAD_DIM)
            x = x_ref[:, lanes]
            y = (x * lax.rsqrt(jnp.mean(x * x, axis=-1, keepdims=True) + EPS)) * gain
            o_ref[:, lanes] = (y * cos + pltpu.roll(y, HEAD_DIM // 2, 1) * sin).astype(BF16)

    heads = pl.BlockSpec((ts, A_W), lambda i, j: (i, j))
    tab = pl.BlockSpec((ts, HEAD_DIM), lambda i, j: (i, 0))
    gain = pl.BlockSpec((None, 1, HEAD_DIM), lambda i, j: (j, 0, 0))
    return _pcall(body, name=name, out_shape=jax.ShapeDtypeStruct((s, 2 * A_W), BF16),
                  grid=(s // ts, 2), in_specs=[heads, gain, tab, tab], out_specs=heads)(
                      proj, gains, cos2, sin2)


def _qkrope_bwd(dqk, proj, gains, cos2, sin2, *, name):
    s = proj.shape[0]
    ts = _rows(s, A_W)

    def body(d_ref, x_ref, g_ref, c_ref, s_ref, dx_ref, dg_ref):
        @pl.when(pl.program_id(1) == 0)
        def _():
            dg_ref[...] = jnp.zeros_like(dg_ref)

        gain, cos, sin = g_ref[...], c_ref[...], s_ref[...]
        dg = jnp.zeros((1, HEAD_DIM), F32)
        for h in range(A_HEADS):
            lanes = slice(h * HEAD_DIM, (h + 1) * HEAD_DIM)
            dout = d_ref[:, lanes]
            dy = dout * cos + pltpu.roll(dout * sin, HEAD_DIM // 2, 1)
            x = x_ref[:, lanes]
            r = lax.rsqrt(jnp.mean(x * x, axis=-1, keepdims=True) + EPS)
            xh = x * r
            dg = dg + jnp.sum(dy * xh, axis=0, keepdims=True)
            dxh = dy * gain
            dx_ref[:, lanes] = (r * (dxh - xh * jnp.mean(dxh * xh, axis=-1, keepdims=True))).astype(BF16)
        dg_ref[...] += dg

    heads = pl.BlockSpec((ts, A_W), lambda j, i: (i, j))
    tab = pl.BlockSpec((ts, HEAD_DIM), lambda j, i: (i, 0))
    gain = pl.BlockSpec((None, 1, HEAD_DIM), lambda j, i: (j, 0, 0))
    return _pcall(body, name=name,
                  out_shape=(jax.ShapeDtypeStruct((s, 2 * A_W), BF16), jax.ShapeDtypeStruct((2, 1, HEAD_DIM), F32)),
                  grid=(2, s // ts), in_specs=[heads, heads, gain, tab, tab],
                  out_specs=(heads, gain))(dqk, proj, gains, cos2, sin2)


def _block_rows(blk):
    if isinstance(blk, int):
        return pl.ds(blk * BLOCK, BLOCK)
    return pl.ds(pl.multiple_of(blk * BLOCK, BLOCK), BLOCK)


def _band_masks(n, with_prev):
    row = lax.broadcasted_iota(jnp.int32, (BLOCK, BLOCK), 0)
    col = lax.broadcasted_iota(jnp.int32, (BLOCK, BLOCK), 1)
    cur = col <= row
    if not with_prev:
        return [(n, cur)]
    prev = col >= row + jnp.where(n >= 1, 0, BLOCK)
    return [(n, cur), (jnp.maximum(n - 1, 0), prev)]


def _dil_fwd(q_arr, k_arr, v_arr, offs, length, dil, *, name):
    nj, nb = dil * HEADS_PER_GROUP, length // BLOCK
    ju = HEADS_PER_GROUP
    qo, ko, vo = (off // ju for off in offs)
    assert all(off % ju == 0 for off in offs)

    def body(q_ref, k_ref, v_ref, o_ref, l_ref):
        n = pl.program_id(1)
        masks = _band_masks(n, nb > 1)
        for cb in range(ju):
            lanes = slice(cb * HEAD_DIM, (cb + 1) * HEAD_DIM)
            q = q_ref[:, lanes].astype(BF16)
            parts = []
            for blk, mask in masks:
                rows = _block_rows(blk)
                sc = _dot(q, k_ref[rows, lanes].astype(BF16), NT) * ATT_SCALE
                parts.append((jnp.where(mask, sc, MASKED), rows))
            m = parts[0][0].max(axis=-1, keepdims=True)
            for sc, _ in parts[1:]:
                m = jnp.maximum(m, sc.max(axis=-1, keepdims=True))
            den = jnp.zeros((BLOCK, 1), F32)
            acc = jnp.zeros((BLOCK, HEAD_DIM), F32)
            for sc, rows in parts:
                p = jnp.exp(sc - m)
                den = den + jnp.sum(p, axis=-1, keepdims=True)
                acc = acc + _dot(p.astype(BF16), v_ref[rows, lanes].astype(BF16))
            o_ref[:, lanes] = acc / den
            l_ref[:, lanes] = jnp.broadcast_to(m + jnp.log(den), (BLOCK, HEAD_DIM))

    qspec = pl.BlockSpec((BLOCK, ju * HEAD_DIM), lambda j, n: (n, qo + j))
    kspec = pl.BlockSpec((length, ju * HEAD_DIM), lambda j, n: (0, ko + j))
    vspec = pl.BlockSpec((length, ju * HEAD_DIM), lambda j, n: (0, vo + j))
    ospec = pl.BlockSpec((BLOCK, ju * HEAD_DIM), lambda j, n: (n, j))
    shp = jax.ShapeDtypeStruct((length, nj * HEAD_DIM), F32)
    return _pcall(body, name=name, out_shape=(shp, shp), grid=(nj // ju, nb), in_specs=[qspec, kspec, vspec],
                  out_specs=(ospec, ospec))(q_arr, k_arr, v_arr)


def _dil_bwd(q_arr, k_arr, v_arr, offs, o, lse, do, dlse, length, dil, *, name):
    nj, nb = dil * HEADS_PER_GROUP, length // BLOCK
    ju = HEADS_PER_GROUP if length <= 4 * BLOCK else 2
    qo, ko, vo = (off // ju for off in offs)
    assert all(off % ju == 0 for off in offs)

    def body(q_ref, k_ref, v_ref, o_ref, l_ref, do_ref, dl_ref, dq_ref, dk_ref, dv_ref):
        dk_ref[...] = jnp.zeros_like(dk_ref)
        dv_ref[...] = jnp.zeros_like(dv_ref)

        def step(n, carry):
            qrows = _block_rows(n)
            masks = _band_masks(n, nb > 1)
            for cb in range(ju):
                lanes = slice(cb * HEAD_DIM, (cb + 1) * HEAD_DIM)
                q = q_ref[qrows, lanes].astype(BF16)
                dof = do_ref[qrows, lanes]
                dob = dof.astype(BF16)
                lse_b = l_ref[qrows, lanes]
                shift = dl_ref[qrows, lanes] - jnp.sum(dof * o_ref[qrows, lanes], axis=-1, keepdims=True)
                dq = jnp.zeros((BLOCK, HEAD_DIM), F32)
                for blk, mask in masks:
                    rows = _block_rows(blk)
                    kk, vv = k_ref[rows, lanes].astype(BF16), v_ref[rows, lanes].astype(BF16)
                    sc = _dot(q, kk, NT) * ATT_SCALE
                    p = jnp.where(mask, jnp.exp(sc - lse_b), 0.0)
                    ds = (p * (_dot(dob, vv, NT) + shift)).astype(BF16)
                    dq = dq + _dot(ds, kk)
                    dk_ref[rows, lanes] += _dot(ds, q, TN) * ATT_SCALE
                    dv_ref[rows, lanes] += _dot(p.astype(BF16), dob, TN)
                dq_ref[qrows, lanes] = dq * ATT_SCALE
            return carry

        if nb == 1:
            step(0, 0)
        else:
            lax.fori_loop(0, nb, step, 0)

    def col(off):
        return pl.BlockSpec((length, ju * HEAD_DIM), lambda j: (0, off + j))

    shp = jax.ShapeDtypeStruct((length, nj * HEAD_DIM), F32)
    return _pcall(body, name=name, out_shape=(shp, shp, shp), grid=(nj // ju,),
                  in_specs=[col(qo), col(ko), col(vo), col(0), col(0), col(0), col(0)],
                  out_specs=(col(0), col(0), col(0)))(q_arr, k_arr, v_arr, o, lse, do, dlse)


def _combine_weights(l_refs):
    ls = [r[...] for r in l_refs]
    m = jnp.maximum(jnp.maximum(ls[0], ls[1]), ls[2])
    es = [jnp.exp(l - m) for l in ls]
    den = es[0] + es[1] + es[2]
    return [e / den for e in es]


def _combine_fwd(os_, lses, *, name):
    s = os_[0].shape[0]
    ts = _rows(s, GROUP_W)

    def body(o0, o1, o2, l0, l1, l2, out_ref):
        w = _combine_weights((l0, l1, l2))
        out_ref[...] = (w[0] * o0[...] + w[1] * o1[...] + w[2] * o2[...]).astype(BF16)

    row = pl.BlockSpec((ts, GROUP_W), lambda i: (i, 0))
    return _pcall(body, name=name, out_shape=jax.ShapeDtypeStruct((s, GROUP_W), BF16), grid=(s // ts,),
                  in_specs=[row] * 6, out_specs=row)(*os_, *lses)


def _combine_bwd(do_a, os_, lses, *, name, deps=()):
    s = do_a.shape[0]
    ts = _rows(s, GROUP_W)

    def body(d_ref, o0, o1, o2, l0, l1, l2, do0, do1, do2, dl0, dl1, dl2):
        w = _combine_weights((l0, l1, l2))
        d = d_ref[...]
        og = [o0[...], o1[...], o2[...]]
        oa = w[0] * og[0] + w[1] * og[1] + w[2] * og[2]
        ta = jnp.sum(d * oa, axis=-1, keepdims=True)
        for g, (do_ref, dl_ref) in enumerate(((do0, dl0), (do1, dl1), (do2, dl2))):
            do_ref[...] = w[g] * d
            dl_ref[...] = w[g] * (jnp.sum(d * og[g], axis=-1, keepdims=True) - ta)

    head = pl.BlockSpec((ts, HEAD_DIM), lambda i, h: (i, h))
    shp = jax.ShapeDtypeStruct((s, GROUP_W), F32)
    return _pcall(body, name=name, out_shape=(shp,) * 6, grid=(s // ts, HEADS_PER_GROUP),
                  in_specs=[head] * 7, out_specs=(head,) * 6, deps=deps)(do_a, *os_, *lses)


def _dot_exact(x, ones_mask):
    hi = x.astype(BF16)
    r1 = x - hi.astype(F32)
    mid = r1.astype(BF16)
    lo = (r1 - mid.astype(F32)).astype(BF16)
    return _dot(hi, ones_mask) + _dot(mid, ones_mask) + _dot(lo, ones_mask)


SB_QROWS = 2 * BLOCK
SB_UNROLL = 4


def _sb_mask(j, i):
    row = lax.broadcasted_iota(jnp.int32, (SB_QROWS, BLOCK), 0)
    col = lax.broadcasted_iota(jnp.int32, (SB_QROWS, BLOCK), 1)
    return col + (j * BLOCK - i * SB_QROWS) < row


def _sb_steps(i):
    return ((i + 1) * (SB_QROWS // BLOCK) + SB_UNROLL - 1) // SB_UNROLL


def _sb_scores(q, kk, j, i):
    mask = _sb_mask(j, i)
    z = _dot(q, kk, NT) * ATT_SCALE
    sp = jnp.log1p(jnp.exp(-jnp.abs(z)))
    log_beta = jnp.minimum(z, 0.0) - sp
    log_1mb = jnp.where(mask, jnp.minimum(-z, 0.0) - sp, 0.0)
    return z, log_beta, log_1mb, mask


def _sb_weights(log_beta, log_1mb, mask, run, upper):
    after = run + _dot_exact(log_1mb, upper)
    return jnp.where(mask, jnp.exp(log_beta + after), 0.0)


def _tri(strict_lower):
    row = lax.broadcasted_iota(jnp.int32, (BLOCK, BLOCK), 0)
    col = lax.broadcasted_iota(jnp.int32, (BLOCK, BLOCK), 1)
    return ((row > col) if strict_lower else (row < col)).astype(BF16)


def _sb_fwd(proj, *, name):
    s = proj.shape[0]
    assert s % (BLOCK * SB_UNROLL) == 0 and s % SB_QROWS == 0
    qb, kb, vb = OFF_QB // HEAD_DIM, OFF_KB // HEAD_DIM, OFF_VB // HEAD_DIM

    def body(q_ref, k_ref, v_ref, o_ref):
        i = pl.program_id(1)
        q = q_ref[...].astype(BF16)
        upper = _tri(True)
        nsteps = _sb_steps(i)

        def step(t, carry):
            acc, run = carry
            for b in reversed(range(SB_UNROLL)):
                j = (nsteps - 1 - t) * SB_UNROLL + b
                rows = _block_rows(j)
                _, log_beta, log_1mb, mask = _sb_scores(q, k_ref[rows, :].astype(BF16), j, i)
                a = _sb_weights(log_beta, log_1mb, mask, run, upper)
                acc = acc + _dot(a.astype(BF16), v_ref[rows, :].astype(BF16))
                run = run + jnp.sum(log_1mb, axis=-1, keepdims=True)
            return acc, run

        acc, _ = lax.fori_loop(0, nsteps, step,
                               (jnp.zeros((SB_QROWS, HEAD_DIM), F32), jnp.zeros((SB_QROWS, 1), F32)))
        o_ref[...] = acc.astype(BF16)

    return _pcall(body, name=name, out_shape=jax.ShapeDtypeStruct((s, B_W), BF16), grid=(SB_HEADS, s // SB_QROWS),
                  in_specs=[pl.BlockSpec((SB_QROWS, HEAD_DIM), lambda h, i: (i, qb + h)),
                            pl.BlockSpec((s, HEAD_DIM), lambda h, i: (0, kb + h)),
                            pl.BlockSpec((s, HEAD_DIM), lambda h, i: (0, vb + h))],
                  out_specs=pl.BlockSpec((SB_QROWS, HEAD_DIM), lambda h, i: (i, h)))(proj, proj, proj)


def _sb_bwd(proj, do_b, *, name):
    s = proj.shape[0]
    assert s % (BLOCK * SB_UNROLL) == 0 and s % SB_QROWS == 0
    nkb = s // BLOCK
    qb, kb, vb = OFF_QB // HEAD_DIM, OFF_KB // HEAD_DIM, OFF_VB // HEAD_DIM

    def body(q_ref, k_ref, v_ref, do_ref, dq_ref, dk_ref, dv_ref, z_s, a_s):
        i = pl.program_id(1)

        @pl.when(i == 0)
        def _():
            dk_ref[...] = jnp.zeros_like(dk_ref)
            dv_ref[...] = jnp.zeros_like(dv_ref)

        q = q_ref[...].astype(BF16)
        dob = do_ref[...].astype(BF16)
        upper, lower = _tri(True), _tri(False)
        nsteps = _sb_steps(i)

        def recompute(t, run):
            for b in reversed(range(SB_UNROLL)):
                j = (nsteps - 1 - t) * SB_UNROLL + b
                z, log_beta, log_1mb, mask = _sb_scores(q, k_ref[_block_rows(j), :].astype(BF16), j, i)
                z_s[j] = z
                a_s[j] = _sb_weights(log_beta, log_1mb, mask, run, upper)
                run = run + jnp.sum(log_1mb, axis=-1, keepdims=True)
            return run

        lax.fori_loop(0, nsteps, recompute, jnp.zeros((SB_QROWS, 1), F32))

        def grads(t, carry):
            dq, run = carry
            for b in range(SB_UNROLL):
                j = t * SB_UNROLL + b
                rows = _block_rows(j)
                kk, vv = k_ref[rows, :].astype(BF16), v_ref[rows, :].astype(BF16)
                z, a = z_s[j], a_s[j]
                de = _dot(dob, vv, NT) * a
                before = run + _dot_exact(de, lower)
                dz = (de * jax.nn.sigmoid(-z)
                      - jnp.where(_sb_mask(j, i), jax.nn.sigmoid(z), 0.0) * before).astype(BF16)
                dk_ref[rows, :] += _dot(dz, q, TN) * ATT_SCALE
                dv_ref[rows, :] += _dot(a.astype(BF16), dob, TN)
                dq = dq + _dot(dz, kk)
                run = run + jnp.sum(de, axis=-1, keepdims=True)
            return dq, run

        dq, _ = lax.fori_loop(0, nsteps, grads,
                              (jnp.zeros((SB_QROWS, HEAD_DIM), F32), jnp.zeros((SB_QROWS, 1), F32)))
        dq_ref[...] = dq * ATT_SCALE

    blk = pl.BlockSpec((SB_QROWS, HEAD_DIM), lambda h, i: (i, h))
    full = pl.BlockSpec((s, HEAD_DIM), lambda h, i: (0, h))
    shp = jax.ShapeDtypeStruct((s, B_W), F32)
    return _pcall(body, name=name, out_shape=(shp, shp, shp), grid=(SB_HEADS, s // SB_QROWS),
                  in_specs=[pl.BlockSpec((SB_QROWS, HEAD_DIM), lambda h, i: (i, qb + h)),
                            pl.BlockSpec((s, HEAD_DIM), lambda h, i: (0, kb + h)),
                            pl.BlockSpec((s, HEAD_DIM), lambda h, i: (0, vb + h)), blk],
                  out_specs=(blk, full, full),
                  scratch=[pltpu.VMEM((nkb, SB_QROWS, BLOCK), F32), pltpu.VMEM((nkb, SB_QROWS, BLOCK), F32)])(
                      proj, proj, proj, do_b)


def _coords():
    return lax.axis_index("x"), lax.axis_index("y"), lax.axis_index("c")


def _flip(v, bit):
    return 1 - v if bit else v


def _shard_of(ref, axis, idx, size):
    if axis == 0:
        sl = pl.ds(pl.multiple_of(idx * size, 16), size)
        return ref.at[sl, :] if len(ref.shape) == 2 else ref.at[:, sl, :]
    sl = pl.ds(pl.multiple_of(idx * size, 128), size)
    return ref.at[:, sl] if len(ref.shape) == 2 else ref.at[:, :, sl]


def _small_allgather(v, *, name, silu=False, deps=()):
    n = v.shape[1]

    def body(v_ref, out_ref, send_sems, recv_sems):
        x, y, c = _coords()
        me = 4 * x + 2 * y + c
        val = v_ref[...]
        out_ref[me] = val * jax.nn.sigmoid(val) if silu else val
        copies = []
        for k in range(1, N_DEV):
            peer = (_flip(x, k & 4), _flip(y, k & 2), _flip(c, k & 1))
            copies.append(pltpu.make_async_remote_copy(
                src_ref=out_ref.at[me], dst_ref=out_ref.at[me], send_sem=send_sems.at[k - 1],
                recv_sem=recv_sems.at[k - 1], device_id=peer, device_id_type=MESH))
        for cp in copies:
            cp.start()
        for cp in copies:
            cp.wait_recv()
        for cp in copies:
            cp.wait_send()

    return _pcall(body, name=name, out_shape=jax.ShapeDtypeStruct((N_DEV, 1, n), F32),
                  in_specs=[pl.BlockSpec(memory_space=pltpu.VMEM)], out_specs=pl.BlockSpec(memory_space=pltpu.VMEM),
                  scratch=[pltpu.SemaphoreType.DMA((N_DEV - 1,)), pltpu.SemaphoreType.DMA((N_DEV - 1,))],
                  deps=deps)(v)


def _cast_place(w, layer, axis, me, *, name):
    _, r, c = w.shape
    tr = _rows(r, c)
    nrt = r // tr

    def body(me_ref, w_ref, o_ref):
        o_ref[...] = w_ref[...].astype(BF16)

    wspec = pl.BlockSpec((None, tr, c), lambda i, me_ref: (layer, i, 0))
    if axis == 0:
        ospec = pl.BlockSpec((tr, c), lambda i, me_ref: (me_ref[0] * nrt + i, 0))
        shape = (r * N_DEV, c)
    else:
        ospec = pl.BlockSpec((tr, c), lambda i, me_ref: (i, me_ref[0]))
        shape = (r, c * N_DEV)
    grid_spec = pltpu.PrefetchScalarGridSpec(num_scalar_prefetch=1, grid=(nrt,), in_specs=[wspec], out_specs=ospec)
    return _pcall(body, name=name, out_shape=jax.ShapeDtypeStruct(shape, BF16), grid_spec=grid_spec)(me, w)


def _pair_sum(grad, sib, core, axis, *, name):
    _, r, c = sib.shape
    tr = _rows(r, c)
    nrt = r // tr

    def body(core_ref, g_ref, s_ref, o_ref):
        o_ref[...] = (g_ref[...].astype(F32) + s_ref[...].astype(F32)).astype(BF16)

    if axis == 0:
        gspec = pl.BlockSpec((tr, c), lambda q, i, core_ref: ((2 * q + core_ref[0]) * nrt + i, 0))
    else:
        gspec = pl.BlockSpec((tr, c), lambda q, i, core_ref: (i, 2 * q + core_ref[0]))
    sspec = pl.BlockSpec((None, tr, c), lambda q, i, core_ref: (q, i, 0))
    grid_spec = pltpu.PrefetchScalarGridSpec(num_scalar_prefetch=1, grid=(N_CHIPS, nrt),
                                             in_specs=[gspec, sspec], out_specs=sspec)
    return _pcall(body, name=name, out_shape=jax.ShapeDtypeStruct(sib.shape, BF16), grid_spec=grid_spec)(
        core, grad, sib)


HBM_SPEC = pl.BlockSpec(memory_space=pltpu.HBM)
SEM_SPEC = pl.BlockSpec(memory_space=pltpu.SEMAPHORE)
SPLIT_PARAMS = dict(has_side_effects=pltpu.SideEffectType.DATAFLOW_SIDE_EFFECTING)


def _hbm(a):
    return pltpu.with_memory_space_constraint(a, pltpu.HBM)


def _split_start(copies_fn, buffers, sem_shape, after, *, name):
    n = len(buffers)
    rows, cols = sem_shape
    ns = rows * cols

    def body(*refs):
        sems = refs[n + 1:n + 1 + 2 * ns]
        for cp in copies_fn(refs[:n], _sem_rows(sems[:ns], cols), _sem_rows(sems[ns:], cols)):
            cp.start()
        refs[-1][...] = jnp.zeros_like(refs[-1])

    sem = pltpu.SemaphoreType.DMA(())
    outs = pl.pallas_call(
        body, name=name,
        out_shape=((sem,) * (2 * ns) + tuple(pltpu.HBM(b.shape, b.dtype) for b in buffers)
                   + (jax.ShapeDtypeStruct((8, 128), F32),)),
        in_specs=(HBM_SPEC,) * n + (pl.BlockSpec(memory_space=pl.ANY),),
        out_specs=(SEM_SPEC,) * (2 * ns) + (HBM_SPEC,) * n + (pl.BlockSpec(memory_space=pltpu.VMEM),),
        input_output_aliases={i: 2 * ns + i for i in range(n)},
        compiler_params=pltpu.CompilerParams(**SPLIT_PARAMS))(*[_hbm(b) for b in buffers], after)
    return list(outs[:ns]), list(outs[ns:2 * ns]), list(outs[2 * ns:2 * ns + n]), outs[-1]


def _split_wait(copies_fn, send_sems, recv_sems, buffers, after, sem_rows, *, name):
    n, ns = len(buffers), len(send_sems)
    cols = ns // sem_rows

    def body(*refs):
        sems = refs[n:n + 2 * ns]
        copies = copies_fn(refs[:n], _sem_rows(sems[:ns], cols), _sem_rows(sems[ns:], cols))
        for cp in copies:
            cp.wait_send()
        for cp in copies:
            cp.wait_recv()

    outs = pl.pallas_call(
        body, name=name, out_shape=tuple(pltpu.HBM(b.shape, b.dtype) for b in buffers),
        in_specs=(HBM_SPEC,) * n + (SEM_SPEC,) * (2 * ns) + (pl.BlockSpec(memory_space=pl.ANY),),
        out_specs=(HBM_SPEC,) * n, input_output_aliases={i: i for i in range(n)},
        compiler_params=pltpu.CompilerParams(**SPLIT_PARAMS))(*buffers, *send_sems, *recv_sems, after)
    return list(outs)


def _sem_rows(sems, cols):
    return [sems[i:i + cols] for i in range(0, len(sems), cols)]


def _empty_hbm(shape, dtype):
    return _hbm(lax.empty(shape, dtype))


class _SplitGather:
    def __init__(self, fulls, axes, tag):
        self.axes, self.tag, self.nt = list(axes), tag, len(fulls)
        self.sizes = [f.shape[ax] // N_DEV for f, ax in zip(fulls, axes)]
        self.fulls = list(fulls)

    def _slot(self, ref, t, dev):
        return _shard_of(ref, self.axes[t], 4 * dev[0] + 2 * dev[1] + dev[2], self.sizes[t])

    def _first_copies(self, refs, send_sems, recv_sems):
        x, y, c = _coords()
        peers = [(x, y, 1 - c), (1 - x, y, c), (x, 1 - y, c), (1 - x, 1 - y, c)]
        return [pltpu.make_async_remote_copy(
            src_ref=self._slot(refs[t], t, (x, y, c)), dst_ref=self._slot(refs[t], t, (x, y, c)),
            send_sem=send_sems[t][k], recv_sem=recv_sems[t][k], device_id=peer, device_id_type=MESH)
            for t in range(self.nt) for k, peer in enumerate(peers)]

    def _forward_copies(self, refs, send_sems, recv_sems):
        x, y, c = _coords()
        chips = [(1 - x, y), (x, 1 - y), (1 - x, 1 - y)]
        return [pltpu.make_async_remote_copy(
            src_ref=self._slot(refs[t], t, (*chip, c)), dst_ref=self._slot(refs[t], t, (*chip, c)),
            send_sem=send_sems[t][j], recv_sem=recv_sems[t][j], device_id=(x, y, 1 - c), device_id_type=MESH)
            for t in range(self.nt) for j, chip in enumerate(chips)]

    def first(self, after):
        self.s1, self.r1, self.fulls, token = _split_start(
            self._first_copies, self.fulls, (self.nt, 4), after, name=f"comm_gather1_start_{self.tag}")
        return token

    def forward(self, after):
        bufs = _split_wait(self._first_copies, self.s1, self.r1, self.fulls, after, self.nt,
                           name=f"comm_gather1_wait_{self.tag}")
        self.s2, self.r2, self.fulls, token = _split_start(
            self._forward_copies, bufs, (self.nt, 3), after, name=f"comm_gather2_start_{self.tag}")
        return token

    def finish(self, after):
        return _split_wait(self._forward_copies, self.s2, self.r2, self.fulls, after, self.nt,
                           name=f"comm_gather2_wait_{self.tag}")


class _SplitPairExchange:
    def __init__(self, grads, axes, tag):
        self.nt, self.tag, self.axes = len(grads), tag, list(axes)
        self.grads = list(grads)
        self.sizes = [g.shape[ax] // N_DEV for g, ax in zip(grads, axes)]

    def _copies(self, refs, send_sems, recv_sems):
        nt = self.nt
        x, y, c = _coords()
        return [pltpu.make_async_remote_copy(
            src_ref=_shard_of(refs[t], self.axes[t], 2 * q + 1 - c, self.sizes[t]), dst_ref=refs[nt + t].at[q],
            send_sem=send_sems[t][q], recv_sem=recv_sems[t][q], device_id=(x, y, 1 - c), device_id_type=MESH)
            for t in range(nt) for q in range(N_CHIPS)]

    def start(self):
        landing = []
        for g, ax in zip(self.grads, self.axes):
            dims = list(g.shape)
            dims[ax] //= N_DEV
            landing.append(_empty_hbm((N_CHIPS, *dims), g.dtype))
        self.s, self.r, self.bufs, token = _split_start(
            self._copies, self.grads + landing, (self.nt, N_CHIPS), self.grads[-1],
            name=f"comm_rs_pair_start_{self.tag}")
        return token

    def finish(self, after):
        bufs = _split_wait(self._copies, self.s, self.r, self.bufs, after, self.nt,
                           name=f"comm_rs_pair_wait_{self.tag}")
        return bufs[:self.nt], bufs[self.nt:]


class _ReducePipeline:
    def __init__(self, core):
        self.core, self.items, self.done, self.now = core, [], [], 0

    def add(self, keys, grads, layer):
        axes = [SHARD_AXIS[k] for k in keys]
        pair = _SplitPairExchange([grads[k] for k in keys], axes, f"{keys[0]}{layer}")
        token = pair.start()
        self.items.append(dict(keys=keys, layer=layer, axes=axes, pair=pair, state="pair", since=self.now))
        return [token]

    def tick(self, after, flush=False):
        self.now += 1
        deps = []
        for it in self.items:
            if it["state"] == "pair" and it["since"] < self.now:
                grads, sib = it["pair"].finish(after)
                sums = [_pair_sum(g, s_, self.core, ax, name="pair_sum_" + k)
                        for k, g, s_, ax in zip(it["keys"], grads, sib, it["axes"])]
                it["chip"] = _SplitChipExchange(sums, f"{it['keys'][0]}{it['layer']}")
                deps.append(it["chip"].start())
                it.update(state="chip", since=self.now)
            elif it["state"] == "chip" and (flush or self.now - it["since"] >= 2):
                sums, remote = it["chip"].finish(after)
                self.done.append((it["keys"], it["layer"], sums, remote))
                it["state"] = "done"
        return deps

    def take_done(self):
        out, self.done = self.done, []
        return out


class _SplitChipExchange:
    def __init__(self, sums, tag):
        self.nt, self.tag = len(sums), tag
        self.sums = list(sums)

    def _copies(self, refs, send_sems, recv_sems):
        nt = self.nt
        x, y, c = _coords()
        copies = []
        for t in range(nt):
            for k in range(1, N_CHIPS):
                px, py = _flip(x, k & 2), _flip(y, k & 1)
                copies.append(pltpu.make_async_remote_copy(
                    src_ref=refs[t].at[2 * px + py], dst_ref=refs[nt + t].at[k - 1], send_sem=send_sems[t][k - 1],
                    recv_sem=recv_sems[t][k - 1], device_id=(px, py, c), device_id_type=MESH))
        return copies

    def start(self):
        landing = [_empty_hbm((N_CHIPS - 1,) + s.shape[1:], s.dtype) for s in self.sums]
        self.s, self.r, self.bufs, token = _split_start(
            self._copies, self.sums + landing, (self.nt, N_CHIPS - 1), self.sums[-1],
            name=f"comm_rs_chip_start_{self.tag}")
        return token

    def finish(self, after):
        bufs = _split_wait(self._copies, self.s, self.r, self.bufs, after, self.nt,
                           name=f"comm_rs_chip_wait_{self.tag}")
        return bufs[:self.nt], bufs[self.nt:]


def _adam_math(g, w, m, v):
    m2 = ADAM_B1 * m + (1.0 - ADAM_B1) * g
    v2 = ADAM_B2 * v + (1.0 - ADAM_B2) * (g * g)
    m_hat = m2 / (1.0 - ADAM_B1 ** ADAM_STEP)
    v_hat = v2 / (1.0 - ADAM_B2 ** ADAM_STEP)
    delta = -ADAM_LR * (m_hat / (jnp.sqrt(v_hat) + ADAM_EPS) + ADAM_WD * w)
    return delta, m2, v2


def _adamw_sharded(chip_sums, remote, chip, w, m, v, layer, prev, deps, *, name):
    nl, r, c = w.shape
    tr = _rows(r, c)

    def body(*refs):
        p_ref, r0_ref, r1_ref, r2_ref, w_ref, m_ref, v_ref = refs[1:8]
        g_out, d_out, m_out, v_out = refs[-4:]
        g = ((p_ref[...].astype(F32) + r0_ref[...].astype(F32)) + r1_ref[...].astype(F32)) + r2_ref[...].astype(F32)
        g_out[...] = g
        d_out[...], m_out[...], v_out[...] = _adam_math(g, w_ref[...], m_ref[...], v_ref[...])

    pspec = pl.BlockSpec((None, tr, c), lambda i, chip_ref: (chip_ref[0], i, 0))

    def rspec(k):
        return pl.BlockSpec((None, tr, c), lambda i, chip_ref: (k, i, 0))

    wspec = pl.BlockSpec((None, tr, c), lambda i, chip_ref: (layer, i, 0))
    in_specs = [pspec, rspec(0), rspec(1), rspec(2), wspec, wspec, wspec]
    args = [chip, chip_sums, remote, remote, remote, w, m, v]
    aliases = {}
    if prev is not None:
        in_specs += [pl.BlockSpec(memory_space=pl.ANY)] * 4
        aliases = {len(args) + i: i for i in range(4)}
        args += list(prev)
    in_specs += [pl.BlockSpec(memory_space=pl.ANY)] * len(deps)
    args += list(deps)
    grid_spec = pltpu.PrefetchScalarGridSpec(num_scalar_prefetch=1, grid=(r // tr,), in_specs=in_specs,
                                             out_specs=(wspec,) * 4)
    shp = jax.ShapeDtypeStruct(w.shape, F32)
    return _pcall(body, name=name, out_shape=(shp,) * 4, grid_spec=grid_spec, aliases=aliases)(*args)


def _adamw_local(g, w, m, v, *, name):
    nl, r, c = w.shape
    tr = _rows(r, c)

    def body(g_ref, w_ref, m_ref, v_ref, d_out, m_out, v_out):
        d_out[...], m_out[...], v_out[...] = _adam_math(g_ref[...], w_ref[...], m_ref[...], v_ref[...])

    spec = pl.BlockSpec((None, tr, c), lambda l, i: (l, i, 0))
    shp = jax.ShapeDtypeStruct(w.shape, F32)
    return _pcall(body, name=name, out_shape=(shp,) * 3, grid=(nl, r // tr), in_specs=[spec] * 4,
                  out_specs=(spec,) * 3)(g, w, m, v)


def _adamw_replicated(parts, w, m, v, *, name):
    n = w.shape[1]

    def body(p_ref, w_ref, m_ref, v_ref, g_out, d_out, m_out, v_out):
        g = p_ref[0]
        for k in range(1, N_DEV):
            g = g + p_ref[k]
        g_out[...] = g
        d_out[...], m_out[...], v_out[...] = _adam_math(g, w_ref[...], m_ref[...], v_ref[...])

    vm = pl.BlockSpec(memory_space=pltpu.VMEM)
    shp = jax.ShapeDtypeStruct((1, n), F32)
    return _pcall(body, name=name, out_shape=(shp,) * 4, in_specs=[vm] * 4, out_specs=(vm,) * 4)(parts, w, m, v)


def _group_views(qk, proj, g, dil, seq):
    if dil == 1:
        return (qk, qk, proj), (0, A_HEADS, 2 * A_HEADS)
    length = seq // dil
    lo = g * GROUP_W
    q = qk[:, lo:lo + GROUP_W].reshape(length, dil * GROUP_W)
    k = qk[:, A_W + lo:A_W + lo + GROUP_W].reshape(length, dil * GROUP_W)
    v = proj[:, OFF_VA + lo:OFF_VA + lo + GROUP_W].astype(BF16).reshape(length, dil * GROUP_W)
    return (q, k, v), (0, 0, 0)


def _mod_rows(mod, d):
    return [mod[:, i * d:(i + 1) * d] for i in range(6)]


MIXER_W = ("w_in", "w_branch_a", "w_branch_b", "w_out")
FFN_W = ("w_gate_up", "w_down")
SHARD_AXIS = {"w_in": 1, "w_branch_a": 1, "w_branch_b": 1, "w_out": 0, "w_gate_up": 1, "w_down": 0}


def _mixer_fwd_a(h, mod, g1, gains, w_in, cos2, sin2, deps):
    seq, d = h.shape
    sh1, sc1 = _mod_rows(mod, d)[:2]
    u = _rmsmod_fwd(h, g1, sc1, sh1, name="rmsmod_fwd", deps=deps)
    proj = _mm(u, w_in, name="mm_in")
    qk = _qkrope_fwd(proj, gains, cos2, sin2, name="qkrope_fwd")
    os_, lses = [], []
    for g, dil in enumerate(DILATIONS):
        arrs, offs = _group_views(qk, proj, g, dil, seq)
        o, lse = _dil_fwd(*arrs, offs, seq // dil, dil, name=f"dil_fwd_{dil}")
        os_.append(o.reshape(seq, GROUP_W))
        lses.append(lse.reshape(seq, GROUP_W))
    o_a = _combine_fwd(os_, lses, name="combine_fwd")
    o_b = _sb_fwd(proj, name="sb_fwd")
    return dict(h_in=h, u=u, proj=proj, qk=qk, os=os_, lses=lses, o_a=o_a, o_b=o_b)


def _mixer_fwd_b(sv, mod, wts, deps):
    d = sv["h_in"].shape[1]
    ga1 = _mod_rows(mod, d)[2]
    y_a = _mm(sv["o_a"], wts["w_branch_a"], name="mm_branch", deps=deps)
    y_b = _mm(sv["o_b"], wts["w_branch_b"], name="mm_branch")
    merged = _merge_fwd(sv["proj"], y_a, y_b, name="merge_fwd")
    t = _mm(merged, wts["w_out"], name="mm_out")
    h_mid = _resid_gate(sv["h_in"], ga1, t, name="resid_gate")
    sv.update(y_a=y_a, y_b=y_b, merged=merged, t=t, h_mid=h_mid)
    return h_mid


def _ffn_fwd_a(sv, mod, g2, w_gate_up, deps):
    d = sv["h_mid"].shape[1]
    sh2, sc2 = _mod_rows(mod, d)[3:5]
    u2 = _rmsmod_fwd(sv["h_mid"], g2, sc2, sh2, name="rmsmod_fwd", deps=deps)
    a, g, u = _mm_swiglu(u2, w_gate_up, name="mm_gate_up")
    sv.update(u2=u2, g=g, up=u, a=a)
    return a


def _ffn_fwd_b(sv, mod, w_down, deps):
    d = sv["h_mid"].shape[1]
    ga2 = _mod_rows(mod, d)[5]
    f = _mm(sv["a"], w_down, name="mm_down", deps=deps)
    sv["f"] = f
    return _resid_gate(sv["h_mid"], ga2, f, name="resid_gate")


def _wgrad(act, dout, key):
    return _mm(act, dout, ta=True, out_dtype=BF16, name="mm_wgrad_" + key)


def _ffn_bwd(dh, sv, mod, g2, wts, deps, hook):
    d = dh.shape[1]
    sc2, ga2 = _mod_rows(mod, d)[4:6]
    df, dgate2 = _resid_gate_bwd(dh, sv["f"], ga2, name="resid_gate_bwd", deps=deps)
    da = _mm(df, wts["w_down"], tb=True, out_dtype=BF16, name="mm_down_t")
    grads = {"w_down": _wgrad(sv["a"], df, "w_down")}
    dgu = _swiglu_bwd(sv["g"], sv["up"], da, name="swiglu_bwd")
    du2 = _mm(dgu, wts["w_gate_up"], tb=True, name="mm_gate_up_t", deps=hook(dgu))
    grads["w_gate_up"] = _wgrad(sv["u2"], dgu, "w_gate_up")
    dh_mid, dsh2, dsc2, dg2 = _rmsmod_bwd(du2, sv["h_mid"], g2, sc2, dh, name="rmsmod_bwd")
    return dh_mid, [dsh2, dsc2, dgate2], dg2, grads


def _mixer_bwd(dh_mid, sv, mod, g1, gains, wts, cos2, sin2, deps, hook):
    seq, d = dh_mid.shape
    sc1, ga1 = _mod_rows(mod, d)[1:3]
    dt, dgate1 = _resid_gate_bwd(dh_mid, sv["t"], ga1, name="resid_gate_bwd", deps=deps)
    dmerged = _mm(dt, wts["w_out"], tb=True, name="mm_out_t")
    grads = {"w_out": _wgrad(sv["merged"], dt, "w_out")}
    dy_a, dy_b, dga, dgb = _merge_bwd(dmerged, sv["proj"], sv["y_a"], sv["y_b"], name="merge_bwd")
    do_a = _mm(dy_a, wts["w_branch_a"], tb=True, name="mm_branch_t")
    do_b = _mm(dy_b, wts["w_branch_b"], tb=True, name="mm_branch_t")
    grads["w_branch_a"] = _wgrad(sv["o_a"], dy_a, "w_branch_a")
    grads["w_branch_b"] = _wgrad(sv["o_b"], dy_b, "w_branch_b")
    dqb, dkb, dvb = _sb_bwd(sv["proj"], do_b, name="sb_bwd")
    comb = _combine_bwd(do_a, sv["os"], sv["lses"], name="combine_bwd", deps=hook(dqb, grads))
    grads = {}
    dos, dls = comb[:3], comb[3:]
    dqs, dks, dvs = [], [], []
    for g, dil in enumerate(DILATIONS):
        length = seq // dil
        arrs, offs = _group_views(sv["qk"], sv["proj"], g, dil, seq)
        view = (length, dil * GROUP_W)
        dq, dk, dv = _dil_bwd(*arrs, offs, sv["os"][g].reshape(view), sv["lses"][g].reshape(view),
                              dos[g].reshape(view), dls[g].reshape(view), length, dil, name=f"dil_bwd_{dil}")
        dqs.append(dq.reshape(seq, GROUP_W))
        dks.append(dk.reshape(seq, GROUP_W))
        dvs.append(dv.reshape(seq, GROUP_W))
    dqk, dgains = _qkrope_bwd(jnp.concatenate(dqs + dks, axis=1), sv["proj"], gains, cos2, sin2,
                              name="qkrope_bwd")
    dproj = jnp.concatenate(
        [dqk] + [t_.astype(BF16) for t_ in dvs + [dqb, dkb, dvb]] + [dga, dgb], axis=1)
    du = _mm(dproj, wts["w_in"], tb=True, name="mm_in_t")
    grads["w_in"] = _wgrad(sv["u"], dproj, "w_in")
    dh_in, dsh1, dsc1, dg1 = _rmsmod_bwd(du, sv["h_in"], g1, sc1, dh_mid, name="rmsmod_bwd")
    return dh_in, [dsh1, dsc1, dgate1], dg1, dgains, grads


def kernel(x, c, w_ada, b_ada, norm1_g, norm2_g, w_in, qn_g, kn_g, w_branch_a, w_branch_b, w_out, w_gate_up, w_down, loss_target, m_w_ada, m_b_ada, m_norm1_g, m_norm2_g, m_w_in, m_qn_g, m_kn_g, m_w_branch_a, m_w_branch_b, m_w_out, m_w_gate_up, m_w_down, v_w_ada, v_b_ada, v_norm1_g, v_norm2_g, v_w_in, v_qn_g, v_kn_g, v_w_branch_a, v_w_branch_b, v_w_out, v_w_gate_up, v_w_down):
    seq, d = x.shape[1], x.shape[2]
    depth = w_in.shape[0]
    weights = dict(w_in=w_in, w_branch_a=w_branch_a, w_branch_b=w_branch_b, w_out=w_out, w_gate_up=w_gate_up,
                   w_down=w_down)
    moments_m = dict(w_in=m_w_in, w_branch_a=m_w_branch_a, w_branch_b=m_w_branch_b, w_out=m_w_out,
                     w_gate_up=m_w_gate_up, w_down=m_w_down)
    moments_v = dict(w_in=v_w_in, w_branch_a=v_w_branch_a, w_branch_b=v_w_branch_b, w_out=v_w_out,
                     w_gate_up=v_w_gate_up, w_down=v_w_down)
    xi, yi, ci = _coords()
    me = 4 * xi + 2 * yi + ci
    core = jnp.reshape(ci, (1,)).astype(jnp.int32)
    chip = jnp.reshape(2 * xi + yi, (1,)).astype(jnp.int32)

    ada_w = w_ada.shape[2]
    c_act = _small_allgather(c, name="comm_gather_c", silu=True).reshape(N_DEV, d)
    c_pad = jnp.concatenate([c_act, jnp.zeros_like(c_act)], axis=0).astype(BF16)
    bias = lax.dynamic_slice(b_ada, (0, me * ada_w), (depth, ada_w))
    mod_part = jnp.stack([_mm(c_pad, w_ada[l], name="mm_ada")[:N_DEV] for l in range(depth)]) + bias[:, None, :]
    mod_all = _small_allgather(mod_part.reshape(1, depth * N_DEV * ada_w), name="comm_gather_mod")
    mod_all = mod_all.reshape(N_DEV, depth, N_DEV, ada_w)
    mod_mine = lax.dynamic_index_in_dim(mod_all, me, axis=2, keepdims=False)
    mods = jnp.transpose(mod_mine, (1, 0, 2)).reshape(depth, 1, 6 * d)

    cos2, sin2 = _rope_tables(seq)
    gains = [jnp.stack([qn_g[l], kn_g[l]])[:, None, :] for l in range(depth)]
    g1s = [norm1_g[l][None] for l in range(depth)]
    g2s = [norm2_g[l][None] for l in range(depth)]

    me_arr = jnp.reshape(me, (1,)).astype(jnp.int32)

    def placed(keys, l):
        return [_cast_place(weights[k], l, SHARD_AXIS[k], me_arr, name="cast_place_" + k) for k in keys]

    def gather_of(keys, l, tag):
        return _SplitGather(placed(keys, l), [SHARD_AXIS[k] for k in keys], f"{tag}{l}")

    groups = [("w_in", 0, MIXER_W[:1]), ("rest", 0, MIXER_W[1:]), ("ffn", 0, FFN_W)]
    for l in range(1, depth):
        groups += [("mixer", l, MIXER_W), ("ffn", l, FFN_W)]
    gathers, token = {}, mods
    for tag, l, keys in groups:
        gathers[tag, l] = gather_of(keys, l, tag)
        token = gathers[tag, l].first(after=token)
    token = gathers["w_in", 0].forward(after=token)
    wm = {"w_in": gathers["w_in", 0].finish(after=token)[0]}
    deps = []
    h = x[0]
    saved, full = [], []
    for l in range(depth):
        last = l + 1 == depth
        sv = _mixer_fwd_a(h, mods[l], g1s[l], gains[l], wm["w_in"], cos2, sin2, deps)
        deps = [gathers["ffn", l].forward(after=sv["o_b"])]
        if l == 0:
            deps.append(gathers["rest", 0].forward(after=sv["o_b"]))
            wm.update(zip(MIXER_W[1:], gathers["rest", 0].finish(after=sv["o_b"])))
        h_mid = _mixer_fwd_b(sv, mods[l], wm, deps)
        wf = dict(zip(FFN_W, gathers["ffn", l].finish(after=h_mid)))
        a = _ffn_fwd_a(sv, mods[l], g2s[l], wf["w_gate_up"], [])
        deps = [] if last else [gathers["mixer", l + 1].forward(after=a)]
        h = _ffn_fwd_b(sv, mods[l], wf["w_down"], deps)
        saved.append(sv)
        full.append({**wm, **wf})
        if not last:
            wm = dict(zip(MIXER_W, gathers["mixer", l + 1].finish(after=h)))
        deps = []
    loss_part, dh = _loss_fwd(h, loss_target[0], name="loss")
    loss = lax.psum(loss_part[0, 0], ("x", "y", "c"))

    pipe = _ReducePipeline(core)
    dmods, dg1s, dg2s, dgains = [None] * depth, [None] * depth, [None] * depth, [None] * depth
    deps = []
    for l in reversed(range(depth)):
        dh_mid, dmod_f, dg2s[l], grads = _ffn_bwd(dh, saved[l], mods[l], g2s[l], full[l], deps, pipe.tick)
        deps = pipe.tick(dh_mid) + pipe.add(FFN_W, grads, l)
        dh, dmod_m, dg1s[l], dgains[l], grads = _mixer_bwd(
            dh_mid, saved[l], mods[l], g1s[l], gains[l], full[l], cos2, sin2, deps,
            lambda after, early, l=l: pipe.tick(after) + pipe.add(MIXER_W[1:], early, l))
        dmods[l] = jnp.concatenate(dmod_m + dmod_f, axis=1)
        deps = pipe.tick(dh) + pipe.add(MIXER_W[:1], grads, l)
    grad_x = dh[None]

    stacked = {}

    def update(deps):
        last = None
        for keys, l, sums, remote in pipe.take_done():
            for k, p_, r_ in zip(keys, sums, remote):
                stacked[k] = _adamw_sharded(p_, r_, chip, weights[k], moments_m[k], moments_v[k], l,
                                            stacked.get(k), deps, name="adamw_" + k)
                deps, last = [], stacked[k][0]
        return last

    after = update(deps)

    small = jnp.concatenate(
        dmods + dg1s + dg2s + [dgains[l][0] for l in range(depth)] + [dgains[l][1] for l in range(depth)], axis=1)
    small_all = _small_allgather(small, name="comm_gather_small", deps=[after])
    update(pipe.tick(small_all))

    def pack(b, n1, n2, qn, kn):
        return jnp.concatenate([t_.reshape(1, -1) for t_ in (b, n1, n2, qn, kn)], axis=1)

    sg, sd, sm, sv_ = _adamw_replicated(small_all, pack(b_ada, norm1_g, norm2_g, qn_g, kn_g),
                                        pack(m_b_ada, m_norm1_g, m_norm2_g, m_qn_g, m_kn_g),
                                        pack(v_b_ada, v_norm1_g, v_norm2_g, v_qn_g, v_kn_g), name="adamw_replicated")

    def unpack(p):
        sizes = [depth * 6 * d, depth * d, depth * d, depth * HEAD_DIM, depth * HEAD_DIM]
        shapes = [b_ada.shape, norm1_g.shape, norm2_g.shape, qn_g.shape, kn_g.shape]
        out, off = [], 0
        for n, shp in zip(sizes, shapes):
            out.append(p[0, off:off + n].reshape(shp))
            off += n
        return dict(zip(("b_ada", "norm1_g", "norm2_g", "qn_g", "kn_g"), out))

    ug, ud, um, uv = unpack(sg), unpack(sd), unpack(sm), unpack(sv_)
    res = {k: dict(g=ug[k], d=ud[k], m=um[k], v=uv[k]) for k in ug}

    dmod_all = small_all[:, 0, :depth * 6 * d].reshape(N_DEV, depth, 6 * d)
    g_ada = None
    for l in range(depth):
        dm = lax.dynamic_slice(dmod_all[:, l, :], (0, me * ada_w), (N_DEV, ada_w))
        dm = jnp.concatenate([dm, jnp.zeros_like(dm)], axis=0).astype(BF16)
        g_ada = _mm(c_pad, dm, ta=True, name="mm_wgrad_ada", stack=(l, depth, g_ada))
    d_ada, m_ada, v_ada = _adamw_local(g_ada, w_ada, m_w_ada, v_w_ada, name="adamw_local")
    res["w_ada"] = dict(g=g_ada, d=d_ada, m=m_ada, v=v_ada)

    after = update(pipe.tick(d_ada))
    update(pipe.tick(d_ada if after is None else after, flush=True))
    for k, (g_, d_, m_, v_) in stacked.items():
        res[k] = dict(g=g_, d=d_, m=m_, v=v_)

    order = ("w_ada", "b_ada", "norm1_g", "norm2_g", "w_in", "qn_g", "kn_g", "w_branch_a", "w_branch_b", "w_out",
             "w_gate_up", "w_down")
    return (loss, grad_x, *[res[k]["g"] for k in order], *[res[k]["d"] for k in order],
            *[res[k]["m"] for k in order], *[res[k]["v"] for k in order])
```

```python
import functools

import jax
import jax.numpy as jnp
from jax import lax
from jax.experimental import pallas as pl
from jax.experimental.pallas import tpu as pltpu

F32 = jnp.float32
BF16 = jnp.bfloat16

HEAD_DIM = 128
BLOCK = 128
DILATIONS = (1, 4, 16)
HEADS_PER_GROUP = 4
A_HEADS = 12
SB_HEADS = 4
GROUP_W = HEADS_PER_GROUP * HEAD_DIM
A_W = A_HEADS * HEAD_DIM
B_W = SB_HEADS * HEAD_DIM
OFF_QA, OFF_KA, OFF_VA = 0, A_W, 2 * A_W
OFF_QB, OFF_KB, OFF_VB = 3 * A_W, 3 * A_W + B_W, 3 * A_W + 2 * B_W
OFF_GATES = 3 * A_W + 3 * B_W
ROPE_THETA = 10000.0
EPS = 1e-6
ATT_SCALE = HEAD_DIM ** -0.5
MASKED = -1e30

ADAM_LR, ADAM_B1, ADAM_B2, ADAM_EPS, ADAM_WD, ADAM_STEP = 0.001, 0.9, 0.999, 1e-08, 0.01, 10

N_DEV = 8
N_CHIPS = 4
V7X_VMEM_LIMIT_BYTES = 56 * 1024 * 1024
ELEMWISE_BLOCK_BYTES = 2 * 1024 * 1024
MESH = pl.DeviceIdType.MESH

NN = (((1,), (0,)), ((), ()))
NT = (((1,), (1,)), ((), ()))
TN = (((0,), (0,)), ((), ()))


def _dot(a, b, dims=NN):
    return lax.dot_general(a, b, dims, preferred_element_type=F32)


def _tile(n, cap, mult=128):
    best = None
    for t in range(mult, min(n, cap) + 1, mult):
        if n % t == 0:
            best = t
    if best is None:
        assert n <= 2 * cap, (n, cap)
        return n
    return best


def _rows(r, c):
    return _tile(r, max(16, ELEMWISE_BLOCK_BYTES // (4 * c)), 16)


_ORDER = {"token": None}
TOKEN = jax.ShapeDtypeStruct((8, 128), F32)


def _take_token():
    prev = _ORDER["token"]
    return [] if prev is None else [prev]


def _pcall(body, *, name, out_shape, grid=None, in_specs=None, out_specs=None, scratch=(), aliases=None,
           prefetch=0, deps=()):
    single = not isinstance(out_shape, (tuple, list))
    out_shapes = [out_shape] if single else list(out_shape)
    out_specs = [out_specs] if single else list(out_specs)
    extra = list(deps) + _take_token()
    n_in, n_extra, n_out = prefetch + len(in_specs), len(extra), len(out_shapes)

    def wrapped(*refs):
        token = refs[n_in + n_extra + n_out]
        token[...] = jnp.zeros_like(token)
        return body(*refs[:n_in], *refs[n_in + n_extra:n_in + n_extra + n_out], *refs[n_in + n_extra + n_out + 1:])

    in_specs = list(in_specs) + [pl.BlockSpec(memory_space=pl.ANY)] * n_extra
    if grid is None:
        out_specs.append(pl.BlockSpec(memory_space=pltpu.VMEM))
    else:
        out_specs.append(pl.BlockSpec(TOKEN.shape, lambda *_: (0, 0)))
    kwargs = dict(name=name, out_shape=out_shapes + [TOKEN], input_output_aliases=aliases or {},
                  compiler_params=pltpu.CompilerParams(vmem_limit_bytes=V7X_VMEM_LIMIT_BYTES))
    if prefetch:
        call = pl.pallas_call(wrapped, grid_spec=pltpu.PrefetchScalarGridSpec(
            num_scalar_prefetch=prefetch, grid=grid, in_specs=in_specs, out_specs=out_specs,
            scratch_shapes=list(scratch)), **kwargs)
    else:
        if grid is not None:
            kwargs["grid"] = grid
        call = pl.pallas_call(wrapped, in_specs=in_specs, out_specs=out_specs, scratch_shapes=list(scratch), **kwargs)

    def run(*args):
        outs = call(*args, *extra)
        _ORDER["token"] = outs[-1]
        return outs[0] if single else tuple(outs[:-1])

    return run


def _mm(a, b, *, name, ta=False, tb=False, out_dtype=F32, caps=(1024, 1024, 3072), stack=None, deps=()):
    kdim, m = a.shape if ta else a.shape[::-1]
    n, k2 = b.shape if tb else b.shape[::-1]
    assert kdim == k2, (a.shape, b.shape, ta, tb)
    tm, tn, tk = _tile(m, caps[0]), _tile(n, caps[1]), _tile(kdim, caps[2])
    nk = kdim // tk
    dims = (((0 if ta else 1,), (1 if tb else 0,)), ((), ()))

    def body(*refs):
        a_ref, b_ref = refs[0], refs[1]
        part = _dot(a_ref[...].astype(BF16), b_ref[...].astype(BF16), dims)
        if nk == 1:
            o_ref = refs[-1]
            o_ref[...] = part.astype(o_ref.dtype)
            return
        o_ref, acc_ref = refs[-2], refs[-1]
        k = pl.program_id(2)

        @pl.when(k == 0)
        def _():
            acc_ref[...] = part

        @pl.when(k > 0)
        def _():
            acc_ref[...] += part

        @pl.when(k == nk - 1)
        def _():
            o_ref[...] = acc_ref[...].astype(o_ref.dtype)

    a_spec = (pl.BlockSpec((tk, tm), lambda i, j, k: (k, i)) if ta
              else pl.BlockSpec((tm, tk), lambda i, j, k: (i, k)))
    b_spec = (pl.BlockSpec((tn, tk), lambda i, j, k: (j, k)) if tb
              else pl.BlockSpec((tk, tn), lambda i, j, k: (k, j)))
    ins, in_specs, aliases = [a, b], [a_spec, b_spec], {}
    if stack is None:
        out_shape = jax.ShapeDtypeStruct((m, n), out_dtype)
        out_spec = pl.BlockSpec((tm, tn), lambda i, j, k: (i, j))
    else:
        layer, n_layers, buf = stack
        out_shape = jax.ShapeDtypeStruct((n_layers, m, n), out_dtype)
        out_spec = pl.BlockSpec((None, tm, tn), lambda i, j, k: (layer, i, j))
        if buf is not None:
            ins.append(buf)
            in_specs.append(pl.BlockSpec(memory_space=pl.ANY))
            aliases = {2: 0}
    scratch = [] if nk == 1 else [pltpu.VMEM((tm, tn), F32)]
    return _pcall(body, name=name, out_shape=out_shape, grid=(m // tm, n // tn, nk), in_specs=in_specs,
                  out_specs=out_spec, scratch=scratch, aliases=aliases, deps=deps)(*ins)


def _rmsmod_fwd(h, g, scale, shift, *, name, deps=()):
    s, d = h.shape
    ts = _rows(s, d)

    def body(h_ref, g_ref, sc_ref, sh_ref, u_ref):
        hf = h_ref[...]
        r = lax.rsqrt(jnp.mean(hf * hf, axis=-1, keepdims=True) + EPS)
        u_ref[...] = (((hf * r) * g_ref[...]) * (1.0 + sc_ref[...]) + sh_ref[...]).astype(BF16)

    row = pl.BlockSpec((ts, d), lambda i: (i, 0))
    vec = pl.BlockSpec((1, d), lambda i: (0, 0))
    return _pcall(body, name=name, out_shape=jax.ShapeDtypeStruct((s, d), BF16), grid=(s // ts,),
                  in_specs=[row, vec, vec, vec], out_specs=row, deps=deps)(h, g, scale, shift)


def _rmsmod_bwd(du, h, g, scale, dres, *, name):
    s, d = h.shape
    ts = _rows(s, d)

    def body(du_ref, h_ref, g_ref, sc_ref, dres_ref, dh_ref, dsh_ref, dsc_ref, dg_ref):
        @pl.when(pl.program_id(0) == 0)
        def _():
            dsh_ref[...] = jnp.zeros_like(dsh_ref)
            dsc_ref[...] = jnp.zeros_like(dsc_ref)
            dg_ref[...] = jnp.zeros_like(dg_ref)

        hf, duf, gain = h_ref[...], du_ref[...], g_ref[...]
        r = lax.rsqrt(jnp.mean(hf * hf, axis=-1, keepdims=True) + EPS)
        xh = hf * r
        dn = duf * (1.0 + sc_ref[...])
        dsh_ref[...] += jnp.sum(duf, axis=0, keepdims=True)
        dsc_ref[...] += jnp.sum(duf * (xh * gain), axis=0, keepdims=True)
        dg_ref[...] += jnp.sum(dn * xh, axis=0, keepdims=True)
        dxh = dn * gain
        dh_ref[...] = dres_ref[...] + r * (dxh - xh * jnp.mean(dxh * xh, axis=-1, keepdims=True))

    row = pl.BlockSpec((ts, d), lambda i: (i, 0))
    vec = pl.BlockSpec((1, d), lambda i: (0, 0))
    vshape = jax.ShapeDtypeStruct((1, d), F32)
    return _pcall(body, name=name, out_shape=(jax.ShapeDtypeStruct((s, d), F32), vshape, vshape, vshape),
                  grid=(s // ts,), in_specs=[row, row, vec, vec, row],
                  out_specs=(row, vec, vec, vec))(du, h, g, scale, dres)


def _resid_gate(h, gate, t, *, name):
    s, d = h.shape
    ts = _rows(s, d)

    def body(h_ref, g_ref, t_ref, o_ref):
        o_ref[...] = h_ref[...] + g_ref[...] * t_ref[...]

    row = pl.BlockSpec((ts, d), lambda i: (i, 0))
    vec = pl.BlockSpec((1, d), lambda i: (0, 0))
    return _pcall(body, name=name, out_shape=jax.ShapeDtypeStruct((s, d), F32), grid=(s // ts,),
                  in_specs=[row, vec, row], out_specs=row)(h, gate, t)


def _resid_gate_bwd(dh, t, gate, *, name, deps=()):
    s, d = dh.shape
    ts = _rows(s, d)

    def body(dh_ref, t_ref, g_ref, dt_ref, dg_ref):
        @pl.when(pl.program_id(0) == 0)
        def _():
            dg_ref[...] = jnp.zeros_like(dg_ref)

        dhf = dh_ref[...]
        dt_ref[...] = (dhf * g_ref[...]).astype(BF16)
        dg_ref[...] += jnp.sum(dhf * t_ref[...], axis=0, keepdims=True)

    row = pl.BlockSpec((ts, d), lambda i: (i, 0))
    vec = pl.BlockSpec((1, d), lambda i: (0, 0))
    return _pcall(body, name=name,
                  out_shape=(jax.ShapeDtypeStruct((s, d), BF16), jax.ShapeDtypeStruct((1, d), F32)),
                  grid=(s // ts,), in_specs=[row, row, vec], out_specs=(row, vec), deps=deps)(dh, t, gate)


def _merge_fwd(proj, y_a, y_b, *, name):
    s, d = y_a.shape
    ts = _rows(s, d)
    ga_blk = OFF_GATES // d

    def body(ga_ref, gb_ref, ya_ref, yb_ref, o_ref):
        o_ref[...] = (jax.nn.sigmoid(ga_ref[...]) * ya_ref[...]
                      + jax.nn.sigmoid(gb_ref[...]) * yb_ref[...]).astype(BF16)

    row = pl.BlockSpec((ts, d), lambda i: (i, 0))
    ga = pl.BlockSpec((ts, d), lambda i: (i, ga_blk))
    gb = pl.BlockSpec((ts, d), lambda i: (i, ga_blk + 1))
    return _pcall(body, name=name, out_shape=jax.ShapeDtypeStruct((s, d), BF16), grid=(s // ts,),
                  in_specs=[ga, gb, row, row], out_specs=row)(proj, proj, y_a, y_b)


def _merge_bwd(dm, proj, y_a, y_b, *, name):
    s, d = y_a.shape
    ts = _rows(s, d)
    ga_blk = OFF_GATES // d

    def body(dm_ref, ga_ref, gb_ref, ya_ref, yb_ref, dya_ref, dyb_ref, dga_ref, dgb_ref):
        dmf = dm_ref[...]
        sa, sb = jax.nn.sigmoid(ga_ref[...]), jax.nn.sigmoid(gb_ref[...])
        dya_ref[...] = (dmf * sa).astype(BF16)
        dyb_ref[...] = (dmf * sb).astype(BF16)
        dga_ref[...] = (dmf * ya_ref[...] * (sa * (1.0 - sa))).astype(BF16)
        dgb_ref[...] = (dmf * yb_ref[...] * (sb * (1.0 - sb))).astype(BF16)

    row = pl.BlockSpec((ts, d), lambda i: (i, 0))
    ga = pl.BlockSpec((ts, d), lambda i: (i, ga_blk))
    gb = pl.BlockSpec((ts, d), lambda i: (i, ga_blk + 1))
    shp = jax.ShapeDtypeStruct((s, d), BF16)
    return _pcall(body, name=name, out_shape=(shp, shp, shp, shp), grid=(s // ts,),
                  in_specs=[row, ga, gb, row, row], out_specs=(row, row, row, row))(dm, proj, proj, y_a, y_b)


def _mm_swiglu(u2, w_gate_up, *, name):
    s, d = u2.shape
    f = w_gate_up.shape[1] // 2
    tm, tn = _tile(s, 1024), _tile(f, 512)
    nj = f // tn

    def body(x_ref, wg_ref, wu_ref, a_ref, g_ref, u_ref):
        x = x_ref[...]
        gf, uf = _dot(x, wg_ref[...]), _dot(x, wu_ref[...])
        a_ref[...] = ((gf * jax.nn.sigmoid(gf)) * uf).astype(BF16)
        g_ref[...] = gf.astype(BF16)
        u_ref[...] = uf.astype(BF16)

    out = pl.BlockSpec((tm, tn), lambda i, j: (i, j))
    shp = jax.ShapeDtypeStruct((s, f), BF16)
    return _pcall(body, name=name, out_shape=(shp, shp, shp), grid=(s // tm, nj),
                  in_specs=[pl.BlockSpec((tm, d), lambda i, j: (i, 0)), pl.BlockSpec((d, tn), lambda i, j: (0, j)),
                            pl.BlockSpec((d, tn), lambda i, j: (0, nj + j))],
                  out_specs=(out, out, out))(u2, w_gate_up, w_gate_up)


def _swiglu_bwd(g, u, da, *, name):
    s, f = g.shape
    ts = _rows(s, f)

    def body(g_ref, u_ref, da_ref, o_ref):
        gf, daf = g_ref[...].astype(F32), da_ref[...].astype(F32)
        sg = jax.nn.sigmoid(gf)
        o_ref[:, :f] = (daf * u_ref[...].astype(F32) * (sg * (1.0 + gf * (1.0 - sg)))).astype(BF16)
        o_ref[:, f:] = (daf * (gf * sg)).astype(BF16)

    row = pl.BlockSpec((ts, f), lambda i: (i, 0))
    return _pcall(body, name=name, out_shape=jax.ShapeDtypeStruct((s, 2 * f), BF16), grid=(s // ts,),
                  in_specs=[row, row, row], out_specs=pl.BlockSpec((ts, 2 * f), lambda i: (i, 0)))(g, u, da)


def _loss_fwd(y, tgt, *, name):
    s, d = y.shape
    ts = _rows(s, d)

    def body(y_ref, t_ref, l_ref, dy_ref):
        @pl.when(pl.program_id(0) == 0)
        def _():
            l_ref[...] = jnp.zeros_like(l_ref)

        e = y_ref[...] - t_ref[...]
        dy_ref[...] = e * (1.0 / d)
        per_tok = jnp.sum(e * e, axis=1, keepdims=True) * (1.0 / d)
        l_ref[...] += 0.5 * jnp.sum(per_tok, axis=0, keepdims=True)

    row = pl.BlockSpec((ts, d), lambda i: (i, 0))
    return _pcall(body, name=name,
                  out_shape=(jax.ShapeDtypeStruct((1, 128), F32), jax.ShapeDtypeStruct((s, d), F32)),
                  grid=(s // ts,), in_specs=[row, row],
                  out_specs=(pl.BlockSpec((1, 128), lambda i: (0, 0)), row))(y, tgt)


def _rope_tables(seq):
    inv = jnp.power(ROPE_THETA, -jnp.arange(0, HEAD_DIM, 2, dtype=F32) / HEAD_DIM)
    ang = jnp.arange(seq, dtype=F32)[:, None] * inv[None, :]
    cos, sin = jnp.cos(ang), jnp.sin(ang)
    return jnp.concatenate([cos, cos], axis=1), jnp.concatenate([-sin, sin], axis=1)


def _qkrope_fwd(proj, gains, cos2, sin2, *, name):
    s = proj.shape[0]
    ts = _rows(s, A_W)

    def body(x_ref, g_ref, c_ref, s_ref, o_ref):
        gain, cos, sin = g_ref[...], c_ref[...], s_ref[...]
        for h in range(A_HEADS):
            lanes = slice(h * HEAD_DIM, (h + 1) * HEAD_DIM)
            x = x_ref[:, lanes]
            y = (x * lax.rsqrt(jnp.mean(x * x, axis=-1, keepdims=True) + EPS)) * gain
            o_ref[:, lanes] = (y * cos + pltpu.roll(y, HEAD_DIM // 2, 1) * sin).astype(BF16)

    heads = pl.BlockSpec((ts, A_W), lambda i, j: (i, j))
    tab = pl.BlockSpec((ts, HEAD_DIM), lambda i, j: (i, 0))
    gain = pl.BlockSpec((None, 1, HEAD_DIM), lambda i, j: (j, 0, 0))
    return _pcall(body, name=name, out_shape=jax.ShapeDtypeStruct((s, 2 * A_W), BF16),
                  grid=(s // ts, 2), in_specs=[heads, gain, tab, tab], out_specs=heads)(
                      proj, gains, cos2, sin2)


def _qkrope_bwd(dqk, proj, gains, cos2, sin2, *, name):
    s = proj.shape[0]
    ts = _rows(s, A_W)

    def body(d_ref, x_ref, g_ref, c_ref, s_ref, dx_ref, dg_ref):
        @pl.when(pl.program_id(1) == 0)
        def _():
            dg_ref[...] = jnp.zeros_like(dg_ref)

        gain, cos, sin = g_ref[...], c_ref[...], s_ref[...]
        dg = jnp.zeros((1, HEAD_DIM), F32)
        for h in range(A_HEADS):
            lanes = slice(h * HEAD_DIM, (h + 1) * HEAD_DIM)
            dout = d_ref[:, lanes]
            dy = dout * cos + pltpu.roll(dout * sin, HEAD_DIM // 2, 1)
            x = x_ref[:, lanes]
            r = lax.rsqrt(jnp.mean(x * x, axis=-1, keepdims=True) + EPS)
            xh = x * r
            dg = dg + jnp.sum(dy * xh, axis=0, keepdims=True)
            dxh = dy * gain
            dx_ref[:, lanes] = (r * (dxh - xh * jnp.mean(dxh * xh, axis=-1, keepdims=True))).astype(BF16)
        dg_ref[...] += dg

    heads = pl.BlockSpec((ts, A_W), lambda j, i: (i, j))
    tab = pl.BlockSpec((ts, HEAD_DIM), lambda j, i: (i, 0))
    gain = pl.BlockSpec((None, 1, HEAD_DIM), lambda j, i: (j, 0, 0))
    return _pcall(body, name=name,
                  out_shape=(jax.ShapeDtypeStruct((s, 2 * A_W), BF16), jax.ShapeDtypeStruct((2, 1, HEAD_DIM), F32)),
                  grid=(2, s // ts), in_specs=[heads, heads, gain, tab, tab],
                  out_specs=(heads, gain))(dqk, proj, gains, cos2, sin2)


def _block_rows(blk):
    if isinstance(blk, int):
        return pl.ds(blk * BLOCK, BLOCK)
    return pl.ds(pl.multiple_of(blk * BLOCK, BLOCK), BLOCK)


def _band_masks(n, with_prev):
    row = lax.broadcasted_iota(jnp.int32, (BLOCK, BLOCK), 0)
    col = lax.broadcasted_iota(jnp.int32, (BLOCK, BLOCK), 1)
    cur = col <= row
    if not with_prev:
        return [(n, cur)]
    prev = col >= row + jnp.where(n >= 1, 0, BLOCK)
    return [(n, cur), (jnp.maximum(n - 1, 0), prev)]


def _dil_fwd(q_arr, k_arr, v_arr, offs, length, dil, *, name):
    nj, nb = dil * HEADS_PER_GROUP, length // BLOCK
    ju = HEADS_PER_GROUP
    qo, ko, vo = (off // ju for off in offs)
    assert all(off % ju == 0 for off in offs)

    def body(q_ref, k_ref, v_ref, o_ref, l_ref):
        n = pl.program_id(1)
        masks = _band_masks(n, nb > 1)
        for cb in range(ju):
            lanes = slice(cb * HEAD_DIM, (cb + 1) * HEAD_DIM)
            q = q_ref[:, lanes].astype(BF16)
            parts = []
            for blk, mask in masks:
                rows = _block_rows(blk)
                sc = _dot(q, k_ref[rows, lanes].astype(BF16), NT) * ATT_SCALE
                parts.append((jnp.where(mask, sc, MASKED), rows))
            m = parts[0][0].max(axis=-1, keepdims=True)
            for sc, _ in parts[1:]:
                m = jnp.maximum(m, sc.max(axis=-1, keepdims=True))
            den = jnp.zeros((BLOCK, 1), F32)
            acc = jnp.zeros((BLOCK, HEAD_DIM), F32)
            for sc, rows in parts:
                p = jnp.exp(sc - m)
                den = den + jnp.sum(p, axis=-1, keepdims=True)
                acc = acc + _dot(p.astype(BF16), v_ref[rows, lanes].astype(BF16))
            o_ref[:, lanes] = acc / den
            l_ref[:, lanes] = jnp.broadcast_to(m + jnp.log(den), (BLOCK, HEAD_DIM))

    qspec = pl.BlockSpec((BLOCK, ju * HEAD_DIM), lambda j, n: (n, qo + j))
    kspec = pl.BlockSpec((length, ju * HEAD_DIM), lambda j, n: (0, ko + j))
    vspec = pl.BlockSpec((length, ju * HEAD_DIM), lambda j, n: (0, vo + j))
    ospec = pl.BlockSpec((BLOCK, ju * HEAD_DIM), lambda j, n: (n, j))
    shp = jax.ShapeDtypeStruct((length, nj * HEAD_DIM), F32)
    return _pcall(body, name=name, out_shape=(shp, shp), grid=(nj // ju, nb), in_specs=[qspec, kspec, vspec],
                  out_specs=(ospec, ospec))(q_arr, k_arr, v_arr)


def _dil_bwd(q_arr, k_arr, v_arr, offs, o, lse, do, dlse, length, dil, *, name):
    nj, nb = dil * HEADS_PER_GROUP, length // BLOCK
    ju = HEADS_PER_GROUP if length <= 4 * BLOCK else 2
    qo, ko, vo = (off // ju for off in offs)
    assert all(off % ju == 0 for off in offs)

    def body(q_ref, k_ref, v_ref, o_ref, l_ref, do_ref, dl_ref, dq_ref, dk_ref, dv_ref):
        dk_ref[...] = jnp.zeros_like(dk_ref)
        dv_ref[...] = jnp.zeros_like(dv_ref)

        def step(n, carry):
            qrows = _block_rows(n)
            masks = _band_masks(n, nb > 1)
            for cb in range(ju):
                lanes = slice(cb * HEAD_DIM, (cb + 1) * HEAD_DIM)
                q = q_ref[qrows, lanes].astype(BF16)
                dof = do_ref[qrows, lanes]
                dob = dof.astype(BF16)
                lse_b = l_ref[qrows, lanes]
                shift = dl_ref[qrows, lanes] - jnp.sum(dof * o_ref[qrows, lanes], axis=-1, keepdims=True)
                dq = jnp.zeros((BLOCK, HEAD_DIM), F32)
                for blk, mask in masks:
                    rows = _block_rows(blk)
                    kk, vv = k_ref[rows, lanes].astype(BF16), v_ref[rows, lanes].astype(BF16)
                    sc = _dot(q, kk, NT) * ATT_SCALE
                    p = jnp.where(mask, jnp.exp(sc - lse_b), 0.0)
                    ds = (p * (_dot(dob, vv, NT) + shift)).astype(BF16)
                    dq = dq + _dot(ds, kk)
                    dk_ref[rows, lanes] += _dot(ds, q, TN) * ATT_SCALE
                    dv_ref[rows, lanes] += _dot(p.astype(BF16), dob, TN)
                dq_ref[qrows, lanes] = dq * ATT_SCALE
            return carry

        if nb == 1:
            step(0, 0)
        else:
            lax.fori_loop(0, nb, step, 0)

    def col(off):
        return pl.BlockSpec((length, ju * HEAD_DIM), lambda j: (0, off + j))

    shp = jax.ShapeDtypeStruct((length, nj * HEAD_DIM), F32)
    return _pcall(body, name=name, out_shape=(shp, shp, shp), grid=(nj // ju,),
                  in_specs=[col(qo), col(ko), col(vo), col(0), col(0), col(0), col(0)],
                  out_specs=(col(0), col(0), col(0)))(q_arr, k_arr, v_arr, o, lse, do, dlse)


def _combine_weights(l_refs):
    ls = [r[...] for r in l_refs]
    m = jnp.maximum(jnp.maximum(ls[0], ls[1]), ls[2])
    es = [jnp.exp(l - m) for l in ls]
    den = es[0] + es[1] + es[2]
    return [e / den for e in es]


def _combine_fwd(os_, lses, *, name):
    s = os_[0].shape[0]
    ts = _rows(s, GROUP_W)

    def body(o0, o1, o2, l0, l1, l2, out_ref):
        w = _combine_weights((l0, l1, l2))
        out_ref[...] = (w[0] * o0[...] + w[1] * o1[...] + w[2] * o2[...]).astype(BF16)

    row = pl.BlockSpec((ts, GROUP_W), lambda i: (i, 0))
    return _pcall(body, name=name, out_shape=jax.ShapeDtypeStruct((s, GROUP_W), BF16), grid=(s // ts,),
                  in_specs=[row] * 6, out_specs=row)(*os_, *lses)


def _combine_bwd(do_a, os_, lses, *, name, deps=()):
    s = do_a.shape[0]
    ts = _rows(s, GROUP_W)

    def body(d_ref, o0, o1, o2, l0, l1, l2, do0, do1, do2, dl0, dl1, dl2):
        w = _combine_weights((l0, l1, l2))
        d = d_ref[...]
        og = [o0[...], o1[...], o2[...]]
        oa = w[0] * og[0] + w[1] * og[1] + w[2] * og[2]
        ta = jnp.sum(d * oa, axis=-1, keepdims=True)
        for g, (do_ref, dl_ref) in enumerate(((do0, dl0), (do1, dl1), (do2, dl2))):
            do_ref[...] = w[g] * d
            dl_ref[...] = w[g] * (jnp.sum(d * og[g], axis=-1, keepdims=True) - ta)

    head = pl.BlockSpec((ts, HEAD_DIM), lambda i, h: (i, h))
    shp = jax.ShapeDtypeStruct((s, GROUP_W), F32)
    return _pcall(body, name=name, out_shape=(shp,) * 6, grid=(s // ts, HEADS_PER_GROUP),
                  in_specs=[head] * 7, out_specs=(head,) * 6, deps=deps)(do_a, *os_, *lses)


def _dot_exact(x, ones_mask):
    hi = x.astype(BF16)
    r1 = x - hi.astype(F32)
    mid = r1.astype(BF16)
    lo = (r1 - mid.astype(F32)).astype(BF16)
    return _dot(hi, ones_mask) + _dot(mid, ones_mask) + _dot(lo, ones_mask)


SB_QROWS = 2 * BLOCK
SB_UNROLL = 4


def _sb_mask(j, i):
    row = lax.broadcasted_iota(jnp.int32, (SB_QROWS, BLOCK), 0)
    col = lax.broadcasted_iota(jnp.int32, (SB_QROWS, BLOCK), 1)
    return col + (j * BLOCK - i * SB_QROWS) < row


def _sb_steps(i):
    return ((i + 1) * (SB_QROWS // BLOCK) + SB_UNROLL - 1) // SB_UNROLL


def _sb_scores(q, kk, j, i):
    mask = _sb_mask(j, i)
    z = _dot(q, kk, NT) * ATT_SCALE
    sp = jnp.log1p(jnp.exp(-jnp.abs(z)))
    log_beta = jnp.minimum(z, 0.0) - sp
    log_1mb = jnp.where(mask, jnp.minimum(-z, 0.0) - sp, 0.0)
    return z, log_beta, log_1mb, mask


def _sb_weights(log_beta, log_1mb, mask, run, upper):
    after = run + _dot_exact(log_1mb, upper)
    return jnp.where(mask, jnp.exp(log_beta + after), 0.0)


def _tri(strict_lower):
    row = lax.broadcasted_iota(jnp.int32, (BLOCK, BLOCK), 0)
    col = lax.broadcasted_iota(jnp.int32, (BLOCK, BLOCK), 1)
    return ((row > col) if strict_lower else (row < col)).astype(BF16)


def _sb_fwd(proj, *, name):
    s = proj.shape[0]
    assert s % (BLOCK * SB_UNROLL) == 0 and s % SB_QROWS == 0
    qb, kb, vb = OFF_QB // HEAD_DIM, OFF_KB // HEAD_DIM, OFF_VB // HEAD_DIM

    def body(q_ref, k_ref, v_ref, o_ref):
        i = pl.program_id(1)
        q = q_ref[...].astype(BF16)
        upper = _tri(True)
        nsteps = _sb_steps(i)

        def step(t, carry):
            acc, run = carry
            for b in reversed(range(SB_UNROLL)):
                j = (nsteps - 1 - t) * SB_UNROLL + b
                rows = _block_rows(j)
                _, log_beta, log_1mb, mask = _sb_scores(q, k_ref[rows, :].astype(BF16), j, i)
                a = _sb_weights(log_beta, log_1mb, mask, run, upper)
                acc = acc + _dot(a.astype(BF16), v_ref[rows, :].astype(BF16))
                run = run + jnp.sum(log_1mb, axis=-1, keepdims=True)
            return acc, run

        acc, _ = lax.fori_loop(0, nsteps, step,
                               (jnp.zeros((SB_QROWS, HEAD_DIM), F32), jnp.zeros((SB_QROWS, 1), F32)))
        o_ref[...] = acc.astype(BF16)

    return _pcall(body, name=name, out_shape=jax.ShapeDtypeStruct((s, B_W), BF16), grid=(SB_HEADS, s // SB_QROWS),
                  in_specs=[pl.BlockSpec((SB_QROWS, HEAD_DIM), lambda h, i: (i, qb + h)),
                            pl.BlockSpec((s, HEAD_DIM), lambda h, i: (0, kb + h)),
                            pl.BlockSpec((s, HEAD_DIM), lambda h, i: (0, vb + h))],
                  out_specs=pl.BlockSpec((SB_QROWS, HEAD_DIM), lambda h, i: (i, h)))(proj, proj, proj)


def _sb_bwd(proj, do_b, *, name):
    s = proj.shape[0]
    assert s % (BLOCK * SB_UNROLL) == 0 and s % SB_QROWS == 0
    nkb = s // BLOCK
    qb, kb, vb = OFF_QB // HEAD_DIM, OFF_KB // HEAD_DIM, OFF_VB // HEAD_DIM

    def body(q_ref, k_ref, v_ref, do_ref, dq_ref, dk_ref, dv_ref, z_s, a_s):
        i = pl.program_id(1)

        @pl.when(i == 0)
        def _():
            dk_ref[...] = jnp.zeros_like(dk_ref)
            dv_ref[...] = jnp.zeros_like(dv_ref)

        q = q_ref[...].astype(BF16)
        dob = do_ref[...].astype(BF16)
        upper, lower = _tri(True), _tri(False)
        nsteps = _sb_steps(i)

        def recompute(t, run):
            for b in reversed(range(SB_UNROLL)):
                j = (nsteps - 1 - t) * SB_UNROLL + b
                z, log_beta, log_1mb, mask = _sb_scores(q, k_ref[_block_rows(j), :].astype(BF16), j, i)
                z_s[j] = z
                a_s[j] = _sb_weights(log_beta, log_1mb, mask, run, upper)
                run = run + jnp.sum(log_1mb, axis=-1, keepdims=True)
            return run

        lax.fori_loop(0, nsteps, recompute, jnp.zeros((SB_QROWS, 1), F32))

        def grads(t, carry):
            dq, run = carry
            for b in range(SB_UNROLL):
                j = t * SB_UNROLL + b
                rows = _block_rows(j)
                kk, vv = k_ref[rows, :].astype(BF16), v_ref[rows, :].astype(BF16)
                z, a = z_s[j], a_s[j]
                de = _dot(dob, vv, NT) * a
                before = run + _dot_exact(de, lower)
                dz = (de * jax.nn.sigmoid(-z)
                      - jnp.where(_sb_mask(j, i), jax.nn.sigmoid(z), 0.0) * before).astype(BF16)
                dk_ref[rows, :] += _dot(dz, q, TN) * ATT_SCALE
                dv_ref[rows, :] += _dot(a.astype(BF16), dob, TN)
                dq = dq + _dot(dz, kk)
                run = run + jnp.sum(de, axis=-1, keepdims=True)
            return dq, run

        dq, _ = lax.fori_loop(0, nsteps, grads,
                              (jnp.zeros((SB_QROWS, HEAD_DIM), F32), jnp.zeros((SB_QROWS, 1), F32)))
        dq_ref[...] = dq * ATT_SCALE

    blk = pl.BlockSpec((SB_QROWS, HEAD_DIM), lambda h, i: (i, h))
    full = pl.BlockSpec((s, HEAD_DIM), lambda h, i: (0, h))
    shp = jax.ShapeDtypeStruct((s, B_W), F32)
    return _pcall(body, name=name, out_shape=(shp, shp, shp), grid=(SB_HEADS, s // SB_QROWS),
                  in_specs=[pl.BlockSpec((SB_QROWS, HEAD_DIM), lambda h, i: (i, qb + h)),
                            pl.BlockSpec((s, HEAD_DIM), lambda h, i: (0, kb + h)),
                            pl.BlockSpec((s, HEAD_DIM), lambda h, i: (0, vb + h)), blk],
                  out_specs=(blk, full, full),
                  scratch=[pltpu.VMEM((nkb, SB_QROWS, BLOCK), F32), pltpu.VMEM((nkb, SB_QROWS, BLOCK), F32)])(
                      proj, proj, proj, do_b)


def _coords():
    return lax.axis_index("x"), lax.axis_index("y"), lax.axis_index("c")


def _flip(v, bit):
    return 1 - v if bit else v


def _shard_of(ref, axis, idx, size):
    if axis == 0:
        sl = pl.ds(pl.multiple_of(idx * size, 16), size)
        return ref.at[sl, :] if len(ref.shape) == 2 else ref.at[:, sl, :]
    sl = pl.ds(pl.multiple_of(idx * size, 128), size)
    return ref.at[:, sl] if len(ref.shape) == 2 else ref.at[:, :, sl]


def _small_allgather(v, *, name, silu=False, deps=()):
    n = v.shape[1]

    def body(v_ref, out_ref, send_sems, recv_sems):
        x, y, c = _coords()
        me = 4 * x + 2 * y + c
        val = v_ref[...]
        out_ref[me] = val * jax.nn.sigmoid(val) if silu else val
        copies = []
        for k in range(1, N_DEV):
            peer = (_flip(x, k & 4), _flip(y, k & 2), _flip(c, k & 1))
            copies.append(pltpu.make_async_remote_copy(
                src_ref=out_ref.at[me], dst_ref=out_ref.at[me], send_sem=send_sems.at[k - 1],
                recv_sem=recv_sems.at[k - 1], device_id=peer, device_id_type=MESH))
        for cp in copies:
            cp.start()
        for cp in copies:
            cp.wait_recv()
        for cp in copies:
            cp.wait_send()

    return _pcall(body, name=name, out_shape=jax.ShapeDtypeStruct((N_DEV, 1, n), F32),
                  in_specs=[pl.BlockSpec(memory_space=pltpu.VMEM)], out_specs=pl.BlockSpec(memory_space=pltpu.VMEM),
                  scratch=[pltpu.SemaphoreType.DMA((N_DEV - 1,)), pltpu.SemaphoreType.DMA((N_DEV - 1,))],
                  deps=deps)(v)


def _cast_place(w, layer, axis, me, *, name):
    _, r, c = w.shape
    tr = _rows(r, c)
    nrt = r // tr

    def body(me_ref, w_ref, o_ref):
        o_ref[...] = w_ref[...].astype(BF16)

    wspec = pl.BlockSpec((None, tr, c), lambda i, me_ref: (layer, i, 0))
    if axis == 0:
        ospec = pl.BlockSpec((tr, c), lambda i, me_ref: (me_ref[0] * nrt + i, 0))
        shape = (r * N_DEV, c)
    else:
        ospec = pl.BlockSpec((tr, c), lambda i, me_ref: (i, me_ref[0]))
        shape = (r, c * N_DEV)
    return _pcall(body, name=name, out_shape=jax.ShapeDtypeStruct(shape, BF16), grid=(nrt,), in_specs=[wspec],
                  out_specs=ospec, prefetch=1)(me, w)


def _pair_sum(grad, sib, core, axis, *, name):
    _, r, c = sib.shape
    tr = _rows(r, c)
    nrt = r // tr

    def body(core_ref, g_ref, s_ref, o_ref):
        o_ref[...] = (g_ref[...].astype(F32) + s_ref[...].astype(F32)).astype(BF16)

    if axis == 0:
        gspec = pl.BlockSpec((tr, c), lambda q, i, core_ref: ((2 * q + core_ref[0]) * nrt + i, 0))
    else:
        gspec = pl.BlockSpec((tr, c), lambda q, i, core_ref: (i, 2 * q + core_ref[0]))
    sspec = pl.BlockSpec((None, tr, c), lambda q, i, core_ref: (q, i, 0))
    return _pcall(body, name=name, out_shape=jax.ShapeDtypeStruct(sib.shape, BF16), grid=(N_CHIPS, nrt),
                  in_specs=[gspec, sspec], out_specs=sspec, prefetch=1)(core, grad, sib)


ANY_SPEC = pl.BlockSpec(memory_space=pl.ANY)
SEM_SPEC = pl.BlockSpec(memory_space=pltpu.SEMAPHORE)
SPLIT_PARAMS = dict(has_side_effects=pltpu.SideEffectType.DATAFLOW_SIDE_EFFECTING)


def _split_start(copies_fn, buffers, sem_shape, after, *, name):
    n = len(buffers)
    rows, cols = sem_shape
    ns = rows * cols
    extra = [after] + _take_token()

    def body(*refs):
        sems = refs[n + len(extra):n + len(extra) + 2 * ns]
        for cp in copies_fn(refs[:n], _sem_rows(sems[:ns], cols), _sem_rows(sems[ns:], cols)):
            cp.start()
        refs[-1][...] = jnp.zeros_like(refs[-1])

    sem = pltpu.SemaphoreType.DMA(())
    outs = pl.pallas_call(
        body, name=name,
        out_shape=((sem,) * (2 * ns) + tuple(jax.ShapeDtypeStruct(b.shape, b.dtype) for b in buffers) + (TOKEN,)),
        in_specs=(ANY_SPEC,) * (n + len(extra)),
        out_specs=(SEM_SPEC,) * (2 * ns) + (ANY_SPEC,) * n + (pl.BlockSpec(memory_space=pltpu.VMEM),),
        input_output_aliases={i: 2 * ns + i for i in range(n)},
        compiler_params=pltpu.CompilerParams(**SPLIT_PARAMS))(*buffers, *extra)
    _ORDER["token"] = outs[-1]
    return list(outs[:ns]), list(outs[ns:2 * ns]), list(outs[2 * ns:2 * ns + n]), outs[-1]


def _split_wait(copies_fn, send_sems, recv_sems, buffers, after, sem_rows, *, name):
    n, ns = len(buffers), len(send_sems)
    cols = ns // sem_rows
    extra = [after] + _take_token()

    def body(*refs):
        sems = refs[n:n + 2 * ns]
        copies = copies_fn(refs[:n], _sem_rows(sems[:ns], cols), _sem_rows(sems[ns:], cols))
        for cp in copies:
            cp.wait_send()
        for cp in copies:
            cp.wait_recv()
        refs[-1][...] = jnp.zeros_like(refs[-1])

    outs = pl.pallas_call(
        body, name=name, out_shape=tuple(jax.ShapeDtypeStruct(b.shape, b.dtype) for b in buffers) + (TOKEN,),
        in_specs=(ANY_SPEC,) * n + (SEM_SPEC,) * (2 * ns) + (ANY_SPEC,) * len(extra),
        out_specs=(ANY_SPEC,) * n + (pl.BlockSpec(memory_space=pltpu.VMEM),),
        input_output_aliases={i: i for i in range(n)},
        compiler_params=pltpu.CompilerParams(**SPLIT_PARAMS))(*buffers, *send_sems, *recv_sems, *extra)
    _ORDER["token"] = outs[-1]
    return list(outs[:n])


def _sem_rows(sems, cols):
    return [sems[i:i + cols] for i in range(0, len(sems), cols)]


def _empty_hbm(shape, dtype):
    return pltpu.with_memory_space_constraint(lax.empty(shape, dtype), pltpu.HBM)


class _SplitGather:
    def __init__(self, fulls, axes, tag):
        self.axes, self.tag, self.nt = list(axes), tag, len(fulls)
        self.sizes = [f.shape[ax] // N_DEV for f, ax in zip(fulls, axes)]
        self.fulls = list(fulls)

    def _slot(self, ref, t, dev):
        return _shard_of(ref, self.axes[t], 4 * dev[0] + 2 * dev[1] + dev[2], self.sizes[t])

    def _first_copies(self, refs, send_sems, recv_sems):
        x, y, c = _coords()
        peers = [(x, y, 1 - c), (1 - x, y, c), (x, 1 - y, c), (1 - x, 1 - y, c)]
        return [pltpu.make_async_remote_copy(
            src_ref=self._slot(refs[t], t, (x, y, c)), dst_ref=self._slot(refs[t], t, (x, y, c)),
            send_sem=send_sems[t][k], recv_sem=recv_sems[t][k], device_id=peer, device_id_type=MESH)
            for t in range(self.nt) for k, peer in enumerate(peers)]

    def _forward_copies(self, refs, send_sems, recv_sems):
        x, y, c = _coords()
        chips = [(1 - x, y), (x, 1 - y), (1 - x, 1 - y)]
        return [pltpu.make_async_remote_copy(
            src_ref=self._slot(refs[t], t, (*chip, c)), dst_ref=self._slot(refs[t], t, (*chip, c)),
            send_sem=send_sems[t][j], recv_sem=recv_sems[t][j], device_id=(x, y, 1 - c), device_id_type=MESH)
            for t in range(self.nt) for j, chip in enumerate(chips)]

    def first(self, after):
        self.s1, self.r1, self.fulls, token = _split_start(
            self._first_copies, self.fulls, (self.nt, 4), after, name=f"comm_gather1_start_{self.tag}")
        return token

    def forward(self, after):
        bufs = _split_wait(self._first_copies, self.s1, self.r1, self.fulls, after, self.nt,
                           name=f"comm_gather1_wait_{self.tag}")
        self.s2, self.r2, self.fulls, token = _split_start(
            self._forward_copies, bufs, (self.nt, 3), after, name=f"comm_gather2_start_{self.tag}")
        return token

    def finish(self, after):
        return _split_wait(self._forward_copies, self.s2, self.r2, self.fulls, after, self.nt,
                           name=f"comm_gather2_wait_{self.tag}")


class _SplitPairExchange:
    def __init__(self, grads, axes, tag):
        self.nt, self.tag, self.axes = len(grads), tag, list(axes)
        self.grads = list(grads)
        self.sizes = [g.shape[ax] // N_DEV for g, ax in zip(grads, axes)]

    def _copies(self, refs, send_sems, recv_sems):
        nt = self.nt
        x, y, c = _coords()
        return [pltpu.make_async_remote_copy(
            src_ref=_shard_of(refs[t], self.axes[t], 2 * q + 1 - c, self.sizes[t]), dst_ref=refs[nt + t].at[q],
            send_sem=send_sems[t][q], recv_sem=recv_sems[t][q], device_id=(x, y, 1 - c), device_id_type=MESH)
            for t in range(nt) for q in range(N_CHIPS)]

    def start(self):
        landing = []
        for g, ax in zip(self.grads, self.axes):
            dims = list(g.shape)
            dims[ax] //= N_DEV
            landing.append(_empty_hbm((N_CHIPS, *dims), g.dtype))
        self.s, self.r, self.bufs, token = _split_start(
            self._copies, self.grads + landing, (self.nt, N_CHIPS), self.grads[-1],
            name=f"comm_rs_pair_start_{self.tag}")
        return token

    def finish(self, after):
        bufs = _split_wait(self._copies, self.s, self.r, self.bufs, after, self.nt,
                           name=f"comm_rs_pair_wait_{self.tag}")
        return bufs[:self.nt], bufs[self.nt:]


class _ReducePipeline:
    def __init__(self, core):
        self.core, self.items, self.done, self.now = core, [], [], 0

    def add(self, keys, grads, layer):
        axes = [SHARD_AXIS[k] for k in keys]
        pair = _SplitPairExchange([grads[k] for k in keys], axes, f"{keys[0]}{layer}")
        token = pair.start()
        self.items.append(dict(keys=keys, layer=layer, axes=axes, pair=pair, state="pair", since=self.now))
        return [token]

    def tick(self, after, flush=False):
        self.now += 1
        deps = []
        for it in self.items:
            if it["state"] == "pair" and it["since"] < self.now:
                grads, sib = it["pair"].finish(after)
                sums = [_pair_sum(g, s_, self.core, ax, name="pair_sum_" + k)
                        for k, g, s_, ax in zip(it["keys"], grads, sib, it["axes"])]
                it["chip"] = _SplitChipExchange(sums, f"{it['keys'][0]}{it['layer']}")
                deps.append(it["chip"].start())
                it.update(state="chip", since=self.now)
            elif it["state"] == "chip" and (flush or self.now - it["since"] >= 2):
                sums, remote = it["chip"].finish(after)
                self.done.append((it["keys"], it["layer"], sums, remote))
                it["state"] = "done"
        return deps

    def take_done(self):
        out, self.done = self.done, []
        return out


class _SplitChipExchange:
    def __init__(self, sums, tag):
        self.nt, self.tag = len(sums), tag
        self.sums = list(sums)

    def _copies(self, refs, send_sems, recv_sems):
        nt = self.nt
        x, y, c = _coords()
        copies = []
        for t in range(nt):
            for k in range(1, N_CHIPS):
                px, py = _flip(x, k & 2), _flip(y, k & 1)
                copies.append(pltpu.make_async_remote_copy(
                    src_ref=refs[t].at[2 * px + py], dst_ref=refs[nt + t].at[k - 1], send_sem=send_sems[t][k - 1],
                    recv_sem=recv_sems[t][k - 1], device_id=(px, py, c), device_id_type=MESH))
        return copies

    def start(self):
        landing = [_empty_hbm((N_CHIPS - 1,) + s.shape[1:], s.dtype) for s in self.sums]
        self.s, self.r, self.bufs, token = _split_start(
            self._copies, self.sums + landing, (self.nt, N_CHIPS - 1), self.sums[-1],
            name=f"comm_rs_chip_start_{self.tag}")
        return token

    def finish(self, after):
        bufs = _split_wait(self._copies, self.s, self.r, self.bufs, after, self.nt,
                           name=f"comm_rs_chip_wait_{self.tag}")
        return bufs[:self.nt], bufs[self.nt:]


def _adam_math(g, w, m, v):
    m2 = ADAM_B1 * m + (1.0 - ADAM_B1) * g
    v2 = ADAM_B2 * v + (1.0 - ADAM_B2) * (g * g)
    m_hat = m2 / (1.0 - ADAM_B1 ** ADAM_STEP)
    v_hat = v2 / (1.0 - ADAM_B2 ** ADAM_STEP)
    delta = -ADAM_LR * (m_hat / (jnp.sqrt(v_hat) + ADAM_EPS) + ADAM_WD * w)
    return delta, m2, v2


def _adamw_sharded(chip_sums, remote, chip, w, m, v, layer, prev, deps, *, name):
    nl, r, c = w.shape
    tr = _rows(r, c)

    def body(*refs):
        p_ref, r0_ref, r1_ref, r2_ref, w_ref, m_ref, v_ref = refs[1:8]
        g_out, d_out, m_out, v_out = refs[-4:]
        g = ((p_ref[...].astype(F32) + r0_ref[...].astype(F32)) + r1_ref[...].astype(F32)) + r2_ref[...].astype(F32)
        g_out[...] = g
        d_out[...], m_out[...], v_out[...] = _adam_math(g, w_ref[...], m_ref[...], v_ref[...])

    pspec = pl.BlockSpec((None, tr, c), lambda i, chip_ref: (chip_ref[0], i, 0))

    def rspec(k):
        return pl.BlockSpec((None, tr, c), lambda i, chip_ref: (k, i, 0))

    wspec = pl.BlockSpec((None, tr, c), lambda i, chip_ref: (layer, i, 0))
    in_specs = [pspec, rspec(0), rspec(1), rspec(2), wspec, wspec, wspec]
    args = [chip, chip_sums, remote, remote, remote, w, m, v]
    aliases = {}
    if prev is not None:
        in_specs += [pl.BlockSpec(memory_space=pl.ANY)] * 4
        aliases = {len(args) + i: i for i in range(4)}
        args += list(prev)
    shp = jax.ShapeDtypeStruct(w.shape, F32)
    return _pcall(body, name=name, out_shape=(shp,) * 4, grid=(r // tr,), in_specs=in_specs, out_specs=(wspec,) * 4,
                  aliases=aliases, prefetch=1, deps=deps)(*args)


def _adamw_local(g, w, m, v, *, name):
    nl, r, c = w.shape
    tr = _rows(r, c)

    def body(g_ref, w_ref, m_ref, v_ref, d_out, m_out, v_out):
        d_out[...], m_out[...], v_out[...] = _adam_math(g_ref[...], w_ref[...], m_ref[...], v_ref[...])

    spec = pl.BlockSpec((None, tr, c), lambda l, i: (l, i, 0))
    shp = jax.ShapeDtypeStruct(w.shape, F32)
    return _pcall(body, name=name, out_shape=(shp,) * 3, grid=(nl, r // tr), in_specs=[spec] * 4,
                  out_specs=(spec,) * 3)(g, w, m, v)


def _adamw_replicated(parts, w, m, v, *, name):
    n = w.shape[1]

    def body(p_ref, w_ref, m_ref, v_ref, g_out, d_out, m_out, v_out):
        g = p_ref[0]
        for k in range(1, N_DEV):
            g = g + p_ref[k]
        g_out[...] = g
        d_out[...], m_out[...], v_out[...] = _adam_math(g, w_ref[...], m_ref[...], v_ref[...])

    vm = pl.BlockSpec(memory_space=pltpu.VMEM)
    shp = jax.ShapeDtypeStruct((1, n), F32)
    return _pcall(body, name=name, out_shape=(shp,) * 4, in_specs=[vm] * 4, out_specs=(vm,) * 4)(parts, w, m, v)


def _group_views(qk, proj, g, dil, seq):
    if dil == 1:
        return (qk, qk, proj), (0, A_HEADS, 2 * A_HEADS)
    length = seq // dil
    lo = g * GROUP_W
    q = qk[:, lo:lo + GROUP_W].reshape(length, dil * GROUP_W)
    k = qk[:, A_W + lo:A_W + lo + GROUP_W].reshape(length, dil * GROUP_W)
    v = proj[:, OFF_VA + lo:OFF_VA + lo + GROUP_W].astype(BF16).reshape(length, dil * GROUP_W)
    return (q, k, v), (0, 0, 0)


def _mod_rows(mod, d):
    return [mod[:, i * d:(i + 1) * d] for i in range(6)]


MIXER_W = ("w_in", "w_branch_a", "w_branch_b", "w_out")
FFN_W = ("w_gate_up", "w_down")
SHARD_AXIS = {"w_in": 1, "w_branch_a": 1, "w_branch_b": 1, "w_out": 0, "w_gate_up": 1, "w_down": 0}


def _mixer_fwd_a(h, mod, g1, gains, w_in, cos2, sin2, deps):
    seq, d = h.shape
    sh1, sc1 = _mod_rows(mod, d)[:2]
    u = _rmsmod_fwd(h, g1, sc1, sh1, name="rmsmod_fwd", deps=deps)
    proj = _mm(u, w_in, name="mm_in")
    qk = _qkrope_fwd(proj, gains, cos2, sin2, name="qkrope_fwd")
    os_, lses = [], []
    for g, dil in enumerate(DILATIONS):
        arrs, offs = _group_views(qk, proj, g, dil, seq)
        o, lse = _dil_fwd(*arrs, offs, seq // dil, dil, name=f"dil_fwd_{dil}")
        os_.append(o.reshape(seq, GROUP_W))
        lses.append(lse.reshape(seq, GROUP_W))
    o_a = _combine_fwd(os_, lses, name="combine_fwd")
    o_b = _sb_fwd(proj, name="sb_fwd")
    return dict(h_in=h, u=u, proj=proj, qk=qk, os=os_, lses=lses, o_a=o_a, o_b=o_b)


def _mixer_fwd_b(sv, mod, wts, deps):
    d = sv["h_in"].shape[1]
    ga1 = _mod_rows(mod, d)[2]
    y_a = _mm(sv["o_a"], wts["w_branch_a"], name="mm_branch", deps=deps)
    y_b = _mm(sv["o_b"], wts["w_branch_b"], name="mm_branch")
    merged = _merge_fwd(sv["proj"], y_a, y_b, name="merge_fwd")
    t = _mm(merged, wts["w_out"], name="mm_out")
    h_mid = _resid_gate(sv["h_in"], ga1, t, name="resid_gate")
    sv.update(y_a=y_a, y_b=y_b, merged=merged, t=t, h_mid=h_mid)
    return h_mid


def _ffn_fwd_a(sv, mod, g2, w_gate_up, deps):
    d = sv["h_mid"].shape[1]
    sh2, sc2 = _mod_rows(mod, d)[3:5]
    u2 = _rmsmod_fwd(sv["h_mid"], g2, sc2, sh2, name="rmsmod_fwd", deps=deps)
    a, g, u = _mm_swiglu(u2, w_gate_up, name="mm_gate_up")
    sv.update(u2=u2, g=g, up=u, a=a)
    return a


def _ffn_fwd_b(sv, mod, w_down, deps):
    d = sv["h_mid"].shape[1]
    ga2 = _mod_rows(mod, d)[5]
    f = _mm(sv["a"], w_down, name="mm_down", deps=deps)
    sv["f"] = f
    return _resid_gate(sv["h_mid"], ga2, f, name="resid_gate")


def _wgrad(act, dout, key):
    return _mm(act, dout, ta=True, out_dtype=BF16, name="mm_wgrad_" + key)


def _ffn_bwd(dh, sv, mod, g2, wts, deps, hook):
    d = dh.shape[1]
    sc2, ga2 = _mod_rows(mod, d)[4:6]
    df, dgate2 = _resid_gate_bwd(dh, sv["f"], ga2, name="resid_gate_bwd", deps=deps)
    da = _mm(df, wts["w_down"], tb=True, out_dtype=BF16, name="mm_down_t")
    grads = {"w_down": _wgrad(sv["a"], df, "w_down")}
    dgu = _swiglu_bwd(sv["g"], sv["up"], da, name="swiglu_bwd")
    du2 = _mm(dgu, wts["w_gate_up"], tb=True, name="mm_gate_up_t", deps=hook(dgu))
    grads["w_gate_up"] = _wgrad(sv["u2"], dgu, "w_gate_up")
    dh_mid, dsh2, dsc2, dg2 = _rmsmod_bwd(du2, sv["h_mid"], g2, sc2, dh, name="rmsmod_bwd")
    return dh_mid, [dsh2, dsc2, dgate2], dg2, grads


def _mixer_bwd(dh_mid, sv, mod, g1, gains, wts, cos2, sin2, deps, hook):
    seq, d = dh_mid.shape
    sc1, ga1 = _mod_rows(mod, d)[1:3]
    dt, dgate1 = _resid_gate_bwd(dh_mid, sv["t"], ga1, name="resid_gate_bwd", deps=deps)
    dmerged = _mm(dt, wts["w_out"], tb=True, name="mm_out_t")
    grads = {"w_out": _wgrad(sv["merged"], dt, "w_out")}
    dy_a, dy_b, dga, dgb = _merge_bwd(dmerged, sv["proj"], sv["y_a"], sv["y_b"], name="merge_bwd")
    do_a = _mm(dy_a, wts["w_branch_a"], tb=True, name="mm_branch_t")
    do_b = _mm(dy_b, wts["w_branch_b"], tb=True, name="mm_branch_t")
    grads["w_branch_a"] = _wgrad(sv["o_a"], dy_a, "w_branch_a")
    grads["w_branch_b"] = _wgrad(sv["o_b"], dy_b, "w_branch_b")
    dqb, dkb, dvb = _sb_bwd(sv["proj"], do_b, name="sb_bwd")
    comb = _combine_bwd(do_a, sv["os"], sv["lses"], name="combine_bwd", deps=hook(dqb, grads))
    grads = {}
    dos, dls = comb[:3], comb[3:]
    dqs, dks, dvs = [], [], []
    for g, dil in enumerate(DILATIONS):
        length = seq // dil
        arrs, offs = _group_views(sv["qk"], sv["proj"], g, dil, seq)
        view = (length, dil * GROUP_W)
        dq, dk, dv = _dil_bwd(*arrs, offs, sv["os"][g].reshape(view), sv["lses"][g].reshape(view),
                              dos[g].reshape(view), dls[g].reshape(view), length, dil, name=f"dil_bwd_{dil}")
        dqs.append(dq.reshape(seq, GROUP_W))
        dks.append(dk.reshape(seq, GROUP_W))
        dvs.append(dv.reshape(seq, GROUP_W))
    dqk, dgains = _qkrope_bwd(jnp.concatenate(dqs + dks, axis=1), sv["proj"], gains, cos2, sin2,
                              name="qkrope_bwd")
    dproj = jnp.concatenate(
        [dqk] + [t_.astype(BF16) for t_ in dvs + [dqb, dkb, dvb]] + [dga, dgb], axis=1)
    du = _mm(dproj, wts["w_in"], tb=True, name="mm_in_t")
    grads["w_in"] = _wgrad(sv["u"], dproj, "w_in")
    dh_in, dsh1, dsc1, dg1 = _rmsmod_bwd(du, sv["h_in"], g1, sc1, dh_mid, name="rmsmod_bwd")
    return dh_in, [dsh1, dsc1, dgate1], dg1, dgains, grads


def kernel(x, c, w_ada, b_ada, norm1_g, norm2_g, w_in, qn_g, kn_g, w_branch_a, w_branch_b, w_out, w_gate_up, w_down, loss_target, m_w_ada, m_b_ada, m_norm1_g, m_norm2_g, m_w_in, m_qn_g, m_kn_g, m_w_branch_a, m_w_branch_b, m_w_out, m_w_gate_up, m_w_down, v_w_ada, v_b_ada, v_norm1_g, v_norm2_g, v_w_in, v_qn_g, v_kn_g, v_w_branch_a, v_w_branch_b, v_w_out, v_w_gate_up, v_w_down):
    _ORDER["token"] = None
    seq, d = x.shape[1], x.shape[2]
    depth = w_in.shape[0]
    weights = dict(w_in=w_in, w_branch_a=w_branch_a, w_branch_b=w_branch_b, w_out=w_out, w_gate_up=w_gate_up,
                   w_down=w_down)
    moments_m = dict(w_in=m_w_in, w_branch_a=m_w_branch_a, w_branch_b=m_w_branch_b, w_out=m_w_out,
                     w_gate_up=m_w_gate_up, w_down=m_w_down)
    moments_v = dict(w_in=v_w_in, w_branch_a=v_w_branch_a, w_branch_b=v_w_branch_b, w_out=v_w_out,
                     w_gate_up=v_w_gate_up, w_down=v_w_down)
    xi, yi, ci = _coords()
    me = 4 * xi + 2 * yi + ci
    core = jnp.reshape(ci, (1,)).astype(jnp.int32)
    chip = jnp.reshape(2 * xi + yi, (1,)).astype(jnp.int32)

    ada_w = w_ada.shape[2]
    c_act = _small_allgather(c, name="comm_gather_c", silu=True).reshape(N_DEV, d)
    c_pad = jnp.concatenate([c_act, jnp.zeros_like(c_act)], axis=0).astype(BF16)
    bias = lax.dynamic_slice(b_ada, (0, me * ada_w), (depth, ada_w))
    mod_part = jnp.stack([_mm(c_pad, w_ada[l], name="mm_ada")[:N_DEV] for l in range(depth)]) + bias[:, None, :]
    mod_all = _small_allgather(mod_part.reshape(1, depth * N_DEV * ada_w), name="comm_gather_mod")
    mod_all = mod_all.reshape(N_DEV, depth, N_DEV, ada_w)
    mod_mine = lax.dynamic_index_in_dim(mod_all, me, axis=2, keepdims=False)
    mods = jnp.transpose(mod_mine, (1, 0, 2)).reshape(depth, 1, 6 * d)

    cos2, sin2 = _rope_tables(seq)
    gains = [jnp.stack([qn_g[l], kn_g[l]])[:, None, :] for l in range(depth)]
    g1s = [norm1_g[l][None] for l in range(depth)]
    g2s = [norm2_g[l][None] for l in range(depth)]

    me_arr = jnp.reshape(me, (1,)).astype(jnp.int32)

    def placed(keys, l):
        return [_cast_place(weights[k], l, SHARD_AXIS[k], me_arr, name="cast_place_" + k) for k in keys]

    def gather_of(keys, l, tag):
        return _SplitGather(placed(keys, l), [SHARD_AXIS[k] for k in keys], f"{tag}{l}")

    groups = [("w_in", 0, MIXER_W[:1]), ("rest", 0, MIXER_W[1:]), ("ffn", 0, FFN_W)]
    for l in range(1, depth):
        groups += [("mixer", l, MIXER_W), ("ffn", l, FFN_W)]
    gathers, token = {}, mods
    for tag, l, keys in groups:
        gathers[tag, l] = gather_of(keys, l, tag)
        token = gathers[tag, l].first(after=token)
    token = gathers["w_in", 0].forward(after=token)
    wm = {"w_in": gathers["w_in", 0].finish(after=token)[0]}
    deps = []
    h = x[0]
    saved, full = [], []
    for l in range(depth):
        last = l + 1 == depth
        sv = _mixer_fwd_a(h, mods[l], g1s[l], gains[l], wm["w_in"], cos2, sin2, deps)
        deps = [gathers["ffn", l].forward(after=sv["o_b"])]
        if l == 0:
            deps.append(gathers["rest", 0].forward(after=sv["o_b"]))
            wm.update(zip(MIXER_W[1:], gathers["rest", 0].finish(after=sv["o_b"])))
        h_mid = _mixer_fwd_b(sv, mods[l], wm, deps)
        wf = dict(zip(FFN_W, gathers["ffn", l].finish(after=h_mid)))
        a = _ffn_fwd_a(sv, mods[l], g2s[l], wf["w_gate_up"], [])
        deps = [] if last else [gathers["mixer", l + 1].forward(after=a)]
        h = _ffn_fwd_b(sv, mods[l], wf["w_down"], deps)
        saved.append(sv)
        full.append({**wm, **wf})
        if not last:
            wm = dict(zip(MIXER_W, gathers["mixer", l + 1].finish(after=h)))
        deps = []
    loss_part, dh = _loss_fwd(h, loss_target[0], name="loss")
    loss = lax.psum(loss_part[0, 0], ("x", "y", "c"))

    pipe = _ReducePipeline(core)
    dmods, dg1s, dg2s, dgains = [None] * depth, [None] * depth, [None] * depth, [None] * depth
    deps = []
    for l in reversed(range(depth)):
        dh_mid, dmod_f, dg2s[l], grads = _ffn_bwd(dh, saved[l], mods[l], g2s[l], full[l], deps, pipe.tick)
        deps = pipe.tick(dh_mid) + pipe.add(FFN_W, grads, l)
        dh, dmod_m, dg1s[l], dgains[l], grads = _mixer_bwd(
            dh_mid, saved[l], mods[l], g1s[l], gains[l], full[l], cos2, sin2, deps,
            lambda after, early, l=l: pipe.tick(after) + pipe.add(MIXER_W[1:], early, l))
        dmods[l] = jnp.concatenate(dmod_m + dmod_f, axis=1)
        deps = pipe.tick(dh) + pipe.add(MIXER_W[:1], grads, l)
    grad_x = dh[None]

    stacked = {}

    def update(deps):
        last = None
        for keys, l, sums, remote in pipe.take_done():
            for k, p_, r_ in zip(keys, sums, remote):
                stacked[k] = _adamw_sharded(p_, r_, chip, weights[k], moments_m[k], moments_v[k], l,
                                            stacked.get(k), deps, name="adamw_" + k)
                deps, last = [], stacked[k][0]
        return last

    after = update(deps)

    small = jnp.concatenate(
        dmods + dg1s + dg2s + [dgains[l][0] for l in range(depth)] + [dgains[l][1] for l in range(depth)], axis=1)
    small_all = _small_allgather(small, name="comm_gather_small", deps=[after])
    update(pipe.tick(small_all))

    def pack(b, n1, n2, qn, kn):
        return jnp.concatenate([t_.reshape(1, -1) for t_ in (b, n1, n2, qn, kn)], axis=1)

    sg, sd, sm, sv_ = _adamw_replicated(small_all, pack(b_ada, norm1_g, norm2_g, qn_g, kn_g),
                                        pack(m_b_ada, m_norm1_g, m_norm2_g, m_qn_g, m_kn_g),
                                        pack(v_b_ada, v_norm1_g, v_norm2_g, v_qn_g, v_kn_g), name="adamw_replicated")

    def unpack(p):
        sizes = [depth * 6 * d, depth * d, depth * d, depth * HEAD_DIM, depth * HEAD_DIM]
        shapes = [b_ada.shape, norm1_g.shape, norm2_g.shape, qn_g.shape, kn_g.shape]
        out, off = [], 0
        for n, shp in zip(sizes, shapes):
            out.append(p[0, off:off + n].reshape(shp))
            off += n
        return dict(zip(("b_ada", "norm1_g", "norm2_g", "qn_g", "kn_g"), out))

    ug, ud, um, uv = unpack(sg), unpack(sd), unpack(sm), unpack(sv_)
    res = {k: dict(g=ug[k], d=ud[k], m=um[k], v=uv[k]) for k in ug}

    dmod_all = small_all[:, 0, :depth * 6 * d].reshape(N_DEV, depth, 6 * d)
    g_ada = None
    for l in range(depth):
        dm = lax.dynamic_slice(dmod_all[:, l, :], (0, me * ada_w), (N_DEV, ada_w))
        dm = jnp.concatenate([dm, jnp.zeros_like(dm)], axis=0).astype(BF16)
        g_ada = _mm(c_pad, dm, ta=True, name="mm_wgrad_ada", stack=(l, depth, g_ada))
    d_ada, m_ada, v_ada = _adamw_local(g_ada, w_ada, m_w_ada, v_w_ada, name="adamw_local")
    res["w_ada"] = dict(g=g_ada, d=d_ada, m=m_ada, v=v_ada)

    after = update(pipe.tick(d_ada))
    update(pipe.tick(d_ada if after is None else after, flush=True))
    for k, (g_, d_, m_, v_) in stacked.items():
        res[k] = dict(g=g_, d=d_, m=m_, v=v_)

    order = ("w_ada", "b_ada", "norm1_g", "norm2_g", "w_in", "qn_g", "kn_g", "w_branch_a", "w_branch_b", "w_out",
             "w_gate_up", "w_down")
    _ORDER["token"] = None
    return (loss, grad_x, *[res[k]["g"] for k in order], *[res[k]["d"] for k in order],
            *[res[k]["m"] for k in order], *[res[k]["v"] for k in order])
```

```python
import functools

import jax
import jax.numpy as jnp
from jax import lax
from jax.experimental import pallas as pl
from jax.experimental.pallas import tpu as pltpu

F32 = jnp.float32
BF16 = jnp.bfloat16

HEAD_DIM = 128
BLOCK = 128
DILATIONS = (1, 4, 16)
HEADS_PER_GROUP = 4
A_HEADS = 12
SB_HEADS = 4
GROUP_W = HEADS_PER_GROUP * HEAD_DIM
A_W = A_HEADS * HEAD_DIM
B_W = SB_HEADS * HEAD_DIM
OFF_QA, OFF_KA, OFF_VA = 0, A_W, 2 * A_W
OFF_QB, OFF_KB, OFF_VB = 3 * A_W, 3 * A_W + B_W, 3 * A_W + 2 * B_W
OFF_GATES = 3 * A_W + 3 * B_W
ROPE_THETA = 10000.0
EPS = 1e-6
ATT_SCALE = HEAD_DIM ** -0.5
MASKED = -1e30

ADAM_LR, ADAM_B1, ADAM_B2, ADAM_EPS, ADAM_WD, ADAM_STEP = 0.001, 0.9, 0.999, 1e-08, 0.01, 10

N_DEV = 8
N_CHIPS = 4
V7X_VMEM_LIMIT_BYTES = 56 * 1024 * 1024
ELEMWISE_BLOCK_BYTES = 2 * 1024 * 1024
MESH = pl.DeviceIdType.MESH

NN = (((1,), (0,)), ((), ()))
NT = (((1,), (1,)), ((), ()))
TN = (((0,), (0,)), ((), ()))


def _dot(a, b, dims=NN):
    return lax.dot_general(a, b, dims, preferred_element_type=F32)


def _tile(n, cap, mult=128):
    best = None
    for t in range(mult, min(n, cap) + 1, mult):
        if n % t == 0:
            best = t
    if best is None:
        assert n <= 2 * cap, (n, cap)
        return n
    return best


def _rows(r, c):
    return _tile(r, max(16, ELEMWISE_BLOCK_BYTES // (4 * c)), 16)


_ORDER = {"token": None}
TOKEN = jax.ShapeDtypeStruct((8, 128), F32)


def _take_token():
    prev = _ORDER["token"]
    return [] if prev is None else [prev]


def _pcall(body, *, name, out_shape, grid=None, in_specs=None, out_specs=None, scratch=(), aliases=None,
           prefetch=0, deps=()):
    single = not isinstance(out_shape, (tuple, list))
    out_shapes = [out_shape] if single else list(out_shape)
    out_specs = [out_specs] if single else list(out_specs)
    extra = list(deps) + _take_token()
    n_in, n_extra, n_out = prefetch + len(in_specs), len(extra), len(out_shapes)

    def wrapped(*refs):
        token = refs[n_in + n_extra + n_out]
        token[...] = jnp.zeros_like(token)
        return body(*refs[:n_in], *refs[n_in + n_extra:n_in + n_extra + n_out], *refs[n_in + n_extra + n_out + 1:])

    in_specs = list(in_specs) + [pl.BlockSpec(memory_space=pl.ANY)] * n_extra
    if grid is None:
        out_specs.append(pl.BlockSpec(memory_space=pltpu.VMEM))
    else:
        out_specs.append(pl.BlockSpec(TOKEN.shape, lambda *_: (0, 0)))
    kwargs = dict(name=name, out_shape=out_shapes + [TOKEN], input_output_aliases=aliases or {},
                  compiler_params=pltpu.CompilerParams(vmem_limit_bytes=V7X_VMEM_LIMIT_BYTES))
    if prefetch:
        call = pl.pallas_call(wrapped, grid_spec=pltpu.PrefetchScalarGridSpec(
            num_scalar_prefetch=prefetch, grid=grid, in_specs=in_specs, out_specs=out_specs,
            scratch_shapes=list(scratch)), **kwargs)
    else:
        if grid is not None:
            kwargs["grid"] = grid
        call = pl.pallas_call(wrapped, in_specs=in_specs, out_specs=out_specs, scratch_shapes=list(scratch), **kwargs)

    def run(*args):
        outs = call(*args, *extra)
        _ORDER["token"] = outs[-1]
        return outs[0] if single else tuple(outs[:-1])

    return run


def _mm(a, b, *, name, ta=False, tb=False, out_dtype=F32, caps=(1024, 1024, 3072), stack=None, deps=()):
    kdim, m = a.shape if ta else a.shape[::-1]
    n, k2 = b.shape if tb else b.shape[::-1]
    assert kdim == k2, (a.shape, b.shape, ta, tb)
    tm, tn, tk = _tile(m, caps[0]), _tile(n, caps[1]), _tile(kdim, caps[2])
    nk = kdim // tk
    dims = (((0 if ta else 1,), (1 if tb else 0,)), ((), ()))

    def body(*refs):
        a_ref, b_ref = refs[0], refs[1]
        part = _dot(a_ref[...].astype(BF16), b_ref[...].astype(BF16), dims)
        if nk == 1:
            o_ref = refs[-1]
            o_ref[...] = part.astype(o_ref.dtype)
            return
        o_ref, acc_ref = refs[-2], refs[-1]
        k = pl.program_id(2)

        @pl.when(k == 0)
        def _():
            acc_ref[...] = part

        @pl.when(k > 0)
        def _():
            acc_ref[...] += part

        @pl.when(k == nk - 1)
        def _():
            o_ref[...] = acc_ref[...].astype(o_ref.dtype)

    a_spec = (pl.BlockSpec((tk, tm), lambda i, j, k: (k, i)) if ta
              else pl.BlockSpec((tm, tk), lambda i, j, k: (i, k)))
    b_spec = (pl.BlockSpec((tn, tk), lambda i, j, k: (j, k)) if tb
              else pl.BlockSpec((tk, tn), lambda i, j, k: (k, j)))
    ins, in_specs, aliases = [a, b], [a_spec, b_spec], {}
    if stack is None:
        out_shape = jax.ShapeDtypeStruct((m, n), out_dtype)
        out_spec = pl.BlockSpec((tm, tn), lambda i, j, k: (i, j))
    else:
        layer, n_layers, buf = stack
        out_shape = jax.ShapeDtypeStruct((n_layers, m, n), out_dtype)
        out_spec = pl.BlockSpec((None, tm, tn), lambda i, j, k: (layer, i, j))
        if buf is not None:
            ins.append(buf)
            in_specs.append(pl.BlockSpec(memory_space=pl.ANY))
            aliases = {2: 0}
    scratch = [] if nk == 1 else [pltpu.VMEM((tm, tn), F32)]
    return _pcall(body, name=name, out_shape=out_shape, grid=(m // tm, n // tn, nk), in_specs=in_specs,
                  out_specs=out_spec, scratch=scratch, aliases=aliases, deps=deps)(*ins)


def _mm_cat_k(a_lo, a_hi, b, *, name):
    m, f = a_lo.shape
    n = b.shape[0]
    tm, tn, tk = _tile(m, 1024), _tile(n, 1024), _tile(f, 3072)
    half = f // tk
    nk = 2 * half

    def body(lo_ref, hi_ref, b_ref, o_ref, acc_ref):
        k = pl.program_id(2)

        def accumulate(a_ref):
            part = _dot(a_ref[...], b_ref[...], NT)

            @pl.when(k == 0)
            def _():
                acc_ref[...] = part

            @pl.when(k > 0)
            def _():
                acc_ref[...] += part

        pl.when(k < half)(lambda: accumulate(lo_ref))
        pl.when(k >= half)(lambda: accumulate(hi_ref))

        @pl.when(k == nk - 1)
        def _():
            o_ref[...] = acc_ref[...]

    return _pcall(body, name=name, out_shape=jax.ShapeDtypeStruct((m, n), F32), grid=(m // tm, n // tn, nk),
                  in_specs=[pl.BlockSpec((tm, tk), lambda i, j, k: (i, jnp.minimum(k, half - 1))),
                            pl.BlockSpec((tm, tk), lambda i, j, k: (i, jnp.maximum(k - half, 0))),
                            pl.BlockSpec((tn, tk), lambda i, j, k: (j, k))],
                  out_specs=pl.BlockSpec((tm, tn), lambda i, j, k: (i, j)),
                  scratch=[pltpu.VMEM((tm, tn), F32)])(a_lo, a_hi, b)


def _mm_cat_n(a, b_lo, b_hi, *, name):
    s, m = a.shape
    f = b_lo.shape[1]
    tm, tn = _tile(m, 1024), _tile(f, 1024)
    half = f // tn

    def body(a_ref, lo_ref, hi_ref, o_ref):
        j = pl.program_id(1)

        @pl.when(j < half)
        def _():
            o_ref[...] = _dot(a_ref[...], lo_ref[...], TN).astype(BF16)

        @pl.when(j >= half)
        def _():
            o_ref[...] = _dot(a_ref[...], hi_ref[...], TN).astype(BF16)

    return _pcall(body, name=name, out_shape=jax.ShapeDtypeStruct((m, 2 * f), BF16), grid=(m // tm, 2 * half),
                  in_specs=[pl.BlockSpec((s, tm), lambda i, j: (0, i)),
                            pl.BlockSpec((s, tn), lambda i, j: (0, jnp.minimum(j, half - 1))),
                            pl.BlockSpec((s, tn), lambda i, j: (0, jnp.maximum(j - half, 0)))],
                  out_specs=pl.BlockSpec((tm, tn), lambda i, j: (i, j)))(a, b_lo, b_hi)


def _mm_resid_norm(a, w, h, gate, norm, *, name):
    s, kdim = a.shape
    d = w.shape[1]
    tk = _tile(kdim, 2048)
    nk = kdim // tk
    tm = _tile(s, 256 if nk == 1 else 512)

    def body(*refs):
        a_ref, w_ref, h_ref, gate_ref = refs[:4]
        outs = refs[7:] if norm is not None else refs[4:]
        part = _dot(a_ref[...], w_ref[...])

        def finish(t):
            hn = h_ref[...] + gate_ref[...] * t
            outs[0][...] = hn
            outs[1][...] = t.astype(BF16)
            if norm is not None:
                g_ref, sc_ref, sh_ref = refs[4:7]
                r = lax.rsqrt(jnp.mean(hn * hn, axis=-1, keepdims=True) + EPS)
                outs[2][...] = (((hn * r) * g_ref[...]) * (1.0 + sc_ref[...]) + sh_ref[...]).astype(BF16)

        if nk == 1:
            finish(part)
            return
        acc_ref = refs[-1]
        k = pl.program_id(1)

        @pl.when(k == 0)
        def _():
            acc_ref[...] = part

        @pl.when(jnp.logical_and(k > 0, k < nk - 1))
        def _():
            acc_ref[...] += part

        @pl.when(k == nk - 1)
        def _():
            finish(acc_ref[...] + part)

    row = pl.BlockSpec((tm, d), lambda i, k: (i, 0))
    vec = pl.BlockSpec((1, d), lambda i, k: (0, 0))
    in_specs = [pl.BlockSpec((tm, tk), lambda i, k: (i, k)), pl.BlockSpec((tk, d), lambda i, k: (k, 0)), row, vec]
    args = [a, w, h, gate]
    out_shape = [jax.ShapeDtypeStruct((s, d), F32), jax.ShapeDtypeStruct((s, d), BF16)]
    if norm is not None:
        in_specs += [vec, vec, vec]
        args += list(norm)
        out_shape.append(jax.ShapeDtypeStruct((s, d), BF16))
    outs = _pcall(body, name=name, out_shape=tuple(out_shape), grid=(s // tm, nk), in_specs=in_specs,
                  out_specs=(row,) * len(out_shape), scratch=[] if nk == 1 else [pltpu.VMEM((tm, d), F32)])(*args)
    return outs if norm is not None else (*outs, None)


def _mm_merge(o_a, o_b, w_a, w_b, proj, *, name):
    s = o_a.shape[0]
    d = w_a.shape[1]
    tm = _tile(s, 512)
    ga_blk = OFF_GATES // d

    def body(oa_ref, ob_ref, wa_ref, wb_ref, ga_ref, gb_ref, m_ref, ya_ref, yb_ref):
        ya, yb = _dot(oa_ref[...], wa_ref[...]), _dot(ob_ref[...], wb_ref[...])
        m_ref[...] = (jax.nn.sigmoid(ga_ref[...]) * ya + jax.nn.sigmoid(gb_ref[...]) * yb).astype(BF16)
        ya_ref[...] = ya.astype(BF16)
        yb_ref[...] = yb.astype(BF16)

    row = pl.BlockSpec((tm, d), lambda i: (i, 0))
    act = pl.BlockSpec((tm, o_a.shape[1]), lambda i: (i, 0))
    wspec = pl.BlockSpec(w_a.shape, lambda i: (0, 0))
    shp = jax.ShapeDtypeStruct((s, d), BF16)
    return _pcall(body, name=name, out_shape=(shp, shp, shp), grid=(s // tm,),
                  in_specs=[act, act, wspec, wspec, pl.BlockSpec((tm, d), lambda i: (i, ga_blk)),
                            pl.BlockSpec((tm, d), lambda i: (i, ga_blk + 1))],
                  out_specs=(row, row, row))(o_a, o_b, w_a, w_b, proj, proj)


def _mm_down_t_swiglu(df, w_down, g, u, *, name):
    s, d = df.shape
    f = w_down.shape[0]
    tm, tn = _tile(s, 1024), _tile(f, 512)

    def body(df_ref, w_ref, g_ref, u_ref, dg_ref, du_ref):
        da = _dot(df_ref[...], w_ref[...], NT)
        gf = g_ref[...].astype(F32)
        sg = jax.nn.sigmoid(gf)
        dg_ref[...] = (da * u_ref[...].astype(F32) * (sg * (1.0 + gf * (1.0 - sg)))).astype(BF16)
        du_ref[...] = (da * (gf * sg)).astype(BF16)

    tile = pl.BlockSpec((tm, tn), lambda i, j: (i, j))
    shp = jax.ShapeDtypeStruct((s, f), BF16)
    return _pcall(body, name=name, out_shape=(shp, shp), grid=(s // tm, f // tn),
                  in_specs=[pl.BlockSpec((tm, d), lambda i, j: (i, 0)), pl.BlockSpec((tn, d), lambda i, j: (j, 0)),
                            tile, tile],
                  out_specs=(tile, tile))(df, w_down, g, u)


def _rmsmod_fwd(h, g, scale, shift, *, name, deps=()):
    s, d = h.shape
    ts = _rows(s, d)

    def body(h_ref, g_ref, sc_ref, sh_ref, u_ref):
        hf = h_ref[...]
        r = lax.rsqrt(jnp.mean(hf * hf, axis=-1, keepdims=True) + EPS)
        u_ref[...] = (((hf * r) * g_ref[...]) * (1.0 + sc_ref[...]) + sh_ref[...]).astype(BF16)

    row = pl.BlockSpec((ts, d), lambda i: (i, 0))
    vec = pl.BlockSpec((1, d), lambda i: (0, 0))
    return _pcall(body, name=name, out_shape=jax.ShapeDtypeStruct((s, d), BF16), grid=(s // ts,),
                  in_specs=[row, vec, vec, vec], out_specs=row, deps=deps)(h, g, scale, shift)


def _rmsmod_bwd(du, h, g, scale, dres, *, name):
    s, d = h.shape
    ts = _rows(s, d)

    def body(du_ref, h_ref, g_ref, sc_ref, dres_ref, dh_ref, dsh_ref, dsc_ref, dg_ref):
        @pl.when(pl.program_id(0) == 0)
        def _():
            dsh_ref[...] = jnp.zeros_like(dsh_ref)
            dsc_ref[...] = jnp.zeros_like(dsc_ref)
            dg_ref[...] = jnp.zeros_like(dg_ref)

        hf, duf, gain = h_ref[...], du_ref[...], g_ref[...]
        r = lax.rsqrt(jnp.mean(hf * hf, axis=-1, keepdims=True) + EPS)
        xh = hf * r
        dn = duf * (1.0 + sc_ref[...])
        dsh_ref[...] += jnp.sum(duf, axis=0, keepdims=True)
        dsc_ref[...] += jnp.sum(duf * (xh * gain), axis=0, keepdims=True)
        dg_ref[...] += jnp.sum(dn * xh, axis=0, keepdims=True)
        dxh = dn * gain
        dh_ref[...] = dres_ref[...] + r * (dxh - xh * jnp.mean(dxh * xh, axis=-1, keepdims=True))

    row = pl.BlockSpec((ts, d), lambda i: (i, 0))
    vec = pl.BlockSpec((1, d), lambda i: (0, 0))
    vshape = jax.ShapeDtypeStruct((1, d), F32)
    return _pcall(body, name=name, out_shape=(jax.ShapeDtypeStruct((s, d), F32), vshape, vshape, vshape),
                  grid=(s // ts,), in_specs=[row, row, vec, vec, row],
                  out_specs=(row, vec, vec, vec))(du, h, g, scale, dres)


def _resid_gate_bwd(dh, t, gate, *, name, deps=()):
    s, d = dh.shape
    ts = _rows(s, d)

    def body(dh_ref, t_ref, g_ref, dt_ref, dg_ref):
        @pl.when(pl.program_id(0) == 0)
        def _():
            dg_ref[...] = jnp.zeros_like(dg_ref)

        dhf = dh_ref[...]
        dt_ref[...] = (dhf * g_ref[...]).astype(BF16)
        dg_ref[...] += jnp.sum(dhf * t_ref[...], axis=0, keepdims=True)

    row = pl.BlockSpec((ts, d), lambda i: (i, 0))
    vec = pl.BlockSpec((1, d), lambda i: (0, 0))
    return _pcall(body, name=name,
                  out_shape=(jax.ShapeDtypeStruct((s, d), BF16), jax.ShapeDtypeStruct((1, d), F32)),
                  grid=(s // ts,), in_specs=[row, row, vec], out_specs=(row, vec), deps=deps)(dh, t, gate)


def _merge_bwd(dm, proj, y_a, y_b, *, name):
    s, d = y_a.shape
    ts = _rows(s, d)
    ga_blk = OFF_GATES // d

    def body(dm_ref, ga_ref, gb_ref, ya_ref, yb_ref, dya_ref, dyb_ref, dga_ref, dgb_ref):
        dmf = dm_ref[...]
        sa, sb = jax.nn.sigmoid(ga_ref[...]), jax.nn.sigmoid(gb_ref[...])
        dya_ref[...] = (dmf * sa).astype(BF16)
        dyb_ref[...] = (dmf * sb).astype(BF16)
        dga_ref[...] = (dmf * ya_ref[...] * (sa * (1.0 - sa))).astype(BF16)
        dgb_ref[...] = (dmf * yb_ref[...] * (sb * (1.0 - sb))).astype(BF16)

    row = pl.BlockSpec((ts, d), lambda i: (i, 0))
    ga = pl.BlockSpec((ts, d), lambda i: (i, ga_blk))
    gb = pl.BlockSpec((ts, d), lambda i: (i, ga_blk + 1))
    shp = jax.ShapeDtypeStruct((s, d), BF16)
    return _pcall(body, name=name, out_shape=(shp, shp, shp, shp), grid=(s // ts,),
                  in_specs=[row, ga, gb, row, row], out_specs=(row, row, row, row))(dm, proj, proj, y_a, y_b)


def _mm_swiglu(u2, w_gate_up, *, name):
    s, d = u2.shape
    f = w_gate_up.shape[1] // 2
    tm, tn = _tile(s, 1024), _tile(f, 512)
    nj = f // tn

    def body(x_ref, wg_ref, wu_ref, a_ref, g_ref, u_ref):
        x = x_ref[...]
        gf, uf = _dot(x, wg_ref[...]), _dot(x, wu_ref[...])
        a_ref[...] = ((gf * jax.nn.sigmoid(gf)) * uf).astype(BF16)
        g_ref[...] = gf.astype(BF16)
        u_ref[...] = uf.astype(BF16)

    out = pl.BlockSpec((tm, tn), lambda i, j: (i, j))
    shp = jax.ShapeDtypeStruct((s, f), BF16)
    return _pcall(body, name=name, out_shape=(shp, shp, shp), grid=(s // tm, nj),
                  in_specs=[pl.BlockSpec((tm, d), lambda i, j: (i, 0)), pl.BlockSpec((d, tn), lambda i, j: (0, j)),
                            pl.BlockSpec((d, tn), lambda i, j: (0, nj + j))],
                  out_specs=(out, out, out))(u2, w_gate_up, w_gate_up)


def _loss_fwd(y, tgt, *, name):
    s, d = y.shape
    ts = _rows(s, d)

    def body(y_ref, t_ref, l_ref, dy_ref):
        @pl.when(pl.program_id(0) == 0)
        def _():
            l_ref[...] = jnp.zeros_like(l_ref)

        e = y_ref[...] - t_ref[...]
        dy_ref[...] = e * (1.0 / d)
        per_tok = jnp.sum(e * e, axis=1, keepdims=True) * (1.0 / d)
        l_ref[...] += 0.5 * jnp.sum(per_tok, axis=0, keepdims=True)

    row = pl.BlockSpec((ts, d), lambda i: (i, 0))
    return _pcall(body, name=name,
                  out_shape=(jax.ShapeDtypeStruct((1, 128), F32), jax.ShapeDtypeStruct((s, d), F32)),
                  grid=(s // ts,), in_specs=[row, row],
                  out_specs=(pl.BlockSpec((1, 128), lambda i: (0, 0)), row))(y, tgt)


def _rope_tables(seq):
    inv = jnp.power(ROPE_THETA, -jnp.arange(0, HEAD_DIM, 2, dtype=F32) / HEAD_DIM)
    ang = jnp.arange(seq, dtype=F32)[:, None] * inv[None, :]
    cos, sin = jnp.cos(ang), jnp.sin(ang)
    return jnp.concatenate([cos, cos], axis=1), jnp.concatenate([-sin, sin], axis=1)


def _qkrope_fwd(proj, gains, cos2, sin2, *, name):
    s = proj.shape[0]
    ts = _rows(s, A_W)

    def body(x_ref, g_ref, c_ref, s_ref, o_ref):
        gain, cos, sin = g_ref[...], c_ref[...], s_ref[...]
        for h in range(A_HEADS):
            lanes = slice(h * HEAD_DIM, (h + 1) * HEAD_DIM)
            x = x_ref[:, lanes]
            y = (x * lax.rsqrt(jnp.mean(x * x, axis=-1, keepdims=True) + EPS)) * gain
            o_ref[:, lanes] = (y * cos + pltpu.roll(y, HEAD_DIM // 2, 1) * sin).astype(BF16)

    heads = pl.BlockSpec((ts, A_W), lambda i, j: (i, j))
    tab = pl.BlockSpec((ts, HEAD_DIM), lambda i, j: (i, 0))
    gain = pl.BlockSpec((None, 1, HEAD_DIM), lambda i, j: (j, 0, 0))
    return _pcall(body, name=name, out_shape=jax.ShapeDtypeStruct((s, 2 * A_W), BF16),
                  grid=(s // ts, 2), in_specs=[heads, gain, tab, tab], out_specs=heads)(
                      proj, gains, cos2, sin2)


def _qkrope_bwd(dqk, proj, gains, cos2, sin2, *, name):
    s = proj.shape[0]
    ts = _rows(s, A_W)

    def body(d_ref, x_ref, g_ref, c_ref, s_ref, dx_ref, dg_ref):
        @pl.when(pl.program_id(1) == 0)
        def _():
            dg_ref[...] = jnp.zeros_like(dg_ref)

        gain, cos, sin = g_ref[...], c_ref[...], s_ref[...]
        dg = jnp.zeros((1, HEAD_DIM), F32)
        for h in range(A_HEADS):
            lanes = slice(h * HEAD_DIM, (h + 1) * HEAD_DIM)
            dout = d_ref[:, lanes]
            dy = dout * cos + pltpu.roll(dout * sin, HEAD_DIM // 2, 1)
            x = x_ref[:, lanes]
            r = lax.rsqrt(jnp.mean(x * x, axis=-1, keepdims=True) + EPS)
            xh = x * r
            dg = dg + jnp.sum(dy * xh, axis=0, keepdims=True)
            dxh = dy * gain
            dx_ref[:, lanes] = (r * (dxh - xh * jnp.mean(dxh * xh, axis=-1, keepdims=True))).astype(BF16)
        dg_ref[...] += dg

    heads = pl.BlockSpec((ts, A_W), lambda j, i: (i, j))
    tab = pl.BlockSpec((ts, HEAD_DIM), lambda j, i: (i, 0))
    gain = pl.BlockSpec((None, 1, HEAD_DIM), lambda j, i: (j, 0, 0))
    return _pcall(body, name=name,
                  out_shape=(jax.ShapeDtypeStruct((s, 2 * A_W), BF16), jax.ShapeDtypeStruct((2, 1, HEAD_DIM), F32)),
                  grid=(2, s // ts), in_specs=[heads, heads, gain, tab, tab],
                  out_specs=(heads, gain))(dqk, proj, gains, cos2, sin2)


def _block_rows(blk):
    if isinstance(blk, int):
        return pl.ds(blk * BLOCK, BLOCK)
    return pl.ds(pl.multiple_of(blk * BLOCK, BLOCK), BLOCK)


def _band_masks(n, with_prev):
    row = lax.broadcasted_iota(jnp.int32, (BLOCK, BLOCK), 0)
    col = lax.broadcasted_iota(jnp.int32, (BLOCK, BLOCK), 1)
    cur = col <= row
    if not with_prev:
        return [(n, cur)]
    prev = col >= row + jnp.where(n >= 1, 0, BLOCK)
    return [(n, cur), (jnp.maximum(n - 1, 0), prev)]


def _dil_fwd(q_arr, k_arr, v_arr, offs, length, dil, *, name):
    nj, nb = dil * HEADS_PER_GROUP, length // BLOCK
    ju = HEADS_PER_GROUP
    qo, ko, vo = (off // ju for off in offs)
    assert all(off % ju == 0 for off in offs)

    def body(q_ref, k_ref, v_ref, o_ref, l_ref):
        n = pl.program_id(1)
        masks = _band_masks(n, nb > 1)
        for cb in range(ju):
            lanes = slice(cb * HEAD_DIM, (cb + 1) * HEAD_DIM)
            q = q_ref[:, lanes].astype(BF16)
            parts = []
            for blk, mask in masks:
                rows = _block_rows(blk)
                sc = _dot(q, k_ref[rows, lanes].astype(BF16), NT) * ATT_SCALE
                parts.append((jnp.where(mask, sc, MASKED), rows))
            m = parts[0][0].max(axis=-1, keepdims=True)
            for sc, _ in parts[1:]:
                m = jnp.maximum(m, sc.max(axis=-1, keepdims=True))
            den = jnp.zeros((BLOCK, 1), F32)
            acc = jnp.zeros((BLOCK, HEAD_DIM), F32)
            for sc, rows in parts:
                p = jnp.exp(sc - m)
                den = den + jnp.sum(p, axis=-1, keepdims=True)
                acc = acc + _dot(p.astype(BF16), v_ref[rows, lanes].astype(BF16))
            o_ref[:, lanes] = acc / den
            l_ref[:, lanes] = jnp.broadcast_to(m + jnp.log(den), (BLOCK, HEAD_DIM))

    qspec = pl.BlockSpec((BLOCK, ju * HEAD_DIM), lambda j, n: (n, qo + j))
    kspec = pl.BlockSpec((length, ju * HEAD_DIM), lambda j, n: (0, ko + j))
    vspec = pl.BlockSpec((length, ju * HEAD_DIM), lambda j, n: (0, vo + j))
    ospec = pl.BlockSpec((BLOCK, ju * HEAD_DIM), lambda j, n: (n, j))
    shp = jax.ShapeDtypeStruct((length, nj * HEAD_DIM), F32)
    return _pcall(body, name=name, out_shape=(shp, shp), grid=(nj // ju, nb), in_specs=[qspec, kspec, vspec],
                  out_specs=(ospec, ospec))(q_arr, k_arr, v_arr)


def _dil_bwd(q_arr, k_arr, v_arr, offs, o, lse, do, dlse, length, dil, *, name):
    nj, nb = dil * HEADS_PER_GROUP, length // BLOCK
    ju = HEADS_PER_GROUP if length <= 4 * BLOCK else 2
    qo, ko, vo = (off // ju for off in offs)
    assert all(off % ju == 0 for off in offs)

    def body(q_ref, k_ref, v_ref, o_ref, l_ref, do_ref, dl_ref, dq_ref, dk_ref, dv_ref):
        dk_ref[...] = jnp.zeros_like(dk_ref)
        dv_ref[...] = jnp.zeros_like(dv_ref)

        def step(n, carry):
            qrows = _block_rows(n)
            masks = _band_masks(n, nb > 1)
            for cb in range(ju):
                lanes = slice(cb * HEAD_DIM, (cb + 1) * HEAD_DIM)
                q = q_ref[qrows, lanes].astype(BF16)
                dof = do_ref[qrows, lanes]
                dob = dof.astype(BF16)
                lse_b = l_ref[qrows, lanes]
                shift = dl_ref[qrows, lanes] - jnp.sum(dof * o_ref[qrows, lanes], axis=-1, keepdims=True)
                dq = jnp.zeros((BLOCK, HEAD_DIM), F32)
                for blk, mask in masks:
                    rows = _block_rows(blk)
                    kk, vv = k_ref[rows, lanes].astype(BF16), v_ref[rows, lanes].astype(BF16)
                    sc = _dot(q, kk, NT) * ATT_SCALE
                    p = jnp.where(mask, jnp.exp(sc - lse_b), 0.0)
                    ds = (p * (_dot(dob, vv, NT) + shift)).astype(BF16)
                    dq = dq + _dot(ds, kk)
                    dk_ref[rows, lanes] += _dot(ds, q, TN) * ATT_SCALE
                    dv_ref[rows, lanes] += _dot(p.astype(BF16), dob, TN)
                dq_ref[qrows, lanes] = dq * ATT_SCALE
            return carry

        if nb == 1:
            step(0, 0)
        else:
            lax.fori_loop(0, nb, step, 0)

    def col(off):
        return pl.BlockSpec((length, ju * HEAD_DIM), lambda j: (0, off + j))

    shp = jax.ShapeDtypeStruct((length, nj * HEAD_DIM), F32)
    return _pcall(body, name=name, out_shape=(shp, shp, shp), grid=(nj // ju,),
                  in_specs=[col(qo), col(ko), col(vo), col(0), col(0), col(0), col(0)],
                  out_specs=(col(0), col(0), col(0)))(q_arr, k_arr, v_arr, o, lse, do, dlse)


def _combine_weights(l_refs):
    ls = [r[...] for r in l_refs]
    m = jnp.maximum(jnp.maximum(ls[0], ls[1]), ls[2])
    es = [jnp.exp(l - m) for l in ls]
    den = es[0] + es[1] + es[2]
    return [e / den for e in es]


def _combine_fwd(os_, lses, *, name):
    s = os_[0].shape[0]
    ts = _rows(s, GROUP_W)

    def body(o0, o1, o2, l0, l1, l2, out_ref):
        w = _combine_weights((l0, l1, l2))
        out_ref[...] = (w[0] * o0[...] + w[1] * o1[...] + w[2] * o2[...]).astype(BF16)

    row = pl.BlockSpec((ts, GROUP_W), lambda i: (i, 0))
    return _pcall(body, name=name, out_shape=jax.ShapeDtypeStruct((s, GROUP_W), BF16), grid=(s // ts,),
                  in_specs=[row] * 6, out_specs=row)(*os_, *lses)


def _combine_bwd(do_a, os_, lses, *, name, deps=()):
    s = do_a.shape[0]
    ts = _rows(s, GROUP_W)

    def body(d_ref, o0, o1, o2, l0, l1, l2, do0, do1, do2, dl0, dl1, dl2):
        w = _combine_weights((l0, l1, l2))
        d = d_ref[...]
        og = [o0[...], o1[...], o2[...]]
        oa = w[0] * og[0] + w[1] * og[1] + w[2] * og[2]
        ta = jnp.sum(d * oa, axis=-1, keepdims=True)
        for g, (do_ref, dl_ref) in enumerate(((do0, dl0), (do1, dl1), (do2, dl2))):
            do_ref[...] = w[g] * d
            dl_ref[...] = w[g] * (jnp.sum(d * og[g], axis=-1, keepdims=True) - ta)

    head = pl.BlockSpec((ts, HEAD_DIM), lambda i, h: (i, h))
    shp = jax.ShapeDtypeStruct((s, GROUP_W), F32)
    return _pcall(body, name=name, out_shape=(shp,) * 6, grid=(s // ts, HEADS_PER_GROUP),
                  in_specs=[head] * 7, out_specs=(head,) * 6, deps=deps)(do_a, *os_, *lses)


def _dot_exact(x, ones_mask):
    hi = x.astype(BF16)
    r1 = x - hi.astype(F32)
    mid = r1.astype(BF16)
    lo = (r1 - mid.astype(F32)).astype(BF16)
    return _dot(hi, ones_mask) + _dot(mid, ones_mask) + _dot(lo, ones_mask)


SB_QROWS = 2 * BLOCK
SB_UNROLL = 4


def _sb_mask(j, i):
    row = lax.broadcasted_iota(jnp.int32, (SB_QROWS, BLOCK), 0)
    col = lax.broadcasted_iota(jnp.int32, (SB_QROWS, BLOCK), 1)
    return col + (j * BLOCK - i * SB_QROWS) < row


def _sb_steps(i):
    return ((i + 1) * (SB_QROWS // BLOCK) + SB_UNROLL - 1) // SB_UNROLL


def _sb_scores(q, kk, j, i):
    mask = _sb_mask(j, i)
    z = _dot(q, kk, NT) * ATT_SCALE
    sp = jnp.log1p(jnp.exp(-jnp.abs(z)))
    log_beta = jnp.minimum(z, 0.0) - sp
    log_1mb = jnp.where(mask, jnp.minimum(-z, 0.0) - sp, 0.0)
    return z, log_beta, log_1mb, mask


def _sb_weights(log_beta, log_1mb, mask, run, upper):
    after = run + _dot_exact(log_1mb, upper)
    return jnp.where(mask, jnp.exp(log_beta + after), 0.0)


def _tri(strict_lower):
    row = lax.broadcasted_iota(jnp.int32, (BLOCK, BLOCK), 0)
    col = lax.broadcasted_iota(jnp.int32, (BLOCK, BLOCK), 1)
    return ((row > col) if strict_lower else (row < col)).astype(BF16)


def _sb_fwd(proj, *, name):
    s = proj.shape[0]
    assert s % (BLOCK * SB_UNROLL) == 0 and s % SB_QROWS == 0
    qb, kb, vb = OFF_QB // HEAD_DIM, OFF_KB // HEAD_DIM, OFF_VB // HEAD_DIM

    def body(q_ref, k_ref, v_ref, o_ref):
        i = pl.program_id(1)
        q = q_ref[...].astype(BF16)
        upper = _tri(True)
        nsteps = _sb_steps(i)

        def step(t, carry):
            acc, run = carry
            for b in reversed(range(SB_UNROLL)):
                j = (nsteps - 1 - t) * SB_UNROLL + b
                rows = _block_rows(j)
                _, log_beta, log_1mb, mask = _sb_scores(q, k_ref[rows, :].astype(BF16), j, i)
                a = _sb_weights(log_beta, log_1mb, mask, run, upper)
                acc = acc + _dot(a.astype(BF16), v_ref[rows, :].astype(BF16))
                run = run + jnp.sum(log_1mb, axis=-1, keepdims=True)
            return acc, run

        acc, _ = lax.fori_loop(0, nsteps, step,
                               (jnp.zeros((SB_QROWS, HEAD_DIM), F32), jnp.zeros((SB_QROWS, 1), F32)))
        o_ref[...] = acc.astype(BF16)

    return _pcall(body, name=name, out_shape=jax.ShapeDtypeStruct((s, B_W), BF16), grid=(SB_HEADS, s // SB_QROWS),
                  in_specs=[pl.BlockSpec((SB_QROWS, HEAD_DIM), lambda h, i: (i, qb + h)),
                            pl.BlockSpec((s, HEAD_DIM), lambda h, i: (0, kb + h)),
                            pl.BlockSpec((s, HEAD_DIM), lambda h, i: (0, vb + h))],
                  out_specs=pl.BlockSpec((SB_QROWS, HEAD_DIM), lambda h, i: (i, h)))(proj, proj, proj)


def _sb_bwd(proj, do_b, *, name):
    s = proj.shape[0]
    assert s % (BLOCK * SB_UNROLL) == 0 and s % SB_QROWS == 0
    nkb = s // BLOCK
    qb, kb, vb = OFF_QB // HEAD_DIM, OFF_KB // HEAD_DIM, OFF_VB // HEAD_DIM

    def body(q_ref, k_ref, v_ref, do_ref, dq_ref, dk_ref, dv_ref, z_s, a_s):
        i = pl.program_id(1)

        @pl.when(i == 0)
        def _():
            dk_ref[...] = jnp.zeros_like(dk_ref)
            dv_ref[...] = jnp.zeros_like(dv_ref)

        q = q_ref[...].astype(BF16)
        dob = do_ref[...].astype(BF16)
        upper, lower = _tri(True), _tri(False)
        nsteps = _sb_steps(i)

        def recompute(t, run):
            for b in reversed(range(SB_UNROLL)):
                j = (nsteps - 1 - t) * SB_UNROLL + b
                z, log_beta, log_1mb, mask = _sb_scores(q, k_ref[_block_rows(j), :].astype(BF16), j, i)
                z_s[j] = z
                a_s[j] = _sb_weights(log_beta, log_1mb, mask, run, upper)
                run = run + jnp.sum(log_1mb, axis=-1, keepdims=True)
            return run

        lax.fori_loop(0, nsteps, recompute, jnp.zeros((SB_QROWS, 1), F32))

        def grads(t, carry):
            dq, run = carry
            for b in range(SB_UNROLL):
                j = t * SB_UNROLL + b
                rows = _block_rows(j)
                kk, vv = k_ref[rows, :].astype(BF16), v_ref[rows, :].astype(BF16)
                z, a = z_s[j], a_s[j]
                de = _dot(dob, vv, NT) * a
                before = run + _dot_exact(de, lower)
                dz = (de * jax.nn.sigmoid(-z)
                      - jnp.where(_sb_mask(j, i), jax.nn.sigmoid(z), 0.0) * before).astype(BF16)
                dk_ref[rows, :] += _dot(dz, q, TN) * ATT_SCALE
                dv_ref[rows, :] += _dot(a.astype(BF16), dob, TN)
                dq = dq + _dot(dz, kk)
                run = run + jnp.sum(de, axis=-1, keepdims=True)
            return dq, run

        dq, _ = lax.fori_loop(0, nsteps, grads,
                              (jnp.zeros((SB_QROWS, HEAD_DIM), F32), jnp.zeros((SB_QROWS, 1), F32)))
        dq_ref[...] = dq * ATT_SCALE

    blk = pl.BlockSpec((SB_QROWS, HEAD_DIM), lambda h, i: (i, h))
    full = pl.BlockSpec((s, HEAD_DIM), lambda h, i: (0, h))
    shp = jax.ShapeDtypeStruct((s, B_W), F32)
    return _pcall(body, name=name, out_shape=(shp, shp, shp), grid=(SB_HEADS, s // SB_QROWS),
                  in_specs=[pl.BlockSpec((SB_QROWS, HEAD_DIM), lambda h, i: (i, qb + h)),
                            pl.BlockSpec((s, HEAD_DIM), lambda h, i: (0, kb + h)),
                            pl.BlockSpec((s, HEAD_DIM), lambda h, i: (0, vb + h)), blk],
                  out_specs=(blk, full, full),
                  scratch=[pltpu.VMEM((nkb, SB_QROWS, BLOCK), F32), pltpu.VMEM((nkb, SB_QROWS, BLOCK), F32)])(
                      proj, proj, proj, do_b)


def _coords():
    return lax.axis_index("x"), lax.axis_index("y"), lax.axis_index("c")


def _flip(v, bit):
    return 1 - v if bit else v


def _shard_of(ref, axis, idx, size):
    if axis == 0:
        sl = pl.ds(pl.multiple_of(idx * size, 16), size)
        return ref.at[sl, :] if len(ref.shape) == 2 else ref.at[:, sl, :]
    sl = pl.ds(pl.multiple_of(idx * size, 128), size)
    return ref.at[:, sl] if len(ref.shape) == 2 else ref.at[:, :, sl]


def _small_allgather(v, *, name, silu=False, deps=()):
    n = v.shape[1]

    def body(v_ref, out_ref, send_sems, recv_sems):
        x, y, c = _coords()
        me = 4 * x + 2 * y + c
        val = v_ref[...]
        out_ref[me] = val * jax.nn.sigmoid(val) if silu else val
        copies = []
        for k in range(1, N_DEV):
            peer = (_flip(x, k & 4), _flip(y, k & 2), _flip(c, k & 1))
            copies.append(pltpu.make_async_remote_copy(
                src_ref=out_ref.at[me], dst_ref=out_ref.at[me], send_sem=send_sems.at[k - 1],
                recv_sem=recv_sems.at[k - 1], device_id=peer, device_id_type=MESH))
        for cp in copies:
            cp.start()
        for cp in copies:
            cp.wait_recv()
        for cp in copies:
            cp.wait_send()

    return _pcall(body, name=name, out_shape=jax.ShapeDtypeStruct((N_DEV, 1, n), F32),
                  in_specs=[pl.BlockSpec(memory_space=pltpu.VMEM)], out_specs=pl.BlockSpec(memory_space=pltpu.VMEM),
                  scratch=[pltpu.SemaphoreType.DMA((N_DEV - 1,)), pltpu.SemaphoreType.DMA((N_DEV - 1,))],
                  deps=deps)(v)


def _cast_place(w, layer, axis, me, *, name):
    _, r, c = w.shape
    tr = _rows(r, c)
    nrt = r // tr

    def body(me_ref, w_ref, o_ref):
        o_ref[...] = w_ref[...].astype(BF16)

    wspec = pl.BlockSpec((None, tr, c), lambda i, me_ref: (layer, i, 0))
    if axis == 0:
        ospec = pl.BlockSpec((tr, c), lambda i, me_ref: (me_ref[0] * nrt + i, 0))
        shape = (r * N_DEV, c)
    else:
        ospec = pl.BlockSpec((tr, c), lambda i, me_ref: (i, me_ref[0]))
        shape = (r, c * N_DEV)
    return _pcall(body, name=name, out_shape=jax.ShapeDtypeStruct(shape, BF16), grid=(nrt,), in_specs=[wspec],
                  out_specs=ospec, prefetch=1)(me, w)


def _pair_sum(grad, sib, core, axis, *, name):
    _, r, c = sib.shape
    tr = _rows(r, c)
    nrt = r // tr

    def body(core_ref, g_ref, s_ref, o_ref):
        o_ref[...] = (g_ref[...].astype(F32) + s_ref[...].astype(F32)).astype(BF16)

    if axis == 0:
        gspec = pl.BlockSpec((tr, c), lambda q, i, core_ref: ((2 * q + core_ref[0]) * nrt + i, 0))
    else:
        gspec = pl.BlockSpec((tr, c), lambda q, i, core_ref: (i, 2 * q + core_ref[0]))
    sspec = pl.BlockSpec((None, tr, c), lambda q, i, core_ref: (q, i, 0))
    return _pcall(body, name=name, out_shape=jax.ShapeDtypeStruct(sib.shape, BF16), grid=(N_CHIPS, nrt),
                  in_specs=[gspec, sspec], out_specs=sspec, prefetch=1)(core, grad, sib)


ANY_SPEC = pl.BlockSpec(memory_space=pl.ANY)
SEM_SPEC = pl.BlockSpec(memory_space=pltpu.SEMAPHORE)
SPLIT_PARAMS = dict(has_side_effects=pltpu.SideEffectType.DATAFLOW_SIDE_EFFECTING)


def _split_start(copies_fn, buffers, sem_shape, after, *, name):
    n = len(buffers)
    rows, cols = sem_shape
    ns = rows * cols
    extra = ([] if after is None else [after]) + _take_token()

    def body(*refs):
        sems = refs[n + len(extra):n + len(extra) + 2 * ns]
        for cp in copies_fn(refs[:n], _sem_rows(sems[:ns], cols), _sem_rows(sems[ns:], cols)):
            cp.start()
        refs[-1][...] = jnp.zeros_like(refs[-1])

    sem = pltpu.SemaphoreType.DMA(())
    outs = pl.pallas_call(
        body, name=name,
        out_shape=((sem,) * (2 * ns) + tuple(jax.ShapeDtypeStruct(b.shape, b.dtype) for b in buffers) + (TOKEN,)),
        in_specs=(ANY_SPEC,) * (n + len(extra)),
        out_specs=(SEM_SPEC,) * (2 * ns) + (ANY_SPEC,) * n + (pl.BlockSpec(memory_space=pltpu.VMEM),),
        input_output_aliases={i: 2 * ns + i for i in range(n)},
        compiler_params=pltpu.CompilerParams(**SPLIT_PARAMS))(*buffers, *extra)
    _ORDER["token"] = outs[-1]
    return list(outs[:ns]), list(outs[ns:2 * ns]), list(outs[2 * ns:2 * ns + n]), outs[-1]


def _split_wait(copies_fn, send_sems, recv_sems, buffers, after, sem_rows, *, name):
    n, ns = len(buffers), len(send_sems)
    cols = ns // sem_rows
    extra = ([] if after is None else [after]) + _take_token()

    def body(*refs):
        sems = refs[n:n + 2 * ns]
        copies = copies_fn(refs[:n], _sem_rows(sems[:ns], cols), _sem_rows(sems[ns:], cols))
        for cp in copies:
            cp.wait_send()
        for cp in copies:
            cp.wait_recv()
        refs[-1][...] = jnp.zeros_like(refs[-1])

    outs = pl.pallas_call(
        body, name=name, out_shape=tuple(jax.ShapeDtypeStruct(b.shape, b.dtype) for b in buffers) + (TOKEN,),
        in_specs=(ANY_SPEC,) * n + (SEM_SPEC,) * (2 * ns) + (ANY_SPEC,) * len(extra),
        out_specs=(ANY_SPEC,) * n + (pl.BlockSpec(memory_space=pltpu.VMEM),),
        input_output_aliases={i: i for i in range(n)},
        compiler_params=pltpu.CompilerParams(**SPLIT_PARAMS))(*buffers, *send_sems, *recv_sems, *extra)
    _ORDER["token"] = outs[-1]
    return list(outs[:n])


def _sem_rows(sems, cols):
    return [sems[i:i + cols] for i in range(0, len(sems), cols)]


def _empty_hbm(shape, dtype):
    return pltpu.with_memory_space_constraint(lax.empty(shape, dtype), pltpu.HBM)


class _SplitGather:
    def __init__(self, fulls, axes, tag):
        self.axes, self.tag, self.nt = list(axes), tag, len(fulls)
        self.sizes = [f.shape[ax] // N_DEV for f, ax in zip(fulls, axes)]
        self.fulls = list(fulls)

    def _slot(self, ref, t, dev):
        return _shard_of(ref, self.axes[t], 4 * dev[0] + 2 * dev[1] + dev[2], self.sizes[t])

    def _first_copies(self, refs, send_sems, recv_sems):
        x, y, c = _coords()
        peers = [(x, y, 1 - c), (1 - x, y, c), (x, 1 - y, c), (1 - x, 1 - y, c)]
        return [pltpu.make_async_remote_copy(
            src_ref=self._slot(refs[t], t, (x, y, c)), dst_ref=self._slot(refs[t], t, (x, y, c)),
            send_sem=send_sems[t][k], recv_sem=recv_sems[t][k], device_id=peer, device_id_type=MESH)
            for t in range(self.nt) for k, peer in enumerate(peers)]

    def _forward_copies(self, refs, send_sems, recv_sems):
        x, y, c = _coords()
        chips = [(1 - x, y), (x, 1 - y), (1 - x, 1 - y)]
        return [pltpu.make_async_remote_copy(
            src_ref=self._slot(refs[t], t, (*chip, c)), dst_ref=self._slot(refs[t], t, (*chip, c)),
            send_sem=send_sems[t][j], recv_sem=recv_sems[t][j], device_id=(x, y, 1 - c), device_id_type=MESH)
            for t in range(self.nt) for j, chip in enumerate(chips)]

    def first(self, after):
        self.s1, self.r1, self.fulls, token = _split_start(
            self._first_copies, self.fulls, (self.nt, 4), after, name=f"comm_gather1_start_{self.tag}")
        return token

    def forward(self, after):
        bufs = _split_wait(self._first_copies, self.s1, self.r1, self.fulls, after, self.nt,
                           name=f"comm_gather1_wait_{self.tag}")
        self.s2, self.r2, self.fulls, token = _split_start(
            self._forward_copies, bufs, (self.nt, 3), after, name=f"comm_gather2_start_{self.tag}")
        return token

    def finish(self, after):
        return _split_wait(self._forward_copies, self.s2, self.r2, self.fulls, after, self.nt,
                           name=f"comm_gather2_wait_{self.tag}")


class _SplitPairExchange:
    def __init__(self, grads, axes, tag):
        self.nt, self.tag, self.axes = len(grads), tag, list(axes)
        self.grads = list(grads)
        self.sizes = [g.shape[ax] // N_DEV for g, ax in zip(grads, axes)]

    def _copies(self, refs, send_sems, recv_sems):
        nt = self.nt
        x, y, c = _coords()
        return [pltpu.make_async_remote_copy(
            src_ref=_shard_of(refs[t], self.axes[t], 2 * q + 1 - c, self.sizes[t]), dst_ref=refs[nt + t].at[q],
            send_sem=send_sems[t][q], recv_sem=recv_sems[t][q], device_id=(x, y, 1 - c), device_id_type=MESH)
            for t in range(nt) for q in range(N_CHIPS)]

    def start(self):
        landing = []
        for g, ax in zip(self.grads, self.axes):
            dims = list(g.shape)
            dims[ax] //= N_DEV
            landing.append(_empty_hbm((N_CHIPS, *dims), g.dtype))
        self.s, self.r, self.bufs, token = _split_start(
            self._copies, self.grads + landing, (self.nt, N_CHIPS), None,
            name=f"comm_rs_pair_start_{self.tag}")
        return token

    def finish(self, after):
        bufs = _split_wait(self._copies, self.s, self.r, self.bufs, after, self.nt,
                           name=f"comm_rs_pair_wait_{self.tag}")
        return bufs[:self.nt], bufs[self.nt:]


class _ReducePipeline:
    def __init__(self, core):
        self.core, self.items, self.done, self.now = core, [], [], 0

    def add(self, keys, grads, layer):
        axes = [SHARD_AXIS[k] for k in keys]
        pair = _SplitPairExchange([grads[k] for k in keys], axes, f"{keys[0]}{layer}")
        token = pair.start()
        self.items.append(dict(keys=keys, layer=layer, axes=axes, pair=pair, state="pair", since=self.now))
        return [token]

    def tick(self, after, flush=False):
        self.now += 1
        deps = []
        for it in self.items:
            if it["state"] == "pair" and it["since"] < self.now:
                grads, sib = it["pair"].finish(after)
                sums = [_pair_sum(g, s_, self.core, ax, name="pair_sum_" + k)
                        for k, g, s_, ax in zip(it["keys"], grads, sib, it["axes"])]
                it["chip"] = _SplitChipExchange(sums, f"{it['keys'][0]}{it['layer']}")
                deps.append(it["chip"].start())
                it.update(state="chip", since=self.now)
            elif it["state"] == "chip" and (flush or self.now - it["since"] >= 2):
                sums, remote = it["chip"].finish(after)
                self.done.append((it["keys"], it["layer"], sums, remote))
                it["state"] = "done"
        return deps

    def take_done(self):
        out, self.done = self.done, []
        return out


class _SplitChipExchange:
    def __init__(self, sums, tag):
        self.nt, self.tag = len(sums), tag
        self.sums = list(sums)

    def _copies(self, refs, send_sems, recv_sems):
        nt = self.nt
        x, y, c = _coords()
        copies = []
        for t in range(nt):
            for k in range(1, N_CHIPS):
                px, py = _flip(x, k & 2), _flip(y, k & 1)
                copies.append(pltpu.make_async_remote_copy(
                    src_ref=refs[t].at[2 * px + py], dst_ref=refs[nt + t].at[k - 1], send_sem=send_sems[t][k - 1],
                    recv_sem=recv_sems[t][k - 1], device_id=(px, py, c), device_id_type=MESH))
        return copies

    def start(self):
        landing = [_empty_hbm((N_CHIPS - 1,) + s.shape[1:], s.dtype) for s in self.sums]
        self.s, self.r, self.bufs, token = _split_start(
            self._copies, self.sums + landing, (self.nt, N_CHIPS - 1), None,
            name=f"comm_rs_chip_start_{self.tag}")
        return token

    def finish(self, after):
        bufs = _split_wait(self._copies, self.s, self.r, self.bufs, after, self.nt,
                           name=f"comm_rs_chip_wait_{self.tag}")
        return bufs[:self.nt], bufs[self.nt:]


def _adam_math(g, w, m, v):
    m2 = ADAM_B1 * m + (1.0 - ADAM_B1) * g
    v2 = ADAM_B2 * v + (1.0 - ADAM_B2) * (g * g)
    m_hat = m2 / (1.0 - ADAM_B1 ** ADAM_STEP)
    v_hat = v2 / (1.0 - ADAM_B2 ** ADAM_STEP)
    delta = -ADAM_LR * (m_hat / (jnp.sqrt(v_hat) + ADAM_EPS) + ADAM_WD * w)
    return delta, m2, v2


def _adamw_sharded(chip_sums, remote, chip, w, m, v, layer, prev, deps, *, name):
    nl, r, c = w.shape
    tr = _rows(r, c)

    def body(*refs):
        p_ref, r0_ref, r1_ref, r2_ref, w_ref, m_ref, v_ref = refs[1:8]
        g_out, d_out, m_out, v_out = refs[-4:]
        g = ((p_ref[...].astype(F32) + r0_ref[...].astype(F32)) + r1_ref[...].astype(F32)) + r2_ref[...].astype(F32)
        g_out[...] = g
        d_out[...], m_out[...], v_out[...] = _adam_math(g, w_ref[...], m_ref[...], v_ref[...])

    pspec = pl.BlockSpec((None, tr, c), lambda i, chip_ref: (chip_ref[0], i, 0))

    def rspec(k):
        return pl.BlockSpec((None, tr, c), lambda i, chip_ref: (k, i, 0))

    wspec = pl.BlockSpec((None, tr, c), lambda i, chip_ref: (layer, i, 0))
    in_specs = [pspec, rspec(0), rspec(1), rspec(2), wspec, wspec, wspec]
    args = [chip, chip_sums, remote, remote, remote, w, m, v]
    aliases = {}
    if prev is not None:
        in_specs += [pl.BlockSpec(memory_space=pl.ANY)] * 4
        aliases = {len(args) + i: i for i in range(4)}
        args += list(prev)
    shp = jax.ShapeDtypeStruct(w.shape, F32)
    return _pcall(body, name=name, out_shape=(shp,) * 4, grid=(r // tr,), in_specs=in_specs, out_specs=(wspec,) * 4,
                  aliases=aliases, prefetch=1, deps=deps)(*args)


def _adamw_local(g, w, m, v, *, name):
    nl, r, c = w.shape
    tr = _rows(r, c)

    def body(g_ref, w_ref, m_ref, v_ref, d_out, m_out, v_out):
        d_out[...], m_out[...], v_out[...] = _adam_math(g_ref[...], w_ref[...], m_ref[...], v_ref[...])

    spec = pl.BlockSpec((None, tr, c), lambda l, i: (l, i, 0))
    shp = jax.ShapeDtypeStruct(w.shape, F32)
    return _pcall(body, name=name, out_shape=(shp,) * 3, grid=(nl, r // tr), in_specs=[spec] * 4,
                  out_specs=(spec,) * 3)(g, w, m, v)


def _adamw_replicated(parts, w, m, v, *, name):
    n = w.shape[1]

    def body(p_ref, w_ref, m_ref, v_ref, g_out, d_out, m_out, v_out):
        g = p_ref[0]
        for k in range(1, N_DEV):
            g = g + p_ref[k]
        g_out[...] = g
        d_out[...], m_out[...], v_out[...] = _adam_math(g, w_ref[...], m_ref[...], v_ref[...])

    vm = pl.BlockSpec(memory_space=pltpu.VMEM)
    shp = jax.ShapeDtypeStruct((1, n), F32)
    return _pcall(body, name=name, out_shape=(shp,) * 4, in_specs=[vm] * 4, out_specs=(vm,) * 4)(parts, w, m, v)


def _group_views(qk, proj, g, dil, seq):
    if dil == 1:
        return (qk, qk, proj), (0, A_HEADS, 2 * A_HEADS)
    length = seq // dil
    lo = g * GROUP_W
    q = qk[:, lo:lo + GROUP_W].reshape(length, dil * GROUP_W)
    k = qk[:, A_W + lo:A_W + lo + GROUP_W].reshape(length, dil * GROUP_W)
    v = proj[:, OFF_VA + lo:OFF_VA + lo + GROUP_W].astype(BF16).reshape(length, dil * GROUP_W)
    return (q, k, v), (0, 0, 0)


def _mod_rows(mod, d):
    return [mod[:, i * d:(i + 1) * d] for i in range(6)]


MIXER_W = ("w_in", "w_branch_a", "w_branch_b", "w_out")
FFN_W = ("w_gate_up", "w_down")
SHARD_AXIS = {"w_in": 1, "w_branch_a": 1, "w_branch_b": 1, "w_out": 0, "w_gate_up": 1, "w_down": 0}


def _norm_args(mod, gain, which, d):
    rows = _mod_rows(mod, d)
    return gain, rows[3 * which + 1], rows[3 * which]


def _mixer_fwd_a(h, u, gains, w_in, cos2, sin2):
    seq = h.shape[0]
    proj = _mm(u, w_in, name="mm_in")
    qk = _qkrope_fwd(proj, gains, cos2, sin2, name="qkrope_fwd")
    os_, lses = [], []
    for g, dil in enumerate(DILATIONS):
        arrs, offs = _group_views(qk, proj, g, dil, seq)
        o, lse = _dil_fwd(*arrs, offs, seq // dil, dil, name=f"dil_fwd_{dil}")
        os_.append(o.reshape(seq, GROUP_W))
        lses.append(lse.reshape(seq, GROUP_W))
    o_a = _combine_fwd(os_, lses, name="combine_fwd")
    o_b = _sb_fwd(proj, name="sb_fwd")
    return dict(h_in=h, u=u, proj=proj, qk=qk, os=os_, lses=lses, o_a=o_a, o_b=o_b)


def _mixer_fwd_b(sv, mod, g2, wts):
    d = sv["h_in"].shape[1]
    merged, y_a, y_b = _mm_merge(sv["o_a"], sv["o_b"], wts["w_branch_a"], wts["w_branch_b"], sv["proj"],
                                 name="mm_branch")
    h_mid, t, u2 = _mm_resid_norm(merged, wts["w_out"], sv["h_in"], _mod_rows(mod, d)[2], _norm_args(mod, g2, 1, d),
                                  name="mm_out")
    sv.update(y_a=y_a, y_b=y_b, merged=merged, t=t, h_mid=h_mid, u2=u2)
    return h_mid


def _ffn_fwd_a(sv, w_gate_up):
    a, g, u = _mm_swiglu(sv["u2"], w_gate_up, name="mm_gate_up")
    sv.update(g=g, up=u, a=a)
    return a


def _ffn_fwd_b(sv, mod, w_down, next_norm):
    d = sv["h_mid"].shape[1]
    h_out, sv["f"], u_next = _mm_resid_norm(sv["a"], w_down, sv["h_mid"], _mod_rows(mod, d)[5], next_norm,
                                            name="mm_down")
    return h_out, u_next


def _wgrad(act, dout, key):
    return _mm(act, dout, ta=True, out_dtype=BF16, name="mm_wgrad_" + key)


def _ffn_bwd(dh, sv, mod, g2, wts, deps, hook):
    d = dh.shape[1]
    sc2, ga2 = _mod_rows(mod, d)[4:6]
    df, dgate2 = _resid_gate_bwd(dh, sv["f"], ga2, name="resid_gate_bwd", deps=deps)
    dg, dup = _mm_down_t_swiglu(df, wts["w_down"], sv["g"], sv["up"], name="mm_down_t")
    grads = {"w_down": _wgrad(sv["a"], df, "w_down")}
    hook(dup)
    du2 = _mm_cat_k(dg, dup, wts["w_gate_up"], name="mm_gate_up_t")
    grads["w_gate_up"] = _mm_cat_n(sv["u2"], dg, dup, name="mm_wgrad_w_gate_up")
    dh_mid, dsh2, dsc2, dg2 = _rmsmod_bwd(du2, sv["h_mid"], g2, sc2, dh, name="rmsmod_bwd")
    return dh_mid, [dsh2, dsc2, dgate2], dg2, grads


def _mixer_bwd(dh_mid, sv, mod, g1, gains, wts, cos2, sin2, deps, hook):
    seq, d = dh_mid.shape
    sc1, ga1 = _mod_rows(mod, d)[1:3]
    dt, dgate1 = _resid_gate_bwd(dh_mid, sv["t"], ga1, name="resid_gate_bwd", deps=deps)
    dmerged = _mm(dt, wts["w_out"], tb=True, name="mm_out_t")
    grads = {"w_out": _wgrad(sv["merged"], dt, "w_out")}
    dy_a, dy_b, dga, dgb = _merge_bwd(dmerged, sv["proj"], sv["y_a"], sv["y_b"], name="merge_bwd")
    do_a = _mm(dy_a, wts["w_branch_a"], tb=True, name="mm_branch_t")
    do_b = _mm(dy_b, wts["w_branch_b"], tb=True, name="mm_branch_t")
    grads["w_branch_a"] = _wgrad(sv["o_a"], dy_a, "w_branch_a")
    grads["w_branch_b"] = _wgrad(sv["o_b"], dy_b, "w_branch_b")
    dqb, dkb, dvb = _sb_bwd(sv["proj"], do_b, name="sb_bwd")
    comb = _combine_bwd(do_a, sv["os"], sv["lses"], name="combine_bwd", deps=hook(dqb, grads))
    grads = {}
    dos, dls = comb[:3], comb[3:]
    dqs, dks, dvs = [], [], []
    for g, dil in enumerate(DILATIONS):
        length = seq // dil
        arrs, offs = _group_views(sv["qk"], sv["proj"], g, dil, seq)
        view = (length, dil * GROUP_W)
        dq, dk, dv = _dil_bwd(*arrs, offs, sv["os"][g].reshape(view), sv["lses"][g].reshape(view),
                              dos[g].reshape(view), dls[g].reshape(view), length, dil, name=f"dil_bwd_{dil}")
        dqs.append(dq.reshape(seq, GROUP_W))
        dks.append(dk.reshape(seq, GROUP_W))
        dvs.append(dv.reshape(seq, GROUP_W))
    dqk, dgains = _qkrope_bwd(jnp.concatenate(dqs + dks, axis=1), sv["proj"], gains, cos2, sin2,
                              name="qkrope_bwd")
    dproj = jnp.concatenate(
        [dqk] + [t_.astype(BF16) for t_ in dvs + [dqb, dkb, dvb]] + [dga, dgb], axis=1)
    du = _mm(dproj, wts["w_in"], tb=True, name="mm_in_t")
    grads["w_in"] = _wgrad(sv["u"], dproj, "w_in")
    dh_in, dsh1, dsc1, dg1 = _rmsmod_bwd(du, sv["h_in"], g1, sc1, dh_mid, name="rmsmod_bwd")
    return dh_in, [dsh1, dsc1, dgate1], dg1, dgains, grads


def kernel(x, c, w_ada, b_ada, norm1_g, norm2_g, w_in, qn_g, kn_g, w_branch_a, w_branch_b, w_out, w_gate_up, w_down, loss_target, m_w_ada, m_b_ada, m_norm1_g, m_norm2_g, m_w_in, m_qn_g, m_kn_g, m_w_branch_a, m_w_branch_b, m_w_out, m_w_gate_up, m_w_down, v_w_ada, v_b_ada, v_norm1_g, v_norm2_g, v_w_in, v_qn_g, v_kn_g, v_w_branch_a, v_w_branch_b, v_w_out, v_w_gate_up, v_w_down):
    _ORDER["token"] = None
    seq, d = x.shape[1], x.shape[2]
    depth = w_in.shape[0]
    weights = dict(w_in=w_in, w_branch_a=w_branch_a, w_branch_b=w_branch_b, w_out=w_out, w_gate_up=w_gate_up,
                   w_down=w_down)
    moments_m = dict(w_in=m_w_in, w_branch_a=m_w_branch_a, w_branch_b=m_w_branch_b, w_out=m_w_out,
                     w_gate_up=m_w_gate_up, w_down=m_w_down)
    moments_v = dict(w_in=v_w_in, w_branch_a=v_w_branch_a, w_branch_b=v_w_branch_b, w_out=v_w_out,
                     w_gate_up=v_w_gate_up, w_down=v_w_down)
    xi, yi, ci = _coords()
    me = 4 * xi + 2 * yi + ci
    core = jnp.reshape(ci, (1,)).astype(jnp.int32)
    chip = jnp.reshape(2 * xi + yi, (1,)).astype(jnp.int32)

    ada_w = w_ada.shape[2]
    c_act = _small_allgather(c, name="comm_gather_c", silu=True).reshape(N_DEV, d)
    c_pad = jnp.concatenate([c_act, jnp.zeros_like(c_act)], axis=0).astype(BF16)
    bias = lax.dynamic_slice(b_ada, (0, me * ada_w), (depth, ada_w))
    mod_part = jnp.stack([_mm(c_pad, w_ada[l], name="mm_ada")[:N_DEV] for l in range(depth)]) + bias[:, None, :]
    mod_all = _small_allgather(mod_part.reshape(1, depth * N_DEV * ada_w), name="comm_gather_mod")
    mod_all = mod_all.reshape(N_DEV, depth, N_DEV, ada_w)
    mod_mine = lax.dynamic_index_in_dim(mod_all, me, axis=2, keepdims=False)
    mods = jnp.transpose(mod_mine, (1, 0, 2)).reshape(depth, 1, 6 * d)

    cos2, sin2 = _rope_tables(seq)
    gains = [jnp.stack([qn_g[l], kn_g[l]])[:, None, :] for l in range(depth)]
    g1s = [norm1_g[l][None] for l in range(depth)]
    g2s = [norm2_g[l][None] for l in range(depth)]

    me_arr = jnp.reshape(me, (1,)).astype(jnp.int32)

    def placed(keys, l):
        return [_cast_place(weights[k], l, SHARD_AXIS[k], me_arr, name="cast_place_" + k) for k in keys]

    def gather_of(keys, l, tag):
        return _SplitGather(placed(keys, l), [SHARD_AXIS[k] for k in keys], f"{tag}{l}")

    groups = [("w_in", 0, MIXER_W[:1]), ("rest", 0, MIXER_W[1:]), ("ffn", 0, FFN_W)]
    for l in range(1, depth):
        groups += [("mixer", l, MIXER_W), ("ffn", l, FFN_W)]
    gathers, token = {}, mods
    for tag, l, keys in groups:
        gathers[tag, l] = gather_of(keys, l, tag)
        token = gathers[tag, l].first(after=token)
    h = x[0]
    u = _rmsmod_fwd(h, *_norm_args(mods[0], g1s[0], 0, d), name="rmsmod_fwd")
    token = gathers["w_in", 0].forward(after=u)
    wm = {"w_in": gathers["w_in", 0].finish(after=token)[0]}
    saved, full = [], []
    for l in range(depth):
        last = l + 1 == depth
        sv = _mixer_fwd_a(h, u, gains[l], wm["w_in"], cos2, sin2)
        gathers["ffn", l].forward(after=sv["o_b"])
        if l == 0:
            gathers["rest", 0].forward(after=sv["o_b"])
            wm.update(zip(MIXER_W[1:], gathers["rest", 0].finish(after=sv["o_b"])))
        h_mid = _mixer_fwd_b(sv, mods[l], g2s[l], wm)
        wf = dict(zip(FFN_W, gathers["ffn", l].finish(after=h_mid)))
        a = _ffn_fwd_a(sv, wf["w_gate_up"])
        if not last:
            gathers["mixer", l + 1].forward(after=a)
        h, u = _ffn_fwd_b(sv, mods[l], wf["w_down"],
                          None if last else _norm_args(mods[l + 1], g1s[l + 1], 0, d))
        saved.append(sv)
        full.append({**wm, **wf})
        if not last:
            wm = dict(zip(MIXER_W, gathers["mixer", l + 1].finish(after=h)))
    loss_part, dh = _loss_fwd(h, loss_target[0], name="loss")
    loss = lax.psum(loss_part[0, 0], ("x", "y", "c"))

    pipe = _ReducePipeline(core)
    dmods, dg1s, dg2s, dgains = [None] * depth, [None] * depth, [None] * depth, [None] * depth
    deps = []
    for l in reversed(range(depth)):
        dh_mid, dmod_f, dg2s[l], grads = _ffn_bwd(dh, saved[l], mods[l], g2s[l], full[l], deps, pipe.tick)
        deps = pipe.tick(dh_mid) + pipe.add(FFN_W, grads, l)
        dh, dmod_m, dg1s[l], dgains[l], grads = _mixer_bwd(
            dh_mid, saved[l], mods[l], g1s[l], gains[l], full[l], cos2, sin2, deps,
            lambda after, early, l=l: pipe.tick(after) + pipe.add(MIXER_W[1:], early, l))
        dmods[l] = jnp.concatenate(dmod_m + dmod_f, axis=1)
        deps = pipe.tick(dh) + pipe.add(MIXER_W[:1], grads, l)
    grad_x = dh[None]

    stacked = {}

    def update(deps):
        last = None
        for keys, l, sums, remote in pipe.take_done():
            for k, p_, r_ in zip(keys, sums, remote):
                stacked[k] = _adamw_sharded(p_, r_, chip, weights[k], moments_m[k], moments_v[k], l,
                                            stacked.get(k), deps, name="adamw_" + k)
                deps, last = [], stacked[k][0]
        return last

    after = update(deps)

    small = jnp.concatenate(
        dmods + dg1s + dg2s + [dgains[l][0] for l in range(depth)] + [dgains[l][1] for l in range(depth)], axis=1)
    small_all = _small_allgather(small, name="comm_gather_small", deps=[after])
    update(pipe.tick(small_all))

    def pack(b, n1, n2, qn, kn):
        return jnp.concatenate([t_.reshape(1, -1) for t_ in (b, n1, n2, qn, kn)], axis=1)

    sg, sd, sm, sv_ = _adamw_replicated(small_all, pack(b_ada, norm1_g, norm2_g, qn_g, kn_g),
                                        pack(m_b_ada, m_norm1_g, m_norm2_g, m_qn_g, m_kn_g),
                                        pack(v_b_ada, v_norm1_g, v_norm2_g, v_qn_g, v_kn_g), name="adamw_replicated")

    def unpack(p):
        sizes = [depth * 6 * d, depth * d, depth * d, depth * HEAD_DIM, depth * HEAD_DIM]
        shapes = [b_ada.shape, norm1_g.shape, norm2_g.shape, qn_g.shape, kn_g.shape]
        out, off = [], 0
        for n, shp in zip(sizes, shapes):
            out.append(p[0, off:off + n].reshape(shp))
            off += n
        return dict(zip(("b_ada", "norm1_g", "norm2_g", "qn_g", "kn_g"), out))

    ug, ud, um, uv = unpack(sg), unpack(sd), unpack(sm), unpack(sv_)
    res = {k: dict(g=ug[k], d=ud[k], m=um[k], v=uv[k]) for k in ug}

    dmod_all = small_all[:, 0, :depth * 6 * d].reshape(N_DEV, depth, 6 * d)
    g_ada = None
    for l in range(depth):
        dm = lax.dynamic_slice(dmod_all[:, l, :], (0, me * ada_w), (N_DEV, ada_w))
        dm = jnp.concatenate([dm, jnp.zeros_like(dm)], axis=0).astype(BF16)
        g_ada = _mm(c_pad, dm, ta=True, name="mm_wgrad_ada", stack=(l, depth, g_ada))
    d_ada, m_ada, v_ada = _adamw_local(g_ada, w_ada, m_w_ada, v_w_ada, name="adamw_local")
    res["w_ada"] = dict(g=g_ada, d=d_ada, m=m_ada, v=v_ada)

    after = update(pipe.tick(d_ada))
    update(pipe.tick(d_ada if after is None else after, flush=True))
    for k, (g_, d_, m_, v_) in stacked.items():
        res[k] = dict(g=g_, d=d_, m=m_, v=v_)

    order = ("w_ada", "b_ada", "norm1_g", "norm2_g", "w_in", "qn_g", "kn_g", "w_branch_a", "w_branch_b", "w_out",
             "w_gate_up", "w_down")
    _ORDER["token"] = None
    return (loss, grad_x, *[res[k]["g"] for k in order], *[res[k]["d"] for k in order],
            *[res[k]["m"] for k in order], *[res[k]["v"] for k in order])
```

```python
import functools

import jax
import jax.numpy as jnp
from jax import lax
from jax.experimental import pallas as pl
from jax.experimental.pallas import tpu as pltpu

F32 = jnp.float32
BF16 = jnp.bfloat16

HEAD_DIM = 128
BLOCK = 128
DILATIONS = (1, 4, 16)
HEADS_PER_GROUP = 4
A_HEADS = 12
SB_HEADS = 4
GROUP_W = HEADS_PER_GROUP * HEAD_DIM
A_W = A_HEADS * HEAD_DIM
B_W = SB_HEADS * HEAD_DIM
OFF_QA, OFF_KA, OFF_VA = 0, A_W, 2 * A_W
OFF_QB, OFF_KB, OFF_VB = 3 * A_W, 3 * A_W + B_W, 3 * A_W + 2 * B_W
OFF_GATES = 3 * A_W + 3 * B_W
ROPE_THETA = 10000.0
EPS = 1e-6
ATT_SCALE = HEAD_DIM ** -0.5
MASKED = -1e30

ADAM_LR, ADAM_B1, ADAM_B2, ADAM_EPS, ADAM_WD, ADAM_STEP = 0.001, 0.9, 0.999, 1e-08, 0.01, 10

N_DEV = 8
N_CHIPS = 4
V7X_VMEM_LIMIT_BYTES = 56 * 1024 * 1024
ELEMWISE_BLOCK_BYTES = 2 * 1024 * 1024
MESH = pl.DeviceIdType.MESH

NN = (((1,), (0,)), ((), ()))
NT = (((1,), (1,)), ((), ()))
TN = (((0,), (0,)), ((), ()))


def _dot(a, b, dims=NN):
    return lax.dot_general(a, b, dims, preferred_element_type=F32)


def _tile(n, cap, mult=128):
    best = None
    for t in range(mult, min(n, cap) + 1, mult):
        if n % t == 0:
            best = t
    if best is None:
        assert n <= 2 * cap, (n, cap)
        return n
    return best


def _rows(r, c):
    return _tile(r, max(16, ELEMWISE_BLOCK_BYTES // (4 * c)), 16)


_ORDER = {"token": None}
TOKEN = jax.ShapeDtypeStruct((8, 128), F32)


def _take_token():
    prev = _ORDER["token"]
    return [] if prev is None else [prev]


def _pcall(body, *, name, out_shape, grid=None, in_specs=None, out_specs=None, scratch=(), aliases=None,
           prefetch=0, deps=()):
    single = not isinstance(out_shape, (tuple, list))
    out_shapes = [out_shape] if single else list(out_shape)
    out_specs = [out_specs] if single else list(out_specs)
    extra = list(deps) + _take_token()
    n_in, n_extra, n_out = prefetch + len(in_specs), len(extra), len(out_shapes)

    def wrapped(*refs):
        token = refs[n_in + n_extra + n_out]
        token[...] = jnp.zeros_like(token)
        return body(*refs[:n_in], *refs[n_in + n_extra:n_in + n_extra + n_out], *refs[n_in + n_extra + n_out + 1:])

    in_specs = list(in_specs) + [pl.BlockSpec(memory_space=pl.ANY)] * n_extra
    if grid is None:
        out_specs.append(pl.BlockSpec(memory_space=pltpu.VMEM))
    else:
        out_specs.append(pl.BlockSpec(TOKEN.shape, lambda *_: (0, 0)))
    kwargs = dict(name=name, out_shape=out_shapes + [TOKEN], input_output_aliases=aliases or {},
                  compiler_params=pltpu.CompilerParams(vmem_limit_bytes=V7X_VMEM_LIMIT_BYTES))
    if prefetch:
        call = pl.pallas_call(wrapped, grid_spec=pltpu.PrefetchScalarGridSpec(
            num_scalar_prefetch=prefetch, grid=grid, in_specs=in_specs, out_specs=out_specs,
            scratch_shapes=list(scratch)), **kwargs)
    else:
        if grid is not None:
            kwargs["grid"] = grid
        call = pl.pallas_call(wrapped, in_specs=in_specs, out_specs=out_specs, scratch_shapes=list(scratch), **kwargs)

    def run(*args):
        outs = call(*args, *extra)
        _ORDER["token"] = outs[-1]
        return outs[0] if single else tuple(outs[:-1])

    return run


def _mm(a, b, *, name, ta=False, tb=False, out_dtype=F32, caps=(1024, 1024, 3072), stack=None, deps=()):
    kdim, m = a.shape if ta else a.shape[::-1]
    n, k2 = b.shape if tb else b.shape[::-1]
    assert kdim == k2, (a.shape, b.shape, ta, tb)
    tm, tn, tk = _tile(m, caps[0]), _tile(n, caps[1]), _tile(kdim, caps[2])
    nk = kdim // tk
    dims = (((0 if ta else 1,), (1 if tb else 0,)), ((), ()))

    def body(*refs):
        a_ref, b_ref = refs[0], refs[1]
        part = _dot(a_ref[...].astype(BF16), b_ref[...].astype(BF16), dims)
        if nk == 1:
            o_ref = refs[-1]
            o_ref[...] = part.astype(o_ref.dtype)
            return
        o_ref, acc_ref = refs[-2], refs[-1]
        k = pl.program_id(2)

        @pl.when(k == 0)
        def _():
            acc_ref[...] = part

        @pl.when(k > 0)
        def _():
            acc_ref[...] += part

        @pl.when(k == nk - 1)
        def _():
            o_ref[...] = acc_ref[...].astype(o_ref.dtype)

    a_spec = (pl.BlockSpec((tk, tm), lambda i, j, k: (k, i)) if ta
              else pl.BlockSpec((tm, tk), lambda i, j, k: (i, k)))
    b_spec = (pl.BlockSpec((tn, tk), lambda i, j, k: (j, k)) if tb
              else pl.BlockSpec((tk, tn), lambda i, j, k: (k, j)))
    ins, in_specs, aliases = [a, b], [a_spec, b_spec], {}
    if stack is None:
        out_shape = jax.ShapeDtypeStruct((m, n), out_dtype)
        out_spec = pl.BlockSpec((tm, tn), lambda i, j, k: (i, j))
    else:
        layer, n_layers, buf = stack
        out_shape = jax.ShapeDtypeStruct((n_layers, m, n), out_dtype)
        out_spec = pl.BlockSpec((None, tm, tn), lambda i, j, k: (layer, i, j))
        if buf is not None:
            ins.append(buf)
            in_specs.append(pl.BlockSpec(memory_space=pl.ANY))
            aliases = {2: 0}
    scratch = [] if nk == 1 else [pltpu.VMEM((tm, tn), F32)]
    return _pcall(body, name=name, out_shape=out_shape, grid=(m // tm, n // tn, nk), in_specs=in_specs,
                  out_specs=out_spec, scratch=scratch, aliases=aliases, deps=deps)(*ins)


def _mm_cat_k(a_lo, a_hi, b, *, name):
    m, f = a_lo.shape
    n = b.shape[0]
    tm, tn, tk = _tile(m, 1024), _tile(n, 1024), _tile(f, 3072)
    half = f // tk
    nk = 2 * half

    def body(lo_ref, hi_ref, b_ref, o_ref, acc_ref):
        k = pl.program_id(2)

        def accumulate(a_ref):
            part = _dot(a_ref[...], b_ref[...], NT)

            @pl.when(k == 0)
            def _():
                acc_ref[...] = part

            @pl.when(k > 0)
            def _():
                acc_ref[...] += part

        pl.when(k < half)(lambda: accumulate(lo_ref))
        pl.when(k >= half)(lambda: accumulate(hi_ref))

        @pl.when(k == nk - 1)
        def _():
            o_ref[...] = acc_ref[...]

    return _pcall(body, name=name, out_shape=jax.ShapeDtypeStruct((m, n), F32), grid=(m // tm, n // tn, nk),
                  in_specs=[pl.BlockSpec((tm, tk), lambda i, j, k: (i, jnp.minimum(k, half - 1))),
                            pl.BlockSpec((tm, tk), lambda i, j, k: (i, jnp.maximum(k - half, 0))),
                            pl.BlockSpec((tn, tk), lambda i, j, k: (j, k))],
                  out_specs=pl.BlockSpec((tm, tn), lambda i, j, k: (i, j)),
                  scratch=[pltpu.VMEM((tm, tn), F32)])(a_lo, a_hi, b)


def _mm_cat_n(a, b_lo, b_hi, *, name):
    s, m = a.shape
    f = b_lo.shape[1]
    tm, tn = _tile(m, 1024), _tile(f, 1024)
    half = f // tn

    def body(a_ref, lo_ref, hi_ref, o_ref):
        j = pl.program_id(1)

        @pl.when(j < half)
        def _():
            o_ref[...] = _dot(a_ref[...], lo_ref[...], TN).astype(BF16)

        @pl.when(j >= half)
        def _():
            o_ref[...] = _dot(a_ref[...], hi_ref[...], TN).astype(BF16)

    return _pcall(body, name=name, out_shape=jax.ShapeDtypeStruct((m, 2 * f), BF16), grid=(m // tm, 2 * half),
                  in_specs=[pl.BlockSpec((s, tm), lambda i, j: (0, i)),
                            pl.BlockSpec((s, tn), lambda i, j: (0, jnp.minimum(j, half - 1))),
                            pl.BlockSpec((s, tn), lambda i, j: (0, jnp.maximum(j - half, 0)))],
                  out_specs=pl.BlockSpec((tm, tn), lambda i, j: (i, j)))(a, b_lo, b_hi)


def _mm_resid_norm(a, w, h, gate, norm, *, name):
    s, kdim = a.shape
    d = w.shape[1]
    tk = _tile(kdim, 2048)
    nk = kdim // tk
    tm = _tile(s, 256 if nk == 1 else 512)

    def body(*refs):
        a_ref, w_ref, h_ref, gate_ref = refs[:4]
        outs = refs[7:] if norm is not None else refs[4:]
        part = _dot(a_ref[...], w_ref[...])

        def finish(t):
            hn = h_ref[...] + gate_ref[...] * t
            outs[0][...] = hn
            outs[1][...] = t.astype(BF16)
            if norm is not None:
                g_ref, sc_ref, sh_ref = refs[4:7]
                r = lax.rsqrt(jnp.mean(hn * hn, axis=-1, keepdims=True) + EPS)
                outs[2][...] = (((hn * r) * g_ref[...]) * (1.0 + sc_ref[...]) + sh_ref[...]).astype(BF16)

        if nk == 1:
            finish(part)
            return
        acc_ref = refs[-1]
        k = pl.program_id(1)

        @pl.when(k == 0)
        def _():
            acc_ref[...] = part

        @pl.when(jnp.logical_and(k > 0, k < nk - 1))
        def _():
            acc_ref[...] += part

        @pl.when(k == nk - 1)
        def _():
            finish(acc_ref[...] + part)

    row = pl.BlockSpec((tm, d), lambda i, k: (i, 0))
    vec = pl.BlockSpec((1, d), lambda i, k: (0, 0))
    in_specs = [pl.BlockSpec((tm, tk), lambda i, k: (i, k)), pl.BlockSpec((tk, d), lambda i, k: (k, 0)), row, vec]
    args = [a, w, h, gate]
    out_shape = [jax.ShapeDtypeStruct((s, d), F32), jax.ShapeDtypeStruct((s, d), BF16)]
    if norm is not None:
        in_specs += [vec, vec, vec]
        args += list(norm)
        out_shape.append(jax.ShapeDtypeStruct((s, d), BF16))
    outs = _pcall(body, name=name, out_shape=tuple(out_shape), grid=(s // tm, nk), in_specs=in_specs,
                  out_specs=(row,) * len(out_shape), scratch=[] if nk == 1 else [pltpu.VMEM((tm, d), F32)])(*args)
    return outs if norm is not None else (*outs, None)


def _mm_merge(o_a, o_b, w_a, w_b, proj, *, name):
    s = o_a.shape[0]
    d = w_a.shape[1]
    tm = _tile(s, 512)
    ga_blk = OFF_GATES // d

    def body(oa_ref, ob_ref, wa_ref, wb_ref, ga_ref, gb_ref, m_ref, ya_ref, yb_ref):
        ya, yb = _dot(oa_ref[...], wa_ref[...]), _dot(ob_ref[...], wb_ref[...])
        m_ref[...] = (jax.nn.sigmoid(ga_ref[...]) * ya + jax.nn.sigmoid(gb_ref[...]) * yb).astype(BF16)
        ya_ref[...] = ya.astype(BF16)
        yb_ref[...] = yb.astype(BF16)

    row = pl.BlockSpec((tm, d), lambda i: (i, 0))
    act = pl.BlockSpec((tm, o_a.shape[1]), lambda i: (i, 0))
    wspec = pl.BlockSpec(w_a.shape, lambda i: (0, 0))
    shp = jax.ShapeDtypeStruct((s, d), BF16)
    return _pcall(body, name=name, out_shape=(shp, shp, shp), grid=(s // tm,),
                  in_specs=[act, act, wspec, wspec, pl.BlockSpec((tm, d), lambda i: (i, ga_blk)),
                            pl.BlockSpec((tm, d), lambda i: (i, ga_blk + 1))],
                  out_specs=(row, row, row))(o_a, o_b, w_a, w_b, proj, proj)


def _mm_down_t_swiglu(df, w_down, g, u, *, name):
    s, d = df.shape
    f = w_down.shape[0]
    tm, tn = _tile(s, 1024), _tile(f, 512)

    def body(df_ref, w_ref, g_ref, u_ref, dg_ref, du_ref):
        da = _dot(df_ref[...], w_ref[...], NT)
        gf = g_ref[...].astype(F32)
        sg = jax.nn.sigmoid(gf)
        dg_ref[...] = (da * u_ref[...].astype(F32) * (sg * (1.0 + gf * (1.0 - sg)))).astype(BF16)
        du_ref[...] = (da * (gf * sg)).astype(BF16)

    tile = pl.BlockSpec((tm, tn), lambda i, j: (i, j))
    shp = jax.ShapeDtypeStruct((s, f), BF16)
    return _pcall(body, name=name, out_shape=(shp, shp), grid=(s // tm, f // tn),
                  in_specs=[pl.BlockSpec((tm, d), lambda i, j: (i, 0)), pl.BlockSpec((tn, d), lambda i, j: (j, 0)),
                            tile, tile],
                  out_specs=(tile, tile))(df, w_down, g, u)


def _rmsmod_fwd(h, g, scale, shift, *, name, deps=()):
    s, d = h.shape
    ts = _rows(s, d)

    def body(h_ref, g_ref, sc_ref, sh_ref, u_ref):
        hf = h_ref[...]
        r = lax.rsqrt(jnp.mean(hf * hf, axis=-1, keepdims=True) + EPS)
        u_ref[...] = (((hf * r) * g_ref[...]) * (1.0 + sc_ref[...]) + sh_ref[...]).astype(BF16)

    row = pl.BlockSpec((ts, d), lambda i: (i, 0))
    vec = pl.BlockSpec((1, d), lambda i: (0, 0))
    return _pcall(body, name=name, out_shape=jax.ShapeDtypeStruct((s, d), BF16), grid=(s // ts,),
                  in_specs=[row, vec, vec, vec], out_specs=row, deps=deps)(h, g, scale, shift)


def _rmsmod_bwd(du, h, g, scale, dres, *, name):
    s, d = h.shape
    ts = _rows(s, d)

    def body(du_ref, h_ref, g_ref, sc_ref, dres_ref, dh_ref, dsh_ref, dsc_ref, dg_ref):
        @pl.when(pl.program_id(0) == 0)
        def _():
            dsh_ref[...] = jnp.zeros_like(dsh_ref)
            dsc_ref[...] = jnp.zeros_like(dsc_ref)
            dg_ref[...] = jnp.zeros_like(dg_ref)

        hf, duf, gain = h_ref[...], du_ref[...], g_ref[...]
        r = lax.rsqrt(jnp.mean(hf * hf, axis=-1, keepdims=True) + EPS)
        xh = hf * r
        dn = duf * (1.0 + sc_ref[...])
        dsh_ref[...] += jnp.sum(duf, axis=0, keepdims=True)
        dsc_ref[...] += jnp.sum(duf * (xh * gain), axis=0, keepdims=True)
        dg_ref[...] += jnp.sum(dn * xh, axis=0, keepdims=True)
        dxh = dn * gain
        dh_ref[...] = dres_ref[...] + r * (dxh - xh * jnp.mean(dxh * xh, axis=-1, keepdims=True))

    row = pl.BlockSpec((ts, d), lambda i: (i, 0))
    vec = pl.BlockSpec((1, d), lambda i: (0, 0))
    vshape = jax.ShapeDtypeStruct((1, d), F32)
    return _pcall(body, name=name, out_shape=(jax.ShapeDtypeStruct((s, d), F32), vshape, vshape, vshape),
                  grid=(s // ts,), in_specs=[row, row, vec, vec, row],
                  out_specs=(row, vec, vec, vec))(du, h, g, scale, dres)


def _resid_gate_bwd(dh, t, gate, *, name, deps=()):
    s, d = dh.shape
    ts = _rows(s, d)

    def body(dh_ref, t_ref, g_ref, dt_ref, dg_ref):
        @pl.when(pl.program_id(0) == 0)
        def _():
            dg_ref[...] = jnp.zeros_like(dg_ref)

        dhf = dh_ref[...]
        dt_ref[...] = (dhf * g_ref[...]).astype(BF16)
        dg_ref[...] += jnp.sum(dhf * t_ref[...], axis=0, keepdims=True)

    row = pl.BlockSpec((ts, d), lambda i: (i, 0))
    vec = pl.BlockSpec((1, d), lambda i: (0, 0))
    return _pcall(body, name=name,
                  out_shape=(jax.ShapeDtypeStruct((s, d), BF16), jax.ShapeDtypeStruct((1, d), F32)),
                  grid=(s // ts,), in_specs=[row, row, vec], out_specs=(row, vec), deps=deps)(dh, t, gate)


def _merge_bwd(dm, proj, y_a, y_b, *, name):
    s, d = y_a.shape
    ts = _rows(s, d)
    ga_blk = OFF_GATES // d

    def body(dm_ref, ga_ref, gb_ref, ya_ref, yb_ref, dya_ref, dyb_ref, dga_ref, dgb_ref):
        dmf = dm_ref[...]
        sa, sb = jax.nn.sigmoid(ga_ref[...]), jax.nn.sigmoid(gb_ref[...])
        dya_ref[...] = (dmf * sa).astype(BF16)
        dyb_ref[...] = (dmf * sb).astype(BF16)
        dga_ref[...] = (dmf * ya_ref[...] * (sa * (1.0 - sa))).astype(BF16)
        dgb_ref[...] = (dmf * yb_ref[...] * (sb * (1.0 - sb))).astype(BF16)

    row = pl.BlockSpec((ts, d), lambda i: (i, 0))
    ga = pl.BlockSpec((ts, d), lambda i: (i, ga_blk))
    gb = pl.BlockSpec((ts, d), lambda i: (i, ga_blk + 1))
    shp = jax.ShapeDtypeStruct((s, d), BF16)
    return _pcall(body, name=name, out_shape=(shp, shp, shp, shp), grid=(s // ts,),
                  in_specs=[row, ga, gb, row, row], out_specs=(row, row, row, row))(dm, proj, proj, y_a, y_b)


def _mm_swiglu(u2, w_gate_up, *, name):
    s, d = u2.shape
    f = w_gate_up.shape[1] // 2
    tm, tn = _tile(s, 1024), _tile(f, 512)
    nj = f // tn

    def body(x_ref, wg_ref, wu_ref, a_ref, g_ref, u_ref):
        x = x_ref[...]
        gf, uf = _dot(x, wg_ref[...]), _dot(x, wu_ref[...])
        a_ref[...] = ((gf * jax.nn.sigmoid(gf)) * uf).astype(BF16)
        g_ref[...] = gf.astype(BF16)
        u_ref[...] = uf.astype(BF16)

    out = pl.BlockSpec((tm, tn), lambda i, j: (i, j))
    shp = jax.ShapeDtypeStruct((s, f), BF16)
    return _pcall(body, name=name, out_shape=(shp, shp, shp), grid=(s // tm, nj),
                  in_specs=[pl.BlockSpec((tm, d), lambda i, j: (i, 0)), pl.BlockSpec((d, tn), lambda i, j: (0, j)),
                            pl.BlockSpec((d, tn), lambda i, j: (0, nj + j))],
                  out_specs=(out, out, out))(u2, w_gate_up, w_gate_up)


def _loss_fwd(y, tgt, *, name):
    s, d = y.shape
    ts = _rows(s, d)

    def body(y_ref, t_ref, l_ref, dy_ref):
        @pl.when(pl.program_id(0) == 0)
        def _():
            l_ref[...] = jnp.zeros_like(l_ref)

        e = y_ref[...] - t_ref[...]
        dy_ref[...] = e * (1.0 / d)
        per_tok = jnp.sum(e * e, axis=1, keepdims=True) * (1.0 / d)
        l_ref[...] += 0.5 * jnp.sum(per_tok, axis=0, keepdims=True)

    row = pl.BlockSpec((ts, d), lambda i: (i, 0))
    return _pcall(body, name=name,
                  out_shape=(jax.ShapeDtypeStruct((1, 128), F32), jax.ShapeDtypeStruct((s, d), F32)),
                  grid=(s // ts,), in_specs=[row, row],
                  out_specs=(pl.BlockSpec((1, 128), lambda i: (0, 0)), row))(y, tgt)


def _rope_tables(seq):
    inv = jnp.power(ROPE_THETA, -jnp.arange(0, HEAD_DIM, 2, dtype=F32) / HEAD_DIM)
    ang = jnp.arange(seq, dtype=F32)[:, None] * inv[None, :]
    cos, sin = jnp.cos(ang), jnp.sin(ang)
    return jnp.concatenate([cos, cos], axis=1), jnp.concatenate([-sin, sin], axis=1)


def _qkrope_fwd(proj, gains, cos2, sin2, *, name):
    s = proj.shape[0]
    ts = _rows(s, A_W)

    def body(x_ref, g_ref, c_ref, s_ref, o_ref):
        gain, cos, sin = g_ref[...], c_ref[...], s_ref[...]
        for h in range(A_HEADS):
            lanes = slice(h * HEAD_DIM, (h + 1) * HEAD_DIM)
            x = x_ref[:, lanes]
            y = (x * lax.rsqrt(jnp.mean(x * x, axis=-1, keepdims=True) + EPS)) * gain
            o_ref[:, lanes] = (y * cos + pltpu.roll(y, HEAD_DIM // 2, 1) * sin).astype(BF16)

    heads = pl.BlockSpec((ts, A_W), lambda i, j: (i, j))
    tab = pl.BlockSpec((ts, HEAD_DIM), lambda i, j: (i, 0))
    gain = pl.BlockSpec((None, 1, HEAD_DIM), lambda i, j: (j, 0, 0))
    return _pcall(body, name=name, out_shape=jax.ShapeDtypeStruct((s, 2 * A_W), BF16),
                  grid=(s // ts, 2), in_specs=[heads, gain, tab, tab], out_specs=heads)(
                      proj, gains, cos2, sin2)


def _qkrope_bwd(d_groups, proj, gains, which, cos2, sin2, *, name):
    s = proj.shape[0]
    ts = _rows(s, A_W)

    def body(d0_ref, d1_ref, d2_ref, x_ref, g_ref, c_ref, s_ref, dx_ref, dg_ref):
        @pl.when(pl.program_id(0) == 0)
        def _():
            dg_ref[...] = jnp.zeros_like(dg_ref)

        gain, cos, sin = g_ref[...], c_ref[...], s_ref[...]
        dg = jnp.zeros((1, HEAD_DIM), F32)
        for h in range(A_HEADS):
            lanes = slice(h * HEAD_DIM, (h + 1) * HEAD_DIM)
            slot = slice((h % HEADS_PER_GROUP) * HEAD_DIM, (h % HEADS_PER_GROUP + 1) * HEAD_DIM)
            dout = (d0_ref, d1_ref, d2_ref)[h // HEADS_PER_GROUP][:, slot]
            dy = dout * cos + pltpu.roll(dout * sin, HEAD_DIM // 2, 1)
            x = x_ref[:, lanes]
            r = lax.rsqrt(jnp.mean(x * x, axis=-1, keepdims=True) + EPS)
            xh = x * r
            dg = dg + jnp.sum(dy * xh, axis=0, keepdims=True)
            dxh = dy * gain
            dx_ref[:, lanes] = (r * (dxh - xh * jnp.mean(dxh * xh, axis=-1, keepdims=True))).astype(BF16)
        dg_ref[...] += dg

    group = pl.BlockSpec((ts, GROUP_W), lambda i: (i, 0))
    tab = pl.BlockSpec((ts, HEAD_DIM), lambda i: (i, 0))
    gain = pl.BlockSpec((None, 1, HEAD_DIM), lambda i: (which, 0, 0))
    return _pcall(body, name=name,
                  out_shape=(jax.ShapeDtypeStruct((s, A_W), BF16), jax.ShapeDtypeStruct((1, HEAD_DIM), F32)),
                  grid=(s // ts,),
                  in_specs=[group, group, group, pl.BlockSpec((ts, A_W), lambda i: (i, which)), gain, tab, tab],
                  out_specs=(pl.BlockSpec((ts, A_W), lambda i: (i, 0)), pl.BlockSpec((1, HEAD_DIM), lambda i: (0, 0))))(
                      *d_groups, proj, gains, cos2, sin2)


def _assemble(pieces, *, name):
    s = pieces[0].shape[0]
    widths = [p.shape[1] for p in pieces]
    total = sum(widths)
    ts = _rows(s, total)

    def body(*refs):
        o_ref, off = refs[-1], 0
        for x_ref, w in zip(refs[:-1], widths):
            o_ref[:, off:off + w] = x_ref[...].astype(BF16)
            off += w

    return _pcall(body, name=name, out_shape=jax.ShapeDtypeStruct((s, total), BF16), grid=(s // ts,),
                  in_specs=[pl.BlockSpec((ts, w), lambda i: (i, 0)) for w in widths],
                  out_specs=pl.BlockSpec((ts, total), lambda i: (i, 0)))(*pieces)


def _block_rows(blk):
    if isinstance(blk, int):
        return pl.ds(blk * BLOCK, BLOCK)
    return pl.ds(pl.multiple_of(blk * BLOCK, BLOCK), BLOCK)


def _band_masks(n, with_prev):
    row = lax.broadcasted_iota(jnp.int32, (BLOCK, BLOCK), 0)
    col = lax.broadcasted_iota(jnp.int32, (BLOCK, BLOCK), 1)
    cur = col <= row
    if not with_prev:
        return [(n, cur)]
    prev = col >= row + jnp.where(n >= 1, 0, BLOCK)
    return [(n, cur), (jnp.maximum(n - 1, 0), prev)]


def _dil_fwd(q_arr, k_arr, v_arr, offs, length, dil, *, name):
    nj, nb = dil * HEADS_PER_GROUP, length // BLOCK
    ju = HEADS_PER_GROUP
    qo, ko, vo = (off // ju for off in offs)
    assert all(off % ju == 0 for off in offs)

    def body(q_ref, k_ref, v_ref, o_ref, l_ref):
        n = pl.program_id(1)
        masks = _band_masks(n, nb > 1)
        for cb in range(ju):
            lanes = slice(cb * HEAD_DIM, (cb + 1) * HEAD_DIM)
            q = q_ref[:, lanes].astype(BF16)
            parts = []
            for blk, mask in masks:
                rows = _block_rows(blk)
                sc = _dot(q, k_ref[rows, lanes].astype(BF16), NT) * ATT_SCALE
                parts.append((jnp.where(mask, sc, MASKED), rows))
            m = parts[0][0].max(axis=-1, keepdims=True)
            for sc, _ in parts[1:]:
                m = jnp.maximum(m, sc.max(axis=-1, keepdims=True))
            den = jnp.zeros((BLOCK, 1), F32)
            acc = jnp.zeros((BLOCK, HEAD_DIM), F32)
            for sc, rows in parts:
                p = jnp.exp(sc - m)
                den = den + jnp.sum(p, axis=-1, keepdims=True)
                acc = acc + _dot(p.astype(BF16), v_ref[rows, lanes].astype(BF16))
            o_ref[:, lanes] = acc / den
            l_ref[:, lanes] = jnp.broadcast_to(m + jnp.log(den), (BLOCK, HEAD_DIM))

    qspec = pl.BlockSpec((BLOCK, ju * HEAD_DIM), lambda j, n: (n, qo + j))
    kspec = pl.BlockSpec((length, ju * HEAD_DIM), lambda j, n: (0, ko + j))
    vspec = pl.BlockSpec((length, ju * HEAD_DIM), lambda j, n: (0, vo + j))
    ospec = pl.BlockSpec((BLOCK, ju * HEAD_DIM), lambda j, n: (n, j))
    shp = jax.ShapeDtypeStruct((length, nj * HEAD_DIM), F32)
    return _pcall(body, name=name, out_shape=(shp, shp), grid=(nj // ju, nb), in_specs=[qspec, kspec, vspec],
                  out_specs=(ospec, ospec))(q_arr, k_arr, v_arr)


def _dil_bwd(q_arr, k_arr, v_arr, offs, o, lse, do, dlse, length, dil, *, name):
    nj, nb = dil * HEADS_PER_GROUP, length // BLOCK
    ju = HEADS_PER_GROUP if length <= 4 * BLOCK else 2
    qo, ko, vo = (off // ju for off in offs)
    assert all(off % ju == 0 for off in offs)

    def body(q_ref, k_ref, v_ref, o_ref, l_ref, do_ref, dl_ref, dq_ref, dk_ref, dv_ref):
        dk_ref[...] = jnp.zeros_like(dk_ref)
        dv_ref[...] = jnp.zeros_like(dv_ref)

        def step(n, carry):
            qrows = _block_rows(n)
            masks = _band_masks(n, nb > 1)
            for cb in range(ju):
                lanes = slice(cb * HEAD_DIM, (cb + 1) * HEAD_DIM)
                q = q_ref[qrows, lanes].astype(BF16)
                dof = do_ref[qrows, lanes]
                dob = dof.astype(BF16)
                lse_b = l_ref[qrows, lanes]
                shift = dl_ref[qrows, lanes] - jnp.sum(dof * o_ref[qrows, lanes], axis=-1, keepdims=True)
                dq = jnp.zeros((BLOCK, HEAD_DIM), F32)
                for blk, mask in masks:
                    rows = _block_rows(blk)
                    kk, vv = k_ref[rows, lanes].astype(BF16), v_ref[rows, lanes].astype(BF16)
                    sc = _dot(q, kk, NT) * ATT_SCALE
                    p = jnp.where(mask, jnp.exp(sc - lse_b), 0.0)
                    ds = (p * (_dot(dob, vv, NT) + shift)).astype(BF16)
                    dq = dq + _dot(ds, kk)
                    dk_ref[rows, lanes] += _dot(ds, q, TN) * ATT_SCALE
                    dv_ref[rows, lanes] += _dot(p.astype(BF16), dob, TN)
                dq_ref[qrows, lanes] = dq * ATT_SCALE
            return carry

        if nb == 1:
            step(0, 0)
        else:
            lax.fori_loop(0, nb, step, 0)

    def col(off):
        return pl.BlockSpec((length, ju * HEAD_DIM), lambda j: (0, off + j))

    shp = jax.ShapeDtypeStruct((length, nj * HEAD_DIM), F32)
    return _pcall(body, name=name, out_shape=(shp, shp, shp), grid=(nj // ju,),
                  in_specs=[col(qo), col(ko), col(vo), col(0), col(0), col(0), col(0)],
                  out_specs=(col(0), col(0), col(0)))(q_arr, k_arr, v_arr, o, lse, do, dlse)


def _combine_weights(l_refs):
    ls = [r[...] for r in l_refs]
    m = jnp.maximum(jnp.maximum(ls[0], ls[1]), ls[2])
    es = [jnp.exp(l - m) for l in ls]
    den = es[0] + es[1] + es[2]
    return [e / den for e in es]


def _combine_fwd(os_, lses, *, name):
    s = os_[0].shape[0]
    ts = _rows(s, GROUP_W)

    def body(o0, o1, o2, l0, l1, l2, out_ref):
        w = _combine_weights((l0, l1, l2))
        out_ref[...] = (w[0] * o0[...] + w[1] * o1[...] + w[2] * o2[...]).astype(BF16)

    row = pl.BlockSpec((ts, GROUP_W), lambda i: (i, 0))
    return _pcall(body, name=name, out_shape=jax.ShapeDtypeStruct((s, GROUP_W), BF16), grid=(s // ts,),
                  in_specs=[row] * 6, out_specs=row)(*os_, *lses)


def _combine_bwd(do_a, os_, lses, *, name, deps=()):
    s = do_a.shape[0]
    ts = _rows(s, GROUP_W)

    def body(d_ref, o0, o1, o2, l0, l1, l2, do0, do1, do2, dl0, dl1, dl2):
        w = _combine_weights((l0, l1, l2))
        d = d_ref[...]
        og = [o0[...], o1[...], o2[...]]
        oa = w[0] * og[0] + w[1] * og[1] + w[2] * og[2]
        ta = jnp.sum(d * oa, axis=-1, keepdims=True)
        for g, (do_ref, dl_ref) in enumerate(((do0, dl0), (do1, dl1), (do2, dl2))):
            do_ref[...] = w[g] * d
            dl_ref[...] = w[g] * (jnp.sum(d * og[g], axis=-1, keepdims=True) - ta)

    head = pl.BlockSpec((ts, HEAD_DIM), lambda i, h: (i, h))
    shp = jax.ShapeDtypeStruct((s, GROUP_W), F32)
    return _pcall(body, name=name, out_shape=(shp,) * 6, grid=(s // ts, HEADS_PER_GROUP),
                  in_specs=[head] * 7, out_specs=(head,) * 6, deps=deps)(do_a, *os_, *lses)


def _dot_exact(x, ones_mask):
    hi = x.astype(BF16)
    r1 = x - hi.astype(F32)
    mid = r1.astype(BF16)
    lo = (r1 - mid.astype(F32)).astype(BF16)
    return _dot(hi, ones_mask) + _dot(mid, ones_mask) + _dot(lo, ones_mask)


SB_QROWS = 2 * BLOCK
SB_UNROLL = 4


def _sb_mask(j, i):
    row = lax.broadcasted_iota(jnp.int32, (SB_QROWS, BLOCK), 0)
    col = lax.broadcasted_iota(jnp.int32, (SB_QROWS, BLOCK), 1)
    return col + (j * BLOCK - i * SB_QROWS) < row


def _sb_steps(i):
    return ((i + 1) * (SB_QROWS // BLOCK) + SB_UNROLL - 1) // SB_UNROLL


def _sb_scores(q, kk, j, i, masked):
    mask = _sb_mask(j, i) if masked else None
    z = _dot(q, kk, NT) * ATT_SCALE
    sp = jnp.log(1.0 + jnp.exp(-jnp.abs(z)))
    log_beta = jnp.minimum(z, 0.0) - sp
    log_1mb = jnp.minimum(-z, 0.0) - sp
    if masked:
        log_1mb = jnp.where(mask, log_1mb, 0.0)
    return z, log_beta, log_1mb, mask


def _sb_weights(log_beta, log_1mb, mask, run, upper):
    a = jnp.exp(log_beta + (run + _dot_exact(log_1mb, upper)))
    return a if mask is None else jnp.where(mask, a, 0.0)


def _sb_peeled(nsteps, make_step, init, masked_first):
    if masked_first:
        return lax.fori_loop(1, nsteps, make_step(False), make_step(True)(0, init))
    return make_step(True)(nsteps - 1, lax.fori_loop(0, nsteps - 1, make_step(False), init))


def _tri(strict_lower):
    row = lax.broadcasted_iota(jnp.int32, (BLOCK, BLOCK), 0)
    col = lax.broadcasted_iota(jnp.int32, (BLOCK, BLOCK), 1)
    return ((row > col) if strict_lower else (row < col)).astype(BF16)


def _sb_fwd(proj, *, name):
    s = proj.shape[0]
    assert s % (BLOCK * SB_UNROLL) == 0 and s % SB_QROWS == 0
    qb, kb, vb = OFF_QB // HEAD_DIM, OFF_KB // HEAD_DIM, OFF_VB // HEAD_DIM

    def body(q_ref, k_ref, v_ref, o_ref):
        i = pl.program_id(1)
        q = q_ref[...].astype(BF16)
        upper = _tri(True)
        nsteps = _sb_steps(i)

        def make_step(masked):
            def step(t, carry):
                acc, run = carry
                for b in reversed(range(SB_UNROLL)):
                    j = (nsteps - 1 - t) * SB_UNROLL + b
                    rows = _block_rows(j)
                    _, log_beta, log_1mb, mask = _sb_scores(q, k_ref[rows, :].astype(BF16), j, i, masked)
                    a = _sb_weights(log_beta, log_1mb, mask, run, upper)
                    acc = acc + _dot(a.astype(BF16), v_ref[rows, :].astype(BF16))
                    run = run + jnp.sum(log_1mb, axis=-1, keepdims=True)
                return acc, run
            return step

        acc, _ = _sb_peeled(nsteps, make_step,
                            (jnp.zeros((SB_QROWS, HEAD_DIM), F32), jnp.zeros((SB_QROWS, 1), F32)), True)
        o_ref[...] = acc.astype(BF16)

    return _pcall(body, name=name, out_shape=jax.ShapeDtypeStruct((s, B_W), BF16), grid=(SB_HEADS, s // SB_QROWS),
                  in_specs=[pl.BlockSpec((SB_QROWS, HEAD_DIM), lambda h, i: (i, qb + h)),
                            pl.BlockSpec((s, HEAD_DIM), lambda h, i: (0, kb + h)),
                            pl.BlockSpec((s, HEAD_DIM), lambda h, i: (0, vb + h))],
                  out_specs=pl.BlockSpec((SB_QROWS, HEAD_DIM), lambda h, i: (i, h)))(proj, proj, proj)


def _sb_bwd(proj, do_b, *, name):
    s = proj.shape[0]
    assert s % (BLOCK * SB_UNROLL) == 0 and s % SB_QROWS == 0
    nkb = s // BLOCK
    qb, kb, vb = OFF_QB // HEAD_DIM, OFF_KB // HEAD_DIM, OFF_VB // HEAD_DIM

    def body(q_ref, k_ref, v_ref, do_ref, dq_ref, dk_ref, dv_ref, z_s, a_s):
        i = pl.program_id(1)

        @pl.when(i == 0)
        def _():
            dk_ref[...] = jnp.zeros_like(dk_ref)
            dv_ref[...] = jnp.zeros_like(dv_ref)

        q = q_ref[...].astype(BF16)
        dob = do_ref[...].astype(BF16)
        upper, lower = _tri(True), _tri(False)
        nsteps = _sb_steps(i)

        def make_recompute(masked):
            def recompute(t, run):
                for b in reversed(range(SB_UNROLL)):
                    j = (nsteps - 1 - t) * SB_UNROLL + b
                    z, log_beta, log_1mb, mask = _sb_scores(q, k_ref[_block_rows(j), :].astype(BF16), j, i, masked)
                    z_s[j] = z
                    a_s[j] = _sb_weights(log_beta, log_1mb, mask, run, upper)
                    run = run + jnp.sum(log_1mb, axis=-1, keepdims=True)
                return run
            return recompute

        _sb_peeled(nsteps, make_recompute, jnp.zeros((SB_QROWS, 1), F32), True)

        def make_grads(masked):
            def grads(t, carry):
                dq, run = carry
                for b in range(SB_UNROLL):
                    j = t * SB_UNROLL + b
                    rows = _block_rows(j)
                    kk, vv = k_ref[rows, :].astype(BF16), v_ref[rows, :].astype(BF16)
                    z, a = z_s[j], a_s[j]
                    de = _dot(dob, vv, NT) * a
                    beta = jax.nn.sigmoid(z)
                    if masked:
                        beta = jnp.where(_sb_mask(j, i), beta, 0.0)
                    dz = (de * jax.nn.sigmoid(-z) - beta * (run + _dot_exact(de, lower))).astype(BF16)
                    dk_ref[rows, :] += _dot(dz, q, TN) * ATT_SCALE
                    dv_ref[rows, :] += _dot(a.astype(BF16), dob, TN)
                    dq = dq + _dot(dz, kk)
                    run = run + jnp.sum(de, axis=-1, keepdims=True)
                return dq, run
            return grads

        dq, _ = _sb_peeled(nsteps, make_grads,
                           (jnp.zeros((SB_QROWS, HEAD_DIM), F32), jnp.zeros((SB_QROWS, 1), F32)), False)
        dq_ref[...] = dq * ATT_SCALE

    blk = pl.BlockSpec((SB_QROWS, HEAD_DIM), lambda h, i: (i, h))
    full = pl.BlockSpec((s, HEAD_DIM), lambda h, i: (0, h))
    shp = jax.ShapeDtypeStruct((s, B_W), F32)
    return _pcall(body, name=name, out_shape=(shp, shp, shp), grid=(SB_HEADS, s // SB_QROWS),
                  in_specs=[pl.BlockSpec((SB_QROWS, HEAD_DIM), lambda h, i: (i, qb + h)),
                            pl.BlockSpec((s, HEAD_DIM), lambda h, i: (0, kb + h)),
                            pl.BlockSpec((s, HEAD_DIM), lambda h, i: (0, vb + h)), blk],
                  out_specs=(blk, full, full),
                  scratch=[pltpu.VMEM((nkb, SB_QROWS, BLOCK), F32), pltpu.VMEM((nkb, SB_QROWS, BLOCK), F32)])(
                      proj, proj, proj, do_b)


def _coords():
    return lax.axis_index("x"), lax.axis_index("y"), lax.axis_index("c")


def _flip(v, bit):
    return 1 - v if bit else v


def _shard_of(ref, axis, idx, size):
    if axis == 0:
        sl = pl.ds(pl.multiple_of(idx * size, 16), size)
        return ref.at[sl, :] if len(ref.shape) == 2 else ref.at[:, sl, :]
    sl = pl.ds(pl.multiple_of(idx * size, 128), size)
    return ref.at[:, sl] if len(ref.shape) == 2 else ref.at[:, :, sl]


def _small_allgather(v, *, name, silu=False, deps=()):
    n = v.shape[1]

    def body(v_ref, out_ref, send_sems, recv_sems):
        x, y, c = _coords()
        me = 4 * x + 2 * y + c
        val = v_ref[...]
        out_ref[me] = val * jax.nn.sigmoid(val) if silu else val
        copies = []
        for k in range(1, N_DEV):
            peer = (_flip(x, k & 4), _flip(y, k & 2), _flip(c, k & 1))
            copies.append(pltpu.make_async_remote_copy(
                src_ref=out_ref.at[me], dst_ref=out_ref.at[me], send_sem=send_sems.at[k - 1],
                recv_sem=recv_sems.at[k - 1], device_id=peer, device_id_type=MESH))
        for cp in copies:
            cp.start()
        for cp in copies:
            cp.wait_recv()
        for cp in copies:
            cp.wait_send()

    return _pcall(body, name=name, out_shape=jax.ShapeDtypeStruct((N_DEV, 1, n), F32),
                  in_specs=[pl.BlockSpec(memory_space=pltpu.VMEM)], out_specs=pl.BlockSpec(memory_space=pltpu.VMEM),
                  scratch=[pltpu.SemaphoreType.DMA((N_DEV - 1,)), pltpu.SemaphoreType.DMA((N_DEV - 1,))],
                  deps=deps)(v)


def _cast_place(w, layer, axis, me, *, name):
    _, r, c = w.shape
    tr = _rows(r, c)
    nrt = r // tr

    def body(me_ref, w_ref, o_ref):
        o_ref[...] = w_ref[...].astype(BF16)

    wspec = pl.BlockSpec((None, tr, c), lambda i, me_ref: (layer, i, 0))
    if axis == 0:
        ospec = pl.BlockSpec((tr, c), lambda i, me_ref: (me_ref[0] * nrt + i, 0))
        shape = (r * N_DEV, c)
    else:
        ospec = pl.BlockSpec((tr, c), lambda i, me_ref: (i, me_ref[0]))
        shape = (r, c * N_DEV)
    return _pcall(body, name=name, out_shape=jax.ShapeDtypeStruct(shape, BF16), grid=(nrt,), in_specs=[wspec],
                  out_specs=ospec, prefetch=1)(me, w)


def _pair_sum(grad, sib, core, axis, *, name):
    _, r, c = sib.shape
    tr = _rows(r, c)
    nrt = r // tr

    def body(core_ref, g_ref, s_ref, o_ref):
        o_ref[...] = (g_ref[...].astype(F32) + s_ref[...].astype(F32)).astype(BF16)

    if axis == 0:
        gspec = pl.BlockSpec((tr, c), lambda q, i, core_ref: ((2 * q + core_ref[0]) * nrt + i, 0))
    else:
        gspec = pl.BlockSpec((tr, c), lambda q, i, core_ref: (i, 2 * q + core_ref[0]))
    sspec = pl.BlockSpec((None, tr, c), lambda q, i, core_ref: (q, i, 0))
    return _pcall(body, name=name, out_shape=jax.ShapeDtypeStruct(sib.shape, BF16), grid=(N_CHIPS, nrt),
                  in_specs=[gspec, sspec], out_specs=sspec, prefetch=1)(core, grad, sib)


ANY_SPEC = pl.BlockSpec(memory_space=pl.ANY)
SEM_SPEC = pl.BlockSpec(memory_space=pltpu.SEMAPHORE)
SPLIT_PARAMS = dict(has_side_effects=pltpu.SideEffectType.DATAFLOW_SIDE_EFFECTING)


def _split_start(copies_fn, buffers, sem_shape, after, *, name):
    n = len(buffers)
    rows, cols = sem_shape
    ns = rows * cols
    extra = ([] if after is None else [after]) + _take_token()

    def body(*refs):
        sems = refs[n + len(extra):n + len(extra) + 2 * ns]
        for cp in copies_fn(refs[:n], _sem_rows(sems[:ns], cols), _sem_rows(sems[ns:], cols)):
            cp.start()
        refs[-1][...] = jnp.zeros_like(refs[-1])

    sem = pltpu.SemaphoreType.DMA(())
    outs = pl.pallas_call(
        body, name=name,
        out_shape=((sem,) * (2 * ns) + tuple(jax.ShapeDtypeStruct(b.shape, b.dtype) for b in buffers) + (TOKEN,)),
        in_specs=(ANY_SPEC,) * (n + len(extra)),
        out_specs=(SEM_SPEC,) * (2 * ns) + (ANY_SPEC,) * n + (pl.BlockSpec(memory_space=pltpu.VMEM),),
        input_output_aliases={i: 2 * ns + i for i in range(n)},
        compiler_params=pltpu.CompilerParams(**SPLIT_PARAMS))(*buffers, *extra)
    _ORDER["token"] = outs[-1]
    return list(outs[:ns]), list(outs[ns:2 * ns]), list(outs[2 * ns:2 * ns + n]), outs[-1]


def _split_wait(copies_fn, send_sems, recv_sems, buffers, after, sem_rows, *, name):
    n, ns = len(buffers), len(send_sems)
    cols = ns // sem_rows
    extra = ([] if after is None else [after]) + _take_token()

    def body(*refs):
        sems = refs[n:n + 2 * ns]
        copies = copies_fn(refs[:n], _sem_rows(sems[:ns], cols), _sem_rows(sems[ns:], cols))
        for cp in copies:
            cp.wait_send()
        for cp in copies:
            cp.wait_recv()
        refs[-1][...] = jnp.zeros_like(refs[-1])

    outs = pl.pallas_call(
        body, name=name, out_shape=tuple(jax.ShapeDtypeStruct(b.shape, b.dtype) for b in buffers) + (TOKEN,),
        in_specs=(ANY_SPEC,) * n + (SEM_SPEC,) * (2 * ns) + (ANY_SPEC,) * len(extra),
        out_specs=(ANY_SPEC,) * n + (pl.BlockSpec(memory_space=pltpu.VMEM),),
        input_output_aliases={i: i for i in range(n)},
        compiler_params=pltpu.CompilerParams(**SPLIT_PARAMS))(*buffers, *send_sems, *recv_sems, *extra)
    _ORDER["token"] = outs[-1]
    return list(outs[:n])


def _sem_rows(sems, cols):
    return [sems[i:i + cols] for i in range(0, len(sems), cols)]


def _empty_hbm(shape, dtype):
    return pltpu.with_memory_space_constraint(lax.empty(shape, dtype), pltpu.HBM)


class _SplitGather:
    def __init__(self, fulls, axes, tag):
        self.axes, self.tag, self.nt = list(axes), tag, len(fulls)
        self.sizes = [f.shape[ax] // N_DEV for f, ax in zip(fulls, axes)]
        self.fulls = list(fulls)

    def _slot(self, ref, t, dev):
        return _shard_of(ref, self.axes[t], 4 * dev[0] + 2 * dev[1] + dev[2], self.sizes[t])

    def _first_copies(self, refs, send_sems, recv_sems):
        x, y, c = _coords()
        peers = [(x, y, 1 - c), (1 - x, y, c), (x, 1 - y, c), (1 - x, 1 - y, c)]
        return [pltpu.make_async_remote_copy(
            src_ref=self._slot(refs[t], t, (x, y, c)), dst_ref=self._slot(refs[t], t, (x, y, c)),
            send_sem=send_sems[t][k], recv_sem=recv_sems[t][k], device_id=peer, device_id_type=MESH)
            for t in range(self.nt) for k, peer in enumerate(peers)]

    def _forward_copies(self, refs, send_sems, recv_sems):
        x, y, c = _coords()
        chips = [(1 - x, y), (x, 1 - y), (1 - x, 1 - y)]
        return [pltpu.make_async_remote_copy(
            src_ref=self._slot(refs[t], t, (*chip, c)), dst_ref=self._slot(refs[t], t, (*chip, c)),
            send_sem=send_sems[t][j], recv_sem=recv_sems[t][j], device_id=(x, y, 1 - c), device_id_type=MESH)
            for t in range(self.nt) for j, chip in enumerate(chips)]

    def first(self, after):
        self.s1, self.r1, self.fulls, token = _split_start(
            self._first_copies, self.fulls, (self.nt, 4), after, name=f"comm_gather1_start_{self.tag}")
        return token

    def forward(self, after):
        bufs = _split_wait(self._first_copies, self.s1, self.r1, self.fulls, after, self.nt,
                           name=f"comm_gather1_wait_{self.tag}")
        self.s2, self.r2, self.fulls, token = _split_start(
            self._forward_copies, bufs, (self.nt, 3), after, name=f"comm_gather2_start_{self.tag}")
        return token

    def finish(self, after):
        return _split_wait(self._forward_copies, self.s2, self.r2, self.fulls, after, self.nt,
                           name=f"comm_gather2_wait_{self.tag}")


class _SplitPairExchange:
    def __init__(self, grads, axes, tag):
        self.nt, self.tag, self.axes = len(grads), tag, list(axes)
        self.grads = list(grads)
        self.sizes = [g.shape[ax] // N_DEV for g, ax in zip(grads, axes)]

    def _copies(self, refs, send_sems, recv_sems):
        nt = self.nt
        x, y, c = _coords()
        return [pltpu.make_async_remote_copy(
            src_ref=_shard_of(refs[t], self.axes[t], 2 * q + 1 - c, self.sizes[t]), dst_ref=refs[nt + t].at[q],
            send_sem=send_sems[t][q], recv_sem=recv_sems[t][q], device_id=(x, y, 1 - c), device_id_type=MESH)
            for t in range(nt) for q in range(N_CHIPS)]

    def start(self):
        landing = []
        for g, ax in zip(self.grads, self.axes):
            dims = list(g.shape)
            dims[ax] //= N_DEV
            landing.append(_empty_hbm((N_CHIPS, *dims), g.dtype))
        self.s, self.r, self.bufs, token = _split_start(
            self._copies, self.grads + landing, (self.nt, N_CHIPS), None,
            name=f"comm_rs_pair_start_{self.tag}")
        return token

    def finish(self, after):
        bufs = _split_wait(self._copies, self.s, self.r, self.bufs, after, self.nt,
                           name=f"comm_rs_pair_wait_{self.tag}")
        return bufs[:self.nt], bufs[self.nt:]


class _ReducePipeline:
    def __init__(self, core):
        self.core, self.items, self.done, self.now = core, [], [], 0

    def add(self, keys, grads, layer):
        axes = [SHARD_AXIS[k] for k in keys]
        pair = _SplitPairExchange([grads[k] for k in keys], axes, f"{keys[0]}{layer}")
        token = pair.start()
        self.items.append(dict(keys=keys, layer=layer, axes=axes, pair=pair, state="pair", since=self.now))
        return [token]

    def tick(self, after, flush=False):
        self.now += 1
        deps = []
        for it in self.items:
            if it["state"] == "pair" and it["since"] < self.now:
                grads, sib = it["pair"].finish(after)
                sums = [_pair_sum(g, s_, self.core, ax, name="pair_sum_" + k)
                        for k, g, s_, ax in zip(it["keys"], grads, sib, it["axes"])]
                it["chip"] = _SplitChipExchange(sums, f"{it['keys'][0]}{it['layer']}")
                deps.append(it["chip"].start())
                it.update(state="chip", since=self.now)
            elif it["state"] == "chip" and (flush or self.now - it["since"] >= 2):
                sums, remote = it["chip"].finish(after)
                self.done.append((it["keys"], it["layer"], sums, remote))
                it["state"] = "done"
        return deps

    def take_done(self):
        out, self.done = self.done, []
        return out


class _SplitChipExchange:
    def __init__(self, sums, tag):
        self.nt, self.tag = len(sums), tag
        self.sums = list(sums)

    def _copies(self, refs, send_sems, recv_sems):
        nt = self.nt
        x, y, c = _coords()
        copies = []
        for t in range(nt):
            for k in range(1, N_CHIPS):
                px, py = _flip(x, k & 2), _flip(y, k & 1)
                copies.append(pltpu.make_async_remote_copy(
                    src_ref=refs[t].at[2 * px + py], dst_ref=refs[nt + t].at[k - 1], send_sem=send_sems[t][k - 1],
                    recv_sem=recv_sems[t][k - 1], device_id=(px, py, c), device_id_type=MESH))
        return copies

    def start(self):
        landing = [_empty_hbm((N_CHIPS - 1,) + s.shape[1:], s.dtype) for s in self.sums]
        self.s, self.r, self.bufs, token = _split_start(
            self._copies, self.sums + landing, (self.nt, N_CHIPS - 1), None,
            name=f"comm_rs_chip_start_{self.tag}")
        return token

    def finish(self, after):
        bufs = _split_wait(self._copies, self.s, self.r, self.bufs, after, self.nt,
                           name=f"comm_rs_chip_wait_{self.tag}")
        return bufs[:self.nt], bufs[self.nt:]


def _adam_math(g, w, m, v):
    m2 = ADAM_B1 * m + (1.0 - ADAM_B1) * g
    v2 = ADAM_B2 * v + (1.0 - ADAM_B2) * (g * g)
    m_hat = m2 / (1.0 - ADAM_B1 ** ADAM_STEP)
    v_hat = v2 / (1.0 - ADAM_B2 ** ADAM_STEP)
    delta = -ADAM_LR * (m_hat / (jnp.sqrt(v_hat) + ADAM_EPS) + ADAM_WD * w)
    return delta, m2, v2


def _adamw_sharded(chip_sums, remote, chip, w, m, v, layer, prev, deps, *, name):
    nl, r, c = w.shape
    tr = _rows(r, c)

    def body(*refs):
        p_ref, r0_ref, r1_ref, r2_ref, w_ref, m_ref, v_ref = refs[1:8]
        g_out, d_out, m_out, v_out = refs[-4:]
        g = ((p_ref[...].astype(F32) + r0_ref[...].astype(F32)) + r1_ref[...].astype(F32)) + r2_ref[...].astype(F32)
        g_out[...] = g
        d_out[...], m_out[...], v_out[...] = _adam_math(g, w_ref[...], m_ref[...], v_ref[...])

    pspec = pl.BlockSpec((None, tr, c), lambda i, chip_ref: (chip_ref[0], i, 0))

    def rspec(k):
        return pl.BlockSpec((None, tr, c), lambda i, chip_ref: (k, i, 0))

    wspec = pl.BlockSpec((None, tr, c), lambda i, chip_ref: (layer, i, 0))
    in_specs = [pspec, rspec(0), rspec(1), rspec(2), wspec, wspec, wspec]
    args = [chip, chip_sums, remote, remote, remote, w, m, v]
    aliases = {}
    if prev is not None:
        in_specs += [pl.BlockSpec(memory_space=pl.ANY)] * 4
        aliases = {len(args) + i: i for i in range(4)}
        args += list(prev)
    shp = jax.ShapeDtypeStruct(w.shape, F32)
    return _pcall(body, name=name, out_shape=(shp,) * 4, grid=(r // tr,), in_specs=in_specs, out_specs=(wspec,) * 4,
                  aliases=aliases, prefetch=1, deps=deps)(*args)


def _adamw_local(g, w, m, v, *, name):
    nl, r, c = w.shape
    tr = _rows(r, c)

    def body(g_ref, w_ref, m_ref, v_ref, d_out, m_out, v_out):
        d_out[...], m_out[...], v_out[...] = _adam_math(g_ref[...], w_ref[...], m_ref[...], v_ref[...])

    spec = pl.BlockSpec((None, tr, c), lambda l, i: (l, i, 0))
    shp = jax.ShapeDtypeStruct(w.shape, F32)
    return _pcall(body, name=name, out_shape=(shp,) * 3, grid=(nl, r // tr), in_specs=[spec] * 4,
                  out_specs=(spec,) * 3)(g, w, m, v)


def _adamw_replicated(parts, w, m, v, *, name):
    n = w.shape[1]

    def body(p_ref, w_ref, m_ref, v_ref, g_out, d_out, m_out, v_out):
        g = p_ref[0]
        for k in range(1, N_DEV):
            g = g + p_ref[k]
        g_out[...] = g
        d_out[...], m_out[...], v_out[...] = _adam_math(g, w_ref[...], m_ref[...], v_ref[...])

    vm = pl.BlockSpec(memory_space=pltpu.VMEM)
    shp = jax.ShapeDtypeStruct((1, n), F32)
    return _pcall(body, name=name, out_shape=(shp,) * 4, in_specs=[vm] * 4, out_specs=(vm,) * 4)(parts, w, m, v)


def _group_views(qk, proj, g, dil, seq):
    if dil == 1:
        return (qk, qk, proj), (0, A_HEADS, 2 * A_HEADS)
    length = seq // dil
    lo = g * GROUP_W
    q = qk[:, lo:lo + GROUP_W].reshape(length, dil * GROUP_W)
    k = qk[:, A_W + lo:A_W + lo + GROUP_W].reshape(length, dil * GROUP_W)
    v = proj[:, OFF_VA + lo:OFF_VA + lo + GROUP_W].astype(BF16).reshape(length, dil * GROUP_W)
    return (q, k, v), (0, 0, 0)


def _mod_rows(mod, d):
    return [mod[:, i * d:(i + 1) * d] for i in range(6)]


MIXER_W = ("w_in", "w_branch_a", "w_branch_b", "w_out")
FFN_W = ("w_gate_up", "w_down")
SHARD_AXIS = {"w_in": 1, "w_branch_a": 1, "w_branch_b": 1, "w_out": 0, "w_gate_up": 1, "w_down": 0}


def _norm_args(mod, gain, which, d):
    rows = _mod_rows(mod, d)
    return gain, rows[3 * which + 1], rows[3 * which]


def _mixer_fwd_a(h, u, gains, w_in, cos2, sin2):
    seq = h.shape[0]
    proj = _mm(u, w_in, name="mm_in")
    qk = _qkrope_fwd(proj, gains, cos2, sin2, name="qkrope_fwd")
    os_, lses = [], []
    for g, dil in enumerate(DILATIONS):
        arrs, offs = _group_views(qk, proj, g, dil, seq)
        o, lse = _dil_fwd(*arrs, offs, seq // dil, dil, name=f"dil_fwd_{dil}")
        os_.append(o.reshape(seq, GROUP_W))
        lses.append(lse.reshape(seq, GROUP_W))
    o_a = _combine_fwd(os_, lses, name="combine_fwd")
    o_b = _sb_fwd(proj, name="sb_fwd")
    return dict(h_in=h, u=u, proj=proj, qk=qk, os=os_, lses=lses, o_a=o_a, o_b=o_b)


def _mixer_fwd_b(sv, mod, g2, wts):
    d = sv["h_in"].shape[1]
    merged, y_a, y_b = _mm_merge(sv["o_a"], sv["o_b"], wts["w_branch_a"], wts["w_branch_b"], sv["proj"],
                                 name="mm_branch")
    h_mid, t, u2 = _mm_resid_norm(merged, wts["w_out"], sv["h_in"], _mod_rows(mod, d)[2], _norm_args(mod, g2, 1, d),
                                  name="mm_out")
    sv.update(y_a=y_a, y_b=y_b, merged=merged, t=t, h_mid=h_mid, u2=u2)
    return h_mid


def _ffn_fwd_a(sv, w_gate_up):
    a, g, u = _mm_swiglu(sv["u2"], w_gate_up, name="mm_gate_up")
    sv.update(g=g, up=u, a=a)
    return a


def _ffn_fwd_b(sv, mod, w_down, next_norm):
    d = sv["h_mid"].shape[1]
    h_out, sv["f"], u_next = _mm_resid_norm(sv["a"], w_down, sv["h_mid"], _mod_rows(mod, d)[5], next_norm,
                                            name="mm_down")
    return h_out, u_next


def _wgrad(act, dout, key):
    return _mm(act, dout, ta=True, out_dtype=BF16, name="mm_wgrad_" + key)


def _ffn_bwd(dh, sv, mod, g2, wts, deps, hook):
    d = dh.shape[1]
    sc2, ga2 = _mod_rows(mod, d)[4:6]
    df, dgate2 = _resid_gate_bwd(dh, sv["f"], ga2, name="resid_gate_bwd", deps=deps)
    dg, dup = _mm_down_t_swiglu(df, wts["w_down"], sv["g"], sv["up"], name="mm_down_t")
    grads = {"w_down": _wgrad(sv["a"], df, "w_down")}
    hook(dup)
    du2 = _mm_cat_k(dg, dup, wts["w_gate_up"], name="mm_gate_up_t")
    grads["w_gate_up"] = _mm_cat_n(sv["u2"], dg, dup, name="mm_wgrad_w_gate_up")
    dh_mid, dsh2, dsc2, dg2 = _rmsmod_bwd(du2, sv["h_mid"], g2, sc2, dh, name="rmsmod_bwd")
    return dh_mid, [dsh2, dsc2, dgate2], dg2, grads


def _mixer_bwd(dh_mid, sv, mod, g1, gains, wts, cos2, sin2, deps, hook):
    seq, d = dh_mid.shape
    sc1, ga1 = _mod_rows(mod, d)[1:3]
    dt, dgate1 = _resid_gate_bwd(dh_mid, sv["t"], ga1, name="resid_gate_bwd", deps=deps)
    dmerged = _mm(dt, wts["w_out"], tb=True, name="mm_out_t")
    grads = {"w_out": _wgrad(sv["merged"], dt, "w_out")}
    dy_a, dy_b, dga, dgb = _merge_bwd(dmerged, sv["proj"], sv["y_a"], sv["y_b"], name="merge_bwd")
    do_a = _mm(dy_a, wts["w_branch_a"], tb=True, name="mm_branch_t")
    do_b = _mm(dy_b, wts["w_branch_b"], tb=True, name="mm_branch_t")
    grads["w_branch_a"] = _wgrad(sv["o_a"], dy_a, "w_branch_a")
    grads["w_branch_b"] = _wgrad(sv["o_b"], dy_b, "w_branch_b")
    dqb, dkb, dvb = _sb_bwd(sv["proj"], do_b, name="sb_bwd")
    comb = _combine_bwd(do_a, sv["os"], sv["lses"], name="combine_bwd", deps=hook(dqb, grads))
    grads = {}
    dos, dls = comb[:3], comb[3:]
    dqs, dks, dvs = [], [], []
    for g, dil in enumerate(DILATIONS):
        length = seq // dil
        arrs, offs = _group_views(sv["qk"], sv["proj"], g, dil, seq)
        view = (length, dil * GROUP_W)
        dq, dk, dv = _dil_bwd(*arrs, offs, sv["os"][g].reshape(view), sv["lses"][g].reshape(view),
                              dos[g].reshape(view), dls[g].reshape(view), length, dil, name=f"dil_bwd_{dil}")
        dqs.append(dq.reshape(seq, GROUP_W))
        dks.append(dk.reshape(seq, GROUP_W))
        dvs.append(dv.reshape(seq, GROUP_W))
    dq_pre, dqn = _qkrope_bwd(dqs, sv["proj"], gains, 0, cos2, sin2, name="qkrope_bwd")
    dk_pre, dkn = _qkrope_bwd(dks, sv["proj"], gains, 1, cos2, sin2, name="qkrope_bwd")
    dgains = jnp.stack([dqn, dkn])
    dproj = _assemble([dq_pre, dk_pre] + dvs + [dqb, dkb, dvb, dga, dgb], name="assemble_dproj")
    du = _mm(dproj, wts["w_in"], tb=True, name="mm_in_t")
    grads["w_in"] = _wgrad(sv["u"], dproj, "w_in")
    dh_in, dsh1, dsc1, dg1 = _rmsmod_bwd(du, sv["h_in"], g1, sc1, dh_mid, name="rmsmod_bwd")
    return dh_in, [dsh1, dsc1, dgate1], dg1, dgains, grads


def kernel(x, c, w_ada, b_ada, norm1_g, norm2_g, w_in, qn_g, kn_g, w_branch_a, w_branch_b, w_out, w_gate_up, w_down, loss_target, m_w_ada, m_b_ada, m_norm1_g, m_norm2_g, m_w_in, m_qn_g, m_kn_g, m_w_branch_a, m_w_branch_b, m_w_out, m_w_gate_up, m_w_down, v_w_ada, v_b_ada, v_norm1_g, v_norm2_g, v_w_in, v_qn_g, v_kn_g, v_w_branch_a, v_w_branch_b, v_w_out, v_w_gate_up, v_w_down):
    _ORDER["token"] = None
    seq, d = x.shape[1], x.shape[2]
    depth = w_in.shape[0]
    weights = dict(w_in=w_in, w_branch_a=w_branch_a, w_branch_b=w_branch_b, w_out=w_out, w_gate_up=w_gate_up,
                   w_down=w_down)
    moments_m = dict(w_in=m_w_in, w_branch_a=m_w_branch_a, w_branch_b=m_w_branch_b, w_out=m_w_out,
                     w_gate_up=m_w_gate_up, w_down=m_w_down)
    moments_v = dict(w_in=v_w_in, w_branch_a=v_w_branch_a, w_branch_b=v_w_branch_b, w_out=v_w_out,
                     w_gate_up=v_w_gate_up, w_down=v_w_down)
    xi, yi, ci = _coords()
    me = 4 * xi + 2 * yi + ci
    core = jnp.reshape(ci, (1,)).astype(jnp.int32)
    chip = jnp.reshape(2 * xi + yi, (1,)).astype(jnp.int32)

    ada_w = w_ada.shape[2]
    c_act = _small_allgather(c, name="comm_gather_c", silu=True).reshape(N_DEV, d)
    c_pad = jnp.concatenate([c_act, jnp.zeros_like(c_act)], axis=0).astype(BF16)
    bias = lax.dynamic_slice(b_ada, (0, me * ada_w), (depth, ada_w))
    mod_part = jnp.stack([_mm(c_pad, w_ada[l], name="mm_ada")[:N_DEV] for l in range(depth)]) + bias[:, None, :]
    mod_all = _small_allgather(mod_part.reshape(1, depth * N_DEV * ada_w), name="comm_gather_mod")
    mod_all = mod_all.reshape(N_DEV, depth, N_DEV, ada_w)
    mod_mine = lax.dynamic_index_in_dim(mod_all, me, axis=2, keepdims=False)
    mods = jnp.transpose(mod_mine, (1, 0, 2)).reshape(depth, 1, 6 * d)

    cos2, sin2 = _rope_tables(seq)
    gains = [jnp.stack([qn_g[l], kn_g[l]])[:, None, :] for l in range(depth)]
    g1s = [norm1_g[l][None] for l in range(depth)]
    g2s = [norm2_g[l][None] for l in range(depth)]

    me_arr = jnp.reshape(me, (1,)).astype(jnp.int32)

    def placed(keys, l):
        return [_cast_place(weights[k], l, SHARD_AXIS[k], me_arr, name="cast_place_" + k) for k in keys]

    def gather_of(keys, l, tag):
        return _SplitGather(placed(keys, l), [SHARD_AXIS[k] for k in keys], f"{tag}{l}")

    groups = [("w_in", 0, MIXER_W[:1]), ("rest", 0, MIXER_W[1:]), ("ffn", 0, FFN_W)]
    for l in range(1, depth):
        groups += [("mixer", l, MIXER_W), ("ffn", l, FFN_W)]
    gathers, token = {}, mods
    for tag, l, keys in groups:
        gathers[tag, l] = gather_of(keys, l, tag)
        token = gathers[tag, l].first(after=token)
    h = x[0]
    u = _rmsmod_fwd(h, *_norm_args(mods[0], g1s[0], 0, d), name="rmsmod_fwd")
    token = gathers["w_in", 0].forward(after=u)
    wm = {"w_in": gathers["w_in", 0].finish(after=token)[0]}
    saved, full = [], []
    for l in range(depth):
        last = l + 1 == depth
        sv = _mixer_fwd_a(h, u, gains[l], wm["w_in"], cos2, sin2)
        gathers["ffn", l].forward(after=sv["o_b"])
        if l == 0:
            gathers["rest", 0].forward(after=sv["o_b"])
            wm.update(zip(MIXER_W[1:], gathers["rest", 0].finish(after=sv["o_b"])))
        h_mid = _mixer_fwd_b(sv, mods[l], g2s[l], wm)
        wf = dict(zip(FFN_W, gathers["ffn", l].finish(after=h_mid)))
        a = _ffn_fwd_a(sv, wf["w_gate_up"])
        if not last:
            gathers["mixer", l + 1].forward(after=a)
        h, u = _ffn_fwd_b(sv, mods[l], wf["w_down"],
                          None if last else _norm_args(mods[l + 1], g1s[l + 1], 0, d))
        saved.append(sv)
        full.append({**wm, **wf})
        if not last:
            wm = dict(zip(MIXER_W, gathers["mixer", l + 1].finish(after=h)))
    loss_part, dh = _loss_fwd(h, loss_target[0], name="loss")
    loss = lax.psum(loss_part[0, 0], ("x", "y", "c"))

    pipe = _ReducePipeline(core)
    dmods, dg1s, dg2s, dgains = [None] * depth, [None] * depth, [None] * depth, [None] * depth
    deps = []
    for l in reversed(range(depth)):
        dh_mid, dmod_f, dg2s[l], grads = _ffn_bwd(dh, saved[l], mods[l], g2s[l], full[l], deps, pipe.tick)
        deps = pipe.tick(dh_mid) + pipe.add(FFN_W, grads, l)
        dh, dmod_m, dg1s[l], dgains[l], grads = _mixer_bwd(
            dh_mid, saved[l], mods[l], g1s[l], gains[l], full[l], cos2, sin2, deps,
            lambda after, early, l=l: pipe.tick(after) + pipe.add(MIXER_W[1:], early, l))
        dmods[l] = jnp.concatenate(dmod_m + dmod_f, axis=1)
        deps = pipe.tick(dh) + pipe.add(MIXER_W[:1], grads, l)
    grad_x = dh[None]

    stacked = {}

    def update(items):
        for keys, l, sums, remote in items:
            for k, p_, r_ in zip(keys, sums, remote):
                stacked[k] = _adamw_sharded(p_, r_, chip, weights[k], moments_m[k], moments_v[k], l,
                                            stacked.get(k), [], name="adamw_" + k)

    ready = pipe.take_done()
    update([it for it in ready if it[0] != FFN_W])

    small = jnp.concatenate(
        dmods + dg1s + dg2s + [dgains[l][0] for l in range(depth)] + [dgains[l][1] for l in range(depth)], axis=1)
    small_all = _small_allgather(small, name="comm_gather_small")
    pipe.tick(small_all)
    update([it for it in ready if it[0] == FFN_W] + pipe.take_done())

    def pack(b, n1, n2, qn, kn):
        return jnp.concatenate([t_.reshape(1, -1) for t_ in (b, n1, n2, qn, kn)], axis=1)

    sg, sd, sm, sv_ = _adamw_replicated(small_all, pack(b_ada, norm1_g, norm2_g, qn_g, kn_g),
                                        pack(m_b_ada, m_norm1_g, m_norm2_g, m_qn_g, m_kn_g),
                                        pack(v_b_ada, v_norm1_g, v_norm2_g, v_qn_g, v_kn_g), name="adamw_replicated")

    def unpack(p):
        sizes = [depth * 6 * d, depth * d, depth * d, depth * HEAD_DIM, depth * HEAD_DIM]
        shapes = [b_ada.shape, norm1_g.shape, norm2_g.shape, qn_g.shape, kn_g.shape]
        out, off = [], 0
        for n, shp in zip(sizes, shapes):
            out.append(p[0, off:off + n].reshape(shp))
            off += n
        return dict(zip(("b_ada", "norm1_g", "norm2_g", "qn_g", "kn_g"), out))

    ug, ud, um, uv = unpack(sg), unpack(sd), unpack(sm), unpack(sv_)
    res = {k: dict(g=ug[k], d=ud[k], m=um[k], v=uv[k]) for k in ug}

    dmod_all = small_all[:, 0, :depth * 6 * d].reshape(N_DEV, depth, 6 * d)
    g_ada = None
    for l in range(depth):
        dm = lax.dynamic_slice(dmod_all[:, l, :], (0, me * ada_w), (N_DEV, ada_w))
        dm = jnp.concatenate([dm, jnp.zeros_like(dm)], axis=0).astype(BF16)
        g_ada = _mm(c_pad, dm, ta=True, name="mm_wgrad_ada", stack=(l, depth, g_ada))
    d_ada, m_ada, v_ada = _adamw_local(g_ada, w_ada, m_w_ada, v_w_ada, name="adamw_local")
    res["w_ada"] = dict(g=g_ada, d=d_ada, m=m_ada, v=v_ada)

    pipe.tick(d_ada)
    update(pipe.take_done())
    pipe.tick(d_ada, flush=True)
    update(pipe.take_done())
    for k, (g_, d_, m_, v_) in stacked.items():
        res[k] = dict(g=g_, d=d_, m=m_, v=v_)

    order = ("w_ada", "b_ada", "norm1_g", "norm2_g", "w_in", "qn_g", "kn_g", "w_branch_a", "w_branch_b", "w_out",
             "w_gate_up", "w_down")
    _ORDER["token"] = None
    return (loss, grad_x, *[res[k]["g"] for k in order], *[res[k]["d"] for k in order],
            *[res[k]["m"] for k in order], *[res[k]["v"] for k in order])
```

```python
import functools

import jax
import jax.numpy as jnp
from jax import lax
from jax.experimental import pallas as pl
from jax.experimental.pallas import tpu as pltpu

F32 = jnp.float32
BF16 = jnp.bfloat16

HEAD_DIM = 128
BLOCK = 128
DILATIONS = (1, 4, 16)
HEADS_PER_GROUP = 4
A_HEADS = 12
SB_HEADS = 4
GROUP_W = HEADS_PER_GROUP * HEAD_DIM
A_W = A_HEADS * HEAD_DIM
B_W = SB_HEADS * HEAD_DIM
OFF_QA, OFF_KA, OFF_VA = 0, A_W, 2 * A_W
OFF_QB, OFF_KB, OFF_VB = 3 * A_W, 3 * A_W + B_W, 3 * A_W + 2 * B_W
OFF_GATES = 3 * A_W + 3 * B_W
ROPE_THETA = 10000.0
EPS = 1e-6
ATT_SCALE = HEAD_DIM ** -0.5
MASKED = -1e30

ADAM_LR, ADAM_B1, ADAM_B2, ADAM_EPS, ADAM_WD, ADAM_STEP = 0.001, 0.9, 0.999, 1e-08, 0.01, 10

N_DEV = 8
N_CHIPS = 4
V7X_VMEM_LIMIT_BYTES = 56 * 1024 * 1024
ELEMWISE_BLOCK_BYTES = 2 * 1024 * 1024
MESH = pl.DeviceIdType.MESH

NN = (((1,), (0,)), ((), ()))
NT = (((1,), (1,)), ((), ()))
TN = (((0,), (0,)), ((), ()))


def _dot(a, b, dims=NN):
    return lax.dot_general(a, b, dims, preferred_element_type=F32)


def _tile(n, cap, mult=128):
    best = None
    for t in range(mult, min(n, cap) + 1, mult):
        if n % t == 0:
            best = t
    if best is None:
        assert n <= 2 * cap, (n, cap)
        return n
    return best


def _rows(r, c):
    return _tile(r, max(16, ELEMWISE_BLOCK_BYTES // (4 * c)), 16)


_ORDER = {"token": None}
TOKEN = jax.ShapeDtypeStruct((8, 128), F32)


def _take_token():
    prev = _ORDER["token"]
    return [] if prev is None else [prev]


def _pcall(body, *, name, out_shape, grid=None, in_specs=None, out_specs=None, scratch=(), aliases=None,
           prefetch=0, deps=()):
    single = not isinstance(out_shape, (tuple, list))
    out_shapes = [out_shape] if single else list(out_shape)
    out_specs = [out_specs] if single else list(out_specs)
    extra = list(deps) + _take_token()
    n_in, n_extra, n_out = prefetch + len(in_specs), len(extra), len(out_shapes)

    def wrapped(*refs):
        token = refs[n_in + n_extra + n_out]
        token[...] = jnp.zeros_like(token)
        return body(*refs[:n_in], *refs[n_in + n_extra:n_in + n_extra + n_out], *refs[n_in + n_extra + n_out + 1:])

    in_specs = list(in_specs) + [pl.BlockSpec(memory_space=pl.ANY)] * n_extra
    if grid is None:
        out_specs.append(pl.BlockSpec(memory_space=pltpu.VMEM))
    else:
        out_specs.append(pl.BlockSpec(TOKEN.shape, lambda *_: (0, 0)))
    kwargs = dict(name=name, out_shape=out_shapes + [TOKEN], input_output_aliases=aliases or {},
                  compiler_params=pltpu.CompilerParams(vmem_limit_bytes=V7X_VMEM_LIMIT_BYTES))
    if prefetch:
        call = pl.pallas_call(wrapped, grid_spec=pltpu.PrefetchScalarGridSpec(
            num_scalar_prefetch=prefetch, grid=grid, in_specs=in_specs, out_specs=out_specs,
            scratch_shapes=list(scratch)), **kwargs)
    else:
        if grid is not None:
            kwargs["grid"] = grid
        call = pl.pallas_call(wrapped, in_specs=in_specs, out_specs=out_specs, scratch_shapes=list(scratch), **kwargs)

    def run(*args):
        outs = call(*args, *extra)
        _ORDER["token"] = outs[-1]
        return outs[0] if single else tuple(outs[:-1])

    return run


def _mm(a, b, *, name, ta=False, tb=False, out_dtype=F32, caps=(1024, 1024, 3072), stack=None, deps=()):
    kdim, m = a.shape if ta else a.shape[::-1]
    n, k2 = b.shape if tb else b.shape[::-1]
    assert kdim == k2, (a.shape, b.shape, ta, tb)
    tm, tn, tk = _tile(m, caps[0]), _tile(n, caps[1]), _tile(kdim, caps[2])
    nk = kdim // tk
    dims = (((0 if ta else 1,), (1 if tb else 0,)), ((), ()))

    def body(*refs):
        a_ref, b_ref = refs[0], refs[1]
        part = _dot(a_ref[...].astype(BF16), b_ref[...].astype(BF16), dims)
        if nk == 1:
            o_ref = refs[-1]
            o_ref[...] = part.astype(o_ref.dtype)
            return
        o_ref, acc_ref = refs[-2], refs[-1]
        k = pl.program_id(2)

        @pl.when(k == 0)
        def _():
            acc_ref[...] = part

        @pl.when(k > 0)
        def _():
            acc_ref[...] += part

        @pl.when(k == nk - 1)
        def _():
            o_ref[...] = acc_ref[...].astype(o_ref.dtype)

    a_spec = (pl.BlockSpec((tk, tm), lambda i, j, k: (k, i)) if ta
              else pl.BlockSpec((tm, tk), lambda i, j, k: (i, k)))
    b_spec = (pl.BlockSpec((tn, tk), lambda i, j, k: (j, k)) if tb
              else pl.BlockSpec((tk, tn), lambda i, j, k: (k, j)))
    ins, in_specs, aliases = [a, b], [a_spec, b_spec], {}
    if stack is None:
        out_shape = jax.ShapeDtypeStruct((m, n), out_dtype)
        out_spec = pl.BlockSpec((tm, tn), lambda i, j, k: (i, j))
    else:
        layer, n_layers, buf = stack
        out_shape = jax.ShapeDtypeStruct((n_layers, m, n), out_dtype)
        out_spec = pl.BlockSpec((None, tm, tn), lambda i, j, k: (layer, i, j))
        if buf is not None:
            ins.append(buf)
            in_specs.append(pl.BlockSpec(memory_space=pl.ANY))
            aliases = {2: 0}
    scratch = [] if nk == 1 else [pltpu.VMEM((tm, tn), F32)]
    return _pcall(body, name=name, out_shape=out_shape, grid=(m // tm, n // tn, nk), in_specs=in_specs,
                  out_specs=out_spec, scratch=scratch, aliases=aliases, deps=deps)(*ins)


EPILOGUE_ROWS = 256


def _row_chunks(tm):
    return [slice(r, r + EPILOGUE_ROWS) for r in range(0, tm, EPILOGUE_ROWS)] if tm > EPILOGUE_ROWS else [slice(0, tm)]


def _mm_cat_k(a_lo, a_hi, b, *, name):
    m, f = a_lo.shape
    n = b.shape[0]
    tm, tn, tk = _tile(m, 1024), _tile(n, 1024), _tile(f, 3072)
    half = f // tk
    nk = 2 * half

    def body(lo_ref, hi_ref, b_ref, o_ref, acc_ref):
        k = pl.program_id(2)

        def accumulate(a_ref):
            part = _dot(a_ref[...], b_ref[...], NT)

            @pl.when(k == 0)
            def _():
                acc_ref[...] = part

            @pl.when(k > 0)
            def _():
                acc_ref[...] += part

        pl.when(k < half)(lambda: accumulate(lo_ref))
        pl.when(k >= half)(lambda: accumulate(hi_ref))

        @pl.when(k == nk - 1)
        def _():
            o_ref[...] = acc_ref[...]

    return _pcall(body, name=name, out_shape=jax.ShapeDtypeStruct((m, n), F32), grid=(m // tm, n // tn, nk),
                  in_specs=[pl.BlockSpec((tm, tk), lambda i, j, k: (i, jnp.minimum(k, half - 1))),
                            pl.BlockSpec((tm, tk), lambda i, j, k: (i, jnp.maximum(k - half, 0))),
                            pl.BlockSpec((tn, tk), lambda i, j, k: (j, k))],
                  out_specs=pl.BlockSpec((tm, tn), lambda i, j, k: (i, j)),
                  scratch=[pltpu.VMEM((tm, tn), F32)])(a_lo, a_hi, b)


def _mm_cat_n(a, b_lo, b_hi, *, name):
    s, m = a.shape
    f = b_lo.shape[1]
    tm, tn = _tile(m, 1024), _tile(f, 1024)
    half = f // tn

    def body(a_ref, lo_ref, hi_ref, o_ref):
        j = pl.program_id(1)

        @pl.when(j < half)
        def _():
            o_ref[...] = _dot(a_ref[...], lo_ref[...], TN).astype(BF16)

        @pl.when(j >= half)
        def _():
            o_ref[...] = _dot(a_ref[...], hi_ref[...], TN).astype(BF16)

    return _pcall(body, name=name, out_shape=jax.ShapeDtypeStruct((m, 2 * f), BF16), grid=(m // tm, 2 * half),
                  in_specs=[pl.BlockSpec((s, tm), lambda i, j: (0, i)),
                            pl.BlockSpec((s, tn), lambda i, j: (0, jnp.minimum(j, half - 1))),
                            pl.BlockSpec((s, tn), lambda i, j: (0, jnp.maximum(j - half, 0)))],
                  out_specs=pl.BlockSpec((tm, tn), lambda i, j: (i, j)))(a, b_lo, b_hi)


def _mm_resid_norm(a, w, h, gate, norm, *, name):
    s, kdim = a.shape
    d = w.shape[1]
    tk = _tile(kdim, 2048)
    nk = kdim // tk
    tm = _tile(s, 256 if nk == 1 else 512)

    def body(*refs):
        a_ref, w_ref, h_ref, gate_ref = refs[:4]
        outs = refs[7:] if norm is not None else refs[4:]

        def finish(rows, t):
            hn = h_ref[rows, :] + gate_ref[...] * t
            outs[0][rows, :] = hn
            outs[1][rows, :] = t.astype(BF16)
            if norm is not None:
                g_ref, sc_ref, sh_ref = refs[4:7]
                r = lax.rsqrt(jnp.mean(hn * hn, axis=-1, keepdims=True) + EPS)
                outs[2][rows, :] = (((hn * r) * g_ref[...]) * (1.0 + sc_ref[...]) + sh_ref[...]).astype(BF16)

        if nk == 1:
            for rows in _row_chunks(tm):
                finish(rows, _dot(a_ref[rows, :], w_ref[...]))
            return
        acc_ref = refs[-1]
        k = pl.program_id(1)

        @pl.when(k == 0)
        def _():
            acc_ref[...] = _dot(a_ref[...], w_ref[...])

        @pl.when(jnp.logical_and(k > 0, k < nk - 1))
        def _():
            acc_ref[...] += _dot(a_ref[...], w_ref[...])

        @pl.when(k == nk - 1)
        def _():
            for rows in _row_chunks(tm):
                finish(rows, acc_ref[rows, :] + _dot(a_ref[rows, :], w_ref[...]))

    row = pl.BlockSpec((tm, d), lambda i, k: (i, 0))
    vec = pl.BlockSpec((1, d), lambda i, k: (0, 0))
    in_specs = [pl.BlockSpec((tm, tk), lambda i, k: (i, k)), pl.BlockSpec((tk, d), lambda i, k: (k, 0)), row, vec]
    args = [a, w, h, gate]
    out_shape = [jax.ShapeDtypeStruct((s, d), F32), jax.ShapeDtypeStruct((s, d), BF16)]
    if norm is not None:
        in_specs += [vec, vec, vec]
        args += list(norm)
        out_shape.append(jax.ShapeDtypeStruct((s, d), BF16))
    outs = _pcall(body, name=name, out_shape=tuple(out_shape), grid=(s // tm, nk), in_specs=in_specs,
                  out_specs=(row,) * len(out_shape), scratch=[] if nk == 1 else [pltpu.VMEM((tm, d), F32)])(*args)
    return outs if norm is not None else (*outs, None)


def _mm_merge(o_a, o_b, w_a, w_b, proj, *, name):
    s = o_a.shape[0]
    d = w_a.shape[1]
    tm = _tile(s, 512)
    ga_blk = OFF_GATES // d

    def body(oa_ref, ob_ref, wa_ref, wb_ref, ga_ref, gb_ref, m_ref, ya_ref, yb_ref):
        for rows in _row_chunks(tm):
            ya, yb = _dot(oa_ref[rows, :], wa_ref[...]), _dot(ob_ref[rows, :], wb_ref[...])
            m_ref[rows, :] = (jax.nn.sigmoid(ga_ref[rows, :]) * ya
                              + jax.nn.sigmoid(gb_ref[rows, :]) * yb).astype(BF16)
            ya_ref[rows, :] = ya.astype(BF16)
            yb_ref[rows, :] = yb.astype(BF16)

    row = pl.BlockSpec((tm, d), lambda i: (i, 0))
    act = pl.BlockSpec((tm, o_a.shape[1]), lambda i: (i, 0))
    wspec = pl.BlockSpec(w_a.shape, lambda i: (0, 0))
    shp = jax.ShapeDtypeStruct((s, d), BF16)
    return _pcall(body, name=name, out_shape=(shp, shp, shp), grid=(s // tm,),
                  in_specs=[act, act, wspec, wspec, pl.BlockSpec((tm, d), lambda i: (i, ga_blk)),
                            pl.BlockSpec((tm, d), lambda i: (i, ga_blk + 1))],
                  out_specs=(row, row, row))(o_a, o_b, w_a, w_b, proj, proj)


def _mm_down_t_swiglu(df, w_down, g, u, *, name):
    s, d = df.shape
    f = w_down.shape[0]
    tm, tn = _tile(s, 1024), _tile(f, 512)

    def body(df_ref, w_ref, g_ref, u_ref, dg_ref, du_ref):
        w = w_ref[...]
        for rows in _row_chunks(tm):
            da = _dot(df_ref[rows, :], w, NT)
            gf = g_ref[rows, :].astype(F32)
            sg = jax.nn.sigmoid(gf)
            dg_ref[rows, :] = (da * u_ref[rows, :].astype(F32) * (sg * (1.0 + gf * (1.0 - sg)))).astype(BF16)
            du_ref[rows, :] = (da * (gf * sg)).astype(BF16)

    tile = pl.BlockSpec((tm, tn), lambda i, j: (i, j))
    shp = jax.ShapeDtypeStruct((s, f), BF16)
    return _pcall(body, name=name, out_shape=(shp, shp), grid=(s // tm, f // tn),
                  in_specs=[pl.BlockSpec((tm, d), lambda i, j: (i, 0)), pl.BlockSpec((tn, d), lambda i, j: (j, 0)),
                            tile, tile],
                  out_specs=(tile, tile))(df, w_down, g, u)


def _rmsmod_fwd(h, g, scale, shift, *, name, deps=()):
    s, d = h.shape
    ts = _rows(s, d)

    def body(h_ref, g_ref, sc_ref, sh_ref, u_ref):
        hf = h_ref[...]
        r = lax.rsqrt(jnp.mean(hf * hf, axis=-1, keepdims=True) + EPS)
        u_ref[...] = (((hf * r) * g_ref[...]) * (1.0 + sc_ref[...]) + sh_ref[...]).astype(BF16)

    row = pl.BlockSpec((ts, d), lambda i: (i, 0))
    vec = pl.BlockSpec((1, d), lambda i: (0, 0))
    return _pcall(body, name=name, out_shape=jax.ShapeDtypeStruct((s, d), BF16), grid=(s // ts,),
                  in_specs=[row, vec, vec, vec], out_specs=row, deps=deps)(h, g, scale, shift)


def _rmsmod_bwd(du, h, g, scale, dres, *, name):
    s, d = h.shape
    ts = _rows(s, d)

    def body(du_ref, h_ref, g_ref, sc_ref, dres_ref, dh_ref, dsh_ref, dsc_ref, dg_ref):
        @pl.when(pl.program_id(0) == 0)
        def _():
            dsh_ref[...] = jnp.zeros_like(dsh_ref)
            dsc_ref[...] = jnp.zeros_like(dsc_ref)
            dg_ref[...] = jnp.zeros_like(dg_ref)

        hf, duf, gain = h_ref[...], du_ref[...], g_ref[...]
        r = lax.rsqrt(jnp.mean(hf * hf, axis=-1, keepdims=True) + EPS)
        xh = hf * r
        dn = duf * (1.0 + sc_ref[...])
        dsh_ref[...] += jnp.sum(duf, axis=0, keepdims=True)
        dsc_ref[...] += jnp.sum(duf * (xh * gain), axis=0, keepdims=True)
        dg_ref[...] += jnp.sum(dn * xh, axis=0, keepdims=True)
        dxh = dn * gain
        dh_ref[...] = dres_ref[...] + r * (dxh - xh * jnp.mean(dxh * xh, axis=-1, keepdims=True))

    row = pl.BlockSpec((ts, d), lambda i: (i, 0))
    vec = pl.BlockSpec((1, d), lambda i: (0, 0))
    vshape = jax.ShapeDtypeStruct((1, d), F32)
    return _pcall(body, name=name, out_shape=(jax.ShapeDtypeStruct((s, d), F32), vshape, vshape, vshape),
                  grid=(s // ts,), in_specs=[row, row, vec, vec, row],
                  out_specs=(row, vec, vec, vec))(du, h, g, scale, dres)


def _resid_gate_bwd(dh, t, gate, *, name, deps=()):
    s, d = dh.shape
    ts = _rows(s, d)

    def body(dh_ref, t_ref, g_ref, dt_ref, dg_ref):
        @pl.when(pl.program_id(0) == 0)
        def _():
            dg_ref[...] = jnp.zeros_like(dg_ref)

        dhf = dh_ref[...]
        dt_ref[...] = (dhf * g_ref[...]).astype(BF16)
        dg_ref[...] += jnp.sum(dhf * t_ref[...], axis=0, keepdims=True)

    row = pl.BlockSpec((ts, d), lambda i: (i, 0))
    vec = pl.BlockSpec((1, d), lambda i: (0, 0))
    return _pcall(body, name=name,
                  out_shape=(jax.ShapeDtypeStruct((s, d), BF16), jax.ShapeDtypeStruct((1, d), F32)),
                  grid=(s // ts,), in_specs=[row, row, vec], out_specs=(row, vec), deps=deps)(dh, t, gate)


def _merge_bwd(dm, proj, y_a, y_b, *, name):
    s, d = y_a.shape
    ts = _rows(s, d)
    ga_blk = OFF_GATES // d

    def body(dm_ref, ga_ref, gb_ref, ya_ref, yb_ref, dya_ref, dyb_ref, dga_ref, dgb_ref):
        dmf = dm_ref[...]
        sa, sb = jax.nn.sigmoid(ga_ref[...]), jax.nn.sigmoid(gb_ref[...])
        dya_ref[...] = (dmf * sa).astype(BF16)
        dyb_ref[...] = (dmf * sb).astype(BF16)
        dga_ref[...] = (dmf * ya_ref[...] * (sa * (1.0 - sa))).astype(BF16)
        dgb_ref[...] = (dmf * yb_ref[...] * (sb * (1.0 - sb))).astype(BF16)

    row = pl.BlockSpec((ts, d), lambda i: (i, 0))
    ga = pl.BlockSpec((ts, d), lambda i: (i, ga_blk))
    gb = pl.BlockSpec((ts, d), lambda i: (i, ga_blk + 1))
    shp = jax.ShapeDtypeStruct((s, d), BF16)
    return _pcall(body, name=name, out_shape=(shp, shp, shp, shp), grid=(s // ts,),
                  in_specs=[row, ga, gb, row, row], out_specs=(row, row, row, row))(dm, proj, proj, y_a, y_b)


def _mm_swiglu(u2, w_gate_up, *, name):
    s, d = u2.shape
    f = w_gate_up.shape[1] // 2
    tm, tn = _tile(s, 1024), _tile(f, 512)
    nj = f // tn

    def body(x_ref, wg_ref, wu_ref, a_ref, g_ref, u_ref):
        for rows in _row_chunks(tm):
            x = x_ref[rows, :]
            gf, uf = _dot(x, wg_ref[...]), _dot(x, wu_ref[...])
            a_ref[rows, :] = ((gf * jax.nn.sigmoid(gf)) * uf).astype(BF16)
            g_ref[rows, :] = gf.astype(BF16)
            u_ref[rows, :] = uf.astype(BF16)

    out = pl.BlockSpec((tm, tn), lambda i, j: (i, j))
    shp = jax.ShapeDtypeStruct((s, f), BF16)
    return _pcall(body, name=name, out_shape=(shp, shp, shp), grid=(s // tm, nj),
                  in_specs=[pl.BlockSpec((tm, d), lambda i, j: (i, 0)), pl.BlockSpec((d, tn), lambda i, j: (0, j)),
                            pl.BlockSpec((d, tn), lambda i, j: (0, nj + j))],
                  out_specs=(out, out, out))(u2, w_gate_up, w_gate_up)


def _loss_fwd(y, tgt, *, name):
    s, d = y.shape
    ts = _rows(s, d)

    def body(y_ref, t_ref, l_ref, dy_ref):
        @pl.when(pl.program_id(0) == 0)
        def _():
            l_ref[...] = jnp.zeros_like(l_ref)

        e = y_ref[...] - t_ref[...]
        dy_ref[...] = e * (1.0 / d)
        per_tok = jnp.sum(e * e, axis=1, keepdims=True) * (1.0 / d)
        l_ref[...] += 0.5 * jnp.sum(per_tok, axis=0, keepdims=True)

    row = pl.BlockSpec((ts, d), lambda i: (i, 0))
    return _pcall(body, name=name,
                  out_shape=(jax.ShapeDtypeStruct((1, 128), F32), jax.ShapeDtypeStruct((s, d), F32)),
                  grid=(s // ts,), in_specs=[row, row],
                  out_specs=(pl.BlockSpec((1, 128), lambda i: (0, 0)), row))(y, tgt)


def _rope_tables(seq):
    inv = jnp.power(ROPE_THETA, -jnp.arange(0, HEAD_DIM, 2, dtype=F32) / HEAD_DIM)
    ang = jnp.arange(seq, dtype=F32)[:, None] * inv[None, :]
    cos, sin = jnp.cos(ang), jnp.sin(ang)
    return jnp.concatenate([cos, cos], axis=1), jnp.concatenate([-sin, sin], axis=1)


def _qkrope_fwd(proj, gains, cos2, sin2, *, name):
    s = proj.shape[0]
    ts = _rows(s, A_W)

    def body(x_ref, g_ref, c_ref, s_ref, o_ref):
        gain, cos, sin = g_ref[...], c_ref[...], s_ref[...]
        for h in range(A_HEADS):
            lanes = slice(h * HEAD_DIM, (h + 1) * HEAD_DIM)
            x = x_ref[:, lanes]
            y = (x * lax.rsqrt(jnp.mean(x * x, axis=-1, keepdims=True) + EPS)) * gain
            o_ref[:, lanes] = (y * cos + pltpu.roll(y, HEAD_DIM // 2, 1) * sin).astype(BF16)

    heads = pl.BlockSpec((ts, A_W), lambda i, j: (i, j))
    tab = pl.BlockSpec((ts, HEAD_DIM), lambda i, j: (i, 0))
    gain = pl.BlockSpec((None, 1, HEAD_DIM), lambda i, j: (j, 0, 0))
    return _pcall(body, name=name, out_shape=jax.ShapeDtypeStruct((s, 2 * A_W), BF16),
                  grid=(s // ts, 2), in_specs=[heads, gain, tab, tab], out_specs=heads)(
                      proj, gains, cos2, sin2)


def _qkrope_bwd(d_groups, proj, gains, which, cos2, sin2, *, name):
    s = proj.shape[0]
    ts = _rows(s, A_W)

    def body(d0_ref, d1_ref, d2_ref, x_ref, g_ref, c_ref, s_ref, dx_ref, dg_ref):
        @pl.when(pl.program_id(0) == 0)
        def _():
            dg_ref[...] = jnp.zeros_like(dg_ref)

        gain, cos, sin = g_ref[...], c_ref[...], s_ref[...]
        dg = jnp.zeros((1, HEAD_DIM), F32)
        for h in range(A_HEADS):
            lanes = slice(h * HEAD_DIM, (h + 1) * HEAD_DIM)
            slot = slice((h % HEADS_PER_GROUP) * HEAD_DIM, (h % HEADS_PER_GROUP + 1) * HEAD_DIM)
            dout = (d0_ref, d1_ref, d2_ref)[h // HEADS_PER_GROUP][:, slot]
            dy = dout * cos + pltpu.roll(dout * sin, HEAD_DIM // 2, 1)
            x = x_ref[:, lanes]
            r = lax.rsqrt(jnp.mean(x * x, axis=-1, keepdims=True) + EPS)
            xh = x * r
            dg = dg + jnp.sum(dy * xh, axis=0, keepdims=True)
            dxh = dy * gain
            dx_ref[:, lanes] = (r * (dxh - xh * jnp.mean(dxh * xh, axis=-1, keepdims=True))).astype(BF16)
        dg_ref[...] += dg

    group = pl.BlockSpec((ts, GROUP_W), lambda i: (i, 0))
    tab = pl.BlockSpec((ts, HEAD_DIM), lambda i: (i, 0))
    gain = pl.BlockSpec((None, 1, HEAD_DIM), lambda i: (which, 0, 0))
    return _pcall(body, name=name,
                  out_shape=(jax.ShapeDtypeStruct((s, A_W), BF16), jax.ShapeDtypeStruct((1, HEAD_DIM), F32)),
                  grid=(s // ts,),
                  in_specs=[group, group, group, pl.BlockSpec((ts, A_W), lambda i: (i, which)), gain, tab, tab],
                  out_specs=(pl.BlockSpec((ts, A_W), lambda i: (i, 0)), pl.BlockSpec((1, HEAD_DIM), lambda i: (0, 0))))(
                      *d_groups, proj, gains, cos2, sin2)


def _assemble(pieces, *, name):
    s = pieces[0].shape[0]
    widths = [p.shape[1] for p in pieces]
    total = sum(widths)
    ts = _rows(s, total // 2)

    def body(*refs):
        o_ref, off = refs[-1], 0
        for x_ref, w in zip(refs[:-1], widths):
            o_ref[:, off:off + w] = x_ref[...].astype(BF16)
            off += w

    return _pcall(body, name=name, out_shape=jax.ShapeDtypeStruct((s, total), BF16), grid=(s // ts,),
                  in_specs=[pl.BlockSpec((ts, w), lambda i: (i, 0)) for w in widths],
                  out_specs=pl.BlockSpec((ts, total), lambda i: (i, 0)))(*pieces)


def _block_rows(blk):
    if isinstance(blk, int):
        return pl.ds(blk * BLOCK, BLOCK)
    return pl.ds(pl.multiple_of(blk * BLOCK, BLOCK), BLOCK)


def _band_masks(n, with_prev):
    row = lax.broadcasted_iota(jnp.int32, (BLOCK, BLOCK), 0)
    col = lax.broadcasted_iota(jnp.int32, (BLOCK, BLOCK), 1)
    cur = col <= row
    if not with_prev:
        return [(n, cur)]
    prev = col >= row + jnp.where(n >= 1, 0, BLOCK)
    return [(n, cur), (jnp.maximum(n - 1, 0), prev)]


def _dil_fwd(q_arr, k_arr, v_arr, offs, length, dil, *, name):
    nj, nb = dil * HEADS_PER_GROUP, length // BLOCK
    ju = HEADS_PER_GROUP
    qo, ko, vo = (off // ju for off in offs)
    assert all(off % ju == 0 for off in offs)

    def body(q_ref, k_ref, v_ref, o_ref, l_ref):
        n = pl.program_id(1)
        masks = _band_masks(n, nb > 1)
        for cb in range(ju):
            lanes = slice(cb * HEAD_DIM, (cb + 1) * HEAD_DIM)
            q = q_ref[:, lanes].astype(BF16)
            parts = []
            for blk, mask in masks:
                rows = _block_rows(blk)
                sc = _dot(q, k_ref[rows, lanes].astype(BF16), NT) * ATT_SCALE
                parts.append((jnp.where(mask, sc, MASKED), rows))
            m = parts[0][0].max(axis=-1, keepdims=True)
            for sc, _ in parts[1:]:
                m = jnp.maximum(m, sc.max(axis=-1, keepdims=True))
            den = jnp.zeros((BLOCK, 1), F32)
            acc = jnp.zeros((BLOCK, HEAD_DIM), F32)
            for sc, rows in parts:
                p = jnp.exp(sc - m)
                den = den + jnp.sum(p, axis=-1, keepdims=True)
                acc = acc + _dot(p.astype(BF16), v_ref[rows, lanes].astype(BF16))
            o_ref[:, lanes] = acc / den
            l_ref[:, lanes] = jnp.broadcast_to(m + jnp.log(den), (BLOCK, HEAD_DIM))

    qspec = pl.BlockSpec((BLOCK, ju * HEAD_DIM), lambda j, n: (n, qo + j))
    kspec = pl.BlockSpec((length, ju * HEAD_DIM), lambda j, n: (0, ko + j))
    vspec = pl.BlockSpec((length, ju * HEAD_DIM), lambda j, n: (0, vo + j))
    ospec = pl.BlockSpec((BLOCK, ju * HEAD_DIM), lambda j, n: (n, j))
    shp = jax.ShapeDtypeStruct((length, nj * HEAD_DIM), F32)
    return _pcall(body, name=name, out_shape=(shp, shp), grid=(nj // ju, nb), in_specs=[qspec, kspec, vspec],
                  out_specs=(ospec, ospec))(q_arr, k_arr, v_arr)


def _dil_bwd(q_arr, k_arr, v_arr, offs, o, lse, do, dlse, length, dil, *, name):
    nj, nb = dil * HEADS_PER_GROUP, length // BLOCK
    ju = HEADS_PER_GROUP if length <= 4 * BLOCK else 2
    qo, ko, vo = (off // ju for off in offs)
    assert all(off % ju == 0 for off in offs)

    def body(q_ref, k_ref, v_ref, o_ref, l_ref, do_ref, dl_ref, dq_ref, dk_ref, dv_ref):
        dk_ref[...] = jnp.zeros_like(dk_ref)
        dv_ref[...] = jnp.zeros_like(dv_ref)

        def step(n, carry):
            qrows = _block_rows(n)
            masks = _band_masks(n, nb > 1)
            for cb in range(ju):
                lanes = slice(cb * HEAD_DIM, (cb + 1) * HEAD_DIM)
                q = q_ref[qrows, lanes].astype(BF16)
                dof = do_ref[qrows, lanes]
                dob = dof.astype(BF16)
                lse_b = l_ref[qrows, lanes]
                shift = dl_ref[qrows, lanes] - jnp.sum(dof * o_ref[qrows, lanes], axis=-1, keepdims=True)
                dq = jnp.zeros((BLOCK, HEAD_DIM), F32)
                for blk, mask in masks:
                    rows = _block_rows(blk)
                    kk, vv = k_ref[rows, lanes].astype(BF16), v_ref[rows, lanes].astype(BF16)
                    sc = _dot(q, kk, NT) * ATT_SCALE
                    p = jnp.where(mask, jnp.exp(sc - lse_b), 0.0)
                    ds = (p * (_dot(dob, vv, NT) + shift)).astype(BF16)
                    dq = dq + _dot(ds, kk)
                    dk_ref[rows, lanes] += _dot(ds, q, TN) * ATT_SCALE
                    dv_ref[rows, lanes] += _dot(p.astype(BF16), dob, TN)
                dq_ref[qrows, lanes] = dq * ATT_SCALE
            return carry

        if nb == 1:
            step(0, 0)
        else:
            lax.fori_loop(0, nb, step, 0)

    def col(off):
        return pl.BlockSpec((length, ju * HEAD_DIM), lambda j: (0, off + j))

    shp = jax.ShapeDtypeStruct((length, nj * HEAD_DIM), F32)
    return _pcall(body, name=name, out_shape=(shp, shp, shp), grid=(nj // ju,),
                  in_specs=[col(qo), col(ko), col(vo), col(0), col(0), col(0), col(0)],
                  out_specs=(col(0), col(0), col(0)))(q_arr, k_arr, v_arr, o, lse, do, dlse)


def _combine_weights(l_refs):
    ls = [r[...] for r in l_refs]
    m = jnp.maximum(jnp.maximum(ls[0], ls[1]), ls[2])
    es = [jnp.exp(l - m) for l in ls]
    den = es[0] + es[1] + es[2]
    return [e / den for e in es]


def _combine_fwd(os_, lses, *, name):
    s = os_[0].shape[0]
    ts = _rows(s, GROUP_W)

    def body(o0, o1, o2, l0, l1, l2, out_ref):
        w = _combine_weights((l0, l1, l2))
        out_ref[...] = (w[0] * o0[...] + w[1] * o1[...] + w[2] * o2[...]).astype(BF16)

    row = pl.BlockSpec((ts, GROUP_W), lambda i: (i, 0))
    return _pcall(body, name=name, out_shape=jax.ShapeDtypeStruct((s, GROUP_W), BF16), grid=(s // ts,),
                  in_specs=[row] * 6, out_specs=row)(*os_, *lses)


def _combine_bwd(do_a, os_, lses, *, name, deps=()):
    s = do_a.shape[0]
    ts = _rows(s, GROUP_W)

    def body(d_ref, o0, o1, o2, l0, l1, l2, do0, do1, do2, dl0, dl1, dl2):
        w = _combine_weights((l0, l1, l2))
        d = d_ref[...]
        og = [o0[...], o1[...], o2[...]]
        oa = w[0] * og[0] + w[1] * og[1] + w[2] * og[2]
        ta = jnp.sum(d * oa, axis=-1, keepdims=True)
        for g, (do_ref, dl_ref) in enumerate(((do0, dl0), (do1, dl1), (do2, dl2))):
            do_ref[...] = w[g] * d
            dl_ref[...] = w[g] * (jnp.sum(d * og[g], axis=-1, keepdims=True) - ta)

    head = pl.BlockSpec((ts, HEAD_DIM), lambda i, h: (i, h))
    shp = jax.ShapeDtypeStruct((s, GROUP_W), F32)
    return _pcall(body, name=name, out_shape=(shp,) * 6, grid=(s // ts, HEADS_PER_GROUP),
                  in_specs=[head] * 7, out_specs=(head,) * 6, deps=deps)(do_a, *os_, *lses)


def _dot_exact(x, ones_mask):
    hi = x.astype(BF16)
    r1 = x - hi.astype(F32)
    mid = r1.astype(BF16)
    lo = (r1 - mid.astype(F32)).astype(BF16)
    return _dot(hi, ones_mask) + _dot(mid, ones_mask) + _dot(lo, ones_mask)


SB_QROWS = 2 * BLOCK
SB_UNROLL = 4
SB_HEADS_PER_STEP = 2
SB_LANES = [slice(hh * HEAD_DIM, (hh + 1) * HEAD_DIM) for hh in range(SB_HEADS_PER_STEP)]


def _sb_mask(j, i):
    row = lax.broadcasted_iota(jnp.int32, (SB_QROWS, BLOCK), 0)
    col = lax.broadcasted_iota(jnp.int32, (SB_QROWS, BLOCK), 1)
    return col + (j * BLOCK - i * SB_QROWS) < row


def _sb_steps(i):
    return ((i + 1) * (SB_QROWS // BLOCK) + SB_UNROLL - 1) // SB_UNROLL


def _sb_scores(q, kk, j, i, masked):
    mask = _sb_mask(j, i) if masked else None
    z = _dot(q, kk, NT) * ATT_SCALE
    sp = jnp.log(1.0 + jnp.exp(-jnp.abs(z)))
    log_beta = jnp.minimum(z, 0.0) - sp
    log_1mb = jnp.minimum(-z, 0.0) - sp
    if masked:
        log_1mb = jnp.where(mask, log_1mb, 0.0)
    return z, log_beta, log_1mb, mask


def _sb_weights(log_beta, log_1mb, mask, run, upper):
    a = jnp.exp(log_beta + (run + _dot_exact(log_1mb, upper)))
    return a if mask is None else jnp.where(mask, a, 0.0)


def _sb_peeled(nsteps, make_step, init, masked_first):
    if masked_first:
        return lax.fori_loop(1, nsteps, make_step(False), make_step(True)(0, init))
    return make_step(True)(nsteps - 1, lax.fori_loop(0, nsteps - 1, make_step(False), init))


def _tri(strict_lower):
    row = lax.broadcasted_iota(jnp.int32, (BLOCK, BLOCK), 0)
    col = lax.broadcasted_iota(jnp.int32, (BLOCK, BLOCK), 1)
    return ((row > col) if strict_lower else (row < col)).astype(BF16)


def _sb_fwd(proj, *, name):
    s = proj.shape[0]
    assert s % (BLOCK * SB_UNROLL) == 0 and s % SB_QROWS == 0

    def body(q_ref, k_ref, v_ref, o_ref):
        i = pl.program_id(1)
        qs = [q_ref[:, lanes].astype(BF16) for lanes in SB_LANES]
        upper = _tri(True)
        nsteps = _sb_steps(i)

        def make_step(masked):
            def step(t, carry):
                carry = list(carry)
                for b in reversed(range(SB_UNROLL)):
                    j = (nsteps - 1 - t) * SB_UNROLL + b
                    rows = _block_rows(j)
                    for hh, lanes in enumerate(SB_LANES):
                        acc, run = carry[hh]
                        _, log_beta, log_1mb, mask = _sb_scores(qs[hh], k_ref[rows, lanes].astype(BF16), j, i, masked)
                        a = _sb_weights(log_beta, log_1mb, mask, run, upper)
                        carry[hh] = (acc + _dot(a.astype(BF16), v_ref[rows, lanes].astype(BF16)),
                                     run + jnp.sum(log_1mb, axis=-1, keepdims=True))
                return tuple(carry)
            return step

        zero = (jnp.zeros((SB_QROWS, HEAD_DIM), F32), jnp.zeros((SB_QROWS, 1), F32))
        for lanes, (acc, _) in zip(SB_LANES, _sb_peeled(nsteps, make_step, (zero,) * SB_HEADS_PER_STEP, True)):
            o_ref[:, lanes] = acc.astype(BF16)

    width = SB_HEADS_PER_STEP * HEAD_DIM
    qb, kb, vb = (off // width for off in (OFF_QB, OFF_KB, OFF_VB))
    return _pcall(body, name=name, out_shape=jax.ShapeDtypeStruct((s, B_W), BF16),
                  grid=(SB_HEADS // SB_HEADS_PER_STEP, s // SB_QROWS),
                  in_specs=[pl.BlockSpec((SB_QROWS, width), lambda h, i: (i, qb + h)),
                            pl.BlockSpec((s, width), lambda h, i: (0, kb + h)),
                            pl.BlockSpec((s, width), lambda h, i: (0, vb + h))],
                  out_specs=pl.BlockSpec((SB_QROWS, width), lambda h, i: (i, h)))(proj, proj, proj)


def _sb_bwd(proj, do_b, *, name):
    s = proj.shape[0]
    assert s % (BLOCK * SB_UNROLL) == 0 and s % SB_QROWS == 0
    nkb = s // BLOCK

    def body(q_ref, k_ref, v_ref, do_ref, dq_ref, dk_ref, dv_ref, z_s, a_s):
        i = pl.program_id(1)

        @pl.when(i == 0)
        def _():
            dk_ref[...] = jnp.zeros_like(dk_ref)
            dv_ref[...] = jnp.zeros_like(dv_ref)

        qs = [q_ref[:, lanes].astype(BF16) for lanes in SB_LANES]
        dobs = [do_ref[:, lanes].astype(BF16) for lanes in SB_LANES]
        upper, lower = _tri(True), _tri(False)
        nsteps = _sb_steps(i)

        def make_recompute(masked):
            def recompute(t, runs):
                runs = list(runs)
                for b in reversed(range(SB_UNROLL)):
                    j = (nsteps - 1 - t) * SB_UNROLL + b
                    rows = _block_rows(j)
                    for hh, lanes in enumerate(SB_LANES):
                        z, log_beta, log_1mb, mask = _sb_scores(qs[hh], k_ref[rows, lanes].astype(BF16), j, i, masked)
                        z_s[hh, j] = z
                        a_s[hh, j] = _sb_weights(log_beta, log_1mb, mask, runs[hh], upper)
                        runs[hh] = runs[hh] + jnp.sum(log_1mb, axis=-1, keepdims=True)
                return tuple(runs)
            return recompute

        _sb_peeled(nsteps, make_recompute, (jnp.zeros((SB_QROWS, 1), F32),) * SB_HEADS_PER_STEP, True)

        def make_grads(masked):
            def grads(t, carry):
                carry = list(carry)
                for b in range(SB_UNROLL):
                    j = t * SB_UNROLL + b
                    rows = _block_rows(j)
                    for hh, lanes in enumerate(SB_LANES):
                        dq, run = carry[hh]
                        kk, vv = k_ref[rows, lanes].astype(BF16), v_ref[rows, lanes].astype(BF16)
                        z, a = z_s[hh, j], a_s[hh, j]
                        de = _dot(dobs[hh], vv, NT) * a
                        beta = jax.nn.sigmoid(z)
                        if masked:
                            beta = jnp.where(_sb_mask(j, i), beta, 0.0)
                        dz = (de * jax.nn.sigmoid(-z) - beta * (run + _dot_exact(de, lower))).astype(BF16)
                        dk_ref[rows, lanes] += _dot(dz, qs[hh], TN) * ATT_SCALE
                        dv_ref[rows, lanes] += _dot(a.astype(BF16), dobs[hh], TN)
                        carry[hh] = (dq + _dot(dz, kk), run + jnp.sum(de, axis=-1, keepdims=True))
                return tuple(carry)
            return grads

        zero = (jnp.zeros((SB_QROWS, HEAD_DIM), F32), jnp.zeros((SB_QROWS, 1), F32))
        for lanes, (dq, _) in zip(SB_LANES, _sb_peeled(nsteps, make_grads, (zero,) * SB_HEADS_PER_STEP, False)):
            dq_ref[:, lanes] = dq * ATT_SCALE

    width = SB_HEADS_PER_STEP * HEAD_DIM
    qb, kb, vb = (off // width for off in (OFF_QB, OFF_KB, OFF_VB))
    blk = pl.BlockSpec((SB_QROWS, width), lambda h, i: (i, h))
    full = pl.BlockSpec((s, width), lambda h, i: (0, h))
    shp = jax.ShapeDtypeStruct((s, B_W), F32)
    saved = pltpu.VMEM((SB_HEADS_PER_STEP, nkb, SB_QROWS, BLOCK), F32)
    return _pcall(body, name=name, out_shape=(shp, shp, shp), grid=(SB_HEADS // SB_HEADS_PER_STEP, s // SB_QROWS),
                  in_specs=[pl.BlockSpec((SB_QROWS, width), lambda h, i: (i, qb + h)),
                            pl.BlockSpec((s, width), lambda h, i: (0, kb + h)),
                            pl.BlockSpec((s, width), lambda h, i: (0, vb + h)), blk],
                  out_specs=(blk, full, full), scratch=[saved, saved])(proj, proj, proj, do_b)


def _coords():
    return lax.axis_index("x"), lax.axis_index("y"), lax.axis_index("c")


def _flip(v, bit):
    return 1 - v if bit else v


def _shard_of(ref, axis, idx, size):
    if axis == 0:
        sl = pl.ds(pl.multiple_of(idx * size, 16), size)
        return ref.at[sl, :] if len(ref.shape) == 2 else ref.at[:, sl, :]
    sl = pl.ds(pl.multiple_of(idx * size, 128), size)
    return ref.at[:, sl] if len(ref.shape) == 2 else ref.at[:, :, sl]


def _small_allgather(v, *, name, silu=False, deps=()):
    n = v.shape[1]

    def body(v_ref, out_ref, send_sems, recv_sems):
        x, y, c = _coords()
        me = 4 * x + 2 * y + c
        val = v_ref[...]
        out_ref[me] = val * jax.nn.sigmoid(val) if silu else val
        copies = []
        for k in range(1, N_DEV):
            peer = (_flip(x, k & 4), _flip(y, k & 2), _flip(c, k & 1))
            copies.append(pltpu.make_async_remote_copy(
                src_ref=out_ref.at[me], dst_ref=out_ref.at[me], send_sem=send_sems.at[k - 1],
                recv_sem=recv_sems.at[k - 1], device_id=peer, device_id_type=MESH))
        for cp in copies:
            cp.start()
        for cp in copies:
            cp.wait_recv()
        for cp in copies:
            cp.wait_send()

    return _pcall(body, name=name, out_shape=jax.ShapeDtypeStruct((N_DEV, 1, n), F32),
                  in_specs=[pl.BlockSpec(memory_space=pltpu.VMEM)], out_specs=pl.BlockSpec(memory_space=pltpu.VMEM),
                  scratch=[pltpu.SemaphoreType.DMA((N_DEV - 1,)), pltpu.SemaphoreType.DMA((N_DEV - 1,))],
                  deps=deps)(v)


def _cast_place(w, layer, axis, me, *, name):
    _, r, c = w.shape
    tr = _rows(r, c)
    nrt = r // tr

    def body(me_ref, w_ref, o_ref):
        o_ref[...] = w_ref[...].astype(BF16)

    wspec = pl.BlockSpec((None, tr, c), lambda i, me_ref: (layer, i, 0))
    if axis == 0:
        ospec = pl.BlockSpec((tr, c), lambda i, me_ref: (me_ref[0] * nrt + i, 0))
        shape = (r * N_DEV, c)
    else:
        ospec = pl.BlockSpec((tr, c), lambda i, me_ref: (i, me_ref[0]))
        shape = (r, c * N_DEV)
    return _pcall(body, name=name, out_shape=jax.ShapeDtypeStruct(shape, BF16), grid=(nrt,), in_specs=[wspec],
                  out_specs=ospec, prefetch=1)(me, w)


def _pair_sum(grad, sib, core, axis, *, name):
    _, r, c = sib.shape
    tr = _rows(r, c)
    nrt = r // tr

    def body(core_ref, g_ref, s_ref, o_ref):
        o_ref[...] = (g_ref[...].astype(F32) + s_ref[...].astype(F32)).astype(BF16)

    if axis == 0:
        gspec = pl.BlockSpec((tr, c), lambda q, i, core_ref: ((2 * q + core_ref[0]) * nrt + i, 0))
    else:
        gspec = pl.BlockSpec((tr, c), lambda q, i, core_ref: (i, 2 * q + core_ref[0]))
    sspec = pl.BlockSpec((None, tr, c), lambda q, i, core_ref: (q, i, 0))
    return _pcall(body, name=name, out_shape=jax.ShapeDtypeStruct(sib.shape, BF16), grid=(N_CHIPS, nrt),
                  in_specs=[gspec, sspec], out_specs=sspec, prefetch=1)(core, grad, sib)


ANY_SPEC = pl.BlockSpec(memory_space=pl.ANY)
SEM_SPEC = pl.BlockSpec(memory_space=pltpu.SEMAPHORE)
SPLIT_PARAMS = dict(has_side_effects=pltpu.SideEffectType.DATAFLOW_SIDE_EFFECTING)


def _split_start(copies_fn, buffers, sem_shape, after, *, name):
    n = len(buffers)
    rows, cols = sem_shape
    ns = rows * cols
    extra = ([] if after is None else [after]) + _take_token()

    def body(*refs):
        sems = refs[n + len(extra):n + len(extra) + 2 * ns]
        for cp in copies_fn(refs[:n], _sem_rows(sems[:ns], cols), _sem_rows(sems[ns:], cols)):
            cp.start()
        refs[-1][...] = jnp.zeros_like(refs[-1])

    sem = pltpu.SemaphoreType.DMA(())
    outs = pl.pallas_call(
        body, name=name,
        out_shape=((sem,) * (2 * ns) + tuple(jax.ShapeDtypeStruct(b.shape, b.dtype) for b in buffers) + (TOKEN,)),
        in_specs=(ANY_SPEC,) * (n + len(extra)),
        out_specs=(SEM_SPEC,) * (2 * ns) + (ANY_SPEC,) * n + (pl.BlockSpec(memory_space=pltpu.VMEM),),
        input_output_aliases={i: 2 * ns + i for i in range(n)},
        compiler_params=pltpu.CompilerParams(**SPLIT_PARAMS))(*buffers, *extra)
    _ORDER["token"] = outs[-1]
    return list(outs[:ns]), list(outs[ns:2 * ns]), list(outs[2 * ns:2 * ns + n]), outs[-1]


def _split_wait(copies_fn, send_sems, recv_sems, buffers, after, sem_rows, *, name):
    n, ns = len(buffers), len(send_sems)
    cols = ns // sem_rows
    extra = ([] if after is None else [after]) + _take_token()

    def body(*refs):
        sems = refs[n:n + 2 * ns]
        copies = copies_fn(refs[:n], _sem_rows(sems[:ns], cols), _sem_rows(sems[ns:], cols))
        for cp in copies:
            cp.wait_send()
        for cp in copies:
            cp.wait_recv()
        refs[-1][...] = jnp.zeros_like(refs[-1])

    outs = pl.pallas_call(
        body, name=name, out_shape=tuple(jax.ShapeDtypeStruct(b.shape, b.dtype) for b in buffers) + (TOKEN,),
        in_specs=(ANY_SPEC,) * n + (SEM_SPEC,) * (2 * ns) + (ANY_SPEC,) * len(extra),
        out_specs=(ANY_SPEC,) * n + (pl.BlockSpec(memory_space=pltpu.VMEM),),
        input_output_aliases={i: i for i in range(n)},
        compiler_params=pltpu.CompilerParams(**SPLIT_PARAMS))(*buffers, *send_sems, *recv_sems, *extra)
    _ORDER["token"] = outs[-1]
    return list(outs[:n])


def _sem_rows(sems, cols):
    return [sems[i:i + cols] for i in range(0, len(sems), cols)]


def _empty_hbm(shape, dtype):
    return pltpu.with_memory_space_constraint(lax.empty(shape, dtype), pltpu.HBM)


class _SplitGather:
    def __init__(self, fulls, axes, tag):
        self.axes, self.tag, self.nt = list(axes), tag, len(fulls)
        self.sizes = [f.shape[ax] // N_DEV for f, ax in zip(fulls, axes)]
        self.fulls = list(fulls)

    def _slot(self, ref, t, dev):
        return _shard_of(ref, self.axes[t], 4 * dev[0] + 2 * dev[1] + dev[2], self.sizes[t])

    def _first_copies(self, refs, send_sems, recv_sems):
        x, y, c = _coords()
        peers = [(x, y, 1 - c), (1 - x, y, c), (x, 1 - y, c), (1 - x, 1 - y, c)]
        return [pltpu.make_async_remote_copy(
            src_ref=self._slot(refs[t], t, (x, y, c)), dst_ref=self._slot(refs[t], t, (x, y, c)),
            send_sem=send_sems[t][k], recv_sem=recv_sems[t][k], device_id=peer, device_id_type=MESH)
            for t in range(self.nt) for k, peer in enumerate(peers)]

    def _forward_copies(self, refs, send_sems, recv_sems):
        x, y, c = _coords()
        chips = [(1 - x, y), (x, 1 - y), (1 - x, 1 - y)]
        return [pltpu.make_async_remote_copy(
            src_ref=self._slot(refs[t], t, (*chip, c)), dst_ref=self._slot(refs[t], t, (*chip, c)),
            send_sem=send_sems[t][j], recv_sem=recv_sems[t][j], device_id=(x, y, 1 - c), device_id_type=MESH)
            for t in range(self.nt) for j, chip in enumerate(chips)]

    def first(self, after):
        self.s1, self.r1, self.fulls, token = _split_start(
            self._first_copies, self.fulls, (self.nt, 4), after, name=f"comm_gather1_start_{self.tag}")
        return token

    def forward(self, after):
        bufs = _split_wait(self._first_copies, self.s1, self.r1, self.fulls, after, self.nt,
                           name=f"comm_gather1_wait_{self.tag}")
        self.s2, self.r2, self.fulls, token = _split_start(
            self._forward_copies, bufs, (self.nt, 3), after, name=f"comm_gather2_start_{self.tag}")
        return token

    def finish(self, after):
        return _split_wait(self._forward_copies, self.s2, self.r2, self.fulls, after, self.nt,
                           name=f"comm_gather2_wait_{self.tag}")


class _SplitPairExchange:
    def __init__(self, grads, axes, tag):
        self.nt, self.tag, self.axes = len(grads), tag, list(axes)
        self.grads = list(grads)
        self.sizes = [g.shape[ax] // N_DEV for g, ax in zip(grads, axes)]

    def _copies(self, refs, send_sems, recv_sems):
        nt = self.nt
        x, y, c = _coords()
        return [pltpu.make_async_remote_copy(
            src_ref=_shard_of(refs[t], self.axes[t], 2 * q + 1 - c, self.sizes[t]), dst_ref=refs[nt + t].at[q],
            send_sem=send_sems[t][q], recv_sem=recv_sems[t][q], device_id=(x, y, 1 - c), device_id_type=MESH)
            for t in range(nt) for q in range(N_CHIPS)]

    def start(self):
        landing = []
        for g, ax in zip(self.grads, self.axes):
            dims = list(g.shape)
            dims[ax] //= N_DEV
            landing.append(_empty_hbm((N_CHIPS, *dims), g.dtype))
        self.s, self.r, self.bufs, token = _split_start(
            self._copies, self.grads + landing, (self.nt, N_CHIPS), None,
            name=f"comm_rs_pair_start_{self.tag}")
        return token

    def finish(self, after):
        bufs = _split_wait(self._copies, self.s, self.r, self.bufs, after, self.nt,
                           name=f"comm_rs_pair_wait_{self.tag}")
        return bufs[:self.nt], bufs[self.nt:]


class _ReducePipeline:
    def __init__(self, core):
        self.core, self.items, self.done, self.now = core, [], [], 0

    def add(self, keys, grads, layer):
        axes = [SHARD_AXIS[k] for k in keys]
        pair = _SplitPairExchange([grads[k] for k in keys], axes, f"{keys[0]}{layer}")
        token = pair.start()
        self.items.append(dict(keys=keys, layer=layer, axes=axes, pair=pair, state="pair", since=self.now))
        return [token]

    def tick(self, after, flush=False):
        self.now += 1
        deps = []
        for it in self.items:
            if it["state"] == "pair" and it["since"] < self.now:
                grads, sib = it["pair"].finish(after)
                sums = [_pair_sum(g, s_, self.core, ax, name="pair_sum_" + k)
                        for k, g, s_, ax in zip(it["keys"], grads, sib, it["axes"])]
                it["chip"] = _SplitChipExchange(sums, f"{it['keys'][0]}{it['layer']}")
                deps.append(it["chip"].start())
                it.update(state="chip", since=self.now)
            elif it["state"] == "chip" and (flush or self.now - it["since"] >= 2):
                sums, remote = it["chip"].finish(after)
                self.done.append((it["keys"], it["layer"], sums, remote))
                it["state"] = "done"
        return deps

    def take_done(self):
        out, self.done = self.done, []
        return out


class _SplitChipExchange:
    def __init__(self, sums, tag):
        self.nt, self.tag = len(sums), tag
        self.sums = list(sums)

    def _copies(self, refs, send_sems, recv_sems):
        nt = self.nt
        x, y, c = _coords()
        copies = []
        for t in range(nt):
            for k in range(1, N_CHIPS):
                px, py = _flip(x, k & 2), _flip(y, k & 1)
                copies.append(pltpu.make_async_remote_copy(
                    src_ref=refs[t].at[2 * px + py], dst_ref=refs[nt + t].at[k - 1], send_sem=send_sems[t][k - 1],
                    recv_sem=recv_sems[t][k - 1], device_id=(px, py, c), device_id_type=MESH))
        return copies

    def start(self):
        landing = [_empty_hbm((N_CHIPS - 1,) + s.shape[1:], s.dtype) for s in self.sums]
        self.s, self.r, self.bufs, token = _split_start(
            self._copies, self.sums + landing, (self.nt, N_CHIPS - 1), None,
            name=f"comm_rs_chip_start_{self.tag}")
        return token

    def finish(self, after):
        bufs = _split_wait(self._copies, self.s, self.r, self.bufs, after, self.nt,
                           name=f"comm_rs_chip_wait_{self.tag}")
        return bufs[:self.nt], bufs[self.nt:]


def _adam_math(g, w, m, v):
    m2 = ADAM_B1 * m + (1.0 - ADAM_B1) * g
    v2 = ADAM_B2 * v + (1.0 - ADAM_B2) * (g * g)
    m_hat = m2 / (1.0 - ADAM_B1 ** ADAM_STEP)
    v_hat = v2 / (1.0 - ADAM_B2 ** ADAM_STEP)
    delta = -ADAM_LR * (m_hat / (jnp.sqrt(v_hat) + ADAM_EPS) + ADAM_WD * w)
    return delta, m2, v2


def _adamw_sharded(chip_sums, remote, chip, w, m, v, layer, prev, deps, *, name):
    nl, r, c = w.shape
    tr = _rows(r, c)

    def body(*refs):
        p_ref, r0_ref, r1_ref, r2_ref, w_ref, m_ref, v_ref = refs[1:8]
        g_out, d_out, m_out, v_out = refs[-4:]
        g = ((p_ref[...].astype(F32) + r0_ref[...].astype(F32)) + r1_ref[...].astype(F32)) + r2_ref[...].astype(F32)
        g_out[...] = g
        d_out[...], m_out[...], v_out[...] = _adam_math(g, w_ref[...], m_ref[...], v_ref[...])

    pspec = pl.BlockSpec((None, tr, c), lambda i, chip_ref: (chip_ref[0], i, 0))

    def rspec(k):
        return pl.BlockSpec((None, tr, c), lambda i, chip_ref: (k, i, 0))

    wspec = pl.BlockSpec((None, tr, c), lambda i, chip_ref: (layer, i, 0))
    in_specs = [pspec, rspec(0), rspec(1), rspec(2), wspec, wspec, wspec]
    args = [chip, chip_sums, remote, remote, remote, w, m, v]
    aliases = {}
    if prev is not None:
        in_specs += [pl.BlockSpec(memory_space=pl.ANY)] * 4
        aliases = {len(args) + i: i for i in range(4)}
        args += list(prev)
    shp = jax.ShapeDtypeStruct(w.shape, F32)
    return _pcall(body, name=name, out_shape=(shp,) * 4, grid=(r // tr,), in_specs=in_specs, out_specs=(wspec,) * 4,
                  aliases=aliases, prefetch=1, deps=deps)(*args)


def _adamw_local(g, w, m, v, *, name):
    nl, r, c = w.shape
    tr = _rows(r, c)

    def body(g_ref, w_ref, m_ref, v_ref, d_out, m_out, v_out):
        d_out[...], m_out[...], v_out[...] = _adam_math(g_ref[...], w_ref[...], m_ref[...], v_ref[...])

    spec = pl.BlockSpec((None, tr, c), lambda l, i: (l, i, 0))
    shp = jax.ShapeDtypeStruct(w.shape, F32)
    return _pcall(body, name=name, out_shape=(shp,) * 3, grid=(nl, r // tr), in_specs=[spec] * 4,
                  out_specs=(spec,) * 3)(g, w, m, v)


def _adamw_replicated(parts, w, m, v, *, name):
    n = w.shape[1]

    def body(p_ref, w_ref, m_ref, v_ref, g_out, d_out, m_out, v_out):
        g = p_ref[0]
        for k in range(1, N_DEV):
            g = g + p_ref[k]
        g_out[...] = g
        d_out[...], m_out[...], v_out[...] = _adam_math(g, w_ref[...], m_ref[...], v_ref[...])

    vm = pl.BlockSpec(memory_space=pltpu.VMEM)
    shp = jax.ShapeDtypeStruct((1, n), F32)
    return _pcall(body, name=name, out_shape=(shp,) * 4, in_specs=[vm] * 4, out_specs=(vm,) * 4)(parts, w, m, v)


def _group_views(qk, proj, g, dil, seq):
    if dil == 1:
        return (qk, qk, proj), (0, A_HEADS, 2 * A_HEADS)
    length = seq // dil
    lo = g * GROUP_W
    q = qk[:, lo:lo + GROUP_W].reshape(length, dil * GROUP_W)
    k = qk[:, A_W + lo:A_W + lo + GROUP_W].reshape(length, dil * GROUP_W)
    v = proj[:, OFF_VA + lo:OFF_VA + lo + GROUP_W].astype(BF16).reshape(length, dil * GROUP_W)
    return (q, k, v), (0, 0, 0)


def _mod_rows(mod, d):
    return [mod[:, i * d:(i + 1) * d] for i in range(6)]


MIXER_W = ("w_in", "w_branch_a", "w_branch_b", "w_out")
FFN_W = ("w_gate_up", "w_down")
SHARD_AXIS = {"w_in": 1, "w_branch_a": 1, "w_branch_b": 1, "w_out": 0, "w_gate_up": 1, "w_down": 0}


def _norm_args(mod, gain, which, d):
    rows = _mod_rows(mod, d)
    return gain, rows[3 * which + 1], rows[3 * which]


def _mixer_fwd_a(h, u, gains, w_in, cos2, sin2):
    seq = h.shape[0]
    proj = _mm(u, w_in, name="mm_in")
    qk = _qkrope_fwd(proj, gains, cos2, sin2, name="qkrope_fwd")
    os_, lses = [], []
    for g, dil in enumerate(DILATIONS):
        arrs, offs = _group_views(qk, proj, g, dil, seq)
        o, lse = _dil_fwd(*arrs, offs, seq // dil, dil, name=f"dil_fwd_{dil}")
        os_.append(o.reshape(seq, GROUP_W))
        lses.append(lse.reshape(seq, GROUP_W))
    o_a = _combine_fwd(os_, lses, name="combine_fwd")
    o_b = _sb_fwd(proj, name="sb_fwd")
    return dict(h_in=h, u=u, proj=proj, qk=qk, os=os_, lses=lses, o_a=o_a, o_b=o_b)


def _mixer_fwd_b(sv, mod, g2, wts):
    d = sv["h_in"].shape[1]
    merged, y_a, y_b = _mm_merge(sv["o_a"], sv["o_b"], wts["w_branch_a"], wts["w_branch_b"], sv["proj"],
                                 name="mm_branch")
    h_mid, t, u2 = _mm_resid_norm(merged, wts["w_out"], sv["h_in"], _mod_rows(mod, d)[2], _norm_args(mod, g2, 1, d),
                                  name="mm_out")
    sv.update(y_a=y_a, y_b=y_b, merged=merged, t=t, h_mid=h_mid, u2=u2)
    return h_mid


def _ffn_fwd_a(sv, w_gate_up):
    a, g, u = _mm_swiglu(sv["u2"], w_gate_up, name="mm_gate_up")
    sv.update(g=g, up=u, a=a)
    return a


def _ffn_fwd_b(sv, mod, w_down, next_norm):
    d = sv["h_mid"].shape[1]
    h_out, sv["f"], u_next = _mm_resid_norm(sv["a"], w_down, sv["h_mid"], _mod_rows(mod, d)[5], next_norm,
                                            name="mm_down")
    return h_out, u_next


def _wgrad(act, dout, key):
    return _mm(act, dout, ta=True, out_dtype=BF16, name="mm_wgrad_" + key)


def _ffn_bwd(dh, sv, mod, g2, wts, deps, hook):
    d = dh.shape[1]
    sc2, ga2 = _mod_rows(mod, d)[4:6]
    df, dgate2 = _resid_gate_bwd(dh, sv["f"], ga2, name="resid_gate_bwd", deps=deps)
    dg, dup = _mm_down_t_swiglu(df, wts["w_down"], sv["g"], sv["up"], name="mm_down_t")
    grads = {"w_down": _wgrad(sv["a"], df, "w_down")}
    hook(dup)
    du2 = _mm_cat_k(dg, dup, wts["w_gate_up"], name="mm_gate_up_t")
    grads["w_gate_up"] = _mm_cat_n(sv["u2"], dg, dup, name="mm_wgrad_w_gate_up")
    dh_mid, dsh2, dsc2, dg2 = _rmsmod_bwd(du2, sv["h_mid"], g2, sc2, dh, name="rmsmod_bwd")
    return dh_mid, [dsh2, dsc2, dgate2], dg2, grads


def _mixer_bwd(dh_mid, sv, mod, g1, gains, wts, cos2, sin2, deps, hook):
    seq, d = dh_mid.shape
    sc1, ga1 = _mod_rows(mod, d)[1:3]
    dt, dgate1 = _resid_gate_bwd(dh_mid, sv["t"], ga1, name="resid_gate_bwd", deps=deps)
    dmerged = _mm(dt, wts["w_out"], tb=True, name="mm_out_t")
    grads = {"w_out": _wgrad(sv["merged"], dt, "w_out")}
    dy_a, dy_b, dga, dgb = _merge_bwd(dmerged, sv["proj"], sv["y_a"], sv["y_b"], name="merge_bwd")
    do_a = _mm(dy_a, wts["w_branch_a"], tb=True, name="mm_branch_t")
    do_b = _mm(dy_b, wts["w_branch_b"], tb=True, name="mm_branch_t")
    grads["w_branch_a"] = _wgrad(sv["o_a"], dy_a, "w_branch_a")
    grads["w_branch_b"] = _wgrad(sv["o_b"], dy_b, "w_branch_b")
    dqb, dkb, dvb = _sb_bwd(sv["proj"], do_b, name="sb_bwd")
    comb = _combine_bwd(do_a, sv["os"], sv["lses"], name="combine_bwd", deps=hook(dqb, grads))
    grads = {}
    dos, dls = comb[:3], comb[3:]
    dqs, dks, dvs = [], [], []
    for g, dil in enumerate(DILATIONS):
        length = seq // dil
        arrs, offs = _group_views(sv["qk"], sv["proj"], g, dil, seq)
        view = (length, dil * GROUP_W)
        dq, dk, dv = _dil_bwd(*arrs, offs, sv["os"][g].reshape(view), sv["lses"][g].reshape(view),
                              dos[g].reshape(view), dls[g].reshape(view), length, dil, name=f"dil_bwd_{dil}")
        dqs.append(dq.reshape(seq, GROUP_W))
        dks.append(dk.reshape(seq, GROUP_W))
        dvs.append(dv.reshape(seq, GROUP_W))
    dq_pre, dqn = _qkrope_bwd(dqs, sv["proj"], gains, 0, cos2, sin2, name="qkrope_bwd")
    dk_pre, dkn = _qkrope_bwd(dks, sv["proj"], gains, 1, cos2, sin2, name="qkrope_bwd")
    dgains = jnp.stack([dqn, dkn])
    dproj = _assemble([dq_pre, dk_pre] + dvs + [dqb, dkb, dvb, dga, dgb], name="assemble_dproj")
    du = _mm(dproj, wts["w_in"], tb=True, name="mm_in_t")
    grads["w_in"] = _wgrad(sv["u"], dproj, "w_in")
    dh_in, dsh1, dsc1, dg1 = _rmsmod_bwd(du, sv["h_in"], g1, sc1, dh_mid, name="rmsmod_bwd")
    return dh_in, [dsh1, dsc1, dgate1], dg1, dgains, grads


def kernel(x, c, w_ada, b_ada, norm1_g, norm2_g, w_in, qn_g, kn_g, w_branch_a, w_branch_b, w_out, w_gate_up, w_down, loss_target, m_w_ada, m_b_ada, m_norm1_g, m_norm2_g, m_w_in, m_qn_g, m_kn_g, m_w_branch_a, m_w_branch_b, m_w_out, m_w_gate_up, m_w_down, v_w_ada, v_b_ada, v_norm1_g, v_norm2_g, v_w_in, v_qn_g, v_kn_g, v_w_branch_a, v_w_branch_b, v_w_out, v_w_gate_up, v_w_down):
    _ORDER["token"] = None
    seq, d = x.shape[1], x.shape[2]
    depth = w_in.shape[0]
    weights = dict(w_in=w_in, w_branch_a=w_branch_a, w_branch_b=w_branch_b, w_out=w_out, w_gate_up=w_gate_up,
                   w_down=w_down)
    moments_m = dict(w_in=m_w_in, w_branch_a=m_w_branch_a, w_branch_b=m_w_branch_b, w_out=m_w_out,
                     w_gate_up=m_w_gate_up, w_down=m_w_down)
    moments_v = dict(w_in=v_w_in, w_branch_a=v_w_branch_a, w_branch_b=v_w_branch_b, w_out=v_w_out,
                     w_gate_up=v_w_gate_up, w_down=v_w_down)
    xi, yi, ci = _coords()
    me = 4 * xi + 2 * yi + ci
    core = jnp.reshape(ci, (1,)).astype(jnp.int32)
    chip = jnp.reshape(2 * xi + yi, (1,)).astype(jnp.int32)

    ada_w = w_ada.shape[2]
    c_act = _small_allgather(c, name="comm_gather_c", silu=True).reshape(N_DEV, d)
    c_pad = jnp.concatenate([c_act, jnp.zeros_like(c_act)], axis=0).astype(BF16)
    bias = lax.dynamic_slice(b_ada, (0, me * ada_w), (depth, ada_w))
    mod_part = jnp.stack([_mm(c_pad, w_ada[l], name="mm_ada")[:N_DEV] for l in range(depth)]) + bias[:, None, :]
    mod_all = _small_allgather(mod_part.reshape(1, depth * N_DEV * ada_w), name="comm_gather_mod")
    mod_all = mod_all.reshape(N_DEV, depth, N_DEV, ada_w)
    mod_mine = lax.dynamic_index_in_dim(mod_all, me, axis=2, keepdims=False)
    mods = jnp.transpose(mod_mine, (1, 0, 2)).reshape(depth, 1, 6 * d)

    cos2, sin2 = _rope_tables(seq)
    gains = [jnp.stack([qn_g[l], kn_g[l]])[:, None, :] for l in range(depth)]
    g1s = [norm1_g[l][None] for l in range(depth)]
    g2s = [norm2_g[l][None] for l in range(depth)]

    me_arr = jnp.reshape(me, (1,)).astype(jnp.int32)

    def placed(keys, l):
        return [_cast_place(weights[k], l, SHARD_AXIS[k], me_arr, name="cast_place_" + k) for k in keys]

    def gather_of(keys, l, tag):
        return _SplitGather(placed(keys, l), [SHARD_AXIS[k] for k in keys], f"{tag}{l}")

    groups = [("w_in", 0, MIXER_W[:1]), ("rest", 0, MIXER_W[1:]), ("ffn", 0, FFN_W)]
    for l in range(1, depth):
        groups += [("mixer", l, MIXER_W), ("ffn", l, FFN_W)]
    gathers, token = {}, mods
    for tag, l, keys in groups:
        gathers[tag, l] = gather_of(keys, l, tag)
        token = gathers[tag, l].first(after=token)
    h = x[0]
    u = _rmsmod_fwd(h, *_norm_args(mods[0], g1s[0], 0, d), name="rmsmod_fwd")
    token = gathers["w_in", 0].forward(after=u)
    wm = {"w_in": gathers["w_in", 0].finish(after=token)[0]}
    saved, full = [], []
    for l in range(depth):
        last = l + 1 == depth
        sv = _mixer_fwd_a(h, u, gains[l], wm["w_in"], cos2, sin2)
        gathers["ffn", l].forward(after=sv["o_b"])
        if l == 0:
            gathers["rest", 0].forward(after=sv["o_b"])
            wm.update(zip(MIXER_W[1:], gathers["rest", 0].finish(after=sv["o_b"])))
        h_mid = _mixer_fwd_b(sv, mods[l], g2s[l], wm)
        wf = dict(zip(FFN_W, gathers["ffn", l].finish(after=h_mid)))
        a = _ffn_fwd_a(sv, wf["w_gate_up"])
        if not last:
            gathers["mixer", l + 1].forward(after=a)
        h, u = _ffn_fwd_b(sv, mods[l], wf["w_down"],
                          None if last else _norm_args(mods[l + 1], g1s[l + 1], 0, d))
        saved.append(sv)
        full.append({**wm, **wf})
        if not last:
            wm = dict(zip(MIXER_W, gathers["mixer", l + 1].finish(after=h)))
    loss_part, dh = _loss_fwd(h, loss_target[0], name="loss")
    loss = lax.psum(loss_part[0, 0], ("x", "y", "c"))

    pipe = _ReducePipeline(core)
    dmods, dg1s, dg2s, dgains = [None] * depth, [None] * depth, [None] * depth, [None] * depth
    deps = []
    for l in reversed(range(depth)):
        dh_mid, dmod_f, dg2s[l], grads = _ffn_bwd(dh, saved[l], mods[l], g2s[l], full[l], deps, pipe.tick)
        deps = pipe.tick(dh_mid) + pipe.add(FFN_W, grads, l)
        dh, dmod_m, dg1s[l], dgains[l], grads = _mixer_bwd(
            dh_mid, saved[l], mods[l], g1s[l], gains[l], full[l], cos2, sin2, deps,
            lambda after, early, l=l: pipe.tick(after) + pipe.add(MIXER_W[1:], early, l))
        dmods[l] = jnp.concatenate(dmod_m + dmod_f, axis=1)
        deps = pipe.tick(dh) + pipe.add(MIXER_W[:1], grads, l)
    grad_x = dh[None]

    stacked = {}

    def update(items):
        for keys, l, sums, remote in items:
            for k, p_, r_ in zip(keys, sums, remote):
                stacked[k] = _adamw_sharded(p_, r_, chip, weights[k], moments_m[k], moments_v[k], l,
                                            stacked.get(k), [], name="adamw_" + k)

    ready = pipe.take_done()
    update([it for it in ready if it[0] != FFN_W])

    small = jnp.concatenate(
        dmods + dg1s + dg2s + [dgains[l][0] for l in range(depth)] + [dgains[l][1] for l in range(depth)], axis=1)
    small_all = _small_allgather(small, name="comm_gather_small")
    pipe.tick(small_all)
    update([it for it in ready if it[0] == FFN_W] + pipe.take_done())

    def pack(b, n1, n2, qn, kn):
        return jnp.concatenate([t_.reshape(1, -1) for t_ in (b, n1, n2, qn, kn)], axis=1)

    sg, sd, sm, sv_ = _adamw_replicated(small_all, pack(b_ada, norm1_g, norm2_g, qn_g, kn_g),
                                        pack(m_b_ada, m_norm1_g, m_norm2_g, m_qn_g, m_kn_g),
                                        pack(v_b_ada, v_norm1_g, v_norm2_g, v_qn_g, v_kn_g), name="adamw_replicated")

    def unpack(p):
        sizes = [depth * 6 * d, depth * d, depth * d, depth * HEAD_DIM, depth * HEAD_DIM]
        shapes = [b_ada.shape, norm1_g.shape, norm2_g.shape, qn_g.shape, kn_g.shape]
        out, off = [], 0
        for n, shp in zip(sizes, shapes):
            out.append(p[0, off:off + n].reshape(shp))
            off += n
        return dict(zip(("b_ada", "norm1_g", "norm2_g", "qn_g", "kn_g"), out))

    ug, ud, um, uv = unpack(sg), unpack(sd), unpack(sm), unpack(sv_)
    res = {k: dict(g=ug[k], d=ud[k], m=um[k], v=uv[k]) for k in ug}

    dmod_all = small_all[:, 0, :depth * 6 * d].reshape(N_DEV, depth, 6 * d)
    g_ada = None
    for l in range(depth):
        dm = lax.dynamic_slice(dmod_all[:, l, :], (0, me * ada_w), (N_DEV, ada_w))
        dm = jnp.concatenate([dm, jnp.zeros_like(dm)], axis=0).astype(BF16)
        g_ada = _mm(c_pad, dm, ta=True, name="mm_wgrad_ada", stack=(l, depth, g_ada))
    d_ada, m_ada, v_ada = _adamw_local(g_ada, w_ada, m_w_ada, v_w_ada, name="adamw_local")
    res["w_ada"] = dict(g=g_ada, d=d_ada, m=m_ada, v=v_ada)

    pipe.tick(d_ada)
    update(pipe.take_done())
    pipe.tick(d_ada, flush=True)
    update(pipe.take_done())
    for k, (g_, d_, m_, v_) in stacked.items():
        res[k] = dict(g=g_, d=d_, m=m_, v=v_)

    order = ("w_ada", "b_ada", "norm1_g", "norm2_g", "w_in", "qn_g", "kn_g", "w_branch_a", "w_branch_b", "w_out",
             "w_gate_up", "w_down")
    _ORDER["token"] = None
    return (loss, grad_x, *[res[k]["g"] for k in order], *[res[k]["d"] for k in order],
            *[res[k]["m"] for k in order], *[res[k]["v"] for k in order])
```

```python
import functools

import jax
import jax.numpy as jnp
from jax import lax
from jax.experimental import pallas as pl
from jax.experimental.pallas import tpu as pltpu

F32 = jnp.float32
BF16 = jnp.bfloat16

HEAD_DIM = 128
BLOCK = 128
DILATIONS = (1, 4, 16)
HEADS_PER_GROUP = 4
A_HEADS = 12
SB_HEADS = 4
GROUP_W = HEADS_PER_GROUP * HEAD_DIM
A_W = A_HEADS * HEAD_DIM
B_W = SB_HEADS * HEAD_DIM
OFF_QA, OFF_KA, OFF_VA = 0, A_W, 2 * A_W
OFF_QB, OFF_KB, OFF_VB = 3 * A_W, 3 * A_W + B_W, 3 * A_W + 2 * B_W
OFF_GATES = 3 * A_W + 3 * B_W
ROPE_THETA = 10000.0
EPS = 1e-6
ATT_SCALE = HEAD_DIM ** -0.5
MASKED = -1e30

ADAM_LR, ADAM_B1, ADAM_B2, ADAM_EPS, ADAM_WD, ADAM_STEP = 0.001, 0.9, 0.999, 1e-08, 0.01, 10

N_DEV = 8
N_CHIPS = 4
V7X_VMEM_LIMIT_BYTES = 56 * 1024 * 1024
ELEMWISE_BLOCK_BYTES = 2 * 1024 * 1024
MESH = pl.DeviceIdType.MESH

NN = (((1,), (0,)), ((), ()))
NT = (((1,), (1,)), ((), ()))
TN = (((0,), (0,)), ((), ()))


def _dot(a, b, dims=NN):
    return lax.dot_general(a, b, dims, preferred_element_type=F32)


def _tile(n, cap, mult=128):
    best = None
    for t in range(mult, min(n, cap) + 1, mult):
        if n % t == 0:
            best = t
    if best is None:
        assert n <= 2 * cap, (n, cap)
        return n
    return best


def _rows(r, c):
    return _tile(r, max(16, ELEMWISE_BLOCK_BYTES // (4 * c)), 16)


_ORDER = {"token": None}
TOKEN = jax.ShapeDtypeStruct((8, 128), F32)


def _take_token():
    prev = _ORDER["token"]
    return [] if prev is None else [prev]


def _pcall(body, *, name, out_shape, grid=None, in_specs=None, out_specs=None, scratch=(), aliases=None,
           prefetch=0, deps=()):
    single = not isinstance(out_shape, (tuple, list))
    out_shapes = [out_shape] if single else list(out_shape)
    out_specs = [out_specs] if single else list(out_specs)
    extra = list(deps) + _take_token()
    n_in, n_extra, n_out = prefetch + len(in_specs), len(extra), len(out_shapes)

    def wrapped(*refs):
        token = refs[n_in + n_extra + n_out]
        token[...] = jnp.zeros_like(token)
        return body(*refs[:n_in], *refs[n_in + n_extra:n_in + n_extra + n_out], *refs[n_in + n_extra + n_out + 1:])

    in_specs = list(in_specs) + [pl.BlockSpec(memory_space=pl.ANY)] * n_extra
    if grid is None:
        out_specs.append(pl.BlockSpec(memory_space=pltpu.VMEM))
    else:
        out_specs.append(pl.BlockSpec(TOKEN.shape, lambda *_: (0, 0)))
    kwargs = dict(name=name, out_shape=out_shapes + [TOKEN], input_output_aliases=aliases or {},
                  compiler_params=pltpu.CompilerParams(vmem_limit_bytes=V7X_VMEM_LIMIT_BYTES))
    if prefetch:
        call = pl.pallas_call(wrapped, grid_spec=pltpu.PrefetchScalarGridSpec(
            num_scalar_prefetch=prefetch, grid=grid, in_specs=in_specs, out_specs=out_specs,
            scratch_shapes=list(scratch)), **kwargs)
    else:
        if grid is not None:
            kwargs["grid"] = grid
        call = pl.pallas_call(wrapped, in_specs=in_specs, out_specs=out_specs, scratch_shapes=list(scratch), **kwargs)

    def run(*args):
        outs = call(*args, *extra)
        _ORDER["token"] = outs[-1]
        return outs[0] if single else tuple(outs[:-1])

    return run


def _mm(a, b, *, name, ta=False, tb=False, out_dtype=F32, caps=(1024, 1024, 3072), stack=None, deps=()):
    kdim, m = a.shape if ta else a.shape[::-1]
    n, k2 = b.shape if tb else b.shape[::-1]
    assert kdim == k2, (a.shape, b.shape, ta, tb)
    tm, tn, tk = _tile(m, caps[0]), _tile(n, caps[1]), _tile(kdim, caps[2])
    nk = kdim // tk
    dims = (((0 if ta else 1,), (1 if tb else 0,)), ((), ()))

    def body(*refs):
        a_ref, b_ref = refs[0], refs[1]
        part = _dot(a_ref[...].astype(BF16), b_ref[...].astype(BF16), dims)
        if nk == 1:
            o_ref = refs[-1]
            o_ref[...] = part.astype(o_ref.dtype)
            return
        o_ref, acc_ref = refs[-2], refs[-1]
        k = pl.program_id(2)

        @pl.when(k == 0)
        def _():
            acc_ref[...] = part

        @pl.when(k > 0)
        def _():
            acc_ref[...] += part

        @pl.when(k == nk - 1)
        def _():
            o_ref[...] = acc_ref[...].astype(o_ref.dtype)

    a_spec = (pl.BlockSpec((tk, tm), lambda i, j, k: (k, i)) if ta
              else pl.BlockSpec((tm, tk), lambda i, j, k: (i, k)))
    b_spec = (pl.BlockSpec((tn, tk), lambda i, j, k: (j, k)) if tb
              else pl.BlockSpec((tk, tn), lambda i, j, k: (k, j)))
    ins, in_specs, aliases = [a, b], [a_spec, b_spec], {}
    if stack is None:
        out_shape = jax.ShapeDtypeStruct((m, n), out_dtype)
        out_spec = pl.BlockSpec((tm, tn), lambda i, j, k: (i, j))
    else:
        layer, n_layers, buf = stack
        out_shape = jax.ShapeDtypeStruct((n_layers, m, n), out_dtype)
        out_spec = pl.BlockSpec((None, tm, tn), lambda i, j, k: (layer, i, j))
        if buf is not None:
            ins.append(buf)
            in_specs.append(pl.BlockSpec(memory_space=pl.ANY))
            aliases = {2: 0}
    scratch = [] if nk == 1 else [pltpu.VMEM((tm, tn), F32)]
    return _pcall(body, name=name, out_shape=out_shape, grid=(m // tm, n // tn, nk), in_specs=in_specs,
                  out_specs=out_spec, scratch=scratch, aliases=aliases, deps=deps)(*ins)


EPILOGUE_ROWS = 256


def _row_chunks(tm):
    return [slice(r, r + EPILOGUE_ROWS) for r in range(0, tm, EPILOGUE_ROWS)] if tm > EPILOGUE_ROWS else [slice(0, tm)]


def _mm_cat_k(a_lo, a_hi, b, *, name):
    m, f = a_lo.shape
    n = b.shape[0]
    tm, tn, tk = _tile(m, 1024), _tile(n, 1024), _tile(f, 3072)
    half = f // tk
    nk = 2 * half

    def body(lo_ref, hi_ref, b_ref, o_ref, acc_ref):
        k = pl.program_id(2)

        def accumulate(a_ref):
            part = _dot(a_ref[...], b_ref[...], NT)

            @pl.when(k == 0)
            def _():
                acc_ref[...] = part

            @pl.when(k > 0)
            def _():
                acc_ref[...] += part

        pl.when(k < half)(lambda: accumulate(lo_ref))
        pl.when(k >= half)(lambda: accumulate(hi_ref))

        @pl.when(k == nk - 1)
        def _():
            o_ref[...] = acc_ref[...]

    return _pcall(body, name=name, out_shape=jax.ShapeDtypeStruct((m, n), F32), grid=(m // tm, n // tn, nk),
                  in_specs=[pl.BlockSpec((tm, tk), lambda i, j, k: (i, jnp.minimum(k, half - 1))),
                            pl.BlockSpec((tm, tk), lambda i, j, k: (i, jnp.maximum(k - half, 0))),
                            pl.BlockSpec((tn, tk), lambda i, j, k: (j, k))],
                  out_specs=pl.BlockSpec((tm, tn), lambda i, j, k: (i, j)),
                  scratch=[pltpu.VMEM((tm, tn), F32)])(a_lo, a_hi, b)


def _mm_cat_n(a, b_lo, b_hi, *, name):
    s, m = a.shape
    f = b_lo.shape[1]
    tm, tn = _tile(m, 1024), _tile(f, 1024)
    half = f // tn

    def body(a_ref, lo_ref, hi_ref, o_ref):
        j = pl.program_id(1)

        @pl.when(j < half)
        def _():
            o_ref[...] = _dot(a_ref[...], lo_ref[...], TN).astype(BF16)

        @pl.when(j >= half)
        def _():
            o_ref[...] = _dot(a_ref[...], hi_ref[...], TN).astype(BF16)

    return _pcall(body, name=name, out_shape=jax.ShapeDtypeStruct((m, 2 * f), BF16), grid=(m // tm, 2 * half),
                  in_specs=[pl.BlockSpec((s, tm), lambda i, j: (0, i)),
                            pl.BlockSpec((s, tn), lambda i, j: (0, jnp.minimum(j, half - 1))),
                            pl.BlockSpec((s, tn), lambda i, j: (0, jnp.maximum(j - half, 0)))],
                  out_specs=pl.BlockSpec((tm, tn), lambda i, j: (i, j)))(a, b_lo, b_hi)


def _mm_resid_norm(a, w, h, gate, norm, *, name):
    s, kdim = a.shape
    d = w.shape[1]
    tk = _tile(kdim, 2048)
    nk = kdim // tk
    tm = _tile(s, 256 if nk == 1 else 512)

    def body(*refs):
        a_ref, w_ref, h_ref, gate_ref = refs[:4]
        outs = refs[7:] if norm is not None else refs[4:]

        def finish(rows, t):
            hn = h_ref[rows, :] + gate_ref[...] * t
            outs[0][rows, :] = hn
            outs[1][rows, :] = t.astype(BF16)
            if norm is not None:
                g_ref, sc_ref, sh_ref = refs[4:7]
                r = lax.rsqrt(jnp.mean(hn * hn, axis=-1, keepdims=True) + EPS)
                outs[2][rows, :] = (((hn * r) * g_ref[...]) * (1.0 + sc_ref[...]) + sh_ref[...]).astype(BF16)

        if nk == 1:
            for rows in _row_chunks(tm):
                finish(rows, _dot(a_ref[rows, :], w_ref[...]))
            return
        acc_ref = refs[-1]
        k = pl.program_id(1)

        @pl.when(k == 0)
        def _():
            acc_ref[...] = _dot(a_ref[...], w_ref[...])

        @pl.when(jnp.logical_and(k > 0, k < nk - 1))
        def _():
            acc_ref[...] += _dot(a_ref[...], w_ref[...])

        @pl.when(k == nk - 1)
        def _():
            for rows in _row_chunks(tm):
                finish(rows, acc_ref[rows, :] + _dot(a_ref[rows, :], w_ref[...]))

    row = pl.BlockSpec((tm, d), lambda i, k: (i, 0))
    vec = pl.BlockSpec((1, d), lambda i, k: (0, 0))
    in_specs = [pl.BlockSpec((tm, tk), lambda i, k: (i, k)), pl.BlockSpec((tk, d), lambda i, k: (k, 0)), row, vec]
    args = [a, w, h, gate]
    out_shape = [jax.ShapeDtypeStruct((s, d), F32), jax.ShapeDtypeStruct((s, d), BF16)]
    if norm is not None:
        in_specs += [vec, vec, vec]
        args += list(norm)
        out_shape.append(jax.ShapeDtypeStruct((s, d), BF16))
    outs = _pcall(body, name=name, out_shape=tuple(out_shape), grid=(s // tm, nk), in_specs=in_specs,
                  out_specs=(row,) * len(out_shape), scratch=[] if nk == 1 else [pltpu.VMEM((tm, d), F32)])(*args)
    return outs if norm is not None else (*outs, None)


def _mm_merge(o_a, o_b, w_a, w_b, proj, *, name):
    s = o_a.shape[0]
    d = w_a.shape[1]
    tm = _tile(s, 512)
    ga_blk = OFF_GATES // d

    def body(oa_ref, ob_ref, wa_ref, wb_ref, ga_ref, gb_ref, m_ref, ya_ref, yb_ref):
        for rows in _row_chunks(tm):
            ya, yb = _dot(oa_ref[rows, :], wa_ref[...]), _dot(ob_ref[rows, :], wb_ref[...])
            m_ref[rows, :] = (jax.nn.sigmoid(ga_ref[rows, :]) * ya
                              + jax.nn.sigmoid(gb_ref[rows, :]) * yb).astype(BF16)
            ya_ref[rows, :] = ya.astype(BF16)
            yb_ref[rows, :] = yb.astype(BF16)

    row = pl.BlockSpec((tm, d), lambda i: (i, 0))
    act = pl.BlockSpec((tm, o_a.shape[1]), lambda i: (i, 0))
    wspec = pl.BlockSpec(w_a.shape, lambda i: (0, 0))
    shp = jax.ShapeDtypeStruct((s, d), BF16)
    return _pcall(body, name=name, out_shape=(shp, shp, shp), grid=(s // tm,),
                  in_specs=[act, act, wspec, wspec, pl.BlockSpec((tm, d), lambda i: (i, ga_blk)),
                            pl.BlockSpec((tm, d), lambda i: (i, ga_blk + 1))],
                  out_specs=(row, row, row))(o_a, o_b, w_a, w_b, proj, proj)


def _mm_down_t_swiglu(df, w_down, g, u, *, name):
    s, d = df.shape
    f = w_down.shape[0]
    tm, tn = _tile(s, 1024), _tile(f, 512)

    def body(df_ref, w_ref, g_ref, u_ref, dg_ref, du_ref):
        w = w_ref[...]
        for rows in _row_chunks(tm):
            da = _dot(df_ref[rows, :], w, NT)
            gf = g_ref[rows, :].astype(F32)
            sg = jax.nn.sigmoid(gf)
            dg_ref[rows, :] = (da * u_ref[rows, :].astype(F32) * (sg * (1.0 + gf * (1.0 - sg)))).astype(BF16)
            du_ref[rows, :] = (da * (gf * sg)).astype(BF16)

    tile = pl.BlockSpec((tm, tn), lambda i, j: (i, j))
    shp = jax.ShapeDtypeStruct((s, f), BF16)
    return _pcall(body, name=name, out_shape=(shp, shp), grid=(s // tm, f // tn),
                  in_specs=[pl.BlockSpec((tm, d), lambda i, j: (i, 0)), pl.BlockSpec((tn, d), lambda i, j: (j, 0)),
                            tile, tile],
                  out_specs=(tile, tile))(df, w_down, g, u)


def _rmsmod_fwd(h, g, scale, shift, *, name, deps=()):
    s, d = h.shape
    ts = _rows(s, d)

    def body(h_ref, g_ref, sc_ref, sh_ref, u_ref):
        hf = h_ref[...]
        r = lax.rsqrt(jnp.mean(hf * hf, axis=-1, keepdims=True) + EPS)
        u_ref[...] = (((hf * r) * g_ref[...]) * (1.0 + sc_ref[...]) + sh_ref[...]).astype(BF16)

    row = pl.BlockSpec((ts, d), lambda i: (i, 0))
    vec = pl.BlockSpec((1, d), lambda i: (0, 0))
    return _pcall(body, name=name, out_shape=jax.ShapeDtypeStruct((s, d), BF16), grid=(s // ts,),
                  in_specs=[row, vec, vec, vec], out_specs=row, deps=deps)(h, g, scale, shift)


def _rmsmod_bwd(du, h, g, scale, dres, *, name):
    s, d = h.shape
    ts = _rows(s, d)

    def body(du_ref, h_ref, g_ref, sc_ref, dres_ref, dh_ref, dsh_ref, dsc_ref, dg_ref):
        @pl.when(pl.program_id(0) == 0)
        def _():
            dsh_ref[...] = jnp.zeros_like(dsh_ref)
            dsc_ref[...] = jnp.zeros_like(dsc_ref)
            dg_ref[...] = jnp.zeros_like(dg_ref)

        hf, duf, gain = h_ref[...], du_ref[...], g_ref[...]
        r = lax.rsqrt(jnp.mean(hf * hf, axis=-1, keepdims=True) + EPS)
        xh = hf * r
        dn = duf * (1.0 + sc_ref[...])
        dsh_ref[...] += jnp.sum(duf, axis=0, keepdims=True)
        dsc_ref[...] += jnp.sum(duf * (xh * gain), axis=0, keepdims=True)
        dg_ref[...] += jnp.sum(dn * xh, axis=0, keepdims=True)
        dxh = dn * gain
        dh_ref[...] = dres_ref[...] + r * (dxh - xh * jnp.mean(dxh * xh, axis=-1, keepdims=True))

    row = pl.BlockSpec((ts, d), lambda i: (i, 0))
    vec = pl.BlockSpec((1, d), lambda i: (0, 0))
    vshape = jax.ShapeDtypeStruct((1, d), F32)
    return _pcall(body, name=name, out_shape=(jax.ShapeDtypeStruct((s, d), F32), vshape, vshape, vshape),
                  grid=(s // ts,), in_specs=[row, row, vec, vec, row],
                  out_specs=(row, vec, vec, vec))(du, h, g, scale, dres)


def _resid_gate_bwd(dh, t, gate, *, name, deps=()):
    s, d = dh.shape
    ts = _rows(s, d)

    def body(dh_ref, t_ref, g_ref, dt_ref, dg_ref):
        @pl.when(pl.program_id(0) == 0)
        def _():
            dg_ref[...] = jnp.zeros_like(dg_ref)

        dhf = dh_ref[...]
        dt_ref[...] = (dhf * g_ref[...]).astype(BF16)
        dg_ref[...] += jnp.sum(dhf * t_ref[...], axis=0, keepdims=True)

    row = pl.BlockSpec((ts, d), lambda i: (i, 0))
    vec = pl.BlockSpec((1, d), lambda i: (0, 0))
    return _pcall(body, name=name,
                  out_shape=(jax.ShapeDtypeStruct((s, d), BF16), jax.ShapeDtypeStruct((1, d), F32)),
                  grid=(s // ts,), in_specs=[row, row, vec], out_specs=(row, vec), deps=deps)(dh, t, gate)


def _merge_bwd(dm, proj, y_a, y_b, *, name):
    s, d = y_a.shape
    ts = _rows(s, d)
    ga_blk = OFF_GATES // d

    def body(dm_ref, ga_ref, gb_ref, ya_ref, yb_ref, dya_ref, dyb_ref, dga_ref, dgb_ref):
        dmf = dm_ref[...]
        sa, sb = jax.nn.sigmoid(ga_ref[...]), jax.nn.sigmoid(gb_ref[...])
        dya_ref[...] = (dmf * sa).astype(BF16)
        dyb_ref[...] = (dmf * sb).astype(BF16)
        dga_ref[...] = (dmf * ya_ref[...] * (sa * (1.0 - sa))).astype(BF16)
        dgb_ref[...] = (dmf * yb_ref[...] * (sb * (1.0 - sb))).astype(BF16)

    row = pl.BlockSpec((ts, d), lambda i: (i, 0))
    ga = pl.BlockSpec((ts, d), lambda i: (i, ga_blk))
    gb = pl.BlockSpec((ts, d), lambda i: (i, ga_blk + 1))
    shp = jax.ShapeDtypeStruct((s, d), BF16)
    return _pcall(body, name=name, out_shape=(shp, shp, shp, shp), grid=(s // ts,),
                  in_specs=[row, ga, gb, row, row], out_specs=(row, row, row, row))(dm, proj, proj, y_a, y_b)


def _mm_swiglu(u2, w_gate_up, *, name):
    s, d = u2.shape
    f = w_gate_up.shape[1] // 2
    tm, tn = _tile(s, 1024), _tile(f, 512)
    nj = f // tn

    def body(x_ref, wg_ref, wu_ref, a_ref, g_ref, u_ref):
        for rows in _row_chunks(tm):
            x = x_ref[rows, :]
            gf, uf = _dot(x, wg_ref[...]), _dot(x, wu_ref[...])
            a_ref[rows, :] = ((gf * jax.nn.sigmoid(gf)) * uf).astype(BF16)
            g_ref[rows, :] = gf.astype(BF16)
            u_ref[rows, :] = uf.astype(BF16)

    out = pl.BlockSpec((tm, tn), lambda i, j: (i, j))
    shp = jax.ShapeDtypeStruct((s, f), BF16)
    return _pcall(body, name=name, out_shape=(shp, shp, shp), grid=(s // tm, nj),
                  in_specs=[pl.BlockSpec((tm, d), lambda i, j: (i, 0)), pl.BlockSpec((d, tn), lambda i, j: (0, j)),
                            pl.BlockSpec((d, tn), lambda i, j: (0, nj + j))],
                  out_specs=(out, out, out))(u2, w_gate_up, w_gate_up)


def _loss_fwd(y, tgt, *, name):
    s, d = y.shape
    ts = _rows(s, d)

    def body(y_ref, t_ref, l_ref, dy_ref):
        @pl.when(pl.program_id(0) == 0)
        def _():
            l_ref[...] = jnp.zeros_like(l_ref)

        e = y_ref[...] - t_ref[...]
        dy_ref[...] = e * (1.0 / d)
        per_tok = jnp.sum(e * e, axis=1, keepdims=True) * (1.0 / d)
        l_ref[...] += 0.5 * jnp.sum(per_tok, axis=0, keepdims=True)

    row = pl.BlockSpec((ts, d), lambda i: (i, 0))
    return _pcall(body, name=name,
                  out_shape=(jax.ShapeDtypeStruct((1, 128), F32), jax.ShapeDtypeStruct((s, d), F32)),
                  grid=(s // ts,), in_specs=[row, row],
                  out_specs=(pl.BlockSpec((1, 128), lambda i: (0, 0)), row))(y, tgt)


def _rope_tables(seq):
    inv = jnp.power(ROPE_THETA, -jnp.arange(0, HEAD_DIM, 2, dtype=F32) / HEAD_DIM)
    ang = jnp.arange(seq, dtype=F32)[:, None] * inv[None, :]
    cos, sin = jnp.cos(ang), jnp.sin(ang)
    return jnp.concatenate([cos, cos], axis=1), jnp.concatenate([-sin, sin], axis=1)


def _qkrope_fwd(proj, gains, cos2, sin2, *, name):
    s = proj.shape[0]
    ts = _rows(s, A_W)

    def body(x_ref, g_ref, c_ref, s_ref, o_ref):
        gain, cos, sin = g_ref[...], c_ref[...], s_ref[...]
        for h in range(A_HEADS):
            lanes = slice(h * HEAD_DIM, (h + 1) * HEAD_DIM)
            x = x_ref[:, lanes]
            y = (x * lax.rsqrt(jnp.mean(x * x, axis=-1, keepdims=True) + EPS)) * gain
            o_ref[:, lanes] = (y * cos + pltpu.roll(y, HEAD_DIM // 2, 1) * sin).astype(BF16)

    heads = pl.BlockSpec((ts, A_W), lambda i, j: (i, j))
    tab = pl.BlockSpec((ts, HEAD_DIM), lambda i, j: (i, 0))
    gain = pl.BlockSpec((None, 1, HEAD_DIM), lambda i, j: (j, 0, 0))
    return _pcall(body, name=name, out_shape=jax.ShapeDtypeStruct((s, 2 * A_W), BF16),
                  grid=(s // ts, 2), in_specs=[heads, gain, tab, tab], out_specs=heads)(
                      proj, gains, cos2, sin2)


def _qkrope_bwd(d_groups, proj, gains, which, cos2, sin2, *, name):
    s = proj.shape[0]
    ts = _rows(s, A_W)

    def body(d0_ref, d1_ref, d2_ref, x_ref, g_ref, c_ref, s_ref, dx_ref, dg_ref):
        @pl.when(pl.program_id(0) == 0)
        def _():
            dg_ref[...] = jnp.zeros_like(dg_ref)

        gain, cos, sin = g_ref[...], c_ref[...], s_ref[...]
        dg = jnp.zeros((1, HEAD_DIM), F32)
        for h in range(A_HEADS):
            lanes = slice(h * HEAD_DIM, (h + 1) * HEAD_DIM)
            slot = slice((h % HEADS_PER_GROUP) * HEAD_DIM, (h % HEADS_PER_GROUP + 1) * HEAD_DIM)
            dout = (d0_ref, d1_ref, d2_ref)[h // HEADS_PER_GROUP][:, slot]
            dy = dout * cos + pltpu.roll(dout * sin, HEAD_DIM // 2, 1)
            x = x_ref[:, lanes]
            r = lax.rsqrt(jnp.mean(x * x, axis=-1, keepdims=True) + EPS)
            xh = x * r
            dg = dg + jnp.sum(dy * xh, axis=0, keepdims=True)
            dxh = dy * gain
            dx_ref[:, lanes] = (r * (dxh - xh * jnp.mean(dxh * xh, axis=-1, keepdims=True))).astype(BF16)
        dg_ref[...] += dg

    group = pl.BlockSpec((ts, GROUP_W), lambda i: (i, 0))
    tab = pl.BlockSpec((ts, HEAD_DIM), lambda i: (i, 0))
    gain = pl.BlockSpec((None, 1, HEAD_DIM), lambda i: (which, 0, 0))
    return _pcall(body, name=name,
                  out_shape=(jax.ShapeDtypeStruct((s, A_W), BF16), jax.ShapeDtypeStruct((1, HEAD_DIM), F32)),
                  grid=(s // ts,),
                  in_specs=[group, group, group, pl.BlockSpec((ts, A_W), lambda i: (i, which)), gain, tab, tab],
                  out_specs=(pl.BlockSpec((ts, A_W), lambda i: (i, 0)), pl.BlockSpec((1, HEAD_DIM), lambda i: (0, 0))))(
                      *d_groups, proj, gains, cos2, sin2)


def _assemble(pieces, *, name):
    s = pieces[0].shape[0]
    widths = [p.shape[1] for p in pieces]
    total = sum(widths)
    ts = _rows(s, total // 2)

    def body(*refs):
        o_ref, off = refs[-1], 0
        for x_ref, w in zip(refs[:-1], widths):
            o_ref[:, off:off + w] = x_ref[...].astype(BF16)
            off += w

    return _pcall(body, name=name, out_shape=jax.ShapeDtypeStruct((s, total), BF16), grid=(s // ts,),
                  in_specs=[pl.BlockSpec((ts, w), lambda i: (i, 0)) for w in widths],
                  out_specs=pl.BlockSpec((ts, total), lambda i: (i, 0)))(*pieces)


def _block_rows(blk):
    if isinstance(blk, int):
        return pl.ds(blk * BLOCK, BLOCK)
    return pl.ds(pl.multiple_of(blk * BLOCK, BLOCK), BLOCK)


def _band_window(n, length):
    width = min(2 * BLOCK, length)
    row = lax.broadcasted_iota(jnp.int32, (BLOCK, width), 0)
    col = lax.broadcasted_iota(jnp.int32, (BLOCK, width), 1)
    if width == BLOCK:
        return pl.ds(0, BLOCK), col <= row
    first = n - 1 if isinstance(n, int) else jnp.maximum(n - 1, 0)
    first = max(first, 0) if isinstance(first, int) else first
    dist = row - col + (n - first) * BLOCK
    start = first * BLOCK if isinstance(first, int) else pl.multiple_of(first * BLOCK, BLOCK)
    return pl.ds(start, width), jnp.logical_and(dist >= 0, dist <= BLOCK)


def _dil_fwd(q_arr, k_arr, v_arr, offs, length, dil, *, name):
    nj, nb = dil * HEADS_PER_GROUP, length // BLOCK
    ju, nq = (HEADS_PER_GROUP, 2) if nb > 1 else (2 * HEADS_PER_GROUP, 1)
    qo, ko, vo = (off // ju for off in offs)
    assert all(off % ju == 0 for off in offs) and nb % nq == 0 and nj % ju == 0

    def body(q_ref, k_ref, v_ref, o_ref, l_ref):
        for qq in range(nq):
            qrows = slice(qq * BLOCK, (qq + 1) * BLOCK)
            rows, mask = _band_window(pl.program_id(1) * nq + qq, length)
            for cb in range(ju):
                lanes = slice(cb * HEAD_DIM, (cb + 1) * HEAD_DIM)
                sc = _dot(q_ref[qrows, lanes].astype(BF16), k_ref[rows, lanes].astype(BF16), NT) * ATT_SCALE
                sc = jnp.where(mask, sc, MASKED)
                m = sc.max(axis=-1, keepdims=True)
                p = jnp.exp(sc - m)
                den = jnp.sum(p, axis=-1, keepdims=True)
                acc = _dot(p.astype(BF16), v_ref[rows, lanes].astype(BF16))
                o_ref[qrows, lanes] = acc / den
                l_ref[qrows, lanes] = jnp.broadcast_to(m + jnp.log(den), (BLOCK, HEAD_DIM))

    qspec = pl.BlockSpec((nq * BLOCK, ju * HEAD_DIM), lambda j, n: (n, qo + j))
    kspec = pl.BlockSpec((length, ju * HEAD_DIM), lambda j, n: (0, ko + j))
    vspec = pl.BlockSpec((length, ju * HEAD_DIM), lambda j, n: (0, vo + j))
    ospec = pl.BlockSpec((nq * BLOCK, ju * HEAD_DIM), lambda j, n: (n, j))
    shp = jax.ShapeDtypeStruct((length, nj * HEAD_DIM), F32)
    return _pcall(body, name=name, out_shape=(shp, shp), grid=(nj // ju, nb // nq), in_specs=[qspec, kspec, vspec],
                  out_specs=(ospec, ospec))(q_arr, k_arr, v_arr)


def _dil_bwd(q_arr, k_arr, v_arr, offs, o, lse, do, dlse, length, dil, *, name):
    nj, nb = dil * HEADS_PER_GROUP, length // BLOCK
    ju = 2 * HEADS_PER_GROUP if length <= 4 * BLOCK else 2
    qo, ko, vo = (off // ju for off in offs)
    assert all(off % ju == 0 for off in offs)

    def body(q_ref, k_ref, v_ref, o_ref, l_ref, do_ref, dl_ref, dq_ref, dk_ref, dv_ref):
        dk_ref[...] = jnp.zeros_like(dk_ref)
        dv_ref[...] = jnp.zeros_like(dv_ref)

        def step(n, carry):
            qrows = _block_rows(n)
            rows, mask = _band_window(n, length)
            for cb in range(ju):
                lanes = slice(cb * HEAD_DIM, (cb + 1) * HEAD_DIM)
                q = q_ref[qrows, lanes].astype(BF16)
                dof = do_ref[qrows, lanes]
                dob = dof.astype(BF16)
                lse_c = l_ref[qrows, lanes][:, :1]
                shift = dl_ref[qrows, lanes][:, :1] - jnp.sum(dof * o_ref[qrows, lanes], axis=-1, keepdims=True)
                kk, vv = k_ref[rows, lanes].astype(BF16), v_ref[rows, lanes].astype(BF16)
                sc = _dot(q, kk, NT) * ATT_SCALE
                p = jnp.where(mask, jnp.exp(sc - lse_c), 0.0)
                ds = (p * (_dot(dob, vv, NT) + shift)).astype(BF16)
                dq_ref[qrows, lanes] = _dot(ds, kk) * ATT_SCALE
                dk_ref[rows, lanes] += _dot(ds, q, TN) * ATT_SCALE
                dv_ref[rows, lanes] += _dot(p.astype(BF16), dob, TN)
            return carry

        if nb == 1:
            step(0, 0)
        else:
            lax.fori_loop(0, nb, step, 0)

    def col(off):
        return pl.BlockSpec((length, ju * HEAD_DIM), lambda j: (0, off + j))

    shp = jax.ShapeDtypeStruct((length, nj * HEAD_DIM), F32)
    return _pcall(body, name=name, out_shape=(shp, shp, shp), grid=(nj // ju,),
                  in_specs=[col(qo), col(ko), col(vo), col(0), col(0), col(0), col(0)],
                  out_specs=(col(0), col(0), col(0)))(q_arr, k_arr, v_arr, o, lse, do, dlse)


def _combine_weights(l_refs):
    ls = [r[...] for r in l_refs]
    m = jnp.maximum(jnp.maximum(ls[0], ls[1]), ls[2])
    es = [jnp.exp(l - m) for l in ls]
    den = es[0] + es[1] + es[2]
    return [e / den for e in es]


def _combine_fwd(os_, lses, *, name):
    s = os_[0].shape[0]
    ts = _rows(s, GROUP_W)

    def body(o0, o1, o2, l0, l1, l2, out_ref):
        w = _combine_weights((l0, l1, l2))
        out_ref[...] = (w[0] * o0[...] + w[1] * o1[...] + w[2] * o2[...]).astype(BF16)

    row = pl.BlockSpec((ts, GROUP_W), lambda i: (i, 0))
    return _pcall(body, name=name, out_shape=jax.ShapeDtypeStruct((s, GROUP_W), BF16), grid=(s // ts,),
                  in_specs=[row] * 6, out_specs=row)(*os_, *lses)


def _combine_bwd(do_a, os_, lses, *, name, deps=()):
    s = do_a.shape[0]
    ts = _rows(s, GROUP_W)

    def body(d_ref, o0, o1, o2, l0, l1, l2, do0, do1, do2, dl0, dl1, dl2):
        w = _combine_weights((l0, l1, l2))
        d = d_ref[...]
        og = [o0[...], o1[...], o2[...]]
        oa = w[0] * og[0] + w[1] * og[1] + w[2] * og[2]
        ta = jnp.sum(d * oa, axis=-1, keepdims=True)
        for g, (do_ref, dl_ref) in enumerate(((do0, dl0), (do1, dl1), (do2, dl2))):
            do_ref[...] = w[g] * d
            dl_ref[...] = w[g] * (jnp.sum(d * og[g], axis=-1, keepdims=True) - ta)

    head = pl.BlockSpec((ts, HEAD_DIM), lambda i, h: (i, h))
    shp = jax.ShapeDtypeStruct((s, GROUP_W), F32)
    return _pcall(body, name=name, out_shape=(shp,) * 6, grid=(s // ts, HEADS_PER_GROUP),
                  in_specs=[head] * 7, out_specs=(head,) * 6, deps=deps)(do_a, *os_, *lses)


def _dot_exact(x, ones_mask):
    hi = x.astype(BF16)
    r1 = x - hi.astype(F32)
    mid = r1.astype(BF16)
    lo = (r1 - mid.astype(F32)).astype(BF16)
    return _dot(hi, ones_mask) + _dot(mid, ones_mask) + _dot(lo, ones_mask)


SB_QROWS = 2 * BLOCK
SB_UNROLL = 4
SB_HEADS_PER_STEP = 2
SB_LANES = [slice(hh * HEAD_DIM, (hh + 1) * HEAD_DIM) for hh in range(SB_HEADS_PER_STEP)]


def _sb_mask(j, i):
    row = lax.broadcasted_iota(jnp.int32, (SB_QROWS, BLOCK), 0)
    col = lax.broadcasted_iota(jnp.int32, (SB_QROWS, BLOCK), 1)
    return col + (j * BLOCK - i * SB_QROWS) < row


def _sb_steps(i):
    return ((i + 1) * (SB_QROWS // BLOCK) + SB_UNROLL - 1) // SB_UNROLL


def _sb_scores(q, kk, j, i, masked):
    mask = _sb_mask(j, i) if masked else None
    z = _dot(q, kk, NT) * ATT_SCALE
    sp = jnp.log(1.0 + jnp.exp(-jnp.abs(z)))
    log_beta = jnp.minimum(z, 0.0) - sp
    log_1mb = jnp.minimum(-z, 0.0) - sp
    if masked:
        log_1mb = jnp.where(mask, log_1mb, 0.0)
    return z, log_beta, log_1mb, mask


def _sb_weights(log_beta, log_1mb, mask, run, upper):
    a = jnp.exp(log_beta + (run + _dot_exact(log_1mb, upper)))
    return a if mask is None else jnp.where(mask, a, 0.0)


def _sb_peeled(nsteps, make_step, init, masked_first):
    if masked_first:
        return lax.fori_loop(1, nsteps, make_step(False), make_step(True)(0, init))
    return make_step(True)(nsteps - 1, lax.fori_loop(0, nsteps - 1, make_step(False), init))


def _tri(strict_lower):
    row = lax.broadcasted_iota(jnp.int32, (BLOCK, BLOCK), 0)
    col = lax.broadcasted_iota(jnp.int32, (BLOCK, BLOCK), 1)
    return ((row > col) if strict_lower else (row < col)).astype(BF16)


def _sb_fwd(proj, *, name):
    s = proj.shape[0]
    assert s % (BLOCK * SB_UNROLL) == 0 and s % SB_QROWS == 0

    def body(q_ref, k_ref, v_ref, o_ref):
        i = pl.program_id(1)
        qs = [q_ref[:, lanes].astype(BF16) for lanes in SB_LANES]
        upper = _tri(True)
        nsteps = _sb_steps(i)

        def make_step(masked):
            def step(t, carry):
                carry = list(carry)
                for b in reversed(range(SB_UNROLL)):
                    j = (nsteps - 1 - t) * SB_UNROLL + b
                    rows = _block_rows(j)
                    for hh, lanes in enumerate(SB_LANES):
                        acc, run = carry[hh]
                        _, log_beta, log_1mb, mask = _sb_scores(qs[hh], k_ref[rows, lanes].astype(BF16), j, i, masked)
                        a = _sb_weights(log_beta, log_1mb, mask, run, upper)
                        carry[hh] = (acc + _dot(a.astype(BF16), v_ref[rows, lanes].astype(BF16)),
                                     run + jnp.sum(log_1mb, axis=-1, keepdims=True))
                return tuple(carry)
            return step

        zero = (jnp.zeros((SB_QROWS, HEAD_DIM), F32), jnp.zeros((SB_QROWS, 1), F32))
        for lanes, (acc, _) in zip(SB_LANES, _sb_peeled(nsteps, make_step, (zero,) * SB_HEADS_PER_STEP, True)):
            o_ref[:, lanes] = acc.astype(BF16)

    width = SB_HEADS_PER_STEP * HEAD_DIM
    qb, kb, vb = (off // width for off in (OFF_QB, OFF_KB, OFF_VB))
    return _pcall(body, name=name, out_shape=jax.ShapeDtypeStruct((s, B_W), BF16),
                  grid=(SB_HEADS // SB_HEADS_PER_STEP, s // SB_QROWS),
                  in_specs=[pl.BlockSpec((SB_QROWS, width), lambda h, i: (i, qb + h)),
                            pl.BlockSpec((s, width), lambda h, i: (0, kb + h)),
                            pl.BlockSpec((s, width), lambda h, i: (0, vb + h))],
                  out_specs=pl.BlockSpec((SB_QROWS, width), lambda h, i: (i, h)))(proj, proj, proj)


def _sb_bwd(proj, do_b, *, name):
    s = proj.shape[0]
    assert s % (BLOCK * SB_UNROLL) == 0 and s % SB_QROWS == 0
    nkb = s // BLOCK

    def body(q_ref, k_ref, v_ref, do_ref, dq_ref, dk_ref, dv_ref, z_s, a_s):
        i = pl.program_id(1)

        @pl.when(i == 0)
        def _():
            dk_ref[...] = jnp.zeros_like(dk_ref)
            dv_ref[...] = jnp.zeros_like(dv_ref)

        qs = [q_ref[:, lanes].astype(BF16) for lanes in SB_LANES]
        dobs = [do_ref[:, lanes].astype(BF16) for lanes in SB_LANES]
        upper, lower = _tri(True), _tri(False)
        nsteps = _sb_steps(i)

        def make_recompute(masked):
            def recompute(t, runs):
                runs = list(runs)
                for b in reversed(range(SB_UNROLL)):
                    j = (nsteps - 1 - t) * SB_UNROLL + b
                    rows = _block_rows(j)
                    for hh, lanes in enumerate(SB_LANES):
                        z, log_beta, log_1mb, mask = _sb_scores(qs[hh], k_ref[rows, lanes].astype(BF16), j, i, masked)
                        z_s[hh, j] = z
                        a_s[hh, j] = _sb_weights(log_beta, log_1mb, mask, runs[hh], upper)
                        runs[hh] = runs[hh] + jnp.sum(log_1mb, axis=-1, keepdims=True)
                return tuple(runs)
            return recompute

        _sb_peeled(nsteps, make_recompute, (jnp.zeros((SB_QROWS, 1), F32),) * SB_HEADS_PER_STEP, True)

        def make_grads(masked):
            def grads(t, carry):
                carry = list(carry)
                for b in range(SB_UNROLL):
                    j = t * SB_UNROLL + b
                    rows = _block_rows(j)
                    for hh, lanes in enumerate(SB_LANES):
                        dq, run = carry[hh]
                        kk, vv = k_ref[rows, lanes].astype(BF16), v_ref[rows, lanes].astype(BF16)
                        z, a = z_s[hh, j], a_s[hh, j]
                        de = _dot(dobs[hh], vv, NT) * a
                        beta = jax.nn.sigmoid(z)
                        if masked:
                            beta = jnp.where(_sb_mask(j, i), beta, 0.0)
                        dz = (de * jax.nn.sigmoid(-z) - beta * (run + _dot_exact(de, lower))).astype(BF16)
                        dk_ref[rows, lanes] += _dot(dz, qs[hh], TN) * ATT_SCALE
                        dv_ref[rows, lanes] += _dot(a.astype(BF16), dobs[hh], TN)
                        carry[hh] = (dq + _dot(dz, kk), run + jnp.sum(de, axis=-1, keepdims=True))
                return tuple(carry)
            return grads

        zero = (jnp.zeros((SB_QROWS, HEAD_DIM), F32), jnp.zeros((SB_QROWS, 1), F32))
        for lanes, (dq, _) in zip(SB_LANES, _sb_peeled(nsteps, make_grads, (zero,) * SB_HEADS_PER_STEP, False)):
            dq_ref[:, lanes] = dq * ATT_SCALE

    width = SB_HEADS_PER_STEP * HEAD_DIM
    qb, kb, vb = (off // width for off in (OFF_QB, OFF_KB, OFF_VB))
    blk = pl.BlockSpec((SB_QROWS, width), lambda h, i: (i, h))
    full = pl.BlockSpec((s, width), lambda h, i: (0, h))
    shp = jax.ShapeDtypeStruct((s, B_W), F32)
    saved = pltpu.VMEM((SB_HEADS_PER_STEP, nkb, SB_QROWS, BLOCK), F32)
    return _pcall(body, name=name, out_shape=(shp, shp, shp), grid=(SB_HEADS // SB_HEADS_PER_STEP, s // SB_QROWS),
                  in_specs=[pl.BlockSpec((SB_QROWS, width), lambda h, i: (i, qb + h)),
                            pl.BlockSpec((s, width), lambda h, i: (0, kb + h)),
                            pl.BlockSpec((s, width), lambda h, i: (0, vb + h)), blk],
                  out_specs=(blk, full, full), scratch=[saved, saved])(proj, proj, proj, do_b)


def _coords():
    return lax.axis_index("x"), lax.axis_index("y"), lax.axis_index("c")


def _flip(v, bit):
    return 1 - v if bit else v


def _shard_of(ref, axis, idx, size):
    if axis == 0:
        sl = pl.ds(pl.multiple_of(idx * size, 16), size)
        return ref.at[sl, :] if len(ref.shape) == 2 else ref.at[:, sl, :]
    sl = pl.ds(pl.multiple_of(idx * size, 128), size)
    return ref.at[:, sl] if len(ref.shape) == 2 else ref.at[:, :, sl]


def _small_allgather(v, *, name, silu=False, deps=()):
    n = v.shape[1]

    def body(v_ref, out_ref, send_sems, recv_sems):
        x, y, c = _coords()
        me = 4 * x + 2 * y + c
        val = v_ref[...]
        out_ref[me] = val * jax.nn.sigmoid(val) if silu else val
        copies = []
        for k in range(1, N_DEV):
            peer = (_flip(x, k & 4), _flip(y, k & 2), _flip(c, k & 1))
            copies.append(pltpu.make_async_remote_copy(
                src_ref=out_ref.at[me], dst_ref=out_ref.at[me], send_sem=send_sems.at[k - 1],
                recv_sem=recv_sems.at[k - 1], device_id=peer, device_id_type=MESH))
        for cp in copies:
            cp.start()
        for cp in copies:
            cp.wait_recv()
        for cp in copies:
            cp.wait_send()

    return _pcall(body, name=name, out_shape=jax.ShapeDtypeStruct((N_DEV, 1, n), F32),
                  in_specs=[pl.BlockSpec(memory_space=pltpu.VMEM)], out_specs=pl.BlockSpec(memory_space=pltpu.VMEM),
                  scratch=[pltpu.SemaphoreType.DMA((N_DEV - 1,)), pltpu.SemaphoreType.DMA((N_DEV - 1,))],
                  deps=deps)(v)


def _cast_place(w, layer, axis, me, *, name):
    _, r, c = w.shape
    tr = _rows(r, c)
    nrt = r // tr

    def body(me_ref, w_ref, o_ref):
        o_ref[...] = w_ref[...].astype(BF16)

    wspec = pl.BlockSpec((None, tr, c), lambda i, me_ref: (layer, i, 0))
    if axis == 0:
        ospec = pl.BlockSpec((tr, c), lambda i, me_ref: (me_ref[0] * nrt + i, 0))
        shape = (r * N_DEV, c)
    else:
        ospec = pl.BlockSpec((tr, c), lambda i, me_ref: (i, me_ref[0]))
        shape = (r, c * N_DEV)
    return _pcall(body, name=name, out_shape=jax.ShapeDtypeStruct(shape, BF16), grid=(nrt,), in_specs=[wspec],
                  out_specs=ospec, prefetch=1)(me, w)


def _pair_sum(grad, sib, core, axis, *, name):
    _, r, c = sib.shape
    tr = _rows(r, c)
    nrt = r // tr

    def body(core_ref, g_ref, s_ref, o_ref):
        o_ref[...] = (g_ref[...].astype(F32) + s_ref[...].astype(F32)).astype(BF16)

    if axis == 0:
        gspec = pl.BlockSpec((tr, c), lambda q, i, core_ref: ((2 * q + core_ref[0]) * nrt + i, 0))
    else:
        gspec = pl.BlockSpec((tr, c), lambda q, i, core_ref: (i, 2 * q + core_ref[0]))
    sspec = pl.BlockSpec((None, tr, c), lambda q, i, core_ref: (q, i, 0))
    return _pcall(body, name=name, out_shape=jax.ShapeDtypeStruct(sib.shape, BF16), grid=(N_CHIPS, nrt),
                  in_specs=[gspec, sspec], out_specs=sspec, prefetch=1)(core, grad, sib)


ANY_SPEC = pl.BlockSpec(memory_space=pl.ANY)
SEM_SPEC = pl.BlockSpec(memory_space=pltpu.SEMAPHORE)
SPLIT_PARAMS = dict(has_side_effects=pltpu.SideEffectType.DATAFLOW_SIDE_EFFECTING)


def _split_start(copies_fn, buffers, sem_shape, after, *, name):
    n = len(buffers)
    rows, cols = sem_shape
    ns = rows * cols
    extra = ([] if after is None else [after]) + _take_token()

    def body(*refs):
        sems = refs[n + len(extra):n + len(extra) + 2 * ns]
        for cp in copies_fn(refs[:n], _sem_rows(sems[:ns], cols), _sem_rows(sems[ns:], cols)):
            cp.start()
        refs[-1][...] = jnp.zeros_like(refs[-1])

    sem = pltpu.SemaphoreType.DMA(())
    outs = pl.pallas_call(
        body, name=name,
        out_shape=((sem,) * (2 * ns) + tuple(jax.ShapeDtypeStruct(b.shape, b.dtype) for b in buffers) + (TOKEN,)),
        in_specs=(ANY_SPEC,) * (n + len(extra)),
        out_specs=(SEM_SPEC,) * (2 * ns) + (ANY_SPEC,) * n + (pl.BlockSpec(memory_space=pltpu.VMEM),),
        input_output_aliases={i: 2 * ns + i for i in range(n)},
        compiler_params=pltpu.CompilerParams(**SPLIT_PARAMS))(*buffers, *extra)
    _ORDER["token"] = outs[-1]
    return list(outs[:ns]), list(outs[ns:2 * ns]), list(outs[2 * ns:2 * ns + n]), outs[-1]


def _split_wait(copies_fn, send_sems, recv_sems, buffers, after, sem_rows, *, name):
    n, ns = len(buffers), len(send_sems)
    cols = ns // sem_rows
    extra = ([] if after is None else [after]) + _take_token()

    def body(*refs):
        sems = refs[n:n + 2 * ns]
        copies = copies_fn(refs[:n], _sem_rows(sems[:ns], cols), _sem_rows(sems[ns:], cols))
        for cp in copies:
            cp.wait_send()
        for cp in copies:
            cp.wait_recv()
        refs[-1][...] = jnp.zeros_like(refs[-1])

    outs = pl.pallas_call(
        body, name=name, out_shape=tuple(jax.ShapeDtypeStruct(b.shape, b.dtype) for b in buffers) + (TOKEN,),
        in_specs=(ANY_SPEC,) * n + (SEM_SPEC,) * (2 * ns) + (ANY_SPEC,) * len(extra),
        out_specs=(ANY_SPEC,) * n + (pl.BlockSpec(memory_space=pltpu.VMEM),),
        input_output_aliases={i: i for i in range(n)},
        compiler_params=pltpu.CompilerParams(**SPLIT_PARAMS))(*buffers, *send_sems, *recv_sems, *extra)
    _ORDER["token"] = outs[-1]
    return list(outs[:n])


def _sem_rows(sems, cols):
    return [sems[i:i + cols] for i in range(0, len(sems), cols)]


def _empty_hbm(shape, dtype):
    return pltpu.with_memory_space_constraint(lax.empty(shape, dtype), pltpu.HBM)


class _SplitGather:
    def __init__(self, fulls, axes, tag):
        self.axes, self.tag, self.nt = list(axes), tag, len(fulls)
        self.sizes = [f.shape[ax] // N_DEV for f, ax in zip(fulls, axes)]
        self.fulls = list(fulls)

    def _slot(self, ref, t, dev):
        return _shard_of(ref, self.axes[t], 4 * dev[0] + 2 * dev[1] + dev[2], self.sizes[t])

    def _first_copies(self, refs, send_sems, recv_sems):
        x, y, c = _coords()
        peers = [(x, y, 1 - c), (1 - x, y, c), (x, 1 - y, c), (1 - x, 1 - y, c)]
        return [pltpu.make_async_remote_copy(
            src_ref=self._slot(refs[t], t, (x, y, c)), dst_ref=self._slot(refs[t], t, (x, y, c)),
            send_sem=send_sems[t][k], recv_sem=recv_sems[t][k], device_id=peer, device_id_type=MESH)
            for t in range(self.nt) for k, peer in enumerate(peers)]

    def _forward_copies(self, refs, send_sems, recv_sems):
        x, y, c = _coords()
        chips = [(1 - x, y), (x, 1 - y), (1 - x, 1 - y)]
        return [pltpu.make_async_remote_copy(
            src_ref=self._slot(refs[t], t, (*chip, c)), dst_ref=self._slot(refs[t], t, (*chip, c)),
            send_sem=send_sems[t][j], recv_sem=recv_sems[t][j], device_id=(x, y, 1 - c), device_id_type=MESH)
            for t in range(self.nt) for j, chip in enumerate(chips)]

    def first(self, after):
        self.s1, self.r1, self.fulls, token = _split_start(
            self._first_copies, self.fulls, (self.nt, 4), after, name=f"comm_gather1_start_{self.tag}")
        return token

    def forward(self, after):
        bufs = _split_wait(self._first_copies, self.s1, self.r1, self.fulls, after, self.nt,
                           name=f"comm_gather1_wait_{self.tag}")
        self.s2, self.r2, self.fulls, token = _split_start(
            self._forward_copies, bufs, (self.nt, 3), after, name=f"comm_gather2_start_{self.tag}")
        return token

    def finish(self, after):
        return _split_wait(self._forward_copies, self.s2, self.r2, self.fulls, after, self.nt,
                           name=f"comm_gather2_wait_{self.tag}")


class _SplitPairExchange:
    def __init__(self, grads, axes, tag):
        self.nt, self.tag, self.axes = len(grads), tag, list(axes)
        self.grads = list(grads)
        self.sizes = [g.shape[ax] // N_DEV for g, ax in zip(grads, axes)]

    def _copies(self, refs, send_sems, recv_sems):
        nt = self.nt
        x, y, c = _coords()
        return [pltpu.make_async_remote_copy(
            src_ref=_shard_of(refs[t], self.axes[t], 2 * q + 1 - c, self.sizes[t]), dst_ref=refs[nt + t].at[q],
            send_sem=send_sems[t][q], recv_sem=recv_sems[t][q], device_id=(x, y, 1 - c), device_id_type=MESH)
            for t in range(nt) for q in range(N_CHIPS)]

    def start(self):
        landing = []
        for g, ax in zip(self.grads, self.axes):
            dims = list(g.shape)
            dims[ax] //= N_DEV
            landing.append(_empty_hbm((N_CHIPS, *dims), g.dtype))
        self.s, self.r, self.bufs, token = _split_start(
            self._copies, self.grads + landing, (self.nt, N_CHIPS), None,
            name=f"comm_rs_pair_start_{self.tag}")
        return token

    def finish(self, after):
        bufs = _split_wait(self._copies, self.s, self.r, self.bufs, after, self.nt,
                           name=f"comm_rs_pair_wait_{self.tag}")
        return bufs[:self.nt], bufs[self.nt:]


class _ReducePipeline:
    def __init__(self, core):
        self.core, self.items, self.done, self.now = core, [], [], 0

    def add(self, keys, grads, layer):
        axes = [SHARD_AXIS[k] for k in keys]
        pair = _SplitPairExchange([grads[k] for k in keys], axes, f"{keys[0]}{layer}")
        token = pair.start()
        self.items.append(dict(keys=keys, layer=layer, axes=axes, pair=pair, state="pair", since=self.now))
        return [token]

    def tick(self, after, flush=False):
        self.now += 1
        deps = []
        for it in self.items:
            if it["state"] == "pair" and it["since"] < self.now:
                grads, sib = it["pair"].finish(after)
                sums = [_pair_sum(g, s_, self.core, ax, name="pair_sum_" + k)
                        for k, g, s_, ax in zip(it["keys"], grads, sib, it["axes"])]
                it["chip"] = _SplitChipExchange(sums, f"{it['keys'][0]}{it['layer']}")
                deps.append(it["chip"].start())
                it.update(state="chip", since=self.now)
            elif it["state"] == "chip" and (flush or self.now - it["since"] >= 2):
                sums, remote = it["chip"].finish(after)
                self.done.append((it["keys"], it["layer"], sums, remote))
                it["state"] = "done"
        return deps

    def take_done(self):
        out, self.done = self.done, []
        return out


class _SplitChipExchange:
    def __init__(self, sums, tag):
        self.nt, self.tag = len(sums), tag
        self.sums = list(sums)

    def _copies(self, refs, send_sems, recv_sems):
        nt = self.nt
        x, y, c = _coords()
        copies = []
        for t in range(nt):
            for k in range(1, N_CHIPS):
                px, py = _flip(x, k & 2), _flip(y, k & 1)
                copies.append(pltpu.make_async_remote_copy(
                    src_ref=refs[t].at[2 * px + py], dst_ref=refs[nt + t].at[k - 1], send_sem=send_sems[t][k - 1],
                    recv_sem=recv_sems[t][k - 1], device_id=(px, py, c), device_id_type=MESH))
        return copies

    def start(self):
        landing = [_empty_hbm((N_CHIPS - 1,) + s.shape[1:], s.dtype) for s in self.sums]
        self.s, self.r, self.bufs, token = _split_start(
            self._copies, self.sums + landing, (self.nt, N_CHIPS - 1), None,
            name=f"comm_rs_chip_start_{self.tag}")
        return token

    def finish(self, after):
        bufs = _split_wait(self._copies, self.s, self.r, self.bufs, after, self.nt,
                           name=f"comm_rs_chip_wait_{self.tag}")
        return bufs[:self.nt], bufs[self.nt:]


def _adam_math(g, w, m, v):
    m2 = ADAM_B1 * m + (1.0 - ADAM_B1) * g
    v2 = ADAM_B2 * v + (1.0 - ADAM_B2) * (g * g)
    m_hat = m2 / (1.0 - ADAM_B1 ** ADAM_STEP)
    v_hat = v2 / (1.0 - ADAM_B2 ** ADAM_STEP)
    delta = -ADAM_LR * (m_hat / (jnp.sqrt(v_hat) + ADAM_EPS) + ADAM_WD * w)
    return delta, m2, v2


def _adamw_sharded(chip_sums, remote, chip, w, m, v, layer, prev, deps, *, name):
    nl, r, c = w.shape
    tr = _rows(r, c)

    def body(*refs):
        p_ref, r0_ref, r1_ref, r2_ref, w_ref, m_ref, v_ref = refs[1:8]
        g_out, d_out, m_out, v_out = refs[-4:]
        g = ((p_ref[...].astype(F32) + r0_ref[...].astype(F32)) + r1_ref[...].astype(F32)) + r2_ref[...].astype(F32)
        g_out[...] = g
        d_out[...], m_out[...], v_out[...] = _adam_math(g, w_ref[...], m_ref[...], v_ref[...])

    pspec = pl.BlockSpec((None, tr, c), lambda i, chip_ref: (chip_ref[0], i, 0))

    def rspec(k):
        return pl.BlockSpec((None, tr, c), lambda i, chip_ref: (k, i, 0))

    wspec = pl.BlockSpec((None, tr, c), lambda i, chip_ref: (layer, i, 0))
    in_specs = [pspec, rspec(0), rspec(1), rspec(2), wspec, wspec, wspec]
    args = [chip, chip_sums, remote, remote, remote, w, m, v]
    aliases = {}
    if prev is not None:
        in_specs += [pl.BlockSpec(memory_space=pl.ANY)] * 4
        aliases = {len(args) + i: i for i in range(4)}
        args += list(prev)
    shp = jax.ShapeDtypeStruct(w.shape, F32)
    return _pcall(body, name=name, out_shape=(shp,) * 4, grid=(r // tr,), in_specs=in_specs, out_specs=(wspec,) * 4,
                  aliases=aliases, prefetch=1, deps=deps)(*args)


def _adamw_local(g, w, m, v, *, name):
    nl, r, c = w.shape
    tr = _rows(r, c)

    def body(g_ref, w_ref, m_ref, v_ref, d_out, m_out, v_out):
        d_out[...], m_out[...], v_out[...] = _adam_math(g_ref[...], w_ref[...], m_ref[...], v_ref[...])

    spec = pl.BlockSpec((None, tr, c), lambda l, i: (l, i, 0))
    shp = jax.ShapeDtypeStruct(w.shape, F32)
    return _pcall(body, name=name, out_shape=(shp,) * 3, grid=(nl, r // tr), in_specs=[spec] * 4,
                  out_specs=(spec,) * 3)(g, w, m, v)


def _adamw_replicated(parts, w, m, v, *, name):
    n = w.shape[1]

    def body(p_ref, w_ref, m_ref, v_ref, g_out, d_out, m_out, v_out):
        g = p_ref[0]
        for k in range(1, N_DEV):
            g = g + p_ref[k]
        g_out[...] = g
        d_out[...], m_out[...], v_out[...] = _adam_math(g, w_ref[...], m_ref[...], v_ref[...])

    vm = pl.BlockSpec(memory_space=pltpu.VMEM)
    shp = jax.ShapeDtypeStruct((1, n), F32)
    return _pcall(body, name=name, out_shape=(shp,) * 4, in_specs=[vm] * 4, out_specs=(vm,) * 4)(parts, w, m, v)


def _group_views(qk, proj, g, dil, seq):
    if dil == 1:
        return (qk, qk, proj), (0, A_HEADS, 2 * A_HEADS)
    length = seq // dil
    lo = g * GROUP_W
    q = qk[:, lo:lo + GROUP_W].reshape(length, dil * GROUP_W)
    k = qk[:, A_W + lo:A_W + lo + GROUP_W].reshape(length, dil * GROUP_W)
    v = proj[:, OFF_VA + lo:OFF_VA + lo + GROUP_W].astype(BF16).reshape(length, dil * GROUP_W)
    return (q, k, v), (0, 0, 0)


def _mod_rows(mod, d):
    return [mod[:, i * d:(i + 1) * d] for i in range(6)]


MIXER_W = ("w_in", "w_branch_a", "w_branch_b", "w_out")
FFN_W = ("w_gate_up", "w_down")
SHARD_AXIS = {"w_in": 1, "w_branch_a": 1, "w_branch_b": 1, "w_out": 0, "w_gate_up": 1, "w_down": 0}


def _norm_args(mod, gain, which, d):
    rows = _mod_rows(mod, d)
    return gain, rows[3 * which + 1], rows[3 * which]


def _mixer_fwd_a(h, u, gains, w_in, cos2, sin2):
    seq = h.shape[0]
    proj = _mm(u, w_in, name="mm_in")
    qk = _qkrope_fwd(proj, gains, cos2, sin2, name="qkrope_fwd")
    os_, lses = [], []
    for g, dil in enumerate(DILATIONS):
        arrs, offs = _group_views(qk, proj, g, dil, seq)
        o, lse = _dil_fwd(*arrs, offs, seq // dil, dil, name=f"dil_fwd_{dil}")
        os_.append(o.reshape(seq, GROUP_W))
        lses.append(lse.reshape(seq, GROUP_W))
    o_a = _combine_fwd(os_, lses, name="combine_fwd")
    o_b = _sb_fwd(proj, name="sb_fwd")
    return dict(h_in=h, u=u, proj=proj, qk=qk, os=os_, lses=lses, o_a=o_a, o_b=o_b)


def _mixer_fwd_b(sv, mod, g2, wts):
    d = sv["h_in"].shape[1]
    merged, y_a, y_b = _mm_merge(sv["o_a"], sv["o_b"], wts["w_branch_a"], wts["w_branch_b"], sv["proj"],
                                 name="mm_branch")
    h_mid, t, u2 = _mm_resid_norm(merged, wts["w_out"], sv["h_in"], _mod_rows(mod, d)[2], _norm_args(mod, g2, 1, d),
                                  name="mm_out")
    sv.update(y_a=y_a, y_b=y_b, merged=merged, t=t, h_mid=h_mid, u2=u2)
    return h_mid


def _ffn_fwd_a(sv, w_gate_up):
    a, g, u = _mm_swiglu(sv["u2"], w_gate_up, name="mm_gate_up")
    sv.update(g=g, up=u, a=a)
    return a


def _ffn_fwd_b(sv, mod, w_down, next_norm):
    d = sv["h_mid"].shape[1]
    h_out, sv["f"], u_next = _mm_resid_norm(sv["a"], w_down, sv["h_mid"], _mod_rows(mod, d)[5], next_norm,
                                            name="mm_down")
    return h_out, u_next


def _wgrad(act, dout, key):
    return _mm(act, dout, ta=True, out_dtype=BF16, name="mm_wgrad_" + key)


def _ffn_bwd(dh, sv, mod, g2, wts, deps, hook):
    d = dh.shape[1]
    sc2, ga2 = _mod_rows(mod, d)[4:6]
    df, dgate2 = _resid_gate_bwd(dh, sv["f"], ga2, name="resid_gate_bwd", deps=deps)
    dg, dup = _mm_down_t_swiglu(df, wts["w_down"], sv["g"], sv["up"], name="mm_down_t")
    grads = {"w_down": _wgrad(sv["a"], df, "w_down")}
    hook(dup)
    du2 = _mm_cat_k(dg, dup, wts["w_gate_up"], name="mm_gate_up_t")
    grads["w_gate_up"] = _mm_cat_n(sv["u2"], dg, dup, name="mm_wgrad_w_gate_up")
    dh_mid, dsh2, dsc2, dg2 = _rmsmod_bwd(du2, sv["h_mid"], g2, sc2, dh, name="rmsmod_bwd")
    return dh_mid, [dsh2, dsc2, dgate2], dg2, grads


def _mixer_bwd(dh_mid, sv, mod, g1, gains, wts, cos2, sin2, deps, hook):
    seq, d = dh_mid.shape
    sc1, ga1 = _mod_rows(mod, d)[1:3]
    dt, dgate1 = _resid_gate_bwd(dh_mid, sv["t"], ga1, name="resid_gate_bwd", deps=deps)
    dmerged = _mm(dt, wts["w_out"], tb=True, name="mm_out_t")
    grads = {"w_out": _wgrad(sv["merged"], dt, "w_out")}
    dy_a, dy_b, dga, dgb = _merge_bwd(dmerged, sv["proj"], sv["y_a"], sv["y_b"], name="merge_bwd")
    do_a = _mm(dy_a, wts["w_branch_a"], tb=True, name="mm_branch_t")
    do_b = _mm(dy_b, wts["w_branch_b"], tb=True, name="mm_branch_t")
    grads["w_branch_a"] = _wgrad(sv["o_a"], dy_a, "w_branch_a")
    grads["w_branch_b"] = _wgrad(sv["o_b"], dy_b, "w_branch_b")
    dqb, dkb, dvb = _sb_bwd(sv["proj"], do_b, name="sb_bwd")
    comb = _combine_bwd(do_a, sv["os"], sv["lses"], name="combine_bwd", deps=hook(dqb, grads))
    grads = {}
    dos, dls = comb[:3], comb[3:]
    dqs, dks, dvs = [], [], []
    for g, dil in enumerate(DILATIONS):
        length = seq // dil
        arrs, offs = _group_views(sv["qk"], sv["proj"], g, dil, seq)
        view = (length, dil * GROUP_W)
        dq, dk, dv = _dil_bwd(*arrs, offs, sv["os"][g].reshape(view), sv["lses"][g].reshape(view),
                              dos[g].reshape(view), dls[g].reshape(view), length, dil, name=f"dil_bwd_{dil}")
        dqs.append(dq.reshape(seq, GROUP_W))
        dks.append(dk.reshape(seq, GROUP_W))
        dvs.append(dv.reshape(seq, GROUP_W))
    dq_pre, dqn = _qkrope_bwd(dqs, sv["proj"], gains, 0, cos2, sin2, name="qkrope_bwd")
    dk_pre, dkn = _qkrope_bwd(dks, sv["proj"], gains, 1, cos2, sin2, name="qkrope_bwd")
    dgains = jnp.stack([dqn, dkn])
    dproj = _assemble([dq_pre, dk_pre] + dvs + [dqb, dkb, dvb, dga, dgb], name="assemble_dproj")
    du = _mm(dproj, wts["w_in"], tb=True, name="mm_in_t")
    grads["w_in"] = _wgrad(sv["u"], dproj, "w_in")
    dh_in, dsh1, dsc1, dg1 = _rmsmod_bwd(du, sv["h_in"], g1, sc1, dh_mid, name="rmsmod_bwd")
    return dh_in, [dsh1, dsc1, dgate1], dg1, dgains, grads


def kernel(x, c, w_ada, b_ada, norm1_g, norm2_g, w_in, qn_g, kn_g, w_branch_a, w_branch_b, w_out, w_gate_up, w_down, loss_target, m_w_ada, m_b_ada, m_norm1_g, m_norm2_g, m_w_in, m_qn_g, m_kn_g, m_w_branch_a, m_w_branch_b, m_w_out, m_w_gate_up, m_w_down, v_w_ada, v_b_ada, v_norm1_g, v_norm2_g, v_w_in, v_qn_g, v_kn_g, v_w_branch_a, v_w_branch_b, v_w_out, v_w_gate_up, v_w_down):
    _ORDER["token"] = None
    seq, d = x.shape[1], x.shape[2]
    depth = w_in.shape[0]
    weights = dict(w_in=w_in, w_branch_a=w_branch_a, w_branch_b=w_branch_b, w_out=w_out, w_gate_up=w_gate_up,
                   w_down=w_down)
    moments_m = dict(w_in=m_w_in, w_branch_a=m_w_branch_a, w_branch_b=m_w_branch_b, w_out=m_w_out,
                     w_gate_up=m_w_gate_up, w_down=m_w_down)
    moments_v = dict(w_in=v_w_in, w_branch_a=v_w_branch_a, w_branch_b=v_w_branch_b, w_out=v_w_out,
                     w_gate_up=v_w_gate_up, w_down=v_w_down)
    xi, yi, ci = _coords()
    me = 4 * xi + 2 * yi + ci
    core = jnp.reshape(ci, (1,)).astype(jnp.int32)
    chip = jnp.reshape(2 * xi + yi, (1,)).astype(jnp.int32)

    ada_w = w_ada.shape[2]
    c_act = _small_allgather(c, name="comm_gather_c", silu=True).reshape(N_DEV, d)
    c_pad = jnp.concatenate([c_act, jnp.zeros_like(c_act)], axis=0).astype(BF16)
    bias = lax.dynamic_slice(b_ada, (0, me * ada_w), (depth, ada_w))
    mod_part = jnp.stack([_mm(c_pad, w_ada[l], name="mm_ada")[:N_DEV] for l in range(depth)]) + bias[:, None, :]
    mod_all = _small_allgather(mod_part.reshape(1, depth * N_DEV * ada_w), name="comm_gather_mod")
    mod_all = mod_all.reshape(N_DEV, depth, N_DEV, ada_w)
    mod_mine = lax.dynamic_index_in_dim(mod_all, me, axis=2, keepdims=False)
    mods = jnp.transpose(mod_mine, (1, 0, 2)).reshape(depth, 1, 6 * d)

    cos2, sin2 = _rope_tables(seq)
    gains = [jnp.stack([qn_g[l], kn_g[l]])[:, None, :] for l in range(depth)]
    g1s = [norm1_g[l][None] for l in range(depth)]
    g2s = [norm2_g[l][None] for l in range(depth)]

    me_arr = jnp.reshape(me, (1,)).astype(jnp.int32)

    def placed(keys, l):
        return [_cast_place(weights[k], l, SHARD_AXIS[k], me_arr, name="cast_place_" + k) for k in keys]

    def gather_of(keys, l, tag):
        return _SplitGather(placed(keys, l), [SHARD_AXIS[k] for k in keys], f"{tag}{l}")

    groups = [("w_in", 0, MIXER_W[:1]), ("rest", 0, MIXER_W[1:]), ("ffn", 0, FFN_W)]
    for l in range(1, depth):
        groups += [("mixer", l, MIXER_W), ("ffn", l, FFN_W)]
    gathers, token = {}, mods
    for tag, l, keys in groups:
        gathers[tag, l] = gather_of(keys, l, tag)
        token = gathers[tag, l].first(after=token)
    h = x[0]
    u = _rmsmod_fwd(h, *_norm_args(mods[0], g1s[0], 0, d), name="rmsmod_fwd")
    token = gathers["w_in", 0].forward(after=u)
    wm = {"w_in": gathers["w_in", 0].finish(after=token)[0]}
    saved, full = [], []
    for l in range(depth):
        last = l + 1 == depth
        sv = _mixer_fwd_a(h, u, gains[l], wm["w_in"], cos2, sin2)
        gathers["ffn", l].forward(after=sv["o_b"])
        if l == 0:
            gathers["rest", 0].forward(after=sv["o_b"])
            wm.update(zip(MIXER_W[1:], gathers["rest", 0].finish(after=sv["o_b"])))
        h_mid = _mixer_fwd_b(sv, mods[l], g2s[l], wm)
        wf = dict(zip(FFN_W, gathers["ffn", l].finish(after=h_mid)))
        a = _ffn_fwd_a(sv, wf["w_gate_up"])
        if not last:
            gathers["mixer", l + 1].forward(after=a)
        h, u = _ffn_fwd_b(sv, mods[l], wf["w_down"],
                          None if last else _norm_args(mods[l + 1], g1s[l + 1], 0, d))
        saved.append(sv)
        full.append({**wm, **wf})
        if not last:
            wm = dict(zip(MIXER_W, gathers["mixer", l + 1].finish(after=h)))
    loss_part, dh = _loss_fwd(h, loss_target[0], name="loss")
    loss = lax.psum(loss_part[0, 0], ("x", "y", "c"))

    pipe = _ReducePipeline(core)
    dmods, dg1s, dg2s, dgains = [None] * depth, [None] * depth, [None] * depth, [None] * depth
    deps = []
    for l in reversed(range(depth)):
        dh_mid, dmod_f, dg2s[l], grads = _ffn_bwd(dh, saved[l], mods[l], g2s[l], full[l], deps, pipe.tick)
        deps = pipe.tick(dh_mid) + pipe.add(FFN_W, grads, l)
        dh, dmod_m, dg1s[l], dgains[l], grads = _mixer_bwd(
            dh_mid, saved[l], mods[l], g1s[l], gains[l], full[l], cos2, sin2, deps,
            lambda after, early, l=l: pipe.tick(after) + pipe.add(MIXER_W[1:], early, l))
        dmods[l] = jnp.concatenate(dmod_m + dmod_f, axis=1)
        deps = pipe.tick(dh) + pipe.add(MIXER_W[:1], grads, l)
    grad_x = dh[None]

    stacked = {}

    def update(items):
        for keys, l, sums, remote in items:
            for k, p_, r_ in zip(keys, sums, remote):
                stacked[k] = _adamw_sharded(p_, r_, chip, weights[k], moments_m[k], moments_v[k], l,
                                            stacked.get(k), [], name="adamw_" + k)

    ready = pipe.take_done()
    update([it for it in ready if it[0] != FFN_W])

    small = jnp.concatenate(
        dmods + dg1s + dg2s + [dgains[l][0] for l in range(depth)] + [dgains[l][1] for l in range(depth)], axis=1)
    small_all = _small_allgather(small, name="comm_gather_small")
    pipe.tick(small_all)
    update([it for it in ready if it[0] == FFN_W] + pipe.take_done())

    def pack(b, n1, n2, qn, kn):
        return jnp.concatenate([t_.reshape(1, -1) for t_ in (b, n1, n2, qn, kn)], axis=1)

    sg, sd, sm, sv_ = _adamw_replicated(small_all, pack(b_ada, norm1_g, norm2_g, qn_g, kn_g),
                                        pack(m_b_ada, m_norm1_g, m_norm2_g, m_qn_g, m_kn_g),
                                        pack(v_b_ada, v_norm1_g, v_norm2_g, v_qn_g, v_kn_g), name="adamw_replicated")

    def unpack(p):
        sizes = [depth * 6 * d, depth * d, depth * d, depth * HEAD_DIM, depth * HEAD_DIM]
        shapes = [b_ada.shape, norm1_g.shape, norm2_g.shape, qn_g.shape, kn_g.shape]
        out, off = [], 0
        for n, shp in zip(sizes, shapes):
            out.append(p[0, off:off + n].reshape(shp))
            off += n
        return dict(zip(("b_ada", "norm1_g", "norm2_g", "qn_g", "kn_g"), out))

    ug, ud, um, uv = unpack(sg), unpack(sd), unpack(sm), unpack(sv_)
    res = {k: dict(g=ug[k], d=ud[k], m=um[k], v=uv[k]) for k in ug}

    dmod_all = small_all[:, 0, :depth * 6 * d].reshape(N_DEV, depth, 6 * d)
    g_ada = None
    for l in range(depth):
        dm = lax.dynamic_slice(dmod_all[:, l, :], (0, me * ada_w), (N_DEV, ada_w))
        dm = jnp.concatenate([dm, jnp.zeros_like(dm)], axis=0).astype(BF16)
        g_ada = _mm(c_pad, dm, ta=True, name="mm_wgrad_ada", stack=(l, depth, g_ada))
    d_ada, m_ada, v_ada = _adamw_local(g_ada, w_ada, m_w_ada, v_w_ada, name="adamw_local")
    res["w_ada"] = dict(g=g_ada, d=d_ada, m=m_ada, v=v_ada)

    pipe.tick(d_ada)
    update(pipe.take_done())
    pipe.tick(d_ada, flush=True)
    update(pipe.take_done())
    for k, (g_, d_, m_, v_) in stacked.items():
        res[k] = dict(g=g_, d=d_, m=m_, v=v_)

    order = ("w_ada", "b_ada", "norm1_g", "norm2_g", "w_in", "qn_g", "kn_g", "w_branch_a", "w_branch_b", "w_out",
             "w_gate_up", "w_down")
    _ORDER["token"] = None
    return (loss, grad_x, *[res[k]["g"] for k in order], *[res[k]["d"] for k in order],
            *[res[k]["m"] for k in order], *[res[k]["v"] for k in order])
```

```python
import functools

import jax
import jax.numpy as jnp
from jax import lax
from jax.experimental import pallas as pl
from jax.experimental.pallas import tpu as pltpu

F32 = jnp.float32
BF16 = jnp.bfloat16

HEAD_DIM = 128
BLOCK = 128
DILATIONS = (1, 4, 16)
HEADS_PER_GROUP = 4
A_HEADS = 12
SB_HEADS = 4
GROUP_W = HEADS_PER_GROUP * HEAD_DIM
A_W = A_HEADS * HEAD_DIM
B_W = SB_HEADS * HEAD_DIM
OFF_QA, OFF_KA, OFF_VA = 0, A_W, 2 * A_W
OFF_QB, OFF_KB, OFF_VB = 3 * A_W, 3 * A_W + B_W, 3 * A_W + 2 * B_W
OFF_GATES = 3 * A_W + 3 * B_W
ROPE_THETA = 10000.0
EPS = 1e-6
ATT_SCALE = HEAD_DIM ** -0.5
MASKED = -1e30

ADAM_LR, ADAM_B1, ADAM_B2, ADAM_EPS, ADAM_WD, ADAM_STEP = 0.001, 0.9, 0.999, 1e-08, 0.01, 10

N_DEV = 8
N_CHIPS = 4
V7X_VMEM_LIMIT_BYTES = 56 * 1024 * 1024
ELEMWISE_BLOCK_BYTES = 2 * 1024 * 1024
MESH = pl.DeviceIdType.MESH

NN = (((1,), (0,)), ((), ()))
NT = (((1,), (1,)), ((), ()))
TN = (((0,), (0,)), ((), ()))


def _dot(a, b, dims=NN):
    return lax.dot_general(a, b, dims, preferred_element_type=F32)


def _tile(n, cap, mult=128):
    best = None
    for t in range(mult, min(n, cap) + 1, mult):
        if n % t == 0:
            best = t
    if best is None:
        assert n <= 2 * cap, (n, cap)
        return n
    return best


def _rows(r, c):
    return _tile(r, max(16, ELEMWISE_BLOCK_BYTES // (4 * c)), 16)


_ORDER = {"token": None}
TOKEN = jax.ShapeDtypeStruct((8, 128), F32)


def _take_token():
    prev = _ORDER["token"]
    return [] if prev is None else [prev]


def _pcall(body, *, name, out_shape, grid=None, in_specs=None, out_specs=None, scratch=(), aliases=None,
           prefetch=0, deps=()):
    single = not isinstance(out_shape, (tuple, list))
    out_shapes = [out_shape] if single else list(out_shape)
    out_specs = [out_specs] if single else list(out_specs)
    extra = list(deps) + _take_token()
    n_in, n_extra, n_out = prefetch + len(in_specs), len(extra), len(out_shapes)

    def wrapped(*refs):
        token = refs[n_in + n_extra + n_out]
        token[...] = jnp.zeros_like(token)
        return body(*refs[:n_in], *refs[n_in + n_extra:n_in + n_extra + n_out], *refs[n_in + n_extra + n_out + 1:])

    in_specs = list(in_specs) + [pl.BlockSpec(memory_space=pl.ANY)] * n_extra
    if grid is None:
        out_specs.append(pl.BlockSpec(memory_space=pltpu.VMEM))
    else:
        out_specs.append(pl.BlockSpec(TOKEN.shape, lambda *_: (0, 0)))
    kwargs = dict(name=name, out_shape=out_shapes + [TOKEN], input_output_aliases=aliases or {},
                  compiler_params=pltpu.CompilerParams(vmem_limit_bytes=V7X_VMEM_LIMIT_BYTES))
    if prefetch:
        call = pl.pallas_call(wrapped, grid_spec=pltpu.PrefetchScalarGridSpec(
            num_scalar_prefetch=prefetch, grid=grid, in_specs=in_specs, out_specs=out_specs,
            scratch_shapes=list(scratch)), **kwargs)
    else:
        if grid is not None:
            kwargs["grid"] = grid
        call = pl.pallas_call(wrapped, in_specs=in_specs, out_specs=out_specs, scratch_shapes=list(scratch), **kwargs)

    def run(*args):
        outs = call(*args, *extra)
        _ORDER["token"] = outs[-1]
        return outs[0] if single else tuple(outs[:-1])

    return run


def _mm(a, b, *, name, ta=False, tb=False, out_dtype=F32, caps=(1024, 1024, 3072), stack=None, deps=()):
    kdim, m = a.shape if ta else a.shape[::-1]
    n, k2 = b.shape if tb else b.shape[::-1]
    assert kdim == k2, (a.shape, b.shape, ta, tb)
    tm, tn, tk = _tile(m, caps[0]), _tile(n, caps[1]), _tile(kdim, caps[2])
    nk = kdim // tk
    dims = (((0 if ta else 1,), (1 if tb else 0,)), ((), ()))

    def body(*refs):
        a_ref, b_ref = refs[0], refs[1]
        part = _dot(a_ref[...].astype(BF16), b_ref[...].astype(BF16), dims)
        if nk == 1:
            o_ref = refs[-1]
            o_ref[...] = part.astype(o_ref.dtype)
            return
        o_ref, acc_ref = refs[-2], refs[-1]
        k = pl.program_id(2)

        @pl.when(k == 0)
        def _():
            acc_ref[...] = part

        @pl.when(k > 0)
        def _():
            acc_ref[...] += part

        @pl.when(k == nk - 1)
        def _():
            o_ref[...] = acc_ref[...].astype(o_ref.dtype)

    a_spec = (pl.BlockSpec((tk, tm), lambda i, j, k: (k, i)) if ta
              else pl.BlockSpec((tm, tk), lambda i, j, k: (i, k)))
    b_spec = (pl.BlockSpec((tn, tk), lambda i, j, k: (j, k)) if tb
              else pl.BlockSpec((tk, tn), lambda i, j, k: (k, j)))
    ins, in_specs, aliases = [a, b], [a_spec, b_spec], {}
    if stack is None:
        out_shape = jax.ShapeDtypeStruct((m, n), out_dtype)
        out_spec = pl.BlockSpec((tm, tn), lambda i, j, k: (i, j))
    else:
        layer, n_layers, buf = stack
        out_shape = jax.ShapeDtypeStruct((n_layers, m, n), out_dtype)
        out_spec = pl.BlockSpec((None, tm, tn), lambda i, j, k: (layer, i, j))
        if buf is not None:
            ins.append(buf)
            in_specs.append(pl.BlockSpec(memory_space=pl.ANY))
            aliases = {2: 0}
    scratch = [] if nk == 1 else [pltpu.VMEM((tm, tn), F32)]
    return _pcall(body, name=name, out_shape=out_shape, grid=(m // tm, n // tn, nk), in_specs=in_specs,
                  out_specs=out_spec, scratch=scratch, aliases=aliases, deps=deps)(*ins)


EPILOGUE_ROWS = 256


def _row_chunks(tm):
    return [slice(r, r + EPILOGUE_ROWS) for r in range(0, tm, EPILOGUE_ROWS)] if tm > EPILOGUE_ROWS else [slice(0, tm)]


def _mm_cat_k(a_lo, a_hi, b, *, name):
    m, f = a_lo.shape
    n = b.shape[0]
    tm, tn, tk = _tile(m, 1024), _tile(n, 1024), _tile(f, 3072)
    half = f // tk
    nk = 2 * half

    def body(lo_ref, hi_ref, b_ref, o_ref, acc_ref):
        k = pl.program_id(2)

        def accumulate(a_ref):
            part = _dot(a_ref[...], b_ref[...], NT)

            @pl.when(k == 0)
            def _():
                acc_ref[...] = part

            @pl.when(k > 0)
            def _():
                acc_ref[...] += part

        pl.when(k < half)(lambda: accumulate(lo_ref))
        pl.when(k >= half)(lambda: accumulate(hi_ref))

        @pl.when(k == nk - 1)
        def _():
            o_ref[...] = acc_ref[...]

    return _pcall(body, name=name, out_shape=jax.ShapeDtypeStruct((m, n), F32), grid=(m // tm, n // tn, nk),
                  in_specs=[pl.BlockSpec((tm, tk), lambda i, j, k: (i, jnp.minimum(k, half - 1))),
                            pl.BlockSpec((tm, tk), lambda i, j, k: (i, jnp.maximum(k - half, 0))),
                            pl.BlockSpec((tn, tk), lambda i, j, k: (j, k))],
                  out_specs=pl.BlockSpec((tm, tn), lambda i, j, k: (i, j)),
                  scratch=[pltpu.VMEM((tm, tn), F32)])(a_lo, a_hi, b)


def _mm_cat_n(a, b_lo, b_hi, *, name):
    s, m = a.shape
    f = b_lo.shape[1]
    tm, tn = _tile(m, 1024), _tile(f, 1024)
    half = f // tn

    def body(a_ref, lo_ref, hi_ref, o_ref):
        j = pl.program_id(1)

        @pl.when(j < half)
        def _():
            o_ref[...] = _dot(a_ref[...], lo_ref[...], TN).astype(BF16)

        @pl.when(j >= half)
        def _():
            o_ref[...] = _dot(a_ref[...], hi_ref[...], TN).astype(BF16)

    return _pcall(body, name=name, out_shape=jax.ShapeDtypeStruct((m, 2 * f), BF16), grid=(m // tm, 2 * half),
                  in_specs=[pl.BlockSpec((s, tm), lambda i, j: (0, i)),
                            pl.BlockSpec((s, tn), lambda i, j: (0, jnp.minimum(j, half - 1))),
                            pl.BlockSpec((s, tn), lambda i, j: (0, jnp.maximum(j - half, 0)))],
                  out_specs=pl.BlockSpec((tm, tn), lambda i, j: (i, j)))(a, b_lo, b_hi)


def _mm_resid_norm(a, w, h, gate, norm, *, name):
    s, kdim = a.shape
    d = w.shape[1]
    tk = _tile(kdim, 2048)
    nk = kdim // tk
    tm = _tile(s, 256 if nk == 1 else 512)

    def body(*refs):
        a_ref, w_ref, h_ref, gate_ref = refs[:4]
        outs = refs[7:] if norm is not None else refs[4:]

        def finish(rows, t):
            hn = h_ref[rows, :] + gate_ref[...] * t
            outs[0][rows, :] = hn
            outs[1][rows, :] = t.astype(BF16)
            if norm is not None:
                g_ref, sc_ref, sh_ref = refs[4:7]
                r = lax.rsqrt(jnp.mean(hn * hn, axis=-1, keepdims=True) + EPS)
                outs[2][rows, :] = (((hn * r) * g_ref[...]) * (1.0 + sc_ref[...]) + sh_ref[...]).astype(BF16)

        if nk == 1:
            for rows in _row_chunks(tm):
                finish(rows, _dot(a_ref[rows, :], w_ref[...]))
            return
        acc_ref = refs[-1]
        k = pl.program_id(1)

        @pl.when(k == 0)
        def _():
            acc_ref[...] = _dot(a_ref[...], w_ref[...])

        @pl.when(jnp.logical_and(k > 0, k < nk - 1))
        def _():
            acc_ref[...] += _dot(a_ref[...], w_ref[...])

        @pl.when(k == nk - 1)
        def _():
            for rows in _row_chunks(tm):
                finish(rows, acc_ref[rows, :] + _dot(a_ref[rows, :], w_ref[...]))

    row = pl.BlockSpec((tm, d), lambda i, k: (i, 0))
    vec = pl.BlockSpec((1, d), lambda i, k: (0, 0))
    in_specs = [pl.BlockSpec((tm, tk), lambda i, k: (i, k)), pl.BlockSpec((tk, d), lambda i, k: (k, 0)), row, vec]
    args = [a, w, h, gate]
    out_shape = [jax.ShapeDtypeStruct((s, d), F32), jax.ShapeDtypeStruct((s, d), BF16)]
    if norm is not None:
        in_specs += [vec, vec, vec]
        args += list(norm)
        out_shape.append(jax.ShapeDtypeStruct((s, d), BF16))
    outs = _pcall(body, name=name, out_shape=tuple(out_shape), grid=(s // tm, nk), in_specs=in_specs,
                  out_specs=(row,) * len(out_shape), scratch=[] if nk == 1 else [pltpu.VMEM((tm, d), F32)])(*args)
    return outs if norm is not None else (*outs, None)


def _mm_merge(o_a, o_b, w_a, w_b, proj, *, name):
    s = o_a.shape[0]
    d = w_a.shape[1]
    tm = _tile(s, 512)
    ga_blk = OFF_GATES // d

    def body(oa_ref, ob_ref, wa_ref, wb_ref, ga_ref, gb_ref, m_ref, ya_ref, yb_ref):
        for rows in _row_chunks(tm):
            ya, yb = _dot(oa_ref[rows, :], wa_ref[...]), _dot(ob_ref[rows, :], wb_ref[...])
            m_ref[rows, :] = (jax.nn.sigmoid(ga_ref[rows, :]) * ya
                              + jax.nn.sigmoid(gb_ref[rows, :]) * yb).astype(BF16)
            ya_ref[rows, :] = ya.astype(BF16)
            yb_ref[rows, :] = yb.astype(BF16)

    row = pl.BlockSpec((tm, d), lambda i: (i, 0))
    act = pl.BlockSpec((tm, o_a.shape[1]), lambda i: (i, 0))
    wspec = pl.BlockSpec(w_a.shape, lambda i: (0, 0))
    shp = jax.ShapeDtypeStruct((s, d), BF16)
    return _pcall(body, name=name, out_shape=(shp, shp, shp), grid=(s // tm,),
                  in_specs=[act, act, wspec, wspec, pl.BlockSpec((tm, d), lambda i: (i, ga_blk)),
                            pl.BlockSpec((tm, d), lambda i: (i, ga_blk + 1))],
                  out_specs=(row, row, row))(o_a, o_b, w_a, w_b, proj, proj)


def _mm_down_t_swiglu(df, w_down, g, u, *, name):
    s, d = df.shape
    f = w_down.shape[0]
    tm, tn = _tile(s, 1024), _tile(f, 512)

    def body(df_ref, w_ref, g_ref, u_ref, dg_ref, du_ref):
        w = w_ref[...]
        for rows in _row_chunks(tm):
            da = _dot(df_ref[rows, :], w, NT)
            gf = g_ref[rows, :].astype(F32)
            sg = jax.nn.sigmoid(gf)
            dg_ref[rows, :] = (da * u_ref[rows, :].astype(F32) * (sg * (1.0 + gf * (1.0 - sg)))).astype(BF16)
            du_ref[rows, :] = (da * (gf * sg)).astype(BF16)

    tile = pl.BlockSpec((tm, tn), lambda i, j: (i, j))
    shp = jax.ShapeDtypeStruct((s, f), BF16)
    return _pcall(body, name=name, out_shape=(shp, shp), grid=(s // tm, f // tn),
                  in_specs=[pl.BlockSpec((tm, d), lambda i, j: (i, 0)), pl.BlockSpec((tn, d), lambda i, j: (j, 0)),
                            tile, tile],
                  out_specs=(tile, tile))(df, w_down, g, u)


def _rmsmod_fwd(h, g, scale, shift, *, name, deps=()):
    s, d = h.shape
    ts = _rows(s, d)

    def body(h_ref, g_ref, sc_ref, sh_ref, u_ref):
        hf = h_ref[...]
        r = lax.rsqrt(jnp.mean(hf * hf, axis=-1, keepdims=True) + EPS)
        u_ref[...] = (((hf * r) * g_ref[...]) * (1.0 + sc_ref[...]) + sh_ref[...]).astype(BF16)

    row = pl.BlockSpec((ts, d), lambda i: (i, 0))
    vec = pl.BlockSpec((1, d), lambda i: (0, 0))
    return _pcall(body, name=name, out_shape=jax.ShapeDtypeStruct((s, d), BF16), grid=(s // ts,),
                  in_specs=[row, vec, vec, vec], out_specs=row, deps=deps)(h, g, scale, shift)


def _gate_bwd(dhf, t_ref, gate_ref, dt_ref, dgate_ref):
    dt_ref[...] = (dhf * gate_ref[...]).astype(BF16)
    dgate_ref[...] += jnp.sum(dhf * t_ref[...], axis=0, keepdims=True)


def _rmsmod_bwd(du, h, g, scale, dres, t, gate, *, name):
    s, d = h.shape
    ts = _rows(s, d)
    chain = t is not None

    def body(*refs):
        du_ref, h_ref, g_ref, sc_ref, dres_ref = refs[:5]
        dh_ref, dsh_ref, dsc_ref, dg_ref = refs[-6:-2] if chain else refs[-4:]
        sums = (dsh_ref, dsc_ref, dg_ref) + ((refs[-1],) if chain else ())

        @pl.when(pl.program_id(0) == 0)
        def _():
            for ref in sums:
                ref[...] = jnp.zeros_like(ref)

        hf, duf, gain = h_ref[...], du_ref[...], g_ref[...]
        r = lax.rsqrt(jnp.mean(hf * hf, axis=-1, keepdims=True) + EPS)
        xh = hf * r
        dn = duf * (1.0 + sc_ref[...])
        dsh_ref[...] += jnp.sum(duf, axis=0, keepdims=True)
        dsc_ref[...] += jnp.sum(duf * (xh * gain), axis=0, keepdims=True)
        dg_ref[...] += jnp.sum(dn * xh, axis=0, keepdims=True)
        dxh = dn * gain
        dh = dres_ref[...] + r * (dxh - xh * jnp.mean(dxh * xh, axis=-1, keepdims=True))
        dh_ref[...] = dh
        if chain:
            _gate_bwd(dh, refs[5], refs[6], refs[-2], refs[-1])

    row = pl.BlockSpec((ts, d), lambda i: (i, 0))
    vec = pl.BlockSpec((1, d), lambda i: (0, 0))
    vshape = jax.ShapeDtypeStruct((1, d), F32)
    out_shape, out_specs = [jax.ShapeDtypeStruct((s, d), F32), vshape, vshape, vshape], [row, vec, vec, vec]
    in_specs, args = [row, row, vec, vec, row], [du, h, g, scale, dres]
    if chain:
        in_specs, args = in_specs + [row, vec], args + [t, gate]
        out_shape, out_specs = out_shape + [jax.ShapeDtypeStruct((s, d), BF16), vshape], out_specs + [row, vec]
    outs = _pcall(body, name=name, out_shape=tuple(out_shape), grid=(s // ts,), in_specs=in_specs,
                  out_specs=tuple(out_specs))(*args)
    return outs if chain else (*outs, None, None)


def _merge_bwd(dm, proj, y_a, y_b, *, name):
    s, d = y_a.shape
    ts = _rows(s, d)
    ga_blk = OFF_GATES // d

    def body(dm_ref, ga_ref, gb_ref, ya_ref, yb_ref, dya_ref, dyb_ref, dga_ref, dgb_ref):
        dmf = dm_ref[...]
        sa, sb = jax.nn.sigmoid(ga_ref[...]), jax.nn.sigmoid(gb_ref[...])
        dya_ref[...] = (dmf * sa).astype(BF16)
        dyb_ref[...] = (dmf * sb).astype(BF16)
        dga_ref[...] = (dmf * ya_ref[...] * (sa * (1.0 - sa))).astype(BF16)
        dgb_ref[...] = (dmf * yb_ref[...] * (sb * (1.0 - sb))).astype(BF16)

    row = pl.BlockSpec((ts, d), lambda i: (i, 0))
    ga = pl.BlockSpec((ts, d), lambda i: (i, ga_blk))
    gb = pl.BlockSpec((ts, d), lambda i: (i, ga_blk + 1))
    shp = jax.ShapeDtypeStruct((s, d), BF16)
    return _pcall(body, name=name, out_shape=(shp, shp, shp, shp), grid=(s // ts,),
                  in_specs=[row, ga, gb, row, row], out_specs=(row, row, row, row))(dm, proj, proj, y_a, y_b)


def _mm_swiglu(u2, w_gate_up, *, name):
    s, d = u2.shape
    f = w_gate_up.shape[1] // 2
    tm, tn = _tile(s, 1024), _tile(f, 512)
    nj = f // tn

    def body(x_ref, wg_ref, wu_ref, a_ref, g_ref, u_ref):
        for rows in _row_chunks(tm):
            x = x_ref[rows, :]
            gf, uf = _dot(x, wg_ref[...]), _dot(x, wu_ref[...])
            a_ref[rows, :] = ((gf * jax.nn.sigmoid(gf)) * uf).astype(BF16)
            g_ref[rows, :] = gf.astype(BF16)
            u_ref[rows, :] = uf.astype(BF16)

    out = pl.BlockSpec((tm, tn), lambda i, j: (i, j))
    shp = jax.ShapeDtypeStruct((s, f), BF16)
    return _pcall(body, name=name, out_shape=(shp, shp, shp), grid=(s // tm, nj),
                  in_specs=[pl.BlockSpec((tm, d), lambda i, j: (i, 0)), pl.BlockSpec((d, tn), lambda i, j: (0, j)),
                            pl.BlockSpec((d, tn), lambda i, j: (0, nj + j))],
                  out_specs=(out, out, out))(u2, w_gate_up, w_gate_up)


def _loss_fwd(y, tgt, t, gate, *, name):
    s, d = y.shape
    ts = _rows(s, d)

    def body(y_ref, tgt_ref, t_ref, gate_ref, l_ref, dy_ref, dt_ref, dgate_ref):
        @pl.when(pl.program_id(0) == 0)
        def _():
            l_ref[...] = jnp.zeros_like(l_ref)
            dgate_ref[...] = jnp.zeros_like(dgate_ref)

        e = y_ref[...] - tgt_ref[...]
        dy = e * (1.0 / d)
        dy_ref[...] = dy
        per_tok = jnp.sum(e * e, axis=1, keepdims=True) * (1.0 / d)
        l_ref[...] += 0.5 * jnp.sum(per_tok, axis=0, keepdims=True)
        _gate_bwd(dy, t_ref, gate_ref, dt_ref, dgate_ref)

    row = pl.BlockSpec((ts, d), lambda i: (i, 0))
    vec = pl.BlockSpec((1, d), lambda i: (0, 0))
    return _pcall(body, name=name,
                  out_shape=(jax.ShapeDtypeStruct((1, 128), F32), jax.ShapeDtypeStruct((s, d), F32),
                             jax.ShapeDtypeStruct((s, d), BF16), jax.ShapeDtypeStruct((1, d), F32)),
                  grid=(s // ts,), in_specs=[row, row, row, vec],
                  out_specs=(pl.BlockSpec((1, 128), lambda i: (0, 0)), row, row, vec))(y, tgt, t, gate)


def _rope_tables(seq):
    inv = jnp.power(ROPE_THETA, -jnp.arange(0, HEAD_DIM, 2, dtype=F32) / HEAD_DIM)
    ang = jnp.arange(seq, dtype=F32)[:, None] * inv[None, :]
    cos, sin = jnp.cos(ang), jnp.sin(ang)
    return jnp.concatenate([cos, cos], axis=1), jnp.concatenate([-sin, sin], axis=1)


def _qkrope_fwd(proj, gains, cos2, sin2, *, name):
    s = proj.shape[0]
    ts = _rows(s, A_W)

    def body(x_ref, g_ref, c_ref, s_ref, o_ref):
        gain, cos, sin = g_ref[...], c_ref[...], s_ref[...]
        for h in range(A_HEADS):
            lanes = slice(h * HEAD_DIM, (h + 1) * HEAD_DIM)
            x = x_ref[:, lanes]
            y = (x * lax.rsqrt(jnp.mean(x * x, axis=-1, keepdims=True) + EPS)) * gain
            o_ref[:, lanes] = (y * cos + pltpu.roll(y, HEAD_DIM // 2, 1) * sin).astype(BF16)

    heads = pl.BlockSpec((ts, A_W), lambda i, j: (i, j))
    tab = pl.BlockSpec((ts, HEAD_DIM), lambda i, j: (i, 0))
    gain = pl.BlockSpec((None, 1, HEAD_DIM), lambda i, j: (j, 0, 0))
    return _pcall(body, name=name, out_shape=jax.ShapeDtypeStruct((s, 2 * A_W), BF16),
                  grid=(s // ts, 2), in_specs=[heads, gain, tab, tab], out_specs=heads)(
                      proj, gains, cos2, sin2)


def _qkrope_bwd(d_groups, proj, gains, which, cos2, sin2, *, name):
    s = proj.shape[0]
    ts = _rows(s, A_W)

    def body(d0_ref, d1_ref, d2_ref, x_ref, g_ref, c_ref, s_ref, dx_ref, dg_ref):
        @pl.when(pl.program_id(0) == 0)
        def _():
            dg_ref[...] = jnp.zeros_like(dg_ref)

        gain, cos, sin = g_ref[...], c_ref[...], s_ref[...]
        dg = jnp.zeros((1, HEAD_DIM), F32)
        for h in range(A_HEADS):
            lanes = slice(h * HEAD_DIM, (h + 1) * HEAD_DIM)
            slot = slice((h % HEADS_PER_GROUP) * HEAD_DIM, (h % HEADS_PER_GROUP + 1) * HEAD_DIM)
            dout = (d0_ref, d1_ref, d2_ref)[h // HEADS_PER_GROUP][:, slot]
            dy = dout * cos + pltpu.roll(dout * sin, HEAD_DIM // 2, 1)
            x = x_ref[:, lanes]
            r = lax.rsqrt(jnp.mean(x * x, axis=-1, keepdims=True) + EPS)
            xh = x * r
            dg = dg + jnp.sum(dy * xh, axis=0, keepdims=True)
            dxh = dy * gain
            dx_ref[:, lanes] = (r * (dxh - xh * jnp.mean(dxh * xh, axis=-1, keepdims=True))).astype(BF16)
        dg_ref[...] += dg

    group = pl.BlockSpec((ts, GROUP_W), lambda i: (i, 0))
    tab = pl.BlockSpec((ts, HEAD_DIM), lambda i: (i, 0))
    gain = pl.BlockSpec((None, 1, HEAD_DIM), lambda i: (which, 0, 0))
    return _pcall(body, name=name,
                  out_shape=(jax.ShapeDtypeStruct((s, A_W), BF16), jax.ShapeDtypeStruct((1, HEAD_DIM), F32)),
                  grid=(s // ts,),
                  in_specs=[group, group, group, pl.BlockSpec((ts, A_W), lambda i: (i, which)), gain, tab, tab],
                  out_specs=(pl.BlockSpec((ts, A_W), lambda i: (i, 0)), pl.BlockSpec((1, HEAD_DIM), lambda i: (0, 0))))(
                      *d_groups, proj, gains, cos2, sin2)


def _assemble(pieces, *, name):
    s = pieces[0].shape[0]
    widths = [p.shape[1] for p in pieces]
    total = sum(widths)
    ts = _rows(s, total // 2)

    def body(*refs):
        o_ref, off = refs[-1], 0
        for x_ref, w in zip(refs[:-1], widths):
            o_ref[:, off:off + w] = x_ref[...].astype(BF16)
            off += w

    return _pcall(body, name=name, out_shape=jax.ShapeDtypeStruct((s, total), BF16), grid=(s // ts,),
                  in_specs=[pl.BlockSpec((ts, w), lambda i: (i, 0)) for w in widths],
                  out_specs=pl.BlockSpec((ts, total), lambda i: (i, 0)))(*pieces)


def _block_rows(blk):
    if isinstance(blk, int):
        return pl.ds(blk * BLOCK, BLOCK)
    return pl.ds(pl.multiple_of(blk * BLOCK, BLOCK), BLOCK)


def _band_window(n, length):
    width = min(2 * BLOCK, length)
    row = lax.broadcasted_iota(jnp.int32, (BLOCK, width), 0)
    col = lax.broadcasted_iota(jnp.int32, (BLOCK, width), 1)
    if width == BLOCK:
        return pl.ds(0, BLOCK), col <= row
    first = n - 1 if isinstance(n, int) else jnp.maximum(n - 1, 0)
    first = max(first, 0) if isinstance(first, int) else first
    dist = row - col + (n - first) * BLOCK
    start = first * BLOCK if isinstance(first, int) else pl.multiple_of(first * BLOCK, BLOCK)
    return pl.ds(start, width), jnp.logical_and(dist >= 0, dist <= BLOCK)


def _dil_fwd(q_arr, k_arr, v_arr, offs, length, dil, *, name):
    nj, nb = dil * HEADS_PER_GROUP, length // BLOCK
    ju, nq = (HEADS_PER_GROUP, 2) if nb > 1 else (2 * HEADS_PER_GROUP, 1)
    qo, ko, vo = (off // ju for off in offs)
    assert all(off % ju == 0 for off in offs) and nb % nq == 0 and nj % ju == 0

    def body(q_ref, k_ref, v_ref, o_ref, l_ref):
        for qq in range(nq):
            qrows = slice(qq * BLOCK, (qq + 1) * BLOCK)
            rows, mask = _band_window(pl.program_id(1) * nq + qq, length)
            for cb in range(ju):
                lanes = slice(cb * HEAD_DIM, (cb + 1) * HEAD_DIM)
                sc = _dot(q_ref[qrows, lanes].astype(BF16), k_ref[rows, lanes].astype(BF16), NT) * ATT_SCALE
                sc = jnp.where(mask, sc, MASKED)
                m = sc.max(axis=-1, keepdims=True)
                p = jnp.exp(sc - m)
                den = jnp.sum(p, axis=-1, keepdims=True)
                acc = _dot(p.astype(BF16), v_ref[rows, lanes].astype(BF16))
                o_ref[qrows, lanes] = acc / den
                l_ref[qrows, lanes] = jnp.broadcast_to(m + jnp.log(den), (BLOCK, HEAD_DIM))

    qspec = pl.BlockSpec((nq * BLOCK, ju * HEAD_DIM), lambda j, n: (n, qo + j))
    kspec = pl.BlockSpec((length, ju * HEAD_DIM), lambda j, n: (0, ko + j))
    vspec = pl.BlockSpec((length, ju * HEAD_DIM), lambda j, n: (0, vo + j))
    ospec = pl.BlockSpec((nq * BLOCK, ju * HEAD_DIM), lambda j, n: (n, j))
    shp = jax.ShapeDtypeStruct((length, nj * HEAD_DIM), F32)
    return _pcall(body, name=name, out_shape=(shp, shp), grid=(nj // ju, nb // nq), in_specs=[qspec, kspec, vspec],
                  out_specs=(ospec, ospec))(q_arr, k_arr, v_arr)


def _dil_bwd(q_arr, k_arr, v_arr, offs, o, lse, do, dlse, length, dil, *, name):
    nj, nb = dil * HEADS_PER_GROUP, length // BLOCK
    ju = 2 * HEADS_PER_GROUP if length <= 4 * BLOCK else 2
    qo, ko, vo = (off // ju for off in offs)
    assert all(off % ju == 0 for off in offs)

    def body(q_ref, k_ref, v_ref, o_ref, l_ref, do_ref, dl_ref, dq_ref, dk_ref, dv_ref):
        dk_ref[...] = jnp.zeros_like(dk_ref)
        dv_ref[...] = jnp.zeros_like(dv_ref)

        def step(n, carry):
            qrows = _block_rows(n)
            rows, mask = _band_window(n, length)
            for cb in range(ju):
                lanes = slice(cb * HEAD_DIM, (cb + 1) * HEAD_DIM)
                q = q_ref[qrows, lanes].astype(BF16)
                dof = do_ref[qrows, lanes]
                dob = dof.astype(BF16)
                lse_c = l_ref[qrows, lanes][:, :1]
                shift = dl_ref[qrows, lanes][:, :1] - jnp.sum(dof * o_ref[qrows, lanes], axis=-1, keepdims=True)
                kk, vv = k_ref[rows, lanes].astype(BF16), v_ref[rows, lanes].astype(BF16)
                sc = _dot(q, kk, NT) * ATT_SCALE
                p = jnp.where(mask, jnp.exp(sc - lse_c), 0.0)
                ds = (p * (_dot(dob, vv, NT) + shift)).astype(BF16)
                dq_ref[qrows, lanes] = _dot(ds, kk) * ATT_SCALE
                dk_ref[rows, lanes] += _dot(ds, q, TN) * ATT_SCALE
                dv_ref[rows, lanes] += _dot(p.astype(BF16), dob, TN)
            return carry

        if nb == 1:
            step(0, 0)
        else:
            lax.fori_loop(0, nb, step, 0)

    def col(off):
        return pl.BlockSpec((length, ju * HEAD_DIM), lambda j: (0, off + j))

    shp = jax.ShapeDtypeStruct((length, nj * HEAD_DIM), F32)
    return _pcall(body, name=name, out_shape=(shp, shp, shp), grid=(nj // ju,),
                  in_specs=[col(qo), col(ko), col(vo), col(0), col(0), col(0), col(0)],
                  out_specs=(col(0), col(0), col(0)))(q_arr, k_arr, v_arr, o, lse, do, dlse)


def _combine_weights(l_refs):
    ls = [r[...] for r in l_refs]
    m = jnp.maximum(jnp.maximum(ls[0], ls[1]), ls[2])
    es = [jnp.exp(l - m) for l in ls]
    den = es[0] + es[1] + es[2]
    return [e / den for e in es]


def _combine_fwd(os_, lses, *, name):
    s = os_[0].shape[0]
    ts = _rows(s, GROUP_W)

    def body(o0, o1, o2, l0, l1, l2, out_ref):
        w = _combine_weights((l0, l1, l2))
        out_ref[...] = (w[0] * o0[...] + w[1] * o1[...] + w[2] * o2[...]).astype(BF16)

    row = pl.BlockSpec((ts, GROUP_W), lambda i: (i, 0))
    return _pcall(body, name=name, out_shape=jax.ShapeDtypeStruct((s, GROUP_W), BF16), grid=(s // ts,),
                  in_specs=[row] * 6, out_specs=row)(*os_, *lses)


def _combine_bwd(do_a, os_, lses, *, name, deps=()):
    s = do_a.shape[0]
    ts = _rows(s, GROUP_W)

    def body(d_ref, o0, o1, o2, l0, l1, l2, do0, do1, do2, dl0, dl1, dl2):
        w = _combine_weights((l0, l1, l2))
        d = d_ref[...]
        og = [o0[...], o1[...], o2[...]]
        oa = w[0] * og[0] + w[1] * og[1] + w[2] * og[2]
        ta = jnp.sum(d * oa, axis=-1, keepdims=True)
        for g, (do_ref, dl_ref) in enumerate(((do0, dl0), (do1, dl1), (do2, dl2))):
            do_ref[...] = w[g] * d
            dl_ref[...] = w[g] * (jnp.sum(d * og[g], axis=-1, keepdims=True) - ta)

    head = pl.BlockSpec((ts, HEAD_DIM), lambda i, h: (i, h))
    shp = jax.ShapeDtypeStruct((s, GROUP_W), F32)
    return _pcall(body, name=name, out_shape=(shp,) * 6, grid=(s // ts, HEADS_PER_GROUP),
                  in_specs=[head] * 7, out_specs=(head,) * 6, deps=deps)(do_a, *os_, *lses)


def _dot_exact(x, ones_mask):
    hi = x.astype(BF16)
    r1 = x - hi.astype(F32)
    mid = r1.astype(BF16)
    lo = (r1 - mid.astype(F32)).astype(BF16)
    return _dot(hi, ones_mask) + _dot(mid, ones_mask) + _dot(lo, ones_mask)


SB_QROWS = 2 * BLOCK
SB_UNROLL = 4
SB_HEADS_PER_STEP = 2
SB_LANES = [slice(hh * HEAD_DIM, (hh + 1) * HEAD_DIM) for hh in range(SB_HEADS_PER_STEP)]


def _sb_mask(j, i):
    row = lax.broadcasted_iota(jnp.int32, (SB_QROWS, BLOCK), 0)
    col = lax.broadcasted_iota(jnp.int32, (SB_QROWS, BLOCK), 1)
    return col + (j * BLOCK - i * SB_QROWS) < row


def _sb_steps(i):
    return ((i + 1) * (SB_QROWS // BLOCK) + SB_UNROLL - 1) // SB_UNROLL


def _sb_scores(q, kk, j, i, masked):
    mask = _sb_mask(j, i) if masked else None
    z = _dot(q, kk, NT) * ATT_SCALE
    sp = jnp.log(1.0 + jnp.exp(-jnp.abs(z)))
    log_beta = jnp.minimum(z, 0.0) - sp
    log_1mb = jnp.minimum(-z, 0.0) - sp
    if masked:
        log_1mb = jnp.where(mask, log_1mb, 0.0)
    return z, log_beta, log_1mb, mask


def _sb_weights(log_beta, log_1mb, mask, run, upper):
    a = jnp.exp(log_beta + (run + _dot_exact(log_1mb, upper)))
    return a if mask is None else jnp.where(mask, a, 0.0)


def _sb_peeled(nsteps, make_step, init, masked_first):
    if masked_first:
        return lax.fori_loop(1, nsteps, make_step(False), make_step(True)(0, init))
    return make_step(True)(nsteps - 1, lax.fori_loop(0, nsteps - 1, make_step(False), init))


def _tri(strict_lower):
    row = lax.broadcasted_iota(jnp.int32, (BLOCK, BLOCK), 0)
    col = lax.broadcasted_iota(jnp.int32, (BLOCK, BLOCK), 1)
    return ((row > col) if strict_lower else (row < col)).astype(BF16)


def _sb_fwd(proj, *, name):
    s = proj.shape[0]
    assert s % (BLOCK * SB_UNROLL) == 0 and s % SB_QROWS == 0

    def body(q_ref, k_ref, v_ref, o_ref):
        i = pl.program_id(1)
        qs = [q_ref[:, lanes].astype(BF16) for lanes in SB_LANES]
        upper = _tri(True)
        nsteps = _sb_steps(i)

        def make_step(masked):
            def step(t, carry):
                carry = list(carry)
                for b in reversed(range(SB_UNROLL)):
                    j = (nsteps - 1 - t) * SB_UNROLL + b
                    rows = _block_rows(j)
                    for hh, lanes in enumerate(SB_LANES):
                        acc, run = carry[hh]
                        _, log_beta, log_1mb, mask = _sb_scores(qs[hh], k_ref[rows, lanes].astype(BF16), j, i, masked)
                        a = _sb_weights(log_beta, log_1mb, mask, run, upper)
                        carry[hh] = (acc + _dot(a.astype(BF16), v_ref[rows, lanes].astype(BF16)),
                                     run + jnp.sum(log_1mb, axis=-1, keepdims=True))
                return tuple(carry)
            return step

        zero = (jnp.zeros((SB_QROWS, HEAD_DIM), F32), jnp.zeros((SB_QROWS, 1), F32))
        for lanes, (acc, _) in zip(SB_LANES, _sb_peeled(nsteps, make_step, (zero,) * SB_HEADS_PER_STEP, True)):
            o_ref[:, lanes] = acc.astype(BF16)

    width = SB_HEADS_PER_STEP * HEAD_DIM
    qb, kb, vb = (off // width for off in (OFF_QB, OFF_KB, OFF_VB))
    return _pcall(body, name=name, out_shape=jax.ShapeDtypeStruct((s, B_W), BF16),
                  grid=(SB_HEADS // SB_HEADS_PER_STEP, s // SB_QROWS),
                  in_specs=[pl.BlockSpec((SB_QROWS, width), lambda h, i: (i, qb + h)),
                            pl.BlockSpec((s, width), lambda h, i: (0, kb + h)),
                            pl.BlockSpec((s, width), lambda h, i: (0, vb + h))],
                  out_specs=pl.BlockSpec((SB_QROWS, width), lambda h, i: (i, h)))(proj, proj, proj)


def _sb_bwd(proj, do_b, *, name):
    s = proj.shape[0]
    assert s % (BLOCK * SB_UNROLL) == 0 and s % SB_QROWS == 0
    nkb = s // BLOCK

    def body(q_ref, k_ref, v_ref, do_ref, dq_ref, dk_ref, dv_ref, z_s, a_s):
        i = pl.program_id(1)

        @pl.when(i == 0)
        def _():
            dk_ref[...] = jnp.zeros_like(dk_ref)
            dv_ref[...] = jnp.zeros_like(dv_ref)

        qs = [q_ref[:, lanes].astype(BF16) for lanes in SB_LANES]
        dobs = [do_ref[:, lanes].astype(BF16) for lanes in SB_LANES]
        upper, lower = _tri(True), _tri(False)
        nsteps = _sb_steps(i)

        def make_recompute(masked):
            def recompute(t, runs):
                runs = list(runs)
                for b in reversed(range(SB_UNROLL)):
                    j = (nsteps - 1 - t) * SB_UNROLL + b
                    rows = _block_rows(j)
                    for hh, lanes in enumerate(SB_LANES):
                        z, log_beta, log_1mb, mask = _sb_scores(qs[hh], k_ref[rows, lanes].astype(BF16), j, i, masked)
                        z_s[hh, j] = z
                        a_s[hh, j] = _sb_weights(log_beta, log_1mb, mask, runs[hh], upper)
                        runs[hh] = runs[hh] + jnp.sum(log_1mb, axis=-1, keepdims=True)
                return tuple(runs)
            return recompute

        _sb_peeled(nsteps, make_recompute, (jnp.zeros((SB_QROWS, 1), F32),) * SB_HEADS_PER_STEP, True)

        def make_grads(masked):
            def grads(t, carry):
                carry = list(carry)
                for b in range(SB_UNROLL):
                    j = t * SB_UNROLL + b
                    rows = _block_rows(j)
                    for hh, lanes in enumerate(SB_LANES):
                        dq, run = carry[hh]
                        kk, vv = k_ref[rows, lanes].astype(BF16), v_ref[rows, lanes].astype(BF16)
                        z, a = z_s[hh, j], a_s[hh, j]
                        de = _dot(dobs[hh], vv, NT) * a
                        beta = jax.nn.sigmoid(z)
                        if masked:
                            beta = jnp.where(_sb_mask(j, i), beta, 0.0)
                        dz = (de * jax.nn.sigmoid(-z) - beta * (run + _dot_exact(de, lower))).astype(BF16)
                        dk_ref[rows, lanes] += _dot(dz, qs[hh], TN) * ATT_SCALE
                        dv_ref[rows, lanes] += _dot(a.astype(BF16), dobs[hh], TN)
                        carry[hh] = (dq + _dot(dz, kk), run + jnp.sum(de, axis=-1, keepdims=True))
                return tuple(carry)
            return grads

        zero = (jnp.zeros((SB_QROWS, HEAD_DIM), F32), jnp.zeros((SB_QROWS, 1), F32))
        for lanes, (dq, _) in zip(SB_LANES, _sb_peeled(nsteps, make_grads, (zero,) * SB_HEADS_PER_STEP, False)):
            dq_ref[:, lanes] = dq * ATT_SCALE

    width = SB_HEADS_PER_STEP * HEAD_DIM
    qb, kb, vb = (off // width for off in (OFF_QB, OFF_KB, OFF_VB))
    blk = pl.BlockSpec((SB_QROWS, width), lambda h, i: (i, h))
    full = pl.BlockSpec((s, width), lambda h, i: (0, h))
    shp = jax.ShapeDtypeStruct((s, B_W), F32)
    saved = pltpu.VMEM((SB_HEADS_PER_STEP, nkb, SB_QROWS, BLOCK), F32)
    return _pcall(body, name=name, out_shape=(shp, shp, shp), grid=(SB_HEADS // SB_HEADS_PER_STEP, s // SB_QROWS),
                  in_specs=[pl.BlockSpec((SB_QROWS, width), lambda h, i: (i, qb + h)),
                            pl.BlockSpec((s, width), lambda h, i: (0, kb + h)),
                            pl.BlockSpec((s, width), lambda h, i: (0, vb + h)), blk],
                  out_specs=(blk, full, full), scratch=[saved, saved])(proj, proj, proj, do_b)


def _coords():
    return lax.axis_index("x"), lax.axis_index("y"), lax.axis_index("c")


def _flip(v, bit):
    return 1 - v if bit else v


def _shard_of(ref, axis, idx, size):
    if axis == 0:
        sl = pl.ds(pl.multiple_of(idx * size, 16), size)
        return ref.at[sl, :] if len(ref.shape) == 2 else ref.at[:, sl, :]
    sl = pl.ds(pl.multiple_of(idx * size, 128), size)
    return ref.at[:, sl] if len(ref.shape) == 2 else ref.at[:, :, sl]


def _small_allgather(v, *, name, silu=False, deps=()):
    n = v.shape[1]

    def body(v_ref, out_ref, send_sems, recv_sems):
        x, y, c = _coords()
        me = 4 * x + 2 * y + c
        val = v_ref[...]
        out_ref[me] = val * jax.nn.sigmoid(val) if silu else val
        copies = []
        for k in range(1, N_DEV):
            peer = (_flip(x, k & 4), _flip(y, k & 2), _flip(c, k & 1))
            copies.append(pltpu.make_async_remote_copy(
                src_ref=out_ref.at[me], dst_ref=out_ref.at[me], send_sem=send_sems.at[k - 1],
                recv_sem=recv_sems.at[k - 1], device_id=peer, device_id_type=MESH))
        for cp in copies:
            cp.start()
        for cp in copies:
            cp.wait_recv()
        for cp in copies:
            cp.wait_send()

    return _pcall(body, name=name, out_shape=jax.ShapeDtypeStruct((N_DEV, 1, n), F32),
                  in_specs=[pl.BlockSpec(memory_space=pltpu.VMEM)], out_specs=pl.BlockSpec(memory_space=pltpu.VMEM),
                  scratch=[pltpu.SemaphoreType.DMA((N_DEV - 1,)), pltpu.SemaphoreType.DMA((N_DEV - 1,))],
                  deps=deps)(v)


def _cast_place(w, layer, axis, me, *, name):
    _, r, c = w.shape
    tr = _rows(r, c)
    nrt = r // tr

    def body(me_ref, w_ref, o_ref):
        o_ref[...] = w_ref[...].astype(BF16)

    wspec = pl.BlockSpec((None, tr, c), lambda i, me_ref: (layer, i, 0))
    if axis == 0:
        ospec = pl.BlockSpec((tr, c), lambda i, me_ref: (me_ref[0] * nrt + i, 0))
        shape = (r * N_DEV, c)
    else:
        ospec = pl.BlockSpec((tr, c), lambda i, me_ref: (i, me_ref[0]))
        shape = (r, c * N_DEV)
    return _pcall(body, name=name, out_shape=jax.ShapeDtypeStruct(shape, BF16), grid=(nrt,), in_specs=[wspec],
                  out_specs=ospec, prefetch=1)(me, w)


def _pair_sum(grad, sib, core, axis, *, name):
    _, r, c = sib.shape
    tr = _rows(r, c)
    nrt = r // tr

    def body(core_ref, g_ref, s_ref, o_ref):
        o_ref[...] = (g_ref[...].astype(F32) + s_ref[...].astype(F32)).astype(BF16)

    if axis == 0:
        gspec = pl.BlockSpec((tr, c), lambda q, i, core_ref: ((2 * q + core_ref[0]) * nrt + i, 0))
    else:
        gspec = pl.BlockSpec((tr, c), lambda q, i, core_ref: (i, 2 * q + core_ref[0]))
    sspec = pl.BlockSpec((None, tr, c), lambda q, i, core_ref: (q, i, 0))
    return _pcall(body, name=name, out_shape=jax.ShapeDtypeStruct(sib.shape, BF16), grid=(N_CHIPS, nrt),
                  in_specs=[gspec, sspec], out_specs=sspec, prefetch=1)(core, grad, sib)


ANY_SPEC = pl.BlockSpec(memory_space=pl.ANY)
SEM_SPEC = pl.BlockSpec(memory_space=pltpu.SEMAPHORE)
SPLIT_PARAMS = dict(has_side_effects=pltpu.SideEffectType.DATAFLOW_SIDE_EFFECTING)


def _split_start(copies_fn, buffers, sem_shape, after, *, name):
    n = len(buffers)
    rows, cols = sem_shape
    ns = rows * cols
    extra = ([] if after is None else [after]) + _take_token()

    def body(*refs):
        sems = refs[n + len(extra):n + len(extra) + 2 * ns]
        for cp in copies_fn(refs[:n], _sem_rows(sems[:ns], cols), _sem_rows(sems[ns:], cols)):
            cp.start()
        refs[-1][...] = jnp.zeros_like(refs[-1])

    sem = pltpu.SemaphoreType.DMA(())
    outs = pl.pallas_call(
        body, name=name,
        out_shape=((sem,) * (2 * ns) + tuple(jax.ShapeDtypeStruct(b.shape, b.dtype) for b in buffers) + (TOKEN,)),
        in_specs=(ANY_SPEC,) * (n + len(extra)),
        out_specs=(SEM_SPEC,) * (2 * ns) + (ANY_SPEC,) * n + (pl.BlockSpec(memory_space=pltpu.VMEM),),
        input_output_aliases={i: 2 * ns + i for i in range(n)},
        compiler_params=pltpu.CompilerParams(**SPLIT_PARAMS))(*buffers, *extra)
    _ORDER["token"] = outs[-1]
    return list(outs[:ns]), list(outs[ns:2 * ns]), list(outs[2 * ns:2 * ns + n]), outs[-1]


def _split_wait(copies_fn, send_sems, recv_sems, buffers, after, sem_rows, *, name):
    n, ns = len(buffers), len(send_sems)
    cols = ns // sem_rows
    extra = ([] if after is None else [after]) + _take_token()

    def body(*refs):
        sems = refs[n:n + 2 * ns]
        copies = copies_fn(refs[:n], _sem_rows(sems[:ns], cols), _sem_rows(sems[ns:], cols))
        for cp in copies:
            cp.wait_send()
        for cp in copies:
            cp.wait_recv()
        refs[-1][...] = jnp.zeros_like(refs[-1])

    outs = pl.pallas_call(
        body, name=name, out_shape=tuple(jax.ShapeDtypeStruct(b.shape, b.dtype) for b in buffers) + (TOKEN,),
        in_specs=(ANY_SPEC,) * n + (SEM_SPEC,) * (2 * ns) + (ANY_SPEC,) * len(extra),
        out_specs=(ANY_SPEC,) * n + (pl.BlockSpec(memory_space=pltpu.VMEM),),
        input_output_aliases={i: i for i in range(n)},
        compiler_params=pltpu.CompilerParams(**SPLIT_PARAMS))(*buffers, *send_sems, *recv_sems, *extra)
    _ORDER["token"] = outs[-1]
    return list(outs[:n])


def _sem_rows(sems, cols):
    return [sems[i:i + cols] for i in range(0, len(sems), cols)]


def _empty_hbm(shape, dtype):
    return pltpu.with_memory_space_constraint(lax.empty(shape, dtype), pltpu.HBM)


class _SplitGather:
    def __init__(self, fulls, axes, tag):
        self.axes, self.tag, self.nt = list(axes), tag, len(fulls)
        self.sizes = [f.shape[ax] // N_DEV for f, ax in zip(fulls, axes)]
        self.fulls = list(fulls)

    def _slot(self, ref, t, dev):
        return _shard_of(ref, self.axes[t], 4 * dev[0] + 2 * dev[1] + dev[2], self.sizes[t])

    def _first_copies(self, refs, send_sems, recv_sems):
        x, y, c = _coords()
        peers = [(x, y, 1 - c), (1 - x, y, c), (x, 1 - y, c), (1 - x, 1 - y, c)]
        return [pltpu.make_async_remote_copy(
            src_ref=self._slot(refs[t], t, (x, y, c)), dst_ref=self._slot(refs[t], t, (x, y, c)),
            send_sem=send_sems[t][k], recv_sem=recv_sems[t][k], device_id=peer, device_id_type=MESH)
            for t in range(self.nt) for k, peer in enumerate(peers)]

    def _forward_copies(self, refs, send_sems, recv_sems):
        x, y, c = _coords()
        chips = [(1 - x, y), (x, 1 - y), (1 - x, 1 - y)]
        return [pltpu.make_async_remote_copy(
            src_ref=self._slot(refs[t], t, (*chip, c)), dst_ref=self._slot(refs[t], t, (*chip, c)),
            send_sem=send_sems[t][j], recv_sem=recv_sems[t][j], device_id=(x, y, 1 - c), device_id_type=MESH)
            for t in range(self.nt) for j, chip in enumerate(chips)]

    def first(self, after):
        self.s1, self.r1, self.fulls, token = _split_start(
            self._first_copies, self.fulls, (self.nt, 4), after, name=f"comm_gather1_start_{self.tag}")
        return token

    def forward(self, after):
        bufs = _split_wait(self._first_copies, self.s1, self.r1, self.fulls, after, self.nt,
                           name=f"comm_gather1_wait_{self.tag}")
        self.s2, self.r2, self.fulls, token = _split_start(
            self._forward_copies, bufs, (self.nt, 3), after, name=f"comm_gather2_start_{self.tag}")
        return token

    def finish(self, after):
        return _split_wait(self._forward_copies, self.s2, self.r2, self.fulls, after, self.nt,
                           name=f"comm_gather2_wait_{self.tag}")


class _SplitPairExchange:
    def __init__(self, grads, axes, tag):
        self.nt, self.tag, self.axes = len(grads), tag, list(axes)
        self.grads = list(grads)
        self.sizes = [g.shape[ax] // N_DEV for g, ax in zip(grads, axes)]

    def _copies(self, refs, send_sems, recv_sems):
        nt = self.nt
        x, y, c = _coords()
        return [pltpu.make_async_remote_copy(
            src_ref=_shard_of(refs[t], self.axes[t], 2 * q + 1 - c, self.sizes[t]), dst_ref=refs[nt + t].at[q],
            send_sem=send_sems[t][q], recv_sem=recv_sems[t][q], device_id=(x, y, 1 - c), device_id_type=MESH)
            for t in range(nt) for q in range(N_CHIPS)]

    def start(self):
        landing = []
        for g, ax in zip(self.grads, self.axes):
            dims = list(g.shape)
            dims[ax] //= N_DEV
            landing.append(_empty_hbm((N_CHIPS, *dims), g.dtype))
        self.s, self.r, self.bufs, token = _split_start(
            self._copies, self.grads + landing, (self.nt, N_CHIPS), None,
            name=f"comm_rs_pair_start_{self.tag}")
        return token

    def finish(self, after):
        bufs = _split_wait(self._copies, self.s, self.r, self.bufs, after, self.nt,
                           name=f"comm_rs_pair_wait_{self.tag}")
        return bufs[:self.nt], bufs[self.nt:]


class _ReducePipeline:
    def __init__(self, core):
        self.core, self.items, self.done, self.now = core, [], [], 0

    def add(self, keys, grads, layer):
        axes = [SHARD_AXIS[k] for k in keys]
        pair = _SplitPairExchange([grads[k] for k in keys], axes, f"{keys[0]}{layer}")
        token = pair.start()
        self.items.append(dict(keys=keys, layer=layer, axes=axes, pair=pair, state="pair", since=self.now))
        return [token]

    def tick(self, after, flush=False):
        self.now += 1
        deps = []
        for it in self.items:
            if it["state"] == "pair" and it["since"] < self.now:
                grads, sib = it["pair"].finish(after)
                sums = [_pair_sum(g, s_, self.core, ax, name="pair_sum_" + k)
                        for k, g, s_, ax in zip(it["keys"], grads, sib, it["axes"])]
                it["chip"] = _SplitChipExchange(sums, f"{it['keys'][0]}{it['layer']}")
                deps.append(it["chip"].start())
                it.update(state="chip", since=self.now)
            elif it["state"] == "chip" and (flush or self.now - it["since"] >= 2):
                sums, remote = it["chip"].finish(after)
                self.done.append((it["keys"], it["layer"], sums, remote))
                it["state"] = "done"
        return deps

    def take_done(self):
        out, self.done = self.done, []
        return out


class _SplitChipExchange:
    def __init__(self, sums, tag):
        self.nt, self.tag = len(sums), tag
        self.sums = list(sums)

    def _copies(self, refs, send_sems, recv_sems):
        nt = self.nt
        x, y, c = _coords()
        copies = []
        for t in range(nt):
            for k in range(1, N_CHIPS):
                px, py = _flip(x, k & 2), _flip(y, k & 1)
                copies.append(pltpu.make_async_remote_copy(
                    src_ref=refs[t].at[2 * px + py], dst_ref=refs[nt + t].at[k - 1], send_sem=send_sems[t][k - 1],
                    recv_sem=recv_sems[t][k - 1], device_id=(px, py, c), device_id_type=MESH))
        return copies

    def start(self):
        landing = [_empty_hbm((N_CHIPS - 1,) + s.shape[1:], s.dtype) for s in self.sums]
        self.s, self.r, self.bufs, token = _split_start(
            self._copies, self.sums + landing, (self.nt, N_CHIPS - 1), None,
            name=f"comm_rs_chip_start_{self.tag}")
        return token

    def finish(self, after):
        bufs = _split_wait(self._copies, self.s, self.r, self.bufs, after, self.nt,
                           name=f"comm_rs_chip_wait_{self.tag}")
        return bufs[:self.nt], bufs[self.nt:]


def _adam_math(g, w, m, v):
    m2 = ADAM_B1 * m + (1.0 - ADAM_B1) * g
    v2 = ADAM_B2 * v + (1.0 - ADAM_B2) * (g * g)
    m_hat = m2 / (1.0 - ADAM_B1 ** ADAM_STEP)
    v_hat = v2 / (1.0 - ADAM_B2 ** ADAM_STEP)
    delta = -ADAM_LR * (m_hat / (jnp.sqrt(v_hat) + ADAM_EPS) + ADAM_WD * w)
    return delta, m2, v2


def _adamw_sharded(chip_sums, remote, chip, w, m, v, layer, prev, deps, *, name):
    nl, r, c = w.shape
    tr = _rows(r, c)

    def body(*refs):
        p_ref, r0_ref, r1_ref, r2_ref, w_ref, m_ref, v_ref = refs[1:8]
        g_out, d_out, m_out, v_out = refs[-4:]
        g = ((p_ref[...].astype(F32) + r0_ref[...].astype(F32)) + r1_ref[...].astype(F32)) + r2_ref[...].astype(F32)
        g_out[...] = g
        d_out[...], m_out[...], v_out[...] = _adam_math(g, w_ref[...], m_ref[...], v_ref[...])

    pspec = pl.BlockSpec((None, tr, c), lambda i, chip_ref: (chip_ref[0], i, 0))

    def rspec(k):
        return pl.BlockSpec((None, tr, c), lambda i, chip_ref: (k, i, 0))

    wspec = pl.BlockSpec((None, tr, c), lambda i, chip_ref: (layer, i, 0))
    in_specs = [pspec, rspec(0), rspec(1), rspec(2), wspec, wspec, wspec]
    args = [chip, chip_sums, remote, remote, remote, w, m, v]
    aliases = {}
    if prev is not None:
        in_specs += [pl.BlockSpec(memory_space=pl.ANY)] * 4
        aliases = {len(args) + i: i for i in range(4)}
        args += list(prev)
    shp = jax.ShapeDtypeStruct(w.shape, F32)
    return _pcall(body, name=name, out_shape=(shp,) * 4, grid=(r // tr,), in_specs=in_specs, out_specs=(wspec,) * 4,
                  aliases=aliases, prefetch=1, deps=deps)(*args)


def _adamw_local(g, w, m, v, *, name):
    nl, r, c = w.shape
    tr = _rows(r, c)

    def body(g_ref, w_ref, m_ref, v_ref, d_out, m_out, v_out):
        d_out[...], m_out[...], v_out[...] = _adam_math(g_ref[...], w_ref[...], m_ref[...], v_ref[...])

    spec = pl.BlockSpec((None, tr, c), lambda l, i: (l, i, 0))
    shp = jax.ShapeDtypeStruct(w.shape, F32)
    return _pcall(body, name=name, out_shape=(shp,) * 3, grid=(nl, r // tr), in_specs=[spec] * 4,
                  out_specs=(spec,) * 3)(g, w, m, v)


def _adamw_replicated(parts, w, m, v, *, name):
    n = w.shape[1]

    def body(p_ref, w_ref, m_ref, v_ref, g_out, d_out, m_out, v_out):
        g = p_ref[0]
        for k in range(1, N_DEV):
            g = g + p_ref[k]
        g_out[...] = g
        d_out[...], m_out[...], v_out[...] = _adam_math(g, w_ref[...], m_ref[...], v_ref[...])

    vm = pl.BlockSpec(memory_space=pltpu.VMEM)
    shp = jax.ShapeDtypeStruct((1, n), F32)
    return _pcall(body, name=name, out_shape=(shp,) * 4, in_specs=[vm] * 4, out_specs=(vm,) * 4)(parts, w, m, v)


def _group_views(qk, proj, g, dil, seq):
    if dil == 1:
        return (qk, qk, proj), (0, A_HEADS, 2 * A_HEADS)
    length = seq // dil
    lo = g * GROUP_W
    q = qk[:, lo:lo + GROUP_W].reshape(length, dil * GROUP_W)
    k = qk[:, A_W + lo:A_W + lo + GROUP_W].reshape(length, dil * GROUP_W)
    v = proj[:, OFF_VA + lo:OFF_VA + lo + GROUP_W].astype(BF16).reshape(length, dil * GROUP_W)
    return (q, k, v), (0, 0, 0)


def _mod_rows(mod, d):
    return [mod[:, i * d:(i + 1) * d] for i in range(6)]


MIXER_W = ("w_in", "w_branch_a", "w_branch_b", "w_out")
FFN_W = ("w_gate_up", "w_down")
SHARD_AXIS = {"w_in": 1, "w_branch_a": 1, "w_branch_b": 1, "w_out": 0, "w_gate_up": 1, "w_down": 0}


def _norm_args(mod, gain, which, d):
    rows = _mod_rows(mod, d)
    return gain, rows[3 * which + 1], rows[3 * which]


def _mixer_fwd_a(h, u, gains, w_in, cos2, sin2):
    seq = h.shape[0]
    proj = _mm(u, w_in, name="mm_in")
    qk = _qkrope_fwd(proj, gains, cos2, sin2, name="qkrope_fwd")
    os_, lses, views = [], [], []
    for g, dil in enumerate(DILATIONS):
        arrs, offs = _group_views(qk, proj, g, dil, seq)
        o, lse = _dil_fwd(*arrs, offs, seq // dil, dil, name=f"dil_fwd_{dil}")
        views.append((arrs, offs, o, lse))
        os_.append(o.reshape(seq, GROUP_W))
        lses.append(lse.reshape(seq, GROUP_W))
    o_a = _combine_fwd(os_, lses, name="combine_fwd")
    o_b = _sb_fwd(proj, name="sb_fwd")
    return dict(h_in=h, u=u, proj=proj, views=views, os=os_, lses=lses, o_a=o_a, o_b=o_b)


def _mixer_fwd_b(sv, mod, g2, wts):
    d = sv["h_in"].shape[1]
    merged, y_a, y_b = _mm_merge(sv["o_a"], sv["o_b"], wts["w_branch_a"], wts["w_branch_b"], sv["proj"],
                                 name="mm_branch")
    h_mid, t, u2 = _mm_resid_norm(merged, wts["w_out"], sv["h_in"], _mod_rows(mod, d)[2], _norm_args(mod, g2, 1, d),
                                  name="mm_out")
    sv.update(y_a=y_a, y_b=y_b, merged=merged, t=t, h_mid=h_mid, u2=u2)
    return h_mid


def _ffn_fwd_a(sv, w_gate_up):
    a, g, u = _mm_swiglu(sv["u2"], w_gate_up, name="mm_gate_up")
    sv.update(g=g, up=u, a=a)
    return a


def _ffn_fwd_b(sv, mod, w_down, next_norm):
    d = sv["h_mid"].shape[1]
    h_out, sv["f"], u_next = _mm_resid_norm(sv["a"], w_down, sv["h_mid"], _mod_rows(mod, d)[5], next_norm,
                                            name="mm_down")
    return h_out, u_next


def _wgrad(act, dout, key):
    return _mm(act, dout, ta=True, out_dtype=BF16, name="mm_wgrad_" + key)


def _ffn_bwd(dh, df, dgate2, sv, mod, g2, wts, hook):
    d = dh.shape[1]
    sc2, ga1 = _mod_rows(mod, d)[4], _mod_rows(mod, d)[2]
    dg, dup = _mm_down_t_swiglu(df, wts["w_down"], sv["g"], sv["up"], name="mm_down_t")
    grads = {"w_down": _wgrad(sv["a"], df, "w_down")}
    hook(dup)
    du2 = _mm_cat_k(dg, dup, wts["w_gate_up"], name="mm_gate_up_t")
    grads["w_gate_up"] = _mm_cat_n(sv["u2"], dg, dup, name="mm_wgrad_w_gate_up")
    dh_mid, dsh2, dsc2, dg2, dt, dgate1 = _rmsmod_bwd(du2, sv["h_mid"], g2, sc2, dh, sv["t"], ga1, name="rmsmod_bwd")
    return dh_mid, [dsh2, dsc2, dgate2], dg2, grads, dt, dgate1


def _mixer_bwd(dh_mid, dt, dgate1, sv, mod, g1, gains, wts, cos2, sin2, hook, below):
    seq, d = dh_mid.shape
    sc1 = _mod_rows(mod, d)[1]
    dmerged = _mm(dt, wts["w_out"], tb=True, name="mm_out_t")
    grads = {"w_out": _wgrad(sv["merged"], dt, "w_out")}
    dy_a, dy_b, dga, dgb = _merge_bwd(dmerged, sv["proj"], sv["y_a"], sv["y_b"], name="merge_bwd")
    do_a = _mm(dy_a, wts["w_branch_a"], tb=True, name="mm_branch_t")
    do_b = _mm(dy_b, wts["w_branch_b"], tb=True, name="mm_branch_t")
    grads["w_branch_a"] = _wgrad(sv["o_a"], dy_a, "w_branch_a")
    grads["w_branch_b"] = _wgrad(sv["o_b"], dy_b, "w_branch_b")
    dqb, dkb, dvb = _sb_bwd(sv["proj"], do_b, name="sb_bwd")
    comb = _combine_bwd(do_a, sv["os"], sv["lses"], name="combine_bwd", deps=hook(dqb, grads))
    grads = {}
    dos, dls = comb[:3], comb[3:]
    dqs, dks, dvs = [], [], []
    for g, dil in enumerate(DILATIONS):
        length = seq // dil
        arrs, offs, o_view, lse_view = sv["views"][g]
        view = (length, dil * GROUP_W)
        dq, dk, dv = _dil_bwd(*arrs, offs, o_view, lse_view, dos[g].reshape(view), dls[g].reshape(view), length, dil,
                              name=f"dil_bwd_{dil}")
        dqs.append(dq.reshape(seq, GROUP_W))
        dks.append(dk.reshape(seq, GROUP_W))
        dvs.append(dv.reshape(seq, GROUP_W))
    dq_pre, dqn = _qkrope_bwd(dqs, sv["proj"], gains, 0, cos2, sin2, name="qkrope_bwd")
    dk_pre, dkn = _qkrope_bwd(dks, sv["proj"], gains, 1, cos2, sin2, name="qkrope_bwd")
    dgains = jnp.stack([dqn, dkn])
    dproj = _assemble([dq_pre, dk_pre] + dvs + [dqb, dkb, dvb, dga, dgb], name="assemble_dproj")
    du = _mm(dproj, wts["w_in"], tb=True, name="mm_in_t")
    grads["w_in"] = _wgrad(sv["u"], dproj, "w_in")
    dh_in, dsh1, dsc1, dg1, df, dgate2 = _rmsmod_bwd(du, sv["h_in"], g1, sc1, dh_mid, *(below or (None, None)),
                                                     name="rmsmod_bwd")
    return dh_in, [dsh1, dsc1, dgate1], dg1, dgains, grads, df, dgate2


def kernel(x, c, w_ada, b_ada, norm1_g, norm2_g, w_in, qn_g, kn_g, w_branch_a, w_branch_b, w_out, w_gate_up, w_down, loss_target, m_w_ada, m_b_ada, m_norm1_g, m_norm2_g, m_w_in, m_qn_g, m_kn_g, m_w_branch_a, m_w_branch_b, m_w_out, m_w_gate_up, m_w_down, v_w_ada, v_b_ada, v_norm1_g, v_norm2_g, v_w_in, v_qn_g, v_kn_g, v_w_branch_a, v_w_branch_b, v_w_out, v_w_gate_up, v_w_down):
    _ORDER["token"] = None
    seq, d = x.shape[1], x.shape[2]
    depth = w_in.shape[0]
    weights = dict(w_in=w_in, w_branch_a=w_branch_a, w_branch_b=w_branch_b, w_out=w_out, w_gate_up=w_gate_up,
                   w_down=w_down)
    moments_m = dict(w_in=m_w_in, w_branch_a=m_w_branch_a, w_branch_b=m_w_branch_b, w_out=m_w_out,
                     w_gate_up=m_w_gate_up, w_down=m_w_down)
    moments_v = dict(w_in=v_w_in, w_branch_a=v_w_branch_a, w_branch_b=v_w_branch_b, w_out=v_w_out,
                     w_gate_up=v_w_gate_up, w_down=v_w_down)
    xi, yi, ci = _coords()
    me = 4 * xi + 2 * yi + ci
    core = jnp.reshape(ci, (1,)).astype(jnp.int32)
    chip = jnp.reshape(2 * xi + yi, (1,)).astype(jnp.int32)

    ada_w = w_ada.shape[2]
    c_act = _small_allgather(c, name="comm_gather_c", silu=True).reshape(N_DEV, d)
    c_pad = jnp.concatenate([c_act, jnp.zeros_like(c_act)], axis=0).astype(BF16)
    bias = lax.dynamic_slice(b_ada, (0, me * ada_w), (depth, ada_w))
    mod_part = jnp.stack([_mm(c_pad, w_ada[l], name="mm_ada")[:N_DEV] for l in range(depth)]) + bias[:, None, :]
    mod_all = _small_allgather(mod_part.reshape(1, depth * N_DEV * ada_w), name="comm_gather_mod")
    mod_all = mod_all.reshape(N_DEV, depth, N_DEV, ada_w)
    mod_mine = lax.dynamic_index_in_dim(mod_all, me, axis=2, keepdims=False)
    mods = jnp.transpose(mod_mine, (1, 0, 2)).reshape(depth, 1, 6 * d)

    cos2, sin2 = _rope_tables(seq)
    gains = [jnp.stack([qn_g[l], kn_g[l]])[:, None, :] for l in range(depth)]
    g1s = [norm1_g[l][None] for l in range(depth)]
    g2s = [norm2_g[l][None] for l in range(depth)]

    me_arr = jnp.reshape(me, (1,)).astype(jnp.int32)

    def placed(keys, l):
        return [_cast_place(weights[k], l, SHARD_AXIS[k], me_arr, name="cast_place_" + k) for k in keys]

    def gather_of(keys, l, tag):
        return _SplitGather(placed(keys, l), [SHARD_AXIS[k] for k in keys], f"{tag}{l}")

    groups = [("w_in", 0, MIXER_W[:1]), ("rest", 0, MIXER_W[1:]), ("ffn", 0, FFN_W)]
    for l in range(1, depth):
        groups += [("mixer", l, MIXER_W), ("ffn", l, FFN_W)]
    gathers, token = {}, mods
    for tag, l, keys in groups:
        gathers[tag, l] = gather_of(keys, l, tag)
        token = gathers[tag, l].first(after=token)
    h = x[0]
    u = _rmsmod_fwd(h, *_norm_args(mods[0], g1s[0], 0, d), name="rmsmod_fwd")
    token = gathers["w_in", 0].forward(after=u)
    wm = {"w_in": gathers["w_in", 0].finish(after=token)[0]}
    saved, full = [], []
    for l in range(depth):
        last = l + 1 == depth
        sv = _mixer_fwd_a(h, u, gains[l], wm["w_in"], cos2, sin2)
        gathers["ffn", l].forward(after=sv["o_b"])
        if l == 0:
            gathers["rest", 0].forward(after=sv["o_b"])
            wm.update(zip(MIXER_W[1:], gathers["rest", 0].finish(after=sv["o_b"])))
        h_mid = _mixer_fwd_b(sv, mods[l], g2s[l], wm)
        wf = dict(zip(FFN_W, gathers["ffn", l].finish(after=h_mid)))
        a = _ffn_fwd_a(sv, wf["w_gate_up"])
        if not last:
            gathers["mixer", l + 1].forward(after=a)
        h, u = _ffn_fwd_b(sv, mods[l], wf["w_down"],
                          None if last else _norm_args(mods[l + 1], g1s[l + 1], 0, d))
        saved.append(sv)
        full.append({**wm, **wf})
        if not last:
            wm = dict(zip(MIXER_W, gathers["mixer", l + 1].finish(after=h)))
    def ffn_gate(l):
        return saved[l]["f"], _mod_rows(mods[l], d)[5]

    loss_part, dh, df, dgate2 = _loss_fwd(h, loss_target[0], *ffn_gate(depth - 1), name="loss")
    loss = lax.psum(loss_part[0, 0], ("x", "y", "c"))

    pipe = _ReducePipeline(core)
    dmods, dg1s, dg2s, dgains = [None] * depth, [None] * depth, [None] * depth, [None] * depth
    for l in reversed(range(depth)):
        dh_mid, dmod_f, dg2s[l], grads, dt, dgate1 = _ffn_bwd(dh, df, dgate2, saved[l], mods[l], g2s[l], full[l],
                                                              pipe.tick)
        pipe.tick(dh_mid)
        pipe.add(FFN_W, grads, l)
        dh, dmod_m, dg1s[l], dgains[l], grads, df, dgate2 = _mixer_bwd(
            dh_mid, dt, dgate1, saved[l], mods[l], g1s[l], gains[l], full[l], cos2, sin2,
            lambda after, early, l=l: pipe.tick(after) + pipe.add(MIXER_W[1:], early, l),
            ffn_gate(l - 1) if l > 0 else None)
        dmods[l] = jnp.concatenate(dmod_m + dmod_f, axis=1)
        pipe.tick(dh)
        pipe.add(MIXER_W[:1], grads, l)
    grad_x = dh[None]

    stacked = {}

    def update(items):
        for keys, l, sums, remote in items:
            for k, p_, r_ in zip(keys, sums, remote):
                stacked[k] = _adamw_sharded(p_, r_, chip, weights[k], moments_m[k], moments_v[k], l,
                                            stacked.get(k), [], name="adamw_" + k)

    ready = pipe.take_done()
    update([it for it in ready if it[0] != FFN_W])

    small = jnp.concatenate(
        dmods + dg1s + dg2s + [dgains[l][0] for l in range(depth)] + [dgains[l][1] for l in range(depth)], axis=1)
    small_all = _small_allgather(small, name="comm_gather_small")
    pipe.tick(small_all)
    update([it for it in ready if it[0] == FFN_W] + pipe.take_done())

    def pack(b, n1, n2, qn, kn):
        return jnp.concatenate([t_.reshape(1, -1) for t_ in (b, n1, n2, qn, kn)], axis=1)

    sg, sd, sm, sv_ = _adamw_replicated(small_all, pack(b_ada, norm1_g, norm2_g, qn_g, kn_g),
                                        pack(m_b_ada, m_norm1_g, m_norm2_g, m_qn_g, m_kn_g),
                                        pack(v_b_ada, v_norm1_g, v_norm2_g, v_qn_g, v_kn_g), name="adamw_replicated")

    def unpack(p):
        sizes = [depth * 6 * d, depth * d, depth * d, depth * HEAD_DIM, depth * HEAD_DIM]
        shapes = [b_ada.shape, norm1_g.shape, norm2_g.shape, qn_g.shape, kn_g.shape]
        out, off = [], 0
        for n, shp in zip(sizes, shapes):
            out.append(p[0, off:off + n].reshape(shp))
            off += n
        return dict(zip(("b_ada", "norm1_g", "norm2_g", "qn_g", "kn_g"), out))

    ug, ud, um, uv = unpack(sg), unpack(sd), unpack(sm), unpack(sv_)
    res = {k: dict(g=ug[k], d=ud[k], m=um[k], v=uv[k]) for k in ug}

    dmod_all = small_all[:, 0, :depth * 6 * d].reshape(N_DEV, depth, 6 * d)
    g_ada = None
    for l in range(depth):
        dm = lax.dynamic_slice(dmod_all[:, l, :], (0, me * ada_w), (N_DEV, ada_w))
        dm = jnp.concatenate([dm, jnp.zeros_like(dm)], axis=0).astype(BF16)
        g_ada = _mm(c_pad, dm, ta=True, name="mm_wgrad_ada", stack=(l, depth, g_ada))
    d_ada, m_ada, v_ada = _adamw_local(g_ada, w_ada, m_w_ada, v_w_ada, name="adamw_local")
    res["w_ada"] = dict(g=g_ada, d=d_ada, m=m_ada, v=v_ada)

    pipe.tick(d_ada)
    update(pipe.take_done())
    pipe.tick(d_ada, flush=True)
    update(pipe.take_done())
    for k, (g_, d_, m_, v_) in stacked.items():
        res[k] = dict(g=g_, d=d_, m=m_, v=v_)

    order = ("w_ada", "b_ada", "norm1_g", "norm2_g", "w_in", "qn_g", "kn_g", "w_branch_a", "w_branch_b", "w_out",
             "w_gate_up", "w_down")
    _ORDER["token"] = None
    return (loss, grad_x, *[res[k]["g"] for k in order], *[res[k]["d"] for k in order],
            *[res[k]["m"] for k in order], *[res[k]["v"] for k in order])
```

```python
import functools

import jax
import jax.numpy as jnp
from jax import lax
from jax.experimental import pallas as pl
from jax.experimental.pallas import tpu as pltpu

F32 = jnp.float32
BF16 = jnp.bfloat16

HEAD_DIM = 128
BLOCK = 128
DILATIONS = (1, 4, 16)
HEADS_PER_GROUP = 4
A_HEADS = 12
SB_HEADS = 4
GROUP_W = HEADS_PER_GROUP * HEAD_DIM
A_W = A_HEADS * HEAD_DIM
B_W = SB_HEADS * HEAD_DIM
OFF_QA, OFF_KA, OFF_VA = 0, A_W, 2 * A_W
OFF_QB, OFF_KB, OFF_VB = 3 * A_W, 3 * A_W + B_W, 3 * A_W + 2 * B_W
OFF_GATES = 3 * A_W + 3 * B_W
ROPE_THETA = 10000.0
EPS = 1e-6
ATT_SCALE = HEAD_DIM ** -0.5
MASKED = -1e30

ADAM_LR, ADAM_B1, ADAM_B2, ADAM_EPS, ADAM_WD, ADAM_STEP = 0.001, 0.9, 0.999, 1e-08, 0.01, 10

N_DEV = 8
N_CHIPS = 4
V7X_VMEM_LIMIT_BYTES = 56 * 1024 * 1024
ELEMWISE_BLOCK_BYTES = 2 * 1024 * 1024
MESH = pl.DeviceIdType.MESH

NN = (((1,), (0,)), ((), ()))
NT = (((1,), (1,)), ((), ()))
TN = (((0,), (0,)), ((), ()))


def _dot(a, b, dims=NN):
    return lax.dot_general(a, b, dims, preferred_element_type=F32)


def _tile(n, cap, mult=128):
    best = None
    for t in range(mult, min(n, cap) + 1, mult):
        if n % t == 0:
            best = t
    if best is None:
        assert n <= 2 * cap, (n, cap)
        return n
    return best


def _rows(r, c):
    return _tile(r, max(16, ELEMWISE_BLOCK_BYTES // (4 * c)), 16)


_ORDER = {"token": None}
TOKEN = jax.ShapeDtypeStruct((8, 128), F32)


def _take_token():
    prev = _ORDER["token"]
    return [] if prev is None else [prev]


def _pcall(body, *, name, out_shape, grid=None, in_specs=None, out_specs=None, scratch=(), aliases=None,
           prefetch=0):
    single = not isinstance(out_shape, (tuple, list))
    out_shapes = [out_shape] if single else list(out_shape)
    out_specs = [out_specs] if single else list(out_specs)
    extra = _take_token()
    n_in, n_extra, n_out = prefetch + len(in_specs), len(extra), len(out_shapes)

    def wrapped(*refs):
        token = refs[n_in + n_extra + n_out]
        token[...] = jnp.zeros_like(token)
        return body(*refs[:n_in], *refs[n_in + n_extra:n_in + n_extra + n_out], *refs[n_in + n_extra + n_out + 1:])

    in_specs = list(in_specs) + [pl.BlockSpec(memory_space=pl.ANY)] * n_extra
    if grid is None:
        out_specs.append(pl.BlockSpec(memory_space=pltpu.VMEM))
    else:
        out_specs.append(pl.BlockSpec(TOKEN.shape, lambda *_: (0, 0)))
    kwargs = dict(name=name, out_shape=out_shapes + [TOKEN], input_output_aliases=aliases or {},
                  compiler_params=pltpu.CompilerParams(vmem_limit_bytes=V7X_VMEM_LIMIT_BYTES))
    if prefetch:
        call = pl.pallas_call(wrapped, grid_spec=pltpu.PrefetchScalarGridSpec(
            num_scalar_prefetch=prefetch, grid=grid, in_specs=in_specs, out_specs=out_specs,
            scratch_shapes=list(scratch)), **kwargs)
    else:
        if grid is not None:
            kwargs["grid"] = grid
        call = pl.pallas_call(wrapped, in_specs=in_specs, out_specs=out_specs, scratch_shapes=list(scratch), **kwargs)

    def run(*args):
        outs = call(*args, *extra)
        _ORDER["token"] = outs[-1]
        return outs[0] if single else tuple(outs[:-1])

    return run


def _mm(a, b, *, name, ta=False, tb=False, out_dtype=F32, caps=(1024, 1024, 3072), stack=None):
    kdim, m = a.shape if ta else a.shape[::-1]
    n, k2 = b.shape if tb else b.shape[::-1]
    assert kdim == k2, (a.shape, b.shape, ta, tb)
    tm, tn, tk = _tile(m, caps[0]), _tile(n, caps[1]), _tile(kdim, caps[2])
    nk = kdim // tk
    dims = (((0 if ta else 1,), (1 if tb else 0,)), ((), ()))

    def body(*refs):
        a_ref, b_ref = refs[0], refs[1]
        part = _dot(a_ref[...].astype(BF16), b_ref[...].astype(BF16), dims)
        if nk == 1:
            o_ref = refs[-1]
            o_ref[...] = part.astype(o_ref.dtype)
            return
        o_ref, acc_ref = refs[-2], refs[-1]
        k = pl.program_id(2)

        @pl.when(k == 0)
        def _():
            acc_ref[...] = part

        @pl.when(k > 0)
        def _():
            acc_ref[...] += part

        @pl.when(k == nk - 1)
        def _():
            o_ref[...] = acc_ref[...].astype(o_ref.dtype)

    a_spec = (pl.BlockSpec((tk, tm), lambda i, j, k: (k, i)) if ta
              else pl.BlockSpec((tm, tk), lambda i, j, k: (i, k)))
    b_spec = (pl.BlockSpec((tn, tk), lambda i, j, k: (j, k)) if tb
              else pl.BlockSpec((tk, tn), lambda i, j, k: (k, j)))
    ins, in_specs, aliases = [a, b], [a_spec, b_spec], {}
    if stack is None:
        out_shape = jax.ShapeDtypeStruct((m, n), out_dtype)
        out_spec = pl.BlockSpec((tm, tn), lambda i, j, k: (i, j))
    else:
        layer, n_layers, buf = stack
        out_shape = jax.ShapeDtypeStruct((n_layers, m, n), out_dtype)
        out_spec = pl.BlockSpec((None, tm, tn), lambda i, j, k: (layer, i, j))
        if buf is not None:
            ins.append(buf)
            in_specs.append(pl.BlockSpec(memory_space=pl.ANY))
            aliases = {2: 0}
    scratch = [] if nk == 1 else [pltpu.VMEM((tm, tn), F32)]
    return _pcall(body, name=name, out_shape=out_shape, grid=(m // tm, n // tn, nk), in_specs=in_specs,
                  out_specs=out_spec, scratch=scratch, aliases=aliases)(*ins)


EPILOGUE_ROWS = 256


def _row_chunks(tm):
    return [slice(r, r + EPILOGUE_ROWS) for r in range(0, tm, EPILOGUE_ROWS)] if tm > EPILOGUE_ROWS else [slice(0, tm)]


def _mm_cat_k(a_lo, a_hi, b, *, name):
    m, f = a_lo.shape
    n = b.shape[0]
    tm, tn, tk = _tile(m, 1024), _tile(n, 1024), _tile(f, 3072)
    half = f // tk
    nk = 2 * half

    def body(lo_ref, hi_ref, b_ref, o_ref, acc_ref):
        k = pl.program_id(2)

        def accumulate(a_ref):
            part = _dot(a_ref[...], b_ref[...], NT)

            @pl.when(k == 0)
            def _():
                acc_ref[...] = part

            @pl.when(k > 0)
            def _():
                acc_ref[...] += part

        pl.when(k < half)(lambda: accumulate(lo_ref))
        pl.when(k >= half)(lambda: accumulate(hi_ref))

        @pl.when(k == nk - 1)
        def _():
            o_ref[...] = acc_ref[...]

    return _pcall(body, name=name, out_shape=jax.ShapeDtypeStruct((m, n), F32), grid=(m // tm, n // tn, nk),
                  in_specs=[pl.BlockSpec((tm, tk), lambda i, j, k: (i, jnp.minimum(k, half - 1))),
                            pl.BlockSpec((tm, tk), lambda i, j, k: (i, jnp.maximum(k - half, 0))),
                            pl.BlockSpec((tn, tk), lambda i, j, k: (j, k))],
                  out_specs=pl.BlockSpec((tm, tn), lambda i, j, k: (i, j)),
                  scratch=[pltpu.VMEM((tm, tn), F32)])(a_lo, a_hi, b)


def _mm_cat_n(a, b_lo, b_hi, *, name):
    s, m = a.shape
    f = b_lo.shape[1]
    tm, tn = _tile(m, 1024), _tile(f, 1024)
    half = f // tn

    def body(a_ref, lo_ref, hi_ref, o_ref):
        j = pl.program_id(1)

        @pl.when(j < half)
        def _():
            o_ref[...] = _dot(a_ref[...], lo_ref[...], TN).astype(BF16)

        @pl.when(j >= half)
        def _():
            o_ref[...] = _dot(a_ref[...], hi_ref[...], TN).astype(BF16)

    return _pcall(body, name=name, out_shape=jax.ShapeDtypeStruct((m, 2 * f), BF16), grid=(m // tm, 2 * half),
                  in_specs=[pl.BlockSpec((s, tm), lambda i, j: (0, i)),
                            pl.BlockSpec((s, tn), lambda i, j: (0, jnp.minimum(j, half - 1))),
                            pl.BlockSpec((s, tn), lambda i, j: (0, jnp.maximum(j - half, 0)))],
                  out_specs=pl.BlockSpec((tm, tn), lambda i, j: (i, j)))(a, b_lo, b_hi)


def _mm_resid_norm(a, w, h, gate, norm, *, name):
    s, kdim = a.shape
    d = w.shape[1]
    tk = _tile(kdim, 2048)
    nk = kdim // tk
    tm = _tile(s, 256 if nk == 1 else 512)

    def body(*refs):
        a_ref, w_ref, h_ref, gate_ref = refs[:4]
        outs = refs[7:] if norm is not None else refs[4:]

        def finish(rows, t):
            hn = h_ref[rows, :] + gate_ref[...] * t
            outs[0][rows, :] = hn
            outs[1][rows, :] = t.astype(BF16)
            if norm is not None:
                g_ref, sc_ref, sh_ref = refs[4:7]
                r = lax.rsqrt(jnp.mean(hn * hn, axis=-1, keepdims=True) + EPS)
                outs[2][rows, :] = (((hn * r) * g_ref[...]) * (1.0 + sc_ref[...]) + sh_ref[...]).astype(BF16)

        if nk == 1:
            for rows in _row_chunks(tm):
                finish(rows, _dot(a_ref[rows, :], w_ref[...]))
            return
        acc_ref = refs[-1]
        k = pl.program_id(1)

        @pl.when(k == 0)
        def _():
            acc_ref[...] = _dot(a_ref[...], w_ref[...])

        @pl.when(jnp.logical_and(k > 0, k < nk - 1))
        def _():
            acc_ref[...] += _dot(a_ref[...], w_ref[...])

        @pl.when(k == nk - 1)
        def _():
            for rows in _row_chunks(tm):
                finish(rows, acc_ref[rows, :] + _dot(a_ref[rows, :], w_ref[...]))

    row = pl.BlockSpec((tm, d), lambda i, k: (i, 0))
    vec = pl.BlockSpec((1, d), lambda i, k: (0, 0))
    in_specs = [pl.BlockSpec((tm, tk), lambda i, k: (i, k)), pl.BlockSpec((tk, d), lambda i, k: (k, 0)), row, vec]
    args = [a, w, h, gate]
    out_shape = [jax.ShapeDtypeStruct((s, d), F32), jax.ShapeDtypeStruct((s, d), BF16)]
    if norm is not None:
        in_specs += [vec, vec, vec]
        args += list(norm)
        out_shape.append(jax.ShapeDtypeStruct((s, d), BF16))
    outs = _pcall(body, name=name, out_shape=tuple(out_shape), grid=(s // tm, nk), in_specs=in_specs,
                  out_specs=(row,) * len(out_shape), scratch=[] if nk == 1 else [pltpu.VMEM((tm, d), F32)])(*args)
    return outs if norm is not None else (*outs, None)


def _mm_merge(o_a, o_b, w_a, w_b, proj, *, name):
    s = o_a.shape[0]
    d = w_a.shape[1]
    tm = _tile(s, 512)
    ga_blk = OFF_GATES // d

    def body(oa_ref, ob_ref, wa_ref, wb_ref, ga_ref, gb_ref, m_ref, ya_ref, yb_ref):
        for rows in _row_chunks(tm):
            ya, yb = _dot(oa_ref[rows, :], wa_ref[...]), _dot(ob_ref[rows, :], wb_ref[...])
            m_ref[rows, :] = (jax.nn.sigmoid(ga_ref[rows, :]) * ya
                              + jax.nn.sigmoid(gb_ref[rows, :]) * yb).astype(BF16)
            ya_ref[rows, :] = ya.astype(BF16)
            yb_ref[rows, :] = yb.astype(BF16)

    row = pl.BlockSpec((tm, d), lambda i: (i, 0))
    act = pl.BlockSpec((tm, o_a.shape[1]), lambda i: (i, 0))
    wspec = pl.BlockSpec(w_a.shape, lambda i: (0, 0))
    shp = jax.ShapeDtypeStruct((s, d), BF16)
    return _pcall(body, name=name, out_shape=(shp, shp, shp), grid=(s // tm,),
                  in_specs=[act, act, wspec, wspec, pl.BlockSpec((tm, d), lambda i: (i, ga_blk)),
                            pl.BlockSpec((tm, d), lambda i: (i, ga_blk + 1))],
                  out_specs=(row, row, row))(o_a, o_b, w_a, w_b, proj, proj)


def _mm_down_t_swiglu(df, w_down, g, u, *, name):
    s, d = df.shape
    f = w_down.shape[0]
    tm, tn = _tile(s, 1024), _tile(f, 512)

    def body(df_ref, w_ref, g_ref, u_ref, dg_ref, du_ref):
        w = w_ref[...]
        for rows in _row_chunks(tm):
            da = _dot(df_ref[rows, :], w, NT)
            gf = g_ref[rows, :].astype(F32)
            sg = jax.nn.sigmoid(gf)
            dg_ref[rows, :] = (da * u_ref[rows, :].astype(F32) * (sg * (1.0 + gf * (1.0 - sg)))).astype(BF16)
            du_ref[rows, :] = (da * (gf * sg)).astype(BF16)

    tile = pl.BlockSpec((tm, tn), lambda i, j: (i, j))
    shp = jax.ShapeDtypeStruct((s, f), BF16)
    return _pcall(body, name=name, out_shape=(shp, shp), grid=(s // tm, f // tn),
                  in_specs=[pl.BlockSpec((tm, d), lambda i, j: (i, 0)), pl.BlockSpec((tn, d), lambda i, j: (j, 0)),
                            tile, tile],
                  out_specs=(tile, tile))(df, w_down, g, u)


def _rmsmod_fwd(h, g, scale, shift, *, name):
    s, d = h.shape
    ts = _rows(s, d)

    def body(h_ref, g_ref, sc_ref, sh_ref, u_ref):
        hf = h_ref[...]
        r = lax.rsqrt(jnp.mean(hf * hf, axis=-1, keepdims=True) + EPS)
        u_ref[...] = (((hf * r) * g_ref[...]) * (1.0 + sc_ref[...]) + sh_ref[...]).astype(BF16)

    row = pl.BlockSpec((ts, d), lambda i: (i, 0))
    vec = pl.BlockSpec((1, d), lambda i: (0, 0))
    return _pcall(body, name=name, out_shape=jax.ShapeDtypeStruct((s, d), BF16), grid=(s // ts,),
                  in_specs=[row, vec, vec, vec], out_specs=row)(h, g, scale, shift)


def _gate_bwd(dhf, t_ref, gate_ref, dt_ref, dgate_ref):
    dt_ref[...] = (dhf * gate_ref[...]).astype(BF16)
    dgate_ref[...] += jnp.sum(dhf * t_ref[...], axis=0, keepdims=True)


def _rmsmod_bwd(du, h, g, scale, dres, t, gate, *, name):
    s, d = h.shape
    ts = _rows(s, d)
    chain = t is not None

    def body(*refs):
        du_ref, h_ref, g_ref, sc_ref, dres_ref = refs[:5]
        dh_ref, dsh_ref, dsc_ref, dg_ref = refs[-6:-2] if chain else refs[-4:]
        sums = (dsh_ref, dsc_ref, dg_ref) + ((refs[-1],) if chain else ())

        @pl.when(pl.program_id(0) == 0)
        def _():
            for ref in sums:
                ref[...] = jnp.zeros_like(ref)

        hf, duf, gain = h_ref[...], du_ref[...], g_ref[...]
        r = lax.rsqrt(jnp.mean(hf * hf, axis=-1, keepdims=True) + EPS)
        xh = hf * r
        dn = duf * (1.0 + sc_ref[...])
        dsh_ref[...] += jnp.sum(duf, axis=0, keepdims=True)
        dsc_ref[...] += jnp.sum(duf * (xh * gain), axis=0, keepdims=True)
        dg_ref[...] += jnp.sum(dn * xh, axis=0, keepdims=True)
        dxh = dn * gain
        dh = dres_ref[...] + r * (dxh - xh * jnp.mean(dxh * xh, axis=-1, keepdims=True))
        dh_ref[...] = dh
        if chain:
            _gate_bwd(dh, refs[5], refs[6], refs[-2], refs[-1])

    row = pl.BlockSpec((ts, d), lambda i: (i, 0))
    vec = pl.BlockSpec((1, d), lambda i: (0, 0))
    vshape = jax.ShapeDtypeStruct((1, d), F32)
    out_shape, out_specs = [jax.ShapeDtypeStruct((s, d), F32), vshape, vshape, vshape], [row, vec, vec, vec]
    in_specs, args = [row, row, vec, vec, row], [du, h, g, scale, dres]
    if chain:
        in_specs, args = in_specs + [row, vec], args + [t, gate]
        out_shape, out_specs = out_shape + [jax.ShapeDtypeStruct((s, d), BF16), vshape], out_specs + [row, vec]
    outs = _pcall(body, name=name, out_shape=tuple(out_shape), grid=(s // ts,), in_specs=in_specs,
                  out_specs=tuple(out_specs))(*args)
    return outs if chain else (*outs, None, None)


def _merge_bwd(dm, proj, y_a, y_b, *, name):
    s, d = y_a.shape
    ts = _rows(s, d)
    ga_blk = OFF_GATES // d

    def body(dm_ref, ga_ref, gb_ref, ya_ref, yb_ref, dya_ref, dyb_ref, dga_ref, dgb_ref):
        dmf = dm_ref[...]
        sa, sb = jax.nn.sigmoid(ga_ref[...]), jax.nn.sigmoid(gb_ref[...])
        dya_ref[...] = (dmf * sa).astype(BF16)
        dyb_ref[...] = (dmf * sb).astype(BF16)
        dga_ref[...] = (dmf * ya_ref[...] * (sa * (1.0 - sa))).astype(BF16)
        dgb_ref[...] = (dmf * yb_ref[...] * (sb * (1.0 - sb))).astype(BF16)

    row = pl.BlockSpec((ts, d), lambda i: (i, 0))
    ga = pl.BlockSpec((ts, d), lambda i: (i, ga_blk))
    gb = pl.BlockSpec((ts, d), lambda i: (i, ga_blk + 1))
    shp = jax.ShapeDtypeStruct((s, d), BF16)
    return _pcall(body, name=name, out_shape=(shp, shp, shp, shp), grid=(s // ts,),
                  in_specs=[row, ga, gb, row, row], out_specs=(row, row, row, row))(dm, proj, proj, y_a, y_b)


def _mm_swiglu(u2, w_gate_up, *, name):
    s, d = u2.shape
    f = w_gate_up.shape[1] // 2
    tm, tn = _tile(s, 1024), _tile(f, 512)
    nj = f // tn

    def body(x_ref, wg_ref, wu_ref, a_ref, g_ref, u_ref):
        for rows in _row_chunks(tm):
            x = x_ref[rows, :]
            gf, uf = _dot(x, wg_ref[...]), _dot(x, wu_ref[...])
            a_ref[rows, :] = ((gf * jax.nn.sigmoid(gf)) * uf).astype(BF16)
            g_ref[rows, :] = gf.astype(BF16)
            u_ref[rows, :] = uf.astype(BF16)

    out = pl.BlockSpec((tm, tn), lambda i, j: (i, j))
    shp = jax.ShapeDtypeStruct((s, f), BF16)
    return _pcall(body, name=name, out_shape=(shp, shp, shp), grid=(s // tm, nj),
                  in_specs=[pl.BlockSpec((tm, d), lambda i, j: (i, 0)), pl.BlockSpec((d, tn), lambda i, j: (0, j)),
                            pl.BlockSpec((d, tn), lambda i, j: (0, nj + j))],
                  out_specs=(out, out, out))(u2, w_gate_up, w_gate_up)


def _loss_fwd(y, tgt, t, gate, *, name):
    s, d = y.shape
    ts = _rows(s, d)

    def body(y_ref, tgt_ref, t_ref, gate_ref, l_ref, dy_ref, dt_ref, dgate_ref):
        @pl.when(pl.program_id(0) == 0)
        def _():
            l_ref[...] = jnp.zeros_like(l_ref)
            dgate_ref[...] = jnp.zeros_like(dgate_ref)

        e = y_ref[...] - tgt_ref[...]
        dy = e * (1.0 / d)
        dy_ref[...] = dy
        per_tok = jnp.sum(e * e, axis=1, keepdims=True) * (1.0 / d)
        l_ref[...] += 0.5 * jnp.sum(per_tok, axis=0, keepdims=True)
        _gate_bwd(dy, t_ref, gate_ref, dt_ref, dgate_ref)

    row = pl.BlockSpec((ts, d), lambda i: (i, 0))
    vec = pl.BlockSpec((1, d), lambda i: (0, 0))
    return _pcall(body, name=name,
                  out_shape=(jax.ShapeDtypeStruct((1, 128), F32), jax.ShapeDtypeStruct((s, d), F32),
                             jax.ShapeDtypeStruct((s, d), BF16), jax.ShapeDtypeStruct((1, d), F32)),
                  grid=(s // ts,), in_specs=[row, row, row, vec],
                  out_specs=(pl.BlockSpec((1, 128), lambda i: (0, 0)), row, row, vec))(y, tgt, t, gate)


def _rope_tables(seq):
    inv = jnp.power(ROPE_THETA, -jnp.arange(0, HEAD_DIM, 2, dtype=F32) / HEAD_DIM)
    ang = jnp.arange(seq, dtype=F32)[:, None] * inv[None, :]
    cos, sin = jnp.cos(ang), jnp.sin(ang)
    return jnp.concatenate([cos, cos], axis=1), jnp.concatenate([-sin, sin], axis=1)


def _qkrope_fwd(proj, gains, cos2, sin2, *, name):
    s = proj.shape[0]
    ts = _rows(s, A_W)

    def body(x_ref, g_ref, c_ref, s_ref, o_ref):
        gain, cos, sin = g_ref[...], c_ref[...], s_ref[...]
        for h in range(A_HEADS):
            lanes = slice(h * HEAD_DIM, (h + 1) * HEAD_DIM)
            x = x_ref[:, lanes]
            y = (x * lax.rsqrt(jnp.mean(x * x, axis=-1, keepdims=True) + EPS)) * gain
            o_ref[:, lanes] = (y * cos + pltpu.roll(y, HEAD_DIM // 2, 1) * sin).astype(BF16)

    heads = pl.BlockSpec((ts, A_W), lambda i, j: (i, j))
    tab = pl.BlockSpec((ts, HEAD_DIM), lambda i, j: (i, 0))
    gain = pl.BlockSpec((None, 1, HEAD_DIM), lambda i, j: (j, 0, 0))
    return _pcall(body, name=name, out_shape=jax.ShapeDtypeStruct((s, 2 * A_W), BF16),
                  grid=(s // ts, 2), in_specs=[heads, gain, tab, tab], out_specs=heads)(
                      proj, gains, cos2, sin2)


def _qkrope_bwd(d_groups, proj, gains, which, cos2, sin2, *, name):
    s = proj.shape[0]
    ts = _rows(s, A_W)

    def body(d0_ref, d1_ref, d2_ref, x_ref, g_ref, c_ref, s_ref, dx_ref, dg_ref):
        @pl.when(pl.program_id(0) == 0)
        def _():
            dg_ref[...] = jnp.zeros_like(dg_ref)

        gain, cos, sin = g_ref[...], c_ref[...], s_ref[...]
        dg = jnp.zeros((1, HEAD_DIM), F32)
        for h in range(A_HEADS):
            lanes = slice(h * HEAD_DIM, (h + 1) * HEAD_DIM)
            slot = slice((h % HEADS_PER_GROUP) * HEAD_DIM, (h % HEADS_PER_GROUP + 1) * HEAD_DIM)
            dout = (d0_ref, d1_ref, d2_ref)[h // HEADS_PER_GROUP][:, slot]
            dy = dout * cos + pltpu.roll(dout * sin, HEAD_DIM // 2, 1)
            x = x_ref[:, lanes]
            r = lax.rsqrt(jnp.mean(x * x, axis=-1, keepdims=True) + EPS)
            xh = x * r
            dg = dg + jnp.sum(dy * xh, axis=0, keepdims=True)
            dxh = dy * gain
            dx_ref[:, lanes] = (r * (dxh - xh * jnp.mean(dxh * xh, axis=-1, keepdims=True))).astype(BF16)
        dg_ref[...] += dg

    group = pl.BlockSpec((ts, GROUP_W), lambda i: (i, 0))
    tab = pl.BlockSpec((ts, HEAD_DIM), lambda i: (i, 0))
    gain = pl.BlockSpec((None, 1, HEAD_DIM), lambda i: (which, 0, 0))
    return _pcall(body, name=name,
                  out_shape=(jax.ShapeDtypeStruct((s, A_W), BF16), jax.ShapeDtypeStruct((1, HEAD_DIM), F32)),
                  grid=(s // ts,),
                  in_specs=[group, group, group, pl.BlockSpec((ts, A_W), lambda i: (i, which)), gain, tab, tab],
                  out_specs=(pl.BlockSpec((ts, A_W), lambda i: (i, 0)), pl.BlockSpec((1, HEAD_DIM), lambda i: (0, 0))))(
                      *d_groups, proj, gains, cos2, sin2)


def _assemble(pieces, *, name):
    s = pieces[0].shape[0]
    widths = [p.shape[1] for p in pieces]
    total = sum(widths)
    ts = _rows(s, total // 2)

    def body(*refs):
        o_ref, off = refs[-1], 0
        for x_ref, w in zip(refs[:-1], widths):
            o_ref[:, off:off + w] = x_ref[...].astype(BF16)
            off += w

    return _pcall(body, name=name, out_shape=jax.ShapeDtypeStruct((s, total), BF16), grid=(s // ts,),
                  in_specs=[pl.BlockSpec((ts, w), lambda i: (i, 0)) for w in widths],
                  out_specs=pl.BlockSpec((ts, total), lambda i: (i, 0)))(*pieces)


def _block_rows(blk):
    if isinstance(blk, int):
        return pl.ds(blk * BLOCK, BLOCK)
    return pl.ds(pl.multiple_of(blk * BLOCK, BLOCK), BLOCK)


def _band_window(n, length):
    width = min(2 * BLOCK, length)
    row = lax.broadcasted_iota(jnp.int32, (BLOCK, width), 0)
    col = lax.broadcasted_iota(jnp.int32, (BLOCK, width), 1)
    if width == BLOCK:
        return pl.ds(0, BLOCK), col <= row
    first = n - 1 if isinstance(n, int) else jnp.maximum(n - 1, 0)
    first = max(first, 0) if isinstance(first, int) else first
    dist = row - col + (n - first) * BLOCK
    start = first * BLOCK if isinstance(first, int) else pl.multiple_of(first * BLOCK, BLOCK)
    return pl.ds(start, width), jnp.logical_and(dist >= 0, dist <= BLOCK)


def _dil_fwd(q_arr, k_arr, v_arr, offs, length, dil, *, name):
    nj, nb = dil * HEADS_PER_GROUP, length // BLOCK
    ju, nq = (HEADS_PER_GROUP, 2) if nb > 1 else (2 * HEADS_PER_GROUP, 1)
    qo, ko, vo = (off // ju for off in offs)
    assert all(off % ju == 0 for off in offs) and nb % nq == 0 and nj % ju == 0

    def body(q_ref, k_ref, v_ref, o_ref, l_ref):
        for qq in range(nq):
            qrows = slice(qq * BLOCK, (qq + 1) * BLOCK)
            rows, mask = _band_window(pl.program_id(1) * nq + qq, length)
            for cb in range(ju):
                lanes = slice(cb * HEAD_DIM, (cb + 1) * HEAD_DIM)
                sc = _dot(q_ref[qrows, lanes].astype(BF16), k_ref[rows, lanes].astype(BF16), NT) * ATT_SCALE
                sc = jnp.where(mask, sc, MASKED)
                m = sc.max(axis=-1, keepdims=True)
                p = jnp.exp(sc - m)
                den = jnp.sum(p, axis=-1, keepdims=True)
                acc = _dot(p.astype(BF16), v_ref[rows, lanes].astype(BF16))
                o_ref[qrows, lanes] = acc / den
                l_ref[qrows, lanes] = jnp.broadcast_to(m + jnp.log(den), (BLOCK, HEAD_DIM))

    qspec = pl.BlockSpec((nq * BLOCK, ju * HEAD_DIM), lambda j, n: (n, qo + j))
    kspec = pl.BlockSpec((length, ju * HEAD_DIM), lambda j, n: (0, ko + j))
    vspec = pl.BlockSpec((length, ju * HEAD_DIM), lambda j, n: (0, vo + j))
    ospec = pl.BlockSpec((nq * BLOCK, ju * HEAD_DIM), lambda j, n: (n, j))
    shp = jax.ShapeDtypeStruct((length, nj * HEAD_DIM), F32)
    return _pcall(body, name=name, out_shape=(shp, shp), grid=(nj // ju, nb // nq), in_specs=[qspec, kspec, vspec],
                  out_specs=(ospec, ospec))(q_arr, k_arr, v_arr)


def _dil_bwd(q_arr, k_arr, v_arr, offs, o, lse, do, dlse, length, dil, *, name):
    nj, nb = dil * HEADS_PER_GROUP, length // BLOCK
    ju = 2 * HEADS_PER_GROUP if length <= 4 * BLOCK else 2
    qo, ko, vo = (off // ju for off in offs)
    assert all(off % ju == 0 for off in offs)

    def body(q_ref, k_ref, v_ref, o_ref, l_ref, do_ref, dl_ref, dq_ref, dk_ref, dv_ref):
        dk_ref[...] = jnp.zeros_like(dk_ref)
        dv_ref[...] = jnp.zeros_like(dv_ref)

        def step(n, carry):
            qrows = _block_rows(n)
            rows, mask = _band_window(n, length)
            for cb in range(ju):
                lanes = slice(cb * HEAD_DIM, (cb + 1) * HEAD_DIM)
                q = q_ref[qrows, lanes].astype(BF16)
                dof = do_ref[qrows, lanes]
                dob = dof.astype(BF16)
                lse_c = l_ref[qrows, lanes][:, :1]
                shift = dl_ref[qrows, lanes][:, :1] - jnp.sum(dof * o_ref[qrows, lanes], axis=-1, keepdims=True)
                kk, vv = k_ref[rows, lanes].astype(BF16), v_ref[rows, lanes].astype(BF16)
                sc = _dot(q, kk, NT) * ATT_SCALE
                p = jnp.where(mask, jnp.exp(sc - lse_c), 0.0)
                ds = (p * (_dot(dob, vv, NT) + shift)).astype(BF16)
                dq_ref[qrows, lanes] = _dot(ds, kk) * ATT_SCALE
                dk_ref[rows, lanes] += _dot(ds, q, TN) * ATT_SCALE
                dv_ref[rows, lanes] += _dot(p.astype(BF16), dob, TN)
            return carry

        if nb == 1:
            step(0, 0)
        else:
            lax.fori_loop(0, nb, step, 0)

    def col(off):
        return pl.BlockSpec((length, ju * HEAD_DIM), lambda j: (0, off + j))

    shp = jax.ShapeDtypeStruct((length, nj * HEAD_DIM), F32)
    return _pcall(body, name=name, out_shape=(shp, shp, shp), grid=(nj // ju,),
                  in_specs=[col(qo), col(ko), col(vo), col(0), col(0), col(0), col(0)],
                  out_specs=(col(0), col(0), col(0)))(q_arr, k_arr, v_arr, o, lse, do, dlse)


def _combine_weights(l_refs):
    ls = [r[...] for r in l_refs]
    m = jnp.maximum(jnp.maximum(ls[0], ls[1]), ls[2])
    es = [jnp.exp(l - m) for l in ls]
    den = es[0] + es[1] + es[2]
    return [e / den for e in es]


def _combine_fwd(os_, lses, *, name):
    s = os_[0].shape[0]
    ts = _rows(s, GROUP_W)

    def body(o0, o1, o2, l0, l1, l2, out_ref):
        w = _combine_weights((l0, l1, l2))
        out_ref[...] = (w[0] * o0[...] + w[1] * o1[...] + w[2] * o2[...]).astype(BF16)

    row = pl.BlockSpec((ts, GROUP_W), lambda i: (i, 0))
    return _pcall(body, name=name, out_shape=jax.ShapeDtypeStruct((s, GROUP_W), BF16), grid=(s // ts,),
                  in_specs=[row] * 6, out_specs=row)(*os_, *lses)


def _combine_bwd(do_a, os_, lses, *, name):
    s = do_a.shape[0]
    ts = _rows(s, GROUP_W)

    def body(d_ref, o0, o1, o2, l0, l1, l2, do0, do1, do2, dl0, dl1, dl2):
        w = _combine_weights((l0, l1, l2))
        d = d_ref[...]
        og = [o0[...], o1[...], o2[...]]
        oa = w[0] * og[0] + w[1] * og[1] + w[2] * og[2]
        ta = jnp.sum(d * oa, axis=-1, keepdims=True)
        for g, (do_ref, dl_ref) in enumerate(((do0, dl0), (do1, dl1), (do2, dl2))):
            do_ref[...] = w[g] * d
            dl_ref[...] = w[g] * (jnp.sum(d * og[g], axis=-1, keepdims=True) - ta)

    head = pl.BlockSpec((ts, HEAD_DIM), lambda i, h: (i, h))
    shp = jax.ShapeDtypeStruct((s, GROUP_W), F32)
    return _pcall(body, name=name, out_shape=(shp,) * 6, grid=(s // ts, HEADS_PER_GROUP),
                  in_specs=[head] * 7, out_specs=(head,) * 6)(do_a, *os_, *lses)


def _dot_exact(x, ones_mask):
    hi = x.astype(BF16)
    r1 = x - hi.astype(F32)
    mid = r1.astype(BF16)
    lo = (r1 - mid.astype(F32)).astype(BF16)
    return _dot(hi, ones_mask) + _dot(mid, ones_mask) + _dot(lo, ones_mask)


SB_QROWS = 2 * BLOCK
SB_UNROLL = 4
SB_HEADS_PER_STEP = 2
SB_LANES = [slice(hh * HEAD_DIM, (hh + 1) * HEAD_DIM) for hh in range(SB_HEADS_PER_STEP)]


def _sb_mask(j, i):
    row = lax.broadcasted_iota(jnp.int32, (SB_QROWS, BLOCK), 0)
    col = lax.broadcasted_iota(jnp.int32, (SB_QROWS, BLOCK), 1)
    return col + (j * BLOCK - i * SB_QROWS) < row


def _sb_steps(i):
    return ((i + 1) * (SB_QROWS // BLOCK) + SB_UNROLL - 1) // SB_UNROLL


def _sb_scores(q, kk, j, i, masked):
    mask = _sb_mask(j, i) if masked else None
    z = _dot(q, kk, NT) * ATT_SCALE
    sp = jnp.log(1.0 + jnp.exp(-jnp.abs(z)))
    log_beta = jnp.minimum(z, 0.0) - sp
    log_1mb = jnp.minimum(-z, 0.0) - sp
    if masked:
        log_1mb = jnp.where(mask, log_1mb, 0.0)
    return z, log_beta, log_1mb, mask


def _sb_weights(log_beta, log_1mb, mask, run, upper):
    a = jnp.exp(log_beta + (run + _dot_exact(log_1mb, upper)))
    return a if mask is None else jnp.where(mask, a, 0.0)


def _sb_peeled(nsteps, make_step, init, masked_first):
    if masked_first:
        return lax.fori_loop(1, nsteps, make_step(False), make_step(True)(0, init))
    return make_step(True)(nsteps - 1, lax.fori_loop(0, nsteps - 1, make_step(False), init))


def _tri(strict_lower):
    row = lax.broadcasted_iota(jnp.int32, (BLOCK, BLOCK), 0)
    col = lax.broadcasted_iota(jnp.int32, (BLOCK, BLOCK), 1)
    return ((row > col) if strict_lower else (row < col)).astype(BF16)


def _sb_fwd(proj, *, name):
    s = proj.shape[0]
    assert s % (BLOCK * SB_UNROLL) == 0 and s % SB_QROWS == 0

    def body(q_ref, k_ref, v_ref, o_ref):
        i = pl.program_id(1)
        qs = [q_ref[:, lanes].astype(BF16) for lanes in SB_LANES]
        upper = _tri(True)
        nsteps = _sb_steps(i)

        def make_step(masked):
            def step(t, carry):
                carry = list(carry)
                for b in reversed(range(SB_UNROLL)):
                    j = (nsteps - 1 - t) * SB_UNROLL + b
                    rows = _block_rows(j)
                    for hh, lanes in enumerate(SB_LANES):
                        acc, run = carry[hh]
                        _, log_beta, log_1mb, mask = _sb_scores(qs[hh], k_ref[rows, lanes].astype(BF16), j, i, masked)
                        a = _sb_weights(log_beta, log_1mb, mask, run, upper)
                        carry[hh] = (acc + _dot(a.astype(BF16), v_ref[rows, lanes].astype(BF16)),
                                     run + jnp.sum(log_1mb, axis=-1, keepdims=True))
                return tuple(carry)
            return step

        zero = (jnp.zeros((SB_QROWS, HEAD_DIM), F32), jnp.zeros((SB_QROWS, 1), F32))
        for lanes, (acc, _) in zip(SB_LANES, _sb_peeled(nsteps, make_step, (zero,) * SB_HEADS_PER_STEP, True)):
            o_ref[:, lanes] = acc.astype(BF16)

    width = SB_HEADS_PER_STEP * HEAD_DIM
    qb, kb, vb = (off // width for off in (OFF_QB, OFF_KB, OFF_VB))
    return _pcall(body, name=name, out_shape=jax.ShapeDtypeStruct((s, B_W), BF16),
                  grid=(SB_HEADS // SB_HEADS_PER_STEP, s // SB_QROWS),
                  in_specs=[pl.BlockSpec((SB_QROWS, width), lambda h, i: (i, qb + h)),
                            pl.BlockSpec((s, width), lambda h, i: (0, kb + h)),
                            pl.BlockSpec((s, width), lambda h, i: (0, vb + h))],
                  out_specs=pl.BlockSpec((SB_QROWS, width), lambda h, i: (i, h)))(proj, proj, proj)


def _sb_bwd(proj, do_b, *, name):
    s = proj.shape[0]
    assert s % (BLOCK * SB_UNROLL) == 0 and s % SB_QROWS == 0
    nkb = s // BLOCK

    def body(q_ref, k_ref, v_ref, do_ref, dq_ref, dk_ref, dv_ref, z_s, a_s):
        i = pl.program_id(1)

        @pl.when(i == 0)
        def _():
            dk_ref[...] = jnp.zeros_like(dk_ref)
            dv_ref[...] = jnp.zeros_like(dv_ref)

        qs = [q_ref[:, lanes].astype(BF16) for lanes in SB_LANES]
        dobs = [do_ref[:, lanes].astype(BF16) for lanes in SB_LANES]
        upper, lower = _tri(True), _tri(False)
        nsteps = _sb_steps(i)

        def make_recompute(masked):
            def recompute(t, runs):
                runs = list(runs)
                for b in reversed(range(SB_UNROLL)):
                    j = (nsteps - 1 - t) * SB_UNROLL + b
                    rows = _block_rows(j)
                    for hh, lanes in enumerate(SB_LANES):
                        z, log_beta, log_1mb, mask = _sb_scores(qs[hh], k_ref[rows, lanes].astype(BF16), j, i, masked)
                        z_s[hh, j] = z
                        a_s[hh, j] = _sb_weights(log_beta, log_1mb, mask, runs[hh], upper)
                        runs[hh] = runs[hh] + jnp.sum(log_1mb, axis=-1, keepdims=True)
                return tuple(runs)
            return recompute

        _sb_peeled(nsteps, make_recompute, (jnp.zeros((SB_QROWS, 1), F32),) * SB_HEADS_PER_STEP, True)

        def make_grads(masked):
            def grads(t, carry):
                carry = list(carry)
                for b in range(SB_UNROLL):
                    j = t * SB_UNROLL + b
                    rows = _block_rows(j)
                    for hh, lanes in enumerate(SB_LANES):
                        dq, run = carry[hh]
                        kk, vv = k_ref[rows, lanes].astype(BF16), v_ref[rows, lanes].astype(BF16)
                        z, a = z_s[hh, j], a_s[hh, j]
                        de = _dot(dobs[hh], vv, NT) * a
                        beta = jax.nn.sigmoid(z)
                        one_minus_beta = 1.0 - beta
                        if masked:
                            beta = jnp.where(_sb_mask(j, i), beta, 0.0)
                        dz = (de * one_minus_beta - beta * (run + _dot_exact(de, lower))).astype(BF16)
                        dk_ref[rows, lanes] += _dot(dz, qs[hh], TN) * ATT_SCALE
                        dv_ref[rows, lanes] += _dot(a.astype(BF16), dobs[hh], TN)
                        carry[hh] = (dq + _dot(dz, kk), run + jnp.sum(de, axis=-1, keepdims=True))
                return tuple(carry)
            return grads

        zero = (jnp.zeros((SB_QROWS, HEAD_DIM), F32), jnp.zeros((SB_QROWS, 1), F32))
        for lanes, (dq, _) in zip(SB_LANES, _sb_peeled(nsteps, make_grads, (zero,) * SB_HEADS_PER_STEP, False)):
            dq_ref[:, lanes] = dq * ATT_SCALE

    width = SB_HEADS_PER_STEP * HEAD_DIM
    qb, kb, vb = (off // width for off in (OFF_QB, OFF_KB, OFF_VB))
    blk = pl.BlockSpec((SB_QROWS, width), lambda h, i: (i, h))
    full = pl.BlockSpec((s, width), lambda h, i: (0, h))
    shp = jax.ShapeDtypeStruct((s, B_W), F32)
    saved = pltpu.VMEM((SB_HEADS_PER_STEP, nkb, SB_QROWS, BLOCK), F32)
    return _pcall(body, name=name, out_shape=(shp, shp, shp), grid=(SB_HEADS // SB_HEADS_PER_STEP, s // SB_QROWS),
                  in_specs=[pl.BlockSpec((SB_QROWS, width), lambda h, i: (i, qb + h)),
                            pl.BlockSpec((s, width), lambda h, i: (0, kb + h)),
                            pl.BlockSpec((s, width), lambda h, i: (0, vb + h)), blk],
                  out_specs=(blk, full, full), scratch=[saved, saved])(proj, proj, proj, do_b)


def _coords():
    return lax.axis_index("x"), lax.axis_index("y"), lax.axis_index("c")


def _flip(v, bit):
    return 1 - v if bit else v


def _shard_of(ref, axis, idx, size):
    if axis == 0:
        sl = pl.ds(pl.multiple_of(idx * size, 16), size)
        return ref.at[sl, :] if len(ref.shape) == 2 else ref.at[:, sl, :]
    sl = pl.ds(pl.multiple_of(idx * size, 128), size)
    return ref.at[:, sl] if len(ref.shape) == 2 else ref.at[:, :, sl]


def _small_allgather(v, *, name, silu=False):
    n = v.shape[1]

    def body(v_ref, out_ref, send_sems, recv_sems):
        x, y, c = _coords()
        me = 4 * x + 2 * y + c
        val = v_ref[...]
        out_ref[me] = val * jax.nn.sigmoid(val) if silu else val
        copies = []
        for k in range(1, N_DEV):
            peer = (_flip(x, k & 4), _flip(y, k & 2), _flip(c, k & 1))
            copies.append(pltpu.make_async_remote_copy(
                src_ref=out_ref.at[me], dst_ref=out_ref.at[me], send_sem=send_sems.at[k - 1],
                recv_sem=recv_sems.at[k - 1], device_id=peer, device_id_type=MESH))
        for cp in copies:
            cp.start()
        for cp in copies:
            cp.wait_recv()
        for cp in copies:
            cp.wait_send()

    return _pcall(body, name=name, out_shape=jax.ShapeDtypeStruct((N_DEV, 1, n), F32),
                  in_specs=[pl.BlockSpec(memory_space=pltpu.VMEM)], out_specs=pl.BlockSpec(memory_space=pltpu.VMEM),
                  scratch=[pltpu.SemaphoreType.DMA((N_DEV - 1,)), pltpu.SemaphoreType.DMA((N_DEV - 1,))])(v)


def _cast_place(w, layer, axis, me, *, name):
    _, r, c = w.shape
    tr = _rows(r, c)
    nrt = r // tr

    def body(me_ref, w_ref, o_ref):
        o_ref[...] = w_ref[...].astype(BF16)

    wspec = pl.BlockSpec((None, tr, c), lambda i, me_ref: (layer, i, 0))
    if axis == 0:
        ospec = pl.BlockSpec((tr, c), lambda i, me_ref: (me_ref[0] * nrt + i, 0))
        shape = (r * N_DEV, c)
    else:
        ospec = pl.BlockSpec((tr, c), lambda i, me_ref: (i, me_ref[0]))
        shape = (r, c * N_DEV)
    return _pcall(body, name=name, out_shape=jax.ShapeDtypeStruct(shape, BF16), grid=(nrt,), in_specs=[wspec],
                  out_specs=ospec, prefetch=1)(me, w)


def _pair_sum(grad, sib, core, axis, *, name):
    _, r, c = sib.shape
    tr = _rows(r, c // 2)
    nrt = r // tr

    def body(core_ref, g_ref, s_ref, o_ref):
        o_ref[...] = (g_ref[...].astype(F32) + s_ref[...].astype(F32)).astype(BF16)

    if axis == 0:
        gspec = pl.BlockSpec((tr, c), lambda q, i, core_ref: ((2 * q + core_ref[0]) * nrt + i, 0))
    else:
        gspec = pl.BlockSpec((tr, c), lambda q, i, core_ref: (i, 2 * q + core_ref[0]))
    sspec = pl.BlockSpec((None, tr, c), lambda q, i, core_ref: (q, i, 0))
    return _pcall(body, name=name, out_shape=jax.ShapeDtypeStruct(sib.shape, BF16), grid=(N_CHIPS, nrt),
                  in_specs=[gspec, sspec], out_specs=sspec, prefetch=1)(core, grad, sib)


ANY_SPEC = pl.BlockSpec(memory_space=pl.ANY)
SEM_SPEC = pl.BlockSpec(memory_space=pltpu.SEMAPHORE)
SPLIT_PARAMS = dict(has_side_effects=pltpu.SideEffectType.DATAFLOW_SIDE_EFFECTING)


def _split_start(copies_fn, buffers, sem_shape, after, *, name):
    n = len(buffers)
    rows, cols = sem_shape
    ns = rows * cols
    extra = ([] if after is None else [after]) + _take_token()

    def body(*refs):
        sems = refs[n + len(extra):n + len(extra) + 2 * ns]
        for cp in copies_fn(refs[:n], _sem_rows(sems[:ns], cols), _sem_rows(sems[ns:], cols)):
            cp.start()
        refs[-1][...] = jnp.zeros_like(refs[-1])

    sem = pltpu.SemaphoreType.DMA(())
    outs = pl.pallas_call(
        body, name=name,
        out_shape=((sem,) * (2 * ns) + tuple(jax.ShapeDtypeStruct(b.shape, b.dtype) for b in buffers) + (TOKEN,)),
        in_specs=(ANY_SPEC,) * (n + len(extra)),
        out_specs=(SEM_SPEC,) * (2 * ns) + (ANY_SPEC,) * n + (pl.BlockSpec(memory_space=pltpu.VMEM),),
        input_output_aliases={i: 2 * ns + i for i in range(n)},
        compiler_params=pltpu.CompilerParams(**SPLIT_PARAMS))(*buffers, *extra)
    _ORDER["token"] = outs[-1]
    return list(outs[:ns]), list(outs[ns:2 * ns]), list(outs[2 * ns:2 * ns + n]), outs[-1]


def _split_wait(copies_fn, send_sems, recv_sems, buffers, after, sem_rows, *, name):
    n, ns = len(buffers), len(send_sems)
    cols = ns // sem_rows
    extra = ([] if after is None else [after]) + _take_token()

    def body(*refs):
        sems = refs[n:n + 2 * ns]
        copies = copies_fn(refs[:n], _sem_rows(sems[:ns], cols), _sem_rows(sems[ns:], cols))
        for cp in copies:
            cp.wait_send()
        for cp in copies:
            cp.wait_recv()
        refs[-1][...] = jnp.zeros_like(refs[-1])

    outs = pl.pallas_call(
        body, name=name, out_shape=tuple(jax.ShapeDtypeStruct(b.shape, b.dtype) for b in buffers) + (TOKEN,),
        in_specs=(ANY_SPEC,) * n + (SEM_SPEC,) * (2 * ns) + (ANY_SPEC,) * len(extra),
        out_specs=(ANY_SPEC,) * n + (pl.BlockSpec(memory_space=pltpu.VMEM),),
        input_output_aliases={i: i for i in range(n)},
        compiler_params=pltpu.CompilerParams(**SPLIT_PARAMS))(*buffers, *send_sems, *recv_sems, *extra)
    _ORDER["token"] = outs[-1]
    return list(outs[:n])


def _sem_rows(sems, cols):
    return [sems[i:i + cols] for i in range(0, len(sems), cols)]


def _empty_hbm(shape, dtype):
    return pltpu.with_memory_space_constraint(lax.empty(shape, dtype), pltpu.HBM)


class _SplitGather:
    def __init__(self, fulls, axes, tag):
        self.axes, self.tag, self.nt = list(axes), tag, len(fulls)
        self.sizes = [f.shape[ax] // N_DEV for f, ax in zip(fulls, axes)]
        self.fulls = list(fulls)

    def _slot(self, ref, t, dev):
        return _shard_of(ref, self.axes[t], 4 * dev[0] + 2 * dev[1] + dev[2], self.sizes[t])

    def _first_copies(self, refs, send_sems, recv_sems):
        x, y, c = _coords()
        peers = [(x, y, 1 - c), (1 - x, y, c), (x, 1 - y, c), (1 - x, 1 - y, c)]
        return [pltpu.make_async_remote_copy(
            src_ref=self._slot(refs[t], t, (x, y, c)), dst_ref=self._slot(refs[t], t, (x, y, c)),
            send_sem=send_sems[t][k], recv_sem=recv_sems[t][k], device_id=peer, device_id_type=MESH)
            for t in range(self.nt) for k, peer in enumerate(peers)]

    def _forward_copies(self, refs, send_sems, recv_sems):
        x, y, c = _coords()
        chips = [(1 - x, y), (x, 1 - y), (1 - x, 1 - y)]
        return [pltpu.make_async_remote_copy(
            src_ref=self._slot(refs[t], t, (*chip, c)), dst_ref=self._slot(refs[t], t, (*chip, c)),
            send_sem=send_sems[t][j], recv_sem=recv_sems[t][j], device_id=(x, y, 1 - c), device_id_type=MESH)
            for t in range(self.nt) for j, chip in enumerate(chips)]

    def first(self, after):
        self.s1, self.r1, self.fulls, token = _split_start(
            self._first_copies, self.fulls, (self.nt, 4), after, name=f"comm_gather1_start_{self.tag}")
        return token

    def forward(self, after):
        bufs = _split_wait(self._first_copies, self.s1, self.r1, self.fulls, after, self.nt,
                           name=f"comm_gather1_wait_{self.tag}")
        self.s2, self.r2, self.fulls, token = _split_start(
            self._forward_copies, bufs, (self.nt, 3), after, name=f"comm_gather2_start_{self.tag}")
        return token

    def finish(self, after):
        return _split_wait(self._forward_copies, self.s2, self.r2, self.fulls, after, self.nt,
                           name=f"comm_gather2_wait_{self.tag}")


class _SplitPairExchange:
    def __init__(self, grads, axes, tag):
        self.nt, self.tag, self.axes = len(grads), tag, list(axes)
        self.grads = list(grads)
        self.sizes = [g.shape[ax] // N_DEV for g, ax in zip(grads, axes)]

    def _copies(self, refs, send_sems, recv_sems):
        nt = self.nt
        x, y, c = _coords()
        return [pltpu.make_async_remote_copy(
            src_ref=_shard_of(refs[t], self.axes[t], 2 * q + 1 - c, self.sizes[t]), dst_ref=refs[nt + t].at[q],
            send_sem=send_sems[t][q], recv_sem=recv_sems[t][q], device_id=(x, y, 1 - c), device_id_type=MESH)
            for t in range(nt) for q in range(N_CHIPS)]

    def start(self):
        landing = []
        for g, ax in zip(self.grads, self.axes):
            dims = list(g.shape)
            dims[ax] //= N_DEV
            landing.append(_empty_hbm((N_CHIPS, *dims), g.dtype))
        self.s, self.r, self.bufs, token = _split_start(
            self._copies, self.grads + landing, (self.nt, N_CHIPS), None,
            name=f"comm_rs_pair_start_{self.tag}")
        return token

    def finish(self, after):
        bufs = _split_wait(self._copies, self.s, self.r, self.bufs, after, self.nt,
                           name=f"comm_rs_pair_wait_{self.tag}")
        return bufs[:self.nt], bufs[self.nt:]


class _ReducePipeline:
    def __init__(self, core):
        self.core, self.items, self.done, self.now = core, [], [], 0

    def add(self, keys, grads, layer):
        axes = [SHARD_AXIS[k] for k in keys]
        pair = _SplitPairExchange([grads[k] for k in keys], axes, f"{keys[0]}{layer}")
        pair.start()
        self.items.append(dict(keys=keys, layer=layer, axes=axes, pair=pair, state="pair", since=self.now))

    def tick(self, after, flush=False):
        self.now += 1
        for it in self.items:
            if it["state"] == "pair" and it["since"] < self.now:
                grads, sib = it["pair"].finish(after)
                sums = [_pair_sum(g, s_, self.core, ax, name="pair_sum_" + k)
                        for k, g, s_, ax in zip(it["keys"], grads, sib, it["axes"])]
                it["chip"] = _SplitChipExchange(sums, f"{it['keys'][0]}{it['layer']}")
                it["chip"].start()
                it.update(state="chip", since=self.now)
            elif it["state"] == "chip" and (flush or self.now - it["since"] >= 2):
                sums, remote = it["chip"].finish(after)
                self.done.append((it["keys"], it["layer"], sums, remote))
                it["state"] = "done"

    def take_done(self):
        out, self.done = self.done, []
        return out


class _SplitChipExchange:
    def __init__(self, sums, tag):
        self.nt, self.tag = len(sums), tag
        self.sums = list(sums)

    def _copies(self, refs, send_sems, recv_sems):
        nt = self.nt
        x, y, c = _coords()
        copies = []
        for t in range(nt):
            for k in range(1, N_CHIPS):
                px, py = _flip(x, k & 2), _flip(y, k & 1)
                copies.append(pltpu.make_async_remote_copy(
                    src_ref=refs[t].at[2 * px + py], dst_ref=refs[nt + t].at[k - 1], send_sem=send_sems[t][k - 1],
                    recv_sem=recv_sems[t][k - 1], device_id=(px, py, c), device_id_type=MESH))
        return copies

    def start(self):
        landing = [_empty_hbm((N_CHIPS - 1,) + s.shape[1:], s.dtype) for s in self.sums]
        self.s, self.r, self.bufs, token = _split_start(
            self._copies, self.sums + landing, (self.nt, N_CHIPS - 1), None,
            name=f"comm_rs_chip_start_{self.tag}")
        return token

    def finish(self, after):
        bufs = _split_wait(self._copies, self.s, self.r, self.bufs, after, self.nt,
                           name=f"comm_rs_chip_wait_{self.tag}")
        return bufs[:self.nt], bufs[self.nt:]


def _adam_math(g, w, m, v):
    m2 = ADAM_B1 * m + (1.0 - ADAM_B1) * g
    v2 = ADAM_B2 * v + (1.0 - ADAM_B2) * (g * g)
    m_hat = m2 / (1.0 - ADAM_B1 ** ADAM_STEP)
    v_hat = v2 / (1.0 - ADAM_B2 ** ADAM_STEP)
    delta = -ADAM_LR * (m_hat / (jnp.sqrt(v_hat) + ADAM_EPS) + ADAM_WD * w)
    return delta, m2, v2


def _adamw_sharded(chip_sums, remote, chip, w, m, v, layer, prev, *, name):
    nl, r, c = w.shape
    tr = _rows(r, c)

    def body(*refs):
        p_ref, r0_ref, r1_ref, r2_ref, w_ref, m_ref, v_ref = refs[1:8]
        g_out, d_out, m_out, v_out = refs[-4:]
        g = ((p_ref[...].astype(F32) + r0_ref[...].astype(F32)) + r1_ref[...].astype(F32)) + r2_ref[...].astype(F32)
        g_out[...] = g
        d_out[...], m_out[...], v_out[...] = _adam_math(g, w_ref[...], m_ref[...], v_ref[...])

    pspec = pl.BlockSpec((None, tr, c), lambda i, chip_ref: (chip_ref[0], i, 0))

    def rspec(k):
        return pl.BlockSpec((None, tr, c), lambda i, chip_ref: (k, i, 0))

    wspec = pl.BlockSpec((None, tr, c), lambda i, chip_ref: (layer, i, 0))
    in_specs = [pspec, rspec(0), rspec(1), rspec(2), wspec, wspec, wspec]
    args = [chip, chip_sums, remote, remote, remote, w, m, v]
    aliases = {}
    if prev is not None:
        in_specs += [pl.BlockSpec(memory_space=pl.ANY)] * 4
        aliases = {len(args) + i: i for i in range(4)}
        args += list(prev)
    shp = jax.ShapeDtypeStruct(w.shape, F32)
    return _pcall(body, name=name, out_shape=(shp,) * 4, grid=(r // tr,), in_specs=in_specs, out_specs=(wspec,) * 4,
                  aliases=aliases, prefetch=1)(*args)


def _adamw_local(g, w, m, v, *, name):
    nl, r, c = w.shape
    tr = _rows(r, c)

    def body(g_ref, w_ref, m_ref, v_ref, d_out, m_out, v_out):
        d_out[...], m_out[...], v_out[...] = _adam_math(g_ref[...], w_ref[...], m_ref[...], v_ref[...])

    spec = pl.BlockSpec((None, tr, c), lambda l, i: (l, i, 0))
    shp = jax.ShapeDtypeStruct(w.shape, F32)
    return _pcall(body, name=name, out_shape=(shp,) * 3, grid=(nl, r // tr), in_specs=[spec] * 4,
                  out_specs=(spec,) * 3)(g, w, m, v)


def _adamw_replicated(parts, w, m, v, *, name):
    n = w.shape[1]

    def body(p_ref, w_ref, m_ref, v_ref, g_out, d_out, m_out, v_out):
        g = p_ref[0]
        for k in range(1, N_DEV):
            g = g + p_ref[k]
        g_out[...] = g
        d_out[...], m_out[...], v_out[...] = _adam_math(g, w_ref[...], m_ref[...], v_ref[...])

    vm = pl.BlockSpec(memory_space=pltpu.VMEM)
    shp = jax.ShapeDtypeStruct((1, n), F32)
    return _pcall(body, name=name, out_shape=(shp,) * 4, in_specs=[vm] * 4, out_specs=(vm,) * 4)(parts, w, m, v)


def _group_views(qk, proj, g, dil, seq):
    if dil == 1:
        return (qk, qk, proj), (0, A_HEADS, 2 * A_HEADS)
    length = seq // dil
    lo = g * GROUP_W
    q = qk[:, lo:lo + GROUP_W].reshape(length, dil * GROUP_W)
    k = qk[:, A_W + lo:A_W + lo + GROUP_W].reshape(length, dil * GROUP_W)
    v = proj[:, OFF_VA + lo:OFF_VA + lo + GROUP_W].astype(BF16).reshape(length, dil * GROUP_W)
    return (q, k, v), (0, 0, 0)


def _mod_rows(mod, d):
    return [mod[:, i * d:(i + 1) * d] for i in range(6)]


MIXER_W = ("w_in", "w_branch_a", "w_branch_b", "w_out")
FFN_W = ("w_gate_up", "w_down")
SHARD_AXIS = {"w_in": 1, "w_branch_a": 1, "w_branch_b": 1, "w_out": 0, "w_gate_up": 1, "w_down": 0}


def _norm_args(mod, gain, which, d):
    rows = _mod_rows(mod, d)
    return gain, rows[3 * which + 1], rows[3 * which]


def _mixer_fwd_a(h, u, gains, w_in, cos2, sin2):
    seq = h.shape[0]
    proj = _mm(u, w_in, name="mm_in")
    qk = _qkrope_fwd(proj, gains, cos2, sin2, name="qkrope_fwd")
    os_, lses, views = [], [], []
    for g, dil in enumerate(DILATIONS):
        arrs, offs = _group_views(qk, proj, g, dil, seq)
        o, lse = _dil_fwd(*arrs, offs, seq // dil, dil, name=f"dil_fwd_{dil}")
        views.append((arrs, offs, o, lse))
        os_.append(o.reshape(seq, GROUP_W))
        lses.append(lse.reshape(seq, GROUP_W))
    o_a = _combine_fwd(os_, lses, name="combine_fwd")
    o_b = _sb_fwd(proj, name="sb_fwd")
    return dict(h_in=h, u=u, proj=proj, views=views, os=os_, lses=lses, o_a=o_a, o_b=o_b)


def _mixer_fwd_b(sv, mod, g2, wts):
    d = sv["h_in"].shape[1]
    merged, y_a, y_b = _mm_merge(sv["o_a"], sv["o_b"], wts["w_branch_a"], wts["w_branch_b"], sv["proj"],
                                 name="mm_branch")
    h_mid, t, u2 = _mm_resid_norm(merged, wts["w_out"], sv["h_in"], _mod_rows(mod, d)[2], _norm_args(mod, g2, 1, d),
                                  name="mm_out")
    sv.update(y_a=y_a, y_b=y_b, merged=merged, t=t, h_mid=h_mid, u2=u2)
    return h_mid


def _ffn_fwd_a(sv, w_gate_up):
    a, g, u = _mm_swiglu(sv["u2"], w_gate_up, name="mm_gate_up")
    sv.update(g=g, up=u, a=a)
    return a


def _ffn_fwd_b(sv, mod, w_down, next_norm):
    d = sv["h_mid"].shape[1]
    h_out, sv["f"], u_next = _mm_resid_norm(sv["a"], w_down, sv["h_mid"], _mod_rows(mod, d)[5], next_norm,
                                            name="mm_down")
    return h_out, u_next


def _wgrad(act, dout, key):
    return _mm(act, dout, ta=True, out_dtype=BF16, name="mm_wgrad_" + key)


def _ffn_bwd(dh, df, dgate2, sv, mod, g2, wts, hook):
    d = dh.shape[1]
    sc2, ga1 = _mod_rows(mod, d)[4], _mod_rows(mod, d)[2]
    dg, dup = _mm_down_t_swiglu(df, wts["w_down"], sv["g"], sv["up"], name="mm_down_t")
    grads = {"w_down": _wgrad(sv["a"], df, "w_down")}
    hook(dup)
    du2 = _mm_cat_k(dg, dup, wts["w_gate_up"], name="mm_gate_up_t")
    grads["w_gate_up"] = _mm_cat_n(sv["u2"], dg, dup, name="mm_wgrad_w_gate_up")
    dh_mid, dsh2, dsc2, dg2, dt, dgate1 = _rmsmod_bwd(du2, sv["h_mid"], g2, sc2, dh, sv["t"], ga1, name="rmsmod_bwd")
    return dh_mid, [dsh2, dsc2, dgate2], dg2, grads, dt, dgate1


def _mixer_bwd(dh_mid, dt, dgate1, sv, mod, g1, gains, wts, cos2, sin2, hook, below):
    seq, d = dh_mid.shape
    sc1 = _mod_rows(mod, d)[1]
    dmerged = _mm(dt, wts["w_out"], tb=True, name="mm_out_t")
    grads = {"w_out": _wgrad(sv["merged"], dt, "w_out")}
    dy_a, dy_b, dga, dgb = _merge_bwd(dmerged, sv["proj"], sv["y_a"], sv["y_b"], name="merge_bwd")
    do_a = _mm(dy_a, wts["w_branch_a"], tb=True, name="mm_branch_t")
    do_b = _mm(dy_b, wts["w_branch_b"], tb=True, name="mm_branch_t")
    grads["w_branch_a"] = _wgrad(sv["o_a"], dy_a, "w_branch_a")
    grads["w_branch_b"] = _wgrad(sv["o_b"], dy_b, "w_branch_b")
    dqb, dkb, dvb = _sb_bwd(sv["proj"], do_b, name="sb_bwd")
    hook(dqb, grads)
    comb = _combine_bwd(do_a, sv["os"], sv["lses"], name="combine_bwd")
    grads = {}
    dos, dls = comb[:3], comb[3:]
    dqs, dks, dvs = [], [], []
    for g, dil in enumerate(DILATIONS):
        length = seq // dil
        arrs, offs, o_view, lse_view = sv["views"][g]
        view = (length, dil * GROUP_W)
        dq, dk, dv = _dil_bwd(*arrs, offs, o_view, lse_view, dos[g].reshape(view), dls[g].reshape(view), length, dil,
                              name=f"dil_bwd_{dil}")
        dqs.append(dq.reshape(seq, GROUP_W))
        dks.append(dk.reshape(seq, GROUP_W))
        dvs.append(dv.reshape(seq, GROUP_W))
    dq_pre, dqn = _qkrope_bwd(dqs, sv["proj"], gains, 0, cos2, sin2, name="qkrope_bwd")
    dk_pre, dkn = _qkrope_bwd(dks, sv["proj"], gains, 1, cos2, sin2, name="qkrope_bwd")
    dgains = jnp.stack([dqn, dkn])
    dproj = _assemble([dq_pre, dk_pre] + dvs + [dqb, dkb, dvb, dga, dgb], name="assemble_dproj")
    du = _mm(dproj, wts["w_in"], tb=True, name="mm_in_t")
    grads["w_in"] = _wgrad(sv["u"], dproj, "w_in")
    dh_in, dsh1, dsc1, dg1, df, dgate2 = _rmsmod_bwd(du, sv["h_in"], g1, sc1, dh_mid, *(below or (None, None)),
                                                     name="rmsmod_bwd")
    return dh_in, [dsh1, dsc1, dgate1], dg1, dgains, grads, df, dgate2


def kernel(x, c, w_ada, b_ada, norm1_g, norm2_g, w_in, qn_g, kn_g, w_branch_a, w_branch_b, w_out, w_gate_up, w_down, loss_target, m_w_ada, m_b_ada, m_norm1_g, m_norm2_g, m_w_in, m_qn_g, m_kn_g, m_w_branch_a, m_w_branch_b, m_w_out, m_w_gate_up, m_w_down, v_w_ada, v_b_ada, v_norm1_g, v_norm2_g, v_w_in, v_qn_g, v_kn_g, v_w_branch_a, v_w_branch_b, v_w_out, v_w_gate_up, v_w_down):
    _ORDER["token"] = None
    seq, d = x.shape[1], x.shape[2]
    depth = w_in.shape[0]
    weights = dict(w_in=w_in, w_branch_a=w_branch_a, w_branch_b=w_branch_b, w_out=w_out, w_gate_up=w_gate_up,
                   w_down=w_down)
    moments_m = dict(w_in=m_w_in, w_branch_a=m_w_branch_a, w_branch_b=m_w_branch_b, w_out=m_w_out,
                     w_gate_up=m_w_gate_up, w_down=m_w_down)
    moments_v = dict(w_in=v_w_in, w_branch_a=v_w_branch_a, w_branch_b=v_w_branch_b, w_out=v_w_out,
                     w_gate_up=v_w_gate_up, w_down=v_w_down)
    xi, yi, ci = _coords()
    me = 4 * xi + 2 * yi + ci
    core = jnp.reshape(ci, (1,)).astype(jnp.int32)
    chip = jnp.reshape(2 * xi + yi, (1,)).astype(jnp.int32)

    ada_w = w_ada.shape[2]
    c_act = _small_allgather(c, name="comm_gather_c", silu=True).reshape(N_DEV, d)
    c_pad = jnp.concatenate([c_act, jnp.zeros_like(c_act)], axis=0).astype(BF16)
    bias = lax.dynamic_slice(b_ada, (0, me * ada_w), (depth, ada_w))
    mod_part = jnp.stack([_mm(c_pad, w_ada[l], name="mm_ada")[:N_DEV] for l in range(depth)]) + bias[:, None, :]
    mod_all = _small_allgather(mod_part.reshape(1, depth * N_DEV * ada_w), name="comm_gather_mod")
    mod_all = mod_all.reshape(N_DEV, depth, N_DEV, ada_w)
    mod_mine = lax.dynamic_index_in_dim(mod_all, me, axis=2, keepdims=False)
    mods = jnp.transpose(mod_mine, (1, 0, 2)).reshape(depth, 1, 6 * d)

    cos2, sin2 = _rope_tables(seq)
    gains = [jnp.stack([qn_g[l], kn_g[l]])[:, None, :] for l in range(depth)]
    g1s = [norm1_g[l][None] for l in range(depth)]
    g2s = [norm2_g[l][None] for l in range(depth)]

    me_arr = jnp.reshape(me, (1,)).astype(jnp.int32)

    def placed(keys, l):
        return [_cast_place(weights[k], l, SHARD_AXIS[k], me_arr, name="cast_place_" + k) for k in keys]

    def gather_of(keys, l, tag):
        return _SplitGather(placed(keys, l), [SHARD_AXIS[k] for k in keys], f"{tag}{l}")

    groups = [("w_in", 0, MIXER_W[:1]), ("rest", 0, MIXER_W[1:]), ("ffn", 0, FFN_W)]
    for l in range(1, depth):
        groups += [("mixer", l, MIXER_W), ("ffn", l, FFN_W)]
    gathers, token = {}, mods
    for tag, l, keys in groups:
        gathers[tag, l] = gather_of(keys, l, tag)
        token = gathers[tag, l].first(after=token)
    h = x[0]
    u = _rmsmod_fwd(h, *_norm_args(mods[0], g1s[0], 0, d), name="rmsmod_fwd")
    token = gathers["w_in", 0].forward(after=u)
    wm = {"w_in": gathers["w_in", 0].finish(after=token)[0]}
    saved, full = [], []
    for l in range(depth):
        last = l + 1 == depth
        sv = _mixer_fwd_a(h, u, gains[l], wm["w_in"], cos2, sin2)
        gathers["ffn", l].forward(after=sv["o_b"])
        if l == 0:
            gathers["rest", 0].forward(after=sv["o_b"])
            wm.update(zip(MIXER_W[1:], gathers["rest", 0].finish(after=sv["o_b"])))
        h_mid = _mixer_fwd_b(sv, mods[l], g2s[l], wm)
        wf = dict(zip(FFN_W, gathers["ffn", l].finish(after=h_mid)))
        a = _ffn_fwd_a(sv, wf["w_gate_up"])
        if not last:
            gathers["mixer", l + 1].forward(after=a)
        h, u = _ffn_fwd_b(sv, mods[l], wf["w_down"],
                          None if last else _norm_args(mods[l + 1], g1s[l + 1], 0, d))
        saved.append(sv)
        full.append({**wm, **wf})
        if not last:
            wm = dict(zip(MIXER_W, gathers["mixer", l + 1].finish(after=h)))
    def ffn_gate(l):
        return saved[l]["f"], _mod_rows(mods[l], d)[5]

    loss_part, dh, df, dgate2 = _loss_fwd(h, loss_target[0], *ffn_gate(depth - 1), name="loss")
    loss = lax.psum(loss_part[0, 0], ("x", "y", "c"))

    pipe = _ReducePipeline(core)
    dmods, dg1s, dg2s, dgains = [None] * depth, [None] * depth, [None] * depth, [None] * depth
    for l in reversed(range(depth)):
        dh_mid, dmod_f, dg2s[l], grads, dt, dgate1 = _ffn_bwd(dh, df, dgate2, saved[l], mods[l], g2s[l], full[l],
                                                              pipe.tick)
        pipe.tick(dh_mid)
        pipe.add(FFN_W, grads, l)
        dh, dmod_m, dg1s[l], dgains[l], grads, df, dgate2 = _mixer_bwd(
            dh_mid, dt, dgate1, saved[l], mods[l], g1s[l], gains[l], full[l], cos2, sin2,
            lambda after, early, l=l: (pipe.tick(after), pipe.add(MIXER_W[1:], early, l)),
            ffn_gate(l - 1) if l > 0 else None)
        dmods[l] = jnp.concatenate(dmod_m + dmod_f, axis=1)
        pipe.tick(dh)
        pipe.add(MIXER_W[:1], grads, l)
    grad_x = dh[None]

    stacked = {}

    def update(items):
        for keys, l, sums, remote in items:
            for k, p_, r_ in zip(keys, sums, remote):
                stacked[k] = _adamw_sharded(p_, r_, chip, weights[k], moments_m[k], moments_v[k], l,
                                            stacked.get(k), name="adamw_" + k)

    ready = pipe.take_done()
    update([it for it in ready if it[0] != FFN_W])

    small = jnp.concatenate(
        dmods + dg1s + dg2s + [dgains[l][0] for l in range(depth)] + [dgains[l][1] for l in range(depth)], axis=1)
    small_all = _small_allgather(small, name="comm_gather_small")
    pipe.tick(small_all)
    update([it for it in ready if it[0] == FFN_W] + pipe.take_done())

    def pack(b, n1, n2, qn, kn):
        return jnp.concatenate([t_.reshape(1, -1) for t_ in (b, n1, n2, qn, kn)], axis=1)

    sg, sd, sm, sv_ = _adamw_replicated(small_all, pack(b_ada, norm1_g, norm2_g, qn_g, kn_g),
                                        pack(m_b_ada, m_norm1_g, m_norm2_g, m_qn_g, m_kn_g),
                                        pack(v_b_ada, v_norm1_g, v_norm2_g, v_qn_g, v_kn_g), name="adamw_replicated")

    def unpack(p):
        sizes = [depth * 6 * d, depth * d, depth * d, depth * HEAD_DIM, depth * HEAD_DIM]
        shapes = [b_ada.shape, norm1_g.shape, norm2_g.shape, qn_g.shape, kn_g.shape]
        out, off = [], 0
        for n, shp in zip(sizes, shapes):
            out.append(p[0, off:off + n].reshape(shp))
            off += n
        return dict(zip(("b_ada", "norm1_g", "norm2_g", "qn_g", "kn_g"), out))

    ug, ud, um, uv = unpack(sg), unpack(sd), unpack(sm), unpack(sv_)
    res = {k: dict(g=ug[k], d=ud[k], m=um[k], v=uv[k]) for k in ug}

    dmod_all = small_all[:, 0, :depth * 6 * d].reshape(N_DEV, depth, 6 * d)
    g_ada = None
    for l in range(depth):
        dm = lax.dynamic_slice(dmod_all[:, l, :], (0, me * ada_w), (N_DEV, ada_w))
        dm = jnp.concatenate([dm, jnp.zeros_like(dm)], axis=0).astype(BF16)
        g_ada = _mm(c_pad, dm, ta=True, name="mm_wgrad_ada", stack=(l, depth, g_ada))
    d_ada, m_ada, v_ada = _adamw_local(g_ada, w_ada, m_w_ada, v_w_ada, name="adamw_local")
    res["w_ada"] = dict(g=g_ada, d=d_ada, m=m_ada, v=v_ada)

    pipe.tick(d_ada)
    update(pipe.take_done())
    pipe.tick(d_ada, flush=True)
    update(pipe.take_done())
    for k, (g_, d_, m_, v_) in stacked.items():
        res[k] = dict(g=g_, d=d_, m=m_, v=v_)

    order = ("w_ada", "b_ada", "norm1_g", "norm2_g", "w_in", "qn_g", "kn_g", "w_branch_a", "w_branch_b", "w_out",
             "w_gate_up", "w_down")
    _ORDER["token"] = None
    return (loss, grad_x, *[res[k]["g"] for k in order], *[res[k]["d"] for k in order],
            *[res[k]["m"] for k in order], *[res[k]["v"] for k in order])
```

```python
import functools

import jax
import jax.numpy as jnp
from jax import lax
from jax.experimental import pallas as pl
from jax.experimental.pallas import tpu as pltpu

F32 = jnp.float32
BF16 = jnp.bfloat16

HEAD_DIM = 128
BLOCK = 128
DILATIONS = (1, 4, 16)
HEADS_PER_GROUP = 4
A_HEADS = 12
SB_HEADS = 4
GROUP_W = HEADS_PER_GROUP * HEAD_DIM
A_W = A_HEADS * HEAD_DIM
B_W = SB_HEADS * HEAD_DIM
OFF_QA, OFF_KA, OFF_VA = 0, A_W, 2 * A_W
OFF_QB, OFF_KB, OFF_VB = 3 * A_W, 3 * A_W + B_W, 3 * A_W + 2 * B_W
OFF_GATES = 3 * A_W + 3 * B_W
ROPE_THETA = 10000.0
EPS = 1e-6
ATT_SCALE = HEAD_DIM ** -0.5
MASKED = -1e30

ADAM_LR, ADAM_B1, ADAM_B2, ADAM_EPS, ADAM_WD, ADAM_STEP = 0.001, 0.9, 0.999, 1e-08, 0.01, 10

N_DEV = 8
N_CHIPS = 4
V7X_VMEM_LIMIT_BYTES = 56 * 1024 * 1024
ELEMWISE_BLOCK_BYTES = 2 * 1024 * 1024
MESH = pl.DeviceIdType.MESH

NN = (((1,), (0,)), ((), ()))
NT = (((1,), (1,)), ((), ()))
TN = (((0,), (0,)), ((), ()))


def _dot(a, b, dims=NN):
    return lax.dot_general(a, b, dims, preferred_element_type=F32)


def _tile(n, cap, mult=128):
    best = None
    for t in range(mult, min(n, cap) + 1, mult):
        if n % t == 0:
            best = t
    if best is None:
        assert n <= 2 * cap, (n, cap)
        return n
    return best


def _rows(r, c):
    return _tile(r, max(16, ELEMWISE_BLOCK_BYTES // (4 * c)), 16)


_ORDER = {"token": None}
TOKEN = jax.ShapeDtypeStruct((8, 128), F32)


def _take_token():
    prev = _ORDER["token"]
    return [] if prev is None else [prev]


def _pcall(body, *, name, out_shape, grid=None, in_specs=None, out_specs=None, scratch=(), aliases=None,
           prefetch=0):
    single = not isinstance(out_shape, (tuple, list))
    out_shapes = [out_shape] if single else list(out_shape)
    out_specs = [out_specs] if single else list(out_specs)
    extra = _take_token()
    n_in, n_extra, n_out = prefetch + len(in_specs), len(extra), len(out_shapes)

    def wrapped(*refs):
        token = refs[n_in + n_extra + n_out]
        token[...] = jnp.zeros_like(token)
        return body(*refs[:n_in], *refs[n_in + n_extra:n_in + n_extra + n_out], *refs[n_in + n_extra + n_out + 1:])

    in_specs = list(in_specs) + [pl.BlockSpec(memory_space=pl.ANY)] * n_extra
    if grid is None:
        out_specs.append(pl.BlockSpec(memory_space=pltpu.VMEM))
    else:
        out_specs.append(pl.BlockSpec(TOKEN.shape, lambda *_: (0, 0)))
    kwargs = dict(name=name, out_shape=out_shapes + [TOKEN], input_output_aliases=aliases or {},
                  compiler_params=pltpu.CompilerParams(vmem_limit_bytes=V7X_VMEM_LIMIT_BYTES))
    if prefetch:
        call = pl.pallas_call(wrapped, grid_spec=pltpu.PrefetchScalarGridSpec(
            num_scalar_prefetch=prefetch, grid=grid, in_specs=in_specs, out_specs=out_specs,
            scratch_shapes=list(scratch)), **kwargs)
    else:
        if grid is not None:
            kwargs["grid"] = grid
        call = pl.pallas_call(wrapped, in_specs=in_specs, out_specs=out_specs, scratch_shapes=list(scratch), **kwargs)

    def run(*args):
        outs = call(*args, *extra)
        _ORDER["token"] = outs[-1]
        return outs[0] if single else tuple(outs[:-1])

    return run


def _mm(a, b, *, name, ta=False, tb=False, out_dtype=F32, caps=(1024, 1024, 3072), stack=None):
    kdim, m = a.shape if ta else a.shape[::-1]
    n, k2 = b.shape if tb else b.shape[::-1]
    assert kdim == k2, (a.shape, b.shape, ta, tb)
    tm, tn, tk = _tile(m, caps[0]), _tile(n, caps[1]), _tile(kdim, caps[2])
    nk = kdim // tk
    dims = (((0 if ta else 1,), (1 if tb else 0,)), ((), ()))

    def body(*refs):
        a_ref, b_ref = refs[0], refs[1]
        part = _dot(a_ref[...].astype(BF16), b_ref[...].astype(BF16), dims)
        if nk == 1:
            o_ref = refs[-1]
            o_ref[...] = part.astype(o_ref.dtype)
            return
        o_ref, acc_ref = refs[-2], refs[-1]
        k = pl.program_id(2)

        @pl.when(k == 0)
        def _():
            acc_ref[...] = part

        @pl.when(k > 0)
        def _():
            acc_ref[...] += part

        @pl.when(k == nk - 1)
        def _():
            o_ref[...] = acc_ref[...].astype(o_ref.dtype)

    a_spec = (pl.BlockSpec((tk, tm), lambda i, j, k: (k, i)) if ta
              else pl.BlockSpec((tm, tk), lambda i, j, k: (i, k)))
    b_spec = (pl.BlockSpec((tn, tk), lambda i, j, k: (j, k)) if tb
              else pl.BlockSpec((tk, tn), lambda i, j, k: (k, j)))
    ins, in_specs, aliases = [a, b], [a_spec, b_spec], {}
    if stack is None:
        out_shape = jax.ShapeDtypeStruct((m, n), out_dtype)
        out_spec = pl.BlockSpec((tm, tn), lambda i, j, k: (i, j))
    else:
        layer, n_layers, buf = stack
        out_shape = jax.ShapeDtypeStruct((n_layers, m, n), out_dtype)
        out_spec = pl.BlockSpec((None, tm, tn), lambda i, j, k: (layer, i, j))
        if buf is not None:
            ins.append(buf)
            in_specs.append(pl.BlockSpec(memory_space=pl.ANY))
            aliases = {2: 0}
    scratch = [] if nk == 1 else [pltpu.VMEM((tm, tn), F32)]
    return _pcall(body, name=name, out_shape=out_shape, grid=(m // tm, n // tn, nk), in_specs=in_specs,
                  out_specs=out_spec, scratch=scratch, aliases=aliases)(*ins)


EPILOGUE_ROWS = 256


def _row_chunks(tm):
    return [slice(r, r + EPILOGUE_ROWS) for r in range(0, tm, EPILOGUE_ROWS)] if tm > EPILOGUE_ROWS else [slice(0, tm)]


def _mm_cat_k(a_lo, a_hi, b, *, name):
    m, f = a_lo.shape
    n = b.shape[0]
    tm, tn, tk = _tile(m, 1024), _tile(n, 1024), _tile(f, 3072)
    half = f // tk
    nk = 2 * half

    def body(lo_ref, hi_ref, b_ref, o_ref, acc_ref):
        k = pl.program_id(2)

        def accumulate(a_ref):
            part = _dot(a_ref[...], b_ref[...], NT)

            @pl.when(k == 0)
            def _():
                acc_ref[...] = part

            @pl.when(k > 0)
            def _():
                acc_ref[...] += part

        pl.when(k < half)(lambda: accumulate(lo_ref))
        pl.when(k >= half)(lambda: accumulate(hi_ref))

        @pl.when(k == nk - 1)
        def _():
            o_ref[...] = acc_ref[...]

    return _pcall(body, name=name, out_shape=jax.ShapeDtypeStruct((m, n), F32), grid=(m // tm, n // tn, nk),
                  in_specs=[pl.BlockSpec((tm, tk), lambda i, j, k: (i, jnp.minimum(k, half - 1))),
                            pl.BlockSpec((tm, tk), lambda i, j, k: (i, jnp.maximum(k - half, 0))),
                            pl.BlockSpec((tn, tk), lambda i, j, k: (j, k))],
                  out_specs=pl.BlockSpec((tm, tn), lambda i, j, k: (i, j)),
                  scratch=[pltpu.VMEM((tm, tn), F32)])(a_lo, a_hi, b)


def _mm_cat_n(a, b_lo, b_hi, *, name):
    s, m = a.shape
    f = b_lo.shape[1]
    tm, tn = _tile(m, 1024), _tile(f, 1024)
    half = f // tn

    def body(a_ref, lo_ref, hi_ref, o_ref):
        j = pl.program_id(1)

        @pl.when(j < half)
        def _():
            o_ref[...] = _dot(a_ref[...], lo_ref[...], TN).astype(BF16)

        @pl.when(j >= half)
        def _():
            o_ref[...] = _dot(a_ref[...], hi_ref[...], TN).astype(BF16)

    return _pcall(body, name=name, out_shape=jax.ShapeDtypeStruct((m, 2 * f), BF16), grid=(m // tm, 2 * half),
                  in_specs=[pl.BlockSpec((s, tm), lambda i, j: (0, i)),
                            pl.BlockSpec((s, tn), lambda i, j: (0, jnp.minimum(j, half - 1))),
                            pl.BlockSpec((s, tn), lambda i, j: (0, jnp.maximum(j - half, 0)))],
                  out_specs=pl.BlockSpec((tm, tn), lambda i, j: (i, j)))(a, b_lo, b_hi)


def _mm_resid_norm(a, w, h, gate, norm, *, name):
    s, kdim = a.shape
    d = w.shape[1]
    tk = _tile(kdim, 2048)
    nk = kdim // tk
    tm = _tile(s, 256 if nk == 1 else 512)

    def body(*refs):
        a_ref, w_ref, h_ref, gate_ref = refs[:4]
        outs = refs[7:] if norm is not None else refs[4:]

        def finish(rows, t):
            hn = h_ref[rows, :] + gate_ref[...] * t
            outs[0][rows, :] = hn
            outs[1][rows, :] = t.astype(BF16)
            if norm is not None:
                g_ref, sc_ref, sh_ref = refs[4:7]
                r = lax.rsqrt(jnp.mean(hn * hn, axis=-1, keepdims=True) + EPS)
                outs[2][rows, :] = (((hn * r) * g_ref[...]) * (1.0 + sc_ref[...]) + sh_ref[...]).astype(BF16)

        if nk == 1:
            for rows in _row_chunks(tm):
                finish(rows, _dot(a_ref[rows, :], w_ref[...]))
            return
        acc_ref = refs[-1]
        k = pl.program_id(1)

        @pl.when(k == 0)
        def _():
            acc_ref[...] = _dot(a_ref[...], w_ref[...])

        @pl.when(jnp.logical_and(k > 0, k < nk - 1))
        def _():
            acc_ref[...] += _dot(a_ref[...], w_ref[...])

        @pl.when(k == nk - 1)
        def _():
            for rows in _row_chunks(tm):
                finish(rows, acc_ref[rows, :] + _dot(a_ref[rows, :], w_ref[...]))

    row = pl.BlockSpec((tm, d), lambda i, k: (i, 0))
    vec = pl.BlockSpec((1, d), lambda i, k: (0, 0))
    in_specs = [pl.BlockSpec((tm, tk), lambda i, k: (i, k)), pl.BlockSpec((tk, d), lambda i, k: (k, 0)), row, vec]
    args = [a, w, h, gate]
    out_shape = [jax.ShapeDtypeStruct((s, d), F32), jax.ShapeDtypeStruct((s, d), BF16)]
    if norm is not None:
        in_specs += [vec, vec, vec]
        args += list(norm)
        out_shape.append(jax.ShapeDtypeStruct((s, d), BF16))
    outs = _pcall(body, name=name, out_shape=tuple(out_shape), grid=(s // tm, nk), in_specs=in_specs,
                  out_specs=(row,) * len(out_shape), scratch=[] if nk == 1 else [pltpu.VMEM((tm, d), F32)])(*args)
    return outs if norm is not None else (*outs, None)


def _mm_merge(o_a, o_b, w_a, w_b, proj, *, name):
    s = o_a.shape[0]
    d = w_a.shape[1]
    tm = _tile(s, 512)
    ga_blk = OFF_GATES // d

    def body(oa_ref, ob_ref, wa_ref, wb_ref, ga_ref, gb_ref, m_ref, ya_ref, yb_ref):
        for rows in _row_chunks(tm):
            ya, yb = _dot(oa_ref[rows, :], wa_ref[...]), _dot(ob_ref[rows, :], wb_ref[...])
            m_ref[rows, :] = (jax.nn.sigmoid(ga_ref[rows, :]) * ya
                              + jax.nn.sigmoid(gb_ref[rows, :]) * yb).astype(BF16)
            ya_ref[rows, :] = ya.astype(BF16)
            yb_ref[rows, :] = yb.astype(BF16)

    row = pl.BlockSpec((tm, d), lambda i: (i, 0))
    act = pl.BlockSpec((tm, o_a.shape[1]), lambda i: (i, 0))
    wspec = pl.BlockSpec(w_a.shape, lambda i: (0, 0))
    shp = jax.ShapeDtypeStruct((s, d), BF16)
    return _pcall(body, name=name, out_shape=(shp, shp, shp), grid=(s // tm,),
                  in_specs=[act, act, wspec, wspec, pl.BlockSpec((tm, d), lambda i: (i, ga_blk)),
                            pl.BlockSpec((tm, d), lambda i: (i, ga_blk + 1))],
                  out_specs=(row, row, row))(o_a, o_b, w_a, w_b, proj, proj)


def _mm_down_t_swiglu(df, w_down, g, u, *, name):
    s, d = df.shape
    f = w_down.shape[0]
    tm, tn = _tile(s, 1024), _tile(f, 512)

    def body(df_ref, w_ref, g_ref, u_ref, dg_ref, du_ref):
        w = w_ref[...]
        for rows in _row_chunks(tm):
            da = _dot(df_ref[rows, :], w, NT)
            gf = g_ref[rows, :].astype(F32)
            sg = jax.nn.sigmoid(gf)
            dg_ref[rows, :] = (da * u_ref[rows, :].astype(F32) * (sg * (1.0 + gf * (1.0 - sg)))).astype(BF16)
            du_ref[rows, :] = (da * (gf * sg)).astype(BF16)

    tile = pl.BlockSpec((tm, tn), lambda i, j: (i, j))
    shp = jax.ShapeDtypeStruct((s, f), BF16)
    return _pcall(body, name=name, out_shape=(shp, shp), grid=(s // tm, f // tn),
                  in_specs=[pl.BlockSpec((tm, d), lambda i, j: (i, 0)), pl.BlockSpec((tn, d), lambda i, j: (j, 0)),
                            tile, tile],
                  out_specs=(tile, tile))(df, w_down, g, u)


def _rmsmod_fwd(h, g, scale, shift, *, name):
    s, d = h.shape
    ts = _rows(s, d)

    def body(h_ref, g_ref, sc_ref, sh_ref, u_ref):
        hf = h_ref[...]
        r = lax.rsqrt(jnp.mean(hf * hf, axis=-1, keepdims=True) + EPS)
        u_ref[...] = (((hf * r) * g_ref[...]) * (1.0 + sc_ref[...]) + sh_ref[...]).astype(BF16)

    row = pl.BlockSpec((ts, d), lambda i: (i, 0))
    vec = pl.BlockSpec((1, d), lambda i: (0, 0))
    return _pcall(body, name=name, out_shape=jax.ShapeDtypeStruct((s, d), BF16), grid=(s // ts,),
                  in_specs=[row, vec, vec, vec], out_specs=row)(h, g, scale, shift)


def _gate_bwd(dhf, t_ref, gate_ref, dt_ref, dgate_ref):
    dt_ref[...] = (dhf * gate_ref[...]).astype(BF16)
    dgate_ref[...] += jnp.sum(dhf * t_ref[...], axis=0, keepdims=True)


def _rmsmod_bwd(du, h, g, scale, dres, t, gate, *, name):
    s, d = h.shape
    ts = _rows(s, d)
    chain = t is not None

    def body(*refs):
        du_ref, h_ref, g_ref, sc_ref, dres_ref = refs[:5]
        dh_ref, dsh_ref, dsc_ref, dg_ref = refs[-6:-2] if chain else refs[-4:]
        sums = (dsh_ref, dsc_ref, dg_ref) + ((refs[-1],) if chain else ())

        @pl.when(pl.program_id(0) == 0)
        def _():
            for ref in sums:
                ref[...] = jnp.zeros_like(ref)

        hf, duf, gain = h_ref[...], du_ref[...], g_ref[...]
        r = lax.rsqrt(jnp.mean(hf * hf, axis=-1, keepdims=True) + EPS)
        xh = hf * r
        dn = duf * (1.0 + sc_ref[...])
        dsh_ref[...] += jnp.sum(duf, axis=0, keepdims=True)
        dsc_ref[...] += jnp.sum(duf * (xh * gain), axis=0, keepdims=True)
        dg_ref[...] += jnp.sum(dn * xh, axis=0, keepdims=True)
        dxh = dn * gain
        dh = dres_ref[...] + r * (dxh - xh * jnp.mean(dxh * xh, axis=-1, keepdims=True))
        dh_ref[...] = dh
        if chain:
            _gate_bwd(dh, refs[5], refs[6], refs[-2], refs[-1])

    row = pl.BlockSpec((ts, d), lambda i: (i, 0))
    vec = pl.BlockSpec((1, d), lambda i: (0, 0))
    vshape = jax.ShapeDtypeStruct((1, d), F32)
    out_shape, out_specs = [jax.ShapeDtypeStruct((s, d), F32), vshape, vshape, vshape], [row, vec, vec, vec]
    in_specs, args = [row, row, vec, vec, row], [du, h, g, scale, dres]
    if chain:
        in_specs, args = in_specs + [row, vec], args + [t, gate]
        out_shape, out_specs = out_shape + [jax.ShapeDtypeStruct((s, d), BF16), vshape], out_specs + [row, vec]
    outs = _pcall(body, name=name, out_shape=tuple(out_shape), grid=(s // ts,), in_specs=in_specs,
                  out_specs=tuple(out_specs))(*args)
    return outs if chain else (*outs, None, None)


def _merge_bwd(dm, proj, y_a, y_b, *, name):
    s, d = y_a.shape
    ts = _rows(s, d)
    ga_blk = OFF_GATES // d

    def body(dm_ref, ga_ref, gb_ref, ya_ref, yb_ref, dya_ref, dyb_ref, dga_ref, dgb_ref):
        dmf = dm_ref[...]
        sa, sb = jax.nn.sigmoid(ga_ref[...]), jax.nn.sigmoid(gb_ref[...])
        dya_ref[...] = (dmf * sa).astype(BF16)
        dyb_ref[...] = (dmf * sb).astype(BF16)
        dga_ref[...] = (dmf * ya_ref[...] * (sa * (1.0 - sa))).astype(BF16)
        dgb_ref[...] = (dmf * yb_ref[...] * (sb * (1.0 - sb))).astype(BF16)

    row = pl.BlockSpec((ts, d), lambda i: (i, 0))
    ga = pl.BlockSpec((ts, d), lambda i: (i, ga_blk))
    gb = pl.BlockSpec((ts, d), lambda i: (i, ga_blk + 1))
    shp = jax.ShapeDtypeStruct((s, d), BF16)
    return _pcall(body, name=name, out_shape=(shp, shp, shp, shp), grid=(s // ts,),
                  in_specs=[row, ga, gb, row, row], out_specs=(row, row, row, row))(dm, proj, proj, y_a, y_b)


def _mm_swiglu(u2, w_gate_up, *, name):
    s, d = u2.shape
    f = w_gate_up.shape[1] // 2
    tm, tn = _tile(s, 1024), _tile(f, 512)
    nj = f // tn

    def body(x_ref, wg_ref, wu_ref, a_ref, g_ref, u_ref):
        for rows in _row_chunks(tm):
            x = x_ref[rows, :]
            gf, uf = _dot(x, wg_ref[...]), _dot(x, wu_ref[...])
            a_ref[rows, :] = ((gf * jax.nn.sigmoid(gf)) * uf).astype(BF16)
            g_ref[rows, :] = gf.astype(BF16)
            u_ref[rows, :] = uf.astype(BF16)

    out = pl.BlockSpec((tm, tn), lambda i, j: (i, j))
    shp = jax.ShapeDtypeStruct((s, f), BF16)
    return _pcall(body, name=name, out_shape=(shp, shp, shp), grid=(s // tm, nj),
                  in_specs=[pl.BlockSpec((tm, d), lambda i, j: (i, 0)), pl.BlockSpec((d, tn), lambda i, j: (0, j)),
                            pl.BlockSpec((d, tn), lambda i, j: (0, nj + j))],
                  out_specs=(out, out, out))(u2, w_gate_up, w_gate_up)


def _loss_fwd(y, tgt, t, gate, *, name):
    s, d = y.shape
    ts = _rows(s, d)

    def body(y_ref, tgt_ref, t_ref, gate_ref, l_ref, dy_ref, dt_ref, dgate_ref):
        @pl.when(pl.program_id(0) == 0)
        def _():
            l_ref[...] = jnp.zeros_like(l_ref)
            dgate_ref[...] = jnp.zeros_like(dgate_ref)

        e = y_ref[...] - tgt_ref[...]
        dy = e * (1.0 / d)
        dy_ref[...] = dy
        per_tok = jnp.sum(e * e, axis=1, keepdims=True) * (1.0 / d)
        l_ref[...] += 0.5 * jnp.sum(per_tok, axis=0, keepdims=True)
        _gate_bwd(dy, t_ref, gate_ref, dt_ref, dgate_ref)

    row = pl.BlockSpec((ts, d), lambda i: (i, 0))
    vec = pl.BlockSpec((1, d), lambda i: (0, 0))
    return _pcall(body, name=name,
                  out_shape=(jax.ShapeDtypeStruct((1, 128), F32), jax.ShapeDtypeStruct((s, d), F32),
                             jax.ShapeDtypeStruct((s, d), BF16), jax.ShapeDtypeStruct((1, d), F32)),
                  grid=(s // ts,), in_specs=[row, row, row, vec],
                  out_specs=(pl.BlockSpec((1, 128), lambda i: (0, 0)), row, row, vec))(y, tgt, t, gate)


def _rope_tables(seq):
    inv = jnp.power(ROPE_THETA, -jnp.arange(0, HEAD_DIM, 2, dtype=F32) / HEAD_DIM)
    ang = jnp.arange(seq, dtype=F32)[:, None] * inv[None, :]
    cos, sin = jnp.cos(ang), jnp.sin(ang)
    return jnp.concatenate([cos, cos], axis=1), jnp.concatenate([-sin, sin], axis=1)


def _qkrope_fwd(proj, gains, cos2, sin2, *, name):
    s = proj.shape[0]
    ts = _rows(s, A_W)

    def body(x_ref, g_ref, c_ref, s_ref, o_ref):
        gain, cos, sin = g_ref[...], c_ref[...], s_ref[...]
        for h in range(A_HEADS):
            lanes = slice(h * HEAD_DIM, (h + 1) * HEAD_DIM)
            x = x_ref[:, lanes]
            y = (x * lax.rsqrt(jnp.mean(x * x, axis=-1, keepdims=True) + EPS)) * gain
            o_ref[:, lanes] = (y * cos + pltpu.roll(y, HEAD_DIM // 2, 1) * sin).astype(BF16)

    heads = pl.BlockSpec((ts, A_W), lambda i, j: (i, j))
    tab = pl.BlockSpec((ts, HEAD_DIM), lambda i, j: (i, 0))
    gain = pl.BlockSpec((None, 1, HEAD_DIM), lambda i, j: (j, 0, 0))
    return _pcall(body, name=name, out_shape=jax.ShapeDtypeStruct((s, 2 * A_W), BF16),
                  grid=(s // ts, 2), in_specs=[heads, gain, tab, tab], out_specs=heads)(
                      proj, gains, cos2, sin2)


def _qkrope_bwd(d_groups, proj, gains, which, cos2, sin2, *, name):
    s = proj.shape[0]
    ts = _rows(s, A_W)

    def body(d0_ref, d1_ref, d2_ref, x_ref, g_ref, c_ref, s_ref, dx_ref, dg_ref):
        @pl.when(pl.program_id(0) == 0)
        def _():
            dg_ref[...] = jnp.zeros_like(dg_ref)

        gain, cos, sin = g_ref[...], c_ref[...], s_ref[...]
        dg = jnp.zeros((1, HEAD_DIM), F32)
        for h in range(A_HEADS):
            lanes = slice(h * HEAD_DIM, (h + 1) * HEAD_DIM)
            slot = slice((h % HEADS_PER_GROUP) * HEAD_DIM, (h % HEADS_PER_GROUP + 1) * HEAD_DIM)
            dout = (d0_ref, d1_ref, d2_ref)[h // HEADS_PER_GROUP][:, slot]
            dy = dout * cos + pltpu.roll(dout * sin, HEAD_DIM // 2, 1)
            x = x_ref[:, lanes]
            r = lax.rsqrt(jnp.mean(x * x, axis=-1, keepdims=True) + EPS)
            xh = x * r
            dg = dg + jnp.sum(dy * xh, axis=0, keepdims=True)
            dxh = dy * gain
            dx_ref[:, lanes] = (r * (dxh - xh * jnp.mean(dxh * xh, axis=-1, keepdims=True))).astype(BF16)
        dg_ref[...] += dg

    group = pl.BlockSpec((ts, GROUP_W), lambda i: (i, 0))
    tab = pl.BlockSpec((ts, HEAD_DIM), lambda i: (i, 0))
    gain = pl.BlockSpec((None, 1, HEAD_DIM), lambda i: (which, 0, 0))
    return _pcall(body, name=name,
                  out_shape=(jax.ShapeDtypeStruct((s, A_W), BF16), jax.ShapeDtypeStruct((1, HEAD_DIM), F32)),
                  grid=(s // ts,),
                  in_specs=[group, group, group, pl.BlockSpec((ts, A_W), lambda i: (i, which)), gain, tab, tab],
                  out_specs=(pl.BlockSpec((ts, A_W), lambda i: (i, 0)), pl.BlockSpec((1, HEAD_DIM), lambda i: (0, 0))))(
                      *d_groups, proj, gains, cos2, sin2)


def _assemble(pieces, *, name):
    s = pieces[0].shape[0]
    widths = [p.shape[1] for p in pieces]
    total = sum(widths)
    ts = _rows(s, total // 2)

    def body(*refs):
        o_ref, off = refs[-1], 0
        for x_ref, w in zip(refs[:-1], widths):
            o_ref[:, off:off + w] = x_ref[...].astype(BF16)
            off += w

    return _pcall(body, name=name, out_shape=jax.ShapeDtypeStruct((s, total), BF16), grid=(s // ts,),
                  in_specs=[pl.BlockSpec((ts, w), lambda i: (i, 0)) for w in widths],
                  out_specs=pl.BlockSpec((ts, total), lambda i: (i, 0)))(*pieces)


def _block_rows(blk):
    if isinstance(blk, int):
        return pl.ds(blk * BLOCK, BLOCK)
    return pl.ds(pl.multiple_of(blk * BLOCK, BLOCK), BLOCK)


def _band_window(n, length):
    width = min(2 * BLOCK, length)
    row = lax.broadcasted_iota(jnp.int32, (BLOCK, width), 0)
    col = lax.broadcasted_iota(jnp.int32, (BLOCK, width), 1)
    if width == BLOCK:
        return pl.ds(0, BLOCK), col <= row
    first = n - 1 if isinstance(n, int) else jnp.maximum(n - 1, 0)
    first = max(first, 0) if isinstance(first, int) else first
    dist = row - col + (n - first) * BLOCK
    start = first * BLOCK if isinstance(first, int) else pl.multiple_of(first * BLOCK, BLOCK)
    return pl.ds(start, width), jnp.logical_and(dist >= 0, dist <= BLOCK)


def _dil_fwd(q_arr, k_arr, v_arr, offs, length, dil, *, name):
    nj, nb = dil * HEADS_PER_GROUP, length // BLOCK
    ju, nq = (HEADS_PER_GROUP, 2) if nb > 1 else (2 * HEADS_PER_GROUP, 1)
    qo, ko, vo = (off // ju for off in offs)
    assert all(off % ju == 0 for off in offs) and nb % nq == 0 and nj % ju == 0

    def body(q_ref, k_ref, v_ref, o_ref, l_ref):
        for qq in range(nq):
            qrows = slice(qq * BLOCK, (qq + 1) * BLOCK)
            rows, mask = _band_window(pl.program_id(1) * nq + qq, length)
            for cb in range(ju):
                lanes = slice(cb * HEAD_DIM, (cb + 1) * HEAD_DIM)
                sc = _dot(q_ref[qrows, lanes].astype(BF16), k_ref[rows, lanes].astype(BF16), NT) * ATT_SCALE
                sc = jnp.where(mask, sc, MASKED)
                m = sc.max(axis=-1, keepdims=True)
                p = jnp.exp(sc - m)
                den = jnp.sum(p, axis=-1, keepdims=True)
                acc = _dot(p.astype(BF16), v_ref[rows, lanes].astype(BF16))
                o_ref[qrows, lanes] = acc / den
                l_ref[qrows, lanes] = jnp.broadcast_to(m + jnp.log(den), (BLOCK, HEAD_DIM))

    qspec = pl.BlockSpec((nq * BLOCK, ju * HEAD_DIM), lambda j, n: (n, qo + j))
    kspec = pl.BlockSpec((length, ju * HEAD_DIM), lambda j, n: (0, ko + j))
    vspec = pl.BlockSpec((length, ju * HEAD_DIM), lambda j, n: (0, vo + j))
    ospec = pl.BlockSpec((nq * BLOCK, ju * HEAD_DIM), lambda j, n: (n, j))
    shp = jax.ShapeDtypeStruct((length, nj * HEAD_DIM), F32)
    return _pcall(body, name=name, out_shape=(shp, shp), grid=(nj // ju, nb // nq), in_specs=[qspec, kspec, vspec],
                  out_specs=(ospec, ospec))(q_arr, k_arr, v_arr)


def _dil_bwd(q_arr, k_arr, v_arr, offs, o, lse, do, dlse, length, dil, *, name):
    nj, nb = dil * HEADS_PER_GROUP, length // BLOCK
    ju = 2 * HEADS_PER_GROUP if length <= 4 * BLOCK else 2
    qo, ko, vo = (off // ju for off in offs)
    assert all(off % ju == 0 for off in offs)

    def body(q_ref, k_ref, v_ref, o_ref, l_ref, do_ref, dl_ref, dq_ref, dk_ref, dv_ref):
        dk_ref[...] = jnp.zeros_like(dk_ref)
        dv_ref[...] = jnp.zeros_like(dv_ref)

        def step(n, carry):
            qrows = _block_rows(n)
            rows, mask = _band_window(n, length)
            for cb in range(ju):
                lanes = slice(cb * HEAD_DIM, (cb + 1) * HEAD_DIM)
                q = q_ref[qrows, lanes].astype(BF16)
                dof = do_ref[qrows, lanes]
                dob = dof.astype(BF16)
                lse_c = l_ref[qrows, lanes][:, :1]
                shift = dl_ref[qrows, lanes][:, :1] - jnp.sum(dof * o_ref[qrows, lanes], axis=-1, keepdims=True)
                kk, vv = k_ref[rows, lanes].astype(BF16), v_ref[rows, lanes].astype(BF16)
                sc = _dot(q, kk, NT) * ATT_SCALE
                p = jnp.where(mask, jnp.exp(sc - lse_c), 0.0)
                ds = (p * (_dot(dob, vv, NT) + shift)).astype(BF16)
                dq_ref[qrows, lanes] = _dot(ds, kk) * ATT_SCALE
                dk_ref[rows, lanes] += _dot(ds, q, TN) * ATT_SCALE
                dv_ref[rows, lanes] += _dot(p.astype(BF16), dob, TN)
            return carry

        if nb == 1:
            step(0, 0)
        else:
            lax.fori_loop(0, nb, step, 0)

    def col(off):
        return pl.BlockSpec((length, ju * HEAD_DIM), lambda j: (0, off + j))

    shp = jax.ShapeDtypeStruct((length, nj * HEAD_DIM), F32)
    return _pcall(body, name=name, out_shape=(shp, shp, shp), grid=(nj // ju,),
                  in_specs=[col(qo), col(ko), col(vo), col(0), col(0), col(0), col(0)],
                  out_specs=(col(0), col(0), col(0)))(q_arr, k_arr, v_arr, o, lse, do, dlse)


def _combine_weights(l_refs):
    ls = [r[...] for r in l_refs]
    m = jnp.maximum(jnp.maximum(ls[0], ls[1]), ls[2])
    es = [jnp.exp(l - m) for l in ls]
    den = es[0] + es[1] + es[2]
    return [e / den for e in es]


def _combine_fwd(os_, lses, *, name):
    s = os_[0].shape[0]
    ts = _rows(s, GROUP_W)

    def body(o0, o1, o2, l0, l1, l2, out_ref):
        w = _combine_weights((l0, l1, l2))
        out_ref[...] = (w[0] * o0[...] + w[1] * o1[...] + w[2] * o2[...]).astype(BF16)

    row = pl.BlockSpec((ts, GROUP_W), lambda i: (i, 0))
    return _pcall(body, name=name, out_shape=jax.ShapeDtypeStruct((s, GROUP_W), BF16), grid=(s // ts,),
                  in_specs=[row] * 6, out_specs=row)(*os_, *lses)


def _combine_bwd(do_a, os_, lses, *, name):
    s = do_a.shape[0]
    ts = _rows(s, GROUP_W)

    def body(d_ref, o0, o1, o2, l0, l1, l2, do0, do1, do2, dl0, dl1, dl2):
        w = _combine_weights((l0, l1, l2))
        d = d_ref[...]
        og = [o0[...], o1[...], o2[...]]
        oa = w[0] * og[0] + w[1] * og[1] + w[2] * og[2]
        ta = jnp.sum(d * oa, axis=-1, keepdims=True)
        for g, (do_ref, dl_ref) in enumerate(((do0, dl0), (do1, dl1), (do2, dl2))):
            do_ref[...] = w[g] * d
            dl_ref[...] = w[g] * (jnp.sum(d * og[g], axis=-1, keepdims=True) - ta)

    head = pl.BlockSpec((ts, HEAD_DIM), lambda i, h: (i, h))
    shp = jax.ShapeDtypeStruct((s, GROUP_W), F32)
    return _pcall(body, name=name, out_shape=(shp,) * 6, grid=(s // ts, HEADS_PER_GROUP),
                  in_specs=[head] * 7, out_specs=(head,) * 6)(do_a, *os_, *lses)


def _dot_exact(x, ones_mask):
    hi = x.astype(BF16)
    r1 = x - hi.astype(F32)
    mid = r1.astype(BF16)
    lo = (r1 - mid.astype(F32)).astype(BF16)
    return _dot(hi, ones_mask) + _dot(mid, ones_mask) + _dot(lo, ones_mask)


SB_QROWS = 2 * BLOCK
SB_UNROLL = 4
SB_HEADS_PER_STEP = 2
SB_LANES = [slice(hh * HEAD_DIM, (hh + 1) * HEAD_DIM) for hh in range(SB_HEADS_PER_STEP)]


def _sb_mask(j, i):
    row = lax.broadcasted_iota(jnp.int32, (SB_QROWS, BLOCK), 0)
    col = lax.broadcasted_iota(jnp.int32, (SB_QROWS, BLOCK), 1)
    return col + (j * BLOCK - i * SB_QROWS) < row


def _sb_steps(i):
    return ((i + 1) * (SB_QROWS // BLOCK) + SB_UNROLL - 1) // SB_UNROLL


def _sb_scores(q, kk, j, i, masked):
    mask = _sb_mask(j, i) if masked else None
    z = _dot(q, kk, NT) * ATT_SCALE
    sp = jnp.log(1.0 + jnp.exp(-jnp.abs(z)))
    log_beta = jnp.minimum(z, 0.0) - sp
    log_1mb = jnp.minimum(-z, 0.0) - sp
    if masked:
        log_1mb = jnp.where(mask, log_1mb, 0.0)
    return z, log_beta, log_1mb, mask


def _sb_weights(log_beta, log_1mb, mask, run, upper):
    a = jnp.exp(log_beta + (run + _dot_exact(log_1mb, upper)))
    return a if mask is None else jnp.where(mask, a, 0.0)


def _sb_peeled(nsteps, make_step, init, masked_first):
    if masked_first:
        return lax.fori_loop(1, nsteps, make_step(False), make_step(True)(0, init))
    return make_step(True)(nsteps - 1, lax.fori_loop(0, nsteps - 1, make_step(False), init))


def _tri(strict_lower):
    row = lax.broadcasted_iota(jnp.int32, (BLOCK, BLOCK), 0)
    col = lax.broadcasted_iota(jnp.int32, (BLOCK, BLOCK), 1)
    return ((row > col) if strict_lower else (row < col)).astype(BF16)


def _sb_fwd(proj, *, name):
    s = proj.shape[0]
    assert s % (BLOCK * SB_UNROLL) == 0 and s % SB_QROWS == 0

    def body(q_ref, k_ref, v_ref, o_ref):
        i = pl.program_id(1)
        qs = [q_ref[:, lanes].astype(BF16) for lanes in SB_LANES]
        upper = _tri(True)
        nsteps = _sb_steps(i)

        def make_step(masked):
            def step(t, carry):
                carry = list(carry)
                for b in reversed(range(SB_UNROLL)):
                    j = (nsteps - 1 - t) * SB_UNROLL + b
                    rows = _block_rows(j)
                    for hh, lanes in enumerate(SB_LANES):
                        acc, run = carry[hh]
                        _, log_beta, log_1mb, mask = _sb_scores(qs[hh], k_ref[rows, lanes].astype(BF16), j, i, masked)
                        a = _sb_weights(log_beta, log_1mb, mask, run, upper)
                        carry[hh] = (acc + _dot(a.astype(BF16), v_ref[rows, lanes].astype(BF16)),
                                     run + jnp.sum(log_1mb, axis=-1, keepdims=True))
                return tuple(carry)
            return step

        zero = (jnp.zeros((SB_QROWS, HEAD_DIM), F32), jnp.zeros((SB_QROWS, 1), F32))
        for lanes, (acc, _) in zip(SB_LANES, _sb_peeled(nsteps, make_step, (zero,) * SB_HEADS_PER_STEP, True)):
            o_ref[:, lanes] = acc.astype(BF16)

    width = SB_HEADS_PER_STEP * HEAD_DIM
    qb, kb, vb = (off // width for off in (OFF_QB, OFF_KB, OFF_VB))
    return _pcall(body, name=name, out_shape=jax.ShapeDtypeStruct((s, B_W), BF16),
                  grid=(SB_HEADS // SB_HEADS_PER_STEP, s // SB_QROWS),
                  in_specs=[pl.BlockSpec((SB_QROWS, width), lambda h, i: (i, qb + h)),
                            pl.BlockSpec((s, width), lambda h, i: (0, kb + h)),
                            pl.BlockSpec((s, width), lambda h, i: (0, vb + h))],
                  out_specs=pl.BlockSpec((SB_QROWS, width), lambda h, i: (i, h)))(proj, proj, proj)


def _sb_bwd(proj, do_b, *, name):
    s = proj.shape[0]
    assert s % (BLOCK * SB_UNROLL) == 0 and s % SB_QROWS == 0
    nkb = s // BLOCK

    def body(q_ref, k_ref, v_ref, do_ref, dq_ref, dk_ref, dv_ref, z_s, a_s):
        i = pl.program_id(1)

        @pl.when(i == 0)
        def _():
            dk_ref[...] = jnp.zeros_like(dk_ref)
            dv_ref[...] = jnp.zeros_like(dv_ref)

        qs = [q_ref[:, lanes].astype(BF16) for lanes in SB_LANES]
        dobs = [do_ref[:, lanes].astype(BF16) for lanes in SB_LANES]
        upper, lower = _tri(True), _tri(False)
        nsteps = _sb_steps(i)

        def make_recompute(masked):
            def recompute(t, runs):
                runs = list(runs)
                for b in reversed(range(SB_UNROLL)):
                    j = (nsteps - 1 - t) * SB_UNROLL + b
                    rows = _block_rows(j)
                    for hh, lanes in enumerate(SB_LANES):
                        z, log_beta, log_1mb, mask = _sb_scores(qs[hh], k_ref[rows, lanes].astype(BF16), j, i, masked)
                        z_s[hh, j] = z
                        a_s[hh, j] = _sb_weights(log_beta, log_1mb, mask, runs[hh], upper)
                        runs[hh] = runs[hh] + jnp.sum(log_1mb, axis=-1, keepdims=True)
                return tuple(runs)
            return recompute

        _sb_peeled(nsteps, make_recompute, (jnp.zeros((SB_QROWS, 1), F32),) * SB_HEADS_PER_STEP, True)

        def make_grads(masked):
            def grads(t, carry):
                carry = list(carry)
                for b in range(SB_UNROLL):
                    j = t * SB_UNROLL + b
                    rows = _block_rows(j)
                    for hh, lanes in enumerate(SB_LANES):
                        dq, run = carry[hh]
                        kk, vv = k_ref[rows, lanes].astype(BF16), v_ref[rows, lanes].astype(BF16)
                        z, a = z_s[hh, j], a_s[hh, j]
                        de = _dot(dobs[hh], vv, NT) * a
                        beta = jax.nn.sigmoid(z)
                        one_minus_beta = 1.0 - beta
                        if masked:
                            beta = jnp.where(_sb_mask(j, i), beta, 0.0)
                        dz = (de * one_minus_beta - beta * (run + _dot_exact(de, lower))).astype(BF16)
                        dk_ref[rows, lanes] += _dot(dz, qs[hh], TN) * ATT_SCALE
                        dv_ref[rows, lanes] += _dot(a.astype(BF16), dobs[hh], TN)
                        carry[hh] = (dq + _dot(dz, kk), run + jnp.sum(de, axis=-1, keepdims=True))
                return tuple(carry)
            return grads

        zero = (jnp.zeros((SB_QROWS, HEAD_DIM), F32), jnp.zeros((SB_QROWS, 1), F32))
        for lanes, (dq, _) in zip(SB_LANES, _sb_peeled(nsteps, make_grads, (zero,) * SB_HEADS_PER_STEP, False)):
            dq_ref[:, lanes] = dq * ATT_SCALE

    width = SB_HEADS_PER_STEP * HEAD_DIM
    qb, kb, vb = (off // width for off in (OFF_QB, OFF_KB, OFF_VB))
    blk = pl.BlockSpec((SB_QROWS, width), lambda h, i: (i, h))
    full = pl.BlockSpec((s, width), lambda h, i: (0, h))
    shp = jax.ShapeDtypeStruct((s, B_W), F32)
    saved = pltpu.VMEM((SB_HEADS_PER_STEP, nkb, SB_QROWS, BLOCK), F32)
    return _pcall(body, name=name, out_shape=(shp, shp, shp), grid=(SB_HEADS // SB_HEADS_PER_STEP, s // SB_QROWS),
                  in_specs=[pl.BlockSpec((SB_QROWS, width), lambda h, i: (i, qb + h)),
                            pl.BlockSpec((s, width), lambda h, i: (0, kb + h)),
                            pl.BlockSpec((s, width), lambda h, i: (0, vb + h)), blk],
                  out_specs=(blk, full, full), scratch=[saved, saved])(proj, proj, proj, do_b)


def _coords():
    return lax.axis_index("x"), lax.axis_index("y"), lax.axis_index("c")


def _flip(v, bit):
    return 1 - v if bit else v


def _shard_of(ref, axis, idx, size):
    if axis == 0:
        sl = pl.ds(pl.multiple_of(idx * size, 16), size)
        return ref.at[sl, :] if len(ref.shape) == 2 else ref.at[:, sl, :]
    sl = pl.ds(pl.multiple_of(idx * size, 128), size)
    return ref.at[:, sl] if len(ref.shape) == 2 else ref.at[:, :, sl]


def _small_allgather(v, *, name, silu=False):
    n = v.shape[1]

    def body(v_ref, out_ref, send_sems, recv_sems):
        x, y, c = _coords()
        me = 4 * x + 2 * y + c
        val = v_ref[...]
        out_ref[me] = val * jax.nn.sigmoid(val) if silu else val
        copies = []
        for k in range(1, N_DEV):
            peer = (_flip(x, k & 4), _flip(y, k & 2), _flip(c, k & 1))
            copies.append(pltpu.make_async_remote_copy(
                src_ref=out_ref.at[me], dst_ref=out_ref.at[me], send_sem=send_sems.at[k - 1],
                recv_sem=recv_sems.at[k - 1], device_id=peer, device_id_type=MESH))
        for cp in copies:
            cp.start()
        for cp in copies:
            cp.wait_recv()
        for cp in copies:
            cp.wait_send()

    return _pcall(body, name=name, out_shape=jax.ShapeDtypeStruct((N_DEV, 1, n), F32),
                  in_specs=[pl.BlockSpec(memory_space=pltpu.VMEM)], out_specs=pl.BlockSpec(memory_space=pltpu.VMEM),
                  scratch=[pltpu.SemaphoreType.DMA((N_DEV - 1,)), pltpu.SemaphoreType.DMA((N_DEV - 1,))])(v)


def _cast_place(w, layer, axis, me, *, name):
    _, r, c = w.shape
    tr = _rows(r, c)
    nrt = r // tr

    def body(me_ref, w_ref, o_ref):
        o_ref[...] = w_ref[...].astype(BF16)

    wspec = pl.BlockSpec((None, tr, c), lambda i, me_ref: (layer, i, 0))
    if axis == 0:
        ospec = pl.BlockSpec((tr, c), lambda i, me_ref: (me_ref[0] * nrt + i, 0))
        shape = (r * N_DEV, c)
    else:
        ospec = pl.BlockSpec((tr, c), lambda i, me_ref: (i, me_ref[0]))
        shape = (r, c * N_DEV)
    return _pcall(body, name=name, out_shape=jax.ShapeDtypeStruct(shape, BF16), grid=(nrt,), in_specs=[wspec],
                  out_specs=ospec, prefetch=1)(me, w)


def _pair_sum(grad, sib, core, axis, *, name):
    _, r, c = sib.shape
    tr = _rows(r, c // 2)
    nrt = r // tr

    def body(core_ref, g_ref, s_ref, o_ref):
        o_ref[...] = (g_ref[...].astype(F32) + s_ref[...].astype(F32)).astype(BF16)

    if axis == 0:
        gspec = pl.BlockSpec((tr, c), lambda q, i, core_ref: ((2 * q + core_ref[0]) * nrt + i, 0))
    else:
        gspec = pl.BlockSpec((tr, c), lambda q, i, core_ref: (i, 2 * q + core_ref[0]))
    sspec = pl.BlockSpec((None, tr, c), lambda q, i, core_ref: (q, i, 0))
    return _pcall(body, name=name, out_shape=jax.ShapeDtypeStruct(sib.shape, BF16), grid=(N_CHIPS, nrt),
                  in_specs=[gspec, sspec], out_specs=sspec, prefetch=1)(core, grad, sib)


ANY_SPEC = pl.BlockSpec(memory_space=pl.ANY)
SEM_SPEC = pl.BlockSpec(memory_space=pltpu.SEMAPHORE)
SPLIT_PARAMS = dict(has_side_effects=pltpu.SideEffectType.DATAFLOW_SIDE_EFFECTING)


def _split_start(copies_fn, buffers, sem_shape, after, *, name):
    n = len(buffers)
    rows, cols = sem_shape
    ns = rows * cols
    extra = ([] if after is None else [after]) + _take_token()

    def body(*refs):
        sems = refs[n + len(extra):n + len(extra) + 2 * ns]
        for cp in copies_fn(refs[:n], _sem_rows(sems[:ns], cols), _sem_rows(sems[ns:], cols)):
            cp.start()
        refs[-1][...] = jnp.zeros_like(refs[-1])

    sem = pltpu.SemaphoreType.DMA(())
    outs = pl.pallas_call(
        body, name=name,
        out_shape=((sem,) * (2 * ns) + tuple(jax.ShapeDtypeStruct(b.shape, b.dtype) for b in buffers) + (TOKEN,)),
        in_specs=(ANY_SPEC,) * (n + len(extra)),
        out_specs=(SEM_SPEC,) * (2 * ns) + (ANY_SPEC,) * n + (pl.BlockSpec(memory_space=pltpu.VMEM),),
        input_output_aliases={i: 2 * ns + i for i in range(n)},
        compiler_params=pltpu.CompilerParams(**SPLIT_PARAMS))(*buffers, *extra)
    _ORDER["token"] = outs[-1]
    return list(outs[:ns]), list(outs[ns:2 * ns]), list(outs[2 * ns:2 * ns + n]), outs[-1]


def _split_wait(copies_fn, send_sems, recv_sems, buffers, after, sem_rows, *, name):
    n, ns = len(buffers), len(send_sems)
    cols = ns // sem_rows
    extra = ([] if after is None else [after]) + _take_token()

    def body(*refs):
        sems = refs[n:n + 2 * ns]
        copies = copies_fn(refs[:n], _sem_rows(sems[:ns], cols), _sem_rows(sems[ns:], cols))
        for cp in copies:
            cp.wait_send()
        for cp in copies:
            cp.wait_recv()
        refs[-1][...] = jnp.zeros_like(refs[-1])

    outs = pl.pallas_call(
        body, name=name, out_shape=tuple(jax.ShapeDtypeStruct(b.shape, b.dtype) for b in buffers) + (TOKEN,),
        in_specs=(ANY_SPEC,) * n + (SEM_SPEC,) * (2 * ns) + (ANY_SPEC,) * len(extra),
        out_specs=(ANY_SPEC,) * n + (pl.BlockSpec(memory_space=pltpu.VMEM),),
        input_output_aliases={i: i for i in range(n)},
        compiler_params=pltpu.CompilerParams(**SPLIT_PARAMS))(*buffers, *send_sems, *recv_sems, *extra)
    _ORDER["token"] = outs[-1]
    return list(outs[:n])


def _sem_rows(sems, cols):
    return [sems[i:i + cols] for i in range(0, len(sems), cols)]


def _empty_hbm(shape, dtype):
    return pltpu.with_memory_space_constraint(lax.empty(shape, dtype), pltpu.HBM)


class _SplitGather:
    def __init__(self, fulls, axes, tag):
        self.axes, self.tag, self.nt = list(axes), tag, len(fulls)
        self.sizes = [f.shape[ax] // N_DEV for f, ax in zip(fulls, axes)]
        self.fulls = list(fulls)

    def _slot(self, ref, t, dev):
        return _shard_of(ref, self.axes[t], 4 * dev[0] + 2 * dev[1] + dev[2], self.sizes[t])

    def _first_copies(self, refs, send_sems, recv_sems):
        x, y, c = _coords()
        peers = [(x, y, 1 - c), (1 - x, y, c), (x, 1 - y, c), (1 - x, 1 - y, c)]
        return [pltpu.make_async_remote_copy(
            src_ref=self._slot(refs[t], t, (x, y, c)), dst_ref=self._slot(refs[t], t, (x, y, c)),
            send_sem=send_sems[t][k], recv_sem=recv_sems[t][k], device_id=peer, device_id_type=MESH)
            for t in range(self.nt) for k, peer in enumerate(peers)]

    def _forward_copies(self, refs, send_sems, recv_sems):
        x, y, c = _coords()
        chips = [(1 - x, y), (x, 1 - y), (1 - x, 1 - y)]
        return [pltpu.make_async_remote_copy(
            src_ref=self._slot(refs[t], t, (*chip, c)), dst_ref=self._slot(refs[t], t, (*chip, c)),
            send_sem=send_sems[t][j], recv_sem=recv_sems[t][j], device_id=(x, y, 1 - c), device_id_type=MESH)
            for t in range(self.nt) for j, chip in enumerate(chips)]

    def first(self, after):
        self.s1, self.r1, self.fulls, token = _split_start(
            self._first_copies, self.fulls, (self.nt, 4), after, name=f"comm_gather1_start_{self.tag}")
        return token

    def forward(self, after):
        bufs = _split_wait(self._first_copies, self.s1, self.r1, self.fulls, after, self.nt,
                           name=f"comm_gather1_wait_{self.tag}")
        self.s2, self.r2, self.fulls, token = _split_start(
            self._forward_copies, bufs, (self.nt, 3), after, name=f"comm_gather2_start_{self.tag}")
        return token

    def finish(self, after):
        return _split_wait(self._forward_copies, self.s2, self.r2, self.fulls, after, self.nt,
                           name=f"comm_gather2_wait_{self.tag}")


class _SplitPairExchange:
    def __init__(self, grads, axes, tag):
        self.nt, self.tag, self.axes = len(grads), tag, list(axes)
        self.grads = list(grads)
        self.sizes = [g.shape[ax] // N_DEV for g, ax in zip(grads, axes)]

    def _copies(self, refs, send_sems, recv_sems):
        nt = self.nt
        x, y, c = _coords()
        return [pltpu.make_async_remote_copy(
            src_ref=_shard_of(refs[t], self.axes[t], 2 * q + 1 - c, self.sizes[t]), dst_ref=refs[nt + t].at[q],
            send_sem=send_sems[t][q], recv_sem=recv_sems[t][q], device_id=(x, y, 1 - c), device_id_type=MESH)
            for t in range(nt) for q in range(N_CHIPS)]

    def start(self):
        landing = []
        for g, ax in zip(self.grads, self.axes):
            dims = list(g.shape)
            dims[ax] //= N_DEV
            landing.append(_empty_hbm((N_CHIPS, *dims), g.dtype))
        self.s, self.r, self.bufs, token = _split_start(
            self._copies, self.grads + landing, (self.nt, N_CHIPS), None,
            name=f"comm_rs_pair_start_{self.tag}")
        return token

    def finish(self, after):
        bufs = _split_wait(self._copies, self.s, self.r, self.bufs, after, self.nt,
                           name=f"comm_rs_pair_wait_{self.tag}")
        return bufs[:self.nt], bufs[self.nt:]


class _ReducePipeline:
    def __init__(self, core):
        self.core, self.items, self.done, self.now = core, [], [], 0

    def add(self, keys, grads, layer):
        axes = [SHARD_AXIS[k] for k in keys]
        pair = _SplitPairExchange([grads[k] for k in keys], axes, f"{keys[0]}{layer}")
        pair.start()
        self.items.append(dict(keys=keys, layer=layer, axes=axes, pair=pair, state="pair", since=self.now))

    def tick(self, after, flush=False):
        self.now += 1
        for it in self.items:
            if it["state"] == "pair" and it["since"] < self.now:
                grads, sib = it["pair"].finish(after)
                sums = [_pair_sum(g, s_, self.core, ax, name="pair_sum_" + k)
                        for k, g, s_, ax in zip(it["keys"], grads, sib, it["axes"])]
                it["chip"] = _SplitChipExchange(sums, f"{it['keys'][0]}{it['layer']}")
                it["chip"].start()
                it.update(state="chip", since=self.now)
            elif it["state"] == "chip" and (flush or self.now - it["since"] >= 2):
                sums, remote = it["chip"].finish(after)
                self.done.append((it["keys"], it["layer"], sums, remote))
                it["state"] = "done"

    def take_done(self):
        out, self.done = self.done, []
        return out


class _SplitChipExchange:
    def __init__(self, sums, tag):
        self.nt, self.tag = len(sums), tag
        self.sums = list(sums)

    def _copies(self, refs, send_sems, recv_sems):
        nt = self.nt
        x, y, c = _coords()
        copies = []
        for t in range(nt):
            for k in range(1, N_CHIPS):
                px, py = _flip(x, k & 2), _flip(y, k & 1)
                copies.append(pltpu.make_async_remote_copy(
                    src_ref=refs[t].at[2 * px + py], dst_ref=refs[nt + t].at[k - 1], send_sem=send_sems[t][k - 1],
                    recv_sem=recv_sems[t][k - 1], device_id=(px, py, c), device_id_type=MESH))
        return copies

    def start(self):
        landing = [_empty_hbm((N_CHIPS - 1,) + s.shape[1:], s.dtype) for s in self.sums]
        self.s, self.r, self.bufs, token = _split_start(
            self._copies, self.sums + landing, (self.nt, N_CHIPS - 1), None,
            name=f"comm_rs_chip_start_{self.tag}")
        return token

    def finish(self, after):
        bufs = _split_wait(self._copies, self.s, self.r, self.bufs, after, self.nt,
                           name=f"comm_rs_chip_wait_{self.tag}")
        return bufs[:self.nt], bufs[self.nt:]


def _adam_math(g, w, m, v):
    m2 = ADAM_B1 * m + (1.0 - ADAM_B1) * g
    v2 = ADAM_B2 * v + (1.0 - ADAM_B2) * (g * g)
    m_hat = m2 / (1.0 - ADAM_B1 ** ADAM_STEP)
    v_hat = v2 / (1.0 - ADAM_B2 ** ADAM_STEP)
    delta = -ADAM_LR * (m_hat / (jnp.sqrt(v_hat) + ADAM_EPS) + ADAM_WD * w)
    return delta, m2, v2


def _adamw_sharded(chip_sums, remote, chip, w, m, v, layer, prev, *, name):
    nl, r, c = w.shape
    tr = _rows(r, c)

    def body(*refs):
        p_ref, r0_ref, r1_ref, r2_ref, w_ref, m_ref, v_ref = refs[1:8]
        g_out, d_out, m_out, v_out = refs[-4:]
        g = ((p_ref[...].astype(F32) + r0_ref[...].astype(F32)) + r1_ref[...].astype(F32)) + r2_ref[...].astype(F32)
        g_out[...] = g
        d_out[...], m_out[...], v_out[...] = _adam_math(g, w_ref[...], m_ref[...], v_ref[...])

    pspec = pl.BlockSpec((None, tr, c), lambda i, chip_ref: (chip_ref[0], i, 0))

    def rspec(k):
        return pl.BlockSpec((None, tr, c), lambda i, chip_ref: (k, i, 0))

    wspec = pl.BlockSpec((None, tr, c), lambda i, chip_ref: (layer, i, 0))
    in_specs = [pspec, rspec(0), rspec(1), rspec(2), wspec, wspec, wspec]
    args = [chip, chip_sums, remote, remote, remote, w, m, v]
    aliases = {}
    if prev is not None:
        in_specs += [pl.BlockSpec(memory_space=pl.ANY)] * 4
        aliases = {len(args) + i: i for i in range(4)}
        args += list(prev)
    shp = jax.ShapeDtypeStruct(w.shape, F32)
    return _pcall(body, name=name, out_shape=(shp,) * 4, grid=(r // tr,), in_specs=in_specs, out_specs=(wspec,) * 4,
                  aliases=aliases, prefetch=1)(*args)


def _adamw_local(g, w, m, v, *, name):
    nl, r, c = w.shape
    tr = _rows(r, c)

    def body(g_ref, w_ref, m_ref, v_ref, d_out, m_out, v_out):
        d_out[...], m_out[...], v_out[...] = _adam_math(g_ref[...], w_ref[...], m_ref[...], v_ref[...])

    spec = pl.BlockSpec((None, tr, c), lambda l, i: (l, i, 0))
    shp = jax.ShapeDtypeStruct(w.shape, F32)
    return _pcall(body, name=name, out_shape=(shp,) * 3, grid=(nl, r // tr), in_specs=[spec] * 4,
                  out_specs=(spec,) * 3)(g, w, m, v)


def _adamw_replicated(parts, w, m, v, *, name):
    n = w.shape[1]

    def body(p_ref, w_ref, m_ref, v_ref, g_out, d_out, m_out, v_out):
        g = p_ref[0]
        for k in range(1, N_DEV):
            g = g + p_ref[k]
        g_out[...] = g
        d_out[...], m_out[...], v_out[...] = _adam_math(g, w_ref[...], m_ref[...], v_ref[...])

    vm = pl.BlockSpec(memory_space=pltpu.VMEM)
    shp = jax.ShapeDtypeStruct((1, n), F32)
    return _pcall(body, name=name, out_shape=(shp,) * 4, in_specs=[vm] * 4, out_specs=(vm,) * 4)(parts, w, m, v)


def _group_views(qk, proj, g, dil, seq):
    if dil == 1:
        return (qk, qk, proj), (0, A_HEADS, 2 * A_HEADS)
    length = seq // dil
    lo = g * GROUP_W
    q = qk[:, lo:lo + GROUP_W].reshape(length, dil * GROUP_W)
    k = qk[:, A_W + lo:A_W + lo + GROUP_W].reshape(length, dil * GROUP_W)
    v = proj[:, OFF_VA + lo:OFF_VA + lo + GROUP_W].astype(BF16).reshape(length, dil * GROUP_W)
    return (q, k, v), (0, 0, 0)


def _mod_rows(mod, d):
    return [mod[:, i * d:(i + 1) * d] for i in range(6)]


MIXER_W = ("w_in", "w_branch_a", "w_branch_b", "w_out")
FFN_W = ("w_gate_up", "w_down")
SHARD_AXIS = {"w_in": 1, "w_branch_a": 1, "w_branch_b": 1, "w_out": 0, "w_gate_up": 1, "w_down": 0}


def _norm_args(mod, gain, which, d):
    rows = _mod_rows(mod, d)
    return gain, rows[3 * which + 1], rows[3 * which]


def _mixer_fwd_a(h, u, gains, w_in, cos2, sin2, hook):
    seq = h.shape[0]
    proj = _mm(u, w_in, name="mm_in")
    hook(proj)
    qk = _qkrope_fwd(proj, gains, cos2, sin2, name="qkrope_fwd")
    os_, lses, views = [], [], []
    for g, dil in enumerate(DILATIONS):
        arrs, offs = _group_views(qk, proj, g, dil, seq)
        o, lse = _dil_fwd(*arrs, offs, seq // dil, dil, name=f"dil_fwd_{dil}")
        views.append((arrs, offs, o, lse))
        os_.append(o.reshape(seq, GROUP_W))
        lses.append(lse.reshape(seq, GROUP_W))
    o_a = _combine_fwd(os_, lses, name="combine_fwd")
    o_b = _sb_fwd(proj, name="sb_fwd")
    return dict(h_in=h, u=u, proj=proj, views=views, os=os_, lses=lses, o_a=o_a, o_b=o_b)


def _mixer_fwd_b(sv, mod, g2, wts):
    d = sv["h_in"].shape[1]
    merged, y_a, y_b = _mm_merge(sv["o_a"], sv["o_b"], wts["w_branch_a"], wts["w_branch_b"], sv["proj"],
                                 name="mm_branch")
    h_mid, t, u2 = _mm_resid_norm(merged, wts["w_out"], sv["h_in"], _mod_rows(mod, d)[2], _norm_args(mod, g2, 1, d),
                                  name="mm_out")
    sv.update(y_a=y_a, y_b=y_b, merged=merged, t=t, h_mid=h_mid, u2=u2)
    return h_mid


def _ffn_fwd_a(sv, w_gate_up):
    a, g, u = _mm_swiglu(sv["u2"], w_gate_up, name="mm_gate_up")
    sv.update(g=g, up=u, a=a)
    return a


def _ffn_fwd_b(sv, mod, w_down, next_norm):
    d = sv["h_mid"].shape[1]
    h_out, sv["f"], u_next = _mm_resid_norm(sv["a"], w_down, sv["h_mid"], _mod_rows(mod, d)[5], next_norm,
                                            name="mm_down")
    return h_out, u_next


def _wgrad(act, dout, key):
    return _mm(act, dout, ta=True, out_dtype=BF16, name="mm_wgrad_" + key)


def _ffn_bwd(dh, df, dgate2, sv, mod, g2, wts, hook):
    d = dh.shape[1]
    sc2, ga1 = _mod_rows(mod, d)[4], _mod_rows(mod, d)[2]
    dg, dup = _mm_down_t_swiglu(df, wts["w_down"], sv["g"], sv["up"], name="mm_down_t")
    grads = {"w_down": _wgrad(sv["a"], df, "w_down")}
    hook(dup)
    du2 = _mm_cat_k(dg, dup, wts["w_gate_up"], name="mm_gate_up_t")
    grads["w_gate_up"] = _mm_cat_n(sv["u2"], dg, dup, name="mm_wgrad_w_gate_up")
    dh_mid, dsh2, dsc2, dg2, dt, dgate1 = _rmsmod_bwd(du2, sv["h_mid"], g2, sc2, dh, sv["t"], ga1, name="rmsmod_bwd")
    return dh_mid, [dsh2, dsc2, dgate2], dg2, grads, dt, dgate1


def _mixer_bwd(dh_mid, dt, dgate1, sv, mod, g1, gains, wts, cos2, sin2, hook, below):
    seq, d = dh_mid.shape
    sc1 = _mod_rows(mod, d)[1]
    dmerged = _mm(dt, wts["w_out"], tb=True, name="mm_out_t")
    grads = {"w_out": _wgrad(sv["merged"], dt, "w_out")}
    dy_a, dy_b, dga, dgb = _merge_bwd(dmerged, sv["proj"], sv["y_a"], sv["y_b"], name="merge_bwd")
    do_a = _mm(dy_a, wts["w_branch_a"], tb=True, name="mm_branch_t")
    do_b = _mm(dy_b, wts["w_branch_b"], tb=True, name="mm_branch_t")
    grads["w_branch_a"] = _wgrad(sv["o_a"], dy_a, "w_branch_a")
    grads["w_branch_b"] = _wgrad(sv["o_b"], dy_b, "w_branch_b")
    dqb, dkb, dvb = _sb_bwd(sv["proj"], do_b, name="sb_bwd")
    hook(dqb, grads)
    comb = _combine_bwd(do_a, sv["os"], sv["lses"], name="combine_bwd")
    grads = {}
    dos, dls = comb[:3], comb[3:]
    dqs, dks, dvs = [], [], []
    for g, dil in enumerate(DILATIONS):
        length = seq // dil
        arrs, offs, o_view, lse_view = sv["views"][g]
        view = (length, dil * GROUP_W)
        dq, dk, dv = _dil_bwd(*arrs, offs, o_view, lse_view, dos[g].reshape(view), dls[g].reshape(view), length, dil,
                              name=f"dil_bwd_{dil}")
        dqs.append(dq.reshape(seq, GROUP_W))
        dks.append(dk.reshape(seq, GROUP_W))
        dvs.append(dv.reshape(seq, GROUP_W))
    dq_pre, dqn = _qkrope_bwd(dqs, sv["proj"], gains, 0, cos2, sin2, name="qkrope_bwd")
    dk_pre, dkn = _qkrope_bwd(dks, sv["proj"], gains, 1, cos2, sin2, name="qkrope_bwd")
    dgains = jnp.stack([dqn, dkn])
    dproj = _assemble([dq_pre, dk_pre] + dvs + [dqb, dkb, dvb, dga, dgb], name="assemble_dproj")
    du = _mm(dproj, wts["w_in"], tb=True, name="mm_in_t")
    grads["w_in"] = _wgrad(sv["u"], dproj, "w_in")
    dh_in, dsh1, dsc1, dg1, df, dgate2 = _rmsmod_bwd(du, sv["h_in"], g1, sc1, dh_mid, *(below or (None, None)),
                                                     name="rmsmod_bwd")
    return dh_in, [dsh1, dsc1, dgate1], dg1, dgains, grads, df, dgate2


def kernel(x, c, w_ada, b_ada, norm1_g, norm2_g, w_in, qn_g, kn_g, w_branch_a, w_branch_b, w_out, w_gate_up, w_down, loss_target, m_w_ada, m_b_ada, m_norm1_g, m_norm2_g, m_w_in, m_qn_g, m_kn_g, m_w_branch_a, m_w_branch_b, m_w_out, m_w_gate_up, m_w_down, v_w_ada, v_b_ada, v_norm1_g, v_norm2_g, v_w_in, v_qn_g, v_kn_g, v_w_branch_a, v_w_branch_b, v_w_out, v_w_gate_up, v_w_down):
    _ORDER["token"] = None
    seq, d = x.shape[1], x.shape[2]
    depth = w_in.shape[0]
    weights = dict(w_in=w_in, w_branch_a=w_branch_a, w_branch_b=w_branch_b, w_out=w_out, w_gate_up=w_gate_up,
                   w_down=w_down)
    moments_m = dict(w_in=m_w_in, w_branch_a=m_w_branch_a, w_branch_b=m_w_branch_b, w_out=m_w_out,
                     w_gate_up=m_w_gate_up, w_down=m_w_down)
    moments_v = dict(w_in=v_w_in, w_branch_a=v_w_branch_a, w_branch_b=v_w_branch_b, w_out=v_w_out,
                     w_gate_up=v_w_gate_up, w_down=v_w_down)
    xi, yi, ci = _coords()
    me = 4 * xi + 2 * yi + ci
    core = jnp.reshape(ci, (1,)).astype(jnp.int32)
    chip = jnp.reshape(2 * xi + yi, (1,)).astype(jnp.int32)

    ada_w = w_ada.shape[2]
    c_act = _small_allgather(c, name="comm_gather_c", silu=True).reshape(N_DEV, d)
    c_pad = jnp.concatenate([c_act, jnp.zeros_like(c_act)], axis=0).astype(BF16)
    bias = lax.dynamic_slice(b_ada, (0, me * ada_w), (depth, ada_w))
    mod_part = jnp.stack([_mm(c_pad, w_ada[l], name="mm_ada")[:N_DEV] for l in range(depth)]) + bias[:, None, :]
    mod_all = _small_allgather(mod_part.reshape(1, depth * N_DEV * ada_w), name="comm_gather_mod")
    mod_all = mod_all.reshape(N_DEV, depth, N_DEV, ada_w)
    mod_mine = lax.dynamic_index_in_dim(mod_all, me, axis=2, keepdims=False)
    mods = jnp.transpose(mod_mine, (1, 0, 2)).reshape(depth, 1, 6 * d)

    cos2, sin2 = _rope_tables(seq)
    gains = [jnp.stack([qn_g[l], kn_g[l]])[:, None, :] for l in range(depth)]
    g1s = [norm1_g[l][None] for l in range(depth)]
    g2s = [norm2_g[l][None] for l in range(depth)]

    me_arr = jnp.reshape(me, (1,)).astype(jnp.int32)

    def placed(keys, l):
        return [_cast_place(weights[k], l, SHARD_AXIS[k], me_arr, name="cast_place_" + k) for k in keys]

    def gather_of(keys, l, tag):
        return _SplitGather(placed(keys, l), [SHARD_AXIS[k] for k in keys], f"{tag}{l}")

    groups = [("w_in", 0, MIXER_W[:1]), ("rest", 0, MIXER_W[1:]), ("ffn", 0, FFN_W)]
    for l in range(1, depth):
        groups += [("mixer", l, MIXER_W), ("ffn", l, FFN_W)]
    gathers = {}

    def issue(some):
        for tag, l, keys in some:
            gathers[tag, l] = gather_of(keys, l, tag)
            gathers[tag, l].first(after=mods)

    issue(groups[:3])
    h = x[0]
    u = _rmsmod_fwd(h, *_norm_args(mods[0], g1s[0], 0, d), name="rmsmod_fwd")
    gathers["w_in", 0].forward(after=u)
    issue(groups[3:])
    wm = {"w_in": gathers["w_in", 0].finish(after=u)[0]}
    saved, full = [], []
    for l in range(depth):
        last = l + 1 == depth
        sv = _mixer_fwd_a(h, u, gains[l], wm["w_in"], cos2, sin2,
                          gathers["rest", 0].forward if l == 0 else lambda after: None)
        gathers["ffn", l].forward(after=sv["o_b"])
        if l == 0:
            wm.update(zip(MIXER_W[1:], gathers["rest", 0].finish(after=sv["o_b"])))
        h_mid = _mixer_fwd_b(sv, mods[l], g2s[l], wm)
        wf = dict(zip(FFN_W, gathers["ffn", l].finish(after=h_mid)))
        a = _ffn_fwd_a(sv, wf["w_gate_up"])
        if not last:
            gathers["mixer", l + 1].forward(after=a)
        h, u = _ffn_fwd_b(sv, mods[l], wf["w_down"],
                          None if last else _norm_args(mods[l + 1], g1s[l + 1], 0, d))
        saved.append(sv)
        full.append({**wm, **wf})
        if not last:
            wm = dict(zip(MIXER_W, gathers["mixer", l + 1].finish(after=h)))
    def ffn_gate(l):
        return saved[l]["f"], _mod_rows(mods[l], d)[5]

    loss_part, dh, df, dgate2 = _loss_fwd(h, loss_target[0], *ffn_gate(depth - 1), name="loss")
    loss = lax.psum(loss_part[0, 0], ("x", "y", "c"))

    pipe = _ReducePipeline(core)
    dmods, dg1s, dg2s, dgains = [None] * depth, [None] * depth, [None] * depth, [None] * depth
    for l in reversed(range(depth)):
        dh_mid, dmod_f, dg2s[l], grads, dt, dgate1 = _ffn_bwd(dh, df, dgate2, saved[l], mods[l], g2s[l], full[l],
                                                              pipe.tick)
        pipe.tick(dh_mid)
        pipe.add(FFN_W, grads, l)
        dh, dmod_m, dg1s[l], dgains[l], grads, df, dgate2 = _mixer_bwd(
            dh_mid, dt, dgate1, saved[l], mods[l], g1s[l], gains[l], full[l], cos2, sin2,
            lambda after, early, l=l: (pipe.tick(after), pipe.add(MIXER_W[1:], early, l)),
            ffn_gate(l - 1) if l > 0 else None)
        dmods[l] = jnp.concatenate(dmod_m + dmod_f, axis=1)
        pipe.tick(dh)
        pipe.add(MIXER_W[:1], grads, l)
    grad_x = dh[None]

    stacked = {}

    def update(items):
        for keys, l, sums, remote in items:
            for k, p_, r_ in zip(keys, sums, remote):
                stacked[k] = _adamw_sharded(p_, r_, chip, weights[k], moments_m[k], moments_v[k], l,
                                            stacked.get(k), name="adamw_" + k)

    ready = pipe.take_done()
    update([it for it in ready if it[0] != FFN_W])

    small = jnp.concatenate(
        dmods + dg1s + dg2s + [dgains[l][0] for l in range(depth)] + [dgains[l][1] for l in range(depth)], axis=1)
    small_all = _small_allgather(small, name="comm_gather_small")
    pipe.tick(small_all)
    update([it for it in ready if it[0] == FFN_W] + pipe.take_done())

    def pack(b, n1, n2, qn, kn):
        return jnp.concatenate([t_.reshape(1, -1) for t_ in (b, n1, n2, qn, kn)], axis=1)

    sg, sd, sm, sv_ = _adamw_replicated(small_all, pack(b_ada, norm1_g, norm2_g, qn_g, kn_g),
                                        pack(m_b_ada, m_norm1_g, m_norm2_g, m_qn_g, m_kn_g),
                                        pack(v_b_ada, v_norm1_g, v_norm2_g, v_qn_g, v_kn_g), name="adamw_replicated")

    def unpack(p):
        sizes = [depth * 6 * d, depth * d, depth * d, depth * HEAD_DIM, depth * HEAD_DIM]
        shapes = [b_ada.shape, norm1_g.shape, norm2_g.shape, qn_g.shape, kn_g.shape]
        out, off = [], 0
        for n, shp in zip(sizes, shapes):
            out.append(p[0, off:off + n].reshape(shp))
            off += n
        return dict(zip(("b_ada", "norm1_g", "norm2_g", "qn_g", "kn_g"), out))

    ug, ud, um, uv = unpack(sg), unpack(sd), unpack(sm), unpack(sv_)
    res = {k: dict(g=ug[k], d=ud[k], m=um[k], v=uv[k]) for k in ug}

    dmod_all = small_all[:, 0, :depth * 6 * d].reshape(N_DEV, depth, 6 * d)
    g_ada = None
    for l in range(depth):
        dm = lax.dynamic_slice(dmod_all[:, l, :], (0, me * ada_w), (N_DEV, ada_w))
        dm = jnp.concatenate([dm, jnp.zeros_like(dm)], axis=0).astype(BF16)
        g_ada = _mm(c_pad, dm, ta=True, name="mm_wgrad_ada", stack=(l, depth, g_ada))
    d_ada, m_ada, v_ada = _adamw_local(g_ada, w_ada, m_w_ada, v_w_ada, name="adamw_local")
    res["w_ada"] = dict(g=g_ada, d=d_ada, m=m_ada, v=v_ada)

    pipe.tick(d_ada)
    update(pipe.take_done())
    pipe.tick(d_ada, flush=True)
    update(pipe.take_done())
    for k, (g_, d_, m_, v_) in stacked.items():
        res[k] = dict(g=g_, d=d_, m=m_, v=v_)

    order = ("w_ada", "b_ada", "norm1_g", "norm2_g", "w_in", "qn_g", "kn_g", "w_branch_a", "w_branch_b", "w_out",
             "w_gate_up", "w_down")
    _ORDER["token"] = None
    return (loss, grad_x, *[res[k]["g"] for k in order], *[res[k]["d"] for k in order],
            *[res[k]["m"] for k in order], *[res[k]["v"] for k in order])
```

```python
import functools

import jax
import jax.numpy as jnp
from jax import lax
from jax.experimental import pallas as pl
from jax.experimental.pallas import tpu as pltpu

F32 = jnp.float32
BF16 = jnp.bfloat16

HEAD_DIM = 128
BLOCK = 128
DILATIONS = (1, 4, 16)
HEADS_PER_GROUP = 4
A_HEADS = 12
SB_HEADS = 4
GROUP_W = HEADS_PER_GROUP * HEAD_DIM
A_W = A_HEADS * HEAD_DIM
B_W = SB_HEADS * HEAD_DIM
OFF_QA, OFF_KA, OFF_VA = 0, A_W, 2 * A_W
OFF_QB, OFF_KB, OFF_VB = 3 * A_W, 3 * A_W + B_W, 3 * A_W + 2 * B_W
OFF_GATES = 3 * A_W + 3 * B_W
ROPE_THETA = 10000.0
EPS = 1e-6
ATT_SCALE = HEAD_DIM ** -0.5
MASKED = -1e30

ADAM_LR, ADAM_B1, ADAM_B2, ADAM_EPS, ADAM_WD, ADAM_STEP = 0.001, 0.9, 0.999, 1e-08, 0.01, 10

N_DEV = 8
N_CHIPS = 4
V7X_VMEM_LIMIT_BYTES = 56 * 1024 * 1024
ELEMWISE_BLOCK_BYTES = 2 * 1024 * 1024
MESH = pl.DeviceIdType.MESH

NN = (((1,), (0,)), ((), ()))
NT = (((1,), (1,)), ((), ()))
TN = (((0,), (0,)), ((), ()))


def _dot(a, b, dims=NN):
    return lax.dot_general(a, b, dims, preferred_element_type=F32)


def _tile(n, cap, mult=128):
    best = None
    for t in range(mult, min(n, cap) + 1, mult):
        if n % t == 0:
            best = t
    if best is None:
        assert n <= 2 * cap, (n, cap)
        return n
    return best


def _rows(r, c):
    return _tile(r, max(16, ELEMWISE_BLOCK_BYTES // (4 * c)), 16)


_ORDER = {"token": None}
TOKEN = jax.ShapeDtypeStruct((8, 128), F32)


def _take_token():
    prev = _ORDER["token"]
    return [] if prev is None else [prev]


def _pcall(body, *, name, out_shape, grid=None, in_specs=None, out_specs=None, scratch=(), aliases=None,
           prefetch=0):
    single = not isinstance(out_shape, (tuple, list))
    out_shapes = [out_shape] if single else list(out_shape)
    out_specs = [out_specs] if single else list(out_specs)
    extra = _take_token()
    n_in, n_extra, n_out = prefetch + len(in_specs), len(extra), len(out_shapes)

    def wrapped(*refs):
        token = refs[n_in + n_extra + n_out]
        token[...] = jnp.zeros_like(token)
        return body(*refs[:n_in], *refs[n_in + n_extra:n_in + n_extra + n_out], *refs[n_in + n_extra + n_out + 1:])

    in_specs = list(in_specs) + [pl.BlockSpec(memory_space=pl.ANY)] * n_extra
    if grid is None:
        out_specs.append(pl.BlockSpec(memory_space=pltpu.VMEM))
    else:
        out_specs.append(pl.BlockSpec(TOKEN.shape, lambda *_: (0, 0)))
    kwargs = dict(name=name, out_shape=out_shapes + [TOKEN], input_output_aliases=aliases or {},
                  compiler_params=pltpu.CompilerParams(vmem_limit_bytes=V7X_VMEM_LIMIT_BYTES))
    if prefetch:
        call = pl.pallas_call(wrapped, grid_spec=pltpu.PrefetchScalarGridSpec(
            num_scalar_prefetch=prefetch, grid=grid, in_specs=in_specs, out_specs=out_specs,
            scratch_shapes=list(scratch)), **kwargs)
    else:
        if grid is not None:
            kwargs["grid"] = grid
        call = pl.pallas_call(wrapped, in_specs=in_specs, out_specs=out_specs, scratch_shapes=list(scratch), **kwargs)

    def run(*args):
        outs = call(*args, *extra)
        _ORDER["token"] = outs[-1]
        return outs[0] if single else tuple(outs[:-1])

    return run


def _mm(a, b, *, name, ta=False, tb=False, out_dtype=F32, caps=(1024, 1024, 3072), stack=None):
    kdim, m = a.shape if ta else a.shape[::-1]
    n, k2 = b.shape if tb else b.shape[::-1]
    assert kdim == k2, (a.shape, b.shape, ta, tb)
    tm, tn, tk = _tile(m, caps[0]), _tile(n, caps[1]), _tile(kdim, caps[2])
    nk = kdim // tk
    dims = (((0 if ta else 1,), (1 if tb else 0,)), ((), ()))

    def body(*refs):
        a_ref, b_ref = refs[0], refs[1]
        part = _dot(a_ref[...].astype(BF16), b_ref[...].astype(BF16), dims)
        if nk == 1:
            o_ref = refs[-1]
            o_ref[...] = part.astype(o_ref.dtype)
            return
        o_ref, acc_ref = refs[-2], refs[-1]
        k = pl.program_id(2)

        @pl.when(k == 0)
        def _():
            acc_ref[...] = part

        @pl.when(k > 0)
        def _():
            acc_ref[...] += part

        @pl.when(k == nk - 1)
        def _():
            o_ref[...] = acc_ref[...].astype(o_ref.dtype)

    a_spec = (pl.BlockSpec((tk, tm), lambda i, j, k: (k, i)) if ta
              else pl.BlockSpec((tm, tk), lambda i, j, k: (i, k)))
    b_spec = (pl.BlockSpec((tn, tk), lambda i, j, k: (j, k)) if tb
              else pl.BlockSpec((tk, tn), lambda i, j, k: (k, j)))
    ins, in_specs, aliases = [a, b], [a_spec, b_spec], {}
    if stack is None:
        out_shape = jax.ShapeDtypeStruct((m, n), out_dtype)
        out_spec = pl.BlockSpec((tm, tn), lambda i, j, k: (i, j))
    else:
        layer, n_layers, buf = stack
        out_shape = jax.ShapeDtypeStruct((n_layers, m, n), out_dtype)
        out_spec = pl.BlockSpec((None, tm, tn), lambda i, j, k: (layer, i, j))
        if buf is not None:
            ins.append(buf)
            in_specs.append(pl.BlockSpec(memory_space=pl.ANY))
            aliases = {2: 0}
    scratch = [] if nk == 1 else [pltpu.VMEM((tm, tn), F32)]
    return _pcall(body, name=name, out_shape=out_shape, grid=(m // tm, n // tn, nk), in_specs=in_specs,
                  out_specs=out_spec, scratch=scratch, aliases=aliases)(*ins)


EPILOGUE_ROWS = 256


def _row_chunks(tm):
    return [slice(r, r + EPILOGUE_ROWS) for r in range(0, tm, EPILOGUE_ROWS)] if tm > EPILOGUE_ROWS else [slice(0, tm)]


def _mm_cat_k(a_lo, a_hi, b, *, name):
    m, f = a_lo.shape
    n = b.shape[0]
    tm, tn, tk = _tile(m, 1024), _tile(n, 1024), _tile(f, 3072)
    half = f // tk
    nk = 2 * half

    def body(lo_ref, hi_ref, b_ref, o_ref, acc_ref):
        k = pl.program_id(2)

        def accumulate(a_ref):
            part = _dot(a_ref[...], b_ref[...], NT)

            @pl.when(k == 0)
            def _():
                acc_ref[...] = part

            @pl.when(k > 0)
            def _():
                acc_ref[...] += part

        pl.when(k < half)(lambda: accumulate(lo_ref))
        pl.when(k >= half)(lambda: accumulate(hi_ref))

        @pl.when(k == nk - 1)
        def _():
            o_ref[...] = acc_ref[...]

    return _pcall(body, name=name, out_shape=jax.ShapeDtypeStruct((m, n), F32), grid=(m // tm, n // tn, nk),
                  in_specs=[pl.BlockSpec((tm, tk), lambda i, j, k: (i, jnp.minimum(k, half - 1))),
                            pl.BlockSpec((tm, tk), lambda i, j, k: (i, jnp.maximum(k - half, 0))),
                            pl.BlockSpec((tn, tk), lambda i, j, k: (j, k))],
                  out_specs=pl.BlockSpec((tm, tn), lambda i, j, k: (i, j)),
                  scratch=[pltpu.VMEM((tm, tn), F32)])(a_lo, a_hi, b)


def _mm_cat_n(a, b_lo, b_hi, *, name):
    s, m = a.shape
    f = b_lo.shape[1]
    tm, tn = _tile(m, 1024), _tile(f, 1024)
    half = f // tn

    def body(a_ref, lo_ref, hi_ref, o_ref):
        j = pl.program_id(1)

        @pl.when(j < half)
        def _():
            o_ref[...] = _dot(a_ref[...], lo_ref[...], TN).astype(BF16)

        @pl.when(j >= half)
        def _():
            o_ref[...] = _dot(a_ref[...], hi_ref[...], TN).astype(BF16)

    return _pcall(body, name=name, out_shape=jax.ShapeDtypeStruct((m, 2 * f), BF16), grid=(m // tm, 2 * half),
                  in_specs=[pl.BlockSpec((s, tm), lambda i, j: (0, i)),
                            pl.BlockSpec((s, tn), lambda i, j: (0, jnp.minimum(j, half - 1))),
                            pl.BlockSpec((s, tn), lambda i, j: (0, jnp.maximum(j - half, 0)))],
                  out_specs=pl.BlockSpec((tm, tn), lambda i, j: (i, j)))(a, b_lo, b_hi)


def _mm_resid_norm(a, w, h, gate, norm, *, name):
    s, kdim = a.shape
    d = w.shape[1]
    tk = _tile(kdim, 2048)
    nk = kdim // tk
    tm = _tile(s, 256 if nk == 1 else 512)

    def body(*refs):
        a_ref, w_ref, h_ref, gate_ref = refs[:4]
        outs = refs[7:] if norm is not None else refs[4:]

        def finish(rows, t):
            hn = h_ref[rows, :] + gate_ref[...] * t
            outs[0][rows, :] = hn
            outs[1][rows, :] = t.astype(BF16)
            if norm is not None:
                g_ref, sc_ref, sh_ref = refs[4:7]
                r = lax.rsqrt(jnp.mean(hn * hn, axis=-1, keepdims=True) + EPS)
                outs[2][rows, :] = (((hn * r) * g_ref[...]) * (1.0 + sc_ref[...]) + sh_ref[...]).astype(BF16)

        if nk == 1:
            for rows in _row_chunks(tm):
                finish(rows, _dot(a_ref[rows, :], w_ref[...]))
            return
        acc_ref = refs[-1]
        k = pl.program_id(1)

        @pl.when(k == 0)
        def _():
            acc_ref[...] = _dot(a_ref[...], w_ref[...])

        @pl.when(jnp.logical_and(k > 0, k < nk - 1))
        def _():
            acc_ref[...] += _dot(a_ref[...], w_ref[...])

        @pl.when(k == nk - 1)
        def _():
            for rows in _row_chunks(tm):
                finish(rows, acc_ref[rows, :] + _dot(a_ref[rows, :], w_ref[...]))

    row = pl.BlockSpec((tm, d), lambda i, k: (i, 0))
    vec = pl.BlockSpec((1, d), lambda i, k: (0, 0))
    in_specs = [pl.BlockSpec((tm, tk), lambda i, k: (i, k)), pl.BlockSpec((tk, d), lambda i, k: (k, 0)), row, vec]
    args = [a, w, h, gate]
    out_shape = [jax.ShapeDtypeStruct((s, d), F32), jax.ShapeDtypeStruct((s, d), BF16)]
    if norm is not None:
        in_specs += [vec, vec, vec]
        args += list(norm)
        out_shape.append(jax.ShapeDtypeStruct((s, d), BF16))
    outs = _pcall(body, name=name, out_shape=tuple(out_shape), grid=(s // tm, nk), in_specs=in_specs,
                  out_specs=(row,) * len(out_shape), scratch=[] if nk == 1 else [pltpu.VMEM((tm, d), F32)])(*args)
    return outs if norm is not None else (*outs, None)


def _mm_merge(o_a, o_b, w_a, w_b, proj, *, name):
    s = o_a.shape[0]
    d = w_a.shape[1]
    tm = _tile(s, 512)
    ga_blk = OFF_GATES // d

    def body(oa_ref, ob_ref, wa_ref, wb_ref, ga_ref, gb_ref, m_ref, ya_ref, yb_ref):
        for rows in _row_chunks(tm):
            ya, yb = _dot(oa_ref[rows, :], wa_ref[...]), _dot(ob_ref[rows, :], wb_ref[...])
            m_ref[rows, :] = (jax.nn.sigmoid(ga_ref[rows, :]) * ya
                              + jax.nn.sigmoid(gb_ref[rows, :]) * yb).astype(BF16)
            ya_ref[rows, :] = ya.astype(BF16)
            yb_ref[rows, :] = yb.astype(BF16)

    row = pl.BlockSpec((tm, d), lambda i: (i, 0))
    act = pl.BlockSpec((tm, o_a.shape[1]), lambda i: (i, 0))
    wspec = pl.BlockSpec(w_a.shape, lambda i: (0, 0))
    shp = jax.ShapeDtypeStruct((s, d), BF16)
    return _pcall(body, name=name, out_shape=(shp, shp, shp), grid=(s // tm,),
                  in_specs=[act, act, wspec, wspec, pl.BlockSpec((tm, d), lambda i: (i, ga_blk)),
                            pl.BlockSpec((tm, d), lambda i: (i, ga_blk + 1))],
                  out_specs=(row, row, row))(o_a, o_b, w_a, w_b, proj, proj)


def _mm_down_t_swiglu(df, w_down, g, u, *, name):
    s, d = df.shape
    f = w_down.shape[0]
    tm, tn = _tile(s, 1024), _tile(f, 512)

    def body(df_ref, w_ref, g_ref, u_ref, dg_ref, du_ref):
        w = w_ref[...]
        for rows in _row_chunks(tm):
            da = _dot(df_ref[rows, :], w, NT)
            gf = g_ref[rows, :].astype(F32)
            sg = jax.nn.sigmoid(gf)
            dg_ref[rows, :] = (da * u_ref[rows, :].astype(F32) * (sg * (1.0 + gf * (1.0 - sg)))).astype(BF16)
            du_ref[rows, :] = (da * (gf * sg)).astype(BF16)

    tile = pl.BlockSpec((tm, tn), lambda i, j: (i, j))
    shp = jax.ShapeDtypeStruct((s, f), BF16)
    return _pcall(body, name=name, out_shape=(shp, shp), grid=(s // tm, f // tn),
                  in_specs=[pl.BlockSpec((tm, d), lambda i, j: (i, 0)), pl.BlockSpec((tn, d), lambda i, j: (j, 0)),
                            tile, tile],
                  out_specs=(tile, tile))(df, w_down, g, u)


def _rmsmod_fwd(h, g, scale, shift, *, name):
    s, d = h.shape
    ts = _rows(s, d)

    def body(h_ref, g_ref, sc_ref, sh_ref, u_ref):
        hf = h_ref[...]
        r = lax.rsqrt(jnp.mean(hf * hf, axis=-1, keepdims=True) + EPS)
        u_ref[...] = (((hf * r) * g_ref[...]) * (1.0 + sc_ref[...]) + sh_ref[...]).astype(BF16)

    row = pl.BlockSpec((ts, d), lambda i: (i, 0))
    vec = pl.BlockSpec((1, d), lambda i: (0, 0))
    return _pcall(body, name=name, out_shape=jax.ShapeDtypeStruct((s, d), BF16), grid=(s // ts,),
                  in_specs=[row, vec, vec, vec], out_specs=row)(h, g, scale, shift)


def _gate_bwd(dhf, t_ref, gate_ref, dt_ref, dgate_ref):
    dt_ref[...] = (dhf * gate_ref[...]).astype(BF16)
    dgate_ref[...] += jnp.sum(dhf * t_ref[...], axis=0, keepdims=True)


def _rmsmod_bwd(du, h, g, scale, dres, t, gate, *, name):
    s, d = h.shape
    ts = _rows(s, d)
    chain = t is not None

    def body(*refs):
        du_ref, h_ref, g_ref, sc_ref, dres_ref = refs[:5]
        dh_ref, dsh_ref, dsc_ref, dg_ref = refs[-6:-2] if chain else refs[-4:]
        sums = (dsh_ref, dsc_ref, dg_ref) + ((refs[-1],) if chain else ())

        @pl.when(pl.program_id(0) == 0)
        def _():
            for ref in sums:
                ref[...] = jnp.zeros_like(ref)

        hf, duf, gain = h_ref[...], du_ref[...], g_ref[...]
        r = lax.rsqrt(jnp.mean(hf * hf, axis=-1, keepdims=True) + EPS)
        xh = hf * r
        dn = duf * (1.0 + sc_ref[...])
        dsh_ref[...] += jnp.sum(duf, axis=0, keepdims=True)
        dsc_ref[...] += jnp.sum(duf * (xh * gain), axis=0, keepdims=True)
        dg_ref[...] += jnp.sum(dn * xh, axis=0, keepdims=True)
        dxh = dn * gain
        dh = dres_ref[...] + r * (dxh - xh * jnp.mean(dxh * xh, axis=-1, keepdims=True))
        dh_ref[...] = dh
        if chain:
            _gate_bwd(dh, refs[5], refs[6], refs[-2], refs[-1])

    row = pl.BlockSpec((ts, d), lambda i: (i, 0))
    vec = pl.BlockSpec((1, d), lambda i: (0, 0))
    vshape = jax.ShapeDtypeStruct((1, d), F32)
    out_shape, out_specs = [jax.ShapeDtypeStruct((s, d), F32), vshape, vshape, vshape], [row, vec, vec, vec]
    in_specs, args = [row, row, vec, vec, row], [du, h, g, scale, dres]
    if chain:
        in_specs, args = in_specs + [row, vec], args + [t, gate]
        out_shape, out_specs = out_shape + [jax.ShapeDtypeStruct((s, d), BF16), vshape], out_specs + [row, vec]
    outs = _pcall(body, name=name, out_shape=tuple(out_shape), grid=(s // ts,), in_specs=in_specs,
                  out_specs=tuple(out_specs))(*args)
    return outs if chain else (*outs, None, None)


def _merge_bwd(dm, proj, y_a, y_b, *, name):
    s, d = y_a.shape
    ts = _rows(s, d)
    ga_blk = OFF_GATES // d

    def body(dm_ref, ga_ref, gb_ref, ya_ref, yb_ref, dya_ref, dyb_ref, dga_ref, dgb_ref):
        dmf = dm_ref[...]
        sa, sb = jax.nn.sigmoid(ga_ref[...]), jax.nn.sigmoid(gb_ref[...])
        dya_ref[...] = (dmf * sa).astype(BF16)
        dyb_ref[...] = (dmf * sb).astype(BF16)
        dga_ref[...] = (dmf * ya_ref[...] * (sa * (1.0 - sa))).astype(BF16)
        dgb_ref[...] = (dmf * yb_ref[...] * (sb * (1.0 - sb))).astype(BF16)

    row = pl.BlockSpec((ts, d), lambda i: (i, 0))
    ga = pl.BlockSpec((ts, d), lambda i: (i, ga_blk))
    gb = pl.BlockSpec((ts, d), lambda i: (i, ga_blk + 1))
    shp = jax.ShapeDtypeStruct((s, d), BF16)
    return _pcall(body, name=name, out_shape=(shp, shp, shp, shp), grid=(s // ts,),
                  in_specs=[row, ga, gb, row, row], out_specs=(row, row, row, row))(dm, proj, proj, y_a, y_b)


def _mm_swiglu(u2, w_gate_up, *, name):
    s, d = u2.shape
    f = w_gate_up.shape[1] // 2
    tm, tn = _tile(s, 1024), _tile(f, 512)
    nj = f // tn

    def body(x_ref, wg_ref, wu_ref, a_ref, g_ref, u_ref):
        for rows in _row_chunks(tm):
            x = x_ref[rows, :]
            gf, uf = _dot(x, wg_ref[...]), _dot(x, wu_ref[...])
            a_ref[rows, :] = ((gf * jax.nn.sigmoid(gf)) * uf).astype(BF16)
            g_ref[rows, :] = gf.astype(BF16)
            u_ref[rows, :] = uf.astype(BF16)

    out = pl.BlockSpec((tm, tn), lambda i, j: (i, j))
    shp = jax.ShapeDtypeStruct((s, f), BF16)
    return _pcall(body, name=name, out_shape=(shp, shp, shp), grid=(s // tm, nj),
                  in_specs=[pl.BlockSpec((tm, d), lambda i, j: (i, 0)), pl.BlockSpec((d, tn), lambda i, j: (0, j)),
                            pl.BlockSpec((d, tn), lambda i, j: (0, nj + j))],
                  out_specs=(out, out, out))(u2, w_gate_up, w_gate_up)


def _loss_fwd(y, tgt, t, gate, *, name):
    s, d = y.shape
    ts = _rows(s, d)

    def body(y_ref, tgt_ref, t_ref, gate_ref, l_ref, dy_ref, dt_ref, dgate_ref):
        @pl.when(pl.program_id(0) == 0)
        def _():
            l_ref[...] = jnp.zeros_like(l_ref)
            dgate_ref[...] = jnp.zeros_like(dgate_ref)

        e = y_ref[...] - tgt_ref[...]
        dy = e * (1.0 / d)
        dy_ref[...] = dy
        per_tok = jnp.sum(e * e, axis=1, keepdims=True) * (1.0 / d)
        l_ref[...] += 0.5 * jnp.sum(per_tok, axis=0, keepdims=True)
        _gate_bwd(dy, t_ref, gate_ref, dt_ref, dgate_ref)

    row = pl.BlockSpec((ts, d), lambda i: (i, 0))
    vec = pl.BlockSpec((1, d), lambda i: (0, 0))
    return _pcall(body, name=name,
                  out_shape=(jax.ShapeDtypeStruct((1, 128), F32), jax.ShapeDtypeStruct((s, d), F32),
                             jax.ShapeDtypeStruct((s, d), BF16), jax.ShapeDtypeStruct((1, d), F32)),
                  grid=(s // ts,), in_specs=[row, row, row, vec],
                  out_specs=(pl.BlockSpec((1, 128), lambda i: (0, 0)), row, row, vec))(y, tgt, t, gate)


def _rope_tables(seq):
    inv = jnp.power(ROPE_THETA, -jnp.arange(0, HEAD_DIM, 2, dtype=F32) / HEAD_DIM)
    ang = jnp.arange(seq, dtype=F32)[:, None] * inv[None, :]
    cos, sin = jnp.cos(ang), jnp.sin(ang)
    return jnp.concatenate([cos, cos], axis=1), jnp.concatenate([-sin, sin], axis=1)


def _qkrope_fwd(proj, gains, cos2, sin2, *, name):
    s = proj.shape[0]
    ts = _rows(s, A_W)

    def body(x_ref, g_ref, c_ref, s_ref, o_ref):
        gain, cos, sin = g_ref[...], c_ref[...], s_ref[...]
        for h in range(A_HEADS):
            lanes = slice(h * HEAD_DIM, (h + 1) * HEAD_DIM)
            x = x_ref[:, lanes]
            y = (x * lax.rsqrt(jnp.mean(x * x, axis=-1, keepdims=True) + EPS)) * gain
            o_ref[:, lanes] = (y * cos + pltpu.roll(y, HEAD_DIM // 2, 1) * sin).astype(BF16)

    heads = pl.BlockSpec((ts, A_W), lambda i, j: (i, j))
    tab = pl.BlockSpec((ts, HEAD_DIM), lambda i, j: (i, 0))
    gain = pl.BlockSpec((None, 1, HEAD_DIM), lambda i, j: (j, 0, 0))
    return _pcall(body, name=name, out_shape=jax.ShapeDtypeStruct((s, 2 * A_W), BF16),
                  grid=(s // ts, 2), in_specs=[heads, gain, tab, tab], out_specs=heads)(
                      proj, gains, cos2, sin2)


def _qkrope_bwd(d_groups, proj, gains, which, cos2, sin2, *, name):
    s = proj.shape[0]
    ts = _rows(s, A_W)

    def body(d0_ref, d1_ref, d2_ref, x_ref, g_ref, c_ref, s_ref, dx_ref, dg_ref):
        @pl.when(pl.program_id(0) == 0)
        def _():
            dg_ref[...] = jnp.zeros_like(dg_ref)

        gain, cos, sin = g_ref[...], c_ref[...], s_ref[...]
        dg = jnp.zeros((1, HEAD_DIM), F32)
        for h in range(A_HEADS):
            lanes = slice(h * HEAD_DIM, (h + 1) * HEAD_DIM)
            slot = slice((h % HEADS_PER_GROUP) * HEAD_DIM, (h % HEADS_PER_GROUP + 1) * HEAD_DIM)
            dout = (d0_ref, d1_ref, d2_ref)[h // HEADS_PER_GROUP][:, slot]
            dy = dout * cos + pltpu.roll(dout * sin, HEAD_DIM // 2, 1)
            x = x_ref[:, lanes]
            r = lax.rsqrt(jnp.mean(x * x, axis=-1, keepdims=True) + EPS)
            xh = x * r
            dg = dg + jnp.sum(dy * xh, axis=0, keepdims=True)
            dxh = dy * gain
            dx_ref[:, lanes] = (r * (dxh - xh * jnp.mean(dxh * xh, axis=-1, keepdims=True))).astype(BF16)
        dg_ref[...] += dg

    group = pl.BlockSpec((ts, GROUP_W), lambda i: (i, 0))
    tab = pl.BlockSpec((ts, HEAD_DIM), lambda i: (i, 0))
    gain = pl.BlockSpec((None, 1, HEAD_DIM), lambda i: (which, 0, 0))
    return _pcall(body, name=name,
                  out_shape=(jax.ShapeDtypeStruct((s, A_W), BF16), jax.ShapeDtypeStruct((1, HEAD_DIM), F32)),
                  grid=(s // ts,),
                  in_specs=[group, group, group, pl.BlockSpec((ts, A_W), lambda i: (i, which)), gain, tab, tab],
                  out_specs=(pl.BlockSpec((ts, A_W), lambda i: (i, 0)), pl.BlockSpec((1, HEAD_DIM), lambda i: (0, 0))))(
                      *d_groups, proj, gains, cos2, sin2)


def _assemble(pieces, *, name):
    s = pieces[0].shape[0]
    widths = [p.shape[1] for p in pieces]
    total = sum(widths)
    ts = _rows(s, total // 2)

    def body(*refs):
        o_ref, off = refs[-1], 0
        for x_ref, w in zip(refs[:-1], widths):
            o_ref[:, off:off + w] = x_ref[...].astype(BF16)
            off += w

    return _pcall(body, name=name, out_shape=jax.ShapeDtypeStruct((s, total), BF16), grid=(s // ts,),
                  in_specs=[pl.BlockSpec((ts, w), lambda i: (i, 0)) for w in widths],
                  out_specs=pl.BlockSpec((ts, total), lambda i: (i, 0)))(*pieces)


def _block_rows(blk):
    if isinstance(blk, int):
        return pl.ds(blk * BLOCK, BLOCK)
    return pl.ds(pl.multiple_of(blk * BLOCK, BLOCK), BLOCK)


def _band_window(n, length):
    width = min(2 * BLOCK, length)
    row = lax.broadcasted_iota(jnp.int32, (BLOCK, width), 0)
    col = lax.broadcasted_iota(jnp.int32, (BLOCK, width), 1)
    if width == BLOCK:
        return pl.ds(0, BLOCK), col <= row
    first = n - 1 if isinstance(n, int) else jnp.maximum(n - 1, 0)
    first = max(first, 0) if isinstance(first, int) else first
    dist = row - col + (n - first) * BLOCK
    start = first * BLOCK if isinstance(first, int) else pl.multiple_of(first * BLOCK, BLOCK)
    return pl.ds(start, width), jnp.logical_and(dist >= 0, dist <= BLOCK)


def _dil_fwd(q_arr, k_arr, v_arr, offs, length, dil, *, name):
    nj, nb = dil * HEADS_PER_GROUP, length // BLOCK
    ju, nq = (HEADS_PER_GROUP, 2) if nb > 1 else (2 * HEADS_PER_GROUP, 1)
    qo, ko, vo = (off // ju for off in offs)
    assert all(off % ju == 0 for off in offs) and nb % nq == 0 and nj % ju == 0

    def body(q_ref, k_ref, v_ref, o_ref, l_ref):
        for qq in range(nq):
            qrows = slice(qq * BLOCK, (qq + 1) * BLOCK)
            rows, mask = _band_window(pl.program_id(1) * nq + qq, length)
            for cb in range(ju):
                lanes = slice(cb * HEAD_DIM, (cb + 1) * HEAD_DIM)
                sc = _dot(q_ref[qrows, lanes].astype(BF16), k_ref[rows, lanes].astype(BF16), NT) * ATT_SCALE
                sc = jnp.where(mask, sc, MASKED)
                m = sc.max(axis=-1, keepdims=True)
                p = jnp.exp(sc - m)
                den = jnp.sum(p, axis=-1, keepdims=True)
                acc = _dot(p.astype(BF16), v_ref[rows, lanes].astype(BF16))
                o_ref[qrows, lanes] = acc / den
                l_ref[qrows, lanes] = jnp.broadcast_to(m + jnp.log(den), (BLOCK, HEAD_DIM))

    qspec = pl.BlockSpec((nq * BLOCK, ju * HEAD_DIM), lambda j, n: (n, qo + j))
    kspec = pl.BlockSpec((length, ju * HEAD_DIM), lambda j, n: (0, ko + j))
    vspec = pl.BlockSpec((length, ju * HEAD_DIM), lambda j, n: (0, vo + j))
    ospec = pl.BlockSpec((nq * BLOCK, ju * HEAD_DIM), lambda j, n: (n, j))
    shp = jax.ShapeDtypeStruct((length, nj * HEAD_DIM), F32)
    return _pcall(body, name=name, out_shape=(shp, shp), grid=(nj // ju, nb // nq), in_specs=[qspec, kspec, vspec],
                  out_specs=(ospec, ospec))(q_arr, k_arr, v_arr)


def _dil_bwd(q_arr, k_arr, v_arr, offs, o, lse, do, dlse, length, dil, *, name):
    nj, nb = dil * HEADS_PER_GROUP, length // BLOCK
    ju = 2 * HEADS_PER_GROUP if length <= 4 * BLOCK else 2
    qo, ko, vo = (off // ju for off in offs)
    assert all(off % ju == 0 for off in offs)

    def body(q_ref, k_ref, v_ref, o_ref, l_ref, do_ref, dl_ref, dq_ref, dk_ref, dv_ref):
        dk_ref[...] = jnp.zeros_like(dk_ref)
        dv_ref[...] = jnp.zeros_like(dv_ref)

        def step(n, carry):
            qrows = _block_rows(n)
            rows, mask = _band_window(n, length)
            for cb in range(ju):
                lanes = slice(cb * HEAD_DIM, (cb + 1) * HEAD_DIM)
                q = q_ref[qrows, lanes].astype(BF16)
                dof = do_ref[qrows, lanes]
                dob = dof.astype(BF16)
                lse_c = l_ref[qrows, lanes][:, :1]
                shift = dl_ref[qrows, lanes][:, :1] - jnp.sum(dof * o_ref[qrows, lanes], axis=-1, keepdims=True)
                kk, vv = k_ref[rows, lanes].astype(BF16), v_ref[rows, lanes].astype(BF16)
                sc = _dot(q, kk, NT) * ATT_SCALE
                p = jnp.where(mask, jnp.exp(sc - lse_c), 0.0)
                ds = (p * (_dot(dob, vv, NT) + shift)).astype(BF16)
                dq_ref[qrows, lanes] = _dot(ds, kk) * ATT_SCALE
                dk_ref[rows, lanes] += _dot(ds, q, TN) * ATT_SCALE
                dv_ref[rows, lanes] += _dot(p.astype(BF16), dob, TN)
            return carry

        if nb == 1:
            step(0, 0)
        else:
            lax.fori_loop(0, nb, step, 0)

    def col(off):
        return pl.BlockSpec((length, ju * HEAD_DIM), lambda j: (0, off + j))

    shp = jax.ShapeDtypeStruct((length, nj * HEAD_DIM), F32)
    return _pcall(body, name=name, out_shape=(shp, shp, shp), grid=(nj // ju,),
                  in_specs=[col(qo), col(ko), col(vo), col(0), col(0), col(0), col(0)],
                  out_specs=(col(0), col(0), col(0)))(q_arr, k_arr, v_arr, o, lse, do, dlse)


def _dil_bwd_strided(q_view, k_view, v_view, o, lse, do, dlse, length, dil, *, name):
    nb = length // BLOCK
    seq = length * dil
    width = min(2 * BLOCK, length)
    rp = 4
    assert dil % rp == 0

    def body(*refs):
        q_refs, k_refs, v_refs = refs[:rp], refs[rp:2 * rp], refs[2 * rp:3 * rp]
        o_ref, l_ref, do_ref, dl_ref, dq_ref, dk_ref, dv_ref = refs[3 * rp:]
        rgroup = pl.program_id(1)

        @pl.when(rgroup == 0)
        def _():
            dk_ref[...] = jnp.zeros_like(dk_ref)
            dv_ref[...] = jnp.zeros_like(dv_ref)

        def step(n, carry):
            rows, mask = _band_window(n, length)
            first = 0 if width == BLOCK else jnp.maximum(n - 1, 0)
            for rr in range(rp):
                r = rgroup * rp + rr
                tok_q = pl.ds(n * (BLOCK * dil) + r, BLOCK, stride=dil)
                tok_k = pl.ds(first * (BLOCK * dil) + r, width, stride=dil)
                q = q_refs[rr][_block_rows(n), :].astype(BF16)
                dof = do_ref[tok_q, :]
                dob = dof.astype(BF16)
                lse_c = l_ref[tok_q, :][:, :1]
                shift = dl_ref[tok_q, :][:, :1] - jnp.sum(dof * o_ref[tok_q, :], axis=-1, keepdims=True)
                kk, vv = k_refs[rr][rows, :].astype(BF16), v_refs[rr][rows, :].astype(BF16)
                sc = _dot(q, kk, NT) * ATT_SCALE
                p = jnp.where(mask, jnp.exp(sc - lse_c), 0.0)
                ds = (p * (_dot(dob, vv, NT) + shift)).astype(BF16)
                dq_ref[tok_q, :] = _dot(ds, kk) * ATT_SCALE
                dk_ref[tok_k, :] += _dot(ds, q, TN) * ATT_SCALE
                dv_ref[tok_k, :] += _dot(p.astype(BF16), dob, TN)
            return carry

        if nb == 1:
            step(0, 0)
        else:
            lax.fori_loop(0, nb, step, 0)

    def view(rr):
        return pl.BlockSpec((length, HEAD_DIM), lambda h, r: (0, (r * rp + rr) * HEADS_PER_GROUP + h))

    views = [view(rr) for rr in range(rp)]
    nat = pl.BlockSpec((seq, HEAD_DIM), lambda h, r: (0, h))
    shp = jax.ShapeDtypeStruct((seq, GROUP_W), F32)
    return _pcall(body, name=name, out_shape=(shp, shp, shp), grid=(HEADS_PER_GROUP, dil // rp),
                  in_specs=views * 3 + [nat, nat, nat, nat], out_specs=(nat, nat, nat))(
                      *[q_view] * rp, *[k_view] * rp, *[v_view] * rp, o, lse, do, dlse)


def _combine_weights(l_refs):
    ls = [r[...] for r in l_refs]
    m = jnp.maximum(jnp.maximum(ls[0], ls[1]), ls[2])
    es = [jnp.exp(l - m) for l in ls]
    den = es[0] + es[1] + es[2]
    return [e / den for e in es]


def _combine_fwd(os_, lses, *, name):
    s = os_[0].shape[0]
    ts = _rows(s, GROUP_W)

    def body(o0, o1, o2, l0, l1, l2, out_ref):
        w = _combine_weights((l0, l1, l2))
        out_ref[...] = (w[0] * o0[...] + w[1] * o1[...] + w[2] * o2[...]).astype(BF16)

    row = pl.BlockSpec((ts, GROUP_W), lambda i: (i, 0))
    return _pcall(body, name=name, out_shape=jax.ShapeDtypeStruct((s, GROUP_W), BF16), grid=(s // ts,),
                  in_specs=[row] * 6, out_specs=row)(*os_, *lses)


def _combine_bwd(do_a, os_, lses, *, name):
    s = do_a.shape[0]
    ts = _rows(s, GROUP_W)

    def body(d_ref, o0, o1, o2, l0, l1, l2, do0, do1, do2, dl0, dl1, dl2):
        w = _combine_weights((l0, l1, l2))
        d = d_ref[...]
        og = [o0[...], o1[...], o2[...]]
        oa = w[0] * og[0] + w[1] * og[1] + w[2] * og[2]
        ta = jnp.sum(d * oa, axis=-1, keepdims=True)
        for g, (do_ref, dl_ref) in enumerate(((do0, dl0), (do1, dl1), (do2, dl2))):
            do_ref[...] = w[g] * d
            dl_ref[...] = w[g] * (jnp.sum(d * og[g], axis=-1, keepdims=True) - ta)

    head = pl.BlockSpec((ts, HEAD_DIM), lambda i, h: (i, h))
    shp = jax.ShapeDtypeStruct((s, GROUP_W), F32)
    return _pcall(body, name=name, out_shape=(shp,) * 6, grid=(s // ts, HEADS_PER_GROUP),
                  in_specs=[head] * 7, out_specs=(head,) * 6)(do_a, *os_, *lses)


def _dot_exact(x, ones_mask):
    hi = x.astype(BF16)
    r1 = x - hi.astype(F32)
    mid = r1.astype(BF16)
    lo = (r1 - mid.astype(F32)).astype(BF16)
    return _dot(hi, ones_mask) + _dot(mid, ones_mask) + _dot(lo, ones_mask)


SB_QROWS = 2 * BLOCK
SB_UNROLL = 4
SB_HEADS_PER_STEP = 2
SB_LANES = [slice(hh * HEAD_DIM, (hh + 1) * HEAD_DIM) for hh in range(SB_HEADS_PER_STEP)]


def _sb_mask(j, i):
    row = lax.broadcasted_iota(jnp.int32, (SB_QROWS, BLOCK), 0)
    col = lax.broadcasted_iota(jnp.int32, (SB_QROWS, BLOCK), 1)
    return col + (j * BLOCK - i * SB_QROWS) < row


def _sb_steps(i):
    return ((i + 1) * (SB_QROWS // BLOCK) + SB_UNROLL - 1) // SB_UNROLL


def _sb_scores(q, kk, j, i, masked):
    mask = _sb_mask(j, i) if masked else None
    z = _dot(q, kk, NT) * ATT_SCALE
    sp = jnp.log(1.0 + jnp.exp(-jnp.abs(z)))
    log_beta = jnp.minimum(z, 0.0) - sp
    log_1mb = jnp.minimum(-z, 0.0) - sp
    if masked:
        log_1mb = jnp.where(mask, log_1mb, 0.0)
    return z, log_beta, log_1mb, mask


def _sb_weights(log_beta, log_1mb, mask, run, upper):
    a = jnp.exp(log_beta + (run + _dot_exact(log_1mb, upper)))
    return a if mask is None else jnp.where(mask, a, 0.0)


def _sb_peeled(nsteps, make_step, init, masked_first):
    if masked_first:
        return lax.fori_loop(1, nsteps, make_step(False), make_step(True)(0, init))
    return make_step(True)(nsteps - 1, lax.fori_loop(0, nsteps - 1, make_step(False), init))


def _tri(strict_lower):
    row = lax.broadcasted_iota(jnp.int32, (BLOCK, BLOCK), 0)
    col = lax.broadcasted_iota(jnp.int32, (BLOCK, BLOCK), 1)
    return ((row > col) if strict_lower else (row < col)).astype(BF16)


def _sb_fwd(proj, *, name):
    s = proj.shape[0]
    assert s % (BLOCK * SB_UNROLL) == 0 and s % SB_QROWS == 0

    def body(q_ref, k_ref, v_ref, o_ref):
        i = pl.program_id(1)
        qs = [q_ref[:, lanes].astype(BF16) for lanes in SB_LANES]
        upper = _tri(True)
        nsteps = _sb_steps(i)

        def make_step(masked):
            def step(t, carry):
                carry = list(carry)
                for b in reversed(range(SB_UNROLL)):
                    j = (nsteps - 1 - t) * SB_UNROLL + b
                    rows = _block_rows(j)
                    for hh, lanes in enumerate(SB_LANES):
                        acc, run = carry[hh]
                        _, log_beta, log_1mb, mask = _sb_scores(qs[hh], k_ref[rows, lanes].astype(BF16), j, i, masked)
                        a = _sb_weights(log_beta, log_1mb, mask, run, upper)
                        carry[hh] = (acc + _dot(a.astype(BF16), v_ref[rows, lanes].astype(BF16)),
                                     run + jnp.sum(log_1mb, axis=-1, keepdims=True))
                return tuple(carry)
            return step

        zero = (jnp.zeros((SB_QROWS, HEAD_DIM), F32), jnp.zeros((SB_QROWS, 1), F32))
        for lanes, (acc, _) in zip(SB_LANES, _sb_peeled(nsteps, make_step, (zero,) * SB_HEADS_PER_STEP, True)):
            o_ref[:, lanes] = acc.astype(BF16)

    width = SB_HEADS_PER_STEP * HEAD_DIM
    qb, kb, vb = (off // width for off in (OFF_QB, OFF_KB, OFF_VB))
    return _pcall(body, name=name, out_shape=jax.ShapeDtypeStruct((s, B_W), BF16),
                  grid=(SB_HEADS // SB_HEADS_PER_STEP, s // SB_QROWS),
                  in_specs=[pl.BlockSpec((SB_QROWS, width), lambda h, i: (i, qb + h)),
                            pl.BlockSpec((s, width), lambda h, i: (0, kb + h)),
                            pl.BlockSpec((s, width), lambda h, i: (0, vb + h))],
                  out_specs=pl.BlockSpec((SB_QROWS, width), lambda h, i: (i, h)))(proj, proj, proj)


def _sb_bwd(proj, do_b, *, name):
    s = proj.shape[0]
    assert s % (BLOCK * SB_UNROLL) == 0 and s % SB_QROWS == 0
    nkb = s // BLOCK

    def body(q_ref, k_ref, v_ref, do_ref, dq_ref, dk_ref, dv_ref, z_s, a_s):
        i = pl.program_id(1)

        @pl.when(i == 0)
        def _():
            dk_ref[...] = jnp.zeros_like(dk_ref)
            dv_ref[...] = jnp.zeros_like(dv_ref)

        qs = [q_ref[:, lanes].astype(BF16) for lanes in SB_LANES]
        dobs = [do_ref[:, lanes].astype(BF16) for lanes in SB_LANES]
        upper, lower = _tri(True), _tri(False)
        nsteps = _sb_steps(i)

        def make_recompute(masked):
            def recompute(t, runs):
                runs = list(runs)
                for b in reversed(range(SB_UNROLL)):
                    j = (nsteps - 1 - t) * SB_UNROLL + b
                    rows = _block_rows(j)
                    for hh, lanes in enumerate(SB_LANES):
                        z, log_beta, log_1mb, mask = _sb_scores(qs[hh], k_ref[rows, lanes].astype(BF16), j, i, masked)
                        z_s[hh, j] = z
                        a_s[hh, j] = _sb_weights(log_beta, log_1mb, mask, runs[hh], upper)
                        runs[hh] = runs[hh] + jnp.sum(log_1mb, axis=-1, keepdims=True)
                return tuple(runs)
            return recompute

        _sb_peeled(nsteps, make_recompute, (jnp.zeros((SB_QROWS, 1), F32),) * SB_HEADS_PER_STEP, True)

        def make_grads(masked):
            def grads(t, carry):
                carry = list(carry)
                for b in range(SB_UNROLL):
                    j = t * SB_UNROLL + b
                    rows = _block_rows(j)
                    for hh, lanes in enumerate(SB_LANES):
                        dq, run = carry[hh]
                        kk, vv = k_ref[rows, lanes].astype(BF16), v_ref[rows, lanes].astype(BF16)
                        z, a = z_s[hh, j], a_s[hh, j]
                        de = _dot(dobs[hh], vv, NT) * a
                        beta = jax.nn.sigmoid(z)
                        one_minus_beta = 1.0 - beta
                        if masked:
                            beta = jnp.where(_sb_mask(j, i), beta, 0.0)
                        dz = (de * one_minus_beta - beta * (run + _dot_exact(de, lower))).astype(BF16)
                        dk_ref[rows, lanes] += _dot(dz, qs[hh], TN) * ATT_SCALE
                        dv_ref[rows, lanes] += _dot(a.astype(BF16), dobs[hh], TN)
                        carry[hh] = (dq + _dot(dz, kk), run + jnp.sum(de, axis=-1, keepdims=True))
                return tuple(carry)
            return grads

        zero = (jnp.zeros((SB_QROWS, HEAD_DIM), F32), jnp.zeros((SB_QROWS, 1), F32))
        for lanes, (dq, _) in zip(SB_LANES, _sb_peeled(nsteps, make_grads, (zero,) * SB_HEADS_PER_STEP, False)):
            dq_ref[:, lanes] = dq * ATT_SCALE

    width = SB_HEADS_PER_STEP * HEAD_DIM
    qb, kb, vb = (off // width for off in (OFF_QB, OFF_KB, OFF_VB))
    blk = pl.BlockSpec((SB_QROWS, width), lambda h, i: (i, h))
    full = pl.BlockSpec((s, width), lambda h, i: (0, h))
    shp = jax.ShapeDtypeStruct((s, B_W), F32)
    saved = pltpu.VMEM((SB_HEADS_PER_STEP, nkb, SB_QROWS, BLOCK), F32)
    return _pcall(body, name=name, out_shape=(shp, shp, shp), grid=(SB_HEADS // SB_HEADS_PER_STEP, s // SB_QROWS),
                  in_specs=[pl.BlockSpec((SB_QROWS, width), lambda h, i: (i, qb + h)),
                            pl.BlockSpec((s, width), lambda h, i: (0, kb + h)),
                            pl.BlockSpec((s, width), lambda h, i: (0, vb + h)), blk],
                  out_specs=(blk, full, full), scratch=[saved, saved])(proj, proj, proj, do_b)


def _coords():
    return lax.axis_index("x"), lax.axis_index("y"), lax.axis_index("c")


def _flip(v, bit):
    return 1 - v if bit else v


def _shard_of(ref, axis, idx, size):
    if axis == 0:
        sl = pl.ds(pl.multiple_of(idx * size, 16), size)
        return ref.at[sl, :] if len(ref.shape) == 2 else ref.at[:, sl, :]
    sl = pl.ds(pl.multiple_of(idx * size, 128), size)
    return ref.at[:, sl] if len(ref.shape) == 2 else ref.at[:, :, sl]


def _small_allgather(v, *, name, silu=False):
    n = v.shape[1]

    def body(v_ref, out_ref, send_sems, recv_sems):
        x, y, c = _coords()
        me = 4 * x + 2 * y + c
        val = v_ref[...]
        out_ref[me] = val * jax.nn.sigmoid(val) if silu else val
        copies = []
        for k in range(1, N_DEV):
            peer = (_flip(x, k & 4), _flip(y, k & 2), _flip(c, k & 1))
            copies.append(pltpu.make_async_remote_copy(
                src_ref=out_ref.at[me], dst_ref=out_ref.at[me], send_sem=send_sems.at[k - 1],
                recv_sem=recv_sems.at[k - 1], device_id=peer, device_id_type=MESH))
        for cp in copies:
            cp.start()
        for cp in copies:
            cp.wait_recv()
        for cp in copies:
            cp.wait_send()

    return _pcall(body, name=name, out_shape=jax.ShapeDtypeStruct((N_DEV, 1, n), F32),
                  in_specs=[pl.BlockSpec(memory_space=pltpu.VMEM)], out_specs=pl.BlockSpec(memory_space=pltpu.VMEM),
                  scratch=[pltpu.SemaphoreType.DMA((N_DEV - 1,)), pltpu.SemaphoreType.DMA((N_DEV - 1,))])(v)


def _cast_place(w, layer, axis, me, *, name):
    _, r, c = w.shape
    tr = _rows(r, c)
    nrt = r // tr

    def body(me_ref, w_ref, o_ref):
        o_ref[...] = w_ref[...].astype(BF16)

    wspec = pl.BlockSpec((None, tr, c), lambda i, me_ref: (layer, i, 0))
    if axis == 0:
        ospec = pl.BlockSpec((tr, c), lambda i, me_ref: (me_ref[0] * nrt + i, 0))
        shape = (r * N_DEV, c)
    else:
        ospec = pl.BlockSpec((tr, c), lambda i, me_ref: (i, me_ref[0]))
        shape = (r, c * N_DEV)
    return _pcall(body, name=name, out_shape=jax.ShapeDtypeStruct(shape, BF16), grid=(nrt,), in_specs=[wspec],
                  out_specs=ospec, prefetch=1)(me, w)


def _pair_sum(grad, sib, core, axis, *, name):
    _, r, c = sib.shape
    tr = _rows(r, c // 2)
    nrt = r // tr

    def body(core_ref, g_ref, s_ref, o_ref):
        o_ref[...] = (g_ref[...].astype(F32) + s_ref[...].astype(F32)).astype(BF16)

    if axis == 0:
        gspec = pl.BlockSpec((tr, c), lambda q, i, core_ref: ((2 * q + core_ref[0]) * nrt + i, 0))
    else:
        gspec = pl.BlockSpec((tr, c), lambda q, i, core_ref: (i, 2 * q + core_ref[0]))
    sspec = pl.BlockSpec((None, tr, c), lambda q, i, core_ref: (q, i, 0))
    return _pcall(body, name=name, out_shape=jax.ShapeDtypeStruct(sib.shape, BF16), grid=(N_CHIPS, nrt),
                  in_specs=[gspec, sspec], out_specs=sspec, prefetch=1)(core, grad, sib)


ANY_SPEC = pl.BlockSpec(memory_space=pl.ANY)
SEM_SPEC = pl.BlockSpec(memory_space=pltpu.SEMAPHORE)
SPLIT_PARAMS = dict(has_side_effects=pltpu.SideEffectType.DATAFLOW_SIDE_EFFECTING)


def _split_start(copies_fn, buffers, sem_shape, after, *, name):
    n = len(buffers)
    rows, cols = sem_shape
    ns = rows * cols
    extra = ([] if after is None else [after]) + _take_token()

    def body(*refs):
        sems = refs[n + len(extra):n + len(extra) + 2 * ns]
        for cp in copies_fn(refs[:n], _sem_rows(sems[:ns], cols), _sem_rows(sems[ns:], cols)):
            cp.start()
        refs[-1][...] = jnp.zeros_like(refs[-1])

    sem = pltpu.SemaphoreType.DMA(())
    outs = pl.pallas_call(
        body, name=name,
        out_shape=((sem,) * (2 * ns) + tuple(jax.ShapeDtypeStruct(b.shape, b.dtype) for b in buffers) + (TOKEN,)),
        in_specs=(ANY_SPEC,) * (n + len(extra)),
        out_specs=(SEM_SPEC,) * (2 * ns) + (ANY_SPEC,) * n + (pl.BlockSpec(memory_space=pltpu.VMEM),),
        input_output_aliases={i: 2 * ns + i for i in range(n)},
        compiler_params=pltpu.CompilerParams(**SPLIT_PARAMS))(*buffers, *extra)
    _ORDER["token"] = outs[-1]
    return list(outs[:ns]), list(outs[ns:2 * ns]), list(outs[2 * ns:2 * ns + n]), outs[-1]


def _split_wait(copies_fn, send_sems, recv_sems, buffers, after, sem_rows, *, name):
    n, ns = len(buffers), len(send_sems)
    cols = ns // sem_rows
    extra = ([] if after is None else [after]) + _take_token()

    def body(*refs):
        sems = refs[n:n + 2 * ns]
        copies = copies_fn(refs[:n], _sem_rows(sems[:ns], cols), _sem_rows(sems[ns:], cols))
        for cp in copies:
            cp.wait_send()
        for cp in copies:
            cp.wait_recv()
        refs[-1][...] = jnp.zeros_like(refs[-1])

    outs = pl.pallas_call(
        body, name=name, out_shape=tuple(jax.ShapeDtypeStruct(b.shape, b.dtype) for b in buffers) + (TOKEN,),
        in_specs=(ANY_SPEC,) * n + (SEM_SPEC,) * (2 * ns) + (ANY_SPEC,) * len(extra),
        out_specs=(ANY_SPEC,) * n + (pl.BlockSpec(memory_space=pltpu.VMEM),),
        input_output_aliases={i: i for i in range(n)},
        compiler_params=pltpu.CompilerParams(**SPLIT_PARAMS))(*buffers, *send_sems, *recv_sems, *extra)
    _ORDER["token"] = outs[-1]
    return list(outs[:n])


def _sem_rows(sems, cols):
    return [sems[i:i + cols] for i in range(0, len(sems), cols)]


def _empty_hbm(shape, dtype):
    return pltpu.with_memory_space_constraint(lax.empty(shape, dtype), pltpu.HBM)


class _SplitGather:
    def __init__(self, fulls, axes, tag):
        self.axes, self.tag, self.nt = list(axes), tag, len(fulls)
        self.sizes = [f.shape[ax] // N_DEV for f, ax in zip(fulls, axes)]
        self.fulls = list(fulls)

    def _slot(self, ref, t, dev):
        return _shard_of(ref, self.axes[t], 4 * dev[0] + 2 * dev[1] + dev[2], self.sizes[t])

    def _first_copies(self, refs, send_sems, recv_sems):
        x, y, c = _coords()
        peers = [(x, y, 1 - c), (1 - x, y, c), (x, 1 - y, c), (1 - x, 1 - y, c)]
        return [pltpu.make_async_remote_copy(
            src_ref=self._slot(refs[t], t, (x, y, c)), dst_ref=self._slot(refs[t], t, (x, y, c)),
            send_sem=send_sems[t][k], recv_sem=recv_sems[t][k], device_id=peer, device_id_type=MESH)
            for t in range(self.nt) for k, peer in enumerate(peers)]

    def _forward_copies(self, refs, send_sems, recv_sems):
        x, y, c = _coords()
        chips = [(1 - x, y), (x, 1 - y), (1 - x, 1 - y)]
        return [pltpu.make_async_remote_copy(
            src_ref=self._slot(refs[t], t, (*chip, c)), dst_ref=self._slot(refs[t], t, (*chip, c)),
            send_sem=send_sems[t][j], recv_sem=recv_sems[t][j], device_id=(x, y, 1 - c), device_id_type=MESH)
            for t in range(self.nt) for j, chip in enumerate(chips)]

    def first(self, after):
        self.s1, self.r1, self.fulls, token = _split_start(
            self._first_copies, self.fulls, (self.nt, 4), after, name=f"comm_gather1_start_{self.tag}")
        return token

    def forward(self, after):
        bufs = _split_wait(self._first_copies, self.s1, self.r1, self.fulls, after, self.nt,
                           name=f"comm_gather1_wait_{self.tag}")
        self.s2, self.r2, self.fulls, token = _split_start(
            self._forward_copies, bufs, (self.nt, 3), after, name=f"comm_gather2_start_{self.tag}")
        return token

    def finish(self, after):
        return _split_wait(self._forward_copies, self.s2, self.r2, self.fulls, after, self.nt,
                           name=f"comm_gather2_wait_{self.tag}")


class _SplitPairExchange:
    def __init__(self, grads, axes, tag):
        self.nt, self.tag, self.axes = len(grads), tag, list(axes)
        self.grads = list(grads)
        self.sizes = [g.shape[ax] // N_DEV for g, ax in zip(grads, axes)]

    def _copies(self, refs, send_sems, recv_sems):
        nt = self.nt
        x, y, c = _coords()
        return [pltpu.make_async_remote_copy(
            src_ref=_shard_of(refs[t], self.axes[t], 2 * q + 1 - c, self.sizes[t]), dst_ref=refs[nt + t].at[q],
            send_sem=send_sems[t][q], recv_sem=recv_sems[t][q], device_id=(x, y, 1 - c), device_id_type=MESH)
            for t in range(nt) for q in range(N_CHIPS)]

    def start(self):
        landing = []
        for g, ax in zip(self.grads, self.axes):
            dims = list(g.shape)
            dims[ax] //= N_DEV
            landing.append(_empty_hbm((N_CHIPS, *dims), g.dtype))
        self.s, self.r, self.bufs, token = _split_start(
            self._copies, self.grads + landing, (self.nt, N_CHIPS), None,
            name=f"comm_rs_pair_start_{self.tag}")
        return token

    def finish(self, after):
        bufs = _split_wait(self._copies, self.s, self.r, self.bufs, after, self.nt,
                           name=f"comm_rs_pair_wait_{self.tag}")
        return bufs[:self.nt], bufs[self.nt:]


class _ReducePipeline:
    def __init__(self, core):
        self.core, self.items, self.done, self.now = core, [], [], 0

    def add(self, keys, grads, layer):
        axes = [SHARD_AXIS[k] for k in keys]
        pair = _SplitPairExchange([grads[k] for k in keys], axes, f"{keys[0]}{layer}")
        pair.start()
        self.items.append(dict(keys=keys, layer=layer, axes=axes, pair=pair, state="pair", since=self.now))

    def tick(self, after, flush=False):
        self.now += 1
        for it in self.items:
            if it["state"] == "pair" and it["since"] < self.now:
                grads, sib = it["pair"].finish(after)
                sums = [_pair_sum(g, s_, self.core, ax, name="pair_sum_" + k)
                        for k, g, s_, ax in zip(it["keys"], grads, sib, it["axes"])]
                it["chip"] = _SplitChipExchange(sums, f"{it['keys'][0]}{it['layer']}")
                it["chip"].start()
                it.update(state="chip", since=self.now)
            elif it["state"] == "chip" and (flush or self.now - it["since"] >= 2):
                sums, remote = it["chip"].finish(after)
                self.done.append((it["keys"], it["layer"], sums, remote))
                it["state"] = "done"

    def take_done(self):
        out, self.done = self.done, []
        return out


class _SplitChipExchange:
    def __init__(self, sums, tag):
        self.nt, self.tag = len(sums), tag
        self.sums = list(sums)

    def _copies(self, refs, send_sems, recv_sems):
        nt = self.nt
        x, y, c = _coords()
        copies = []
        for t in range(nt):
            for k in range(1, N_CHIPS):
                px, py = _flip(x, k & 2), _flip(y, k & 1)
                copies.append(pltpu.make_async_remote_copy(
                    src_ref=refs[t].at[2 * px + py], dst_ref=refs[nt + t].at[k - 1], send_sem=send_sems[t][k - 1],
                    recv_sem=recv_sems[t][k - 1], device_id=(px, py, c), device_id_type=MESH))
        return copies

    def start(self):
        landing = [_empty_hbm((N_CHIPS - 1,) + s.shape[1:], s.dtype) for s in self.sums]
        self.s, self.r, self.bufs, token = _split_start(
            self._copies, self.sums + landing, (self.nt, N_CHIPS - 1), None,
            name=f"comm_rs_chip_start_{self.tag}")
        return token

    def finish(self, after):
        bufs = _split_wait(self._copies, self.s, self.r, self.bufs, after, self.nt,
                           name=f"comm_rs_chip_wait_{self.tag}")
        return bufs[:self.nt], bufs[self.nt:]


def _adam_math(g, w, m, v):
    m2 = ADAM_B1 * m + (1.0 - ADAM_B1) * g
    v2 = ADAM_B2 * v + (1.0 - ADAM_B2) * (g * g)
    m_hat = m2 / (1.0 - ADAM_B1 ** ADAM_STEP)
    v_hat = v2 / (1.0 - ADAM_B2 ** ADAM_STEP)
    delta = -ADAM_LR * (m_hat / (jnp.sqrt(v_hat) + ADAM_EPS) + ADAM_WD * w)
    return delta, m2, v2


def _adamw_sharded(chip_sums, remote, chip, w, m, v, layer, prev, *, name):
    nl, r, c = w.shape
    tr = _rows(r, c)

    def body(*refs):
        p_ref, r0_ref, r1_ref, r2_ref, w_ref, m_ref, v_ref = refs[1:8]
        g_out, d_out, m_out, v_out = refs[-4:]
        g = ((p_ref[...].astype(F32) + r0_ref[...].astype(F32)) + r1_ref[...].astype(F32)) + r2_ref[...].astype(F32)
        g_out[...] = g
        d_out[...], m_out[...], v_out[...] = _adam_math(g, w_ref[...], m_ref[...], v_ref[...])

    pspec = pl.BlockSpec((None, tr, c), lambda i, chip_ref: (chip_ref[0], i, 0))

    def rspec(k):
        return pl.BlockSpec((None, tr, c), lambda i, chip_ref: (k, i, 0))

    wspec = pl.BlockSpec((None, tr, c), lambda i, chip_ref: (layer, i, 0))
    in_specs = [pspec, rspec(0), rspec(1), rspec(2), wspec, wspec, wspec]
    args = [chip, chip_sums, remote, remote, remote, w, m, v]
    aliases = {}
    if prev is not None:
        in_specs += [pl.BlockSpec(memory_space=pl.ANY)] * 4
        aliases = {len(args) + i: i for i in range(4)}
        args += list(prev)
    shp = jax.ShapeDtypeStruct(w.shape, F32)
    return _pcall(body, name=name, out_shape=(shp,) * 4, grid=(r // tr,), in_specs=in_specs, out_specs=(wspec,) * 4,
                  aliases=aliases, prefetch=1)(*args)


def _adamw_local(g, w, m, v, *, name):
    nl, r, c = w.shape
    tr = _rows(r, c)

    def body(g_ref, w_ref, m_ref, v_ref, d_out, m_out, v_out):
        d_out[...], m_out[...], v_out[...] = _adam_math(g_ref[...], w_ref[...], m_ref[...], v_ref[...])

    spec = pl.BlockSpec((None, tr, c), lambda l, i: (l, i, 0))
    shp = jax.ShapeDtypeStruct(w.shape, F32)
    return _pcall(body, name=name, out_shape=(shp,) * 3, grid=(nl, r // tr), in_specs=[spec] * 4,
                  out_specs=(spec,) * 3)(g, w, m, v)


def _adamw_replicated(parts, w, m, v, *, name):
    n = w.shape[1]

    def body(p_ref, w_ref, m_ref, v_ref, g_out, d_out, m_out, v_out):
        g = p_ref[0]
        for k in range(1, N_DEV):
            g = g + p_ref[k]
        g_out[...] = g
        d_out[...], m_out[...], v_out[...] = _adam_math(g, w_ref[...], m_ref[...], v_ref[...])

    vm = pl.BlockSpec(memory_space=pltpu.VMEM)
    shp = jax.ShapeDtypeStruct((1, n), F32)
    return _pcall(body, name=name, out_shape=(shp,) * 4, in_specs=[vm] * 4, out_specs=(vm,) * 4)(parts, w, m, v)


def _group_views(qk, proj, g, dil, seq):
    if dil == 1:
        return (qk, qk, proj), (0, A_HEADS, 2 * A_HEADS)
    length = seq // dil
    lo = g * GROUP_W
    q = qk[:, lo:lo + GROUP_W].reshape(length, dil * GROUP_W)
    k = qk[:, A_W + lo:A_W + lo + GROUP_W].reshape(length, dil * GROUP_W)
    v = proj[:, OFF_VA + lo:OFF_VA + lo + GROUP_W].astype(BF16).reshape(length, dil * GROUP_W)
    return (q, k, v), (0, 0, 0)


def _mod_rows(mod, d):
    return [mod[:, i * d:(i + 1) * d] for i in range(6)]


MIXER_W = ("w_in", "w_branch_a", "w_branch_b", "w_out")
FFN_W = ("w_gate_up", "w_down")
SHARD_AXIS = {"w_in": 1, "w_branch_a": 1, "w_branch_b": 1, "w_out": 0, "w_gate_up": 1, "w_down": 0}


def _norm_args(mod, gain, which, d):
    rows = _mod_rows(mod, d)
    return gain, rows[3 * which + 1], rows[3 * which]


def _mixer_fwd_a(h, u, gains, w_in, cos2, sin2, hook):
    seq = h.shape[0]
    proj = _mm(u, w_in, name="mm_in")
    hook(proj)
    qk = _qkrope_fwd(proj, gains, cos2, sin2, name="qkrope_fwd")
    os_, lses, views = [], [], []
    for g, dil in enumerate(DILATIONS):
        arrs, offs = _group_views(qk, proj, g, dil, seq)
        o, lse = _dil_fwd(*arrs, offs, seq // dil, dil, name=f"dil_fwd_{dil}")
        views.append((arrs, offs, o, lse))
        os_.append(o.reshape(seq, GROUP_W))
        lses.append(lse.reshape(seq, GROUP_W))
    o_a = _combine_fwd(os_, lses, name="combine_fwd")
    o_b = _sb_fwd(proj, name="sb_fwd")
    return dict(h_in=h, u=u, proj=proj, views=views, os=os_, lses=lses, o_a=o_a, o_b=o_b)


def _mixer_fwd_b(sv, mod, g2, wts):
    d = sv["h_in"].shape[1]
    merged, y_a, y_b = _mm_merge(sv["o_a"], sv["o_b"], wts["w_branch_a"], wts["w_branch_b"], sv["proj"],
                                 name="mm_branch")
    h_mid, t, u2 = _mm_resid_norm(merged, wts["w_out"], sv["h_in"], _mod_rows(mod, d)[2], _norm_args(mod, g2, 1, d),
                                  name="mm_out")
    sv.update(y_a=y_a, y_b=y_b, merged=merged, t=t, h_mid=h_mid, u2=u2)
    return h_mid


def _ffn_fwd_a(sv, w_gate_up):
    a, g, u = _mm_swiglu(sv["u2"], w_gate_up, name="mm_gate_up")
    sv.update(g=g, up=u, a=a)
    return a


def _ffn_fwd_b(sv, mod, w_down, next_norm):
    d = sv["h_mid"].shape[1]
    h_out, sv["f"], u_next = _mm_resid_norm(sv["a"], w_down, sv["h_mid"], _mod_rows(mod, d)[5], next_norm,
                                            name="mm_down")
    return h_out, u_next


def _wgrad(act, dout, key):
    return _mm(act, dout, ta=True, out_dtype=BF16, name="mm_wgrad_" + key)


def _ffn_bwd(dh, df, dgate2, sv, mod, g2, wts, hook):
    d = dh.shape[1]
    sc2, ga1 = _mod_rows(mod, d)[4], _mod_rows(mod, d)[2]
    dg, dup = _mm_down_t_swiglu(df, wts["w_down"], sv["g"], sv["up"], name="mm_down_t")
    grads = {"w_down": _wgrad(sv["a"], df, "w_down")}
    hook(dup)
    du2 = _mm_cat_k(dg, dup, wts["w_gate_up"], name="mm_gate_up_t")
    grads["w_gate_up"] = _mm_cat_n(sv["u2"], dg, dup, name="mm_wgrad_w_gate_up")
    dh_mid, dsh2, dsc2, dg2, dt, dgate1 = _rmsmod_bwd(du2, sv["h_mid"], g2, sc2, dh, sv["t"], ga1, name="rmsmod_bwd")
    return dh_mid, [dsh2, dsc2, dgate2], dg2, grads, dt, dgate1


def _mixer_bwd(dh_mid, dt, dgate1, sv, mod, g1, gains, wts, cos2, sin2, hook, below):
    seq, d = dh_mid.shape
    sc1 = _mod_rows(mod, d)[1]
    dmerged = _mm(dt, wts["w_out"], tb=True, name="mm_out_t")
    grads = {"w_out": _wgrad(sv["merged"], dt, "w_out")}
    dy_a, dy_b, dga, dgb = _merge_bwd(dmerged, sv["proj"], sv["y_a"], sv["y_b"], name="merge_bwd")
    do_a = _mm(dy_a, wts["w_branch_a"], tb=True, name="mm_branch_t")
    do_b = _mm(dy_b, wts["w_branch_b"], tb=True, name="mm_branch_t")
    grads["w_branch_a"] = _wgrad(sv["o_a"], dy_a, "w_branch_a")
    grads["w_branch_b"] = _wgrad(sv["o_b"], dy_b, "w_branch_b")
    dqb, dkb, dvb = _sb_bwd(sv["proj"], do_b, name="sb_bwd")
    hook(dqb, grads)
    comb = _combine_bwd(do_a, sv["os"], sv["lses"], name="combine_bwd")
    grads = {}
    dos, dls = comb[:3], comb[3:]
    dqs, dks, dvs = [], [], []
    for g, dil in enumerate(DILATIONS):
        length = seq // dil
        arrs, offs, o_view, lse_view = sv["views"][g]
        if dil == 1:
            dq, dk, dv = _dil_bwd(*arrs, offs, o_view, lse_view, dos[g], dls[g], length, dil, name=f"dil_bwd_{dil}")
        else:
            dq, dk, dv = _dil_bwd_strided(*arrs, sv["os"][g], sv["lses"][g], dos[g], dls[g], length, dil,
                                          name=f"dil_bwd_{dil}")
        dqs.append(dq)
        dks.append(dk)
        dvs.append(dv)
    dq_pre, dqn = _qkrope_bwd(dqs, sv["proj"], gains, 0, cos2, sin2, name="qkrope_bwd")
    dk_pre, dkn = _qkrope_bwd(dks, sv["proj"], gains, 1, cos2, sin2, name="qkrope_bwd")
    dgains = jnp.stack([dqn, dkn])
    dproj = _assemble([dq_pre, dk_pre] + dvs + [dqb, dkb, dvb, dga, dgb], name="assemble_dproj")
    du = _mm(dproj, wts["w_in"], tb=True, name="mm_in_t")
    grads["w_in"] = _wgrad(sv["u"], dproj, "w_in")
    dh_in, dsh1, dsc1, dg1, df, dgate2 = _rmsmod_bwd(du, sv["h_in"], g1, sc1, dh_mid, *(below or (None, None)),
                                                     name="rmsmod_bwd")
    return dh_in, [dsh1, dsc1, dgate1], dg1, dgains, grads, df, dgate2


def kernel(x, c, w_ada, b_ada, norm1_g, norm2_g, w_in, qn_g, kn_g, w_branch_a, w_branch_b, w_out, w_gate_up, w_down, loss_target, m_w_ada, m_b_ada, m_norm1_g, m_norm2_g, m_w_in, m_qn_g, m_kn_g, m_w_branch_a, m_w_branch_b, m_w_out, m_w_gate_up, m_w_down, v_w_ada, v_b_ada, v_norm1_g, v_norm2_g, v_w_in, v_qn_g, v_kn_g, v_w_branch_a, v_w_branch_b, v_w_out, v_w_gate_up, v_w_down):
    _ORDER["token"] = None
    seq, d = x.shape[1], x.shape[2]
    depth = w_in.shape[0]
    weights = dict(w_in=w_in, w_branch_a=w_branch_a, w_branch_b=w_branch_b, w_out=w_out, w_gate_up=w_gate_up,
                   w_down=w_down)
    moments_m = dict(w_in=m_w_in, w_branch_a=m_w_branch_a, w_branch_b=m_w_branch_b, w_out=m_w_out,
                     w_gate_up=m_w_gate_up, w_down=m_w_down)
    moments_v = dict(w_in=v_w_in, w_branch_a=v_w_branch_a, w_branch_b=v_w_branch_b, w_out=v_w_out,
                     w_gate_up=v_w_gate_up, w_down=v_w_down)
    xi, yi, ci = _coords()
    me = 4 * xi + 2 * yi + ci
    core = jnp.reshape(ci, (1,)).astype(jnp.int32)
    chip = jnp.reshape(2 * xi + yi, (1,)).astype(jnp.int32)

    ada_w = w_ada.shape[2]
    c_act = _small_allgather(c, name="comm_gather_c", silu=True).reshape(N_DEV, d)
    c_pad = jnp.concatenate([c_act, jnp.zeros_like(c_act)], axis=0).astype(BF16)
    bias = lax.dynamic_slice(b_ada, (0, me * ada_w), (depth, ada_w))
    mod_part = jnp.stack([_mm(c_pad, w_ada[l], name="mm_ada")[:N_DEV] for l in range(depth)]) + bias[:, None, :]
    mod_all = _small_allgather(mod_part.reshape(1, depth * N_DEV * ada_w), name="comm_gather_mod")
    mod_all = mod_all.reshape(N_DEV, depth, N_DEV, ada_w)
    mod_mine = lax.dynamic_index_in_dim(mod_all, me, axis=2, keepdims=False)
    mods = jnp.transpose(mod_mine, (1, 0, 2)).reshape(depth, 1, 6 * d)

    cos2, sin2 = _rope_tables(seq)
    gains = [jnp.stack([qn_g[l], kn_g[l]])[:, None, :] for l in range(depth)]
    g1s = [norm1_g[l][None] for l in range(depth)]
    g2s = [norm2_g[l][None] for l in range(depth)]

    me_arr = jnp.reshape(me, (1,)).astype(jnp.int32)

    def placed(keys, l):
        return [_cast_place(weights[k], l, SHARD_AXIS[k], me_arr, name="cast_place_" + k) for k in keys]

    def gather_of(keys, l, tag):
        return _SplitGather(placed(keys, l), [SHARD_AXIS[k] for k in keys], f"{tag}{l}")

    groups = [("w_in", 0, MIXER_W[:1]), ("rest", 0, MIXER_W[1:]), ("ffn", 0, FFN_W)]
    for l in range(1, depth):
        groups += [("mixer", l, MIXER_W), ("ffn", l, FFN_W)]
    gathers = {}

    def issue(some):
        for tag, l, keys in some:
            gathers[tag, l] = gather_of(keys, l, tag)
            gathers[tag, l].first(after=mods)

    issue(groups[:3])
    h = x[0]
    u = _rmsmod_fwd(h, *_norm_args(mods[0], g1s[0], 0, d), name="rmsmod_fwd")
    gathers["w_in", 0].forward(after=u)
    issue(groups[3:])
    wm = {"w_in": gathers["w_in", 0].finish(after=u)[0]}
    saved, full = [], []
    for l in range(depth):
        last = l + 1 == depth
        sv = _mixer_fwd_a(h, u, gains[l], wm["w_in"], cos2, sin2,
                          gathers["rest", 0].forward if l == 0 else lambda after: None)
        gathers["ffn", l].forward(after=sv["o_b"])
        if l == 0:
            wm.update(zip(MIXER_W[1:], gathers["rest", 0].finish(after=sv["o_b"])))
        h_mid = _mixer_fwd_b(sv, mods[l], g2s[l], wm)
        wf = dict(zip(FFN_W, gathers["ffn", l].finish(after=h_mid)))
        a = _ffn_fwd_a(sv, wf["w_gate_up"])
        if not last:
            gathers["mixer", l + 1].forward(after=a)
        h, u = _ffn_fwd_b(sv, mods[l], wf["w_down"],
                          None if last else _norm_args(mods[l + 1], g1s[l + 1], 0, d))
        saved.append(sv)
        full.append({**wm, **wf})
        if not last:
            wm = dict(zip(MIXER_W, gathers["mixer", l + 1].finish(after=h)))
    def ffn_gate(l):
        return saved[l]["f"], _mod_rows(mods[l], d)[5]

    loss_part, dh, df, dgate2 = _loss_fwd(h, loss_target[0], *ffn_gate(depth - 1), name="loss")
    loss = lax.psum(loss_part[0, 0], ("x", "y", "c"))

    pipe = _ReducePipeline(core)
    dmods, dg1s, dg2s, dgains = [None] * depth, [None] * depth, [None] * depth, [None] * depth
    for l in reversed(range(depth)):
        dh_mid, dmod_f, dg2s[l], grads, dt, dgate1 = _ffn_bwd(dh, df, dgate2, saved[l], mods[l], g2s[l], full[l],
                                                              pipe.tick)
        pipe.tick(dh_mid)
        pipe.add(FFN_W, grads, l)
        dh, dmod_m, dg1s[l], dgains[l], grads, df, dgate2 = _mixer_bwd(
            dh_mid, dt, dgate1, saved[l], mods[l], g1s[l], gains[l], full[l], cos2, sin2,
            lambda after, early, l=l: (pipe.tick(after), pipe.add(MIXER_W[1:], early, l)),
            ffn_gate(l - 1) if l > 0 else None)
        dmods[l] = jnp.concatenate(dmod_m + dmod_f, axis=1)
        pipe.tick(dh)
        pipe.add(MIXER_W[:1], grads, l)
    grad_x = dh[None]

    stacked = {}

    def update(items):
        for keys, l, sums, remote in items:
            for k, p_, r_ in zip(keys, sums, remote):
                stacked[k] = _adamw_sharded(p_, r_, chip, weights[k], moments_m[k], moments_v[k], l,
                                            stacked.get(k), name="adamw_" + k)

    ready = pipe.take_done()
    update([it for it in ready if it[0] != FFN_W])

    small = jnp.concatenate(
        dmods + dg1s + dg2s + [dgains[l][0] for l in range(depth)] + [dgains[l][1] for l in range(depth)], axis=1)
    small_all = _small_allgather(small, name="comm_gather_small")
    pipe.tick(small_all)
    update([it for it in ready if it[0] == FFN_W] + pipe.take_done())

    def pack(b, n1, n2, qn, kn):
        return jnp.concatenate([t_.reshape(1, -1) for t_ in (b, n1, n2, qn, kn)], axis=1)

    sg, sd, sm, sv_ = _adamw_replicated(small_all, pack(b_ada, norm1_g, norm2_g, qn_g, kn_g),
                                        pack(m_b_ada, m_norm1_g, m_norm2_g, m_qn_g, m_kn_g),
                                        pack(v_b_ada, v_norm1_g, v_norm2_g, v_qn_g, v_kn_g), name="adamw_replicated")

    def unpack(p):
        sizes = [depth * 6 * d, depth * d, depth * d, depth * HEAD_DIM, depth * HEAD_DIM]
        shapes = [b_ada.shape, norm1_g.shape, norm2_g.shape, qn_g.shape, kn_g.shape]
        out, off = [], 0
        for n, shp in zip(sizes, shapes):
            out.append(p[0, off:off + n].reshape(shp))
            off += n
        return dict(zip(("b_ada", "norm1_g", "norm2_g", "qn_g", "kn_g"), out))

    ug, ud, um, uv = unpack(sg), unpack(sd), unpack(sm), unpack(sv_)
    res = {k: dict(g=ug[k], d=ud[k], m=um[k], v=uv[k]) for k in ug}

    dmod_all = small_all[:, 0, :depth * 6 * d].reshape(N_DEV, depth, 6 * d)
    g_ada = None
    for l in range(depth):
        dm = lax.dynamic_slice(dmod_all[:, l, :], (0, me * ada_w), (N_DEV, ada_w))
        dm = jnp.concatenate([dm, jnp.zeros_like(dm)], axis=0).astype(BF16)
        g_ada = _mm(c_pad, dm, ta=True, name="mm_wgrad_ada", stack=(l, depth, g_ada))
    d_ada, m_ada, v_ada = _adamw_local(g_ada, w_ada, m_w_ada, v_w_ada, name="adamw_local")
    res["w_ada"] = dict(g=g_ada, d=d_ada, m=m_ada, v=v_ada)

    pipe.tick(d_ada)
    update(pipe.take_done())
    pipe.tick(d_ada, flush=True)
    update(pipe.take_done())
    for k, (g_, d_, m_, v_) in stacked.items():
        res[k] = dict(g=g_, d=d_, m=m_, v=v_)

    order = ("w_ada", "b_ada", "norm1_g", "norm2_g", "w_in", "qn_g", "kn_g", "w_branch_a", "w_branch_b", "w_out",
             "w_gate_up", "w_down")
    _ORDER["token"] = None
    return (loss, grad_x, *[res[k]["g"] for k in order], *[res[k]["d"] for k in order],
            *[res[k]["m"] for k in order], *[res[k]["v"] for k in order])
```

```python
import functools

import jax
import jax.numpy as jnp
from jax import lax
from jax.experimental import pallas as pl
from jax.experimental.pallas import tpu as pltpu

F32 = jnp.float32
BF16 = jnp.bfloat16

HEAD_DIM = 128
BLOCK = 128
DILATIONS = (1, 4, 16)
HEADS_PER_GROUP = 4
A_HEADS = 12
SB_HEADS = 4
GROUP_W = HEADS_PER_GROUP * HEAD_DIM
A_W = A_HEADS * HEAD_DIM
B_W = SB_HEADS * HEAD_DIM
OFF_QA, OFF_KA, OFF_VA = 0, A_W, 2 * A_W
OFF_QB, OFF_KB, OFF_VB = 3 * A_W, 3 * A_W + B_W, 3 * A_W + 2 * B_W
OFF_GATES = 3 * A_W + 3 * B_W
ROPE_THETA = 10000.0
EPS = 1e-6
ATT_SCALE = HEAD_DIM ** -0.5
MASKED = -1e30

ADAM_LR, ADAM_B1, ADAM_B2, ADAM_EPS, ADAM_WD, ADAM_STEP = 0.001, 0.9, 0.999, 1e-08, 0.01, 10

N_DEV = 8
N_CHIPS = 4
V7X_VMEM_LIMIT_BYTES = 56 * 1024 * 1024
ELEMWISE_BLOCK_BYTES = 2 * 1024 * 1024
MESH = pl.DeviceIdType.MESH

NN = (((1,), (0,)), ((), ()))
NT = (((1,), (1,)), ((), ()))
TN = (((0,), (0,)), ((), ()))


def _dot(a, b, dims=NN):
    return lax.dot_general(a, b, dims, preferred_element_type=F32)


def _tile(n, cap, mult=128):
    best = None
    for t in range(mult, min(n, cap) + 1, mult):
        if n % t == 0:
            best = t
    if best is None:
        assert n <= 2 * cap, (n, cap)
        return n
    return best


def _rows(r, c):
    return _tile(r, max(16, ELEMWISE_BLOCK_BYTES // (4 * c)), 16)


_ORDER = {"token": None}
TOKEN = jax.ShapeDtypeStruct((8, 128), F32)


def _take_token():
    prev = _ORDER["token"]
    return [] if prev is None else [prev]


def _pcall(body, *, name, out_shape, grid=None, in_specs=None, out_specs=None, scratch=(), aliases=None,
           prefetch=0):
    single = not isinstance(out_shape, (tuple, list))
    out_shapes = [out_shape] if single else list(out_shape)
    out_specs = [out_specs] if single else list(out_specs)
    extra = _take_token()
    n_in, n_extra, n_out = prefetch + len(in_specs), len(extra), len(out_shapes)

    def wrapped(*refs):
        token = refs[n_in + n_extra + n_out]
        token[...] = jnp.zeros_like(token)
        return body(*refs[:n_in], *refs[n_in + n_extra:n_in + n_extra + n_out], *refs[n_in + n_extra + n_out + 1:])

    in_specs = list(in_specs) + [pl.BlockSpec(memory_space=pl.ANY)] * n_extra
    if grid is None:
        out_specs.append(pl.BlockSpec(memory_space=pltpu.VMEM))
    else:
        out_specs.append(pl.BlockSpec(TOKEN.shape, lambda *_: (0, 0)))
    kwargs = dict(name=name, out_shape=out_shapes + [TOKEN], input_output_aliases=aliases or {},
                  compiler_params=pltpu.CompilerParams(vmem_limit_bytes=V7X_VMEM_LIMIT_BYTES))
    if prefetch:
        call = pl.pallas_call(wrapped, grid_spec=pltpu.PrefetchScalarGridSpec(
            num_scalar_prefetch=prefetch, grid=grid, in_specs=in_specs, out_specs=out_specs,
            scratch_shapes=list(scratch)), **kwargs)
    else:
        if grid is not None:
            kwargs["grid"] = grid
        call = pl.pallas_call(wrapped, in_specs=in_specs, out_specs=out_specs, scratch_shapes=list(scratch), **kwargs)

    def run(*args):
        outs = call(*args, *extra)
        _ORDER["token"] = outs[-1]
        return outs[0] if single else tuple(outs[:-1])

    return run


def _mm(a, b, *, name, ta=False, tb=False, out_dtype=F32, caps=(1024, 1024, 3072), stack=None):
    kdim, m = a.shape if ta else a.shape[::-1]
    n, k2 = b.shape if tb else b.shape[::-1]
    assert kdim == k2, (a.shape, b.shape, ta, tb)
    tm, tn, tk = _tile(m, caps[0]), _tile(n, caps[1]), _tile(kdim, caps[2])
    nk = kdim // tk
    dims = (((0 if ta else 1,), (1 if tb else 0,)), ((), ()))

    def body(*refs):
        a_ref, b_ref = refs[0], refs[1]
        part = _dot(a_ref[...].astype(BF16), b_ref[...].astype(BF16), dims)
        if nk == 1:
            o_ref = refs[-1]
            o_ref[...] = part.astype(o_ref.dtype)
            return
        o_ref, acc_ref = refs[-2], refs[-1]
        k = pl.program_id(2)

        @pl.when(k == 0)
        def _():
            acc_ref[...] = part

        @pl.when(k > 0)
        def _():
            acc_ref[...] += part

        @pl.when(k == nk - 1)
        def _():
            o_ref[...] = acc_ref[...].astype(o_ref.dtype)

    a_spec = (pl.BlockSpec((tk, tm), lambda i, j, k: (k, i)) if ta
              else pl.BlockSpec((tm, tk), lambda i, j, k: (i, k)))
    b_spec = (pl.BlockSpec((tn, tk), lambda i, j, k: (j, k)) if tb
              else pl.BlockSpec((tk, tn), lambda i, j, k: (k, j)))
    ins, in_specs, aliases = [a, b], [a_spec, b_spec], {}
    if stack is None:
        out_shape = jax.ShapeDtypeStruct((m, n), out_dtype)
        out_spec = pl.BlockSpec((tm, tn), lambda i, j, k: (i, j))
    else:
        layer, n_layers, buf = stack
        out_shape = jax.ShapeDtypeStruct((n_layers, m, n), out_dtype)
        out_spec = pl.BlockSpec((None, tm, tn), lambda i, j, k: (layer, i, j))
        if buf is not None:
            ins.append(buf)
            in_specs.append(pl.BlockSpec(memory_space=pl.ANY))
            aliases = {2: 0}
    scratch = [] if nk == 1 else [pltpu.VMEM((tm, tn), F32)]
    return _pcall(body, name=name, out_shape=out_shape, grid=(m // tm, n // tn, nk), in_specs=in_specs,
                  out_specs=out_spec, scratch=scratch, aliases=aliases)(*ins)


EPILOGUE_ROWS = 256


def _row_chunks(tm):
    return [slice(r, r + EPILOGUE_ROWS) for r in range(0, tm, EPILOGUE_ROWS)] if tm > EPILOGUE_ROWS else [slice(0, tm)]


def _mm_cat_k(a_lo, a_hi, b, *, name):
    m, f = a_lo.shape
    n = b.shape[0]
    tm, tn, tk = _tile(m, 1024), _tile(n, 1024), _tile(f, 3072)
    half = f // tk
    nk = 2 * half

    def body(lo_ref, hi_ref, b_ref, o_ref, acc_ref):
        k = pl.program_id(2)

        def accumulate(a_ref):
            part = _dot(a_ref[...], b_ref[...], NT)

            @pl.when(k == 0)
            def _():
                acc_ref[...] = part

            @pl.when(k > 0)
            def _():
                acc_ref[...] += part

        pl.when(k < half)(lambda: accumulate(lo_ref))
        pl.when(k >= half)(lambda: accumulate(hi_ref))

        @pl.when(k == nk - 1)
        def _():
            o_ref[...] = acc_ref[...]

    return _pcall(body, name=name, out_shape=jax.ShapeDtypeStruct((m, n), F32), grid=(m // tm, n // tn, nk),
                  in_specs=[pl.BlockSpec((tm, tk), lambda i, j, k: (i, jnp.minimum(k, half - 1))),
                            pl.BlockSpec((tm, tk), lambda i, j, k: (i, jnp.maximum(k - half, 0))),
                            pl.BlockSpec((tn, tk), lambda i, j, k: (j, k))],
                  out_specs=pl.BlockSpec((tm, tn), lambda i, j, k: (i, j)),
                  scratch=[pltpu.VMEM((tm, tn), F32)])(a_lo, a_hi, b)


def _mm_cat_n(a, b_lo, b_hi, *, name):
    s, m = a.shape
    f = b_lo.shape[1]
    tm, tn = _tile(m, 2048), _tile(f, 1024)
    half = f // tn

    def body(a_ref, lo_ref, hi_ref, o_ref):
        j = pl.program_id(1)

        @pl.when(j < half)
        def _():
            o_ref[...] = _dot(a_ref[...], lo_ref[...], TN).astype(BF16)

        @pl.when(j >= half)
        def _():
            o_ref[...] = _dot(a_ref[...], hi_ref[...], TN).astype(BF16)

    return _pcall(body, name=name, out_shape=jax.ShapeDtypeStruct((m, 2 * f), BF16), grid=(m // tm, 2 * half),
                  in_specs=[pl.BlockSpec((s, tm), lambda i, j: (0, i)),
                            pl.BlockSpec((s, tn), lambda i, j: (0, jnp.minimum(j, half - 1))),
                            pl.BlockSpec((s, tn), lambda i, j: (0, jnp.maximum(j - half, 0)))],
                  out_specs=pl.BlockSpec((tm, tn), lambda i, j: (i, j)))(a, b_lo, b_hi)


def _mm_resid_norm(a, w, h, gate, norm, *, name):
    s, kdim = a.shape
    d = w.shape[1]
    tk = _tile(kdim, 2048)
    nk = kdim // tk
    tm = _tile(s, 256 if nk == 1 else 512)

    def body(*refs):
        a_ref, w_ref, h_ref, gate_ref = refs[:4]
        outs = refs[7:] if norm is not None else refs[4:]

        def finish(rows, t):
            hn = h_ref[rows, :] + gate_ref[...] * t
            outs[0][rows, :] = hn
            outs[1][rows, :] = t.astype(BF16)
            if norm is not None:
                g_ref, sc_ref, sh_ref = refs[4:7]
                r = lax.rsqrt(jnp.mean(hn * hn, axis=-1, keepdims=True) + EPS)
                outs[2][rows, :] = (((hn * r) * g_ref[...]) * (1.0 + sc_ref[...]) + sh_ref[...]).astype(BF16)

        if nk == 1:
            for rows in _row_chunks(tm):
                finish(rows, _dot(a_ref[rows, :], w_ref[...]))
            return
        acc_ref = refs[-1]
        k = pl.program_id(1)

        @pl.when(k == 0)
        def _():
            acc_ref[...] = _dot(a_ref[...], w_ref[...])

        @pl.when(jnp.logical_and(k > 0, k < nk - 1))
        def _():
            acc_ref[...] += _dot(a_ref[...], w_ref[...])

        @pl.when(k == nk - 1)
        def _():
            for rows in _row_chunks(tm):
                finish(rows, acc_ref[rows, :] + _dot(a_ref[rows, :], w_ref[...]))

    row = pl.BlockSpec((tm, d), lambda i, k: (i, 0))
    vec = pl.BlockSpec((1, d), lambda i, k: (0, 0))
    in_specs = [pl.BlockSpec((tm, tk), lambda i, k: (i, k)), pl.BlockSpec((tk, d), lambda i, k: (k, 0)), row, vec]
    args = [a, w, h, gate]
    out_shape = [jax.ShapeDtypeStruct((s, d), F32), jax.ShapeDtypeStruct((s, d), BF16)]
    if norm is not None:
        in_specs += [vec, vec, vec]
        args += list(norm)
        out_shape.append(jax.ShapeDtypeStruct((s, d), BF16))
    outs = _pcall(body, name=name, out_shape=tuple(out_shape), grid=(s // tm, nk), in_specs=in_specs,
                  out_specs=(row,) * len(out_shape), scratch=[] if nk == 1 else [pltpu.VMEM((tm, d), F32)])(*args)
    return outs if norm is not None else (*outs, None)


def _mm_merge(o_a, o_b, w_a, w_b, proj, *, name):
    s = o_a.shape[0]
    d = w_a.shape[1]
    tm = _tile(s, 512)
    ga_blk = OFF_GATES // d

    def body(oa_ref, ob_ref, wa_ref, wb_ref, ga_ref, gb_ref, m_ref, ya_ref, yb_ref):
        for rows in _row_chunks(tm):
            ya, yb = _dot(oa_ref[rows, :], wa_ref[...]), _dot(ob_ref[rows, :], wb_ref[...])
            m_ref[rows, :] = (jax.nn.sigmoid(ga_ref[rows, :]) * ya
                              + jax.nn.sigmoid(gb_ref[rows, :]) * yb).astype(BF16)
            ya_ref[rows, :] = ya.astype(BF16)
            yb_ref[rows, :] = yb.astype(BF16)

    row = pl.BlockSpec((tm, d), lambda i: (i, 0))
    act = pl.BlockSpec((tm, o_a.shape[1]), lambda i: (i, 0))
    wspec = pl.BlockSpec(w_a.shape, lambda i: (0, 0))
    shp = jax.ShapeDtypeStruct((s, d), BF16)
    return _pcall(body, name=name, out_shape=(shp, shp, shp), grid=(s // tm,),
                  in_specs=[act, act, wspec, wspec, pl.BlockSpec((tm, d), lambda i: (i, ga_blk)),
                            pl.BlockSpec((tm, d), lambda i: (i, ga_blk + 1))],
                  out_specs=(row, row, row))(o_a, o_b, w_a, w_b, proj, proj)


def _mm_out_t_merge(dt, w_out, proj, y_a, y_b, *, name):
    s, d = dt.shape
    tm, tn = _tile(s, 1024), _tile(d, 512)
    ga_blk = OFF_GATES // tn

    def body(dt_ref, w_ref, ga_ref, gb_ref, ya_ref, yb_ref, dya_ref, dyb_ref, dga_ref, dgb_ref):
        w = w_ref[...]
        for rows in _row_chunks(tm):
            dm = _dot(dt_ref[rows, :], w, NT)
            sa, sb = jax.nn.sigmoid(ga_ref[rows, :]), jax.nn.sigmoid(gb_ref[rows, :])
            dya_ref[rows, :] = (dm * sa).astype(BF16)
            dyb_ref[rows, :] = (dm * sb).astype(BF16)
            dga_ref[rows, :] = (dm * ya_ref[rows, :] * (sa * (1.0 - sa))).astype(BF16)
            dgb_ref[rows, :] = (dm * yb_ref[rows, :] * (sb * (1.0 - sb))).astype(BF16)

    tile = pl.BlockSpec((tm, tn), lambda i, j: (i, j))
    shp = jax.ShapeDtypeStruct((s, d), BF16)
    return _pcall(body, name=name, out_shape=(shp,) * 4, grid=(s // tm, d // tn),
                  in_specs=[pl.BlockSpec((tm, d), lambda i, j: (i, 0)), pl.BlockSpec((tn, d), lambda i, j: (j, 0)),
                            pl.BlockSpec((tm, tn), lambda i, j: (i, ga_blk + j)),
                            pl.BlockSpec((tm, tn), lambda i, j: (i, ga_blk + d // tn + j)), tile, tile],
                  out_specs=(tile,) * 4)(dt, w_out, proj, proj, y_a, y_b)


def _mm_down_t_swiglu(df, w_down, g, u, *, name):
    s, d = df.shape
    f = w_down.shape[0]
    tm, tn = _tile(s, 1024), _tile(f, 512)

    def body(df_ref, w_ref, g_ref, u_ref, dg_ref, du_ref):
        w = w_ref[...]
        for rows in _row_chunks(tm):
            da = _dot(df_ref[rows, :], w, NT)
            gf = g_ref[rows, :].astype(F32)
            sg = jax.nn.sigmoid(gf)
            dg_ref[rows, :] = (da * u_ref[rows, :].astype(F32) * (sg * (1.0 + gf * (1.0 - sg)))).astype(BF16)
            du_ref[rows, :] = (da * (gf * sg)).astype(BF16)

    tile = pl.BlockSpec((tm, tn), lambda i, j: (i, j))
    shp = jax.ShapeDtypeStruct((s, f), BF16)
    return _pcall(body, name=name, out_shape=(shp, shp), grid=(s // tm, f // tn),
                  in_specs=[pl.BlockSpec((tm, d), lambda i, j: (i, 0)), pl.BlockSpec((tn, d), lambda i, j: (j, 0)),
                            tile, tile],
                  out_specs=(tile, tile))(df, w_down, g, u)


def _rmsmod_fwd(h, g, scale, shift, *, name):
    s, d = h.shape
    ts = _rows(s, d)

    def body(h_ref, g_ref, sc_ref, sh_ref, u_ref):
        hf = h_ref[...]
        r = lax.rsqrt(jnp.mean(hf * hf, axis=-1, keepdims=True) + EPS)
        u_ref[...] = (((hf * r) * g_ref[...]) * (1.0 + sc_ref[...]) + sh_ref[...]).astype(BF16)

    row = pl.BlockSpec((ts, d), lambda i: (i, 0))
    vec = pl.BlockSpec((1, d), lambda i: (0, 0))
    return _pcall(body, name=name, out_shape=jax.ShapeDtypeStruct((s, d), BF16), grid=(s // ts,),
                  in_specs=[row, vec, vec, vec], out_specs=row)(h, g, scale, shift)


def _gate_bwd(dhf, t_ref, gate_ref, dt_ref, dgate_ref):
    dt_ref[...] = (dhf * gate_ref[...]).astype(BF16)
    dgate_ref[...] += jnp.sum(dhf * t_ref[...], axis=0, keepdims=True)


def _rmsmod_bwd(du, h, g, scale, dres, t, gate, *, name):
    s, d = h.shape
    ts = _rows(s, d)
    chain = t is not None

    def body(*refs):
        du_ref, h_ref, g_ref, sc_ref, dres_ref = refs[:5]
        dh_ref, dsh_ref, dsc_ref, dg_ref = refs[-6:-2] if chain else refs[-4:]
        sums = (dsh_ref, dsc_ref, dg_ref) + ((refs[-1],) if chain else ())

        @pl.when(pl.program_id(0) == 0)
        def _():
            for ref in sums:
                ref[...] = jnp.zeros_like(ref)

        hf, duf, gain = h_ref[...], du_ref[...], g_ref[...]
        r = lax.rsqrt(jnp.mean(hf * hf, axis=-1, keepdims=True) + EPS)
        xh = hf * r
        dn = duf * (1.0 + sc_ref[...])
        dsh_ref[...] += jnp.sum(duf, axis=0, keepdims=True)
        dsc_ref[...] += jnp.sum(duf * (xh * gain), axis=0, keepdims=True)
        dg_ref[...] += jnp.sum(dn * xh, axis=0, keepdims=True)
        dxh = dn * gain
        dh = dres_ref[...] + r * (dxh - xh * jnp.mean(dxh * xh, axis=-1, keepdims=True))
        dh_ref[...] = dh
        if chain:
            _gate_bwd(dh, refs[5], refs[6], refs[-2], refs[-1])

    row = pl.BlockSpec((ts, d), lambda i: (i, 0))
    vec = pl.BlockSpec((1, d), lambda i: (0, 0))
    vshape = jax.ShapeDtypeStruct((1, d), F32)
    out_shape, out_specs = [jax.ShapeDtypeStruct((s, d), F32), vshape, vshape, vshape], [row, vec, vec, vec]
    in_specs, args = [row, row, vec, vec, row], [du, h, g, scale, dres]
    if chain:
        in_specs, args = in_specs + [row, vec], args + [t, gate]
        out_shape, out_specs = out_shape + [jax.ShapeDtypeStruct((s, d), BF16), vshape], out_specs + [row, vec]
    outs = _pcall(body, name=name, out_shape=tuple(out_shape), grid=(s // ts,), in_specs=in_specs,
                  out_specs=tuple(out_specs))(*args)
    return outs if chain else (*outs, None, None)


def _mm_swiglu(u2, w_gate_up, *, name):
    s, d = u2.shape
    f = w_gate_up.shape[1] // 2
    tm, tn = _tile(s, 1024), _tile(f, 512)
    nj = f // tn

    def body(x_ref, wg_ref, wu_ref, a_ref, g_ref, u_ref):
        for rows in _row_chunks(tm):
            x = x_ref[rows, :]
            gf, uf = _dot(x, wg_ref[...]), _dot(x, wu_ref[...])
            a_ref[rows, :] = ((gf * jax.nn.sigmoid(gf)) * uf).astype(BF16)
            g_ref[rows, :] = gf.astype(BF16)
            u_ref[rows, :] = uf.astype(BF16)

    out = pl.BlockSpec((tm, tn), lambda i, j: (i, j))
    shp = jax.ShapeDtypeStruct((s, f), BF16)
    return _pcall(body, name=name, out_shape=(shp, shp, shp), grid=(s // tm, nj),
                  in_specs=[pl.BlockSpec((tm, d), lambda i, j: (i, 0)), pl.BlockSpec((d, tn), lambda i, j: (0, j)),
                            pl.BlockSpec((d, tn), lambda i, j: (0, nj + j))],
                  out_specs=(out, out, out))(u2, w_gate_up, w_gate_up)


def _loss_fwd(y, tgt, t, gate, *, name):
    s, d = y.shape
    ts = _rows(s, d)

    def body(y_ref, tgt_ref, t_ref, gate_ref, l_ref, dy_ref, dt_ref, dgate_ref):
        @pl.when(pl.program_id(0) == 0)
        def _():
            l_ref[...] = jnp.zeros_like(l_ref)
            dgate_ref[...] = jnp.zeros_like(dgate_ref)

        e = y_ref[...] - tgt_ref[...]
        dy = e * (1.0 / d)
        dy_ref[...] = dy
        per_tok = jnp.sum(e * e, axis=1, keepdims=True) * (1.0 / d)
        l_ref[...] += 0.5 * jnp.sum(per_tok, axis=0, keepdims=True)
        _gate_bwd(dy, t_ref, gate_ref, dt_ref, dgate_ref)

    row = pl.BlockSpec((ts, d), lambda i: (i, 0))
    vec = pl.BlockSpec((1, d), lambda i: (0, 0))
    return _pcall(body, name=name,
                  out_shape=(jax.ShapeDtypeStruct((1, 128), F32), jax.ShapeDtypeStruct((s, d), F32),
                             jax.ShapeDtypeStruct((s, d), BF16), jax.ShapeDtypeStruct((1, d), F32)),
                  grid=(s // ts,), in_specs=[row, row, row, vec],
                  out_specs=(pl.BlockSpec((1, 128), lambda i: (0, 0)), row, row, vec))(y, tgt, t, gate)


def _rope_tables(seq):
    inv = jnp.power(ROPE_THETA, -jnp.arange(0, HEAD_DIM, 2, dtype=F32) / HEAD_DIM)
    ang = jnp.arange(seq, dtype=F32)[:, None] * inv[None, :]
    cos, sin = jnp.cos(ang), jnp.sin(ang)
    return jnp.concatenate([cos, cos], axis=1), jnp.concatenate([-sin, sin], axis=1)


def _qkrope_fwd(proj, gains, cos2, sin2, *, name):
    s = proj.shape[0]
    ts = _rows(s, A_W)

    def body(x_ref, g_ref, c_ref, s_ref, o_ref):
        gain, cos, sin = g_ref[...], c_ref[...], s_ref[...]
        for h in range(A_HEADS):
            lanes = slice(h * HEAD_DIM, (h + 1) * HEAD_DIM)
            x = x_ref[:, lanes]
            y = (x * lax.rsqrt(jnp.mean(x * x, axis=-1, keepdims=True) + EPS)) * gain
            o_ref[:, lanes] = (y * cos + pltpu.roll(y, HEAD_DIM // 2, 1) * sin).astype(BF16)

    heads = pl.BlockSpec((ts, A_W), lambda i, j: (i, j))
    tab = pl.BlockSpec((ts, HEAD_DIM), lambda i, j: (i, 0))
    gain = pl.BlockSpec((None, 1, HEAD_DIM), lambda i, j: (j, 0, 0))
    return _pcall(body, name=name, out_shape=jax.ShapeDtypeStruct((s, 2 * A_W), BF16),
                  grid=(s // ts, 2), in_specs=[heads, gain, tab, tab], out_specs=heads)(
                      proj, gains, cos2, sin2)


def _qkrope_bwd(d_groups, proj, gains, which, cos2, sin2, *, name):
    s = proj.shape[0]
    ts = _rows(s, A_W)

    def body(d0_ref, d1_ref, d2_ref, x_ref, g_ref, c_ref, s_ref, dx_ref, dg_ref):
        @pl.when(pl.program_id(0) == 0)
        def _():
            dg_ref[...] = jnp.zeros_like(dg_ref)

        gain, cos, sin = g_ref[...], c_ref[...], s_ref[...]
        dg = jnp.zeros((1, HEAD_DIM), F32)
        for h in range(A_HEADS):
            lanes = slice(h * HEAD_DIM, (h + 1) * HEAD_DIM)
            slot = slice((h % HEADS_PER_GROUP) * HEAD_DIM, (h % HEADS_PER_GROUP + 1) * HEAD_DIM)
            dout = (d0_ref, d1_ref, d2_ref)[h // HEADS_PER_GROUP][:, slot]
            dy = dout * cos + pltpu.roll(dout * sin, HEAD_DIM // 2, 1)
            x = x_ref[:, lanes]
            r = lax.rsqrt(jnp.mean(x * x, axis=-1, keepdims=True) + EPS)
            xh = x * r
            dg = dg + jnp.sum(dy * xh, axis=0, keepdims=True)
            dxh = dy * gain
            dx_ref[:, lanes] = (r * (dxh - xh * jnp.mean(dxh * xh, axis=-1, keepdims=True))).astype(BF16)
        dg_ref[...] += dg

    group = pl.BlockSpec((ts, GROUP_W), lambda i: (i, 0))
    tab = pl.BlockSpec((ts, HEAD_DIM), lambda i: (i, 0))
    gain = pl.BlockSpec((None, 1, HEAD_DIM), lambda i: (which, 0, 0))
    return _pcall(body, name=name,
                  out_shape=(jax.ShapeDtypeStruct((s, A_W), BF16), jax.ShapeDtypeStruct((1, HEAD_DIM), F32)),
                  grid=(s // ts,),
                  in_specs=[group, group, group, pl.BlockSpec((ts, A_W), lambda i: (i, which)), gain, tab, tab],
                  out_specs=(pl.BlockSpec((ts, A_W), lambda i: (i, 0)), pl.BlockSpec((1, HEAD_DIM), lambda i: (0, 0))))(
                      *d_groups, proj, gains, cos2, sin2)


def _assemble(pieces, *, name):
    s = pieces[0].shape[0]
    widths = [p.shape[1] for p in pieces]
    total = sum(widths)
    ts = _rows(s, total // 2)

    def body(*refs):
        o_ref, off = refs[-1], 0
        for x_ref, w in zip(refs[:-1], widths):
            o_ref[:, off:off + w] = x_ref[...].astype(BF16)
            off += w

    return _pcall(body, name=name, out_shape=jax.ShapeDtypeStruct((s, total), BF16), grid=(s // ts,),
                  in_specs=[pl.BlockSpec((ts, w), lambda i: (i, 0)) for w in widths],
                  out_specs=pl.BlockSpec((ts, total), lambda i: (i, 0)))(*pieces)


def _block_rows(blk):
    if isinstance(blk, int):
        return pl.ds(blk * BLOCK, BLOCK)
    return pl.ds(pl.multiple_of(blk * BLOCK, BLOCK), BLOCK)


def _band_window(n, length):
    width = min(2 * BLOCK, length)
    row = lax.broadcasted_iota(jnp.int32, (BLOCK, width), 0)
    col = lax.broadcasted_iota(jnp.int32, (BLOCK, width), 1)
    if width == BLOCK:
        return pl.ds(0, BLOCK), col <= row
    first = n - 1 if isinstance(n, int) else jnp.maximum(n - 1, 0)
    first = max(first, 0) if isinstance(first, int) else first
    dist = row - col + (n - first) * BLOCK
    start = first * BLOCK if isinstance(first, int) else pl.multiple_of(first * BLOCK, BLOCK)
    return pl.ds(start, width), jnp.logical_and(dist >= 0, dist <= BLOCK)


def _dil_fwd(q_arr, k_arr, v_arr, offs, length, dil, *, name):
    nj, nb = dil * HEADS_PER_GROUP, length // BLOCK
    ju, nq = (HEADS_PER_GROUP, 2) if nb > 1 else (2 * HEADS_PER_GROUP, 1)
    qo, ko, vo = (off // ju for off in offs)
    assert all(off % ju == 0 for off in offs) and nb % nq == 0 and nj % ju == 0

    def body(q_ref, k_ref, v_ref, o_ref, l_ref):
        for qq in range(nq):
            qrows = slice(qq * BLOCK, (qq + 1) * BLOCK)
            rows, mask = _band_window(pl.program_id(1) * nq + qq, length)
            for cb in range(ju):
                lanes = slice(cb * HEAD_DIM, (cb + 1) * HEAD_DIM)
                sc = _dot(q_ref[qrows, lanes].astype(BF16), k_ref[rows, lanes].astype(BF16), NT) * ATT_SCALE
                sc = jnp.where(mask, sc, MASKED)
                m = sc.max(axis=-1, keepdims=True)
                p = jnp.exp(sc - m)
                den = jnp.sum(p, axis=-1, keepdims=True)
                acc = _dot(p.astype(BF16), v_ref[rows, lanes].astype(BF16))
                o_ref[qrows, lanes] = acc / den
                l_ref[qrows, lanes] = jnp.broadcast_to(m + jnp.log(den), (BLOCK, HEAD_DIM))

    qspec = pl.BlockSpec((nq * BLOCK, ju * HEAD_DIM), lambda j, n: (n, qo + j))
    kspec = pl.BlockSpec((length, ju * HEAD_DIM), lambda j, n: (0, ko + j))
    vspec = pl.BlockSpec((length, ju * HEAD_DIM), lambda j, n: (0, vo + j))
    ospec = pl.BlockSpec((nq * BLOCK, ju * HEAD_DIM), lambda j, n: (n, j))
    shp = jax.ShapeDtypeStruct((length, nj * HEAD_DIM), F32)
    return _pcall(body, name=name, out_shape=(shp, shp), grid=(nj // ju, nb // nq), in_specs=[qspec, kspec, vspec],
                  out_specs=(ospec, ospec))(q_arr, k_arr, v_arr)


def _dil_bwd(q_arr, k_arr, v_arr, offs, o, lse, do, dlse, length, dil, *, name):
    nj, nb = dil * HEADS_PER_GROUP, length // BLOCK
    ju = 2 * HEADS_PER_GROUP if length <= 4 * BLOCK else 2
    qo, ko, vo = (off // ju for off in offs)
    assert all(off % ju == 0 for off in offs)

    def body(q_ref, k_ref, v_ref, o_ref, l_ref, do_ref, dl_ref, dq_ref, dk_ref, dv_ref):
        dk_ref[...] = jnp.zeros_like(dk_ref)
        dv_ref[...] = jnp.zeros_like(dv_ref)

        def step(n, carry):
            qrows = _block_rows(n)
            rows, mask = _band_window(n, length)
            for cb in range(ju):
                lanes = slice(cb * HEAD_DIM, (cb + 1) * HEAD_DIM)
                q = q_ref[qrows, lanes].astype(BF16)
                dof = do_ref[qrows, lanes]
                dob = dof.astype(BF16)
                lse_c = l_ref[qrows, lanes][:, :1]
                shift = dl_ref[qrows, lanes][:, :1] - jnp.sum(dof * o_ref[qrows, lanes], axis=-1, keepdims=True)
                kk, vv = k_ref[rows, lanes].astype(BF16), v_ref[rows, lanes].astype(BF16)
                sc = _dot(q, kk, NT) * ATT_SCALE
                p = jnp.where(mask, jnp.exp(sc - lse_c), 0.0)
                ds = (p * (_dot(dob, vv, NT) + shift)).astype(BF16)
                dq_ref[qrows, lanes] = _dot(ds, kk) * ATT_SCALE
                dk_ref[rows, lanes] += _dot(ds, q, TN) * ATT_SCALE
                dv_ref[rows, lanes] += _dot(p.astype(BF16), dob, TN)
            return carry

        if nb == 1:
            step(0, 0)
        else:
            lax.fori_loop(0, nb, step, 0)

    def col(off):
        return pl.BlockSpec((length, ju * HEAD_DIM), lambda j: (0, off + j))

    shp = jax.ShapeDtypeStruct((length, nj * HEAD_DIM), F32)
    return _pcall(body, name=name, out_shape=(shp, shp, shp), grid=(nj // ju,),
                  in_specs=[col(qo), col(ko), col(vo), col(0), col(0), col(0), col(0)],
                  out_specs=(col(0), col(0), col(0)))(q_arr, k_arr, v_arr, o, lse, do, dlse)


def _dil_bwd_strided(q_view, k_view, v_view, o, lse, do, dlse, length, dil, *, name):
    nb = length // BLOCK
    seq = length * dil
    width = min(2 * BLOCK, length)
    rp = 4
    assert dil % rp == 0

    def body(*refs):
        q_refs, k_refs, v_refs = refs[:rp], refs[rp:2 * rp], refs[2 * rp:3 * rp]
        o_ref, l_ref, do_ref, dl_ref, dq_ref, dk_ref, dv_ref = refs[3 * rp:]
        rgroup = pl.program_id(1)

        @pl.when(rgroup == 0)
        def _():
            dk_ref[...] = jnp.zeros_like(dk_ref)
            dv_ref[...] = jnp.zeros_like(dv_ref)

        def step(n, carry):
            rows, mask = _band_window(n, length)
            first = 0 if width == BLOCK else jnp.maximum(n - 1, 0)
            for rr in range(rp):
                r = rgroup * rp + rr
                tok_q = pl.ds(n * (BLOCK * dil) + r, BLOCK, stride=dil)
                tok_k = pl.ds(first * (BLOCK * dil) + r, width, stride=dil)
                q = q_refs[rr][_block_rows(n), :].astype(BF16)
                dof = do_ref[tok_q, :]
                dob = dof.astype(BF16)
                lse_c = l_ref[tok_q, :][:, :1]
                shift = dl_ref[tok_q, :][:, :1] - jnp.sum(dof * o_ref[tok_q, :], axis=-1, keepdims=True)
                kk, vv = k_refs[rr][rows, :].astype(BF16), v_refs[rr][rows, :].astype(BF16)
                sc = _dot(q, kk, NT) * ATT_SCALE
                p = jnp.where(mask, jnp.exp(sc - lse_c), 0.0)
                ds = (p * (_dot(dob, vv, NT) + shift)).astype(BF16)
                dq_ref[tok_q, :] = _dot(ds, kk) * ATT_SCALE
                dk_ref[tok_k, :] += _dot(ds, q, TN) * ATT_SCALE
                dv_ref[tok_k, :] += _dot(p.astype(BF16), dob, TN)
            return carry

        if nb == 1:
            step(0, 0)
        else:
            lax.fori_loop(0, nb, step, 0)

    def view(rr):
        return pl.BlockSpec((length, HEAD_DIM), lambda h, r: (0, (r * rp + rr) * HEADS_PER_GROUP + h))

    views = [view(rr) for rr in range(rp)]
    nat = pl.BlockSpec((seq, HEAD_DIM), lambda h, r: (0, h))
    shp = jax.ShapeDtypeStruct((seq, GROUP_W), F32)
    return _pcall(body, name=name, out_shape=(shp, shp, shp), grid=(HEADS_PER_GROUP, dil // rp),
                  in_specs=views * 3 + [nat, nat, nat, nat], out_specs=(nat, nat, nat))(
                      *[q_view] * rp, *[k_view] * rp, *[v_view] * rp, o, lse, do, dlse)


def _combine_weights(l_refs):
    ls = [r[...] for r in l_refs]
    m = jnp.maximum(jnp.maximum(ls[0], ls[1]), ls[2])
    es = [jnp.exp(l - m) for l in ls]
    den = es[0] + es[1] + es[2]
    return [e / den for e in es]


def _combine_fwd(os_, lses, *, name):
    s = os_[0].shape[0]
    ts = _rows(s, GROUP_W)

    def body(o0, o1, o2, l0, l1, l2, out_ref):
        w = _combine_weights((l0, l1, l2))
        out_ref[...] = (w[0] * o0[...] + w[1] * o1[...] + w[2] * o2[...]).astype(BF16)

    row = pl.BlockSpec((ts, GROUP_W), lambda i: (i, 0))
    return _pcall(body, name=name, out_shape=jax.ShapeDtypeStruct((s, GROUP_W), BF16), grid=(s // ts,),
                  in_specs=[row] * 6, out_specs=row)(*os_, *lses)


def _combine_bwd(do_a, os_, lses, *, name):
    s = do_a.shape[0]
    ts = _rows(s, GROUP_W)

    def body(d_ref, o0, o1, o2, l0, l1, l2, do0, do1, do2, dl0, dl1, dl2):
        w = _combine_weights((l0, l1, l2))
        d = d_ref[...]
        og = [o0[...], o1[...], o2[...]]
        oa = w[0] * og[0] + w[1] * og[1] + w[2] * og[2]
        ta = jnp.sum(d * oa, axis=-1, keepdims=True)
        for g, (do_ref, dl_ref) in enumerate(((do0, dl0), (do1, dl1), (do2, dl2))):
            do_ref[...] = w[g] * d
            dl_ref[...] = w[g] * (jnp.sum(d * og[g], axis=-1, keepdims=True) - ta)

    head = pl.BlockSpec((ts, HEAD_DIM), lambda i, h: (i, h))
    shp = jax.ShapeDtypeStruct((s, GROUP_W), F32)
    return _pcall(body, name=name, out_shape=(shp,) * 6, grid=(s // ts, HEADS_PER_GROUP),
                  in_specs=[head] * 7, out_specs=(head,) * 6)(do_a, *os_, *lses)


def _dot_exact(x, ones_mask):
    hi = x.astype(BF16)
    r1 = x - hi.astype(F32)
    mid = r1.astype(BF16)
    lo = (r1 - mid.astype(F32)).astype(BF16)
    return _dot(hi, ones_mask) + _dot(mid, ones_mask) + _dot(lo, ones_mask)


SB_QROWS = 2 * BLOCK
SB_UNROLL = 4
SB_HEADS_PER_STEP = 2
SB_LANES = [slice(hh * HEAD_DIM, (hh + 1) * HEAD_DIM) for hh in range(SB_HEADS_PER_STEP)]


def _sb_mask(j, i):
    row = lax.broadcasted_iota(jnp.int32, (SB_QROWS, BLOCK), 0)
    col = lax.broadcasted_iota(jnp.int32, (SB_QROWS, BLOCK), 1)
    return col + (j * BLOCK - i * SB_QROWS) < row


def _sb_steps(i):
    return ((i + 1) * (SB_QROWS // BLOCK) + SB_UNROLL - 1) // SB_UNROLL


def _sb_scores(q, kk, j, i, masked):
    mask = _sb_mask(j, i) if masked else None
    z = _dot(q, kk, NT) * ATT_SCALE
    sp = jnp.log(1.0 + jnp.exp(-jnp.abs(z)))
    log_beta = jnp.minimum(z, 0.0) - sp
    log_1mb = jnp.minimum(-z, 0.0) - sp
    if masked:
        log_1mb = jnp.where(mask, log_1mb, 0.0)
    return z, log_beta, log_1mb, mask


def _sb_weights(log_beta, log_1mb, mask, run, upper):
    a = jnp.exp(log_beta + (run + _dot_exact(log_1mb, upper)))
    return a if mask is None else jnp.where(mask, a, 0.0)


def _sb_peeled(nsteps, make_step, init, masked_first):
    if masked_first:
        return lax.fori_loop(1, nsteps, make_step(False), make_step(True)(0, init))
    return make_step(True)(nsteps - 1, lax.fori_loop(0, nsteps - 1, make_step(False), init))


def _tri(strict_lower):
    row = lax.broadcasted_iota(jnp.int32, (BLOCK, BLOCK), 0)
    col = lax.broadcasted_iota(jnp.int32, (BLOCK, BLOCK), 1)
    return ((row > col) if strict_lower else (row < col)).astype(BF16)


def _sb_fwd(proj, *, name):
    s = proj.shape[0]
    assert s % (BLOCK * SB_UNROLL) == 0 and s % SB_QROWS == 0

    def body(q_ref, k_ref, v_ref, o_ref):
        i = pl.program_id(1)
        qs = [q_ref[:, lanes].astype(BF16) for lanes in SB_LANES]
        upper = _tri(True)
        nsteps = _sb_steps(i)

        def make_step(masked):
            def step(t, carry):
                carry = list(carry)
                for b in reversed(range(SB_UNROLL)):
                    j = (nsteps - 1 - t) * SB_UNROLL + b
                    rows = _block_rows(j)
                    for hh, lanes in enumerate(SB_LANES):
                        acc, run = carry[hh]
                        _, log_beta, log_1mb, mask = _sb_scores(qs[hh], k_ref[rows, lanes].astype(BF16), j, i, masked)
                        a = _sb_weights(log_beta, log_1mb, mask, run, upper)
                        carry[hh] = (acc + _dot(a.astype(BF16), v_ref[rows, lanes].astype(BF16)),
                                     run + jnp.sum(log_1mb, axis=-1, keepdims=True))
                return tuple(carry)
            return step

        zero = (jnp.zeros((SB_QROWS, HEAD_DIM), F32), jnp.zeros((SB_QROWS, 1), F32))
        for lanes, (acc, _) in zip(SB_LANES, _sb_peeled(nsteps, make_step, (zero,) * SB_HEADS_PER_STEP, True)):
            o_ref[:, lanes] = acc.astype(BF16)

    width = SB_HEADS_PER_STEP * HEAD_DIM
    qb, kb, vb = (off // width for off in (OFF_QB, OFF_KB, OFF_VB))
    return _pcall(body, name=name, out_shape=jax.ShapeDtypeStruct((s, B_W), BF16),
                  grid=(SB_HEADS // SB_HEADS_PER_STEP, s // SB_QROWS),
                  in_specs=[pl.BlockSpec((SB_QROWS, width), lambda h, i: (i, qb + h)),
                            pl.BlockSpec((s, width), lambda h, i: (0, kb + h)),
                            pl.BlockSpec((s, width), lambda h, i: (0, vb + h))],
                  out_specs=pl.BlockSpec((SB_QROWS, width), lambda h, i: (i, h)))(proj, proj, proj)


def _sb_bwd(proj, do_b, *, name):
    s = proj.shape[0]
    assert s % (BLOCK * SB_UNROLL) == 0 and s % SB_QROWS == 0
    nkb = s // BLOCK

    def body(q_ref, k_ref, v_ref, do_ref, dq_ref, dk_ref, dv_ref, z_s, a_s):
        i = pl.program_id(1)

        @pl.when(i == 0)
        def _():
            dk_ref[...] = jnp.zeros_like(dk_ref)
            dv_ref[...] = jnp.zeros_like(dv_ref)

        qs = [q_ref[:, lanes].astype(BF16) for lanes in SB_LANES]
        dobs = [do_ref[:, lanes].astype(BF16) for lanes in SB_LANES]
        upper, lower = _tri(True), _tri(False)
        nsteps = _sb_steps(i)

        def make_recompute(masked):
            def recompute(t, runs):
                runs = list(runs)
                for b in reversed(range(SB_UNROLL)):
                    j = (nsteps - 1 - t) * SB_UNROLL + b
                    rows = _block_rows(j)
                    for hh, lanes in enumerate(SB_LANES):
                        z, log_beta, log_1mb, mask = _sb_scores(qs[hh], k_ref[rows, lanes].astype(BF16), j, i, masked)
                        z_s[hh, j] = z
                        a_s[hh, j] = _sb_weights(log_beta, log_1mb, mask, runs[hh], upper)
                        runs[hh] = runs[hh] + jnp.sum(log_1mb, axis=-1, keepdims=True)
                return tuple(runs)
            return recompute

        _sb_peeled(nsteps, make_recompute, (jnp.zeros((SB_QROWS, 1), F32),) * SB_HEADS_PER_STEP, True)

        def make_grads(masked):
            def grads(t, carry):
                carry = list(carry)
                for b in range(SB_UNROLL):
                    j = t * SB_UNROLL + b
                    rows = _block_rows(j)
                    for hh, lanes in enumerate(SB_LANES):
                        dq, run = carry[hh]
                        kk, vv = k_ref[rows, lanes].astype(BF16), v_ref[rows, lanes].astype(BF16)
                        z, a = z_s[hh, j], a_s[hh, j]
                        de = _dot(dobs[hh], vv, NT) * a
                        beta = jax.nn.sigmoid(z)
                        one_minus_beta = 1.0 - beta
                        if masked:
                            beta = jnp.where(_sb_mask(j, i), beta, 0.0)
                        dz = (de * one_minus_beta - beta * (run + _dot_exact(de, lower))).astype(BF16)
                        dk_ref[rows, lanes] += _dot(dz, qs[hh], TN) * ATT_SCALE
                        dv_ref[rows, lanes] += _dot(a.astype(BF16), dobs[hh], TN)
                        carry[hh] = (dq + _dot(dz, kk), run + jnp.sum(de, axis=-1, keepdims=True))
                return tuple(carry)
            return grads

        zero = (jnp.zeros((SB_QROWS, HEAD_DIM), F32), jnp.zeros((SB_QROWS, 1), F32))
        for lanes, (dq, _) in zip(SB_LANES, _sb_peeled(nsteps, make_grads, (zero,) * SB_HEADS_PER_STEP, False)):
            dq_ref[:, lanes] = dq * ATT_SCALE

    width = SB_HEADS_PER_STEP * HEAD_DIM
    qb, kb, vb = (off // width for off in (OFF_QB, OFF_KB, OFF_VB))
    blk = pl.BlockSpec((SB_QROWS, width), lambda h, i: (i, h))
    full = pl.BlockSpec((s, width), lambda h, i: (0, h))
    shp = jax.ShapeDtypeStruct((s, B_W), F32)
    saved = pltpu.VMEM((SB_HEADS_PER_STEP, nkb, SB_QROWS, BLOCK), F32)
    return _pcall(body, name=name, out_shape=(shp, shp, shp), grid=(SB_HEADS // SB_HEADS_PER_STEP, s // SB_QROWS),
                  in_specs=[pl.BlockSpec((SB_QROWS, width), lambda h, i: (i, qb + h)),
                            pl.BlockSpec((s, width), lambda h, i: (0, kb + h)),
                            pl.BlockSpec((s, width), lambda h, i: (0, vb + h)), blk],
                  out_specs=(blk, full, full), scratch=[saved, saved])(proj, proj, proj, do_b)


def _coords():
    return lax.axis_index("x"), lax.axis_index("y"), lax.axis_index("c")


def _flip(v, bit):
    return 1 - v if bit else v


def _shard_of(ref, axis, idx, size):
    if axis == 0:
        sl = pl.ds(pl.multiple_of(idx * size, 16), size)
        return ref.at[sl, :] if len(ref.shape) == 2 else ref.at[:, sl, :]
    sl = pl.ds(pl.multiple_of(idx * size, 128), size)
    return ref.at[:, sl] if len(ref.shape) == 2 else ref.at[:, :, sl]


def _small_allgather(v, *, name, silu=False):
    n = v.shape[1]

    def body(v_ref, out_ref, send_sems, recv_sems):
        x, y, c = _coords()
        me = 4 * x + 2 * y + c
        val = v_ref[...]
        out_ref[me] = val * jax.nn.sigmoid(val) if silu else val
        copies = []
        for k in range(1, N_DEV):
            peer = (_flip(x, k & 4), _flip(y, k & 2), _flip(c, k & 1))
            copies.append(pltpu.make_async_remote_copy(
                src_ref=out_ref.at[me], dst_ref=out_ref.at[me], send_sem=send_sems.at[k - 1],
                recv_sem=recv_sems.at[k - 1], device_id=peer, device_id_type=MESH))
        for cp in copies:
            cp.start()
        for cp in copies:
            cp.wait_recv()
        for cp in copies:
            cp.wait_send()

    return _pcall(body, name=name, out_shape=jax.ShapeDtypeStruct((N_DEV, 1, n), F32),
                  in_specs=[pl.BlockSpec(memory_space=pltpu.VMEM)], out_specs=pl.BlockSpec(memory_space=pltpu.VMEM),
                  scratch=[pltpu.SemaphoreType.DMA((N_DEV - 1,)), pltpu.SemaphoreType.DMA((N_DEV - 1,))])(v)


def _cast_place(w, layer, axis, me, *, name):
    _, r, c = w.shape
    tr = _rows(r, c)
    nrt = r // tr

    def body(me_ref, w_ref, o_ref):
        o_ref[...] = w_ref[...].astype(BF16)

    wspec = pl.BlockSpec((None, tr, c), lambda i, me_ref: (layer, i, 0))
    if axis == 0:
        ospec = pl.BlockSpec((tr, c), lambda i, me_ref: (me_ref[0] * nrt + i, 0))
        shape = (r * N_DEV, c)
    else:
        ospec = pl.BlockSpec((tr, c), lambda i, me_ref: (i, me_ref[0]))
        shape = (r, c * N_DEV)
    return _pcall(body, name=name, out_shape=jax.ShapeDtypeStruct(shape, BF16), grid=(nrt,), in_specs=[wspec],
                  out_specs=ospec, prefetch=1)(me, w)


def _pair_sum(grad, sib, core, axis, *, name):
    _, r, c = sib.shape
    tr = _rows(r, c // 2)
    nrt = r // tr

    def body(core_ref, g_ref, s_ref, o_ref):
        o_ref[...] = (g_ref[...].astype(F32) + s_ref[...].astype(F32)).astype(BF16)

    if axis == 0:
        gspec = pl.BlockSpec((tr, c), lambda q, i, core_ref: ((2 * q + core_ref[0]) * nrt + i, 0))
    else:
        gspec = pl.BlockSpec((tr, c), lambda q, i, core_ref: (i, 2 * q + core_ref[0]))
    sspec = pl.BlockSpec((None, tr, c), lambda q, i, core_ref: (q, i, 0))
    return _pcall(body, name=name, out_shape=jax.ShapeDtypeStruct(sib.shape, BF16), grid=(N_CHIPS, nrt),
                  in_specs=[gspec, sspec], out_specs=sspec, prefetch=1)(core, grad, sib)


ANY_SPEC = pl.BlockSpec(memory_space=pl.ANY)
SEM_SPEC = pl.BlockSpec(memory_space=pltpu.SEMAPHORE)
SPLIT_PARAMS = dict(has_side_effects=pltpu.SideEffectType.DATAFLOW_SIDE_EFFECTING)


def _split_start(copies_fn, buffers, sem_shape, after, *, name):
    n = len(buffers)
    rows, cols = sem_shape
    ns = rows * cols
    extra = ([] if after is None else [after]) + _take_token()

    def body(*refs):
        sems = refs[n + len(extra):n + len(extra) + 2 * ns]
        for cp in copies_fn(refs[:n], _sem_rows(sems[:ns], cols), _sem_rows(sems[ns:], cols)):
            cp.start()
        refs[-1][...] = jnp.zeros_like(refs[-1])

    sem = pltpu.SemaphoreType.DMA(())
    outs = pl.pallas_call(
        body, name=name,
        out_shape=((sem,) * (2 * ns) + tuple(jax.ShapeDtypeStruct(b.shape, b.dtype) for b in buffers) + (TOKEN,)),
        in_specs=(ANY_SPEC,) * (n + len(extra)),
        out_specs=(SEM_SPEC,) * (2 * ns) + (ANY_SPEC,) * n + (pl.BlockSpec(memory_space=pltpu.VMEM),),
        input_output_aliases={i: 2 * ns + i for i in range(n)},
        compiler_params=pltpu.CompilerParams(**SPLIT_PARAMS))(*buffers, *extra)
    _ORDER["token"] = outs[-1]
    return list(outs[:ns]), list(outs[ns:2 * ns]), list(outs[2 * ns:2 * ns + n]), outs[-1]


def _split_wait(copies_fn, send_sems, recv_sems, buffers, after, sem_rows, *, name):
    n, ns = len(buffers), len(send_sems)
    cols = ns // sem_rows
    extra = ([] if after is None else [after]) + _take_token()

    def body(*refs):
        sems = refs[n:n + 2 * ns]
        copies = copies_fn(refs[:n], _sem_rows(sems[:ns], cols), _sem_rows(sems[ns:], cols))
        for cp in copies:
            cp.wait_send()
        for cp in copies:
            cp.wait_recv()
        refs[-1][...] = jnp.zeros_like(refs[-1])

    outs = pl.pallas_call(
        body, name=name, out_shape=tuple(jax.ShapeDtypeStruct(b.shape, b.dtype) for b in buffers) + (TOKEN,),
        in_specs=(ANY_SPEC,) * n + (SEM_SPEC,) * (2 * ns) + (ANY_SPEC,) * len(extra),
        out_specs=(ANY_SPEC,) * n + (pl.BlockSpec(memory_space=pltpu.VMEM),),
        input_output_aliases={i: i for i in range(n)},
        compiler_params=pltpu.CompilerParams(**SPLIT_PARAMS))(*buffers, *send_sems, *recv_sems, *extra)
    _ORDER["token"] = outs[-1]
    return list(outs[:n])


def _sem_rows(sems, cols):
    return [sems[i:i + cols] for i in range(0, len(sems), cols)]


def _empty_hbm(shape, dtype):
    return pltpu.with_memory_space_constraint(lax.empty(shape, dtype), pltpu.HBM)


class _SplitGather:
    def __init__(self, fulls, axes, tag):
        self.axes, self.tag, self.nt = list(axes), tag, len(fulls)
        self.sizes = [f.shape[ax] // N_DEV for f, ax in zip(fulls, axes)]
        self.fulls = list(fulls)

    def _slot(self, ref, t, dev):
        return _shard_of(ref, self.axes[t], 4 * dev[0] + 2 * dev[1] + dev[2], self.sizes[t])

    def _first_copies(self, refs, send_sems, recv_sems):
        x, y, c = _coords()
        peers = [(x, y, 1 - c), (1 - x, y, c), (x, 1 - y, c), (1 - x, 1 - y, c)]
        return [pltpu.make_async_remote_copy(
            src_ref=self._slot(refs[t], t, (x, y, c)), dst_ref=self._slot(refs[t], t, (x, y, c)),
            send_sem=send_sems[t][k], recv_sem=recv_sems[t][k], device_id=peer, device_id_type=MESH)
            for t in range(self.nt) for k, peer in enumerate(peers)]

    def _forward_copies(self, refs, send_sems, recv_sems):
        x, y, c = _coords()
        chips = [(1 - x, y), (x, 1 - y), (1 - x, 1 - y)]
        return [pltpu.make_async_remote_copy(
            src_ref=self._slot(refs[t], t, (*chip, c)), dst_ref=self._slot(refs[t], t, (*chip, c)),
            send_sem=send_sems[t][j], recv_sem=recv_sems[t][j], device_id=(x, y, 1 - c), device_id_type=MESH)
            for t in range(self.nt) for j, chip in enumerate(chips)]

    def first(self, after):
        self.s1, self.r1, self.fulls, token = _split_start(
            self._first_copies, self.fulls, (self.nt, 4), after, name=f"comm_gather1_start_{self.tag}")
        return token

    def forward(self, after):
        bufs = _split_wait(self._first_copies, self.s1, self.r1, self.fulls, after, self.nt,
                           name=f"comm_gather1_wait_{self.tag}")
        self.s2, self.r2, self.fulls, token = _split_start(
            self._forward_copies, bufs, (self.nt, 3), after, name=f"comm_gather2_start_{self.tag}")
        return token

    def finish(self, after):
        return _split_wait(self._forward_copies, self.s2, self.r2, self.fulls, after, self.nt,
                           name=f"comm_gather2_wait_{self.tag}")


class _SplitPairExchange:
    def __init__(self, grads, axes, tag):
        self.nt, self.tag, self.axes = len(grads), tag, list(axes)
        self.grads = list(grads)
        self.sizes = [g.shape[ax] // N_DEV for g, ax in zip(grads, axes)]

    def _copies(self, refs, send_sems, recv_sems):
        nt = self.nt
        x, y, c = _coords()
        return [pltpu.make_async_remote_copy(
            src_ref=_shard_of(refs[t], self.axes[t], 2 * q + 1 - c, self.sizes[t]), dst_ref=refs[nt + t].at[q],
            send_sem=send_sems[t][q], recv_sem=recv_sems[t][q], device_id=(x, y, 1 - c), device_id_type=MESH)
            for t in range(nt) for q in range(N_CHIPS)]

    def start(self):
        landing = []
        for g, ax in zip(self.grads, self.axes):
            dims = list(g.shape)
            dims[ax] //= N_DEV
            landing.append(_empty_hbm((N_CHIPS, *dims), g.dtype))
        self.s, self.r, self.bufs, token = _split_start(
            self._copies, self.grads + landing, (self.nt, N_CHIPS), None,
            name=f"comm_rs_pair_start_{self.tag}")
        return token

    def finish(self, after):
        bufs = _split_wait(self._copies, self.s, self.r, self.bufs, after, self.nt,
                           name=f"comm_rs_pair_wait_{self.tag}")
        return bufs[:self.nt], bufs[self.nt:]


class _ReducePipeline:
    def __init__(self, core):
        self.core, self.items, self.done, self.now = core, [], [], 0

    def add(self, keys, grads, layer):
        axes = [SHARD_AXIS[k] for k in keys]
        pair = _SplitPairExchange([grads[k] for k in keys], axes, f"{keys[0]}{layer}")
        pair.start()
        self.items.append(dict(keys=keys, layer=layer, axes=axes, pair=pair, state="pair", since=self.now))

    def tick(self, after, flush=False):
        self.now += 1
        for it in self.items:
            if it["state"] == "pair" and it["since"] < self.now:
                grads, sib = it["pair"].finish(after)
                sums = [_pair_sum(g, s_, self.core, ax, name="pair_sum_" + k)
                        for k, g, s_, ax in zip(it["keys"], grads, sib, it["axes"])]
                it["chip"] = _SplitChipExchange(sums, f"{it['keys'][0]}{it['layer']}")
                it["chip"].start()
                it.update(state="chip", since=self.now)
            elif it["state"] == "chip" and (flush or self.now - it["since"] >= 2):
                sums, remote = it["chip"].finish(after)
                self.done.append((it["keys"], it["layer"], sums, remote))
                it["state"] = "done"

    def take_done(self):
        out, self.done = self.done, []
        return out


class _SplitChipExchange:
    def __init__(self, sums, tag):
        self.nt, self.tag = len(sums), tag
        self.sums = list(sums)

    def _copies(self, refs, send_sems, recv_sems):
        nt = self.nt
        x, y, c = _coords()
        copies = []
        for t in range(nt):
            for k in range(1, N_CHIPS):
                px, py = _flip(x, k & 2), _flip(y, k & 1)
                copies.append(pltpu.make_async_remote_copy(
                    src_ref=refs[t].at[2 * px + py], dst_ref=refs[nt + t].at[k - 1], send_sem=send_sems[t][k - 1],
                    recv_sem=recv_sems[t][k - 1], device_id=(px, py, c), device_id_type=MESH))
        return copies

    def start(self):
        landing = [_empty_hbm((N_CHIPS - 1,) + s.shape[1:], s.dtype) for s in self.sums]
        self.s, self.r, self.bufs, token = _split_start(
            self._copies, self.sums + landing, (self.nt, N_CHIPS - 1), None,
            name=f"comm_rs_chip_start_{self.tag}")
        return token

    def finish(self, after):
        bufs = _split_wait(self._copies, self.s, self.r, self.bufs, after, self.nt,
                           name=f"comm_rs_chip_wait_{self.tag}")
        return bufs[:self.nt], bufs[self.nt:]


def _adam_math(g, w, m, v):
    m2 = ADAM_B1 * m + (1.0 - ADAM_B1) * g
    v2 = ADAM_B2 * v + (1.0 - ADAM_B2) * (g * g)
    m_hat = m2 / (1.0 - ADAM_B1 ** ADAM_STEP)
    v_hat = v2 / (1.0 - ADAM_B2 ** ADAM_STEP)
    delta = -ADAM_LR * (m_hat / (jnp.sqrt(v_hat) + ADAM_EPS) + ADAM_WD * w)
    return delta, m2, v2


def _adamw_sharded(chip_sums, remote, chip, w, m, v, layer, prev, *, name):
    nl, r, c = w.shape
    tr = _rows(r, c)

    def body(*refs):
        p_ref, r0_ref, r1_ref, r2_ref, w_ref, m_ref, v_ref = refs[1:8]
        g_out, d_out, m_out, v_out = refs[-4:]
        g = ((p_ref[...].astype(F32) + r0_ref[...].astype(F32)) + r1_ref[...].astype(F32)) + r2_ref[...].astype(F32)
        g_out[...] = g
        d_out[...], m_out[...], v_out[...] = _adam_math(g, w_ref[...], m_ref[...], v_ref[...])

    pspec = pl.BlockSpec((None, tr, c), lambda i, chip_ref: (chip_ref[0], i, 0))

    def rspec(k):
        return pl.BlockSpec((None, tr, c), lambda i, chip_ref: (k, i, 0))

    wspec = pl.BlockSpec((None, tr, c), lambda i, chip_ref: (layer, i, 0))
    in_specs = [pspec, rspec(0), rspec(1), rspec(2), wspec, wspec, wspec]
    args = [chip, chip_sums, remote, remote, remote, w, m, v]
    aliases = {}
    if prev is not None:
        in_specs += [pl.BlockSpec(memory_space=pl.ANY)] * 4
        aliases = {len(args) + i: i for i in range(4)}
        args += list(prev)
    shp = jax.ShapeDtypeStruct(w.shape, F32)
    return _pcall(body, name=name, out_shape=(shp,) * 4, grid=(r // tr,), in_specs=in_specs, out_specs=(wspec,) * 4,
                  aliases=aliases, prefetch=1)(*args)


def _adamw_local(g, w, m, v, *, name):
    nl, r, c = w.shape
    tr = _rows(r, c)

    def body(g_ref, w_ref, m_ref, v_ref, d_out, m_out, v_out):
        d_out[...], m_out[...], v_out[...] = _adam_math(g_ref[...], w_ref[...], m_ref[...], v_ref[...])

    spec = pl.BlockSpec((None, tr, c), lambda l, i: (l, i, 0))
    shp = jax.ShapeDtypeStruct(w.shape, F32)
    return _pcall(body, name=name, out_shape=(shp,) * 3, grid=(nl, r // tr), in_specs=[spec] * 4,
                  out_specs=(spec,) * 3)(g, w, m, v)


def _adamw_replicated(parts, w, m, v, *, name):
    n = w.shape[1]

    def body(p_ref, w_ref, m_ref, v_ref, g_out, d_out, m_out, v_out):
        g = p_ref[0]
        for k in range(1, N_DEV):
            g = g + p_ref[k]
        g_out[...] = g
        d_out[...], m_out[...], v_out[...] = _adam_math(g, w_ref[...], m_ref[...], v_ref[...])

    vm = pl.BlockSpec(memory_space=pltpu.VMEM)
    shp = jax.ShapeDtypeStruct((1, n), F32)
    return _pcall(body, name=name, out_shape=(shp,) * 4, in_specs=[vm] * 4, out_specs=(vm,) * 4)(parts, w, m, v)


def _group_views(qk, proj, g, dil, seq):
    if dil == 1:
        return (qk, qk, proj), (0, A_HEADS, 2 * A_HEADS)
    length = seq // dil
    lo = g * GROUP_W
    q = qk[:, lo:lo + GROUP_W].reshape(length, dil * GROUP_W)
    k = qk[:, A_W + lo:A_W + lo + GROUP_W].reshape(length, dil * GROUP_W)
    v = proj[:, OFF_VA + lo:OFF_VA + lo + GROUP_W].astype(BF16).reshape(length, dil * GROUP_W)
    return (q, k, v), (0, 0, 0)


def _mod_rows(mod, d):
    return [mod[:, i * d:(i + 1) * d] for i in range(6)]


MIXER_W = ("w_in", "w_branch_a", "w_branch_b", "w_out")
FFN_W = ("w_gate_up", "w_down")
SHARD_AXIS = {"w_in": 1, "w_branch_a": 1, "w_branch_b": 1, "w_out": 0, "w_gate_up": 1, "w_down": 0}


def _norm_args(mod, gain, which, d):
    rows = _mod_rows(mod, d)
    return gain, rows[3 * which + 1], rows[3 * which]


def _mixer_fwd_a(h, u, gains, w_in, cos2, sin2, hook):
    seq = h.shape[0]
    proj = _mm(u, w_in, name="mm_in")
    hook(proj)
    qk = _qkrope_fwd(proj, gains, cos2, sin2, name="qkrope_fwd")
    os_, lses, views = [], [], []
    for g, dil in enumerate(DILATIONS):
        arrs, offs = _group_views(qk, proj, g, dil, seq)
        o, lse = _dil_fwd(*arrs, offs, seq // dil, dil, name=f"dil_fwd_{dil}")
        views.append((arrs, offs, o, lse))
        os_.append(o.reshape(seq, GROUP_W))
        lses.append(lse.reshape(seq, GROUP_W))
    o_a = _combine_fwd(os_, lses, name="combine_fwd")
    o_b = _sb_fwd(proj, name="sb_fwd")
    return dict(h_in=h, u=u, proj=proj, views=views, os=os_, lses=lses, o_a=o_a, o_b=o_b)


def _mixer_fwd_b(sv, mod, g2, wts):
    d = sv["h_in"].shape[1]
    merged, y_a, y_b = _mm_merge(sv["o_a"], sv["o_b"], wts["w_branch_a"], wts["w_branch_b"], sv["proj"],
                                 name="mm_branch")
    h_mid, t, u2 = _mm_resid_norm(merged, wts["w_out"], sv["h_in"], _mod_rows(mod, d)[2], _norm_args(mod, g2, 1, d),
                                  name="mm_out")
    sv.update(y_a=y_a, y_b=y_b, merged=merged, t=t, h_mid=h_mid, u2=u2)
    return h_mid


def _ffn_fwd_a(sv, w_gate_up):
    a, g, u = _mm_swiglu(sv["u2"], w_gate_up, name="mm_gate_up")
    sv.update(g=g, up=u, a=a)
    return a


def _ffn_fwd_b(sv, mod, w_down, next_norm):
    d = sv["h_mid"].shape[1]
    h_out, sv["f"], u_next = _mm_resid_norm(sv["a"], w_down, sv["h_mid"], _mod_rows(mod, d)[5], next_norm,
                                            name="mm_down")
    return h_out, u_next


def _wgrad(act, dout, key):
    return _mm(act, dout, ta=True, out_dtype=BF16, caps=(2048, 1024, 3072), name="mm_wgrad_" + key)


def _ffn_bwd(dh, df, dgate2, sv, mod, g2, wts, hook):
    d = dh.shape[1]
    sc2, ga1 = _mod_rows(mod, d)[4], _mod_rows(mod, d)[2]
    dg, dup = _mm_down_t_swiglu(df, wts["w_down"], sv["g"], sv["up"], name="mm_down_t")
    grads = {"w_down": _wgrad(sv["a"], df, "w_down")}
    hook(dup)
    du2 = _mm_cat_k(dg, dup, wts["w_gate_up"], name="mm_gate_up_t")
    grads["w_gate_up"] = _mm_cat_n(sv["u2"], dg, dup, name="mm_wgrad_w_gate_up")
    dh_mid, dsh2, dsc2, dg2, dt, dgate1 = _rmsmod_bwd(du2, sv["h_mid"], g2, sc2, dh, sv["t"], ga1, name="rmsmod_bwd")
    return dh_mid, [dsh2, dsc2, dgate2], dg2, grads, dt, dgate1


def _mixer_bwd(dh_mid, dt, dgate1, sv, mod, g1, gains, wts, cos2, sin2, hook, below):
    seq, d = dh_mid.shape
    sc1 = _mod_rows(mod, d)[1]
    dy_a, dy_b, dga, dgb = _mm_out_t_merge(dt, wts["w_out"], sv["proj"], sv["y_a"], sv["y_b"], name="mm_out_t")
    grads = {"w_out": _wgrad(sv["merged"], dt, "w_out")}
    do_a = _mm(dy_a, wts["w_branch_a"], tb=True, name="mm_branch_t")
    do_b = _mm(dy_b, wts["w_branch_b"], tb=True, name="mm_branch_t")
    grads["w_branch_a"] = _wgrad(sv["o_a"], dy_a, "w_branch_a")
    grads["w_branch_b"] = _wgrad(sv["o_b"], dy_b, "w_branch_b")
    dqb, dkb, dvb = _sb_bwd(sv["proj"], do_b, name="sb_bwd")
    hook(dqb, grads)
    comb = _combine_bwd(do_a, sv["os"], sv["lses"], name="combine_bwd")
    grads = {}
    dos, dls = comb[:3], comb[3:]
    dqs, dks, dvs = [], [], []
    for g, dil in enumerate(DILATIONS):
        length = seq // dil
        arrs, offs, o_view, lse_view = sv["views"][g]
        if dil == 1:
            dq, dk, dv = _dil_bwd(*arrs, offs, o_view, lse_view, dos[g], dls[g], length, dil, name=f"dil_bwd_{dil}")
        else:
            dq, dk, dv = _dil_bwd_strided(*arrs, sv["os"][g], sv["lses"][g], dos[g], dls[g], length, dil,
                                          name=f"dil_bwd_{dil}")
        dqs.append(dq)
        dks.append(dk)
        dvs.append(dv)
    dq_pre, dqn = _qkrope_bwd(dqs, sv["proj"], gains, 0, cos2, sin2, name="qkrope_bwd")
    dk_pre, dkn = _qkrope_bwd(dks, sv["proj"], gains, 1, cos2, sin2, name="qkrope_bwd")
    dgains = jnp.stack([dqn, dkn])
    dproj = _assemble([dq_pre, dk_pre] + dvs + [dqb, dkb, dvb, dga, dgb], name="assemble_dproj")
    du = _mm(dproj, wts["w_in"], tb=True, name="mm_in_t")
    grads["w_in"] = _wgrad(sv["u"], dproj, "w_in")
    dh_in, dsh1, dsc1, dg1, df, dgate2 = _rmsmod_bwd(du, sv["h_in"], g1, sc1, dh_mid, *(below or (None, None)),
                                                     name="rmsmod_bwd")
    return dh_in, [dsh1, dsc1, dgate1], dg1, dgains, grads, df, dgate2


def kernel(x, c, w_ada, b_ada, norm1_g, norm2_g, w_in, qn_g, kn_g, w_branch_a, w_branch_b, w_out, w_gate_up, w_down, loss_target, m_w_ada, m_b_ada, m_norm1_g, m_norm2_g, m_w_in, m_qn_g, m_kn_g, m_w_branch_a, m_w_branch_b, m_w_out, m_w_gate_up, m_w_down, v_w_ada, v_b_ada, v_norm1_g, v_norm2_g, v_w_in, v_qn_g, v_kn_g, v_w_branch_a, v_w_branch_b, v_w_out, v_w_gate_up, v_w_down):
    _ORDER["token"] = None
    seq, d = x.shape[1], x.shape[2]
    depth = w_in.shape[0]
    weights = dict(w_in=w_in, w_branch_a=w_branch_a, w_branch_b=w_branch_b, w_out=w_out, w_gate_up=w_gate_up,
                   w_down=w_down)
    moments_m = dict(w_in=m_w_in, w_branch_a=m_w_branch_a, w_branch_b=m_w_branch_b, w_out=m_w_out,
                     w_gate_up=m_w_gate_up, w_down=m_w_down)
    moments_v = dict(w_in=v_w_in, w_branch_a=v_w_branch_a, w_branch_b=v_w_branch_b, w_out=v_w_out,
                     w_gate_up=v_w_gate_up, w_down=v_w_down)
    xi, yi, ci = _coords()
    me = 4 * xi + 2 * yi + ci
    core = jnp.reshape(ci, (1,)).astype(jnp.int32)
    chip = jnp.reshape(2 * xi + yi, (1,)).astype(jnp.int32)

    ada_w = w_ada.shape[2]
    c_act = _small_allgather(c, name="comm_gather_c", silu=True).reshape(N_DEV, d)
    c_pad = jnp.concatenate([c_act, jnp.zeros_like(c_act)], axis=0).astype(BF16)
    bias = lax.dynamic_slice(b_ada, (0, me * ada_w), (depth, ada_w))
    mod_part = jnp.stack([_mm(c_pad, w_ada[l], name="mm_ada")[:N_DEV] for l in range(depth)]) + bias[:, None, :]
    mod_all = _small_allgather(mod_part.reshape(1, depth * N_DEV * ada_w), name="comm_gather_mod")
    mod_all = mod_all.reshape(N_DEV, depth, N_DEV, ada_w)
    mod_mine = lax.dynamic_index_in_dim(mod_all, me, axis=2, keepdims=False)
    mods = jnp.transpose(mod_mine, (1, 0, 2)).reshape(depth, 1, 6 * d)

    cos2, sin2 = _rope_tables(seq)
    gains = [jnp.stack([qn_g[l], kn_g[l]])[:, None, :] for l in range(depth)]
    g1s = [norm1_g[l][None] for l in range(depth)]
    g2s = [norm2_g[l][None] for l in range(depth)]

    me_arr = jnp.reshape(me, (1,)).astype(jnp.int32)

    def placed(keys, l):
        return [_cast_place(weights[k], l, SHARD_AXIS[k], me_arr, name="cast_place_" + k) for k in keys]

    def gather_of(keys, l, tag):
        return _SplitGather(placed(keys, l), [SHARD_AXIS[k] for k in keys], f"{tag}{l}")

    groups = [("w_in", 0, MIXER_W[:1]), ("rest", 0, MIXER_W[1:]), ("ffn", 0, FFN_W)]
    for l in range(1, depth):
        groups += [("mixer", l, MIXER_W), ("ffn", l, FFN_W)]
    gathers = {}

    def issue(some):
        for tag, l, keys in some:
            gathers[tag, l] = gather_of(keys, l, tag)
            gathers[tag, l].first(after=mods)

    issue(groups[:3])
    h = x[0]
    u = _rmsmod_fwd(h, *_norm_args(mods[0], g1s[0], 0, d), name="rmsmod_fwd")
    gathers["w_in", 0].forward(after=u)
    issue(groups[3:])
    wm = {"w_in": gathers["w_in", 0].finish(after=u)[0]}
    saved, full = [], []
    for l in range(depth):
        last = l + 1 == depth
        sv = _mixer_fwd_a(h, u, gains[l], wm["w_in"], cos2, sin2,
                          gathers["rest", 0].forward if l == 0 else lambda after: None)
        gathers["ffn", l].forward(after=sv["o_b"])
        if l == 0:
            wm.update(zip(MIXER_W[1:], gathers["rest", 0].finish(after=sv["o_b"])))
        h_mid = _mixer_fwd_b(sv, mods[l], g2s[l], wm)
        wf = dict(zip(FFN_W, gathers["ffn", l].finish(after=h_mid)))
        a = _ffn_fwd_a(sv, wf["w_gate_up"])
        if not last:
            gathers["mixer", l + 1].forward(after=a)
        h, u = _ffn_fwd_b(sv, mods[l], wf["w_down"],
                          None if last else _norm_args(mods[l + 1], g1s[l + 1], 0, d))
        saved.append(sv)
        full.append({**wm, **wf})
        if not last:
            wm = dict(zip(MIXER_W, gathers["mixer", l + 1].finish(after=h)))
    def ffn_gate(l):
        return saved[l]["f"], _mod_rows(mods[l], d)[5]

    loss_part, dh, df, dgate2 = _loss_fwd(h, loss_target[0], *ffn_gate(depth - 1), name="loss")
    loss = lax.psum(loss_part[0, 0], ("x", "y", "c"))

    pipe = _ReducePipeline(core)
    dmods, dg1s, dg2s, dgains = [None] * depth, [None] * depth, [None] * depth, [None] * depth
    for l in reversed(range(depth)):
        dh_mid, dmod_f, dg2s[l], grads, dt, dgate1 = _ffn_bwd(dh, df, dgate2, saved[l], mods[l], g2s[l], full[l],
                                                              pipe.tick)
        pipe.tick(dh_mid)
        pipe.add(FFN_W, grads, l)
        dh, dmod_m, dg1s[l], dgains[l], grads, df, dgate2 = _mixer_bwd(
            dh_mid, dt, dgate1, saved[l], mods[l], g1s[l], gains[l], full[l], cos2, sin2,
            lambda after, early, l=l: (pipe.tick(after), pipe.add(MIXER_W[1:], early, l)),
            ffn_gate(l - 1) if l > 0 else None)
        dmods[l] = jnp.concatenate(dmod_m + dmod_f, axis=1)
        pipe.tick(dh)
        pipe.add(MIXER_W[:1], grads, l)
    grad_x = dh[None]

    stacked = {}

    def update(items):
        for keys, l, sums, remote in items:
            for k, p_, r_ in zip(keys, sums, remote):
                stacked[k] = _adamw_sharded(p_, r_, chip, weights[k], moments_m[k], moments_v[k], l,
                                            stacked.get(k), name="adamw_" + k)

    ready = pipe.take_done()
    update([it for it in ready if it[0] != FFN_W])

    small = jnp.concatenate(
        dmods + dg1s + dg2s + [dgains[l][0] for l in range(depth)] + [dgains[l][1] for l in range(depth)], axis=1)
    small_all = _small_allgather(small, name="comm_gather_small")
    pipe.tick(small_all)
    update([it for it in ready if it[0] == FFN_W] + pipe.take_done())

    def pack(b, n1, n2, qn, kn):
        return jnp.concatenate([t_.reshape(1, -1) for t_ in (b, n1, n2, qn, kn)], axis=1)

    sg, sd, sm, sv_ = _adamw_replicated(small_all, pack(b_ada, norm1_g, norm2_g, qn_g, kn_g),
                                        pack(m_b_ada, m_norm1_g, m_norm2_g, m_qn_g, m_kn_g),
                                        pack(v_b_ada, v_norm1_g, v_norm2_g, v_qn_g, v_kn_g), name="adamw_replicated")

    def unpack(p):
        sizes = [depth * 6 * d, depth * d, depth * d, depth * HEAD_DIM, depth * HEAD_DIM]
        shapes = [b_ada.shape, norm1_g.shape, norm2_g.shape, qn_g.shape, kn_g.shape]
        out, off = [], 0
        for n, shp in zip(sizes, shapes):
            out.append(p[0, off:off + n].reshape(shp))
            off += n
        return dict(zip(("b_ada", "norm1_g", "norm2_g", "qn_g", "kn_g"), out))

    ug, ud, um, uv = unpack(sg), unpack(sd), unpack(sm), unpack(sv_)
    res = {k: dict(g=ug[k], d=ud[k], m=um[k], v=uv[k]) for k in ug}

    dmod_all = small_all[:, 0, :depth * 6 * d].reshape(N_DEV, depth, 6 * d)
    g_ada = None
    for l in range(depth):
        dm = lax.dynamic_slice(dmod_all[:, l, :], (0, me * ada_w), (N_DEV, ada_w))
        dm = jnp.concatenate([dm, jnp.zeros_like(dm)], axis=0).astype(BF16)
        g_ada = _mm(c_pad, dm, ta=True, name="mm_wgrad_ada", stack=(l, depth, g_ada))
    d_ada, m_ada, v_ada = _adamw_local(g_ada, w_ada, m_w_ada, v_w_ada, name="adamw_local")
    res["w_ada"] = dict(g=g_ada, d=d_ada, m=m_ada, v=v_ada)

    pipe.tick(d_ada)
    update(pipe.take_done())
    pipe.tick(d_ada, flush=True)
    update(pipe.take_done())
    for k, (g_, d_, m_, v_) in stacked.items():
        res[k] = dict(g=g_, d=d_, m=m_, v=v_)

    order = ("w_ada", "b_ada", "norm1_g", "norm2_g", "w_in", "qn_g", "kn_g", "w_branch_a", "w_branch_b", "w_out",
             "w_gate_up", "w_down")
    _ORDER["token"] = None
    return (loss, grad_x, *[res[k]["g"] for k in order], *[res[k]["d"] for k in order],
            *[res[k]["m"] for k in order], *[res[k]["v"] for k in order])
```

```python
import functools

import jax
import jax.numpy as jnp
from jax import lax
from jax.experimental import pallas as pl
from jax.experimental.pallas import tpu as pltpu

F32 = jnp.float32
BF16 = jnp.bfloat16

HEAD_DIM = 128
BLOCK = 128
DILATIONS = (1, 4, 16)
HEADS_PER_GROUP = 4
A_HEADS = 12
SB_HEADS = 4
GROUP_W = HEADS_PER_GROUP * HEAD_DIM
A_W = A_HEADS * HEAD_DIM
B_W = SB_HEADS * HEAD_DIM
OFF_QA, OFF_KA, OFF_VA = 0, A_W, 2 * A_W
OFF_QB, OFF_KB, OFF_VB = 3 * A_W, 3 * A_W + B_W, 3 * A_W + 2 * B_W
OFF_GATES = 3 * A_W + 3 * B_W
ROPE_THETA = 10000.0
EPS = 1e-6
ATT_SCALE = HEAD_DIM ** -0.5
MASKED = -1e30

ADAM_LR, ADAM_B1, ADAM_B2, ADAM_EPS, ADAM_WD, ADAM_STEP = 0.001, 0.9, 0.999, 1e-08, 0.01, 10

N_DEV = 8
N_CHIPS = 4
V7X_VMEM_LIMIT_BYTES = 56 * 1024 * 1024
ELEMWISE_BLOCK_BYTES = 2 * 1024 * 1024
MESH = pl.DeviceIdType.MESH

NN = (((1,), (0,)), ((), ()))
NT = (((1,), (1,)), ((), ()))
TN = (((0,), (0,)), ((), ()))


def _dot(a, b, dims=NN):
    return lax.dot_general(a, b, dims, preferred_element_type=F32)


def _tile(n, cap, mult=128):
    best = None
    for t in range(mult, min(n, cap) + 1, mult):
        if n % t == 0:
            best = t
    if best is None:
        assert n <= 2 * cap, (n, cap)
        return n
    return best


def _rows(r, c):
    return _tile(r, max(16, ELEMWISE_BLOCK_BYTES // (4 * c)), 16)


_ORDER = {"token": None}
TOKEN = jax.ShapeDtypeStruct((8, 128), F32)


def _take_token():
    prev = _ORDER["token"]
    return [] if prev is None else [prev]


def _pcall(body, *, name, out_shape, grid=None, in_specs=None, out_specs=None, scratch=(), aliases=None,
           prefetch=0):
    single = not isinstance(out_shape, (tuple, list))
    out_shapes = [out_shape] if single else list(out_shape)
    out_specs = [out_specs] if single else list(out_specs)
    extra = _take_token()
    n_in, n_extra, n_out = prefetch + len(in_specs), len(extra), len(out_shapes)

    def wrapped(*refs):
        token = refs[n_in + n_extra + n_out]
        token[...] = jnp.zeros_like(token)
        return body(*refs[:n_in], *refs[n_in + n_extra:n_in + n_extra + n_out], *refs[n_in + n_extra + n_out + 1:])

    in_specs = list(in_specs) + [pl.BlockSpec(memory_space=pl.ANY)] * n_extra
    if grid is None:
        out_specs.append(pl.BlockSpec(memory_space=pltpu.VMEM))
    else:
        out_specs.append(pl.BlockSpec(TOKEN.shape, lambda *_: (0, 0)))
    kwargs = dict(name=name, out_shape=out_shapes + [TOKEN], input_output_aliases=aliases or {},
                  compiler_params=pltpu.CompilerParams(vmem_limit_bytes=V7X_VMEM_LIMIT_BYTES))
    if prefetch:
        call = pl.pallas_call(wrapped, grid_spec=pltpu.PrefetchScalarGridSpec(
            num_scalar_prefetch=prefetch, grid=grid, in_specs=in_specs, out_specs=out_specs,
            scratch_shapes=list(scratch)), **kwargs)
    else:
        if grid is not None:
            kwargs["grid"] = grid
        call = pl.pallas_call(wrapped, in_specs=in_specs, out_specs=out_specs, scratch_shapes=list(scratch), **kwargs)

    def run(*args):
        outs = call(*args, *extra)
        _ORDER["token"] = outs[-1]
        return outs[0] if single else tuple(outs[:-1])

    return run


def _mm(a, b, *, name, ta=False, tb=False, out_dtype=F32, caps=(1024, 1024, 3072), stack=None):
    kdim, m = a.shape if ta else a.shape[::-1]
    n, k2 = b.shape if tb else b.shape[::-1]
    assert kdim == k2, (a.shape, b.shape, ta, tb)
    tm, tn, tk = _tile(m, caps[0]), _tile(n, caps[1]), _tile(kdim, caps[2])
    nk = kdim // tk
    dims = (((0 if ta else 1,), (1 if tb else 0,)), ((), ()))

    def body(*refs):
        a_ref, b_ref = refs[0], refs[1]
        part = _dot(a_ref[...].astype(BF16), b_ref[...].astype(BF16), dims)
        if nk == 1:
            o_ref = refs[-1]
            o_ref[...] = part.astype(o_ref.dtype)
            return
        o_ref, acc_ref = refs[-2], refs[-1]
        k = pl.program_id(2)

        @pl.when(k == 0)
        def _():
            acc_ref[...] = part

        @pl.when(k > 0)
        def _():
            acc_ref[...] += part

        @pl.when(k == nk - 1)
        def _():
            o_ref[...] = acc_ref[...].astype(o_ref.dtype)

    a_spec = (pl.BlockSpec((tk, tm), lambda i, j, k: (k, i)) if ta
              else pl.BlockSpec((tm, tk), lambda i, j, k: (i, k)))
    b_spec = (pl.BlockSpec((tn, tk), lambda i, j, k: (j, k)) if tb
              else pl.BlockSpec((tk, tn), lambda i, j, k: (k, j)))
    ins, in_specs, aliases = [a, b], [a_spec, b_spec], {}
    if stack is None:
        out_shape = jax.ShapeDtypeStruct((m, n), out_dtype)
        out_spec = pl.BlockSpec((tm, tn), lambda i, j, k: (i, j))
    else:
        layer, n_layers, buf = stack
        out_shape = jax.ShapeDtypeStruct((n_layers, m, n), out_dtype)
        out_spec = pl.BlockSpec((None, tm, tn), lambda i, j, k: (layer, i, j))
        if buf is not None:
            ins.append(buf)
            in_specs.append(pl.BlockSpec(memory_space=pl.ANY))
            aliases = {2: 0}
    scratch = [] if nk == 1 else [pltpu.VMEM((tm, tn), F32)]
    return _pcall(body, name=name, out_shape=out_shape, grid=(m // tm, n // tn, nk), in_specs=in_specs,
                  out_specs=out_spec, scratch=scratch, aliases=aliases)(*ins)


EPILOGUE_ROWS = 256


def _row_chunks(tm):
    return [slice(r, r + EPILOGUE_ROWS) for r in range(0, tm, EPILOGUE_ROWS)] if tm > EPILOGUE_ROWS else [slice(0, tm)]


def _mm_cat_k(a_lo, a_hi, b, *, name):
    m, f = a_lo.shape
    n = b.shape[0]
    tm, tn, tk = _tile(m, 1024), _tile(n, 1024), _tile(f, 3072)
    half = f // tk
    nk = 2 * half

    def body(lo_ref, hi_ref, b_ref, o_ref, acc_ref):
        k = pl.program_id(2)

        def accumulate(a_ref):
            part = _dot(a_ref[...], b_ref[...], NT)

            @pl.when(k == 0)
            def _():
                acc_ref[...] = part

            @pl.when(k > 0)
            def _():
                acc_ref[...] += part

        pl.when(k < half)(lambda: accumulate(lo_ref))
        pl.when(k >= half)(lambda: accumulate(hi_ref))

        @pl.when(k == nk - 1)
        def _():
            o_ref[...] = acc_ref[...]

    return _pcall(body, name=name, out_shape=jax.ShapeDtypeStruct((m, n), F32), grid=(m // tm, n // tn, nk),
                  in_specs=[pl.BlockSpec((tm, tk), lambda i, j, k: (i, jnp.minimum(k, half - 1))),
                            pl.BlockSpec((tm, tk), lambda i, j, k: (i, jnp.maximum(k - half, 0))),
                            pl.BlockSpec((tn, tk), lambda i, j, k: (j, k))],
                  out_specs=pl.BlockSpec((tm, tn), lambda i, j, k: (i, j)),
                  scratch=[pltpu.VMEM((tm, tn), F32)])(a_lo, a_hi, b)


def _mm_cat_n(a, b_lo, b_hi, *, name):
    s, m = a.shape
    f = b_lo.shape[1]
    tm, tn = _tile(m, 2048), _tile(f, 1024)
    half = f // tn

    def body(a_ref, lo_ref, hi_ref, o_ref):
        j = pl.program_id(1)

        @pl.when(j < half)
        def _():
            o_ref[...] = _dot(a_ref[...], lo_ref[...], TN).astype(BF16)

        @pl.when(j >= half)
        def _():
            o_ref[...] = _dot(a_ref[...], hi_ref[...], TN).astype(BF16)

    return _pcall(body, name=name, out_shape=jax.ShapeDtypeStruct((m, 2 * f), BF16), grid=(m // tm, 2 * half),
                  in_specs=[pl.BlockSpec((s, tm), lambda i, j: (0, i)),
                            pl.BlockSpec((s, tn), lambda i, j: (0, jnp.minimum(j, half - 1))),
                            pl.BlockSpec((s, tn), lambda i, j: (0, jnp.maximum(j - half, 0)))],
                  out_specs=pl.BlockSpec((tm, tn), lambda i, j: (i, j)))(a, b_lo, b_hi)


def _mm_resid_norm(a, w, h, gate, norm, *, name):
    s, kdim = a.shape
    d = w.shape[1]
    tk = _tile(kdim, 2048)
    nk = kdim // tk
    tm = _tile(s, 256 if nk == 1 else 512)

    def body(*refs):
        a_ref, w_ref, h_ref, gate_ref = refs[:4]
        outs = refs[7:] if norm is not None else refs[4:]

        def finish(rows, t):
            hn = h_ref[rows, :] + gate_ref[...] * t
            outs[0][rows, :] = hn
            outs[1][rows, :] = t.astype(BF16)
            if norm is not None:
                g_ref, sc_ref, sh_ref = refs[4:7]
                r = lax.rsqrt(jnp.mean(hn * hn, axis=-1, keepdims=True) + EPS)
                outs[2][rows, :] = (((hn * r) * g_ref[...]) * (1.0 + sc_ref[...]) + sh_ref[...]).astype(BF16)

        if nk == 1:
            for rows in _row_chunks(tm):
                finish(rows, _dot(a_ref[rows, :], w_ref[...]))
            return
        acc_ref = refs[-1]
        k = pl.program_id(1)

        @pl.when(k == 0)
        def _():
            acc_ref[...] = _dot(a_ref[...], w_ref[...])

        @pl.when(jnp.logical_and(k > 0, k < nk - 1))
        def _():
            acc_ref[...] += _dot(a_ref[...], w_ref[...])

        @pl.when(k == nk - 1)
        def _():
            for rows in _row_chunks(tm):
                finish(rows, acc_ref[rows, :] + _dot(a_ref[rows, :], w_ref[...]))

    row = pl.BlockSpec((tm, d), lambda i, k: (i, 0))
    vec = pl.BlockSpec((1, d), lambda i, k: (0, 0))
    in_specs = [pl.BlockSpec((tm, tk), lambda i, k: (i, k)), pl.BlockSpec((tk, d), lambda i, k: (k, 0)), row, vec]
    args = [a, w, h, gate]
    out_shape = [jax.ShapeDtypeStruct((s, d), F32), jax.ShapeDtypeStruct((s, d), BF16)]
    if norm is not None:
        in_specs += [vec, vec, vec]
        args += list(norm)
        out_shape.append(jax.ShapeDtypeStruct((s, d), BF16))
    outs = _pcall(body, name=name, out_shape=tuple(out_shape), grid=(s // tm, nk), in_specs=in_specs,
                  out_specs=(row,) * len(out_shape), scratch=[] if nk == 1 else [pltpu.VMEM((tm, d), F32)])(*args)
    return outs if norm is not None else (*outs, None)


def _mm_merge(o_a, o_b, w_a, w_b, proj, *, name):
    s = o_a.shape[0]
    d = w_a.shape[1]
    tm = _tile(s, 512)
    ga_blk = OFF_GATES // d

    def body(oa_ref, ob_ref, wa_ref, wb_ref, ga_ref, gb_ref, m_ref, ya_ref, yb_ref):
        for rows in _row_chunks(tm):
            ya, yb = _dot(oa_ref[rows, :], wa_ref[...]), _dot(ob_ref[rows, :], wb_ref[...])
            m_ref[rows, :] = (jax.nn.sigmoid(ga_ref[rows, :]) * ya
                              + jax.nn.sigmoid(gb_ref[rows, :]) * yb).astype(BF16)
            ya_ref[rows, :] = ya.astype(BF16)
            yb_ref[rows, :] = yb.astype(BF16)

    row = pl.BlockSpec((tm, d), lambda i: (i, 0))
    act = pl.BlockSpec((tm, o_a.shape[1]), lambda i: (i, 0))
    wspec = pl.BlockSpec(w_a.shape, lambda i: (0, 0))
    shp = jax.ShapeDtypeStruct((s, d), BF16)
    return _pcall(body, name=name, out_shape=(shp, shp, shp), grid=(s // tm,),
                  in_specs=[act, act, wspec, wspec, pl.BlockSpec((tm, d), lambda i: (i, ga_blk)),
                            pl.BlockSpec((tm, d), lambda i: (i, ga_blk + 1))],
                  out_specs=(row, row, row))(o_a, o_b, w_a, w_b, proj, proj)


def _mm_out_t_merge(dt, w_out, proj, y_a, y_b, *, name):
    s, d = dt.shape
    tm, tn = _tile(s, 1024), _tile(d, 512)
    ga_blk = OFF_GATES // tn

    def body(dt_ref, w_ref, ga_ref, gb_ref, ya_ref, yb_ref, dya_ref, dyb_ref, dga_ref, dgb_ref):
        w = w_ref[...]
        for rows in _row_chunks(tm):
            dm = _dot(dt_ref[rows, :], w, NT)
            sa, sb = jax.nn.sigmoid(ga_ref[rows, :]), jax.nn.sigmoid(gb_ref[rows, :])
            dya_ref[rows, :] = (dm * sa).astype(BF16)
            dyb_ref[rows, :] = (dm * sb).astype(BF16)
            dga_ref[rows, :] = (dm * ya_ref[rows, :] * (sa * (1.0 - sa))).astype(BF16)
            dgb_ref[rows, :] = (dm * yb_ref[rows, :] * (sb * (1.0 - sb))).astype(BF16)

    tile = pl.BlockSpec((tm, tn), lambda i, j: (i, j))
    shp = jax.ShapeDtypeStruct((s, d), BF16)
    return _pcall(body, name=name, out_shape=(shp,) * 4, grid=(s // tm, d // tn),
                  in_specs=[pl.BlockSpec((tm, d), lambda i, j: (i, 0)), pl.BlockSpec((tn, d), lambda i, j: (j, 0)),
                            pl.BlockSpec((tm, tn), lambda i, j: (i, ga_blk + j)),
                            pl.BlockSpec((tm, tn), lambda i, j: (i, ga_blk + d // tn + j)), tile, tile],
                  out_specs=(tile,) * 4)(dt, w_out, proj, proj, y_a, y_b)


def _mm_down_t_swiglu(df, w_down, g, u, *, name):
    s, d = df.shape
    f = w_down.shape[0]
    tm, tn = _tile(s, 1024), _tile(f, 512)

    def body(df_ref, w_ref, g_ref, u_ref, dg_ref, du_ref):
        w = w_ref[...]
        for rows in _row_chunks(tm):
            da = _dot(df_ref[rows, :], w, NT)
            gf = g_ref[rows, :].astype(F32)
            sg = jax.nn.sigmoid(gf)
            dg_ref[rows, :] = (da * u_ref[rows, :].astype(F32) * (sg * (1.0 + gf * (1.0 - sg)))).astype(BF16)
            du_ref[rows, :] = (da * (gf * sg)).astype(BF16)

    tile = pl.BlockSpec((tm, tn), lambda i, j: (i, j))
    shp = jax.ShapeDtypeStruct((s, f), BF16)
    return _pcall(body, name=name, out_shape=(shp, shp), grid=(s // tm, f // tn),
                  in_specs=[pl.BlockSpec((tm, d), lambda i, j: (i, 0)), pl.BlockSpec((tn, d), lambda i, j: (j, 0)),
                            tile, tile],
                  out_specs=(tile, tile))(df, w_down, g, u)


def _rmsmod_fwd(h, g, scale, shift, *, name):
    s, d = h.shape
    ts = _rows(s, d)

    def body(h_ref, g_ref, sc_ref, sh_ref, u_ref):
        hf = h_ref[...]
        r = lax.rsqrt(jnp.mean(hf * hf, axis=-1, keepdims=True) + EPS)
        u_ref[...] = (((hf * r) * g_ref[...]) * (1.0 + sc_ref[...]) + sh_ref[...]).astype(BF16)

    row = pl.BlockSpec((ts, d), lambda i: (i, 0))
    vec = pl.BlockSpec((1, d), lambda i: (0, 0))
    return _pcall(body, name=name, out_shape=jax.ShapeDtypeStruct((s, d), BF16), grid=(s // ts,),
                  in_specs=[row, vec, vec, vec], out_specs=row)(h, g, scale, shift)


def _gate_bwd(dhf, t_ref, gate_ref, dt_ref, dgate_ref):
    dt_ref[...] = (dhf * gate_ref[...]).astype(BF16)
    dgate_ref[...] += jnp.sum(dhf * t_ref[...], axis=0, keepdims=True)


def _rmsmod_bwd(du, h, g, scale, dres, t, gate, *, name):
    s, d = h.shape
    ts = _rows(s, d)
    chain = t is not None

    def body(*refs):
        du_ref, h_ref, g_ref, sc_ref, dres_ref = refs[:5]
        dh_ref, dsh_ref, dsc_ref, dg_ref = refs[-6:-2] if chain else refs[-4:]
        sums = (dsh_ref, dsc_ref, dg_ref) + ((refs[-1],) if chain else ())

        @pl.when(pl.program_id(0) == 0)
        def _():
            for ref in sums:
                ref[...] = jnp.zeros_like(ref)

        hf, duf, gain = h_ref[...], du_ref[...], g_ref[...]
        r = lax.rsqrt(jnp.mean(hf * hf, axis=-1, keepdims=True) + EPS)
        xh = hf * r
        dn = duf * (1.0 + sc_ref[...])
        dsh_ref[...] += jnp.sum(duf, axis=0, keepdims=True)
        dsc_ref[...] += jnp.sum(duf * (xh * gain), axis=0, keepdims=True)
        dg_ref[...] += jnp.sum(dn * xh, axis=0, keepdims=True)
        dxh = dn * gain
        dh = dres_ref[...] + r * (dxh - xh * jnp.mean(dxh * xh, axis=-1, keepdims=True))
        dh_ref[...] = dh
        if chain:
            _gate_bwd(dh, refs[5], refs[6], refs[-2], refs[-1])

    row = pl.BlockSpec((ts, d), lambda i: (i, 0))
    vec = pl.BlockSpec((1, d), lambda i: (0, 0))
    vshape = jax.ShapeDtypeStruct((1, d), F32)
    out_shape, out_specs = [jax.ShapeDtypeStruct((s, d), F32), vshape, vshape, vshape], [row, vec, vec, vec]
    in_specs, args = [row, row, vec, vec, row], [du, h, g, scale, dres]
    if chain:
        in_specs, args = in_specs + [row, vec], args + [t, gate]
        out_shape, out_specs = out_shape + [jax.ShapeDtypeStruct((s, d), BF16), vshape], out_specs + [row, vec]
    outs = _pcall(body, name=name, out_shape=tuple(out_shape), grid=(s // ts,), in_specs=in_specs,
                  out_specs=tuple(out_specs))(*args)
    return outs if chain else (*outs, None, None)


def _mm_swiglu(u2, w_gate_up, *, name):
    s, d = u2.shape
    f = w_gate_up.shape[1] // 2
    tm, tn = _tile(s, 1024), _tile(f, 512)
    nj = f // tn

    def body(x_ref, wg_ref, wu_ref, a_ref, g_ref, u_ref):
        for rows in _row_chunks(tm):
            x = x_ref[rows, :]
            gf, uf = _dot(x, wg_ref[...]), _dot(x, wu_ref[...])
            a_ref[rows, :] = ((gf * jax.nn.sigmoid(gf)) * uf).astype(BF16)
            g_ref[rows, :] = gf.astype(BF16)
            u_ref[rows, :] = uf.astype(BF16)

    out = pl.BlockSpec((tm, tn), lambda i, j: (i, j))
    shp = jax.ShapeDtypeStruct((s, f), BF16)
    return _pcall(body, name=name, out_shape=(shp, shp, shp), grid=(s // tm, nj),
                  in_specs=[pl.BlockSpec((tm, d), lambda i, j: (i, 0)), pl.BlockSpec((d, tn), lambda i, j: (0, j)),
                            pl.BlockSpec((d, tn), lambda i, j: (0, nj + j))],
                  out_specs=(out, out, out))(u2, w_gate_up, w_gate_up)


def _loss_fwd(y, tgt, t, gate, *, name):
    s, d = y.shape
    ts = _rows(s, d)

    def body(y_ref, tgt_ref, t_ref, gate_ref, l_ref, dy_ref, dt_ref, dgate_ref):
        @pl.when(pl.program_id(0) == 0)
        def _():
            l_ref[...] = jnp.zeros_like(l_ref)
            dgate_ref[...] = jnp.zeros_like(dgate_ref)

        e = y_ref[...] - tgt_ref[...]
        dy = e * (1.0 / d)
        dy_ref[...] = dy
        per_tok = jnp.sum(e * e, axis=1, keepdims=True) * (1.0 / d)
        l_ref[...] += 0.5 * jnp.sum(per_tok, axis=0, keepdims=True)
        _gate_bwd(dy, t_ref, gate_ref, dt_ref, dgate_ref)

    row = pl.BlockSpec((ts, d), lambda i: (i, 0))
    vec = pl.BlockSpec((1, d), lambda i: (0, 0))
    return _pcall(body, name=name,
                  out_shape=(jax.ShapeDtypeStruct((1, 128), F32), jax.ShapeDtypeStruct((s, d), F32),
                             jax.ShapeDtypeStruct((s, d), BF16), jax.ShapeDtypeStruct((1, d), F32)),
                  grid=(s // ts,), in_specs=[row, row, row, vec],
                  out_specs=(pl.BlockSpec((1, 128), lambda i: (0, 0)), row, row, vec))(y, tgt, t, gate)


def _rope_tables(seq):
    inv = jnp.power(ROPE_THETA, -jnp.arange(0, HEAD_DIM, 2, dtype=F32) / HEAD_DIM)
    ang = jnp.arange(seq, dtype=F32)[:, None] * inv[None, :]
    cos, sin = jnp.cos(ang), jnp.sin(ang)
    return jnp.concatenate([cos, cos], axis=1), jnp.concatenate([-sin, sin], axis=1)


def _qkrope_fwd(proj, gains, cos2, sin2, *, name):
    s = proj.shape[0]
    ts = _rows(s, A_W)

    def body(x_ref, g_ref, c_ref, s_ref, o_ref, o32_ref):
        gain, cos, sin = g_ref[...], c_ref[...], s_ref[...]
        for h in range(A_HEADS):
            lanes = slice(h * HEAD_DIM, (h + 1) * HEAD_DIM)
            x = x_ref[:, lanes]
            y = (x * lax.rsqrt(jnp.mean(x * x, axis=-1, keepdims=True) + EPS)) * gain
            out = y * cos + pltpu.roll(y, HEAD_DIM // 2, 1) * sin
            o_ref[:, lanes] = out.astype(BF16)
            o32_ref[:, lanes] = out

    heads = pl.BlockSpec((ts, A_W), lambda i, j: (i, j))
    tab = pl.BlockSpec((ts, HEAD_DIM), lambda i, j: (i, 0))
    gain = pl.BlockSpec((None, 1, HEAD_DIM), lambda i, j: (j, 0, 0))
    return _pcall(body, name=name,
                  out_shape=(jax.ShapeDtypeStruct((s, 2 * A_W), BF16), jax.ShapeDtypeStruct((s, 2 * A_W), F32)),
                  grid=(s // ts, 2), in_specs=[heads, gain, tab, tab], out_specs=(heads, heads))(
                      proj, gains, cos2, sin2)


def _qkrope_bwd(d_groups, proj, gains, which, cos2, sin2, *, name):
    s = proj.shape[0]
    ts = _rows(s, A_W)

    def body(d0_ref, d1_ref, d2_ref, x_ref, g_ref, c_ref, s_ref, dx_ref, dg_ref):
        @pl.when(pl.program_id(0) == 0)
        def _():
            dg_ref[...] = jnp.zeros_like(dg_ref)

        gain, cos, sin = g_ref[...], c_ref[...], s_ref[...]
        dg = jnp.zeros((1, HEAD_DIM), F32)
        for h in range(A_HEADS):
            lanes = slice(h * HEAD_DIM, (h + 1) * HEAD_DIM)
            slot = slice((h % HEADS_PER_GROUP) * HEAD_DIM, (h % HEADS_PER_GROUP + 1) * HEAD_DIM)
            dout = (d0_ref, d1_ref, d2_ref)[h // HEADS_PER_GROUP][:, slot]
            dy = dout * cos + pltpu.roll(dout * sin, HEAD_DIM // 2, 1)
            x = x_ref[:, lanes]
            r = lax.rsqrt(jnp.mean(x * x, axis=-1, keepdims=True) + EPS)
            xh = x * r
            dg = dg + jnp.sum(dy * xh, axis=0, keepdims=True)
            dxh = dy * gain
            dx_ref[:, lanes] = (r * (dxh - xh * jnp.mean(dxh * xh, axis=-1, keepdims=True))).astype(BF16)
        dg_ref[...] += dg

    group = pl.BlockSpec((ts, GROUP_W), lambda i: (i, 0))
    tab = pl.BlockSpec((ts, HEAD_DIM), lambda i: (i, 0))
    gain = pl.BlockSpec((None, 1, HEAD_DIM), lambda i: (which, 0, 0))
    return _pcall(body, name=name,
                  out_shape=(jax.ShapeDtypeStruct((s, A_W), BF16), jax.ShapeDtypeStruct((1, HEAD_DIM), F32)),
                  grid=(s // ts,),
                  in_specs=[group, group, group, pl.BlockSpec((ts, A_W), lambda i: (i, which)), gain, tab, tab],
                  out_specs=(pl.BlockSpec((ts, A_W), lambda i: (i, 0)), pl.BlockSpec((1, HEAD_DIM), lambda i: (0, 0))))(
                      *d_groups, proj, gains, cos2, sin2)


def _assemble(pieces, *, name):
    s = pieces[0].shape[0]
    widths = [p.shape[1] for p in pieces]
    total = sum(widths)
    ts = _rows(s, total // 2)

    def body(*refs):
        o_ref, off = refs[-1], 0
        for x_ref, w in zip(refs[:-1], widths):
            o_ref[:, off:off + w] = x_ref[...].astype(BF16)
            off += w

    return _pcall(body, name=name, out_shape=jax.ShapeDtypeStruct((s, total), BF16), grid=(s // ts,),
                  in_specs=[pl.BlockSpec((ts, w), lambda i: (i, 0)) for w in widths],
                  out_specs=pl.BlockSpec((ts, total), lambda i: (i, 0)))(*pieces)


def _block_rows(blk):
    if isinstance(blk, int):
        return pl.ds(blk * BLOCK, BLOCK)
    return pl.ds(pl.multiple_of(blk * BLOCK, BLOCK), BLOCK)


def _band_window(n, length):
    width = min(2 * BLOCK, length)
    row = lax.broadcasted_iota(jnp.int32, (BLOCK, width), 0)
    col = lax.broadcasted_iota(jnp.int32, (BLOCK, width), 1)
    if width == BLOCK:
        return pl.ds(0, BLOCK), col <= row
    first = n - 1 if isinstance(n, int) else jnp.maximum(n - 1, 0)
    first = max(first, 0) if isinstance(first, int) else first
    dist = row - col + (n - first) * BLOCK
    start = first * BLOCK if isinstance(first, int) else pl.multiple_of(first * BLOCK, BLOCK)
    return pl.ds(start, width), jnp.logical_and(dist >= 0, dist <= BLOCK)


def _dil_fwd(q_arr, k_arr, v_arr, offs, length, dil, *, name):
    nj, nb = dil * HEADS_PER_GROUP, length // BLOCK
    ju, nq = (HEADS_PER_GROUP, 2) if nb > 1 else (2 * HEADS_PER_GROUP, 1)
    qo, ko, vo = (off // ju for off in offs)
    assert all(off % ju == 0 for off in offs) and nb % nq == 0 and nj % ju == 0

    def body(q_ref, k_ref, v_ref, o_ref, l_ref):
        for qq in range(nq):
            qrows = slice(qq * BLOCK, (qq + 1) * BLOCK)
            rows, mask = _band_window(pl.program_id(1) * nq + qq, length)
            for cb in range(ju):
                lanes = slice(cb * HEAD_DIM, (cb + 1) * HEAD_DIM)
                sc = _dot(q_ref[qrows, lanes].astype(BF16), k_ref[rows, lanes].astype(BF16), NT) * ATT_SCALE
                sc = jnp.where(mask, sc, MASKED)
                m = sc.max(axis=-1, keepdims=True)
                p = jnp.exp(sc - m)
                den = jnp.sum(p, axis=-1, keepdims=True)
                acc = _dot(p.astype(BF16), v_ref[rows, lanes].astype(BF16))
                o_ref[qrows, lanes] = acc / den
                l_ref[qrows, lanes] = jnp.broadcast_to(m + jnp.log(den), (BLOCK, HEAD_DIM))

    qspec = pl.BlockSpec((nq * BLOCK, ju * HEAD_DIM), lambda j, n: (n, qo + j))
    kspec = pl.BlockSpec((length, ju * HEAD_DIM), lambda j, n: (0, ko + j))
    vspec = pl.BlockSpec((length, ju * HEAD_DIM), lambda j, n: (0, vo + j))
    ospec = pl.BlockSpec((nq * BLOCK, ju * HEAD_DIM), lambda j, n: (n, j))
    shp = jax.ShapeDtypeStruct((length, nj * HEAD_DIM), F32)
    return _pcall(body, name=name, out_shape=(shp, shp), grid=(nj // ju, nb // nq), in_specs=[qspec, kspec, vspec],
                  out_specs=(ospec, ospec))(q_arr, k_arr, v_arr)


def _dil_bwd(q_arr, k_arr, v_arr, offs, o, lse, do, dlse, length, dil, *, name):
    nj, nb = dil * HEADS_PER_GROUP, length // BLOCK
    ju = 2 * HEADS_PER_GROUP if length <= 4 * BLOCK else 2
    qo, ko, vo = (off // ju for off in offs)
    assert all(off % ju == 0 for off in offs)

    def body(q_ref, k_ref, v_ref, o_ref, l_ref, do_ref, dl_ref, dq_ref, dk_ref, dv_ref):
        dk_ref[...] = jnp.zeros_like(dk_ref)
        dv_ref[...] = jnp.zeros_like(dv_ref)

        def step(n, carry):
            qrows = _block_rows(n)
            rows, mask = _band_window(n, length)
            for cb in range(ju):
                lanes = slice(cb * HEAD_DIM, (cb + 1) * HEAD_DIM)
                q = q_ref[qrows, lanes].astype(BF16)
                dof = do_ref[qrows, lanes]
                dob = dof.astype(BF16)
                lse_c = l_ref[qrows, lanes][:, :1]
                shift = dl_ref[qrows, lanes][:, :1] - jnp.sum(dof * o_ref[qrows, lanes], axis=-1, keepdims=True)
                kk, vv = k_ref[rows, lanes].astype(BF16), v_ref[rows, lanes].astype(BF16)
                sc = _dot(q, kk, NT) * ATT_SCALE
                p = jnp.where(mask, jnp.exp(sc - lse_c), 0.0)
                ds = (p * (_dot(dob, vv, NT) + shift)).astype(BF16)
                dq_ref[qrows, lanes] = _dot(ds, kk) * ATT_SCALE
                dk_ref[rows, lanes] += _dot(ds, q, TN) * ATT_SCALE
                dv_ref[rows, lanes] += _dot(p.astype(BF16), dob, TN)
            return carry

        if nb == 1:
            step(0, 0)
        else:
            lax.fori_loop(0, nb, step, 0)

    def col(off):
        return pl.BlockSpec((length, ju * HEAD_DIM), lambda j: (0, off + j))

    shp = jax.ShapeDtypeStruct((length, nj * HEAD_DIM), F32)
    return _pcall(body, name=name, out_shape=(shp, shp, shp), grid=(nj // ju,),
                  in_specs=[col(qo), col(ko), col(vo), col(0), col(0), col(0), col(0)],
                  out_specs=(col(0), col(0), col(0)))(q_arr, k_arr, v_arr, o, lse, do, dlse)


DIL_RESIDUES_PER_STEP = 4


def _dil_tokens(n, r, dil, length):
    width = min(2 * BLOCK, length)
    _, mask = _band_window(n, length)
    first = 0 if width == BLOCK else jnp.maximum(n - 1, 0)
    return (pl.ds(n * (BLOCK * dil) + r, BLOCK, stride=dil), pl.ds(first * (BLOCK * dil) + r, width, stride=dil),
            mask)


def _dil_head_specs(seq, group):
    first = group * HEADS_PER_GROUP

    def col(c0):
        return pl.BlockSpec((seq, HEAD_DIM), lambda h, r: (0, c0 + h))

    return col(first), col(A_HEADS + first), col(OFF_VA // HEAD_DIM + first), col(0)


def _dil_fwd_strided(qk32, proj, group, dil, *, name):
    seq = proj.shape[0]
    length = seq // dil
    nb, rp = length // BLOCK, DIL_RESIDUES_PER_STEP
    assert dil % rp == 0

    def body(q_ref, k_ref, v_ref, o_ref, l_ref):
        rgroup = pl.program_id(1)

        def step(n, carry):
            for rr in range(rp):
                tok_q, tok_k, mask = _dil_tokens(n, rgroup * rp + rr, dil, length)
                sc = _dot(q_ref[tok_q, :].astype(BF16), k_ref[tok_k, :].astype(BF16), NT) * ATT_SCALE
                sc = jnp.where(mask, sc, MASKED)
                m = sc.max(axis=-1, keepdims=True)
                p = jnp.exp(sc - m)
                den = jnp.sum(p, axis=-1, keepdims=True)
                o_ref[tok_q, :] = _dot(p.astype(BF16), v_ref[tok_k, :].astype(BF16)) / den
                l_ref[tok_q, :] = jnp.broadcast_to(m + jnp.log(den), (BLOCK, HEAD_DIM))
            return carry

        if nb == 1:
            step(0, 0)
        else:
            lax.fori_loop(0, nb, step, 0)

    qs, ks, vs, nat = _dil_head_specs(seq, group)
    shp = jax.ShapeDtypeStruct((seq, GROUP_W), F32)
    return _pcall(body, name=name, out_shape=(shp, shp), grid=(HEADS_PER_GROUP, dil // rp), in_specs=[qs, ks, vs],
                  out_specs=(nat, nat))(qk32, qk32, proj)


def _dil_bwd_strided(qk32, proj, group, o, lse, do, dlse, dil, *, name):
    seq = proj.shape[0]
    length = seq // dil
    nb, rp = length // BLOCK, DIL_RESIDUES_PER_STEP
    assert dil % rp == 0

    def body(q_ref, k_ref, v_ref, o_ref, l_ref, do_ref, dl_ref, dq_ref, dk_ref, dv_ref):
        rgroup = pl.program_id(1)

        @pl.when(rgroup == 0)
        def _():
            dk_ref[...] = jnp.zeros_like(dk_ref)
            dv_ref[...] = jnp.zeros_like(dv_ref)

        def step(n, carry):
            for rr in range(rp):
                tok_q, tok_k, mask = _dil_tokens(n, rgroup * rp + rr, dil, length)
                q = q_ref[tok_q, :].astype(BF16)
                dof = do_ref[tok_q, :]
                dob = dof.astype(BF16)
                lse_c = l_ref[tok_q, :][:, :1]
                shift = dl_ref[tok_q, :][:, :1] - jnp.sum(dof * o_ref[tok_q, :], axis=-1, keepdims=True)
                kk, vv = k_ref[tok_k, :].astype(BF16), v_ref[tok_k, :].astype(BF16)
                sc = _dot(q, kk, NT) * ATT_SCALE
                p = jnp.where(mask, jnp.exp(sc - lse_c), 0.0)
                ds = (p * (_dot(dob, vv, NT) + shift)).astype(BF16)
                dq_ref[tok_q, :] = _dot(ds, kk) * ATT_SCALE
                dk_ref[tok_k, :] += _dot(ds, q, TN) * ATT_SCALE
                dv_ref[tok_k, :] += _dot(p.astype(BF16), dob, TN)
            return carry

        if nb == 1:
            step(0, 0)
        else:
            lax.fori_loop(0, nb, step, 0)

    qs, ks, vs, nat = _dil_head_specs(seq, group)
    shp = jax.ShapeDtypeStruct((seq, GROUP_W), F32)
    return _pcall(body, name=name, out_shape=(shp, shp, shp), grid=(HEADS_PER_GROUP, dil // rp),
                  in_specs=[qs, ks, vs, nat, nat, nat, nat], out_specs=(nat, nat, nat))(
                      qk32, qk32, proj, o, lse, do, dlse)


def _combine_weights(l_refs):
    ls = [r[...] for r in l_refs]
    m = jnp.maximum(jnp.maximum(ls[0], ls[1]), ls[2])
    es = [jnp.exp(l - m) for l in ls]
    den = es[0] + es[1] + es[2]
    return [e / den for e in es]


def _combine_fwd(os_, lses, *, name):
    s = os_[0].shape[0]
    ts = _rows(s, GROUP_W)

    def body(o0, o1, o2, l0, l1, l2, out_ref):
        w = _combine_weights((l0, l1, l2))
        out_ref[...] = (w[0] * o0[...] + w[1] * o1[...] + w[2] * o2[...]).astype(BF16)

    row = pl.BlockSpec((ts, GROUP_W), lambda i: (i, 0))
    return _pcall(body, name=name, out_shape=jax.ShapeDtypeStruct((s, GROUP_W), BF16), grid=(s // ts,),
                  in_specs=[row] * 6, out_specs=row)(*os_, *lses)


def _combine_bwd(do_a, os_, lses, *, name):
    s = do_a.shape[0]
    ts = _rows(s, GROUP_W)

    def body(d_ref, o0, o1, o2, l0, l1, l2, do0, do1, do2, dl0, dl1, dl2):
        w = _combine_weights((l0, l1, l2))
        d = d_ref[...]
        og = [o0[...], o1[...], o2[...]]
        oa = w[0] * og[0] + w[1] * og[1] + w[2] * og[2]
        ta = jnp.sum(d * oa, axis=-1, keepdims=True)
        for g, (do_ref, dl_ref) in enumerate(((do0, dl0), (do1, dl1), (do2, dl2))):
            do_ref[...] = w[g] * d
            dl_ref[...] = w[g] * (jnp.sum(d * og[g], axis=-1, keepdims=True) - ta)

    head = pl.BlockSpec((ts, HEAD_DIM), lambda i, h: (i, h))
    shp = jax.ShapeDtypeStruct((s, GROUP_W), F32)
    return _pcall(body, name=name, out_shape=(shp,) * 6, grid=(s // ts, HEADS_PER_GROUP),
                  in_specs=[head] * 7, out_specs=(head,) * 6)(do_a, *os_, *lses)


def _dot_exact(x, ones_mask):
    hi = x.astype(BF16)
    r1 = x - hi.astype(F32)
    mid = r1.astype(BF16)
    lo = (r1 - mid.astype(F32)).astype(BF16)
    return _dot(hi, ones_mask) + _dot(mid, ones_mask) + _dot(lo, ones_mask)


SB_QROWS = 2 * BLOCK
SB_UNROLL = 4
SB_HEADS_PER_STEP = 2
SB_LANES = [slice(hh * HEAD_DIM, (hh + 1) * HEAD_DIM) for hh in range(SB_HEADS_PER_STEP)]


def _sb_mask(j, i):
    row = lax.broadcasted_iota(jnp.int32, (SB_QROWS, BLOCK), 0)
    col = lax.broadcasted_iota(jnp.int32, (SB_QROWS, BLOCK), 1)
    return col + (j * BLOCK - i * SB_QROWS) < row


def _sb_steps(i):
    return ((i + 1) * (SB_QROWS // BLOCK) + SB_UNROLL - 1) // SB_UNROLL


def _sb_scores(q, kk, j, i, masked):
    mask = _sb_mask(j, i) if masked else None
    z = _dot(q, kk, NT) * ATT_SCALE
    sp = jnp.log(1.0 + jnp.exp(-jnp.abs(z)))
    log_beta = jnp.minimum(z, 0.0) - sp
    log_1mb = jnp.minimum(-z, 0.0) - sp
    if masked:
        log_1mb = jnp.where(mask, log_1mb, 0.0)
    return z, log_beta, log_1mb, mask


def _sb_weights(log_beta, log_1mb, mask, run, upper):
    a = jnp.exp(log_beta + (run + _dot_exact(log_1mb, upper)))
    return a if mask is None else jnp.where(mask, a, 0.0)


def _sb_peeled(nsteps, make_step, init, masked_first):
    if masked_first:
        return lax.fori_loop(1, nsteps, make_step(False), make_step(True)(0, init))
    return make_step(True)(nsteps - 1, lax.fori_loop(0, nsteps - 1, make_step(False), init))


def _tri(strict_lower):
    row = lax.broadcasted_iota(jnp.int32, (BLOCK, BLOCK), 0)
    col = lax.broadcasted_iota(jnp.int32, (BLOCK, BLOCK), 1)
    return ((row > col) if strict_lower else (row < col)).astype(BF16)


def _sb_fwd(proj, *, name):
    s = proj.shape[0]
    assert s % (BLOCK * SB_UNROLL) == 0 and s % SB_QROWS == 0

    def body(q_ref, k_ref, v_ref, o_ref):
        i = pl.program_id(1)
        qs = [q_ref[:, lanes].astype(BF16) for lanes in SB_LANES]
        upper = _tri(True)
        nsteps = _sb_steps(i)

        def make_step(masked):
            def step(t, carry):
                carry = list(carry)
                for b in reversed(range(SB_UNROLL)):
                    j = (nsteps - 1 - t) * SB_UNROLL + b
                    rows = _block_rows(j)
                    for hh, lanes in enumerate(SB_LANES):
                        acc, run = carry[hh]
                        _, log_beta, log_1mb, mask = _sb_scores(qs[hh], k_ref[rows, lanes].astype(BF16), j, i, masked)
                        a = _sb_weights(log_beta, log_1mb, mask, run, upper)
                        carry[hh] = (acc + _dot(a.astype(BF16), v_ref[rows, lanes].astype(BF16)),
                                     run + jnp.sum(log_1mb, axis=-1, keepdims=True))
                return tuple(carry)
            return step

        zero = (jnp.zeros((SB_QROWS, HEAD_DIM), F32), jnp.zeros((SB_QROWS, 1), F32))
        for lanes, (acc, _) in zip(SB_LANES, _sb_peeled(nsteps, make_step, (zero,) * SB_HEADS_PER_STEP, True)):
            o_ref[:, lanes] = acc.astype(BF16)

    width = SB_HEADS_PER_STEP * HEAD_DIM
    qb, kb, vb = (off // width for off in (OFF_QB, OFF_KB, OFF_VB))
    return _pcall(body, name=name, out_shape=jax.ShapeDtypeStruct((s, B_W), BF16),
                  grid=(SB_HEADS // SB_HEADS_PER_STEP, s // SB_QROWS),
                  in_specs=[pl.BlockSpec((SB_QROWS, width), lambda h, i: (i, qb + h)),
                            pl.BlockSpec((s, width), lambda h, i: (0, kb + h)),
                            pl.BlockSpec((s, width), lambda h, i: (0, vb + h))],
                  out_specs=pl.BlockSpec((SB_QROWS, width), lambda h, i: (i, h)))(proj, proj, proj)


def _sb_bwd(proj, do_b, *, name):
    s = proj.shape[0]
    assert s % (BLOCK * SB_UNROLL) == 0 and s % SB_QROWS == 0
    nkb = s // BLOCK

    def body(q_ref, k_ref, v_ref, do_ref, dq_ref, dk_ref, dv_ref, z_s, a_s):
        i = pl.program_id(1)

        @pl.when(i == 0)
        def _():
            dk_ref[...] = jnp.zeros_like(dk_ref)
            dv_ref[...] = jnp.zeros_like(dv_ref)

        qs = [q_ref[:, lanes].astype(BF16) for lanes in SB_LANES]
        dobs = [do_ref[:, lanes].astype(BF16) for lanes in SB_LANES]
        upper, lower = _tri(True), _tri(False)
        nsteps = _sb_steps(i)

        def make_recompute(masked):
            def recompute(t, runs):
                runs = list(runs)
                for b in reversed(range(SB_UNROLL)):
                    j = (nsteps - 1 - t) * SB_UNROLL + b
                    rows = _block_rows(j)
                    for hh, lanes in enumerate(SB_LANES):
                        z, log_beta, log_1mb, mask = _sb_scores(qs[hh], k_ref[rows, lanes].astype(BF16), j, i, masked)
                        z_s[hh, j] = z
                        a_s[hh, j] = _sb_weights(log_beta, log_1mb, mask, runs[hh], upper)
                        runs[hh] = runs[hh] + jnp.sum(log_1mb, axis=-1, keepdims=True)
                return tuple(runs)
            return recompute

        _sb_peeled(nsteps, make_recompute, (jnp.zeros((SB_QROWS, 1), F32),) * SB_HEADS_PER_STEP, True)

        def make_grads(masked):
            def grads(t, carry):
                carry = list(carry)
                for b in range(SB_UNROLL):
                    j = t * SB_UNROLL + b
                    rows = _block_rows(j)
                    for hh, lanes in enumerate(SB_LANES):
                        dq, run = carry[hh]
                        kk, vv = k_ref[rows, lanes].astype(BF16), v_ref[rows, lanes].astype(BF16)
                        z, a = z_s[hh, j], a_s[hh, j]
                        de = _dot(dobs[hh], vv, NT) * a
                        beta = jax.nn.sigmoid(z)
                        one_minus_beta = 1.0 - beta
                        if masked:
                            beta = jnp.where(_sb_mask(j, i), beta, 0.0)
                        dz = (de * one_minus_beta - beta * (run + _dot_exact(de, lower))).astype(BF16)
                        dk_ref[rows, lanes] += _dot(dz, qs[hh], TN) * ATT_SCALE
                        dv_ref[rows, lanes] += _dot(a.astype(BF16), dobs[hh], TN)
                        carry[hh] = (dq + _dot(dz, kk), run + jnp.sum(de, axis=-1, keepdims=True))
                return tuple(carry)
            return grads

        zero = (jnp.zeros((SB_QROWS, HEAD_DIM), F32), jnp.zeros((SB_QROWS, 1), F32))
        for lanes, (dq, _) in zip(SB_LANES, _sb_peeled(nsteps, make_grads, (zero,) * SB_HEADS_PER_STEP, False)):
            dq_ref[:, lanes] = dq * ATT_SCALE

    width = SB_HEADS_PER_STEP * HEAD_DIM
    qb, kb, vb = (off // width for off in (OFF_QB, OFF_KB, OFF_VB))
    blk = pl.BlockSpec((SB_QROWS, width), lambda h, i: (i, h))
    full = pl.BlockSpec((s, width), lambda h, i: (0, h))
    shp = jax.ShapeDtypeStruct((s, B_W), F32)
    saved = pltpu.VMEM((SB_HEADS_PER_STEP, nkb, SB_QROWS, BLOCK), F32)
    return _pcall(body, name=name, out_shape=(shp, shp, shp), grid=(SB_HEADS // SB_HEADS_PER_STEP, s // SB_QROWS),
                  in_specs=[pl.BlockSpec((SB_QROWS, width), lambda h, i: (i, qb + h)),
                            pl.BlockSpec((s, width), lambda h, i: (0, kb + h)),
                            pl.BlockSpec((s, width), lambda h, i: (0, vb + h)), blk],
                  out_specs=(blk, full, full), scratch=[saved, saved])(proj, proj, proj, do_b)


def _coords():
    return lax.axis_index("x"), lax.axis_index("y"), lax.axis_index("c")


def _flip(v, bit):
    return 1 - v if bit else v


def _shard_of(ref, axis, idx, size):
    if axis == 0:
        sl = pl.ds(pl.multiple_of(idx * size, 16), size)
        return ref.at[sl, :] if len(ref.shape) == 2 else ref.at[:, sl, :]
    sl = pl.ds(pl.multiple_of(idx * size, 128), size)
    return ref.at[:, sl] if len(ref.shape) == 2 else ref.at[:, :, sl]


def _small_allgather(v, *, name, silu=False):
    n = v.shape[1]

    def body(v_ref, out_ref, send_sems, recv_sems):
        x, y, c = _coords()
        me = 4 * x + 2 * y + c
        val = v_ref[...]
        out_ref[me] = val * jax.nn.sigmoid(val) if silu else val
        copies = []
        for k in range(1, N_DEV):
            peer = (_flip(x, k & 4), _flip(y, k & 2), _flip(c, k & 1))
            copies.append(pltpu.make_async_remote_copy(
                src_ref=out_ref.at[me], dst_ref=out_ref.at[me], send_sem=send_sems.at[k - 1],
                recv_sem=recv_sems.at[k - 1], device_id=peer, device_id_type=MESH))
        for cp in copies:
            cp.start()
        for cp in copies:
            cp.wait_recv()
        for cp in copies:
            cp.wait_send()

    return _pcall(body, name=name, out_shape=jax.ShapeDtypeStruct((N_DEV, 1, n), F32),
                  in_specs=[pl.BlockSpec(memory_space=pltpu.VMEM)], out_specs=pl.BlockSpec(memory_space=pltpu.VMEM),
                  scratch=[pltpu.SemaphoreType.DMA((N_DEV - 1,)), pltpu.SemaphoreType.DMA((N_DEV - 1,))])(v)


def _cast_place(w, layer, axis, me, *, name):
    _, r, c = w.shape
    tr = _rows(r, c)
    nrt = r // tr

    def body(me_ref, w_ref, o_ref):
        o_ref[...] = w_ref[...].astype(BF16)

    wspec = pl.BlockSpec((None, tr, c), lambda i, me_ref: (layer, i, 0))
    if axis == 0:
        ospec = pl.BlockSpec((tr, c), lambda i, me_ref: (me_ref[0] * nrt + i, 0))
        shape = (r * N_DEV, c)
    else:
        ospec = pl.BlockSpec((tr, c), lambda i, me_ref: (i, me_ref[0]))
        shape = (r, c * N_DEV)
    return _pcall(body, name=name, out_shape=jax.ShapeDtypeStruct(shape, BF16), grid=(nrt,), in_specs=[wspec],
                  out_specs=ospec, prefetch=1)(me, w)


def _pair_sum(grad, sib, core, axis, *, name):
    _, r, c = sib.shape
    tr = _rows(r, c // 2)
    nrt = r // tr

    def body(core_ref, g_ref, s_ref, o_ref):
        o_ref[...] = (g_ref[...].astype(F32) + s_ref[...].astype(F32)).astype(BF16)

    if axis == 0:
        gspec = pl.BlockSpec((tr, c), lambda q, i, core_ref: ((2 * q + core_ref[0]) * nrt + i, 0))
    else:
        gspec = pl.BlockSpec((tr, c), lambda q, i, core_ref: (i, 2 * q + core_ref[0]))
    sspec = pl.BlockSpec((None, tr, c), lambda q, i, core_ref: (q, i, 0))
    return _pcall(body, name=name, out_shape=jax.ShapeDtypeStruct(sib.shape, BF16), grid=(N_CHIPS, nrt),
                  in_specs=[gspec, sspec], out_specs=sspec, prefetch=1)(core, grad, sib)


ANY_SPEC = pl.BlockSpec(memory_space=pl.ANY)
SEM_SPEC = pl.BlockSpec(memory_space=pltpu.SEMAPHORE)
SPLIT_PARAMS = dict(has_side_effects=pltpu.SideEffectType.DATAFLOW_SIDE_EFFECTING)


def _split_start(copies_fn, buffers, sem_shape, after, *, name):
    n = len(buffers)
    rows, cols = sem_shape
    ns = rows * cols
    extra = ([] if after is None else [after]) + _take_token()

    def body(*refs):
        sems = refs[n + len(extra):n + len(extra) + 2 * ns]
        for cp in copies_fn(refs[:n], _sem_rows(sems[:ns], cols), _sem_rows(sems[ns:], cols)):
            cp.start()
        refs[-1][...] = jnp.zeros_like(refs[-1])

    sem = pltpu.SemaphoreType.DMA(())
    outs = pl.pallas_call(
        body, name=name,
        out_shape=((sem,) * (2 * ns) + tuple(jax.ShapeDtypeStruct(b.shape, b.dtype) for b in buffers) + (TOKEN,)),
        in_specs=(ANY_SPEC,) * (n + len(extra)),
        out_specs=(SEM_SPEC,) * (2 * ns) + (ANY_SPEC,) * n + (pl.BlockSpec(memory_space=pltpu.VMEM),),
        input_output_aliases={i: 2 * ns + i for i in range(n)},
        compiler_params=pltpu.CompilerParams(**SPLIT_PARAMS))(*buffers, *extra)
    _ORDER["token"] = outs[-1]
    return list(outs[:ns]), list(outs[ns:2 * ns]), list(outs[2 * ns:2 * ns + n]), outs[-1]


def _split_wait(copies_fn, send_sems, recv_sems, buffers, after, sem_rows, *, name):
    n, ns = len(buffers), len(send_sems)
    cols = ns // sem_rows
    extra = ([] if after is None else [after]) + _take_token()

    def body(*refs):
        sems = refs[n:n + 2 * ns]
        copies = copies_fn(refs[:n], _sem_rows(sems[:ns], cols), _sem_rows(sems[ns:], cols))
        for cp in copies:
            cp.wait_send()
        for cp in copies:
            cp.wait_recv()
        refs[-1][...] = jnp.zeros_like(refs[-1])

    outs = pl.pallas_call(
        body, name=name, out_shape=tuple(jax.ShapeDtypeStruct(b.shape, b.dtype) for b in buffers) + (TOKEN,),
        in_specs=(ANY_SPEC,) * n + (SEM_SPEC,) * (2 * ns) + (ANY_SPEC,) * len(extra),
        out_specs=(ANY_SPEC,) * n + (pl.BlockSpec(memory_space=pltpu.VMEM),),
        input_output_aliases={i: i for i in range(n)},
        compiler_params=pltpu.CompilerParams(**SPLIT_PARAMS))(*buffers, *send_sems, *recv_sems, *extra)
    _ORDER["token"] = outs[-1]
    return list(outs[:n])


def _sem_rows(sems, cols):
    return [sems[i:i + cols] for i in range(0, len(sems), cols)]


def _empty_hbm(shape, dtype):
    return pltpu.with_memory_space_constraint(lax.empty(shape, dtype), pltpu.HBM)


class _SplitGather:
    def __init__(self, fulls, axes, tag):
        self.axes, self.tag, self.nt = list(axes), tag, len(fulls)
        self.sizes = [f.shape[ax] // N_DEV for f, ax in zip(fulls, axes)]
        self.fulls = list(fulls)

    def _slot(self, ref, t, dev):
        return _shard_of(ref, self.axes[t], 4 * dev[0] + 2 * dev[1] + dev[2], self.sizes[t])

    def _first_copies(self, refs, send_sems, recv_sems):
        x, y, c = _coords()
        peers = [(x, y, 1 - c), (1 - x, y, c), (x, 1 - y, c), (1 - x, 1 - y, c)]
        return [pltpu.make_async_remote_copy(
            src_ref=self._slot(refs[t], t, (x, y, c)), dst_ref=self._slot(refs[t], t, (x, y, c)),
            send_sem=send_sems[t][k], recv_sem=recv_sems[t][k], device_id=peer, device_id_type=MESH)
            for t in range(self.nt) for k, peer in enumerate(peers)]

    def _forward_copies(self, refs, send_sems, recv_sems):
        x, y, c = _coords()
        chips = [(1 - x, y), (x, 1 - y), (1 - x, 1 - y)]
        return [pltpu.make_async_remote_copy(
            src_ref=self._slot(refs[t], t, (*chip, c)), dst_ref=self._slot(refs[t], t, (*chip, c)),
            send_sem=send_sems[t][j], recv_sem=recv_sems[t][j], device_id=(x, y, 1 - c), device_id_type=MESH)
            for t in range(self.nt) for j, chip in enumerate(chips)]

    def first(self, after):
        self.s1, self.r1, self.fulls, token = _split_start(
            self._first_copies, self.fulls, (self.nt, 4), after, name=f"comm_gather1_start_{self.tag}")
        return token

    def forward(self, after):
        bufs = _split_wait(self._first_copies, self.s1, self.r1, self.fulls, after, self.nt,
                           name=f"comm_gather1_wait_{self.tag}")
        self.s2, self.r2, self.fulls, token = _split_start(
            self._forward_copies, bufs, (self.nt, 3), after, name=f"comm_gather2_start_{self.tag}")
        return token

    def finish(self, after):
        return _split_wait(self._forward_copies, self.s2, self.r2, self.fulls, after, self.nt,
                           name=f"comm_gather2_wait_{self.tag}")


class _SplitPairExchange:
    def __init__(self, grads, axes, tag):
        self.nt, self.tag, self.axes = len(grads), tag, list(axes)
        self.grads = list(grads)
        self.sizes = [g.shape[ax] // N_DEV for g, ax in zip(grads, axes)]

    def _copies(self, refs, send_sems, recv_sems):
        nt = self.nt
        x, y, c = _coords()
        return [pltpu.make_async_remote_copy(
            src_ref=_shard_of(refs[t], self.axes[t], 2 * q + 1 - c, self.sizes[t]), dst_ref=refs[nt + t].at[q],
            send_sem=send_sems[t][q], recv_sem=recv_sems[t][q], device_id=(x, y, 1 - c), device_id_type=MESH)
            for t in range(nt) for q in range(N_CHIPS)]

    def start(self):
        landing = []
        for g, ax in zip(self.grads, self.axes):
            dims = list(g.shape)
            dims[ax] //= N_DEV
            landing.append(_empty_hbm((N_CHIPS, *dims), g.dtype))
        self.s, self.r, self.bufs, token = _split_start(
            self._copies, self.grads + landing, (self.nt, N_CHIPS), None,
            name=f"comm_rs_pair_start_{self.tag}")
        return token

    def finish(self, after):
        bufs = _split_wait(self._copies, self.s, self.r, self.bufs, after, self.nt,
                           name=f"comm_rs_pair_wait_{self.tag}")
        return bufs[:self.nt], bufs[self.nt:]


class _ReducePipeline:
    def __init__(self, core):
        self.core, self.items, self.done, self.now = core, [], [], 0

    def add(self, keys, grads, layer):
        axes = [SHARD_AXIS[k] for k in keys]
        pair = _SplitPairExchange([grads[k] for k in keys], axes, f"{keys[0]}{layer}")
        pair.start()
        self.items.append(dict(keys=keys, layer=layer, axes=axes, pair=pair, state="pair", since=self.now))

    def tick(self, after, flush=False):
        self.now += 1
        for it in self.items:
            if it["state"] == "pair" and it["since"] < self.now:
                grads, sib = it["pair"].finish(after)
                sums = [_pair_sum(g, s_, self.core, ax, name="pair_sum_" + k)
                        for k, g, s_, ax in zip(it["keys"], grads, sib, it["axes"])]
                it["chip"] = _SplitChipExchange(sums, f"{it['keys'][0]}{it['layer']}")
                it["chip"].start()
                it.update(state="chip", since=self.now)
            elif it["state"] == "chip" and (flush or self.now - it["since"] >= 2):
                sums, remote = it["chip"].finish(after)
                self.done.append((it["keys"], it["layer"], sums, remote))
                it["state"] = "done"

    def take_done(self):
        out, self.done = self.done, []
        return out


class _SplitChipExchange:
    def __init__(self, sums, tag):
        self.nt, self.tag = len(sums), tag
        self.sums = list(sums)

    def _copies(self, refs, send_sems, recv_sems):
        nt = self.nt
        x, y, c = _coords()
        copies = []
        for t in range(nt):
            for k in range(1, N_CHIPS):
                px, py = _flip(x, k & 2), _flip(y, k & 1)
                copies.append(pltpu.make_async_remote_copy(
                    src_ref=refs[t].at[2 * px + py], dst_ref=refs[nt + t].at[k - 1], send_sem=send_sems[t][k - 1],
                    recv_sem=recv_sems[t][k - 1], device_id=(px, py, c), device_id_type=MESH))
        return copies

    def start(self):
        landing = [_empty_hbm((N_CHIPS - 1,) + s.shape[1:], s.dtype) for s in self.sums]
        self.s, self.r, self.bufs, token = _split_start(
            self._copies, self.sums + landing, (self.nt, N_CHIPS - 1), None,
            name=f"comm_rs_chip_start_{self.tag}")
        return token

    def finish(self, after):
        bufs = _split_wait(self._copies, self.s, self.r, self.bufs, after, self.nt,
                           name=f"comm_rs_chip_wait_{self.tag}")
        return bufs[:self.nt], bufs[self.nt:]


def _adam_math(g, w, m, v):
    m2 = ADAM_B1 * m + (1.0 - ADAM_B1) * g
    v2 = ADAM_B2 * v + (1.0 - ADAM_B2) * (g * g)
    m_hat = m2 / (1.0 - ADAM_B1 ** ADAM_STEP)
    v_hat = v2 / (1.0 - ADAM_B2 ** ADAM_STEP)
    delta = -ADAM_LR * (m_hat / (jnp.sqrt(v_hat) + ADAM_EPS) + ADAM_WD * w)
    return delta, m2, v2


def _adamw_sharded(chip_sums, remote, chip, w, m, v, layer, prev, *, name):
    nl, r, c = w.shape
    tr = _rows(r, c)

    def body(*refs):
        p_ref, r0_ref, r1_ref, r2_ref, w_ref, m_ref, v_ref = refs[1:8]
        g_out, d_out, m_out, v_out = refs[-4:]
        g = ((p_ref[...].astype(F32) + r0_ref[...].astype(F32)) + r1_ref[...].astype(F32)) + r2_ref[...].astype(F32)
        g_out[...] = g
        d_out[...], m_out[...], v_out[...] = _adam_math(g, w_ref[...], m_ref[...], v_ref[...])

    pspec = pl.BlockSpec((None, tr, c), lambda i, chip_ref: (chip_ref[0], i, 0))

    def rspec(k):
        return pl.BlockSpec((None, tr, c), lambda i, chip_ref: (k, i, 0))

    wspec = pl.BlockSpec((None, tr, c), lambda i, chip_ref: (layer, i, 0))
    in_specs = [pspec, rspec(0), rspec(1), rspec(2), wspec, wspec, wspec]
    args = [chip, chip_sums, remote, remote, remote, w, m, v]
    aliases = {}
    if prev is not None:
        in_specs += [pl.BlockSpec(memory_space=pl.ANY)] * 4
        aliases = {len(args) + i: i for i in range(4)}
        args += list(prev)
    shp = jax.ShapeDtypeStruct(w.shape, F32)
    return _pcall(body, name=name, out_shape=(shp,) * 4, grid=(r // tr,), in_specs=in_specs, out_specs=(wspec,) * 4,
                  aliases=aliases, prefetch=1)(*args)


def _adamw_local(g, w, m, v, *, name):
    nl, r, c = w.shape
    tr = _rows(r, c)

    def body(g_ref, w_ref, m_ref, v_ref, d_out, m_out, v_out):
        d_out[...], m_out[...], v_out[...] = _adam_math(g_ref[...], w_ref[...], m_ref[...], v_ref[...])

    spec = pl.BlockSpec((None, tr, c), lambda l, i: (l, i, 0))
    shp = jax.ShapeDtypeStruct(w.shape, F32)
    return _pcall(body, name=name, out_shape=(shp,) * 3, grid=(nl, r // tr), in_specs=[spec] * 4,
                  out_specs=(spec,) * 3)(g, w, m, v)


def _adamw_replicated(parts, w, m, v, *, name):
    n = w.shape[1]

    def body(p_ref, w_ref, m_ref, v_ref, g_out, d_out, m_out, v_out):
        g = p_ref[0]
        for k in range(1, N_DEV):
            g = g + p_ref[k]
        g_out[...] = g
        d_out[...], m_out[...], v_out[...] = _adam_math(g, w_ref[...], m_ref[...], v_ref[...])

    vm = pl.BlockSpec(memory_space=pltpu.VMEM)
    shp = jax.ShapeDtypeStruct((1, n), F32)
    return _pcall(body, name=name, out_shape=(shp,) * 4, in_specs=[vm] * 4, out_specs=(vm,) * 4)(parts, w, m, v)


UNDILATED_OFFS = (0, A_HEADS, OFF_VA // HEAD_DIM)


def _mod_rows(mod, d):
    return [mod[:, i * d:(i + 1) * d] for i in range(6)]


MIXER_W = ("w_in", "w_branch_a", "w_branch_b", "w_out")
FFN_W = ("w_gate_up", "w_down")
SHARD_AXIS = {"w_in": 1, "w_branch_a": 1, "w_branch_b": 1, "w_out": 0, "w_gate_up": 1, "w_down": 0}


def _norm_args(mod, gain, which, d):
    rows = _mod_rows(mod, d)
    return gain, rows[3 * which + 1], rows[3 * which]


def _mixer_fwd_a(h, u, gains, w_in, cos2, sin2, hook):
    seq = h.shape[0]
    proj = _mm(u, w_in, name="mm_in")
    hook(proj)
    qk, qk32 = _qkrope_fwd(proj, gains, cos2, sin2, name="qkrope_fwd")
    os_, lses = [], []
    for g, dil in enumerate(DILATIONS):
        if dil == 1:
            o, lse = _dil_fwd(qk, qk, proj, UNDILATED_OFFS, seq, 1, name="dil_fwd_1")
        else:
            o, lse = _dil_fwd_strided(qk32, proj, g, dil, name=f"dil_fwd_{dil}")
        os_.append(o)
        lses.append(lse)
    o_a = _combine_fwd(os_, lses, name="combine_fwd")
    o_b = _sb_fwd(proj, name="sb_fwd")
    return dict(h_in=h, u=u, proj=proj, qk=qk, qk32=qk32, os=os_, lses=lses, o_a=o_a, o_b=o_b)


def _mixer_fwd_b(sv, mod, g2, wts):
    d = sv["h_in"].shape[1]
    merged, y_a, y_b = _mm_merge(sv["o_a"], sv["o_b"], wts["w_branch_a"], wts["w_branch_b"], sv["proj"],
                                 name="mm_branch")
    h_mid, t, u2 = _mm_resid_norm(merged, wts["w_out"], sv["h_in"], _mod_rows(mod, d)[2], _norm_args(mod, g2, 1, d),
                                  name="mm_out")
    sv.update(y_a=y_a, y_b=y_b, merged=merged, t=t, h_mid=h_mid, u2=u2)
    return h_mid


def _ffn_fwd_a(sv, w_gate_up):
    a, g, u = _mm_swiglu(sv["u2"], w_gate_up, name="mm_gate_up")
    sv.update(g=g, up=u, a=a)
    return a


def _ffn_fwd_b(sv, mod, w_down, next_norm):
    d = sv["h_mid"].shape[1]
    h_out, sv["f"], u_next = _mm_resid_norm(sv["a"], w_down, sv["h_mid"], _mod_rows(mod, d)[5], next_norm,
                                            name="mm_down")
    return h_out, u_next


def _wgrad(act, dout, key):
    return _mm(act, dout, ta=True, out_dtype=BF16, caps=(2048, 1024, 3072), name="mm_wgrad_" + key)


def _ffn_bwd(dh, df, dgate2, sv, mod, g2, wts, hook):
    d = dh.shape[1]
    sc2, ga1 = _mod_rows(mod, d)[4], _mod_rows(mod, d)[2]
    dg, dup = _mm_down_t_swiglu(df, wts["w_down"], sv["g"], sv["up"], name="mm_down_t")
    grads = {"w_down": _wgrad(sv["a"], df, "w_down")}
    hook(dup)
    du2 = _mm_cat_k(dg, dup, wts["w_gate_up"], name="mm_gate_up_t")
    grads["w_gate_up"] = _mm_cat_n(sv["u2"], dg, dup, name="mm_wgrad_w_gate_up")
    dh_mid, dsh2, dsc2, dg2, dt, dgate1 = _rmsmod_bwd(du2, sv["h_mid"], g2, sc2, dh, sv["t"], ga1, name="rmsmod_bwd")
    return dh_mid, [dsh2, dsc2, dgate2], dg2, grads, dt, dgate1


def _mixer_bwd(dh_mid, dt, dgate1, sv, mod, g1, gains, wts, cos2, sin2, hook, below):
    seq, d = dh_mid.shape
    sc1 = _mod_rows(mod, d)[1]
    dy_a, dy_b, dga, dgb = _mm_out_t_merge(dt, wts["w_out"], sv["proj"], sv["y_a"], sv["y_b"], name="mm_out_t")
    grads = {"w_out": _wgrad(sv["merged"], dt, "w_out")}
    do_a = _mm(dy_a, wts["w_branch_a"], tb=True, name="mm_branch_t")
    do_b = _mm(dy_b, wts["w_branch_b"], tb=True, name="mm_branch_t")
    grads["w_branch_a"] = _wgrad(sv["o_a"], dy_a, "w_branch_a")
    grads["w_branch_b"] = _wgrad(sv["o_b"], dy_b, "w_branch_b")
    dqb, dkb, dvb = _sb_bwd(sv["proj"], do_b, name="sb_bwd")
    hook(dqb, grads)
    comb = _combine_bwd(do_a, sv["os"], sv["lses"], name="combine_bwd")
    grads = {}
    dos, dls = comb[:3], comb[3:]
    dqs, dks, dvs = [], [], []
    for g, dil in enumerate(DILATIONS):
        if dil == 1:
            dq, dk, dv = _dil_bwd(sv["qk"], sv["qk"], sv["proj"], UNDILATED_OFFS, sv["os"][g], sv["lses"][g], dos[g],
                                  dls[g], seq, 1, name="dil_bwd_1")
        else:
            dq, dk, dv = _dil_bwd_strided(sv["qk32"], sv["proj"], g, sv["os"][g], sv["lses"][g], dos[g], dls[g], dil,
                                          name=f"dil_bwd_{dil}")
        dqs.append(dq)
        dks.append(dk)
        dvs.append(dv)
    dq_pre, dqn = _qkrope_bwd(dqs, sv["proj"], gains, 0, cos2, sin2, name="qkrope_bwd")
    dk_pre, dkn = _qkrope_bwd(dks, sv["proj"], gains, 1, cos2, sin2, name="qkrope_bwd")
    dgains = jnp.stack([dqn, dkn])
    dproj = _assemble([dq_pre, dk_pre] + dvs + [dqb, dkb, dvb, dga, dgb], name="assemble_dproj")
    du = _mm(dproj, wts["w_in"], tb=True, name="mm_in_t")
    grads["w_in"] = _wgrad(sv["u"], dproj, "w_in")
    dh_in, dsh1, dsc1, dg1, df, dgate2 = _rmsmod_bwd(du, sv["h_in"], g1, sc1, dh_mid, *(below or (None, None)),
                                                     name="rmsmod_bwd")
    return dh_in, [dsh1, dsc1, dgate1], dg1, dgains, grads, df, dgate2


def kernel(x, c, w_ada, b_ada, norm1_g, norm2_g, w_in, qn_g, kn_g, w_branch_a, w_branch_b, w_out, w_gate_up, w_down, loss_target, m_w_ada, m_b_ada, m_norm1_g, m_norm2_g, m_w_in, m_qn_g, m_kn_g, m_w_branch_a, m_w_branch_b, m_w_out, m_w_gate_up, m_w_down, v_w_ada, v_b_ada, v_norm1_g, v_norm2_g, v_w_in, v_qn_g, v_kn_g, v_w_branch_a, v_w_branch_b, v_w_out, v_w_gate_up, v_w_down):
    _ORDER["token"] = None
    seq, d = x.shape[1], x.shape[2]
    depth = w_in.shape[0]
    weights = dict(w_in=w_in, w_branch_a=w_branch_a, w_branch_b=w_branch_b, w_out=w_out, w_gate_up=w_gate_up,
                   w_down=w_down)
    moments_m = dict(w_in=m_w_in, w_branch_a=m_w_branch_a, w_branch_b=m_w_branch_b, w_out=m_w_out,
                     w_gate_up=m_w_gate_up, w_down=m_w_down)
    moments_v = dict(w_in=v_w_in, w_branch_a=v_w_branch_a, w_branch_b=v_w_branch_b, w_out=v_w_out,
                     w_gate_up=v_w_gate_up, w_down=v_w_down)
    xi, yi, ci = _coords()
    me = 4 * xi + 2 * yi + ci
    core = jnp.reshape(ci, (1,)).astype(jnp.int32)
    chip = jnp.reshape(2 * xi + yi, (1,)).astype(jnp.int32)

    ada_w = w_ada.shape[2]
    c_act = _small_allgather(c, name="comm_gather_c", silu=True).reshape(N_DEV, d)
    c_pad = jnp.concatenate([c_act, jnp.zeros_like(c_act)], axis=0).astype(BF16)
    bias = lax.dynamic_slice(b_ada, (0, me * ada_w), (depth, ada_w))
    mod_part = jnp.stack([_mm(c_pad, w_ada[l], name="mm_ada")[:N_DEV] for l in range(depth)]) + bias[:, None, :]
    mod_all = _small_allgather(mod_part.reshape(1, depth * N_DEV * ada_w), name="comm_gather_mod")
    mod_all = mod_all.reshape(N_DEV, depth, N_DEV, ada_w)
    mod_mine = lax.dynamic_index_in_dim(mod_all, me, axis=2, keepdims=False)
    mods = jnp.transpose(mod_mine, (1, 0, 2)).reshape(depth, 1, 6 * d)

    cos2, sin2 = _rope_tables(seq)
    gains = [jnp.stack([qn_g[l], kn_g[l]])[:, None, :] for l in range(depth)]
    g1s = [norm1_g[l][None] for l in range(depth)]
    g2s = [norm2_g[l][None] for l in range(depth)]

    me_arr = jnp.reshape(me, (1,)).astype(jnp.int32)

    def placed(keys, l):
        return [_cast_place(weights[k], l, SHARD_AXIS[k], me_arr, name="cast_place_" + k) for k in keys]

    def gather_of(keys, l, tag):
        return _SplitGather(placed(keys, l), [SHARD_AXIS[k] for k in keys], f"{tag}{l}")

    groups = []
    for l in range(depth):
        groups += [("w_in", l, MIXER_W[:1]), ("rest", l, MIXER_W[1:]), ("ffn", l, FFN_W)]
    gathers = {}

    def issue(some):
        for tag, l, keys in some:
            gathers[tag, l] = gather_of(keys, l, tag)
            gathers[tag, l].first(after=mods)

    issue(groups[:3])
    h = x[0]
    u = _rmsmod_fwd(h, *_norm_args(mods[0], g1s[0], 0, d), name="rmsmod_fwd")
    gathers["w_in", 0].forward(after=u)
    issue(groups[3:])
    wm = {"w_in": gathers["w_in", 0].finish(after=u)[0]}
    saved, full = [], []
    for l in range(depth):
        last = l + 1 == depth
        sv = _mixer_fwd_a(h, u, gains[l], wm["w_in"], cos2, sin2, gathers["rest", l].forward)
        gathers["ffn", l].forward(after=sv["o_b"])
        wm.update(zip(MIXER_W[1:], gathers["rest", l].finish(after=sv["o_b"])))
        h_mid = _mixer_fwd_b(sv, mods[l], g2s[l], wm)
        wf = dict(zip(FFN_W, gathers["ffn", l].finish(after=h_mid)))
        a = _ffn_fwd_a(sv, wf["w_gate_up"])
        if not last:
            gathers["w_in", l + 1].forward(after=a)
        h, u = _ffn_fwd_b(sv, mods[l], wf["w_down"],
                          None if last else _norm_args(mods[l + 1], g1s[l + 1], 0, d))
        saved.append(sv)
        full.append({**wm, **wf})
        if not last:
            wm = {"w_in": gathers["w_in", l + 1].finish(after=h)[0]}
    def ffn_gate(l):
        return saved[l]["f"], _mod_rows(mods[l], d)[5]

    loss_part, dh, df, dgate2 = _loss_fwd(h, loss_target[0], *ffn_gate(depth - 1), name="loss")
    loss = lax.psum(loss_part[0, 0], ("x", "y", "c"))

    pipe = _ReducePipeline(core)
    dmods, dg1s, dg2s, dgains = [None] * depth, [None] * depth, [None] * depth, [None] * depth
    for l in reversed(range(depth)):
        dh_mid, dmod_f, dg2s[l], grads, dt, dgate1 = _ffn_bwd(dh, df, dgate2, saved[l], mods[l], g2s[l], full[l],
                                                              pipe.tick)
        pipe.tick(dh_mid)
        pipe.add(FFN_W, grads, l)
        dh, dmod_m, dg1s[l], dgains[l], grads, df, dgate2 = _mixer_bwd(
            dh_mid, dt, dgate1, saved[l], mods[l], g1s[l], gains[l], full[l], cos2, sin2,
            lambda after, early, l=l: (pipe.tick(after), pipe.add(MIXER_W[1:], early, l)),
            ffn_gate(l - 1) if l > 0 else None)
        dmods[l] = jnp.concatenate(dmod_m + dmod_f, axis=1)
        pipe.tick(dh)
        pipe.add(MIXER_W[:1], grads, l)
    grad_x = dh[None]

    stacked = {}

    def update(items):
        for keys, l, sums, remote in items:
            for k, p_, r_ in zip(keys, sums, remote):
                stacked[k] = _adamw_sharded(p_, r_, chip, weights[k], moments_m[k], moments_v[k], l,
                                            stacked.get(k), name="adamw_" + k)

    ready = pipe.take_done()
    update([it for it in ready if it[0] != FFN_W])

    small = jnp.concatenate(
        dmods + dg1s + dg2s + [dgains[l][0] for l in range(depth)] + [dgains[l][1] for l in range(depth)], axis=1)
    small_all = _small_allgather(small, name="comm_gather_small")
    pipe.tick(small_all)
    update([it for it in ready if it[0] == FFN_W] + pipe.take_done())

    def pack(b, n1, n2, qn, kn):
        return jnp.concatenate([t_.reshape(1, -1) for t_ in (b, n1, n2, qn, kn)], axis=1)

    sg, sd, sm, sv_ = _adamw_replicated(small_all, pack(b_ada, norm1_g, norm2_g, qn_g, kn_g),
                                        pack(m_b_ada, m_norm1_g, m_norm2_g, m_qn_g, m_kn_g),
                                        pack(v_b_ada, v_norm1_g, v_norm2_g, v_qn_g, v_kn_g), name="adamw_replicated")

    def unpack(p):
        sizes = [depth * 6 * d, depth * d, depth * d, depth * HEAD_DIM, depth * HEAD_DIM]
        shapes = [b_ada.shape, norm1_g.shape, norm2_g.shape, qn_g.shape, kn_g.shape]
        out, off = [], 0
        for n, shp in zip(sizes, shapes):
            out.append(p[0, off:off + n].reshape(shp))
            off += n
        return dict(zip(("b_ada", "norm1_g", "norm2_g", "qn_g", "kn_g"), out))

    ug, ud, um, uv = unpack(sg), unpack(sd), unpack(sm), unpack(sv_)
    res = {k: dict(g=ug[k], d=ud[k], m=um[k], v=uv[k]) for k in ug}

    dmod_all = small_all[:, 0, :depth * 6 * d].reshape(N_DEV, depth, 6 * d)
    g_ada = None
    for l in range(depth):
        dm = lax.dynamic_slice(dmod_all[:, l, :], (0, me * ada_w), (N_DEV, ada_w))
        dm = jnp.concatenate([dm, jnp.zeros_like(dm)], axis=0).astype(BF16)
        g_ada = _mm(c_pad, dm, ta=True, name="mm_wgrad_ada", stack=(l, depth, g_ada))
    d_ada, m_ada, v_ada = _adamw_local(g_ada, w_ada, m_w_ada, v_w_ada, name="adamw_local")
    res["w_ada"] = dict(g=g_ada, d=d_ada, m=m_ada, v=v_ada)

    pipe.tick(d_ada)
    update(pipe.take_done())
    pipe.tick(d_ada, flush=True)
    update(pipe.take_done())
    for k, (g_, d_, m_, v_) in stacked.items():
        res[k] = dict(g=g_, d=d_, m=m_, v=v_)

    order = ("w_ada", "b_ada", "norm1_g", "norm2_g", "w_in", "qn_g", "kn_g", "w_branch_a", "w_branch_b", "w_out",
             "w_gate_up", "w_down")
    _ORDER["token"] = None
    return (loss, grad_x, *[res[k]["g"] for k in order], *[res[k]["d"] for k in order],
            *[res[k]["m"] for k in order], *[res[k]["v"] for k in order])
```

```python
import jax
import jax.numpy as jnp
from jax import lax
from jax.experimental import pallas as pl
from jax.experimental.pallas import tpu as pltpu

F32 = jnp.float32
BF16 = jnp.bfloat16

HEAD_DIM = 128
BLOCK = 128
DILATIONS = (1, 4, 16)
HEADS_PER_GROUP = 4
A_HEADS = 12
SB_HEADS = 4
GROUP_W = HEADS_PER_GROUP * HEAD_DIM
A_W = A_HEADS * HEAD_DIM
B_W = SB_HEADS * HEAD_DIM
OFF_QA, OFF_KA, OFF_VA = 0, A_W, 2 * A_W
OFF_QB, OFF_KB, OFF_VB = 3 * A_W, 3 * A_W + B_W, 3 * A_W + 2 * B_W
OFF_GATES = 3 * A_W + 3 * B_W
ROPE_THETA = 10000.0
EPS = 1e-6
ATT_SCALE = HEAD_DIM ** -0.5
MASKED = -1e30

ADAM_LR, ADAM_B1, ADAM_B2, ADAM_EPS, ADAM_WD, ADAM_STEP = 0.001, 0.9, 0.999, 1e-08, 0.01, 10

N_DEV = 8
N_CHIPS = 4
V7X_VMEM_LIMIT_BYTES = 56 * 1024 * 1024
ELEMWISE_BLOCK_BYTES = 2 * 1024 * 1024
MESH = pl.DeviceIdType.MESH

NN = (((1,), (0,)), ((), ()))
NT = (((1,), (1,)), ((), ()))
TN = (((0,), (0,)), ((), ()))


def _dot(a, b, dims=NN):
    return lax.dot_general(a, b, dims, preferred_element_type=F32)


def _tile(n, cap, mult=128):
    best = None
    for t in range(mult, min(n, cap) + 1, mult):
        if n % t == 0:
            best = t
    if best is None:
        assert n <= 2 * cap, (n, cap)
        return n
    return best


def _rows(r, c):
    return _tile(r, max(16, ELEMWISE_BLOCK_BYTES // (4 * c)), 16)


_ORDER = {"token": None}
TOKEN = jax.ShapeDtypeStruct((8, 128), F32)


def _take_token():
    prev = _ORDER["token"]
    return [] if prev is None else [prev]


def _pcall(body, *, name, out_shape, grid=None, in_specs=None, out_specs=None, scratch=(), aliases=None,
           prefetch=0):
    single = not isinstance(out_shape, (tuple, list))
    out_shapes = [out_shape] if single else list(out_shape)
    out_specs = [out_specs] if single else list(out_specs)
    extra = _take_token()
    n_in, n_extra, n_out = prefetch + len(in_specs), len(extra), len(out_shapes)

    def wrapped(*refs):
        token = refs[n_in + n_extra + n_out]
        token[...] = jnp.zeros_like(token)
        return body(*refs[:n_in], *refs[n_in + n_extra:n_in + n_extra + n_out], *refs[n_in + n_extra + n_out + 1:])

    in_specs = list(in_specs) + [pl.BlockSpec(memory_space=pl.ANY)] * n_extra
    if grid is None:
        out_specs.append(pl.BlockSpec(memory_space=pltpu.VMEM))
    else:
        out_specs.append(pl.BlockSpec(TOKEN.shape, lambda *_: (0, 0)))
    kwargs = dict(name=name, out_shape=out_shapes + [TOKEN], input_output_aliases=aliases or {},
                  compiler_params=pltpu.CompilerParams(vmem_limit_bytes=V7X_VMEM_LIMIT_BYTES))
    if prefetch:
        call = pl.pallas_call(wrapped, grid_spec=pltpu.PrefetchScalarGridSpec(
            num_scalar_prefetch=prefetch, grid=grid, in_specs=in_specs, out_specs=out_specs,
            scratch_shapes=list(scratch)), **kwargs)
    else:
        if grid is not None:
            kwargs["grid"] = grid
        call = pl.pallas_call(wrapped, in_specs=in_specs, out_specs=out_specs, scratch_shapes=list(scratch), **kwargs)

    def run(*args):
        outs = call(*args, *extra)
        _ORDER["token"] = outs[-1]
        return outs[0] if single else tuple(outs[:-1])

    return run


def _mm(a, b, *, name, ta=False, tb=False, out_dtype=F32, caps=(1024, 1024, 3072), stack=None):
    kdim, m = a.shape if ta else a.shape[::-1]
    n, k2 = b.shape if tb else b.shape[::-1]
    assert kdim == k2, (a.shape, b.shape, ta, tb)
    tm, tn, tk = _tile(m, caps[0]), _tile(n, caps[1]), _tile(kdim, caps[2])
    nk = kdim // tk
    dims = (((0 if ta else 1,), (1 if tb else 0,)), ((), ()))

    def body(*refs):
        a_ref, b_ref = refs[0], refs[1]
        part = _dot(a_ref[...].astype(BF16), b_ref[...].astype(BF16), dims)
        if nk == 1:
            o_ref = refs[-1]
            o_ref[...] = part.astype(o_ref.dtype)
            return
        o_ref, acc_ref = refs[-2], refs[-1]
        k = pl.program_id(2)

        @pl.when(k == 0)
        def _():
            acc_ref[...] = part

        @pl.when(k > 0)
        def _():
            acc_ref[...] += part

        @pl.when(k == nk - 1)
        def _():
            o_ref[...] = acc_ref[...].astype(o_ref.dtype)

    a_spec = (pl.BlockSpec((tk, tm), lambda i, j, k: (k, i)) if ta
              else pl.BlockSpec((tm, tk), lambda i, j, k: (i, k)))
    b_spec = (pl.BlockSpec((tn, tk), lambda i, j, k: (j, k)) if tb
              else pl.BlockSpec((tk, tn), lambda i, j, k: (k, j)))
    ins, in_specs, aliases = [a, b], [a_spec, b_spec], {}
    if stack is None:
        out_shape = jax.ShapeDtypeStruct((m, n), out_dtype)
        out_spec = pl.BlockSpec((tm, tn), lambda i, j, k: (i, j))
    else:
        layer, n_layers, buf = stack
        out_shape = jax.ShapeDtypeStruct((n_layers, m, n), out_dtype)
        out_spec = pl.BlockSpec((None, tm, tn), lambda i, j, k: (layer, i, j))
        if buf is not None:
            ins.append(buf)
            in_specs.append(pl.BlockSpec(memory_space=pl.ANY))
            aliases = {2: 0}
    scratch = [] if nk == 1 else [pltpu.VMEM((tm, tn), F32)]
    return _pcall(body, name=name, out_shape=out_shape, grid=(m // tm, n // tn, nk), in_specs=in_specs,
                  out_specs=out_spec, scratch=scratch, aliases=aliases)(*ins)


EPILOGUE_ROWS = 256


def _row_chunks(tm):
    return [slice(r, r + EPILOGUE_ROWS) for r in range(0, tm, EPILOGUE_ROWS)] if tm > EPILOGUE_ROWS else [slice(0, tm)]


def _mm_cat_k(a_lo, a_hi, b, *, name):
    m, f = a_lo.shape
    n = b.shape[0]
    tm, tn, tk = _tile(m, 1024), _tile(n, 1024), _tile(f, 3072)
    half = f // tk
    nk = 2 * half

    def body(lo_ref, hi_ref, b_ref, o_ref, acc_ref):
        k = pl.program_id(2)

        def accumulate(a_ref):
            part = _dot(a_ref[...], b_ref[...], NT)

            @pl.when(k == 0)
            def _():
                acc_ref[...] = part

            @pl.when(k > 0)
            def _():
                acc_ref[...] += part

        pl.when(k < half)(lambda: accumulate(lo_ref))
        pl.when(k >= half)(lambda: accumulate(hi_ref))

        @pl.when(k == nk - 1)
        def _():
            o_ref[...] = acc_ref[...]

    return _pcall(body, name=name, out_shape=jax.ShapeDtypeStruct((m, n), F32), grid=(m // tm, n // tn, nk),
                  in_specs=[pl.BlockSpec((tm, tk), lambda i, j, k: (i, jnp.minimum(k, half - 1))),
                            pl.BlockSpec((tm, tk), lambda i, j, k: (i, jnp.maximum(k - half, 0))),
                            pl.BlockSpec((tn, tk), lambda i, j, k: (j, k))],
                  out_specs=pl.BlockSpec((tm, tn), lambda i, j, k: (i, j)),
                  scratch=[pltpu.VMEM((tm, tn), F32)])(a_lo, a_hi, b)


def _mm_cat_n(a, b_lo, b_hi, *, name):
    s, m = a.shape
    f = b_lo.shape[1]
    tm, tn = _tile(m, 2048), _tile(f, 1024)
    half = f // tn

    def body(a_ref, lo_ref, hi_ref, o_ref):
        j = pl.program_id(1)

        @pl.when(j < half)
        def _():
            o_ref[...] = _dot(a_ref[...], lo_ref[...], TN).astype(BF16)

        @pl.when(j >= half)
        def _():
            o_ref[...] = _dot(a_ref[...], hi_ref[...], TN).astype(BF16)

    return _pcall(body, name=name, out_shape=jax.ShapeDtypeStruct((m, 2 * f), BF16), grid=(m // tm, 2 * half),
                  in_specs=[pl.BlockSpec((s, tm), lambda i, j: (0, i)),
                            pl.BlockSpec((s, tn), lambda i, j: (0, jnp.minimum(j, half - 1))),
                            pl.BlockSpec((s, tn), lambda i, j: (0, jnp.maximum(j - half, 0)))],
                  out_specs=pl.BlockSpec((tm, tn), lambda i, j: (i, j)))(a, b_lo, b_hi)


def _mm_resid_norm(a, w, h, gate, norm, *, name):
    s, kdim = a.shape
    d = w.shape[1]
    tk = _tile(kdim, 2048)
    nk = kdim // tk
    tm = _tile(s, 256 if nk == 1 else 512)

    def body(*refs):
        a_ref, w_ref, h_ref, gate_ref = refs[:4]
        outs = refs[7:] if norm is not None else refs[4:]

        def finish(rows, t):
            hn = h_ref[rows, :] + gate_ref[...] * t
            outs[0][rows, :] = hn
            outs[1][rows, :] = t.astype(BF16)
            if norm is not None:
                g_ref, sc_ref, sh_ref = refs[4:7]
                r = lax.rsqrt(jnp.mean(hn * hn, axis=-1, keepdims=True) + EPS)
                outs[2][rows, :] = (((hn * r) * g_ref[...]) * (1.0 + sc_ref[...]) + sh_ref[...]).astype(BF16)

        if nk == 1:
            for rows in _row_chunks(tm):
                finish(rows, _dot(a_ref[rows, :], w_ref[...]))
            return
        acc_ref = refs[-1]
        k = pl.program_id(1)

        @pl.when(k == 0)
        def _():
            acc_ref[...] = _dot(a_ref[...], w_ref[...])

        @pl.when(jnp.logical_and(k > 0, k < nk - 1))
        def _():
            acc_ref[...] += _dot(a_ref[...], w_ref[...])

        @pl.when(k == nk - 1)
        def _():
            for rows in _row_chunks(tm):
                finish(rows, acc_ref[rows, :] + _dot(a_ref[rows, :], w_ref[...]))

    row = pl.BlockSpec((tm, d), lambda i, k: (i, 0))
    vec = pl.BlockSpec((1, d), lambda i, k: (0, 0))
    in_specs = [pl.BlockSpec((tm, tk), lambda i, k: (i, k)), pl.BlockSpec((tk, d), lambda i, k: (k, 0)), row, vec]
    args = [a, w, h, gate]
    out_shape = [jax.ShapeDtypeStruct((s, d), F32), jax.ShapeDtypeStruct((s, d), BF16)]
    if norm is not None:
        in_specs += [vec, vec, vec]
        args += list(norm)
        out_shape.append(jax.ShapeDtypeStruct((s, d), BF16))
    outs = _pcall(body, name=name, out_shape=tuple(out_shape), grid=(s // tm, nk), in_specs=in_specs,
                  out_specs=(row,) * len(out_shape), scratch=[] if nk == 1 else [pltpu.VMEM((tm, d), F32)])(*args)
    return outs if norm is not None else (*outs, None)


def _mm_merge(o_a, o_b, w_a, w_b, proj, *, name):
    s = o_a.shape[0]
    d = w_a.shape[1]
    tm = _tile(s, 512)
    ga_blk = OFF_GATES // d

    def body(oa_ref, ob_ref, wa_ref, wb_ref, ga_ref, gb_ref, m_ref, ya_ref, yb_ref):
        for rows in _row_chunks(tm):
            ya, yb = _dot(oa_ref[rows, :], wa_ref[...]), _dot(ob_ref[rows, :], wb_ref[...])
            m_ref[rows, :] = (jax.nn.sigmoid(ga_ref[rows, :]) * ya
                              + jax.nn.sigmoid(gb_ref[rows, :]) * yb).astype(BF16)
            ya_ref[rows, :] = ya.astype(BF16)
            yb_ref[rows, :] = yb.astype(BF16)

    row = pl.BlockSpec((tm, d), lambda i: (i, 0))
    act = pl.BlockSpec((tm, o_a.shape[1]), lambda i: (i, 0))
    wspec = pl.BlockSpec(w_a.shape, lambda i: (0, 0))
    shp = jax.ShapeDtypeStruct((s, d), BF16)
    return _pcall(body, name=name, out_shape=(shp, shp, shp), grid=(s // tm,),
                  in_specs=[act, act, wspec, wspec, pl.BlockSpec((tm, d), lambda i: (i, ga_blk)),
                            pl.BlockSpec((tm, d), lambda i: (i, ga_blk + 1))],
                  out_specs=(row, row, row))(o_a, o_b, w_a, w_b, proj, proj)


def _mm_out_t_merge(dt, w_out, proj, y_a, y_b, *, name):
    s, d = dt.shape
    tm, tn = _tile(s, 1024), _tile(d, 512)
    ga_blk = OFF_GATES // tn

    def body(dt_ref, w_ref, ga_ref, gb_ref, ya_ref, yb_ref, dya_ref, dyb_ref, dga_ref, dgb_ref):
        w = w_ref[...]
        for rows in _row_chunks(tm):
            dm = _dot(dt_ref[rows, :], w, NT)
            sa, sb = jax.nn.sigmoid(ga_ref[rows, :]), jax.nn.sigmoid(gb_ref[rows, :])
            dya_ref[rows, :] = (dm * sa).astype(BF16)
            dyb_ref[rows, :] = (dm * sb).astype(BF16)
            dga_ref[rows, :] = (dm * ya_ref[rows, :] * (sa * (1.0 - sa))).astype(BF16)
            dgb_ref[rows, :] = (dm * yb_ref[rows, :] * (sb * (1.0 - sb))).astype(BF16)

    tile = pl.BlockSpec((tm, tn), lambda i, j: (i, j))
    shp = jax.ShapeDtypeStruct((s, d), BF16)
    return _pcall(body, name=name, out_shape=(shp,) * 4, grid=(s // tm, d // tn),
                  in_specs=[pl.BlockSpec((tm, d), lambda i, j: (i, 0)), pl.BlockSpec((tn, d), lambda i, j: (j, 0)),
                            pl.BlockSpec((tm, tn), lambda i, j: (i, ga_blk + j)),
                            pl.BlockSpec((tm, tn), lambda i, j: (i, ga_blk + d // tn + j)), tile, tile],
                  out_specs=(tile,) * 4)(dt, w_out, proj, proj, y_a, y_b)


def _mm_down_t_swiglu(df, w_down, g, u, *, name):
    s, d = df.shape
    f = w_down.shape[0]
    tm, tn = _tile(s, 1024), _tile(f, 512)

    def body(df_ref, w_ref, g_ref, u_ref, dg_ref, du_ref):
        w = w_ref[...]
        for rows in _row_chunks(tm):
            da = _dot(df_ref[rows, :], w, NT)
            gf = g_ref[rows, :].astype(F32)
            sg = jax.nn.sigmoid(gf)
            dg_ref[rows, :] = (da * u_ref[rows, :].astype(F32) * (sg * (1.0 + gf * (1.0 - sg)))).astype(BF16)
            du_ref[rows, :] = (da * (gf * sg)).astype(BF16)

    tile = pl.BlockSpec((tm, tn), lambda i, j: (i, j))
    shp = jax.ShapeDtypeStruct((s, f), BF16)
    return _pcall(body, name=name, out_shape=(shp, shp), grid=(s // tm, f // tn),
                  in_specs=[pl.BlockSpec((tm, d), lambda i, j: (i, 0)), pl.BlockSpec((tn, d), lambda i, j: (j, 0)),
                            tile, tile],
                  out_specs=(tile, tile))(df, w_down, g, u)


def _rmsmod_fwd(h, g, scale, shift, *, name):
    s, d = h.shape
    ts = _rows(s, d)

    def body(h_ref, g_ref, sc_ref, sh_ref, u_ref):
        hf = h_ref[...]
        r = lax.rsqrt(jnp.mean(hf * hf, axis=-1, keepdims=True) + EPS)
        u_ref[...] = (((hf * r) * g_ref[...]) * (1.0 + sc_ref[...]) + sh_ref[...]).astype(BF16)

    row = pl.BlockSpec((ts, d), lambda i: (i, 0))
    vec = pl.BlockSpec((1, d), lambda i: (0, 0))
    return _pcall(body, name=name, out_shape=jax.ShapeDtypeStruct((s, d), BF16), grid=(s // ts,),
                  in_specs=[row, vec, vec, vec], out_specs=row)(h, g, scale, shift)


def _gate_bwd(dhf, t_ref, gate_ref, dt_ref, dgate_ref):
    dt_ref[...] = (dhf * gate_ref[...]).astype(BF16)
    dgate_ref[...] += jnp.sum(dhf * t_ref[...], axis=0, keepdims=True)


def _rmsmod_bwd(du, h, g, scale, dres, t, gate, *, name):
    s, d = h.shape
    ts = _rows(s, d)
    chain = t is not None

    def body(*refs):
        du_ref, h_ref, g_ref, sc_ref, dres_ref = refs[:5]
        dh_ref, dsh_ref, dsc_ref, dg_ref = refs[-6:-2] if chain else refs[-4:]
        sums = (dsh_ref, dsc_ref, dg_ref) + ((refs[-1],) if chain else ())

        @pl.when(pl.program_id(0) == 0)
        def _():
            for ref in sums:
                ref[...] = jnp.zeros_like(ref)

        hf, duf, gain = h_ref[...], du_ref[...], g_ref[...]
        r = lax.rsqrt(jnp.mean(hf * hf, axis=-1, keepdims=True) + EPS)
        xh = hf * r
        dn = duf * (1.0 + sc_ref[...])
        dsh_ref[...] += jnp.sum(duf, axis=0, keepdims=True)
        dsc_ref[...] += jnp.sum(duf * (xh * gain), axis=0, keepdims=True)
        dg_ref[...] += jnp.sum(dn * xh, axis=0, keepdims=True)
        dxh = dn * gain
        dh = dres_ref[...] + r * (dxh - xh * jnp.mean(dxh * xh, axis=-1, keepdims=True))
        dh_ref[...] = dh
        if chain:
            _gate_bwd(dh, refs[5], refs[6], refs[-2], refs[-1])

    row = pl.BlockSpec((ts, d), lambda i: (i, 0))
    vec = pl.BlockSpec((1, d), lambda i: (0, 0))
    vshape = jax.ShapeDtypeStruct((1, d), F32)
    out_shape, out_specs = [jax.ShapeDtypeStruct((s, d), F32), vshape, vshape, vshape], [row, vec, vec, vec]
    in_specs, args = [row, row, vec, vec, row], [du, h, g, scale, dres]
    if chain:
        in_specs, args = in_specs + [row, vec], args + [t, gate]
        out_shape, out_specs = out_shape + [jax.ShapeDtypeStruct((s, d), BF16), vshape], out_specs + [row, vec]
    outs = _pcall(body, name=name, out_shape=tuple(out_shape), grid=(s // ts,), in_specs=in_specs,
                  out_specs=tuple(out_specs))(*args)
    return outs if chain else (*outs, None, None)


def _mm_swiglu(u2, w_gate_up, *, name):
    s, d = u2.shape
    f = w_gate_up.shape[1] // 2
    tm, tn = _tile(s, 1024), _tile(f, 512)
    nj = f // tn

    def body(x_ref, wg_ref, wu_ref, a_ref, g_ref, u_ref):
        for rows in _row_chunks(tm):
            x = x_ref[rows, :]
            gf, uf = _dot(x, wg_ref[...]), _dot(x, wu_ref[...])
            a_ref[rows, :] = ((gf * jax.nn.sigmoid(gf)) * uf).astype(BF16)
            g_ref[rows, :] = gf.astype(BF16)
            u_ref[rows, :] = uf.astype(BF16)

    out = pl.BlockSpec((tm, tn), lambda i, j: (i, j))
    shp = jax.ShapeDtypeStruct((s, f), BF16)
    return _pcall(body, name=name, out_shape=(shp, shp, shp), grid=(s // tm, nj),
                  in_specs=[pl.BlockSpec((tm, d), lambda i, j: (i, 0)), pl.BlockSpec((d, tn), lambda i, j: (0, j)),
                            pl.BlockSpec((d, tn), lambda i, j: (0, nj + j))],
                  out_specs=(out, out, out))(u2, w_gate_up, w_gate_up)


def _loss_fwd(y, tgt, t, gate, *, name):
    s, d = y.shape
    ts = _rows(s, d)

    def body(y_ref, tgt_ref, t_ref, gate_ref, l_ref, dy_ref, dt_ref, dgate_ref):
        @pl.when(pl.program_id(0) == 0)
        def _():
            l_ref[...] = jnp.zeros_like(l_ref)
            dgate_ref[...] = jnp.zeros_like(dgate_ref)

        e = y_ref[...] - tgt_ref[...]
        dy = e * (1.0 / d)
        dy_ref[...] = dy
        per_tok = jnp.sum(e * e, axis=1, keepdims=True) * (1.0 / d)
        l_ref[...] += 0.5 * jnp.sum(per_tok, axis=0, keepdims=True)
        _gate_bwd(dy, t_ref, gate_ref, dt_ref, dgate_ref)

    row = pl.BlockSpec((ts, d), lambda i: (i, 0))
    vec = pl.BlockSpec((1, d), lambda i: (0, 0))
    return _pcall(body, name=name,
                  out_shape=(jax.ShapeDtypeStruct((1, 128), F32), jax.ShapeDtypeStruct((s, d), F32),
                             jax.ShapeDtypeStruct((s, d), BF16), jax.ShapeDtypeStruct((1, d), F32)),
                  grid=(s // ts,), in_specs=[row, row, row, vec],
                  out_specs=(pl.BlockSpec((1, 128), lambda i: (0, 0)), row, row, vec))(y, tgt, t, gate)


def _rope_tables(seq):
    inv = jnp.power(ROPE_THETA, -jnp.arange(0, HEAD_DIM, 2, dtype=F32) / HEAD_DIM)
    ang = jnp.arange(seq, dtype=F32)[:, None] * inv[None, :]
    cos, sin = jnp.cos(ang), jnp.sin(ang)
    return jnp.concatenate([cos, cos], axis=1), jnp.concatenate([-sin, sin], axis=1)


def _qkrope_fwd(proj, gains, cos2, sin2, *, name):
    s = proj.shape[0]
    ts = _rows(s, A_W)

    def body(x_ref, g_ref, c_ref, s_ref, o_ref, o32_ref):
        gain, cos, sin = g_ref[...], c_ref[...], s_ref[...]
        for h in range(A_HEADS):
            lanes = slice(h * HEAD_DIM, (h + 1) * HEAD_DIM)
            x = x_ref[:, lanes]
            y = (x * lax.rsqrt(jnp.mean(x * x, axis=-1, keepdims=True) + EPS)) * gain
            out = y * cos + pltpu.roll(y, HEAD_DIM // 2, 1) * sin
            o_ref[:, lanes] = out.astype(BF16)
            o32_ref[:, lanes] = out

    heads = pl.BlockSpec((ts, A_W), lambda i, j: (i, j))
    tab = pl.BlockSpec((ts, HEAD_DIM), lambda i, j: (i, 0))
    gain = pl.BlockSpec((None, 1, HEAD_DIM), lambda i, j: (j, 0, 0))
    return _pcall(body, name=name,
                  out_shape=(jax.ShapeDtypeStruct((s, 2 * A_W), BF16), jax.ShapeDtypeStruct((s, 2 * A_W), F32)),
                  grid=(s // ts, 2), in_specs=[heads, gain, tab, tab], out_specs=(heads, heads))(
                      proj, gains, cos2, sin2)


def _qkrope_bwd(d_groups, proj, gains, which, cos2, sin2, *, name):
    s = proj.shape[0]
    ts = _rows(s, A_W)

    def body(d0_ref, d1_ref, d2_ref, x_ref, g_ref, c_ref, s_ref, dx_ref, dg_ref):
        @pl.when(pl.program_id(0) == 0)
        def _():
            dg_ref[...] = jnp.zeros_like(dg_ref)

        gain, cos, sin = g_ref[...], c_ref[...], s_ref[...]
        dg = jnp.zeros((1, HEAD_DIM), F32)
        for h in range(A_HEADS):
            lanes = slice(h * HEAD_DIM, (h + 1) * HEAD_DIM)
            slot = slice((h % HEADS_PER_GROUP) * HEAD_DIM, (h % HEADS_PER_GROUP + 1) * HEAD_DIM)
            dout = (d0_ref, d1_ref, d2_ref)[h // HEADS_PER_GROUP][:, slot]
            dy = dout * cos + pltpu.roll(dout * sin, HEAD_DIM // 2, 1)
            x = x_ref[:, lanes]
            r = lax.rsqrt(jnp.mean(x * x, axis=-1, keepdims=True) + EPS)
            xh = x * r
            dg = dg + jnp.sum(dy * xh, axis=0, keepdims=True)
            dxh = dy * gain
            dx_ref[:, lanes] = (r * (dxh - xh * jnp.mean(dxh * xh, axis=-1, keepdims=True))).astype(BF16)
        dg_ref[...] += dg

    group = pl.BlockSpec((ts, GROUP_W), lambda i: (i, 0))
    tab = pl.BlockSpec((ts, HEAD_DIM), lambda i: (i, 0))
    gain = pl.BlockSpec((None, 1, HEAD_DIM), lambda i: (which, 0, 0))
    return _pcall(body, name=name,
                  out_shape=(jax.ShapeDtypeStruct((s, A_W), BF16), jax.ShapeDtypeStruct((1, HEAD_DIM), F32)),
                  grid=(s // ts,),
                  in_specs=[group, group, group, pl.BlockSpec((ts, A_W), lambda i: (i, which)), gain, tab, tab],
                  out_specs=(pl.BlockSpec((ts, A_W), lambda i: (i, 0)), pl.BlockSpec((1, HEAD_DIM), lambda i: (0, 0))))(
                      *d_groups, proj, gains, cos2, sin2)


def _assemble(pieces, *, name):
    s = pieces[0].shape[0]
    widths = [p.shape[1] for p in pieces]
    total = sum(widths)
    ts = _rows(s, total // 2)

    def body(*refs):
        o_ref, off = refs[-1], 0
        for x_ref, w in zip(refs[:-1], widths):
            o_ref[:, off:off + w] = x_ref[...].astype(BF16)
            off += w

    return _pcall(body, name=name, out_shape=jax.ShapeDtypeStruct((s, total), BF16), grid=(s // ts,),
                  in_specs=[pl.BlockSpec((ts, w), lambda i: (i, 0)) for w in widths],
                  out_specs=pl.BlockSpec((ts, total), lambda i: (i, 0)))(*pieces)


def _block_rows(blk):
    if isinstance(blk, int):
        return pl.ds(blk * BLOCK, BLOCK)
    return pl.ds(pl.multiple_of(blk * BLOCK, BLOCK), BLOCK)


def _band_window(n, length):
    width = min(2 * BLOCK, length)
    row = lax.broadcasted_iota(jnp.int32, (BLOCK, width), 0)
    col = lax.broadcasted_iota(jnp.int32, (BLOCK, width), 1)
    if width == BLOCK:
        return pl.ds(0, BLOCK), col <= row
    first = n - 1 if isinstance(n, int) else jnp.maximum(n - 1, 0)
    first = max(first, 0) if isinstance(first, int) else first
    dist = row - col + (n - first) * BLOCK
    start = first * BLOCK if isinstance(first, int) else pl.multiple_of(first * BLOCK, BLOCK)
    return pl.ds(start, width), jnp.logical_and(dist >= 0, dist <= BLOCK)


def _dil_fwd(q_arr, k_arr, v_arr, offs, length, dil, *, name):
    nj, nb = dil * HEADS_PER_GROUP, length // BLOCK
    ju, nq = (HEADS_PER_GROUP, 2) if nb > 1 else (2 * HEADS_PER_GROUP, 1)
    qo, ko, vo = (off // ju for off in offs)
    assert all(off % ju == 0 for off in offs) and nb % nq == 0 and nj % ju == 0

    def body(q_ref, k_ref, v_ref, o_ref, l_ref):
        for qq in range(nq):
            qrows = slice(qq * BLOCK, (qq + 1) * BLOCK)
            rows, mask = _band_window(pl.program_id(1) * nq + qq, length)
            for cb in range(ju):
                lanes = slice(cb * HEAD_DIM, (cb + 1) * HEAD_DIM)
                sc = _dot(q_ref[qrows, lanes].astype(BF16), k_ref[rows, lanes].astype(BF16), NT) * ATT_SCALE
                sc = jnp.where(mask, sc, MASKED)
                m = sc.max(axis=-1, keepdims=True)
                p = jnp.exp(sc - m)
                den = jnp.sum(p, axis=-1, keepdims=True)
                acc = _dot(p.astype(BF16), v_ref[rows, lanes].astype(BF16))
                o_ref[qrows, lanes] = acc / den
                l_ref[qrows, lanes] = jnp.broadcast_to(m + jnp.log(den), (BLOCK, HEAD_DIM))

    qspec = pl.BlockSpec((nq * BLOCK, ju * HEAD_DIM), lambda j, n: (n, qo + j))
    kspec = pl.BlockSpec((length, ju * HEAD_DIM), lambda j, n: (0, ko + j))
    vspec = pl.BlockSpec((length, ju * HEAD_DIM), lambda j, n: (0, vo + j))
    ospec = pl.BlockSpec((nq * BLOCK, ju * HEAD_DIM), lambda j, n: (n, j))
    shp = jax.ShapeDtypeStruct((length, nj * HEAD_DIM), F32)
    return _pcall(body, name=name, out_shape=(shp, shp), grid=(nj // ju, nb // nq), in_specs=[qspec, kspec, vspec],
                  out_specs=(ospec, ospec))(q_arr, k_arr, v_arr)


def _dil_bwd(q_arr, k_arr, v_arr, offs, o, lse, do, dlse, length, dil, *, name):
    nj, nb = dil * HEADS_PER_GROUP, length // BLOCK
    ju = 2 * HEADS_PER_GROUP if length <= 4 * BLOCK else 2
    qo, ko, vo = (off // ju for off in offs)
    assert all(off % ju == 0 for off in offs)

    def body(q_ref, k_ref, v_ref, o_ref, l_ref, do_ref, dl_ref, dq_ref, dk_ref, dv_ref):
        dk_ref[...] = jnp.zeros_like(dk_ref)
        dv_ref[...] = jnp.zeros_like(dv_ref)

        def step(n, carry):
            qrows = _block_rows(n)
            rows, mask = _band_window(n, length)
            for cb in range(ju):
                lanes = slice(cb * HEAD_DIM, (cb + 1) * HEAD_DIM)
                q = q_ref[qrows, lanes].astype(BF16)
                dof = do_ref[qrows, lanes]
                dob = dof.astype(BF16)
                lse_c = l_ref[qrows, lanes][:, :1]
                shift = dl_ref[qrows, lanes][:, :1] - jnp.sum(dof * o_ref[qrows, lanes], axis=-1, keepdims=True)
                kk, vv = k_ref[rows, lanes].astype(BF16), v_ref[rows, lanes].astype(BF16)
                sc = _dot(q, kk, NT) * ATT_SCALE
                p = jnp.where(mask, jnp.exp(sc - lse_c), 0.0)
                ds = (p * (_dot(dob, vv, NT) + shift)).astype(BF16)
                dq_ref[qrows, lanes] = _dot(ds, kk) * ATT_SCALE
                dk_ref[rows, lanes] += _dot(ds, q, TN) * ATT_SCALE
                dv_ref[rows, lanes] += _dot(p.astype(BF16), dob, TN)
            return carry

        if nb == 1:
            step(0, 0)
        else:
            lax.fori_loop(0, nb, step, 0)

    def col(off):
        return pl.BlockSpec((length, ju * HEAD_DIM), lambda j: (0, off + j))

    shp = jax.ShapeDtypeStruct((length, nj * HEAD_DIM), F32)
    return _pcall(body, name=name, out_shape=(shp, shp, shp), grid=(nj // ju,),
                  in_specs=[col(qo), col(ko), col(vo), col(0), col(0), col(0), col(0)],
                  out_specs=(col(0), col(0), col(0)))(q_arr, k_arr, v_arr, o, lse, do, dlse)


DIL_RESIDUES_PER_STEP = 8


def _dil_tokens(n, r, dil, length):
    width = min(2 * BLOCK, length)
    _, mask = _band_window(n, length)
    first = 0 if width == BLOCK else jnp.maximum(n - 1, 0)
    return (pl.ds(n * (BLOCK * dil) + r, BLOCK, stride=dil), pl.ds(first * (BLOCK * dil) + r, width, stride=dil),
            mask)


def _dil_head_specs(seq, group):
    first = group * HEADS_PER_GROUP

    def col(c0):
        return pl.BlockSpec((seq, HEAD_DIM), lambda h, r: (0, c0 + h))

    return col(first), col(A_HEADS + first), col(OFF_VA // HEAD_DIM + first), col(0)


def _dil_fwd_strided(qk32, proj, group, dil, *, name):
    seq = proj.shape[0]
    length = seq // dil
    nb, rp = length // BLOCK, min(dil, DIL_RESIDUES_PER_STEP)
    assert dil % rp == 0

    def body(q_ref, k_ref, v_ref, o_ref, l_ref):
        rgroup = pl.program_id(1)

        def step(n, carry):
            for rr in range(rp):
                tok_q, tok_k, mask = _dil_tokens(n, rgroup * rp + rr, dil, length)
                sc = _dot(q_ref[tok_q, :].astype(BF16), k_ref[tok_k, :].astype(BF16), NT) * ATT_SCALE
                sc = jnp.where(mask, sc, MASKED)
                m = sc.max(axis=-1, keepdims=True)
                p = jnp.exp(sc - m)
                den = jnp.sum(p, axis=-1, keepdims=True)
                o_ref[tok_q, :] = _dot(p.astype(BF16), v_ref[tok_k, :].astype(BF16)) / den
                l_ref[tok_q, :] = jnp.broadcast_to(m + jnp.log(den), (BLOCK, HEAD_DIM))
            return carry

        if nb == 1:
            step(0, 0)
        else:
            lax.fori_loop(0, nb, step, 0)

    qs, ks, vs, nat = _dil_head_specs(seq, group)
    shp = jax.ShapeDtypeStruct((seq, GROUP_W), F32)
    return _pcall(body, name=name, out_shape=(shp, shp), grid=(HEADS_PER_GROUP, dil // rp), in_specs=[qs, ks, vs],
                  out_specs=(nat, nat))(qk32, qk32, proj)


def _dil_bwd_strided(qk32, proj, group, o, lse, do, dlse, dil, *, name):
    seq = proj.shape[0]
    length = seq // dil
    nb, rp = length // BLOCK, min(dil, DIL_RESIDUES_PER_STEP)
    assert dil % rp == 0

    def body(q_ref, k_ref, v_ref, o_ref, l_ref, do_ref, dl_ref, dq_ref, dk_ref, dv_ref):
        rgroup = pl.program_id(1)

        @pl.when(rgroup == 0)
        def _():
            dk_ref[...] = jnp.zeros_like(dk_ref)
            dv_ref[...] = jnp.zeros_like(dv_ref)

        def step(n, carry):
            for rr in range(rp):
                tok_q, tok_k, mask = _dil_tokens(n, rgroup * rp + rr, dil, length)
                q = q_ref[tok_q, :].astype(BF16)
                dof = do_ref[tok_q, :]
                dob = dof.astype(BF16)
                lse_c = l_ref[tok_q, :][:, :1]
                shift = dl_ref[tok_q, :][:, :1] - jnp.sum(dof * o_ref[tok_q, :], axis=-1, keepdims=True)
                kk, vv = k_ref[tok_k, :].astype(BF16), v_ref[tok_k, :].astype(BF16)
                sc = _dot(q, kk, NT) * ATT_SCALE
                p = jnp.where(mask, jnp.exp(sc - lse_c), 0.0)
                ds = (p * (_dot(dob, vv, NT) + shift)).astype(BF16)
                dq_ref[tok_q, :] = _dot(ds, kk) * ATT_SCALE
                dk_ref[tok_k, :] += _dot(ds, q, TN) * ATT_SCALE
                dv_ref[tok_k, :] += _dot(p.astype(BF16), dob, TN)
            return carry

        if nb == 1:
            step(0, 0)
        else:
            lax.fori_loop(0, nb, step, 0)

    qs, ks, vs, nat = _dil_head_specs(seq, group)
    shp = jax.ShapeDtypeStruct((seq, GROUP_W), F32)
    return _pcall(body, name=name, out_shape=(shp, shp, shp), grid=(HEADS_PER_GROUP, dil // rp),
                  in_specs=[qs, ks, vs, nat, nat, nat, nat], out_specs=(nat, nat, nat))(
                      qk32, qk32, proj, o, lse, do, dlse)


def _combine_weights(l_refs):
    ls = [r[...] for r in l_refs]
    m = jnp.maximum(jnp.maximum(ls[0], ls[1]), ls[2])
    es = [jnp.exp(l - m) for l in ls]
    den = es[0] + es[1] + es[2]
    return [e / den for e in es]


def _combine_fwd(os_, lses, *, name):
    s = os_[0].shape[0]
    ts = _rows(s, GROUP_W)

    def body(o0, o1, o2, l0, l1, l2, out_ref):
        w = _combine_weights((l0, l1, l2))
        out_ref[...] = (w[0] * o0[...] + w[1] * o1[...] + w[2] * o2[...]).astype(BF16)

    row = pl.BlockSpec((ts, GROUP_W), lambda i: (i, 0))
    return _pcall(body, name=name, out_shape=jax.ShapeDtypeStruct((s, GROUP_W), BF16), grid=(s // ts,),
                  in_specs=[row] * 6, out_specs=row)(*os_, *lses)


def _combine_bwd(do_a, os_, lses, *, name):
    s = do_a.shape[0]
    ts = _rows(s, GROUP_W)

    def body(d_ref, o0, o1, o2, l0, l1, l2, do0, do1, do2, dl0, dl1, dl2):
        w = _combine_weights((l0, l1, l2))
        d = d_ref[...]
        og = [o0[...], o1[...], o2[...]]
        oa = w[0] * og[0] + w[1] * og[1] + w[2] * og[2]
        ta = jnp.sum(d * oa, axis=-1, keepdims=True)
        for g, (do_ref, dl_ref) in enumerate(((do0, dl0), (do1, dl1), (do2, dl2))):
            do_ref[...] = w[g] * d
            dl_ref[...] = w[g] * (jnp.sum(d * og[g], axis=-1, keepdims=True) - ta)

    head = pl.BlockSpec((ts, HEAD_DIM), lambda i, h: (i, h))
    shp = jax.ShapeDtypeStruct((s, GROUP_W), F32)
    return _pcall(body, name=name, out_shape=(shp,) * 6, grid=(s // ts, HEADS_PER_GROUP),
                  in_specs=[head] * 7, out_specs=(head,) * 6)(do_a, *os_, *lses)


def _dot_exact(x, ones_mask):
    hi = x.astype(BF16)
    r1 = x - hi.astype(F32)
    mid = r1.astype(BF16)
    lo = (r1 - mid.astype(F32)).astype(BF16)
    return _dot(hi, ones_mask) + _dot(mid, ones_mask) + _dot(lo, ones_mask)


SB_QROWS = 2 * BLOCK
SB_UNROLL = 4
SB_HEADS_PER_STEP = 2
SB_LANES = [slice(hh * HEAD_DIM, (hh + 1) * HEAD_DIM) for hh in range(SB_HEADS_PER_STEP)]


def _sb_mask(j, i):
    row = lax.broadcasted_iota(jnp.int32, (SB_QROWS, BLOCK), 0)
    col = lax.broadcasted_iota(jnp.int32, (SB_QROWS, BLOCK), 1)
    return col + (j * BLOCK - i * SB_QROWS) < row


def _sb_steps(i):
    return ((i + 1) * (SB_QROWS // BLOCK) + SB_UNROLL - 1) // SB_UNROLL


def _sb_scores(q, kk, j, i, masked):
    mask = _sb_mask(j, i) if masked else None
    z = _dot(q, kk, NT) * ATT_SCALE
    sp = jnp.log(1.0 + jnp.exp(-jnp.abs(z)))
    log_beta = jnp.minimum(z, 0.0) - sp
    log_1mb = jnp.minimum(-z, 0.0) - sp
    if masked:
        log_1mb = jnp.where(mask, log_1mb, 0.0)
    return z, log_beta, log_1mb, mask


def _sb_weights(log_beta, log_1mb, mask, run, upper):
    a = jnp.exp(log_beta + (run + _dot_exact(log_1mb, upper)))
    return a if mask is None else jnp.where(mask, a, 0.0)


def _sb_peeled(nsteps, make_step, init, masked_first):
    if masked_first:
        return lax.fori_loop(1, nsteps, make_step(False), make_step(True)(0, init))
    return make_step(True)(nsteps - 1, lax.fori_loop(0, nsteps - 1, make_step(False), init))


def _tri(strict_lower):
    row = lax.broadcasted_iota(jnp.int32, (BLOCK, BLOCK), 0)
    col = lax.broadcasted_iota(jnp.int32, (BLOCK, BLOCK), 1)
    return ((row > col) if strict_lower else (row < col)).astype(BF16)


def _sb_fwd(proj, *, name):
    s = proj.shape[0]
    assert s % (BLOCK * SB_UNROLL) == 0 and s % SB_QROWS == 0

    def body(q_ref, k_ref, v_ref, o_ref):
        i = pl.program_id(1)
        qs = [q_ref[:, lanes].astype(BF16) for lanes in SB_LANES]
        upper = _tri(True)
        nsteps = _sb_steps(i)

        def make_step(masked):
            def step(t, carry):
                carry = list(carry)
                for b in reversed(range(SB_UNROLL)):
                    j = (nsteps - 1 - t) * SB_UNROLL + b
                    rows = _block_rows(j)
                    for hh, lanes in enumerate(SB_LANES):
                        acc, run = carry[hh]
                        _, log_beta, log_1mb, mask = _sb_scores(qs[hh], k_ref[rows, lanes].astype(BF16), j, i, masked)
                        a = _sb_weights(log_beta, log_1mb, mask, run, upper)
                        carry[hh] = (acc + _dot(a.astype(BF16), v_ref[rows, lanes].astype(BF16)),
                                     run + jnp.sum(log_1mb, axis=-1, keepdims=True))
                return tuple(carry)
            return step

        zero = (jnp.zeros((SB_QROWS, HEAD_DIM), F32), jnp.zeros((SB_QROWS, 1), F32))
        for lanes, (acc, _) in zip(SB_LANES, _sb_peeled(nsteps, make_step, (zero,) * SB_HEADS_PER_STEP, True)):
            o_ref[:, lanes] = acc.astype(BF16)

    width = SB_HEADS_PER_STEP * HEAD_DIM
    qb, kb, vb = (off // width for off in (OFF_QB, OFF_KB, OFF_VB))
    return _pcall(body, name=name, out_shape=jax.ShapeDtypeStruct((s, B_W), BF16),
                  grid=(SB_HEADS // SB_HEADS_PER_STEP, s // SB_QROWS),
                  in_specs=[pl.BlockSpec((SB_QROWS, width), lambda h, i: (i, qb + h)),
                            pl.BlockSpec((s, width), lambda h, i: (0, kb + h)),
                            pl.BlockSpec((s, width), lambda h, i: (0, vb + h))],
                  out_specs=pl.BlockSpec((SB_QROWS, width), lambda h, i: (i, h)))(proj, proj, proj)


def _sb_bwd(proj, do_b, *, name):
    s = proj.shape[0]
    assert s % (BLOCK * SB_UNROLL) == 0 and s % SB_QROWS == 0
    nkb = s // BLOCK

    def body(q_ref, k_ref, v_ref, do_ref, dq_ref, dk_ref, dv_ref, z_s, a_s):
        i = pl.program_id(1)

        @pl.when(i == 0)
        def _():
            dk_ref[...] = jnp.zeros_like(dk_ref)
            dv_ref[...] = jnp.zeros_like(dv_ref)

        qs = [q_ref[:, lanes].astype(BF16) for lanes in SB_LANES]
        dobs = [do_ref[:, lanes].astype(BF16) for lanes in SB_LANES]
        upper, lower = _tri(True), _tri(False)
        nsteps = _sb_steps(i)

        def make_recompute(masked):
            def recompute(t, runs):
                runs = list(runs)
                for b in reversed(range(SB_UNROLL)):
                    j = (nsteps - 1 - t) * SB_UNROLL + b
                    rows = _block_rows(j)
                    for hh, lanes in enumerate(SB_LANES):
                        z, log_beta, log_1mb, mask = _sb_scores(qs[hh], k_ref[rows, lanes].astype(BF16), j, i, masked)
                        z_s[hh, j] = z
                        a_s[hh, j] = _sb_weights(log_beta, log_1mb, mask, runs[hh], upper)
                        runs[hh] = runs[hh] + jnp.sum(log_1mb, axis=-1, keepdims=True)
                return tuple(runs)
            return recompute

        _sb_peeled(nsteps, make_recompute, (jnp.zeros((SB_QROWS, 1), F32),) * SB_HEADS_PER_STEP, True)

        def make_grads(masked):
            def grads(t, carry):
                carry = list(carry)
                for b in range(SB_UNROLL):
                    j = t * SB_UNROLL + b
                    rows = _block_rows(j)
                    for hh, lanes in enumerate(SB_LANES):
                        dq, run = carry[hh]
                        kk, vv = k_ref[rows, lanes].astype(BF16), v_ref[rows, lanes].astype(BF16)
                        z, a = z_s[hh, j], a_s[hh, j]
                        de = _dot(dobs[hh], vv, NT) * a
                        beta = jax.nn.sigmoid(z)
                        one_minus_beta = 1.0 - beta
                        if masked:
                            beta = jnp.where(_sb_mask(j, i), beta, 0.0)
                        dz = (de * one_minus_beta - beta * (run + _dot_exact(de, lower))).astype(BF16)
                        dk_ref[rows, lanes] += _dot(dz, qs[hh], TN) * ATT_SCALE
                        dv_ref[rows, lanes] += _dot(a.astype(BF16), dobs[hh], TN)
                        carry[hh] = (dq + _dot(dz, kk), run + jnp.sum(de, axis=-1, keepdims=True))
                return tuple(carry)
            return grads

        zero = (jnp.zeros((SB_QROWS, HEAD_DIM), F32), jnp.zeros((SB_QROWS, 1), F32))
        for lanes, (dq, _) in zip(SB_LANES, _sb_peeled(nsteps, make_grads, (zero,) * SB_HEADS_PER_STEP, False)):
            dq_ref[:, lanes] = dq * ATT_SCALE

    width = SB_HEADS_PER_STEP * HEAD_DIM
    qb, kb, vb = (off // width for off in (OFF_QB, OFF_KB, OFF_VB))
    blk = pl.BlockSpec((SB_QROWS, width), lambda h, i: (i, h))
    full = pl.BlockSpec((s, width), lambda h, i: (0, h))
    shp = jax.ShapeDtypeStruct((s, B_W), F32)
    saved = pltpu.VMEM((SB_HEADS_PER_STEP, nkb, SB_QROWS, BLOCK), F32)
    return _pcall(body, name=name, out_shape=(shp, shp, shp), grid=(SB_HEADS // SB_HEADS_PER_STEP, s // SB_QROWS),
                  in_specs=[pl.BlockSpec((SB_QROWS, width), lambda h, i: (i, qb + h)),
                            pl.BlockSpec((s, width), lambda h, i: (0, kb + h)),
                            pl.BlockSpec((s, width), lambda h, i: (0, vb + h)), blk],
                  out_specs=(blk, full, full), scratch=[saved, saved])(proj, proj, proj, do_b)


def _coords():
    return lax.axis_index("x"), lax.axis_index("y"), lax.axis_index("c")


def _flip(v, bit):
    return 1 - v if bit else v


def _shard_of(ref, axis, idx, size):
    if axis == 0:
        sl = pl.ds(pl.multiple_of(idx * size, 16), size)
        return ref.at[sl, :] if len(ref.shape) == 2 else ref.at[:, sl, :]
    sl = pl.ds(pl.multiple_of(idx * size, 128), size)
    return ref.at[:, sl] if len(ref.shape) == 2 else ref.at[:, :, sl]


def _small_allgather(v, *, name, silu=False):
    n = v.shape[1]

    def body(v_ref, out_ref, send_sems, recv_sems):
        x, y, c = _coords()
        me = 4 * x + 2 * y + c
        val = v_ref[...]
        out_ref[me] = val * jax.nn.sigmoid(val) if silu else val
        copies = []
        for k in range(1, N_DEV):
            peer = (_flip(x, k & 4), _flip(y, k & 2), _flip(c, k & 1))
            copies.append(pltpu.make_async_remote_copy(
                src_ref=out_ref.at[me], dst_ref=out_ref.at[me], send_sem=send_sems.at[k - 1],
                recv_sem=recv_sems.at[k - 1], device_id=peer, device_id_type=MESH))
        for cp in copies:
            cp.start()
        for cp in copies:
            cp.wait_recv()
        for cp in copies:
            cp.wait_send()

    return _pcall(body, name=name, out_shape=jax.ShapeDtypeStruct((N_DEV, 1, n), F32),
                  in_specs=[pl.BlockSpec(memory_space=pltpu.VMEM)], out_specs=pl.BlockSpec(memory_space=pltpu.VMEM),
                  scratch=[pltpu.SemaphoreType.DMA((N_DEV - 1,)), pltpu.SemaphoreType.DMA((N_DEV - 1,))])(v)


def _cast_place(w, layer, axis, me, *, name):
    _, r, c = w.shape
    tr = _rows(r, c)
    nrt = r // tr

    def body(me_ref, w_ref, o_ref):
        o_ref[...] = w_ref[...].astype(BF16)

    wspec = pl.BlockSpec((None, tr, c), lambda i, me_ref: (layer, i, 0))
    if axis == 0:
        ospec = pl.BlockSpec((tr, c), lambda i, me_ref: (me_ref[0] * nrt + i, 0))
        shape = (r * N_DEV, c)
    else:
        ospec = pl.BlockSpec((tr, c), lambda i, me_ref: (i, me_ref[0]))
        shape = (r, c * N_DEV)
    return _pcall(body, name=name, out_shape=jax.ShapeDtypeStruct(shape, BF16), grid=(nrt,), in_specs=[wspec],
                  out_specs=ospec, prefetch=1)(me, w)


def _pair_sum(grad, sib, core, axis, *, name):
    _, r, c = sib.shape
    tr = _rows(r, c // 2)
    nrt = r // tr

    def body(core_ref, g_ref, s_ref, o_ref):
        o_ref[...] = (g_ref[...].astype(F32) + s_ref[...].astype(F32)).astype(BF16)

    if axis == 0:
        gspec = pl.BlockSpec((tr, c), lambda q, i, core_ref: ((2 * q + core_ref[0]) * nrt + i, 0))
    else:
        gspec = pl.BlockSpec((tr, c), lambda q, i, core_ref: (i, 2 * q + core_ref[0]))
    sspec = pl.BlockSpec((None, tr, c), lambda q, i, core_ref: (q, i, 0))
    return _pcall(body, name=name, out_shape=jax.ShapeDtypeStruct(sib.shape, BF16), grid=(N_CHIPS, nrt),
                  in_specs=[gspec, sspec], out_specs=sspec, prefetch=1)(core, grad, sib)


ANY_SPEC = pl.BlockSpec(memory_space=pl.ANY)
SEM_SPEC = pl.BlockSpec(memory_space=pltpu.SEMAPHORE)
SPLIT_PARAMS = dict(has_side_effects=pltpu.SideEffectType.DATAFLOW_SIDE_EFFECTING)


def _split_start(copies_fn, buffers, sem_shape, after, *, name):
    n = len(buffers)
    rows, cols = sem_shape
    ns = rows * cols
    extra = ([] if after is None else [after]) + _take_token()

    def body(*refs):
        sems = refs[n + len(extra):n + len(extra) + 2 * ns]
        for cp in copies_fn(refs[:n], _sem_rows(sems[:ns], cols), _sem_rows(sems[ns:], cols)):
            cp.start()
        refs[-1][...] = jnp.zeros_like(refs[-1])

    sem = pltpu.SemaphoreType.DMA(())
    outs = pl.pallas_call(
        body, name=name,
        out_shape=((sem,) * (2 * ns) + tuple(jax.ShapeDtypeStruct(b.shape, b.dtype) for b in buffers) + (TOKEN,)),
        in_specs=(ANY_SPEC,) * (n + len(extra)),
        out_specs=(SEM_SPEC,) * (2 * ns) + (ANY_SPEC,) * n + (pl.BlockSpec(memory_space=pltpu.VMEM),),
        input_output_aliases={i: 2 * ns + i for i in range(n)},
        compiler_params=pltpu.CompilerParams(**SPLIT_PARAMS))(*buffers, *extra)
    _ORDER["token"] = outs[-1]
    return list(outs[:ns]), list(outs[ns:2 * ns]), list(outs[2 * ns:2 * ns + n]), outs[-1]


def _split_wait(copies_fn, send_sems, recv_sems, buffers, after, sem_rows, *, name):
    n, ns = len(buffers), len(send_sems)
    cols = ns // sem_rows
    extra = ([] if after is None else [after]) + _take_token()

    def body(*refs):
        sems = refs[n:n + 2 * ns]
        copies = copies_fn(refs[:n], _sem_rows(sems[:ns], cols), _sem_rows(sems[ns:], cols))
        for cp in copies:
            cp.wait_send()
        for cp in copies:
            cp.wait_recv()
        refs[-1][...] = jnp.zeros_like(refs[-1])

    outs = pl.pallas_call(
        body, name=name, out_shape=tuple(jax.ShapeDtypeStruct(b.shape, b.dtype) for b in buffers) + (TOKEN,),
        in_specs=(ANY_SPEC,) * n + (SEM_SPEC,) * (2 * ns) + (ANY_SPEC,) * len(extra),
        out_specs=(ANY_SPEC,) * n + (pl.BlockSpec(memory_space=pltpu.VMEM),),
        input_output_aliases={i: i for i in range(n)},
        compiler_params=pltpu.CompilerParams(**SPLIT_PARAMS))(*buffers, *send_sems, *recv_sems, *extra)
    _ORDER["token"] = outs[-1]
    return list(outs[:n])


def _sem_rows(sems, cols):
    return [sems[i:i + cols] for i in range(0, len(sems), cols)]


def _empty_hbm(shape, dtype):
    return pltpu.with_memory_space_constraint(lax.empty(shape, dtype), pltpu.HBM)


class _SplitGather:
    def __init__(self, fulls, axes, tag):
        self.axes, self.tag, self.nt = list(axes), tag, len(fulls)
        self.sizes = [f.shape[ax] // N_DEV for f, ax in zip(fulls, axes)]
        self.fulls = list(fulls)

    def _slot(self, ref, t, dev):
        return _shard_of(ref, self.axes[t], 4 * dev[0] + 2 * dev[1] + dev[2], self.sizes[t])

    def _first_copies(self, refs, send_sems, recv_sems):
        x, y, c = _coords()
        peers = [(x, y, 1 - c), (1 - x, y, c), (x, 1 - y, c), (1 - x, 1 - y, c)]
        return [pltpu.make_async_remote_copy(
            src_ref=self._slot(refs[t], t, (x, y, c)), dst_ref=self._slot(refs[t], t, (x, y, c)),
            send_sem=send_sems[t][k], recv_sem=recv_sems[t][k], device_id=peer, device_id_type=MESH)
            for t in range(self.nt) for k, peer in enumerate(peers)]

    def _forward_copies(self, refs, send_sems, recv_sems):
        x, y, c = _coords()
        chips = [(1 - x, y), (x, 1 - y), (1 - x, 1 - y)]
        return [pltpu.make_async_remote_copy(
            src_ref=self._slot(refs[t], t, (*chip, c)), dst_ref=self._slot(refs[t], t, (*chip, c)),
            send_sem=send_sems[t][j], recv_sem=recv_sems[t][j], device_id=(x, y, 1 - c), device_id_type=MESH)
            for t in range(self.nt) for j, chip in enumerate(chips)]

    def first(self, after):
        self.s1, self.r1, self.fulls, token = _split_start(
            self._first_copies, self.fulls, (self.nt, 4), after, name=f"comm_gather1_start_{self.tag}")
        return token

    def forward(self, after):
        bufs = _split_wait(self._first_copies, self.s1, self.r1, self.fulls, after, self.nt,
                           name=f"comm_gather1_wait_{self.tag}")
        self.s2, self.r2, self.fulls, token = _split_start(
            self._forward_copies, bufs, (self.nt, 3), after, name=f"comm_gather2_start_{self.tag}")
        return token

    def finish(self, after):
        return _split_wait(self._forward_copies, self.s2, self.r2, self.fulls, after, self.nt,
                           name=f"comm_gather2_wait_{self.tag}")


class _SplitPairExchange:
    def __init__(self, grads, axes, tag):
        self.nt, self.tag, self.axes = len(grads), tag, list(axes)
        self.grads = list(grads)
        self.sizes = [g.shape[ax] // N_DEV for g, ax in zip(grads, axes)]

    def _copies(self, refs, send_sems, recv_sems):
        nt = self.nt
        x, y, c = _coords()
        return [pltpu.make_async_remote_copy(
            src_ref=_shard_of(refs[t], self.axes[t], 2 * q + 1 - c, self.sizes[t]), dst_ref=refs[nt + t].at[q],
            send_sem=send_sems[t][q], recv_sem=recv_sems[t][q], device_id=(x, y, 1 - c), device_id_type=MESH)
            for t in range(nt) for q in range(N_CHIPS)]

    def start(self):
        landing = []
        for g, ax in zip(self.grads, self.axes):
            dims = list(g.shape)
            dims[ax] //= N_DEV
            landing.append(_empty_hbm((N_CHIPS, *dims), g.dtype))
        self.s, self.r, self.bufs, token = _split_start(
            self._copies, self.grads + landing, (self.nt, N_CHIPS), None,
            name=f"comm_rs_pair_start_{self.tag}")
        return token

    def finish(self, after):
        bufs = _split_wait(self._copies, self.s, self.r, self.bufs, after, self.nt,
                           name=f"comm_rs_pair_wait_{self.tag}")
        return bufs[:self.nt], bufs[self.nt:]


class _ReducePipeline:
    def __init__(self, core):
        self.core, self.items, self.done, self.now = core, [], [], 0

    def add(self, keys, grads, layer):
        axes = [SHARD_AXIS[k] for k in keys]
        pair = _SplitPairExchange([grads[k] for k in keys], axes, f"{keys[0]}{layer}")
        pair.start()
        self.items.append(dict(keys=keys, layer=layer, axes=axes, pair=pair, state="pair", since=self.now))

    def tick(self, after, flush=False):
        self.now += 1
        for it in self.items:
            if it["state"] == "pair" and it["since"] < self.now:
                grads, sib = it["pair"].finish(after)
                sums = [_pair_sum(g, s_, self.core, ax, name="pair_sum_" + k)
                        for k, g, s_, ax in zip(it["keys"], grads, sib, it["axes"])]
                it["chip"] = _SplitChipExchange(sums, f"{it['keys'][0]}{it['layer']}")
                it["chip"].start()
                it.update(state="chip", since=self.now)
            elif it["state"] == "chip" and (flush or self.now - it["since"] >= 2):
                sums, remote = it["chip"].finish(after)
                self.done.append((it["keys"], it["layer"], sums, remote))
                it["state"] = "done"

    def take_done(self):
        out, self.done = self.done, []
        return out


class _SplitChipExchange:
    def __init__(self, sums, tag):
        self.nt, self.tag = len(sums), tag
        self.sums = list(sums)

    def _copies(self, refs, send_sems, recv_sems):
        nt = self.nt
        x, y, c = _coords()
        copies = []
        for t in range(nt):
            for k in range(1, N_CHIPS):
                px, py = _flip(x, k & 2), _flip(y, k & 1)
                copies.append(pltpu.make_async_remote_copy(
                    src_ref=refs[t].at[2 * px + py], dst_ref=refs[nt + t].at[k - 1], send_sem=send_sems[t][k - 1],
                    recv_sem=recv_sems[t][k - 1], device_id=(px, py, c), device_id_type=MESH))
        return copies

    def start(self):
        landing = [_empty_hbm((N_CHIPS - 1,) + s.shape[1:], s.dtype) for s in self.sums]
        self.s, self.r, self.bufs, token = _split_start(
            self._copies, self.sums + landing, (self.nt, N_CHIPS - 1), None,
            name=f"comm_rs_chip_start_{self.tag}")
        return token

    def finish(self, after):
        bufs = _split_wait(self._copies, self.s, self.r, self.bufs, after, self.nt,
                           name=f"comm_rs_chip_wait_{self.tag}")
        return bufs[:self.nt], bufs[self.nt:]


def _adam_math(g, w, m, v):
    m2 = ADAM_B1 * m + (1.0 - ADAM_B1) * g
    v2 = ADAM_B2 * v + (1.0 - ADAM_B2) * (g * g)
    m_hat = m2 / (1.0 - ADAM_B1 ** ADAM_STEP)
    v_hat = v2 / (1.0 - ADAM_B2 ** ADAM_STEP)
    delta = -ADAM_LR * (m_hat / (jnp.sqrt(v_hat) + ADAM_EPS) + ADAM_WD * w)
    return delta, m2, v2


def _adamw_sharded(chip_sums, remote, chip, w, m, v, layer, prev, *, name):
    nl, r, c = w.shape
    tr = _rows(r, c)

    def body(*refs):
        p_ref, r0_ref, r1_ref, r2_ref, w_ref, m_ref, v_ref = refs[1:8]
        g_out, d_out, m_out, v_out = refs[-4:]
        g = ((p_ref[...].astype(F32) + r0_ref[...].astype(F32)) + r1_ref[...].astype(F32)) + r2_ref[...].astype(F32)
        g_out[...] = g
        d_out[...], m_out[...], v_out[...] = _adam_math(g, w_ref[...], m_ref[...], v_ref[...])

    pspec = pl.BlockSpec((None, tr, c), lambda i, chip_ref: (chip_ref[0], i, 0))

    def rspec(k):
        return pl.BlockSpec((None, tr, c), lambda i, chip_ref: (k, i, 0))

    wspec = pl.BlockSpec((None, tr, c), lambda i, chip_ref: (layer, i, 0))
    in_specs = [pspec, rspec(0), rspec(1), rspec(2), wspec, wspec, wspec]
    args = [chip, chip_sums, remote, remote, remote, w, m, v]
    aliases = {}
    if prev is not None:
        in_specs += [pl.BlockSpec(memory_space=pl.ANY)] * 4
        aliases = {len(args) + i: i for i in range(4)}
        args += list(prev)
    shp = jax.ShapeDtypeStruct(w.shape, F32)
    return _pcall(body, name=name, out_shape=(shp,) * 4, grid=(r // tr,), in_specs=in_specs, out_specs=(wspec,) * 4,
                  aliases=aliases, prefetch=1)(*args)


def _adamw_local(g, w, m, v, *, name):
    nl, r, c = w.shape
    tr = _rows(r, c)

    def body(g_ref, w_ref, m_ref, v_ref, d_out, m_out, v_out):
        d_out[...], m_out[...], v_out[...] = _adam_math(g_ref[...], w_ref[...], m_ref[...], v_ref[...])

    spec = pl.BlockSpec((None, tr, c), lambda l, i: (l, i, 0))
    shp = jax.ShapeDtypeStruct(w.shape, F32)
    return _pcall(body, name=name, out_shape=(shp,) * 3, grid=(nl, r // tr), in_specs=[spec] * 4,
                  out_specs=(spec,) * 3)(g, w, m, v)


def _adamw_replicated(parts, w, m, v, *, name):
    n = w.shape[1]

    def body(p_ref, w_ref, m_ref, v_ref, g_out, d_out, m_out, v_out):
        g = p_ref[0]
        for k in range(1, N_DEV):
            g = g + p_ref[k]
        g_out[...] = g
        d_out[...], m_out[...], v_out[...] = _adam_math(g, w_ref[...], m_ref[...], v_ref[...])

    vm = pl.BlockSpec(memory_space=pltpu.VMEM)
    shp = jax.ShapeDtypeStruct((1, n), F32)
    return _pcall(body, name=name, out_shape=(shp,) * 4, in_specs=[vm] * 4, out_specs=(vm,) * 4)(parts, w, m, v)


UNDILATED_OFFS = (0, A_HEADS, OFF_VA // HEAD_DIM)


def _mod_rows(mod, d):
    return [mod[:, i * d:(i + 1) * d] for i in range(6)]


MIXER_W = ("w_in", "w_branch_a", "w_branch_b", "w_out")
FFN_W = ("w_gate_up", "w_down")
SHARD_AXIS = {"w_in": 1, "w_branch_a": 1, "w_branch_b": 1, "w_out": 0, "w_gate_up": 1, "w_down": 0}


def _norm_args(mod, gain, which, d):
    rows = _mod_rows(mod, d)
    return gain, rows[3 * which + 1], rows[3 * which]


def _mixer_fwd_a(h, u, gains, w_in, cos2, sin2, hook):
    seq = h.shape[0]
    proj = _mm(u, w_in, name="mm_in")
    hook(proj)
    qk, qk32 = _qkrope_fwd(proj, gains, cos2, sin2, name="qkrope_fwd")
    os_, lses = [], []
    for g, dil in enumerate(DILATIONS):
        if dil == 1:
            o, lse = _dil_fwd(qk, qk, proj, UNDILATED_OFFS, seq, 1, name="dil_fwd_1")
        else:
            o, lse = _dil_fwd_strided(qk32, proj, g, dil, name=f"dil_fwd_{dil}")
        os_.append(o)
        lses.append(lse)
    o_a = _combine_fwd(os_, lses, name="combine_fwd")
    o_b = _sb_fwd(proj, name="sb_fwd")
    return dict(h_in=h, u=u, proj=proj, qk=qk, qk32=qk32, os=os_, lses=lses, o_a=o_a, o_b=o_b)


def _mixer_fwd_b(sv, mod, g2, wts):
    d = sv["h_in"].shape[1]
    merged, y_a, y_b = _mm_merge(sv["o_a"], sv["o_b"], wts["w_branch_a"], wts["w_branch_b"], sv["proj"],
                                 name="mm_branch")
    h_mid, t, u2 = _mm_resid_norm(merged, wts["w_out"], sv["h_in"], _mod_rows(mod, d)[2], _norm_args(mod, g2, 1, d),
                                  name="mm_out")
    sv.update(y_a=y_a, y_b=y_b, merged=merged, t=t, h_mid=h_mid, u2=u2)
    return h_mid


def _ffn_fwd_a(sv, w_gate_up):
    a, g, u = _mm_swiglu(sv["u2"], w_gate_up, name="mm_gate_up")
    sv.update(g=g, up=u, a=a)
    return a


def _ffn_fwd_b(sv, mod, w_down, next_norm):
    d = sv["h_mid"].shape[1]
    h_out, sv["f"], u_next = _mm_resid_norm(sv["a"], w_down, sv["h_mid"], _mod_rows(mod, d)[5], next_norm,
                                            name="mm_down")
    return h_out, u_next


def _wgrad(act, dout, key):
    return _mm(act, dout, ta=True, out_dtype=BF16, caps=(2048, 1024, 3072), name="mm_wgrad_" + key)


def _ffn_bwd(dh, df, dgate2, sv, mod, g2, wts, hook):
    d = dh.shape[1]
    sc2, ga1 = _mod_rows(mod, d)[4], _mod_rows(mod, d)[2]
    dg, dup = _mm_down_t_swiglu(df, wts["w_down"], sv["g"], sv["up"], name="mm_down_t")
    grads = {"w_down": _wgrad(sv["a"], df, "w_down")}
    hook(dup)
    du2 = _mm_cat_k(dg, dup, wts["w_gate_up"], name="mm_gate_up_t")
    grads["w_gate_up"] = _mm_cat_n(sv["u2"], dg, dup, name="mm_wgrad_w_gate_up")
    dh_mid, dsh2, dsc2, dg2, dt, dgate1 = _rmsmod_bwd(du2, sv["h_mid"], g2, sc2, dh, sv["t"], ga1, name="rmsmod_bwd")
    return dh_mid, [dsh2, dsc2, dgate2], dg2, grads, dt, dgate1


def _mixer_bwd(dh_mid, dt, dgate1, sv, mod, g1, gains, wts, cos2, sin2, hook, below):
    seq, d = dh_mid.shape
    sc1 = _mod_rows(mod, d)[1]
    dy_a, dy_b, dga, dgb = _mm_out_t_merge(dt, wts["w_out"], sv["proj"], sv["y_a"], sv["y_b"], name="mm_out_t")
    grads = {"w_out": _wgrad(sv["merged"], dt, "w_out")}
    do_a = _mm(dy_a, wts["w_branch_a"], tb=True, name="mm_branch_t")
    do_b = _mm(dy_b, wts["w_branch_b"], tb=True, name="mm_branch_t")
    grads["w_branch_a"] = _wgrad(sv["o_a"], dy_a, "w_branch_a")
    grads["w_branch_b"] = _wgrad(sv["o_b"], dy_b, "w_branch_b")
    dqb, dkb, dvb = _sb_bwd(sv["proj"], do_b, name="sb_bwd")
    hook(dqb, grads)
    comb = _combine_bwd(do_a, sv["os"], sv["lses"], name="combine_bwd")
    grads = {}
    dos, dls = comb[:3], comb[3:]
    dqs, dks, dvs = [], [], []
    for g, dil in enumerate(DILATIONS):
        if dil == 1:
            dq, dk, dv = _dil_bwd(sv["qk"], sv["qk"], sv["proj"], UNDILATED_OFFS, sv["os"][g], sv["lses"][g], dos[g],
                                  dls[g], seq, 1, name="dil_bwd_1")
        else:
            dq, dk, dv = _dil_bwd_strided(sv["qk32"], sv["proj"], g, sv["os"][g], sv["lses"][g], dos[g], dls[g], dil,
                                          name=f"dil_bwd_{dil}")
        dqs.append(dq)
        dks.append(dk)
        dvs.append(dv)
    dq_pre, dqn = _qkrope_bwd(dqs, sv["proj"], gains, 0, cos2, sin2, name="qkrope_bwd")
    dk_pre, dkn = _qkrope_bwd(dks, sv["proj"], gains, 1, cos2, sin2, name="qkrope_bwd")
    dgains = jnp.stack([dqn, dkn])
    dproj = _assemble([dq_pre, dk_pre] + dvs + [dqb, dkb, dvb, dga, dgb], name="assemble_dproj")
    du = _mm(dproj, wts["w_in"], tb=True, name="mm_in_t")
    grads["w_in"] = _wgrad(sv["u"], dproj, "w_in")
    dh_in, dsh1, dsc1, dg1, df, dgate2 = _rmsmod_bwd(du, sv["h_in"], g1, sc1, dh_mid, *(below or (None, None)),
                                                     name="rmsmod_bwd")
    return dh_in, [dsh1, dsc1, dgate1], dg1, dgains, grads, df, dgate2


def kernel(x, c, w_ada, b_ada, norm1_g, norm2_g, w_in, qn_g, kn_g, w_branch_a, w_branch_b, w_out, w_gate_up, w_down, loss_target, m_w_ada, m_b_ada, m_norm1_g, m_norm2_g, m_w_in, m_qn_g, m_kn_g, m_w_branch_a, m_w_branch_b, m_w_out, m_w_gate_up, m_w_down, v_w_ada, v_b_ada, v_norm1_g, v_norm2_g, v_w_in, v_qn_g, v_kn_g, v_w_branch_a, v_w_branch_b, v_w_out, v_w_gate_up, v_w_down):
    _ORDER["token"] = None
    seq, d = x.shape[1], x.shape[2]
    depth = w_in.shape[0]
    weights = dict(w_in=w_in, w_branch_a=w_branch_a, w_branch_b=w_branch_b, w_out=w_out, w_gate_up=w_gate_up,
                   w_down=w_down)
    moments_m = dict(w_in=m_w_in, w_branch_a=m_w_branch_a, w_branch_b=m_w_branch_b, w_out=m_w_out,
                     w_gate_up=m_w_gate_up, w_down=m_w_down)
    moments_v = dict(w_in=v_w_in, w_branch_a=v_w_branch_a, w_branch_b=v_w_branch_b, w_out=v_w_out,
                     w_gate_up=v_w_gate_up, w_down=v_w_down)
    xi, yi, ci = _coords()
    me = 4 * xi + 2 * yi + ci
    core = jnp.reshape(ci, (1,)).astype(jnp.int32)
    chip = jnp.reshape(2 * xi + yi, (1,)).astype(jnp.int32)

    ada_w = w_ada.shape[2]
    c_act = _small_allgather(c, name="comm_gather_c", silu=True).reshape(N_DEV, d)
    c_pad = jnp.concatenate([c_act, jnp.zeros_like(c_act)], axis=0).astype(BF16)
    bias = lax.dynamic_slice(b_ada, (0, me * ada_w), (depth, ada_w))
    mod_part = jnp.stack([_mm(c_pad, w_ada[l], name="mm_ada")[:N_DEV] for l in range(depth)]) + bias[:, None, :]
    mod_all = _small_allgather(mod_part.reshape(1, depth * N_DEV * ada_w), name="comm_gather_mod")
    mod_all = mod_all.reshape(N_DEV, depth, N_DEV, ada_w)
    mod_mine = lax.dynamic_index_in_dim(mod_all, me, axis=2, keepdims=False)
    mods = jnp.transpose(mod_mine, (1, 0, 2)).reshape(depth, 1, 6 * d)

    cos2, sin2 = _rope_tables(seq)
    gains = [jnp.stack([qn_g[l], kn_g[l]])[:, None, :] for l in range(depth)]
    g1s = [norm1_g[l][None] for l in range(depth)]
    g2s = [norm2_g[l][None] for l in range(depth)]

    me_arr = jnp.reshape(me, (1,)).astype(jnp.int32)

    def placed(keys, l):
        return [_cast_place(weights[k], l, SHARD_AXIS[k], me_arr, name="cast_place_" + k) for k in keys]

    def gather_of(keys, l, tag):
        return _SplitGather(placed(keys, l), [SHARD_AXIS[k] for k in keys], f"{tag}{l}")

    groups = []
    for l in range(depth):
        groups += [("w_in", l, MIXER_W[:1]), ("rest", l, MIXER_W[1:]), ("ffn", l, FFN_W)]
    gathers = {}

    def issue(some):
        for tag, l, keys in some:
            gathers[tag, l] = gather_of(keys, l, tag)
            gathers[tag, l].first(after=mods)

    issue(groups[:3])
    h = x[0]
    u = _rmsmod_fwd(h, *_norm_args(mods[0], g1s[0], 0, d), name="rmsmod_fwd")
    gathers["w_in", 0].forward(after=u)
    issue(groups[3:])
    wm = {"w_in": gathers["w_in", 0].finish(after=u)[0]}
    saved, full = [], []
    for l in range(depth):
        last = l + 1 == depth
        sv = _mixer_fwd_a(h, u, gains[l], wm["w_in"], cos2, sin2, gathers["rest", l].forward)
        gathers["ffn", l].forward(after=sv["o_b"])
        wm.update(zip(MIXER_W[1:], gathers["rest", l].finish(after=sv["o_b"])))
        h_mid = _mixer_fwd_b(sv, mods[l], g2s[l], wm)
        wf = dict(zip(FFN_W, gathers["ffn", l].finish(after=h_mid)))
        a = _ffn_fwd_a(sv, wf["w_gate_up"])
        if not last:
            gathers["w_in", l + 1].forward(after=a)
        h, u = _ffn_fwd_b(sv, mods[l], wf["w_down"],
                          None if last else _norm_args(mods[l + 1], g1s[l + 1], 0, d))
        saved.append(sv)
        full.append({**wm, **wf})
        if not last:
            wm = {"w_in": gathers["w_in", l + 1].finish(after=h)[0]}
    def ffn_gate(l):
        return saved[l]["f"], _mod_rows(mods[l], d)[5]

    loss_part, dh, df, dgate2 = _loss_fwd(h, loss_target[0], *ffn_gate(depth - 1), name="loss")
    loss = lax.psum(loss_part[0, 0], ("x", "y", "c"))

    pipe = _ReducePipeline(core)
    dmods, dg1s, dg2s, dgains = [None] * depth, [None] * depth, [None] * depth, [None] * depth
    for l in reversed(range(depth)):
        dh_mid, dmod_f, dg2s[l], grads, dt, dgate1 = _ffn_bwd(dh, df, dgate2, saved[l], mods[l], g2s[l], full[l],
                                                              pipe.tick)
        pipe.tick(dh_mid)
        pipe.add(FFN_W, grads, l)
        dh, dmod_m, dg1s[l], dgains[l], grads, df, dgate2 = _mixer_bwd(
            dh_mid, dt, dgate1, saved[l], mods[l], g1s[l], gains[l], full[l], cos2, sin2,
            lambda after, early, l=l: (pipe.tick(after), pipe.add(MIXER_W[1:], early, l)),
            ffn_gate(l - 1) if l > 0 else None)
        dmods[l] = jnp.concatenate(dmod_m + dmod_f, axis=1)
        pipe.tick(dh)
        pipe.add(MIXER_W[:1], grads, l)
    grad_x = dh[None]

    stacked = {}

    def update(items):
        for keys, l, sums, remote in items:
            for k, p_, r_ in zip(keys, sums, remote):
                stacked[k] = _adamw_sharded(p_, r_, chip, weights[k], moments_m[k], moments_v[k], l,
                                            stacked.get(k), name="adamw_" + k)

    ready = pipe.take_done()
    update([it for it in ready if it[0] != FFN_W])

    small = jnp.concatenate(
        dmods + dg1s + dg2s + [dgains[l][0] for l in range(depth)] + [dgains[l][1] for l in range(depth)], axis=1)
    small_all = _small_allgather(small, name="comm_gather_small")
    pipe.tick(small_all)
    update([it for it in ready if it[0] == FFN_W] + pipe.take_done())

    def pack(b, n1, n2, qn, kn):
        return jnp.concatenate([t_.reshape(1, -1) for t_ in (b, n1, n2, qn, kn)], axis=1)

    sg, sd, sm, sv_ = _adamw_replicated(small_all, pack(b_ada, norm1_g, norm2_g, qn_g, kn_g),
                                        pack(m_b_ada, m_norm1_g, m_norm2_g, m_qn_g, m_kn_g),
                                        pack(v_b_ada, v_norm1_g, v_norm2_g, v_qn_g, v_kn_g), name="adamw_replicated")

    def unpack(p):
        sizes = [depth * 6 * d, depth * d, depth * d, depth * HEAD_DIM, depth * HEAD_DIM]
        shapes = [b_ada.shape, norm1_g.shape, norm2_g.shape, qn_g.shape, kn_g.shape]
        out, off = [], 0
        for n, shp in zip(sizes, shapes):
            out.append(p[0, off:off + n].reshape(shp))
            off += n
        return dict(zip(("b_ada", "norm1_g", "norm2_g", "qn_g", "kn_g"), out))

    ug, ud, um, uv = unpack(sg), unpack(sd), unpack(sm), unpack(sv_)
    res = {k: dict(g=ug[k], d=ud[k], m=um[k], v=uv[k]) for k in ug}

    dmod_all = small_all[:, 0, :depth * 6 * d].reshape(N_DEV, depth, 6 * d)
    g_ada = None
    for l in range(depth):
        dm = lax.dynamic_slice(dmod_all[:, l, :], (0, me * ada_w), (N_DEV, ada_w))
        dm = jnp.concatenate([dm, jnp.zeros_like(dm)], axis=0).astype(BF16)
        g_ada = _mm(c_pad, dm, ta=True, name="mm_wgrad_ada", stack=(l, depth, g_ada))
    d_ada, m_ada, v_ada = _adamw_local(g_ada, w_ada, m_w_ada, v_w_ada, name="adamw_local")
    res["w_ada"] = dict(g=g_ada, d=d_ada, m=m_ada, v=v_ada)

    pipe.tick(d_ada)
    update(pipe.take_done())
    pipe.tick(d_ada, flush=True)
    update(pipe.take_done())
    for k, (g_, d_, m_, v_) in stacked.items():
        res[k] = dict(g=g_, d=d_, m=m_, v=v_)

    order = ("w_ada", "b_ada", "norm1_g", "norm2_g", "w_in", "qn_g", "kn_g", "w_branch_a", "w_branch_b", "w_out",
             "w_gate_up", "w_down")
    _ORDER["token"] = None
    return (loss, grad_x, *[res[k]["g"] for k in order], *[res[k]["d"] for k in order],
            *[res[k]["m"] for k in order], *[res[k]["v"] for k in order])
```

```python
import jax
import jax.numpy as jnp
from jax import lax
from jax.experimental import pallas as pl
from jax.experimental.pallas import tpu as pltpu

F32 = jnp.float32
BF16 = jnp.bfloat16

HEAD_DIM = 128
BLOCK = 128
DILATIONS = (1, 4, 16)
HEADS_PER_GROUP = 4
A_HEADS = 12
SB_HEADS = 4
GROUP_W = HEADS_PER_GROUP * HEAD_DIM
A_W = A_HEADS * HEAD_DIM
B_W = SB_HEADS * HEAD_DIM
OFF_QA, OFF_KA, OFF_VA = 0, A_W, 2 * A_W
OFF_QB, OFF_KB, OFF_VB = 3 * A_W, 3 * A_W + B_W, 3 * A_W + 2 * B_W
OFF_GATES = 3 * A_W + 3 * B_W
ROPE_THETA = 10000.0
EPS = 1e-6
ATT_SCALE = HEAD_DIM ** -0.5
MASKED = -1e30

ADAM_LR, ADAM_B1, ADAM_B2, ADAM_EPS, ADAM_WD, ADAM_STEP = 0.001, 0.9, 0.999, 1e-08, 0.01, 10

N_DEV = 8
N_CHIPS = 4
V7X_VMEM_LIMIT_BYTES = 56 * 1024 * 1024
ELEMWISE_BLOCK_BYTES = 2 * 1024 * 1024
MESH = pl.DeviceIdType.MESH

NN = (((1,), (0,)), ((), ()))
NT = (((1,), (1,)), ((), ()))
TN = (((0,), (0,)), ((), ()))


def _dot(a, b, dims=NN):
    return lax.dot_general(a, b, dims, preferred_element_type=F32)


def _tile(n, cap, mult=128):
    best = None
    for t in range(mult, min(n, cap) + 1, mult):
        if n % t == 0:
            best = t
    if best is None:
        assert n <= 2 * cap, (n, cap)
        return n
    return best


def _rows(r, c):
    return _tile(r, max(16, ELEMWISE_BLOCK_BYTES // (4 * c)), 16)


_ORDER = {"token": None}
TOKEN = jax.ShapeDtypeStruct((8, 128), F32)


def _take_token():
    prev = _ORDER["token"]
    return [] if prev is None else [prev]


def _pcall(body, *, name, out_shape, grid=None, in_specs=None, out_specs=None, scratch=(), aliases=None,
           prefetch=0):
    single = not isinstance(out_shape, (tuple, list))
    out_shapes = [out_shape] if single else list(out_shape)
    out_specs = [out_specs] if single else list(out_specs)
    extra = _take_token()
    n_in, n_extra, n_out = prefetch + len(in_specs), len(extra), len(out_shapes)

    def wrapped(*refs):
        token = refs[n_in + n_extra + n_out]
        token[...] = jnp.zeros_like(token)
        return body(*refs[:n_in], *refs[n_in + n_extra:n_in + n_extra + n_out], *refs[n_in + n_extra + n_out + 1:])

    in_specs = list(in_specs) + [pl.BlockSpec(memory_space=pl.ANY)] * n_extra
    if grid is None:
        out_specs.append(pl.BlockSpec(memory_space=pltpu.VMEM))
    else:
        out_specs.append(pl.BlockSpec(TOKEN.shape, lambda *_: (0, 0)))
    kwargs = dict(name=name, out_shape=out_shapes + [TOKEN], input_output_aliases=aliases or {},
                  compiler_params=pltpu.CompilerParams(vmem_limit_bytes=V7X_VMEM_LIMIT_BYTES))
    if prefetch:
        call = pl.pallas_call(wrapped, grid_spec=pltpu.PrefetchScalarGridSpec(
            num_scalar_prefetch=prefetch, grid=grid, in_specs=in_specs, out_specs=out_specs,
            scratch_shapes=list(scratch)), **kwargs)
    else:
        if grid is not None:
            kwargs["grid"] = grid
        call = pl.pallas_call(wrapped, in_specs=in_specs, out_specs=out_specs, scratch_shapes=list(scratch), **kwargs)

    def run(*args):
        outs = call(*args, *extra)
        _ORDER["token"] = outs[-1]
        return outs[0] if single else tuple(outs[:-1])

    return run


def _mm(a, b, *, name, ta=False, tb=False, out_dtype=F32, caps=(1024, 1024, 3072), stack=None):
    kdim, m = a.shape if ta else a.shape[::-1]
    n, k2 = b.shape if tb else b.shape[::-1]
    assert kdim == k2, (a.shape, b.shape, ta, tb)
    tm, tn, tk = _tile(m, caps[0]), _tile(n, caps[1]), _tile(kdim, caps[2])
    nk = kdim // tk
    dims = (((0 if ta else 1,), (1 if tb else 0,)), ((), ()))

    def body(*refs):
        a_ref, b_ref = refs[0], refs[1]
        part = _dot(a_ref[...].astype(BF16), b_ref[...].astype(BF16), dims)
        if nk == 1:
            o_ref = refs[-1]
            o_ref[...] = part.astype(o_ref.dtype)
            return
        o_ref, acc_ref = refs[-2], refs[-1]
        k = pl.program_id(2)

        @pl.when(k == 0)
        def _():
            acc_ref[...] = part

        @pl.when(k > 0)
        def _():
            acc_ref[...] += part

        @pl.when(k == nk - 1)
        def _():
            o_ref[...] = acc_ref[...].astype(o_ref.dtype)

    a_spec = (pl.BlockSpec((tk, tm), lambda i, j, k: (k, i)) if ta
              else pl.BlockSpec((tm, tk), lambda i, j, k: (i, k)))
    b_spec = (pl.BlockSpec((tn, tk), lambda i, j, k: (j, k)) if tb
              else pl.BlockSpec((tk, tn), lambda i, j, k: (k, j)))
    ins, in_specs, aliases = [a, b], [a_spec, b_spec], {}
    if stack is None:
        out_shape = jax.ShapeDtypeStruct((m, n), out_dtype)
        out_spec = pl.BlockSpec((tm, tn), lambda i, j, k: (i, j))
    else:
        layer, n_layers, buf = stack
        out_shape = jax.ShapeDtypeStruct((n_layers, m, n), out_dtype)
        out_spec = pl.BlockSpec((None, tm, tn), lambda i, j, k: (layer, i, j))
        if buf is not None:
            ins.append(buf)
            in_specs.append(pl.BlockSpec(memory_space=pl.ANY))
            aliases = {2: 0}
    scratch = [] if nk == 1 else [pltpu.VMEM((tm, tn), F32)]
    return _pcall(body, name=name, out_shape=out_shape, grid=(m // tm, n // tn, nk), in_specs=in_specs,
                  out_specs=out_spec, scratch=scratch, aliases=aliases)(*ins)


EPILOGUE_ROWS = 256


def _row_chunks(tm):
    return [slice(r, r + EPILOGUE_ROWS) for r in range(0, tm, EPILOGUE_ROWS)] if tm > EPILOGUE_ROWS else [slice(0, tm)]


def _mm_cat_k(a_lo, a_hi, b, *, name):
    m, f = a_lo.shape
    n = b.shape[0]
    tm, tn, tk = _tile(m, 1024), _tile(n, 1024), _tile(f, 3072)
    half = f // tk
    nk = 2 * half

    def body(lo_ref, hi_ref, b_ref, o_ref, acc_ref):
        k = pl.program_id(2)

        def accumulate(a_ref):
            part = _dot(a_ref[...], b_ref[...], NT)

            @pl.when(k == 0)
            def _():
                acc_ref[...] = part

            @pl.when(k > 0)
            def _():
                acc_ref[...] += part

        pl.when(k < half)(lambda: accumulate(lo_ref))
        pl.when(k >= half)(lambda: accumulate(hi_ref))

        @pl.when(k == nk - 1)
        def _():
            o_ref[...] = acc_ref[...]

    return _pcall(body, name=name, out_shape=jax.ShapeDtypeStruct((m, n), F32), grid=(m // tm, n // tn, nk),
                  in_specs=[pl.BlockSpec((tm, tk), lambda i, j, k: (i, jnp.minimum(k, half - 1))),
                            pl.BlockSpec((tm, tk), lambda i, j, k: (i, jnp.maximum(k - half, 0))),
                            pl.BlockSpec((tn, tk), lambda i, j, k: (j, k))],
                  out_specs=pl.BlockSpec((tm, tn), lambda i, j, k: (i, j)),
                  scratch=[pltpu.VMEM((tm, tn), F32)])(a_lo, a_hi, b)


def _mm_cat_n(a, b_lo, b_hi, *, name):
    s, m = a.shape
    f = b_lo.shape[1]
    tm, tn = _tile(m, 2048), _tile(f, 1024)
    half = f // tn

    def body(a_ref, lo_ref, hi_ref, o_ref):
        j = pl.program_id(1)

        @pl.when(j < half)
        def _():
            o_ref[...] = _dot(a_ref[...], lo_ref[...], TN).astype(BF16)

        @pl.when(j >= half)
        def _():
            o_ref[...] = _dot(a_ref[...], hi_ref[...], TN).astype(BF16)

    return _pcall(body, name=name, out_shape=jax.ShapeDtypeStruct((m, 2 * f), BF16), grid=(m // tm, 2 * half),
                  in_specs=[pl.BlockSpec((s, tm), lambda i, j: (0, i)),
                            pl.BlockSpec((s, tn), lambda i, j: (0, jnp.minimum(j, half - 1))),
                            pl.BlockSpec((s, tn), lambda i, j: (0, jnp.maximum(j - half, 0)))],
                  out_specs=pl.BlockSpec((tm, tn), lambda i, j: (i, j)))(a, b_lo, b_hi)


def _mm_resid_norm(a, w, h, gate, norm, *, name):
    s, kdim = a.shape
    d = w.shape[1]
    tk = _tile(kdim, 2048)
    nk = kdim // tk
    tm = _tile(s, 256 if nk == 1 else 512)

    def body(*refs):
        a_ref, w_ref, h_ref, gate_ref = refs[:4]
        outs = refs[7:] if norm is not None else refs[4:]

        def finish(rows, t):
            hn = h_ref[rows, :] + gate_ref[...] * t
            outs[0][rows, :] = hn
            outs[1][rows, :] = t.astype(BF16)
            if norm is not None:
                g_ref, sc_ref, sh_ref = refs[4:7]
                r = lax.rsqrt(jnp.mean(hn * hn, axis=-1, keepdims=True) + EPS)
                outs[2][rows, :] = (((hn * r) * g_ref[...]) * (1.0 + sc_ref[...]) + sh_ref[...]).astype(BF16)

        if nk == 1:
            for rows in _row_chunks(tm):
                finish(rows, _dot(a_ref[rows, :], w_ref[...]))
            return
        acc_ref = refs[-1]
        k = pl.program_id(1)

        @pl.when(k == 0)
        def _():
            acc_ref[...] = _dot(a_ref[...], w_ref[...])

        @pl.when(jnp.logical_and(k > 0, k < nk - 1))
        def _():
            acc_ref[...] += _dot(a_ref[...], w_ref[...])

        @pl.when(k == nk - 1)
        def _():
            for rows in _row_chunks(tm):
                finish(rows, acc_ref[rows, :] + _dot(a_ref[rows, :], w_ref[...]))

    row = pl.BlockSpec((tm, d), lambda i, k: (i, 0))
    vec = pl.BlockSpec((1, d), lambda i, k: (0, 0))
    in_specs = [pl.BlockSpec((tm, tk), lambda i, k: (i, k)), pl.BlockSpec((tk, d), lambda i, k: (k, 0)), row, vec]
    args = [a, w, h, gate]
    out_shape = [jax.ShapeDtypeStruct((s, d), F32), jax.ShapeDtypeStruct((s, d), BF16)]
    if norm is not None:
        in_specs += [vec, vec, vec]
        args += list(norm)
        out_shape.append(jax.ShapeDtypeStruct((s, d), BF16))
    outs = _pcall(body, name=name, out_shape=tuple(out_shape), grid=(s // tm, nk), in_specs=in_specs,
                  out_specs=(row,) * len(out_shape), scratch=[] if nk == 1 else [pltpu.VMEM((tm, d), F32)])(*args)
    return outs if norm is not None else (*outs, None)


def _mm_merge(o_a, o_b, w_a, w_b, proj, *, name):
    s = o_a.shape[0]
    d = w_a.shape[1]
    tm = _tile(s, 512)
    ga_blk = OFF_GATES // d

    def body(oa_ref, ob_ref, wa_ref, wb_ref, ga_ref, gb_ref, m_ref, ya_ref, yb_ref):
        for rows in _row_chunks(tm):
            ya, yb = _dot(oa_ref[rows, :], wa_ref[...]), _dot(ob_ref[rows, :], wb_ref[...])
            m_ref[rows, :] = (jax.nn.sigmoid(ga_ref[rows, :]) * ya
                              + jax.nn.sigmoid(gb_ref[rows, :]) * yb).astype(BF16)
            ya_ref[rows, :] = ya.astype(BF16)
            yb_ref[rows, :] = yb.astype(BF16)

    row = pl.BlockSpec((tm, d), lambda i: (i, 0))
    act = pl.BlockSpec((tm, o_a.shape[1]), lambda i: (i, 0))
    wspec = pl.BlockSpec(w_a.shape, lambda i: (0, 0))
    shp = jax.ShapeDtypeStruct((s, d), BF16)
    return _pcall(body, name=name, out_shape=(shp, shp, shp), grid=(s // tm,),
                  in_specs=[act, act, wspec, wspec, pl.BlockSpec((tm, d), lambda i: (i, ga_blk)),
                            pl.BlockSpec((tm, d), lambda i: (i, ga_blk + 1))],
                  out_specs=(row, row, row))(o_a, o_b, w_a, w_b, proj, proj)


def _mm_out_t_merge(dt, w_out, proj, y_a, y_b, *, name):
    s, d = dt.shape
    tm, tn = _tile(s, 1024), _tile(d, 512)
    ga_blk = OFF_GATES // tn

    def body(dt_ref, w_ref, ga_ref, gb_ref, ya_ref, yb_ref, dya_ref, dyb_ref, dga_ref, dgb_ref):
        w = w_ref[...]
        for rows in _row_chunks(tm):
            dm = _dot(dt_ref[rows, :], w, NT)
            sa, sb = jax.nn.sigmoid(ga_ref[rows, :]), jax.nn.sigmoid(gb_ref[rows, :])
            dya_ref[rows, :] = (dm * sa).astype(BF16)
            dyb_ref[rows, :] = (dm * sb).astype(BF16)
            dga_ref[rows, :] = (dm * ya_ref[rows, :] * (sa * (1.0 - sa))).astype(BF16)
            dgb_ref[rows, :] = (dm * yb_ref[rows, :] * (sb * (1.0 - sb))).astype(BF16)

    tile = pl.BlockSpec((tm, tn), lambda i, j: (i, j))
    shp = jax.ShapeDtypeStruct((s, d), BF16)
    return _pcall(body, name=name, out_shape=(shp,) * 4, grid=(s // tm, d // tn),
                  in_specs=[pl.BlockSpec((tm, d), lambda i, j: (i, 0)), pl.BlockSpec((tn, d), lambda i, j: (j, 0)),
                            pl.BlockSpec((tm, tn), lambda i, j: (i, ga_blk + j)),
                            pl.BlockSpec((tm, tn), lambda i, j: (i, ga_blk + d // tn + j)), tile, tile],
                  out_specs=(tile,) * 4)(dt, w_out, proj, proj, y_a, y_b)


def _mm_down_t_swiglu(df, w_down, g, u, *, name):
    s, d = df.shape
    f = w_down.shape[0]
    tm, tn = _tile(s, 1024), _tile(f, 512)

    def body(df_ref, w_ref, g_ref, u_ref, dg_ref, du_ref):
        w = w_ref[...]
        for rows in _row_chunks(tm):
            da = _dot(df_ref[rows, :], w, NT)
            gf = g_ref[rows, :].astype(F32)
            sg = jax.nn.sigmoid(gf)
            dg_ref[rows, :] = (da * u_ref[rows, :].astype(F32) * (sg * (1.0 + gf * (1.0 - sg)))).astype(BF16)
            du_ref[rows, :] = (da * (gf * sg)).astype(BF16)

    tile = pl.BlockSpec((tm, tn), lambda i, j: (i, j))
    shp = jax.ShapeDtypeStruct((s, f), BF16)
    return _pcall(body, name=name, out_shape=(shp, shp), grid=(s // tm, f // tn),
                  in_specs=[pl.BlockSpec((tm, d), lambda i, j: (i, 0)), pl.BlockSpec((tn, d), lambda i, j: (j, 0)),
                            tile, tile],
                  out_specs=(tile, tile))(df, w_down, g, u)


def _rmsmod_fwd(h, g, scale, shift, *, name):
    s, d = h.shape
    ts = _rows(s, d)

    def body(h_ref, g_ref, sc_ref, sh_ref, u_ref):
        hf = h_ref[...]
        r = lax.rsqrt(jnp.mean(hf * hf, axis=-1, keepdims=True) + EPS)
        u_ref[...] = (((hf * r) * g_ref[...]) * (1.0 + sc_ref[...]) + sh_ref[...]).astype(BF16)

    row = pl.BlockSpec((ts, d), lambda i: (i, 0))
    vec = pl.BlockSpec((1, d), lambda i: (0, 0))
    return _pcall(body, name=name, out_shape=jax.ShapeDtypeStruct((s, d), BF16), grid=(s // ts,),
                  in_specs=[row, vec, vec, vec], out_specs=row)(h, g, scale, shift)


def _gate_bwd(dhf, t_ref, gate_ref, dt_ref, dgate_ref):
    dt_ref[...] = (dhf * gate_ref[...]).astype(BF16)
    dgate_ref[...] += jnp.sum(dhf * t_ref[...], axis=0, keepdims=True)


def _rmsmod_bwd(du, h, g, scale, dres, t, gate, *, name):
    s, d = h.shape
    ts = _rows(s, d)
    chain = t is not None

    def body(*refs):
        du_ref, h_ref, g_ref, sc_ref, dres_ref = refs[:5]
        dh_ref, dsh_ref, dsc_ref, dg_ref = refs[-6:-2] if chain else refs[-4:]
        sums = (dsh_ref, dsc_ref, dg_ref) + ((refs[-1],) if chain else ())

        @pl.when(pl.program_id(0) == 0)
        def _():
            for ref in sums:
                ref[...] = jnp.zeros_like(ref)

        hf, duf, gain = h_ref[...], du_ref[...], g_ref[...]
        r = lax.rsqrt(jnp.mean(hf * hf, axis=-1, keepdims=True) + EPS)
        xh = hf * r
        dn = duf * (1.0 + sc_ref[...])
        dsh_ref[...] += jnp.sum(duf, axis=0, keepdims=True)
        dsc_ref[...] += jnp.sum(duf * (xh * gain), axis=0, keepdims=True)
        dg_ref[...] += jnp.sum(dn * xh, axis=0, keepdims=True)
        dxh = dn * gain
        dh = dres_ref[...] + r * (dxh - xh * jnp.mean(dxh * xh, axis=-1, keepdims=True))
        dh_ref[...] = dh
        if chain:
            _gate_bwd(dh, refs[5], refs[6], refs[-2], refs[-1])

    row = pl.BlockSpec((ts, d), lambda i: (i, 0))
    vec = pl.BlockSpec((1, d), lambda i: (0, 0))
    vshape = jax.ShapeDtypeStruct((1, d), F32)
    out_shape, out_specs = [jax.ShapeDtypeStruct((s, d), F32), vshape, vshape, vshape], [row, vec, vec, vec]
    in_specs, args = [row, row, vec, vec, row], [du, h, g, scale, dres]
    if chain:
        in_specs, args = in_specs + [row, vec], args + [t, gate]
        out_shape, out_specs = out_shape + [jax.ShapeDtypeStruct((s, d), BF16), vshape], out_specs + [row, vec]
    outs = _pcall(body, name=name, out_shape=tuple(out_shape), grid=(s // ts,), in_specs=in_specs,
                  out_specs=tuple(out_specs))(*args)
    return outs if chain else (*outs, None, None)


def _mm_swiglu(u2, w_gate_up, *, name):
    s, d = u2.shape
    f = w_gate_up.shape[1] // 2
    tm, tn = _tile(s, 1024), _tile(f, 512)
    nj = f // tn

    def body(x_ref, wg_ref, wu_ref, a_ref, g_ref, u_ref):
        for rows in _row_chunks(tm):
            x = x_ref[rows, :]
            gf, uf = _dot(x, wg_ref[...]), _dot(x, wu_ref[...])
            a_ref[rows, :] = ((gf * jax.nn.sigmoid(gf)) * uf).astype(BF16)
            g_ref[rows, :] = gf.astype(BF16)
            u_ref[rows, :] = uf.astype(BF16)

    out = pl.BlockSpec((tm, tn), lambda i, j: (i, j))
    shp = jax.ShapeDtypeStruct((s, f), BF16)
    return _pcall(body, name=name, out_shape=(shp, shp, shp), grid=(s // tm, nj),
                  in_specs=[pl.BlockSpec((tm, d), lambda i, j: (i, 0)), pl.BlockSpec((d, tn), lambda i, j: (0, j)),
                            pl.BlockSpec((d, tn), lambda i, j: (0, nj + j))],
                  out_specs=(out, out, out))(u2, w_gate_up, w_gate_up)


def _loss_fwd(y, tgt, t, gate, *, name):
    s, d = y.shape
    ts = _rows(s, d)

    def body(y_ref, tgt_ref, t_ref, gate_ref, l_ref, dy_ref, dt_ref, dgate_ref):
        @pl.when(pl.program_id(0) == 0)
        def _():
            l_ref[...] = jnp.zeros_like(l_ref)
            dgate_ref[...] = jnp.zeros_like(dgate_ref)

        e = y_ref[...] - tgt_ref[...]
        dy = e * (1.0 / d)
        dy_ref[...] = dy
        per_tok = jnp.sum(e * e, axis=1, keepdims=True) * (1.0 / d)
        l_ref[...] += 0.5 * jnp.sum(per_tok, axis=0, keepdims=True)
        _gate_bwd(dy, t_ref, gate_ref, dt_ref, dgate_ref)

    row = pl.BlockSpec((ts, d), lambda i: (i, 0))
    vec = pl.BlockSpec((1, d), lambda i: (0, 0))
    return _pcall(body, name=name,
                  out_shape=(jax.ShapeDtypeStruct((1, 128), F32), jax.ShapeDtypeStruct((s, d), F32),
                             jax.ShapeDtypeStruct((s, d), BF16), jax.ShapeDtypeStruct((1, d), F32)),
                  grid=(s // ts,), in_specs=[row, row, row, vec],
                  out_specs=(pl.BlockSpec((1, 128), lambda i: (0, 0)), row, row, vec))(y, tgt, t, gate)


def _rope_tables(seq):
    inv = jnp.power(ROPE_THETA, -jnp.arange(0, HEAD_DIM, 2, dtype=F32) / HEAD_DIM)
    ang = jnp.arange(seq, dtype=F32)[:, None] * inv[None, :]
    cos, sin = jnp.cos(ang), jnp.sin(ang)
    return jnp.concatenate([cos, cos], axis=1), jnp.concatenate([-sin, sin], axis=1)


def _qkrope_fwd(proj, gains, cos2, sin2, *, name):
    s = proj.shape[0]
    ts = _rows(s, A_W)

    def body(x_ref, g_ref, c_ref, s_ref, o_ref, o32_ref):
        gain, cos, sin = g_ref[...], c_ref[...], s_ref[...]
        for h in range(A_HEADS):
            lanes = slice(h * HEAD_DIM, (h + 1) * HEAD_DIM)
            x = x_ref[:, lanes]
            y = (x * lax.rsqrt(jnp.mean(x * x, axis=-1, keepdims=True) + EPS)) * gain
            out = y * cos + pltpu.roll(y, HEAD_DIM // 2, 1) * sin
            o_ref[:, lanes] = out.astype(BF16)
            o32_ref[:, lanes] = out

    heads = pl.BlockSpec((ts, A_W), lambda i, j: (i, j))
    tab = pl.BlockSpec((ts, HEAD_DIM), lambda i, j: (i, 0))
    gain = pl.BlockSpec((None, 1, HEAD_DIM), lambda i, j: (j, 0, 0))
    return _pcall(body, name=name,
                  out_shape=(jax.ShapeDtypeStruct((s, 2 * A_W), BF16), jax.ShapeDtypeStruct((s, 2 * A_W), F32)),
                  grid=(s // ts, 2), in_specs=[heads, gain, tab, tab], out_specs=(heads, heads))(
                      proj, gains, cos2, sin2)


def _qkrope_bwd(d_groups, proj, gains, which, cos2, sin2, *, name):
    s = proj.shape[0]
    ts = _rows(s, A_W)

    def body(d0_ref, d1_ref, d2_ref, x_ref, g_ref, c_ref, s_ref, dx_ref, dg_ref):
        @pl.when(pl.program_id(0) == 0)
        def _():
            dg_ref[...] = jnp.zeros_like(dg_ref)

        gain, cos, sin = g_ref[...], c_ref[...], s_ref[...]
        dg = jnp.zeros((1, HEAD_DIM), F32)
        for h in range(A_HEADS):
            lanes = slice(h * HEAD_DIM, (h + 1) * HEAD_DIM)
            slot = slice((h % HEADS_PER_GROUP) * HEAD_DIM, (h % HEADS_PER_GROUP + 1) * HEAD_DIM)
            dout = (d0_ref, d1_ref, d2_ref)[h // HEADS_PER_GROUP][:, slot]
            dy = dout * cos + pltpu.roll(dout * sin, HEAD_DIM // 2, 1)
            x = x_ref[:, lanes]
            r = lax.rsqrt(jnp.mean(x * x, axis=-1, keepdims=True) + EPS)
            xh = x * r
            dg = dg + jnp.sum(dy * xh, axis=0, keepdims=True)
            dxh = dy * gain
            dx_ref[:, lanes] = (r * (dxh - xh * jnp.mean(dxh * xh, axis=-1, keepdims=True))).astype(BF16)
        dg_ref[...] += dg

    group = pl.BlockSpec((ts, GROUP_W), lambda i: (i, 0))
    tab = pl.BlockSpec((ts, HEAD_DIM), lambda i: (i, 0))
    gain = pl.BlockSpec((None, 1, HEAD_DIM), lambda i: (which, 0, 0))
    return _pcall(body, name=name,
                  out_shape=(jax.ShapeDtypeStruct((s, A_W), BF16), jax.ShapeDtypeStruct((1, HEAD_DIM), F32)),
                  grid=(s // ts,),
                  in_specs=[group, group, group, pl.BlockSpec((ts, A_W), lambda i: (i, which)), gain, tab, tab],
                  out_specs=(pl.BlockSpec((ts, A_W), lambda i: (i, 0)), pl.BlockSpec((1, HEAD_DIM), lambda i: (0, 0))))(
                      *d_groups, proj, gains, cos2, sin2)


def _assemble(pieces, *, name):
    s = pieces[0].shape[0]
    widths = [p.shape[1] for p in pieces]
    total = sum(widths)
    ts = _rows(s, total // 2)

    def body(*refs):
        o_ref, off = refs[-1], 0
        for x_ref, w in zip(refs[:-1], widths):
            o_ref[:, off:off + w] = x_ref[...].astype(BF16)
            off += w

    return _pcall(body, name=name, out_shape=jax.ShapeDtypeStruct((s, total), BF16), grid=(s // ts,),
                  in_specs=[pl.BlockSpec((ts, w), lambda i: (i, 0)) for w in widths],
                  out_specs=pl.BlockSpec((ts, total), lambda i: (i, 0)))(*pieces)


def _block_rows(blk):
    if isinstance(blk, int):
        return pl.ds(blk * BLOCK, BLOCK)
    return pl.ds(pl.multiple_of(blk * BLOCK, BLOCK), BLOCK)


def _band_window(n, length):
    width = min(2 * BLOCK, length)
    row = lax.broadcasted_iota(jnp.int32, (BLOCK, width), 0)
    col = lax.broadcasted_iota(jnp.int32, (BLOCK, width), 1)
    if width == BLOCK:
        return pl.ds(0, BLOCK), col <= row
    first = n - 1 if isinstance(n, int) else jnp.maximum(n - 1, 0)
    first = max(first, 0) if isinstance(first, int) else first
    dist = row - col + (n - first) * BLOCK
    start = first * BLOCK if isinstance(first, int) else pl.multiple_of(first * BLOCK, BLOCK)
    return pl.ds(start, width), jnp.logical_and(dist >= 0, dist <= BLOCK)


def _dil_fwd(q_arr, k_arr, v_arr, offs, length, dil, *, name):
    nj, nb = dil * HEADS_PER_GROUP, length // BLOCK
    ju, nq = (HEADS_PER_GROUP, 2) if nb > 1 else (2 * HEADS_PER_GROUP, 1)
    qo, ko, vo = (off // ju for off in offs)
    assert all(off % ju == 0 for off in offs) and nb % nq == 0 and nj % ju == 0

    def body(q_ref, k_ref, v_ref, o_ref, l_ref):
        for qq in range(nq):
            qrows = slice(qq * BLOCK, (qq + 1) * BLOCK)
            rows, mask = _band_window(pl.program_id(1) * nq + qq, length)
            for cb in range(ju):
                lanes = slice(cb * HEAD_DIM, (cb + 1) * HEAD_DIM)
                sc = _dot(q_ref[qrows, lanes].astype(BF16), k_ref[rows, lanes].astype(BF16), NT) * ATT_SCALE
                sc = jnp.where(mask, sc, MASKED)
                m = sc.max(axis=-1, keepdims=True)
                p = jnp.exp(sc - m)
                den = jnp.sum(p, axis=-1, keepdims=True)
                acc = _dot(p.astype(BF16), v_ref[rows, lanes].astype(BF16))
                o_ref[qrows, lanes] = acc / den
                l_ref[qrows, lanes] = jnp.broadcast_to(m + jnp.log(den), (BLOCK, HEAD_DIM))

    qspec = pl.BlockSpec((nq * BLOCK, ju * HEAD_DIM), lambda j, n: (n, qo + j))
    kspec = pl.BlockSpec((length, ju * HEAD_DIM), lambda j, n: (0, ko + j))
    vspec = pl.BlockSpec((length, ju * HEAD_DIM), lambda j, n: (0, vo + j))
    ospec = pl.BlockSpec((nq * BLOCK, ju * HEAD_DIM), lambda j, n: (n, j))
    shp = jax.ShapeDtypeStruct((length, nj * HEAD_DIM), F32)
    return _pcall(body, name=name, out_shape=(shp, shp), grid=(nj // ju, nb // nq), in_specs=[qspec, kspec, vspec],
                  out_specs=(ospec, ospec))(q_arr, k_arr, v_arr)


def _dil_bwd(q_arr, k_arr, v_arr, offs, o, lse, do, dlse, length, dil, *, name):
    nj, nb = dil * HEADS_PER_GROUP, length // BLOCK
    ju = 2 * HEADS_PER_GROUP if length <= 4 * BLOCK else 2
    qo, ko, vo = (off // ju for off in offs)
    assert all(off % ju == 0 for off in offs)

    def body(q_ref, k_ref, v_ref, o_ref, l_ref, do_ref, dl_ref, dq_ref, dk_ref, dv_ref):
        dk_ref[...] = jnp.zeros_like(dk_ref)
        dv_ref[...] = jnp.zeros_like(dv_ref)

        def step(n, carry):
            qrows = _block_rows(n)
            rows, mask = _band_window(n, length)
            for cb in range(ju):
                lanes = slice(cb * HEAD_DIM, (cb + 1) * HEAD_DIM)
                q = q_ref[qrows, lanes].astype(BF16)
                dof = do_ref[qrows, lanes]
                dob = dof.astype(BF16)
                lse_c = l_ref[qrows, lanes][:, :1]
                shift = dl_ref[qrows, lanes][:, :1] - jnp.sum(dof * o_ref[qrows, lanes], axis=-1, keepdims=True)
                kk, vv = k_ref[rows, lanes].astype(BF16), v_ref[rows, lanes].astype(BF16)
                sc = _dot(q, kk, NT) * ATT_SCALE
                p = jnp.where(mask, jnp.exp(sc - lse_c), 0.0)
                ds = (p * (_dot(dob, vv, NT) + shift)).astype(BF16)
                dq_ref[qrows, lanes] = _dot(ds, kk) * ATT_SCALE
                dk_ref[rows, lanes] += _dot(ds, q, TN) * ATT_SCALE
                dv_ref[rows, lanes] += _dot(p.astype(BF16), dob, TN)
            return carry

        if nb == 1:
            step(0, 0)
        else:
            lax.fori_loop(0, nb, step, 0)

    def col(off):
        return pl.BlockSpec((length, ju * HEAD_DIM), lambda j: (0, off + j))

    shp = jax.ShapeDtypeStruct((length, nj * HEAD_DIM), F32)
    return _pcall(body, name=name, out_shape=(shp, shp, shp), grid=(nj // ju,),
                  in_specs=[col(qo), col(ko), col(vo), col(0), col(0), col(0), col(0)],
                  out_specs=(col(0), col(0), col(0)))(q_arr, k_arr, v_arr, o, lse, do, dlse)


DIL_RESIDUES_PER_STEP = 8


def _dil_tokens(n, r, dil, length):
    width = min(2 * BLOCK, length)
    _, mask = _band_window(n, length)
    first = 0 if width == BLOCK else jnp.maximum(n - 1, 0)
    return (pl.ds(n * (BLOCK * dil) + r, BLOCK, stride=dil), pl.ds(first * (BLOCK * dil) + r, width, stride=dil),
            mask)


def _dil_head_specs(seq, group):
    first = group * HEADS_PER_GROUP

    def col(c0):
        return pl.BlockSpec((seq, HEAD_DIM), lambda h, r: (0, c0 + h))

    return col(first), col(A_HEADS + first), col(OFF_VA // HEAD_DIM + first), col(0)


def _dil_fwd_strided(qk32, proj, group, dil, *, name):
    seq = proj.shape[0]
    length = seq // dil
    nb, rp = length // BLOCK, min(dil, DIL_RESIDUES_PER_STEP)
    assert dil % rp == 0

    def body(q_ref, k_ref, v_ref, o_ref, l_ref):
        rgroup = pl.program_id(1)

        def step(n, carry):
            for rr in range(rp):
                tok_q, tok_k, mask = _dil_tokens(n, rgroup * rp + rr, dil, length)
                sc = _dot(q_ref[tok_q, :].astype(BF16), k_ref[tok_k, :].astype(BF16), NT) * ATT_SCALE
                sc = jnp.where(mask, sc, MASKED)
                m = sc.max(axis=-1, keepdims=True)
                p = jnp.exp(sc - m)
                den = jnp.sum(p, axis=-1, keepdims=True)
                o_ref[tok_q, :] = _dot(p.astype(BF16), v_ref[tok_k, :].astype(BF16)) / den
                l_ref[tok_q, :] = jnp.broadcast_to(m + jnp.log(den), (BLOCK, HEAD_DIM))
            return carry

        if nb == 1:
            step(0, 0)
        else:
            lax.fori_loop(0, nb, step, 0)

    qs, ks, vs, nat = _dil_head_specs(seq, group)
    shp = jax.ShapeDtypeStruct((seq, GROUP_W), F32)
    return _pcall(body, name=name, out_shape=(shp, shp), grid=(HEADS_PER_GROUP, dil // rp), in_specs=[qs, ks, vs],
                  out_specs=(nat, nat))(qk32, qk32, proj)


def _dil_bwd_strided(qk32, proj, group, o, lse, do, dlse, dil, *, name):
    seq = proj.shape[0]
    length = seq // dil
    nb, rp = length // BLOCK, min(dil, DIL_RESIDUES_PER_STEP)
    assert dil % rp == 0

    def body(q_ref, k_ref, v_ref, o_ref, l_ref, do_ref, dl_ref, dq_ref, dk_ref, dv_ref):
        rgroup = pl.program_id(1)

        @pl.when(rgroup == 0)
        def _():
            dk_ref[...] = jnp.zeros_like(dk_ref)
            dv_ref[...] = jnp.zeros_like(dv_ref)

        def step(n, carry):
            for rr in range(rp):
                tok_q, tok_k, mask = _dil_tokens(n, rgroup * rp + rr, dil, length)
                q = q_ref[tok_q, :].astype(BF16)
                dof = do_ref[tok_q, :]
                dob = dof.astype(BF16)
                lse_c = l_ref[tok_q, :][:, :1]
                shift = dl_ref[tok_q, :][:, :1] - jnp.sum(dof * o_ref[tok_q, :], axis=-1, keepdims=True)
                kk, vv = k_ref[tok_k, :].astype(BF16), v_ref[tok_k, :].astype(BF16)
                sc = _dot(q, kk, NT) * ATT_SCALE
                p = jnp.where(mask, jnp.exp(sc - lse_c), 0.0)
                ds = (p * (_dot(dob, vv, NT) + shift)).astype(BF16)
                dq_ref[tok_q, :] = _dot(ds, kk) * ATT_SCALE
                dk_ref[tok_k, :] += _dot(ds, q, TN) * ATT_SCALE
                dv_ref[tok_k, :] += _dot(p.astype(BF16), dob, TN)
            return carry

        if nb == 1:
            step(0, 0)
        else:
            lax.fori_loop(0, nb, step, 0)

    qs, ks, vs, nat = _dil_head_specs(seq, group)
    shp = jax.ShapeDtypeStruct((seq, GROUP_W), F32)
    return _pcall(body, name=name, out_shape=(shp, shp, shp), grid=(HEADS_PER_GROUP, dil // rp),
                  in_specs=[qs, ks, vs, nat, nat, nat, nat], out_specs=(nat, nat, nat))(
                      qk32, qk32, proj, o, lse, do, dlse)


def _combine_weights(l_refs):
    ls = [r[...] for r in l_refs]
    m = jnp.maximum(jnp.maximum(ls[0], ls[1]), ls[2])
    es = [jnp.exp(l - m) for l in ls]
    den = es[0] + es[1] + es[2]
    return [e / den for e in es]


def _combine_fwd(os_, lses, *, name):
    s = os_[0].shape[0]
    ts = _rows(s, GROUP_W)

    def body(o0, o1, o2, l0, l1, l2, out_ref):
        w = _combine_weights((l0, l1, l2))
        out_ref[...] = (w[0] * o0[...] + w[1] * o1[...] + w[2] * o2[...]).astype(BF16)

    row = pl.BlockSpec((ts, GROUP_W), lambda i: (i, 0))
    return _pcall(body, name=name, out_shape=jax.ShapeDtypeStruct((s, GROUP_W), BF16), grid=(s // ts,),
                  in_specs=[row] * 6, out_specs=row)(*os_, *lses)


def _combine_bwd(do_a, os_, lses, *, name):
    s = do_a.shape[0]
    ts = _rows(s, GROUP_W)

    def body(d_ref, o0, o1, o2, l0, l1, l2, do0, do1, do2, dl0, dl1, dl2):
        w = _combine_weights((l0, l1, l2))
        d = d_ref[...]
        og = [o0[...], o1[...], o2[...]]
        oa = w[0] * og[0] + w[1] * og[1] + w[2] * og[2]
        ta = jnp.sum(d * oa, axis=-1, keepdims=True)
        for g, (do_ref, dl_ref) in enumerate(((do0, dl0), (do1, dl1), (do2, dl2))):
            do_ref[...] = w[g] * d
            dl_ref[...] = w[g] * (jnp.sum(d * og[g], axis=-1, keepdims=True) - ta)

    head = pl.BlockSpec((ts, HEAD_DIM), lambda i, h: (i, h))
    shp = jax.ShapeDtypeStruct((s, GROUP_W), F32)
    return _pcall(body, name=name, out_shape=(shp,) * 6, grid=(s // ts, HEADS_PER_GROUP),
                  in_specs=[head] * 7, out_specs=(head,) * 6)(do_a, *os_, *lses)


def _dot_exact(x, ones_mask):
    hi = x.astype(BF16)
    r1 = x - hi.astype(F32)
    mid = r1.astype(BF16)
    lo = (r1 - mid.astype(F32)).astype(BF16)
    return _dot(hi, ones_mask) + _dot(mid, ones_mask) + _dot(lo, ones_mask)


SB_QROWS = 4 * BLOCK
SB_UNROLL = 4
SB_HEADS_PER_STEP = 2
SB_LANES = [slice(hh * HEAD_DIM, (hh + 1) * HEAD_DIM) for hh in range(SB_HEADS_PER_STEP)]


def _sb_mask(j, i):
    row = lax.broadcasted_iota(jnp.int32, (SB_QROWS, BLOCK), 0)
    col = lax.broadcasted_iota(jnp.int32, (SB_QROWS, BLOCK), 1)
    return col + (j * BLOCK - i * SB_QROWS) < row


def _sb_steps(i):
    return ((i + 1) * (SB_QROWS // BLOCK) + SB_UNROLL - 1) // SB_UNROLL


def _sb_scores(q, kk, j, i, masked):
    mask = _sb_mask(j, i) if masked else None
    z = _dot(q, kk, NT) * ATT_SCALE
    sp = jnp.log(1.0 + jnp.exp(-jnp.abs(z)))
    log_beta = jnp.minimum(z, 0.0) - sp
    log_1mb = jnp.minimum(-z, 0.0) - sp
    if masked:
        log_1mb = jnp.where(mask, log_1mb, 0.0)
    return z, log_beta, log_1mb, mask


def _sb_weights(log_beta, log_1mb, mask, run, upper):
    a = jnp.exp(log_beta + (run + _dot_exact(log_1mb, upper)))
    return a if mask is None else jnp.where(mask, a, 0.0)


def _sb_peeled(nsteps, make_step, init, masked_first):
    if masked_first:
        return lax.fori_loop(1, nsteps, make_step(False), make_step(True)(0, init))
    return make_step(True)(nsteps - 1, lax.fori_loop(0, nsteps - 1, make_step(False), init))


def _tri(strict_lower):
    row = lax.broadcasted_iota(jnp.int32, (BLOCK, BLOCK), 0)
    col = lax.broadcasted_iota(jnp.int32, (BLOCK, BLOCK), 1)
    return ((row > col) if strict_lower else (row < col)).astype(BF16)


def _sb_fwd(proj, *, name):
    s = proj.shape[0]
    assert s % (BLOCK * SB_UNROLL) == 0 and s % SB_QROWS == 0

    def body(q_ref, k_ref, v_ref, o_ref):
        i = pl.program_id(1)
        qs = [q_ref[:, lanes].astype(BF16) for lanes in SB_LANES]
        upper = _tri(True)
        nsteps = _sb_steps(i)

        def make_step(masked):
            def step(t, carry):
                carry = list(carry)
                for b in reversed(range(SB_UNROLL)):
                    j = (nsteps - 1 - t) * SB_UNROLL + b
                    rows = _block_rows(j)
                    for hh, lanes in enumerate(SB_LANES):
                        acc, run = carry[hh]
                        _, log_beta, log_1mb, mask = _sb_scores(qs[hh], k_ref[rows, lanes].astype(BF16), j, i, masked)
                        a = _sb_weights(log_beta, log_1mb, mask, run, upper)
                        carry[hh] = (acc + _dot(a.astype(BF16), v_ref[rows, lanes].astype(BF16)),
                                     run + jnp.sum(log_1mb, axis=-1, keepdims=True))
                return tuple(carry)
            return step

        zero = (jnp.zeros((SB_QROWS, HEAD_DIM), F32), jnp.zeros((SB_QROWS, 1), F32))
        for lanes, (acc, _) in zip(SB_LANES, _sb_peeled(nsteps, make_step, (zero,) * SB_HEADS_PER_STEP, True)):
            o_ref[:, lanes] = acc.astype(BF16)

    width = SB_HEADS_PER_STEP * HEAD_DIM
    qb, kb, vb = (off // width for off in (OFF_QB, OFF_KB, OFF_VB))
    return _pcall(body, name=name, out_shape=jax.ShapeDtypeStruct((s, B_W), BF16),
                  grid=(SB_HEADS // SB_HEADS_PER_STEP, s // SB_QROWS),
                  in_specs=[pl.BlockSpec((SB_QROWS, width), lambda h, i: (i, qb + h)),
                            pl.BlockSpec((s, width), lambda h, i: (0, kb + h)),
                            pl.BlockSpec((s, width), lambda h, i: (0, vb + h))],
                  out_specs=pl.BlockSpec((SB_QROWS, width), lambda h, i: (i, h)))(proj, proj, proj)


def _sb_bwd(proj, do_b, *, name):
    s = proj.shape[0]
    assert s % (BLOCK * SB_UNROLL) == 0 and s % SB_QROWS == 0
    nkb = s // BLOCK

    def body(q_ref, k_ref, v_ref, do_ref, dq_ref, dk_ref, dv_ref, z_s, a_s):
        i = pl.program_id(1)

        @pl.when(i == 0)
        def _():
            dk_ref[...] = jnp.zeros_like(dk_ref)
            dv_ref[...] = jnp.zeros_like(dv_ref)

        qs = [q_ref[:, lanes].astype(BF16) for lanes in SB_LANES]
        dobs = [do_ref[:, lanes].astype(BF16) for lanes in SB_LANES]
        upper, lower = _tri(True), _tri(False)
        nsteps = _sb_steps(i)

        def make_recompute(masked):
            def recompute(t, runs):
                runs = list(runs)
                for b in reversed(range(SB_UNROLL)):
                    j = (nsteps - 1 - t) * SB_UNROLL + b
                    rows = _block_rows(j)
                    for hh, lanes in enumerate(SB_LANES):
                        z, log_beta, log_1mb, mask = _sb_scores(qs[hh], k_ref[rows, lanes].astype(BF16), j, i, masked)
                        z_s[hh, j] = z
                        a_s[hh, j] = _sb_weights(log_beta, log_1mb, mask, runs[hh], upper)
                        runs[hh] = runs[hh] + jnp.sum(log_1mb, axis=-1, keepdims=True)
                return tuple(runs)
            return recompute

        _sb_peeled(nsteps, make_recompute, (jnp.zeros((SB_QROWS, 1), F32),) * SB_HEADS_PER_STEP, True)

        def make_grads(masked):
            def grads(t, carry):
                carry = list(carry)
                for b in range(SB_UNROLL):
                    j = t * SB_UNROLL + b
                    rows = _block_rows(j)
                    for hh, lanes in enumerate(SB_LANES):
                        dq, run = carry[hh]
                        kk, vv = k_ref[rows, lanes].astype(BF16), v_ref[rows, lanes].astype(BF16)
                        z, a = z_s[hh, j], a_s[hh, j]
                        de = _dot(dobs[hh], vv, NT) * a
                        beta = jax.nn.sigmoid(z)
                        one_minus_beta = 1.0 - beta
                        if masked:
                            beta = jnp.where(_sb_mask(j, i), beta, 0.0)
                        dz = (de * one_minus_beta - beta * (run + _dot_exact(de, lower))).astype(BF16)
                        dk_ref[rows, lanes] += _dot(dz, qs[hh], TN) * ATT_SCALE
                        dv_ref[rows, lanes] += _dot(a.astype(BF16), dobs[hh], TN)
                        carry[hh] = (dq + _dot(dz, kk), run + jnp.sum(de, axis=-1, keepdims=True))
                return tuple(carry)
            return grads

        zero = (jnp.zeros((SB_QROWS, HEAD_DIM), F32), jnp.zeros((SB_QROWS, 1), F32))
        for lanes, (dq, _) in zip(SB_LANES, _sb_peeled(nsteps, make_grads, (zero,) * SB_HEADS_PER_STEP, False)):
            dq_ref[:, lanes] = dq * ATT_SCALE

    width = SB_HEADS_PER_STEP * HEAD_DIM
    qb, kb, vb = (off // width for off in (OFF_QB, OFF_KB, OFF_VB))
    blk = pl.BlockSpec((SB_QROWS, width), lambda h, i: (i, h))
    full = pl.BlockSpec((s, width), lambda h, i: (0, h))
    shp = jax.ShapeDtypeStruct((s, B_W), F32)
    saved = pltpu.VMEM((SB_HEADS_PER_STEP, nkb, SB_QROWS, BLOCK), F32)
    return _pcall(body, name=name, out_shape=(shp, shp, shp), grid=(SB_HEADS // SB_HEADS_PER_STEP, s // SB_QROWS),
                  in_specs=[pl.BlockSpec((SB_QROWS, width), lambda h, i: (i, qb + h)),
                            pl.BlockSpec((s, width), lambda h, i: (0, kb + h)),
                            pl.BlockSpec((s, width), lambda h, i: (0, vb + h)), blk],
                  out_specs=(blk, full, full), scratch=[saved, saved])(proj, proj, proj, do_b)


def _coords():
    return lax.axis_index("x"), lax.axis_index("y"), lax.axis_index("c")


def _flip(v, bit):
    return 1 - v if bit else v


def _shard_of(ref, axis, idx, size):
    if axis == 0:
        sl = pl.ds(pl.multiple_of(idx * size, 16), size)
        return ref.at[sl, :] if len(ref.shape) == 2 else ref.at[:, sl, :]
    sl = pl.ds(pl.multiple_of(idx * size, 128), size)
    return ref.at[:, sl] if len(ref.shape) == 2 else ref.at[:, :, sl]


def _small_allgather(v, *, name, silu=False):
    n = v.shape[1]

    def body(v_ref, out_ref, send_sems, recv_sems):
        x, y, c = _coords()
        me = 4 * x + 2 * y + c
        val = v_ref[...]
        out_ref[me] = val * jax.nn.sigmoid(val) if silu else val
        copies = []
        for k in range(1, N_DEV):
            peer = (_flip(x, k & 4), _flip(y, k & 2), _flip(c, k & 1))
            copies.append(pltpu.make_async_remote_copy(
                src_ref=out_ref.at[me], dst_ref=out_ref.at[me], send_sem=send_sems.at[k - 1],
                recv_sem=recv_sems.at[k - 1], device_id=peer, device_id_type=MESH))
        for cp in copies:
            cp.start()
        for cp in copies:
            cp.wait_recv()
        for cp in copies:
            cp.wait_send()

    return _pcall(body, name=name, out_shape=jax.ShapeDtypeStruct((N_DEV, 1, n), F32),
                  in_specs=[pl.BlockSpec(memory_space=pltpu.VMEM)], out_specs=pl.BlockSpec(memory_space=pltpu.VMEM),
                  scratch=[pltpu.SemaphoreType.DMA((N_DEV - 1,)), pltpu.SemaphoreType.DMA((N_DEV - 1,))])(v)


def _cast_place(w, layer, axis, me, *, name):
    _, r, c = w.shape
    tr = _rows(r, c)
    nrt = r // tr

    def body(me_ref, w_ref, o_ref):
        o_ref[...] = w_ref[...].astype(BF16)

    wspec = pl.BlockSpec((None, tr, c), lambda i, me_ref: (layer, i, 0))
    if axis == 0:
        ospec = pl.BlockSpec((tr, c), lambda i, me_ref: (me_ref[0] * nrt + i, 0))
        shape = (r * N_DEV, c)
    else:
        ospec = pl.BlockSpec((tr, c), lambda i, me_ref: (i, me_ref[0]))
        shape = (r, c * N_DEV)
    return _pcall(body, name=name, out_shape=jax.ShapeDtypeStruct(shape, BF16), grid=(nrt,), in_specs=[wspec],
                  out_specs=ospec, prefetch=1)(me, w)


def _pair_sum(grad, sib, core, axis, *, name):
    _, r, c = sib.shape
    tr = _rows(r, c // 2)
    nrt = r // tr

    def body(core_ref, g_ref, s_ref, o_ref):
        o_ref[...] = (g_ref[...].astype(F32) + s_ref[...].astype(F32)).astype(BF16)

    if axis == 0:
        gspec = pl.BlockSpec((tr, c), lambda q, i, core_ref: ((2 * q + core_ref[0]) * nrt + i, 0))
    else:
        gspec = pl.BlockSpec((tr, c), lambda q, i, core_ref: (i, 2 * q + core_ref[0]))
    sspec = pl.BlockSpec((None, tr, c), lambda q, i, core_ref: (q, i, 0))
    return _pcall(body, name=name, out_shape=jax.ShapeDtypeStruct(sib.shape, BF16), grid=(N_CHIPS, nrt),
                  in_specs=[gspec, sspec], out_specs=sspec, prefetch=1)(core, grad, sib)


ANY_SPEC = pl.BlockSpec(memory_space=pl.ANY)
SEM_SPEC = pl.BlockSpec(memory_space=pltpu.SEMAPHORE)
SPLIT_PARAMS = dict(has_side_effects=pltpu.SideEffectType.DATAFLOW_SIDE_EFFECTING)


def _split_start(copies_fn, buffers, sem_shape, after, *, name):
    n = len(buffers)
    rows, cols = sem_shape
    ns = rows * cols
    extra = ([] if after is None else [after]) + _take_token()

    def body(*refs):
        sems = refs[n + len(extra):n + len(extra) + 2 * ns]
        for cp in copies_fn(refs[:n], _sem_rows(sems[:ns], cols), _sem_rows(sems[ns:], cols)):
            cp.start()
        refs[-1][...] = jnp.zeros_like(refs[-1])

    sem = pltpu.SemaphoreType.DMA(())
    outs = pl.pallas_call(
        body, name=name,
        out_shape=((sem,) * (2 * ns) + tuple(jax.ShapeDtypeStruct(b.shape, b.dtype) for b in buffers) + (TOKEN,)),
        in_specs=(ANY_SPEC,) * (n + len(extra)),
        out_specs=(SEM_SPEC,) * (2 * ns) + (ANY_SPEC,) * n + (pl.BlockSpec(memory_space=pltpu.VMEM),),
        input_output_aliases={i: 2 * ns + i for i in range(n)},
        compiler_params=pltpu.CompilerParams(**SPLIT_PARAMS))(*buffers, *extra)
    _ORDER["token"] = outs[-1]
    return list(outs[:ns]), list(outs[ns:2 * ns]), list(outs[2 * ns:2 * ns + n]), outs[-1]


def _split_wait(copies_fn, send_sems, recv_sems, buffers, after, sem_rows, *, name):
    n, ns = len(buffers), len(send_sems)
    cols = ns // sem_rows
    extra = ([] if after is None else [after]) + _take_token()

    def body(*refs):
        sems = refs[n:n + 2 * ns]
        copies = copies_fn(refs[:n], _sem_rows(sems[:ns], cols), _sem_rows(sems[ns:], cols))
        for cp in copies:
            cp.wait_send()
        for cp in copies:
            cp.wait_recv()
        refs[-1][...] = jnp.zeros_like(refs[-1])

    outs = pl.pallas_call(
        body, name=name, out_shape=tuple(jax.ShapeDtypeStruct(b.shape, b.dtype) for b in buffers) + (TOKEN,),
        in_specs=(ANY_SPEC,) * n + (SEM_SPEC,) * (2 * ns) + (ANY_SPEC,) * len(extra),
        out_specs=(ANY_SPEC,) * n + (pl.BlockSpec(memory_space=pltpu.VMEM),),
        input_output_aliases={i: i for i in range(n)},
        compiler_params=pltpu.CompilerParams(**SPLIT_PARAMS))(*buffers, *send_sems, *recv_sems, *extra)
    _ORDER["token"] = outs[-1]
    return list(outs[:n])


def _sem_rows(sems, cols):
    return [sems[i:i + cols] for i in range(0, len(sems), cols)]


def _empty_hbm(shape, dtype):
    return pltpu.with_memory_space_constraint(lax.empty(shape, dtype), pltpu.HBM)


class _SplitGather:
    def __init__(self, fulls, axes, tag):
        self.axes, self.tag, self.nt = list(axes), tag, len(fulls)
        self.sizes = [f.shape[ax] // N_DEV for f, ax in zip(fulls, axes)]
        self.fulls = list(fulls)

    def _slot(self, ref, t, dev):
        return _shard_of(ref, self.axes[t], 4 * dev[0] + 2 * dev[1] + dev[2], self.sizes[t])

    def _first_copies(self, refs, send_sems, recv_sems):
        x, y, c = _coords()
        peers = [(x, y, 1 - c), (1 - x, y, c), (x, 1 - y, c), (1 - x, 1 - y, c)]
        return [pltpu.make_async_remote_copy(
            src_ref=self._slot(refs[t], t, (x, y, c)), dst_ref=self._slot(refs[t], t, (x, y, c)),
            send_sem=send_sems[t][k], recv_sem=recv_sems[t][k], device_id=peer, device_id_type=MESH)
            for t in range(self.nt) for k, peer in enumerate(peers)]

    def _forward_copies(self, refs, send_sems, recv_sems):
        x, y, c = _coords()
        chips = [(1 - x, y), (x, 1 - y), (1 - x, 1 - y)]
        return [pltpu.make_async_remote_copy(
            src_ref=self._slot(refs[t], t, (*chip, c)), dst_ref=self._slot(refs[t], t, (*chip, c)),
            send_sem=send_sems[t][j], recv_sem=recv_sems[t][j], device_id=(x, y, 1 - c), device_id_type=MESH)
            for t in range(self.nt) for j, chip in enumerate(chips)]

    def first(self, after):
        self.s1, self.r1, self.fulls, token = _split_start(
            self._first_copies, self.fulls, (self.nt, 4), after, name=f"comm_gather1_start_{self.tag}")
        return token

    def forward(self, after):
        bufs = _split_wait(self._first_copies, self.s1, self.r1, self.fulls, after, self.nt,
                           name=f"comm_gather1_wait_{self.tag}")
        self.s2, self.r2, self.fulls, token = _split_start(
            self._forward_copies, bufs, (self.nt, 3), after, name=f"comm_gather2_start_{self.tag}")
        return token

    def finish(self, after):
        return _split_wait(self._forward_copies, self.s2, self.r2, self.fulls, after, self.nt,
                           name=f"comm_gather2_wait_{self.tag}")


class _SplitPairExchange:
    def __init__(self, grads, axes, tag):
        self.nt, self.tag, self.axes = len(grads), tag, list(axes)
        self.grads = list(grads)
        self.sizes = [g.shape[ax] // N_DEV for g, ax in zip(grads, axes)]

    def _copies(self, refs, send_sems, recv_sems):
        nt = self.nt
        x, y, c = _coords()
        return [pltpu.make_async_remote_copy(
            src_ref=_shard_of(refs[t], self.axes[t], 2 * q + 1 - c, self.sizes[t]), dst_ref=refs[nt + t].at[q],
            send_sem=send_sems[t][q], recv_sem=recv_sems[t][q], device_id=(x, y, 1 - c), device_id_type=MESH)
            for t in range(nt) for q in range(N_CHIPS)]

    def start(self):
        landing = []
        for g, ax in zip(self.grads, self.axes):
            dims = list(g.shape)
            dims[ax] //= N_DEV
            landing.append(_empty_hbm((N_CHIPS, *dims), g.dtype))
        self.s, self.r, self.bufs, token = _split_start(
            self._copies, self.grads + landing, (self.nt, N_CHIPS), None,
            name=f"comm_rs_pair_start_{self.tag}")
        return token

    def finish(self, after):
        bufs = _split_wait(self._copies, self.s, self.r, self.bufs, after, self.nt,
                           name=f"comm_rs_pair_wait_{self.tag}")
        return bufs[:self.nt], bufs[self.nt:]


class _ReducePipeline:
    def __init__(self, core):
        self.core, self.items, self.done, self.now = core, [], [], 0

    def add(self, keys, grads, layer):
        axes = [SHARD_AXIS[k] for k in keys]
        pair = _SplitPairExchange([grads[k] for k in keys], axes, f"{keys[0]}{layer}")
        pair.start()
        self.items.append(dict(keys=keys, layer=layer, axes=axes, pair=pair, state="pair", since=self.now))

    def tick(self, after, flush=False):
        self.now += 1
        for it in self.items:
            if it["state"] == "pair" and it["since"] < self.now:
                grads, sib = it["pair"].finish(after)
                sums = [_pair_sum(g, s_, self.core, ax, name="pair_sum_" + k)
                        for k, g, s_, ax in zip(it["keys"], grads, sib, it["axes"])]
                it["chip"] = _SplitChipExchange(sums, f"{it['keys'][0]}{it['layer']}")
                it["chip"].start()
                it.update(state="chip", since=self.now)
            elif it["state"] == "chip" and (flush or self.now - it["since"] >= 2):
                sums, remote = it["chip"].finish(after)
                self.done.append((it["keys"], it["layer"], sums, remote))
                it["state"] = "done"

    def take_done(self):
        out, self.done = self.done, []
        return out


class _SplitChipExchange:
    def __init__(self, sums, tag):
        self.nt, self.tag = len(sums), tag
        self.sums = list(sums)

    def _copies(self, refs, send_sems, recv_sems):
        nt = self.nt
        x, y, c = _coords()
        copies = []
        for t in range(nt):
            for k in range(1, N_CHIPS):
                px, py = _flip(x, k & 2), _flip(y, k & 1)
                copies.append(pltpu.make_async_remote_copy(
                    src_ref=refs[t].at[2 * px + py], dst_ref=refs[nt + t].at[k - 1], send_sem=send_sems[t][k - 1],
                    recv_sem=recv_sems[t][k - 1], device_id=(px, py, c), device_id_type=MESH))
        return copies

    def start(self):
        landing = [_empty_hbm((N_CHIPS - 1,) + s.shape[1:], s.dtype) for s in self.sums]
        self.s, self.r, self.bufs, token = _split_start(
            self._copies, self.sums + landing, (self.nt, N_CHIPS - 1), None,
            name=f"comm_rs_chip_start_{self.tag}")
        return token

    def finish(self, after):
        bufs = _split_wait(self._copies, self.s, self.r, self.bufs, after, self.nt,
                           name=f"comm_rs_chip_wait_{self.tag}")
        return bufs[:self.nt], bufs[self.nt:]


def _adam_math(g, w, m, v):
    m2 = ADAM_B1 * m + (1.0 - ADAM_B1) * g
    v2 = ADAM_B2 * v + (1.0 - ADAM_B2) * (g * g)
    m_hat = m2 / (1.0 - ADAM_B1 ** ADAM_STEP)
    v_hat = v2 / (1.0 - ADAM_B2 ** ADAM_STEP)
    delta = -ADAM_LR * (m_hat / (jnp.sqrt(v_hat) + ADAM_EPS) + ADAM_WD * w)
    return delta, m2, v2


def _adamw_sharded(chip_sums, remote, chip, w, m, v, layer, prev, *, name):
    nl, r, c = w.shape
    tr = _rows(r, c)

    def body(*refs):
        p_ref, r0_ref, r1_ref, r2_ref, w_ref, m_ref, v_ref = refs[1:8]
        g_out, d_out, m_out, v_out = refs[-4:]
        g = ((p_ref[...].astype(F32) + r0_ref[...].astype(F32)) + r1_ref[...].astype(F32)) + r2_ref[...].astype(F32)
        g_out[...] = g
        d_out[...], m_out[...], v_out[...] = _adam_math(g, w_ref[...], m_ref[...], v_ref[...])

    pspec = pl.BlockSpec((None, tr, c), lambda i, chip_ref: (chip_ref[0], i, 0))

    def rspec(k):
        return pl.BlockSpec((None, tr, c), lambda i, chip_ref: (k, i, 0))

    wspec = pl.BlockSpec((None, tr, c), lambda i, chip_ref: (layer, i, 0))
    in_specs = [pspec, rspec(0), rspec(1), rspec(2), wspec, wspec, wspec]
    args = [chip, chip_sums, remote, remote, remote, w, m, v]
    aliases = {}
    if prev is not None:
        in_specs += [pl.BlockSpec(memory_space=pl.ANY)] * 4
        aliases = {len(args) + i: i for i in range(4)}
        args += list(prev)
    shp = jax.ShapeDtypeStruct(w.shape, F32)
    return _pcall(body, name=name, out_shape=(shp,) * 4, grid=(r // tr,), in_specs=in_specs, out_specs=(wspec,) * 4,
                  aliases=aliases, prefetch=1)(*args)


def _adamw_local(g, w, m, v, *, name):
    nl, r, c = w.shape
    tr = _rows(r, c)

    def body(g_ref, w_ref, m_ref, v_ref, d_out, m_out, v_out):
        d_out[...], m_out[...], v_out[...] = _adam_math(g_ref[...], w_ref[...], m_ref[...], v_ref[...])

    spec = pl.BlockSpec((None, tr, c), lambda l, i: (l, i, 0))
    shp = jax.ShapeDtypeStruct(w.shape, F32)
    return _pcall(body, name=name, out_shape=(shp,) * 3, grid=(nl, r // tr), in_specs=[spec] * 4,
                  out_specs=(spec,) * 3)(g, w, m, v)


def _adamw_replicated(parts, w, m, v, *, name):
    n = w.shape[1]

    def body(p_ref, w_ref, m_ref, v_ref, g_out, d_out, m_out, v_out):
        g = p_ref[0]
        for k in range(1, N_DEV):
            g = g + p_ref[k]
        g_out[...] = g
        d_out[...], m_out[...], v_out[...] = _adam_math(g, w_ref[...], m_ref[...], v_ref[...])

    vm = pl.BlockSpec(memory_space=pltpu.VMEM)
    shp = jax.ShapeDtypeStruct((1, n), F32)
    return _pcall(body, name=name, out_shape=(shp,) * 4, in_specs=[vm] * 4, out_specs=(vm,) * 4)(parts, w, m, v)


UNDILATED_OFFS = (0, A_HEADS, OFF_VA // HEAD_DIM)


def _mod_rows(mod, d):
    return [mod[:, i * d:(i + 1) * d] for i in range(6)]


MIXER_W = ("w_in", "w_branch_a", "w_branch_b", "w_out")
FFN_W = ("w_gate_up", "w_down")
SHARD_AXIS = {"w_in": 1, "w_branch_a": 1, "w_branch_b": 1, "w_out": 0, "w_gate_up": 1, "w_down": 0}


def _norm_args(mod, gain, which, d):
    rows = _mod_rows(mod, d)
    return gain, rows[3 * which + 1], rows[3 * which]


def _mixer_fwd_a(h, u, gains, w_in, cos2, sin2, hook):
    seq = h.shape[0]
    proj = _mm(u, w_in, name="mm_in")
    hook(proj)
    qk, qk32 = _qkrope_fwd(proj, gains, cos2, sin2, name="qkrope_fwd")
    os_, lses = [], []
    for g, dil in enumerate(DILATIONS):
        if dil == 1:
            o, lse = _dil_fwd(qk, qk, proj, UNDILATED_OFFS, seq, 1, name="dil_fwd_1")
        else:
            o, lse = _dil_fwd_strided(qk32, proj, g, dil, name=f"dil_fwd_{dil}")
        os_.append(o)
        lses.append(lse)
    o_a = _combine_fwd(os_, lses, name="combine_fwd")
    o_b = _sb_fwd(proj, name="sb_fwd")
    return dict(h_in=h, u=u, proj=proj, qk=qk, qk32=qk32, os=os_, lses=lses, o_a=o_a, o_b=o_b)


def _mixer_fwd_b(sv, mod, g2, wts):
    d = sv["h_in"].shape[1]
    merged, y_a, y_b = _mm_merge(sv["o_a"], sv["o_b"], wts["w_branch_a"], wts["w_branch_b"], sv["proj"],
                                 name="mm_branch")
    h_mid, t, u2 = _mm_resid_norm(merged, wts["w_out"], sv["h_in"], _mod_rows(mod, d)[2], _norm_args(mod, g2, 1, d),
                                  name="mm_out")
    sv.update(y_a=y_a, y_b=y_b, merged=merged, t=t, h_mid=h_mid, u2=u2)
    return h_mid


def _ffn_fwd_a(sv, w_gate_up):
    a, g, u = _mm_swiglu(sv["u2"], w_gate_up, name="mm_gate_up")
    sv.update(g=g, up=u, a=a)
    return a


def _ffn_fwd_b(sv, mod, w_down, next_norm):
    d = sv["h_mid"].shape[1]
    h_out, sv["f"], u_next = _mm_resid_norm(sv["a"], w_down, sv["h_mid"], _mod_rows(mod, d)[5], next_norm,
                                            name="mm_down")
    return h_out, u_next


def _wgrad(act, dout, key):
    return _mm(act, dout, ta=True, out_dtype=BF16, caps=(2048, 1024, 3072), name="mm_wgrad_" + key)


def _ffn_bwd(dh, df, dgate2, sv, mod, g2, wts, hook):
    d = dh.shape[1]
    sc2, ga1 = _mod_rows(mod, d)[4], _mod_rows(mod, d)[2]
    dg, dup = _mm_down_t_swiglu(df, wts["w_down"], sv["g"], sv["up"], name="mm_down_t")
    grads = {"w_down": _wgrad(sv["a"], df, "w_down")}
    hook(dup)
    du2 = _mm_cat_k(dg, dup, wts["w_gate_up"], name="mm_gate_up_t")
    grads["w_gate_up"] = _mm_cat_n(sv["u2"], dg, dup, name="mm_wgrad_w_gate_up")
    dh_mid, dsh2, dsc2, dg2, dt, dgate1 = _rmsmod_bwd(du2, sv["h_mid"], g2, sc2, dh, sv["t"], ga1, name="rmsmod_bwd")
    return dh_mid, [dsh2, dsc2, dgate2], dg2, grads, dt, dgate1


def _mixer_bwd(dh_mid, dt, dgate1, sv, mod, g1, gains, wts, cos2, sin2, hook, below):
    seq, d = dh_mid.shape
    sc1 = _mod_rows(mod, d)[1]
    dy_a, dy_b, dga, dgb = _mm_out_t_merge(dt, wts["w_out"], sv["proj"], sv["y_a"], sv["y_b"], name="mm_out_t")
    grads = {"w_out": _wgrad(sv["merged"], dt, "w_out")}
    do_a = _mm(dy_a, wts["w_branch_a"], tb=True, name="mm_branch_t")
    do_b = _mm(dy_b, wts["w_branch_b"], tb=True, name="mm_branch_t")
    grads["w_branch_a"] = _wgrad(sv["o_a"], dy_a, "w_branch_a")
    grads["w_branch_b"] = _wgrad(sv["o_b"], dy_b, "w_branch_b")
    dqb, dkb, dvb = _sb_bwd(sv["proj"], do_b, name="sb_bwd")
    hook(dqb, grads)
    comb = _combine_bwd(do_a, sv["os"], sv["lses"], name="combine_bwd")
    grads = {}
    dos, dls = comb[:3], comb[3:]
    dqs, dks, dvs = [], [], []
    for g, dil in enumerate(DILATIONS):
        if dil == 1:
            dq, dk, dv = _dil_bwd(sv["qk"], sv["qk"], sv["proj"], UNDILATED_OFFS, sv["os"][g], sv["lses"][g], dos[g],
                                  dls[g], seq, 1, name="dil_bwd_1")
        else:
            dq, dk, dv = _dil_bwd_strided(sv["qk32"], sv["proj"], g, sv["os"][g], sv["lses"][g], dos[g], dls[g], dil,
                                          name=f"dil_bwd_{dil}")
        dqs.append(dq)
        dks.append(dk)
        dvs.append(dv)
    dq_pre, dqn = _qkrope_bwd(dqs, sv["proj"], gains, 0, cos2, sin2, name="qkrope_bwd")
    dk_pre, dkn = _qkrope_bwd(dks, sv["proj"], gains, 1, cos2, sin2, name="qkrope_bwd")
    dgains = jnp.stack([dqn, dkn])
    dproj = _assemble([dq_pre, dk_pre] + dvs + [dqb, dkb, dvb, dga, dgb], name="assemble_dproj")
    du = _mm(dproj, wts["w_in"], tb=True, name="mm_in_t")
    grads["w_in"] = _wgrad(sv["u"], dproj, "w_in")
    dh_in, dsh1, dsc1, dg1, df, dgate2 = _rmsmod_bwd(du, sv["h_in"], g1, sc1, dh_mid, *(below or (None, None)),
                                                     name="rmsmod_bwd")
    return dh_in, [dsh1, dsc1, dgate1], dg1, dgains, grads, df, dgate2


def kernel(x, c, w_ada, b_ada, norm1_g, norm2_g, w_in, qn_g, kn_g, w_branch_a, w_branch_b, w_out, w_gate_up, w_down, loss_target, m_w_ada, m_b_ada, m_norm1_g, m_norm2_g, m_w_in, m_qn_g, m_kn_g, m_w_branch_a, m_w_branch_b, m_w_out, m_w_gate_up, m_w_down, v_w_ada, v_b_ada, v_norm1_g, v_norm2_g, v_w_in, v_qn_g, v_kn_g, v_w_branch_a, v_w_branch_b, v_w_out, v_w_gate_up, v_w_down):
    _ORDER["token"] = None
    seq, d = x.shape[1], x.shape[2]
    depth = w_in.shape[0]
    weights = dict(w_in=w_in, w_branch_a=w_branch_a, w_branch_b=w_branch_b, w_out=w_out, w_gate_up=w_gate_up,
                   w_down=w_down)
    moments_m = dict(w_in=m_w_in, w_branch_a=m_w_branch_a, w_branch_b=m_w_branch_b, w_out=m_w_out,
                     w_gate_up=m_w_gate_up, w_down=m_w_down)
    moments_v = dict(w_in=v_w_in, w_branch_a=v_w_branch_a, w_branch_b=v_w_branch_b, w_out=v_w_out,
                     w_gate_up=v_w_gate_up, w_down=v_w_down)
    xi, yi, ci = _coords()
    me = 4 * xi + 2 * yi + ci
    core = jnp.reshape(ci, (1,)).astype(jnp.int32)
    chip = jnp.reshape(2 * xi + yi, (1,)).astype(jnp.int32)

    ada_w = w_ada.shape[2]
    c_act = _small_allgather(c, name="comm_gather_c", silu=True).reshape(N_DEV, d)
    c_pad = jnp.concatenate([c_act, jnp.zeros_like(c_act)], axis=0).astype(BF16)
    bias = lax.dynamic_slice(b_ada, (0, me * ada_w), (depth, ada_w))
    mod_part = jnp.stack([_mm(c_pad, w_ada[l], name="mm_ada")[:N_DEV] for l in range(depth)]) + bias[:, None, :]
    mod_all = _small_allgather(mod_part.reshape(1, depth * N_DEV * ada_w), name="comm_gather_mod")
    mod_all = mod_all.reshape(N_DEV, depth, N_DEV, ada_w)
    mod_mine = lax.dynamic_index_in_dim(mod_all, me, axis=2, keepdims=False)
    mods = jnp.transpose(mod_mine, (1, 0, 2)).reshape(depth, 1, 6 * d)

    cos2, sin2 = _rope_tables(seq)
    gains = [jnp.stack([qn_g[l], kn_g[l]])[:, None, :] for l in range(depth)]
    g1s = [norm1_g[l][None] for l in range(depth)]
    g2s = [norm2_g[l][None] for l in range(depth)]

    me_arr = jnp.reshape(me, (1,)).astype(jnp.int32)

    def placed(keys, l):
        return [_cast_place(weights[k], l, SHARD_AXIS[k], me_arr, name="cast_place_" + k) for k in keys]

    def gather_of(keys, l, tag):
        return _SplitGather(placed(keys, l), [SHARD_AXIS[k] for k in keys], f"{tag}{l}")

    groups = []
    for l in range(depth):
        groups += [("w_in", l, MIXER_W[:1]), ("rest", l, MIXER_W[1:]), ("ffn", l, FFN_W)]
    gathers = {}

    def issue(some):
        for tag, l, keys in some:
            gathers[tag, l] = gather_of(keys, l, tag)
            gathers[tag, l].first(after=mods)

    issue(groups[:3])
    h = x[0]
    u = _rmsmod_fwd(h, *_norm_args(mods[0], g1s[0], 0, d), name="rmsmod_fwd")
    gathers["w_in", 0].forward(after=u)
    issue(groups[3:])
    wm = {"w_in": gathers["w_in", 0].finish(after=u)[0]}
    saved, full = [], []
    for l in range(depth):
        last = l + 1 == depth
        sv = _mixer_fwd_a(h, u, gains[l], wm["w_in"], cos2, sin2, gathers["rest", l].forward)
        gathers["ffn", l].forward(after=sv["o_b"])
        wm.update(zip(MIXER_W[1:], gathers["rest", l].finish(after=sv["o_b"])))
        h_mid = _mixer_fwd_b(sv, mods[l], g2s[l], wm)
        wf = dict(zip(FFN_W, gathers["ffn", l].finish(after=h_mid)))
        a = _ffn_fwd_a(sv, wf["w_gate_up"])
        if not last:
            gathers["w_in", l + 1].forward(after=a)
        h, u = _ffn_fwd_b(sv, mods[l], wf["w_down"],
                          None if last else _norm_args(mods[l + 1], g1s[l + 1], 0, d))
        saved.append(sv)
        full.append({**wm, **wf})
        if not last:
            wm = {"w_in": gathers["w_in", l + 1].finish(after=h)[0]}
    def ffn_gate(l):
        return saved[l]["f"], _mod_rows(mods[l], d)[5]

    loss_part, dh, df, dgate2 = _loss_fwd(h, loss_target[0], *ffn_gate(depth - 1), name="loss")
    loss = lax.psum(loss_part[0, 0], ("x", "y", "c"))

    pipe = _ReducePipeline(core)
    dmods, dg1s, dg2s, dgains = [None] * depth, [None] * depth, [None] * depth, [None] * depth
    for l in reversed(range(depth)):
        dh_mid, dmod_f, dg2s[l], grads, dt, dgate1 = _ffn_bwd(dh, df, dgate2, saved[l], mods[l], g2s[l], full[l],
                                                              pipe.tick)
        pipe.tick(dh_mid)
        pipe.add(FFN_W, grads, l)
        dh, dmod_m, dg1s[l], dgains[l], grads, df, dgate2 = _mixer_bwd(
            dh_mid, dt, dgate1, saved[l], mods[l], g1s[l], gains[l], full[l], cos2, sin2,
            lambda after, early, l=l: (pipe.tick(after), pipe.add(MIXER_W[1:], early, l)),
            ffn_gate(l - 1) if l > 0 else None)
        dmods[l] = jnp.concatenate(dmod_m + dmod_f, axis=1)
        pipe.tick(dh)
        pipe.add(MIXER_W[:1], grads, l)
    grad_x = dh[None]

    stacked = {}

    def update(items):
        for keys, l, sums, remote in items:
            for k, p_, r_ in zip(keys, sums, remote):
                stacked[k] = _adamw_sharded(p_, r_, chip, weights[k], moments_m[k], moments_v[k], l,
                                            stacked.get(k), name="adamw_" + k)

    ready = pipe.take_done()
    update([it for it in ready if it[0] != FFN_W])

    small = jnp.concatenate(
        dmods + dg1s + dg2s + [dgains[l][0] for l in range(depth)] + [dgains[l][1] for l in range(depth)], axis=1)
    small_all = _small_allgather(small, name="comm_gather_small")
    pipe.tick(small_all)
    update([it for it in ready if it[0] == FFN_W] + pipe.take_done())

    def pack(b, n1, n2, qn, kn):
        return jnp.concatenate([t_.reshape(1, -1) for t_ in (b, n1, n2, qn, kn)], axis=1)

    sg, sd, sm, sv_ = _adamw_replicated(small_all, pack(b_ada, norm1_g, norm2_g, qn_g, kn_g),
                                        pack(m_b_ada, m_norm1_g, m_norm2_g, m_qn_g, m_kn_g),
                                        pack(v_b_ada, v_norm1_g, v_norm2_g, v_qn_g, v_kn_g), name="adamw_replicated")

    def unpack(p):
        sizes = [depth * 6 * d, depth * d, depth * d, depth * HEAD_DIM, depth * HEAD_DIM]
        shapes = [b_ada.shape, norm1_g.shape, norm2_g.shape, qn_g.shape, kn_g.shape]
        out, off = [], 0
        for n, shp in zip(sizes, shapes):
            out.append(p[0, off:off + n].reshape(shp))
            off += n
        return dict(zip(("b_ada", "norm1_g", "norm2_g", "qn_g", "kn_g"), out))

    ug, ud, um, uv = unpack(sg), unpack(sd), unpack(sm), unpack(sv_)
    res = {k: dict(g=ug[k], d=ud[k], m=um[k], v=uv[k]) for k in ug}

    dmod_all = small_all[:, 0, :depth * 6 * d].reshape(N_DEV, depth, 6 * d)
    g_ada = None
    for l in range(depth):
        dm = lax.dynamic_slice(dmod_all[:, l, :], (0, me * ada_w), (N_DEV, ada_w))
        dm = jnp.concatenate([dm, jnp.zeros_like(dm)], axis=0).astype(BF16)
        g_ada = _mm(c_pad, dm, ta=True, name="mm_wgrad_ada", stack=(l, depth, g_ada))
    d_ada, m_ada, v_ada = _adamw_local(g_ada, w_ada, m_w_ada, v_w_ada, name="adamw_local")
    res["w_ada"] = dict(g=g_ada, d=d_ada, m=m_ada, v=v_ada)

    pipe.tick(d_ada)
    update(pipe.take_done())
    pipe.tick(d_ada, flush=True)
    update(pipe.take_done())
    for k, (g_, d_, m_, v_) in stacked.items():
        res[k] = dict(g=g_, d=d_, m=m_, v=v_)

    order = ("w_ada", "b_ada", "norm1_g", "norm2_g", "w_in", "qn_g", "kn_g", "w_branch_a", "w_branch_b", "w_out",
             "w_gate_up", "w_down")
    _ORDER["token"] = None
    return (loss, grad_x, *[res[k]["g"] for k in order], *[res[k]["d"] for k in order],
            *[res[k]["m"] for k in order], *[res[k]["v"] for k in order])
```

```python
import jax
import jax.numpy as jnp
from jax import lax
from jax.experimental import pallas as pl
from jax.experimental.pallas import tpu as pltpu

F32 = jnp.float32
BF16 = jnp.bfloat16

HEAD_DIM = 128
BLOCK = 128
DILATIONS = (1, 4, 16)
HEADS_PER_GROUP = 4
A_HEADS = 12
SB_HEADS = 4
GROUP_W = HEADS_PER_GROUP * HEAD_DIM
A_W = A_HEADS * HEAD_DIM
B_W = SB_HEADS * HEAD_DIM
OFF_QA, OFF_KA, OFF_VA = 0, A_W, 2 * A_W
OFF_QB, OFF_KB, OFF_VB = 3 * A_W, 3 * A_W + B_W, 3 * A_W + 2 * B_W
OFF_GATES = 3 * A_W + 3 * B_W
ROPE_THETA = 10000.0
EPS = 1e-6
ATT_SCALE = HEAD_DIM ** -0.5
MASKED = -1e30

ADAM_LR, ADAM_B1, ADAM_B2, ADAM_EPS, ADAM_WD, ADAM_STEP = 0.001, 0.9, 0.999, 1e-08, 0.01, 10

N_DEV = 8
N_CHIPS = 4
V7X_VMEM_LIMIT_BYTES = 56 * 1024 * 1024
ELEMWISE_BLOCK_BYTES = 2 * 1024 * 1024
MESH = pl.DeviceIdType.MESH

NN = (((1,), (0,)), ((), ()))
NT = (((1,), (1,)), ((), ()))
TN = (((0,), (0,)), ((), ()))


def _dot(a, b, dims=NN):
    return lax.dot_general(a, b, dims, preferred_element_type=F32)


def _tile(n, cap, mult=128):
    best = None
    for t in range(mult, min(n, cap) + 1, mult):
        if n % t == 0:
            best = t
    if best is None:
        assert n <= 2 * cap, (n, cap)
        return n
    return best


def _rows(r, c):
    return _tile(r, max(16, ELEMWISE_BLOCK_BYTES // (4 * c)), 16)


_ORDER = {"token": None}
TOKEN = jax.ShapeDtypeStruct((8, 128), F32)


def _take_token():
    prev = _ORDER["token"]
    return [] if prev is None else [prev]


def _pcall(body, *, name, out_shape, grid=None, in_specs=None, out_specs=None, scratch=(), aliases=None,
           prefetch=0):
    single = not isinstance(out_shape, (tuple, list))
    out_shapes = [out_shape] if single else list(out_shape)
    out_specs = [out_specs] if single else list(out_specs)
    extra = _take_token()
    n_in, n_extra, n_out = prefetch + len(in_specs), len(extra), len(out_shapes)

    def wrapped(*refs):
        token = refs[n_in + n_extra + n_out]
        token[...] = jnp.zeros_like(token)
        return body(*refs[:n_in], *refs[n_in + n_extra:n_in + n_extra + n_out], *refs[n_in + n_extra + n_out + 1:])

    in_specs = list(in_specs) + [pl.BlockSpec(memory_space=pl.ANY)] * n_extra
    if grid is None:
        out_specs.append(pl.BlockSpec(memory_space=pltpu.VMEM))
    else:
        out_specs.append(pl.BlockSpec(TOKEN.shape, lambda *_: (0, 0)))
    kwargs = dict(name=name, out_shape=out_shapes + [TOKEN], input_output_aliases=aliases or {},
                  compiler_params=pltpu.CompilerParams(vmem_limit_bytes=V7X_VMEM_LIMIT_BYTES))
    if prefetch:
        call = pl.pallas_call(wrapped, grid_spec=pltpu.PrefetchScalarGridSpec(
            num_scalar_prefetch=prefetch, grid=grid, in_specs=in_specs, out_specs=out_specs,
            scratch_shapes=list(scratch)), **kwargs)
    else:
        if grid is not None:
            kwargs["grid"] = grid
        call = pl.pallas_call(wrapped, in_specs=in_specs, out_specs=out_specs, scratch_shapes=list(scratch), **kwargs)

    def run(*args):
        outs = call(*args, *extra)
        _ORDER["token"] = outs[-1]
        return outs[0] if single else tuple(outs[:-1])

    return run


def _mm(a, b, *, name, ta=False, tb=False, out_dtype=F32, caps=(1024, 1024, 3072), stack=None):
    kdim, m = a.shape if ta else a.shape[::-1]
    n, k2 = b.shape if tb else b.shape[::-1]
    assert kdim == k2, (a.shape, b.shape, ta, tb)
    tm, tn, tk = _tile(m, caps[0]), _tile(n, caps[1]), _tile(kdim, caps[2])
    nk = kdim // tk
    dims = (((0 if ta else 1,), (1 if tb else 0,)), ((), ()))

    def body(*refs):
        a_ref, b_ref = refs[0], refs[1]
        part = _dot(a_ref[...].astype(BF16), b_ref[...].astype(BF16), dims)
        if nk == 1:
            o_ref = refs[-1]
            o_ref[...] = part.astype(o_ref.dtype)
            return
        o_ref, acc_ref = refs[-2], refs[-1]
        k = pl.program_id(2)

        @pl.when(k == 0)
        def _():
            acc_ref[...] = part

        @pl.when(k > 0)
        def _():
            acc_ref[...] += part

        @pl.when(k == nk - 1)
        def _():
            o_ref[...] = acc_ref[...].astype(o_ref.dtype)

    a_spec = (pl.BlockSpec((tk, tm), lambda i, j, k: (k, i)) if ta
              else pl.BlockSpec((tm, tk), lambda i, j, k: (i, k)))
    b_spec = (pl.BlockSpec((tn, tk), lambda i, j, k: (j, k)) if tb
              else pl.BlockSpec((tk, tn), lambda i, j, k: (k, j)))
    ins, in_specs, aliases = [a, b], [a_spec, b_spec], {}
    if stack is None:
        out_shape = jax.ShapeDtypeStruct((m, n), out_dtype)
        out_spec = pl.BlockSpec((tm, tn), lambda i, j, k: (i, j))
    else:
        layer, n_layers, buf = stack
        out_shape = jax.ShapeDtypeStruct((n_layers, m, n), out_dtype)
        out_spec = pl.BlockSpec((None, tm, tn), lambda i, j, k: (layer, i, j))
        if buf is not None:
            ins.append(buf)
            in_specs.append(pl.BlockSpec(memory_space=pl.ANY))
            aliases = {2: 0}
    scratch = [] if nk == 1 else [pltpu.VMEM((tm, tn), F32)]
    return _pcall(body, name=name, out_shape=out_shape, grid=(m // tm, n // tn, nk), in_specs=in_specs,
                  out_specs=out_spec, scratch=scratch, aliases=aliases)(*ins)


EPILOGUE_ROWS = 256


def _row_chunks(tm):
    return [slice(r, r + EPILOGUE_ROWS) for r in range(0, tm, EPILOGUE_ROWS)] if tm > EPILOGUE_ROWS else [slice(0, tm)]


def _mm_cat_k(a_lo, a_hi, b, *, name):
    m, f = a_lo.shape
    n = b.shape[0]
    tm, tn, tk = _tile(m, 1024), _tile(n, 1024), _tile(f, 3072)
    half = f // tk
    nk = 2 * half

    def body(lo_ref, hi_ref, b_ref, o_ref, acc_ref):
        k = pl.program_id(2)

        def accumulate(a_ref):
            part = _dot(a_ref[...], b_ref[...], NT)

            @pl.when(k == 0)
            def _():
                acc_ref[...] = part

            @pl.when(k > 0)
            def _():
                acc_ref[...] += part

        pl.when(k < half)(lambda: accumulate(lo_ref))
        pl.when(k >= half)(lambda: accumulate(hi_ref))

        @pl.when(k == nk - 1)
        def _():
            o_ref[...] = acc_ref[...]

    return _pcall(body, name=name, out_shape=jax.ShapeDtypeStruct((m, n), F32), grid=(m // tm, n // tn, nk),
                  in_specs=[pl.BlockSpec((tm, tk), lambda i, j, k: (i, jnp.minimum(k, half - 1))),
                            pl.BlockSpec((tm, tk), lambda i, j, k: (i, jnp.maximum(k - half, 0))),
                            pl.BlockSpec((tn, tk), lambda i, j, k: (j, k))],
                  out_specs=pl.BlockSpec((tm, tn), lambda i, j, k: (i, j)),
                  scratch=[pltpu.VMEM((tm, tn), F32)])(a_lo, a_hi, b)


def _mm_cat_n(a, b_lo, b_hi, *, name):
    s, m = a.shape
    f = b_lo.shape[1]
    tm, tn = _tile(m, 2048), _tile(f, 1024)
    half = f // tn

    def body(a_ref, lo_ref, hi_ref, o_ref):
        j = pl.program_id(1)

        @pl.when(j < half)
        def _():
            o_ref[...] = _dot(a_ref[...], lo_ref[...], TN).astype(BF16)

        @pl.when(j >= half)
        def _():
            o_ref[...] = _dot(a_ref[...], hi_ref[...], TN).astype(BF16)

    return _pcall(body, name=name, out_shape=jax.ShapeDtypeStruct((m, 2 * f), BF16), grid=(m // tm, 2 * half),
                  in_specs=[pl.BlockSpec((s, tm), lambda i, j: (0, i)),
                            pl.BlockSpec((s, tn), lambda i, j: (0, jnp.minimum(j, half - 1))),
                            pl.BlockSpec((s, tn), lambda i, j: (0, jnp.maximum(j - half, 0)))],
                  out_specs=pl.BlockSpec((tm, tn), lambda i, j: (i, j)))(a, b_lo, b_hi)


def _mm_resid_norm(a, w, h, gate, norm, *, name):
    s, kdim = a.shape
    d = w.shape[1]
    tk = _tile(kdim, 2048)
    nk = kdim // tk
    tm = _tile(s, 256 if nk == 1 else 512)

    def body(*refs):
        a_ref, w_ref, h_ref, gate_ref = refs[:4]
        outs = refs[7:] if norm is not None else refs[4:]

        def finish(rows, t):
            hn = h_ref[rows, :] + gate_ref[...] * t
            outs[0][rows, :] = hn
            outs[1][rows, :] = t.astype(BF16)
            if norm is not None:
                g_ref, sc_ref, sh_ref = refs[4:7]
                r = lax.rsqrt(jnp.mean(hn * hn, axis=-1, keepdims=True) + EPS)
                outs[2][rows, :] = (((hn * r) * g_ref[...]) * (1.0 + sc_ref[...]) + sh_ref[...]).astype(BF16)

        if nk == 1:
            for rows in _row_chunks(tm):
                finish(rows, _dot(a_ref[rows, :], w_ref[...]))
            return
        acc_ref = refs[-1]
        k = pl.program_id(1)

        @pl.when(k == 0)
        def _():
            acc_ref[...] = _dot(a_ref[...], w_ref[...])

        @pl.when(jnp.logical_and(k > 0, k < nk - 1))
        def _():
            acc_ref[...] += _dot(a_ref[...], w_ref[...])

        @pl.when(k == nk - 1)
        def _():
            for rows in _row_chunks(tm):
                finish(rows, acc_ref[rows, :] + _dot(a_ref[rows, :], w_ref[...]))

    row = pl.BlockSpec((tm, d), lambda i, k: (i, 0))
    vec = pl.BlockSpec((1, d), lambda i, k: (0, 0))
    in_specs = [pl.BlockSpec((tm, tk), lambda i, k: (i, k)), pl.BlockSpec((tk, d), lambda i, k: (k, 0)), row, vec]
    args = [a, w, h, gate]
    out_shape = [jax.ShapeDtypeStruct((s, d), F32), jax.ShapeDtypeStruct((s, d), BF16)]
    if norm is not None:
        in_specs += [vec, vec, vec]
        args += list(norm)
        out_shape.append(jax.ShapeDtypeStruct((s, d), BF16))
    outs = _pcall(body, name=name, out_shape=tuple(out_shape), grid=(s // tm, nk), in_specs=in_specs,
                  out_specs=(row,) * len(out_shape), scratch=[] if nk == 1 else [pltpu.VMEM((tm, d), F32)])(*args)
    return outs if norm is not None else (*outs, None)


def _mm_merge(o_a, o_b, w_a, w_b, proj, *, name):
    s = o_a.shape[0]
    d = w_a.shape[1]
    tm = _tile(s, 512)
    ga_blk = OFF_GATES // d

    def body(oa_ref, ob_ref, wa_ref, wb_ref, ga_ref, gb_ref, m_ref, ya_ref, yb_ref):
        for rows in _row_chunks(tm):
            ya, yb = _dot(oa_ref[rows, :], wa_ref[...]), _dot(ob_ref[rows, :], wb_ref[...])
            m_ref[rows, :] = (jax.nn.sigmoid(ga_ref[rows, :]) * ya
                              + jax.nn.sigmoid(gb_ref[rows, :]) * yb).astype(BF16)
            ya_ref[rows, :] = ya.astype(BF16)
            yb_ref[rows, :] = yb.astype(BF16)

    row = pl.BlockSpec((tm, d), lambda i: (i, 0))
    act = pl.BlockSpec((tm, o_a.shape[1]), lambda i: (i, 0))
    wspec = pl.BlockSpec(w_a.shape, lambda i: (0, 0))
    shp = jax.ShapeDtypeStruct((s, d), BF16)
    return _pcall(body, name=name, out_shape=(shp, shp, shp), grid=(s // tm,),
                  in_specs=[act, act, wspec, wspec, pl.BlockSpec((tm, d), lambda i: (i, ga_blk)),
                            pl.BlockSpec((tm, d), lambda i: (i, ga_blk + 1))],
                  out_specs=(row, row, row))(o_a, o_b, w_a, w_b, proj, proj)


def _mm_out_t_merge(dt, w_out, proj, y_a, y_b, *, name):
    s, d = dt.shape
    tm, tn = _tile(s, 1024), _tile(d, 512)
    ga_blk = OFF_GATES // tn

    def body(dt_ref, w_ref, ga_ref, gb_ref, ya_ref, yb_ref, dya_ref, dyb_ref, dga_ref, dgb_ref):
        w = w_ref[...]
        for rows in _row_chunks(tm):
            dm = _dot(dt_ref[rows, :], w, NT)
            sa, sb = jax.nn.sigmoid(ga_ref[rows, :]), jax.nn.sigmoid(gb_ref[rows, :])
            dya_ref[rows, :] = (dm * sa).astype(BF16)
            dyb_ref[rows, :] = (dm * sb).astype(BF16)
            dga_ref[rows, :] = (dm * ya_ref[rows, :] * (sa * (1.0 - sa))).astype(BF16)
            dgb_ref[rows, :] = (dm * yb_ref[rows, :] * (sb * (1.0 - sb))).astype(BF16)

    tile = pl.BlockSpec((tm, tn), lambda i, j: (i, j))
    shp = jax.ShapeDtypeStruct((s, d), BF16)
    return _pcall(body, name=name, out_shape=(shp,) * 4, grid=(s // tm, d // tn),
                  in_specs=[pl.BlockSpec((tm, d), lambda i, j: (i, 0)), pl.BlockSpec((tn, d), lambda i, j: (j, 0)),
                            pl.BlockSpec((tm, tn), lambda i, j: (i, ga_blk + j)),
                            pl.BlockSpec((tm, tn), lambda i, j: (i, ga_blk + d // tn + j)), tile, tile],
                  out_specs=(tile,) * 4)(dt, w_out, proj, proj, y_a, y_b)


def _mm_down_t_swiglu(df, w_down, g, u, *, name):
    s, d = df.shape
    f = w_down.shape[0]
    tm, tn = _tile(s, 1024), _tile(f, 512)

    def body(df_ref, w_ref, g_ref, u_ref, dg_ref, du_ref):
        w = w_ref[...]
        for rows in _row_chunks(tm):
            da = _dot(df_ref[rows, :], w, NT)
            gf = g_ref[rows, :].astype(F32)
            sg = jax.nn.sigmoid(gf)
            dg_ref[rows, :] = (da * u_ref[rows, :].astype(F32) * (sg * (1.0 + gf * (1.0 - sg)))).astype(BF16)
            du_ref[rows, :] = (da * (gf * sg)).astype(BF16)

    tile = pl.BlockSpec((tm, tn), lambda i, j: (i, j))
    shp = jax.ShapeDtypeStruct((s, f), BF16)
    return _pcall(body, name=name, out_shape=(shp, shp), grid=(s // tm, f // tn),
                  in_specs=[pl.BlockSpec((tm, d), lambda i, j: (i, 0)), pl.BlockSpec((tn, d), lambda i, j: (j, 0)),
                            tile, tile],
                  out_specs=(tile, tile))(df, w_down, g, u)


def _rmsmod_fwd(h, g, scale, shift, *, name):
    s, d = h.shape
    ts = _rows(s, d)

    def body(h_ref, g_ref, sc_ref, sh_ref, u_ref):
        hf = h_ref[...]
        r = lax.rsqrt(jnp.mean(hf * hf, axis=-1, keepdims=True) + EPS)
        u_ref[...] = (((hf * r) * g_ref[...]) * (1.0 + sc_ref[...]) + sh_ref[...]).astype(BF16)

    row = pl.BlockSpec((ts, d), lambda i: (i, 0))
    vec = pl.BlockSpec((1, d), lambda i: (0, 0))
    return _pcall(body, name=name, out_shape=jax.ShapeDtypeStruct((s, d), BF16), grid=(s // ts,),
                  in_specs=[row, vec, vec, vec], out_specs=row)(h, g, scale, shift)


def _gate_bwd(dhf, t_ref, gate_ref, dt_ref, dgate_ref):
    dt_ref[...] = (dhf * gate_ref[...]).astype(BF16)
    dgate_ref[...] += jnp.sum(dhf * t_ref[...], axis=0, keepdims=True)


def _rmsmod_bwd(du, h, g, scale, dres, t, gate, *, name):
    s, d = h.shape
    ts = _rows(s, d)
    chain = t is not None

    def body(*refs):
        du_ref, h_ref, g_ref, sc_ref, dres_ref = refs[:5]
        dh_ref, dsh_ref, dsc_ref, dg_ref = refs[-6:-2] if chain else refs[-4:]
        sums = (dsh_ref, dsc_ref, dg_ref) + ((refs[-1],) if chain else ())

        @pl.when(pl.program_id(0) == 0)
        def _():
            for ref in sums:
                ref[...] = jnp.zeros_like(ref)

        hf, duf, gain = h_ref[...], du_ref[...], g_ref[...]
        r = lax.rsqrt(jnp.mean(hf * hf, axis=-1, keepdims=True) + EPS)
        xh = hf * r
        dn = duf * (1.0 + sc_ref[...])
        dsh_ref[...] += jnp.sum(duf, axis=0, keepdims=True)
        dsc_ref[...] += jnp.sum(duf * (xh * gain), axis=0, keepdims=True)
        dg_ref[...] += jnp.sum(dn * xh, axis=0, keepdims=True)
        dxh = dn * gain
        dh = dres_ref[...] + r * (dxh - xh * jnp.mean(dxh * xh, axis=-1, keepdims=True))
        dh_ref[...] = dh
        if chain:
            _gate_bwd(dh, refs[5], refs[6], refs[-2], refs[-1])

    row = pl.BlockSpec((ts, d), lambda i: (i, 0))
    vec = pl.BlockSpec((1, d), lambda i: (0, 0))
    vshape = jax.ShapeDtypeStruct((1, d), F32)
    out_shape, out_specs = [jax.ShapeDtypeStruct((s, d), F32), vshape, vshape, vshape], [row, vec, vec, vec]
    in_specs, args = [row, row, vec, vec, row], [du, h, g, scale, dres]
    if chain:
        in_specs, args = in_specs + [row, vec], args + [t, gate]
        out_shape, out_specs = out_shape + [jax.ShapeDtypeStruct((s, d), BF16), vshape], out_specs + [row, vec]
    outs = _pcall(body, name=name, out_shape=tuple(out_shape), grid=(s // ts,), in_specs=in_specs,
                  out_specs=tuple(out_specs))(*args)
    return outs if chain else (*outs, None, None)


def _mm_swiglu(u2, w_gate_up, *, name):
    s, d = u2.shape
    f = w_gate_up.shape[1] // 2
    tm, tn = _tile(s, 1024), _tile(f, 512)
    nj = f // tn

    def body(x_ref, wg_ref, wu_ref, a_ref, g_ref, u_ref):
        for rows in _row_chunks(tm):
            x = x_ref[rows, :]
            gf, uf = _dot(x, wg_ref[...]), _dot(x, wu_ref[...])
            a_ref[rows, :] = ((gf * jax.nn.sigmoid(gf)) * uf).astype(BF16)
            g_ref[rows, :] = gf.astype(BF16)
            u_ref[rows, :] = uf.astype(BF16)

    out = pl.BlockSpec((tm, tn), lambda i, j: (i, j))
    shp = jax.ShapeDtypeStruct((s, f), BF16)
    return _pcall(body, name=name, out_shape=(shp, shp, shp), grid=(s // tm, nj),
                  in_specs=[pl.BlockSpec((tm, d), lambda i, j: (i, 0)), pl.BlockSpec((d, tn), lambda i, j: (0, j)),
                            pl.BlockSpec((d, tn), lambda i, j: (0, nj + j))],
                  out_specs=(out, out, out))(u2, w_gate_up, w_gate_up)


def _loss_fwd(y, tgt, t, gate, *, name):
    s, d = y.shape
    ts = _rows(s, d)

    def body(y_ref, tgt_ref, t_ref, gate_ref, l_ref, dy_ref, dt_ref, dgate_ref):
        @pl.when(pl.program_id(0) == 0)
        def _():
            l_ref[...] = jnp.zeros_like(l_ref)
            dgate_ref[...] = jnp.zeros_like(dgate_ref)

        e = y_ref[...] - tgt_ref[...]
        dy = e * (1.0 / d)
        dy_ref[...] = dy
        per_tok = jnp.sum(e * e, axis=1, keepdims=True) * (1.0 / d)
        l_ref[...] += 0.5 * jnp.sum(per_tok, axis=0, keepdims=True)
        _gate_bwd(dy, t_ref, gate_ref, dt_ref, dgate_ref)

    row = pl.BlockSpec((ts, d), lambda i: (i, 0))
    vec = pl.BlockSpec((1, d), lambda i: (0, 0))
    return _pcall(body, name=name,
                  out_shape=(jax.ShapeDtypeStruct((1, 128), F32), jax.ShapeDtypeStruct((s, d), F32),
                             jax.ShapeDtypeStruct((s, d), BF16), jax.ShapeDtypeStruct((1, d), F32)),
                  grid=(s // ts,), in_specs=[row, row, row, vec],
                  out_specs=(pl.BlockSpec((1, 128), lambda i: (0, 0)), row, row, vec))(y, tgt, t, gate)


def _rope_tables(seq):
    inv = jnp.power(ROPE_THETA, -jnp.arange(0, HEAD_DIM, 2, dtype=F32) / HEAD_DIM)
    ang = jnp.arange(seq, dtype=F32)[:, None] * inv[None, :]
    cos, sin = jnp.cos(ang), jnp.sin(ang)
    return jnp.concatenate([cos, cos], axis=1), jnp.concatenate([-sin, sin], axis=1)


def _qkrope_fwd(proj, gains, cos2, sin2, *, name):
    s = proj.shape[0]
    ts = _rows(s, A_W)

    def body(x_ref, g_ref, c_ref, s_ref, o_ref, o32_ref):
        gain, cos, sin = g_ref[...], c_ref[...], s_ref[...]
        for h in range(A_HEADS):
            lanes = slice(h * HEAD_DIM, (h + 1) * HEAD_DIM)
            x = x_ref[:, lanes]
            y = (x * lax.rsqrt(jnp.mean(x * x, axis=-1, keepdims=True) + EPS)) * gain
            out = y * cos + pltpu.roll(y, HEAD_DIM // 2, 1) * sin
            o_ref[:, lanes] = out.astype(BF16)
            o32_ref[:, lanes] = out

    heads = pl.BlockSpec((ts, A_W), lambda i, j: (i, j))
    tab = pl.BlockSpec((ts, HEAD_DIM), lambda i, j: (i, 0))
    gain = pl.BlockSpec((None, 1, HEAD_DIM), lambda i, j: (j, 0, 0))
    return _pcall(body, name=name,
                  out_shape=(jax.ShapeDtypeStruct((s, 2 * A_W), BF16), jax.ShapeDtypeStruct((s, 2 * A_W), F32)),
                  grid=(s // ts, 2), in_specs=[heads, gain, tab, tab], out_specs=(heads, heads))(
                      proj, gains, cos2, sin2)


def _qkrope_bwd(d_groups, proj, gains, which, cos2, sin2, *, name):
    s = proj.shape[0]
    ts = _rows(s, A_W)

    def body(d0_ref, d1_ref, d2_ref, x_ref, g_ref, c_ref, s_ref, dx_ref, dg_ref):
        @pl.when(pl.program_id(0) == 0)
        def _():
            dg_ref[...] = jnp.zeros_like(dg_ref)

        gain, cos, sin = g_ref[...], c_ref[...], s_ref[...]
        dg = jnp.zeros((1, HEAD_DIM), F32)
        for h in range(A_HEADS):
            lanes = slice(h * HEAD_DIM, (h + 1) * HEAD_DIM)
            slot = slice((h % HEADS_PER_GROUP) * HEAD_DIM, (h % HEADS_PER_GROUP + 1) * HEAD_DIM)
            dout = (d0_ref, d1_ref, d2_ref)[h // HEADS_PER_GROUP][:, slot]
            dy = dout * cos + pltpu.roll(dout * sin, HEAD_DIM // 2, 1)
            x = x_ref[:, lanes]
            r = lax.rsqrt(jnp.mean(x * x, axis=-1, keepdims=True) + EPS)
            xh = x * r
            dg = dg + jnp.sum(dy * xh, axis=0, keepdims=True)
            dxh = dy * gain
            dx_ref[:, lanes] = (r * (dxh - xh * jnp.mean(dxh * xh, axis=-1, keepdims=True))).astype(BF16)
        dg_ref[...] += dg

    group = pl.BlockSpec((ts, GROUP_W), lambda i: (i, 0))
    tab = pl.BlockSpec((ts, HEAD_DIM), lambda i: (i, 0))
    gain = pl.BlockSpec((None, 1, HEAD_DIM), lambda i: (which, 0, 0))
    return _pcall(body, name=name,
                  out_shape=(jax.ShapeDtypeStruct((s, A_W), BF16), jax.ShapeDtypeStruct((1, HEAD_DIM), F32)),
                  grid=(s // ts,),
                  in_specs=[group, group, group, pl.BlockSpec((ts, A_W), lambda i: (i, which)), gain, tab, tab],
                  out_specs=(pl.BlockSpec((ts, A_W), lambda i: (i, 0)), pl.BlockSpec((1, HEAD_DIM), lambda i: (0, 0))))(
                      *d_groups, proj, gains, cos2, sin2)


def _assemble(pieces, *, name):
    s = pieces[0].shape[0]
    widths = [p.shape[1] for p in pieces]
    total = sum(widths)
    ts = _rows(s, total // 2)

    def body(*refs):
        o_ref, off = refs[-1], 0
        for x_ref, w in zip(refs[:-1], widths):
            o_ref[:, off:off + w] = x_ref[...].astype(BF16)
            off += w

    return _pcall(body, name=name, out_shape=jax.ShapeDtypeStruct((s, total), BF16), grid=(s // ts,),
                  in_specs=[pl.BlockSpec((ts, w), lambda i: (i, 0)) for w in widths],
                  out_specs=pl.BlockSpec((ts, total), lambda i: (i, 0)))(*pieces)


def _block_rows(blk):
    if isinstance(blk, int):
        return pl.ds(blk * BLOCK, BLOCK)
    return pl.ds(pl.multiple_of(blk * BLOCK, BLOCK), BLOCK)


def _band_window(n, length):
    width = min(2 * BLOCK, length)
    row = lax.broadcasted_iota(jnp.int32, (BLOCK, width), 0)
    col = lax.broadcasted_iota(jnp.int32, (BLOCK, width), 1)
    if width == BLOCK:
        return pl.ds(0, BLOCK), col <= row
    first = n - 1 if isinstance(n, int) else jnp.maximum(n - 1, 0)
    first = max(first, 0) if isinstance(first, int) else first
    dist = row - col + (n - first) * BLOCK
    start = first * BLOCK if isinstance(first, int) else pl.multiple_of(first * BLOCK, BLOCK)
    return pl.ds(start, width), jnp.logical_and(dist >= 0, dist <= BLOCK)


def _dil_fwd(q_arr, k_arr, v_arr, offs, length, dil, *, name):
    nj, nb = dil * HEADS_PER_GROUP, length // BLOCK
    ju, nq = (HEADS_PER_GROUP, 2) if nb > 1 else (2 * HEADS_PER_GROUP, 1)
    qo, ko, vo = (off // ju for off in offs)
    assert all(off % ju == 0 for off in offs) and nb % nq == 0 and nj % ju == 0

    def body(q_ref, k_ref, v_ref, o_ref, l_ref):
        for qq in range(nq):
            qrows = slice(qq * BLOCK, (qq + 1) * BLOCK)
            rows, mask = _band_window(pl.program_id(1) * nq + qq, length)
            for cb in range(ju):
                lanes = slice(cb * HEAD_DIM, (cb + 1) * HEAD_DIM)
                sc = _dot(q_ref[qrows, lanes].astype(BF16), k_ref[rows, lanes].astype(BF16), NT) * ATT_SCALE
                sc = jnp.where(mask, sc, MASKED)
                m = sc.max(axis=-1, keepdims=True)
                p = jnp.exp(sc - m)
                den = jnp.sum(p, axis=-1, keepdims=True)
                acc = _dot(p.astype(BF16), v_ref[rows, lanes].astype(BF16))
                o_ref[qrows, lanes] = acc / den
                l_ref[qrows, lanes] = jnp.broadcast_to(m + jnp.log(den), (BLOCK, HEAD_DIM))

    qspec = pl.BlockSpec((nq * BLOCK, ju * HEAD_DIM), lambda j, n: (n, qo + j))
    kspec = pl.BlockSpec((length, ju * HEAD_DIM), lambda j, n: (0, ko + j))
    vspec = pl.BlockSpec((length, ju * HEAD_DIM), lambda j, n: (0, vo + j))
    ospec = pl.BlockSpec((nq * BLOCK, ju * HEAD_DIM), lambda j, n: (n, j))
    shp = jax.ShapeDtypeStruct((length, nj * HEAD_DIM), F32)
    return _pcall(body, name=name, out_shape=(shp, shp), grid=(nj // ju, nb // nq), in_specs=[qspec, kspec, vspec],
                  out_specs=(ospec, ospec))(q_arr, k_arr, v_arr)


def _dil_bwd(q_arr, k_arr, v_arr, offs, o, lse, do, dlse, length, dil, *, name):
    nj, nb = dil * HEADS_PER_GROUP, length // BLOCK
    ju = 2 * HEADS_PER_GROUP if length <= 4 * BLOCK else 2
    qo, ko, vo = (off // ju for off in offs)
    assert all(off % ju == 0 for off in offs)

    def body(q_ref, k_ref, v_ref, o_ref, l_ref, do_ref, dl_ref, dq_ref, dk_ref, dv_ref):
        dk_ref[...] = jnp.zeros_like(dk_ref)
        dv_ref[...] = jnp.zeros_like(dv_ref)

        def step(n, carry):
            qrows = _block_rows(n)
            rows, mask = _band_window(n, length)
            for cb in range(ju):
                lanes = slice(cb * HEAD_DIM, (cb + 1) * HEAD_DIM)
                q = q_ref[qrows, lanes].astype(BF16)
                dof = do_ref[qrows, lanes]
                dob = dof.astype(BF16)
                lse_c = l_ref[qrows, lanes][:, :1]
                shift = dl_ref[qrows, lanes][:, :1] - jnp.sum(dof * o_ref[qrows, lanes], axis=-1, keepdims=True)
                kk, vv = k_ref[rows, lanes].astype(BF16), v_ref[rows, lanes].astype(BF16)
                sc = _dot(q, kk, NT) * ATT_SCALE
                p = jnp.where(mask, jnp.exp(sc - lse_c), 0.0)
                ds = (p * (_dot(dob, vv, NT) + shift)).astype(BF16)
                dq_ref[qrows, lanes] = _dot(ds, kk) * ATT_SCALE
                dk_ref[rows, lanes] += _dot(ds, q, TN) * ATT_SCALE
                dv_ref[rows, lanes] += _dot(p.astype(BF16), dob, TN)
            return carry

        if nb == 1:
            step(0, 0)
        else:
            lax.fori_loop(0, nb, step, 0)

    def col(off):
        return pl.BlockSpec((length, ju * HEAD_DIM), lambda j: (0, off + j))

    shp = jax.ShapeDtypeStruct((length, nj * HEAD_DIM), F32)
    return _pcall(body, name=name, out_shape=(shp, shp, shp), grid=(nj // ju,),
                  in_specs=[col(qo), col(ko), col(vo), col(0), col(0), col(0), col(0)],
                  out_specs=(col(0), col(0), col(0)))(q_arr, k_arr, v_arr, o, lse, do, dlse)


DIL_RESIDUES_PER_STEP = 8


def _dil_tokens(n, r, dil, length):
    width = min(2 * BLOCK, length)
    _, mask = _band_window(n, length)
    first = 0 if width == BLOCK else jnp.maximum(n - 1, 0)
    return (pl.ds(n * (BLOCK * dil) + r, BLOCK, stride=dil), pl.ds(first * (BLOCK * dil) + r, width, stride=dil),
            mask)


def _dil_head_specs(seq, group):
    first = group * HEADS_PER_GROUP

    def col(c0):
        return pl.BlockSpec((seq, HEAD_DIM), lambda h, r: (0, c0 + h))

    return col(first), col(A_HEADS + first), col(OFF_VA // HEAD_DIM + first), col(0)


def _dil_fwd_strided(qk32, proj, group, dil, *, name):
    seq = proj.shape[0]
    length = seq // dil
    nb, rp = length // BLOCK, min(dil, DIL_RESIDUES_PER_STEP)
    assert dil % rp == 0

    def body(q_ref, k_ref, v_ref, o_ref, l_ref):
        rgroup = pl.program_id(1)

        def step(n, carry):
            for rr in range(rp):
                tok_q, tok_k, mask = _dil_tokens(n, rgroup * rp + rr, dil, length)
                sc = _dot(q_ref[tok_q, :].astype(BF16), k_ref[tok_k, :].astype(BF16), NT) * ATT_SCALE
                sc = jnp.where(mask, sc, MASKED)
                m = sc.max(axis=-1, keepdims=True)
                p = jnp.exp(sc - m)
                den = jnp.sum(p, axis=-1, keepdims=True)
                o_ref[tok_q, :] = _dot(p.astype(BF16), v_ref[tok_k, :].astype(BF16)) / den
                l_ref[tok_q, :] = jnp.broadcast_to(m + jnp.log(den), (BLOCK, HEAD_DIM))
            return carry

        if nb == 1:
            step(0, 0)
        else:
            lax.fori_loop(0, nb, step, 0)

    qs, ks, vs, nat = _dil_head_specs(seq, group)
    shp = jax.ShapeDtypeStruct((seq, GROUP_W), F32)
    return _pcall(body, name=name, out_shape=(shp, shp), grid=(HEADS_PER_GROUP, dil // rp), in_specs=[qs, ks, vs],
                  out_specs=(nat, nat))(qk32, qk32, proj)


def _dil_bwd_strided(qk32, proj, group, o, lse, do, dlse, dil, *, name):
    seq = proj.shape[0]
    length = seq // dil
    nb, rp = length // BLOCK, min(dil, DIL_RESIDUES_PER_STEP)
    assert dil % rp == 0

    def body(q_ref, k_ref, v_ref, o_ref, l_ref, do_ref, dl_ref, dq_ref, dk_ref, dv_ref):
        rgroup = pl.program_id(1)

        @pl.when(rgroup == 0)
        def _():
            dk_ref[...] = jnp.zeros_like(dk_ref)
            dv_ref[...] = jnp.zeros_like(dv_ref)

        def step(n, carry):
            for rr in range(rp):
                tok_q, tok_k, mask = _dil_tokens(n, rgroup * rp + rr, dil, length)
                q = q_ref[tok_q, :].astype(BF16)
                dof = do_ref[tok_q, :]
                dob = dof.astype(BF16)
                lse_c = l_ref[tok_q, :][:, :1]
                shift = dl_ref[tok_q, :][:, :1] - jnp.sum(dof * o_ref[tok_q, :], axis=-1, keepdims=True)
                kk, vv = k_ref[tok_k, :].astype(BF16), v_ref[tok_k, :].astype(BF16)
                sc = _dot(q, kk, NT) * ATT_SCALE
                p = jnp.where(mask, jnp.exp(sc - lse_c), 0.0)
                ds = (p * (_dot(dob, vv, NT) + shift)).astype(BF16)
                dq_ref[tok_q, :] = _dot(ds, kk) * ATT_SCALE
                dk_ref[tok_k, :] += _dot(ds, q, TN) * ATT_SCALE
                dv_ref[tok_k, :] += _dot(p.astype(BF16), dob, TN)
            return carry

        if nb == 1:
            step(0, 0)
        else:
            lax.fori_loop(0, nb, step, 0)

    qs, ks, vs, nat = _dil_head_specs(seq, group)
    shp = jax.ShapeDtypeStruct((seq, GROUP_W), F32)
    return _pcall(body, name=name, out_shape=(shp, shp, shp), grid=(HEADS_PER_GROUP, dil // rp),
                  in_specs=[qs, ks, vs, nat, nat, nat, nat], out_specs=(nat, nat, nat))(
                      qk32, qk32, proj, o, lse, do, dlse)


def _combine_weights(l_refs):
    ls = [r[...] for r in l_refs]
    m = jnp.maximum(jnp.maximum(ls[0], ls[1]), ls[2])
    es = [jnp.exp(l - m) for l in ls]
    den = es[0] + es[1] + es[2]
    return [e / den for e in es]


def _combine_fwd(os_, lses, *, name):
    s = os_[0].shape[0]
    ts = _rows(s, GROUP_W)

    def body(o0, o1, o2, l0, l1, l2, out_ref):
        w = _combine_weights((l0, l1, l2))
        out_ref[...] = (w[0] * o0[...] + w[1] * o1[...] + w[2] * o2[...]).astype(BF16)

    row = pl.BlockSpec((ts, GROUP_W), lambda i: (i, 0))
    return _pcall(body, name=name, out_shape=jax.ShapeDtypeStruct((s, GROUP_W), BF16), grid=(s // ts,),
                  in_specs=[row] * 6, out_specs=row)(*os_, *lses)


def _combine_bwd(do_a, os_, lses, *, name):
    s = do_a.shape[0]
    ts = _rows(s, GROUP_W)

    def body(d_ref, o0, o1, o2, l0, l1, l2, do0, do1, do2, dl0, dl1, dl2):
        w = _combine_weights((l0, l1, l2))
        d = d_ref[...]
        og = [o0[...], o1[...], o2[...]]
        oa = w[0] * og[0] + w[1] * og[1] + w[2] * og[2]
        ta = jnp.sum(d * oa, axis=-1, keepdims=True)
        for g, (do_ref, dl_ref) in enumerate(((do0, dl0), (do1, dl1), (do2, dl2))):
            do_ref[...] = w[g] * d
            dl_ref[...] = w[g] * (jnp.sum(d * og[g], axis=-1, keepdims=True) - ta)

    head = pl.BlockSpec((ts, HEAD_DIM), lambda i, h: (i, h))
    shp = jax.ShapeDtypeStruct((s, GROUP_W), F32)
    return _pcall(body, name=name, out_shape=(shp,) * 6, grid=(s // ts, HEADS_PER_GROUP),
                  in_specs=[head] * 7, out_specs=(head,) * 6)(do_a, *os_, *lses)


def _dot_exact(x, ones_mask):
    hi = x.astype(BF16)
    r1 = x - hi.astype(F32)
    mid = r1.astype(BF16)
    lo = (r1 - mid.astype(F32)).astype(BF16)
    return _dot(hi, ones_mask) + _dot(mid, ones_mask) + _dot(lo, ones_mask)


SB_QROWS = 4 * BLOCK
SB_UNROLL = 4
SB_HEADS_PER_STEP = 2
SB_LANES = [slice(hh * HEAD_DIM, (hh + 1) * HEAD_DIM) for hh in range(SB_HEADS_PER_STEP)]


def _sb_mask(j, i):
    row = lax.broadcasted_iota(jnp.int32, (SB_QROWS, BLOCK), 0)
    col = lax.broadcasted_iota(jnp.int32, (SB_QROWS, BLOCK), 1)
    return col + (j * BLOCK - i * SB_QROWS) < row


def _sb_steps(i):
    return ((i + 1) * (SB_QROWS // BLOCK) + SB_UNROLL - 1) // SB_UNROLL


def _sb_scores(q, kk, j, i, masked):
    mask = _sb_mask(j, i) if masked else None
    z = _dot(q, kk, NT) * ATT_SCALE
    sp = jnp.log(1.0 + jnp.exp(-jnp.abs(z)))
    log_beta = jnp.minimum(z, 0.0) - sp
    log_1mb = jnp.minimum(-z, 0.0) - sp
    if masked:
        log_1mb = jnp.where(mask, log_1mb, 0.0)
    return z, log_beta, log_1mb, mask


def _sb_weights(log_beta, log_1mb, mask, run, upper):
    a = jnp.exp(log_beta + (run + _dot_exact(log_1mb, upper)))
    return a if mask is None else jnp.where(mask, a, 0.0)


def _sb_peeled(nsteps, make_step, init, masked_first):
    if masked_first:
        return lax.fori_loop(1, nsteps, make_step(False), make_step(True)(0, init))
    return make_step(True)(nsteps - 1, lax.fori_loop(0, nsteps - 1, make_step(False), init))


def _tri(strict_lower):
    row = lax.broadcasted_iota(jnp.int32, (BLOCK, BLOCK), 0)
    col = lax.broadcasted_iota(jnp.int32, (BLOCK, BLOCK), 1)
    return ((row > col) if strict_lower else (row < col)).astype(BF16)


def _sb_fwd(proj, *, name):
    s = proj.shape[0]
    assert s % (BLOCK * SB_UNROLL) == 0 and s % SB_QROWS == 0

    def body(q_ref, k_ref, v_ref, o_ref):
        i = pl.program_id(1)
        qs = [q_ref[:, lanes].astype(BF16) for lanes in SB_LANES]
        upper = _tri(True)
        nsteps = _sb_steps(i)

        def make_step(masked):
            def step(t, carry):
                carry = list(carry)
                for b in reversed(range(SB_UNROLL)):
                    j = (nsteps - 1 - t) * SB_UNROLL + b
                    rows = _block_rows(j)
                    for hh, lanes in enumerate(SB_LANES):
                        acc, run = carry[hh]
                        _, log_beta, log_1mb, mask = _sb_scores(qs[hh], k_ref[rows, lanes].astype(BF16), j, i, masked)
                        a = _sb_weights(log_beta, log_1mb, mask, run, upper)
                        carry[hh] = (acc + _dot(a.astype(BF16), v_ref[rows, lanes].astype(BF16)),
                                     run + jnp.sum(log_1mb, axis=-1, keepdims=True))
                return tuple(carry)
            return step

        zero = (jnp.zeros((SB_QROWS, HEAD_DIM), F32), jnp.zeros((SB_QROWS, 1), F32))
        for lanes, (acc, _) in zip(SB_LANES, _sb_peeled(nsteps, make_step, (zero,) * SB_HEADS_PER_STEP, True)):
            o_ref[:, lanes] = acc.astype(BF16)

    width = SB_HEADS_PER_STEP * HEAD_DIM
    qb, kb, vb = (off // width for off in (OFF_QB, OFF_KB, OFF_VB))
    return _pcall(body, name=name, out_shape=jax.ShapeDtypeStruct((s, B_W), BF16),
                  grid=(SB_HEADS // SB_HEADS_PER_STEP, s // SB_QROWS),
                  in_specs=[pl.BlockSpec((SB_QROWS, width), lambda h, i: (i, qb + h)),
                            pl.BlockSpec((s, width), lambda h, i: (0, kb + h)),
                            pl.BlockSpec((s, width), lambda h, i: (0, vb + h))],
                  out_specs=pl.BlockSpec((SB_QROWS, width), lambda h, i: (i, h)))(proj, proj, proj)


def _sb_bwd(proj, do_b, *, name):
    s = proj.shape[0]
    assert s % (BLOCK * SB_UNROLL) == 0 and s % SB_QROWS == 0
    nkb = s // BLOCK

    def body(q_ref, k_ref, v_ref, do_ref, dq_ref, dk_ref, dv_ref, z_s, a_s):
        i = pl.program_id(1)

        @pl.when(i == 0)
        def _():
            dk_ref[...] = jnp.zeros_like(dk_ref)
            dv_ref[...] = jnp.zeros_like(dv_ref)

        qs = [q_ref[:, lanes].astype(BF16) for lanes in SB_LANES]
        dobs = [do_ref[:, lanes].astype(BF16) for lanes in SB_LANES]
        upper, lower = _tri(True), _tri(False)
        nsteps = _sb_steps(i)

        def make_recompute(masked):
            def recompute(t, runs):
                runs = list(runs)
                for b in reversed(range(SB_UNROLL)):
                    j = (nsteps - 1 - t) * SB_UNROLL + b
                    rows = _block_rows(j)
                    for hh, lanes in enumerate(SB_LANES):
                        z, log_beta, log_1mb, mask = _sb_scores(qs[hh], k_ref[rows, lanes].astype(BF16), j, i, masked)
                        z_s[hh, j] = z
                        a_s[hh, j] = _sb_weights(log_beta, log_1mb, mask, runs[hh], upper)
                        runs[hh] = runs[hh] + jnp.sum(log_1mb, axis=-1, keepdims=True)
                return tuple(runs)
            return recompute

        _sb_peeled(nsteps, make_recompute, (jnp.zeros((SB_QROWS, 1), F32),) * SB_HEADS_PER_STEP, True)

        def make_grads(masked):
            def grads(t, carry):
                carry = list(carry)
                for b in range(SB_UNROLL):
                    j = t * SB_UNROLL + b
                    rows = _block_rows(j)
                    for hh, lanes in enumerate(SB_LANES):
                        dq, run = carry[hh]
                        kk, vv = k_ref[rows, lanes].astype(BF16), v_ref[rows, lanes].astype(BF16)
                        z, a = z_s[hh, j], a_s[hh, j]
                        de = _dot(dobs[hh], vv, NT) * a
                        beta = jax.nn.sigmoid(z)
                        one_minus_beta = 1.0 - beta
                        if masked:
                            beta = jnp.where(_sb_mask(j, i), beta, 0.0)
                        dz = (de * one_minus_beta - beta * (run + _dot_exact(de, lower))).astype(BF16)
                        dk_ref[rows, lanes] += _dot(dz, qs[hh], TN) * ATT_SCALE
                        dv_ref[rows, lanes] += _dot(a.astype(BF16), dobs[hh], TN)
                        carry[hh] = (dq + _dot(dz, kk), run + jnp.sum(de, axis=-1, keepdims=True))
                return tuple(carry)
            return grads

        zero = (jnp.zeros((SB_QROWS, HEAD_DIM), F32), jnp.zeros((SB_QROWS, 1), F32))
        for lanes, (dq, _) in zip(SB_LANES, _sb_peeled(nsteps, make_grads, (zero,) * SB_HEADS_PER_STEP, False)):
            dq_ref[:, lanes] = dq * ATT_SCALE

    width = SB_HEADS_PER_STEP * HEAD_DIM
    qb, kb, vb = (off // width for off in (OFF_QB, OFF_KB, OFF_VB))
    blk = pl.BlockSpec((SB_QROWS, width), lambda h, i: (i, h))
    full = pl.BlockSpec((s, width), lambda h, i: (0, h))
    shp = jax.ShapeDtypeStruct((s, B_W), F32)
    saved = pltpu.VMEM((SB_HEADS_PER_STEP, nkb, SB_QROWS, BLOCK), F32)
    return _pcall(body, name=name, out_shape=(shp, shp, shp), grid=(SB_HEADS // SB_HEADS_PER_STEP, s // SB_QROWS),
                  in_specs=[pl.BlockSpec((SB_QROWS, width), lambda h, i: (i, qb + h)),
                            pl.BlockSpec((s, width), lambda h, i: (0, kb + h)),
                            pl.BlockSpec((s, width), lambda h, i: (0, vb + h)), blk],
                  out_specs=(blk, full, full), scratch=[saved, saved])(proj, proj, proj, do_b)


def _coords():
    return lax.axis_index("x"), lax.axis_index("y"), lax.axis_index("c")


def _flip(v, bit):
    return 1 - v if bit else v


def _shard_of(ref, axis, idx, size):
    if axis == 0:
        sl = pl.ds(pl.multiple_of(idx * size, 16), size)
        return ref.at[sl, :] if len(ref.shape) == 2 else ref.at[:, sl, :]
    sl = pl.ds(pl.multiple_of(idx * size, 128), size)
    return ref.at[:, sl] if len(ref.shape) == 2 else ref.at[:, :, sl]


def _small_allgather(v, *, name, silu=False):
    n = v.shape[1]

    def body(v_ref, out_ref, send_sems, recv_sems):
        x, y, c = _coords()
        me = 4 * x + 2 * y + c
        val = v_ref[...]
        out_ref[me] = val * jax.nn.sigmoid(val) if silu else val
        copies = []
        for k in range(1, N_DEV):
            peer = (_flip(x, k & 4), _flip(y, k & 2), _flip(c, k & 1))
            copies.append(pltpu.make_async_remote_copy(
                src_ref=out_ref.at[me], dst_ref=out_ref.at[me], send_sem=send_sems.at[k - 1],
                recv_sem=recv_sems.at[k - 1], device_id=peer, device_id_type=MESH))
        for cp in copies:
            cp.start()
        for cp in copies:
            cp.wait_recv()
        for cp in copies:
            cp.wait_send()

    return _pcall(body, name=name, out_shape=jax.ShapeDtypeStruct((N_DEV, 1, n), F32),
                  in_specs=[pl.BlockSpec(memory_space=pltpu.VMEM)], out_specs=pl.BlockSpec(memory_space=pltpu.VMEM),
                  scratch=[pltpu.SemaphoreType.DMA((N_DEV - 1,)), pltpu.SemaphoreType.DMA((N_DEV - 1,))])(v)


def _cast_place(w, layer, axis, me, *, name):
    _, r, c = w.shape
    tr = _rows(r, c)
    nrt = r // tr

    def body(me_ref, w_ref, o_ref):
        o_ref[...] = w_ref[...].astype(BF16)

    wspec = pl.BlockSpec((None, tr, c), lambda i, me_ref: (layer, i, 0))
    if axis == 0:
        ospec = pl.BlockSpec((tr, c), lambda i, me_ref: (me_ref[0] * nrt + i, 0))
        shape = (r * N_DEV, c)
    else:
        ospec = pl.BlockSpec((tr, c), lambda i, me_ref: (i, me_ref[0]))
        shape = (r, c * N_DEV)
    return _pcall(body, name=name, out_shape=jax.ShapeDtypeStruct(shape, BF16), grid=(nrt,), in_specs=[wspec],
                  out_specs=ospec, prefetch=1)(me, w)


def _pair_sum(grad, sib, core, axis, *, name):
    _, r, c = sib.shape
    tr = _rows(r, c // 2)
    nrt = r // tr

    def body(core_ref, g_ref, s_ref, o_ref):
        o_ref[...] = (g_ref[...].astype(F32) + s_ref[...].astype(F32)).astype(BF16)

    if axis == 0:
        gspec = pl.BlockSpec((tr, c), lambda q, i, core_ref: ((2 * q + core_ref[0]) * nrt + i, 0))
    else:
        gspec = pl.BlockSpec((tr, c), lambda q, i, core_ref: (i, 2 * q + core_ref[0]))
    sspec = pl.BlockSpec((None, tr, c), lambda q, i, core_ref: (q, i, 0))
    return _pcall(body, name=name, out_shape=jax.ShapeDtypeStruct(sib.shape, BF16), grid=(N_CHIPS, nrt),
                  in_specs=[gspec, sspec], out_specs=sspec, prefetch=1)(core, grad, sib)


ANY_SPEC = pl.BlockSpec(memory_space=pl.ANY)
SEM_SPEC = pl.BlockSpec(memory_space=pltpu.SEMAPHORE)
SPLIT_PARAMS = dict(has_side_effects=pltpu.SideEffectType.DATAFLOW_SIDE_EFFECTING)


def _split_start(copies_fn, buffers, sem_shape, after, *, name):
    n = len(buffers)
    rows, cols = sem_shape
    ns = rows * cols
    extra = ([] if after is None else [after]) + _take_token()

    def body(*refs):
        sems = refs[n + len(extra):n + len(extra) + 2 * ns]
        for cp in copies_fn(refs[:n], _sem_rows(sems[:ns], cols), _sem_rows(sems[ns:], cols)):
            cp.start()
        refs[-1][...] = jnp.zeros_like(refs[-1])

    sem = pltpu.SemaphoreType.DMA(())
    outs = pl.pallas_call(
        body, name=name,
        out_shape=((sem,) * (2 * ns) + tuple(jax.ShapeDtypeStruct(b.shape, b.dtype) for b in buffers) + (TOKEN,)),
        in_specs=(ANY_SPEC,) * (n + len(extra)),
        out_specs=(SEM_SPEC,) * (2 * ns) + (ANY_SPEC,) * n + (pl.BlockSpec(memory_space=pltpu.VMEM),),
        input_output_aliases={i: 2 * ns + i for i in range(n)},
        compiler_params=pltpu.CompilerParams(**SPLIT_PARAMS))(*buffers, *extra)
    _ORDER["token"] = outs[-1]
    return list(outs[:ns]), list(outs[ns:2 * ns]), list(outs[2 * ns:2 * ns + n]), outs[-1]


def _split_wait(copies_fn, send_sems, recv_sems, buffers, after, sem_rows, *, name):
    n, ns = len(buffers), len(send_sems)
    cols = ns // sem_rows
    extra = ([] if after is None else [after]) + _take_token()

    def body(*refs):
        sems = refs[n:n + 2 * ns]
        copies = copies_fn(refs[:n], _sem_rows(sems[:ns], cols), _sem_rows(sems[ns:], cols))
        for cp in copies:
            cp.wait_send()
        for cp in copies:
            cp.wait_recv()
        refs[-1][...] = jnp.zeros_like(refs[-1])

    outs = pl.pallas_call(
        body, name=name, out_shape=tuple(jax.ShapeDtypeStruct(b.shape, b.dtype) for b in buffers) + (TOKEN,),
        in_specs=(ANY_SPEC,) * n + (SEM_SPEC,) * (2 * ns) + (ANY_SPEC,) * len(extra),
        out_specs=(ANY_SPEC,) * n + (pl.BlockSpec(memory_space=pltpu.VMEM),),
        input_output_aliases={i: i for i in range(n)},
        compiler_params=pltpu.CompilerParams(**SPLIT_PARAMS))(*buffers, *send_sems, *recv_sems, *extra)
    _ORDER["token"] = outs[-1]
    return list(outs[:n])


def _sem_rows(sems, cols):
    return [sems[i:i + cols] for i in range(0, len(sems), cols)]


def _empty_hbm(shape, dtype):
    return pltpu.with_memory_space_constraint(lax.empty(shape, dtype), pltpu.HBM)


class _SplitGather:
    def __init__(self, fulls, axes, tag):
        self.axes, self.tag, self.nt = list(axes), tag, len(fulls)
        self.sizes = [f.shape[ax] // N_DEV for f, ax in zip(fulls, axes)]
        self.fulls = list(fulls)

    def _slot(self, ref, t, dev):
        return _shard_of(ref, self.axes[t], 4 * dev[0] + 2 * dev[1] + dev[2], self.sizes[t])

    def _first_copies(self, refs, send_sems, recv_sems):
        x, y, c = _coords()
        peers = [(x, y, 1 - c), (1 - x, y, c), (x, 1 - y, c), (1 - x, 1 - y, c)]
        return [pltpu.make_async_remote_copy(
            src_ref=self._slot(refs[t], t, (x, y, c)), dst_ref=self._slot(refs[t], t, (x, y, c)),
            send_sem=send_sems[t][k], recv_sem=recv_sems[t][k], device_id=peer, device_id_type=MESH)
            for t in range(self.nt) for k, peer in enumerate(peers)]

    def _forward_copies(self, refs, send_sems, recv_sems):
        x, y, c = _coords()
        chips = [(1 - x, y), (x, 1 - y), (1 - x, 1 - y)]
        return [pltpu.make_async_remote_copy(
            src_ref=self._slot(refs[t], t, (*chip, c)), dst_ref=self._slot(refs[t], t, (*chip, c)),
            send_sem=send_sems[t][j], recv_sem=recv_sems[t][j], device_id=(x, y, 1 - c), device_id_type=MESH)
            for t in range(self.nt) for j, chip in enumerate(chips)]

    def first(self, after):
        self.s1, self.r1, self.fulls, token = _split_start(
            self._first_copies, self.fulls, (self.nt, 4), after, name=f"comm_gather1_start_{self.tag}")
        return token

    def forward(self, after):
        bufs = _split_wait(self._first_copies, self.s1, self.r1, self.fulls, after, self.nt,
                           name=f"comm_gather1_wait_{self.tag}")
        self.s2, self.r2, self.fulls, token = _split_start(
            self._forward_copies, bufs, (self.nt, 3), after, name=f"comm_gather2_start_{self.tag}")
        return token

    def finish(self, after):
        return _split_wait(self._forward_copies, self.s2, self.r2, self.fulls, after, self.nt,
                           name=f"comm_gather2_wait_{self.tag}")


class _SplitPairExchange:
    def __init__(self, grads, axes, tag):
        self.nt, self.tag, self.axes = len(grads), tag, list(axes)
        self.grads = list(grads)
        self.sizes = [g.shape[ax] // N_DEV for g, ax in zip(grads, axes)]

    def _copies(self, refs, send_sems, recv_sems):
        nt = self.nt
        x, y, c = _coords()
        return [pltpu.make_async_remote_copy(
            src_ref=_shard_of(refs[t], self.axes[t], 2 * q + 1 - c, self.sizes[t]), dst_ref=refs[nt + t].at[q],
            send_sem=send_sems[t][q], recv_sem=recv_sems[t][q], device_id=(x, y, 1 - c), device_id_type=MESH)
            for t in range(nt) for q in range(N_CHIPS)]

    def start(self):
        landing = []
        for g, ax in zip(self.grads, self.axes):
            dims = list(g.shape)
            dims[ax] //= N_DEV
            landing.append(_empty_hbm((N_CHIPS, *dims), g.dtype))
        self.s, self.r, self.bufs, token = _split_start(
            self._copies, self.grads + landing, (self.nt, N_CHIPS), None,
            name=f"comm_rs_pair_start_{self.tag}")
        return token

    def finish(self, after):
        bufs = _split_wait(self._copies, self.s, self.r, self.bufs, after, self.nt,
                           name=f"comm_rs_pair_wait_{self.tag}")
        return bufs[:self.nt], bufs[self.nt:]


class _ReducePipeline:
    def __init__(self, core):
        self.core, self.items, self.done, self.now = core, [], [], 0

    def add(self, keys, grads, layer):
        axes = [SHARD_AXIS[k] for k in keys]
        pair = _SplitPairExchange([grads[k] for k in keys], axes, f"{keys[0]}{layer}")
        pair.start()
        self.items.append(dict(keys=keys, layer=layer, axes=axes, pair=pair, state="pair", since=self.now))

    def tick(self, after, flush=False):
        self.now += 1
        for it in self.items:
            if it["state"] == "pair" and it["since"] < self.now:
                grads, sib = it["pair"].finish(after)
                sums = [_pair_sum(g, s_, self.core, ax, name="pair_sum_" + k)
                        for k, g, s_, ax in zip(it["keys"], grads, sib, it["axes"])]
                it["chip"] = _SplitChipExchange(sums, f"{it['keys'][0]}{it['layer']}")
                it["chip"].start()
                it.update(state="chip", since=self.now)
            elif it["state"] == "chip" and (flush or self.now - it["since"] >= 2):
                sums, remote = it["chip"].finish(after)
                self.done.append((it["keys"], it["layer"], sums, remote))
                it["state"] = "done"

    def take_done(self):
        out, self.done = self.done, []
        return out


class _SplitChipExchange:
    def __init__(self, sums, tag):
        self.nt, self.tag = len(sums), tag
        self.sums = list(sums)

    def _copies(self, refs, send_sems, recv_sems):
        nt = self.nt
        x, y, c = _coords()
        copies = []
        for t in range(nt):
            for k in range(1, N_CHIPS):
                px, py = _flip(x, k & 2), _flip(y, k & 1)
                copies.append(pltpu.make_async_remote_copy(
                    src_ref=refs[t].at[2 * px + py], dst_ref=refs[nt + t].at[k - 1], send_sem=send_sems[t][k - 1],
                    recv_sem=recv_sems[t][k - 1], device_id=(px, py, c), device_id_type=MESH))
        return copies

    def start(self):
        landing = [_empty_hbm((N_CHIPS - 1,) + s.shape[1:], s.dtype) for s in self.sums]
        self.s, self.r, self.bufs, token = _split_start(
            self._copies, self.sums + landing, (self.nt, N_CHIPS - 1), None,
            name=f"comm_rs_chip_start_{self.tag}")
        return token

    def finish(self, after):
        bufs = _split_wait(self._copies, self.s, self.r, self.bufs, after, self.nt,
                           name=f"comm_rs_chip_wait_{self.tag}")
        return bufs[:self.nt], bufs[self.nt:]


def _adam_math(g, w, m, v):
    m2 = ADAM_B1 * m + (1.0 - ADAM_B1) * g
    v2 = ADAM_B2 * v + (1.0 - ADAM_B2) * (g * g)
    m_hat = m2 / (1.0 - ADAM_B1 ** ADAM_STEP)
    v_hat = v2 / (1.0 - ADAM_B2 ** ADAM_STEP)
    delta = -ADAM_LR * (m_hat / (jnp.sqrt(v_hat) + ADAM_EPS) + ADAM_WD * w)
    return delta, m2, v2


def _adamw_sharded(chip_sums, remote, chip, w, m, v, layer, prev, *, name):
    nl, r, c = w.shape
    tr = _rows(r, c)

    def body(*refs):
        p_ref, r0_ref, r1_ref, r2_ref, w_ref, m_ref, v_ref = refs[1:8]
        g_out, d_out, m_out, v_out = refs[-4:]
        g = ((p_ref[...].astype(F32) + r0_ref[...].astype(F32)) + r1_ref[...].astype(F32)) + r2_ref[...].astype(F32)
        g_out[...] = g
        d_out[...], m_out[...], v_out[...] = _adam_math(g, w_ref[...], m_ref[...], v_ref[...])

    pspec = pl.BlockSpec((None, tr, c), lambda i, chip_ref: (chip_ref[0], i, 0))

    def rspec(k):
        return pl.BlockSpec((None, tr, c), lambda i, chip_ref: (k, i, 0))

    wspec = pl.BlockSpec((None, tr, c), lambda i, chip_ref: (layer, i, 0))
    in_specs = [pspec, rspec(0), rspec(1), rspec(2), wspec, wspec, wspec]
    args = [chip, chip_sums, remote, remote, remote, w, m, v]
    aliases = {}
    if prev is not None:
        in_specs += [pl.BlockSpec(memory_space=pl.ANY)] * 4
        aliases = {len(args) + i: i for i in range(4)}
        args += list(prev)
    shp = jax.ShapeDtypeStruct(w.shape, F32)
    return _pcall(body, name=name, out_shape=(shp,) * 4, grid=(r // tr,), in_specs=in_specs, out_specs=(wspec,) * 4,
                  aliases=aliases, prefetch=1)(*args)


def _adamw_local(g, w, m, v, *, name):
    nl, r, c = w.shape
    tr = _rows(r, c)

    def body(g_ref, w_ref, m_ref, v_ref, d_out, m_out, v_out):
        d_out[...], m_out[...], v_out[...] = _adam_math(g_ref[...], w_ref[...], m_ref[...], v_ref[...])

    spec = pl.BlockSpec((None, tr, c), lambda l, i: (l, i, 0))
    shp = jax.ShapeDtypeStruct(w.shape, F32)
    return _pcall(body, name=name, out_shape=(shp,) * 3, grid=(nl, r // tr), in_specs=[spec] * 4,
                  out_specs=(spec,) * 3)(g, w, m, v)


def _adamw_replicated(parts, w, m, v, *, name):
    n = w.shape[1]

    def body(p_ref, w_ref, m_ref, v_ref, g_out, d_out, m_out, v_out):
        g = p_ref[0]
        for k in range(1, N_DEV):
            g = g + p_ref[k]
        g_out[...] = g
        d_out[...], m_out[...], v_out[...] = _adam_math(g, w_ref[...], m_ref[...], v_ref[...])

    vm = pl.BlockSpec(memory_space=pltpu.VMEM)
    shp = jax.ShapeDtypeStruct((1, n), F32)
    return _pcall(body, name=name, out_shape=(shp,) * 4, in_specs=[vm] * 4, out_specs=(vm,) * 4)(parts, w, m, v)


UNDILATED_OFFS = (0, A_HEADS, OFF_VA // HEAD_DIM)


def _mod_rows(mod, d):
    return [mod[:, i * d:(i + 1) * d] for i in range(6)]


MIXER_W = ("w_in", "w_branch_a", "w_branch_b", "w_out")
FFN_W = ("w_gate_up", "w_down")
SHARD_AXIS = {"w_in": 1, "w_branch_a": 1, "w_branch_b": 1, "w_out": 0, "w_gate_up": 1, "w_down": 0}


def _norm_args(mod, gain, which, d):
    rows = _mod_rows(mod, d)
    return gain, rows[3 * which + 1], rows[3 * which]


def _mixer_fwd_a(h, u, gains, w_in, cos2, sin2, hook):
    seq = h.shape[0]
    proj = _mm(u, w_in, name="mm_in")
    hook(proj)
    qk, qk32 = _qkrope_fwd(proj, gains, cos2, sin2, name="qkrope_fwd")
    os_, lses = [], []
    for g, dil in enumerate(DILATIONS):
        if dil == 1:
            o, lse = _dil_fwd(qk, qk, proj, UNDILATED_OFFS, seq, 1, name="dil_fwd_1")
        else:
            o, lse = _dil_fwd_strided(qk32, proj, g, dil, name=f"dil_fwd_{dil}")
        os_.append(o)
        lses.append(lse)
    o_a = _combine_fwd(os_, lses, name="combine_fwd")
    o_b = _sb_fwd(proj, name="sb_fwd")
    return dict(h_in=h, u=u, proj=proj, qk=qk, qk32=qk32, os=os_, lses=lses, o_a=o_a, o_b=o_b)


def _mixer_fwd_b(sv, mod, g2, wts):
    d = sv["h_in"].shape[1]
    merged, y_a, y_b = _mm_merge(sv["o_a"], sv["o_b"], wts["w_branch_a"], wts["w_branch_b"], sv["proj"],
                                 name="mm_branch")
    h_mid, t, u2 = _mm_resid_norm(merged, wts["w_out"], sv["h_in"], _mod_rows(mod, d)[2], _norm_args(mod, g2, 1, d),
                                  name="mm_out")
    sv.update(y_a=y_a, y_b=y_b, merged=merged, t=t, h_mid=h_mid, u2=u2)
    return h_mid


def _ffn_fwd_a(sv, w_gate_up):
    a, g, u = _mm_swiglu(sv["u2"], w_gate_up, name="mm_gate_up")
    sv.update(g=g, up=u, a=a)
    return a


def _ffn_fwd_b(sv, mod, w_down, next_norm):
    d = sv["h_mid"].shape[1]
    h_out, sv["f"], u_next = _mm_resid_norm(sv["a"], w_down, sv["h_mid"], _mod_rows(mod, d)[5], next_norm,
                                            name="mm_down")
    return h_out, u_next


def _wgrad(act, dout, key):
    return _mm(act, dout, ta=True, out_dtype=BF16, caps=(2048, 1024, 3072), name="mm_wgrad_" + key)


def _ffn_bwd(dh, df, dgate2, sv, mod, g2, wts, hook):
    d = dh.shape[1]
    sc2, ga1 = _mod_rows(mod, d)[4], _mod_rows(mod, d)[2]
    dg, dup = _mm_down_t_swiglu(df, wts["w_down"], sv["g"], sv["up"], name="mm_down_t")
    grads = {"w_down": _wgrad(sv["a"], df, "w_down")}
    hook(dup)
    du2 = _mm_cat_k(dg, dup, wts["w_gate_up"], name="mm_gate_up_t")
    grads["w_gate_up"] = _mm_cat_n(sv["u2"], dg, dup, name="mm_wgrad_w_gate_up")
    dh_mid, dsh2, dsc2, dg2, dt, dgate1 = _rmsmod_bwd(du2, sv["h_mid"], g2, sc2, dh, sv["t"], ga1, name="rmsmod_bwd")
    return dh_mid, [dsh2, dsc2, dgate2], dg2, grads, dt, dgate1


def _mixer_bwd(dh_mid, dt, dgate1, sv, mod, g1, gains, wts, cos2, sin2, hook, below):
    seq, d = dh_mid.shape
    sc1 = _mod_rows(mod, d)[1]
    dy_a, dy_b, dga, dgb = _mm_out_t_merge(dt, wts["w_out"], sv["proj"], sv["y_a"], sv["y_b"], name="mm_out_t")
    grads = {"w_out": _wgrad(sv["merged"], dt, "w_out")}
    do_a = _mm(dy_a, wts["w_branch_a"], tb=True, name="mm_branch_t")
    do_b = _mm(dy_b, wts["w_branch_b"], tb=True, name="mm_branch_t")
    grads["w_branch_a"] = _wgrad(sv["o_a"], dy_a, "w_branch_a")
    grads["w_branch_b"] = _wgrad(sv["o_b"], dy_b, "w_branch_b")
    dqb, dkb, dvb = _sb_bwd(sv["proj"], do_b, name="sb_bwd")
    hook(dqb, grads)
    comb = _combine_bwd(do_a, sv["os"], sv["lses"], name="combine_bwd")
    grads = {}
    dos, dls = comb[:3], comb[3:]
    dqs, dks, dvs = [], [], []
    for g, dil in enumerate(DILATIONS):
        if dil == 1:
            dq, dk, dv = _dil_bwd(sv["qk"], sv["qk"], sv["proj"], UNDILATED_OFFS, sv["os"][g], sv["lses"][g], dos[g],
                                  dls[g], seq, 1, name="dil_bwd_1")
        else:
            dq, dk, dv = _dil_bwd_strided(sv["qk32"], sv["proj"], g, sv["os"][g], sv["lses"][g], dos[g], dls[g], dil,
                                          name=f"dil_bwd_{dil}")
        dqs.append(dq)
        dks.append(dk)
        dvs.append(dv)
    dq_pre, dqn = _qkrope_bwd(dqs, sv["proj"], gains, 0, cos2, sin2, name="qkrope_bwd")
    dk_pre, dkn = _qkrope_bwd(dks, sv["proj"], gains, 1, cos2, sin2, name="qkrope_bwd")
    dgains = jnp.stack([dqn, dkn])
    dproj = _assemble([dq_pre, dk_pre] + dvs + [dqb, dkb, dvb, dga, dgb], name="assemble_dproj")
    du = _mm(dproj, wts["w_in"], tb=True, name="mm_in_t")
    grads["w_in"] = _wgrad(sv["u"], dproj, "w_in")
    dh_in, dsh1, dsc1, dg1, df, dgate2 = _rmsmod_bwd(du, sv["h_in"], g1, sc1, dh_mid, *(below or (None, None)),
                                                     name="rmsmod_bwd")
    return dh_in, [dsh1, dsc1, dgate1], dg1, dgains, grads, df, dgate2


def kernel(x, c, w_ada, b_ada, norm1_g, norm2_g, w_in, qn_g, kn_g, w_branch_a, w_branch_b, w_out, w_gate_up, w_down, loss_target, m_w_ada, m_b_ada, m_norm1_g, m_norm2_g, m_w_in, m_qn_g, m_kn_g, m_w_branch_a, m_w_branch_b, m_w_out, m_w_gate_up, m_w_down, v_w_ada, v_b_ada, v_norm1_g, v_norm2_g, v_w_in, v_qn_g, v_kn_g, v_w_branch_a, v_w_branch_b, v_w_out, v_w_gate_up, v_w_down):
    _ORDER["token"] = None
    seq, d = x.shape[1], x.shape[2]
    depth = w_in.shape[0]
    weights = dict(w_in=w_in, w_branch_a=w_branch_a, w_branch_b=w_branch_b, w_out=w_out, w_gate_up=w_gate_up,
                   w_down=w_down)
    moments_m = dict(w_in=m_w_in, w_branch_a=m_w_branch_a, w_branch_b=m_w_branch_b, w_out=m_w_out,
                     w_gate_up=m_w_gate_up, w_down=m_w_down)
    moments_v = dict(w_in=v_w_in, w_branch_a=v_w_branch_a, w_branch_b=v_w_branch_b, w_out=v_w_out,
                     w_gate_up=v_w_gate_up, w_down=v_w_down)
    xi, yi, ci = _coords()
    me = 4 * xi + 2 * yi + ci
    core = jnp.reshape(ci, (1,)).astype(jnp.int32)
    chip = jnp.reshape(2 * xi + yi, (1,)).astype(jnp.int32)

    ada_w = w_ada.shape[2]
    c_act = _small_allgather(c, name="comm_gather_c", silu=True).reshape(N_DEV, d)
    c_pad = jnp.concatenate([c_act, jnp.zeros_like(c_act)], axis=0).astype(BF16)
    bias = lax.dynamic_slice(b_ada, (0, me * ada_w), (depth, ada_w))
    mod_part = jnp.stack([_mm(c_pad, w_ada[l], name="mm_ada")[:N_DEV] for l in range(depth)]) + bias[:, None, :]
    mod_all = _small_allgather(mod_part.reshape(1, depth * N_DEV * ada_w), name="comm_gather_mod")
    mod_all = mod_all.reshape(N_DEV, depth, N_DEV, ada_w)
    mod_mine = lax.dynamic_index_in_dim(mod_all, me, axis=2, keepdims=False)
    mods = jnp.transpose(mod_mine, (1, 0, 2)).reshape(depth, 1, 6 * d)

    cos2, sin2 = _rope_tables(seq)
    gains = [jnp.stack([qn_g[l], kn_g[l]])[:, None, :] for l in range(depth)]
    g1s = [norm1_g[l][None] for l in range(depth)]
    g2s = [norm2_g[l][None] for l in range(depth)]

    me_arr = jnp.reshape(me, (1,)).astype(jnp.int32)

    def placed(keys, l):
        return [_cast_place(weights[k], l, SHARD_AXIS[k], me_arr, name="cast_place_" + k) for k in keys]

    def gather_of(keys, l, tag):
        return _SplitGather(placed(keys, l), [SHARD_AXIS[k] for k in keys], f"{tag}{l}")

    groups = []
    for l in range(depth):
        groups += [("w_in", l, MIXER_W[:1]), ("rest", l, MIXER_W[1:]), ("up", l, FFN_W[:1]), ("down", l, FFN_W[1:])]
    gathers = {}

    def issue(some):
        for tag, l, keys in some:
            gathers[tag, l] = gather_of(keys, l, tag)
            gathers[tag, l].first(after=mods)

    issue(groups[:4])
    h = x[0]
    u = _rmsmod_fwd(h, *_norm_args(mods[0], g1s[0], 0, d), name="rmsmod_fwd")
    gathers["w_in", 0].forward(after=u)
    issue(groups[4:])
    wm = {"w_in": gathers["w_in", 0].finish(after=u)[0]}
    saved, full = [], []
    for l in range(depth):
        last = l + 1 == depth
        sv = _mixer_fwd_a(h, u, gains[l], wm["w_in"], cos2, sin2, gathers["rest", l].forward)
        gathers["up", l].forward(after=sv["o_b"])
        wm.update(zip(MIXER_W[1:], gathers["rest", l].finish(after=sv["o_b"])))
        h_mid = _mixer_fwd_b(sv, mods[l], g2s[l], wm)
        wf = {"w_gate_up": gathers["up", l].finish(after=h_mid)[0]}
        gathers["down", l].forward(after=h_mid)
        a = _ffn_fwd_a(sv, wf["w_gate_up"])
        wf["w_down"] = gathers["down", l].finish(after=a)[0]
        if not last:
            gathers["w_in", l + 1].forward(after=a)
        h, u = _ffn_fwd_b(sv, mods[l], wf["w_down"],
                          None if last else _norm_args(mods[l + 1], g1s[l + 1], 0, d))
        saved.append(sv)
        full.append({**wm, **wf})
        if not last:
            wm = {"w_in": gathers["w_in", l + 1].finish(after=h)[0]}
    def ffn_gate(l):
        return saved[l]["f"], _mod_rows(mods[l], d)[5]

    loss_part, dh, df, dgate2 = _loss_fwd(h, loss_target[0], *ffn_gate(depth - 1), name="loss")
    loss = lax.psum(loss_part[0, 0], ("x", "y", "c"))

    pipe = _ReducePipeline(core)
    dmods, dg1s, dg2s, dgains = [None] * depth, [None] * depth, [None] * depth, [None] * depth
    for l in reversed(range(depth)):
        dh_mid, dmod_f, dg2s[l], grads, dt, dgate1 = _ffn_bwd(dh, df, dgate2, saved[l], mods[l], g2s[l], full[l],
                                                              pipe.tick)
        pipe.tick(dh_mid)
        pipe.add(FFN_W, grads, l)
        dh, dmod_m, dg1s[l], dgains[l], grads, df, dgate2 = _mixer_bwd(
            dh_mid, dt, dgate1, saved[l], mods[l], g1s[l], gains[l], full[l], cos2, sin2,
            lambda after, early, l=l: (pipe.tick(after), pipe.add(MIXER_W[1:], early, l)),
            ffn_gate(l - 1) if l > 0 else None)
        dmods[l] = jnp.concatenate(dmod_m + dmod_f, axis=1)
        pipe.tick(dh)
        pipe.add(MIXER_W[:1], grads, l)
    grad_x = dh[None]

    stacked = {}

    def update(items):
        for keys, l, sums, remote in items:
            for k, p_, r_ in zip(keys, sums, remote):
                stacked[k] = _adamw_sharded(p_, r_, chip, weights[k], moments_m[k], moments_v[k], l,
                                            stacked.get(k), name="adamw_" + k)

    ready = pipe.take_done()
    update([it for it in ready if it[0] != FFN_W])

    small = jnp.concatenate(
        dmods + dg1s + dg2s + [dgains[l][0] for l in range(depth)] + [dgains[l][1] for l in range(depth)], axis=1)
    small_all = _small_allgather(small, name="comm_gather_small")
    pipe.tick(small_all)
    update([it for it in ready if it[0] == FFN_W] + pipe.take_done())

    def pack(b, n1, n2, qn, kn):
        return jnp.concatenate([t_.reshape(1, -1) for t_ in (b, n1, n2, qn, kn)], axis=1)

    sg, sd, sm, sv_ = _adamw_replicated(small_all, pack(b_ada, norm1_g, norm2_g, qn_g, kn_g),
                                        pack(m_b_ada, m_norm1_g, m_norm2_g, m_qn_g, m_kn_g),
                                        pack(v_b_ada, v_norm1_g, v_norm2_g, v_qn_g, v_kn_g), name="adamw_replicated")

    def unpack(p):
        sizes = [depth * 6 * d, depth * d, depth * d, depth * HEAD_DIM, depth * HEAD_DIM]
        shapes = [b_ada.shape, norm1_g.shape, norm2_g.shape, qn_g.shape, kn_g.shape]
        out, off = [], 0
        for n, shp in zip(sizes, shapes):
            out.append(p[0, off:off + n].reshape(shp))
            off += n
        return dict(zip(("b_ada", "norm1_g", "norm2_g", "qn_g", "kn_g"), out))

    ug, ud, um, uv = unpack(sg), unpack(sd), unpack(sm), unpack(sv_)
    res = {k: dict(g=ug[k], d=ud[k], m=um[k], v=uv[k]) for k in ug}

    dmod_all = small_all[:, 0, :depth * 6 * d].reshape(N_DEV, depth, 6 * d)
    g_ada = None
    for l in range(depth):
        dm = lax.dynamic_slice(dmod_all[:, l, :], (0, me * ada_w), (N_DEV, ada_w))
        dm = jnp.concatenate([dm, jnp.zeros_like(dm)], axis=0).astype(BF16)
        g_ada = _mm(c_pad, dm, ta=True, name="mm_wgrad_ada", stack=(l, depth, g_ada))
    d_ada, m_ada, v_ada = _adamw_local(g_ada, w_ada, m_w_ada, v_w_ada, name="adamw_local")
    res["w_ada"] = dict(g=g_ada, d=d_ada, m=m_ada, v=v_ada)

    pipe.tick(d_ada)
    update(pipe.take_done())
    pipe.tick(d_ada, flush=True)
    update(pipe.take_done())
    for k, (g_, d_, m_, v_) in stacked.items():
        res[k] = dict(g=g_, d=d_, m=m_, v=v_)

    order = ("w_ada", "b_ada", "norm1_g", "norm2_g", "w_in", "qn_g", "kn_g", "w_branch_a", "w_branch_b", "w_out",
             "w_gate_up", "w_down")
    _ORDER["token"] = None
    return (loss, grad_x, *[res[k]["g"] for k in order], *[res[k]["d"] for k in order],
            *[res[k]["m"] for k in order], *[res[k]["v"] for k in order])
```

```python
import functools

import jax
import jax.numpy as jnp
from jax import lax
from jax.experimental import pallas as pl
from jax.experimental.pallas import tpu as pltpu

F32 = jnp.float32
BF16 = jnp.bfloat16

HEAD_DIM = 128
BLOCK = 128
DILATIONS = (1, 4, 16)
HEADS_PER_GROUP = 4
A_HEADS = 12
SB_HEADS = 4
GROUP_W = HEADS_PER_GROUP * HEAD_DIM
A_W = A_HEADS * HEAD_DIM
B_W = SB_HEADS * HEAD_DIM
OFF_QA, OFF_KA, OFF_VA = 0, A_W, 2 * A_W
OFF_QB, OFF_KB, OFF_VB = 3 * A_W, 3 * A_W + B_W, 3 * A_W + 2 * B_W
OFF_GATES = 3 * A_W + 3 * B_W
ROPE_THETA = 10000.0
EPS = 1e-6
ATT_SCALE = HEAD_DIM ** -0.5
MASKED = -1e30

ADAM_LR, ADAM_B1, ADAM_B2, ADAM_EPS, ADAM_WD, ADAM_STEP = 0.001, 0.9, 0.999, 1e-08, 0.01, 10

N_DEV = 8
N_CHIPS = 4
V7X_VMEM_LIMIT_BYTES = 56 * 1024 * 1024
ELEMWISE_BLOCK_BYTES = 2 * 1024 * 1024
MESH = pl.DeviceIdType.MESH

NN = (((1,), (0,)), ((), ()))
NT = (((1,), (1,)), ((), ()))
TN = (((0,), (0,)), ((), ()))


def _dot(a, b, dims=NN):
    return lax.dot_general(a, b, dims, preferred_element_type=F32)


def _tile(n, cap, mult=128):
    best = None
    for t in range(mult, min(n, cap) + 1, mult):
        if n % t == 0:
            best = t
    if best is None:
        assert n <= 2 * cap, (n, cap)
        return n
    return best


def _rows(r, c):
    return _tile(r, max(16, ELEMWISE_BLOCK_BYTES // (4 * c)), 16)


_ORDER = {"token": None}
TOKEN = jax.ShapeDtypeStruct((8, 128), F32)


def _take_token():
    prev = _ORDER["token"]
    return [] if prev is None else [prev]


def _pcall(body, *, name, out_shape, grid=None, in_specs=None, out_specs=None, scratch=(), aliases=None,
           prefetch=0):
    single = not isinstance(out_shape, (tuple, list))
    out_shapes = [out_shape] if single else list(out_shape)
    out_specs = [out_specs] if single else list(out_specs)
    extra = _take_token()
    n_in, n_extra, n_out = prefetch + len(in_specs), len(extra), len(out_shapes)

    def wrapped(*refs):
        token = refs[n_in + n_extra + n_out]
        token[...] = jnp.zeros_like(token)
        return body(*refs[:n_in], *refs[n_in + n_extra:n_in + n_extra + n_out], *refs[n_in + n_extra + n_out + 1:])

    in_specs = list(in_specs) + [pl.BlockSpec(memory_space=pl.ANY)] * n_extra
    if grid is None:
        out_specs.append(pl.BlockSpec(memory_space=pltpu.VMEM))
    else:
        out_specs.append(pl.BlockSpec(TOKEN.shape, lambda *_: (0, 0)))
    kwargs = dict(name=name, out_shape=out_shapes + [TOKEN], input_output_aliases=aliases or {},
                  compiler_params=pltpu.CompilerParams(vmem_limit_bytes=V7X_VMEM_LIMIT_BYTES))
    if prefetch:
        call = pl.pallas_call(wrapped, grid_spec=pltpu.PrefetchScalarGridSpec(
            num_scalar_prefetch=prefetch, grid=grid, in_specs=in_specs, out_specs=out_specs,
            scratch_shapes=list(scratch)), **kwargs)
    else:
        if grid is not None:
            kwargs["grid"] = grid
        call = pl.pallas_call(wrapped, in_specs=in_specs, out_specs=out_specs, scratch_shapes=list(scratch), **kwargs)

    def run(*args):
        outs = call(*args, *extra)
        _ORDER["token"] = outs[-1]
        return outs[0] if single else tuple(outs[:-1])

    return run


def _mm(a, b, *, name, ta=False, tb=False, out_dtype=F32, caps=(1024, 1024, 3072), stack=None):
    kdim, m = a.shape if ta else a.shape[::-1]
    n, k2 = b.shape if tb else b.shape[::-1]
    assert kdim == k2, (a.shape, b.shape, ta, tb)
    tm, tn, tk = _tile(m, caps[0]), _tile(n, caps[1]), _tile(kdim, caps[2])
    nk = kdim // tk
    dims = (((0 if ta else 1,), (1 if tb else 0,)), ((), ()))

    def body(*refs):
        a_ref, b_ref = refs[0], refs[1]
        part = _dot(a_ref[...].astype(BF16), b_ref[...].astype(BF16), dims)
        if nk == 1:
            o_ref = refs[-1]
            o_ref[...] = part.astype(o_ref.dtype)
            return
        o_ref, acc_ref = refs[-2], refs[-1]
        k = pl.program_id(2)

        @pl.when(k == 0)
        def _():
            acc_ref[...] = part

        @pl.when(k > 0)
        def _():
            acc_ref[...] += part

        @pl.when(k == nk - 1)
        def _():
            o_ref[...] = acc_ref[...].astype(o_ref.dtype)

    a_spec = (pl.BlockSpec((tk, tm), lambda i, j, k: (k, i)) if ta
              else pl.BlockSpec((tm, tk), lambda i, j, k: (i, k)))
    b_spec = (pl.BlockSpec((tn, tk), lambda i, j, k: (j, k)) if tb
              else pl.BlockSpec((tk, tn), lambda i, j, k: (k, j)))
    ins, in_specs, aliases = [a, b], [a_spec, b_spec], {}
    if stack is None:
        out_shape = jax.ShapeDtypeStruct((m, n), out_dtype)
        out_spec = pl.BlockSpec((tm, tn), lambda i, j, k: (i, j))
    else:
        layer, n_layers, buf = stack
        out_shape = jax.ShapeDtypeStruct((n_layers, m, n), out_dtype)
        out_spec = pl.BlockSpec((None, tm, tn), lambda i, j, k: (layer, i, j))
        if buf is not None:
            ins.append(buf)
            in_specs.append(pl.BlockSpec(memory_space=pl.ANY))
            aliases = {2: 0}
    scratch = [] if nk == 1 else [pltpu.VMEM((tm, tn), F32)]
    return _pcall(body, name=name, out_shape=out_shape, grid=(m // tm, n // tn, nk), in_specs=in_specs,
                  out_specs=out_spec, scratch=scratch, aliases=aliases)(*ins)


EPILOGUE_ROWS = 256


def _row_chunks(tm):
    return [slice(r, r + EPILOGUE_ROWS) for r in range(0, tm, EPILOGUE_ROWS)] if tm > EPILOGUE_ROWS else [slice(0, tm)]


def _mm_cat_k(a_lo, a_hi, b, *, name):
    m, f = a_lo.shape
    n = b.shape[0]
    tm, tn, tk = _tile(m, 1024), _tile(n, 1024), _tile(f, 3072)
    half = f // tk
    nk = 2 * half

    def body(lo_ref, hi_ref, b_ref, o_ref, acc_ref):
        k = pl.program_id(2)

        def accumulate(a_ref):
            part = _dot(a_ref[...], b_ref[...], NT)

            @pl.when(k == 0)
            def _():
                acc_ref[...] = part

            @pl.when(k > 0)
            def _():
                acc_ref[...] += part

        pl.when(k < half)(lambda: accumulate(lo_ref))
        pl.when(k >= half)(lambda: accumulate(hi_ref))

        @pl.when(k == nk - 1)
        def _():
            o_ref[...] = acc_ref[...]

    return _pcall(body, name=name, out_shape=jax.ShapeDtypeStruct((m, n), F32), grid=(m // tm, n // tn, nk),
                  in_specs=[pl.BlockSpec((tm, tk), lambda i, j, k: (i, jnp.minimum(k, half - 1))),
                            pl.BlockSpec((tm, tk), lambda i, j, k: (i, jnp.maximum(k - half, 0))),
                            pl.BlockSpec((tn, tk), lambda i, j, k: (j, k))],
                  out_specs=pl.BlockSpec((tm, tn), lambda i, j, k: (i, j)),
                  scratch=[pltpu.VMEM((tm, tn), F32)])(a_lo, a_hi, b)


def _mm_cat_n(a, b_lo, b_hi, *, name):
    s, m = a.shape
    f = b_lo.shape[1]
    tm, tn = _tile(m, 2048), _tile(f, 1024)
    half = f // tn

    def body(a_ref, lo_ref, hi_ref, o_ref):
        j = pl.program_id(1)

        @pl.when(j < half)
        def _():
            o_ref[...] = _dot(a_ref[...], lo_ref[...], TN).astype(BF16)

        @pl.when(j >= half)
        def _():
            o_ref[...] = _dot(a_ref[...], hi_ref[...], TN).astype(BF16)

    return _pcall(body, name=name, out_shape=jax.ShapeDtypeStruct((m, 2 * f), BF16), grid=(m // tm, 2 * half),
                  in_specs=[pl.BlockSpec((s, tm), lambda i, j: (0, i)),
                            pl.BlockSpec((s, tn), lambda i, j: (0, jnp.minimum(j, half - 1))),
                            pl.BlockSpec((s, tn), lambda i, j: (0, jnp.maximum(j - half, 0)))],
                  out_specs=pl.BlockSpec((tm, tn), lambda i, j: (i, j)))(a, b_lo, b_hi)


def _mm_resid_norm(a, w, h, gate, norm, *, name):
    s, kdim = a.shape
    d = w.shape[1]
    tk = _tile(kdim, 2048)
    nk = kdim // tk
    tm = _tile(s, 256 if nk == 1 else 512)

    def body(*refs):
        a_ref, w_ref, h_ref, gate_ref = refs[:4]
        outs = refs[7:] if norm is not None else refs[4:]

        def finish(rows, t):
            hn = h_ref[rows, :] + gate_ref[...] * t
            outs[0][rows, :] = hn
            outs[1][rows, :] = t.astype(BF16)
            if norm is not None:
                g_ref, sc_ref, sh_ref = refs[4:7]
                r = lax.rsqrt(jnp.mean(hn * hn, axis=-1, keepdims=True) + EPS)
                outs[2][rows, :] = (((hn * r) * g_ref[...]) * (1.0 + sc_ref[...]) + sh_ref[...]).astype(BF16)

        if nk == 1:
            for rows in _row_chunks(tm):
                finish(rows, _dot(a_ref[rows, :], w_ref[...]))
            return
        acc_ref = refs[-1]
        k = pl.program_id(1)

        @pl.when(k == 0)
        def _():
            acc_ref[...] = _dot(a_ref[...], w_ref[...])

        @pl.when(jnp.logical_and(k > 0, k < nk - 1))
        def _():
            acc_ref[...] += _dot(a_ref[...], w_ref[...])

        @pl.when(k == nk - 1)
        def _():
            for rows in _row_chunks(tm):
                finish(rows, acc_ref[rows, :] + _dot(a_ref[rows, :], w_ref[...]))

    row = pl.BlockSpec((tm, d), lambda i, k: (i, 0))
    vec = pl.BlockSpec((1, d), lambda i, k: (0, 0))
    in_specs = [pl.BlockSpec((tm, tk), lambda i, k: (i, k)), pl.BlockSpec((tk, d), lambda i, k: (k, 0)), row, vec]
    args = [a, w, h, gate]
    out_shape = [jax.ShapeDtypeStruct((s, d), F32), jax.ShapeDtypeStruct((s, d), BF16)]
    if norm is not None:
        in_specs += [vec, vec, vec]
        args += list(norm)
        out_shape.append(jax.ShapeDtypeStruct((s, d), BF16))
    outs = _pcall(body, name=name, out_shape=tuple(out_shape), grid=(s // tm, nk), in_specs=in_specs,
                  out_specs=(row,) * len(out_shape), scratch=[] if nk == 1 else [pltpu.VMEM((tm, d), F32)])(*args)
    return outs if norm is not None else (*outs, None)


def _mm_merge(o_a, o_b, w_a, w_b, proj, *, name):
    s = o_a.shape[0]
    d = w_a.shape[1]
    tm = _tile(s, 512)
    ga_blk = OFF_GATES // d

    def body(oa_ref, ob_ref, wa_ref, wb_ref, ga_ref, gb_ref, m_ref, ya_ref, yb_ref):
        for rows in _row_chunks(tm):
            ya, yb = _dot(oa_ref[rows, :], wa_ref[...]), _dot(ob_ref[rows, :], wb_ref[...])
            m_ref[rows, :] = (jax.nn.sigmoid(ga_ref[rows, :]) * ya
                              + jax.nn.sigmoid(gb_ref[rows, :]) * yb).astype(BF16)
            ya_ref[rows, :] = ya.astype(BF16)
            yb_ref[rows, :] = yb.astype(BF16)

    row = pl.BlockSpec((tm, d), lambda i: (i, 0))
    act = pl.BlockSpec((tm, o_a.shape[1]), lambda i: (i, 0))
    wspec = pl.BlockSpec(w_a.shape, lambda i: (0, 0))
    shp = jax.ShapeDtypeStruct((s, d), BF16)
    return _pcall(body, name=name, out_shape=(shp, shp, shp), grid=(s // tm,),
                  in_specs=[act, act, wspec, wspec, pl.BlockSpec((tm, d), lambda i: (i, ga_blk)),
                            pl.BlockSpec((tm, d), lambda i: (i, ga_blk + 1))],
                  out_specs=(row, row, row))(o_a, o_b, w_a, w_b, proj, proj)


def _mm_out_t_merge(dt, w_out, proj, y_a, y_b, *, name):
    s, d = dt.shape
    tm, tn = _tile(s, 1024), _tile(d, 512)
    ga_blk = OFF_GATES // tn

    def body(dt_ref, w_ref, ga_ref, gb_ref, ya_ref, yb_ref, dya_ref, dyb_ref, dga_ref, dgb_ref):
        w = w_ref[...]
        for rows in _row_chunks(tm):
            dm = _dot(dt_ref[rows, :], w, NT)
            sa, sb = jax.nn.sigmoid(ga_ref[rows, :]), jax.nn.sigmoid(gb_ref[rows, :])
            dya_ref[rows, :] = (dm * sa).astype(BF16)
            dyb_ref[rows, :] = (dm * sb).astype(BF16)
            dga_ref[rows, :] = (dm * ya_ref[rows, :] * (sa * (1.0 - sa))).astype(BF16)
            dgb_ref[rows, :] = (dm * yb_ref[rows, :] * (sb * (1.0 - sb))).astype(BF16)

    tile = pl.BlockSpec((tm, tn), lambda i, j: (i, j))
    shp = jax.ShapeDtypeStruct((s, d), BF16)
    return _pcall(body, name=name, out_shape=(shp,) * 4, grid=(s // tm, d // tn),
                  in_specs=[pl.BlockSpec((tm, d), lambda i, j: (i, 0)), pl.BlockSpec((tn, d), lambda i, j: (j, 0)),
                            pl.BlockSpec((tm, tn), lambda i, j: (i, ga_blk + j)),
                            pl.BlockSpec((tm, tn), lambda i, j: (i, ga_blk + d // tn + j)), tile, tile],
                  out_specs=(tile,) * 4)(dt, w_out, proj, proj, y_a, y_b)


def _mm_down_t_swiglu(df, w_down, g, u, *, name):
    s, d = df.shape
    f = w_down.shape[0]
    tm, tn = _tile(s, 1024), _tile(f, 512)

    def body(df_ref, w_ref, g_ref, u_ref, dg_ref, du_ref):
        w = w_ref[...]
        for rows in _row_chunks(tm):
            da = _dot(df_ref[rows, :], w, NT)
            gf = g_ref[rows, :].astype(F32)
            sg = jax.nn.sigmoid(gf)
            dg_ref[rows, :] = (da * u_ref[rows, :].astype(F32) * (sg * (1.0 + gf * (1.0 - sg)))).astype(BF16)
            du_ref[rows, :] = (da * (gf * sg)).astype(BF16)

    tile = pl.BlockSpec((tm, tn), lambda i, j: (i, j))
    shp = jax.ShapeDtypeStruct((s, f), BF16)
    return _pcall(body, name=name, out_shape=(shp, shp), grid=(s // tm, f // tn),
                  in_specs=[pl.BlockSpec((tm, d), lambda i, j: (i, 0)), pl.BlockSpec((tn, d), lambda i, j: (j, 0)),
                            tile, tile],
                  out_specs=(tile, tile))(df, w_down, g, u)


def _rmsmod_fwd(h, g, scale, shift, *, name):
    s, d = h.shape
    ts = _rows(s, d)

    def body(h_ref, g_ref, sc_ref, sh_ref, u_ref):
        hf = h_ref[...]
        r = lax.rsqrt(jnp.mean(hf * hf, axis=-1, keepdims=True) + EPS)
        u_ref[...] = (((hf * r) * g_ref[...]) * (1.0 + sc_ref[...]) + sh_ref[...]).astype(BF16)

    row = pl.BlockSpec((ts, d), lambda i: (i, 0))
    vec = pl.BlockSpec((1, d), lambda i: (0, 0))
    return _pcall(body, name=name, out_shape=jax.ShapeDtypeStruct((s, d), BF16), grid=(s // ts,),
                  in_specs=[row, vec, vec, vec], out_specs=row)(h, g, scale, shift)


def _gate_bwd(dhf, t_ref, gate_ref, dt_ref, dgate_ref):
    dt_ref[...] = (dhf * gate_ref[...]).astype(BF16)
    dgate_ref[...] += jnp.sum(dhf * t_ref[...], axis=0, keepdims=True)


def _rmsmod_bwd(du, h, g, scale, dres, t, gate, *, name):
    s, d = h.shape
    ts = _rows(s, d)
    chain = t is not None

    def body(*refs):
        du_ref, h_ref, g_ref, sc_ref, dres_ref = refs[:5]
        dh_ref, dsh_ref, dsc_ref, dg_ref = refs[-6:-2] if chain else refs[-4:]
        sums = (dsh_ref, dsc_ref, dg_ref) + ((refs[-1],) if chain else ())

        @pl.when(pl.program_id(0) == 0)
        def _():
            for ref in sums:
                ref[...] = jnp.zeros_like(ref)

        hf, duf, gain = h_ref[...], du_ref[...], g_ref[...]
        r = lax.rsqrt(jnp.mean(hf * hf, axis=-1, keepdims=True) + EPS)
        xh = hf * r
        dn = duf * (1.0 + sc_ref[...])
        dsh_ref[...] += jnp.sum(duf, axis=0, keepdims=True)
        dsc_ref[...] += jnp.sum(duf * (xh * gain), axis=0, keepdims=True)
        dg_ref[...] += jnp.sum(dn * xh, axis=0, keepdims=True)
        dxh = dn * gain
        dh = dres_ref[...] + r * (dxh - xh * jnp.mean(dxh * xh, axis=-1, keepdims=True))
        dh_ref[...] = dh
        if chain:
            _gate_bwd(dh, refs[5], refs[6], refs[-2], refs[-1])

    row = pl.BlockSpec((ts, d), lambda i: (i, 0))
    vec = pl.BlockSpec((1, d), lambda i: (0, 0))
    vshape = jax.ShapeDtypeStruct((1, d), F32)
    out_shape, out_specs = [jax.ShapeDtypeStruct((s, d), F32), vshape, vshape, vshape], [row, vec, vec, vec]
    in_specs, args = [row, row, vec, vec, row], [du, h, g, scale, dres]
    if chain:
        in_specs, args = in_specs + [row, vec], args + [t, gate]
        out_shape, out_specs = out_shape + [jax.ShapeDtypeStruct((s, d), BF16), vshape], out_specs + [row, vec]
    outs = _pcall(body, name=name, out_shape=tuple(out_shape), grid=(s // ts,), in_specs=in_specs,
                  out_specs=tuple(out_specs))(*args)
    return outs if chain else (*outs, None, None)


def _mm_swiglu(u2, w_gate_up, *, name):
    s, d = u2.shape
    f = w_gate_up.shape[1] // 2
    tm, tn = _tile(s, 1024), _tile(f, 512)
    nj = f // tn

    def body(x_ref, wg_ref, wu_ref, a_ref, g_ref, u_ref):
        for rows in _row_chunks(tm):
            x = x_ref[rows, :]
            gf, uf = _dot(x, wg_ref[...]), _dot(x, wu_ref[...])
            a_ref[rows, :] = ((gf * jax.nn.sigmoid(gf)) * uf).astype(BF16)
            g_ref[rows, :] = gf.astype(BF16)
            u_ref[rows, :] = uf.astype(BF16)

    out = pl.BlockSpec((tm, tn), lambda i, j: (i, j))
    shp = jax.ShapeDtypeStruct((s, f), BF16)
    return _pcall(body, name=name, out_shape=(shp, shp, shp), grid=(s // tm, nj),
                  in_specs=[pl.BlockSpec((tm, d), lambda i, j: (i, 0)), pl.BlockSpec((d, tn), lambda i, j: (0, j)),
                            pl.BlockSpec((d, tn), lambda i, j: (0, nj + j))],
                  out_specs=(out, out, out))(u2, w_gate_up, w_gate_up)


def _loss_fwd(y, tgt, t, gate, *, name):
    s, d = y.shape
    ts = _rows(s, d)

    def body(y_ref, tgt_ref, t_ref, gate_ref, l_ref, dy_ref, dt_ref, dgate_ref):
        @pl.when(pl.program_id(0) == 0)
        def _():
            l_ref[...] = jnp.zeros_like(l_ref)
            dgate_ref[...] = jnp.zeros_like(dgate_ref)

        e = y_ref[...] - tgt_ref[...]
        dy = e * (1.0 / d)
        dy_ref[...] = dy
        per_tok = jnp.sum(e * e, axis=1, keepdims=True) * (1.0 / d)
        l_ref[...] += 0.5 * jnp.sum(per_tok, axis=0, keepdims=True)
        _gate_bwd(dy, t_ref, gate_ref, dt_ref, dgate_ref)

    row = pl.BlockSpec((ts, d), lambda i: (i, 0))
    vec = pl.BlockSpec((1, d), lambda i: (0, 0))
    return _pcall(body, name=name,
                  out_shape=(jax.ShapeDtypeStruct((1, 128), F32), jax.ShapeDtypeStruct((s, d), F32),
                             jax.ShapeDtypeStruct((s, d), BF16), jax.ShapeDtypeStruct((1, d), F32)),
                  grid=(s // ts,), in_specs=[row, row, row, vec],
                  out_specs=(pl.BlockSpec((1, 128), lambda i: (0, 0)), row, row, vec))(y, tgt, t, gate)


def _rope_tables(seq):
    inv = jnp.power(ROPE_THETA, -jnp.arange(0, HEAD_DIM, 2, dtype=F32) / HEAD_DIM)
    ang = jnp.arange(seq, dtype=F32)[:, None] * inv[None, :]
    cos, sin = jnp.cos(ang), jnp.sin(ang)
    return jnp.concatenate([cos, cos], axis=1), jnp.concatenate([-sin, sin], axis=1)


def _qkrope_fwd(proj, gains, cos2, sin2, *, name):
    s = proj.shape[0]
    ts = _rows(s, A_W)

    def body(x_ref, g_ref, c_ref, s_ref, o_ref, o32_ref):
        gain, cos, sin = g_ref[...], c_ref[...], s_ref[...]
        for h in range(A_HEADS):
            lanes = slice(h * HEAD_DIM, (h + 1) * HEAD_DIM)
            x = x_ref[:, lanes]
            y = (x * lax.rsqrt(jnp.mean(x * x, axis=-1, keepdims=True) + EPS)) * gain
            out = y * cos + pltpu.roll(y, HEAD_DIM // 2, 1) * sin
            o_ref[:, lanes] = out.astype(BF16)
            o32_ref[:, lanes] = out

    heads = pl.BlockSpec((ts, A_W), lambda i, j: (i, j))
    tab = pl.BlockSpec((ts, HEAD_DIM), lambda i, j: (i, 0))
    gain = pl.BlockSpec((None, 1, HEAD_DIM), lambda i, j: (j, 0, 0))
    return _pcall(body, name=name,
                  out_shape=(jax.ShapeDtypeStruct((s, 2 * A_W), BF16), jax.ShapeDtypeStruct((s, 2 * A_W), F32)),
                  grid=(s // ts, 2), in_specs=[heads, gain, tab, tab], out_specs=(heads, heads))(
                      proj, gains, cos2, sin2)


def _qkrope_bwd(d_groups, proj, gains, which, cos2, sin2, *, name):
    s = proj.shape[0]
    ts = _rows(s, A_W)

    def body(d0_ref, d1_ref, d2_ref, x_ref, g_ref, c_ref, s_ref, dx_ref, dg_ref):
        @pl.when(pl.program_id(0) == 0)
        def _():
            dg_ref[...] = jnp.zeros_like(dg_ref)

        gain, cos, sin = g_ref[...], c_ref[...], s_ref[...]
        dg = jnp.zeros((1, HEAD_DIM), F32)
        for h in range(A_HEADS):
            lanes = slice(h * HEAD_DIM, (h + 1) * HEAD_DIM)
            slot = slice((h % HEADS_PER_GROUP) * HEAD_DIM, (h % HEADS_PER_GROUP + 1) * HEAD_DIM)
            dout = (d0_ref, d1_ref, d2_ref)[h // HEADS_PER_GROUP][:, slot]
            dy = dout * cos + pltpu.roll(dout * sin, HEAD_DIM // 2, 1)
            x = x_ref[:, lanes]
            r = lax.rsqrt(jnp.mean(x * x, axis=-1, keepdims=True) + EPS)
            xh = x * r
            dg = dg + jnp.sum(dy * xh, axis=0, keepdims=True)
            dxh = dy * gain
            dx_ref[:, lanes] = (r * (dxh - xh * jnp.mean(dxh * xh, axis=-1, keepdims=True))).astype(BF16)
        dg_ref[...] += dg

    group = pl.BlockSpec((ts, GROUP_W), lambda i: (i, 0))
    tab = pl.BlockSpec((ts, HEAD_DIM), lambda i: (i, 0))
    gain = pl.BlockSpec((None, 1, HEAD_DIM), lambda i: (which, 0, 0))
    return _pcall(body, name=name,
                  out_shape=(jax.ShapeDtypeStruct((s, A_W), BF16), jax.ShapeDtypeStruct((1, HEAD_DIM), F32)),
                  grid=(s // ts,),
                  in_specs=[group, group, group, pl.BlockSpec((ts, A_W), lambda i: (i, which)), gain, tab, tab],
                  out_specs=(pl.BlockSpec((ts, A_W), lambda i: (i, 0)), pl.BlockSpec((1, HEAD_DIM), lambda i: (0, 0))))(
                      *d_groups, proj, gains, cos2, sin2)


def _assemble(pieces, *, name):
    s = pieces[0].shape[0]
    widths = [p.shape[1] for p in pieces]
    total = sum(widths)
    ts = _rows(s, total // 2)

    def body(*refs):
        o_ref, off = refs[-1], 0
        for x_ref, w in zip(refs[:-1], widths):
            o_ref[:, off:off + w] = x_ref[...].astype(BF16)
            off += w

    return _pcall(body, name=name, out_shape=jax.ShapeDtypeStruct((s, total), BF16), grid=(s // ts,),
                  in_specs=[pl.BlockSpec((ts, w), lambda i: (i, 0)) for w in widths],
                  out_specs=pl.BlockSpec((ts, total), lambda i: (i, 0)))(*pieces)


def _block_rows(blk):
    if isinstance(blk, int):
        return pl.ds(blk * BLOCK, BLOCK)
    return pl.ds(pl.multiple_of(blk * BLOCK, BLOCK), BLOCK)


def _band_window(n, length):
    width = min(2 * BLOCK, length)
    row = lax.broadcasted_iota(jnp.int32, (BLOCK, width), 0)
    col = lax.broadcasted_iota(jnp.int32, (BLOCK, width), 1)
    if width == BLOCK:
        return pl.ds(0, BLOCK), col <= row
    first = n - 1 if isinstance(n, int) else jnp.maximum(n - 1, 0)
    first = max(first, 0) if isinstance(first, int) else first
    dist = row - col + (n - first) * BLOCK
    start = first * BLOCK if isinstance(first, int) else pl.multiple_of(first * BLOCK, BLOCK)
    return pl.ds(start, width), jnp.logical_and(dist >= 0, dist <= BLOCK)


def _dil_fwd(q_arr, k_arr, v_arr, offs, length, dil, *, name):
    nj, nb = dil * HEADS_PER_GROUP, length // BLOCK
    ju, nq = (HEADS_PER_GROUP, 2) if nb > 1 else (2 * HEADS_PER_GROUP, 1)
    qo, ko, vo = (off // ju for off in offs)
    assert all(off % ju == 0 for off in offs) and nb % nq == 0 and nj % ju == 0

    def body(q_ref, k_ref, v_ref, o_ref, l_ref):
        for qq in range(nq):
            qrows = slice(qq * BLOCK, (qq + 1) * BLOCK)
            rows, mask = _band_window(pl.program_id(1) * nq + qq, length)
            for cb in range(ju):
                lanes = slice(cb * HEAD_DIM, (cb + 1) * HEAD_DIM)
                sc = _dot(q_ref[qrows, lanes].astype(BF16), k_ref[rows, lanes].astype(BF16), NT) * ATT_SCALE
                sc = jnp.where(mask, sc, MASKED)
                m = sc.max(axis=-1, keepdims=True)
                p = jnp.exp(sc - m)
                den = jnp.sum(p, axis=-1, keepdims=True)
                acc = _dot(p.astype(BF16), v_ref[rows, lanes].astype(BF16))
                o_ref[qrows, lanes] = acc / den
                l_ref[qrows, lanes] = jnp.broadcast_to(m + jnp.log(den), (BLOCK, HEAD_DIM))

    qspec = pl.BlockSpec((nq * BLOCK, ju * HEAD_DIM), lambda j, n: (n, qo + j))
    kspec = pl.BlockSpec((length, ju * HEAD_DIM), lambda j, n: (0, ko + j))
    vspec = pl.BlockSpec((length, ju * HEAD_DIM), lambda j, n: (0, vo + j))
    ospec = pl.BlockSpec((nq * BLOCK, ju * HEAD_DIM), lambda j, n: (n, j))
    shp = jax.ShapeDtypeStruct((length, nj * HEAD_DIM), F32)
    return _pcall(body, name=name, out_shape=(shp, shp), grid=(nj // ju, nb // nq), in_specs=[qspec, kspec, vspec],
                  out_specs=(ospec, ospec))(q_arr, k_arr, v_arr)


def _dil_bwd(q_arr, k_arr, v_arr, offs, o, lse, do, dlse, length, dil, *, name):
    nj, nb = dil * HEADS_PER_GROUP, length // BLOCK
    ju = 2 * HEADS_PER_GROUP if length <= 4 * BLOCK else 2
    qo, ko, vo = (off // ju for off in offs)
    assert all(off % ju == 0 for off in offs)

    def body(q_ref, k_ref, v_ref, o_ref, l_ref, do_ref, dl_ref, dq_ref, dk_ref, dv_ref):
        dk_ref[...] = jnp.zeros_like(dk_ref)
        dv_ref[...] = jnp.zeros_like(dv_ref)

        def step(n, carry):
            qrows = _block_rows(n)
            rows, mask = _band_window(n, length)
            for cb in range(ju):
                lanes = slice(cb * HEAD_DIM, (cb + 1) * HEAD_DIM)
                q = q_ref[qrows, lanes].astype(BF16)
                dof = do_ref[qrows, lanes]
                dob = dof.astype(BF16)
                lse_c = l_ref[qrows, lanes][:, :1]
                shift = dl_ref[qrows, lanes][:, :1] - jnp.sum(dof * o_ref[qrows, lanes], axis=-1, keepdims=True)
                kk, vv = k_ref[rows, lanes].astype(BF16), v_ref[rows, lanes].astype(BF16)
                sc = _dot(q, kk, NT) * ATT_SCALE
                p = jnp.where(mask, jnp.exp(sc - lse_c), 0.0)
                ds = (p * (_dot(dob, vv, NT) + shift)).astype(BF16)
                dq_ref[qrows, lanes] = _dot(ds, kk) * ATT_SCALE
                dk_ref[rows, lanes] += _dot(ds, q, TN) * ATT_SCALE
                dv_ref[rows, lanes] += _dot(p.astype(BF16), dob, TN)
            return carry

        if nb == 1:
            step(0, 0)
        else:
            lax.fori_loop(0, nb, step, 0)

    def col(off):
        return pl.BlockSpec((length, ju * HEAD_DIM), lambda j: (0, off + j))

    shp = jax.ShapeDtypeStruct((length, nj * HEAD_DIM), F32)
    return _pcall(body, name=name, out_shape=(shp, shp, shp), grid=(nj // ju,),
                  in_specs=[col(qo), col(ko), col(vo), col(0), col(0), col(0), col(0)],
                  out_specs=(col(0), col(0), col(0)))(q_arr, k_arr, v_arr, o, lse, do, dlse)


DIL_RESIDUES_PER_STEP = 8


def _dil_tokens(n, r, dil, length):
    width = min(2 * BLOCK, length)
    _, mask = _band_window(n, length)
    first = 0 if width == BLOCK else jnp.maximum(n - 1, 0)
    return (pl.ds(n * (BLOCK * dil) + r, BLOCK, stride=dil), pl.ds(first * (BLOCK * dil) + r, width, stride=dil),
            mask)


def _dil_head_specs(seq, group):
    first = group * HEADS_PER_GROUP

    def col(c0):
        return pl.BlockSpec((seq, HEAD_DIM), lambda h, r: (0, c0 + h))

    return col(first), col(A_HEADS + first), col(OFF_VA // HEAD_DIM + first), col(0)


def _dil_fwd_strided(qk32, proj, group, dil, *, name):
    seq = proj.shape[0]
    length = seq // dil
    nb, rp = length // BLOCK, min(dil, DIL_RESIDUES_PER_STEP)
    assert dil % rp == 0

    def body(q_ref, k_ref, v_ref, o_ref, l_ref):
        rgroup = pl.program_id(1)

        def step(n, carry):
            for rr in range(rp):
                tok_q, tok_k, mask = _dil_tokens(n, rgroup * rp + rr, dil, length)
                sc = _dot(q_ref[tok_q, :].astype(BF16), k_ref[tok_k, :].astype(BF16), NT) * ATT_SCALE
                sc = jnp.where(mask, sc, MASKED)
                m = sc.max(axis=-1, keepdims=True)
                p = jnp.exp(sc - m)
                den = jnp.sum(p, axis=-1, keepdims=True)
                o_ref[tok_q, :] = _dot(p.astype(BF16), v_ref[tok_k, :].astype(BF16)) / den
                l_ref[tok_q, :] = jnp.broadcast_to(m + jnp.log(den), (BLOCK, HEAD_DIM))
            return carry

        if nb == 1:
            step(0, 0)
        else:
            lax.fori_loop(0, nb, step, 0)

    qs, ks, vs, nat = _dil_head_specs(seq, group)
    shp = jax.ShapeDtypeStruct((seq, GROUP_W), F32)
    return _pcall(body, name=name, out_shape=(shp, shp), grid=(HEADS_PER_GROUP, dil // rp), in_specs=[qs, ks, vs],
                  out_specs=(nat, nat))(qk32, qk32, proj)


def _dil_bwd_strided(qk32, proj, group, o, lse, do, dlse, dil, *, name):
    seq = proj.shape[0]
    length = seq // dil
    nb, rp = length // BLOCK, min(dil, DIL_RESIDUES_PER_STEP)
    assert dil % rp == 0

    def body(q_ref, k_ref, v_ref, o_ref, l_ref, do_ref, dl_ref, dq_ref, dk_ref, dv_ref):
        rgroup = pl.program_id(1)

        @pl.when(rgroup == 0)
        def _():
            dk_ref[...] = jnp.zeros_like(dk_ref)
            dv_ref[...] = jnp.zeros_like(dv_ref)

        def step(n, carry):
            for rr in range(rp):
                tok_q, tok_k, mask = _dil_tokens(n, rgroup * rp + rr, dil, length)
                q = q_ref[tok_q, :].astype(BF16)
                dof = do_ref[tok_q, :]
                dob = dof.astype(BF16)
                lse_c = l_ref[tok_q, :][:, :1]
                shift = dl_ref[tok_q, :][:, :1] - jnp.sum(dof * o_ref[tok_q, :], axis=-1, keepdims=True)
                kk, vv = k_ref[tok_k, :].astype(BF16), v_ref[tok_k, :].astype(BF16)
                sc = _dot(q, kk, NT) * ATT_SCALE
                p = jnp.where(mask, jnp.exp(sc - lse_c), 0.0)
                ds = (p * (_dot(dob, vv, NT) + shift)).astype(BF16)
                dq_ref[tok_q, :] = _dot(ds, kk) * ATT_SCALE
                dk_ref[tok_k, :] += _dot(ds, q, TN) * ATT_SCALE
                dv_ref[tok_k, :] += _dot(p.astype(BF16), dob, TN)
            return carry

        if nb == 1:
            step(0, 0)
        else:
            lax.fori_loop(0, nb, step, 0)

    qs, ks, vs, nat = _dil_head_specs(seq, group)
    shp = jax.ShapeDtypeStruct((seq, GROUP_W), F32)
    return _pcall(body, name=name, out_shape=(shp, shp, shp), grid=(HEADS_PER_GROUP, dil // rp),
                  in_specs=[qs, ks, vs, nat, nat, nat, nat], out_specs=(nat, nat, nat))(
                      qk32, qk32, proj, o, lse, do, dlse)


def _combine_weights(l_refs):
    ls = [r[...] for r in l_refs]
    m = jnp.maximum(jnp.maximum(ls[0], ls[1]), ls[2])
    es = [jnp.exp(l - m) for l in ls]
    den = es[0] + es[1] + es[2]
    return [e / den for e in es]


def _combine_fwd(os_, lses, *, name):
    s = os_[0].shape[0]
    ts = _rows(s, GROUP_W)

    def body(o0, o1, o2, l0, l1, l2, out_ref):
        w = _combine_weights((l0, l1, l2))
        out_ref[...] = (w[0] * o0[...] + w[1] * o1[...] + w[2] * o2[...]).astype(BF16)

    row = pl.BlockSpec((ts, GROUP_W), lambda i: (i, 0))
    return _pcall(body, name=name, out_shape=jax.ShapeDtypeStruct((s, GROUP_W), BF16), grid=(s // ts,),
                  in_specs=[row] * 6, out_specs=row)(*os_, *lses)


def _combine_bwd(do_a, os_, lses, *, name):
    s = do_a.shape[0]
    ts = _rows(s, GROUP_W)

    def body(d_ref, o0, o1, o2, l0, l1, l2, do0, do1, do2, dl0, dl1, dl2):
        w = _combine_weights((l0, l1, l2))
        d = d_ref[...]
        og = [o0[...], o1[...], o2[...]]
        oa = w[0] * og[0] + w[1] * og[1] + w[2] * og[2]
        ta = jnp.sum(d * oa, axis=-1, keepdims=True)
        for g, (do_ref, dl_ref) in enumerate(((do0, dl0), (do1, dl1), (do2, dl2))):
            do_ref[...] = w[g] * d
            dl_ref[...] = w[g] * (jnp.sum(d * og[g], axis=-1, keepdims=True) - ta)

    head = pl.BlockSpec((ts, HEAD_DIM), lambda i, h: (i, h))
    shp = jax.ShapeDtypeStruct((s, GROUP_W), F32)
    return _pcall(body, name=name, out_shape=(shp,) * 6, grid=(s // ts, HEADS_PER_GROUP),
                  in_specs=[head] * 7, out_specs=(head,) * 6)(do_a, *os_, *lses)


def _dot_exact(x, ones_mask):
    hi = x.astype(BF16)
    r1 = x - hi.astype(F32)
    mid = r1.astype(BF16)
    lo = (r1 - mid.astype(F32)).astype(BF16)
    return _dot(hi, ones_mask) + _dot(mid, ones_mask) + _dot(lo, ones_mask)


SB_QROWS = 4 * BLOCK
SB_UNROLL = 4
SB_HEADS_PER_STEP = 2
SB_LANES = [slice(hh * HEAD_DIM, (hh + 1) * HEAD_DIM) for hh in range(SB_HEADS_PER_STEP)]


def _sb_mask(j, i):
    row = lax.broadcasted_iota(jnp.int32, (SB_QROWS, BLOCK), 0)
    col = lax.broadcasted_iota(jnp.int32, (SB_QROWS, BLOCK), 1)
    return col + (j * BLOCK - i * SB_QROWS) < row


def _sb_steps(i):
    return ((i + 1) * (SB_QROWS // BLOCK) + SB_UNROLL - 1) // SB_UNROLL


def _sb_scores(q, kk, j, i, masked):
    mask = _sb_mask(j, i) if masked else None
    z = _dot(q, kk, NT) * ATT_SCALE
    sp = jnp.log(1.0 + jnp.exp(-jnp.abs(z)))
    log_beta = jnp.minimum(z, 0.0) - sp
    log_1mb = jnp.minimum(-z, 0.0) - sp
    if masked:
        log_1mb = jnp.where(mask, log_1mb, 0.0)
    return z, log_beta, log_1mb, mask


def _sb_weights(log_beta, log_1mb, mask, run, upper):
    a = jnp.exp(log_beta + (run + _dot_exact(log_1mb, upper)))
    return a if mask is None else jnp.where(mask, a, 0.0)


def _sb_peeled(nsteps, make_step, init, masked_first):
    if masked_first:
        return lax.fori_loop(1, nsteps, make_step(False), make_step(True)(0, init))
    return make_step(True)(nsteps - 1, lax.fori_loop(0, nsteps - 1, make_step(False), init))


def _tri(strict_lower):
    row = lax.broadcasted_iota(jnp.int32, (BLOCK, BLOCK), 0)
    col = lax.broadcasted_iota(jnp.int32, (BLOCK, BLOCK), 1)
    return ((row > col) if strict_lower else (row < col)).astype(BF16)


def _sb_fwd(proj, *, name):
    s = proj.shape[0]
    assert s % (BLOCK * SB_UNROLL) == 0 and s % SB_QROWS == 0

    def body(q_ref, k_ref, v_ref, o_ref):
        i = pl.program_id(1)
        qs = [q_ref[:, lanes].astype(BF16) for lanes in SB_LANES]
        upper = _tri(True)
        nsteps = _sb_steps(i)

        def make_step(masked):
            def step(t, carry):
                carry = list(carry)
                for b in reversed(range(SB_UNROLL)):
                    j = (nsteps - 1 - t) * SB_UNROLL + b
                    rows = _block_rows(j)
                    for hh, lanes in enumerate(SB_LANES):
                        acc, run = carry[hh]
                        _, log_beta, log_1mb, mask = _sb_scores(qs[hh], k_ref[rows, lanes].astype(BF16), j, i, masked)
                        a = _sb_weights(log_beta, log_1mb, mask, run, upper)
                        carry[hh] = (acc + _dot(a.astype(BF16), v_ref[rows, lanes].astype(BF16)),
                                     run + jnp.sum(log_1mb, axis=-1, keepdims=True))
                return tuple(carry)
            return step

        zero = (jnp.zeros((SB_QROWS, HEAD_DIM), F32), jnp.zeros((SB_QROWS, 1), F32))
        for lanes, (acc, _) in zip(SB_LANES, _sb_peeled(nsteps, make_step, (zero,) * SB_HEADS_PER_STEP, True)):
            o_ref[:, lanes] = acc.astype(BF16)

    width = SB_HEADS_PER_STEP * HEAD_DIM
    qb, kb, vb = (off // width for off in (OFF_QB, OFF_KB, OFF_VB))
    return _pcall(body, name=name, out_shape=jax.ShapeDtypeStruct((s, B_W), BF16),
                  grid=(SB_HEADS // SB_HEADS_PER_STEP, s // SB_QROWS),
                  in_specs=[pl.BlockSpec((SB_QROWS, width), lambda h, i: (i, qb + h)),
                            pl.BlockSpec((s, width), lambda h, i: (0, kb + h)),
                            pl.BlockSpec((s, width), lambda h, i: (0, vb + h))],
                  out_specs=pl.BlockSpec((SB_QROWS, width), lambda h, i: (i, h)))(proj, proj, proj)


def _sb_bwd(proj, do_b, *, name):
    s = proj.shape[0]
    assert s % (BLOCK * SB_UNROLL) == 0 and s % SB_QROWS == 0
    nkb = s // BLOCK

    def body(q_ref, k_ref, v_ref, do_ref, dq_ref, dk_ref, dv_ref, z_s, a_s):
        i = pl.program_id(1)

        @pl.when(i == 0)
        def _():
            dk_ref[...] = jnp.zeros_like(dk_ref)
            dv_ref[...] = jnp.zeros_like(dv_ref)

        qs = [q_ref[:, lanes].astype(BF16) for lanes in SB_LANES]
        dobs = [do_ref[:, lanes].astype(BF16) for lanes in SB_LANES]
        upper, lower = _tri(True), _tri(False)
        nsteps = _sb_steps(i)

        def make_recompute(masked):
            def recompute(t, runs):
                runs = list(runs)
                for b in reversed(range(SB_UNROLL)):
                    j = (nsteps - 1 - t) * SB_UNROLL + b
                    rows = _block_rows(j)
                    for hh, lanes in enumerate(SB_LANES):
                        z, log_beta, log_1mb, mask = _sb_scores(qs[hh], k_ref[rows, lanes].astype(BF16), j, i, masked)
                        z_s[hh, j] = z
                        a_s[hh, j] = _sb_weights(log_beta, log_1mb, mask, runs[hh], upper)
                        runs[hh] = runs[hh] + jnp.sum(log_1mb, axis=-1, keepdims=True)
                return tuple(runs)
            return recompute

        _sb_peeled(nsteps, make_recompute, (jnp.zeros((SB_QROWS, 1), F32),) * SB_HEADS_PER_STEP, True)

        def make_grads(masked):
            def grads(t, carry):
                carry = list(carry)
                for b in range(SB_UNROLL):
                    j = t * SB_UNROLL + b
                    rows = _block_rows(j)
                    for hh, lanes in enumerate(SB_LANES):
                        dq, run = carry[hh]
                        kk, vv = k_ref[rows, lanes].astype(BF16), v_ref[rows, lanes].astype(BF16)
                        z, a = z_s[hh, j], a_s[hh, j]
                        de = _dot(dobs[hh], vv, NT) * a
                        beta = jax.nn.sigmoid(z)
                        one_minus_beta = 1.0 - beta
                        if masked:
                            beta = jnp.where(_sb_mask(j, i), beta, 0.0)
                        dz = (de * one_minus_beta - beta * (run + _dot_exact(de, lower))).astype(BF16)
                        dk_ref[rows, lanes] += _dot(dz, qs[hh], TN) * ATT_SCALE
                        dv_ref[rows, lanes] += _dot(a.astype(BF16), dobs[hh], TN)
                        carry[hh] = (dq + _dot(dz, kk), run + jnp.sum(de, axis=-1, keepdims=True))
                return tuple(carry)
            return grads

        zero = (jnp.zeros((SB_QROWS, HEAD_DIM), F32), jnp.zeros((SB_QROWS, 1), F32))
        for lanes, (dq, _) in zip(SB_LANES, _sb_peeled(nsteps, make_grads, (zero,) * SB_HEADS_PER_STEP, False)):
            dq_ref[:, lanes] = dq * ATT_SCALE

    width = SB_HEADS_PER_STEP * HEAD_DIM
    qb, kb, vb = (off // width for off in (OFF_QB, OFF_KB, OFF_VB))
    blk = pl.BlockSpec((SB_QROWS, width), lambda h, i: (i, h))
    full = pl.BlockSpec((s, width), lambda h, i: (0, h))
    shp = jax.ShapeDtypeStruct((s, B_W), F32)
    saved = pltpu.VMEM((SB_HEADS_PER_STEP, nkb, SB_QROWS, BLOCK), F32)
    return _pcall(body, name=name, out_shape=(shp, shp, shp), grid=(SB_HEADS // SB_HEADS_PER_STEP, s // SB_QROWS),
                  in_specs=[pl.BlockSpec((SB_QROWS, width), lambda h, i: (i, qb + h)),
                            pl.BlockSpec((s, width), lambda h, i: (0, kb + h)),
                            pl.BlockSpec((s, width), lambda h, i: (0, vb + h)), blk],
                  out_specs=(blk, full, full), scratch=[saved, saved])(proj, proj, proj, do_b)


def _coords():
    return lax.axis_index("x"), lax.axis_index("y"), lax.axis_index("c")


def _flip(v, bit):
    return 1 - v if bit else v


def _shard_of(ref, axis, idx, size):
    if axis == 0:
        sl = pl.ds(pl.multiple_of(idx * size, 16), size)
        return ref.at[sl, :] if len(ref.shape) == 2 else ref.at[:, sl, :]
    sl = pl.ds(pl.multiple_of(idx * size, 128), size)
    return ref.at[:, sl] if len(ref.shape) == 2 else ref.at[:, :, sl]


def _small_allgather(v, *, name, silu=False):
    n = v.shape[1]

    def body(v_ref, out_ref, send_sems, recv_sems):
        x, y, c = _coords()
        me = 4 * x + 2 * y + c
        val = v_ref[...]
        out_ref[me] = val * jax.nn.sigmoid(val) if silu else val
        copies = []
        for k in range(1, N_DEV):
            peer = (_flip(x, k & 4), _flip(y, k & 2), _flip(c, k & 1))
            copies.append(pltpu.make_async_remote_copy(
                src_ref=out_ref.at[me], dst_ref=out_ref.at[me], send_sem=send_sems.at[k - 1],
                recv_sem=recv_sems.at[k - 1], device_id=peer, device_id_type=MESH))
        for cp in copies:
            cp.start()
        for cp in copies:
            cp.wait_recv()
        for cp in copies:
            cp.wait_send()

    return _pcall(body, name=name, out_shape=jax.ShapeDtypeStruct((N_DEV, 1, n), F32),
                  in_specs=[pl.BlockSpec(memory_space=pltpu.VMEM)], out_specs=pl.BlockSpec(memory_space=pltpu.VMEM),
                  scratch=[pltpu.SemaphoreType.DMA((N_DEV - 1,)), pltpu.SemaphoreType.DMA((N_DEV - 1,))])(v)


def _cast_place(w, layer, axis, me, *, name):
    _, r, c = w.shape
    tr = _rows(r, c)
    nrt = r // tr

    def body(me_ref, w_ref, o_ref):
        o_ref[...] = w_ref[...].astype(BF16)

    wspec = pl.BlockSpec((None, tr, c), lambda i, me_ref: (layer, i, 0))
    if axis == 0:
        ospec = pl.BlockSpec((tr, c), lambda i, me_ref: (me_ref[0] * nrt + i, 0))
        shape = (r * N_DEV, c)
    else:
        ospec = pl.BlockSpec((tr, c), lambda i, me_ref: (i, me_ref[0]))
        shape = (r, c * N_DEV)
    return _pcall(body, name=name, out_shape=jax.ShapeDtypeStruct(shape, BF16), grid=(nrt,), in_specs=[wspec],
                  out_specs=ospec, prefetch=1)(me, w)


def _pair_sum(grad, sib, core, axis, *, name):
    _, r, c = sib.shape
    tr = _rows(r, c // 2)
    nrt = r // tr

    def body(core_ref, g_ref, s_ref, o_ref):
        o_ref[...] = (g_ref[...].astype(F32) + s_ref[...].astype(F32)).astype(BF16)

    if axis == 0:
        gspec = pl.BlockSpec((tr, c), lambda q, i, core_ref: ((2 * q + core_ref[0]) * nrt + i, 0))
    else:
        gspec = pl.BlockSpec((tr, c), lambda q, i, core_ref: (i, 2 * q + core_ref[0]))
    sspec = pl.BlockSpec((None, tr, c), lambda q, i, core_ref: (q, i, 0))
    return _pcall(body, name=name, out_shape=jax.ShapeDtypeStruct(sib.shape, BF16), grid=(N_CHIPS, nrt),
                  in_specs=[gspec, sspec], out_specs=sspec, prefetch=1)(core, grad, sib)


ANY_SPEC = pl.BlockSpec(memory_space=pl.ANY)
SEM_SPEC = pl.BlockSpec(memory_space=pltpu.SEMAPHORE)
SPLIT_PARAMS = dict(has_side_effects=pltpu.SideEffectType.DATAFLOW_SIDE_EFFECTING)


def _split_start(copies_fn, buffers, sem_shape, after, *, name):
    n = len(buffers)
    rows, cols = sem_shape
    ns = rows * cols
    extra = ([] if after is None else [after]) + _take_token()

    def body(*refs):
        sems = refs[n + len(extra):n + len(extra) + 2 * ns]
        for cp in copies_fn(refs[:n], _sem_rows(sems[:ns], cols), _sem_rows(sems[ns:], cols)):
            cp.start()
        refs[-1][...] = jnp.zeros_like(refs[-1])

    sem = pltpu.SemaphoreType.DMA(())
    outs = pl.pallas_call(
        body, name=name,
        out_shape=((sem,) * (2 * ns) + tuple(jax.ShapeDtypeStruct(b.shape, b.dtype) for b in buffers) + (TOKEN,)),
        in_specs=(ANY_SPEC,) * (n + len(extra)),
        out_specs=(SEM_SPEC,) * (2 * ns) + (ANY_SPEC,) * n + (pl.BlockSpec(memory_space=pltpu.VMEM),),
        input_output_aliases={i: 2 * ns + i for i in range(n)},
        compiler_params=pltpu.CompilerParams(**SPLIT_PARAMS))(*buffers, *extra)
    _ORDER["token"] = outs[-1]
    return list(outs[:ns]), list(outs[ns:2 * ns]), list(outs[2 * ns:2 * ns + n]), outs[-1]


def _split_wait(copies_fn, send_sems, recv_sems, buffers, after, sem_rows, *, name):
    n, ns = len(buffers), len(send_sems)
    cols = ns // sem_rows
    extra = ([] if after is None else [after]) + _take_token()

    def body(*refs):
        sems = refs[n:n + 2 * ns]
        copies = copies_fn(refs[:n], _sem_rows(sems[:ns], cols), _sem_rows(sems[ns:], cols))
        for cp in copies:
            cp.wait_send()
        for cp in copies:
            cp.wait_recv()
        refs[-1][...] = jnp.zeros_like(refs[-1])

    outs = pl.pallas_call(
        body, name=name, out_shape=tuple(jax.ShapeDtypeStruct(b.shape, b.dtype) for b in buffers) + (TOKEN,),
        in_specs=(ANY_SPEC,) * n + (SEM_SPEC,) * (2 * ns) + (ANY_SPEC,) * len(extra),
        out_specs=(ANY_SPEC,) * n + (pl.BlockSpec(memory_space=pltpu.VMEM),),
        input_output_aliases={i: i for i in range(n)},
        compiler_params=pltpu.CompilerParams(**SPLIT_PARAMS))(*buffers, *send_sems, *recv_sems, *extra)
    _ORDER["token"] = outs[-1]
    return list(outs[:n])


def _sem_rows(sems, cols):
    return [sems[i:i + cols] for i in range(0, len(sems), cols)]


def _empty_hbm(shape, dtype):
    return pltpu.with_memory_space_constraint(lax.empty(shape, dtype), pltpu.HBM)


class _SplitGather:
    def __init__(self, fulls, axes, tag):
        self.axes, self.tag, self.nt = list(axes), tag, len(fulls)
        self.sizes = [f.shape[ax] // N_DEV for f, ax in zip(fulls, axes)]
        self.fulls = list(fulls)

    def _slot(self, ref, t, dev):
        return _shard_of(ref, self.axes[t], 4 * dev[0] + 2 * dev[1] + dev[2], self.sizes[t])

    def _first_copies(self, refs, send_sems, recv_sems):
        x, y, c = _coords()
        peers = [(x, y, 1 - c), (1 - x, y, c), (x, 1 - y, c), (1 - x, 1 - y, c)]
        return [pltpu.make_async_remote_copy(
            src_ref=self._slot(refs[t], t, (x, y, c)), dst_ref=self._slot(refs[t], t, (x, y, c)),
            send_sem=send_sems[t][k], recv_sem=recv_sems[t][k], device_id=peer, device_id_type=MESH)
            for t in range(self.nt) for k, peer in enumerate(peers)]

    def _forward_copies(self, refs, send_sems, recv_sems):
        x, y, c = _coords()
        chips = [(1 - x, y), (x, 1 - y), (1 - x, 1 - y)]
        return [pltpu.make_async_remote_copy(
            src_ref=self._slot(refs[t], t, (*chip, c)), dst_ref=self._slot(refs[t], t, (*chip, c)),
            send_sem=send_sems[t][j], recv_sem=recv_sems[t][j], device_id=(x, y, 1 - c), device_id_type=MESH)
            for t in range(self.nt) for j, chip in enumerate(chips)]

    def first(self, after):
        self.s1, self.r1, self.fulls, token = _split_start(
            self._first_copies, self.fulls, (self.nt, 4), after, name=f"comm_gather1_start_{self.tag}")
        return token

    def forward(self, after):
        bufs = _split_wait(self._first_copies, self.s1, self.r1, self.fulls, after, self.nt,
                           name=f"comm_gather1_wait_{self.tag}")
        self.s2, self.r2, self.fulls, token = _split_start(
            self._forward_copies, bufs, (self.nt, 3), after, name=f"comm_gather2_start_{self.tag}")
        return token

    def finish(self, after):
        return _split_wait(self._forward_copies, self.s2, self.r2, self.fulls, after, self.nt,
                           name=f"comm_gather2_wait_{self.tag}")


class _SplitGatherViaNeighbours:
    def __init__(self, fulls, axes, tag):
        self.axes, self.tag, self.nt = list(axes), tag, len(fulls)
        self.sizes = [f.shape[ax] // N_DEV for f, ax in zip(fulls, axes)]
        self.fulls = list(fulls)
        assert all(f.shape[0] % 32 == 0 for f in fulls)

    def _slot(self, ref, t, dev, half=None):
        idx, size = 4 * dev[0] + 2 * dev[1] + dev[2], self.sizes[t]
        if half is None:
            return _shard_of(ref, self.axes[t], idx, size)
        if self.axes[t] == 0:
            return ref.at[pl.ds(pl.multiple_of(idx * size + half * (size // 2), 16), size // 2), :]
        rows = ref.shape[0] // 2
        return ref.at[pl.ds(half * rows, rows), pl.ds(pl.multiple_of(idx * size, 128), size)]

    def _copies(self, stage, refs, send_sems, recv_sems):
        x, y, c = _coords()
        sib, xn, yn, diag = (x, y, 1 - c), (1 - x, y, c), (x, 1 - y, c), (1 - x, 1 - y, c)
        plan = {1: [((x, y, c), None, sib), ((x, y, c), None, xn), ((x, y, c), None, yn)],
                2: [(xn, None, sib), (yn, None, sib), (xn, 0, yn), (yn, 1, xn)],
                3: [(diag, 0, sib), (diag, 1, sib)]}[stage]
        return [pltpu.make_async_remote_copy(
            src_ref=self._slot(refs[t], t, block, half), dst_ref=self._slot(refs[t], t, block, half),
            send_sem=send_sems[t][k], recv_sem=recv_sems[t][k], device_id=to, device_id_type=MESH)
            for t in range(self.nt) for k, (block, half, to) in enumerate(plan)]

    def stage(self, number, after):
        if number > 1:
            self.fulls = _split_wait(functools.partial(self._copies, number - 1), self.s, self.r, self.fulls, after,
                                     self.nt, name=f"comm_gather{number - 1}_wait_{self.tag}")
        if number <= 3:
            self.s, self.r, self.fulls, _ = _split_start(
                functools.partial(self._copies, number), self.fulls, (self.nt, {1: 3, 2: 4, 3: 2}[number]), after,
                name=f"comm_gather{number}_start_{self.tag}")
        return self.fulls


class _SplitPairExchange:
    def __init__(self, grads, axes, tag):
        self.nt, self.tag, self.axes = len(grads), tag, list(axes)
        self.grads = list(grads)
        self.sizes = [g.shape[ax] // N_DEV for g, ax in zip(grads, axes)]

    def _copies(self, refs, send_sems, recv_sems):
        nt = self.nt
        x, y, c = _coords()
        return [pltpu.make_async_remote_copy(
            src_ref=_shard_of(refs[t], self.axes[t], 2 * q + 1 - c, self.sizes[t]), dst_ref=refs[nt + t].at[q],
            send_sem=send_sems[t][q], recv_sem=recv_sems[t][q], device_id=(x, y, 1 - c), device_id_type=MESH)
            for t in range(nt) for q in range(N_CHIPS)]

    def start(self):
        landing = []
        for g, ax in zip(self.grads, self.axes):
            dims = list(g.shape)
            dims[ax] //= N_DEV
            landing.append(_empty_hbm((N_CHIPS, *dims), g.dtype))
        self.s, self.r, self.bufs, token = _split_start(
            self._copies, self.grads + landing, (self.nt, N_CHIPS), None,
            name=f"comm_rs_pair_start_{self.tag}")
        return token

    def finish(self, after):
        bufs = _split_wait(self._copies, self.s, self.r, self.bufs, after, self.nt,
                           name=f"comm_rs_pair_wait_{self.tag}")
        return bufs[:self.nt], bufs[self.nt:]


class _ReducePipeline:
    def __init__(self, core):
        self.core, self.items, self.done, self.now = core, [], [], 0

    def add(self, keys, grads, layer):
        axes = [SHARD_AXIS[k] for k in keys]
        pair = _SplitPairExchange([grads[k] for k in keys], axes, f"{keys[0]}{layer}")
        pair.start()
        self.items.append(dict(keys=keys, layer=layer, axes=axes, pair=pair, state="pair", since=self.now))

    def tick(self, after, flush=False):
        self.now += 1
        for it in self.items:
            if it["state"] == "pair" and it["since"] < self.now:
                grads, sib = it["pair"].finish(after)
                sums = [_pair_sum(g, s_, self.core, ax, name="pair_sum_" + k)
                        for k, g, s_, ax in zip(it["keys"], grads, sib, it["axes"])]
                it["chip"] = _SplitChipExchange(sums, f"{it['keys'][0]}{it['layer']}")
                it["chip"].start()
                it.update(state="chip", since=self.now)
            elif it["state"] == "chip" and (flush or self.now - it["since"] >= 2):
                sums, remote = it["chip"].finish(after)
                self.done.append((it["keys"], it["layer"], sums, remote))
                it["state"] = "done"

    def take_done(self):
        out, self.done = self.done, []
        return out


class _SplitChipExchange:
    def __init__(self, sums, tag):
        self.nt, self.tag = len(sums), tag
        self.sums = list(sums)

    def _copies(self, refs, send_sems, recv_sems):
        nt = self.nt
        x, y, c = _coords()
        copies = []
        for t in range(nt):
            for k in range(1, N_CHIPS):
                px, py = _flip(x, k & 2), _flip(y, k & 1)
                copies.append(pltpu.make_async_remote_copy(
                    src_ref=refs[t].at[2 * px + py], dst_ref=refs[nt + t].at[k - 1], send_sem=send_sems[t][k - 1],
                    recv_sem=recv_sems[t][k - 1], device_id=(px, py, c), device_id_type=MESH))
        return copies

    def start(self):
        landing = [_empty_hbm((N_CHIPS - 1,) + s.shape[1:], s.dtype) for s in self.sums]
        self.s, self.r, self.bufs, token = _split_start(
            self._copies, self.sums + landing, (self.nt, N_CHIPS - 1), None,
            name=f"comm_rs_chip_start_{self.tag}")
        return token

    def finish(self, after):
        bufs = _split_wait(self._copies, self.s, self.r, self.bufs, after, self.nt,
                           name=f"comm_rs_chip_wait_{self.tag}")
        return bufs[:self.nt], bufs[self.nt:]


def _adam_math(g, w, m, v):
    m2 = ADAM_B1 * m + (1.0 - ADAM_B1) * g
    v2 = ADAM_B2 * v + (1.0 - ADAM_B2) * (g * g)
    m_hat = m2 / (1.0 - ADAM_B1 ** ADAM_STEP)
    v_hat = v2 / (1.0 - ADAM_B2 ** ADAM_STEP)
    delta = -ADAM_LR * (m_hat / (jnp.sqrt(v_hat) + ADAM_EPS) + ADAM_WD * w)
    return delta, m2, v2


def _adamw_sharded(chip_sums, remote, chip, w, m, v, layer, prev, *, name):
    nl, r, c = w.shape
    tr = _rows(r, c)

    def body(*refs):
        p_ref, r0_ref, r1_ref, r2_ref, w_ref, m_ref, v_ref = refs[1:8]
        g_out, d_out, m_out, v_out = refs[-4:]
        g = ((p_ref[...].astype(F32) + r0_ref[...].astype(F32)) + r1_ref[...].astype(F32)) + r2_ref[...].astype(F32)
        g_out[...] = g
        d_out[...], m_out[...], v_out[...] = _adam_math(g, w_ref[...], m_ref[...], v_ref[...])

    pspec = pl.BlockSpec((None, tr, c), lambda i, chip_ref: (chip_ref[0], i, 0))

    def rspec(k):
        return pl.BlockSpec((None, tr, c), lambda i, chip_ref: (k, i, 0))

    wspec = pl.BlockSpec((None, tr, c), lambda i, chip_ref: (layer, i, 0))
    in_specs = [pspec, rspec(0), rspec(1), rspec(2), wspec, wspec, wspec]
    args = [chip, chip_sums, remote, remote, remote, w, m, v]
    aliases = {}
    if prev is not None:
        in_specs += [pl.BlockSpec(memory_space=pl.ANY)] * 4
        aliases = {len(args) + i: i for i in range(4)}
        args += list(prev)
    shp = jax.ShapeDtypeStruct(w.shape, F32)
    return _pcall(body, name=name, out_shape=(shp,) * 4, grid=(r // tr,), in_specs=in_specs, out_specs=(wspec,) * 4,
                  aliases=aliases, prefetch=1)(*args)


def _adamw_local(g, w, m, v, *, name):
    nl, r, c = w.shape
    tr = _rows(r, c)

    def body(g_ref, w_ref, m_ref, v_ref, d_out, m_out, v_out):
        d_out[...], m_out[...], v_out[...] = _adam_math(g_ref[...], w_ref[...], m_ref[...], v_ref[...])

    spec = pl.BlockSpec((None, tr, c), lambda l, i: (l, i, 0))
    shp = jax.ShapeDtypeStruct(w.shape, F32)
    return _pcall(body, name=name, out_shape=(shp,) * 3, grid=(nl, r // tr), in_specs=[spec] * 4,
                  out_specs=(spec,) * 3)(g, w, m, v)


def _adamw_replicated(parts, w, m, v, *, name):
    n = w.shape[1]

    def body(p_ref, w_ref, m_ref, v_ref, g_out, d_out, m_out, v_out):
        g = p_ref[0]
        for k in range(1, N_DEV):
            g = g + p_ref[k]
        g_out[...] = g
        d_out[...], m_out[...], v_out[...] = _adam_math(g, w_ref[...], m_ref[...], v_ref[...])

    vm = pl.BlockSpec(memory_space=pltpu.VMEM)
    shp = jax.ShapeDtypeStruct((1, n), F32)
    return _pcall(body, name=name, out_shape=(shp,) * 4, in_specs=[vm] * 4, out_specs=(vm,) * 4)(parts, w, m, v)


UNDILATED_OFFS = (0, A_HEADS, OFF_VA // HEAD_DIM)


def _mod_rows(mod, d):
    return [mod[:, i * d:(i + 1) * d] for i in range(6)]


MIXER_W = ("w_in", "w_branch_a", "w_branch_b", "w_out")
FFN_W = ("w_gate_up", "w_down")
SHARD_AXIS = {"w_in": 1, "w_branch_a": 1, "w_branch_b": 1, "w_out": 0, "w_gate_up": 1, "w_down": 0}


def _norm_args(mod, gain, which, d):
    rows = _mod_rows(mod, d)
    return gain, rows[3 * which + 1], rows[3 * which]


def _mixer_fwd_a(h, u, gains, w_in, cos2, sin2, hook):
    seq = h.shape[0]
    proj = _mm(u, w_in, name="mm_in")
    hook(proj)
    qk, qk32 = _qkrope_fwd(proj, gains, cos2, sin2, name="qkrope_fwd")
    os_, lses = [], []
    for g, dil in enumerate(DILATIONS):
        if dil == 1:
            o, lse = _dil_fwd(qk, qk, proj, UNDILATED_OFFS, seq, 1, name="dil_fwd_1")
        else:
            o, lse = _dil_fwd_strided(qk32, proj, g, dil, name=f"dil_fwd_{dil}")
        os_.append(o)
        lses.append(lse)
    o_a = _combine_fwd(os_, lses, name="combine_fwd")
    o_b = _sb_fwd(proj, name="sb_fwd")
    return dict(h_in=h, u=u, proj=proj, qk=qk, qk32=qk32, os=os_, lses=lses, o_a=o_a, o_b=o_b)


def _mixer_fwd_b(sv, mod, g2, wts):
    d = sv["h_in"].shape[1]
    merged, y_a, y_b = _mm_merge(sv["o_a"], sv["o_b"], wts["w_branch_a"], wts["w_branch_b"], sv["proj"],
                                 name="mm_branch")
    h_mid, t, u2 = _mm_resid_norm(merged, wts["w_out"], sv["h_in"], _mod_rows(mod, d)[2], _norm_args(mod, g2, 1, d),
                                  name="mm_out")
    sv.update(y_a=y_a, y_b=y_b, merged=merged, t=t, h_mid=h_mid, u2=u2)
    return h_mid


def _ffn_fwd_a(sv, w_gate_up):
    a, g, u = _mm_swiglu(sv["u2"], w_gate_up, name="mm_gate_up")
    sv.update(g=g, up=u, a=a)
    return a


def _ffn_fwd_b(sv, mod, w_down, next_norm):
    d = sv["h_mid"].shape[1]
    h_out, sv["f"], u_next = _mm_resid_norm(sv["a"], w_down, sv["h_mid"], _mod_rows(mod, d)[5], next_norm,
                                            name="mm_down")
    return h_out, u_next


def _wgrad(act, dout, key):
    return _mm(act, dout, ta=True, out_dtype=BF16, caps=(2048, 1024, 3072), name="mm_wgrad_" + key)


def _ffn_bwd(dh, df, dgate2, sv, mod, g2, wts, hook):
    d = dh.shape[1]
    sc2, ga1 = _mod_rows(mod, d)[4], _mod_rows(mod, d)[2]
    dg, dup = _mm_down_t_swiglu(df, wts["w_down"], sv["g"], sv["up"], name="mm_down_t")
    grads = {"w_down": _wgrad(sv["a"], df, "w_down")}
    hook(dup)
    du2 = _mm_cat_k(dg, dup, wts["w_gate_up"], name="mm_gate_up_t")
    grads["w_gate_up"] = _mm_cat_n(sv["u2"], dg, dup, name="mm_wgrad_w_gate_up")
    dh_mid, dsh2, dsc2, dg2, dt, dgate1 = _rmsmod_bwd(du2, sv["h_mid"], g2, sc2, dh, sv["t"], ga1, name="rmsmod_bwd")
    return dh_mid, [dsh2, dsc2, dgate2], dg2, grads, dt, dgate1


def _mixer_bwd(dh_mid, dt, dgate1, sv, mod, g1, gains, wts, cos2, sin2, hook, below):
    seq, d = dh_mid.shape
    sc1 = _mod_rows(mod, d)[1]
    dy_a, dy_b, dga, dgb = _mm_out_t_merge(dt, wts["w_out"], sv["proj"], sv["y_a"], sv["y_b"], name="mm_out_t")
    grads = {"w_out": _wgrad(sv["merged"], dt, "w_out")}
    do_a = _mm(dy_a, wts["w_branch_a"], tb=True, name="mm_branch_t")
    do_b = _mm(dy_b, wts["w_branch_b"], tb=True, name="mm_branch_t")
    grads["w_branch_a"] = _wgrad(sv["o_a"], dy_a, "w_branch_a")
    grads["w_branch_b"] = _wgrad(sv["o_b"], dy_b, "w_branch_b")
    dqb, dkb, dvb = _sb_bwd(sv["proj"], do_b, name="sb_bwd")
    hook(dqb, grads)
    comb = _combine_bwd(do_a, sv["os"], sv["lses"], name="combine_bwd")
    grads = {}
    dos, dls = comb[:3], comb[3:]
    dqs, dks, dvs = [], [], []
    for g, dil in enumerate(DILATIONS):
        if dil == 1:
            dq, dk, dv = _dil_bwd(sv["qk"], sv["qk"], sv["proj"], UNDILATED_OFFS, sv["os"][g], sv["lses"][g], dos[g],
                                  dls[g], seq, 1, name="dil_bwd_1")
        else:
            dq, dk, dv = _dil_bwd_strided(sv["qk32"], sv["proj"], g, sv["os"][g], sv["lses"][g], dos[g], dls[g], dil,
                                          name=f"dil_bwd_{dil}")
        dqs.append(dq)
        dks.append(dk)
        dvs.append(dv)
    dq_pre, dqn = _qkrope_bwd(dqs, sv["proj"], gains, 0, cos2, sin2, name="qkrope_bwd")
    dk_pre, dkn = _qkrope_bwd(dks, sv["proj"], gains, 1, cos2, sin2, name="qkrope_bwd")
    dgains = jnp.stack([dqn, dkn])
    dproj = _assemble([dq_pre, dk_pre] + dvs + [dqb, dkb, dvb, dga, dgb], name="assemble_dproj")
    du = _mm(dproj, wts["w_in"], tb=True, name="mm_in_t")
    grads["w_in"] = _wgrad(sv["u"], dproj, "w_in")
    dh_in, dsh1, dsc1, dg1, df, dgate2 = _rmsmod_bwd(du, sv["h_in"], g1, sc1, dh_mid, *(below or (None, None)),
                                                     name="rmsmod_bwd")
    return dh_in, [dsh1, dsc1, dgate1], dg1, dgains, grads, df, dgate2


def kernel(x, c, w_ada, b_ada, norm1_g, norm2_g, w_in, qn_g, kn_g, w_branch_a, w_branch_b, w_out, w_gate_up, w_down, loss_target, m_w_ada, m_b_ada, m_norm1_g, m_norm2_g, m_w_in, m_qn_g, m_kn_g, m_w_branch_a, m_w_branch_b, m_w_out, m_w_gate_up, m_w_down, v_w_ada, v_b_ada, v_norm1_g, v_norm2_g, v_w_in, v_qn_g, v_kn_g, v_w_branch_a, v_w_branch_b, v_w_out, v_w_gate_up, v_w_down):
    _ORDER["token"] = None
    seq, d = x.shape[1], x.shape[2]
    depth = w_in.shape[0]
    weights = dict(w_in=w_in, w_branch_a=w_branch_a, w_branch_b=w_branch_b, w_out=w_out, w_gate_up=w_gate_up,
                   w_down=w_down)
    moments_m = dict(w_in=m_w_in, w_branch_a=m_w_branch_a, w_branch_b=m_w_branch_b, w_out=m_w_out,
                     w_gate_up=m_w_gate_up, w_down=m_w_down)
    moments_v = dict(w_in=v_w_in, w_branch_a=v_w_branch_a, w_branch_b=v_w_branch_b, w_out=v_w_out,
                     w_gate_up=v_w_gate_up, w_down=v_w_down)
    xi, yi, ci = _coords()
    me = 4 * xi + 2 * yi + ci
    core = jnp.reshape(ci, (1,)).astype(jnp.int32)
    chip = jnp.reshape(2 * xi + yi, (1,)).astype(jnp.int32)

    ada_w = w_ada.shape[2]
    c_act = _small_allgather(c, name="comm_gather_c", silu=True).reshape(N_DEV, d)
    c_pad = jnp.concatenate([c_act, jnp.zeros_like(c_act)], axis=0).astype(BF16)
    bias = lax.dynamic_slice(b_ada, (0, me * ada_w), (depth, ada_w))
    mod_part = jnp.stack([_mm(c_pad, w_ada[l], name="mm_ada")[:N_DEV] for l in range(depth)]) + bias[:, None, :]
    mod_all = _small_allgather(mod_part.reshape(1, depth * N_DEV * ada_w), name="comm_gather_mod")
    mod_all = mod_all.reshape(N_DEV, depth, N_DEV, ada_w)
    mod_mine = lax.dynamic_index_in_dim(mod_all, me, axis=2, keepdims=False)
    mods = jnp.transpose(mod_mine, (1, 0, 2)).reshape(depth, 1, 6 * d)

    cos2, sin2 = _rope_tables(seq)
    gains = [jnp.stack([qn_g[l], kn_g[l]])[:, None, :] for l in range(depth)]
    g1s = [norm1_g[l][None] for l in range(depth)]
    g2s = [norm2_g[l][None] for l in range(depth)]

    me_arr = jnp.reshape(me, (1,)).astype(jnp.int32)

    def placed(keys, l):
        return [_cast_place(weights[k], l, SHARD_AXIS[k], me_arr, name="cast_place_" + k) for k in keys]

    def gather_of(keys, l, tag):
        return _SplitGather(placed(keys, l), [SHARD_AXIS[k] for k in keys], f"{tag}{l}")

    groups = []
    for l in range(depth):
        groups += [("w_in", l, MIXER_W[:1]), ("rest", l, MIXER_W[1:]), ("up", l, FFN_W[:1]), ("down", l, FFN_W[1:])]
    gathers = {}

    first_in = _SplitGatherViaNeighbours(placed(MIXER_W[:1], 0), [SHARD_AXIS["w_in"]], "w_in0")
    first_in.stage(1, mods)
    h = x[0]
    u = _rmsmod_fwd(h, *_norm_args(mods[0], g1s[0], 0, d), name="rmsmod_fwd")
    for tag, l, keys in groups[1:]:
        gathers[tag, l] = gather_of(keys, l, tag)
    first_in.stage(2, u)
    for tag, l, keys in groups[1:]:
        gathers[tag, l].first(after=mods)
    first_in.stage(3, u)
    wm = {"w_in": first_in.stage(4, u)[0]}
    saved, full = [], []
    for l in range(depth):
        last = l + 1 == depth
        sv = _mixer_fwd_a(h, u, gains[l], wm["w_in"], cos2, sin2, gathers["rest", l].forward)
        gathers["up", l].forward(after=sv["o_b"])
        wm.update(zip(MIXER_W[1:], gathers["rest", l].finish(after=sv["o_b"])))
        h_mid = _mixer_fwd_b(sv, mods[l], g2s[l], wm)
        wf = {"w_gate_up": gathers["up", l].finish(after=h_mid)[0]}
        gathers["down", l].forward(after=h_mid)
        a = _ffn_fwd_a(sv, wf["w_gate_up"])
        wf["w_down"] = gathers["down", l].finish(after=a)[0]
        if not last:
            gathers["w_in", l + 1].forward(after=a)
        h, u = _ffn_fwd_b(sv, mods[l], wf["w_down"],
                          None if last else _norm_args(mods[l + 1], g1s[l + 1], 0, d))
        saved.append(sv)
        full.append({**wm, **wf})
        if not last:
            wm = {"w_in": gathers["w_in", l + 1].finish(after=h)[0]}
    def ffn_gate(l):
        return saved[l]["f"], _mod_rows(mods[l], d)[5]

    loss_part, dh, df, dgate2 = _loss_fwd(h, loss_target[0], *ffn_gate(depth - 1), name="loss")
    loss = lax.psum(loss_part[0, 0], ("x", "y", "c"))

    pipe = _ReducePipeline(core)
    dmods, dg1s, dg2s, dgains = [None] * depth, [None] * depth, [None] * depth, [None] * depth
    for l in reversed(range(depth)):
        dh_mid, dmod_f, dg2s[l], grads, dt, dgate1 = _ffn_bwd(dh, df, dgate2, saved[l], mods[l], g2s[l], full[l],
                                                              pipe.tick)
        pipe.tick(dh_mid)
        pipe.add(FFN_W, grads, l)
        dh, dmod_m, dg1s[l], dgains[l], grads, df, dgate2 = _mixer_bwd(
            dh_mid, dt, dgate1, saved[l], mods[l], g1s[l], gains[l], full[l], cos2, sin2,
            lambda after, early, l=l: (pipe.tick(after), pipe.add(MIXER_W[1:], early, l)),
            ffn_gate(l - 1) if l > 0 else None)
        dmods[l] = jnp.concatenate(dmod_m + dmod_f, axis=1)
        pipe.tick(dh)
        pipe.add(MIXER_W[:1], grads, l)
    grad_x = dh[None]

    stacked = {}

    def update(items):
        for keys, l, sums, remote in items:
            for k, p_, r_ in zip(keys, sums, remote):
                stacked[k] = _adamw_sharded(p_, r_, chip, weights[k], moments_m[k], moments_v[k], l,
                                            stacked.get(k), name="adamw_" + k)

    ready = pipe.take_done()
    update([it for it in ready if it[0] != FFN_W])

    small = jnp.concatenate(
        dmods + dg1s + dg2s + [dgains[l][0] for l in range(depth)] + [dgains[l][1] for l in range(depth)], axis=1)
    small_all = _small_allgather(small, name="comm_gather_small")
    pipe.tick(small_all)
    update([it for it in ready if it[0] == FFN_W] + pipe.take_done())

    def pack(b, n1, n2, qn, kn):
        return jnp.concatenate([t_.reshape(1, -1) for t_ in (b, n1, n2, qn, kn)], axis=1)

    sg, sd, sm, sv_ = _adamw_replicated(small_all, pack(b_ada, norm1_g, norm2_g, qn_g, kn_g),
                                        pack(m_b_ada, m_norm1_g, m_norm2_g, m_qn_g, m_kn_g),
                                        pack(v_b_ada, v_norm1_g, v_norm2_g, v_qn_g, v_kn_g), name="adamw_replicated")

    def unpack(p):
        sizes = [depth * 6 * d, depth * d, depth * d, depth * HEAD_DIM, depth * HEAD_DIM]
        shapes = [b_ada.shape, norm1_g.shape, norm2_g.shape, qn_g.shape, kn_g.shape]
        out, off = [], 0
        for n, shp in zip(sizes, shapes):
            out.append(p[0, off:off + n].reshape(shp))
            off += n
        return dict(zip(("b_ada", "norm1_g", "norm2_g", "qn_g", "kn_g"), out))

    ug, ud, um, uv = unpack(sg), unpack(sd), unpack(sm), unpack(sv_)
    res = {k: dict(g=ug[k], d=ud[k], m=um[k], v=uv[k]) for k in ug}

    dmod_all = small_all[:, 0, :depth * 6 * d].reshape(N_DEV, depth, 6 * d)
    g_ada = None
    for l in range(depth):
        dm = lax.dynamic_slice(dmod_all[:, l, :], (0, me * ada_w), (N_DEV, ada_w))
        dm = jnp.concatenate([dm, jnp.zeros_like(dm)], axis=0).astype(BF16)
        g_ada = _mm(c_pad, dm, ta=True, name="mm_wgrad_ada", stack=(l, depth, g_ada))
    d_ada, m_ada, v_ada = _adamw_local(g_ada, w_ada, m_w_ada, v_w_ada, name="adamw_local")
    res["w_ada"] = dict(g=g_ada, d=d_ada, m=m_ada, v=v_ada)

    pipe.tick(d_ada)
    update(pipe.take_done())
    pipe.tick(d_ada, flush=True)
    update(pipe.take_done())
    for k, (g_, d_, m_, v_) in stacked.items():
        res[k] = dict(g=g_, d=d_, m=m_, v=v_)

    order = ("w_ada", "b_ada", "norm1_g", "norm2_g", "w_in", "qn_g", "kn_g", "w_branch_a", "w_branch_b", "w_out",
             "w_gate_up", "w_down")
    _ORDER["token"] = None
    return (loss, grad_x, *[res[k]["g"] for k in order], *[res[k]["d"] for k in order],
            *[res[k]["m"] for k in order], *[res[k]["v"] for k in order])
```

```python
import functools

import jax
import jax.numpy as jnp
from jax import lax
from jax.experimental import pallas as pl
from jax.experimental.pallas import tpu as pltpu

F32 = jnp.float32
BF16 = jnp.bfloat16

HEAD_DIM = 128
BLOCK = 128
DILATIONS = (1, 4, 16)
HEADS_PER_GROUP = 4
A_HEADS = 12
SB_HEADS = 4
GROUP_W = HEADS_PER_GROUP * HEAD_DIM
A_W = A_HEADS * HEAD_DIM
B_W = SB_HEADS * HEAD_DIM
OFF_QA, OFF_KA, OFF_VA = 0, A_W, 2 * A_W
OFF_QB, OFF_KB, OFF_VB = 3 * A_W, 3 * A_W + B_W, 3 * A_W + 2 * B_W
OFF_GATES = 3 * A_W + 3 * B_W
ROPE_THETA = 10000.0
EPS = 1e-6
ATT_SCALE = HEAD_DIM ** -0.5
MASKED = -1e30

ADAM_LR, ADAM_B1, ADAM_B2, ADAM_EPS, ADAM_WD, ADAM_STEP = 0.001, 0.9, 0.999, 1e-08, 0.01, 10

N_DEV = 8
N_CHIPS = 4
V7X_VMEM_LIMIT_BYTES = 56 * 1024 * 1024
ELEMWISE_BLOCK_BYTES = 2 * 1024 * 1024
MESH = pl.DeviceIdType.MESH

NN = (((1,), (0,)), ((), ()))
NT = (((1,), (1,)), ((), ()))
TN = (((0,), (0,)), ((), ()))


def _dot(a, b, dims=NN):
    return lax.dot_general(a, b, dims, preferred_element_type=F32)


def _tile(n, cap, mult=128):
    best = None
    for t in range(mult, min(n, cap) + 1, mult):
        if n % t == 0:
            best = t
    if best is None:
        assert n <= 2 * cap, (n, cap)
        return n
    return best


def _rows(r, c):
    return _tile(r, max(16, ELEMWISE_BLOCK_BYTES // (4 * c)), 16)


_ORDER = {"token": None}
TOKEN = jax.ShapeDtypeStruct((8, 128), F32)


def _take_token():
    prev = _ORDER["token"]
    return [] if prev is None else [prev]


def _pcall(body, *, name, out_shape, grid=None, in_specs=None, out_specs=None, scratch=(), aliases=None,
           prefetch=0):
    single = not isinstance(out_shape, (tuple, list))
    out_shapes = [out_shape] if single else list(out_shape)
    out_specs = [out_specs] if single else list(out_specs)
    extra = _take_token()
    n_in, n_extra, n_out = prefetch + len(in_specs), len(extra), len(out_shapes)

    def wrapped(*refs):
        token = refs[n_in + n_extra + n_out]
        token[...] = jnp.zeros_like(token)
        return body(*refs[:n_in], *refs[n_in + n_extra:n_in + n_extra + n_out], *refs[n_in + n_extra + n_out + 1:])

    in_specs = list(in_specs) + [pl.BlockSpec(memory_space=pl.ANY)] * n_extra
    if grid is None:
        out_specs.append(pl.BlockSpec(memory_space=pltpu.VMEM))
    else:
        out_specs.append(pl.BlockSpec(TOKEN.shape, lambda *_: (0, 0)))
    kwargs = dict(name=name, out_shape=out_shapes + [TOKEN], input_output_aliases=aliases or {},
                  compiler_params=pltpu.CompilerParams(vmem_limit_bytes=V7X_VMEM_LIMIT_BYTES))
    if prefetch:
        call = pl.pallas_call(wrapped, grid_spec=pltpu.PrefetchScalarGridSpec(
            num_scalar_prefetch=prefetch, grid=grid, in_specs=in_specs, out_specs=out_specs,
            scratch_shapes=list(scratch)), **kwargs)
    else:
        if grid is not None:
            kwargs["grid"] = grid
        call = pl.pallas_call(wrapped, in_specs=in_specs, out_specs=out_specs, scratch_shapes=list(scratch), **kwargs)

    def run(*args):
        outs = call(*args, *extra)
        _ORDER["token"] = outs[-1]
        return outs[0] if single else tuple(outs[:-1])

    return run


def _mm(a, b, *, name, ta=False, tb=False, out_dtype=F32, caps=(1024, 1024, 3072), stack=None):
    kdim, m = a.shape if ta else a.shape[::-1]
    n, k2 = b.shape if tb else b.shape[::-1]
    assert kdim == k2, (a.shape, b.shape, ta, tb)
    tm, tn, tk = _tile(m, caps[0]), _tile(n, caps[1]), _tile(kdim, caps[2])
    nk = kdim // tk
    dims = (((0 if ta else 1,), (1 if tb else 0,)), ((), ()))

    def body(*refs):
        a_ref, b_ref = refs[0], refs[1]
        part = _dot(a_ref[...].astype(BF16), b_ref[...].astype(BF16), dims)
        if nk == 1:
            o_ref = refs[-1]
            o_ref[...] = part.astype(o_ref.dtype)
            return
        o_ref, acc_ref = refs[-2], refs[-1]
        k = pl.program_id(2)

        @pl.when(k == 0)
        def _():
            acc_ref[...] = part

        @pl.when(k > 0)
        def _():
            acc_ref[...] += part

        @pl.when(k == nk - 1)
        def _():
            o_ref[...] = acc_ref[...].astype(o_ref.dtype)

    a_spec = (pl.BlockSpec((tk, tm), lambda i, j, k: (k, i)) if ta
              else pl.BlockSpec((tm, tk), lambda i, j, k: (i, k)))
    b_spec = (pl.BlockSpec((tn, tk), lambda i, j, k: (j, k)) if tb
              else pl.BlockSpec((tk, tn), lambda i, j, k: (k, j)))
    ins, in_specs, aliases = [a, b], [a_spec, b_spec], {}
    if stack is None:
        out_shape = jax.ShapeDtypeStruct((m, n), out_dtype)
        out_spec = pl.BlockSpec((tm, tn), lambda i, j, k: (i, j))
    else:
        layer, n_layers, buf = stack
        out_shape = jax.ShapeDtypeStruct((n_layers, m, n), out_dtype)
        out_spec = pl.BlockSpec((None, tm, tn), lambda i, j, k: (layer, i, j))
        if buf is not None:
            ins.append(buf)
            in_specs.append(pl.BlockSpec(memory_space=pl.ANY))
            aliases = {2: 0}
    scratch = [] if nk == 1 else [pltpu.VMEM((tm, tn), F32)]
    return _pcall(body, name=name, out_shape=out_shape, grid=(m // tm, n // tn, nk), in_specs=in_specs,
                  out_specs=out_spec, scratch=scratch, aliases=aliases)(*ins)


EPILOGUE_ROWS = 256


def _row_chunks(tm):
    return [slice(r, r + EPILOGUE_ROWS) for r in range(0, tm, EPILOGUE_ROWS)] if tm > EPILOGUE_ROWS else [slice(0, tm)]


def _mm_cat_k(a_lo, a_hi, b, *, name):
    m, f = a_lo.shape
    n = b.shape[0]
    tm, tn, tk = _tile(m, 1024), _tile(n, 1024), _tile(f, 3072)
    half = f // tk
    nk = 2 * half

    def body(lo_ref, hi_ref, b_ref, o_ref, acc_ref):
        k = pl.program_id(2)

        def accumulate(a_ref):
            part = _dot(a_ref[...], b_ref[...], NT)

            @pl.when(k == 0)
            def _():
                acc_ref[...] = part

            @pl.when(k > 0)
            def _():
                acc_ref[...] += part

        pl.when(k < half)(lambda: accumulate(lo_ref))
        pl.when(k >= half)(lambda: accumulate(hi_ref))

        @pl.when(k == nk - 1)
        def _():
            o_ref[...] = acc_ref[...]

    return _pcall(body, name=name, out_shape=jax.ShapeDtypeStruct((m, n), F32), grid=(m // tm, n // tn, nk),
                  in_specs=[pl.BlockSpec((tm, tk), lambda i, j, k: (i, jnp.minimum(k, half - 1))),
                            pl.BlockSpec((tm, tk), lambda i, j, k: (i, jnp.maximum(k - half, 0))),
                            pl.BlockSpec((tn, tk), lambda i, j, k: (j, k))],
                  out_specs=pl.BlockSpec((tm, tn), lambda i, j, k: (i, j)),
                  scratch=[pltpu.VMEM((tm, tn), F32)])(a_lo, a_hi, b)


def _mm_cat_n(a, b_lo, b_hi, *, name):
    s, m = a.shape
    f = b_lo.shape[1]
    tm, tn = _tile(m, 2048), _tile(f, 1024)
    half = f // tn

    def body(a_ref, lo_ref, hi_ref, o_ref):
        j = pl.program_id(1)

        @pl.when(j < half)
        def _():
            o_ref[...] = _dot(a_ref[...], lo_ref[...], TN).astype(BF16)

        @pl.when(j >= half)
        def _():
            o_ref[...] = _dot(a_ref[...], hi_ref[...], TN).astype(BF16)

    return _pcall(body, name=name, out_shape=jax.ShapeDtypeStruct((m, 2 * f), BF16), grid=(m // tm, 2 * half),
                  in_specs=[pl.BlockSpec((s, tm), lambda i, j: (0, i)),
                            pl.BlockSpec((s, tn), lambda i, j: (0, jnp.minimum(j, half - 1))),
                            pl.BlockSpec((s, tn), lambda i, j: (0, jnp.maximum(j - half, 0)))],
                  out_specs=pl.BlockSpec((tm, tn), lambda i, j: (i, j)))(a, b_lo, b_hi)


def _mm_resid_norm(a, w, h, gate, norm, *, name):
    s, kdim = a.shape
    d = w.shape[1]
    tk = _tile(kdim, 2048)
    nk = kdim // tk
    tm = _tile(s, 256 if nk == 1 else 512)

    def body(*refs):
        a_ref, w_ref, h_ref, gate_ref = refs[:4]
        outs = refs[7:] if norm is not None else refs[4:]

        def finish(rows, t):
            hn = h_ref[rows, :] + gate_ref[...] * t
            outs[0][rows, :] = hn
            outs[1][rows, :] = t.astype(BF16)
            if norm is not None:
                g_ref, sc_ref, sh_ref = refs[4:7]
                r = lax.rsqrt(jnp.mean(hn * hn, axis=-1, keepdims=True) + EPS)
                outs[2][rows, :] = (((hn * r) * g_ref[...]) * (1.0 + sc_ref[...]) + sh_ref[...]).astype(BF16)

        if nk == 1:
            for rows in _row_chunks(tm):
                finish(rows, _dot(a_ref[rows, :], w_ref[...]))
            return
        acc_ref = refs[-1]
        k = pl.program_id(1)

        @pl.when(k == 0)
        def _():
            acc_ref[...] = _dot(a_ref[...], w_ref[...])

        @pl.when(jnp.logical_and(k > 0, k < nk - 1))
        def _():
            acc_ref[...] += _dot(a_ref[...], w_ref[...])

        @pl.when(k == nk - 1)
        def _():
            for rows in _row_chunks(tm):
                finish(rows, acc_ref[rows, :] + _dot(a_ref[rows, :], w_ref[...]))

    row = pl.BlockSpec((tm, d), lambda i, k: (i, 0))
    vec = pl.BlockSpec((1, d), lambda i, k: (0, 0))
    in_specs = [pl.BlockSpec((tm, tk), lambda i, k: (i, k)), pl.BlockSpec((tk, d), lambda i, k: (k, 0)), row, vec]
    args = [a, w, h, gate]
    out_shape = [jax.ShapeDtypeStruct((s, d), F32), jax.ShapeDtypeStruct((s, d), BF16)]
    if norm is not None:
        in_specs += [vec, vec, vec]
        args += list(norm)
        out_shape.append(jax.ShapeDtypeStruct((s, d), BF16))
    outs = _pcall(body, name=name, out_shape=tuple(out_shape), grid=(s // tm, nk), in_specs=in_specs,
                  out_specs=(row,) * len(out_shape), scratch=[] if nk == 1 else [pltpu.VMEM((tm, d), F32)])(*args)
    return outs if norm is not None else (*outs, None)


def _mm_merge(o_a, o_b, w_a, w_b, proj, *, name):
    s = o_a.shape[0]
    d = w_a.shape[1]
    tm = _tile(s, 512)
    ga_blk = OFF_GATES // d

    def body(oa_ref, ob_ref, wa_ref, wb_ref, ga_ref, gb_ref, m_ref, ya_ref, yb_ref):
        for rows in _row_chunks(tm):
            ya, yb = _dot(oa_ref[rows, :], wa_ref[...]), _dot(ob_ref[rows, :], wb_ref[...])
            m_ref[rows, :] = (jax.nn.sigmoid(ga_ref[rows, :]) * ya
                              + jax.nn.sigmoid(gb_ref[rows, :]) * yb).astype(BF16)
            ya_ref[rows, :] = ya.astype(BF16)
            yb_ref[rows, :] = yb.astype(BF16)

    row = pl.BlockSpec((tm, d), lambda i: (i, 0))
    act = pl.BlockSpec((tm, o_a.shape[1]), lambda i: (i, 0))
    wspec = pl.BlockSpec(w_a.shape, lambda i: (0, 0))
    shp = jax.ShapeDtypeStruct((s, d), BF16)
    return _pcall(body, name=name, out_shape=(shp, shp, shp), grid=(s // tm,),
                  in_specs=[act, act, wspec, wspec, pl.BlockSpec((tm, d), lambda i: (i, ga_blk)),
                            pl.BlockSpec((tm, d), lambda i: (i, ga_blk + 1))],
                  out_specs=(row, row, row))(o_a, o_b, w_a, w_b, proj, proj)


def _mm_out_t_merge(dt, w_out, proj, y_a, y_b, *, name):
    s, d = dt.shape
    tm, tn = _tile(s, 1024), _tile(d, 512)
    ga_blk = OFF_GATES // tn

    def body(dt_ref, w_ref, ga_ref, gb_ref, ya_ref, yb_ref, dya_ref, dyb_ref, dga_ref, dgb_ref):
        w = w_ref[...]
        for rows in _row_chunks(tm):
            dm = _dot(dt_ref[rows, :], w, NT)
            sa, sb = jax.nn.sigmoid(ga_ref[rows, :]), jax.nn.sigmoid(gb_ref[rows, :])
            dya_ref[rows, :] = (dm * sa).astype(BF16)
            dyb_ref[rows, :] = (dm * sb).astype(BF16)
            dga_ref[rows, :] = (dm * ya_ref[rows, :] * (sa * (1.0 - sa))).astype(BF16)
            dgb_ref[rows, :] = (dm * yb_ref[rows, :] * (sb * (1.0 - sb))).astype(BF16)

    tile = pl.BlockSpec((tm, tn), lambda i, j: (i, j))
    shp = jax.ShapeDtypeStruct((s, d), BF16)
    return _pcall(body, name=name, out_shape=(shp,) * 4, grid=(s // tm, d // tn),
                  in_specs=[pl.BlockSpec((tm, d), lambda i, j: (i, 0)), pl.BlockSpec((tn, d), lambda i, j: (j, 0)),
                            pl.BlockSpec((tm, tn), lambda i, j: (i, ga_blk + j)),
                            pl.BlockSpec((tm, tn), lambda i, j: (i, ga_blk + d // tn + j)), tile, tile],
                  out_specs=(tile,) * 4)(dt, w_out, proj, proj, y_a, y_b)


def _mm_down_t_swiglu(df, w_down, g, u, *, name):
    s, d = df.shape
    f = w_down.shape[0]
    tm, tn = _tile(s, 1024), _tile(f, 512)

    def body(df_ref, w_ref, g_ref, u_ref, dg_ref, du_ref):
        w = w_ref[...]
        for rows in _row_chunks(tm):
            da = _dot(df_ref[rows, :], w, NT)
            gf = g_ref[rows, :].astype(F32)
            sg = jax.nn.sigmoid(gf)
            dg_ref[rows, :] = (da * u_ref[rows, :].astype(F32) * (sg * (1.0 + gf * (1.0 - sg)))).astype(BF16)
            du_ref[rows, :] = (da * (gf * sg)).astype(BF16)

    tile = pl.BlockSpec((tm, tn), lambda i, j: (i, j))
    shp = jax.ShapeDtypeStruct((s, f), BF16)
    return _pcall(body, name=name, out_shape=(shp, shp), grid=(s // tm, f // tn),
                  in_specs=[pl.BlockSpec((tm, d), lambda i, j: (i, 0)), pl.BlockSpec((tn, d), lambda i, j: (j, 0)),
                            tile, tile],
                  out_specs=(tile, tile))(df, w_down, g, u)


def _rmsmod_fwd(h, g, scale, shift, *, name):
    s, d = h.shape
    ts = _rows(s, d)

    def body(h_ref, g_ref, sc_ref, sh_ref, u_ref):
        hf = h_ref[...]
        r = lax.rsqrt(jnp.mean(hf * hf, axis=-1, keepdims=True) + EPS)
        u_ref[...] = (((hf * r) * g_ref[...]) * (1.0 + sc_ref[...]) + sh_ref[...]).astype(BF16)

    row = pl.BlockSpec((ts, d), lambda i: (i, 0))
    vec = pl.BlockSpec((1, d), lambda i: (0, 0))
    return _pcall(body, name=name, out_shape=jax.ShapeDtypeStruct((s, d), BF16), grid=(s // ts,),
                  in_specs=[row, vec, vec, vec], out_specs=row)(h, g, scale, shift)


def _gate_bwd(dhf, t_ref, gate_ref, dt_ref, dgate_ref):
    dt_ref[...] = (dhf * gate_ref[...]).astype(BF16)
    dgate_ref[...] += jnp.sum(dhf * t_ref[...], axis=0, keepdims=True)


def _rmsmod_bwd(du, h, g, scale, dres, t, gate, *, name):
    s, d = h.shape
    ts = _rows(s, d)
    chain = t is not None

    def body(*refs):
        du_ref, h_ref, g_ref, sc_ref, dres_ref = refs[:5]
        dh_ref, dsh_ref, dsc_ref, dg_ref = refs[-6:-2] if chain else refs[-4:]
        sums = (dsh_ref, dsc_ref, dg_ref) + ((refs[-1],) if chain else ())

        @pl.when(pl.program_id(0) == 0)
        def _():
            for ref in sums:
                ref[...] = jnp.zeros_like(ref)

        hf, duf, gain = h_ref[...], du_ref[...], g_ref[...]
        r = lax.rsqrt(jnp.mean(hf * hf, axis=-1, keepdims=True) + EPS)
        xh = hf * r
        dn = duf * (1.0 + sc_ref[...])
        dsh_ref[...] += jnp.sum(duf, axis=0, keepdims=True)
        dsc_ref[...] += jnp.sum(duf * (xh * gain), axis=0, keepdims=True)
        dg_ref[...] += jnp.sum(dn * xh, axis=0, keepdims=True)
        dxh = dn * gain
        dh = dres_ref[...] + r * (dxh - xh * jnp.mean(dxh * xh, axis=-1, keepdims=True))
        dh_ref[...] = dh
        if chain:
            _gate_bwd(dh, refs[5], refs[6], refs[-2], refs[-1])

    row = pl.BlockSpec((ts, d), lambda i: (i, 0))
    vec = pl.BlockSpec((1, d), lambda i: (0, 0))
    vshape = jax.ShapeDtypeStruct((1, d), F32)
    out_shape, out_specs = [jax.ShapeDtypeStruct((s, d), F32), vshape, vshape, vshape], [row, vec, vec, vec]
    in_specs, args = [row, row, vec, vec, row], [du, h, g, scale, dres]
    if chain:
        in_specs, args = in_specs + [row, vec], args + [t, gate]
        out_shape, out_specs = out_shape + [jax.ShapeDtypeStruct((s, d), BF16), vshape], out_specs + [row, vec]
    outs = _pcall(body, name=name, out_shape=tuple(out_shape), grid=(s // ts,), in_specs=in_specs,
                  out_specs=tuple(out_specs))(*args)
    return outs if chain else (*outs, None, None)


def _mm_swiglu(u2, w_gate_up, *, name):
    s, d = u2.shape
    f = w_gate_up.shape[1] // 2
    tm, tn = _tile(s, 1024), _tile(f, 512)
    nj = f // tn

    def body(x_ref, wg_ref, wu_ref, a_ref, g_ref, u_ref):
        for rows in _row_chunks(tm):
            x = x_ref[rows, :]
            gf, uf = _dot(x, wg_ref[...]), _dot(x, wu_ref[...])
            a_ref[rows, :] = ((gf * jax.nn.sigmoid(gf)) * uf).astype(BF16)
            g_ref[rows, :] = gf.astype(BF16)
            u_ref[rows, :] = uf.astype(BF16)

    out = pl.BlockSpec((tm, tn), lambda i, j: (i, j))
    shp = jax.ShapeDtypeStruct((s, f), BF16)
    return _pcall(body, name=name, out_shape=(shp, shp, shp), grid=(s // tm, nj),
                  in_specs=[pl.BlockSpec((tm, d), lambda i, j: (i, 0)), pl.BlockSpec((d, tn), lambda i, j: (0, j)),
                            pl.BlockSpec((d, tn), lambda i, j: (0, nj + j))],
                  out_specs=(out, out, out))(u2, w_gate_up, w_gate_up)


def _loss_fwd(y, tgt, t, gate, *, name):
    s, d = y.shape
    ts = _rows(s, d)

    def body(y_ref, tgt_ref, t_ref, gate_ref, l_ref, dy_ref, dt_ref, dgate_ref):
        @pl.when(pl.program_id(0) == 0)
        def _():
            l_ref[...] = jnp.zeros_like(l_ref)
            dgate_ref[...] = jnp.zeros_like(dgate_ref)

        e = y_ref[...] - tgt_ref[...]
        dy = e * (1.0 / d)
        dy_ref[...] = dy
        per_tok = jnp.sum(e * e, axis=1, keepdims=True) * (1.0 / d)
        l_ref[...] += 0.5 * jnp.sum(per_tok, axis=0, keepdims=True)
        _gate_bwd(dy, t_ref, gate_ref, dt_ref, dgate_ref)

    row = pl.BlockSpec((ts, d), lambda i: (i, 0))
    vec = pl.BlockSpec((1, d), lambda i: (0, 0))
    return _pcall(body, name=name,
                  out_shape=(jax.ShapeDtypeStruct((1, 128), F32), jax.ShapeDtypeStruct((s, d), F32),
                             jax.ShapeDtypeStruct((s, d), BF16), jax.ShapeDtypeStruct((1, d), F32)),
                  grid=(s // ts,), in_specs=[row, row, row, vec],
                  out_specs=(pl.BlockSpec((1, 128), lambda i: (0, 0)), row, row, vec))(y, tgt, t, gate)


def _rope_tables(seq):
    inv = jnp.power(ROPE_THETA, -jnp.arange(0, HEAD_DIM, 2, dtype=F32) / HEAD_DIM)
    ang = jnp.arange(seq, dtype=F32)[:, None] * inv[None, :]
    cos, sin = jnp.cos(ang), jnp.sin(ang)
    return jnp.concatenate([cos, cos], axis=1), jnp.concatenate([-sin, sin], axis=1)


def _qkrope_fwd(proj, gains, cos2, sin2, *, name):
    s = proj.shape[0]
    ts = _rows(s, A_W)

    def body(x_ref, g_ref, c_ref, s_ref, o_ref, o32_ref):
        gain, cos, sin = g_ref[...], c_ref[...], s_ref[...]
        for h in range(A_HEADS):
            lanes = slice(h * HEAD_DIM, (h + 1) * HEAD_DIM)
            x = x_ref[:, lanes]
            y = (x * lax.rsqrt(jnp.mean(x * x, axis=-1, keepdims=True) + EPS)) * gain
            out = y * cos + pltpu.roll(y, HEAD_DIM // 2, 1) * sin
            o_ref[:, lanes] = out.astype(BF16)
            o32_ref[:, lanes] = out

    heads = pl.BlockSpec((ts, A_W), lambda i, j: (i, j))
    tab = pl.BlockSpec((ts, HEAD_DIM), lambda i, j: (i, 0))
    gain = pl.BlockSpec((None, 1, HEAD_DIM), lambda i, j: (j, 0, 0))
    return _pcall(body, name=name,
                  out_shape=(jax.ShapeDtypeStruct((s, 2 * A_W), BF16), jax.ShapeDtypeStruct((s, 2 * A_W), F32)),
                  grid=(s // ts, 2), in_specs=[heads, gain, tab, tab], out_specs=(heads, heads))(
                      proj, gains, cos2, sin2)


def _qkrope_bwd(d_groups, proj, gains, which, cos2, sin2, *, name):
    s = proj.shape[0]
    ts = _rows(s, A_W)

    def body(d0_ref, d1_ref, d2_ref, x_ref, g_ref, c_ref, s_ref, dx_ref, dg_ref):
        @pl.when(pl.program_id(0) == 0)
        def _():
            dg_ref[...] = jnp.zeros_like(dg_ref)

        gain, cos, sin = g_ref[...], c_ref[...], s_ref[...]
        dg = jnp.zeros((1, HEAD_DIM), F32)
        for h in range(A_HEADS):
            lanes = slice(h * HEAD_DIM, (h + 1) * HEAD_DIM)
            slot = slice((h % HEADS_PER_GROUP) * HEAD_DIM, (h % HEADS_PER_GROUP + 1) * HEAD_DIM)
            dout = (d0_ref, d1_ref, d2_ref)[h // HEADS_PER_GROUP][:, slot]
            dy = dout * cos + pltpu.roll(dout * sin, HEAD_DIM // 2, 1)
            x = x_ref[:, lanes]
            r = lax.rsqrt(jnp.mean(x * x, axis=-1, keepdims=True) + EPS)
            xh = x * r
            dg = dg + jnp.sum(dy * xh, axis=0, keepdims=True)
            dxh = dy * gain
            dx_ref[:, lanes] = (r * (dxh - xh * jnp.mean(dxh * xh, axis=-1, keepdims=True))).astype(BF16)
        dg_ref[...] += dg

    group = pl.BlockSpec((ts, GROUP_W), lambda i: (i, 0))
    tab = pl.BlockSpec((ts, HEAD_DIM), lambda i: (i, 0))
    gain = pl.BlockSpec((None, 1, HEAD_DIM), lambda i: (which, 0, 0))
    return _pcall(body, name=name,
                  out_shape=(jax.ShapeDtypeStruct((s, A_W), BF16), jax.ShapeDtypeStruct((1, HEAD_DIM), F32)),
                  grid=(s // ts,),
                  in_specs=[group, group, group, pl.BlockSpec((ts, A_W), lambda i: (i, which)), gain, tab, tab],
                  out_specs=(pl.BlockSpec((ts, A_W), lambda i: (i, 0)), pl.BlockSpec((1, HEAD_DIM), lambda i: (0, 0))))(
                      *d_groups, proj, gains, cos2, sin2)


def _assemble(pieces, *, name):
    s = pieces[0].shape[0]
    widths = [p.shape[1] for p in pieces]
    total = sum(widths)
    ts = _rows(s, total // 2)

    def body(*refs):
        o_ref, off = refs[-1], 0
        for x_ref, w in zip(refs[:-1], widths):
            o_ref[:, off:off + w] = x_ref[...].astype(BF16)
            off += w

    return _pcall(body, name=name, out_shape=jax.ShapeDtypeStruct((s, total), BF16), grid=(s // ts,),
                  in_specs=[pl.BlockSpec((ts, w), lambda i: (i, 0)) for w in widths],
                  out_specs=pl.BlockSpec((ts, total), lambda i: (i, 0)))(*pieces)


def _block_rows(blk):
    if isinstance(blk, int):
        return pl.ds(blk * BLOCK, BLOCK)
    return pl.ds(pl.multiple_of(blk * BLOCK, BLOCK), BLOCK)


def _band_window(n, length):
    width = min(2 * BLOCK, length)
    row = lax.broadcasted_iota(jnp.int32, (BLOCK, width), 0)
    col = lax.broadcasted_iota(jnp.int32, (BLOCK, width), 1)
    if width == BLOCK:
        return pl.ds(0, BLOCK), col <= row
    first = n - 1 if isinstance(n, int) else jnp.maximum(n - 1, 0)
    first = max(first, 0) if isinstance(first, int) else first
    dist = row - col + (n - first) * BLOCK
    start = first * BLOCK if isinstance(first, int) else pl.multiple_of(first * BLOCK, BLOCK)
    return pl.ds(start, width), jnp.logical_and(dist >= 0, dist <= BLOCK)


def _dil_fwd(q_arr, k_arr, v_arr, offs, length, dil, *, name):
    nj, nb = dil * HEADS_PER_GROUP, length // BLOCK
    ju, nq = (HEADS_PER_GROUP, 2) if nb > 1 else (2 * HEADS_PER_GROUP, 1)
    qo, ko, vo = (off // ju for off in offs)
    assert all(off % ju == 0 for off in offs) and nb % nq == 0 and nj % ju == 0

    def body(q_ref, k_ref, v_ref, o_ref, l_ref):
        for qq in range(nq):
            qrows = slice(qq * BLOCK, (qq + 1) * BLOCK)
            rows, mask = _band_window(pl.program_id(1) * nq + qq, length)
            for cb in range(ju):
                lanes = slice(cb * HEAD_DIM, (cb + 1) * HEAD_DIM)
                sc = _dot(q_ref[qrows, lanes].astype(BF16), k_ref[rows, lanes].astype(BF16), NT) * ATT_SCALE
                sc = jnp.where(mask, sc, MASKED)
                m = sc.max(axis=-1, keepdims=True)
                p = jnp.exp(sc - m)
                den = jnp.sum(p, axis=-1, keepdims=True)
                acc = _dot(p.astype(BF16), v_ref[rows, lanes].astype(BF16))
                o_ref[qrows, lanes] = acc / den
                l_ref[qrows, lanes] = jnp.broadcast_to(m + jnp.log(den), (BLOCK, HEAD_DIM))

    qspec = pl.BlockSpec((nq * BLOCK, ju * HEAD_DIM), lambda j, n: (n, qo + j))
    kspec = pl.BlockSpec((length, ju * HEAD_DIM), lambda j, n: (0, ko + j))
    vspec = pl.BlockSpec((length, ju * HEAD_DIM), lambda j, n: (0, vo + j))
    ospec = pl.BlockSpec((nq * BLOCK, ju * HEAD_DIM), lambda j, n: (n, j))
    shp = jax.ShapeDtypeStruct((length, nj * HEAD_DIM), F32)
    return _pcall(body, name=name, out_shape=(shp, shp), grid=(nj // ju, nb // nq), in_specs=[qspec, kspec, vspec],
                  out_specs=(ospec, ospec))(q_arr, k_arr, v_arr)


def _dil_bwd(q_arr, k_arr, v_arr, offs, o, lse, do, dlse, length, dil, *, name):
    nj, nb = dil * HEADS_PER_GROUP, length // BLOCK
    ju = 2 * HEADS_PER_GROUP if length <= 4 * BLOCK else 2
    qo, ko, vo = (off // ju for off in offs)
    assert all(off % ju == 0 for off in offs)

    def body(q_ref, k_ref, v_ref, o_ref, l_ref, do_ref, dl_ref, dq_ref, dk_ref, dv_ref):
        dk_ref[...] = jnp.zeros_like(dk_ref)
        dv_ref[...] = jnp.zeros_like(dv_ref)

        def step(n, carry):
            qrows = _block_rows(n)
            rows, mask = _band_window(n, length)
            for cb in range(ju):
                lanes = slice(cb * HEAD_DIM, (cb + 1) * HEAD_DIM)
                q = q_ref[qrows, lanes].astype(BF16)
                dof = do_ref[qrows, lanes]
                dob = dof.astype(BF16)
                lse_c = l_ref[qrows, lanes][:, :1]
                shift = dl_ref[qrows, lanes][:, :1] - jnp.sum(dof * o_ref[qrows, lanes], axis=-1, keepdims=True)
                kk, vv = k_ref[rows, lanes].astype(BF16), v_ref[rows, lanes].astype(BF16)
                sc = _dot(q, kk, NT) * ATT_SCALE
                p = jnp.where(mask, jnp.exp(sc - lse_c), 0.0)
                ds = (p * (_dot(dob, vv, NT) + shift)).astype(BF16)
                dq_ref[qrows, lanes] = _dot(ds, kk) * ATT_SCALE
                dk_ref[rows, lanes] += _dot(ds, q, TN) * ATT_SCALE
                dv_ref[rows, lanes] += _dot(p.astype(BF16), dob, TN)
            return carry

        if nb == 1:
            step(0, 0)
        else:
            lax.fori_loop(0, nb, step, 0)

    def col(off):
        return pl.BlockSpec((length, ju * HEAD_DIM), lambda j: (0, off + j))

    shp = jax.ShapeDtypeStruct((length, nj * HEAD_DIM), F32)
    return _pcall(body, name=name, out_shape=(shp, shp, shp), grid=(nj // ju,),
                  in_specs=[col(qo), col(ko), col(vo), col(0), col(0), col(0), col(0)],
                  out_specs=(col(0), col(0), col(0)))(q_arr, k_arr, v_arr, o, lse, do, dlse)


DIL_RESIDUES_PER_STEP = 8


def _dil_tokens(n, r, dil, length):
    width = min(2 * BLOCK, length)
    _, mask = _band_window(n, length)
    first = 0 if width == BLOCK else jnp.maximum(n - 1, 0)
    return (pl.ds(n * (BLOCK * dil) + r, BLOCK, stride=dil), pl.ds(first * (BLOCK * dil) + r, width, stride=dil),
            mask)


def _dil_head_specs(seq, group):
    first = group * HEADS_PER_GROUP

    def col(c0):
        return pl.BlockSpec((seq, HEAD_DIM), lambda h, r: (0, c0 + h))

    return col(first), col(A_HEADS + first), col(OFF_VA // HEAD_DIM + first), col(0)


def _dil_fwd_strided(qk32, proj, group, dil, *, name):
    seq = proj.shape[0]
    length = seq // dil
    nb, rp = length // BLOCK, min(dil, DIL_RESIDUES_PER_STEP)
    assert dil % rp == 0

    def body(q_ref, k_ref, v_ref, o_ref, l_ref):
        rgroup = pl.program_id(1)

        def step(n, carry):
            for rr in range(rp):
                tok_q, tok_k, mask = _dil_tokens(n, rgroup * rp + rr, dil, length)
                sc = _dot(q_ref[tok_q, :].astype(BF16), k_ref[tok_k, :].astype(BF16), NT) * ATT_SCALE
                sc = jnp.where(mask, sc, MASKED)
                m = sc.max(axis=-1, keepdims=True)
                p = jnp.exp(sc - m)
                den = jnp.sum(p, axis=-1, keepdims=True)
                o_ref[tok_q, :] = _dot(p.astype(BF16), v_ref[tok_k, :].astype(BF16)) / den
                l_ref[tok_q, :] = jnp.broadcast_to(m + jnp.log(den), (BLOCK, HEAD_DIM))
            return carry

        if nb == 1:
            step(0, 0)
        else:
            lax.fori_loop(0, nb, step, 0)

    qs, ks, vs, nat = _dil_head_specs(seq, group)
    shp = jax.ShapeDtypeStruct((seq, GROUP_W), F32)
    return _pcall(body, name=name, out_shape=(shp, shp), grid=(HEADS_PER_GROUP, dil // rp), in_specs=[qs, ks, vs],
                  out_specs=(nat, nat))(qk32, qk32, proj)


def _dil_bwd_strided(qk32, proj, group, o, lse, do, dlse, dil, *, name):
    seq = proj.shape[0]
    length = seq // dil
    nb, rp = length // BLOCK, min(dil, DIL_RESIDUES_PER_STEP)
    assert dil % rp == 0

    def body(q_ref, k_ref, v_ref, o_ref, l_ref, do_ref, dl_ref, dq_ref, dk_ref, dv_ref):
        rgroup = pl.program_id(1)

        @pl.when(rgroup == 0)
        def _():
            dk_ref[...] = jnp.zeros_like(dk_ref)
            dv_ref[...] = jnp.zeros_like(dv_ref)

        def step(n, carry):
            for rr in range(rp):
                tok_q, tok_k, mask = _dil_tokens(n, rgroup * rp + rr, dil, length)
                q = q_ref[tok_q, :].astype(BF16)
                dof = do_ref[tok_q, :]
                dob = dof.astype(BF16)
                lse_c = l_ref[tok_q, :][:, :1]
                shift = dl_ref[tok_q, :][:, :1] - jnp.sum(dof * o_ref[tok_q, :], axis=-1, keepdims=True)
                kk, vv = k_ref[tok_k, :].astype(BF16), v_ref[tok_k, :].astype(BF16)
                sc = _dot(q, kk, NT) * ATT_SCALE
                p = jnp.where(mask, jnp.exp(sc - lse_c), 0.0)
                ds = (p * (_dot(dob, vv, NT) + shift)).astype(BF16)
                dq_ref[tok_q, :] = _dot(ds, kk) * ATT_SCALE
                dk_ref[tok_k, :] += _dot(ds, q, TN) * ATT_SCALE
                dv_ref[tok_k, :] += _dot(p.astype(BF16), dob, TN)
            return carry

        if nb == 1:
            step(0, 0)
        else:
            lax.fori_loop(0, nb, step, 0)

    qs, ks, vs, nat = _dil_head_specs(seq, group)
    shp = jax.ShapeDtypeStruct((seq, GROUP_W), F32)
    return _pcall(body, name=name, out_shape=(shp, shp, shp), grid=(HEADS_PER_GROUP, dil // rp),
                  in_specs=[qs, ks, vs, nat, nat, nat, nat], out_specs=(nat, nat, nat))(
                      qk32, qk32, proj, o, lse, do, dlse)


def _combine_weights(l_refs):
    ls = [r[...] for r in l_refs]
    m = jnp.maximum(jnp.maximum(ls[0], ls[1]), ls[2])
    es = [jnp.exp(l - m) for l in ls]
    den = es[0] + es[1] + es[2]
    return [e / den for e in es]


def _combine_fwd(os_, lses, *, name):
    s = os_[0].shape[0]
    ts = _rows(s, GROUP_W)

    def body(o0, o1, o2, l0, l1, l2, out_ref):
        w = _combine_weights((l0, l1, l2))
        out_ref[...] = (w[0] * o0[...] + w[1] * o1[...] + w[2] * o2[...]).astype(BF16)

    row = pl.BlockSpec((ts, GROUP_W), lambda i: (i, 0))
    return _pcall(body, name=name, out_shape=jax.ShapeDtypeStruct((s, GROUP_W), BF16), grid=(s // ts,),
                  in_specs=[row] * 6, out_specs=row)(*os_, *lses)


def _combine_bwd(do_a, os_, lses, *, name):
    s = do_a.shape[0]
    ts = _rows(s, GROUP_W)

    def body(d_ref, o0, o1, o2, l0, l1, l2, do0, do1, do2, dl0, dl1, dl2):
        w = _combine_weights((l0, l1, l2))
        d = d_ref[...]
        og = [o0[...], o1[...], o2[...]]
        oa = w[0] * og[0] + w[1] * og[1] + w[2] * og[2]
        ta = jnp.sum(d * oa, axis=-1, keepdims=True)
        for g, (do_ref, dl_ref) in enumerate(((do0, dl0), (do1, dl1), (do2, dl2))):
            do_ref[...] = w[g] * d
            dl_ref[...] = w[g] * (jnp.sum(d * og[g], axis=-1, keepdims=True) - ta)

    head = pl.BlockSpec((ts, HEAD_DIM), lambda i, h: (i, h))
    shp = jax.ShapeDtypeStruct((s, GROUP_W), F32)
    return _pcall(body, name=name, out_shape=(shp,) * 6, grid=(s // ts, HEADS_PER_GROUP),
                  in_specs=[head] * 7, out_specs=(head,) * 6)(do_a, *os_, *lses)


def _dot_exact(x, ones_mask):
    hi = x.astype(BF16)
    r1 = x - hi.astype(F32)
    mid = r1.astype(BF16)
    lo = (r1 - mid.astype(F32)).astype(BF16)
    return _dot(hi, ones_mask) + _dot(mid, ones_mask) + _dot(lo, ones_mask)


SB_QROWS = 4 * BLOCK
SB_UNROLL = 4
SB_HEADS_PER_STEP = 2
SB_LANES = [slice(hh * HEAD_DIM, (hh + 1) * HEAD_DIM) for hh in range(SB_HEADS_PER_STEP)]


def _sb_mask(j, i):
    row = lax.broadcasted_iota(jnp.int32, (SB_QROWS, BLOCK), 0)
    col = lax.broadcasted_iota(jnp.int32, (SB_QROWS, BLOCK), 1)
    return col + (j * BLOCK - i * SB_QROWS) < row


def _sb_steps(i):
    return ((i + 1) * (SB_QROWS // BLOCK) + SB_UNROLL - 1) // SB_UNROLL


def _sb_scores(q, kk, j, i, masked):
    mask = _sb_mask(j, i) if masked else None
    z = _dot(q, kk, NT) * ATT_SCALE
    sp = jnp.log(1.0 + jnp.exp(-jnp.abs(z)))
    log_beta = jnp.minimum(z, 0.0) - sp
    log_1mb = jnp.minimum(-z, 0.0) - sp
    if masked:
        log_1mb = jnp.where(mask, log_1mb, 0.0)
    return z, log_beta, log_1mb, mask


def _sb_weights(log_beta, log_1mb, mask, run, upper):
    a = jnp.exp(log_beta + (run + _dot_exact(log_1mb, upper)))
    return a if mask is None else jnp.where(mask, a, 0.0)


def _sb_peeled(nsteps, make_step, init, masked_first):
    if masked_first:
        return lax.fori_loop(1, nsteps, make_step(False), make_step(True)(0, init))
    return make_step(True)(nsteps - 1, lax.fori_loop(0, nsteps - 1, make_step(False), init))


def _tri(strict_lower):
    row = lax.broadcasted_iota(jnp.int32, (BLOCK, BLOCK), 0)
    col = lax.broadcasted_iota(jnp.int32, (BLOCK, BLOCK), 1)
    return ((row > col) if strict_lower else (row < col)).astype(BF16)


def _sb_fwd(proj, *, name):
    s = proj.shape[0]
    assert s % (BLOCK * SB_UNROLL) == 0 and s % SB_QROWS == 0

    def body(q_ref, k_ref, v_ref, o_ref):
        i = pl.program_id(1)
        qs = [q_ref[:, lanes].astype(BF16) for lanes in SB_LANES]
        upper = _tri(True)
        nsteps = _sb_steps(i)

        def make_step(masked):
            def step(t, carry):
                carry = list(carry)
                for b in reversed(range(SB_UNROLL)):
                    j = (nsteps - 1 - t) * SB_UNROLL + b
                    rows = _block_rows(j)
                    for hh, lanes in enumerate(SB_LANES):
                        acc, run = carry[hh]
                        _, log_beta, log_1mb, mask = _sb_scores(qs[hh], k_ref[rows, lanes].astype(BF16), j, i, masked)
                        a = _sb_weights(log_beta, log_1mb, mask, run, upper)
                        carry[hh] = (acc + _dot(a.astype(BF16), v_ref[rows, lanes].astype(BF16)),
                                     run + jnp.sum(log_1mb, axis=-1, keepdims=True))
                return tuple(carry)
            return step

        zero = (jnp.zeros((SB_QROWS, HEAD_DIM), F32), jnp.zeros((SB_QROWS, 1), F32))
        for lanes, (acc, _) in zip(SB_LANES, _sb_peeled(nsteps, make_step, (zero,) * SB_HEADS_PER_STEP, True)):
            o_ref[:, lanes] = acc.astype(BF16)

    width = SB_HEADS_PER_STEP * HEAD_DIM
    qb, kb, vb = (off // width for off in (OFF_QB, OFF_KB, OFF_VB))
    return _pcall(body, name=name, out_shape=jax.ShapeDtypeStruct((s, B_W), BF16),
                  grid=(SB_HEADS // SB_HEADS_PER_STEP, s // SB_QROWS),
                  in_specs=[pl.BlockSpec((SB_QROWS, width), lambda h, i: (i, qb + h)),
                            pl.BlockSpec((s, width), lambda h, i: (0, kb + h)),
                            pl.BlockSpec((s, width), lambda h, i: (0, vb + h))],
                  out_specs=pl.BlockSpec((SB_QROWS, width), lambda h, i: (i, h)))(proj, proj, proj)


def _sb_bwd(proj, do_b, *, name):
    s = proj.shape[0]
    assert s % (BLOCK * SB_UNROLL) == 0 and s % SB_QROWS == 0
    nkb = s // BLOCK

    def body(q_ref, k_ref, v_ref, do_ref, dq_ref, dk_ref, dv_ref, z_s, a_s):
        i = pl.program_id(1)

        @pl.when(i == 0)
        def _():
            dk_ref[...] = jnp.zeros_like(dk_ref)
            dv_ref[...] = jnp.zeros_like(dv_ref)

        qs = [q_ref[:, lanes].astype(BF16) for lanes in SB_LANES]
        dobs = [do_ref[:, lanes].astype(BF16) for lanes in SB_LANES]
        upper, lower = _tri(True), _tri(False)
        nsteps = _sb_steps(i)

        def make_recompute(masked):
            def recompute(t, runs):
                runs = list(runs)
                for b in reversed(range(SB_UNROLL)):
                    j = (nsteps - 1 - t) * SB_UNROLL + b
                    rows = _block_rows(j)
                    for hh, lanes in enumerate(SB_LANES):
                        z, log_beta, log_1mb, mask = _sb_scores(qs[hh], k_ref[rows, lanes].astype(BF16), j, i, masked)
                        z_s[hh, j] = z
                        a_s[hh, j] = _sb_weights(log_beta, log_1mb, mask, runs[hh], upper)
                        runs[hh] = runs[hh] + jnp.sum(log_1mb, axis=-1, keepdims=True)
                return tuple(runs)
            return recompute

        _sb_peeled(nsteps, make_recompute, (jnp.zeros((SB_QROWS, 1), F32),) * SB_HEADS_PER_STEP, True)

        def make_grads(masked):
            def grads(t, carry):
                carry = list(carry)
                for b in range(SB_UNROLL):
                    j = t * SB_UNROLL + b
                    rows = _block_rows(j)
                    for hh, lanes in enumerate(SB_LANES):
                        dq, run = carry[hh]
                        kk, vv = k_ref[rows, lanes].astype(BF16), v_ref[rows, lanes].astype(BF16)
                        z, a = z_s[hh, j], a_s[hh, j]
                        de = _dot(dobs[hh], vv, NT) * a
                        beta = jax.nn.sigmoid(z)
                        one_minus_beta = 1.0 - beta
                        if masked:
                            beta = jnp.where(_sb_mask(j, i), beta, 0.0)
                        dz = (de * one_minus_beta - beta * (run + _dot_exact(de, lower))).astype(BF16)
                        dk_ref[rows, lanes] += _dot(dz, qs[hh], TN) * ATT_SCALE
                        dv_ref[rows, lanes] += _dot(a.astype(BF16), dobs[hh], TN)
                        carry[hh] = (dq + _dot(dz, kk), run + jnp.sum(de, axis=-1, keepdims=True))
                return tuple(carry)
            return grads

        zero = (jnp.zeros((SB_QROWS, HEAD_DIM), F32), jnp.zeros((SB_QROWS, 1), F32))
        for lanes, (dq, _) in zip(SB_LANES, _sb_peeled(nsteps, make_grads, (zero,) * SB_HEADS_PER_STEP, False)):
            dq_ref[:, lanes] = dq * ATT_SCALE

    width = SB_HEADS_PER_STEP * HEAD_DIM
    qb, kb, vb = (off // width for off in (OFF_QB, OFF_KB, OFF_VB))
    blk = pl.BlockSpec((SB_QROWS, width), lambda h, i: (i, h))
    full = pl.BlockSpec((s, width), lambda h, i: (0, h))
    shp = jax.ShapeDtypeStruct((s, B_W), F32)
    saved = pltpu.VMEM((SB_HEADS_PER_STEP, nkb, SB_QROWS, BLOCK), F32)
    return _pcall(body, name=name, out_shape=(shp, shp, shp), grid=(SB_HEADS // SB_HEADS_PER_STEP, s // SB_QROWS),
                  in_specs=[pl.BlockSpec((SB_QROWS, width), lambda h, i: (i, qb + h)),
                            pl.BlockSpec((s, width), lambda h, i: (0, kb + h)),
                            pl.BlockSpec((s, width), lambda h, i: (0, vb + h)), blk],
                  out_specs=(blk, full, full), scratch=[saved, saved])(proj, proj, proj, do_b)


def _coords():
    return lax.axis_index("x"), lax.axis_index("y"), lax.axis_index("c")


def _flip(v, bit):
    return 1 - v if bit else v


def _shard_of(ref, axis, idx, size):
    if axis == 0:
        sl = pl.ds(pl.multiple_of(idx * size, 16), size)
        return ref.at[sl, :] if len(ref.shape) == 2 else ref.at[:, sl, :]
    sl = pl.ds(pl.multiple_of(idx * size, 128), size)
    return ref.at[:, sl] if len(ref.shape) == 2 else ref.at[:, :, sl]


def _small_allgather(v, *, name, silu=False):
    n = v.shape[1]

    def body(v_ref, out_ref, send_sems, recv_sems):
        x, y, c = _coords()
        me = 4 * x + 2 * y + c
        val = v_ref[...]
        out_ref[me] = val * jax.nn.sigmoid(val) if silu else val
        copies = []
        for k in range(1, N_DEV):
            peer = (_flip(x, k & 4), _flip(y, k & 2), _flip(c, k & 1))
            copies.append(pltpu.make_async_remote_copy(
                src_ref=out_ref.at[me], dst_ref=out_ref.at[me], send_sem=send_sems.at[k - 1],
                recv_sem=recv_sems.at[k - 1], device_id=peer, device_id_type=MESH))
        for cp in copies:
            cp.start()
        for cp in copies:
            cp.wait_recv()
        for cp in copies:
            cp.wait_send()

    return _pcall(body, name=name, out_shape=jax.ShapeDtypeStruct((N_DEV, 1, n), F32),
                  in_specs=[pl.BlockSpec(memory_space=pltpu.VMEM)], out_specs=pl.BlockSpec(memory_space=pltpu.VMEM),
                  scratch=[pltpu.SemaphoreType.DMA((N_DEV - 1,)), pltpu.SemaphoreType.DMA((N_DEV - 1,))])(v)


def _cast_place(w, layer, axis, me, *, name):
    _, r, c = w.shape
    tr = _rows(r, c)
    nrt = r // tr

    def body(me_ref, w_ref, o_ref):
        o_ref[...] = w_ref[...].astype(BF16)

    wspec = pl.BlockSpec((None, tr, c), lambda i, me_ref: (layer, i, 0))
    if axis == 0:
        ospec = pl.BlockSpec((tr, c), lambda i, me_ref: (me_ref[0] * nrt + i, 0))
        shape = (r * N_DEV, c)
    else:
        ospec = pl.BlockSpec((tr, c), lambda i, me_ref: (i, me_ref[0]))
        shape = (r, c * N_DEV)
    return _pcall(body, name=name, out_shape=jax.ShapeDtypeStruct(shape, BF16), grid=(nrt,), in_specs=[wspec],
                  out_specs=ospec, prefetch=1)(me, w)


def _pair_sum(grad, sib, core, axis, *, name):
    _, r, c = sib.shape
    tr = _rows(r, c // 2)
    nrt = r // tr

    def body(core_ref, g_ref, s_ref, o_ref):
        o_ref[...] = (g_ref[...].astype(F32) + s_ref[...].astype(F32)).astype(BF16)

    if axis == 0:
        gspec = pl.BlockSpec((tr, c), lambda q, i, core_ref: ((2 * q + core_ref[0]) * nrt + i, 0))
    else:
        gspec = pl.BlockSpec((tr, c), lambda q, i, core_ref: (i, 2 * q + core_ref[0]))
    sspec = pl.BlockSpec((None, tr, c), lambda q, i, core_ref: (q, i, 0))
    return _pcall(body, name=name, out_shape=jax.ShapeDtypeStruct(sib.shape, BF16), grid=(N_CHIPS, nrt),
                  in_specs=[gspec, sspec], out_specs=sspec, prefetch=1)(core, grad, sib)


ANY_SPEC = pl.BlockSpec(memory_space=pl.ANY)
SEM_SPEC = pl.BlockSpec(memory_space=pltpu.SEMAPHORE)
SPLIT_PARAMS = dict(has_side_effects=pltpu.SideEffectType.DATAFLOW_SIDE_EFFECTING)


def _split_start(copies_fn, buffers, sem_shape, after, *, name):
    n = len(buffers)
    rows, cols = sem_shape
    ns = rows * cols
    extra = ([] if after is None else [after]) + _take_token()

    def body(*refs):
        sems = refs[n + len(extra):n + len(extra) + 2 * ns]
        for cp in copies_fn(refs[:n], _sem_rows(sems[:ns], cols), _sem_rows(sems[ns:], cols)):
            cp.start()
        refs[-1][...] = jnp.zeros_like(refs[-1])

    sem = pltpu.SemaphoreType.DMA(())
    outs = pl.pallas_call(
        body, name=name,
        out_shape=((sem,) * (2 * ns) + tuple(jax.ShapeDtypeStruct(b.shape, b.dtype) for b in buffers) + (TOKEN,)),
        in_specs=(ANY_SPEC,) * (n + len(extra)),
        out_specs=(SEM_SPEC,) * (2 * ns) + (ANY_SPEC,) * n + (pl.BlockSpec(memory_space=pltpu.VMEM),),
        input_output_aliases={i: 2 * ns + i for i in range(n)},
        compiler_params=pltpu.CompilerParams(**SPLIT_PARAMS))(*buffers, *extra)
    _ORDER["token"] = outs[-1]
    return list(outs[:ns]), list(outs[ns:2 * ns]), list(outs[2 * ns:2 * ns + n]), outs[-1]


def _split_wait(copies_fn, send_sems, recv_sems, buffers, after, sem_rows, *, name):
    n, ns = len(buffers), len(send_sems)
    cols = ns // sem_rows
    extra = ([] if after is None else [after]) + _take_token()

    def body(*refs):
        sems = refs[n:n + 2 * ns]
        copies = copies_fn(refs[:n], _sem_rows(sems[:ns], cols), _sem_rows(sems[ns:], cols))
        for cp in copies:
            cp.wait_send()
        for cp in copies:
            cp.wait_recv()
        refs[-1][...] = jnp.zeros_like(refs[-1])

    outs = pl.pallas_call(
        body, name=name, out_shape=tuple(jax.ShapeDtypeStruct(b.shape, b.dtype) for b in buffers) + (TOKEN,),
        in_specs=(ANY_SPEC,) * n + (SEM_SPEC,) * (2 * ns) + (ANY_SPEC,) * len(extra),
        out_specs=(ANY_SPEC,) * n + (pl.BlockSpec(memory_space=pltpu.VMEM),),
        input_output_aliases={i: i for i in range(n)},
        compiler_params=pltpu.CompilerParams(**SPLIT_PARAMS))(*buffers, *send_sems, *recv_sems, *extra)
    _ORDER["token"] = outs[-1]
    return list(outs[:n])


def _sem_rows(sems, cols):
    return [sems[i:i + cols] for i in range(0, len(sems), cols)]


def _empty_hbm(shape, dtype):
    return pltpu.with_memory_space_constraint(lax.empty(shape, dtype), pltpu.HBM)


class _SplitGather:
    def __init__(self, fulls, axes, tag):
        self.axes, self.tag, self.nt = list(axes), tag, len(fulls)
        self.sizes = [f.shape[ax] // N_DEV for f, ax in zip(fulls, axes)]
        self.fulls = list(fulls)

    def _slot(self, ref, t, dev):
        return _shard_of(ref, self.axes[t], 4 * dev[0] + 2 * dev[1] + dev[2], self.sizes[t])

    def _first_copies(self, refs, send_sems, recv_sems):
        x, y, c = _coords()
        peers = [(x, y, 1 - c), (1 - x, y, c), (x, 1 - y, c), (1 - x, 1 - y, c)]
        return [pltpu.make_async_remote_copy(
            src_ref=self._slot(refs[t], t, (x, y, c)), dst_ref=self._slot(refs[t], t, (x, y, c)),
            send_sem=send_sems[t][k], recv_sem=recv_sems[t][k], device_id=peer, device_id_type=MESH)
            for t in range(self.nt) for k, peer in enumerate(peers)]

    def _forward_copies(self, refs, send_sems, recv_sems):
        x, y, c = _coords()
        chips = [(1 - x, y), (x, 1 - y), (1 - x, 1 - y)]
        return [pltpu.make_async_remote_copy(
            src_ref=self._slot(refs[t], t, (*chip, c)), dst_ref=self._slot(refs[t], t, (*chip, c)),
            send_sem=send_sems[t][j], recv_sem=recv_sems[t][j], device_id=(x, y, 1 - c), device_id_type=MESH)
            for t in range(self.nt) for j, chip in enumerate(chips)]

    def first(self, after):
        self.s1, self.r1, self.fulls, token = _split_start(
            self._first_copies, self.fulls, (self.nt, 4), after, name=f"comm_gather1_start_{self.tag}")
        return token

    def forward(self, after):
        bufs = _split_wait(self._first_copies, self.s1, self.r1, self.fulls, after, self.nt,
                           name=f"comm_gather1_wait_{self.tag}")
        self.s2, self.r2, self.fulls, token = _split_start(
            self._forward_copies, bufs, (self.nt, 3), after, name=f"comm_gather2_start_{self.tag}")
        return token

    def finish(self, after):
        return _split_wait(self._forward_copies, self.s2, self.r2, self.fulls, after, self.nt,
                           name=f"comm_gather2_wait_{self.tag}")


class _SplitGatherViaNeighbours:
    def __init__(self, fulls, axes, tag):
        self.axes, self.tag, self.nt = list(axes), tag, len(fulls)
        self.sizes = [f.shape[ax] // N_DEV for f, ax in zip(fulls, axes)]
        self.fulls = list(fulls)
        assert all(f.shape[0] % 32 == 0 for f in fulls)

    def _slot(self, ref, t, dev, half=None):
        idx, size = 4 * dev[0] + 2 * dev[1] + dev[2], self.sizes[t]
        if half is None:
            return _shard_of(ref, self.axes[t], idx, size)
        if self.axes[t] == 0:
            return ref.at[pl.ds(pl.multiple_of(idx * size + half * (size // 2), 16), size // 2), :]
        rows = ref.shape[0] // 2
        return ref.at[pl.ds(half * rows, rows), pl.ds(pl.multiple_of(idx * size, 128), size)]

    def _copies(self, stage, refs, send_sems, recv_sems):
        x, y, c = _coords()
        sib, xn, yn, diag = (x, y, 1 - c), (1 - x, y, c), (x, 1 - y, c), (1 - x, 1 - y, c)
        plan = {1: [((x, y, c), None, sib), ((x, y, c), None, xn), ((x, y, c), None, yn)],
                2: [(xn, None, sib), (yn, None, sib), (xn, 0, yn), (yn, 1, xn)],
                3: [(diag, 0, sib), (diag, 1, sib)]}[stage]
        return [pltpu.make_async_remote_copy(
            src_ref=self._slot(refs[t], t, block, half), dst_ref=self._slot(refs[t], t, block, half),
            send_sem=send_sems[t][k], recv_sem=recv_sems[t][k], device_id=to, device_id_type=MESH)
            for t in range(self.nt) for k, (block, half, to) in enumerate(plan)]

    def stage(self, number, after):
        if number > 1:
            self.fulls = _split_wait(functools.partial(self._copies, number - 1), self.s, self.r, self.fulls, after,
                                     self.nt, name=f"comm_gather{number - 1}_wait_{self.tag}")
        if number <= 3:
            self.s, self.r, self.fulls, _ = _split_start(
                functools.partial(self._copies, number), self.fulls, (self.nt, {1: 3, 2: 4, 3: 2}[number]), after,
                name=f"comm_gather{number}_start_{self.tag}")
        return self.fulls


class _SplitPairExchange:
    def __init__(self, grads, axes, tag):
        self.nt, self.tag, self.axes = len(grads), tag, list(axes)
        self.grads = list(grads)
        self.sizes = [g.shape[ax] // N_DEV for g, ax in zip(grads, axes)]

    def _copies(self, refs, send_sems, recv_sems):
        nt = self.nt
        x, y, c = _coords()
        return [pltpu.make_async_remote_copy(
            src_ref=_shard_of(refs[t], self.axes[t], 2 * q + 1 - c, self.sizes[t]), dst_ref=refs[nt + t].at[q],
            send_sem=send_sems[t][q], recv_sem=recv_sems[t][q], device_id=(x, y, 1 - c), device_id_type=MESH)
            for t in range(nt) for q in range(N_CHIPS)]

    def start(self):
        landing = []
        for g, ax in zip(self.grads, self.axes):
            dims = list(g.shape)
            dims[ax] //= N_DEV
            landing.append(_empty_hbm((N_CHIPS, *dims), g.dtype))
        self.s, self.r, self.bufs, token = _split_start(
            self._copies, self.grads + landing, (self.nt, N_CHIPS), None,
            name=f"comm_rs_pair_start_{self.tag}")
        return token

    def finish(self, after):
        bufs = _split_wait(self._copies, self.s, self.r, self.bufs, after, self.nt,
                           name=f"comm_rs_pair_wait_{self.tag}")
        return bufs[:self.nt], bufs[self.nt:]


class _ReducePipeline:
    def __init__(self, core):
        self.core, self.items, self.done, self.now = core, [], [], 0

    def add(self, keys, grads, layer):
        axes = [SHARD_AXIS[k] for k in keys]
        pair = _SplitPairExchange([grads[k] for k in keys], axes, f"{keys[0]}{layer}")
        pair.start()
        self.items.append(dict(keys=keys, layer=layer, axes=axes, pair=pair, state="pair", since=self.now))

    def tick(self, after, flush=False):
        self.now += 1
        for it in self.items:
            if it["state"] == "pair" and it["since"] < self.now:
                grads, sib = it["pair"].finish(after)
                sums = [_pair_sum(g, s_, self.core, ax, name="pair_sum_" + k)
                        for k, g, s_, ax in zip(it["keys"], grads, sib, it["axes"])]
                it["chip"] = _SplitChipExchange(sums, f"{it['keys'][0]}{it['layer']}")
                it["chip"].start()
                it.update(state="chip", since=self.now)
            elif it["state"] == "chip" and (flush or self.now - it["since"] >= 2):
                sums, remote = it["chip"].finish(after)
                self.done.append((it["keys"], it["layer"], sums, remote))
                it["state"] = "done"

    def take_done(self):
        out, self.done = self.done, []
        return out


class _SplitChipExchange:
    def __init__(self, sums, tag):
        self.nt, self.tag = len(sums), tag
        self.sums = list(sums)

    def _copies(self, refs, send_sems, recv_sems):
        nt = self.nt
        x, y, c = _coords()
        copies = []
        for t in range(nt):
            for k in range(1, N_CHIPS):
                px, py = _flip(x, k & 2), _flip(y, k & 1)
                copies.append(pltpu.make_async_remote_copy(
                    src_ref=refs[t].at[2 * px + py], dst_ref=refs[nt + t].at[k - 1], send_sem=send_sems[t][k - 1],
                    recv_sem=recv_sems[t][k - 1], device_id=(px, py, c), device_id_type=MESH))
        return copies

    def start(self):
        landing = [_empty_hbm((N_CHIPS - 1,) + s.shape[1:], s.dtype) for s in self.sums]
        self.s, self.r, self.bufs, token = _split_start(
            self._copies, self.sums + landing, (self.nt, N_CHIPS - 1), None,
            name=f"comm_rs_chip_start_{self.tag}")
        return token

    def finish(self, after):
        bufs = _split_wait(self._copies, self.s, self.r, self.bufs, after, self.nt,
                           name=f"comm_rs_chip_wait_{self.tag}")
        return bufs[:self.nt], bufs[self.nt:]


def _adam_math(g, w, m, v):
    m2 = ADAM_B1 * m + (1.0 - ADAM_B1) * g
    v2 = ADAM_B2 * v + (1.0 - ADAM_B2) * (g * g)
    m_hat = m2 / (1.0 - ADAM_B1 ** ADAM_STEP)
    v_hat = v2 / (1.0 - ADAM_B2 ** ADAM_STEP)
    delta = -ADAM_LR * (m_hat / (jnp.sqrt(v_hat) + ADAM_EPS) + ADAM_WD * w)
    return delta, m2, v2


def _adamw_sharded(chip_sums, remote, chip, w, m, v, layer, prev, *, name):
    nl, r, c = w.shape
    tr = _rows(r, c)

    def body(*refs):
        p_ref, r0_ref, r1_ref, r2_ref, w_ref, m_ref, v_ref = refs[1:8]
        g_out, d_out, m_out, v_out = refs[-4:]
        g = ((p_ref[...].astype(F32) + r0_ref[...].astype(F32)) + r1_ref[...].astype(F32)) + r2_ref[...].astype(F32)
        g_out[...] = g
        d_out[...], m_out[...], v_out[...] = _adam_math(g, w_ref[...], m_ref[...], v_ref[...])

    pspec = pl.BlockSpec((None, tr, c), lambda i, chip_ref: (chip_ref[0], i, 0))

    def rspec(k):
        return pl.BlockSpec((None, tr, c), lambda i, chip_ref: (k, i, 0))

    wspec = pl.BlockSpec((None, tr, c), lambda i, chip_ref: (layer, i, 0))
    in_specs = [pspec, rspec(0), rspec(1), rspec(2), wspec, wspec, wspec]
    args = [chip, chip_sums, remote, remote, remote, w, m, v]
    aliases = {}
    if prev is not None:
        in_specs += [pl.BlockSpec(memory_space=pl.ANY)] * 4
        aliases = {len(args) + i: i for i in range(4)}
        args += list(prev)
    shp = jax.ShapeDtypeStruct(w.shape, F32)
    return _pcall(body, name=name, out_shape=(shp,) * 4, grid=(r // tr,), in_specs=in_specs, out_specs=(wspec,) * 4,
                  aliases=aliases, prefetch=1)(*args)


def _adamw_local(g, w, m, v, *, name):
    nl, r, c = w.shape
    tr = _rows(r, c)

    def body(g_ref, w_ref, m_ref, v_ref, d_out, m_out, v_out):
        d_out[...], m_out[...], v_out[...] = _adam_math(g_ref[...], w_ref[...], m_ref[...], v_ref[...])

    spec = pl.BlockSpec((None, tr, c), lambda l, i: (l, i, 0))
    shp = jax.ShapeDtypeStruct(w.shape, F32)
    return _pcall(body, name=name, out_shape=(shp,) * 3, grid=(nl, r // tr), in_specs=[spec] * 4,
                  out_specs=(spec,) * 3)(g, w, m, v)


def _adamw_replicated(parts, w, m, v, *, name):
    n = w.shape[1]

    def body(p_ref, w_ref, m_ref, v_ref, g_out, d_out, m_out, v_out):
        g = p_ref[0]
        for k in range(1, N_DEV):
            g = g + p_ref[k]
        g_out[...] = g
        d_out[...], m_out[...], v_out[...] = _adam_math(g, w_ref[...], m_ref[...], v_ref[...])

    vm = pl.BlockSpec(memory_space=pltpu.VMEM)
    shp = jax.ShapeDtypeStruct((1, n), F32)
    return _pcall(body, name=name, out_shape=(shp,) * 4, in_specs=[vm] * 4, out_specs=(vm,) * 4)(parts, w, m, v)


UNDILATED_OFFS = (0, A_HEADS, OFF_VA // HEAD_DIM)


def _mod_rows(mod, d):
    return [mod[:, i * d:(i + 1) * d] for i in range(6)]


MIXER_W = ("w_in", "w_branch_a", "w_branch_b", "w_out")
FFN_W = ("w_gate_up", "w_down")
SHARD_AXIS = {"w_in": 1, "w_branch_a": 1, "w_branch_b": 1, "w_out": 0, "w_gate_up": 1, "w_down": 0}


def _norm_args(mod, gain, which, d):
    rows = _mod_rows(mod, d)
    return gain, rows[3 * which + 1], rows[3 * which]


def _mixer_fwd_a(h, u, gains, w_in, cos2, sin2, hook):
    seq = h.shape[0]
    proj = _mm(u, w_in, name="mm_in")
    hook(proj)
    qk, qk32 = _qkrope_fwd(proj, gains, cos2, sin2, name="qkrope_fwd")
    os_, lses = [], []
    for g, dil in enumerate(DILATIONS):
        if dil == 1:
            o, lse = _dil_fwd(qk, qk, proj, UNDILATED_OFFS, seq, 1, name="dil_fwd_1")
        else:
            o, lse = _dil_fwd_strided(qk32, proj, g, dil, name=f"dil_fwd_{dil}")
        os_.append(o)
        lses.append(lse)
    o_a = _combine_fwd(os_, lses, name="combine_fwd")
    o_b = _sb_fwd(proj, name="sb_fwd")
    return dict(h_in=h, u=u, proj=proj, qk=qk, qk32=qk32, os=os_, lses=lses, o_a=o_a, o_b=o_b)


def _mixer_fwd_b(sv, mod, g2, wts):
    d = sv["h_in"].shape[1]
    merged, y_a, y_b = _mm_merge(sv["o_a"], sv["o_b"], wts["w_branch_a"], wts["w_branch_b"], sv["proj"],
                                 name="mm_branch")
    h_mid, t, u2 = _mm_resid_norm(merged, wts["w_out"], sv["h_in"], _mod_rows(mod, d)[2], _norm_args(mod, g2, 1, d),
                                  name="mm_out")
    sv.update(y_a=y_a, y_b=y_b, merged=merged, t=t, h_mid=h_mid, u2=u2)
    return h_mid


def _ffn_fwd_a(sv, w_gate_up):
    a, g, u = _mm_swiglu(sv["u2"], w_gate_up, name="mm_gate_up")
    sv.update(g=g, up=u, a=a)
    return a


def _ffn_fwd_b(sv, mod, w_down, next_norm):
    d = sv["h_mid"].shape[1]
    h_out, sv["f"], u_next = _mm_resid_norm(sv["a"], w_down, sv["h_mid"], _mod_rows(mod, d)[5], next_norm,
                                            name="mm_down")
    return h_out, u_next


def _wgrad(act, dout, key):
    return _mm(act, dout, ta=True, out_dtype=BF16, caps=(2048, 1024, 3072), name="mm_wgrad_" + key)


def _ffn_bwd(dh, df, dgate2, sv, mod, g2, wts, hook):
    d = dh.shape[1]
    sc2, ga1 = _mod_rows(mod, d)[4], _mod_rows(mod, d)[2]
    dg, dup = _mm_down_t_swiglu(df, wts["w_down"], sv["g"], sv["up"], name="mm_down_t")
    grads = {"w_down": _wgrad(sv["a"], df, "w_down")}
    hook(dup)
    du2 = _mm_cat_k(dg, dup, wts["w_gate_up"], name="mm_gate_up_t")
    grads["w_gate_up"] = _mm_cat_n(sv["u2"], dg, dup, name="mm_wgrad_w_gate_up")
    dh_mid, dsh2, dsc2, dg2, dt, dgate1 = _rmsmod_bwd(du2, sv["h_mid"], g2, sc2, dh, sv["t"], ga1, name="rmsmod_bwd")
    return dh_mid, [dsh2, dsc2, dgate2], dg2, grads, dt, dgate1


def _mixer_bwd(dh_mid, dt, dgate1, sv, mod, g1, gains, wts, cos2, sin2, hook, below):
    seq, d = dh_mid.shape
    sc1 = _mod_rows(mod, d)[1]
    dy_a, dy_b, dga, dgb = _mm_out_t_merge(dt, wts["w_out"], sv["proj"], sv["y_a"], sv["y_b"], name="mm_out_t")
    grads = {"w_out": _wgrad(sv["merged"], dt, "w_out")}
    do_a = _mm(dy_a, wts["w_branch_a"], tb=True, name="mm_branch_t")
    do_b = _mm(dy_b, wts["w_branch_b"], tb=True, name="mm_branch_t")
    grads["w_branch_a"] = _wgrad(sv["o_a"], dy_a, "w_branch_a")
    grads["w_branch_b"] = _wgrad(sv["o_b"], dy_b, "w_branch_b")
    dqb, dkb, dvb = _sb_bwd(sv["proj"], do_b, name="sb_bwd")
    hook(dqb, grads)
    comb = _combine_bwd(do_a, sv["os"], sv["lses"], name="combine_bwd")
    grads = {}
    dos, dls = comb[:3], comb[3:]
    dqs, dks, dvs = [], [], []
    for g, dil in enumerate(DILATIONS):
        if dil == 1:
            dq, dk, dv = _dil_bwd(sv["qk"], sv["qk"], sv["proj"], UNDILATED_OFFS, sv["os"][g], sv["lses"][g], dos[g],
                                  dls[g], seq, 1, name="dil_bwd_1")
        else:
            dq, dk, dv = _dil_bwd_strided(sv["qk32"], sv["proj"], g, sv["os"][g], sv["lses"][g], dos[g], dls[g], dil,
                                          name=f"dil_bwd_{dil}")
        dqs.append(dq)
        dks.append(dk)
        dvs.append(dv)
    dq_pre, dqn = _qkrope_bwd(dqs, sv["proj"], gains, 0, cos2, sin2, name="qkrope_bwd")
    dk_pre, dkn = _qkrope_bwd(dks, sv["proj"], gains, 1, cos2, sin2, name="qkrope_bwd")
    dgains = jnp.stack([dqn, dkn])
    dproj = _assemble([dq_pre, dk_pre] + dvs + [dqb, dkb, dvb, dga, dgb], name="assemble_dproj")
    du = _mm(dproj, wts["w_in"], tb=True, name="mm_in_t")
    grads["w_in"] = _wgrad(sv["u"], dproj, "w_in")
    dh_in, dsh1, dsc1, dg1, df, dgate2 = _rmsmod_bwd(du, sv["h_in"], g1, sc1, dh_mid, *(below or (None, None)),
                                                     name="rmsmod_bwd")
    return dh_in, [dsh1, dsc1, dgate1], dg1, dgains, grads, df, dgate2


def kernel(x, c, w_ada, b_ada, norm1_g, norm2_g, w_in, qn_g, kn_g, w_branch_a, w_branch_b, w_out, w_gate_up, w_down, loss_target, m_w_ada, m_b_ada, m_norm1_g, m_norm2_g, m_w_in, m_qn_g, m_kn_g, m_w_branch_a, m_w_branch_b, m_w_out, m_w_gate_up, m_w_down, v_w_ada, v_b_ada, v_norm1_g, v_norm2_g, v_w_in, v_qn_g, v_kn_g, v_w_branch_a, v_w_branch_b, v_w_out, v_w_gate_up, v_w_down):
    _ORDER["token"] = None
    seq, d = x.shape[1], x.shape[2]
    depth = w_in.shape[0]
    weights = dict(w_in=w_in, w_branch_a=w_branch_a, w_branch_b=w_branch_b, w_out=w_out, w_gate_up=w_gate_up,
                   w_down=w_down)
    moments_m = dict(w_in=m_w_in, w_branch_a=m_w_branch_a, w_branch_b=m_w_branch_b, w_out=m_w_out,
                     w_gate_up=m_w_gate_up, w_down=m_w_down)
    moments_v = dict(w_in=v_w_in, w_branch_a=v_w_branch_a, w_branch_b=v_w_branch_b, w_out=v_w_out,
                     w_gate_up=v_w_gate_up, w_down=v_w_down)
    xi, yi, ci = _coords()
    me = 4 * xi + 2 * yi + ci
    core = jnp.reshape(ci, (1,)).astype(jnp.int32)
    chip = jnp.reshape(2 * xi + yi, (1,)).astype(jnp.int32)

    ada_w = w_ada.shape[2]
    c_act = _small_allgather(c, name="comm_gather_c", silu=True).reshape(N_DEV, d)
    c_pad = jnp.concatenate([c_act, jnp.zeros_like(c_act)], axis=0).astype(BF16)
    bias = lax.dynamic_slice(b_ada, (0, me * ada_w), (depth, ada_w))
    mod_part = jnp.stack([_mm(c_pad, w_ada[l], name="mm_ada")[:N_DEV] for l in range(depth)]) + bias[:, None, :]
    mod_all = _small_allgather(mod_part.reshape(1, depth * N_DEV * ada_w), name="comm_gather_mod")
    mod_all = mod_all.reshape(N_DEV, depth, N_DEV, ada_w)
    mod_mine = lax.dynamic_index_in_dim(mod_all, me, axis=2, keepdims=False)
    mods = jnp.transpose(mod_mine, (1, 0, 2)).reshape(depth, 1, 6 * d)

    cos2, sin2 = _rope_tables(seq)
    gains = [jnp.stack([qn_g[l], kn_g[l]])[:, None, :] for l in range(depth)]
    g1s = [norm1_g[l][None] for l in range(depth)]
    g2s = [norm2_g[l][None] for l in range(depth)]

    me_arr = jnp.reshape(me, (1,)).astype(jnp.int32)

    def placed(keys, l):
        return [_cast_place(weights[k], l, SHARD_AXIS[k], me_arr, name="cast_place_" + k) for k in keys]

    def gather_of(keys, l, tag):
        return _SplitGather(placed(keys, l), [SHARD_AXIS[k] for k in keys], f"{tag}{l}")

    groups = []
    for l in range(depth):
        groups += [("w_in", l, MIXER_W[:1]), ("rest", l, MIXER_W[1:]), ("up", l, FFN_W[:1]), ("down", l, FFN_W[1:])]
    gathers = {}

    def via_neighbours(l):
        return _SplitGatherViaNeighbours(placed(MIXER_W[:1], l), [SHARD_AXIS["w_in"]], f"w_in{l}")

    ins = [via_neighbours(0)]
    ins[0].stage(1, mods)
    h = x[0]
    u = _rmsmod_fwd(h, *_norm_args(mods[0], g1s[0], 0, d), name="rmsmod_fwd")
    for tag, l, keys in groups:
        if tag != "w_in":
            gathers[tag, l] = gather_of(keys, l, tag)
        elif l > 0:
            ins.append(via_neighbours(l))

    def second_stage(l, after):
        ins[l].stage(2, after)
        for tag, l2, _ in groups:
            if l2 == l and tag != "w_in":
                gathers[tag, l].first(after=mods)
        if l + 1 < depth:
            ins[l + 1].stage(1, mods)

    second_stage(0, u)
    ins[0].stage(3, u)
    wm = {"w_in": ins[0].stage(4, u)[0]}
    saved, full = [], []
    for l in range(depth):
        last = l + 1 == depth
        sv = _mixer_fwd_a(h, u, gains[l], wm["w_in"], cos2, sin2, gathers["rest", l].forward)
        gathers["up", l].forward(after=sv["o_b"])
        wm.update(zip(MIXER_W[1:], gathers["rest", l].finish(after=sv["o_b"])))
        h_mid = _mixer_fwd_b(sv, mods[l], g2s[l], wm)
        wf = {"w_gate_up": gathers["up", l].finish(after=h_mid)[0]}
        gathers["down", l].forward(after=h_mid)
        a = _ffn_fwd_a(sv, wf["w_gate_up"])
        wf["w_down"] = gathers["down", l].finish(after=a)[0]
        if not last:
            second_stage(l + 1, a)
        h, u = _ffn_fwd_b(sv, mods[l], wf["w_down"],
                          None if last else _norm_args(mods[l + 1], g1s[l + 1], 0, d))
        saved.append(sv)
        full.append({**wm, **wf})
        if not last:
            ins[l + 1].stage(3, h)
            wm = {"w_in": ins[l + 1].stage(4, h)[0]}
    def ffn_gate(l):
        return saved[l]["f"], _mod_rows(mods[l], d)[5]

    loss_part, dh, df, dgate2 = _loss_fwd(h, loss_target[0], *ffn_gate(depth - 1), name="loss")
    loss = lax.psum(loss_part[0, 0], ("x", "y", "c"))

    pipe = _ReducePipeline(core)
    dmods, dg1s, dg2s, dgains = [None] * depth, [None] * depth, [None] * depth, [None] * depth
    for l in reversed(range(depth)):
        dh_mid, dmod_f, dg2s[l], grads, dt, dgate1 = _ffn_bwd(dh, df, dgate2, saved[l], mods[l], g2s[l], full[l],
                                                              pipe.tick)
        pipe.tick(dh_mid)
        pipe.add(FFN_W, grads, l)
        dh, dmod_m, dg1s[l], dgains[l], grads, df, dgate2 = _mixer_bwd(
            dh_mid, dt, dgate1, saved[l], mods[l], g1s[l], gains[l], full[l], cos2, sin2,
            lambda after, early, l=l: (pipe.tick(after), pipe.add(MIXER_W[1:], early, l)),
            ffn_gate(l - 1) if l > 0 else None)
        dmods[l] = jnp.concatenate(dmod_m + dmod_f, axis=1)
        pipe.tick(dh)
        pipe.add(MIXER_W[:1], grads, l)
    grad_x = dh[None]

    stacked = {}

    def update(items):
        for keys, l, sums, remote in items:
            for k, p_, r_ in zip(keys, sums, remote):
                stacked[k] = _adamw_sharded(p_, r_, chip, weights[k], moments_m[k], moments_v[k], l,
                                            stacked.get(k), name="adamw_" + k)

    ready = pipe.take_done()
    update([it for it in ready if it[0] != FFN_W])

    small = jnp.concatenate(
        dmods + dg1s + dg2s + [dgains[l][0] for l in range(depth)] + [dgains[l][1] for l in range(depth)], axis=1)
    small_all = _small_allgather(small, name="comm_gather_small")
    pipe.tick(small_all)
    update([it for it in ready if it[0] == FFN_W] + pipe.take_done())

    def pack(b, n1, n2, qn, kn):
        return jnp.concatenate([t_.reshape(1, -1) for t_ in (b, n1, n2, qn, kn)], axis=1)

    sg, sd, sm, sv_ = _adamw_replicated(small_all, pack(b_ada, norm1_g, norm2_g, qn_g, kn_g),
                                        pack(m_b_ada, m_norm1_g, m_norm2_g, m_qn_g, m_kn_g),
                                        pack(v_b_ada, v_norm1_g, v_norm2_g, v_qn_g, v_kn_g), name="adamw_replicated")

    def unpack(p):
        sizes = [depth * 6 * d, depth * d, depth * d, depth * HEAD_DIM, depth * HEAD_DIM]
        shapes = [b_ada.shape, norm1_g.shape, norm2_g.shape, qn_g.shape, kn_g.shape]
        out, off = [], 0
        for n, shp in zip(sizes, shapes):
            out.append(p[0, off:off + n].reshape(shp))
            off += n
        return dict(zip(("b_ada", "norm1_g", "norm2_g", "qn_g", "kn_g"), out))

    ug, ud, um, uv = unpack(sg), unpack(sd), unpack(sm), unpack(sv_)
    res = {k: dict(g=ug[k], d=ud[k], m=um[k], v=uv[k]) for k in ug}

    dmod_all = small_all[:, 0, :depth * 6 * d].reshape(N_DEV, depth, 6 * d)
    g_ada = None
    for l in range(depth):
        dm = lax.dynamic_slice(dmod_all[:, l, :], (0, me * ada_w), (N_DEV, ada_w))
        dm = jnp.concatenate([dm, jnp.zeros_like(dm)], axis=0).astype(BF16)
        g_ada = _mm(c_pad, dm, ta=True, name="mm_wgrad_ada", stack=(l, depth, g_ada))
    d_ada, m_ada, v_ada = _adamw_local(g_ada, w_ada, m_w_ada, v_w_ada, name="adamw_local")
    res["w_ada"] = dict(g=g_ada, d=d_ada, m=m_ada, v=v_ada)

    pipe.tick(d_ada)
    update(pipe.take_done())
    pipe.tick(d_ada, flush=True)
    update(pipe.take_done())
    for k, (g_, d_, m_, v_) in stacked.items():
        res[k] = dict(g=g_, d=d_, m=m_, v=v_)

    order = ("w_ada", "b_ada", "norm1_g", "norm2_g", "w_in", "qn_g", "kn_g", "w_branch_a", "w_branch_b", "w_out",
             "w_gate_up", "w_down")
    _ORDER["token"] = None
    return (loss, grad_x, *[res[k]["g"] for k in order], *[res[k]["d"] for k in order],
            *[res[k]["m"] for k in order], *[res[k]["v"] for k in order])
```

```python
import functools

import jax
import jax.numpy as jnp
from jax import lax
from jax.experimental import pallas as pl
from jax.experimental.pallas import tpu as pltpu

F32 = jnp.float32
BF16 = jnp.bfloat16

HEAD_DIM = 128
BLOCK = 128
DILATIONS = (1, 4, 16)
HEADS_PER_GROUP = 4
A_HEADS = 12
SB_HEADS = 4
GROUP_W = HEADS_PER_GROUP * HEAD_DIM
A_W = A_HEADS * HEAD_DIM
B_W = SB_HEADS * HEAD_DIM
OFF_QA, OFF_KA, OFF_VA = 0, A_W, 2 * A_W
OFF_QB, OFF_KB, OFF_VB = 3 * A_W, 3 * A_W + B_W, 3 * A_W + 2 * B_W
OFF_GATES = 3 * A_W + 3 * B_W
ROPE_THETA = 10000.0
EPS = 1e-6
ATT_SCALE = HEAD_DIM ** -0.5
MASKED = -1e30

ADAM_LR, ADAM_B1, ADAM_B2, ADAM_EPS, ADAM_WD, ADAM_STEP = 0.001, 0.9, 0.999, 1e-08, 0.01, 10

N_DEV = 8
N_CHIPS = 4
V7X_VMEM_LIMIT_BYTES = 56 * 1024 * 1024
ELEMWISE_BLOCK_BYTES = 2 * 1024 * 1024
MESH = pl.DeviceIdType.MESH

NN = (((1,), (0,)), ((), ()))
NT = (((1,), (1,)), ((), ()))
TN = (((0,), (0,)), ((), ()))


def _dot(a, b, dims=NN):
    return lax.dot_general(a, b, dims, preferred_element_type=F32)


def _tile(n, cap, mult=128):
    best = None
    for t in range(mult, min(n, cap) + 1, mult):
        if n % t == 0:
            best = t
    if best is None:
        assert n <= 2 * cap, (n, cap)
        return n
    return best


def _rows(r, c):
    return _tile(r, max(16, ELEMWISE_BLOCK_BYTES // (4 * c)), 16)


_ORDER = {"token": None}
TOKEN = jax.ShapeDtypeStruct((8, 128), F32)


def _take_token():
    prev = _ORDER["token"]
    return [] if prev is None else [prev]


def _pcall(body, *, name, out_shape, grid=None, in_specs=None, out_specs=None, scratch=(), aliases=None,
           prefetch=0):
    single = not isinstance(out_shape, (tuple, list))
    out_shapes = [out_shape] if single else list(out_shape)
    out_specs = [out_specs] if single else list(out_specs)
    extra = _take_token()
    n_in, n_extra, n_out = prefetch + len(in_specs), len(extra), len(out_shapes)

    def wrapped(*refs):
        token = refs[n_in + n_extra + n_out]
        token[...] = jnp.zeros_like(token)
        return body(*refs[:n_in], *refs[n_in + n_extra:n_in + n_extra + n_out], *refs[n_in + n_extra + n_out + 1:])

    in_specs = list(in_specs) + [pl.BlockSpec(memory_space=pl.ANY)] * n_extra
    if grid is None:
        out_specs.append(pl.BlockSpec(memory_space=pltpu.VMEM))
    else:
        out_specs.append(pl.BlockSpec(TOKEN.shape, lambda *_: (0, 0)))
    kwargs = dict(name=name, out_shape=out_shapes + [TOKEN], input_output_aliases=aliases or {},
                  compiler_params=pltpu.CompilerParams(vmem_limit_bytes=V7X_VMEM_LIMIT_BYTES))
    if prefetch:
        call = pl.pallas_call(wrapped, grid_spec=pltpu.PrefetchScalarGridSpec(
            num_scalar_prefetch=prefetch, grid=grid, in_specs=in_specs, out_specs=out_specs,
            scratch_shapes=list(scratch)), **kwargs)
    else:
        if grid is not None:
            kwargs["grid"] = grid
        call = pl.pallas_call(wrapped, in_specs=in_specs, out_specs=out_specs, scratch_shapes=list(scratch), **kwargs)

    def run(*args):
        outs = call(*args, *extra)
        _ORDER["token"] = outs[-1]
        return outs[0] if single else tuple(outs[:-1])

    return run


def _mm(a, b, *, name, ta=False, tb=False, out_dtype=F32, caps=(1024, 1024, 3072), stack=None):
    kdim, m = a.shape if ta else a.shape[::-1]
    n, k2 = b.shape if tb else b.shape[::-1]
    assert kdim == k2, (a.shape, b.shape, ta, tb)
    tm, tn, tk = _tile(m, caps[0]), _tile(n, caps[1]), _tile(kdim, caps[2])
    nk = kdim // tk
    dims = (((0 if ta else 1,), (1 if tb else 0,)), ((), ()))

    def body(*refs):
        a_ref, b_ref = refs[0], refs[1]
        part = _dot(a_ref[...].astype(BF16), b_ref[...].astype(BF16), dims)
        if nk == 1:
            o_ref = refs[-1]
            o_ref[...] = part.astype(o_ref.dtype)
            return
        o_ref, acc_ref = refs[-2], refs[-1]
        k = pl.program_id(2)

        @pl.when(k == 0)
        def _():
            acc_ref[...] = part

        @pl.when(k > 0)
        def _():
            acc_ref[...] += part

        @pl.when(k == nk - 1)
        def _():
            o_ref[...] = acc_ref[...].astype(o_ref.dtype)

    a_spec = (pl.BlockSpec((tk, tm), lambda i, j, k: (k, i)) if ta
              else pl.BlockSpec((tm, tk), lambda i, j, k: (i, k)))
    b_spec = (pl.BlockSpec((tn, tk), lambda i, j, k: (j, k)) if tb
              else pl.BlockSpec((tk, tn), lambda i, j, k: (k, j)))
    ins, in_specs, aliases = [a, b], [a_spec, b_spec], {}
    if stack is None:
        out_shape = jax.ShapeDtypeStruct((m, n), out_dtype)
        out_spec = pl.BlockSpec((tm, tn), lambda i, j, k: (i, j))
    else:
        layer, n_layers, buf = stack
        out_shape = jax.ShapeDtypeStruct((n_layers, m, n), out_dtype)
        out_spec = pl.BlockSpec((None, tm, tn), lambda i, j, k: (layer, i, j))
        if buf is not None:
            ins.append(buf)
            in_specs.append(pl.BlockSpec(memory_space=pl.ANY))
            aliases = {2: 0}
    scratch = [] if nk == 1 else [pltpu.VMEM((tm, tn), F32)]
    return _pcall(body, name=name, out_shape=out_shape, grid=(m // tm, n // tn, nk), in_specs=in_specs,
                  out_specs=out_spec, scratch=scratch, aliases=aliases)(*ins)


EPILOGUE_ROWS = 256


def _row_chunks(tm):
    return [slice(r, r + EPILOGUE_ROWS) for r in range(0, tm, EPILOGUE_ROWS)] if tm > EPILOGUE_ROWS else [slice(0, tm)]


def _mm_cat_k(a_lo, a_hi, b, *, name):
    m, f = a_lo.shape
    n = b.shape[0]
    tm, tn, tk = _tile(m, 1024), _tile(n, 1024), _tile(f, 3072)
    half = f // tk
    nk = 2 * half

    def body(lo_ref, hi_ref, b_ref, o_ref, acc_ref):
        k = pl.program_id(2)

        def accumulate(a_ref):
            part = _dot(a_ref[...], b_ref[...], NT)

            @pl.when(k == 0)
            def _():
                acc_ref[...] = part

            @pl.when(k > 0)
            def _():
                acc_ref[...] += part

        pl.when(k < half)(lambda: accumulate(lo_ref))
        pl.when(k >= half)(lambda: accumulate(hi_ref))

        @pl.when(k == nk - 1)
        def _():
            o_ref[...] = acc_ref[...]

    return _pcall(body, name=name, out_shape=jax.ShapeDtypeStruct((m, n), F32), grid=(m // tm, n // tn, nk),
                  in_specs=[pl.BlockSpec((tm, tk), lambda i, j, k: (i, jnp.minimum(k, half - 1))),
                            pl.BlockSpec((tm, tk), lambda i, j, k: (i, jnp.maximum(k - half, 0))),
                            pl.BlockSpec((tn, tk), lambda i, j, k: (j, k))],
                  out_specs=pl.BlockSpec((tm, tn), lambda i, j, k: (i, j)),
                  scratch=[pltpu.VMEM((tm, tn), F32)])(a_lo, a_hi, b)


def _mm_cat_n(a, b_lo, b_hi, *, name):
    s, m = a.shape
    f = b_lo.shape[1]
    tm, tn = _tile(m, 2048), _tile(f, 1024)
    half = f // tn

    def body(a_ref, lo_ref, hi_ref, o_ref):
        j = pl.program_id(1)

        @pl.when(j < half)
        def _():
            o_ref[...] = _dot(a_ref[...], lo_ref[...], TN).astype(BF16)

        @pl.when(j >= half)
        def _():
            o_ref[...] = _dot(a_ref[...], hi_ref[...], TN).astype(BF16)

    return _pcall(body, name=name, out_shape=jax.ShapeDtypeStruct((m, 2 * f), BF16), grid=(m // tm, 2 * half),
                  in_specs=[pl.BlockSpec((s, tm), lambda i, j: (0, i)),
                            pl.BlockSpec((s, tn), lambda i, j: (0, jnp.minimum(j, half - 1))),
                            pl.BlockSpec((s, tn), lambda i, j: (0, jnp.maximum(j - half, 0)))],
                  out_specs=pl.BlockSpec((tm, tn), lambda i, j: (i, j)))(a, b_lo, b_hi)


def _mm_resid_norm(a, w, h, gate, norm, *, name):
    s, kdim = a.shape
    d = w.shape[1]
    tk = _tile(kdim, 2048)
    nk = kdim // tk
    tm = _tile(s, 256 if nk == 1 else 512)

    def body(*refs):
        a_ref, w_ref, h_ref, gate_ref = refs[:4]
        outs = refs[7:] if norm is not None else refs[4:]

        def finish(rows, t):
            hn = h_ref[rows, :] + gate_ref[...] * t
            outs[0][rows, :] = hn
            outs[1][rows, :] = t.astype(BF16)
            if norm is not None:
                g_ref, sc_ref, sh_ref = refs[4:7]
                r = lax.rsqrt(jnp.mean(hn * hn, axis=-1, keepdims=True) + EPS)
                outs[2][rows, :] = (((hn * r) * g_ref[...]) * (1.0 + sc_ref[...]) + sh_ref[...]).astype(BF16)

        if nk == 1:
            for rows in _row_chunks(tm):
                finish(rows, _dot(a_ref[rows, :], w_ref[...]))
            return
        acc_ref = refs[-1]
        k = pl.program_id(1)

        @pl.when(k == 0)
        def _():
            acc_ref[...] = _dot(a_ref[...], w_ref[...])

        @pl.when(jnp.logical_and(k > 0, k < nk - 1))
        def _():
            acc_ref[...] += _dot(a_ref[...], w_ref[...])

        @pl.when(k == nk - 1)
        def _():
            for rows in _row_chunks(tm):
                finish(rows, acc_ref[rows, :] + _dot(a_ref[rows, :], w_ref[...]))

    row = pl.BlockSpec((tm, d), lambda i, k: (i, 0))
    vec = pl.BlockSpec((1, d), lambda i, k: (0, 0))
    in_specs = [pl.BlockSpec((tm, tk), lambda i, k: (i, k)), pl.BlockSpec((tk, d), lambda i, k: (k, 0)), row, vec]
    args = [a, w, h, gate]
    out_shape = [jax.ShapeDtypeStruct((s, d), F32), jax.ShapeDtypeStruct((s, d), BF16)]
    if norm is not None:
        in_specs += [vec, vec, vec]
        args += list(norm)
        out_shape.append(jax.ShapeDtypeStruct((s, d), BF16))
    outs = _pcall(body, name=name, out_shape=tuple(out_shape), grid=(s // tm, nk), in_specs=in_specs,
                  out_specs=(row,) * len(out_shape), scratch=[] if nk == 1 else [pltpu.VMEM((tm, d), F32)])(*args)
    return outs if norm is not None else (*outs, None)


def _mm_merge(o_a, o_b, w_a, w_b, proj, *, name):
    s = o_a.shape[0]
    d = w_a.shape[1]
    tm = _tile(s, 512)
    ga_blk = OFF_GATES // d

    def body(oa_ref, ob_ref, wa_ref, wb_ref, ga_ref, gb_ref, m_ref, ya_ref, yb_ref):
        for rows in _row_chunks(tm):
            ya, yb = _dot(oa_ref[rows, :], wa_ref[...]), _dot(ob_ref[rows, :], wb_ref[...])
            m_ref[rows, :] = (jax.nn.sigmoid(ga_ref[rows, :]) * ya
                              + jax.nn.sigmoid(gb_ref[rows, :]) * yb).astype(BF16)
            ya_ref[rows, :] = ya.astype(BF16)
            yb_ref[rows, :] = yb.astype(BF16)

    row = pl.BlockSpec((tm, d), lambda i: (i, 0))
    act = pl.BlockSpec((tm, o_a.shape[1]), lambda i: (i, 0))
    wspec = pl.BlockSpec(w_a.shape, lambda i: (0, 0))
    shp = jax.ShapeDtypeStruct((s, d), BF16)
    return _pcall(body, name=name, out_shape=(shp, shp, shp), grid=(s // tm,),
                  in_specs=[act, act, wspec, wspec, pl.BlockSpec((tm, d), lambda i: (i, ga_blk)),
                            pl.BlockSpec((tm, d), lambda i: (i, ga_blk + 1))],
                  out_specs=(row, row, row))(o_a, o_b, w_a, w_b, proj, proj)


def _mm_out_t_merge(dt, w_out, proj, y_a, y_b, *, name):
    s, d = dt.shape
    tm, tn = _tile(s, 1024), _tile(d, 512)
    ga_blk = OFF_GATES // tn

    def body(dt_ref, w_ref, ga_ref, gb_ref, ya_ref, yb_ref, dya_ref, dyb_ref, dga_ref, dgb_ref):
        w = w_ref[...]
        for rows in _row_chunks(tm):
            dm = _dot(dt_ref[rows, :], w, NT)
            sa, sb = jax.nn.sigmoid(ga_ref[rows, :]), jax.nn.sigmoid(gb_ref[rows, :])
            dya_ref[rows, :] = (dm * sa).astype(BF16)
            dyb_ref[rows, :] = (dm * sb).astype(BF16)
            dga_ref[rows, :] = (dm * ya_ref[rows, :] * (sa * (1.0 - sa))).astype(BF16)
            dgb_ref[rows, :] = (dm * yb_ref[rows, :] * (sb * (1.0 - sb))).astype(BF16)

    tile = pl.BlockSpec((tm, tn), lambda i, j: (i, j))
    shp = jax.ShapeDtypeStruct((s, d), BF16)
    return _pcall(body, name=name, out_shape=(shp,) * 4, grid=(s // tm, d // tn),
                  in_specs=[pl.BlockSpec((tm, d), lambda i, j: (i, 0)), pl.BlockSpec((tn, d), lambda i, j: (j, 0)),
                            pl.BlockSpec((tm, tn), lambda i, j: (i, ga_blk + j)),
                            pl.BlockSpec((tm, tn), lambda i, j: (i, ga_blk + d // tn + j)), tile, tile],
                  out_specs=(tile,) * 4)(dt, w_out, proj, proj, y_a, y_b)


def _mm_down_t_swiglu(df, w_down, g, u, *, name):
    s, d = df.shape
    f = w_down.shape[0]
    tm, tn = _tile(s, 1024), _tile(f, 512)

    def body(df_ref, w_ref, g_ref, u_ref, dg_ref, du_ref):
        w = w_ref[...]
        for rows in _row_chunks(tm):
            da = _dot(df_ref[rows, :], w, NT)
            gf = g_ref[rows, :].astype(F32)
            sg = jax.nn.sigmoid(gf)
            dg_ref[rows, :] = (da * u_ref[rows, :].astype(F32) * (sg * (1.0 + gf * (1.0 - sg)))).astype(BF16)
            du_ref[rows, :] = (da * (gf * sg)).astype(BF16)

    tile = pl.BlockSpec((tm, tn), lambda i, j: (i, j))
    shp = jax.ShapeDtypeStruct((s, f), BF16)
    return _pcall(body, name=name, out_shape=(shp, shp), grid=(s // tm, f // tn),
                  in_specs=[pl.BlockSpec((tm, d), lambda i, j: (i, 0)), pl.BlockSpec((tn, d), lambda i, j: (j, 0)),
                            tile, tile],
                  out_specs=(tile, tile))(df, w_down, g, u)


def _rmsmod_fwd(h, g, scale, shift, *, name):
    s, d = h.shape
    ts = _rows(s, d)

    def body(h_ref, g_ref, sc_ref, sh_ref, u_ref):
        hf = h_ref[...]
        r = lax.rsqrt(jnp.mean(hf * hf, axis=-1, keepdims=True) + EPS)
        u_ref[...] = (((hf * r) * g_ref[...]) * (1.0 + sc_ref[...]) + sh_ref[...]).astype(BF16)

    row = pl.BlockSpec((ts, d), lambda i: (i, 0))
    vec = pl.BlockSpec((1, d), lambda i: (0, 0))
    return _pcall(body, name=name, out_shape=jax.ShapeDtypeStruct((s, d), BF16), grid=(s // ts,),
                  in_specs=[row, vec, vec, vec], out_specs=row)(h, g, scale, shift)


def _gate_bwd(dhf, t_ref, gate_ref, dt_ref, dgate_ref):
    dt_ref[...] = (dhf * gate_ref[...]).astype(BF16)
    dgate_ref[...] += jnp.sum(dhf * t_ref[...], axis=0, keepdims=True)


def _rmsmod_bwd(du, h, g, scale, dres, t, gate, *, name):
    s, d = h.shape
    ts = _rows(s, d)
    chain = t is not None

    def body(*refs):
        du_ref, h_ref, g_ref, sc_ref, dres_ref = refs[:5]
        dh_ref, dsh_ref, dsc_ref, dg_ref = refs[-6:-2] if chain else refs[-4:]
        sums = (dsh_ref, dsc_ref, dg_ref) + ((refs[-1],) if chain else ())

        @pl.when(pl.program_id(0) == 0)
        def _():
            for ref in sums:
                ref[...] = jnp.zeros_like(ref)

        hf, duf, gain = h_ref[...], du_ref[...], g_ref[...]
        r = lax.rsqrt(jnp.mean(hf * hf, axis=-1, keepdims=True) + EPS)
        xh = hf * r
        dn = duf * (1.0 + sc_ref[...])
        dsh_ref[...] += jnp.sum(duf, axis=0, keepdims=True)
        dsc_ref[...] += jnp.sum(duf * (xh * gain), axis=0, keepdims=True)
        dg_ref[...] += jnp.sum(dn * xh, axis=0, keepdims=True)
        dxh = dn * gain
        dh = dres_ref[...] + r * (dxh - xh * jnp.mean(dxh * xh, axis=-1, keepdims=True))
        dh_ref[...] = dh
        if chain:
            _gate_bwd(dh, refs[5], refs[6], refs[-2], refs[-1])

    row = pl.BlockSpec((ts, d), lambda i: (i, 0))
    vec = pl.BlockSpec((1, d), lambda i: (0, 0))
    vshape = jax.ShapeDtypeStruct((1, d), F32)
    out_shape, out_specs = [jax.ShapeDtypeStruct((s, d), F32), vshape, vshape, vshape], [row, vec, vec, vec]
    in_specs, args = [row, row, vec, vec, row], [du, h, g, scale, dres]
    if chain:
        in_specs, args = in_specs + [row, vec], args + [t, gate]
        out_shape, out_specs = out_shape + [jax.ShapeDtypeStruct((s, d), BF16), vshape], out_specs + [row, vec]
    outs = _pcall(body, name=name, out_shape=tuple(out_shape), grid=(s // ts,), in_specs=in_specs,
                  out_specs=tuple(out_specs))(*args)
    return outs if chain else (*outs, None, None)


def _mm_swiglu(u2, w_gate_up, *, name):
    s, d = u2.shape
    f = w_gate_up.shape[1] // 2
    tm, tn = _tile(s, 1024), _tile(f, 512)
    nj = f // tn

    def body(x_ref, wg_ref, wu_ref, a_ref, g_ref, u_ref):
        for rows in _row_chunks(tm):
            x = x_ref[rows, :]
            gf, uf = _dot(x, wg_ref[...]), _dot(x, wu_ref[...])
            a_ref[rows, :] = ((gf * jax.nn.sigmoid(gf)) * uf).astype(BF16)
            g_ref[rows, :] = gf.astype(BF16)
            u_ref[rows, :] = uf.astype(BF16)

    out = pl.BlockSpec((tm, tn), lambda i, j: (i, j))
    shp = jax.ShapeDtypeStruct((s, f), BF16)
    return _pcall(body, name=name, out_shape=(shp, shp, shp), grid=(s // tm, nj),
                  in_specs=[pl.BlockSpec((tm, d), lambda i, j: (i, 0)), pl.BlockSpec((d, tn), lambda i, j: (0, j)),
                            pl.BlockSpec((d, tn), lambda i, j: (0, nj + j))],
                  out_specs=(out, out, out))(u2, w_gate_up, w_gate_up)


def _loss_fwd(y, tgt, t, gate, *, name):
    s, d = y.shape
    ts = _rows(s, d)

    def body(y_ref, tgt_ref, t_ref, gate_ref, l_ref, dy_ref, dt_ref, dgate_ref):
        @pl.when(pl.program_id(0) == 0)
        def _():
            l_ref[...] = jnp.zeros_like(l_ref)
            dgate_ref[...] = jnp.zeros_like(dgate_ref)

        e = y_ref[...] - tgt_ref[...]
        dy = e * (1.0 / d)
        dy_ref[...] = dy
        per_tok = jnp.sum(e * e, axis=1, keepdims=True) * (1.0 / d)
        l_ref[...] += 0.5 * jnp.sum(per_tok, axis=0, keepdims=True)
        _gate_bwd(dy, t_ref, gate_ref, dt_ref, dgate_ref)

    row = pl.BlockSpec((ts, d), lambda i: (i, 0))
    vec = pl.BlockSpec((1, d), lambda i: (0, 0))
    return _pcall(body, name=name,
                  out_shape=(jax.ShapeDtypeStruct((1, 128), F32), jax.ShapeDtypeStruct((s, d), F32),
                             jax.ShapeDtypeStruct((s, d), BF16), jax.ShapeDtypeStruct((1, d), F32)),
                  grid=(s // ts,), in_specs=[row, row, row, vec],
                  out_specs=(pl.BlockSpec((1, 128), lambda i: (0, 0)), row, row, vec))(y, tgt, t, gate)


def _rope_tables(seq):
    inv = jnp.power(ROPE_THETA, -jnp.arange(0, HEAD_DIM, 2, dtype=F32) / HEAD_DIM)
    ang = jnp.arange(seq, dtype=F32)[:, None] * inv[None, :]
    cos, sin = jnp.cos(ang), jnp.sin(ang)
    return jnp.concatenate([cos, cos], axis=1), jnp.concatenate([-sin, sin], axis=1)


def _qkrope_fwd(proj, gains, cos2, sin2, *, name):
    s = proj.shape[0]
    ts = _rows(s, A_W)

    def body(x_ref, g_ref, c_ref, s_ref, o_ref, o32_ref):
        gain, cos, sin = g_ref[...], c_ref[...], s_ref[...]
        for h in range(A_HEADS):
            lanes = slice(h * HEAD_DIM, (h + 1) * HEAD_DIM)
            x = x_ref[:, lanes]
            y = (x * lax.rsqrt(jnp.mean(x * x, axis=-1, keepdims=True) + EPS)) * gain
            out = y * cos + pltpu.roll(y, HEAD_DIM // 2, 1) * sin
            o_ref[:, lanes] = out.astype(BF16)
            o32_ref[:, lanes] = out

    heads = pl.BlockSpec((ts, A_W), lambda i, j: (i, j))
    tab = pl.BlockSpec((ts, HEAD_DIM), lambda i, j: (i, 0))
    gain = pl.BlockSpec((None, 1, HEAD_DIM), lambda i, j: (j, 0, 0))
    return _pcall(body, name=name,
                  out_shape=(jax.ShapeDtypeStruct((s, 2 * A_W), BF16), jax.ShapeDtypeStruct((s, 2 * A_W), F32)),
                  grid=(s // ts, 2), in_specs=[heads, gain, tab, tab], out_specs=(heads, heads))(
                      proj, gains, cos2, sin2)


def _qkrope_bwd(d_groups, proj, gains, which, cos2, sin2, *, name):
    s = proj.shape[0]
    ts = _rows(s, A_W)

    def body(d0_ref, d1_ref, d2_ref, x_ref, g_ref, c_ref, s_ref, dx_ref, dg_ref):
        @pl.when(pl.program_id(0) == 0)
        def _():
            dg_ref[...] = jnp.zeros_like(dg_ref)

        gain, cos, sin = g_ref[...], c_ref[...], s_ref[...]
        dg = jnp.zeros((1, HEAD_DIM), F32)
        for h in range(A_HEADS):
            lanes = slice(h * HEAD_DIM, (h + 1) * HEAD_DIM)
            slot = slice((h % HEADS_PER_GROUP) * HEAD_DIM, (h % HEADS_PER_GROUP + 1) * HEAD_DIM)
            dout = (d0_ref, d1_ref, d2_ref)[h // HEADS_PER_GROUP][:, slot]
            dy = dout * cos + pltpu.roll(dout * sin, HEAD_DIM // 2, 1)
            x = x_ref[:, lanes]
            r = lax.rsqrt(jnp.mean(x * x, axis=-1, keepdims=True) + EPS)
            xh = x * r
            dg = dg + jnp.sum(dy * xh, axis=0, keepdims=True)
            dxh = dy * gain
            dx_ref[:, lanes] = (r * (dxh - xh * jnp.mean(dxh * xh, axis=-1, keepdims=True))).astype(BF16)
        dg_ref[...] += dg

    group = pl.BlockSpec((ts, GROUP_W), lambda i: (i, 0))
    tab = pl.BlockSpec((ts, HEAD_DIM), lambda i: (i, 0))
    gain = pl.BlockSpec((None, 1, HEAD_DIM), lambda i: (which, 0, 0))
    return _pcall(body, name=name,
                  out_shape=(jax.ShapeDtypeStruct((s, A_W), BF16), jax.ShapeDtypeStruct((1, HEAD_DIM), F32)),
                  grid=(s // ts,),
                  in_specs=[group, group, group, pl.BlockSpec((ts, A_W), lambda i: (i, which)), gain, tab, tab],
                  out_specs=(pl.BlockSpec((ts, A_W), lambda i: (i, 0)), pl.BlockSpec((1, HEAD_DIM), lambda i: (0, 0))))(
                      *d_groups, proj, gains, cos2, sin2)


def _assemble(pieces, *, name):
    s = pieces[0].shape[0]
    widths = [p.shape[1] for p in pieces]
    total = sum(widths)
    ts = _rows(s, total // 2)

    def body(*refs):
        o_ref, off = refs[-1], 0
        for x_ref, w in zip(refs[:-1], widths):
            o_ref[:, off:off + w] = x_ref[...].astype(BF16)
            off += w

    return _pcall(body, name=name, out_shape=jax.ShapeDtypeStruct((s, total), BF16), grid=(s // ts,),
                  in_specs=[pl.BlockSpec((ts, w), lambda i: (i, 0)) for w in widths],
                  out_specs=pl.BlockSpec((ts, total), lambda i: (i, 0)))(*pieces)


def _block_rows(blk):
    if isinstance(blk, int):
        return pl.ds(blk * BLOCK, BLOCK)
    return pl.ds(pl.multiple_of(blk * BLOCK, BLOCK), BLOCK)


def _band_window(n, length):
    width = min(2 * BLOCK, length)
    row = lax.broadcasted_iota(jnp.int32, (BLOCK, width), 0)
    col = lax.broadcasted_iota(jnp.int32, (BLOCK, width), 1)
    if width == BLOCK:
        return pl.ds(0, BLOCK), col <= row
    first = n - 1 if isinstance(n, int) else jnp.maximum(n - 1, 0)
    first = max(first, 0) if isinstance(first, int) else first
    dist = row - col + (n - first) * BLOCK
    start = first * BLOCK if isinstance(first, int) else pl.multiple_of(first * BLOCK, BLOCK)
    return pl.ds(start, width), jnp.logical_and(dist >= 0, dist <= BLOCK)


def _dil_fwd(q_arr, k_arr, v_arr, offs, length, dil, *, name):
    nj, nb = dil * HEADS_PER_GROUP, length // BLOCK
    ju, nq = (HEADS_PER_GROUP, 2) if nb > 1 else (2 * HEADS_PER_GROUP, 1)
    qo, ko, vo = (off // ju for off in offs)
    assert all(off % ju == 0 for off in offs) and nb % nq == 0 and nj % ju == 0

    def body(q_ref, k_ref, v_ref, o_ref, l_ref):
        for qq in range(nq):
            qrows = slice(qq * BLOCK, (qq + 1) * BLOCK)
            rows, mask = _band_window(pl.program_id(1) * nq + qq, length)
            for cb in range(ju):
                lanes = slice(cb * HEAD_DIM, (cb + 1) * HEAD_DIM)
                sc = _dot(q_ref[qrows, lanes].astype(BF16), k_ref[rows, lanes].astype(BF16), NT) * ATT_SCALE
                sc = jnp.where(mask, sc, MASKED)
                m = sc.max(axis=-1, keepdims=True)
                p = jnp.exp(sc - m)
                den = jnp.sum(p, axis=-1, keepdims=True)
                acc = _dot(p.astype(BF16), v_ref[rows, lanes].astype(BF16))
                o_ref[qrows, lanes] = acc / den
                l_ref[qrows, lanes] = jnp.broadcast_to(m + jnp.log(den), (BLOCK, HEAD_DIM))

    qspec = pl.BlockSpec((nq * BLOCK, ju * HEAD_DIM), lambda j, n: (n, qo + j))
    kspec = pl.BlockSpec((length, ju * HEAD_DIM), lambda j, n: (0, ko + j))
    vspec = pl.BlockSpec((length, ju * HEAD_DIM), lambda j, n: (0, vo + j))
    ospec = pl.BlockSpec((nq * BLOCK, ju * HEAD_DIM), lambda j, n: (n, j))
    shp = jax.ShapeDtypeStruct((length, nj * HEAD_DIM), F32)
    return _pcall(body, name=name, out_shape=(shp, shp), grid=(nj // ju, nb // nq), in_specs=[qspec, kspec, vspec],
                  out_specs=(ospec, ospec))(q_arr, k_arr, v_arr)


def _dil_bwd(q_arr, k_arr, v_arr, offs, o, lse, do, dlse, length, dil, *, name):
    nj, nb = dil * HEADS_PER_GROUP, length // BLOCK
    ju = 2 * HEADS_PER_GROUP if length <= 4 * BLOCK else 2
    qo, ko, vo = (off // ju for off in offs)
    assert all(off % ju == 0 for off in offs)

    def body(q_ref, k_ref, v_ref, o_ref, l_ref, do_ref, dl_ref, dq_ref, dk_ref, dv_ref):
        dk_ref[...] = jnp.zeros_like(dk_ref)
        dv_ref[...] = jnp.zeros_like(dv_ref)

        def step(n, carry):
            qrows = _block_rows(n)
            rows, mask = _band_window(n, length)
            for cb in range(ju):
                lanes = slice(cb * HEAD_DIM, (cb + 1) * HEAD_DIM)
                q = q_ref[qrows, lanes].astype(BF16)
                dof = do_ref[qrows, lanes]
                dob = dof.astype(BF16)
                lse_c = l_ref[qrows, lanes][:, :1]
                shift = dl_ref[qrows, lanes][:, :1] - jnp.sum(dof * o_ref[qrows, lanes], axis=-1, keepdims=True)
                kk, vv = k_ref[rows, lanes].astype(BF16), v_ref[rows, lanes].astype(BF16)
                sc = _dot(q, kk, NT) * ATT_SCALE
                p = jnp.where(mask, jnp.exp(sc - lse_c), 0.0)
                ds = (p * (_dot(dob, vv, NT) + shift)).astype(BF16)
                dq_ref[qrows, lanes] = _dot(ds, kk) * ATT_SCALE
                dk_ref[rows, lanes] += _dot(ds, q, TN) * ATT_SCALE
                dv_ref[rows, lanes] += _dot(p.astype(BF16), dob, TN)
            return carry

        if nb == 1:
            step(0, 0)
        else:
            lax.fori_loop(0, nb, step, 0)

    def col(off):
        return pl.BlockSpec((length, ju * HEAD_DIM), lambda j: (0, off + j))

    shp = jax.ShapeDtypeStruct((length, nj * HEAD_DIM), F32)
    return _pcall(body, name=name, out_shape=(shp, shp, shp), grid=(nj // ju,),
                  in_specs=[col(qo), col(ko), col(vo), col(0), col(0), col(0), col(0)],
                  out_specs=(col(0), col(0), col(0)))(q_arr, k_arr, v_arr, o, lse, do, dlse)


DIL_RESIDUES_PER_STEP = 8


def _dil_tokens(n, r, dil, length):
    width = min(2 * BLOCK, length)
    _, mask = _band_window(n, length)
    first = 0 if width == BLOCK else jnp.maximum(n - 1, 0)
    return (pl.ds(n * (BLOCK * dil) + r, BLOCK, stride=dil), pl.ds(first * (BLOCK * dil) + r, width, stride=dil),
            mask)


def _dil_head_specs(seq, group):
    first = group * HEADS_PER_GROUP

    def col(c0):
        return pl.BlockSpec((seq, HEAD_DIM), lambda h, r: (0, c0 + h))

    return col(first), col(A_HEADS + first), col(OFF_VA // HEAD_DIM + first), col(0)


def _dil_fwd_strided(qk32, proj, group, dil, *, name):
    seq = proj.shape[0]
    length = seq // dil
    nb, rp = length // BLOCK, min(dil, DIL_RESIDUES_PER_STEP)
    assert dil % rp == 0

    def body(q_ref, k_ref, v_ref, o_ref, l_ref):
        rgroup = pl.program_id(1)

        def step(n, carry):
            for rr in range(rp):
                tok_q, tok_k, mask = _dil_tokens(n, rgroup * rp + rr, dil, length)
                sc = _dot(q_ref[tok_q, :].astype(BF16), k_ref[tok_k, :].astype(BF16), NT) * ATT_SCALE
                sc = jnp.where(mask, sc, MASKED)
                m = sc.max(axis=-1, keepdims=True)
                p = jnp.exp(sc - m)
                den = jnp.sum(p, axis=-1, keepdims=True)
                o_ref[tok_q, :] = _dot(p.astype(BF16), v_ref[tok_k, :].astype(BF16)) / den
                l_ref[tok_q, :] = jnp.broadcast_to(m + jnp.log(den), (BLOCK, HEAD_DIM))
            return carry

        if nb == 1:
            step(0, 0)
        else:
            lax.fori_loop(0, nb, step, 0)

    qs, ks, vs, nat = _dil_head_specs(seq, group)
    shp = jax.ShapeDtypeStruct((seq, GROUP_W), F32)
    return _pcall(body, name=name, out_shape=(shp, shp), grid=(HEADS_PER_GROUP, dil // rp), in_specs=[qs, ks, vs],
                  out_specs=(nat, nat))(qk32, qk32, proj)


def _dil_bwd_strided(qk32, proj, group, o, lse, do, dlse, dil, *, name):
    seq = proj.shape[0]
    length = seq // dil
    nb, rp = length // BLOCK, min(dil, DIL_RESIDUES_PER_STEP)
    assert dil % rp == 0

    def body(q_ref, k_ref, v_ref, o_ref, l_ref, do_ref, dl_ref, dq_ref, dk_ref, dv_ref):
        rgroup = pl.program_id(1)

        @pl.when(rgroup == 0)
        def _():
            dk_ref[...] = jnp.zeros_like(dk_ref)
            dv_ref[...] = jnp.zeros_like(dv_ref)

        def step(n, carry):
            for rr in range(rp):
                tok_q, tok_k, mask = _dil_tokens(n, rgroup * rp + rr, dil, length)
                q = q_ref[tok_q, :].astype(BF16)
                dof = do_ref[tok_q, :]
                dob = dof.astype(BF16)
                lse_c = l_ref[tok_q, :][:, :1]
                shift = dl_ref[tok_q, :][:, :1] - jnp.sum(dof * o_ref[tok_q, :], axis=-1, keepdims=True)
                kk, vv = k_ref[tok_k, :].astype(BF16), v_ref[tok_k, :].astype(BF16)
                sc = _dot(q, kk, NT) * ATT_SCALE
                p = jnp.where(mask, jnp.exp(sc - lse_c), 0.0)
                ds = (p * (_dot(dob, vv, NT) + shift)).astype(BF16)
                dq_ref[tok_q, :] = _dot(ds, kk) * ATT_SCALE
                dk_ref[tok_k, :] += _dot(ds, q, TN) * ATT_SCALE
                dv_ref[tok_k, :] += _dot(p.astype(BF16), dob, TN)
            return carry

        if nb == 1:
            step(0, 0)
        else:
            lax.fori_loop(0, nb, step, 0)

    qs, ks, vs, nat = _dil_head_specs(seq, group)
    shp = jax.ShapeDtypeStruct((seq, GROUP_W), F32)
    return _pcall(body, name=name, out_shape=(shp, shp, shp), grid=(HEADS_PER_GROUP, dil // rp),
                  in_specs=[qs, ks, vs, nat, nat, nat, nat], out_specs=(nat, nat, nat))(
                      qk32, qk32, proj, o, lse, do, dlse)


def _combine_weights(l_refs):
    ls = [r[...] for r in l_refs]
    m = jnp.maximum(jnp.maximum(ls[0], ls[1]), ls[2])
    es = [jnp.exp(l - m) for l in ls]
    den = es[0] + es[1] + es[2]
    return [e / den for e in es]


def _combine_fwd(os_, lses, *, name):
    s = os_[0].shape[0]
    ts = _rows(s, GROUP_W)

    def body(o0, o1, o2, l0, l1, l2, out_ref):
        w = _combine_weights((l0, l1, l2))
        out_ref[...] = (w[0] * o0[...] + w[1] * o1[...] + w[2] * o2[...]).astype(BF16)

    row = pl.BlockSpec((ts, GROUP_W), lambda i: (i, 0))
    return _pcall(body, name=name, out_shape=jax.ShapeDtypeStruct((s, GROUP_W), BF16), grid=(s // ts,),
                  in_specs=[row] * 6, out_specs=row)(*os_, *lses)


def _combine_bwd(do_a, os_, lses, *, name):
    s = do_a.shape[0]
    ts = _rows(s, GROUP_W)

    def body(d_ref, o0, o1, o2, l0, l1, l2, do0, do1, do2, dl0, dl1, dl2):
        w = _combine_weights((l0, l1, l2))
        d = d_ref[...]
        og = [o0[...], o1[...], o2[...]]
        oa = w[0] * og[0] + w[1] * og[1] + w[2] * og[2]
        ta = jnp.sum(d * oa, axis=-1, keepdims=True)
        for g, (do_ref, dl_ref) in enumerate(((do0, dl0), (do1, dl1), (do2, dl2))):
            do_ref[...] = w[g] * d
            dl_ref[...] = w[g] * (jnp.sum(d * og[g], axis=-1, keepdims=True) - ta)

    head = pl.BlockSpec((ts, HEAD_DIM), lambda i, h: (i, h))
    shp = jax.ShapeDtypeStruct((s, GROUP_W), F32)
    return _pcall(body, name=name, out_shape=(shp,) * 6, grid=(s // ts, HEADS_PER_GROUP),
                  in_specs=[head] * 7, out_specs=(head,) * 6)(do_a, *os_, *lses)


def _dot_exact(x, ones_mask):
    hi = x.astype(BF16)
    r1 = x - hi.astype(F32)
    mid = r1.astype(BF16)
    lo = (r1 - mid.astype(F32)).astype(BF16)
    return _dot(hi, ones_mask) + _dot(mid, ones_mask) + _dot(lo, ones_mask)


SB_QROWS = 4 * BLOCK
SB_UNROLL = 4
SB_HEADS_PER_STEP = 2
SB_LANES = [slice(hh * HEAD_DIM, (hh + 1) * HEAD_DIM) for hh in range(SB_HEADS_PER_STEP)]


def _sb_mask(j, i):
    row = lax.broadcasted_iota(jnp.int32, (SB_QROWS, BLOCK), 0)
    col = lax.broadcasted_iota(jnp.int32, (SB_QROWS, BLOCK), 1)
    return col + (j * BLOCK - i * SB_QROWS) < row


def _sb_steps(i):
    return ((i + 1) * (SB_QROWS // BLOCK) + SB_UNROLL - 1) // SB_UNROLL


def _sb_scores(q, kk, j, i, masked):
    mask = _sb_mask(j, i) if masked else None
    z = _dot(q, kk, NT) * ATT_SCALE
    sp = jnp.log(1.0 + jnp.exp(-jnp.abs(z)))
    log_beta = jnp.minimum(z, 0.0) - sp
    log_1mb = jnp.minimum(-z, 0.0) - sp
    if masked:
        log_1mb = jnp.where(mask, log_1mb, 0.0)
    return z, log_beta, log_1mb, mask


def _sb_weights(log_beta, log_1mb, mask, run, upper):
    a = jnp.exp(log_beta + (run + _dot_exact(log_1mb, upper)))
    return a if mask is None else jnp.where(mask, a, 0.0)


def _sb_peeled(nsteps, make_step, init, masked_first):
    if masked_first:
        return lax.fori_loop(1, nsteps, make_step(False), make_step(True)(0, init))
    return make_step(True)(nsteps - 1, lax.fori_loop(0, nsteps - 1, make_step(False), init))


def _tri(strict_lower):
    row = lax.broadcasted_iota(jnp.int32, (BLOCK, BLOCK), 0)
    col = lax.broadcasted_iota(jnp.int32, (BLOCK, BLOCK), 1)
    return ((row > col) if strict_lower else (row < col)).astype(BF16)


def _sb_fwd(proj, *, name):
    s = proj.shape[0]
    assert s % (BLOCK * SB_UNROLL) == 0 and s % SB_QROWS == 0

    def body(q_ref, k_ref, v_ref, o_ref):
        i = pl.program_id(1)
        qs = [q_ref[:, lanes].astype(BF16) for lanes in SB_LANES]
        upper = _tri(True)
        nsteps = _sb_steps(i)

        def make_step(masked):
            def step(t, carry):
                carry = list(carry)
                for b in reversed(range(SB_UNROLL)):
                    j = (nsteps - 1 - t) * SB_UNROLL + b
                    rows = _block_rows(j)
                    for hh, lanes in enumerate(SB_LANES):
                        acc, run = carry[hh]
                        _, log_beta, log_1mb, mask = _sb_scores(qs[hh], k_ref[rows, lanes].astype(BF16), j, i, masked)
                        a = _sb_weights(log_beta, log_1mb, mask, run, upper)
                        carry[hh] = (acc + _dot(a.astype(BF16), v_ref[rows, lanes].astype(BF16)),
                                     run + jnp.sum(log_1mb, axis=-1, keepdims=True))
                return tuple(carry)
            return step

        zero = (jnp.zeros((SB_QROWS, HEAD_DIM), F32), jnp.zeros((SB_QROWS, 1), F32))
        for lanes, (acc, _) in zip(SB_LANES, _sb_peeled(nsteps, make_step, (zero,) * SB_HEADS_PER_STEP, True)):
            o_ref[:, lanes] = acc.astype(BF16)

    width = SB_HEADS_PER_STEP * HEAD_DIM
    qb, kb, vb = (off // width for off in (OFF_QB, OFF_KB, OFF_VB))
    return _pcall(body, name=name, out_shape=jax.ShapeDtypeStruct((s, B_W), BF16),
                  grid=(SB_HEADS // SB_HEADS_PER_STEP, s // SB_QROWS),
                  in_specs=[pl.BlockSpec((SB_QROWS, width), lambda h, i: (i, qb + h)),
                            pl.BlockSpec((s, width), lambda h, i: (0, kb + h)),
                            pl.BlockSpec((s, width), lambda h, i: (0, vb + h))],
                  out_specs=pl.BlockSpec((SB_QROWS, width), lambda h, i: (i, h)))(proj, proj, proj)


def _sb_bwd(proj, do_b, *, name):
    s = proj.shape[0]
    assert s % (BLOCK * SB_UNROLL) == 0 and s % SB_QROWS == 0
    nkb = s // BLOCK

    def body(q_ref, k_ref, v_ref, do_ref, dq_ref, dk_ref, dv_ref, z_s, a_s):
        i = pl.program_id(1)

        @pl.when(i == 0)
        def _():
            dk_ref[...] = jnp.zeros_like(dk_ref)
            dv_ref[...] = jnp.zeros_like(dv_ref)

        qs = [q_ref[:, lanes].astype(BF16) for lanes in SB_LANES]
        dobs = [do_ref[:, lanes].astype(BF16) for lanes in SB_LANES]
        upper, lower = _tri(True), _tri(False)
        nsteps = _sb_steps(i)

        def make_recompute(masked):
            def recompute(t, runs):
                runs = list(runs)
                for b in reversed(range(SB_UNROLL)):
                    j = (nsteps - 1 - t) * SB_UNROLL + b
                    rows = _block_rows(j)
                    for hh, lanes in enumerate(SB_LANES):
                        z, log_beta, log_1mb, mask = _sb_scores(qs[hh], k_ref[rows, lanes].astype(BF16), j, i, masked)
                        z_s[hh, j] = z
                        a_s[hh, j] = _sb_weights(log_beta, log_1mb, mask, runs[hh], upper)
                        runs[hh] = runs[hh] + jnp.sum(log_1mb, axis=-1, keepdims=True)
                return tuple(runs)
            return recompute

        _sb_peeled(nsteps, make_recompute, (jnp.zeros((SB_QROWS, 1), F32),) * SB_HEADS_PER_STEP, True)

        def make_grads(masked):
            def grads(t, carry):
                carry = list(carry)
                for b in range(SB_UNROLL):
                    j = t * SB_UNROLL + b
                    rows = _block_rows(j)
                    for hh, lanes in enumerate(SB_LANES):
                        dq, run = carry[hh]
                        kk, vv = k_ref[rows, lanes].astype(BF16), v_ref[rows, lanes].astype(BF16)
                        z, a = z_s[hh, j], a_s[hh, j]
                        de = _dot(dobs[hh], vv, NT) * a
                        beta = jax.nn.sigmoid(z)
                        one_minus_beta = 1.0 - beta
                        if masked:
                            beta = jnp.where(_sb_mask(j, i), beta, 0.0)
                        dz = (de * one_minus_beta - beta * (run + _dot_exact(de, lower))).astype(BF16)
                        dk_ref[rows, lanes] += _dot(dz, qs[hh], TN) * ATT_SCALE
                        dv_ref[rows, lanes] += _dot(a.astype(BF16), dobs[hh], TN)
                        carry[hh] = (dq + _dot(dz, kk), run + jnp.sum(de, axis=-1, keepdims=True))
                return tuple(carry)
            return grads

        zero = (jnp.zeros((SB_QROWS, HEAD_DIM), F32), jnp.zeros((SB_QROWS, 1), F32))
        for lanes, (dq, _) in zip(SB_LANES, _sb_peeled(nsteps, make_grads, (zero,) * SB_HEADS_PER_STEP, False)):
            dq_ref[:, lanes] = dq * ATT_SCALE

    width = SB_HEADS_PER_STEP * HEAD_DIM
    qb, kb, vb = (off // width for off in (OFF_QB, OFF_KB, OFF_VB))
    blk = pl.BlockSpec((SB_QROWS, width), lambda h, i: (i, h))
    full = pl.BlockSpec((s, width), lambda h, i: (0, h))
    shp = jax.ShapeDtypeStruct((s, B_W), F32)
    saved = pltpu.VMEM((SB_HEADS_PER_STEP, nkb, SB_QROWS, BLOCK), F32)
    return _pcall(body, name=name, out_shape=(shp, shp, shp), grid=(SB_HEADS // SB_HEADS_PER_STEP, s // SB_QROWS),
                  in_specs=[pl.BlockSpec((SB_QROWS, width), lambda h, i: (i, qb + h)),
                            pl.BlockSpec((s, width), lambda h, i: (0, kb + h)),
                            pl.BlockSpec((s, width), lambda h, i: (0, vb + h)), blk],
                  out_specs=(blk, full, full), scratch=[saved, saved])(proj, proj, proj, do_b)


def _coords():
    return lax.axis_index("x"), lax.axis_index("y"), lax.axis_index("c")


def _flip(v, bit):
    return 1 - v if bit else v


def _shard_of(ref, axis, idx, size):
    if axis == 0:
        sl = pl.ds(pl.multiple_of(idx * size, 16), size)
        return ref.at[sl, :] if len(ref.shape) == 2 else ref.at[:, sl, :]
    sl = pl.ds(pl.multiple_of(idx * size, 128), size)
    return ref.at[:, sl] if len(ref.shape) == 2 else ref.at[:, :, sl]


def _small_allgather(v, *, name, silu=False):
    n = v.shape[1]

    def body(v_ref, out_ref, send_sems, recv_sems):
        x, y, c = _coords()
        me = 4 * x + 2 * y + c
        val = v_ref[...]
        out_ref[me] = val * jax.nn.sigmoid(val) if silu else val
        copies = []
        for k in range(1, N_DEV):
            peer = (_flip(x, k & 4), _flip(y, k & 2), _flip(c, k & 1))
            copies.append(pltpu.make_async_remote_copy(
                src_ref=out_ref.at[me], dst_ref=out_ref.at[me], send_sem=send_sems.at[k - 1],
                recv_sem=recv_sems.at[k - 1], device_id=peer, device_id_type=MESH))
        for cp in copies:
            cp.start()
        for cp in copies:
            cp.wait_recv()
        for cp in copies:
            cp.wait_send()

    return _pcall(body, name=name, out_shape=jax.ShapeDtypeStruct((N_DEV, 1, n), F32),
                  in_specs=[pl.BlockSpec(memory_space=pltpu.VMEM)], out_specs=pl.BlockSpec(memory_space=pltpu.VMEM),
                  scratch=[pltpu.SemaphoreType.DMA((N_DEV - 1,)), pltpu.SemaphoreType.DMA((N_DEV - 1,))])(v)


def _cast_place(w, layer, axis, me, *, name):
    _, r, c = w.shape
    tr = _rows(r, c)
    nrt = r // tr

    def body(me_ref, w_ref, o_ref):
        o_ref[...] = w_ref[...].astype(BF16)

    wspec = pl.BlockSpec((None, tr, c), lambda i, me_ref: (layer, i, 0))
    if axis == 0:
        ospec = pl.BlockSpec((tr, c), lambda i, me_ref: (me_ref[0] * nrt + i, 0))
        shape = (r * N_DEV, c)
    else:
        ospec = pl.BlockSpec((tr, c), lambda i, me_ref: (i, me_ref[0]))
        shape = (r, c * N_DEV)
    return _pcall(body, name=name, out_shape=jax.ShapeDtypeStruct(shape, BF16), grid=(nrt,), in_specs=[wspec],
                  out_specs=ospec, prefetch=1)(me, w)


def _pair_sum(grad, sib, core, axis, *, name):
    _, r, c = sib.shape
    tr = _rows(r, c // 2)
    nrt = r // tr

    def body(core_ref, g_ref, s_ref, o_ref):
        o_ref[...] = (g_ref[...].astype(F32) + s_ref[...].astype(F32)).astype(BF16)

    if axis == 0:
        gspec = pl.BlockSpec((tr, c), lambda q, i, core_ref: ((2 * q + core_ref[0]) * nrt + i, 0))
    else:
        gspec = pl.BlockSpec((tr, c), lambda q, i, core_ref: (i, 2 * q + core_ref[0]))
    sspec = pl.BlockSpec((None, tr, c), lambda q, i, core_ref: (q, i, 0))
    return _pcall(body, name=name, out_shape=jax.ShapeDtypeStruct(sib.shape, BF16), grid=(N_CHIPS, nrt),
                  in_specs=[gspec, sspec], out_specs=sspec, prefetch=1)(core, grad, sib)


ANY_SPEC = pl.BlockSpec(memory_space=pl.ANY)
SEM_SPEC = pl.BlockSpec(memory_space=pltpu.SEMAPHORE)
SPLIT_PARAMS = dict(has_side_effects=pltpu.SideEffectType.DATAFLOW_SIDE_EFFECTING)


def _split_start(copies_fn, buffers, sem_shape, after, *, name):
    n = len(buffers)
    rows, cols = sem_shape
    ns = rows * cols
    extra = ([] if after is None else [after]) + _take_token()

    def body(*refs):
        sems = refs[n + len(extra):n + len(extra) + 2 * ns]
        for cp in copies_fn(refs[:n], _sem_rows(sems[:ns], cols), _sem_rows(sems[ns:], cols)):
            cp.start()
        refs[-1][...] = jnp.zeros_like(refs[-1])

    sem = pltpu.SemaphoreType.DMA(())
    outs = pl.pallas_call(
        body, name=name,
        out_shape=((sem,) * (2 * ns) + tuple(jax.ShapeDtypeStruct(b.shape, b.dtype) for b in buffers) + (TOKEN,)),
        in_specs=(ANY_SPEC,) * (n + len(extra)),
        out_specs=(SEM_SPEC,) * (2 * ns) + (ANY_SPEC,) * n + (pl.BlockSpec(memory_space=pltpu.VMEM),),
        input_output_aliases={i: 2 * ns + i for i in range(n)},
        compiler_params=pltpu.CompilerParams(**SPLIT_PARAMS))(*buffers, *extra)
    _ORDER["token"] = outs[-1]
    return list(outs[:ns]), list(outs[ns:2 * ns]), list(outs[2 * ns:2 * ns + n]), outs[-1]


def _split_wait(copies_fn, send_sems, recv_sems, buffers, after, sem_rows, *, name):
    n, ns = len(buffers), len(send_sems)
    cols = ns // sem_rows
    extra = ([] if after is None else [after]) + _take_token()

    def body(*refs):
        sems = refs[n:n + 2 * ns]
        copies = copies_fn(refs[:n], _sem_rows(sems[:ns], cols), _sem_rows(sems[ns:], cols))
        for cp in copies:
            cp.wait_send()
        for cp in copies:
            cp.wait_recv()
        refs[-1][...] = jnp.zeros_like(refs[-1])

    outs = pl.pallas_call(
        body, name=name, out_shape=tuple(jax.ShapeDtypeStruct(b.shape, b.dtype) for b in buffers) + (TOKEN,),
        in_specs=(ANY_SPEC,) * n + (SEM_SPEC,) * (2 * ns) + (ANY_SPEC,) * len(extra),
        out_specs=(ANY_SPEC,) * n + (pl.BlockSpec(memory_space=pltpu.VMEM),),
        input_output_aliases={i: i for i in range(n)},
        compiler_params=pltpu.CompilerParams(**SPLIT_PARAMS))(*buffers, *send_sems, *recv_sems, *extra)
    _ORDER["token"] = outs[-1]
    return list(outs[:n])


def _sem_rows(sems, cols):
    return [sems[i:i + cols] for i in range(0, len(sems), cols)]


def _empty_hbm(shape, dtype):
    return pltpu.with_memory_space_constraint(lax.empty(shape, dtype), pltpu.HBM)


class _SplitGather:
    def __init__(self, fulls, axes, tag):
        self.axes, self.tag, self.nt = list(axes), tag, len(fulls)
        self.sizes = [f.shape[ax] // N_DEV for f, ax in zip(fulls, axes)]
        self.fulls = list(fulls)

    def _slot(self, ref, t, dev):
        return _shard_of(ref, self.axes[t], 4 * dev[0] + 2 * dev[1] + dev[2], self.sizes[t])

    def _first_copies(self, refs, send_sems, recv_sems):
        x, y, c = _coords()
        peers = [(x, y, 1 - c), (1 - x, y, c), (x, 1 - y, c), (1 - x, 1 - y, c)]
        return [pltpu.make_async_remote_copy(
            src_ref=self._slot(refs[t], t, (x, y, c)), dst_ref=self._slot(refs[t], t, (x, y, c)),
            send_sem=send_sems[t][k], recv_sem=recv_sems[t][k], device_id=peer, device_id_type=MESH)
            for t in range(self.nt) for k, peer in enumerate(peers)]

    def _forward_copies(self, refs, send_sems, recv_sems):
        x, y, c = _coords()
        chips = [(1 - x, y), (x, 1 - y), (1 - x, 1 - y)]
        return [pltpu.make_async_remote_copy(
            src_ref=self._slot(refs[t], t, (*chip, c)), dst_ref=self._slot(refs[t], t, (*chip, c)),
            send_sem=send_sems[t][j], recv_sem=recv_sems[t][j], device_id=(x, y, 1 - c), device_id_type=MESH)
            for t in range(self.nt) for j, chip in enumerate(chips)]

    def first(self, after):
        self.s1, self.r1, self.fulls, token = _split_start(
            self._first_copies, self.fulls, (self.nt, 4), after, name=f"comm_gather1_start_{self.tag}")
        return token

    def forward(self, after):
        bufs = _split_wait(self._first_copies, self.s1, self.r1, self.fulls, after, self.nt,
                           name=f"comm_gather1_wait_{self.tag}")
        self.s2, self.r2, self.fulls, token = _split_start(
            self._forward_copies, bufs, (self.nt, 3), after, name=f"comm_gather2_start_{self.tag}")
        return token

    def finish(self, after):
        return _split_wait(self._forward_copies, self.s2, self.r2, self.fulls, after, self.nt,
                           name=f"comm_gather2_wait_{self.tag}")


class _SplitGatherViaNeighbours:
    def __init__(self, fulls, axes, tag):
        self.axes, self.tag, self.nt = list(axes), tag, len(fulls)
        self.sizes = [f.shape[ax] // N_DEV for f, ax in zip(fulls, axes)]
        self.fulls = list(fulls)
        assert all(f.shape[0] % 32 == 0 for f in fulls)

    def _slot(self, ref, t, dev, half=None):
        idx, size = 4 * dev[0] + 2 * dev[1] + dev[2], self.sizes[t]
        if half is None:
            return _shard_of(ref, self.axes[t], idx, size)
        if self.axes[t] == 0:
            return ref.at[pl.ds(pl.multiple_of(idx * size + half * (size // 2), 16), size // 2), :]
        rows = ref.shape[0] // 2
        return ref.at[pl.ds(half * rows, rows), pl.ds(pl.multiple_of(idx * size, 128), size)]

    def _copies(self, stage, refs, send_sems, recv_sems):
        x, y, c = _coords()
        sib, xn, yn, diag = (x, y, 1 - c), (1 - x, y, c), (x, 1 - y, c), (1 - x, 1 - y, c)
        plan = {1: [((x, y, c), None, sib), ((x, y, c), None, xn), ((x, y, c), None, yn)],
                2: [(xn, None, sib), (yn, None, sib), (xn, 0, yn), (yn, 1, xn)],
                3: [(diag, 0, sib), (diag, 1, sib)]}[stage]
        return [pltpu.make_async_remote_copy(
            src_ref=self._slot(refs[t], t, block, half), dst_ref=self._slot(refs[t], t, block, half),
            send_sem=send_sems[t][k], recv_sem=recv_sems[t][k], device_id=to, device_id_type=MESH)
            for t in range(self.nt) for k, (block, half, to) in enumerate(plan)]

    def stage(self, number, after):
        if number > 1:
            self.fulls = _split_wait(functools.partial(self._copies, number - 1), self.s, self.r, self.fulls, after,
                                     self.nt, name=f"comm_gather{number - 1}_wait_{self.tag}")
        if number <= 3:
            self.s, self.r, self.fulls, _ = _split_start(
                functools.partial(self._copies, number), self.fulls, (self.nt, {1: 3, 2: 4, 3: 2}[number]), after,
                name=f"comm_gather{number}_start_{self.tag}")
        return self.fulls


class _SplitPairExchange:
    def __init__(self, grads, axes, tag):
        self.nt, self.tag, self.axes = len(grads), tag, list(axes)
        self.grads = list(grads)
        self.sizes = [g.shape[ax] // N_DEV for g, ax in zip(grads, axes)]

    def _copies(self, refs, send_sems, recv_sems):
        nt = self.nt
        x, y, c = _coords()
        return [pltpu.make_async_remote_copy(
            src_ref=_shard_of(refs[t], self.axes[t], 2 * q + 1 - c, self.sizes[t]), dst_ref=refs[nt + t].at[q],
            send_sem=send_sems[t][q], recv_sem=recv_sems[t][q], device_id=(x, y, 1 - c), device_id_type=MESH)
            for t in range(nt) for q in range(N_CHIPS)]

    def start(self):
        landing = []
        for g, ax in zip(self.grads, self.axes):
            dims = list(g.shape)
            dims[ax] //= N_DEV
            landing.append(_empty_hbm((N_CHIPS, *dims), g.dtype))
        self.s, self.r, self.bufs, token = _split_start(
            self._copies, self.grads + landing, (self.nt, N_CHIPS), None,
            name=f"comm_rs_pair_start_{self.tag}")
        return token

    def finish(self, after):
        bufs = _split_wait(self._copies, self.s, self.r, self.bufs, after, self.nt,
                           name=f"comm_rs_pair_wait_{self.tag}")
        return bufs[:self.nt], bufs[self.nt:]


class _ReducePipeline:
    def __init__(self, core):
        self.core, self.items, self.done, self.now = core, [], [], 0

    def add(self, keys, grads, layer):
        axes = [SHARD_AXIS[k] for k in keys]
        pair = _SplitPairExchange([grads[k] for k in keys], axes, f"{keys[0]}{layer}")
        pair.start()
        self.items.append(dict(keys=keys, layer=layer, axes=axes, pair=pair, state="pair", since=self.now))

    def tick(self, after, flush=False):
        self.now += 1
        for it in self.items:
            if it["state"] == "pair" and it["since"] < self.now:
                grads, sib = it["pair"].finish(after)
                sums = [_pair_sum(g, s_, self.core, ax, name="pair_sum_" + k)
                        for k, g, s_, ax in zip(it["keys"], grads, sib, it["axes"])]
                it["chip"] = _SplitChipExchange(sums, f"{it['keys'][0]}{it['layer']}")
                it["chip"].start()
                it.update(state="chip", since=self.now)
            elif it["state"] == "chip" and (flush or self.now - it["since"] >= 2):
                sums, remote = it["chip"].finish(after)
                self.done.append((it["keys"], it["layer"], sums, remote))
                it["state"] = "done"

    def take_done(self):
        out, self.done = self.done, []
        return out


class _SplitChipExchange:
    def __init__(self, sums, tag):
        self.nt, self.tag = len(sums), tag
        self.sums = list(sums)

    def _copies(self, refs, send_sems, recv_sems):
        nt = self.nt
        x, y, c = _coords()
        copies = []
        for t in range(nt):
            for k in range(1, N_CHIPS):
                px, py = _flip(x, k & 2), _flip(y, k & 1)
                copies.append(pltpu.make_async_remote_copy(
                    src_ref=refs[t].at[2 * px + py], dst_ref=refs[nt + t].at[k - 1], send_sem=send_sems[t][k - 1],
                    recv_sem=recv_sems[t][k - 1], device_id=(px, py, c), device_id_type=MESH))
        return copies

    def start(self):
        landing = [_empty_hbm((N_CHIPS - 1,) + s.shape[1:], s.dtype) for s in self.sums]
        self.s, self.r, self.bufs, token = _split_start(
            self._copies, self.sums + landing, (self.nt, N_CHIPS - 1), None,
            name=f"comm_rs_chip_start_{self.tag}")
        return token

    def finish(self, after):
        bufs = _split_wait(self._copies, self.s, self.r, self.bufs, after, self.nt,
                           name=f"comm_rs_chip_wait_{self.tag}")
        return bufs[:self.nt], bufs[self.nt:]


def _adam_math(g, w, m, v):
    m2 = ADAM_B1 * m + (1.0 - ADAM_B1) * g
    v2 = ADAM_B2 * v + (1.0 - ADAM_B2) * (g * g)
    m_hat = m2 / (1.0 - ADAM_B1 ** ADAM_STEP)
    v_hat = v2 / (1.0 - ADAM_B2 ** ADAM_STEP)
    delta = -ADAM_LR * (m_hat / (jnp.sqrt(v_hat) + ADAM_EPS) + ADAM_WD * w)
    return delta, m2, v2


def _adamw_sharded(chip_sums, remote, chip, w, m, v, layer, prev, *, name):
    nl, r, c = w.shape
    tr = _rows(r, c)

    def body(*refs):
        p_ref, r0_ref, r1_ref, r2_ref, w_ref, m_ref, v_ref = refs[1:8]
        g_out, d_out, m_out, v_out = refs[-4:]
        g = ((p_ref[...].astype(F32) + r0_ref[...].astype(F32)) + r1_ref[...].astype(F32)) + r2_ref[...].astype(F32)
        g_out[...] = g
        d_out[...], m_out[...], v_out[...] = _adam_math(g, w_ref[...], m_ref[...], v_ref[...])

    pspec = pl.BlockSpec((None, tr, c), lambda i, chip_ref: (chip_ref[0], i, 0))

    def rspec(k):
        return pl.BlockSpec((None, tr, c), lambda i, chip_ref: (k, i, 0))

    wspec = pl.BlockSpec((None, tr, c), lambda i, chip_ref: (layer, i, 0))
    in_specs = [pspec, rspec(0), rspec(1), rspec(2), wspec, wspec, wspec]
    args = [chip, chip_sums, remote, remote, remote, w, m, v]
    aliases = {}
    if prev is not None:
        in_specs += [pl.BlockSpec(memory_space=pl.ANY)] * 4
        aliases = {len(args) + i: i for i in range(4)}
        args += list(prev)
    shp = jax.ShapeDtypeStruct(w.shape, F32)
    return _pcall(body, name=name, out_shape=(shp,) * 4, grid=(r // tr,), in_specs=in_specs, out_specs=(wspec,) * 4,
                  aliases=aliases, prefetch=1)(*args)


def _adamw_local(g, w, m, v, *, name):
    nl, r, c = w.shape
    tr = _rows(r, c)

    def body(g_ref, w_ref, m_ref, v_ref, d_out, m_out, v_out):
        d_out[...], m_out[...], v_out[...] = _adam_math(g_ref[...], w_ref[...], m_ref[...], v_ref[...])

    spec = pl.BlockSpec((None, tr, c), lambda l, i: (l, i, 0))
    shp = jax.ShapeDtypeStruct(w.shape, F32)
    return _pcall(body, name=name, out_shape=(shp,) * 3, grid=(nl, r // tr), in_specs=[spec] * 4,
                  out_specs=(spec,) * 3)(g, w, m, v)


def _adamw_replicated(parts, w, m, v, *, name):
    n = w.shape[1]

    def body(p_ref, w_ref, m_ref, v_ref, g_out, d_out, m_out, v_out):
        g = p_ref[0]
        for k in range(1, N_DEV):
            g = g + p_ref[k]
        g_out[...] = g
        d_out[...], m_out[...], v_out[...] = _adam_math(g, w_ref[...], m_ref[...], v_ref[...])

    vm = pl.BlockSpec(memory_space=pltpu.VMEM)
    shp = jax.ShapeDtypeStruct((1, n), F32)
    return _pcall(body, name=name, out_shape=(shp,) * 4, in_specs=[vm] * 4, out_specs=(vm,) * 4)(parts, w, m, v)


UNDILATED_OFFS = (0, A_HEADS, OFF_VA // HEAD_DIM)


def _mod_rows(mod, d):
    return [mod[:, i * d:(i + 1) * d] for i in range(6)]


MIXER_W = ("w_in", "w_branch_a", "w_branch_b", "w_out")
FFN_W = ("w_gate_up", "w_down")
SHARD_AXIS = {"w_in": 1, "w_branch_a": 1, "w_branch_b": 1, "w_out": 0, "w_gate_up": 1, "w_down": 0}


def _norm_args(mod, gain, which, d):
    rows = _mod_rows(mod, d)
    return gain, rows[3 * which + 1], rows[3 * which]


def _mixer_fwd_a(h, u, gains, w_in, cos2, sin2, hook):
    seq = h.shape[0]
    proj = _mm(u, w_in, name="mm_in")
    hook(proj)
    qk, qk32 = _qkrope_fwd(proj, gains, cos2, sin2, name="qkrope_fwd")
    os_, lses = [], []
    for g, dil in enumerate(DILATIONS):
        if dil == 1:
            o, lse = _dil_fwd(qk, qk, proj, UNDILATED_OFFS, seq, 1, name="dil_fwd_1")
        else:
            o, lse = _dil_fwd_strided(qk32, proj, g, dil, name=f"dil_fwd_{dil}")
        os_.append(o)
        lses.append(lse)
    o_a = _combine_fwd(os_, lses, name="combine_fwd")
    o_b = _sb_fwd(proj, name="sb_fwd")
    return dict(h_in=h, u=u, proj=proj, qk=qk, qk32=qk32, os=os_, lses=lses, o_a=o_a, o_b=o_b)


def _mixer_fwd_b(sv, mod, g2, wts):
    d = sv["h_in"].shape[1]
    merged, y_a, y_b = _mm_merge(sv["o_a"], sv["o_b"], wts["w_branch_a"], wts["w_branch_b"], sv["proj"],
                                 name="mm_branch")
    h_mid, t, u2 = _mm_resid_norm(merged, wts["w_out"], sv["h_in"], _mod_rows(mod, d)[2], _norm_args(mod, g2, 1, d),
                                  name="mm_out")
    sv.update(y_a=y_a, y_b=y_b, merged=merged, t=t, h_mid=h_mid, u2=u2)
    return h_mid


def _ffn_fwd_a(sv, w_gate_up):
    a, g, u = _mm_swiglu(sv["u2"], w_gate_up, name="mm_gate_up")
    sv.update(g=g, up=u, a=a)
    return a


def _ffn_fwd_b(sv, mod, w_down, next_norm):
    d = sv["h_mid"].shape[1]
    h_out, sv["f"], u_next = _mm_resid_norm(sv["a"], w_down, sv["h_mid"], _mod_rows(mod, d)[5], next_norm,
                                            name="mm_down")
    return h_out, u_next


def _wgrad(act, dout, key):
    return _mm(act, dout, ta=True, out_dtype=BF16, caps=(2048, 1024, 3072), name="mm_wgrad_" + key)


def _ffn_bwd(dh, df, dgate2, sv, mod, g2, wts, hook):
    d = dh.shape[1]
    sc2, ga1 = _mod_rows(mod, d)[4], _mod_rows(mod, d)[2]
    dg, dup = _mm_down_t_swiglu(df, wts["w_down"], sv["g"], sv["up"], name="mm_down_t")
    grads = {"w_down": _wgrad(sv["a"], df, "w_down")}
    hook(dup)
    du2 = _mm_cat_k(dg, dup, wts["w_gate_up"], name="mm_gate_up_t")
    grads["w_gate_up"] = _mm_cat_n(sv["u2"], dg, dup, name="mm_wgrad_w_gate_up")
    dh_mid, dsh2, dsc2, dg2, dt, dgate1 = _rmsmod_bwd(du2, sv["h_mid"], g2, sc2, dh, sv["t"], ga1, name="rmsmod_bwd")
    return dh_mid, [dsh2, dsc2, dgate2], dg2, grads, dt, dgate1


def _mixer_bwd(dh_mid, dt, dgate1, sv, mod, g1, gains, wts, cos2, sin2, hook, below):
    seq, d = dh_mid.shape
    sc1 = _mod_rows(mod, d)[1]
    dy_a, dy_b, dga, dgb = _mm_out_t_merge(dt, wts["w_out"], sv["proj"], sv["y_a"], sv["y_b"], name="mm_out_t")
    grads = {"w_out": _wgrad(sv["merged"], dt, "w_out")}
    do_a = _mm(dy_a, wts["w_branch_a"], tb=True, name="mm_branch_t")
    do_b = _mm(dy_b, wts["w_branch_b"], tb=True, name="mm_branch_t")
    grads["w_branch_a"] = _wgrad(sv["o_a"], dy_a, "w_branch_a")
    grads["w_branch_b"] = _wgrad(sv["o_b"], dy_b, "w_branch_b")
    dqb, dkb, dvb = _sb_bwd(sv["proj"], do_b, name="sb_bwd")
    hook(dqb, grads)
    comb = _combine_bwd(do_a, sv["os"], sv["lses"], name="combine_bwd")
    grads = {}
    dos, dls = comb[:3], comb[3:]
    dqs, dks, dvs = [], [], []
    for g, dil in enumerate(DILATIONS):
        if dil == 1:
            dq, dk, dv = _dil_bwd(sv["qk"], sv["qk"], sv["proj"], UNDILATED_OFFS, sv["os"][g], sv["lses"][g], dos[g],
                                  dls[g], seq, 1, name="dil_bwd_1")
        else:
            dq, dk, dv = _dil_bwd_strided(sv["qk32"], sv["proj"], g, sv["os"][g], sv["lses"][g], dos[g], dls[g], dil,
                                          name=f"dil_bwd_{dil}")
        dqs.append(dq)
        dks.append(dk)
        dvs.append(dv)
    dq_pre, dqn = _qkrope_bwd(dqs, sv["proj"], gains, 0, cos2, sin2, name="qkrope_bwd")
    dk_pre, dkn = _qkrope_bwd(dks, sv["proj"], gains, 1, cos2, sin2, name="qkrope_bwd")
    dgains = jnp.stack([dqn, dkn])
    dproj = _assemble([dq_pre, dk_pre] + dvs + [dqb, dkb, dvb, dga, dgb], name="assemble_dproj")
    du = _mm(dproj, wts["w_in"], tb=True, name="mm_in_t")
    grads["w_in"] = _wgrad(sv["u"], dproj, "w_in")
    dh_in, dsh1, dsc1, dg1, df, dgate2 = _rmsmod_bwd(du, sv["h_in"], g1, sc1, dh_mid, *(below or (None, None)),
                                                     name="rmsmod_bwd")
    return dh_in, [dsh1, dsc1, dgate1], dg1, dgains, grads, df, dgate2


def kernel(x, c, w_ada, b_ada, norm1_g, norm2_g, w_in, qn_g, kn_g, w_branch_a, w_branch_b, w_out, w_gate_up, w_down, loss_target, m_w_ada, m_b_ada, m_norm1_g, m_norm2_g, m_w_in, m_qn_g, m_kn_g, m_w_branch_a, m_w_branch_b, m_w_out, m_w_gate_up, m_w_down, v_w_ada, v_b_ada, v_norm1_g, v_norm2_g, v_w_in, v_qn_g, v_kn_g, v_w_branch_a, v_w_branch_b, v_w_out, v_w_gate_up, v_w_down):
    _ORDER["token"] = None
    seq, d = x.shape[1], x.shape[2]
    depth = w_in.shape[0]
    weights = dict(w_in=w_in, w_branch_a=w_branch_a, w_branch_b=w_branch_b, w_out=w_out, w_gate_up=w_gate_up,
                   w_down=w_down)
    moments_m = dict(w_in=m_w_in, w_branch_a=m_w_branch_a, w_branch_b=m_w_branch_b, w_out=m_w_out,
                     w_gate_up=m_w_gate_up, w_down=m_w_down)
    moments_v = dict(w_in=v_w_in, w_branch_a=v_w_branch_a, w_branch_b=v_w_branch_b, w_out=v_w_out,
                     w_gate_up=v_w_gate_up, w_down=v_w_down)
    xi, yi, ci = _coords()
    me = 4 * xi + 2 * yi + ci
    core = jnp.reshape(ci, (1,)).astype(jnp.int32)
    chip = jnp.reshape(2 * xi + yi, (1,)).astype(jnp.int32)

    ada_w = w_ada.shape[2]
    bias = lax.dynamic_slice(b_ada, (0, me * ada_w), (depth, ada_w))

    cos2, sin2 = _rope_tables(seq)
    gains = [jnp.stack([qn_g[l], kn_g[l]])[:, None, :] for l in range(depth)]
    g1s = [norm1_g[l][None] for l in range(depth)]
    g2s = [norm2_g[l][None] for l in range(depth)]

    me_arr = jnp.reshape(me, (1,)).astype(jnp.int32)

    def placed(keys, l):
        return [_cast_place(weights[k], l, SHARD_AXIS[k], me_arr, name="cast_place_" + k) for k in keys]

    def gather_of(keys, l, tag):
        return _SplitGather(placed(keys, l), [SHARD_AXIS[k] for k in keys], f"{tag}{l}")

    groups = []
    for l in range(depth):
        groups += [("w_in", l, MIXER_W[:1]), ("rest", l, MIXER_W[1:]), ("up", l, FFN_W[:1]), ("down", l, FFN_W[1:])]
    gathers = {}

    def via_neighbours(l):
        return _SplitGatherViaNeighbours(placed(MIXER_W[:1], l), [SHARD_AXIS["w_in"]], f"w_in{l}")

    ins = [via_neighbours(0)]
    ins[0].stage(1, None)
    for tag, l, keys in groups:
        if tag != "w_in":
            gathers[tag, l] = gather_of(keys, l, tag)
        elif l > 0:
            ins.append(via_neighbours(l))
    c_act = _small_allgather(c, name="comm_gather_c", silu=True).reshape(N_DEV, d)
    c_pad = jnp.concatenate([c_act, jnp.zeros_like(c_act)], axis=0).astype(BF16)
    mod_part = jnp.stack([_mm(c_pad, w_ada[l], name="mm_ada")[:N_DEV] for l in range(depth)]) + bias[:, None, :]
    ins[0].stage(2, None)
    mod_all = _small_allgather(mod_part.reshape(1, depth * N_DEV * ada_w), name="comm_gather_mod")
    mod_all = mod_all.reshape(N_DEV, depth, N_DEV, ada_w)
    mod_mine = lax.dynamic_index_in_dim(mod_all, me, axis=2, keepdims=False)
    mods = jnp.transpose(mod_mine, (1, 0, 2)).reshape(depth, 1, 6 * d)

    def rest_of_layer(l):
        for tag, l2, _ in groups:
            if l2 == l and tag != "w_in":
                gathers[tag, l].first(after=mods)
        if l + 1 < depth:
            ins[l + 1].stage(1, mods)

    def second_stage(l, after):
        ins[l].stage(2, after)
        rest_of_layer(l)

    rest_of_layer(0)
    h = x[0]
    u = _rmsmod_fwd(h, *_norm_args(mods[0], g1s[0], 0, d), name="rmsmod_fwd")
    ins[0].stage(3, u)
    wm = {"w_in": ins[0].stage(4, u)[0]}
    saved, full = [], []
    for l in range(depth):
        last = l + 1 == depth
        sv = _mixer_fwd_a(h, u, gains[l], wm["w_in"], cos2, sin2, gathers["rest", l].forward)
        gathers["up", l].forward(after=sv["o_b"])
        wm.update(zip(MIXER_W[1:], gathers["rest", l].finish(after=sv["o_b"])))
        h_mid = _mixer_fwd_b(sv, mods[l], g2s[l], wm)
        wf = {"w_gate_up": gathers["up", l].finish(after=h_mid)[0]}
        gathers["down", l].forward(after=h_mid)
        a = _ffn_fwd_a(sv, wf["w_gate_up"])
        wf["w_down"] = gathers["down", l].finish(after=a)[0]
        if not last:
            second_stage(l + 1, a)
        h, u = _ffn_fwd_b(sv, mods[l], wf["w_down"],
                          None if last else _norm_args(mods[l + 1], g1s[l + 1], 0, d))
        saved.append(sv)
        full.append({**wm, **wf})
        if not last:
            ins[l + 1].stage(3, h)
            wm = {"w_in": ins[l + 1].stage(4, h)[0]}
    def ffn_gate(l):
        return saved[l]["f"], _mod_rows(mods[l], d)[5]

    loss_part, dh, df, dgate2 = _loss_fwd(h, loss_target[0], *ffn_gate(depth - 1), name="loss")
    loss = lax.psum(loss_part[0, 0], ("x", "y", "c"))

    pipe = _ReducePipeline(core)
    dmods, dg1s, dg2s, dgains = [None] * depth, [None] * depth, [None] * depth, [None] * depth
    for l in reversed(range(depth)):
        dh_mid, dmod_f, dg2s[l], grads, dt, dgate1 = _ffn_bwd(dh, df, dgate2, saved[l], mods[l], g2s[l], full[l],
                                                              pipe.tick)
        pipe.tick(dh_mid)
        pipe.add(FFN_W, grads, l)
        dh, dmod_m, dg1s[l], dgains[l], grads, df, dgate2 = _mixer_bwd(
            dh_mid, dt, dgate1, saved[l], mods[l], g1s[l], gains[l], full[l], cos2, sin2,
            lambda after, early, l=l: (pipe.tick(after), pipe.add(MIXER_W[1:], early, l)),
            ffn_gate(l - 1) if l > 0 else None)
        dmods[l] = jnp.concatenate(dmod_m + dmod_f, axis=1)
        pipe.tick(dh)
        pipe.add(MIXER_W[:1], grads, l)
    grad_x = dh[None]

    stacked = {}

    def update(items):
        for keys, l, sums, remote in items:
            for k, p_, r_ in zip(keys, sums, remote):
                stacked[k] = _adamw_sharded(p_, r_, chip, weights[k], moments_m[k], moments_v[k], l,
                                            stacked.get(k), name="adamw_" + k)

    ready = pipe.take_done()
    update([it for it in ready if it[0] != FFN_W])

    small = jnp.concatenate(
        dmods + dg1s + dg2s + [dgains[l][0] for l in range(depth)] + [dgains[l][1] for l in range(depth)], axis=1)
    small_all = _small_allgather(small, name="comm_gather_small")
    pipe.tick(small_all)
    update([it for it in ready if it[0] == FFN_W] + pipe.take_done())

    def pack(b, n1, n2, qn, kn):
        return jnp.concatenate([t_.reshape(1, -1) for t_ in (b, n1, n2, qn, kn)], axis=1)

    sg, sd, sm, sv_ = _adamw_replicated(small_all, pack(b_ada, norm1_g, norm2_g, qn_g, kn_g),
                                        pack(m_b_ada, m_norm1_g, m_norm2_g, m_qn_g, m_kn_g),
                                        pack(v_b_ada, v_norm1_g, v_norm2_g, v_qn_g, v_kn_g), name="adamw_replicated")

    def unpack(p):
        sizes = [depth * 6 * d, depth * d, depth * d, depth * HEAD_DIM, depth * HEAD_DIM]
        shapes = [b_ada.shape, norm1_g.shape, norm2_g.shape, qn_g.shape, kn_g.shape]
        out, off = [], 0
        for n, shp in zip(sizes, shapes):
            out.append(p[0, off:off + n].reshape(shp))
            off += n
        return dict(zip(("b_ada", "norm1_g", "norm2_g", "qn_g", "kn_g"), out))

    ug, ud, um, uv = unpack(sg), unpack(sd), unpack(sm), unpack(sv_)
    res = {k: dict(g=ug[k], d=ud[k], m=um[k], v=uv[k]) for k in ug}

    dmod_all = small_all[:, 0, :depth * 6 * d].reshape(N_DEV, depth, 6 * d)
    g_ada = None
    for l in range(depth):
        dm = lax.dynamic_slice(dmod_all[:, l, :], (0, me * ada_w), (N_DEV, ada_w))
        dm = jnp.concatenate([dm, jnp.zeros_like(dm)], axis=0).astype(BF16)
        g_ada = _mm(c_pad, dm, ta=True, name="mm_wgrad_ada", stack=(l, depth, g_ada))
    d_ada, m_ada, v_ada = _adamw_local(g_ada, w_ada, m_w_ada, v_w_ada, name="adamw_local")
    res["w_ada"] = dict(g=g_ada, d=d_ada, m=m_ada, v=v_ada)

    pipe.tick(d_ada)
    update(pipe.take_done())
    pipe.tick(d_ada, flush=True)
    update(pipe.take_done())
    for k, (g_, d_, m_, v_) in stacked.items():
        res[k] = dict(g=g_, d=d_, m=m_, v=v_)

    order = ("w_ada", "b_ada", "norm1_g", "norm2_g", "w_in", "qn_g", "kn_g", "w_branch_a", "w_branch_b", "w_out",
             "w_gate_up", "w_down")
    _ORDER["token"] = None
    return (loss, grad_x, *[res[k]["g"] for k in order], *[res[k]["d"] for k in order],
            *[res[k]["m"] for k in order], *[res[k]["v"] for k in order])
```
